```python
import jax, jax.numpy as jnp
from jax import lax
import numpy as np

D_MODEL = 1024
BATCH = 16
SEQ = 2048
DEPTH = 2

MEM_LEN = 256
C_CONV = D_MODEL // 2
CONV_K = 31
C_POOL = D_MODEL // 2
POOL_WINDOWS = (2, 4, 8, 16)
N_POOL_GROUPS = len(POOL_WINDOWS)
POOL_GROUP_DIM = C_POOL // N_POOL_GROUPS
POOL_GROUP_OUT = D_MODEL // N_POOL_GROUPS
N_IN = 2 * C_CONV + C_POOL + 2 * D_MODEL
XA_HEADS = 4
XA_HEAD_DIM = D_MODEL // XA_HEADS
D_FF = 2816
FFN_CONV_K = 3
EPS = 1e-6

kernel_name = "hybrid_conformer_pool_gated_block"


def rms_norm(x, g):
    xf = x.astype(jnp.float32)
    y = xf * lax.rsqrt(jnp.mean(xf * xf, axis=-1, keepdims=True) + EPS)
    return (y * g.astype(jnp.float32)).astype(x.dtype)


def layer_norm(x, g, b):
    xf = x.astype(jnp.float32)
    mu = jnp.mean(xf, axis=-1, keepdims=True)
    xc = xf - mu
    var = jnp.mean(xc * xc, axis=-1, keepdims=True)
    y = xc * lax.rsqrt(var + EPS) * g.astype(jnp.float32) + b.astype(jnp.float32)
    return y.astype(x.dtype)


def causal_dwconv(u, w):
    k = w.shape[0]
    return lax.conv_general_dilated(
        u, w[:, None, :].astype(u.dtype), window_strides=(1,), padding=[(k - 1, 0)],
        dimension_numbers=("NWC", "WIO", "NWC"), feature_group_count=u.shape[-1])


def multiscale_pool(u):
    b, s, _ = u.shape
    uf = u.astype(jnp.float32).reshape(b, s, N_POOL_GROUPS, POOL_GROUP_DIM)
    cs = jnp.cumsum(uf, axis=1)
    t = jnp.arange(s)
    outs = []
    for g, w in enumerate(POOL_WINDOWS):
        c = cs[:, :, g]
        lag = jnp.pad(c, ((0, 0), (w, 0), (0, 0)))[:, :s]
        cnt = jnp.minimum(t + 1, w).astype(jnp.float32)[None, :, None]
        outs.append((c - lag) / cnt - uf[:, :, g])
    return jnp.stack(outs, axis=2).astype(u.dtype)


def _fwd_setup_inputs(seed: int = 0) -> dict:
    key = jax.random.key(seed)
    ks = jax.random.split(key, 24)
    f32 = jnp.float32
    nrm = lambda k, shape, scale: jax.random.normal(k, shape, f32) * scale
    gain = lambda k, shape: 1.0 + 0.05 * jax.random.normal(k, shape, f32)
    return {
        "x": jax.random.normal(ks[0], (BATCH, SEQ, D_MODEL), f32),
        "mem": jax.random.normal(ks[1], (BATCH, MEM_LEN, D_MODEL), f32),
        "mix_norm_g": gain(ks[2], (DEPTH, D_MODEL)),
        "w_in": nrm(ks[3], (DEPTH, D_MODEL, N_IN), D_MODEL ** -0.5),
        "conv_dw_w": nrm(ks[4], (DEPTH, CONV_K, C_CONV), CONV_K ** -0.5),
        "conv_dw_b": nrm(ks[5], (DEPTH, C_CONV), 0.02),
        "conv_ln_g": gain(ks[6], (DEPTH, C_CONV)),
        "conv_ln_b": nrm(ks[7], (DEPTH, C_CONV), 0.02),
        "w_conv_out": nrm(ks[8], (DEPTH, C_CONV, D_MODEL), C_CONV ** -0.5),
        "w_pool_grp": nrm(ks[9], (DEPTH, N_POOL_GROUPS, POOL_GROUP_DIM, POOL_GROUP_OUT), POOL_GROUP_DIM ** -0.5),
        "pool_scale": gain(ks[10], (DEPTH, D_MODEL)),
        "w_out": nrm(ks[11], (DEPTH, D_MODEL, D_MODEL), D_MODEL ** -0.5),
        "xattn_norm_g": gain(ks[12], (DEPTH, D_MODEL)),
        "mem_norm_g": gain(ks[13], (D_MODEL,)),
        "w_q": nrm(ks[14], (DEPTH, D_MODEL, D_MODEL), D_MODEL ** -0.5),
        "w_kv": nrm(ks[15], (DEPTH, D_MODEL, 2 * D_MODEL), D_MODEL ** -0.5),
        "w_o": nrm(ks[16], (DEPTH, D_MODEL, D_MODEL), D_MODEL ** -0.5),
        "ffn_norm_g": gain(ks[17], (DEPTH, D_MODEL)),
        "w_up": nrm(ks[18], (DEPTH, D_MODEL, 2 * D_FF), D_MODEL ** -0.5),
        "ffn_dw_w": nrm(ks[19], (DEPTH, FFN_CONV_K, 2 * D_FF), FFN_CONV_K ** -0.5),
        "w_down": nrm(ks[20], (DEPTH, D_FF, D_MODEL), D_FF ** -0.5),
        "final_norm_g": gain(ks[21], (D_MODEL,)),
    }


def _fwd_reference(x, mem, mix_norm_g, w_in, conv_dw_w, conv_dw_b, conv_ln_g, conv_ln_b, w_conv_out,
              w_pool_grp, pool_scale, w_out, xattn_norm_g, mem_norm_g, w_q, w_kv, w_o,
              ffn_norm_g, w_up, ffn_dw_w, w_down, final_norm_g):
    b, s, d = x.shape
    m_len = mem.shape[1]
    mem_n = rms_norm(mem, mem_norm_g)
    split_at = [C_CONV, 2 * C_CONV, 2 * C_CONV + C_POOL, 2 * C_CONV + C_POOL + D_MODEL]
    xa_scale = XA_HEAD_DIM ** -0.5
    for l in range(DEPTH):
        h = rms_norm(x, mix_norm_g[l])
        proj = h @ w_in[l]
        a, gl, u_pool, g_conv, g_pool = jnp.split(proj, split_at, axis=-1)
        yc = a * jax.nn.sigmoid(gl)
        yc = causal_dwconv(yc, conv_dw_w[l]) + conv_dw_b[l]
        yc = jax.nn.silu(layer_norm(yc, conv_ln_g[l], conv_ln_b[l]))
        yc = yc @ w_conv_out[l]
        zp = multiscale_pool(u_pool)
        yp = jnp.einsum("bsgc,gcd->bsgd", zp, w_pool_grp[l]).reshape(b, s, d) * pool_scale[l]
        merged = jax.nn.sigmoid(g_conv) * yc + jax.nn.sigmoid(g_pool) * yp
        x = x + merged @ w_out[l]
        hq = rms_norm(x, xattn_norm_g[l])
        q = (hq @ w_q[l]).reshape(b, s, XA_HEADS, XA_HEAD_DIM)
        kv = mem_n @ w_kv[l]
        k, v = jnp.split(kv, 2, axis=-1)
        k = k.reshape(b, m_len, XA_HEADS, XA_HEAD_DIM)
        v = v.reshape(b, m_len, XA_HEADS, XA_HEAD_DIM)
        sc = jnp.einsum("bshd,bmhd->bhsm", q, k).astype(jnp.float32) * xa_scale
        pr = jax.nn.softmax(sc, axis=-1).astype(v.dtype)
        att = jnp.einsum("bhsm,bmhd->bshd", pr, v).reshape(b, s, d)
        x = x + att @ w_o[l]
        hf = rms_norm(x, ffn_norm_g[l])
        up = causal_dwconv(hf @ w_up[l], ffn_dw_w[l])
        gate, val = jnp.split(up, 2, axis=-1)
        x = x + (jax.nn.gelu(gate) * val) @ w_down[l]
    return rms_norm(x, final_norm_g)


import jax as _jax
import jax.numpy as _jnp

TWIN_FORMAT = 'train_step'
FWD_PARAMS = ['x', 'mem', 'mix_norm_g', 'w_in', 'conv_dw_w', 'conv_dw_b', 'conv_ln_g', 'conv_ln_b', 'w_conv_out', 'w_pool_grp', 'pool_scale', 'w_out', 'xattn_norm_g', 'mem_norm_g', 'w_q', 'w_kv', 'w_o', 'ffn_norm_g', 'w_up', 'ffn_dw_w', 'w_down', 'final_norm_g']
TWIN_WEIGHTS = ['mix_norm_g', 'w_in', 'conv_dw_w', 'conv_dw_b', 'conv_ln_g', 'conv_ln_b', 'w_conv_out', 'w_pool_grp', 'pool_scale', 'w_out', 'xattn_norm_g', 'mem_norm_g', 'w_q', 'w_kv', 'w_o', 'ffn_norm_g', 'w_up', 'ffn_dw_w', 'w_down', 'final_norm_g']
TWIN_DIFF_INPUT = 'x'
TWIN_INPUTS = ['x', 'mem', 'mix_norm_g', 'w_in', 'conv_dw_w', 'conv_dw_b', 'conv_ln_g', 'conv_ln_b', 'w_conv_out', 'w_pool_grp', 'pool_scale', 'w_out', 'xattn_norm_g', 'mem_norm_g', 'w_q', 'w_kv', 'w_o', 'ffn_norm_g', 'w_up', 'ffn_dw_w', 'w_down', 'final_norm_g', 'loss_target', 'm_mix_norm_g', 'm_w_in', 'm_conv_dw_w', 'm_conv_dw_b', 'm_conv_ln_g', 'm_conv_ln_b', 'm_w_conv_out', 'm_w_pool_grp', 'm_pool_scale', 'm_w_out', 'm_xattn_norm_g', 'm_mem_norm_g', 'm_w_q', 'm_w_kv', 'm_w_o', 'm_ffn_norm_g', 'm_w_up', 'm_ffn_dw_w', 'm_w_down', 'm_final_norm_g', 'v_mix_norm_g', 'v_w_in', 'v_conv_dw_w', 'v_conv_dw_b', 'v_conv_ln_g', 'v_conv_ln_b', 'v_w_conv_out', 'v_w_pool_grp', 'v_pool_scale', 'v_w_out', 'v_xattn_norm_g', 'v_mem_norm_g', 'v_w_q', 'v_w_kv', 'v_w_o', 'v_ffn_norm_g', 'v_w_up', 'v_ffn_dw_w', 'v_w_down', 'v_final_norm_g']
TWIN_OUTPUTS = ['loss', 'grad_x', 'grad_mix_norm_g', 'grad_w_in', 'grad_conv_dw_w', 'grad_conv_dw_b', 'grad_conv_ln_g', 'grad_conv_ln_b', 'grad_w_conv_out', 'grad_w_pool_grp', 'grad_pool_scale', 'grad_w_out', 'grad_xattn_norm_g', 'grad_mem_norm_g', 'grad_w_q', 'grad_w_kv', 'grad_w_o', 'grad_ffn_norm_g', 'grad_w_up', 'grad_ffn_dw_w', 'grad_w_down', 'grad_final_norm_g', 'delta_mix_norm_g', 'delta_w_in', 'delta_conv_dw_w', 'delta_conv_dw_b', 'delta_conv_ln_g', 'delta_conv_ln_b', 'delta_w_conv_out', 'delta_w_pool_grp', 'delta_pool_scale', 'delta_w_out', 'delta_xattn_norm_g', 'delta_mem_norm_g', 'delta_w_q', 'delta_w_kv', 'delta_w_o', 'delta_ffn_norm_g', 'delta_w_up', 'delta_ffn_dw_w', 'delta_w_down', 'delta_final_norm_g', 'new_m_mix_norm_g', 'new_m_w_in', 'new_m_conv_dw_w', 'new_m_conv_dw_b', 'new_m_conv_ln_g', 'new_m_conv_ln_b', 'new_m_w_conv_out', 'new_m_w_pool_grp', 'new_m_pool_scale', 'new_m_w_out', 'new_m_xattn_norm_g', 'new_m_mem_norm_g', 'new_m_w_q', 'new_m_w_kv', 'new_m_w_o', 'new_m_ffn_norm_g', 'new_m_w_up', 'new_m_ffn_dw_w', 'new_m_w_down', 'new_m_final_norm_g', 'new_v_mix_norm_g', 'new_v_w_in', 'new_v_conv_dw_w', 'new_v_conv_dw_b', 'new_v_conv_ln_g', 'new_v_conv_ln_b', 'new_v_w_conv_out', 'new_v_w_pool_grp', 'new_v_pool_scale', 'new_v_w_out', 'new_v_xattn_norm_g', 'new_v_mem_norm_g', 'new_v_w_q', 'new_v_w_kv', 'new_v_w_o', 'new_v_ffn_norm_g', 'new_v_w_up', 'new_v_ffn_dw_w', 'new_v_w_down', 'new_v_final_norm_g']
TWIN_LEAF_KINDS = {'loss': 'loss', 'grad_x': 'grad_x', 'grad_mix_norm_g': 'grad_w', 'grad_w_in': 'grad_w', 'grad_conv_dw_w': 'grad_w', 'grad_conv_dw_b': 'grad_w', 'grad_conv_ln_g': 'grad_w', 'grad_conv_ln_b': 'grad_w', 'grad_w_conv_out': 'grad_w', 'grad_w_pool_grp': 'grad_w', 'grad_pool_scale': 'grad_w', 'grad_w_out': 'grad_w', 'grad_xattn_norm_g': 'grad_w', 'grad_mem_norm_g': 'grad_w', 'grad_w_q': 'grad_w', 'grad_w_kv': 'grad_w', 'grad_w_o': 'grad_w', 'grad_ffn_norm_g': 'grad_w', 'grad_w_up': 'grad_w', 'grad_ffn_dw_w': 'grad_w', 'grad_w_down': 'grad_w', 'grad_final_norm_g': 'grad_w', 'delta_mix_norm_g': 'delta_w', 'delta_w_in': 'delta_w', 'delta_conv_dw_w': 'delta_w', 'delta_conv_dw_b': 'delta_w', 'delta_conv_ln_g': 'delta_w', 'delta_conv_ln_b': 'delta_w', 'delta_w_conv_out': 'delta_w', 'delta_w_pool_grp': 'delta_w', 'delta_pool_scale': 'delta_w', 'delta_w_out': 'delta_w', 'delta_xattn_norm_g': 'delta_w', 'delta_mem_norm_g': 'delta_w', 'delta_w_q': 'delta_w', 'delta_w_kv': 'delta_w', 'delta_w_o': 'delta_w', 'delta_ffn_norm_g': 'delta_w', 'delta_w_up': 'delta_w', 'delta_ffn_dw_w': 'delta_w', 'delta_w_down': 'delta_w', 'delta_final_norm_g': 'delta_w', 'new_m_mix_norm_g': 'new_m', 'new_m_w_in': 'new_m', 'new_m_conv_dw_w': 'new_m', 'new_m_conv_dw_b': 'new_m', 'new_m_conv_ln_g': 'new_m', 'new_m_conv_ln_b': 'new_m', 'new_m_w_conv_out': 'new_m', 'new_m_w_pool_grp': 'new_m', 'new_m_pool_scale': 'new_m', 'new_m_w_out': 'new_m', 'new_m_xattn_norm_g': 'new_m', 'new_m_mem_norm_g': 'new_m', 'new_m_w_q': 'new_m', 'new_m_w_kv': 'new_m', 'new_m_w_o': 'new_m', 'new_m_ffn_norm_g': 'new_m', 'new_m_w_up': 'new_m', 'new_m_ffn_dw_w': 'new_m', 'new_m_w_down': 'new_m', 'new_m_final_norm_g': 'new_m', 'new_v_mix_norm_g': 'new_v', 'new_v_w_in': 'new_v', 'new_v_conv_dw_w': 'new_v', 'new_v_conv_dw_b': 'new_v', 'new_v_conv_ln_g': 'new_v', 'new_v_conv_ln_b': 'new_v', 'new_v_w_conv_out': 'new_v', 'new_v_w_pool_grp': 'new_v', 'new_v_pool_scale': 'new_v', 'new_v_w_out': 'new_v', 'new_v_xattn_norm_g': 'new_v', 'new_v_mem_norm_g': 'new_v', 'new_v_w_q': 'new_v', 'new_v_w_kv': 'new_v', 'new_v_w_o': 'new_v', 'new_v_ffn_norm_g': 'new_v', 'new_v_w_up': 'new_v', 'new_v_ffn_dw_w': 'new_v', 'new_v_w_down': 'new_v', 'new_v_final_norm_g': 'new_v'}


def _forward(args):
    return _fwd_reference(*[args[k] for k in FWD_PARAMS])


def _output_shape():
    out = _jax.eval_shape(lambda: _forward(_fwd_setup_inputs(0)))
    return out.shape, out.dtype

N_MICROBATCH = 1
ADAM_LR = 0.001
ADAM_B1 = 0.9
ADAM_B2 = 0.999
ADAM_EPS = 1e-08
ADAM_WD = 0.01
ADAM_STEP = 10
PER_EXAMPLE_BATCH_AXIS = {'x': 0, 'mem': 0, 'loss_target': 0}
SHARED_INPUTS = []
_WEIGHT_DTYPES = {'mix_norm_g': _jnp.float32, 'w_in': _jnp.float32, 'conv_dw_w': _jnp.float32, 'conv_dw_b': _jnp.float32, 'conv_ln_g': _jnp.float32, 'conv_ln_b': _jnp.float32, 'w_conv_out': _jnp.float32, 'w_pool_grp': _jnp.float32, 'pool_scale': _jnp.float32, 'w_out': _jnp.float32, 'xattn_norm_g': _jnp.float32, 'mem_norm_g': _jnp.float32, 'w_q': _jnp.float32, 'w_kv': _jnp.float32, 'w_o': _jnp.float32, 'ffn_norm_g': _jnp.float32, 'w_up': _jnp.float32, 'ffn_dw_w': _jnp.float32, 'w_down': _jnp.float32, 'final_norm_g': _jnp.float32}
MOMENT_SCALE = {'mix_norm_g': 1.059221e-01, 'w_in': 5.712408e-02, 'conv_dw_w': 7.812779e-02, 'conv_dw_b': 1.706568e-01, 'conv_ln_g': 9.217990e-02, 'conv_ln_b': 8.382759e-02, 'w_conv_out': 5.452627e-02, 'w_pool_grp': 8.046468e-02, 'pool_scale': 7.904801e-02, 'w_out': 9.686945e-02, 'xattn_norm_g': 1.753657e-02, 'mem_norm_g': 3.770548e-02, 'w_q': 1.742336e-02, 'w_kv': 1.750432e-02, 'w_o': 1.790350e-02, 'ffn_norm_g': 1.236247e-01, 'w_up': 5.332878e-02, 'ffn_dw_w': 5.300412e-02, 'w_down': 8.814386e-02, 'final_norm_g': 3.205669e+01}


def _to_microbatches(a, axis):
    t = _jnp.moveaxis(a, axis, 0)
    t = t.reshape((N_MICROBATCH, t.shape[0] // N_MICROBATCH) + t.shape[1:])
    return _jnp.moveaxis(t, 1, axis + 1)


def setup_inputs(seed: int = 0) -> dict:
    inp = _fwd_setup_inputs(seed)
    key = _jax.random.fold_in(_jax.random.key(seed), 7919)
    shape, _ = _output_shape()
    out = dict(inp)
    out["loss_target"] = _jax.random.normal(_jax.random.fold_in(key, 0), shape, _jnp.float32)
    for i, name in enumerate(TWIN_WEIGHTS):
        w = inp[name].astype(_jnp.float32)
        if MOMENT_SCALE is None:
            s = _jnp.sqrt(_jnp.mean(_jnp.square(w)) + 1e-30)
        else:
            s = MOMENT_SCALE[name]
        km, kv = _jax.random.split(_jax.random.fold_in(key, i + 1))
        out[name] = w
        out["m_" + name] = s * _jax.random.normal(km, w.shape, _jnp.float32)
        out["v_" + name] = (s * s) * _jax.random.uniform(kv, w.shape, _jnp.float32, 0.5, 1.5)
    if N_MICROBATCH > 1:
        for name, axis in PER_EXAMPLE_BATCH_AXIS.items():
            out[name] = _to_microbatches(out[name], axis)
    return {'x': out['x'], 'mem': out['mem'], 'mix_norm_g': out['mix_norm_g'], 'w_in': out['w_in'], 'conv_dw_w': out['conv_dw_w'], 'conv_dw_b': out['conv_dw_b'], 'conv_ln_g': out['conv_ln_g'], 'conv_ln_b': out['conv_ln_b'], 'w_conv_out': out['w_conv_out'], 'w_pool_grp': out['w_pool_grp'], 'pool_scale': out['pool_scale'], 'w_out': out['w_out'], 'xattn_norm_g': out['xattn_norm_g'], 'mem_norm_g': out['mem_norm_g'], 'w_q': out['w_q'], 'w_kv': out['w_kv'], 'w_o': out['w_o'], 'ffn_norm_g': out['ffn_norm_g'], 'w_up': out['w_up'], 'ffn_dw_w': out['ffn_dw_w'], 'w_down': out['w_down'], 'final_norm_g': out['final_norm_g'], 'loss_target': out['loss_target'], 'm_mix_norm_g': out['m_mix_norm_g'], 'm_w_in': out['m_w_in'], 'm_conv_dw_w': out['m_conv_dw_w'], 'm_conv_dw_b': out['m_conv_dw_b'], 'm_conv_ln_g': out['m_conv_ln_g'], 'm_conv_ln_b': out['m_conv_ln_b'], 'm_w_conv_out': out['m_w_conv_out'], 'm_w_pool_grp': out['m_w_pool_grp'], 'm_pool_scale': out['m_pool_scale'], 'm_w_out': out['m_w_out'], 'm_xattn_norm_g': out['m_xattn_norm_g'], 'm_mem_norm_g': out['m_mem_norm_g'], 'm_w_q': out['m_w_q'], 'm_w_kv': out['m_w_kv'], 'm_w_o': out['m_w_o'], 'm_ffn_norm_g': out['m_ffn_norm_g'], 'm_w_up': out['m_w_up'], 'm_ffn_dw_w': out['m_ffn_dw_w'], 'm_w_down': out['m_w_down'], 'm_final_norm_g': out['m_final_norm_g'], 'v_mix_norm_g': out['v_mix_norm_g'], 'v_w_in': out['v_w_in'], 'v_conv_dw_w': out['v_conv_dw_w'], 'v_conv_dw_b': out['v_conv_dw_b'], 'v_conv_ln_g': out['v_conv_ln_g'], 'v_conv_ln_b': out['v_conv_ln_b'], 'v_w_conv_out': out['v_w_conv_out'], 'v_w_pool_grp': out['v_w_pool_grp'], 'v_pool_scale': out['v_pool_scale'], 'v_w_out': out['v_w_out'], 'v_xattn_norm_g': out['v_xattn_norm_g'], 'v_mem_norm_g': out['v_mem_norm_g'], 'v_w_q': out['v_w_q'], 'v_w_kv': out['v_w_kv'], 'v_w_o': out['v_w_o'], 'v_ffn_norm_g': out['v_ffn_norm_g'], 'v_w_up': out['v_w_up'], 'v_ffn_dw_w': out['v_ffn_dw_w'], 'v_w_down': out['v_w_down'], 'v_final_norm_g': out['v_final_norm_g']}


def _loss(weights, diff, rest, loss_target):
    with _jax.named_scope("forward"):
        args = {**rest, TWIN_DIFF_INPUT: diff, **{k: w.astype(_WEIGHT_DTYPES[k]) for k, w in weights.items()}}
        y = _forward(args)
    with _jax.named_scope("loss_head"):
        err = _jnp.square(y.astype(_jnp.float32) - loss_target)
        return 0.5 * _jnp.sum(_jnp.mean(err, axis=-1)) if err.ndim else 0.5 * err


def _adamw(w, g, m, v):
    m = ADAM_B1 * m + (1.0 - ADAM_B1) * g
    v = ADAM_B2 * v + (1.0 - ADAM_B2) * _jnp.square(g)
    m_hat = m / (1.0 - ADAM_B1 ** ADAM_STEP)
    v_hat = v / (1.0 - ADAM_B2 ** ADAM_STEP)
    delta = -ADAM_LR * (m_hat / (_jnp.sqrt(v_hat) + ADAM_EPS) + ADAM_WD * w)
    return delta, m, v


def reference(x, mem, mix_norm_g, w_in, conv_dw_w, conv_dw_b, conv_ln_g, conv_ln_b, w_conv_out, w_pool_grp, pool_scale, w_out, xattn_norm_g, mem_norm_g, w_q, w_kv, w_o, ffn_norm_g, w_up, ffn_dw_w, w_down, final_norm_g, loss_target, m_mix_norm_g, m_w_in, m_conv_dw_w, m_conv_dw_b, m_conv_ln_g, m_conv_ln_b, m_w_conv_out, m_w_pool_grp, m_pool_scale, m_w_out, m_xattn_norm_g, m_mem_norm_g, m_w_q, m_w_kv, m_w_o, m_ffn_norm_g, m_w_up, m_ffn_dw_w, m_w_down, m_final_norm_g, v_mix_norm_g, v_w_in, v_conv_dw_w, v_conv_dw_b, v_conv_ln_g, v_conv_ln_b, v_w_conv_out, v_w_pool_grp, v_pool_scale, v_w_out, v_xattn_norm_g, v_mem_norm_g, v_w_q, v_w_kv, v_w_o, v_ffn_norm_g, v_w_up, v_ffn_dw_w, v_w_down, v_final_norm_g):
    given = dict(x=x, mem=mem, mix_norm_g=mix_norm_g, w_in=w_in, conv_dw_w=conv_dw_w, conv_dw_b=conv_dw_b, conv_ln_g=conv_ln_g, conv_ln_b=conv_ln_b, w_conv_out=w_conv_out, w_pool_grp=w_pool_grp, pool_scale=pool_scale, w_out=w_out, xattn_norm_g=xattn_norm_g, mem_norm_g=mem_norm_g, w_q=w_q, w_kv=w_kv, w_o=w_o, ffn_norm_g=ffn_norm_g, w_up=w_up, ffn_dw_w=ffn_dw_w, w_down=w_down, final_norm_g=final_norm_g, loss_target=loss_target, m_mix_norm_g=m_mix_norm_g, m_w_in=m_w_in, m_conv_dw_w=m_conv_dw_w, m_conv_dw_b=m_conv_dw_b, m_conv_ln_g=m_conv_ln_g, m_conv_ln_b=m_conv_ln_b, m_w_conv_out=m_w_conv_out, m_w_pool_grp=m_w_pool_grp, m_pool_scale=m_pool_scale, m_w_out=m_w_out, m_xattn_norm_g=m_xattn_norm_g, m_mem_norm_g=m_mem_norm_g, m_w_q=m_w_q, m_w_kv=m_w_kv, m_w_o=m_w_o, m_ffn_norm_g=m_ffn_norm_g, m_w_up=m_w_up, m_ffn_dw_w=m_ffn_dw_w, m_w_down=m_w_down, m_final_norm_g=m_final_norm_g, v_mix_norm_g=v_mix_norm_g, v_w_in=v_w_in, v_conv_dw_w=v_conv_dw_w, v_conv_dw_b=v_conv_dw_b, v_conv_ln_g=v_conv_ln_g, v_conv_ln_b=v_conv_ln_b, v_w_conv_out=v_w_conv_out, v_w_pool_grp=v_w_pool_grp, v_pool_scale=v_pool_scale, v_w_out=v_w_out, v_xattn_norm_g=v_xattn_norm_g, v_mem_norm_g=v_mem_norm_g, v_w_q=v_w_q, v_w_kv=v_w_kv, v_w_o=v_w_o, v_ffn_norm_g=v_ffn_norm_g, v_w_up=v_w_up, v_ffn_dw_w=v_ffn_dw_w, v_w_down=v_w_down, v_final_norm_g=v_final_norm_g)
    weights = {n: given[n] for n in TWIN_WEIGHTS}
    shared = {n: given[n] for n in SHARED_INPUTS}
    per_example = {n: given[n] for n in ['x', 'mem']}
    grad_fn = _jax.value_and_grad(_loss, argnums=(0, 1))

    def one_microbatch(ex, loss_target):
        ex = dict(ex)
        diff = ex.pop(TWIN_DIFF_INPUT)
        return grad_fn(weights, diff, {**shared, **ex}, loss_target)

    if N_MICROBATCH == 1:
        loss, (grad_w, grad_x) = one_microbatch(per_example, given["loss_target"])
    else:
        def body(carry, xs):
            loss_sum, grad_sum = carry
            l_k, (gw_k, gx_k) = one_microbatch(xs[0], xs[1])
            with _jax.named_scope("update"):
                return (loss_sum + l_k, _jax.tree.map(_jnp.add, grad_sum, gw_k)), gx_k

        init = (_jnp.zeros((), _jnp.float32), _jax.tree.map(_jnp.zeros_like, weights))
        (loss, grad_w), grad_x = _jax.lax.scan(body, init, (per_example, given["loss_target"]))
    with _jax.named_scope("update"):
        delta_w, new_m, new_v = {}, {}, {}
        for n in TWIN_WEIGHTS:
            delta_w[n], new_m[n], new_v[n] = _adamw(weights[n], grad_w[n], given["m_" + n], given["v_" + n])
    return (loss, grad_x, *[grad_w[n] for n in TWIN_WEIGHTS], *[delta_w[n] for n in TWIN_WEIGHTS],
            *[new_m[n] for n in TWIN_WEIGHTS], *[new_v[n] for n in TWIN_WEIGHTS])
```

```python
import functools

import jax
import jax.numpy as jnp
from jax import lax
from jax.experimental import pallas as pl
from jax.experimental.pallas import tpu as pltpu

F32 = jnp.float32
BF = jnp.bfloat16
SDS = jax.ShapeDtypeStruct
MESH = pl.DeviceIdType.MESH

EPS = 1e-6
XA_HEADS = 4
POOL_WINDOWS = (2, 4, 8, 16)
N_CHIPS = 4
ADAM_LR, ADAM_B1, ADAM_B2, ADAM_EPS, ADAM_WD, ADAM_STEP = 0.001, 0.9, 0.999, 1e-08, 0.01, 10

LANES = 128
ROW_BLOCK = 512
VMEM_LIMIT = 56 * 1024 * 1024


def _params(*sem):
    return pltpu.CompilerParams(dimension_semantics=sem if sem else None, vmem_limit_bytes=VMEM_LIMIT)


def _tile(n, cap, mult=LANES):
    if n <= cap:
        return n
    for t in range(cap - cap % mult, 0, -mult):
        if n % t == 0:
            return t
    return n


_DN = {"nn": (((1,), (0,)), ((), ())), "nt": (((1,), (1,)), ((), ())), "tn": (((0,), (0,)), ((), ()))}


def _mm(a, b, dims, out_dtype, name, res=None, bl=None, tm=1024, tn=512, tk=1024):
    bs = b.shape[1:] if bl is not None else b.shape
    if dims == "nn":
        (M, K), (K2, N) = a.shape, bs
    elif dims == "nt":
        (M, K), (N, K2) = a.shape, bs
    else:
        (K, M), (K2, N) = a.shape, bs
    assert K == K2, (name, a.shape, b.shape)
    tm, tn, tk = _tile(M, tm), _tile(N, tn), _tile(K, tk)
    nk = K // tk
    lead = (None,) if bl is not None else ()
    pre = (lambda *ix: (bl,) + ix) if bl is not None else (lambda *ix: ix)
    if dims == "tn":
        a_spec = pl.BlockSpec((tk, tm), lambda i, j, k: (k, i))
    else:
        a_spec = pl.BlockSpec((tm, tk), lambda i, j, k: (i, k))
    if dims == "nt":
        b_spec = pl.BlockSpec(lead + (tn, tk), lambda i, j, k: pre(j, k))
    else:
        b_spec = pl.BlockSpec(lead + (tk, tn), lambda i, j, k: pre(k, j))
    o_spec = pl.BlockSpec((tm, tn), lambda i, j, k: (i, j))
    in_specs, args = [a_spec, b_spec], [a, b]
    if res is not None:
        in_specs.append(o_spec)
        args.append(res)

    def body(*refs):
        a_ref, b_ref = refs[0], refs[1]
        r_ref = refs[2] if res is not None else None
        o_ref = refs[3] if res is not None else refs[2]
        p = lax.dot_general(a_ref[...].astype(BF), b_ref[...].astype(BF), _DN[dims], preferred_element_type=F32)

        def finish(t):
            if r_ref is not None:
                t = t + r_ref[...]
            o_ref[...] = t.astype(out_dtype)

        if nk == 1:
            finish(p)
        else:
            acc = refs[-1]
            k = pl.program_id(2)

            @pl.when(k == 0)
            def _():
                acc[...] = p

            @pl.when(k > 0)
            def _():
                acc[...] += p

            @pl.when(k == nk - 1)
            def _():
                finish(acc[...])

    return pl.pallas_call(
        body, grid=(M // tm, N // tn, nk), in_specs=in_specs, out_specs=o_spec, out_shape=SDS((M, N), out_dtype),
        scratch_shapes=[pltpu.VMEM((tm, tn), F32)] if nk > 1 else [],
        compiler_params=_params("parallel", "parallel", "arbitrary"), name=name)(*args)


def _rms(x, g):
    return x * lax.rsqrt(jnp.mean(x * x, axis=-1, keepdims=True) + EPS) * g


def _ln_silu(x, g, b):
    mu = jnp.mean(x, axis=-1, keepdims=True)
    xc = x - mu
    var = jnp.mean(xc * xc, axis=-1, keepdims=True)
    return jax.nn.silu(xc * lax.rsqrt(var + EPS) * g + b)


def _merge(gc, gp, yc, yp, ps):
    return jax.nn.sigmoid(gc) * yc + jax.nn.sigmoid(gp) * (yp * ps)


def _gated(gate, val):
    return jax.nn.gelu(gate) * val


def _rms_fwd(x, g, name):
    T, D = x.shape
    tb = _tile(T, ROW_BLOCK, 8)

    def body(x_ref, g_ref, o_ref):
        o_ref[...] = _rms(x_ref[...], g_ref[...]).astype(BF)

    row = pl.BlockSpec((tb, D), lambda i: (i, 0))
    return pl.pallas_call(body, grid=(T // tb,), in_specs=[row, pl.BlockSpec((1, D), lambda i: (0, 0))], out_specs=row,
                          out_shape=SDS((T, D), BF), compiler_params=_params("parallel"), name=name)(x, g.reshape(1, D))


def _rms_bwd(x, g, dh, dres, name):
    T, D = x.shape
    tb = _tile(T, ROW_BLOCK, 8)

    def body(*refs):
        if dres is not None:
            x_ref, g_ref, dh_ref, dres_ref, dx_ref, dxb_ref, dg_ref = refs
        else:
            x_ref, g_ref, dh_ref, dx_ref, dxb_ref, dg_ref = refs
        _, vjp = jax.vjp(_rms, x_ref[...], g_ref[...])
        dx, dg = vjp(dh_ref[...].astype(F32))
        if dres is not None:
            dx = dx + dres_ref[...]
        dx_ref[...] = dx
        dxb_ref[...] = dx.astype(BF)

        @pl.when(pl.program_id(0) == 0)
        def _():
            dg_ref[...] = jnp.zeros_like(dg_ref)

        dg_ref[...] += dg

    row = pl.BlockSpec((tb, D), lambda i: (i, 0))
    vec = pl.BlockSpec((1, D), lambda i: (0, 0))
    ins = [x, g.reshape(1, D), dh] + ([dres] if dres is not None else [])
    return pl.pallas_call(
        body, grid=(T // tb,), in_specs=[row, vec, row] + ([row] if dres is not None else []), out_specs=[row, row, vec],
        out_shape=[SDS((T, D), F32), SDS((T, D), BF), SDS((1, D), F32)], compiler_params=_params("arbitrary"), name=name)(*ins)


def _loss_bwd(x, g, target, name):
    T, D = x.shape
    tb = _tile(T, ROW_BLOCK, 8)
    nb = T // tb

    def body(x_ref, g_ref, t_ref, loss_ref, dx_ref, dg_ref, acc):
        i = pl.program_id(0)
        y, vjp = jax.vjp(_rms, x_ref[...], g_ref[...])
        err = y - t_ref[...]
        dx, dg = vjp(err * (1.0 / D))
        dx_ref[...] = dx

        @pl.when(i == 0)
        def _():
            dg_ref[...] = jnp.zeros_like(dg_ref)
            acc[...] = jnp.zeros_like(acc)

        dg_ref[...] += dg
        acc[...] += jnp.sum(err * err, axis=0, keepdims=True)

        @pl.when(i == nb - 1)
        def _():
            loss_ref[...] = jnp.full(loss_ref.shape, (0.5 / D) * jnp.sum(acc[...]), F32)

    row = pl.BlockSpec((tb, D), lambda i: (i, 0))
    vec = pl.BlockSpec((1, D), lambda i: (0, 0))
    return pl.pallas_call(
        body, grid=(nb,), in_specs=[row, vec, row], out_specs=[pl.BlockSpec((1, LANES), lambda i: (0, 0)), row, vec],
        out_shape=[SDS((1, LANES), F32), SDS((T, D), F32), SDS((1, D), F32)], scratch_shapes=[pltpu.VMEM((1, D), F32)],
        compiler_params=_params("arbitrary"), name=name)(x, g.reshape(1, D), target)


def _ln_silu_fwd(cv, g, b, name):
    T, C = cv.shape
    tb = _tile(T, ROW_BLOCK, 8)

    def body(x_ref, g_ref, b_ref, o_ref):
        o_ref[...] = _ln_silu(x_ref[...], g_ref[...], b_ref[...]).astype(BF)

    row = pl.BlockSpec((tb, C), lambda i: (i, 0))
    vec = pl.BlockSpec((1, C), lambda i: (0, 0))
    return pl.pallas_call(body, grid=(T // tb,), in_specs=[row, vec, vec], out_specs=row, out_shape=SDS((T, C), BF),
                          compiler_params=_params("parallel"), name=name)(cv, g.reshape(1, C), b.reshape(1, C))


def _ln_silu_bwd(cv, g, b, dy, name):
    T, C = cv.shape
    tb = _tile(T, ROW_BLOCK, 8)

    def body(x_ref, g_ref, b_ref, dy_ref, dx_ref, dg_ref, db_ref):
        _, vjp = jax.vjp(_ln_silu, x_ref[...], g_ref[...], b_ref[...])
        dx, dg, db = vjp(dy_ref[...].astype(F32))
        dx_ref[...] = dx

        @pl.when(pl.program_id(0) == 0)
        def _():
            dg_ref[...] = jnp.zeros_like(dg_ref)
            db_ref[...] = jnp.zeros_like(db_ref)

        dg_ref[...] += dg
        db_ref[...] += db

    row = pl.BlockSpec((tb, C), lambda i: (i, 0))
    vec = pl.BlockSpec((1, C), lambda i: (0, 0))
    return pl.pallas_call(
        body, grid=(T // tb,), in_specs=[row, vec, vec, row], out_specs=[row, vec, vec],
        out_shape=[SDS((T, C), F32), SDS((1, C), F32), SDS((1, C), F32)], compiler_params=_params("arbitrary"),
        name=name)(cv, g.reshape(1, C), b.reshape(1, C), dy)


def _merge_fwd(proj, yc, yp, ps, C, name):
    T, D = yc.shape
    tb = _tile(T, ROW_BLOCK, 8)
    nj = D // C

    def body(gc_ref, gp_ref, yc_ref, yp_ref, ps_ref, o_ref):
        o_ref[...] = _merge(gc_ref[...], gp_ref[...], yc_ref[...], yp_ref[...], ps_ref[...]).astype(BF)

    blk = pl.BlockSpec((tb, C), lambda i, j: (i, j))
    return pl.pallas_call(
        body, grid=(T // tb, nj),
        in_specs=[pl.BlockSpec((tb, C), lambda i, j: (i, 3 + j)), pl.BlockSpec((tb, C), lambda i, j: (i, 3 + nj + j)), blk, blk,
                  pl.BlockSpec((1, C), lambda i, j: (0, j))],
        out_specs=blk, out_shape=SDS((T, D), BF), compiler_params=_params("parallel", "parallel"), name=name)(proj, proj, yc, yp, ps.reshape(1, D))


def _merge_bwd(proj, yc, yp, ps, dm, C, name):
    T, D = yc.shape
    tb = _tile(T, ROW_BLOCK, 8)
    nj = D // C

    def body(gc_ref, gp_ref, yc_ref, yp_ref, ps_ref, dm_ref, dgc_ref, dgp_ref, dyc_ref, dyp_ref, dps_ref):
        _, vjp = jax.vjp(_merge, gc_ref[...], gp_ref[...], yc_ref[...], yp_ref[...], ps_ref[...])
        dgc, dgp, dyc, dyp, dps = vjp(dm_ref[...].astype(F32))
        dgc_ref[...] = dgc.astype(BF)
        dgp_ref[...] = dgp.astype(BF)
        dyc_ref[...] = dyc.astype(BF)
        dyp_ref[...] = dyp.astype(BF)

        @pl.when(pl.program_id(1) == 0)
        def _():
            dps_ref[...] = jnp.zeros_like(dps_ref)

        dps_ref[...] += dps

    blk = pl.BlockSpec((tb, C), lambda j, i: (i, j))
    vec = pl.BlockSpec((1, C), lambda j, i: (0, j))
    return pl.pallas_call(
        body, grid=(nj, T // tb),
        in_specs=[pl.BlockSpec((tb, C), lambda j, i: (i, 3 + j)), pl.BlockSpec((tb, C), lambda j, i: (i, 3 + nj + j)), blk, blk, vec, blk],
        out_specs=[blk, blk, blk, blk, vec], out_shape=[SDS((T, D), BF)] * 4 + [SDS((1, D), F32)],
        compiler_params=_params("parallel", "arbitrary"), name=name)(proj, proj, yc, yp, ps.reshape(1, D), dm)


def _shd(v, s, rows):
    if s == 0:
        return v
    return jnp.where(rows >= s, pltpu.roll(v, s, 0), 0.0)


def _shu(v, s, rows):
    if s == 0:
        return v
    n = v.shape[0]
    return jnp.where(rows < n - s, pltpu.roll(v, n - s, 0), 0.0)


def _glu_conv_fwd(proj, w, b, Bn, S, C, name):
    K = w.shape[0]
    sl = min(LANES, C)
    ns = C // sl

    def body(a_ref, gl_ref, w_ref, b_ref, o_ref):
        y0 = a_ref[...] * jax.nn.sigmoid(gl_ref[...])
        rows = lax.broadcasted_iota(jnp.int32, y0.shape, 0)
        acc = jnp.zeros_like(y0) + b_ref[...]
        for k in range(K):
            acc = acc + w_ref[k:k + 1, :] * _shd(y0, K - 1 - k, rows)
        o_ref[...] = acc

    return pl.pallas_call(
        body, grid=(Bn, ns),
        in_specs=[pl.BlockSpec((S, sl), lambda bi, j: (bi, j)), pl.BlockSpec((S, sl), lambda bi, j: (bi, ns + j)),
                  pl.BlockSpec((K, sl), lambda bi, j: (0, j)), pl.BlockSpec((1, sl), lambda bi, j: (0, j))],
        out_specs=pl.BlockSpec((S, sl), lambda bi, j: (bi, j)), out_shape=SDS((Bn * S, C), F32),
        compiler_params=_params("parallel", "parallel"), name=name)(proj, proj, w, b.reshape(1, C))


def _glu_conv_bwd(proj, w, dcv, Bn, S, C, name):
    K = w.shape[0]
    sl = min(LANES, C)
    ns = C // sl

    def body(a_ref, gl_ref, w_ref, d_ref, da_ref, dgl_ref, dw_ref, db_ref):
        a = a_ref[...]
        sg = jax.nn.sigmoid(gl_ref[...])
        y0 = a * sg
        d = d_ref[...]
        rows = lax.broadcasted_iota(jnp.int32, y0.shape, 0)

        @pl.when(pl.program_id(1) == 0)
        def _():
            dw_ref[...] = jnp.zeros_like(dw_ref)
            db_ref[...] = jnp.zeros_like(db_ref)

        dy0 = jnp.zeros_like(y0)
        for k in range(K):
            s = K - 1 - k
            dw_ref[k:k + 1, :] += jnp.sum(d * _shd(y0, s, rows), axis=0, keepdims=True)
            dy0 = dy0 + w_ref[k:k + 1, :] * _shu(d, s, rows)
        db_ref[...] += jnp.sum(d, axis=0, keepdims=True)
        da_ref[...] = (dy0 * sg).astype(BF)
        dgl_ref[...] = (dy0 * a * sg * (1.0 - sg)).astype(BF)

    blk = pl.BlockSpec((S, sl), lambda j, bi: (bi, j))
    return pl.pallas_call(
        body, grid=(ns, Bn),
        in_specs=[blk, pl.BlockSpec((S, sl), lambda j, bi: (bi, ns + j)), pl.BlockSpec((K, sl), lambda j, bi: (0, j)), blk],
        out_specs=[blk, blk, pl.BlockSpec((K, sl), lambda j, bi: (0, j)), pl.BlockSpec((1, sl), lambda j, bi: (0, j))],
        out_shape=[SDS((Bn * S, C), BF), SDS((Bn * S, C), BF), SDS((K, C), F32), SDS((1, C), F32)],
        compiler_params=_params("parallel", "arbitrary"), name=name)(proj, proj, w, dcv)


def _pool_z(u, g, rows):
    s2 = u + _shd(u, 1, rows)
    s4 = s2 + _shd(s2, 2, rows)
    s8 = s4 + _shd(s4, 4, rows)
    s16 = s8 + _shd(s8, 8, rows)
    sw = jnp.where(g == 0, s2, jnp.where(g == 1, s4, jnp.where(g == 2, s8, s16)))
    cnt = jnp.minimum(rows + 1, POOL_WINDOWS[0] << g).astype(F32)
    return sw / cnt - u, cnt


def _pool_fwd(proj, wpt, l, Bn, S, C, D, name):
    G = len(POOL_WINDOWS)
    gd, go = C // G, D // G

    def body(u_ref, w_ref, o_ref):
        g = pl.program_id(1)
        u = u_ref[...]
        rows = lax.broadcasted_iota(jnp.int32, u.shape, 0)
        zp, _ = _pool_z(u, g, rows)
        o_ref[...] = lax.dot_general(zp.astype(BF), w_ref[...], _DN["nt"], preferred_element_type=F32)

    return pl.pallas_call(
        body, grid=(Bn, G),
        in_specs=[pl.BlockSpec((S, gd), lambda bi, g: (bi, 2 * G + g)), pl.BlockSpec((None, go, gd), lambda bi, g: (l * G + g, 0, 0))],
        out_specs=pl.BlockSpec((S, go), lambda bi, g: (bi, g)), out_shape=SDS((Bn * S, D), F32),
        compiler_params=_params("parallel", "parallel"), name=name)(proj, wpt)


def _pool_bwd(proj, wpt, dyp, l, Bn, S, C, D, name):
    G = len(POOL_WINDOWS)
    gd, go = C // G, D // G

    def body(u_ref, w_ref, d_ref, du_ref, dw_ref):
        g = pl.program_id(0)
        u = u_ref[...]
        rows = lax.broadcasted_iota(jnp.int32, u.shape, 0)
        zp, cnt = _pool_z(u, g, rows)
        d = d_ref[...]
        dzp = lax.dot_general(d, w_ref[...], _DN["nn"], preferred_element_type=F32)

        @pl.when(pl.program_id(1) == 0)
        def _():
            dw_ref[...] = jnp.zeros_like(dw_ref)

        dw_ref[...] += lax.dot_general(d, zp.astype(BF), _DN["tn"], preferred_element_type=F32)
        dsw = dzp / cnt
        zero = jnp.zeros_like(dsw)
        d16 = jnp.where(g == 3, dsw, zero)
        d8 = jnp.where(g == 2, dsw, zero) + d16 + _shu(d16, 8, rows)
        d4 = jnp.where(g == 1, dsw, zero) + d8 + _shu(d8, 4, rows)
        d2 = jnp.where(g == 0, dsw, zero) + d4 + _shu(d4, 2, rows)
        d1 = d2 + _shu(d2, 1, rows)
        du_ref[...] = (d1 - dzp).astype(BF)

    return pl.pallas_call(
        body, grid=(G, Bn),
        in_specs=[pl.BlockSpec((S, gd), lambda g, bi: (bi, 2 * G + g)), pl.BlockSpec((None, go, gd), lambda g, bi: (l * G + g, 0, 0)),
                  pl.BlockSpec((S, go), lambda g, bi: (bi, g))],
        out_specs=[pl.BlockSpec((S, gd), lambda g, bi: (bi, g)), pl.BlockSpec((None, go, gd), lambda g, bi: (g, 0, 0))],
        out_shape=[SDS((Bn * S, C), BF), SDS((G, go, gd), F32)],
        compiler_params=_params("parallel", "arbitrary"), name=name)(proj, wpt, dyp)


def _ffn_conv(u, w_ref, rows):
    K = w_ref.shape[0]
    acc = w_ref[K - 1:K, :] * u
    for k in range(K - 1):
        acc = acc + w_ref[k:k + 1, :] * _shd(u, K - 1 - k, rows)
    return acc


def _ffn_cb(F):
    return _tile(F, 256)


def _ffn_act_fwd(up0, w, Bn, S, F, name):
    cb = _ffn_cb(F)
    nj = F // cb

    def body(g_ref, v_ref, wg_ref, wv_ref, o_ref):
        rows = lax.broadcasted_iota(jnp.int32, g_ref.shape, 0)
        o_ref[...] = _gated(_ffn_conv(g_ref[...], wg_ref, rows), _ffn_conv(v_ref[...], wv_ref, rows)).astype(BF)

    K = w.shape[0]
    return pl.pallas_call(
        body, grid=(Bn, nj),
        in_specs=[pl.BlockSpec((S, cb), lambda bi, j: (bi, j)), pl.BlockSpec((S, cb), lambda bi, j: (bi, nj + j)),
                  pl.BlockSpec((K, cb), lambda bi, j: (0, j)), pl.BlockSpec((K, cb), lambda bi, j: (0, nj + j))],
        out_specs=pl.BlockSpec((S, cb), lambda bi, j: (bi, j)), out_shape=SDS((Bn * S, F), BF),
        compiler_params=_params("parallel", "parallel"), name=name)(up0, up0, w, w)


def _ffn_act_bwd(up0, w, dg, Bn, S, F, name):
    cb = _ffn_cb(F)
    nj = F // cb
    K = w.shape[0]

    def body(g_ref, v_ref, wg_ref, wv_ref, d_ref, dgo_ref, dvo_ref, dwg_ref, dwv_ref):
        rows = lax.broadcasted_iota(jnp.int32, g_ref.shape, 0)
        g0, v0 = g_ref[...], v_ref[...]
        _, vjp = jax.vjp(_gated, _ffn_conv(g0, wg_ref, rows), _ffn_conv(v0, wv_ref, rows))
        dgc, dvc = vjp(d_ref[...].astype(F32))

        @pl.when(pl.program_id(1) == 0)
        def _():
            dwg_ref[...] = jnp.zeros_like(dwg_ref)
            dwv_ref[...] = jnp.zeros_like(dwv_ref)

        for u0, dc, w_ref, dw_ref, do_ref in ((g0, dgc, wg_ref, dwg_ref, dgo_ref), (v0, dvc, wv_ref, dwv_ref, dvo_ref)):
            du = jnp.zeros_like(u0)
            for k in range(K):
                s = K - 1 - k
                dw_ref[k:k + 1, :] += jnp.sum(dc * _shd(u0, s, rows), axis=0, keepdims=True)
                du = du + w_ref[k:k + 1, :] * _shu(dc, s, rows)
            do_ref[...] = du.astype(BF)

    blk = pl.BlockSpec((S, cb), lambda j, bi: (bi, j))
    wblk = pl.BlockSpec((K, cb), lambda j, bi: (0, j))
    return pl.pallas_call(
        body, grid=(nj, Bn),
        in_specs=[blk, pl.BlockSpec((S, cb), lambda j, bi: (bi, nj + j)), wblk, pl.BlockSpec((K, cb), lambda j, bi: (0, nj + j)), blk],
        out_specs=[blk, blk, wblk, wblk],
        out_shape=[SDS((Bn * S, F), BF), SDS((Bn * S, F), BF), SDS((K, F), F32), SDS((K, F), F32)],
        compiler_params=_params("parallel", "arbitrary"), name=name)(up0, up0, w, w, dg)


def _softmax_rows(q, k, scale):
    sc = lax.dot_general(q, k, _DN["nt"], preferred_element_type=F32) * scale
    e = jnp.exp(sc - jnp.max(sc, axis=-1, keepdims=True))
    return e / jnp.sum(e, axis=-1, keepdims=True)


def _attn_ts(S):
    return _tile(S, 1024, 8)


def _attn_fwd(q, kv, Bn, S, Mn, D, name):
    H = XA_HEADS
    dh = D // H
    ts = _attn_ts(S)
    nsb = S // ts
    scale = dh ** -0.5

    def body(q_ref, k_ref, v_ref, o_ref):
        p = _softmax_rows(q_ref[...], k_ref[...], scale)
        o_ref[...] = lax.dot_general(p.astype(BF), v_ref[...], _DN["nn"], preferred_element_type=F32).astype(BF)

    qblk = pl.BlockSpec((ts, dh), lambda bi, h, s: (bi * nsb + s, h))
    return pl.pallas_call(
        body, grid=(Bn, H, nsb),
        in_specs=[qblk, pl.BlockSpec((Mn, dh), lambda bi, h, s: (bi, h)), pl.BlockSpec((Mn, dh), lambda bi, h, s: (bi, H + h))],
        out_specs=qblk, out_shape=SDS((Bn * S, D), BF), compiler_params=_params("parallel", "parallel", "parallel"), name=name)(q, kv, kv)


def _attn_bwd(q, kv, datt, Bn, S, Mn, D, name):
    H = XA_HEADS
    dh = D // H
    ts = _attn_ts(S)
    nsb = S // ts
    scale = dh ** -0.5

    def body(q_ref, k_ref, v_ref, do_ref, dq_ref, dk_ref, dv_ref):
        q, k, v, do = q_ref[...], k_ref[...], v_ref[...], do_ref[...]
        p = _softmax_rows(q, k, scale)
        dp = lax.dot_general(do, v, _DN["nt"], preferred_element_type=F32)
        ds = (p * (dp - jnp.sum(dp * p, axis=-1, keepdims=True)) * scale).astype(BF)
        dq_ref[...] = lax.dot_general(ds, k, _DN["nn"], preferred_element_type=F32).astype(BF)

        @pl.when(pl.program_id(2) == 0)
        def _():
            dk_ref[...] = jnp.zeros_like(dk_ref)
            dv_ref[...] = jnp.zeros_like(dv_ref)

        dk_ref[...] += lax.dot_general(ds, q, _DN["tn"], preferred_element_type=F32)
        dv_ref[...] += lax.dot_general(p.astype(BF), do, _DN["tn"], preferred_element_type=F32)

    qblk = pl.BlockSpec((ts, dh), lambda bi, h, s: (bi * nsb + s, h))
    kblk = pl.BlockSpec((Mn, dh), lambda bi, h, s: (bi, h))
    return pl.pallas_call(
        body, grid=(Bn, H, nsb),
        in_specs=[qblk, kblk, pl.BlockSpec((Mn, dh), lambda bi, h, s: (bi, H + h)), qblk],
        out_specs=[qblk, kblk, kblk], out_shape=[SDS((Bn * S, D), BF), SDS((Bn * Mn, D), F32), SDS((Bn * Mn, D), F32)],
        compiler_params=_params("parallel", "parallel", "arbitrary"), name=name)(q, kv, kv, datt)


def _local_step(x, mem, target, W, V):
    Bn, S, D = x.shape
    Mn = mem.shape[1]
    T = Bn * S
    L = W["w_in"].shape[0]
    C = V["conv_dw_b"].shape[1]
    F = W["w_down"].shape[1]
    x = x.reshape(T, D)
    memf = mem.reshape(Bn * Mn, D)
    target = target.reshape(T, D)

    mem_n = _rms_fwd(memf, V["mem_norm_g"], "mem_norm")
    saved = []
    for l in range(L):
        n = f"l{l}_"
        h = _rms_fwd(x, V["mix_norm_g"][l], n + "mix_norm")
        proj = _mm(h, W["w_in"], "nn", F32, n + "proj", bl=l, tn=C)
        cv = _glu_conv_fwd(proj, V["conv_dw_w"][l], V["conv_dw_b"][l], Bn, S, C, n + "glu_conv")
        yc1 = _ln_silu_fwd(cv, V["conv_ln_g"][l], V["conv_ln_b"][l], n + "ln_silu")
        yc = _mm(yc1, W["w_conv_out"], "nn", F32, n + "conv_out", bl=l)
        yp = _pool_fwd(proj, W["w_pool"], l, Bn, S, C, D, n + "pool")
        merged = _merge_fwd(proj, yc, yp, V["pool_scale"][l], C, n + "merge")
        x1 = _mm(merged, W["w_out"], "nn", F32, n + "out_proj", res=x, bl=l)
        hq = _rms_fwd(x1, V["xattn_norm_g"][l], n + "xattn_norm")
        q = _mm(hq, W["w_q"], "nn", BF, n + "q_proj", bl=l)
        kv = _mm(mem_n, W["w_kv"], "nn", BF, n + "kv_proj", bl=l)
        att = _attn_fwd(q, kv, Bn, S, Mn, D, n + "attn")
        x2 = _mm(att, W["w_o"], "nn", F32, n + "o_proj", res=x1, bl=l)
        hf = _rms_fwd(x2, V["ffn_norm_g"][l], n + "ffn_norm")
        up0 = _mm(hf, W["w_up"], "nn", F32, n + "up_proj", bl=l)
        gact = _ffn_act_fwd(up0, V["ffn_dw_w"][l], Bn, S, F, n + "ffn_act")
        x3 = _mm(gact, W["w_down"], "nn", F32, n + "down_proj", res=x2, bl=l, tk=1408)
        saved.append(dict(x=x, h=h, proj=proj, cv=cv, yc1=yc1, yc=yc, yp=yp, merged=merged, x1=x1, hq=hq, q=q, kv=kv, att=att,
                          x2=x2, hf=hf, up0=up0, gact=gact))
        x = x3

    loss, dx, dgf = _loss_bwd(x, V["final_norm_g"], target, "loss")
    dxb = dx
    big = [None] * L
    small = {k: [None] * L for k in ("mix_norm_g", "conv_dw_w", "conv_dw_b", "conv_ln_g", "conv_ln_b", "pool_scale", "xattn_norm_g",
                                    "ffn_norm_g", "ffn_dw_w")}
    dmem_n = None
    for l in reversed(range(L)):
        n = f"l{l}_b_"
        sv = saved[l]
        gw = {}
        dgact = _mm(dxb, W["w_down"], "nt", BF, n + "d_gact", bl=l)
        gw["w_down"] = _mm(sv["gact"], dxb, "tn", F32, n + "dw_down")
        dg0, dv0, dwg, dwv = _ffn_act_bwd(sv["up0"], V["ffn_dw_w"][l], dgact, Bn, S, F, n + "ffn_act")
        small["ffn_dw_w"][l] = jnp.concatenate([dwg, dwv], axis=1)
        dup0 = jnp.concatenate([dg0, dv0], axis=1)
        dhf = _mm(dup0, W["w_up"], "nt", F32, n + "d_hf", bl=l, tk=1408)
        gw["w_up"] = _mm(sv["hf"], dup0, "tn", F32, n + "dw_up")
        dx2, dx2b, small["ffn_norm_g"][l] = _rms_bwd(sv["x2"], V["ffn_norm_g"][l], dhf, dx, n + "ffn_norm")
        datt = _mm(dx2b, W["w_o"], "nt", BF, n + "d_att", bl=l)
        gw["w_o"] = _mm(sv["att"], dx2b, "tn", F32, n + "dw_o")
        dq, dk, dv = _attn_bwd(sv["q"], sv["kv"], datt, Bn, S, Mn, D, n + "attn")
        dkv = jnp.concatenate([dk, dv], axis=1)
        gw["w_kv"] = _mm(mem_n, dkv, "tn", F32, n + "dw_kv")
        dmem_n = _mm(dkv, W["w_kv"], "nt", F32, n + "d_mem", res=dmem_n, bl=l)
        dhq = _mm(dq, W["w_q"], "nt", F32, n + "d_hq", bl=l)
        gw["w_q"] = _mm(sv["hq"], dq, "tn", F32, n + "dw_q")
        dx1, dx1b, small["xattn_norm_g"][l] = _rms_bwd(sv["x1"], V["xattn_norm_g"][l], dhq, dx2, n + "xattn_norm")
        dmerged = _mm(dx1b, W["w_out"], "nt", BF, n + "d_merged", bl=l)
        gw["w_out"] = _mm(sv["merged"], dx1b, "tn", F32, n + "dw_out")
        dgc, dgp, dyc, dyp, small["pool_scale"][l] = _merge_bwd(sv["proj"], sv["yc"], sv["yp"], V["pool_scale"][l], dmerged, C, n + "merge")
        du, gw["w_pool"] = _pool_bwd(sv["proj"], W["w_pool"], dyp, l, Bn, S, C, D, n + "pool")
        dyc1 = _mm(dyc, W["w_conv_out"], "nt", F32, n + "d_yc1", bl=l)
        gw["w_conv_out"] = _mm(sv["yc1"], dyc, "tn", F32, n + "dw_conv_out")
        dcv, small["conv_ln_g"][l], small["conv_ln_b"][l] = _ln_silu_bwd(sv["cv"], V["conv_ln_g"][l], V["conv_ln_b"][l], dyc1, n + "ln_silu")
        da, dgl, small["conv_dw_w"][l], small["conv_dw_b"][l] = _glu_conv_bwd(sv["proj"], V["conv_dw_w"][l], dcv, Bn, S, C, n + "glu_conv")
        dproj = jnp.concatenate([da, dgl, du, dgc, dgp], axis=1)
        dh = _mm(dproj, W["w_in"], "nt", F32, n + "d_h", bl=l, tk=896)
        gw["w_in"] = _mm(sv["h"], dproj, "tn", F32, n + "dw_in")
        dx, dxb, small["mix_norm_g"][l] = _rms_bwd(sv["x"], V["mix_norm_g"][l], dh, dx1, n + "mix_norm")
        big[l] = gw
    _, _, dgm = _rms_bwd(memf, V["mem_norm_g"], dmem_n, None, "mem_norm_b")
    small = {k: jnp.stack(v) if k in ("conv_dw_w", "ffn_dw_w") else jnp.concatenate(v, axis=0) for k, v in small.items()}
    small["mem_norm_g"] = dgm
    small["final_norm_g"] = dgf
    return loss, dx.reshape(Bn, S, D), big, small


BIG = (("w_in", "col"), ("w_conv_out", "col"), ("w_pool", "row"), ("w_out", "row"), ("w_q", "row"), ("w_kv", "col"),
       ("w_o", "row"), ("w_up", "col"), ("w_down", "row"))
ANY = pl.BlockSpec(memory_space=pl.ANY)


def _place():
    xi, yi, ci = lax.axis_index("x"), lax.axis_index("y"), lax.axis_index("c")
    return xi, yi, ci, 2 * xi + yi


def _chip_peer(xi, yi, ci, r):
    return (xi ^ (r >> 1), yi ^ (r & 1), ci)


def _full_shard(ref, kind, k, cs):
    if kind == "col":
        return ref.at[:, :, :, :, pl.ds(pl.multiple_of(k * cs, cs), cs)]
    return ref.at[:, :, k]


def _gather_weights(shards, kinds):
    n = len(shards)
    outs = []
    for s, kind in zip(shards, kinds):
        L, P, _, RH, CS = s.shape
        outs.append(SDS((L, P, 2, RH, CS * N_CHIPS) if kind == "col" else (L, P, N_CHIPS, 2, RH, CS), s.dtype))

    def body(*refs):
        srcs, fulls, (ssem, rsem, lsem) = refs[:n], refs[n:2 * n], refs[2 * n:]
        xi, yi, ci, j = _place()
        sib = (xi, yi, 1 - ci)

        def piece(i, k, c):
            kind, cs = kinds[i], shards[i].shape[-1]
            if kind == "col":
                return fulls[i].at[:, :, c, :, pl.ds(pl.multiple_of(k * cs, cs), cs)]
            return fulls[i].at[:, :, k, c]

        def copy(i, slot, src, dst, dev):
            return pltpu.make_async_remote_copy(src_ref=src, dst_ref=dst, send_sem=ssem.at[6 * i + slot], recv_sem=rsem.at[6 * i + slot],
                                                device_id=dev, device_id_type=MESH)

        own, first, passed = [], [], []
        for i in range(n):
            cs = shards[i].shape[-1]
            own.append(pltpu.make_async_copy(srcs[i], _full_shard(fulls[i], kinds[i], j, cs), lsem.at[i]))
            own[-1].start()
            for r in (1, 2, 3):
                first.append(copy(i, r - 1, srcs[i].at[:, :, ci], piece(i, j, ci), _chip_peer(xi, yi, ci, r)))
                first[-1].start()
        for i in range(n):
            for r in (1, 2, 3):
                got = piece(i, j ^ r, ci)
                copy(i, r - 1, got, got, sib).wait_recv()
                passed.append(copy(i, 2 + r, got, got, sib))
                passed[-1].start()
        for i in range(n):
            for r in (1, 2, 3):
                got = piece(i, j ^ r, 1 - ci)
                copy(i, 2 + r, got, got, sib).wait_recv()
        for cp in first + passed:
            cp.wait_send()
        for cp in own:
            cp.wait()

    return pl.pallas_call(
        body, in_specs=[ANY] * n, out_specs=[ANY] * n, out_shape=outs,
        scratch_shapes=[pltpu.SemaphoreType.DMA((6 * n,)), pltpu.SemaphoreType.DMA((6 * n,)), pltpu.SemaphoreType.DMA((n,))],
        name="gather_weights")(*shards)


def _sibling_exchange(gviews, kinds):
    n = len(gviews)
    outs = [SDS(g.shape[:1] + g.shape[2:] if kind == "col" else g.shape[:2] + g.shape[3:], g.dtype) for g, kind in zip(gviews, kinds)]

    def body(*refs):
        gs, lands, (ssem, rsem) = refs[:n], refs[n:2 * n], refs[2 * n:]
        xi, yi, ci, _ = _place()
        cps = []
        for i in range(n):
            src = gs[i].at[:, 1 - ci] if kinds[i] == "col" else gs[i].at[:, :, 1 - ci]
            cps.append(pltpu.make_async_remote_copy(src_ref=src, dst_ref=lands[i], send_sem=ssem.at[i], recv_sem=rsem.at[i],
                                                    device_id=(xi, yi, 1 - ci), device_id_type=MESH))
            cps[-1].start()
        for cp in cps:
            cp.wait()

    return pl.pallas_call(body, in_specs=[ANY] * n, out_specs=[ANY] * n, out_shape=outs,
                          scratch_shapes=[pltpu.SemaphoreType.DMA((n,)), pltpu.SemaphoreType.DMA((n,))], name="grad_sibling_exchange")(*gviews)


def _chip_sum(g, land, kind, jc, name):
    if kind == "col":
        P, _, RH, C = g.shape
        CS = C // N_CHIPS
        g_spec = pl.BlockSpec((None, None, RH, CS), lambda p, r, jc: (p, jc[1], 0, jc[0] ^ r))
        l_spec = pl.BlockSpec((None, RH, CS), lambda p, r, jc: (p, 0, jc[0] ^ r))
    else:
        P, _, _, RH, CS = g.shape
        g_spec = pl.BlockSpec((None, None, None, RH, CS), lambda p, r, jc: (p, jc[0] ^ r, jc[1], 0, 0))
        l_spec = pl.BlockSpec((None, None, RH, CS), lambda p, r, jc: (p, jc[0] ^ r, 0, 0))

    def body(jc_ref, g_ref, l_ref, own_ref, all_ref):
        s = g_ref[...] + l_ref[...]
        all_ref[...] = s.astype(BF)

        @pl.when(pl.program_id(1) == 0)
        def _():
            own_ref[...] = s

    return pl.pallas_call(
        body, grid_spec=pltpu.PrefetchScalarGridSpec(
            num_scalar_prefetch=1, grid=(P, N_CHIPS), in_specs=[g_spec, l_spec],
            out_specs=[pl.BlockSpec((None, RH, CS), lambda p, r, jc: (p, 0, 0)), pl.BlockSpec((None, None, RH, CS), lambda p, r, jc: (r, p, 0, 0))]),
        out_shape=[SDS((P, RH, CS), F32), SDS((N_CHIPS, P, RH, CS), BF)], compiler_params=_params("parallel", "arbitrary"), name=name)(jc, g, land)


def _chip_exchange(pieces):
    n = len(pieces)

    def body(*refs):
        srcs, lands, (ssem, rsem) = refs[:n], refs[n:2 * n], refs[2 * n:]
        xi, yi, ci, _ = _place()
        cps = []
        for i in range(n):
            for r in (1, 2, 3):
                cps.append(pltpu.make_async_remote_copy(src_ref=srcs[i].at[r], dst_ref=lands[i].at[r], send_sem=ssem.at[3 * i + r - 1],
                                                        recv_sem=rsem.at[3 * i + r - 1], device_id=_chip_peer(xi, yi, ci, r), device_id_type=MESH))
                cps[-1].start()
        for cp in cps:
            cp.wait()

    return pl.pallas_call(body, in_specs=[ANY] * n, out_specs=[ANY] * n, out_shape=[SDS(p.shape, p.dtype) for p in pieces],
                          scratch_shapes=[pltpu.SemaphoreType.DMA((3 * n,)), pltpu.SemaphoreType.DMA((3 * n,))], name="grad_chip_exchange")(*pieces)


def _final_sum(own, land, name):
    P, RH, CS = own.shape

    def body(o_ref, a_ref, b_ref, c_ref, out_ref):
        out_ref[...] = ((o_ref[...] + a_ref[...].astype(F32)) + b_ref[...].astype(F32)) + c_ref[...].astype(F32)

    blk = pl.BlockSpec((None, RH, CS), lambda p: (p, 0, 0))
    return pl.pallas_call(
        body, grid=(P,), in_specs=[blk] + [pl.BlockSpec((None, None, RH, CS), functools.partial(lambda r, p: (r, p, 0, 0), r)) for r in (1, 2, 3)],
        out_specs=blk, out_shape=SDS((P, RH, CS), F32), compiler_params=_params("parallel"), name=name)(own, land, land, land)


def _halves_exchange(halves, L):
    n = len(halves)
    flat = [h for hs in halves for h in hs]

    def body(*refs):
        srcs, outs, (ssem, rsem, lsem) = refs[:n * L], refs[n * L:n * L + n], refs[n * L + n:]
        xi, yi, ci, _ = _place()
        cps, loc = [], []
        for i in range(n):
            for l in range(L):
                t = i * L + l
                loc.append(pltpu.make_async_copy(srcs[t], outs[i].at[l, :, ci], lsem.at[t]))
                loc[-1].start()
                cps.append(pltpu.make_async_remote_copy(src_ref=srcs[t], dst_ref=outs[i].at[l, :, ci], send_sem=ssem.at[t], recv_sem=rsem.at[t],
                                                        device_id=(xi, yi, 1 - ci), device_id_type=MESH))
                cps[-1].start()
        for i in range(n):
            for l in range(L):
                t = i * L + l
                land = outs[i].at[l, :, 1 - ci]
                pltpu.make_async_remote_copy(src_ref=land, dst_ref=land, send_sem=ssem.at[t], recv_sem=rsem.at[t],
                                             device_id=(xi, yi, 1 - ci), device_id_type=MESH).wait_recv()
        for cp in cps:
            cp.wait_send()
        for cp in loc:
            cp.wait()

    outs = [SDS((L, hs[0].shape[0], 2) + hs[0].shape[1:], F32) for hs in halves]
    return pl.pallas_call(body, in_specs=[ANY] * (n * L), out_specs=[ANY] * n, out_shape=outs,
                          scratch_shapes=[pltpu.SemaphoreType.DMA((n * L,))] * 3, name="grad_halves_exchange")(*flat)


def _reduce_small(part):
    NR, Wd = part.shape
    ND = 2 * N_CHIPS

    def body(p_ref, o_ref, land, ssem, rsem):
        xi, yi, ci, j = _place()
        me = 2 * j + ci
        land[me] = p_ref[...]
        cps = []
        for rr in range(1, ND):
            dev = (xi ^ (rr >> 2), yi ^ ((rr >> 1) & 1), ci ^ (rr & 1))
            cps.append(pltpu.make_async_remote_copy(src_ref=p_ref, dst_ref=land.at[me], send_sem=ssem.at[rr - 1], recv_sem=rsem.at[rr - 1],
                                                    device_id=dev, device_id_type=MESH))
            cps[-1].start()
        for rr in range(1, ND):
            got = land.at[me ^ rr]
            pltpu.make_async_remote_copy(src_ref=got, dst_ref=got, send_sem=ssem.at[rr - 1], recv_sem=rsem.at[rr - 1],
                                         device_id=(xi, yi, ci), device_id_type=MESH).wait_recv()
        acc = land[0]
        for d in range(1, ND):
            acc = acc + land[d]
        o_ref[...] = acc
        for cp in cps:
            cp.wait_send()

    vm = pl.BlockSpec(memory_space=pltpu.VMEM)
    return pl.pallas_call(body, in_specs=[vm], out_specs=vm, out_shape=SDS((NR, Wd), F32),
                          scratch_shapes=[pltpu.VMEM((ND, NR, Wd), F32), pltpu.SemaphoreType.DMA((ND - 1,)), pltpu.SemaphoreType.DMA((ND - 1,))],
                          name="small_grad_allreduce")(part)


def _adamw(w, g, m, v, name):
    shape = w.shape
    C = shape[-1]
    R = w.size // C
    tb = _tile(R, max(8, (1 << 18) // C), 8)

    def body(w_ref, g_ref, m_ref, v_ref, d_ref, mo_ref, vo_ref):
        g = g_ref[...]
        m = ADAM_B1 * m_ref[...] + (1.0 - ADAM_B1) * g
        v = ADAM_B2 * v_ref[...] + (1.0 - ADAM_B2) * jnp.square(g)
        m_hat = m / (1.0 - ADAM_B1 ** ADAM_STEP)
        v_hat = v / (1.0 - ADAM_B2 ** ADAM_STEP)
        d_ref[...] = -ADAM_LR * (m_hat / (jnp.sqrt(v_hat) + ADAM_EPS) + ADAM_WD * w_ref[...])
        mo_ref[...] = m
        vo_ref[...] = v

    blk = pl.BlockSpec((tb, C), lambda i: (i, 0))
    outs = pl.pallas_call(body, grid=(R // tb,), in_specs=[blk] * 4, out_specs=[blk] * 3, out_shape=[SDS((R, C), F32)] * 3,
                          compiler_params=_params("parallel"), name=name)(*[t.reshape(R, C) for t in (w, g, m, v)])
    return [t.reshape(shape) for t in outs]


WEIGHTS = ("mix_norm_g", "w_in", "conv_dw_w", "conv_dw_b", "conv_ln_g", "conv_ln_b", "w_conv_out", "w_pool_grp", "pool_scale", "w_out",
           "xattn_norm_g", "mem_norm_g", "w_q", "w_kv", "w_o", "ffn_norm_g", "w_up", "ffn_dw_w", "w_down", "final_norm_g")
VECTORS = ("mix_norm_g", "conv_dw_b", "conv_ln_g", "conv_ln_b", "pool_scale", "xattn_norm_g", "mem_norm_g", "ffn_norm_g", "final_norm_g")


def _shard_view(t, kind):
    L, P, R, C = t.shape
    return t.reshape(L, P, 2, R // 2, C)


def _rows(t, width):
    return t.reshape(-1, width)


def kernel(x, mem, mix_norm_g, w_in, conv_dw_w, conv_dw_b, conv_ln_g, conv_ln_b, w_conv_out, w_pool_grp, pool_scale, w_out, xattn_norm_g, mem_norm_g, w_q, w_kv, w_o, ffn_norm_g, w_up, ffn_dw_w, w_down, final_norm_g, loss_target, m_mix_norm_g, m_w_in, m_conv_dw_w, m_conv_dw_b, m_conv_ln_g, m_conv_ln_b, m_w_conv_out, m_w_pool_grp, m_pool_scale, m_w_out, m_xattn_norm_g, m_mem_norm_g, m_w_q, m_w_kv, m_w_o, m_ffn_norm_g, m_w_up, m_ffn_dw_w, m_w_down, m_final_norm_g, v_mix_norm_g, v_w_in, v_conv_dw_w, v_conv_dw_b, v_conv_ln_g, v_conv_ln_b, v_w_conv_out, v_w_pool_grp, v_pool_scale, v_w_out, v_xattn_norm_g, v_mem_norm_g, v_w_q, v_w_kv, v_w_o, v_ffn_norm_g, v_w_up, v_ffn_dw_w, v_w_down, v_final_norm_g):
    w = dict(mix_norm_g=mix_norm_g, w_in=w_in, conv_dw_w=conv_dw_w, conv_dw_b=conv_dw_b, conv_ln_g=conv_ln_g, conv_ln_b=conv_ln_b,
             w_conv_out=w_conv_out, w_pool_grp=w_pool_grp, pool_scale=pool_scale, w_out=w_out, xattn_norm_g=xattn_norm_g,
             mem_norm_g=mem_norm_g, w_q=w_q, w_kv=w_kv, w_o=w_o, ffn_norm_g=ffn_norm_g, w_up=w_up, ffn_dw_w=ffn_dw_w, w_down=w_down,
             final_norm_g=final_norm_g)
    m = dict(zip(WEIGHTS, (m_mix_norm_g, m_w_in, m_conv_dw_w, m_conv_dw_b, m_conv_ln_g, m_conv_ln_b, m_w_conv_out, m_w_pool_grp, m_pool_scale,
                           m_w_out, m_xattn_norm_g, m_mem_norm_g, m_w_q, m_w_kv, m_w_o, m_ffn_norm_g, m_w_up, m_ffn_dw_w, m_w_down, m_final_norm_g)))
    v = dict(zip(WEIGHTS, (v_mix_norm_g, v_w_in, v_conv_dw_w, v_conv_dw_b, v_conv_ln_g, v_conv_ln_b, v_w_conv_out, v_w_pool_grp, v_pool_scale,
                           v_w_out, v_xattn_norm_g, v_mem_norm_g, v_w_q, v_w_kv, v_w_o, v_ffn_norm_g, v_w_up, v_ffn_dw_w, v_w_down, v_final_norm_g)))
    xi, yi, ci, j = _place()
    jc = jnp.stack([j, ci]).astype(jnp.int32)
    L = w_in.shape[0]
    G = len(POOL_WINDOWS)
    kinds = dict(BIG)

    def to_mat(name, t):
        if name == "w_pool":
            return jnp.swapaxes(t, 2, 3)
        return t[:, None]

    def from_mat(name, t):
        if name == "w_pool":
            return jnp.swapaxes(t, 2, 3)
        return t[:, 0]

    src = {name: w["w_pool_grp" if name == "w_pool" else name] for name, _ in BIG}

    KC, cs_c = conv_dw_w.shape[1], conv_dw_w.shape[2]
    KF, cs_f = ffn_dw_w.shape[1], ffn_dw_w.shape[2]
    taps = jnp.concatenate([conv_dw_w.reshape(L * KC, cs_c), ffn_dw_w.reshape(L * KF * (cs_f // cs_c), cs_c)], axis=0)
    n_taps = taps.shape[0]
    taps = jnp.pad(taps, ((0, (-n_taps) % 16), (0, 0)))
    shards = [_shard_view(to_mat(name, src[name]).astype(BF), kind) for name, kind in BIG] + [_shard_view(taps[None, None], "row")]
    fulls = _gather_weights(shards, [kind for _, kind in BIG] + ["row"])
    W = {}
    for (name, kind), f in zip(BIG, fulls):
        W[name] = f.reshape(L * G if name == "w_pool" else L, -1, f.shape[-1])
    taps_all = fulls[-1].reshape(N_CHIPS, -1, cs_c)[:, :n_taps]
    V = {name: w[name] for name in VECTORS}
    V["conv_dw_w"] = taps_all[:, :L * KC].reshape(N_CHIPS, L, KC, cs_c).transpose(1, 2, 0, 3).reshape(L, KC, N_CHIPS * cs_c)
    V["ffn_dw_w"] = taps_all[:, L * KC:].reshape(N_CHIPS, L, KF, cs_f).transpose(1, 2, 0, 3).reshape(L, KF, N_CHIPS * cs_f)

    loss, grad_x, big, small = _local_step(x, mem, loss_target, W, V)
    loss = lax.psum(loss[0, 0], ("x", "y", "c"))

    gviews, gkinds, gnames = [], [], []
    for name, kind in BIG:
        for l in range(L):
            g = big[l][name]
            if g.ndim == 2:
                g = g[None]
            P, R, C = g.shape
            gviews.append(g.reshape(P, 2, R // 2, C) if kind == "col" else g.reshape(P, N_CHIPS, 2, R // (2 * N_CHIPS), C))
            gkinds.append(kind)
            gnames.append(f"{name}_{l}")
    lands = _sibling_exchange(gviews, gkinds)
    owns, pieces = [], []
    for g, land, kind, nm in zip(gviews, lands, gkinds, gnames):
        own, allp = _chip_sum(g, land, kind, jc, "chip_sum_" + nm)
        owns.append(own)
        pieces.append(allp)
    got = _chip_exchange(pieces)
    halves = [_final_sum(own, land, "final_sum_" + nm) for own, land, nm in zip(owns, got, gnames)]
    gshards = _halves_exchange([halves[i * L:(i + 1) * L] for i in range(len(BIG))], L)
    grads = {}
    for (name, kind), gs in zip(BIG, gshards):
        Lg, P, _, RH, CS = gs.shape
        grads["w_pool_grp" if name == "w_pool" else name] = from_mat(name, gs.reshape(Lg, P, 2 * RH, CS))

    small_w = conv_dw_b.shape[1]
    order = VECTORS + ("conv_dw_w", "ffn_dw_w")
    parts = [_rows(small[name], small_w) for name in order]
    counts = [p.shape[0] for p in parts]
    packed = jnp.concatenate(parts, axis=0)
    packed = jnp.pad(packed, ((0, (-packed.shape[0]) % 8), (0, 0)))
    summed = _reduce_small(packed)
    off = 0
    for name, cnt in zip(order, counts):
        t = summed[off:off + cnt]
        off += cnt
        if name in VECTORS:
            grads[name] = t.reshape(w[name].shape)
        else:
            full = t.reshape(small[name].shape)
            cs = w[name].shape[2]
            grads[name] = lax.dynamic_slice_in_dim(full, j * cs, cs, axis=2)

    delta, new_m, new_v = {}, {}, {}
    for name, _ in BIG:
        key = "w_pool_grp" if name == "w_pool" else name
        outs = _adamw(*[to_mat(name, t) for t in (w[key], grads[key], m[key], v[key])], "adamw_" + name)
        delta[key], new_m[key], new_v[key] = [from_mat(name, t) for t in outs]
    vec = [jnp.concatenate([_rows(d[name], small_w) for name in VECTORS], axis=0) for d in (w, grads, m, v)]
    outs = _adamw(*vec, "adamw_vectors")
    off = 0
    for name in VECTORS:
        cnt = w[name].size // small_w
        for d, t in zip((delta, new_m, new_v), outs):
            d[name] = t[off:off + cnt].reshape(w[name].shape)
        off += cnt
    for name in ("conv_dw_w", "ffn_dw_w"):
        delta[name], new_m[name], new_v[name] = _adamw(w[name], grads[name], m[name], v[name], "adamw_" + name)

    return (loss, grad_x, *[grads[k] for k in WEIGHTS], *[delta[k] for k in WEIGHTS], *[new_m[k] for k in WEIGHTS], *[new_v[k] for k in WEIGHTS])
```

```python
import functools

import jax
import jax.numpy as jnp
from jax import lax
from jax.experimental import pallas as pl
from jax.experimental.pallas import tpu as pltpu

F32 = jnp.float32
BF = jnp.bfloat16
SDS = jax.ShapeDtypeStruct
MESH = pl.DeviceIdType.MESH

EPS = 1e-6
XA_HEADS = 4
POOL_WINDOWS = (2, 4, 8, 16)
N_CHIPS = 4
ADAM_LR, ADAM_B1, ADAM_B2, ADAM_EPS, ADAM_WD, ADAM_STEP = 0.001, 0.9, 0.999, 1e-08, 0.01, 10

LANES = 128
ROW_BLOCK = 512
VMEM_LIMIT = 56 * 1024 * 1024


def _params(*sem):
    return pltpu.CompilerParams(dimension_semantics=sem if sem else None, vmem_limit_bytes=VMEM_LIMIT)


def _tile(n, cap, mult=LANES):
    if n <= cap:
        return n
    for t in range(cap - cap % mult, 0, -mult):
        if n % t == 0:
            return t
    return n


_DN = {"nn": (((1,), (0,)), ((), ())), "nt": (((1,), (1,)), ((), ())), "tn": (((0,), (0,)), ((), ()))}


def _mm(a, b, dims, out_dtype, name, res=None, bl=None, tm=1024, tn=512, tk=1024):
    bs = b.shape[1:] if bl is not None else b.shape
    if dims == "nn":
        (M, K), (K2, N) = a.shape, bs
    elif dims == "nt":
        (M, K), (N, K2) = a.shape, bs
    else:
        (K, M), (K2, N) = a.shape, bs
    assert K == K2, (name, a.shape, b.shape)
    tm, tn, tk = _tile(M, tm), _tile(N, tn), _tile(K, tk)
    nk = K // tk
    lead = (None,) if bl is not None else ()
    pre = (lambda *ix: (bl,) + ix) if bl is not None else (lambda *ix: ix)
    if dims == "tn":
        a_spec = pl.BlockSpec((tk, tm), lambda i, j, k: (k, i))
    else:
        a_spec = pl.BlockSpec((tm, tk), lambda i, j, k: (i, k))
    if dims == "nt":
        b_spec = pl.BlockSpec(lead + (tn, tk), lambda i, j, k: pre(j, k))
    else:
        b_spec = pl.BlockSpec(lead + (tk, tn), lambda i, j, k: pre(k, j))
    o_spec = pl.BlockSpec((tm, tn), lambda i, j, k: (i, j))
    in_specs, args = [a_spec, b_spec], [a, b]
    if res is not None:
        in_specs.append(o_spec)
        args.append(res)

    def body(*refs):
        a_ref, b_ref = refs[0], refs[1]
        r_ref = refs[2] if res is not None else None
        o_ref = refs[3] if res is not None else refs[2]
        p = lax.dot_general(a_ref[...].astype(BF), b_ref[...].astype(BF), _DN[dims], preferred_element_type=F32)

        def finish(t):
            if r_ref is not None:
                t = t + r_ref[...]
            o_ref[...] = t.astype(out_dtype)

        if nk == 1:
            finish(p)
        else:
            acc = refs[-1]
            k = pl.program_id(2)

            @pl.when(k == 0)
            def _():
                acc[...] = p

            @pl.when(k > 0)
            def _():
                acc[...] += p

            @pl.when(k == nk - 1)
            def _():
                finish(acc[...])

    return pl.pallas_call(
        body, grid=(M // tm, N // tn, nk), in_specs=in_specs, out_specs=o_spec, out_shape=SDS((M, N), out_dtype),
        scratch_shapes=[pltpu.VMEM((tm, tn), F32)] if nk > 1 else [],
        compiler_params=_params("parallel", "parallel", "arbitrary"), name=name)(*args)


def _rms(x, g):
    return x * lax.rsqrt(jnp.mean(x * x, axis=-1, keepdims=True) + EPS) * g


def _ln_silu(x, g, b):
    mu = jnp.mean(x, axis=-1, keepdims=True)
    xc = x - mu
    var = jnp.mean(xc * xc, axis=-1, keepdims=True)
    return jax.nn.silu(xc * lax.rsqrt(var + EPS) * g + b)


def _merge(gc, gp, yc, yp, ps):
    return jax.nn.sigmoid(gc) * yc + jax.nn.sigmoid(gp) * (yp * ps)


def _gated(gate, val):
    return jax.nn.gelu(gate) * val


def _rms_fwd(x, g, name):
    T, D = x.shape
    tb = _tile(T, ROW_BLOCK, 8)

    def body(x_ref, g_ref, o_ref):
        o_ref[...] = _rms(x_ref[...], g_ref[...]).astype(BF)

    row = pl.BlockSpec((tb, D), lambda i: (i, 0))
    return pl.pallas_call(body, grid=(T // tb,), in_specs=[row, pl.BlockSpec((1, D), lambda i: (0, 0))], out_specs=row,
                          out_shape=SDS((T, D), BF), compiler_params=_params("parallel"), name=name)(x, g.reshape(1, D))


def _rms_bwd(x, g, dh, dres, name):
    T, D = x.shape
    tb = _tile(T, ROW_BLOCK, 8)

    def body(*refs):
        if dres is not None:
            x_ref, g_ref, dh_ref, dres_ref, dx_ref, dxb_ref, dg_ref = refs
        else:
            x_ref, g_ref, dh_ref, dx_ref, dxb_ref, dg_ref = refs
        _, vjp = jax.vjp(_rms, x_ref[...], g_ref[...])
        dx, dg = vjp(dh_ref[...].astype(F32))
        if dres is not None:
            dx = dx + dres_ref[...]
        dx_ref[...] = dx
        dxb_ref[...] = dx.astype(BF)

        @pl.when(pl.program_id(0) == 0)
        def _():
            dg_ref[...] = jnp.zeros_like(dg_ref)

        dg_ref[...] += dg

    row = pl.BlockSpec((tb, D), lambda i: (i, 0))
    vec = pl.BlockSpec((1, D), lambda i: (0, 0))
    ins = [x, g.reshape(1, D), dh] + ([dres] if dres is not None else [])
    return pl.pallas_call(
        body, grid=(T // tb,), in_specs=[row, vec, row] + ([row] if dres is not None else []), out_specs=[row, row, vec],
        out_shape=[SDS((T, D), F32), SDS((T, D), BF), SDS((1, D), F32)], compiler_params=_params("arbitrary"), name=name)(*ins)


def _loss_bwd(x, g, target, name):
    T, D = x.shape
    tb = _tile(T, ROW_BLOCK, 8)
    nb = T // tb

    def body(x_ref, g_ref, t_ref, loss_ref, dx_ref, dg_ref, acc):
        i = pl.program_id(0)
        y, vjp = jax.vjp(_rms, x_ref[...], g_ref[...])
        err = y - t_ref[...]
        dx, dg = vjp(err * (1.0 / D))
        dx_ref[...] = dx

        @pl.when(i == 0)
        def _():
            dg_ref[...] = jnp.zeros_like(dg_ref)
            acc[...] = jnp.zeros_like(acc)

        dg_ref[...] += dg
        acc[...] += jnp.sum(err * err, axis=0, keepdims=True)

        @pl.when(i == nb - 1)
        def _():
            loss_ref[...] = jnp.full(loss_ref.shape, (0.5 / D) * jnp.sum(acc[...]), F32)

    row = pl.BlockSpec((tb, D), lambda i: (i, 0))
    vec = pl.BlockSpec((1, D), lambda i: (0, 0))
    return pl.pallas_call(
        body, grid=(nb,), in_specs=[row, vec, row], out_specs=[pl.BlockSpec((1, LANES), lambda i: (0, 0)), row, vec],
        out_shape=[SDS((1, LANES), F32), SDS((T, D), F32), SDS((1, D), F32)], scratch_shapes=[pltpu.VMEM((1, D), F32)],
        compiler_params=_params("arbitrary"), name=name)(x, g.reshape(1, D), target)


def _ln_silu_fwd(cv, g, b, name):
    T, C = cv.shape
    tb = _tile(T, ROW_BLOCK, 8)

    def body(x_ref, g_ref, b_ref, o_ref):
        o_ref[...] = _ln_silu(x_ref[...], g_ref[...], b_ref[...]).astype(BF)

    row = pl.BlockSpec((tb, C), lambda i: (i, 0))
    vec = pl.BlockSpec((1, C), lambda i: (0, 0))
    return pl.pallas_call(body, grid=(T // tb,), in_specs=[row, vec, vec], out_specs=row, out_shape=SDS((T, C), BF),
                          compiler_params=_params("parallel"), name=name)(cv, g.reshape(1, C), b.reshape(1, C))


def _ln_silu_bwd(cv, g, b, dy, name):
    T, C = cv.shape
    tb = _tile(T, ROW_BLOCK, 8)

    def body(x_ref, g_ref, b_ref, dy_ref, dx_ref, dg_ref, db_ref):
        _, vjp = jax.vjp(_ln_silu, x_ref[...], g_ref[...], b_ref[...])
        dx, dg, db = vjp(dy_ref[...].astype(F32))
        dx_ref[...] = dx

        @pl.when(pl.program_id(0) == 0)
        def _():
            dg_ref[...] = jnp.zeros_like(dg_ref)
            db_ref[...] = jnp.zeros_like(db_ref)

        dg_ref[...] += dg
        db_ref[...] += db

    row = pl.BlockSpec((tb, C), lambda i: (i, 0))
    vec = pl.BlockSpec((1, C), lambda i: (0, 0))
    return pl.pallas_call(
        body, grid=(T // tb,), in_specs=[row, vec, vec, row], out_specs=[row, vec, vec],
        out_shape=[SDS((T, C), F32), SDS((1, C), F32), SDS((1, C), F32)], compiler_params=_params("arbitrary"),
        name=name)(cv, g.reshape(1, C), b.reshape(1, C), dy)


def _merge_fwd(proj, yc, yp, ps, C, name):
    T, D = yc.shape
    tb = _tile(T, ROW_BLOCK, 8)
    nj = D // C

    def body(gc_ref, gp_ref, yc_ref, yp_ref, ps_ref, o_ref):
        o_ref[...] = _merge(gc_ref[...], gp_ref[...], yc_ref[...], yp_ref[...], ps_ref[...]).astype(BF)

    blk = pl.BlockSpec((tb, C), lambda i, j: (i, j))
    return pl.pallas_call(
        body, grid=(T // tb, nj),
        in_specs=[pl.BlockSpec((tb, C), lambda i, j: (i, 3 + j)), pl.BlockSpec((tb, C), lambda i, j: (i, 3 + nj + j)), blk, blk,
                  pl.BlockSpec((1, C), lambda i, j: (0, j))],
        out_specs=blk, out_shape=SDS((T, D), BF), compiler_params=_params("parallel", "parallel"), name=name)(proj, proj, yc, yp, ps.reshape(1, D))


def _merge_bwd(proj, yc, yp, ps, dm, C, name):
    T, D = yc.shape
    tb = _tile(T, ROW_BLOCK, 8)
    nj = D // C

    def body(gc_ref, gp_ref, yc_ref, yp_ref, ps_ref, dm_ref, dgc_ref, dgp_ref, dyc_ref, dyp_ref, dps_ref):
        _, vjp = jax.vjp(_merge, gc_ref[...], gp_ref[...], yc_ref[...], yp_ref[...], ps_ref[...])
        dgc, dgp, dyc, dyp, dps = vjp(dm_ref[...].astype(F32))
        dgc_ref[...] = dgc.astype(BF)
        dgp_ref[...] = dgp.astype(BF)
        dyc_ref[...] = dyc.astype(BF)
        dyp_ref[...] = dyp.astype(BF)

        @pl.when(pl.program_id(1) == 0)
        def _():
            dps_ref[...] = jnp.zeros_like(dps_ref)

        dps_ref[...] += dps

    blk = pl.BlockSpec((tb, C), lambda j, i: (i, j))
    vec = pl.BlockSpec((1, C), lambda j, i: (0, j))
    return pl.pallas_call(
        body, grid=(nj, T // tb),
        in_specs=[pl.BlockSpec((tb, C), lambda j, i: (i, 3 + j)), pl.BlockSpec((tb, C), lambda j, i: (i, 3 + nj + j)), blk, blk, vec, blk],
        out_specs=[blk, blk, blk, blk, vec], out_shape=[SDS((T, D), BF)] * 4 + [SDS((1, D), F32)],
        compiler_params=_params("parallel", "arbitrary"), name=name)(proj, proj, yc, yp, ps.reshape(1, D), dm)


def _shd(v, s, rows):
    if s == 0:
        return v
    return jnp.where(rows >= s, pltpu.roll(v, s, 0), 0.0)


def _shu(v, s, rows):
    if s == 0:
        return v
    n = v.shape[0]
    return jnp.where(rows < n - s, pltpu.roll(v, n - s, 0), 0.0)


def _glu_conv_fwd(proj, w, b, Bn, S, C, name):
    K = w.shape[0]
    sl = min(LANES, C)
    ns = C // sl

    def body(a_ref, gl_ref, w_ref, b_ref, o_ref):
        y0 = a_ref[...] * jax.nn.sigmoid(gl_ref[...])
        rows = lax.broadcasted_iota(jnp.int32, y0.shape, 0)
        acc = jnp.zeros_like(y0) + b_ref[...]
        for k in range(K):
            acc = acc + w_ref[k:k + 1, :] * _shd(y0, K - 1 - k, rows)
        o_ref[...] = acc

    return pl.pallas_call(
        body, grid=(Bn, ns),
        in_specs=[pl.BlockSpec((S, sl), lambda bi, j: (bi, j)), pl.BlockSpec((S, sl), lambda bi, j: (bi, ns + j)),
                  pl.BlockSpec((K, sl), lambda bi, j: (0, j)), pl.BlockSpec((1, sl), lambda bi, j: (0, j))],
        out_specs=pl.BlockSpec((S, sl), lambda bi, j: (bi, j)), out_shape=SDS((Bn * S, C), F32),
        compiler_params=_params("parallel", "parallel"), name=name)(proj, proj, w, b.reshape(1, C))


def _glu_conv_bwd(proj, w, dcv, Bn, S, C, name):
    K = w.shape[0]
    sl = min(LANES, C)
    ns = C // sl

    def body(a_ref, gl_ref, w_ref, d_ref, da_ref, dgl_ref, dw_ref, db_ref):
        a = a_ref[...]
        sg = jax.nn.sigmoid(gl_ref[...])
        y0 = a * sg
        d = d_ref[...]
        rows = lax.broadcasted_iota(jnp.int32, y0.shape, 0)

        @pl.when(pl.program_id(1) == 0)
        def _():
            dw_ref[...] = jnp.zeros_like(dw_ref)
            db_ref[...] = jnp.zeros_like(db_ref)

        dy0 = jnp.zeros_like(y0)
        for k in range(K):
            s = K - 1 - k
            dw_ref[k:k + 1, :] += jnp.sum(d * _shd(y0, s, rows), axis=0, keepdims=True)
            dy0 = dy0 + w_ref[k:k + 1, :] * _shu(d, s, rows)
        db_ref[...] += jnp.sum(d, axis=0, keepdims=True)
        da_ref[...] = (dy0 * sg).astype(BF)
        dgl_ref[...] = (dy0 * a * sg * (1.0 - sg)).astype(BF)

    blk = pl.BlockSpec((S, sl), lambda j, bi: (bi, j))
    return pl.pallas_call(
        body, grid=(ns, Bn),
        in_specs=[blk, pl.BlockSpec((S, sl), lambda j, bi: (bi, ns + j)), pl.BlockSpec((K, sl), lambda j, bi: (0, j)), blk],
        out_specs=[blk, blk, pl.BlockSpec((K, sl), lambda j, bi: (0, j)), pl.BlockSpec((1, sl), lambda j, bi: (0, j))],
        out_shape=[SDS((Bn * S, C), BF), SDS((Bn * S, C), BF), SDS((K, C), F32), SDS((1, C), F32)],
        compiler_params=_params("parallel", "arbitrary"), name=name)(proj, proj, w, dcv)


def _pool_z(u, g, rows):
    s2 = u + _shd(u, 1, rows)
    s4 = s2 + _shd(s2, 2, rows)
    s8 = s4 + _shd(s4, 4, rows)
    s16 = s8 + _shd(s8, 8, rows)
    sw = jnp.where(g == 0, s2, jnp.where(g == 1, s4, jnp.where(g == 2, s8, s16)))
    cnt = jnp.minimum(rows + 1, POOL_WINDOWS[0] << g).astype(F32)
    return sw / cnt - u, cnt


def _pool_fwd(proj, wpt, l, Bn, S, C, D, name):
    G = len(POOL_WINDOWS)
    gd, go = C // G, D // G

    def body(u_ref, w_ref, o_ref):
        g = pl.program_id(1)
        u = u_ref[...]
        rows = lax.broadcasted_iota(jnp.int32, u.shape, 0)
        zp, _ = _pool_z(u, g, rows)
        o_ref[...] = lax.dot_general(zp.astype(BF), w_ref[...], _DN["nt"], preferred_element_type=F32)

    return pl.pallas_call(
        body, grid=(Bn, G),
        in_specs=[pl.BlockSpec((S, gd), lambda bi, g: (bi, 2 * G + g)), pl.BlockSpec((None, go, gd), lambda bi, g: (l * G + g, 0, 0))],
        out_specs=pl.BlockSpec((S, go), lambda bi, g: (bi, g)), out_shape=SDS((Bn * S, D), F32),
        compiler_params=_params("parallel", "parallel"), name=name)(proj, wpt)


def _pool_bwd(proj, wpt, dyp, l, Bn, S, C, D, name):
    G = len(POOL_WINDOWS)
    gd, go = C // G, D // G

    def body(u_ref, w_ref, d_ref, du_ref, dw_ref):
        g = pl.program_id(0)
        u = u_ref[...]
        rows = lax.broadcasted_iota(jnp.int32, u.shape, 0)
        zp, cnt = _pool_z(u, g, rows)
        d = d_ref[...]
        dzp = lax.dot_general(d, w_ref[...], _DN["nn"], preferred_element_type=F32)

        @pl.when(pl.program_id(1) == 0)
        def _():
            dw_ref[...] = jnp.zeros_like(dw_ref)

        dw_ref[...] += lax.dot_general(d, zp.astype(BF), _DN["tn"], preferred_element_type=F32)
        dsw = dzp / cnt
        zero = jnp.zeros_like(dsw)
        d16 = jnp.where(g == 3, dsw, zero)
        d8 = jnp.where(g == 2, dsw, zero) + d16 + _shu(d16, 8, rows)
        d4 = jnp.where(g == 1, dsw, zero) + d8 + _shu(d8, 4, rows)
        d2 = jnp.where(g == 0, dsw, zero) + d4 + _shu(d4, 2, rows)
        d1 = d2 + _shu(d2, 1, rows)
        du_ref[...] = (d1 - dzp).astype(BF)

    return pl.pallas_call(
        body, grid=(G, Bn),
        in_specs=[pl.BlockSpec((S, gd), lambda g, bi: (bi, 2 * G + g)), pl.BlockSpec((None, go, gd), lambda g, bi: (l * G + g, 0, 0)),
                  pl.BlockSpec((S, go), lambda g, bi: (bi, g))],
        out_specs=[pl.BlockSpec((S, gd), lambda g, bi: (bi, g)), pl.BlockSpec((None, go, gd), lambda g, bi: (g, 0, 0))],
        out_shape=[SDS((Bn * S, C), BF), SDS((G, go, gd), F32)],
        compiler_params=_params("parallel", "arbitrary"), name=name)(proj, wpt, dyp)


def _ffn_conv(u, w_ref, rows):
    K = w_ref.shape[0]
    acc = w_ref[K - 1:K, :] * u
    for k in range(K - 1):
        acc = acc + w_ref[k:k + 1, :] * _shd(u, K - 1 - k, rows)
    return acc


def _ffn_cb(F):
    return _tile(F, 256)


def _ffn_act_fwd(up0, w, Bn, S, F, name):
    cb = _ffn_cb(F)
    nj = F // cb

    def body(g_ref, v_ref, wg_ref, wv_ref, o_ref):
        rows = lax.broadcasted_iota(jnp.int32, g_ref.shape, 0)
        o_ref[...] = _gated(_ffn_conv(g_ref[...], wg_ref, rows), _ffn_conv(v_ref[...], wv_ref, rows)).astype(BF)

    K = w.shape[0]
    return pl.pallas_call(
        body, grid=(Bn, nj),
        in_specs=[pl.BlockSpec((S, cb), lambda bi, j: (bi, j)), pl.BlockSpec((S, cb), lambda bi, j: (bi, nj + j)),
                  pl.BlockSpec((K, cb), lambda bi, j: (0, j)), pl.BlockSpec((K, cb), lambda bi, j: (0, nj + j))],
        out_specs=pl.BlockSpec((S, cb), lambda bi, j: (bi, j)), out_shape=SDS((Bn * S, F), BF),
        compiler_params=_params("parallel", "parallel"), name=name)(up0, up0, w, w)


def _ffn_act_bwd(up0, w, dg, Bn, S, F, name):
    cb = _ffn_cb(F)
    nj = F // cb
    K = w.shape[0]

    def body(g_ref, v_ref, wg_ref, wv_ref, d_ref, dgo_ref, dvo_ref, dwg_ref, dwv_ref):
        rows = lax.broadcasted_iota(jnp.int32, g_ref.shape, 0)
        g0, v0 = g_ref[...], v_ref[...]
        _, vjp = jax.vjp(_gated, _ffn_conv(g0, wg_ref, rows), _ffn_conv(v0, wv_ref, rows))
        dgc, dvc = vjp(d_ref[...].astype(F32))

        @pl.when(pl.program_id(1) == 0)
        def _():
            dwg_ref[...] = jnp.zeros_like(dwg_ref)
            dwv_ref[...] = jnp.zeros_like(dwv_ref)

        for u0, dc, w_ref, dw_ref, do_ref in ((g0, dgc, wg_ref, dwg_ref, dgo_ref), (v0, dvc, wv_ref, dwv_ref, dvo_ref)):
            du = jnp.zeros_like(u0)
            for k in range(K):
                s = K - 1 - k
                dw_ref[k:k + 1, :] += jnp.sum(dc * _shd(u0, s, rows), axis=0, keepdims=True)
                du = du + w_ref[k:k + 1, :] * _shu(dc, s, rows)
            do_ref[...] = du.astype(BF)

    blk = pl.BlockSpec((S, cb), lambda j, bi: (bi, j))
    wblk = pl.BlockSpec((K, cb), lambda j, bi: (0, j))
    return pl.pallas_call(
        body, grid=(nj, Bn),
        in_specs=[blk, pl.BlockSpec((S, cb), lambda j, bi: (bi, nj + j)), wblk, pl.BlockSpec((K, cb), lambda j, bi: (0, nj + j)), blk],
        out_specs=[blk, blk, wblk, wblk],
        out_shape=[SDS((Bn * S, F), BF), SDS((Bn * S, F), BF), SDS((K, F), F32), SDS((K, F), F32)],
        compiler_params=_params("parallel", "arbitrary"), name=name)(up0, up0, w, w, dg)


def _softmax_rows(q, k, scale):
    sc = lax.dot_general(q, k, _DN["nt"], preferred_element_type=F32) * scale
    e = jnp.exp(sc - jnp.max(sc, axis=-1, keepdims=True))
    return e / jnp.sum(e, axis=-1, keepdims=True)


def _attn_ts(S):
    return _tile(S, 1024, 8)


def _attn_fwd(q, kv, Bn, S, Mn, D, name):
    H = XA_HEADS
    dh = D // H
    ts = _attn_ts(S)
    nsb = S // ts
    scale = dh ** -0.5

    def body(q_ref, k_ref, v_ref, o_ref):
        p = _softmax_rows(q_ref[...], k_ref[...], scale)
        o_ref[...] = lax.dot_general(p.astype(BF), v_ref[...], _DN["nn"], preferred_element_type=F32).astype(BF)

    qblk = pl.BlockSpec((ts, dh), lambda bi, h, s: (bi * nsb + s, h))
    return pl.pallas_call(
        body, grid=(Bn, H, nsb),
        in_specs=[qblk, pl.BlockSpec((Mn, dh), lambda bi, h, s: (bi, h)), pl.BlockSpec((Mn, dh), lambda bi, h, s: (bi, H + h))],
        out_specs=qblk, out_shape=SDS((Bn * S, D), BF), compiler_params=_params("parallel", "parallel", "parallel"), name=name)(q, kv, kv)


def _attn_bwd(q, kv, datt, Bn, S, Mn, D, name):
    H = XA_HEADS
    dh = D // H
    ts = _attn_ts(S)
    nsb = S // ts
    scale = dh ** -0.5

    def body(q_ref, k_ref, v_ref, do_ref, dq_ref, dk_ref, dv_ref):
        q, k, v, do = q_ref[...], k_ref[...], v_ref[...], do_ref[...]
        p = _softmax_rows(q, k, scale)
        dp = lax.dot_general(do, v, _DN["nt"], preferred_element_type=F32)
        ds = (p * (dp - jnp.sum(dp * p, axis=-1, keepdims=True)) * scale).astype(BF)
        dq_ref[...] = lax.dot_general(ds, k, _DN["nn"], preferred_element_type=F32).astype(BF)

        @pl.when(pl.program_id(2) == 0)
        def _():
            dk_ref[...] = jnp.zeros_like(dk_ref)
            dv_ref[...] = jnp.zeros_like(dv_ref)

        dk_ref[...] += lax.dot_general(ds, q, _DN["tn"], preferred_element_type=F32)
        dv_ref[...] += lax.dot_general(p.astype(BF), do, _DN["tn"], preferred_element_type=F32)

    qblk = pl.BlockSpec((ts, dh), lambda bi, h, s: (bi * nsb + s, h))
    kblk = pl.BlockSpec((Mn, dh), lambda bi, h, s: (bi, h))
    return pl.pallas_call(
        body, grid=(Bn, H, nsb),
        in_specs=[qblk, kblk, pl.BlockSpec((Mn, dh), lambda bi, h, s: (bi, H + h)), qblk],
        out_specs=[qblk, kblk, kblk], out_shape=[SDS((Bn * S, D), BF), SDS((Bn * Mn, D), F32), SDS((Bn * Mn, D), F32)],
        compiler_params=_params("parallel", "parallel", "arbitrary"), name=name)(q, kv, kv, datt)


def _local_step(x, mem, target, W, V):
    Bn, S, D = x.shape
    Mn = mem.shape[1]
    T = Bn * S
    L = W["w_in"].shape[0]
    C = V["conv_dw_b"].shape[1]
    F = W["w_down"].shape[1]
    x = x.reshape(T, D)
    memf = mem.reshape(Bn * Mn, D)
    target = target.reshape(T, D)

    mem_n = _rms_fwd(memf, V["mem_norm_g"], "mem_norm")
    saved = []
    for l in range(L):
        n = f"l{l}_"
        h = _rms_fwd(x, V["mix_norm_g"][l], n + "mix_norm")
        proj = _mm(h, W["w_in"], "nn", F32, n + "proj", bl=l, tn=C)
        cv = _glu_conv_fwd(proj, V["conv_dw_w"][l], V["conv_dw_b"][l], Bn, S, C, n + "glu_conv")
        yc1 = _ln_silu_fwd(cv, V["conv_ln_g"][l], V["conv_ln_b"][l], n + "ln_silu")
        yc = _mm(yc1, W["w_conv_out"], "nn", F32, n + "conv_out", bl=l)
        yp = _pool_fwd(proj, W["w_pool"], l, Bn, S, C, D, n + "pool")
        merged = _merge_fwd(proj, yc, yp, V["pool_scale"][l], C, n + "merge")
        x1 = _mm(merged, W["w_out"], "nn", F32, n + "out_proj", res=x, bl=l)
        hq = _rms_fwd(x1, V["xattn_norm_g"][l], n + "xattn_norm")
        q = _mm(hq, W["w_q"], "nn", BF, n + "q_proj", bl=l)
        kv = _mm(mem_n, W["w_kv"], "nn", BF, n + "kv_proj", bl=l)
        att = _attn_fwd(q, kv, Bn, S, Mn, D, n + "attn")
        x2 = _mm(att, W["w_o"], "nn", F32, n + "o_proj", res=x1, bl=l)
        hf = _rms_fwd(x2, V["ffn_norm_g"][l], n + "ffn_norm")
        up0 = _mm(hf, W["w_up"], "nn", F32, n + "up_proj", bl=l)
        gact = _ffn_act_fwd(up0, V["ffn_dw_w"][l], Bn, S, F, n + "ffn_act")
        x3 = _mm(gact, W["w_down"], "nn", F32, n + "down_proj", res=x2, bl=l, tk=1408)
        saved.append(dict(x=x, h=h, proj=proj, cv=cv, yc1=yc1, yc=yc, yp=yp, merged=merged, x1=x1, hq=hq, q=q, kv=kv, att=att,
                          x2=x2, hf=hf, up0=up0, gact=gact))
        x = x3

    loss, dx, dgf = _loss_bwd(x, V["final_norm_g"], target, "loss")
    dxb = dx
    big = [None] * L
    small = {k: [None] * L for k in ("mix_norm_g", "conv_dw_w", "conv_dw_b", "conv_ln_g", "conv_ln_b", "pool_scale", "xattn_norm_g",
                                    "ffn_norm_g", "ffn_dw_w")}
    dmem_n = None
    for l in reversed(range(L)):
        n = f"l{l}_b_"
        sv = saved[l]
        gw = {}
        dgact = _mm(dxb, W["w_down"], "nt", BF, n + "d_gact", bl=l)
        gw["w_down"] = _mm(sv["gact"], dxb, "tn", F32, n + "dw_down")
        dg0, dv0, dwg, dwv = _ffn_act_bwd(sv["up0"], V["ffn_dw_w"][l], dgact, Bn, S, F, n + "ffn_act")
        small["ffn_dw_w"][l] = jnp.concatenate([dwg, dwv], axis=1)
        dup0 = jnp.concatenate([dg0, dv0], axis=1)
        dhf = _mm(dup0, W["w_up"], "nt", F32, n + "d_hf", bl=l, tk=1408)
        gw["w_up"] = _mm(sv["hf"], dup0, "tn", F32, n + "dw_up")
        dx2, dx2b, small["ffn_norm_g"][l] = _rms_bwd(sv["x2"], V["ffn_norm_g"][l], dhf, dx, n + "ffn_norm")
        datt = _mm(dx2b, W["w_o"], "nt", BF, n + "d_att", bl=l)
        gw["w_o"] = _mm(sv["att"], dx2b, "tn", F32, n + "dw_o")
        dq, dk, dv = _attn_bwd(sv["q"], sv["kv"], datt, Bn, S, Mn, D, n + "attn")
        dkv = jnp.concatenate([dk, dv], axis=1)
        gw["w_kv"] = _mm(mem_n, dkv, "tn", F32, n + "dw_kv")
        dmem_n = _mm(dkv, W["w_kv"], "nt", F32, n + "d_mem", res=dmem_n, bl=l)
        dhq = _mm(dq, W["w_q"], "nt", F32, n + "d_hq", bl=l)
        gw["w_q"] = _mm(sv["hq"], dq, "tn", F32, n + "dw_q")
        dx1, dx1b, small["xattn_norm_g"][l] = _rms_bwd(sv["x1"], V["xattn_norm_g"][l], dhq, dx2, n + "xattn_norm")
        dmerged = _mm(dx1b, W["w_out"], "nt", BF, n + "d_merged", bl=l)
        gw["w_out"] = _mm(sv["merged"], dx1b, "tn", F32, n + "dw_out")
        dgc, dgp, dyc, dyp, small["pool_scale"][l] = _merge_bwd(sv["proj"], sv["yc"], sv["yp"], V["pool_scale"][l], dmerged, C, n + "merge")
        du, gw["w_pool"] = _pool_bwd(sv["proj"], W["w_pool"], dyp, l, Bn, S, C, D, n + "pool")
        dyc1 = _mm(dyc, W["w_conv_out"], "nt", F32, n + "d_yc1", bl=l)
        gw["w_conv_out"] = _mm(sv["yc1"], dyc, "tn", F32, n + "dw_conv_out")
        dcv, small["conv_ln_g"][l], small["conv_ln_b"][l] = _ln_silu_bwd(sv["cv"], V["conv_ln_g"][l], V["conv_ln_b"][l], dyc1, n + "ln_silu")
        da, dgl, small["conv_dw_w"][l], small["conv_dw_b"][l] = _glu_conv_bwd(sv["proj"], V["conv_dw_w"][l], dcv, Bn, S, C, n + "glu_conv")
        dproj = jnp.concatenate([da, dgl, du, dgc, dgp], axis=1)
        dh = _mm(dproj, W["w_in"], "nt", F32, n + "d_h", bl=l, tk=896)
        gw["w_in"] = _mm(sv["h"], dproj, "tn", F32, n + "dw_in")
        dx, dxb, small["mix_norm_g"][l] = _rms_bwd(sv["x"], V["mix_norm_g"][l], dh, dx1, n + "mix_norm")
        big[l] = gw
    _, _, dgm = _rms_bwd(memf, V["mem_norm_g"], dmem_n, None, "mem_norm_b")
    small = {k: jnp.stack(v) if k in ("conv_dw_w", "ffn_dw_w") else jnp.concatenate(v, axis=0) for k, v in small.items()}
    small["mem_norm_g"] = dgm
    small["final_norm_g"] = dgf
    return loss, dx.reshape(Bn, S, D), big, small


BIG = (("w_in", "col"), ("w_conv_out", "col"), ("w_pool", "row"), ("w_out", "row"), ("w_q", "row"), ("w_kv", "col"),
       ("w_o", "row"), ("w_up", "col"), ("w_down", "row"))
ANY = pl.BlockSpec(memory_space=pl.ANY)


def _place():
    xi, yi, ci = lax.axis_index("x"), lax.axis_index("y"), lax.axis_index("c")
    return xi, yi, ci, 2 * xi + yi


def _chip_peer(xi, yi, ci, r):
    return (xi ^ (r >> 1), yi ^ (r & 1), ci)


def _full_shard(ref, kind, k, cs):
    if kind == "col":
        return ref.at[:, :, :, :, pl.ds(pl.multiple_of(k * cs, cs), cs)]
    return ref.at[:, :, k]


def _gather_weights(shards, kinds):
    n = len(shards)
    outs = []
    for s, kind in zip(shards, kinds):
        L, P, _, RH, CS = s.shape
        outs.append(SDS((L, P, 2, RH, CS * N_CHIPS) if kind == "col" else (L, P, N_CHIPS, 2, RH, CS), s.dtype))
    per = 7

    def body(*refs):
        srcs, fulls, (ssem, rsem) = refs[:n], refs[n:2 * n], refs[2 * n:]
        xi, yi, ci, j = _place()
        sib = (xi, yi, 1 - ci)

        def piece(i, k, c):
            kind, cs = kinds[i], shards[i].shape[-1]
            if kind == "col":
                return fulls[i].at[:, :, c, :, pl.ds(pl.multiple_of(k * cs, cs), cs)]
            return fulls[i].at[:, :, k, c]

        def copy(i, slot, src, dst, dev):
            return pltpu.make_async_remote_copy(src_ref=src, dst_ref=dst, send_sem=ssem.at[per * i + slot], recv_sem=rsem.at[per * i + slot],
                                                device_id=dev, device_id_type=MESH)

        own, first, passed = [], [], []
        for i in range(n):
            for r in (1, 2, 3):
                first.append(copy(i, r - 1, srcs[i].at[:, :, ci], piece(i, j, ci), _chip_peer(xi, yi, ci, r)))
                first[-1].start()
        for i in range(n):
            own.append(copy(i, 6, srcs[i], _full_shard(fulls[i], kinds[i], j, shards[i].shape[-1]), sib))
            own[-1].start()
        for i in range(n):
            for r in (1, 2, 3):
                got = piece(i, j ^ r, ci)
                copy(i, r - 1, got, got, sib).wait_recv()
                passed.append(copy(i, 2 + r, got, got, sib))
                passed[-1].start()
        for i in range(n):
            for r in (1, 2, 3):
                got = piece(i, j ^ r, 1 - ci)
                copy(i, 2 + r, got, got, sib).wait_recv()
        for cp in own:
            cp.wait()
        for cp in first + passed:
            cp.wait_send()

    return pl.pallas_call(
        body, in_specs=[ANY] * n, out_specs=[ANY] * n, out_shape=outs,
        scratch_shapes=[pltpu.SemaphoreType.DMA((per * n,)), pltpu.SemaphoreType.DMA((per * n,))], name="gather_weights")(*shards)


def _sibling_exchange(gviews, kinds):
    n = len(gviews)
    outs = [SDS(g.shape[:1] + g.shape[2:] if kind == "col" else g.shape[:2] + g.shape[3:], g.dtype) for g, kind in zip(gviews, kinds)]

    def body(*refs):
        gs, lands, (ssem, rsem) = refs[:n], refs[n:2 * n], refs[2 * n:]
        xi, yi, ci, _ = _place()
        cps = []
        for i in range(n):
            src = gs[i].at[:, 1 - ci] if kinds[i] == "col" else gs[i].at[:, :, 1 - ci]
            cps.append(pltpu.make_async_remote_copy(src_ref=src, dst_ref=lands[i], send_sem=ssem.at[i], recv_sem=rsem.at[i],
                                                    device_id=(xi, yi, 1 - ci), device_id_type=MESH))
            cps[-1].start()
        for cp in cps:
            cp.wait()

    return pl.pallas_call(body, in_specs=[ANY] * n, out_specs=[ANY] * n, out_shape=outs,
                          scratch_shapes=[pltpu.SemaphoreType.DMA((n,)), pltpu.SemaphoreType.DMA((n,))], name="grad_sibling_exchange")(*gviews)


def _chip_sum(g, land, kind, jc, name):
    if kind == "col":
        P, _, RH, C = g.shape
        CS = C // N_CHIPS
        g_spec = pl.BlockSpec((None, None, RH, CS), lambda p, r, jc: (p, jc[1], 0, jc[0] ^ r))
        l_spec = pl.BlockSpec((None, RH, CS), lambda p, r, jc: (p, 0, jc[0] ^ r))
    else:
        P, _, _, RH, CS = g.shape
        g_spec = pl.BlockSpec((None, None, None, RH, CS), lambda p, r, jc: (p, jc[0] ^ r, jc[1], 0, 0))
        l_spec = pl.BlockSpec((None, None, RH, CS), lambda p, r, jc: (p, jc[0] ^ r, 0, 0))

    def body(jc_ref, g_ref, l_ref, own_ref, all_ref):
        s = g_ref[...] + l_ref[...]
        all_ref[...] = s.astype(BF)

        @pl.when(pl.program_id(1) == 0)
        def _():
            own_ref[...] = s

    return pl.pallas_call(
        body, grid_spec=pltpu.PrefetchScalarGridSpec(
            num_scalar_prefetch=1, grid=(P, N_CHIPS), in_specs=[g_spec, l_spec],
            out_specs=[pl.BlockSpec((None, RH, CS), lambda p, r, jc: (p, 0, 0)), pl.BlockSpec((None, None, RH, CS), lambda p, r, jc: (r, p, 0, 0))]),
        out_shape=[SDS((P, RH, CS), F32), SDS((N_CHIPS, P, RH, CS), BF)], compiler_params=_params("parallel", "arbitrary"), name=name)(jc, g, land)


def _chip_exchange(pieces):
    n = len(pieces)

    def body(*refs):
        srcs, lands, (ssem, rsem) = refs[:n], refs[n:2 * n], refs[2 * n:]
        xi, yi, ci, _ = _place()
        cps = []
        for i in range(n):
            for r in (1, 2, 3):
                cps.append(pltpu.make_async_remote_copy(src_ref=srcs[i].at[r], dst_ref=lands[i].at[r], send_sem=ssem.at[3 * i + r - 1],
                                                        recv_sem=rsem.at[3 * i + r - 1], device_id=_chip_peer(xi, yi, ci, r), device_id_type=MESH))
                cps[-1].start()
        for cp in cps:
            cp.wait()

    return pl.pallas_call(body, in_specs=[ANY] * n, out_specs=[ANY] * n, out_shape=[SDS(p.shape, p.dtype) for p in pieces],
                          scratch_shapes=[pltpu.SemaphoreType.DMA((3 * n,)), pltpu.SemaphoreType.DMA((3 * n,))], name="grad_chip_exchange")(*pieces)


def _final_sum(own, land, jc, shard, l, L, name):
    P, RH, CS = own.shape

    def body(jc_ref, o_ref, a_ref, b_ref, c_ref, *rest):
        rest[-1][...] = ((o_ref[...] + a_ref[...].astype(F32)) + b_ref[...].astype(F32)) + c_ref[...].astype(F32)

    blk = pl.BlockSpec((None, RH, CS), lambda p, jc: (p, 0, 0))
    in_specs = [blk] + [pl.BlockSpec((None, None, RH, CS), functools.partial(lambda r, p, jc: (r, p, 0, 0), r)) for r in (1, 2, 3)]
    args = [jc, own, land, land, land]
    if shard is not None:
        in_specs.append(ANY)
        args.append(shard)
    return pl.pallas_call(
        body, grid_spec=pltpu.PrefetchScalarGridSpec(
            num_scalar_prefetch=1, grid=(P,), in_specs=in_specs,
            out_specs=pl.BlockSpec((None, None, None, RH, CS), lambda p, jc: (l, p, jc[1], 0, 0))),
        out_shape=SDS((L, P, 2, RH, CS), F32), input_output_aliases={5: 0} if shard is not None else {},
        compiler_params=_params("arbitrary"), name=name)(*args)


def _halves_exchange(shards):
    n = len(shards)

    def body(*refs):
        outs, (ssem, rsem) = refs[n:2 * n], refs[2 * n:]
        xi, yi, ci, _ = _place()
        cps = []
        for i in range(n):
            mine = outs[i].at[:, :, ci]
            cps.append(pltpu.make_async_remote_copy(src_ref=mine, dst_ref=mine, send_sem=ssem.at[i], recv_sem=rsem.at[i],
                                                    device_id=(xi, yi, 1 - ci), device_id_type=MESH))
            cps[-1].start()
        for i in range(n):
            land = outs[i].at[:, :, 1 - ci]
            pltpu.make_async_remote_copy(src_ref=land, dst_ref=land, send_sem=ssem.at[i], recv_sem=rsem.at[i],
                                         device_id=(xi, yi, 1 - ci), device_id_type=MESH).wait_recv()
        for cp in cps:
            cp.wait_send()

    return pl.pallas_call(body, in_specs=[ANY] * n, out_specs=[ANY] * n, out_shape=[SDS(s.shape, s.dtype) for s in shards],
                          input_output_aliases={i: i for i in range(n)},
                          scratch_shapes=[pltpu.SemaphoreType.DMA((n,)), pltpu.SemaphoreType.DMA((n,))], name="grad_halves_exchange")(*shards)


def _reduce_small(part):
    NR, Wd = part.shape
    ND = 2 * N_CHIPS

    def body(p_ref, o_ref, land, ssem, rsem):
        xi, yi, ci, j = _place()
        me = 2 * j + ci
        land[me] = p_ref[...]
        cps = []
        for rr in range(1, ND):
            dev = (xi ^ (rr >> 2), yi ^ ((rr >> 1) & 1), ci ^ (rr & 1))
            cps.append(pltpu.make_async_remote_copy(src_ref=p_ref, dst_ref=land.at[me], send_sem=ssem.at[rr - 1], recv_sem=rsem.at[rr - 1],
                                                    device_id=dev, device_id_type=MESH))
            cps[-1].start()
        for rr in range(1, ND):
            got = land.at[me ^ rr]
            pltpu.make_async_remote_copy(src_ref=got, dst_ref=got, send_sem=ssem.at[rr - 1], recv_sem=rsem.at[rr - 1],
                                         device_id=(xi, yi, ci), device_id_type=MESH).wait_recv()
        acc = land[0]
        for d in range(1, ND):
            acc = acc + land[d]
        o_ref[...] = acc
        for cp in cps:
            cp.wait_send()

    vm = pl.BlockSpec(memory_space=pltpu.VMEM)
    return pl.pallas_call(body, in_specs=[vm], out_specs=vm, out_shape=SDS((NR, Wd), F32),
                          scratch_shapes=[pltpu.VMEM((ND, NR, Wd), F32), pltpu.SemaphoreType.DMA((ND - 1,)), pltpu.SemaphoreType.DMA((ND - 1,))],
                          name="small_grad_allreduce")(part)


def _adamw(w, g, m, v, name):
    shape = w.shape
    C = shape[-1]
    R = w.size // C
    tb = _tile(R, max(8, (1 << 18) // C), 8)

    def body(w_ref, g_ref, m_ref, v_ref, d_ref, mo_ref, vo_ref):
        g = g_ref[...]
        m = ADAM_B1 * m_ref[...] + (1.0 - ADAM_B1) * g
        v = ADAM_B2 * v_ref[...] + (1.0 - ADAM_B2) * jnp.square(g)
        m_hat = m / (1.0 - ADAM_B1 ** ADAM_STEP)
        v_hat = v / (1.0 - ADAM_B2 ** ADAM_STEP)
        d_ref[...] = -ADAM_LR * (m_hat / (jnp.sqrt(v_hat) + ADAM_EPS) + ADAM_WD * w_ref[...])
        mo_ref[...] = m
        vo_ref[...] = v

    blk = pl.BlockSpec((tb, C), lambda i: (i, 0))
    outs = pl.pallas_call(body, grid=(R // tb,), in_specs=[blk] * 4, out_specs=[blk] * 3, out_shape=[SDS((R, C), F32)] * 3,
                          compiler_params=_params("parallel"), name=name)(*[t.reshape(R, C) for t in (w, g, m, v)])
    return [t.reshape(shape) for t in outs]


WEIGHTS = ("mix_norm_g", "w_in", "conv_dw_w", "conv_dw_b", "conv_ln_g", "conv_ln_b", "w_conv_out", "w_pool_grp", "pool_scale", "w_out",
           "xattn_norm_g", "mem_norm_g", "w_q", "w_kv", "w_o", "ffn_norm_g", "w_up", "ffn_dw_w", "w_down", "final_norm_g")
VECTORS = ("mix_norm_g", "conv_dw_b", "conv_ln_g", "conv_ln_b", "pool_scale", "xattn_norm_g", "mem_norm_g", "ffn_norm_g", "final_norm_g")


def _shard_view(t, kind):
    L, P, R, C = t.shape
    return t.reshape(L, P, 2, R // 2, C)


def _rows(t, width):
    return t.reshape(-1, width)


def _pack(parts):
    return jnp.concatenate([jnp.pad(p, ((0, (-p.shape[0]) % 8), (0, 0))) for p in parts], axis=0)


def kernel(x, mem, mix_norm_g, w_in, conv_dw_w, conv_dw_b, conv_ln_g, conv_ln_b, w_conv_out, w_pool_grp, pool_scale, w_out, xattn_norm_g, mem_norm_g, w_q, w_kv, w_o, ffn_norm_g, w_up, ffn_dw_w, w_down, final_norm_g, loss_target, m_mix_norm_g, m_w_in, m_conv_dw_w, m_conv_dw_b, m_conv_ln_g, m_conv_ln_b, m_w_conv_out, m_w_pool_grp, m_pool_scale, m_w_out, m_xattn_norm_g, m_mem_norm_g, m_w_q, m_w_kv, m_w_o, m_ffn_norm_g, m_w_up, m_ffn_dw_w, m_w_down, m_final_norm_g, v_mix_norm_g, v_w_in, v_conv_dw_w, v_conv_dw_b, v_conv_ln_g, v_conv_ln_b, v_w_conv_out, v_w_pool_grp, v_pool_scale, v_w_out, v_xattn_norm_g, v_mem_norm_g, v_w_q, v_w_kv, v_w_o, v_ffn_norm_g, v_w_up, v_ffn_dw_w, v_w_down, v_final_norm_g):
    w = dict(mix_norm_g=mix_norm_g, w_in=w_in, conv_dw_w=conv_dw_w, conv_dw_b=conv_dw_b, conv_ln_g=conv_ln_g, conv_ln_b=conv_ln_b,
             w_conv_out=w_conv_out, w_pool_grp=w_pool_grp, pool_scale=pool_scale, w_out=w_out, xattn_norm_g=xattn_norm_g,
             mem_norm_g=mem_norm_g, w_q=w_q, w_kv=w_kv, w_o=w_o, ffn_norm_g=ffn_norm_g, w_up=w_up, ffn_dw_w=ffn_dw_w, w_down=w_down,
             final_norm_g=final_norm_g)
    m = dict(zip(WEIGHTS, (m_mix_norm_g, m_w_in, m_conv_dw_w, m_conv_dw_b, m_conv_ln_g, m_conv_ln_b, m_w_conv_out, m_w_pool_grp, m_pool_scale,
                           m_w_out, m_xattn_norm_g, m_mem_norm_g, m_w_q, m_w_kv, m_w_o, m_ffn_norm_g, m_w_up, m_ffn_dw_w, m_w_down, m_final_norm_g)))
    v = dict(zip(WEIGHTS, (v_mix_norm_g, v_w_in, v_conv_dw_w, v_conv_dw_b, v_conv_ln_g, v_conv_ln_b, v_w_conv_out, v_w_pool_grp, v_pool_scale,
                           v_w_out, v_xattn_norm_g, v_mem_norm_g, v_w_q, v_w_kv, v_w_o, v_ffn_norm_g, v_w_up, v_ffn_dw_w, v_w_down, v_final_norm_g)))
    xi, yi, ci, j = _place()
    jc = jnp.stack([j, ci]).astype(jnp.int32)
    L = w_in.shape[0]
    G = len(POOL_WINDOWS)
    kinds = dict(BIG)

    def to_mat(name, t):
        if name == "w_pool":
            return jnp.swapaxes(t, 2, 3)
        return t[:, None]

    def from_mat(name, t):
        if name == "w_pool":
            return jnp.swapaxes(t, 2, 3)
        return t[:, 0]

    src = {name: w["w_pool_grp" if name == "w_pool" else name] for name, _ in BIG}

    KC, cs_c = conv_dw_w.shape[1], conv_dw_w.shape[2]
    KF, cs_f = ffn_dw_w.shape[1], ffn_dw_w.shape[2]
    taps = jnp.concatenate([conv_dw_w.reshape(L * KC, cs_c), ffn_dw_w.reshape(L * KF * (cs_f // cs_c), cs_c)], axis=0)
    n_taps = taps.shape[0]
    taps = jnp.pad(taps, ((0, (-n_taps) % 16), (0, 0)))
    shards = [_shard_view(to_mat(name, src[name]).astype(BF), kind) for name, kind in BIG] + [_shard_view(taps[None, None], "row")]
    fulls = _gather_weights(shards, [kind for _, kind in BIG] + ["row"])
    W = {}
    for (name, kind), f in zip(BIG, fulls):
        W[name] = f.reshape(L * G if name == "w_pool" else L, -1, f.shape[-1])
    taps_all = fulls[-1].reshape(N_CHIPS, -1, cs_c)[:, :n_taps]
    V = {name: w[name] for name in VECTORS}
    V["conv_dw_w"] = taps_all[:, :L * KC].reshape(N_CHIPS, L, KC, cs_c).transpose(1, 2, 0, 3).reshape(L, KC, N_CHIPS * cs_c)
    V["ffn_dw_w"] = taps_all[:, L * KC:].reshape(N_CHIPS, L, KF, cs_f).transpose(1, 2, 0, 3).reshape(L, KF, N_CHIPS * cs_f)

    loss, grad_x, big, small = _local_step(x, mem, loss_target, W, V)
    loss = lax.psum(loss[0, 0], ("x", "y", "c"))

    gviews, gkinds, gnames = [], [], []
    for name, kind in BIG:
        for l in range(L):
            g = big[l][name]
            if g.ndim == 2:
                g = g[None]
            P, R, C = g.shape
            gviews.append(g.reshape(P, 2, R // 2, C) if kind == "col" else g.reshape(P, N_CHIPS, 2, R // (2 * N_CHIPS), C))
            gkinds.append(kind)
            gnames.append(f"{name}_{l}")
    lands = _sibling_exchange(gviews, gkinds)
    owns, pieces = [], []
    for g, land, kind, nm in zip(gviews, lands, gkinds, gnames):
        own, allp = _chip_sum(g, land, kind, jc, "chip_sum_" + nm)
        owns.append(own)
        pieces.append(allp)
    got = _chip_exchange(pieces)
    mine = []
    for i in range(len(BIG)):
        shard = None
        for l in range(L):
            t = i * L + l
            shard = _final_sum(owns[t], got[t], jc, shard, l, L, "final_sum_" + gnames[t])
        mine.append(shard)
    gshards = _halves_exchange(mine)
    grads = {}
    for (name, kind), gs in zip(BIG, gshards):
        Lg, P, _, RH, CS = gs.shape
        grads["w_pool_grp" if name == "w_pool" else name] = from_mat(name, gs.reshape(Lg, P, 2 * RH, CS))

    small_w = conv_dw_b.shape[1]
    order = VECTORS + ("conv_dw_w", "ffn_dw_w")
    parts = [_rows(small[name], small_w) for name in order]
    counts = [p.shape[0] for p in parts]
    summed = _reduce_small(_pack(parts))
    off = 0
    for name, cnt in zip(order, counts):
        t = summed[off:off + cnt]
        off += cnt + (-cnt) % 8
        if name in VECTORS:
            grads[name] = t.reshape(w[name].shape)
        else:
            full = t.reshape(small[name].shape)
            cs = w[name].shape[2]
            grads[name] = lax.dynamic_slice_in_dim(full, j * cs, cs, axis=2)

    delta, new_m, new_v = {}, {}, {}
    for name, _ in BIG:
        key = "w_pool_grp" if name == "w_pool" else name
        outs = _adamw(*[to_mat(name, t) for t in (w[key], grads[key], m[key], v[key])], "adamw_" + name)
        delta[key], new_m[key], new_v[key] = [from_mat(name, t) for t in outs]
    vec = [_pack([_rows(d[name], small_w) for name in VECTORS]) for d in (w, grads, m, v)]
    outs = _adamw(*vec, "adamw_vectors")
    off = 0
    for name in VECTORS:
        cnt = w[name].size // small_w
        for d, t in zip((delta, new_m, new_v), outs):
            d[name] = t[off:off + cnt].reshape(w[name].shape)
        off += cnt + (-cnt) % 8
    for name in ("conv_dw_w", "ffn_dw_w"):
        delta[name], new_m[name], new_v[name] = _adamw(w[name], grads[name], m[name], v[name], "adamw_" + name)

    return (loss, grad_x, *[grads[k] for k in WEIGHTS], *[delta[k] for k in WEIGHTS], *[new_m[k] for k in WEIGHTS], *[new_v[k] for k in WEIGHTS])
```

```python
import functools

import jax
import jax.numpy as jnp
from jax import lax
from jax.experimental import pallas as pl
from jax.experimental.pallas import tpu as pltpu

F32 = jnp.float32
BF = jnp.bfloat16
SDS = jax.ShapeDtypeStruct
MESH = pl.DeviceIdType.MESH
ANY = pl.BlockSpec(memory_space=pl.ANY)

EPS = 1e-6
XA_HEADS = 4
POOL_WINDOWS = (2, 4, 8, 16)
N_CHIPS = 4
ADAM_LR, ADAM_B1, ADAM_B2, ADAM_EPS, ADAM_WD, ADAM_STEP = 0.001, 0.9, 0.999, 1e-08, 0.01, 10

LANES = 128
ROW_BLOCK = 512
VMEM_LIMIT = 56 * 1024 * 1024


def _params(*sem):
    return pltpu.CompilerParams(dimension_semantics=sem if sem else None, vmem_limit_bytes=VMEM_LIMIT)


def _tile(n, cap, mult=LANES):
    if n <= cap:
        return n
    for t in range(cap - cap % mult, 0, -mult):
        if n % t == 0:
            return t
    return n


_DN = {"nn": (((1,), (0,)), ((), ())), "nt": (((1,), (1,)), ((), ())), "tn": (((0,), (0,)), ((), ()))}


class _Side:
    def __init__(self, ins, outs, n, make):
        self.ins, self.outs, self.n, self.make = list(ins), list(outs), n, make


def _mm(a, b, dims, out_dtype, name, res=None, bl=None, tm=1024, tn=512, tk=1024, side=None):
    bs = b.shape[1:] if bl is not None else b.shape
    if dims == "nn":
        (M, K), (K2, N) = a.shape, bs
    elif dims == "nt":
        (M, K), (N, K2) = a.shape, bs
    else:
        (K, M), (K2, N) = a.shape, bs
    assert K == K2, (name, a.shape, b.shape)
    tm, tn, tk = _tile(M, tm), _tile(N, tn), _tile(K, tk)
    nk = K // tk
    lead = (None,) if bl is not None else ()
    pre = (lambda *ix: (bl,) + ix) if bl is not None else (lambda *ix: ix)
    if dims == "tn":
        a_spec = pl.BlockSpec((tk, tm), lambda i, j, k: (k, i))
    else:
        a_spec = pl.BlockSpec((tm, tk), lambda i, j, k: (i, k))
    if dims == "nt":
        b_spec = pl.BlockSpec(lead + (tn, tk), lambda i, j, k: pre(j, k))
    else:
        b_spec = pl.BlockSpec(lead + (tk, tn), lambda i, j, k: pre(k, j))
    o_spec = pl.BlockSpec((tm, tn), lambda i, j, k: (i, j))
    in_specs, args = [a_spec, b_spec], [a, b]
    if res is not None:
        in_specs.append(o_spec)
        args.append(res)
    n_main = len(args)
    n_si, n_so = (len(side.ins), len(side.outs)) if side is not None else (0, 0)
    gi, gj = M // tm, N // tn

    def body(*refs):
        a_ref, b_ref = refs[0], refs[1]
        r_ref = refs[2] if res is not None else None
        o_ref = refs[n_main + n_si]
        scratch = refs[n_main + n_si + 1 + n_so:]
        i, j, k = pl.program_id(0), pl.program_id(1), pl.program_id(2)
        if side is not None:
            copies = side.make(refs[n_main:n_main + n_si], refs[n_main + n_si + 1:n_main + n_si + 1 + n_so], scratch[-2], scratch[-1])

            @pl.when((i == 0) & (j == 0) & (k == 0))
            def _():
                for cp in copies:
                    cp.start()

        p = lax.dot_general(a_ref[...].astype(BF), b_ref[...].astype(BF), _DN[dims], preferred_element_type=F32)

        def finish(t):
            if r_ref is not None:
                t = t + r_ref[...]
            o_ref[...] = t.astype(out_dtype)

        if nk == 1:
            finish(p)
        else:
            acc = scratch[0]

            @pl.when(k == 0)
            def _():
                acc[...] = p

            @pl.when(k > 0)
            def _():
                acc[...] += p

            @pl.when(k == nk - 1)
            def _():
                finish(acc[...])

        if side is not None:
            @pl.when((i == gi - 1) & (j == gj - 1) & (k == nk - 1))
            def _():
                for cp in copies:
                    cp.wait()

    scratch_shapes = [pltpu.VMEM((tm, tn), F32)] if nk > 1 else []
    if side is None:
        return pl.pallas_call(
            body, grid=(gi, gj, nk), in_specs=in_specs, out_specs=o_spec, out_shape=SDS((M, N), out_dtype), scratch_shapes=scratch_shapes,
            compiler_params=_params("parallel", "parallel", "arbitrary"), name=name)(*args)
    outs = pl.pallas_call(
        body, grid=(gi, gj, nk), in_specs=in_specs + [ANY] * n_si, out_specs=[o_spec] + [ANY] * n_so,
        out_shape=[SDS((M, N), out_dtype)] + side.outs,
        scratch_shapes=scratch_shapes + [pltpu.SemaphoreType.DMA((side.n,)), pltpu.SemaphoreType.DMA((side.n,))],
        compiler_params=_params("arbitrary", "arbitrary", "arbitrary"), name=name)(*args, *side.ins)
    return outs[0], list(outs[1:])


def _rms(x, g):
    return x * lax.rsqrt(jnp.mean(x * x, axis=-1, keepdims=True) + EPS) * g


def _ln_silu(x, g, b):
    mu = jnp.mean(x, axis=-1, keepdims=True)
    xc = x - mu
    var = jnp.mean(xc * xc, axis=-1, keepdims=True)
    return jax.nn.silu(xc * lax.rsqrt(var + EPS) * g + b)


def _merge(gc, gp, yc, yp, ps):
    return jax.nn.sigmoid(gc) * yc + jax.nn.sigmoid(gp) * (yp * ps)


def _gated(gate, val):
    return jax.nn.gelu(gate) * val


def _rms_fwd(x, g, name):
    T, D = x.shape
    tb = _tile(T, ROW_BLOCK, 8)

    def body(x_ref, g_ref, o_ref):
        o_ref[...] = _rms(x_ref[...], g_ref[...]).astype(BF)

    row = pl.BlockSpec((tb, D), lambda i: (i, 0))
    return pl.pallas_call(body, grid=(T // tb,), in_specs=[row, pl.BlockSpec((1, D), lambda i: (0, 0))], out_specs=row,
                          out_shape=SDS((T, D), BF), compiler_params=_params("parallel"), name=name)(x, g.reshape(1, D))


def _rms_bwd(x, g, dh, dres, name):
    T, D = x.shape
    tb = _tile(T, ROW_BLOCK, 8)

    def body(*refs):
        if dres is not None:
            x_ref, g_ref, dh_ref, dres_ref, dx_ref, dxb_ref, dg_ref = refs
        else:
            x_ref, g_ref, dh_ref, dx_ref, dxb_ref, dg_ref = refs
        _, vjp = jax.vjp(_rms, x_ref[...], g_ref[...])
        dx, dg = vjp(dh_ref[...].astype(F32))
        if dres is not None:
            dx = dx + dres_ref[...]
        dx_ref[...] = dx
        dxb_ref[...] = dx.astype(BF)

        @pl.when(pl.program_id(0) == 0)
        def _():
            dg_ref[...] = jnp.zeros_like(dg_ref)

        dg_ref[...] += dg

    row = pl.BlockSpec((tb, D), lambda i: (i, 0))
    vec = pl.BlockSpec((1, D), lambda i: (0, 0))
    ins = [x, g.reshape(1, D), dh] + ([dres] if dres is not None else [])
    return pl.pallas_call(
        body, grid=(T // tb,), in_specs=[row, vec, row] + ([row] if dres is not None else []), out_specs=[row, row, vec],
        out_shape=[SDS((T, D), F32), SDS((T, D), BF), SDS((1, D), F32)], compiler_params=_params("arbitrary"), name=name)(*ins)


def _loss_bwd(x, g, target, name):
    T, D = x.shape
    tb = _tile(T, ROW_BLOCK, 8)
    nb = T // tb

    def body(x_ref, g_ref, t_ref, loss_ref, dx_ref, dg_ref, acc):
        i = pl.program_id(0)
        y, vjp = jax.vjp(_rms, x_ref[...], g_ref[...])
        err = y - t_ref[...]
        dx, dg = vjp(err * (1.0 / D))
        dx_ref[...] = dx

        @pl.when(i == 0)
        def _():
            dg_ref[...] = jnp.zeros_like(dg_ref)
            acc[...] = jnp.zeros_like(acc)

        dg_ref[...] += dg
        acc[...] += jnp.sum(err * err, axis=0, keepdims=True)

        @pl.when(i == nb - 1)
        def _():
            loss_ref[...] = jnp.full(loss_ref.shape, (0.5 / D) * jnp.sum(acc[...]), F32)

    row = pl.BlockSpec((tb, D), lambda i: (i, 0))
    vec = pl.BlockSpec((1, D), lambda i: (0, 0))
    return pl.pallas_call(
        body, grid=(nb,), in_specs=[row, vec, row], out_specs=[pl.BlockSpec((1, LANES), lambda i: (0, 0)), row, vec],
        out_shape=[SDS((1, LANES), F32), SDS((T, D), F32), SDS((1, D), F32)], scratch_shapes=[pltpu.VMEM((1, D), F32)],
        compiler_params=_params("arbitrary"), name=name)(x, g.reshape(1, D), target)


def _ln_silu_fwd(cv, g, b, name):
    T, C = cv.shape
    tb = _tile(T, ROW_BLOCK, 8)

    def body(x_ref, g_ref, b_ref, o_ref):
        o_ref[...] = _ln_silu(x_ref[...], g_ref[...], b_ref[...]).astype(BF)

    row = pl.BlockSpec((tb, C), lambda i: (i, 0))
    vec = pl.BlockSpec((1, C), lambda i: (0, 0))
    return pl.pallas_call(body, grid=(T // tb,), in_specs=[row, vec, vec], out_specs=row, out_shape=SDS((T, C), BF),
                          compiler_params=_params("parallel"), name=name)(cv, g.reshape(1, C), b.reshape(1, C))


def _ln_silu_bwd(cv, g, b, dy, name):
    T, C = cv.shape
    tb = _tile(T, ROW_BLOCK, 8)

    def body(x_ref, g_ref, b_ref, dy_ref, dx_ref, dg_ref, db_ref):
        _, vjp = jax.vjp(_ln_silu, x_ref[...], g_ref[...], b_ref[...])
        dx, dg, db = vjp(dy_ref[...].astype(F32))
        dx_ref[...] = dx

        @pl.when(pl.program_id(0) == 0)
        def _():
            dg_ref[...] = jnp.zeros_like(dg_ref)
            db_ref[...] = jnp.zeros_like(db_ref)

        dg_ref[...] += dg
        db_ref[...] += db

    row = pl.BlockSpec((tb, C), lambda i: (i, 0))
    vec = pl.BlockSpec((1, C), lambda i: (0, 0))
    return pl.pallas_call(
        body, grid=(T // tb,), in_specs=[row, vec, vec, row], out_specs=[row, vec, vec],
        out_shape=[SDS((T, C), F32), SDS((1, C), F32), SDS((1, C), F32)], compiler_params=_params("arbitrary"),
        name=name)(cv, g.reshape(1, C), b.reshape(1, C), dy)


def _merge_fwd(proj, yc, yp, ps, C, name):
    T, D = yc.shape
    tb = _tile(T, ROW_BLOCK, 8)
    nj = D // C

    def body(gc_ref, gp_ref, yc_ref, yp_ref, ps_ref, o_ref):
        o_ref[...] = _merge(gc_ref[...], gp_ref[...], yc_ref[...], yp_ref[...], ps_ref[...]).astype(BF)

    blk = pl.BlockSpec((tb, C), lambda i, j: (i, j))
    return pl.pallas_call(
        body, grid=(T // tb, nj),
        in_specs=[pl.BlockSpec((tb, C), lambda i, j: (i, 3 + j)), pl.BlockSpec((tb, C), lambda i, j: (i, 3 + nj + j)), blk, blk,
                  pl.BlockSpec((1, C), lambda i, j: (0, j))],
        out_specs=blk, out_shape=SDS((T, D), BF), compiler_params=_params("parallel", "parallel"), name=name)(proj, proj, yc, yp, ps.reshape(1, D))


def _merge_bwd(proj, yc, yp, ps, dm, C, name):
    T, D = yc.shape
    tb = _tile(T, ROW_BLOCK, 8)
    nj = D // C

    def body(gc_ref, gp_ref, yc_ref, yp_ref, ps_ref, dm_ref, dgc_ref, dgp_ref, dyc_ref, dyp_ref, dps_ref):
        _, vjp = jax.vjp(_merge, gc_ref[...], gp_ref[...], yc_ref[...], yp_ref[...], ps_ref[...])
        dgc, dgp, dyc, dyp, dps = vjp(dm_ref[...].astype(F32))
        dgc_ref[...] = dgc.astype(BF)
        dgp_ref[...] = dgp.astype(BF)
        dyc_ref[...] = dyc.astype(BF)
        dyp_ref[...] = dyp.astype(BF)

        @pl.when(pl.program_id(1) == 0)
        def _():
            dps_ref[...] = jnp.zeros_like(dps_ref)

        dps_ref[...] += dps

    blk = pl.BlockSpec((tb, C), lambda j, i: (i, j))
    vec = pl.BlockSpec((1, C), lambda j, i: (0, j))
    return pl.pallas_call(
        body, grid=(nj, T // tb),
        in_specs=[pl.BlockSpec((tb, C), lambda j, i: (i, 3 + j)), pl.BlockSpec((tb, C), lambda j, i: (i, 3 + nj + j)), blk, blk, vec, blk],
        out_specs=[blk, blk, blk, blk, vec], out_shape=[SDS((T, D), BF)] * 4 + [SDS((1, D), F32)],
        compiler_params=_params("parallel", "arbitrary"), name=name)(proj, proj, yc, yp, ps.reshape(1, D), dm)


def _shd(v, s, rows):
    if s == 0:
        return v
    return jnp.where(rows >= s, pltpu.roll(v, s, 0), 0.0)


def _shu(v, s, rows):
    if s == 0:
        return v
    n = v.shape[0]
    return jnp.where(rows < n - s, pltpu.roll(v, n - s, 0), 0.0)


def _glu_conv_fwd(proj, w, b, Bn, S, C, name):
    K = w.shape[0]
    sl = min(LANES, C)
    ns = C // sl

    def body(a_ref, gl_ref, w_ref, b_ref, o_ref):
        y0 = a_ref[...] * jax.nn.sigmoid(gl_ref[...])
        rows = lax.broadcasted_iota(jnp.int32, y0.shape, 0)
        acc = jnp.zeros_like(y0) + b_ref[...]
        for k in range(K):
            acc = acc + w_ref[k:k + 1, :] * _shd(y0, K - 1 - k, rows)
        o_ref[...] = acc

    return pl.pallas_call(
        body, grid=(Bn, ns),
        in_specs=[pl.BlockSpec((S, sl), lambda bi, j: (bi, j)), pl.BlockSpec((S, sl), lambda bi, j: (bi, ns + j)),
                  pl.BlockSpec((K, sl), lambda bi, j: (0, j)), pl.BlockSpec((1, sl), lambda bi, j: (0, j))],
        out_specs=pl.BlockSpec((S, sl), lambda bi, j: (bi, j)), out_shape=SDS((Bn * S, C), F32),
        compiler_params=_params("parallel", "parallel"), name=name)(proj, proj, w, b.reshape(1, C))


def _glu_conv_bwd(proj, w, dcv, Bn, S, C, name):
    K = w.shape[0]
    sl = min(LANES, C)
    ns = C // sl

    def body(a_ref, gl_ref, w_ref, d_ref, da_ref, dgl_ref, dw_ref, db_ref):
        a = a_ref[...]
        sg = jax.nn.sigmoid(gl_ref[...])
        y0 = a * sg
        d = d_ref[...]
        rows = lax.broadcasted_iota(jnp.int32, y0.shape, 0)

        @pl.when(pl.program_id(1) == 0)
        def _():
            dw_ref[...] = jnp.zeros_like(dw_ref)
            db_ref[...] = jnp.zeros_like(db_ref)

        dy0 = jnp.zeros_like(y0)
        for k in range(K):
            s = K - 1 - k
            dw_ref[k:k + 1, :] += jnp.sum(d * _shd(y0, s, rows), axis=0, keepdims=True)
            dy0 = dy0 + w_ref[k:k + 1, :] * _shu(d, s, rows)
        db_ref[...] += jnp.sum(d, axis=0, keepdims=True)
        da_ref[...] = (dy0 * sg).astype(BF)
        dgl_ref[...] = (dy0 * a * sg * (1.0 - sg)).astype(BF)

    blk = pl.BlockSpec((S, sl), lambda j, bi: (bi, j))
    return pl.pallas_call(
        body, grid=(ns, Bn),
        in_specs=[blk, pl.BlockSpec((S, sl), lambda j, bi: (bi, ns + j)), pl.BlockSpec((K, sl), lambda j, bi: (0, j)), blk],
        out_specs=[blk, blk, pl.BlockSpec((K, sl), lambda j, bi: (0, j)), pl.BlockSpec((1, sl), lambda j, bi: (0, j))],
        out_shape=[SDS((Bn * S, C), BF), SDS((Bn * S, C), BF), SDS((K, C), F32), SDS((1, C), F32)],
        compiler_params=_params("parallel", "arbitrary"), name=name)(proj, proj, w, dcv)


def _pool_z(u, g, rows):
    s2 = u + _shd(u, 1, rows)
    s4 = s2 + _shd(s2, 2, rows)
    s8 = s4 + _shd(s4, 4, rows)
    s16 = s8 + _shd(s8, 8, rows)
    sw = jnp.where(g == 0, s2, jnp.where(g == 1, s4, jnp.where(g == 2, s8, s16)))
    cnt = jnp.minimum(rows + 1, POOL_WINDOWS[0] << g).astype(F32)
    return sw / cnt - u, cnt


def _pool_fwd(proj, wpt, l, Bn, S, C, D, name):
    G = len(POOL_WINDOWS)
    gd, go = C // G, D // G

    def body(u_ref, w_ref, o_ref):
        g = pl.program_id(1)
        u = u_ref[...]
        rows = lax.broadcasted_iota(jnp.int32, u.shape, 0)
        zp, _ = _pool_z(u, g, rows)
        o_ref[...] = lax.dot_general(zp.astype(BF), w_ref[...], _DN["nt"], preferred_element_type=F32)

    return pl.pallas_call(
        body, grid=(Bn, G),
        in_specs=[pl.BlockSpec((S, gd), lambda bi, g: (bi, 2 * G + g)), pl.BlockSpec((None, go, gd), lambda bi, g: (l * G + g, 0, 0))],
        out_specs=pl.BlockSpec((S, go), lambda bi, g: (bi, g)), out_shape=SDS((Bn * S, D), F32),
        compiler_params=_params("parallel", "parallel"), name=name)(proj, wpt)


def _pool_bwd(proj, wpt, dyp, l, Bn, S, C, D, name):
    G = len(POOL_WINDOWS)
    gd, go = C // G, D // G

    def body(u_ref, w_ref, d_ref, du_ref, dw_ref):
        g = pl.program_id(0)
        u = u_ref[...]
        rows = lax.broadcasted_iota(jnp.int32, u.shape, 0)
        zp, cnt = _pool_z(u, g, rows)
        d = d_ref[...]
        dzp = lax.dot_general(d, w_ref[...], _DN["nn"], preferred_element_type=F32)

        @pl.when(pl.program_id(1) == 0)
        def _():
            dw_ref[...] = jnp.zeros_like(dw_ref)

        dw_ref[...] += lax.dot_general(d, zp.astype(BF), _DN["tn"], preferred_element_type=F32)
        dsw = dzp / cnt
        zero = jnp.zeros_like(dsw)
        d16 = jnp.where(g == 3, dsw, zero)
        d8 = jnp.where(g == 2, dsw, zero) + d16 + _shu(d16, 8, rows)
        d4 = jnp.where(g == 1, dsw, zero) + d8 + _shu(d8, 4, rows)
        d2 = jnp.where(g == 0, dsw, zero) + d4 + _shu(d4, 2, rows)
        d1 = d2 + _shu(d2, 1, rows)
        du_ref[...] = (d1 - dzp).astype(BF)

    return pl.pallas_call(
        body, grid=(G, Bn),
        in_specs=[pl.BlockSpec((S, gd), lambda g, bi: (bi, 2 * G + g)), pl.BlockSpec((None, go, gd), lambda g, bi: (l * G + g, 0, 0)),
                  pl.BlockSpec((S, go), lambda g, bi: (bi, g))],
        out_specs=[pl.BlockSpec((S, gd), lambda g, bi: (bi, g)), pl.BlockSpec((None, go, gd), lambda g, bi: (g, 0, 0))],
        out_shape=[SDS((Bn * S, C), BF), SDS((G, go, gd), F32)],
        compiler_params=_params("parallel", "arbitrary"), name=name)(proj, wpt, dyp)


def _ffn_conv(u, w_ref, rows):
    K = w_ref.shape[0]
    acc = w_ref[K - 1:K, :] * u
    for k in range(K - 1):
        acc = acc + w_ref[k:k + 1, :] * _shd(u, K - 1 - k, rows)
    return acc


def _ffn_cb(F):
    return _tile(F, 256)


def _ffn_act_fwd(up0, w, Bn, S, F, name):
    cb = _ffn_cb(F)
    nj = F // cb

    def body(g_ref, v_ref, wg_ref, wv_ref, o_ref):
        rows = lax.broadcasted_iota(jnp.int32, g_ref.shape, 0)
        o_ref[...] = _gated(_ffn_conv(g_ref[...], wg_ref, rows), _ffn_conv(v_ref[...], wv_ref, rows)).astype(BF)

    K = w.shape[0]
    return pl.pallas_call(
        body, grid=(Bn, nj),
        in_specs=[pl.BlockSpec((S, cb), lambda bi, j: (bi, j)), pl.BlockSpec((S, cb), lambda bi, j: (bi, nj + j)),
                  pl.BlockSpec((K, cb), lambda bi, j: (0, j)), pl.BlockSpec((K, cb), lambda bi, j: (0, nj + j))],
        out_specs=pl.BlockSpec((S, cb), lambda bi, j: (bi, j)), out_shape=SDS((Bn * S, F), BF),
        compiler_params=_params("parallel", "parallel"), name=name)(up0, up0, w, w)


def _ffn_act_bwd(up0, w, dg, Bn, S, F, name):
    cb = _ffn_cb(F)
    nj = F // cb
    K = w.shape[0]

    def body(g_ref, v_ref, wg_ref, wv_ref, d_ref, dgo_ref, dvo_ref, dwg_ref, dwv_ref):
        rows = lax.broadcasted_iota(jnp.int32, g_ref.shape, 0)
        g0, v0 = g_ref[...], v_ref[...]
        _, vjp = jax.vjp(_gated, _ffn_conv(g0, wg_ref, rows), _ffn_conv(v0, wv_ref, rows))
        dgc, dvc = vjp(d_ref[...].astype(F32))

        @pl.when(pl.program_id(1) == 0)
        def _():
            dwg_ref[...] = jnp.zeros_like(dwg_ref)
            dwv_ref[...] = jnp.zeros_like(dwv_ref)

        for u0, dc, w_ref, dw_ref, do_ref in ((g0, dgc, wg_ref, dwg_ref, dgo_ref), (v0, dvc, wv_ref, dwv_ref, dvo_ref)):
            du = jnp.zeros_like(u0)
            for k in range(K):
                s = K - 1 - k
                dw_ref[k:k + 1, :] += jnp.sum(dc * _shd(u0, s, rows), axis=0, keepdims=True)
                du = du + w_ref[k:k + 1, :] * _shu(dc, s, rows)
            do_ref[...] = du.astype(BF)

    blk = pl.BlockSpec((S, cb), lambda j, bi: (bi, j))
    wblk = pl.BlockSpec((K, cb), lambda j, bi: (0, j))
    return pl.pallas_call(
        body, grid=(nj, Bn),
        in_specs=[blk, pl.BlockSpec((S, cb), lambda j, bi: (bi, nj + j)), wblk, pl.BlockSpec((K, cb), lambda j, bi: (0, nj + j)), blk],
        out_specs=[blk, blk, wblk, wblk],
        out_shape=[SDS((Bn * S, F), BF), SDS((Bn * S, F), BF), SDS((K, F), F32), SDS((K, F), F32)],
        compiler_params=_params("parallel", "arbitrary"), name=name)(up0, up0, w, w, dg)


def _softmax_rows(q, k, scale):
    sc = lax.dot_general(q, k, _DN["nt"], preferred_element_type=F32) * scale
    e = jnp.exp(sc - jnp.max(sc, axis=-1, keepdims=True))
    return e / jnp.sum(e, axis=-1, keepdims=True)


def _attn_ts(S):
    return _tile(S, 1024, 8)


def _attn_fwd(q, kv, Bn, S, Mn, D, name):
    H = XA_HEADS
    dh = D // H
    ts = _attn_ts(S)
    nsb = S // ts
    scale = dh ** -0.5

    def body(q_ref, k_ref, v_ref, o_ref):
        p = _softmax_rows(q_ref[...], k_ref[...], scale)
        o_ref[...] = lax.dot_general(p.astype(BF), v_ref[...], _DN["nn"], preferred_element_type=F32).astype(BF)

    qblk = pl.BlockSpec((ts, dh), lambda bi, h, s: (bi * nsb + s, h))
    return pl.pallas_call(
        body, grid=(Bn, H, nsb),
        in_specs=[qblk, pl.BlockSpec((Mn, dh), lambda bi, h, s: (bi, h)), pl.BlockSpec((Mn, dh), lambda bi, h, s: (bi, H + h))],
        out_specs=qblk, out_shape=SDS((Bn * S, D), BF), compiler_params=_params("parallel", "parallel", "parallel"), name=name)(q, kv, kv)


def _attn_bwd(q, kv, datt, Bn, S, Mn, D, name):
    H = XA_HEADS
    dh = D // H
    ts = _attn_ts(S)
    nsb = S // ts
    scale = dh ** -0.5

    def body(q_ref, k_ref, v_ref, do_ref, dq_ref, dk_ref, dv_ref):
        q, k, v, do = q_ref[...], k_ref[...], v_ref[...], do_ref[...]
        p = _softmax_rows(q, k, scale)
        dp = lax.dot_general(do, v, _DN["nt"], preferred_element_type=F32)
        ds = (p * (dp - jnp.sum(dp * p, axis=-1, keepdims=True)) * scale).astype(BF)
        dq_ref[...] = lax.dot_general(ds, k, _DN["nn"], preferred_element_type=F32).astype(BF)

        @pl.when(pl.program_id(2) == 0)
        def _():
            dk_ref[...] = jnp.zeros_like(dk_ref)
            dv_ref[...] = jnp.zeros_like(dv_ref)

        dk_ref[...] += lax.dot_general(ds, q, _DN["tn"], preferred_element_type=F32)
        dv_ref[...] += lax.dot_general(p.astype(BF), do, _DN["tn"], preferred_element_type=F32)

    qblk = pl.BlockSpec((ts, dh), lambda bi, h, s: (bi * nsb + s, h))
    kblk = pl.BlockSpec((Mn, dh), lambda bi, h, s: (bi, h))
    return pl.pallas_call(
        body, grid=(Bn, H, nsb),
        in_specs=[qblk, kblk, pl.BlockSpec((Mn, dh), lambda bi, h, s: (bi, H + h)), qblk],
        out_specs=[qblk, kblk, kblk], out_shape=[SDS((Bn * S, D), BF), SDS((Bn * Mn, D), F32), SDS((Bn * Mn, D), F32)],
        compiler_params=_params("parallel", "parallel", "arbitrary"), name=name)(q, kv, kv, datt)


class _Sides:
    def __init__(self, by_key=None):
        self.by_key, self.landed = dict(by_key or {}), {}

    def mm(self, key, *args, **kw):
        side = self.by_key.get(key)
        if side is None:
            return _mm(*args, **kw)
        out, self.landed[key] = _mm(*args, side=side, **kw)
        return out


def _layer_fwd(x, mem_n, W, V, l, dims, sides):
    Bn, S, Mn, D, C, F = dims
    n = f"l{l}_"
    h = _rms_fwd(x, V["mix_norm_g"][l], n + "mix_norm")
    proj = sides.mm("proj", h, W["w_in"], "nn", F32, n + "proj", bl=0, tn=C)
    cv = _glu_conv_fwd(proj, V["conv_dw_w"][l], V["conv_dw_b"][l], Bn, S, C, n + "glu_conv")
    yc1 = _ln_silu_fwd(cv, V["conv_ln_g"][l], V["conv_ln_b"][l], n + "ln_silu")
    yc = sides.mm("conv_out", yc1, W["w_conv_out"], "nn", F32, n + "conv_out", bl=0)
    yp = _pool_fwd(proj, W["w_pool"], 0, Bn, S, C, D, n + "pool")
    merged = _merge_fwd(proj, yc, yp, V["pool_scale"][l], C, n + "merge")
    x1 = sides.mm("out_proj", merged, W["w_out"], "nn", F32, n + "out_proj", res=x, bl=0)
    hq = _rms_fwd(x1, V["xattn_norm_g"][l], n + "xattn_norm")
    q = sides.mm("q_proj", hq, W["w_q"], "nn", BF, n + "q_proj", bl=0)
    kv = _mm(mem_n, W["w_kv"], "nn", BF, n + "kv_proj", bl=0)
    att = _attn_fwd(q, kv, Bn, S, Mn, D, n + "attn")
    x2 = sides.mm("o_proj", att, W["w_o"], "nn", F32, n + "o_proj", res=x1, bl=0)
    hf = _rms_fwd(x2, V["ffn_norm_g"][l], n + "ffn_norm")
    up0 = sides.mm("up_proj", hf, W["w_up"], "nn", F32, n + "up_proj", bl=0)
    gact = _ffn_act_fwd(up0, V["ffn_dw_w"][l], Bn, S, F, n + "ffn_act")
    x3 = sides.mm("down_proj", gact, W["w_down"], "nn", F32, n + "down_proj", res=x2, bl=0, tk=1408)
    return x3, dict(x=x, h=h, proj=proj, cv=cv, yc1=yc1, yc=yc, yp=yp, merged=merged, x1=x1, hq=hq, q=q, kv=kv, att=att, x2=x2, hf=hf,
                    up0=up0, gact=gact)


def _layer_bwd(dx, dxb, dmem_n, sv, mem_n, W, V, l, dims, sides):
    Bn, S, Mn, D, C, F = dims
    n = f"l{l}_b_"
    gw, sm = {}, {}
    dgact = sides.mm("d_gact", dxb, W["w_down"], "nt", BF, n + "d_gact", bl=0)
    gw["w_down"] = sides.mm("dw_down", sv["gact"], dxb, "tn", F32, n + "dw_down")
    dg0, dv0, dwg, dwv = _ffn_act_bwd(sv["up0"], V["ffn_dw_w"][l], dgact, Bn, S, F, n + "ffn_act")
    sm["ffn_dw_w"] = jnp.concatenate([dwg, dwv], axis=1)
    dup0 = jnp.concatenate([dg0, dv0], axis=1)
    dhf = sides.mm("d_hf", dup0, W["w_up"], "nt", F32, n + "d_hf", bl=0, tk=1408)
    gw["w_up"] = sides.mm("dw_up", sv["hf"], dup0, "tn", F32, n + "dw_up")
    dx2, dx2b, sm["ffn_norm_g"] = _rms_bwd(sv["x2"], V["ffn_norm_g"][l], dhf, dx, n + "ffn_norm")
    datt = _mm(dx2b, W["w_o"], "nt", BF, n + "d_att", bl=0)
    gw["w_o"] = _mm(sv["att"], dx2b, "tn", F32, n + "dw_o")
    dq, dk, dv = _attn_bwd(sv["q"], sv["kv"], datt, Bn, S, Mn, D, n + "attn")
    dkv = jnp.concatenate([dk, dv], axis=1)
    gw["w_kv"] = _mm(mem_n, dkv, "tn", F32, n + "dw_kv")
    dmem_n = _mm(dkv, W["w_kv"], "nt", F32, n + "d_mem", res=dmem_n, bl=0)
    dhq = _mm(dq, W["w_q"], "nt", F32, n + "d_hq", bl=0)
    gw["w_q"] = _mm(sv["hq"], dq, "tn", F32, n + "dw_q")
    dx1, dx1b, sm["xattn_norm_g"] = _rms_bwd(sv["x1"], V["xattn_norm_g"][l], dhq, dx2, n + "xattn_norm")
    dmerged = _mm(dx1b, W["w_out"], "nt", BF, n + "d_merged", bl=0)
    gw["w_out"] = _mm(sv["merged"], dx1b, "tn", F32, n + "dw_out")
    dgc, dgp, dyc, dyp, sm["pool_scale"] = _merge_bwd(sv["proj"], sv["yc"], sv["yp"], V["pool_scale"][l], dmerged, C, n + "merge")
    du, gw["w_pool"] = _pool_bwd(sv["proj"], W["w_pool"], dyp, 0, Bn, S, C, D, n + "pool")
    dyc1 = _mm(dyc, W["w_conv_out"], "nt", F32, n + "d_yc1", bl=0)
    gw["w_conv_out"] = _mm(sv["yc1"], dyc, "tn", F32, n + "dw_conv_out")
    dcv, sm["conv_ln_g"], sm["conv_ln_b"] = _ln_silu_bwd(sv["cv"], V["conv_ln_g"][l], V["conv_ln_b"][l], dyc1, n + "ln_silu")
    da, dgl, sm["conv_dw_w"], sm["conv_dw_b"] = _glu_conv_bwd(sv["proj"], V["conv_dw_w"][l], dcv, Bn, S, C, n + "glu_conv")
    dproj = jnp.concatenate([da, dgl, du, dgc, dgp], axis=1)
    dh = _mm(dproj, W["w_in"], "nt", F32, n + "d_h", bl=0, tk=896)
    gw["w_in"] = _mm(sv["h"], dproj, "tn", F32, n + "dw_in")
    dx, dxb, sm["mix_norm_g"] = _rms_bwd(sv["x"], V["mix_norm_g"][l], dh, dx1, n + "mix_norm")
    return dx, dxb, dmem_n, gw, sm


BIG = (("w_in", "col"), ("w_conv_out", "col"), ("w_pool", "row"), ("w_out", "row"), ("w_q", "row"), ("w_kv", "col"),
       ("w_o", "row"), ("w_up", "col"), ("w_down", "row"))
FWD_CARRY = {"proj": ("w_in",), "conv_out": ("w_conv_out", "w_pool"), "out_proj": ("w_out", "w_q"), "q_proj": ("w_o",), "o_proj": ("w_kv",),
             "up_proj": ("w_up",), "down_proj": ("w_down",)}
BWD_CARRY = {"d_gact": ("w_down",), "dw_down": ("w_up",), "d_hf": ("w_in", "w_kv"), "dw_up": ("w_q", "w_o", "w_out", "w_conv_out", "w_pool")}


def _place():
    xi, yi, ci = lax.axis_index("x"), lax.axis_index("y"), lax.axis_index("c")
    return xi, yi, ci, 2 * xi + yi


def _chip_peer(xi, yi, ci, r):
    return (xi ^ (r >> 1), yi ^ (r & 1), ci)


def _full_shard(ref, kind, k, cs):
    if kind == "col":
        return ref.at[:, :, :, :, pl.ds(pl.multiple_of(k * cs, cs), cs)]
    return ref.at[:, :, k]


def _gather_weights(shards, kinds):
    n = len(shards)
    outs = []
    for s, kind in zip(shards, kinds):
        L, P, _, RH, CS = s.shape
        outs.append(SDS((L, P, 2, RH, CS * N_CHIPS) if kind == "col" else (L, P, N_CHIPS, 2, RH, CS), s.dtype))
    per = 7

    def body(*refs):
        srcs, fulls, (ssem, rsem) = refs[:n], refs[n:2 * n], refs[2 * n:]
        xi, yi, ci, j = _place()
        sib = (xi, yi, 1 - ci)

        def piece(i, k, c):
            kind, cs = kinds[i], shards[i].shape[-1]
            if kind == "col":
                return fulls[i].at[:, :, c, :, pl.ds(pl.multiple_of(k * cs, cs), cs)]
            return fulls[i].at[:, :, k, c]

        def copy(i, slot, src, dst, dev):
            return pltpu.make_async_remote_copy(src_ref=src, dst_ref=dst, send_sem=ssem.at[per * i + slot], recv_sem=rsem.at[per * i + slot],
                                                device_id=dev, device_id_type=MESH)

        own, first, passed = [], [], []
        for i in range(n):
            for r in (1, 2, 3):
                first.append(copy(i, r - 1, srcs[i].at[:, :, ci], piece(i, j, ci), _chip_peer(xi, yi, ci, r)))
                first[-1].start()
        for i in range(n):
            own.append(copy(i, 6, srcs[i], _full_shard(fulls[i], kinds[i], j, shards[i].shape[-1]), sib))
            own[-1].start()
        for i in range(n):
            for r in (1, 2, 3):
                got = piece(i, j ^ r, ci)
                copy(i, r - 1, got, got, sib).wait_recv()
                passed.append(copy(i, 2 + r, got, got, sib))
                passed[-1].start()
        for i in range(n):
            for r in (1, 2, 3):
                got = piece(i, j ^ r, 1 - ci)
                copy(i, 2 + r, got, got, sib).wait_recv()
        for cp in own:
            cp.wait()
        for cp in first + passed:
            cp.wait_send()

    return pl.pallas_call(
        body, in_specs=[ANY] * n, out_specs=[ANY] * n, out_shape=outs,
        scratch_shapes=[pltpu.SemaphoreType.DMA((per * n,)), pltpu.SemaphoreType.DMA((per * n,))], name="gather_weights")(*shards)


def _full_sds(s, kind):
    L, P, _, RH, CS = s.shape
    return SDS((L, P, 2, RH, CS * N_CHIPS) if kind == "col" else (L, P, N_CHIPS, 2, RH, CS), s.dtype)


def _gather_piece(full, kind, cs, k, c):
    if kind == "col":
        return full.at[:, :, c, :, pl.ds(pl.multiple_of(k * cs, cs), cs)]
    return full.at[:, :, k, c]


def _side_gather(shards, kinds):
    n = len(shards)

    def make(srcs, fulls, ssem, rsem):
        xi, yi, ci, j = _place()
        return [pltpu.make_async_remote_copy(
            src_ref=srcs[i].at[:, :, ci], dst_ref=_gather_piece(fulls[i], kinds[i], shards[i].shape[-1], j, ci), send_sem=ssem.at[3 * i + r - 1],
            recv_sem=rsem.at[3 * i + r - 1], device_id=_chip_peer(xi, yi, ci, r), device_id_type=MESH) for i in range(n) for r in (1, 2, 3)]

    return _Side(shards, [_full_sds(s, k) for s, k in zip(shards, kinds)], 3 * n, make)


def _gather_pass(fulls, shards, kinds, name):
    n = len(fulls)

    def body(*refs):
        srcs, outs, (ssem, rsem) = refs[n:2 * n], refs[2 * n:3 * n], refs[3 * n:]
        xi, yi, ci, j = _place()
        sib = (xi, yi, 1 - ci)
        cps = []
        for i in range(n):
            cs = shards[i].shape[-1]
            for r in (1, 2, 3):
                got = _gather_piece(outs[i], kinds[i], cs, j ^ r, ci)
                cps.append(pltpu.make_async_remote_copy(src_ref=got, dst_ref=got, send_sem=ssem.at[4 * i + r - 1], recv_sem=rsem.at[4 * i + r - 1],
                                                        device_id=sib, device_id_type=MESH))
            cps.append(pltpu.make_async_remote_copy(src_ref=srcs[i], dst_ref=_full_shard(outs[i], kinds[i], j, cs), send_sem=ssem.at[4 * i + 3],
                                                    recv_sem=rsem.at[4 * i + 3], device_id=sib, device_id_type=MESH))
        for cp in cps:
            cp.start()
        for cp in cps:
            cp.wait()

    return pl.pallas_call(
        body, in_specs=[ANY] * (2 * n), out_specs=[ANY] * n, out_shape=[SDS(f.shape, f.dtype) for f in fulls],
        input_output_aliases={i: i for i in range(n)},
        scratch_shapes=[pltpu.SemaphoreType.DMA((4 * n,)), pltpu.SemaphoreType.DMA((4 * n,))], name=name)(*fulls, *shards)


def _sibling_exchange(gviews, kinds, name):
    n = len(gviews)
    outs = [SDS(g.shape[:1] + g.shape[2:] if kind == "col" else g.shape[:2] + g.shape[3:], g.dtype) for g, kind in zip(gviews, kinds)]

    def body(*refs):
        gs, lands, (ssem, rsem) = refs[:n], refs[n:2 * n], refs[2 * n:]
        xi, yi, ci, _ = _place()
        cps = []
        for i in range(n):
            src = gs[i].at[:, 1 - ci] if kinds[i] == "col" else gs[i].at[:, :, 1 - ci]
            cps.append(pltpu.make_async_remote_copy(src_ref=src, dst_ref=lands[i], send_sem=ssem.at[i], recv_sem=rsem.at[i],
                                                    device_id=(xi, yi, 1 - ci), device_id_type=MESH))
            cps[-1].start()
        for cp in cps:
            cp.wait()

    return pl.pallas_call(body, in_specs=[ANY] * n, out_specs=[ANY] * n, out_shape=outs,
                          scratch_shapes=[pltpu.SemaphoreType.DMA((n,)), pltpu.SemaphoreType.DMA((n,))], name=name)(*gviews)


def _chip_sum(g, land, kind, jc, name):
    if kind == "col":
        P, _, RH, C = g.shape
        CS = C // N_CHIPS
        g_spec = pl.BlockSpec((None, None, RH, CS), lambda p, r, jc: (p, jc[1], 0, jc[0] ^ r))
        l_spec = pl.BlockSpec((None, RH, CS), lambda p, r, jc: (p, 0, jc[0] ^ r))
    else:
        P, _, _, RH, CS = g.shape
        g_spec = pl.BlockSpec((None, None, None, RH, CS), lambda p, r, jc: (p, jc[0] ^ r, jc[1], 0, 0))
        l_spec = pl.BlockSpec((None, None, RH, CS), lambda p, r, jc: (p, jc[0] ^ r, 0, 0))

    def body(jc_ref, g_ref, l_ref, own_ref, all_ref):
        s = g_ref[...] + l_ref[...]
        all_ref[...] = s.astype(BF)

        @pl.when(pl.program_id(1) == 0)
        def _():
            own_ref[...] = s

    return pl.pallas_call(
        body, grid_spec=pltpu.PrefetchScalarGridSpec(
            num_scalar_prefetch=1, grid=(P, N_CHIPS), in_specs=[g_spec, l_spec],
            out_specs=[pl.BlockSpec((None, RH, CS), lambda p, r, jc: (p, 0, 0)), pl.BlockSpec((None, None, RH, CS), lambda p, r, jc: (r, p, 0, 0))]),
        out_shape=[SDS((P, RH, CS), F32), SDS((N_CHIPS, P, RH, CS), BF)], compiler_params=_params("parallel", "arbitrary"), name=name)(jc, g, land)


def _chip_exchange_copies(srcs, lands, ssem, rsem):
    xi, yi, ci, _ = _place()
    return [pltpu.make_async_remote_copy(src_ref=srcs[i].at[r], dst_ref=lands[i].at[r], send_sem=ssem.at[3 * i + r - 1],
                                         recv_sem=rsem.at[3 * i + r - 1], device_id=_chip_peer(xi, yi, ci, r), device_id_type=MESH)
            for i in range(len(srcs)) for r in (1, 2, 3)]


def _side_chip_exchange(pieces):
    return _Side(pieces, [SDS(p.shape, p.dtype) for p in pieces], 3 * len(pieces), _chip_exchange_copies)


def _chip_exchange(pieces):
    n = len(pieces)

    def body(*refs):
        cps = _chip_exchange_copies(refs[:n], refs[n:2 * n], *refs[2 * n:])
        for cp in cps:
            cp.start()
        for cp in cps:
            cp.wait()

    return pl.pallas_call(body, in_specs=[ANY] * n, out_specs=[ANY] * n, out_shape=[SDS(p.shape, p.dtype) for p in pieces],
                          scratch_shapes=[pltpu.SemaphoreType.DMA((3 * n,)), pltpu.SemaphoreType.DMA((3 * n,))], name="grad_chip_exchange")(*pieces)


def _final_sum(own, land, jc, shard, l, L, name):
    P, RH, CS = own.shape

    def body(jc_ref, o_ref, a_ref, b_ref, c_ref, *rest):
        rest[-1][...] = ((o_ref[...] + a_ref[...].astype(F32)) + b_ref[...].astype(F32)) + c_ref[...].astype(F32)

    blk = pl.BlockSpec((None, RH, CS), lambda p, jc: (p, 0, 0))
    in_specs = [blk] + [pl.BlockSpec((None, None, RH, CS), functools.partial(lambda r, p, jc: (r, p, 0, 0), r)) for r in (1, 2, 3)]
    args = [jc, own, land, land, land]
    if shard is not None:
        in_specs.append(ANY)
        args.append(shard)
    return pl.pallas_call(
        body, grid_spec=pltpu.PrefetchScalarGridSpec(
            num_scalar_prefetch=1, grid=(P,), in_specs=in_specs,
            out_specs=pl.BlockSpec((None, None, None, RH, CS), lambda p, jc: (l, p, jc[1], 0, 0))),
        out_shape=SDS((L, P, 2, RH, CS), F32), input_output_aliases={5: 0} if shard is not None else {},
        compiler_params=_params("arbitrary"), name=name)(*args)


def _halves_exchange(shards):
    n = len(shards)

    def body(*refs):
        outs, (ssem, rsem) = refs[n:2 * n], refs[2 * n:]
        xi, yi, ci, _ = _place()
        cps = []
        for i in range(n):
            mine = outs[i].at[:, :, ci]
            cps.append(pltpu.make_async_remote_copy(src_ref=mine, dst_ref=mine, send_sem=ssem.at[i], recv_sem=rsem.at[i],
                                                    device_id=(xi, yi, 1 - ci), device_id_type=MESH))
            cps[-1].start()
        for i in range(n):
            land = outs[i].at[:, :, 1 - ci]
            pltpu.make_async_remote_copy(src_ref=land, dst_ref=land, send_sem=ssem.at[i], recv_sem=rsem.at[i],
                                         device_id=(xi, yi, 1 - ci), device_id_type=MESH).wait_recv()
        for cp in cps:
            cp.wait_send()

    return pl.pallas_call(body, in_specs=[ANY] * n, out_specs=[ANY] * n, out_shape=[SDS(s.shape, s.dtype) for s in shards],
                          input_output_aliases={i: i for i in range(n)},
                          scratch_shapes=[pltpu.SemaphoreType.DMA((n,)), pltpu.SemaphoreType.DMA((n,))], name="grad_halves_exchange")(*shards)


def _reduce_small(part):
    NR, Wd = part.shape
    ND = 2 * N_CHIPS

    def body(p_ref, o_ref, land, ssem, rsem):
        xi, yi, ci, j = _place()
        me = 2 * j + ci
        land[me] = p_ref[...]
        cps = []
        for rr in range(1, ND):
            dev = (xi ^ (rr >> 2), yi ^ ((rr >> 1) & 1), ci ^ (rr & 1))
            cps.append(pltpu.make_async_remote_copy(src_ref=p_ref, dst_ref=land.at[me], send_sem=ssem.at[rr - 1], recv_sem=rsem.at[rr - 1],
                                                    device_id=dev, device_id_type=MESH))
            cps[-1].start()
        for rr in range(1, ND):
            got = land.at[me ^ rr]
            pltpu.make_async_remote_copy(src_ref=got, dst_ref=got, send_sem=ssem.at[rr - 1], recv_sem=rsem.at[rr - 1],
                                         device_id=(xi, yi, ci), device_id_type=MESH).wait_recv()
        acc = land[0]
        for d in range(1, ND):
            acc = acc + land[d]
        o_ref[...] = acc
        for cp in cps:
            cp.wait_send()

    vm = pl.BlockSpec(memory_space=pltpu.VMEM)
    return pl.pallas_call(body, in_specs=[vm], out_specs=vm, out_shape=SDS((NR, Wd), F32),
                          scratch_shapes=[pltpu.VMEM((ND, NR, Wd), F32), pltpu.SemaphoreType.DMA((ND - 1,)), pltpu.SemaphoreType.DMA((ND - 1,))],
                          name="small_grad_allreduce")(part)


def _adamw(w, g, m, v, name):
    shape = w.shape
    C = shape[-1]
    R = w.size // C
    tb = _tile(R, max(8, (1 << 18) // C), 8)

    def body(w_ref, g_ref, m_ref, v_ref, d_ref, mo_ref, vo_ref):
        g = g_ref[...]
        m = ADAM_B1 * m_ref[...] + (1.0 - ADAM_B1) * g
        v = ADAM_B2 * v_ref[...] + (1.0 - ADAM_B2) * jnp.square(g)
        m_hat = m / (1.0 - ADAM_B1 ** ADAM_STEP)
        v_hat = v / (1.0 - ADAM_B2 ** ADAM_STEP)
        d_ref[...] = -ADAM_LR * (m_hat / (jnp.sqrt(v_hat) + ADAM_EPS) + ADAM_WD * w_ref[...])
        mo_ref[...] = m
        vo_ref[...] = v

    blk = pl.BlockSpec((tb, C), lambda i: (i, 0))
    outs = pl.pallas_call(body, grid=(R // tb,), in_specs=[blk] * 4, out_specs=[blk] * 3, out_shape=[SDS((R, C), F32)] * 3,
                          compiler_params=_params("parallel"), name=name)(*[t.reshape(R, C) for t in (w, g, m, v)])
    return [t.reshape(shape) for t in outs]


WEIGHTS = ("mix_norm_g", "w_in", "conv_dw_w", "conv_dw_b", "conv_ln_g", "conv_ln_b", "w_conv_out", "w_pool_grp", "pool_scale", "w_out",
           "xattn_norm_g", "mem_norm_g", "w_q", "w_kv", "w_o", "ffn_norm_g", "w_up", "ffn_dw_w", "w_down", "final_norm_g")
VECTORS = ("mix_norm_g", "conv_dw_b", "conv_ln_g", "conv_ln_b", "pool_scale", "xattn_norm_g", "mem_norm_g", "ffn_norm_g", "final_norm_g")


def _shard_view(t, kind):
    L, P, R, C = t.shape
    return t.reshape(L, P, 2, R // 2, C)


def _rows(t, width):
    return t.reshape(-1, width)


def _pack(parts):
    return jnp.concatenate([jnp.pad(p, ((0, (-p.shape[0]) % 8), (0, 0))) for p in parts], axis=0)


def kernel(x, mem, mix_norm_g, w_in, conv_dw_w, conv_dw_b, conv_ln_g, conv_ln_b, w_conv_out, w_pool_grp, pool_scale, w_out, xattn_norm_g, mem_norm_g, w_q, w_kv, w_o, ffn_norm_g, w_up, ffn_dw_w, w_down, final_norm_g, loss_target, m_mix_norm_g, m_w_in, m_conv_dw_w, m_conv_dw_b, m_conv_ln_g, m_conv_ln_b, m_w_conv_out, m_w_pool_grp, m_pool_scale, m_w_out, m_xattn_norm_g, m_mem_norm_g, m_w_q, m_w_kv, m_w_o, m_ffn_norm_g, m_w_up, m_ffn_dw_w, m_w_down, m_final_norm_g, v_mix_norm_g, v_w_in, v_conv_dw_w, v_conv_dw_b, v_conv_ln_g, v_conv_ln_b, v_w_conv_out, v_w_pool_grp, v_pool_scale, v_w_out, v_xattn_norm_g, v_mem_norm_g, v_w_q, v_w_kv, v_w_o, v_ffn_norm_g, v_w_up, v_ffn_dw_w, v_w_down, v_final_norm_g):
    w = dict(mix_norm_g=mix_norm_g, w_in=w_in, conv_dw_w=conv_dw_w, conv_dw_b=conv_dw_b, conv_ln_g=conv_ln_g, conv_ln_b=conv_ln_b,
             w_conv_out=w_conv_out, w_pool_grp=w_pool_grp, pool_scale=pool_scale, w_out=w_out, xattn_norm_g=xattn_norm_g,
             mem_norm_g=mem_norm_g, w_q=w_q, w_kv=w_kv, w_o=w_o, ffn_norm_g=ffn_norm_g, w_up=w_up, ffn_dw_w=ffn_dw_w, w_down=w_down,
             final_norm_g=final_norm_g)
    m = dict(zip(WEIGHTS, (m_mix_norm_g, m_w_in, m_conv_dw_w, m_conv_dw_b, m_conv_ln_g, m_conv_ln_b, m_w_conv_out, m_w_pool_grp, m_pool_scale,
                           m_w_out, m_xattn_norm_g, m_mem_norm_g, m_w_q, m_w_kv, m_w_o, m_ffn_norm_g, m_w_up, m_ffn_dw_w, m_w_down, m_final_norm_g)))
    v = dict(zip(WEIGHTS, (v_mix_norm_g, v_w_in, v_conv_dw_w, v_conv_dw_b, v_conv_ln_g, v_conv_ln_b, v_w_conv_out, v_w_pool_grp, v_pool_scale,
                           v_w_out, v_xattn_norm_g, v_mem_norm_g, v_w_q, v_w_kv, v_w_o, v_ffn_norm_g, v_w_up, v_ffn_dw_w, v_w_down, v_final_norm_g)))
    xi, yi, ci, j = _place()
    jc = jnp.stack([j, ci]).astype(jnp.int32)
    L = w_in.shape[0]
    G = len(POOL_WINDOWS)
    kinds = dict(BIG)

    def to_mat(name, t):
        if name == "w_pool":
            return jnp.swapaxes(t, 2, 3)
        return t[:, None]

    def from_mat(name, t):
        if name == "w_pool":
            return jnp.swapaxes(t, 2, 3)
        return t[:, 0]

    src = {name: w["w_pool_grp" if name == "w_pool" else name] for name, _ in BIG}

    KC, cs_c = conv_dw_w.shape[1], conv_dw_w.shape[2]
    KF, cs_f = ffn_dw_w.shape[1], ffn_dw_w.shape[2]
    taps = jnp.concatenate([conv_dw_w.reshape(L * KC, cs_c), ffn_dw_w.reshape(L * KF * (cs_f // cs_c), cs_c)], axis=0)
    n_taps = taps.shape[0]
    taps = jnp.pad(taps, ((0, (-n_taps) % 16), (0, 0)))
    names = [name for name, _ in BIG]
    mats = {name: to_mat(name, src[name]).astype(BF) for name in names}

    def layer_shards(l, subset):
        return [_shard_view(mats[name][l:l + 1], kinds[name]) for name in subset]

    def as_weights(subset, fulls):
        return {name: f.reshape(G if name == "w_pool" else 1, -1, f.shape[-1]) for name, f in zip(subset, fulls)}

    fulls = _gather_weights(layer_shards(0, names) + [_shard_view(taps[None, None], "row")], [kinds[name] for name in names] + ["row"])
    W = [as_weights(names, fulls[:-1])]
    taps_all = fulls[-1].reshape(N_CHIPS, -1, cs_c)[:, :n_taps]
    V = {name: w[name] for name in VECTORS}
    V["conv_dw_w"] = taps_all[:, :L * KC].reshape(N_CHIPS, L, KC, cs_c).transpose(1, 2, 0, 3).reshape(L, KC, N_CHIPS * cs_c)
    V["ffn_dw_w"] = taps_all[:, L * KC:].reshape(N_CHIPS, L, KF, cs_f).transpose(1, 2, 0, 3).reshape(L, KF, N_CHIPS * cs_f)

    Bn, S, D = x.shape
    Mn = mem.shape[1]
    dims = (Bn, S, Mn, D, conv_dw_b.shape[1], w_down.shape[1] * N_CHIPS)
    xt = x.reshape(Bn * S, D)
    memf = mem.reshape(Bn * Mn, D)
    mem_n = _rms_fwd(memf, V["mem_norm_g"], "mem_norm")
    saved = []
    for l in range(L):
        sides = _Sides()
        if l + 1 < L:
            sides = _Sides({key: _side_gather(layer_shards(l + 1, subset), [kinds[name] for name in subset]) for key, subset in FWD_CARRY.items()})
        xt, sv = _layer_fwd(xt, mem_n, W[l], V, l, dims, sides)
        saved.append(sv)
        if l + 1 < L:
            nxt = {}
            for key, subset in FWD_CARRY.items():
                done = _gather_pass(sides.landed[key], layer_shards(l + 1, subset), [kinds[name] for name in subset], f"gather_pass_l{l + 1}_{key}")
                nxt.update(as_weights(subset, done))
            W.append(nxt)
    loss, dx, dgf = _loss_bwd(xt, V["final_norm_g"], loss_target.reshape(Bn * S, D), "loss")
    loss = lax.psum(loss[0, 0], ("x", "y", "c"))

    def views(gw):
        out = []
        for name in names:
            g = gw[name] if gw[name].ndim == 3 else gw[name][None]
            P, R, C = g.shape
            out.append(g.reshape(P, 2, R // 2, C) if kinds[name] == "col" else g.reshape(P, N_CHIPS, 2, R // (2 * N_CHIPS), C))
        return out

    dxb, dmem_n = dx, None
    smalls, owns, got = [None] * L, [None] * L, [None] * L
    pieces = None
    for l in reversed(range(L)):
        sides = _Sides()
        if pieces is not None:
            sides = _Sides({key: _side_chip_exchange([pieces[names.index(name)] for name in subset]) for key, subset in BWD_CARRY.items()})
        dx, dxb, dmem_n, gw, smalls[l] = _layer_bwd(dx, dxb, dmem_n, saved[l], mem_n, W[l], V, l, dims, sides)
        if pieces is not None:
            got[l + 1] = [None] * len(names)
            for key, subset in BWD_CARRY.items():
                for name, land in zip(subset, sides.landed[key]):
                    got[l + 1][names.index(name)] = land
        gv = views(gw)
        lands = _sibling_exchange(gv, [kinds[name] for name in names], f"grad_sibling_exchange_l{l}")
        owns[l], pieces = [], []
        for name, g, land in zip(names, gv, lands):
            own, allp = _chip_sum(g, land, kinds[name], jc, f"chip_sum_{name}_{l}")
            owns[l].append(own)
            pieces.append(allp)
    got[0] = _chip_exchange(pieces)
    grad_x = dx.reshape(Bn, S, D)
    _, _, dgm = _rms_bwd(memf, V["mem_norm_g"], dmem_n, None, "mem_norm_b")
    small = {k: jnp.stack([sm[k] for sm in smalls]) if k in ("conv_dw_w", "ffn_dw_w") else jnp.concatenate([sm[k] for sm in smalls], axis=0)
             for k in smalls[0]}
    small["mem_norm_g"] = dgm
    small["final_norm_g"] = dgf
    mine = []
    for i, name in enumerate(names):
        shard = None
        for l in range(L):
            shard = _final_sum(owns[l][i], got[l][i], jc, shard, l, L, f"final_sum_{name}_{l}")
        mine.append(shard)
    gshards = _halves_exchange(mine)
    grads = {}
    for (name, kind), gs in zip(BIG, gshards):
        Lg, P, _, RH, CS = gs.shape
        grads["w_pool_grp" if name == "w_pool" else name] = from_mat(name, gs.reshape(Lg, P, 2 * RH, CS))

    small_w = conv_dw_b.shape[1]
    order = VECTORS + ("conv_dw_w", "ffn_dw_w")
    parts = [_rows(small[name], small_w) for name in order]
    counts = [p.shape[0] for p in parts]
    summed = _reduce_small(_pack(parts))
    off = 0
    for name, cnt in zip(order, counts):
        t = summed[off:off + cnt]
        off += cnt + (-cnt) % 8
        if name in VECTORS:
            grads[name] = t.reshape(w[name].shape)
        else:
            full = t.reshape(small[name].shape)
            cs = w[name].shape[2]
            grads[name] = lax.dynamic_slice_in_dim(full, j * cs, cs, axis=2)

    delta, new_m, new_v = {}, {}, {}
    for name, _ in BIG:
        key = "w_pool_grp" if name == "w_pool" else name
        outs = _adamw(*[to_mat(name, t) for t in (w[key], grads[key], m[key], v[key])], "adamw_" + name)
        delta[key], new_m[key], new_v[key] = [from_mat(name, t) for t in outs]
    vec = [_pack([_rows(d[name], small_w) for name in VECTORS]) for d in (w, grads, m, v)]
    outs = _adamw(*vec, "adamw_vectors")
    off = 0
    for name in VECTORS:
        cnt = w[name].size // small_w
        for d, t in zip((delta, new_m, new_v), outs):
            d[name] = t[off:off + cnt].reshape(w[name].shape)
        off += cnt + (-cnt) % 8
    for name in ("conv_dw_w", "ffn_dw_w"):
        delta[name], new_m[name], new_v[name] = _adamw(w[name], grads[name], m[name], v[name], "adamw_" + name)

    return (loss, grad_x, *[grads[k] for k in WEIGHTS], *[delta[k] for k in WEIGHTS], *[new_m[k] for k in WEIGHTS], *[new_v[k] for k in WEIGHTS])
```

```python
import functools

import jax
import jax.numpy as jnp
from jax import lax
from jax.experimental import pallas as pl
from jax.experimental.pallas import tpu as pltpu

F32 = jnp.float32
BF = jnp.bfloat16
SDS = jax.ShapeDtypeStruct
MESH = pl.DeviceIdType.MESH
ANY = pl.BlockSpec(memory_space=pl.ANY)

EPS = 1e-6
XA_HEADS = 4
POOL_WINDOWS = (2, 4, 8, 16)
N_CHIPS = 4
ADAM_LR, ADAM_B1, ADAM_B2, ADAM_EPS, ADAM_WD, ADAM_STEP = 0.001, 0.9, 0.999, 1e-08, 0.01, 10

LANES = 128
ROW_BLOCK = 512
VMEM_LIMIT = 56 * 1024 * 1024


def _params(*sem):
    return pltpu.CompilerParams(dimension_semantics=sem if sem else None, vmem_limit_bytes=VMEM_LIMIT)


def _tile(n, cap, mult=LANES):
    if n <= cap:
        return n
    for t in range(cap - cap % mult, 0, -mult):
        if n % t == 0:
            return t
    return n


_DN = {"nn": (((1,), (0,)), ((), ())), "nt": (((1,), (1,)), ((), ())), "tn": (((0,), (0,)), ((), ()))}


class _Side:
    def __init__(self, ins, outs, n, make):
        self.ins, self.outs, self.n, self.make = list(ins), list(outs), n, make


MM_TILE_CAP = 1408


def _mm(a, b, dims, out_dtype, name, res=None, bl=None, tm=MM_TILE_CAP, tn=MM_TILE_CAP, tk=MM_TILE_CAP, side=None):
    bs = b.shape[1:] if bl is not None else b.shape
    if dims == "nn":
        (M, K), (K2, N) = a.shape, bs
    elif dims == "nt":
        (M, K), (N, K2) = a.shape, bs
    else:
        (K, M), (K2, N) = a.shape, bs
    assert K == K2, (name, a.shape, b.shape)
    tm, tn, tk = _tile(M, tm), _tile(N, tn), _tile(K, tk)
    nk = K // tk
    lead = (None,) if bl is not None else ()
    pre = (lambda *ix: (bl,) + ix) if bl is not None else (lambda *ix: ix)
    if dims == "tn":
        a_spec = pl.BlockSpec((tk, tm), lambda i, j, k: (k, i))
    else:
        a_spec = pl.BlockSpec((tm, tk), lambda i, j, k: (i, k))
    if dims == "nt":
        b_spec = pl.BlockSpec(lead + (tn, tk), lambda i, j, k: pre(j, k))
    else:
        b_spec = pl.BlockSpec(lead + (tk, tn), lambda i, j, k: pre(k, j))
    o_spec = pl.BlockSpec((tm, tn), lambda i, j, k: (i, j))
    in_specs, args = [a_spec, b_spec], [a, b]
    if res is not None:
        in_specs.append(o_spec)
        args.append(res)
    n_main = len(args)
    n_si, n_so = (len(side.ins), len(side.outs)) if side is not None else (0, 0)
    gi, gj = M // tm, N // tn

    def body(*refs):
        a_ref, b_ref = refs[0], refs[1]
        r_ref = refs[2] if res is not None else None
        o_ref = refs[n_main + n_si]
        scratch = refs[n_main + n_si + 1 + n_so:]
        i, j, k = pl.program_id(0), pl.program_id(1), pl.program_id(2)
        if side is not None:
            copies = side.make(refs[n_main:n_main + n_si], refs[n_main + n_si + 1:n_main + n_si + 1 + n_so], scratch[-2], scratch[-1])

            @pl.when((i == 0) & (j == 0) & (k == 0))
            def _():
                for cp in copies:
                    cp.start()

        p = lax.dot_general(a_ref[...].astype(BF), b_ref[...].astype(BF), _DN[dims], preferred_element_type=F32)

        def finish(t):
            if r_ref is not None:
                t = t + r_ref[...]
            o_ref[...] = t.astype(out_dtype)

        if nk == 1:
            finish(p)
        else:
            acc = scratch[0]

            @pl.when(k == 0)
            def _():
                acc[...] = p

            @pl.when(k > 0)
            def _():
                acc[...] += p

            @pl.when(k == nk - 1)
            def _():
                finish(acc[...])

        if side is not None:
            @pl.when((i == gi - 1) & (j == gj - 1) & (k == nk - 1))
            def _():
                for cp in copies:
                    cp.wait()

    scratch_shapes = [pltpu.VMEM((tm, tn), F32)] if nk > 1 else []
    if side is None:
        return pl.pallas_call(
            body, grid=(gi, gj, nk), in_specs=in_specs, out_specs=o_spec, out_shape=SDS((M, N), out_dtype), scratch_shapes=scratch_shapes,
            compiler_params=_params("parallel", "parallel", "arbitrary"), name=name)(*args)
    outs = pl.pallas_call(
        body, grid=(gi, gj, nk), in_specs=in_specs + [ANY] * n_si, out_specs=[o_spec] + [ANY] * n_so,
        out_shape=[SDS((M, N), out_dtype)] + side.outs,
        scratch_shapes=scratch_shapes + [pltpu.SemaphoreType.DMA((side.n,)), pltpu.SemaphoreType.DMA((side.n,))],
        compiler_params=_params("arbitrary", "arbitrary", "arbitrary"), name=name)(*args, *side.ins)
    return outs[0], list(outs[1:])


def _rms(x, g):
    return x * lax.rsqrt(jnp.mean(x * x, axis=-1, keepdims=True) + EPS) * g


def _ln_silu(x, g, b):
    mu = jnp.mean(x, axis=-1, keepdims=True)
    xc = x - mu
    var = jnp.mean(xc * xc, axis=-1, keepdims=True)
    return jax.nn.silu(xc * lax.rsqrt(var + EPS) * g + b)


def _merge(gc, gp, yc, yp, ps):
    return jax.nn.sigmoid(gc) * yc + jax.nn.sigmoid(gp) * (yp * ps)


def _gated(gate, val):
    return jax.nn.gelu(gate) * val


def _rms_fwd(x, g, name):
    T, D = x.shape
    tb = _tile(T, ROW_BLOCK, 8)

    def body(x_ref, g_ref, o_ref):
        o_ref[...] = _rms(x_ref[...], g_ref[...]).astype(BF)

    row = pl.BlockSpec((tb, D), lambda i: (i, 0))
    return pl.pallas_call(body, grid=(T // tb,), in_specs=[row, pl.BlockSpec((1, D), lambda i: (0, 0))], out_specs=row,
                          out_shape=SDS((T, D), BF), compiler_params=_params("parallel"), name=name)(x, g.reshape(1, D))


def _rms_bwd(x, g, dh, dres, name):
    T, D = x.shape
    tb = _tile(T, ROW_BLOCK, 8)

    def body(*refs):
        if dres is not None:
            x_ref, g_ref, dh_ref, dres_ref, dx_ref, dxb_ref, dg_ref = refs
        else:
            x_ref, g_ref, dh_ref, dx_ref, dxb_ref, dg_ref = refs
        _, vjp = jax.vjp(_rms, x_ref[...], g_ref[...])
        dx, dg = vjp(dh_ref[...].astype(F32))
        if dres is not None:
            dx = dx + dres_ref[...]
        dx_ref[...] = dx
        dxb_ref[...] = dx.astype(BF)

        @pl.when(pl.program_id(0) == 0)
        def _():
            dg_ref[...] = jnp.zeros_like(dg_ref)

        dg_ref[...] += dg

    row = pl.BlockSpec((tb, D), lambda i: (i, 0))
    vec = pl.BlockSpec((1, D), lambda i: (0, 0))
    ins = [x, g.reshape(1, D), dh] + ([dres] if dres is not None else [])
    return pl.pallas_call(
        body, grid=(T // tb,), in_specs=[row, vec, row] + ([row] if dres is not None else []), out_specs=[row, row, vec],
        out_shape=[SDS((T, D), F32), SDS((T, D), BF), SDS((1, D), F32)], compiler_params=_params("arbitrary"), name=name)(*ins)


def _loss_bwd(x, g, target, name):
    T, D = x.shape
    tb = _tile(T, ROW_BLOCK, 8)
    nb = T // tb

    def body(x_ref, g_ref, t_ref, loss_ref, dx_ref, dg_ref, acc):
        i = pl.program_id(0)
        y, vjp = jax.vjp(_rms, x_ref[...], g_ref[...])
        err = y - t_ref[...]
        dx, dg = vjp(err * (1.0 / D))
        dx_ref[...] = dx

        @pl.when(i == 0)
        def _():
            dg_ref[...] = jnp.zeros_like(dg_ref)
            acc[...] = jnp.zeros_like(acc)

        dg_ref[...] += dg
        acc[...] += jnp.sum(err * err, axis=0, keepdims=True)

        @pl.when(i == nb - 1)
        def _():
            loss_ref[...] = jnp.full(loss_ref.shape, (0.5 / D) * jnp.sum(acc[...]), F32)

    row = pl.BlockSpec((tb, D), lambda i: (i, 0))
    vec = pl.BlockSpec((1, D), lambda i: (0, 0))
    return pl.pallas_call(
        body, grid=(nb,), in_specs=[row, vec, row], out_specs=[pl.BlockSpec((1, LANES), lambda i: (0, 0)), row, vec],
        out_shape=[SDS((1, LANES), F32), SDS((T, D), F32), SDS((1, D), F32)], scratch_shapes=[pltpu.VMEM((1, D), F32)],
        compiler_params=_params("arbitrary"), name=name)(x, g.reshape(1, D), target)


def _ln_silu_fwd(cv, g, b, name):
    T, C = cv.shape
    tb = _tile(T, ROW_BLOCK, 8)

    def body(x_ref, g_ref, b_ref, o_ref):
        o_ref[...] = _ln_silu(x_ref[...], g_ref[...], b_ref[...]).astype(BF)

    row = pl.BlockSpec((tb, C), lambda i: (i, 0))
    vec = pl.BlockSpec((1, C), lambda i: (0, 0))
    return pl.pallas_call(body, grid=(T // tb,), in_specs=[row, vec, vec], out_specs=row, out_shape=SDS((T, C), BF),
                          compiler_params=_params("parallel"), name=name)(cv, g.reshape(1, C), b.reshape(1, C))


def _ln_silu_bwd(cv, g, b, dy, name):
    T, C = cv.shape
    tb = _tile(T, ROW_BLOCK, 8)

    def body(x_ref, g_ref, b_ref, dy_ref, dx_ref, dg_ref, db_ref):
        _, vjp = jax.vjp(_ln_silu, x_ref[...], g_ref[...], b_ref[...])
        dx, dg, db = vjp(dy_ref[...].astype(F32))
        dx_ref[...] = dx

        @pl.when(pl.program_id(0) == 0)
        def _():
            dg_ref[...] = jnp.zeros_like(dg_ref)
            db_ref[...] = jnp.zeros_like(db_ref)

        dg_ref[...] += dg
        db_ref[...] += db

    row = pl.BlockSpec((tb, C), lambda i: (i, 0))
    vec = pl.BlockSpec((1, C), lambda i: (0, 0))
    return pl.pallas_call(
        body, grid=(T // tb,), in_specs=[row, vec, vec, row], out_specs=[row, vec, vec],
        out_shape=[SDS((T, C), F32), SDS((1, C), F32), SDS((1, C), F32)], compiler_params=_params("arbitrary"),
        name=name)(cv, g.reshape(1, C), b.reshape(1, C), dy)


def _merge_fwd(proj, yc, yp, ps, C, name):
    T, D = yc.shape
    tb = _tile(T, ROW_BLOCK, 8)
    nj = D // C

    def body(gc_ref, gp_ref, yc_ref, yp_ref, ps_ref, o_ref):
        o_ref[...] = _merge(gc_ref[...], gp_ref[...], yc_ref[...], yp_ref[...], ps_ref[...]).astype(BF)

    blk = pl.BlockSpec((tb, C), lambda i, j: (i, j))
    return pl.pallas_call(
        body, grid=(T // tb, nj),
        in_specs=[pl.BlockSpec((tb, C), lambda i, j: (i, 3 + j)), pl.BlockSpec((tb, C), lambda i, j: (i, 3 + nj + j)), blk, blk,
                  pl.BlockSpec((1, C), lambda i, j: (0, j))],
        out_specs=blk, out_shape=SDS((T, D), BF), compiler_params=_params("parallel", "parallel"), name=name)(proj, proj, yc, yp, ps.reshape(1, D))


def _merge_bwd(proj, yc, yp, ps, dm, C, name):
    T, D = yc.shape
    tb = _tile(T, ROW_BLOCK, 8)
    nj = D // C

    def body(gc_ref, gp_ref, yc_ref, yp_ref, ps_ref, dm_ref, dgc_ref, dgp_ref, dyc_ref, dyp_ref, dps_ref):
        _, vjp = jax.vjp(_merge, gc_ref[...], gp_ref[...], yc_ref[...], yp_ref[...], ps_ref[...])
        dgc, dgp, dyc, dyp, dps = vjp(dm_ref[...].astype(F32))
        dgc_ref[...] = dgc.astype(BF)
        dgp_ref[...] = dgp.astype(BF)
        dyc_ref[...] = dyc.astype(BF)
        dyp_ref[...] = dyp.astype(BF)

        @pl.when(pl.program_id(1) == 0)
        def _():
            dps_ref[...] = jnp.zeros_like(dps_ref)

        dps_ref[...] += dps

    blk = pl.BlockSpec((tb, C), lambda j, i: (i, j))
    vec = pl.BlockSpec((1, C), lambda j, i: (0, j))
    return pl.pallas_call(
        body, grid=(nj, T // tb),
        in_specs=[pl.BlockSpec((tb, C), lambda j, i: (i, 3 + j)), pl.BlockSpec((tb, C), lambda j, i: (i, 3 + nj + j)), blk, blk, vec, blk],
        out_specs=[blk, blk, blk, blk, vec], out_shape=[SDS((T, D), BF)] * 4 + [SDS((1, D), F32)],
        compiler_params=_params("parallel", "arbitrary"), name=name)(proj, proj, yc, yp, ps.reshape(1, D), dm)


def _shd(v, s, rows):
    if s == 0:
        return v
    return jnp.where(rows >= s, pltpu.roll(v, s, 0), 0.0)


def _shu(v, s, rows):
    if s == 0:
        return v
    n = v.shape[0]
    return jnp.where(rows < n - s, pltpu.roll(v, n - s, 0), 0.0)


def _glu_conv_fwd(proj, w, b, Bn, S, C, name):
    K = w.shape[0]
    sl = min(LANES, C)
    ns = C // sl

    def body(a_ref, gl_ref, w_ref, b_ref, o_ref):
        y0 = a_ref[...] * jax.nn.sigmoid(gl_ref[...])
        rows = lax.broadcasted_iota(jnp.int32, y0.shape, 0)
        acc = jnp.zeros_like(y0) + b_ref[...]
        for k in range(K):
            acc = acc + w_ref[k:k + 1, :] * _shd(y0, K - 1 - k, rows)
        o_ref[...] = acc

    return pl.pallas_call(
        body, grid=(Bn, ns),
        in_specs=[pl.BlockSpec((S, sl), lambda bi, j: (bi, j)), pl.BlockSpec((S, sl), lambda bi, j: (bi, ns + j)),
                  pl.BlockSpec((K, sl), lambda bi, j: (0, j)), pl.BlockSpec((1, sl), lambda bi, j: (0, j))],
        out_specs=pl.BlockSpec((S, sl), lambda bi, j: (bi, j)), out_shape=SDS((Bn * S, C), F32),
        compiler_params=_params("parallel", "parallel"), name=name)(proj, proj, w, b.reshape(1, C))


def _glu_conv_bwd(proj, w, dcv, Bn, S, C, name):
    K = w.shape[0]
    sl = min(LANES, C)
    ns = C // sl

    def body(a_ref, gl_ref, w_ref, d_ref, da_ref, dgl_ref, dw_ref, db_ref):
        a = a_ref[...]
        sg = jax.nn.sigmoid(gl_ref[...])
        y0 = a * sg
        d = d_ref[...]
        rows = lax.broadcasted_iota(jnp.int32, y0.shape, 0)

        @pl.when(pl.program_id(1) == 0)
        def _():
            dw_ref[...] = jnp.zeros_like(dw_ref)
            db_ref[...] = jnp.zeros_like(db_ref)

        dy0 = jnp.zeros_like(y0)
        for k in range(K):
            s = K - 1 - k
            dw_ref[k:k + 1, :] += jnp.sum(d * _shd(y0, s, rows), axis=0, keepdims=True)
            dy0 = dy0 + w_ref[k:k + 1, :] * _shu(d, s, rows)
        db_ref[...] += jnp.sum(d, axis=0, keepdims=True)
        da_ref[...] = (dy0 * sg).astype(BF)
        dgl_ref[...] = (dy0 * a * sg * (1.0 - sg)).astype(BF)

    blk = pl.BlockSpec((S, sl), lambda j, bi: (bi, j))
    return pl.pallas_call(
        body, grid=(ns, Bn),
        in_specs=[blk, pl.BlockSpec((S, sl), lambda j, bi: (bi, ns + j)), pl.BlockSpec((K, sl), lambda j, bi: (0, j)), blk],
        out_specs=[blk, blk, pl.BlockSpec((K, sl), lambda j, bi: (0, j)), pl.BlockSpec((1, sl), lambda j, bi: (0, j))],
        out_shape=[SDS((Bn * S, C), BF), SDS((Bn * S, C), BF), SDS((K, C), F32), SDS((1, C), F32)],
        compiler_params=_params("parallel", "arbitrary"), name=name)(proj, proj, w, dcv)


def _pool_z(u, g, rows):
    s2 = u + _shd(u, 1, rows)
    s4 = s2 + _shd(s2, 2, rows)
    s8 = s4 + _shd(s4, 4, rows)
    s16 = s8 + _shd(s8, 8, rows)
    sw = jnp.where(g == 0, s2, jnp.where(g == 1, s4, jnp.where(g == 2, s8, s16)))
    cnt = jnp.minimum(rows + 1, POOL_WINDOWS[0] << g).astype(F32)
    return sw / cnt - u, cnt


def _pool_fwd(proj, wpt, l, Bn, S, C, D, name):
    G = len(POOL_WINDOWS)
    gd, go = C // G, D // G

    def body(u_ref, w_ref, o_ref):
        g = pl.program_id(1)
        u = u_ref[...]
        rows = lax.broadcasted_iota(jnp.int32, u.shape, 0)
        zp, _ = _pool_z(u, g, rows)
        o_ref[...] = lax.dot_general(zp.astype(BF), w_ref[...], _DN["nt"], preferred_element_type=F32)

    return pl.pallas_call(
        body, grid=(Bn, G),
        in_specs=[pl.BlockSpec((S, gd), lambda bi, g: (bi, 2 * G + g)), pl.BlockSpec((None, go, gd), lambda bi, g: (l * G + g, 0, 0))],
        out_specs=pl.BlockSpec((S, go), lambda bi, g: (bi, g)), out_shape=SDS((Bn * S, D), F32),
        compiler_params=_params("parallel", "parallel"), name=name)(proj, wpt)


def _pool_bwd(proj, wpt, dyp, l, Bn, S, C, D, name):
    G = len(POOL_WINDOWS)
    gd, go = C // G, D // G

    def body(u_ref, w_ref, d_ref, du_ref, dw_ref):
        g = pl.program_id(0)
        u = u_ref[...]
        rows = lax.broadcasted_iota(jnp.int32, u.shape, 0)
        zp, cnt = _pool_z(u, g, rows)
        d = d_ref[...]
        dzp = lax.dot_general(d, w_ref[...], _DN["nn"], preferred_element_type=F32)

        @pl.when(pl.program_id(1) == 0)
        def _():
            dw_ref[...] = jnp.zeros_like(dw_ref)

        dw_ref[...] += lax.dot_general(d, zp.astype(BF), _DN["tn"], preferred_element_type=F32)
        dsw = dzp / cnt
        zero = jnp.zeros_like(dsw)
        d16 = jnp.where(g == 3, dsw, zero)
        d8 = jnp.where(g == 2, dsw, zero) + d16 + _shu(d16, 8, rows)
        d4 = jnp.where(g == 1, dsw, zero) + d8 + _shu(d8, 4, rows)
        d2 = jnp.where(g == 0, dsw, zero) + d4 + _shu(d4, 2, rows)
        d1 = d2 + _shu(d2, 1, rows)
        du_ref[...] = (d1 - dzp).astype(BF)

    return pl.pallas_call(
        body, grid=(G, Bn),
        in_specs=[pl.BlockSpec((S, gd), lambda g, bi: (bi, 2 * G + g)), pl.BlockSpec((None, go, gd), lambda g, bi: (l * G + g, 0, 0)),
                  pl.BlockSpec((S, go), lambda g, bi: (bi, g))],
        out_specs=[pl.BlockSpec((S, gd), lambda g, bi: (bi, g)), pl.BlockSpec((None, go, gd), lambda g, bi: (g, 0, 0))],
        out_shape=[SDS((Bn * S, C), BF), SDS((G, go, gd), F32)],
        compiler_params=_params("parallel", "arbitrary"), name=name)(proj, wpt, dyp)


def _ffn_conv(u, w_ref, rows):
    K = w_ref.shape[0]
    acc = w_ref[K - 1:K, :] * u
    for k in range(K - 1):
        acc = acc + w_ref[k:k + 1, :] * _shd(u, K - 1 - k, rows)
    return acc


def _ffn_cb(F):
    return _tile(F, 256)


def _ffn_act_fwd(up0, w, Bn, S, F, name):
    cb = _ffn_cb(F)
    nj = F // cb

    def body(g_ref, v_ref, wg_ref, wv_ref, o_ref):
        rows = lax.broadcasted_iota(jnp.int32, g_ref.shape, 0)
        o_ref[...] = _gated(_ffn_conv(g_ref[...], wg_ref, rows), _ffn_conv(v_ref[...], wv_ref, rows)).astype(BF)

    K = w.shape[0]
    return pl.pallas_call(
        body, grid=(Bn, nj),
        in_specs=[pl.BlockSpec((S, cb), lambda bi, j: (bi, j)), pl.BlockSpec((S, cb), lambda bi, j: (bi, nj + j)),
                  pl.BlockSpec((K, cb), lambda bi, j: (0, j)), pl.BlockSpec((K, cb), lambda bi, j: (0, nj + j))],
        out_specs=pl.BlockSpec((S, cb), lambda bi, j: (bi, j)), out_shape=SDS((Bn * S, F), BF),
        compiler_params=_params("parallel", "parallel"), name=name)(up0, up0, w, w)


def _ffn_act_bwd(up0, w, dg, Bn, S, F, name):
    cb = _ffn_cb(F)
    nj = F // cb
    K = w.shape[0]

    def body(g_ref, v_ref, wg_ref, wv_ref, d_ref, dgo_ref, dvo_ref, dwg_ref, dwv_ref):
        rows = lax.broadcasted_iota(jnp.int32, g_ref.shape, 0)
        g0, v0 = g_ref[...], v_ref[...]
        _, vjp = jax.vjp(_gated, _ffn_conv(g0, wg_ref, rows), _ffn_conv(v0, wv_ref, rows))
        dgc, dvc = vjp(d_ref[...].astype(F32))

        @pl.when(pl.program_id(1) == 0)
        def _():
            dwg_ref[...] = jnp.zeros_like(dwg_ref)
            dwv_ref[...] = jnp.zeros_like(dwv_ref)

        for u0, dc, w_ref, dw_ref, do_ref in ((g0, dgc, wg_ref, dwg_ref, dgo_ref), (v0, dvc, wv_ref, dwv_ref, dvo_ref)):
            du = jnp.zeros_like(u0)
            for k in range(K):
                s = K - 1 - k
                dw_ref[k:k + 1, :] += jnp.sum(dc * _shd(u0, s, rows), axis=0, keepdims=True)
                du = du + w_ref[k:k + 1, :] * _shu(dc, s, rows)
            do_ref[...] = du.astype(BF)

    blk = pl.BlockSpec((S, cb), lambda j, bi: (bi, j))
    wblk = pl.BlockSpec((K, cb), lambda j, bi: (0, j))
    return pl.pallas_call(
        body, grid=(nj, Bn),
        in_specs=[blk, pl.BlockSpec((S, cb), lambda j, bi: (bi, nj + j)), wblk, pl.BlockSpec((K, cb), lambda j, bi: (0, nj + j)), blk],
        out_specs=[blk, blk, wblk, wblk],
        out_shape=[SDS((Bn * S, F), BF), SDS((Bn * S, F), BF), SDS((K, F), F32), SDS((K, F), F32)],
        compiler_params=_params("parallel", "arbitrary"), name=name)(up0, up0, w, w, dg)


def _softmax_rows(q, k, scale):
    sc = lax.dot_general(q, k, _DN["nt"], preferred_element_type=F32) * scale
    e = jnp.exp(sc - jnp.max(sc, axis=-1, keepdims=True))
    return e / jnp.sum(e, axis=-1, keepdims=True)


def _attn_ts(S):
    return _tile(S, 1024, 8)


def _attn_fwd(q, kv, Bn, S, Mn, D, name):
    H = XA_HEADS
    dh = D // H
    ts = _attn_ts(S)
    nsb = S // ts
    scale = dh ** -0.5

    def body(q_ref, k_ref, v_ref, o_ref):
        p = _softmax_rows(q_ref[...], k_ref[...], scale)
        o_ref[...] = lax.dot_general(p.astype(BF), v_ref[...], _DN["nn"], preferred_element_type=F32).astype(BF)

    qblk = pl.BlockSpec((ts, dh), lambda bi, h, s: (bi * nsb + s, h))
    return pl.pallas_call(
        body, grid=(Bn, H, nsb),
        in_specs=[qblk, pl.BlockSpec((Mn, dh), lambda bi, h, s: (bi, h)), pl.BlockSpec((Mn, dh), lambda bi, h, s: (bi, H + h))],
        out_specs=qblk, out_shape=SDS((Bn * S, D), BF), compiler_params=_params("parallel", "parallel", "parallel"), name=name)(q, kv, kv)


def _attn_bwd(q, kv, datt, Bn, S, Mn, D, name):
    H = XA_HEADS
    dh = D // H
    ts = _attn_ts(S)
    nsb = S // ts
    scale = dh ** -0.5

    def body(q_ref, k_ref, v_ref, do_ref, dq_ref, dk_ref, dv_ref):
        q, k, v, do = q_ref[...], k_ref[...], v_ref[...], do_ref[...]
        p = _softmax_rows(q, k, scale)
        dp = lax.dot_general(do, v, _DN["nt"], preferred_element_type=F32)
        ds = (p * (dp - jnp.sum(dp * p, axis=-1, keepdims=True)) * scale).astype(BF)
        dq_ref[...] = lax.dot_general(ds, k, _DN["nn"], preferred_element_type=F32).astype(BF)

        @pl.when(pl.program_id(2) == 0)
        def _():
            dk_ref[...] = jnp.zeros_like(dk_ref)
            dv_ref[...] = jnp.zeros_like(dv_ref)

        dk_ref[...] += lax.dot_general(ds, q, _DN["tn"], preferred_element_type=F32)
        dv_ref[...] += lax.dot_general(p.astype(BF), do, _DN["tn"], preferred_element_type=F32)

    qblk = pl.BlockSpec((ts, dh), lambda bi, h, s: (bi * nsb + s, h))
    kblk = pl.BlockSpec((Mn, dh), lambda bi, h, s: (bi, h))
    return pl.pallas_call(
        body, grid=(Bn, H, nsb),
        in_specs=[qblk, kblk, pl.BlockSpec((Mn, dh), lambda bi, h, s: (bi, H + h)), qblk],
        out_specs=[qblk, kblk, kblk], out_shape=[SDS((Bn * S, D), BF), SDS((Bn * Mn, D), F32), SDS((Bn * Mn, D), F32)],
        compiler_params=_params("parallel", "parallel", "arbitrary"), name=name)(q, kv, kv, datt)


class _Sides:
    def __init__(self, by_key=None):
        self.by_key, self.landed = dict(by_key or {}), {}

    def mm(self, key, *args, **kw):
        side = self.by_key.get(key)
        if side is None:
            return _mm(*args, **kw)
        out, self.landed[key] = _mm(*args, side=side, **kw)
        return out


def _layer_fwd(x, mem_n, W, V, l, dims, sides):
    Bn, S, Mn, D, C, F = dims
    n = f"l{l}_"
    h = _rms_fwd(x, V["mix_norm_g"][l], n + "mix_norm")
    proj = sides.mm("proj", h, W["w_in"], "nn", F32, n + "proj", bl=0)
    cv = _glu_conv_fwd(proj, V["conv_dw_w"][l], V["conv_dw_b"][l], Bn, S, C, n + "glu_conv")
    yc1 = _ln_silu_fwd(cv, V["conv_ln_g"][l], V["conv_ln_b"][l], n + "ln_silu")
    yc = sides.mm("conv_out", yc1, W["w_conv_out"], "nn", F32, n + "conv_out", bl=0)
    yp = _pool_fwd(proj, W["w_pool"], 0, Bn, S, C, D, n + "pool")
    merged = _merge_fwd(proj, yc, yp, V["pool_scale"][l], C, n + "merge")
    x1 = sides.mm("out_proj", merged, W["w_out"], "nn", F32, n + "out_proj", res=x, bl=0)
    hq = _rms_fwd(x1, V["xattn_norm_g"][l], n + "xattn_norm")
    q = sides.mm("q_proj", hq, W["w_q"], "nn", BF, n + "q_proj", bl=0)
    kv = _mm(mem_n, W["w_kv"], "nn", BF, n + "kv_proj", bl=0)
    att = _attn_fwd(q, kv, Bn, S, Mn, D, n + "attn")
    x2 = sides.mm("o_proj", att, W["w_o"], "nn", F32, n + "o_proj", res=x1, bl=0)
    hf = _rms_fwd(x2, V["ffn_norm_g"][l], n + "ffn_norm")
    up0 = sides.mm("up_proj", hf, W["w_up"], "nn", F32, n + "up_proj", bl=0)
    gact = _ffn_act_fwd(up0, V["ffn_dw_w"][l], Bn, S, F, n + "ffn_act")
    x3 = sides.mm("down_proj", gact, W["w_down"], "nn", F32, n + "down_proj", res=x2, bl=0)
    return x3, dict(x=x, h=h, proj=proj, cv=cv, yc1=yc1, yc=yc, yp=yp, merged=merged, x1=x1, hq=hq, q=q, kv=kv, att=att, x2=x2, hf=hf,
                    up0=up0, gact=gact)


def _layer_bwd(dx, dxb, dmem_n, sv, mem_n, W, V, l, dims, sides):
    Bn, S, Mn, D, C, F = dims
    n = f"l{l}_b_"
    gw, sm = {}, {}
    dgact = sides.mm("d_gact", dxb, W["w_down"], "nt", BF, n + "d_gact", bl=0)
    gw["w_down"] = sides.mm("dw_down", sv["gact"], dxb, "tn", F32, n + "dw_down")
    dg0, dv0, dwg, dwv = _ffn_act_bwd(sv["up0"], V["ffn_dw_w"][l], dgact, Bn, S, F, n + "ffn_act")
    sm["ffn_dw_w"] = jnp.concatenate([dwg, dwv], axis=1)
    dup0 = jnp.concatenate([dg0, dv0], axis=1)
    dhf = sides.mm("d_hf", dup0, W["w_up"], "nt", F32, n + "d_hf", bl=0)
    gw["w_up"] = sides.mm("dw_up", sv["hf"], dup0, "tn", F32, n + "dw_up")
    dx2, dx2b, sm["ffn_norm_g"] = _rms_bwd(sv["x2"], V["ffn_norm_g"][l], dhf, dx, n + "ffn_norm")
    datt = _mm(dx2b, W["w_o"], "nt", BF, n + "d_att", bl=0)
    gw["w_o"] = _mm(sv["att"], dx2b, "tn", F32, n + "dw_o")
    dq, dk, dv = _attn_bwd(sv["q"], sv["kv"], datt, Bn, S, Mn, D, n + "attn")
    dkv = jnp.concatenate([dk, dv], axis=1)
    gw["w_kv"] = _mm(mem_n, dkv, "tn", F32, n + "dw_kv")
    dmem_n = _mm(dkv, W["w_kv"], "nt", F32, n + "d_mem", res=dmem_n, bl=0)
    dhq = _mm(dq, W["w_q"], "nt", F32, n + "d_hq", bl=0)
    gw["w_q"] = _mm(sv["hq"], dq, "tn", F32, n + "dw_q")
    dx1, dx1b, sm["xattn_norm_g"] = _rms_bwd(sv["x1"], V["xattn_norm_g"][l], dhq, dx2, n + "xattn_norm")
    dmerged = _mm(dx1b, W["w_out"], "nt", BF, n + "d_merged", bl=0)
    gw["w_out"] = _mm(sv["merged"], dx1b, "tn", F32, n + "dw_out")
    dgc, dgp, dyc, dyp, sm["pool_scale"] = _merge_bwd(sv["proj"], sv["yc"], sv["yp"], V["pool_scale"][l], dmerged, C, n + "merge")
    du, gw["w_pool"] = _pool_bwd(sv["proj"], W["w_pool"], dyp, 0, Bn, S, C, D, n + "pool")
    dyc1 = _mm(dyc, W["w_conv_out"], "nt", F32, n + "d_yc1", bl=0)
    gw["w_conv_out"] = _mm(sv["yc1"], dyc, "tn", F32, n + "dw_conv_out")
    dcv, sm["conv_ln_g"], sm["conv_ln_b"] = _ln_silu_bwd(sv["cv"], V["conv_ln_g"][l], V["conv_ln_b"][l], dyc1, n + "ln_silu")
    da, dgl, sm["conv_dw_w"], sm["conv_dw_b"] = _glu_conv_bwd(sv["proj"], V["conv_dw_w"][l], dcv, Bn, S, C, n + "glu_conv")
    dproj = jnp.concatenate([da, dgl, du, dgc, dgp], axis=1)
    dh = _mm(dproj, W["w_in"], "nt", F32, n + "d_h", bl=0)
    gw["w_in"] = _mm(sv["h"], dproj, "tn", F32, n + "dw_in")
    dx, dxb, sm["mix_norm_g"] = _rms_bwd(sv["x"], V["mix_norm_g"][l], dh, dx1, n + "mix_norm")
    return dx, dxb, dmem_n, gw, sm


BIG = (("w_in", "col"), ("w_conv_out", "col"), ("w_pool", "row"), ("w_out", "row"), ("w_q", "row"), ("w_kv", "col"),
       ("w_o", "row"), ("w_up", "col"), ("w_down", "row"))
FWD_CARRY = {"proj": ("w_in",), "conv_out": ("w_conv_out", "w_pool"), "out_proj": ("w_out", "w_q"), "q_proj": ("w_o",), "o_proj": ("w_kv",),
             "up_proj": ("w_up",), "down_proj": ("w_down",)}
BWD_CARRY = {"d_gact": ("w_down",), "dw_down": ("w_up",), "d_hf": ("w_in", "w_kv"), "dw_up": ("w_q", "w_o", "w_out", "w_conv_out", "w_pool")}


def _place():
    xi, yi, ci = lax.axis_index("x"), lax.axis_index("y"), lax.axis_index("c")
    return xi, yi, ci, 2 * xi + yi


def _chip_peer(xi, yi, ci, r):
    return (xi ^ (r >> 1), yi ^ (r & 1), ci)


def _full_shard(ref, kind, k, cs):
    if kind == "col":
        return ref.at[:, :, :, :, pl.ds(pl.multiple_of(k * cs, cs), cs)]
    return ref.at[:, :, k]


def _gather_weights(shards, kinds):
    n = len(shards)
    outs = []
    for s, kind in zip(shards, kinds):
        L, P, _, RH, CS = s.shape
        outs.append(SDS((L, P, 2, RH, CS * N_CHIPS) if kind == "col" else (L, P, N_CHIPS, 2, RH, CS), s.dtype))
    per = 7

    def body(*refs):
        srcs, fulls, (ssem, rsem) = refs[:n], refs[n:2 * n], refs[2 * n:]
        xi, yi, ci, j = _place()
        sib = (xi, yi, 1 - ci)

        def piece(i, k, c):
            kind, cs = kinds[i], shards[i].shape[-1]
            if kind == "col":
                return fulls[i].at[:, :, c, :, pl.ds(pl.multiple_of(k * cs, cs), cs)]
            return fulls[i].at[:, :, k, c]

        def copy(i, slot, src, dst, dev):
            return pltpu.make_async_remote_copy(src_ref=src, dst_ref=dst, send_sem=ssem.at[per * i + slot], recv_sem=rsem.at[per * i + slot],
                                                device_id=dev, device_id_type=MESH)

        own, first, passed = [], [], []
        for i in range(n):
            for r in (1, 2, 3):
                first.append(copy(i, r - 1, srcs[i].at[:, :, ci], piece(i, j, ci), _chip_peer(xi, yi, ci, r)))
                first[-1].start()
        for i in range(n):
            own.append(copy(i, 6, srcs[i], _full_shard(fulls[i], kinds[i], j, shards[i].shape[-1]), sib))
            own[-1].start()
        for i in range(n):
            for r in (1, 2, 3):
                got = piece(i, j ^ r, ci)
                copy(i, r - 1, got, got, sib).wait_recv()
                passed.append(copy(i, 2 + r, got, got, sib))
                passed[-1].start()
        for i in range(n):
            for r in (1, 2, 3):
                got = piece(i, j ^ r, 1 - ci)
                copy(i, 2 + r, got, got, sib).wait_recv()
        for cp in own:
            cp.wait()
        for cp in first + passed:
            cp.wait_send()

    return pl.pallas_call(
        body, in_specs=[ANY] * n, out_specs=[ANY] * n, out_shape=outs,
        scratch_shapes=[pltpu.SemaphoreType.DMA((per * n,)), pltpu.SemaphoreType.DMA((per * n,))], name="gather_weights")(*shards)


def _full_sds(s, kind):
    L, P, _, RH, CS = s.shape
    return SDS((L, P, 2, RH, CS * N_CHIPS) if kind == "col" else (L, P, N_CHIPS, 2, RH, CS), s.dtype)


def _gather_piece(full, kind, cs, k, c):
    if kind == "col":
        return full.at[:, :, c, :, pl.ds(pl.multiple_of(k * cs, cs), cs)]
    return full.at[:, :, k, c]


def _side_gather(shards, kinds):
    n = len(shards)

    def make(srcs, fulls, ssem, rsem):
        xi, yi, ci, j = _place()
        return [pltpu.make_async_remote_copy(
            src_ref=srcs[i].at[:, :, ci], dst_ref=_gather_piece(fulls[i], kinds[i], shards[i].shape[-1], j, ci), send_sem=ssem.at[3 * i + r - 1],
            recv_sem=rsem.at[3 * i + r - 1], device_id=_chip_peer(xi, yi, ci, r), device_id_type=MESH) for i in range(n) for r in (1, 2, 3)]

    return _Side(shards, [_full_sds(s, k) for s, k in zip(shards, kinds)], 3 * n, make)


def _gather_pass(fulls, shards, kinds, name):
    n = len(fulls)

    def body(*refs):
        srcs, outs, (ssem, rsem) = refs[n:2 * n], refs[2 * n:3 * n], refs[3 * n:]
        xi, yi, ci, j = _place()
        sib = (xi, yi, 1 - ci)
        cps = []
        for i in range(n):
            cs = shards[i].shape[-1]
            for r in (1, 2, 3):
                got = _gather_piece(outs[i], kinds[i], cs, j ^ r, ci)
                cps.append(pltpu.make_async_remote_copy(src_ref=got, dst_ref=got, send_sem=ssem.at[4 * i + r - 1], recv_sem=rsem.at[4 * i + r - 1],
                                                        device_id=sib, device_id_type=MESH))
            cps.append(pltpu.make_async_remote_copy(src_ref=srcs[i], dst_ref=_full_shard(outs[i], kinds[i], j, cs), send_sem=ssem.at[4 * i + 3],
                                                    recv_sem=rsem.at[4 * i + 3], device_id=sib, device_id_type=MESH))
        for cp in cps:
            cp.start()
        for cp in cps:
            cp.wait()

    return pl.pallas_call(
        body, in_specs=[ANY] * (2 * n), out_specs=[ANY] * n, out_shape=[SDS(f.shape, f.dtype) for f in fulls],
        input_output_aliases={i: i for i in range(n)},
        scratch_shapes=[pltpu.SemaphoreType.DMA((4 * n,)), pltpu.SemaphoreType.DMA((4 * n,))], name=name)(*fulls, *shards)


def _sibling_exchange(gviews, kinds, name):
    n = len(gviews)
    outs = [SDS(g.shape[:1] + g.shape[2:] if kind == "col" else g.shape[:2] + g.shape[3:], g.dtype) for g, kind in zip(gviews, kinds)]

    def body(*refs):
        gs, lands, (ssem, rsem) = refs[:n], refs[n:2 * n], refs[2 * n:]
        xi, yi, ci, _ = _place()
        cps = []
        for i in range(n):
            src = gs[i].at[:, 1 - ci] if kinds[i] == "col" else gs[i].at[:, :, 1 - ci]
            cps.append(pltpu.make_async_remote_copy(src_ref=src, dst_ref=lands[i], send_sem=ssem.at[i], recv_sem=rsem.at[i],
                                                    device_id=(xi, yi, 1 - ci), device_id_type=MESH))
            cps[-1].start()
        for cp in cps:
            cp.wait()

    return pl.pallas_call(body, in_specs=[ANY] * n, out_specs=[ANY] * n, out_shape=outs,
                          scratch_shapes=[pltpu.SemaphoreType.DMA((n,)), pltpu.SemaphoreType.DMA((n,))], name=name)(*gviews)


def _chip_sum(g, land, kind, jc, name):
    if kind == "col":
        P, _, RH, C = g.shape
        CS = C // N_CHIPS
        g_spec = pl.BlockSpec((None, None, RH, CS), lambda p, r, jc: (p, jc[1], 0, jc[0] ^ r))
        l_spec = pl.BlockSpec((None, RH, CS), lambda p, r, jc: (p, 0, jc[0] ^ r))
    else:
        P, _, _, RH, CS = g.shape
        g_spec = pl.BlockSpec((None, None, None, RH, CS), lambda p, r, jc: (p, jc[0] ^ r, jc[1], 0, 0))
        l_spec = pl.BlockSpec((None, None, RH, CS), lambda p, r, jc: (p, jc[0] ^ r, 0, 0))

    def body(jc_ref, g_ref, l_ref, own_ref, all_ref):
        s = g_ref[...] + l_ref[...]
        all_ref[...] = s.astype(BF)

        @pl.when(pl.program_id(1) == 0)
        def _():
            own_ref[...] = s

    return pl.pallas_call(
        body, grid_spec=pltpu.PrefetchScalarGridSpec(
            num_scalar_prefetch=1, grid=(P, N_CHIPS), in_specs=[g_spec, l_spec],
            out_specs=[pl.BlockSpec((None, RH, CS), lambda p, r, jc: (p, 0, 0)), pl.BlockSpec((None, None, RH, CS), lambda p, r, jc: (r, p, 0, 0))]),
        out_shape=[SDS((P, RH, CS), F32), SDS((N_CHIPS, P, RH, CS), BF)], compiler_params=_params("parallel", "arbitrary"), name=name)(jc, g, land)


def _chip_exchange_copies(srcs, lands, ssem, rsem):
    xi, yi, ci, _ = _place()
    return [pltpu.make_async_remote_copy(src_ref=srcs[i].at[r], dst_ref=lands[i].at[r], send_sem=ssem.at[3 * i + r - 1],
                                         recv_sem=rsem.at[3 * i + r - 1], device_id=_chip_peer(xi, yi, ci, r), device_id_type=MESH)
            for i in range(len(srcs)) for r in (1, 2, 3)]


def _side_chip_exchange(pieces):
    return _Side(pieces, [SDS(p.shape, p.dtype) for p in pieces], 3 * len(pieces), _chip_exchange_copies)


def _chip_exchange(pieces):
    n = len(pieces)

    def body(*refs):
        cps = _chip_exchange_copies(refs[:n], refs[n:2 * n], *refs[2 * n:])
        for cp in cps:
            cp.start()
        for cp in cps:
            cp.wait()

    return pl.pallas_call(body, in_specs=[ANY] * n, out_specs=[ANY] * n, out_shape=[SDS(p.shape, p.dtype) for p in pieces],
                          scratch_shapes=[pltpu.SemaphoreType.DMA((3 * n,)), pltpu.SemaphoreType.DMA((3 * n,))], name="grad_chip_exchange")(*pieces)


def _final_sum(own, land, jc, shard, l, L, name):
    P, RH, CS = own.shape

    def body(jc_ref, o_ref, a_ref, b_ref, c_ref, *rest):
        rest[-1][...] = ((o_ref[...] + a_ref[...].astype(F32)) + b_ref[...].astype(F32)) + c_ref[...].astype(F32)

    blk = pl.BlockSpec((None, RH, CS), lambda p, jc: (p, 0, 0))
    in_specs = [blk] + [pl.BlockSpec((None, None, RH, CS), functools.partial(lambda r, p, jc: (r, p, 0, 0), r)) for r in (1, 2, 3)]
    args = [jc, own, land, land, land]
    if shard is not None:
        in_specs.append(ANY)
        args.append(shard)
    return pl.pallas_call(
        body, grid_spec=pltpu.PrefetchScalarGridSpec(
            num_scalar_prefetch=1, grid=(P,), in_specs=in_specs,
            out_specs=pl.BlockSpec((None, None, None, RH, CS), lambda p, jc: (l, p, jc[1], 0, 0))),
        out_shape=SDS((L, P, 2, RH, CS), F32), input_output_aliases={5: 0} if shard is not None else {},
        compiler_params=_params("arbitrary"), name=name)(*args)


def _halves_exchange(shards):
    n = len(shards)

    def body(*refs):
        outs, (ssem, rsem) = refs[n:2 * n], refs[2 * n:]
        xi, yi, ci, _ = _place()
        cps = []
        for i in range(n):
            mine = outs[i].at[:, :, ci]
            cps.append(pltpu.make_async_remote_copy(src_ref=mine, dst_ref=mine, send_sem=ssem.at[i], recv_sem=rsem.at[i],
                                                    device_id=(xi, yi, 1 - ci), device_id_type=MESH))
            cps[-1].start()
        for i in range(n):
            land = outs[i].at[:, :, 1 - ci]
            pltpu.make_async_remote_copy(src_ref=land, dst_ref=land, send_sem=ssem.at[i], recv_sem=rsem.at[i],
                                         device_id=(xi, yi, 1 - ci), device_id_type=MESH).wait_recv()
        for cp in cps:
            cp.wait_send()

    return pl.pallas_call(body, in_specs=[ANY] * n, out_specs=[ANY] * n, out_shape=[SDS(s.shape, s.dtype) for s in shards],
                          input_output_aliases={i: i for i in range(n)},
                          scratch_shapes=[pltpu.SemaphoreType.DMA((n,)), pltpu.SemaphoreType.DMA((n,))], name="grad_halves_exchange")(*shards)


def _reduce_small(part):
    NR, Wd = part.shape
    ND = 2 * N_CHIPS

    def body(p_ref, o_ref, land, ssem, rsem):
        xi, yi, ci, j = _place()
        me = 2 * j + ci
        land[me] = p_ref[...]
        cps = []
        for rr in range(1, ND):
            dev = (xi ^ (rr >> 2), yi ^ ((rr >> 1) & 1), ci ^ (rr & 1))
            cps.append(pltpu.make_async_remote_copy(src_ref=p_ref, dst_ref=land.at[me], send_sem=ssem.at[rr - 1], recv_sem=rsem.at[rr - 1],
                                                    device_id=dev, device_id_type=MESH))
            cps[-1].start()
        for rr in range(1, ND):
            got = land.at[me ^ rr]
            pltpu.make_async_remote_copy(src_ref=got, dst_ref=got, send_sem=ssem.at[rr - 1], recv_sem=rsem.at[rr - 1],
                                         device_id=(xi, yi, ci), device_id_type=MESH).wait_recv()
        acc = land[0]
        for d in range(1, ND):
            acc = acc + land[d]
        o_ref[...] = acc
        for cp in cps:
            cp.wait_send()

    vm = pl.BlockSpec(memory_space=pltpu.VMEM)
    return pl.pallas_call(body, in_specs=[vm], out_specs=vm, out_shape=SDS((NR, Wd), F32),
                          scratch_shapes=[pltpu.VMEM((ND, NR, Wd), F32), pltpu.SemaphoreType.DMA((ND - 1,)), pltpu.SemaphoreType.DMA((ND - 1,))],
                          name="small_grad_allreduce")(part)


def _adamw(w, g, m, v, name):
    shape = w.shape
    C = shape[-1]
    R = w.size // C
    tb = _tile(R, max(8, (1 << 18) // C), 8)

    def body(w_ref, g_ref, m_ref, v_ref, d_ref, mo_ref, vo_ref):
        g = g_ref[...]
        m = ADAM_B1 * m_ref[...] + (1.0 - ADAM_B1) * g
        v = ADAM_B2 * v_ref[...] + (1.0 - ADAM_B2) * jnp.square(g)
        m_hat = m / (1.0 - ADAM_B1 ** ADAM_STEP)
        v_hat = v / (1.0 - ADAM_B2 ** ADAM_STEP)
        d_ref[...] = -ADAM_LR * (m_hat / (jnp.sqrt(v_hat) + ADAM_EPS) + ADAM_WD * w_ref[...])
        mo_ref[...] = m
        vo_ref[...] = v

    blk = pl.BlockSpec((tb, C), lambda i: (i, 0))
    outs = pl.pallas_call(body, grid=(R // tb,), in_specs=[blk] * 4, out_specs=[blk] * 3, out_shape=[SDS((R, C), F32)] * 3,
                          compiler_params=_params("parallel"), name=name)(*[t.reshape(R, C) for t in (w, g, m, v)])
    return [t.reshape(shape) for t in outs]


WEIGHTS = ("mix_norm_g", "w_in", "conv_dw_w", "conv_dw_b", "conv_ln_g", "conv_ln_b", "w_conv_out", "w_pool_grp", "pool_scale", "w_out",
           "xattn_norm_g", "mem_norm_g", "w_q", "w_kv", "w_o", "ffn_norm_g", "w_up", "ffn_dw_w", "w_down", "final_norm_g")
VECTORS = ("mix_norm_g", "conv_dw_b", "conv_ln_g", "conv_ln_b", "pool_scale", "xattn_norm_g", "mem_norm_g", "ffn_norm_g", "final_norm_g")


def _shard_view(t, kind):
    L, P, R, C = t.shape
    return t.reshape(L, P, 2, R // 2, C)


def _rows(t, width):
    return t.reshape(-1, width)


def _pack(parts):
    return jnp.concatenate([jnp.pad(p, ((0, (-p.shape[0]) % 8), (0, 0))) for p in parts], axis=0)


def kernel(x, mem, mix_norm_g, w_in, conv_dw_w, conv_dw_b, conv_ln_g, conv_ln_b, w_conv_out, w_pool_grp, pool_scale, w_out, xattn_norm_g, mem_norm_g, w_q, w_kv, w_o, ffn_norm_g, w_up, ffn_dw_w, w_down, final_norm_g, loss_target, m_mix_norm_g, m_w_in, m_conv_dw_w, m_conv_dw_b, m_conv_ln_g, m_conv_ln_b, m_w_conv_out, m_w_pool_grp, m_pool_scale, m_w_out, m_xattn_norm_g, m_mem_norm_g, m_w_q, m_w_kv, m_w_o, m_ffn_norm_g, m_w_up, m_ffn_dw_w, m_w_down, m_final_norm_g, v_mix_norm_g, v_w_in, v_conv_dw_w, v_conv_dw_b, v_conv_ln_g, v_conv_ln_b, v_w_conv_out, v_w_pool_grp, v_pool_scale, v_w_out, v_xattn_norm_g, v_mem_norm_g, v_w_q, v_w_kv, v_w_o, v_ffn_norm_g, v_w_up, v_ffn_dw_w, v_w_down, v_final_norm_g):
    w = dict(mix_norm_g=mix_norm_g, w_in=w_in, conv_dw_w=conv_dw_w, conv_dw_b=conv_dw_b, conv_ln_g=conv_ln_g, conv_ln_b=conv_ln_b,
             w_conv_out=w_conv_out, w_pool_grp=w_pool_grp, pool_scale=pool_scale, w_out=w_out, xattn_norm_g=xattn_norm_g,
             mem_norm_g=mem_norm_g, w_q=w_q, w_kv=w_kv, w_o=w_o, ffn_norm_g=ffn_norm_g, w_up=w_up, ffn_dw_w=ffn_dw_w, w_down=w_down,
             final_norm_g=final_norm_g)
    m = dict(zip(WEIGHTS, (m_mix_norm_g, m_w_in, m_conv_dw_w, m_conv_dw_b, m_conv_ln_g, m_conv_ln_b, m_w_conv_out, m_w_pool_grp, m_pool_scale,
                           m_w_out, m_xattn_norm_g, m_mem_norm_g, m_w_q, m_w_kv, m_w_o, m_ffn_norm_g, m_w_up, m_ffn_dw_w, m_w_down, m_final_norm_g)))
    v = dict(zip(WEIGHTS, (v_mix_norm_g, v_w_in, v_conv_dw_w, v_conv_dw_b, v_conv_ln_g, v_conv_ln_b, v_w_conv_out, v_w_pool_grp, v_pool_scale,
                           v_w_out, v_xattn_norm_g, v_mem_norm_g, v_w_q, v_w_kv, v_w_o, v_ffn_norm_g, v_w_up, v_ffn_dw_w, v_w_down, v_final_norm_g)))
    xi, yi, ci, j = _place()
    jc = jnp.stack([j, ci]).astype(jnp.int32)
    L = w_in.shape[0]
    G = len(POOL_WINDOWS)
    kinds = dict(BIG)

    def to_mat(name, t):
        if name == "w_pool":
            return jnp.swapaxes(t, 2, 3)
        return t[:, None]

    def from_mat(name, t):
        if name == "w_pool":
            return jnp.swapaxes(t, 2, 3)
        return t[:, 0]

    src = {name: w["w_pool_grp" if name == "w_pool" else name] for name, _ in BIG}

    KC, cs_c = conv_dw_w.shape[1], conv_dw_w.shape[2]
    KF, cs_f = ffn_dw_w.shape[1], ffn_dw_w.shape[2]
    taps = jnp.concatenate([conv_dw_w.reshape(L * KC, cs_c), ffn_dw_w.reshape(L * KF * (cs_f // cs_c), cs_c)], axis=0)
    n_taps = taps.shape[0]
    taps = jnp.pad(taps, ((0, (-n_taps) % 16), (0, 0)))
    names = [name for name, _ in BIG]
    mats = {name: to_mat(name, src[name]).astype(BF) for name in names}

    def layer_shards(l, subset):
        return [_shard_view(mats[name][l:l + 1], kinds[name]) for name in subset]

    def as_weights(subset, fulls):
        return {name: f.reshape(G if name == "w_pool" else 1, -1, f.shape[-1]) for name, f in zip(subset, fulls)}

    fulls = _gather_weights(layer_shards(0, names) + [_shard_view(taps[None, None], "row")], [kinds[name] for name in names] + ["row"])
    W = [as_weights(names, fulls[:-1])]
    taps_all = fulls[-1].reshape(N_CHIPS, -1, cs_c)[:, :n_taps]
    V = {name: w[name] for name in VECTORS}
    V["conv_dw_w"] = taps_all[:, :L * KC].reshape(N_CHIPS, L, KC, cs_c).transpose(1, 2, 0, 3).reshape(L, KC, N_CHIPS * cs_c)
    V["ffn_dw_w"] = taps_all[:, L * KC:].reshape(N_CHIPS, L, KF, cs_f).transpose(1, 2, 0, 3).reshape(L, KF, N_CHIPS * cs_f)

    Bn, S, D = x.shape
    Mn = mem.shape[1]
    dims = (Bn, S, Mn, D, conv_dw_b.shape[1], w_down.shape[1] * N_CHIPS)
    xt = x.reshape(Bn * S, D)
    memf = mem.reshape(Bn * Mn, D)
    mem_n = _rms_fwd(memf, V["mem_norm_g"], "mem_norm")
    saved = []
    for l in range(L):
        sides = _Sides()
        if l + 1 < L:
            sides = _Sides({key: _side_gather(layer_shards(l + 1, subset), [kinds[name] for name in subset]) for key, subset in FWD_CARRY.items()})
        xt, sv = _layer_fwd(xt, mem_n, W[l], V, l, dims, sides)
        saved.append(sv)
        if l + 1 < L:
            nxt = {}
            for key, subset in FWD_CARRY.items():
                done = _gather_pass(sides.landed[key], layer_shards(l + 1, subset), [kinds[name] for name in subset], f"gather_pass_l{l + 1}_{key}")
                nxt.update(as_weights(subset, done))
            W.append(nxt)
    loss, dx, dgf = _loss_bwd(xt, V["final_norm_g"], loss_target.reshape(Bn * S, D), "loss")
    loss = lax.psum(loss[0, 0], ("x", "y", "c"))

    def views(gw):
        out = []
        for name in names:
            g = gw[name] if gw[name].ndim == 3 else gw[name][None]
            P, R, C = g.shape
            out.append(g.reshape(P, 2, R // 2, C) if kinds[name] == "col" else g.reshape(P, N_CHIPS, 2, R // (2 * N_CHIPS), C))
        return out

    dxb, dmem_n = dx, None
    smalls, owns, got = [None] * L, [None] * L, [None] * L
    pieces = None
    for l in reversed(range(L)):
        sides = _Sides()
        if pieces is not None:
            sides = _Sides({key: _side_chip_exchange([pieces[names.index(name)] for name in subset]) for key, subset in BWD_CARRY.items()})
        dx, dxb, dmem_n, gw, smalls[l] = _layer_bwd(dx, dxb, dmem_n, saved[l], mem_n, W[l], V, l, dims, sides)
        if pieces is not None:
            got[l + 1] = [None] * len(names)
            for key, subset in BWD_CARRY.items():
                for name, land in zip(subset, sides.landed[key]):
                    got[l + 1][names.index(name)] = land
        gv = views(gw)
        lands = _sibling_exchange(gv, [kinds[name] for name in names], f"grad_sibling_exchange_l{l}")
        owns[l], pieces = [], []
        for name, g, land in zip(names, gv, lands):
            own, allp = _chip_sum(g, land, kinds[name], jc, f"chip_sum_{name}_{l}")
            owns[l].append(own)
            pieces.append(allp)
    got[0] = _chip_exchange(pieces)
    grad_x = dx.reshape(Bn, S, D)
    _, _, dgm = _rms_bwd(memf, V["mem_norm_g"], dmem_n, None, "mem_norm_b")
    small = {k: jnp.stack([sm[k] for sm in smalls]) if k in ("conv_dw_w", "ffn_dw_w") else jnp.concatenate([sm[k] for sm in smalls], axis=0)
             for k in smalls[0]}
    small["mem_norm_g"] = dgm
    small["final_norm_g"] = dgf
    mine = []
    for i, name in enumerate(names):
        shard = None
        for l in range(L):
            shard = _final_sum(owns[l][i], got[l][i], jc, shard, l, L, f"final_sum_{name}_{l}")
        mine.append(shard)
    gshards = _halves_exchange(mine)
    grads = {}
    for (name, kind), gs in zip(BIG, gshards):
        Lg, P, _, RH, CS = gs.shape
        grads["w_pool_grp" if name == "w_pool" else name] = from_mat(name, gs.reshape(Lg, P, 2 * RH, CS))

    small_w = conv_dw_b.shape[1]
    order = VECTORS + ("conv_dw_w", "ffn_dw_w")
    parts = [_rows(small[name], small_w) for name in order]
    counts = [p.shape[0] for p in parts]
    summed = _reduce_small(_pack(parts))
    off = 0
    for name, cnt in zip(order, counts):
        t = summed[off:off + cnt]
        off += cnt + (-cnt) % 8
        if name in VECTORS:
            grads[name] = t.reshape(w[name].shape)
        else:
            full = t.reshape(small[name].shape)
            cs = w[name].shape[2]
            grads[name] = lax.dynamic_slice_in_dim(full, j * cs, cs, axis=2)

    delta, new_m, new_v = {}, {}, {}
    for name, _ in BIG:
        key = "w_pool_grp" if name == "w_pool" else name
        outs = _adamw(*[to_mat(name, t) for t in (w[key], grads[key], m[key], v[key])], "adamw_" + name)
        delta[key], new_m[key], new_v[key] = [from_mat(name, t) for t in outs]
    vec = [_pack([_rows(d[name], small_w) for name in VECTORS]) for d in (w, grads, m, v)]
    outs = _adamw(*vec, "adamw_vectors")
    off = 0
    for name in VECTORS:
        cnt = w[name].size // small_w
        for d, t in zip((delta, new_m, new_v), outs):
            d[name] = t[off:off + cnt].reshape(w[name].shape)
        off += cnt + (-cnt) % 8
    for name in ("conv_dw_w", "ffn_dw_w"):
        delta[name], new_m[name], new_v[name] = _adamw(w[name], grads[name], m[name], v[name], "adamw_" + name)

    return (loss, grad_x, *[grads[k] for k in WEIGHTS], *[delta[k] for k in WEIGHTS], *[new_m[k] for k in WEIGHTS], *[new_v[k] for k in WEIGHTS])
```

```python
import functools

import jax
import jax.numpy as jnp
from jax import lax
from jax.experimental import pallas as pl
from jax.experimental.pallas import tpu as pltpu

F32 = jnp.float32
BF = jnp.bfloat16
SDS = jax.ShapeDtypeStruct
MESH = pl.DeviceIdType.MESH
ANY = pl.BlockSpec(memory_space=pl.ANY)

EPS = 1e-6
XA_HEADS = 4
POOL_WINDOWS = (2, 4, 8, 16)
N_CHIPS = 4
ADAM_LR, ADAM_B1, ADAM_B2, ADAM_EPS, ADAM_WD, ADAM_STEP = 0.001, 0.9, 0.999, 1e-08, 0.01, 10

LANES = 128
ROW_BLOCK = 512
VMEM_LIMIT = 56 * 1024 * 1024


def _params(*sem):
    return pltpu.CompilerParams(dimension_semantics=sem if sem else None, vmem_limit_bytes=VMEM_LIMIT)


def _tile(n, cap, mult=LANES):
    if n <= cap:
        return n
    for t in range(cap - cap % mult, 0, -mult):
        if n % t == 0:
            return t
    return n


_DN = {"nn": (((1,), (0,)), ((), ())), "nt": (((1,), (1,)), ((), ())), "tn": (((0,), (0,)), ((), ()))}


class _Side:
    def __init__(self, ins, outs, n, make):
        self.ins, self.outs, self.n, self.make = list(ins), list(outs), n, make


MM_TILE_CAP = 1408


def _mm(a, b, dims, out_dtype, name, res=None, bl=None, tm=MM_TILE_CAP, tn=MM_TILE_CAP, tk=MM_TILE_CAP, side=None):
    bs = b.shape[1:] if bl is not None else b.shape
    if dims == "nn":
        (M, K), (K2, N) = a.shape, bs
    elif dims == "nt":
        (M, K), (N, K2) = a.shape, bs
    else:
        (K, M), (K2, N) = a.shape, bs
    assert K == K2, (name, a.shape, b.shape)
    tm, tn, tk = _tile(M, tm), _tile(N, tn), _tile(K, tk)
    nk = K // tk
    lead = (None,) if bl is not None else ()
    pre = (lambda *ix: (bl,) + ix) if bl is not None else (lambda *ix: ix)
    if dims == "tn":
        a_spec = pl.BlockSpec((tk, tm), lambda i, j, k: (k, i))
    else:
        a_spec = pl.BlockSpec((tm, tk), lambda i, j, k: (i, k))
    if dims == "nt":
        b_spec = pl.BlockSpec(lead + (tn, tk), lambda i, j, k: pre(j, k))
    else:
        b_spec = pl.BlockSpec(lead + (tk, tn), lambda i, j, k: pre(k, j))
    o_spec = pl.BlockSpec((tm, tn), lambda i, j, k: (i, j))
    in_specs, args = [a_spec, b_spec], [a, b]
    if res is not None:
        in_specs.append(o_spec)
        args.append(res)
    n_main = len(args)
    n_si, n_so = (len(side.ins), len(side.outs)) if side is not None else (0, 0)
    gi, gj = M // tm, N // tn

    def body(*refs):
        a_ref, b_ref = refs[0], refs[1]
        r_ref = refs[2] if res is not None else None
        o_ref = refs[n_main + n_si]
        scratch = refs[n_main + n_si + 1 + n_so:]
        i, j, k = pl.program_id(0), pl.program_id(1), pl.program_id(2)
        if side is not None:
            copies = side.make(refs[n_main:n_main + n_si], refs[n_main + n_si + 1:n_main + n_si + 1 + n_so], scratch[-2], scratch[-1])

            @pl.when((i == 0) & (j == 0) & (k == 0))
            def _():
                for cp in copies:
                    cp.start()

        p = lax.dot_general(a_ref[...].astype(BF), b_ref[...].astype(BF), _DN[dims], preferred_element_type=F32)

        def finish(t):
            if r_ref is not None:
                t = t + r_ref[...]
            o_ref[...] = t.astype(out_dtype)

        if nk == 1:
            finish(p)
        else:
            acc = scratch[0]

            @pl.when(k == 0)
            def _():
                acc[...] = p

            @pl.when(k > 0)
            def _():
                acc[...] += p

            @pl.when(k == nk - 1)
            def _():
                finish(acc[...])

        if side is not None:
            @pl.when((i == gi - 1) & (j == gj - 1) & (k == nk - 1))
            def _():
                for cp in copies:
                    cp.wait()

    scratch_shapes = [pltpu.VMEM((tm, tn), F32)] if nk > 1 else []
    if side is None:
        return pl.pallas_call(
            body, grid=(gi, gj, nk), in_specs=in_specs, out_specs=o_spec, out_shape=SDS((M, N), out_dtype), scratch_shapes=scratch_shapes,
            compiler_params=_params("parallel", "parallel", "arbitrary"), name=name)(*args)
    outs = pl.pallas_call(
        body, grid=(gi, gj, nk), in_specs=in_specs + [ANY] * n_si, out_specs=[o_spec] + [ANY] * n_so,
        out_shape=[SDS((M, N), out_dtype)] + side.outs,
        scratch_shapes=scratch_shapes + [pltpu.SemaphoreType.DMA((side.n,)), pltpu.SemaphoreType.DMA((side.n,))],
        compiler_params=_params("arbitrary", "arbitrary", "arbitrary"), name=name)(*args, *side.ins)
    return outs[0], list(outs[1:])


def _rms(x, g):
    return x * lax.rsqrt(jnp.mean(x * x, axis=-1, keepdims=True) + EPS) * g


def _ln_silu(x, g, b):
    mu = jnp.mean(x, axis=-1, keepdims=True)
    xc = x - mu
    var = jnp.mean(xc * xc, axis=-1, keepdims=True)
    return jax.nn.silu(xc * lax.rsqrt(var + EPS) * g + b)


def _merge(gc, gp, yc, yp, ps):
    return jax.nn.sigmoid(gc) * yc + jax.nn.sigmoid(gp) * (yp * ps)


def _gated(gate, val):
    return jax.nn.gelu(gate) * val


def _rms_fwd(x, g, name):
    T, D = x.shape
    tb = _tile(T, ROW_BLOCK, 8)

    def body(x_ref, g_ref, o_ref):
        o_ref[...] = _rms(x_ref[...], g_ref[...]).astype(BF)

    row = pl.BlockSpec((tb, D), lambda i: (i, 0))
    return pl.pallas_call(body, grid=(T // tb,), in_specs=[row, pl.BlockSpec((1, D), lambda i: (0, 0))], out_specs=row,
                          out_shape=SDS((T, D), BF), compiler_params=_params("parallel"), name=name)(x, g.reshape(1, D))


def _rms_bwd(x, g, dh, dres, name):
    T, D = x.shape
    tb = _tile(T, ROW_BLOCK, 8)

    def body(*refs):
        if dres is not None:
            x_ref, g_ref, dh_ref, dres_ref, dx_ref, dxb_ref, dg_ref = refs
        else:
            x_ref, g_ref, dh_ref, dx_ref, dxb_ref, dg_ref = refs
        _, vjp = jax.vjp(_rms, x_ref[...], g_ref[...])
        dx, dg = vjp(dh_ref[...].astype(F32))
        if dres is not None:
            dx = dx + dres_ref[...]
        dx_ref[...] = dx
        dxb_ref[...] = dx.astype(BF)

        @pl.when(pl.program_id(0) == 0)
        def _():
            dg_ref[...] = jnp.zeros_like(dg_ref)

        dg_ref[...] += dg

    row = pl.BlockSpec((tb, D), lambda i: (i, 0))
    vec = pl.BlockSpec((1, D), lambda i: (0, 0))
    ins = [x, g.reshape(1, D), dh] + ([dres] if dres is not None else [])
    return pl.pallas_call(
        body, grid=(T // tb,), in_specs=[row, vec, row] + ([row] if dres is not None else []), out_specs=[row, row, vec],
        out_shape=[SDS((T, D), F32), SDS((T, D), BF), SDS((1, D), F32)], compiler_params=_params("arbitrary"), name=name)(*ins)


def _loss_bwd(x, g, target, name):
    T, D = x.shape
    tb = _tile(T, ROW_BLOCK, 8)
    nb = T // tb

    def body(x_ref, g_ref, t_ref, loss_ref, dx_ref, dg_ref, acc):
        i = pl.program_id(0)
        y, vjp = jax.vjp(_rms, x_ref[...], g_ref[...])
        err = y - t_ref[...]
        dx, dg = vjp(err * (1.0 / D))
        dx_ref[...] = dx

        @pl.when(i == 0)
        def _():
            dg_ref[...] = jnp.zeros_like(dg_ref)
            acc[...] = jnp.zeros_like(acc)

        dg_ref[...] += dg
        acc[...] += jnp.sum(err * err, axis=0, keepdims=True)

        @pl.when(i == nb - 1)
        def _():
            loss_ref[...] = jnp.full(loss_ref.shape, (0.5 / D) * jnp.sum(acc[...]), F32)

    row = pl.BlockSpec((tb, D), lambda i: (i, 0))
    vec = pl.BlockSpec((1, D), lambda i: (0, 0))
    return pl.pallas_call(
        body, grid=(nb,), in_specs=[row, vec, row], out_specs=[pl.BlockSpec((1, LANES), lambda i: (0, 0)), row, vec],
        out_shape=[SDS((1, LANES), F32), SDS((T, D), F32), SDS((1, D), F32)], scratch_shapes=[pltpu.VMEM((1, D), F32)],
        compiler_params=_params("arbitrary"), name=name)(x, g.reshape(1, D), target)


def _ln_silu_fwd(cv, g, b, name):
    T, C = cv.shape
    tb = _tile(T, ROW_BLOCK, 8)

    def body(x_ref, g_ref, b_ref, o_ref):
        o_ref[...] = _ln_silu(x_ref[...], g_ref[...], b_ref[...]).astype(BF)

    row = pl.BlockSpec((tb, C), lambda i: (i, 0))
    vec = pl.BlockSpec((1, C), lambda i: (0, 0))
    return pl.pallas_call(body, grid=(T // tb,), in_specs=[row, vec, vec], out_specs=row, out_shape=SDS((T, C), BF),
                          compiler_params=_params("parallel"), name=name)(cv, g.reshape(1, C), b.reshape(1, C))


def _ln_silu_bwd(cv, g, b, dy, name):
    T, C = cv.shape
    tb = _tile(T, ROW_BLOCK, 8)

    def body(x_ref, g_ref, b_ref, dy_ref, dx_ref, dg_ref, db_ref):
        _, vjp = jax.vjp(_ln_silu, x_ref[...], g_ref[...], b_ref[...])
        dx, dg, db = vjp(dy_ref[...].astype(F32))
        dx_ref[...] = dx

        @pl.when(pl.program_id(0) == 0)
        def _():
            dg_ref[...] = jnp.zeros_like(dg_ref)
            db_ref[...] = jnp.zeros_like(db_ref)

        dg_ref[...] += dg
        db_ref[...] += db

    row = pl.BlockSpec((tb, C), lambda i: (i, 0))
    vec = pl.BlockSpec((1, C), lambda i: (0, 0))
    return pl.pallas_call(
        body, grid=(T // tb,), in_specs=[row, vec, vec, row], out_specs=[row, vec, vec],
        out_shape=[SDS((T, C), F32), SDS((1, C), F32), SDS((1, C), F32)], compiler_params=_params("arbitrary"),
        name=name)(cv, g.reshape(1, C), b.reshape(1, C), dy)


def _merge_fwd(proj, yc, yp, ps, C, name):
    T, D = yc.shape
    tb = _tile(T, ROW_BLOCK, 8)
    nj = D // C

    def body(gc_ref, gp_ref, yc_ref, yp_ref, ps_ref, o_ref):
        o_ref[...] = _merge(gc_ref[...], gp_ref[...], yc_ref[...], yp_ref[...], ps_ref[...]).astype(BF)

    blk = pl.BlockSpec((tb, C), lambda i, j: (i, j))
    return pl.pallas_call(
        body, grid=(T // tb, nj),
        in_specs=[pl.BlockSpec((tb, C), lambda i, j: (i, 3 + j)), pl.BlockSpec((tb, C), lambda i, j: (i, 3 + nj + j)), blk, blk,
                  pl.BlockSpec((1, C), lambda i, j: (0, j))],
        out_specs=blk, out_shape=SDS((T, D), BF), compiler_params=_params("parallel", "parallel"), name=name)(proj, proj, yc, yp, ps.reshape(1, D))


def _merge_bwd(proj, yc, yp, ps, dm, C, name):
    T, D = yc.shape
    tb = _tile(T, ROW_BLOCK, 8)
    nj = D // C

    def body(gc_ref, gp_ref, yc_ref, yp_ref, ps_ref, dm_ref, dgc_ref, dgp_ref, dyc_ref, dyp_ref, dps_ref):
        _, vjp = jax.vjp(_merge, gc_ref[...], gp_ref[...], yc_ref[...], yp_ref[...], ps_ref[...])
        dgc, dgp, dyc, dyp, dps = vjp(dm_ref[...].astype(F32))
        dgc_ref[...] = dgc.astype(BF)
        dgp_ref[...] = dgp.astype(BF)
        dyc_ref[...] = dyc.astype(BF)
        dyp_ref[...] = dyp.astype(BF)

        @pl.when(pl.program_id(1) == 0)
        def _():
            dps_ref[...] = jnp.zeros_like(dps_ref)

        dps_ref[...] += dps

    blk = pl.BlockSpec((tb, C), lambda j, i: (i, j))
    vec = pl.BlockSpec((1, C), lambda j, i: (0, j))
    return pl.pallas_call(
        body, grid=(nj, T // tb),
        in_specs=[pl.BlockSpec((tb, C), lambda j, i: (i, 3 + j)), pl.BlockSpec((tb, C), lambda j, i: (i, 3 + nj + j)), blk, blk, vec, blk],
        out_specs=[blk, blk, blk, blk, vec], out_shape=[SDS((T, D), BF)] * 4 + [SDS((1, D), F32)],
        compiler_params=_params("parallel", "arbitrary"), name=name)(proj, proj, yc, yp, ps.reshape(1, D), dm)


def _shd(v, s, rows):
    if s == 0:
        return v
    return jnp.where(rows >= s, pltpu.roll(v, s, 0), 0.0)


def _shu(v, s, rows):
    if s == 0:
        return v
    n = v.shape[0]
    return jnp.where(rows < n - s, pltpu.roll(v, n - s, 0), 0.0)


def _glu_conv_fwd(proj, w, b, Bn, S, C, name):
    K = w.shape[0]
    sl = min(LANES, C)
    ns = C // sl

    def body(a_ref, gl_ref, w_ref, b_ref, o_ref):
        y0 = a_ref[...] * jax.nn.sigmoid(gl_ref[...])
        rows = lax.broadcasted_iota(jnp.int32, y0.shape, 0)
        acc = jnp.zeros_like(y0) + b_ref[...]
        for k in range(K):
            acc = acc + w_ref[k:k + 1, :] * _shd(y0, K - 1 - k, rows)
        o_ref[...] = acc

    return pl.pallas_call(
        body, grid=(Bn, ns),
        in_specs=[pl.BlockSpec((S, sl), lambda bi, j: (bi, j)), pl.BlockSpec((S, sl), lambda bi, j: (bi, ns + j)),
                  pl.BlockSpec((K, sl), lambda bi, j: (0, j)), pl.BlockSpec((1, sl), lambda bi, j: (0, j))],
        out_specs=pl.BlockSpec((S, sl), lambda bi, j: (bi, j)), out_shape=SDS((Bn * S, C), F32),
        compiler_params=_params("parallel", "parallel"), name=name)(proj, proj, w, b.reshape(1, C))


def _glu_conv_bwd(proj, w, dcv, Bn, S, C, name):
    K = w.shape[0]
    sl = min(LANES, C)
    ns = C // sl

    def body(a_ref, gl_ref, w_ref, d_ref, da_ref, dgl_ref, dw_ref, db_ref):
        a = a_ref[...]
        sg = jax.nn.sigmoid(gl_ref[...])
        y0 = a * sg
        d = d_ref[...]
        rows = lax.broadcasted_iota(jnp.int32, y0.shape, 0)

        @pl.when(pl.program_id(1) == 0)
        def _():
            dw_ref[...] = jnp.zeros_like(dw_ref)
            db_ref[...] = jnp.zeros_like(db_ref)

        dy0 = jnp.zeros_like(y0)
        for k in range(K):
            s = K - 1 - k
            dw_ref[k:k + 1, :] += jnp.sum(d * _shd(y0, s, rows), axis=0, keepdims=True)
            dy0 = dy0 + w_ref[k:k + 1, :] * _shu(d, s, rows)
        db_ref[...] += jnp.sum(d, axis=0, keepdims=True)
        da_ref[...] = (dy0 * sg).astype(BF)
        dgl_ref[...] = (dy0 * a * sg * (1.0 - sg)).astype(BF)

    blk = pl.BlockSpec((S, sl), lambda j, bi: (bi, j))
    return pl.pallas_call(
        body, grid=(ns, Bn),
        in_specs=[blk, pl.BlockSpec((S, sl), lambda j, bi: (bi, ns + j)), pl.BlockSpec((K, sl), lambda j, bi: (0, j)), blk],
        out_specs=[blk, blk, pl.BlockSpec((K, sl), lambda j, bi: (0, j)), pl.BlockSpec((1, sl), lambda j, bi: (0, j))],
        out_shape=[SDS((Bn * S, C), BF), SDS((Bn * S, C), BF), SDS((K, C), F32), SDS((1, C), F32)],
        compiler_params=_params("parallel", "arbitrary"), name=name)(proj, proj, w, dcv)


def _pool_z(u, g, rows):
    s2 = u + _shd(u, 1, rows)
    s4 = s2 + _shd(s2, 2, rows)
    s8 = s4 + _shd(s4, 4, rows)
    s16 = s8 + _shd(s8, 8, rows)
    sw = jnp.where(g == 0, s2, jnp.where(g == 1, s4, jnp.where(g == 2, s8, s16)))
    cnt = jnp.minimum(rows + 1, POOL_WINDOWS[0] << g).astype(F32)
    return sw / cnt - u, cnt


def _pool_fwd(proj, wpt, l, Bn, S, C, D, name):
    G = len(POOL_WINDOWS)
    gd, go = C // G, D // G

    def body(u_ref, w_ref, o_ref):
        g = pl.program_id(1)
        u = u_ref[...]
        rows = lax.broadcasted_iota(jnp.int32, u.shape, 0)
        zp, _ = _pool_z(u, g, rows)
        o_ref[...] = lax.dot_general(zp.astype(BF), w_ref[...], _DN["nt"], preferred_element_type=F32)

    return pl.pallas_call(
        body, grid=(Bn, G),
        in_specs=[pl.BlockSpec((S, gd), lambda bi, g: (bi, 2 * G + g)), pl.BlockSpec((None, go, gd), lambda bi, g: (l * G + g, 0, 0))],
        out_specs=pl.BlockSpec((S, go), lambda bi, g: (bi, g)), out_shape=SDS((Bn * S, D), F32),
        compiler_params=_params("parallel", "parallel"), name=name)(proj, wpt)


def _pool_bwd(proj, wpt, dyp, l, Bn, S, C, D, name):
    G = len(POOL_WINDOWS)
    gd, go = C // G, D // G

    def body(u_ref, w_ref, d_ref, du_ref, dw_ref):
        g = pl.program_id(0)
        u = u_ref[...]
        rows = lax.broadcasted_iota(jnp.int32, u.shape, 0)
        zp, cnt = _pool_z(u, g, rows)
        d = d_ref[...]
        dzp = lax.dot_general(d, w_ref[...], _DN["nn"], preferred_element_type=F32)

        @pl.when(pl.program_id(1) == 0)
        def _():
            dw_ref[...] = jnp.zeros_like(dw_ref)

        dw_ref[...] += lax.dot_general(d, zp.astype(BF), _DN["tn"], preferred_element_type=F32)
        dsw = dzp / cnt
        zero = jnp.zeros_like(dsw)
        d16 = jnp.where(g == 3, dsw, zero)
        d8 = jnp.where(g == 2, dsw, zero) + d16 + _shu(d16, 8, rows)
        d4 = jnp.where(g == 1, dsw, zero) + d8 + _shu(d8, 4, rows)
        d2 = jnp.where(g == 0, dsw, zero) + d4 + _shu(d4, 2, rows)
        d1 = d2 + _shu(d2, 1, rows)
        du_ref[...] = (d1 - dzp).astype(BF)

    return pl.pallas_call(
        body, grid=(G, Bn),
        in_specs=[pl.BlockSpec((S, gd), lambda g, bi: (bi, 2 * G + g)), pl.BlockSpec((None, go, gd), lambda g, bi: (l * G + g, 0, 0)),
                  pl.BlockSpec((S, go), lambda g, bi: (bi, g))],
        out_specs=[pl.BlockSpec((S, gd), lambda g, bi: (bi, g)), pl.BlockSpec((None, go, gd), lambda g, bi: (g, 0, 0))],
        out_shape=[SDS((Bn * S, C), BF), SDS((G, go, gd), F32)],
        compiler_params=_params("parallel", "arbitrary"), name=name)(proj, wpt, dyp)


def _ffn_conv(u, w_ref, rows):
    K = w_ref.shape[0]
    acc = w_ref[K - 1:K, :] * u
    for k in range(K - 1):
        acc = acc + w_ref[k:k + 1, :] * _shd(u, K - 1 - k, rows)
    return acc


def _ffn_cb(F):
    return _tile(F, 256)


def _ffn_act_fwd(up0, w, Bn, S, F, name):
    cb = _ffn_cb(F)
    nj = F // cb

    def body(g_ref, v_ref, wg_ref, wv_ref, o_ref):
        rows = lax.broadcasted_iota(jnp.int32, g_ref.shape, 0)
        o_ref[...] = _gated(_ffn_conv(g_ref[...], wg_ref, rows), _ffn_conv(v_ref[...], wv_ref, rows)).astype(BF)

    K = w.shape[0]
    return pl.pallas_call(
        body, grid=(Bn, nj),
        in_specs=[pl.BlockSpec((S, cb), lambda bi, j: (bi, j)), pl.BlockSpec((S, cb), lambda bi, j: (bi, nj + j)),
                  pl.BlockSpec((K, cb), lambda bi, j: (0, j)), pl.BlockSpec((K, cb), lambda bi, j: (0, nj + j))],
        out_specs=pl.BlockSpec((S, cb), lambda bi, j: (bi, j)), out_shape=SDS((Bn * S, F), BF),
        compiler_params=_params("parallel", "parallel"), name=name)(up0, up0, w, w)


def _ffn_act_bwd(up0, w, dg, Bn, S, F, name):
    cb = _ffn_cb(F)
    nj = F // cb
    K = w.shape[0]

    def body(g_ref, v_ref, wg_ref, wv_ref, d_ref, dgo_ref, dvo_ref, dwg_ref, dwv_ref):
        rows = lax.broadcasted_iota(jnp.int32, g_ref.shape, 0)
        g0, v0 = g_ref[...], v_ref[...]
        _, vjp = jax.vjp(_gated, _ffn_conv(g0, wg_ref, rows), _ffn_conv(v0, wv_ref, rows))
        dgc, dvc = vjp(d_ref[...].astype(F32))

        @pl.when(pl.program_id(1) == 0)
        def _():
            dwg_ref[...] = jnp.zeros_like(dwg_ref)
            dwv_ref[...] = jnp.zeros_like(dwv_ref)

        for u0, dc, w_ref, dw_ref, do_ref in ((g0, dgc, wg_ref, dwg_ref, dgo_ref), (v0, dvc, wv_ref, dwv_ref, dvo_ref)):
            du = jnp.zeros_like(u0)
            for k in range(K):
                s = K - 1 - k
                dw_ref[k:k + 1, :] += jnp.sum(dc * _shd(u0, s, rows), axis=0, keepdims=True)
                du = du + w_ref[k:k + 1, :] * _shu(dc, s, rows)
            do_ref[...] = du.astype(BF)

    blk = pl.BlockSpec((S, cb), lambda j, bi: (bi, j))
    wblk = pl.BlockSpec((K, cb), lambda j, bi: (0, j))
    return pl.pallas_call(
        body, grid=(nj, Bn),
        in_specs=[blk, pl.BlockSpec((S, cb), lambda j, bi: (bi, nj + j)), wblk, pl.BlockSpec((K, cb), lambda j, bi: (0, nj + j)), blk],
        out_specs=[blk, blk, wblk, wblk],
        out_shape=[SDS((Bn * S, F), BF), SDS((Bn * S, F), BF), SDS((K, F), F32), SDS((K, F), F32)],
        compiler_params=_params("parallel", "arbitrary"), name=name)(up0, up0, w, w, dg)


def _softmax_rows(q, k, scale):
    sc = lax.dot_general(q, k, _DN["nt"], preferred_element_type=F32) * scale
    e = jnp.exp(sc - jnp.max(sc, axis=-1, keepdims=True))
    return e / jnp.sum(e, axis=-1, keepdims=True)


def _attn_ts(S):
    return _tile(S, 1024, 8)


def _attn_fwd(q, kv, Bn, S, Mn, D, name):
    H = XA_HEADS
    dh = D // H
    ts = _attn_ts(S)
    nsb = S // ts
    scale = dh ** -0.5

    def body(q_ref, k_ref, v_ref, o_ref):
        p = _softmax_rows(q_ref[...], k_ref[...], scale)
        o_ref[...] = lax.dot_general(p.astype(BF), v_ref[...], _DN["nn"], preferred_element_type=F32).astype(BF)

    qblk = pl.BlockSpec((ts, dh), lambda bi, h, s: (bi * nsb + s, h))
    return pl.pallas_call(
        body, grid=(Bn, H, nsb),
        in_specs=[qblk, pl.BlockSpec((Mn, dh), lambda bi, h, s: (bi, h)), pl.BlockSpec((Mn, dh), lambda bi, h, s: (bi, H + h))],
        out_specs=qblk, out_shape=SDS((Bn * S, D), BF), compiler_params=_params("parallel", "parallel", "parallel"), name=name)(q, kv, kv)


def _attn_bwd(q, kv, datt, Bn, S, Mn, D, name):
    H = XA_HEADS
    dh = D // H
    ts = _attn_ts(S)
    nsb = S // ts
    scale = dh ** -0.5

    def body(q_ref, k_ref, v_ref, do_ref, dq_ref, dk_ref, dv_ref):
        q, k, v, do = q_ref[...], k_ref[...], v_ref[...], do_ref[...]
        p = _softmax_rows(q, k, scale)
        dp = lax.dot_general(do, v, _DN["nt"], preferred_element_type=F32)
        ds = (p * (dp - jnp.sum(dp * p, axis=-1, keepdims=True)) * scale).astype(BF)
        dq_ref[...] = lax.dot_general(ds, k, _DN["nn"], preferred_element_type=F32).astype(BF)

        @pl.when(pl.program_id(2) == 0)
        def _():
            dk_ref[...] = jnp.zeros_like(dk_ref)
            dv_ref[...] = jnp.zeros_like(dv_ref)

        dk_ref[...] += lax.dot_general(ds, q, _DN["tn"], preferred_element_type=F32)
        dv_ref[...] += lax.dot_general(p.astype(BF), do, _DN["tn"], preferred_element_type=F32)

    qblk = pl.BlockSpec((ts, dh), lambda bi, h, s: (bi * nsb + s, h))
    kblk = pl.BlockSpec((Mn, dh), lambda bi, h, s: (bi, h))
    return pl.pallas_call(
        body, grid=(Bn, H, nsb),
        in_specs=[qblk, kblk, pl.BlockSpec((Mn, dh), lambda bi, h, s: (bi, H + h)), qblk],
        out_specs=[qblk, kblk, kblk], out_shape=[SDS((Bn * S, D), BF), SDS((Bn * Mn, D), F32), SDS((Bn * Mn, D), F32)],
        compiler_params=_params("parallel", "parallel", "arbitrary"), name=name)(q, kv, kv, datt)


class _Sides:
    def __init__(self, by_key=None):
        self.by_key, self.landed = dict(by_key or {}), {}

    def mm(self, key, *args, **kw):
        side = self.by_key.get(key)
        if side is None:
            return _mm(*args, **kw)
        out, self.landed[key] = _mm(*args, side=side, **kw)
        return out


def _layer_fwd(x, mem_n, W, V, l, dims, sides):
    Bn, S, Mn, D, C, F = dims
    n = f"l{l}_"
    h = _rms_fwd(x, V["mix_norm_g"][l], n + "mix_norm")
    proj = sides.mm("proj", h, W["w_in"], "nn", F32, n + "proj", bl=0)
    cv = _glu_conv_fwd(proj, V["conv_dw_w"][l], V["conv_dw_b"][l], Bn, S, C, n + "glu_conv")
    yc1 = _ln_silu_fwd(cv, V["conv_ln_g"][l], V["conv_ln_b"][l], n + "ln_silu")
    yc = sides.mm("conv_out", yc1, W["w_conv_out"], "nn", F32, n + "conv_out", bl=0)
    yp = _pool_fwd(proj, W["w_pool"], 0, Bn, S, C, D, n + "pool")
    merged = _merge_fwd(proj, yc, yp, V["pool_scale"][l], C, n + "merge")
    x1 = sides.mm("out_proj", merged, W["w_out"], "nn", F32, n + "out_proj", res=x, bl=0)
    hq = _rms_fwd(x1, V["xattn_norm_g"][l], n + "xattn_norm")
    q = sides.mm("q_proj", hq, W["w_q"], "nn", BF, n + "q_proj", bl=0)
    kv = _mm(mem_n, W["w_kv"], "nn", BF, n + "kv_proj", bl=0)
    att = _attn_fwd(q, kv, Bn, S, Mn, D, n + "attn")
    x2 = sides.mm("o_proj", att, W["w_o"], "nn", F32, n + "o_proj", res=x1, bl=0)
    hf = _rms_fwd(x2, V["ffn_norm_g"][l], n + "ffn_norm")
    up0 = sides.mm("up_proj", hf, W["w_up"], "nn", F32, n + "up_proj", bl=0)
    gact = _ffn_act_fwd(up0, V["ffn_dw_w"][l], Bn, S, F, n + "ffn_act")
    x3 = sides.mm("down_proj", gact, W["w_down"], "nn", F32, n + "down_proj", res=x2, bl=0)
    return x3, dict(x=x, h=h, proj=proj, cv=cv, yc1=yc1, yc=yc, yp=yp, merged=merged, x1=x1, hq=hq, q=q, kv=kv, att=att, x2=x2, hf=hf,
                    up0=up0, gact=gact)


def _layer_bwd_mlp(dx, dxb, sv, W, V, l, dims, sides):
    Bn, S, Mn, D, C, F = dims
    n = f"l{l}_b_"
    gw, sm = {}, {}
    dgact = sides.mm("d_gact", dxb, W["w_down"], "nt", BF, n + "d_gact", bl=0)
    gw["w_down"] = sides.mm("dw_down", sv["gact"], dxb, "tn", F32, n + "dw_down")
    dg0, dv0, dwg, dwv = _ffn_act_bwd(sv["up0"], V["ffn_dw_w"][l], dgact, Bn, S, F, n + "ffn_act")
    sm["ffn_dw_w"] = jnp.concatenate([dwg, dwv], axis=1)
    dup0 = jnp.concatenate([dg0, dv0], axis=1)
    dhf = sides.mm("d_hf", dup0, W["w_up"], "nt", F32, n + "d_hf", bl=0)
    gw["w_up"] = sides.mm("dw_up", sv["hf"], dup0, "tn", F32, n + "dw_up")
    dx2, dx2b, sm["ffn_norm_g"] = _rms_bwd(sv["x2"], V["ffn_norm_g"][l], dhf, dx, n + "ffn_norm")
    return dx2, dx2b, gw, sm


def _layer_bwd_mix(dx2, dx2b, dmem_n, sv, mem_n, W, V, l, dims, sides):
    Bn, S, Mn, D, C, F = dims
    n = f"l{l}_b_"
    gw, sm = {}, {}
    datt = _mm(dx2b, W["w_o"], "nt", BF, n + "d_att", bl=0)
    gw["w_o"] = _mm(sv["att"], dx2b, "tn", F32, n + "dw_o")
    dq, dk, dv = _attn_bwd(sv["q"], sv["kv"], datt, Bn, S, Mn, D, n + "attn")
    dkv = jnp.concatenate([dk, dv], axis=1)
    gw["w_kv"] = _mm(mem_n, dkv, "tn", F32, n + "dw_kv")
    dmem_n = _mm(dkv, W["w_kv"], "nt", F32, n + "d_mem", res=dmem_n, bl=0)
    dhq = _mm(dq, W["w_q"], "nt", F32, n + "d_hq", bl=0)
    gw["w_q"] = _mm(sv["hq"], dq, "tn", F32, n + "dw_q")
    dx1, dx1b, sm["xattn_norm_g"] = _rms_bwd(sv["x1"], V["xattn_norm_g"][l], dhq, dx2, n + "xattn_norm")
    dmerged = _mm(dx1b, W["w_out"], "nt", BF, n + "d_merged", bl=0)
    gw["w_out"] = _mm(sv["merged"], dx1b, "tn", F32, n + "dw_out")
    dgc, dgp, dyc, dyp, sm["pool_scale"] = _merge_bwd(sv["proj"], sv["yc"], sv["yp"], V["pool_scale"][l], dmerged, C, n + "merge")
    du, gw["w_pool"] = _pool_bwd(sv["proj"], W["w_pool"], dyp, 0, Bn, S, C, D, n + "pool")
    dyc1 = _mm(dyc, W["w_conv_out"], "nt", F32, n + "d_yc1", bl=0)
    gw["w_conv_out"] = _mm(sv["yc1"], dyc, "tn", F32, n + "dw_conv_out")
    dcv, sm["conv_ln_g"], sm["conv_ln_b"] = _ln_silu_bwd(sv["cv"], V["conv_ln_g"][l], V["conv_ln_b"][l], dyc1, n + "ln_silu")
    da, dgl, sm["conv_dw_w"], sm["conv_dw_b"] = _glu_conv_bwd(sv["proj"], V["conv_dw_w"][l], dcv, Bn, S, C, n + "glu_conv")
    dproj = jnp.concatenate([da, dgl, du, dgc, dgp], axis=1)
    dh = sides.mm("d_h", dproj, W["w_in"], "nt", F32, n + "d_h", bl=0)
    gw["w_in"] = sides.mm("dw_in", sv["h"], dproj, "tn", F32, n + "dw_in")
    dx, dxb, sm["mix_norm_g"] = _rms_bwd(sv["x"], V["mix_norm_g"][l], dh, dx1, n + "mix_norm")
    return dx, dxb, dmem_n, gw, sm


BIG = (("w_in", "col"), ("w_conv_out", "col"), ("w_pool", "row"), ("w_out", "row"), ("w_q", "row"), ("w_kv", "col"),
       ("w_o", "row"), ("w_up", "col"), ("w_down", "row"))
FWD_CARRY = {"proj": ("w_in",), "conv_out": ("w_conv_out", "w_pool"), "out_proj": ("w_out", "w_q"), "q_proj": ("w_o",), "o_proj": ("w_kv",),
             "up_proj": ("w_up",), "down_proj": ("w_down",)}
EARLY = ("w_down", "w_up")
BWD_CARRY_EARLY = {"d_h": ("w_up",), "dw_in": ("w_down",)}
BWD_CARRY_LATE = {"d_hf": ("w_in",), "dw_up": ("w_conv_out", "w_pool", "w_out", "w_q", "w_kv", "w_o")}


def _place():
    xi, yi, ci = lax.axis_index("x"), lax.axis_index("y"), lax.axis_index("c")
    return xi, yi, ci, 2 * xi + yi


def _chip_peer(xi, yi, ci, r):
    return (xi ^ (r >> 1), yi ^ (r & 1), ci)


def _full_shard(ref, kind, k, cs):
    if kind == "col":
        return ref.at[:, :, :, :, pl.ds(pl.multiple_of(k * cs, cs), cs)]
    return ref.at[:, :, k]


def _gather_weights(shards, kinds):
    n = len(shards)
    outs = []
    for s, kind in zip(shards, kinds):
        L, P, _, RH, CS = s.shape
        outs.append(SDS((L, P, 2, RH, CS * N_CHIPS) if kind == "col" else (L, P, N_CHIPS, 2, RH, CS), s.dtype))
    per = 7

    def body(*refs):
        srcs, fulls, (ssem, rsem) = refs[:n], refs[n:2 * n], refs[2 * n:]
        xi, yi, ci, j = _place()
        sib = (xi, yi, 1 - ci)

        def piece(i, k, c):
            kind, cs = kinds[i], shards[i].shape[-1]
            if kind == "col":
                return fulls[i].at[:, :, c, :, pl.ds(pl.multiple_of(k * cs, cs), cs)]
            return fulls[i].at[:, :, k, c]

        def copy(i, slot, src, dst, dev):
            return pltpu.make_async_remote_copy(src_ref=src, dst_ref=dst, send_sem=ssem.at[per * i + slot], recv_sem=rsem.at[per * i + slot],
                                                device_id=dev, device_id_type=MESH)

        own, first, passed = [], [], []
        for i in range(n):
            for r in (1, 2, 3):
                first.append(copy(i, r - 1, srcs[i].at[:, :, ci], piece(i, j, ci), _chip_peer(xi, yi, ci, r)))
                first[-1].start()
        for i in range(n):
            own.append(copy(i, 6, srcs[i], _full_shard(fulls[i], kinds[i], j, shards[i].shape[-1]), sib))
            own[-1].start()
        for i in range(n):
            for r in (1, 2, 3):
                got = piece(i, j ^ r, ci)
                copy(i, r - 1, got, got, sib).wait_recv()
                passed.append(copy(i, 2 + r, got, got, sib))
                passed[-1].start()
        for i in range(n):
            for r in (1, 2, 3):
                got = piece(i, j ^ r, 1 - ci)
                copy(i, 2 + r, got, got, sib).wait_recv()
        for cp in own:
            cp.wait()
        for cp in first + passed:
            cp.wait_send()

    return pl.pallas_call(
        body, in_specs=[ANY] * n, out_specs=[ANY] * n, out_shape=outs,
        scratch_shapes=[pltpu.SemaphoreType.DMA((per * n,)), pltpu.SemaphoreType.DMA((per * n,))], name="gather_weights")(*shards)


def _full_sds(s, kind):
    L, P, _, RH, CS = s.shape
    return SDS((L, P, 2, RH, CS * N_CHIPS) if kind == "col" else (L, P, N_CHIPS, 2, RH, CS), s.dtype)


def _gather_piece(full, kind, cs, k, c):
    if kind == "col":
        return full.at[:, :, c, :, pl.ds(pl.multiple_of(k * cs, cs), cs)]
    return full.at[:, :, k, c]


def _side_gather(shards, kinds):
    n = len(shards)

    def make(srcs, fulls, ssem, rsem):
        xi, yi, ci, j = _place()
        return [pltpu.make_async_remote_copy(
            src_ref=srcs[i].at[:, :, ci], dst_ref=_gather_piece(fulls[i], kinds[i], shards[i].shape[-1], j, ci), send_sem=ssem.at[3 * i + r - 1],
            recv_sem=rsem.at[3 * i + r - 1], device_id=_chip_peer(xi, yi, ci, r), device_id_type=MESH) for i in range(n) for r in (1, 2, 3)]

    return _Side(shards, [_full_sds(s, k) for s, k in zip(shards, kinds)], 3 * n, make)


def _gather_pass(fulls, shards, kinds, name):
    n = len(fulls)

    def body(*refs):
        srcs, outs, (ssem, rsem) = refs[n:2 * n], refs[2 * n:3 * n], refs[3 * n:]
        xi, yi, ci, j = _place()
        sib = (xi, yi, 1 - ci)
        cps = []
        for i in range(n):
            cs = shards[i].shape[-1]
            for r in (1, 2, 3):
                got = _gather_piece(outs[i], kinds[i], cs, j ^ r, ci)
                cps.append(pltpu.make_async_remote_copy(src_ref=got, dst_ref=got, send_sem=ssem.at[4 * i + r - 1], recv_sem=rsem.at[4 * i + r - 1],
                                                        device_id=sib, device_id_type=MESH))
            cps.append(pltpu.make_async_remote_copy(src_ref=srcs[i], dst_ref=_full_shard(outs[i], kinds[i], j, cs), send_sem=ssem.at[4 * i + 3],
                                                    recv_sem=rsem.at[4 * i + 3], device_id=sib, device_id_type=MESH))
        for cp in cps:
            cp.start()
        for cp in cps:
            cp.wait()

    return pl.pallas_call(
        body, in_specs=[ANY] * (2 * n), out_specs=[ANY] * n, out_shape=[SDS(f.shape, f.dtype) for f in fulls],
        input_output_aliases={i: i for i in range(n)},
        scratch_shapes=[pltpu.SemaphoreType.DMA((4 * n,)), pltpu.SemaphoreType.DMA((4 * n,))], name=name)(*fulls, *shards)


def _sibling_exchange(gviews, kinds, name):
    n = len(gviews)
    outs = [SDS(g.shape[:1] + g.shape[2:] if kind == "col" else g.shape[:2] + g.shape[3:], g.dtype) for g, kind in zip(gviews, kinds)]

    def body(*refs):
        gs, lands, (ssem, rsem) = refs[:n], refs[n:2 * n], refs[2 * n:]
        xi, yi, ci, _ = _place()
        cps = []
        for i in range(n):
            src = gs[i].at[:, 1 - ci] if kinds[i] == "col" else gs[i].at[:, :, 1 - ci]
            cps.append(pltpu.make_async_remote_copy(src_ref=src, dst_ref=lands[i], send_sem=ssem.at[i], recv_sem=rsem.at[i],
                                                    device_id=(xi, yi, 1 - ci), device_id_type=MESH))
            cps[-1].start()
        for cp in cps:
            cp.wait()

    return pl.pallas_call(body, in_specs=[ANY] * n, out_specs=[ANY] * n, out_shape=outs,
                          scratch_shapes=[pltpu.SemaphoreType.DMA((n,)), pltpu.SemaphoreType.DMA((n,))], name=name)(*gviews)


def _chip_sum(g, land, kind, jc, name):
    if kind == "col":
        P, _, RH, C = g.shape
        CS = C // N_CHIPS
        g_spec = pl.BlockSpec((None, None, RH, CS), lambda p, r, jc: (p, jc[1], 0, jc[0] ^ r))
        l_spec = pl.BlockSpec((None, RH, CS), lambda p, r, jc: (p, 0, jc[0] ^ r))
    else:
        P, _, _, RH, CS = g.shape
        g_spec = pl.BlockSpec((None, None, None, RH, CS), lambda p, r, jc: (p, jc[0] ^ r, jc[1], 0, 0))
        l_spec = pl.BlockSpec((None, None, RH, CS), lambda p, r, jc: (p, jc[0] ^ r, 0, 0))

    def body(jc_ref, g_ref, l_ref, own_ref, all_ref):
        s = g_ref[...] + l_ref[...]
        all_ref[...] = s.astype(BF)

        @pl.when(pl.program_id(1) == 0)
        def _():
            own_ref[...] = s

    return pl.pallas_call(
        body, grid_spec=pltpu.PrefetchScalarGridSpec(
            num_scalar_prefetch=1, grid=(P, N_CHIPS), in_specs=[g_spec, l_spec],
            out_specs=[pl.BlockSpec((None, RH, CS), lambda p, r, jc: (p, 0, 0)), pl.BlockSpec((None, None, RH, CS), lambda p, r, jc: (r, p, 0, 0))]),
        out_shape=[SDS((P, RH, CS), F32), SDS((N_CHIPS, P, RH, CS), BF)], compiler_params=_params("parallel", "arbitrary"), name=name)(jc, g, land)


def _chip_exchange_copies(srcs, lands, ssem, rsem):
    xi, yi, ci, _ = _place()
    return [pltpu.make_async_remote_copy(src_ref=srcs[i].at[r], dst_ref=lands[i].at[r], send_sem=ssem.at[3 * i + r - 1],
                                         recv_sem=rsem.at[3 * i + r - 1], device_id=_chip_peer(xi, yi, ci, r), device_id_type=MESH)
            for i in range(len(srcs)) for r in (1, 2, 3)]


def _side_chip_exchange(pieces):
    return _Side(pieces, [SDS(p.shape, p.dtype) for p in pieces], 3 * len(pieces), _chip_exchange_copies)


def _chip_exchange(pieces):
    n = len(pieces)

    def body(*refs):
        cps = _chip_exchange_copies(refs[:n], refs[n:2 * n], *refs[2 * n:])
        for cp in cps:
            cp.start()
        for cp in cps:
            cp.wait()

    return pl.pallas_call(body, in_specs=[ANY] * n, out_specs=[ANY] * n, out_shape=[SDS(p.shape, p.dtype) for p in pieces],
                          scratch_shapes=[pltpu.SemaphoreType.DMA((3 * n,)), pltpu.SemaphoreType.DMA((3 * n,))], name="grad_chip_exchange")(*pieces)


def _final_sum(own, land, jc, shard, l, L, name):
    P, RH, CS = own.shape

    def body(jc_ref, o_ref, a_ref, b_ref, c_ref, *rest):
        rest[-1][...] = ((o_ref[...] + a_ref[...].astype(F32)) + b_ref[...].astype(F32)) + c_ref[...].astype(F32)

    blk = pl.BlockSpec((None, RH, CS), lambda p, jc: (p, 0, 0))
    in_specs = [blk] + [pl.BlockSpec((None, None, RH, CS), functools.partial(lambda r, p, jc: (r, p, 0, 0), r)) for r in (1, 2, 3)]
    args = [jc, own, land, land, land]
    if shard is not None:
        in_specs.append(ANY)
        args.append(shard)
    return pl.pallas_call(
        body, grid_spec=pltpu.PrefetchScalarGridSpec(
            num_scalar_prefetch=1, grid=(P,), in_specs=in_specs,
            out_specs=pl.BlockSpec((None, None, None, RH, CS), lambda p, jc: (l, p, jc[1], 0, 0))),
        out_shape=SDS((L, P, 2, RH, CS), F32), input_output_aliases={5: 0} if shard is not None else {},
        compiler_params=_params("arbitrary"), name=name)(*args)


def _halves_exchange(shards):
    n = len(shards)

    def body(*refs):
        outs, (ssem, rsem) = refs[n:2 * n], refs[2 * n:]
        xi, yi, ci, _ = _place()
        cps = []
        for i in range(n):
            mine = outs[i].at[:, :, ci]
            cps.append(pltpu.make_async_remote_copy(src_ref=mine, dst_ref=mine, send_sem=ssem.at[i], recv_sem=rsem.at[i],
                                                    device_id=(xi, yi, 1 - ci), device_id_type=MESH))
            cps[-1].start()
        for i in range(n):
            land = outs[i].at[:, :, 1 - ci]
            pltpu.make_async_remote_copy(src_ref=land, dst_ref=land, send_sem=ssem.at[i], recv_sem=rsem.at[i],
                                         device_id=(xi, yi, 1 - ci), device_id_type=MESH).wait_recv()
        for cp in cps:
            cp.wait_send()

    return pl.pallas_call(body, in_specs=[ANY] * n, out_specs=[ANY] * n, out_shape=[SDS(s.shape, s.dtype) for s in shards],
                          input_output_aliases={i: i for i in range(n)},
                          scratch_shapes=[pltpu.SemaphoreType.DMA((n,)), pltpu.SemaphoreType.DMA((n,))], name="grad_halves_exchange")(*shards)


def _reduce_small(part):
    NR, Wd = part.shape
    ND = 2 * N_CHIPS

    def body(p_ref, o_ref, land, ssem, rsem):
        xi, yi, ci, j = _place()
        me = 2 * j + ci
        land[me] = p_ref[...]
        cps = []
        for rr in range(1, ND):
            dev = (xi ^ (rr >> 2), yi ^ ((rr >> 1) & 1), ci ^ (rr & 1))
            cps.append(pltpu.make_async_remote_copy(src_ref=p_ref, dst_ref=land.at[me], send_sem=ssem.at[rr - 1], recv_sem=rsem.at[rr - 1],
                                                    device_id=dev, device_id_type=MESH))
            cps[-1].start()
        for rr in range(1, ND):
            got = land.at[me ^ rr]
            pltpu.make_async_remote_copy(src_ref=got, dst_ref=got, send_sem=ssem.at[rr - 1], recv_sem=rsem.at[rr - 1],
                                         device_id=(xi, yi, ci), device_id_type=MESH).wait_recv()
        acc = land[0]
        for d in range(1, ND):
            acc = acc + land[d]
        o_ref[...] = acc
        for cp in cps:
            cp.wait_send()

    vm = pl.BlockSpec(memory_space=pltpu.VMEM)
    return pl.pallas_call(body, in_specs=[vm], out_specs=vm, out_shape=SDS((NR, Wd), F32),
                          scratch_shapes=[pltpu.VMEM((ND, NR, Wd), F32), pltpu.SemaphoreType.DMA((ND - 1,)), pltpu.SemaphoreType.DMA((ND - 1,))],
                          name="small_grad_allreduce")(part)


def _adamw(w, g, m, v, name):
    shape = w.shape
    C = shape[-1]
    R = w.size // C
    tb = _tile(R, max(8, (1 << 18) // C), 8)

    def body(w_ref, g_ref, m_ref, v_ref, d_ref, mo_ref, vo_ref):
        g = g_ref[...]
        m = ADAM_B1 * m_ref[...] + (1.0 - ADAM_B1) * g
        v = ADAM_B2 * v_ref[...] + (1.0 - ADAM_B2) * jnp.square(g)
        m_hat = m / (1.0 - ADAM_B1 ** ADAM_STEP)
        v_hat = v / (1.0 - ADAM_B2 ** ADAM_STEP)
        d_ref[...] = -ADAM_LR * (m_hat / (jnp.sqrt(v_hat) + ADAM_EPS) + ADAM_WD * w_ref[...])
        mo_ref[...] = m
        vo_ref[...] = v

    blk = pl.BlockSpec((tb, C), lambda i: (i, 0))
    outs = pl.pallas_call(body, grid=(R // tb,), in_specs=[blk] * 4, out_specs=[blk] * 3, out_shape=[SDS((R, C), F32)] * 3,
                          compiler_params=_params("parallel"), name=name)(*[t.reshape(R, C) for t in (w, g, m, v)])
    return [t.reshape(shape) for t in outs]


WEIGHTS = ("mix_norm_g", "w_in", "conv_dw_w", "conv_dw_b", "conv_ln_g", "conv_ln_b", "w_conv_out", "w_pool_grp", "pool_scale", "w_out",
           "xattn_norm_g", "mem_norm_g", "w_q", "w_kv", "w_o", "ffn_norm_g", "w_up", "ffn_dw_w", "w_down", "final_norm_g")
VECTORS = ("mix_norm_g", "conv_dw_b", "conv_ln_g", "conv_ln_b", "pool_scale", "xattn_norm_g", "mem_norm_g", "ffn_norm_g", "final_norm_g")


def _shard_view(t, kind):
    L, P, R, C = t.shape
    return t.reshape(L, P, 2, R // 2, C)


def _rows(t, width):
    return t.reshape(-1, width)


def _pack(parts):
    return jnp.concatenate([jnp.pad(p, ((0, (-p.shape[0]) % 8), (0, 0))) for p in parts], axis=0)


def kernel(x, mem, mix_norm_g, w_in, conv_dw_w, conv_dw_b, conv_ln_g, conv_ln_b, w_conv_out, w_pool_grp, pool_scale, w_out, xattn_norm_g, mem_norm_g, w_q, w_kv, w_o, ffn_norm_g, w_up, ffn_dw_w, w_down, final_norm_g, loss_target, m_mix_norm_g, m_w_in, m_conv_dw_w, m_conv_dw_b, m_conv_ln_g, m_conv_ln_b, m_w_conv_out, m_w_pool_grp, m_pool_scale, m_w_out, m_xattn_norm_g, m_mem_norm_g, m_w_q, m_w_kv, m_w_o, m_ffn_norm_g, m_w_up, m_ffn_dw_w, m_w_down, m_final_norm_g, v_mix_norm_g, v_w_in, v_conv_dw_w, v_conv_dw_b, v_conv_ln_g, v_conv_ln_b, v_w_conv_out, v_w_pool_grp, v_pool_scale, v_w_out, v_xattn_norm_g, v_mem_norm_g, v_w_q, v_w_kv, v_w_o, v_ffn_norm_g, v_w_up, v_ffn_dw_w, v_w_down, v_final_norm_g):
    w = dict(mix_norm_g=mix_norm_g, w_in=w_in, conv_dw_w=conv_dw_w, conv_dw_b=conv_dw_b, conv_ln_g=conv_ln_g, conv_ln_b=conv_ln_b,
             w_conv_out=w_conv_out, w_pool_grp=w_pool_grp, pool_scale=pool_scale, w_out=w_out, xattn_norm_g=xattn_norm_g,
             mem_norm_g=mem_norm_g, w_q=w_q, w_kv=w_kv, w_o=w_o, ffn_norm_g=ffn_norm_g, w_up=w_up, ffn_dw_w=ffn_dw_w, w_down=w_down,
             final_norm_g=final_norm_g)
    m = dict(zip(WEIGHTS, (m_mix_norm_g, m_w_in, m_conv_dw_w, m_conv_dw_b, m_conv_ln_g, m_conv_ln_b, m_w_conv_out, m_w_pool_grp, m_pool_scale,
                           m_w_out, m_xattn_norm_g, m_mem_norm_g, m_w_q, m_w_kv, m_w_o, m_ffn_norm_g, m_w_up, m_ffn_dw_w, m_w_down, m_final_norm_g)))
    v = dict(zip(WEIGHTS, (v_mix_norm_g, v_w_in, v_conv_dw_w, v_conv_dw_b, v_conv_ln_g, v_conv_ln_b, v_w_conv_out, v_w_pool_grp, v_pool_scale,
                           v_w_out, v_xattn_norm_g, v_mem_norm_g, v_w_q, v_w_kv, v_w_o, v_ffn_norm_g, v_w_up, v_ffn_dw_w, v_w_down, v_final_norm_g)))
    xi, yi, ci, j = _place()
    jc = jnp.stack([j, ci]).astype(jnp.int32)
    L = w_in.shape[0]
    G = len(POOL_WINDOWS)
    kinds = dict(BIG)

    def to_mat(name, t):
        if name == "w_pool":
            return jnp.swapaxes(t, 2, 3)
        return t[:, None]

    def from_mat(name, t):
        if name == "w_pool":
            return jnp.swapaxes(t, 2, 3)
        return t[:, 0]

    src = {name: w["w_pool_grp" if name == "w_pool" else name] for name, _ in BIG}

    KC, cs_c = conv_dw_w.shape[1], conv_dw_w.shape[2]
    KF, cs_f = ffn_dw_w.shape[1], ffn_dw_w.shape[2]
    taps = jnp.concatenate([conv_dw_w.reshape(L * KC, cs_c), ffn_dw_w.reshape(L * KF * (cs_f // cs_c), cs_c)], axis=0)
    n_taps = taps.shape[0]
    taps = jnp.pad(taps, ((0, (-n_taps) % 16), (0, 0)))
    names = [name for name, _ in BIG]
    mats = {name: to_mat(name, src[name]).astype(BF) for name in names}

    def layer_shards(l, subset):
        return [_shard_view(mats[name][l:l + 1], kinds[name]) for name in subset]

    def as_weights(subset, fulls):
        return {name: f.reshape(G if name == "w_pool" else 1, -1, f.shape[-1]) for name, f in zip(subset, fulls)}

    fulls = _gather_weights(layer_shards(0, names) + [_shard_view(taps[None, None], "row")], [kinds[name] for name in names] + ["row"])
    W = [as_weights(names, fulls[:-1])]
    taps_all = fulls[-1].reshape(N_CHIPS, -1, cs_c)[:, :n_taps]
    V = {name: w[name] for name in VECTORS}
    V["conv_dw_w"] = taps_all[:, :L * KC].reshape(N_CHIPS, L, KC, cs_c).transpose(1, 2, 0, 3).reshape(L, KC, N_CHIPS * cs_c)
    V["ffn_dw_w"] = taps_all[:, L * KC:].reshape(N_CHIPS, L, KF, cs_f).transpose(1, 2, 0, 3).reshape(L, KF, N_CHIPS * cs_f)

    Bn, S, D = x.shape
    Mn = mem.shape[1]
    dims = (Bn, S, Mn, D, conv_dw_b.shape[1], w_down.shape[1] * N_CHIPS)
    xt = x.reshape(Bn * S, D)
    memf = mem.reshape(Bn * Mn, D)
    mem_n = _rms_fwd(memf, V["mem_norm_g"], "mem_norm")
    saved = []
    for l in range(L):
        sides = _Sides()
        if l + 1 < L:
            sides = _Sides({key: _side_gather(layer_shards(l + 1, subset), [kinds[name] for name in subset]) for key, subset in FWD_CARRY.items()})
        xt, sv = _layer_fwd(xt, mem_n, W[l], V, l, dims, sides)
        saved.append(sv)
        if l + 1 < L:
            nxt = {}
            for key, subset in FWD_CARRY.items():
                done = _gather_pass(sides.landed[key], layer_shards(l + 1, subset), [kinds[name] for name in subset], f"gather_pass_l{l + 1}_{key}")
                nxt.update(as_weights(subset, done))
            W.append(nxt)
    loss, dx, dgf = _loss_bwd(xt, V["final_norm_g"], loss_target.reshape(Bn * S, D), "loss")
    loss = lax.psum(loss[0, 0], ("x", "y", "c"))

    late_names = [name for name in names if name not in EARLY]

    def chip_sums(gw, subset, l, tag):
        gv = []
        for name in subset:
            g = gw[name] if gw[name].ndim == 3 else gw[name][None]
            P, R, C = g.shape
            gv.append(g.reshape(P, 2, R // 2, C) if kinds[name] == "col" else g.reshape(P, N_CHIPS, 2, R // (2 * N_CHIPS), C))
        lands = _sibling_exchange(gv, [kinds[name] for name in subset], f"grad_sibling_exchange_{tag}_l{l}")
        own, pieces = {}, {}
        for name, g, land in zip(subset, gv, lands):
            own[name], pieces[name] = _chip_sum(g, land, kinds[name], jc, f"chip_sum_{name}_{l}")
        return own, pieces

    def carry(table, pieces):
        return _Sides({key: _side_chip_exchange([pieces[name] for name in subset]) for key, subset in table.items()})

    def landed(table, sides):
        return {name: land for key, subset in table.items() for name, land in zip(subset, sides.landed[key])}

    dxb, dmem_n = dx, None
    smalls, owns, got = [None] * L, [{} for _ in range(L)], [{} for _ in range(L)]
    late = None
    for l in reversed(range(L)):
        sides = carry(BWD_CARRY_LATE, late) if late is not None else _Sides()
        dx, dxb, gw, sm = _layer_bwd_mlp(dx, dxb, saved[l], W[l], V, l, dims, sides)
        if late is not None:
            got[l + 1].update(landed(BWD_CARRY_LATE, sides))
        own, early = chip_sums(gw, EARLY, l, "mlp")
        owns[l].update(own)
        sides = carry(BWD_CARRY_EARLY, early)
        dx, dxb, dmem_n, gw, sm2 = _layer_bwd_mix(dx, dxb, dmem_n, saved[l], mem_n, W[l], V, l, dims, sides)
        got[l].update(landed(BWD_CARRY_EARLY, sides))
        smalls[l] = {**sm, **sm2}
        own, late = chip_sums(gw, late_names, l, "mix")
        owns[l].update(own)
    got[0].update(zip(late_names, _chip_exchange([late[name] for name in late_names])))
    owns = [[o[name] for name in names] for o in owns]
    got = [[g[name] for name in names] for g in got]
    grad_x = dx.reshape(Bn, S, D)
    _, _, dgm = _rms_bwd(memf, V["mem_norm_g"], dmem_n, None, "mem_norm_b")
    small = {k: jnp.stack([sm[k] for sm in smalls]) if k in ("conv_dw_w", "ffn_dw_w") else jnp.concatenate([sm[k] for sm in smalls], axis=0)
             for k in smalls[0]}
    small["mem_norm_g"] = dgm
    small["final_norm_g"] = dgf
    mine = []
    for i, name in enumerate(names):
        shard = None
        for l in range(L):
            shard = _final_sum(owns[l][i], got[l][i], jc, shard, l, L, f"final_sum_{name}_{l}")
        mine.append(shard)
    gshards = _halves_exchange(mine)
    grads = {}
    for (name, kind), gs in zip(BIG, gshards):
        Lg, P, _, RH, CS = gs.shape
        grads["w_pool_grp" if name == "w_pool" else name] = from_mat(name, gs.reshape(Lg, P, 2 * RH, CS))

    small_w = conv_dw_b.shape[1]
    order = VECTORS + ("conv_dw_w", "ffn_dw_w")
    parts = [_rows(small[name], small_w) for name in order]
    counts = [p.shape[0] for p in parts]
    summed = _reduce_small(_pack(parts))
    off = 0
    for name, cnt in zip(order, counts):
        t = summed[off:off + cnt]
        off += cnt + (-cnt) % 8
        if name in VECTORS:
            grads[name] = t.reshape(w[name].shape)
        else:
            full = t.reshape(small[name].shape)
            cs = w[name].shape[2]
            grads[name] = lax.dynamic_slice_in_dim(full, j * cs, cs, axis=2)

    delta, new_m, new_v = {}, {}, {}
    for name, _ in BIG:
        key = "w_pool_grp" if name == "w_pool" else name
        outs = _adamw(*[to_mat(name, t) for t in (w[key], grads[key], m[key], v[key])], "adamw_" + name)
        delta[key], new_m[key], new_v[key] = [from_mat(name, t) for t in outs]
    vec = [_pack([_rows(d[name], small_w) for name in VECTORS]) for d in (w, grads, m, v)]
    outs = _adamw(*vec, "adamw_vectors")
    off = 0
    for name in VECTORS:
        cnt = w[name].size // small_w
        for d, t in zip((delta, new_m, new_v), outs):
            d[name] = t[off:off + cnt].reshape(w[name].shape)
        off += cnt + (-cnt) % 8
    for name in ("conv_dw_w", "ffn_dw_w"):
        delta[name], new_m[name], new_v[name] = _adamw(w[name], grads[name], m[name], v[name], "adamw_" + name)

    return (loss, grad_x, *[grads[k] for k in WEIGHTS], *[delta[k] for k in WEIGHTS], *[new_m[k] for k in WEIGHTS], *[new_v[k] for k in WEIGHTS])
```

```python
import functools

import jax
import jax.numpy as jnp
from jax import lax
from jax.experimental import pallas as pl
from jax.experimental.pallas import tpu as pltpu

F32 = jnp.float32
BF = jnp.bfloat16
SDS = jax.ShapeDtypeStruct
MESH = pl.DeviceIdType.MESH
ANY = pl.BlockSpec(memory_space=pl.ANY)

EPS = 1e-6
XA_HEADS = 4
POOL_WINDOWS = (2, 4, 8, 16)
N_CHIPS = 4
ADAM_LR, ADAM_B1, ADAM_B2, ADAM_EPS, ADAM_WD, ADAM_STEP = 0.001, 0.9, 0.999, 1e-08, 0.01, 10

LANES = 128
ROW_BLOCK = 512
VMEM_LIMIT = 56 * 1024 * 1024


def _params(*sem):
    return pltpu.CompilerParams(dimension_semantics=sem if sem else None, vmem_limit_bytes=VMEM_LIMIT)


def _tile(n, cap, mult=LANES):
    if n <= cap:
        return n
    for t in range(cap - cap % mult, 0, -mult):
        if n % t == 0:
            return t
    return n


_DN = {"nn": (((1,), (0,)), ((), ())), "nt": (((1,), (1,)), ((), ())), "tn": (((0,), (0,)), ((), ()))}


class _Side:
    def __init__(self, ins, outs, n, make):
        self.ins, self.outs, self.n, self.make = list(ins), list(outs), n, make


MM_VMEM_BUDGET = 40 * 1024 * 1024
MM_STEP_MACS = 2200 * 1024 * 1024
MXU_WIDTH = 256
MM_STEP_COST_BYTES = 1 << 20


def _divisors(n):
    return [t for t in range(LANES, n + 1, LANES) if n % t == 0] or [n]


def _mm_tiles(M, N, K, a_bytes, b_bytes, o_bytes):
    best = None
    for tk in _divisors(K):
        for tm in _divisors(M):
            for tn in _divisors(N):
                nk = K // tk
                foot = 2 * (tm * tk * a_bytes + tk * tn * b_bytes + tm * tn * o_bytes) + (tm * tn * 4 if nk > 1 else 0)
                if foot > MM_VMEM_BUDGET or tm * tn * tk > MM_STEP_MACS or tn < min(N, MXU_WIDTH) or tm < min(M, MXU_WIDTH):
                    continue
                steps = (M // tm) * (N // tn) * nk
                traffic = M * K * a_bytes * (N // tn if nk > 1 else 1) + K * N * b_bytes * (M // tm) + M * N * o_bytes
                cost = traffic + steps * MM_STEP_COST_BYTES + (nk - 1) * M * N * 8
                if best is None or cost < best[0]:
                    best = (cost, tm, tn, tk)
    assert best is not None, (M, N, K)
    return best[1:]


def _mm(a, b, dims, out_dtype, name, res=None, bl=None, side=None):
    bs = b.shape[1:] if bl is not None else b.shape
    if dims == "nn":
        (M, K), (K2, N) = a.shape, bs
    elif dims == "nt":
        (M, K), (N, K2) = a.shape, bs
    else:
        (K, M), (K2, N) = a.shape, bs
    assert K == K2, (name, a.shape, b.shape)
    tm, tn, tk = _mm_tiles(M, N, K, a.dtype.itemsize, b.dtype.itemsize,
                           jnp.dtype(out_dtype).itemsize + (res.dtype.itemsize if res is not None else 0))
    nk = K // tk
    lead = (None,) if bl is not None else ()
    pre = (lambda *ix: (bl,) + ix) if bl is not None else (lambda *ix: ix)
    if dims == "tn":
        a_spec = pl.BlockSpec((tk, tm), lambda i, j, k: (k, i))
    else:
        a_spec = pl.BlockSpec((tm, tk), lambda i, j, k: (i, k))
    if dims == "nt":
        b_spec = pl.BlockSpec(lead + (tn, tk), lambda i, j, k: pre(j, k))
    else:
        b_spec = pl.BlockSpec(lead + (tk, tn), lambda i, j, k: pre(k, j))
    o_spec = pl.BlockSpec((tm, tn), lambda i, j, k: (i, j))
    in_specs, args = [a_spec, b_spec], [a, b]
    if res is not None:
        in_specs.append(o_spec)
        args.append(res)
    n_main = len(args)
    n_si, n_so = (len(side.ins), len(side.outs)) if side is not None else (0, 0)
    gi, gj = M // tm, N // tn

    def body(*refs):
        a_ref, b_ref = refs[0], refs[1]
        r_ref = refs[2] if res is not None else None
        o_ref = refs[n_main + n_si]
        scratch = refs[n_main + n_si + 1 + n_so:]
        i, j, k = pl.program_id(0), pl.program_id(1), pl.program_id(2)
        if side is not None:
            copies = side.make(refs[n_main:n_main + n_si], refs[n_main + n_si + 1:n_main + n_si + 1 + n_so], scratch[-2], scratch[-1])

            @pl.when((i == 0) & (j == 0) & (k == 0))
            def _():
                for cp in copies:
                    cp.start()

        p = lax.dot_general(a_ref[...].astype(BF), b_ref[...].astype(BF), _DN[dims], preferred_element_type=F32)

        def finish(t):
            if r_ref is not None:
                t = t + r_ref[...]
            o_ref[...] = t.astype(out_dtype)

        if nk == 1:
            finish(p)
        else:
            acc = scratch[0]

            @pl.when(k == 0)
            def _():
                acc[...] = p

            @pl.when(k > 0)
            def _():
                acc[...] += p

            @pl.when(k == nk - 1)
            def _():
                finish(acc[...])

        if side is not None:
            @pl.when((i == gi - 1) & (j == gj - 1) & (k == nk - 1))
            def _():
                for cp in copies:
                    cp.wait()

    scratch_shapes = [pltpu.VMEM((tm, tn), F32)] if nk > 1 else []
    if side is None:
        return pl.pallas_call(
            body, grid=(gi, gj, nk), in_specs=in_specs, out_specs=o_spec, out_shape=SDS((M, N), out_dtype), scratch_shapes=scratch_shapes,
            compiler_params=_params("parallel", "parallel", "arbitrary"), name=name)(*args)
    outs = pl.pallas_call(
        body, grid=(gi, gj, nk), in_specs=in_specs + [ANY] * n_si, out_specs=[o_spec] + [ANY] * n_so,
        out_shape=[SDS((M, N), out_dtype)] + side.outs,
        scratch_shapes=scratch_shapes + [pltpu.SemaphoreType.DMA((side.n,)), pltpu.SemaphoreType.DMA((side.n,))],
        compiler_params=_params("arbitrary", "arbitrary", "arbitrary"), name=name)(*args, *side.ins)
    return outs[0], list(outs[1:])


def _rms(x, g):
    return x * lax.rsqrt(jnp.mean(x * x, axis=-1, keepdims=True) + EPS) * g


def _ln_silu(x, g, b):
    mu = jnp.mean(x, axis=-1, keepdims=True)
    xc = x - mu
    var = jnp.mean(xc * xc, axis=-1, keepdims=True)
    return jax.nn.silu(xc * lax.rsqrt(var + EPS) * g + b)


def _merge(gc, gp, yc, yp, ps):
    return jax.nn.sigmoid(gc) * yc + jax.nn.sigmoid(gp) * (yp * ps)


def _gated(gate, val):
    return jax.nn.gelu(gate) * val


def _rms_fwd(x, g, name):
    T, D = x.shape
    tb = _tile(T, ROW_BLOCK, 8)

    def body(x_ref, g_ref, o_ref):
        o_ref[...] = _rms(x_ref[...], g_ref[...]).astype(BF)

    row = pl.BlockSpec((tb, D), lambda i: (i, 0))
    return pl.pallas_call(body, grid=(T // tb,), in_specs=[row, pl.BlockSpec((1, D), lambda i: (0, 0))], out_specs=row,
                          out_shape=SDS((T, D), BF), compiler_params=_params("parallel"), name=name)(x, g.reshape(1, D))


def _rms_bwd(x, g, dh, dres, name):
    T, D = x.shape
    tb = _tile(T, ROW_BLOCK, 8)

    def body(*refs):
        if dres is not None:
            x_ref, g_ref, dh_ref, dres_ref, dx_ref, dxb_ref, dg_ref = refs
        else:
            x_ref, g_ref, dh_ref, dx_ref, dxb_ref, dg_ref = refs
        _, vjp = jax.vjp(_rms, x_ref[...], g_ref[...])
        dx, dg = vjp(dh_ref[...].astype(F32))
        if dres is not None:
            dx = dx + dres_ref[...]
        dx_ref[...] = dx
        dxb_ref[...] = dx.astype(BF)

        @pl.when(pl.program_id(0) == 0)
        def _():
            dg_ref[...] = jnp.zeros_like(dg_ref)

        dg_ref[...] += dg

    row = pl.BlockSpec((tb, D), lambda i: (i, 0))
    vec = pl.BlockSpec((1, D), lambda i: (0, 0))
    ins = [x, g.reshape(1, D), dh] + ([dres] if dres is not None else [])
    return pl.pallas_call(
        body, grid=(T // tb,), in_specs=[row, vec, row] + ([row] if dres is not None else []), out_specs=[row, row, vec],
        out_shape=[SDS((T, D), F32), SDS((T, D), BF), SDS((1, D), F32)], compiler_params=_params("arbitrary"), name=name)(*ins)


def _loss_bwd(x, g, target, name):
    T, D = x.shape
    tb = _tile(T, ROW_BLOCK, 8)
    nb = T // tb

    def body(x_ref, g_ref, t_ref, loss_ref, dx_ref, dg_ref, acc):
        i = pl.program_id(0)
        y, vjp = jax.vjp(_rms, x_ref[...], g_ref[...])
        err = y - t_ref[...]
        dx, dg = vjp(err * (1.0 / D))
        dx_ref[...] = dx

        @pl.when(i == 0)
        def _():
            dg_ref[...] = jnp.zeros_like(dg_ref)
            acc[...] = jnp.zeros_like(acc)

        dg_ref[...] += dg
        acc[...] += jnp.sum(err * err, axis=0, keepdims=True)

        @pl.when(i == nb - 1)
        def _():
            loss_ref[...] = jnp.full(loss_ref.shape, (0.5 / D) * jnp.sum(acc[...]), F32)

    row = pl.BlockSpec((tb, D), lambda i: (i, 0))
    vec = pl.BlockSpec((1, D), lambda i: (0, 0))
    return pl.pallas_call(
        body, grid=(nb,), in_specs=[row, vec, row], out_specs=[pl.BlockSpec((1, LANES), lambda i: (0, 0)), row, vec],
        out_shape=[SDS((1, LANES), F32), SDS((T, D), F32), SDS((1, D), F32)], scratch_shapes=[pltpu.VMEM((1, D), F32)],
        compiler_params=_params("arbitrary"), name=name)(x, g.reshape(1, D), target)


def _ln_silu_fwd(cv, g, b, name):
    T, C = cv.shape
    tb = _tile(T, ROW_BLOCK, 8)

    def body(x_ref, g_ref, b_ref, o_ref):
        o_ref[...] = _ln_silu(x_ref[...], g_ref[...], b_ref[...]).astype(BF)

    row = pl.BlockSpec((tb, C), lambda i: (i, 0))
    vec = pl.BlockSpec((1, C), lambda i: (0, 0))
    return pl.pallas_call(body, grid=(T // tb,), in_specs=[row, vec, vec], out_specs=row, out_shape=SDS((T, C), BF),
                          compiler_params=_params("parallel"), name=name)(cv, g.reshape(1, C), b.reshape(1, C))


def _ln_silu_bwd(cv, g, b, dy, name):
    T, C = cv.shape
    tb = _tile(T, ROW_BLOCK, 8)

    def body(x_ref, g_ref, b_ref, dy_ref, dx_ref, dg_ref, db_ref):
        _, vjp = jax.vjp(_ln_silu, x_ref[...], g_ref[...], b_ref[...])
        dx, dg, db = vjp(dy_ref[...].astype(F32))
        dx_ref[...] = dx

        @pl.when(pl.program_id(0) == 0)
        def _():
            dg_ref[...] = jnp.zeros_like(dg_ref)
            db_ref[...] = jnp.zeros_like(db_ref)

        dg_ref[...] += dg
        db_ref[...] += db

    row = pl.BlockSpec((tb, C), lambda i: (i, 0))
    vec = pl.BlockSpec((1, C), lambda i: (0, 0))
    return pl.pallas_call(
        body, grid=(T // tb,), in_specs=[row, vec, vec, row], out_specs=[row, vec, vec],
        out_shape=[SDS((T, C), F32), SDS((1, C), F32), SDS((1, C), F32)], compiler_params=_params("arbitrary"),
        name=name)(cv, g.reshape(1, C), b.reshape(1, C), dy)


def _merge_fwd(proj, yc, yp, ps, C, name):
    T, D = yc.shape
    tb = _tile(T, ROW_BLOCK, 8)
    nj = D // C

    def body(gc_ref, gp_ref, yc_ref, yp_ref, ps_ref, o_ref):
        o_ref[...] = _merge(gc_ref[...], gp_ref[...], yc_ref[...], yp_ref[...], ps_ref[...]).astype(BF)

    blk = pl.BlockSpec((tb, C), lambda i, j: (i, j))
    return pl.pallas_call(
        body, grid=(T // tb, nj),
        in_specs=[pl.BlockSpec((tb, C), lambda i, j: (i, 3 + j)), pl.BlockSpec((tb, C), lambda i, j: (i, 3 + nj + j)), blk, blk,
                  pl.BlockSpec((1, C), lambda i, j: (0, j))],
        out_specs=blk, out_shape=SDS((T, D), BF), compiler_params=_params("parallel", "parallel"), name=name)(proj, proj, yc, yp, ps.reshape(1, D))


def _merge_bwd(proj, yc, yp, ps, dm, C, name):
    T, D = yc.shape
    tb = _tile(T, ROW_BLOCK, 8)
    nj = D // C

    def body(gc_ref, gp_ref, yc_ref, yp_ref, ps_ref, dm_ref, dgc_ref, dgp_ref, dyc_ref, dyp_ref, dps_ref):
        _, vjp = jax.vjp(_merge, gc_ref[...], gp_ref[...], yc_ref[...], yp_ref[...], ps_ref[...])
        dgc, dgp, dyc, dyp, dps = vjp(dm_ref[...].astype(F32))
        dgc_ref[...] = dgc.astype(BF)
        dgp_ref[...] = dgp.astype(BF)
        dyc_ref[...] = dyc.astype(BF)
        dyp_ref[...] = dyp.astype(BF)

        @pl.when(pl.program_id(1) == 0)
        def _():
            dps_ref[...] = jnp.zeros_like(dps_ref)

        dps_ref[...] += dps

    blk = pl.BlockSpec((tb, C), lambda j, i: (i, j))
    vec = pl.BlockSpec((1, C), lambda j, i: (0, j))
    return pl.pallas_call(
        body, grid=(nj, T // tb),
        in_specs=[pl.BlockSpec((tb, C), lambda j, i: (i, 3 + j)), pl.BlockSpec((tb, C), lambda j, i: (i, 3 + nj + j)), blk, blk, vec, blk],
        out_specs=[blk, blk, blk, blk, vec], out_shape=[SDS((T, D), BF)] * 4 + [SDS((1, D), F32)],
        compiler_params=_params("parallel", "arbitrary"), name=name)(proj, proj, yc, yp, ps.reshape(1, D), dm)


def _shd(v, s, rows):
    if s == 0:
        return v
    return jnp.where(rows >= s, pltpu.roll(v, s, 0), 0.0)


def _shu(v, s, rows):
    if s == 0:
        return v
    n = v.shape[0]
    return jnp.where(rows < n - s, pltpu.roll(v, n - s, 0), 0.0)


def _glu_conv_fwd(proj, w, b, Bn, S, C, name):
    K = w.shape[0]
    sl = min(LANES, C)
    ns = C // sl

    def body(a_ref, gl_ref, w_ref, b_ref, o_ref):
        y0 = a_ref[...] * jax.nn.sigmoid(gl_ref[...])
        rows = lax.broadcasted_iota(jnp.int32, y0.shape, 0)
        acc = jnp.zeros_like(y0) + b_ref[...]
        for k in range(K):
            acc = acc + w_ref[k:k + 1, :] * _shd(y0, K - 1 - k, rows)
        o_ref[...] = acc

    return pl.pallas_call(
        body, grid=(Bn, ns),
        in_specs=[pl.BlockSpec((S, sl), lambda bi, j: (bi, j)), pl.BlockSpec((S, sl), lambda bi, j: (bi, ns + j)),
                  pl.BlockSpec((K, sl), lambda bi, j: (0, j)), pl.BlockSpec((1, sl), lambda bi, j: (0, j))],
        out_specs=pl.BlockSpec((S, sl), lambda bi, j: (bi, j)), out_shape=SDS((Bn * S, C), F32),
        compiler_params=_params("parallel", "parallel"), name=name)(proj, proj, w, b.reshape(1, C))


def _glu_conv_bwd(proj, w, dcv, Bn, S, C, name):
    K = w.shape[0]
    sl = min(LANES, C)
    ns = C // sl

    def body(a_ref, gl_ref, w_ref, d_ref, da_ref, dgl_ref, dw_ref, db_ref):
        a = a_ref[...]
        sg = jax.nn.sigmoid(gl_ref[...])
        y0 = a * sg
        d = d_ref[...]
        rows = lax.broadcasted_iota(jnp.int32, y0.shape, 0)

        @pl.when(pl.program_id(1) == 0)
        def _():
            dw_ref[...] = jnp.zeros_like(dw_ref)
            db_ref[...] = jnp.zeros_like(db_ref)

        dy0 = jnp.zeros_like(y0)
        for k in range(K):
            s = K - 1 - k
            dw_ref[k:k + 1, :] += jnp.sum(d * _shd(y0, s, rows), axis=0, keepdims=True)
            dy0 = dy0 + w_ref[k:k + 1, :] * _shu(d, s, rows)
        db_ref[...] += jnp.sum(d, axis=0, keepdims=True)
        da_ref[...] = (dy0 * sg).astype(BF)
        dgl_ref[...] = (dy0 * a * sg * (1.0 - sg)).astype(BF)

    blk = pl.BlockSpec((S, sl), lambda j, bi: (bi, j))
    return pl.pallas_call(
        body, grid=(ns, Bn),
        in_specs=[blk, pl.BlockSpec((S, sl), lambda j, bi: (bi, ns + j)), pl.BlockSpec((K, sl), lambda j, bi: (0, j)), blk],
        out_specs=[blk, blk, pl.BlockSpec((K, sl), lambda j, bi: (0, j)), pl.BlockSpec((1, sl), lambda j, bi: (0, j))],
        out_shape=[SDS((Bn * S, C), BF), SDS((Bn * S, C), BF), SDS((K, C), F32), SDS((1, C), F32)],
        compiler_params=_params("parallel", "arbitrary"), name=name)(proj, proj, w, dcv)


def _pool_z(u, g, rows):
    s2 = u + _shd(u, 1, rows)
    s4 = s2 + _shd(s2, 2, rows)
    s8 = s4 + _shd(s4, 4, rows)
    s16 = s8 + _shd(s8, 8, rows)
    sw = jnp.where(g == 0, s2, jnp.where(g == 1, s4, jnp.where(g == 2, s8, s16)))
    cnt = jnp.minimum(rows + 1, POOL_WINDOWS[0] << g).astype(F32)
    return sw / cnt - u, cnt


def _pool_fwd(proj, wpt, l, Bn, S, C, D, name):
    G = len(POOL_WINDOWS)
    gd, go = C // G, D // G

    def body(u_ref, w_ref, o_ref):
        g = pl.program_id(1)
        u = u_ref[...]
        rows = lax.broadcasted_iota(jnp.int32, u.shape, 0)
        zp, _ = _pool_z(u, g, rows)
        o_ref[...] = lax.dot_general(zp.astype(BF), w_ref[...], _DN["nt"], preferred_element_type=F32)

    return pl.pallas_call(
        body, grid=(Bn, G),
        in_specs=[pl.BlockSpec((S, gd), lambda bi, g: (bi, 2 * G + g)), pl.BlockSpec((None, go, gd), lambda bi, g: (l * G + g, 0, 0))],
        out_specs=pl.BlockSpec((S, go), lambda bi, g: (bi, g)), out_shape=SDS((Bn * S, D), F32),
        compiler_params=_params("parallel", "parallel"), name=name)(proj, wpt)


def _pool_bwd(proj, wpt, dyp, l, Bn, S, C, D, name):
    G = len(POOL_WINDOWS)
    gd, go = C // G, D // G

    def body(u_ref, w_ref, d_ref, du_ref, dw_ref):
        g = pl.program_id(0)
        u = u_ref[...]
        rows = lax.broadcasted_iota(jnp.int32, u.shape, 0)
        zp, cnt = _pool_z(u, g, rows)
        d = d_ref[...]
        dzp = lax.dot_general(d, w_ref[...], _DN["nn"], preferred_element_type=F32)

        @pl.when(pl.program_id(1) == 0)
        def _():
            dw_ref[...] = jnp.zeros_like(dw_ref)

        dw_ref[...] += lax.dot_general(d, zp.astype(BF), _DN["tn"], preferred_element_type=F32)
        dsw = dzp / cnt
        zero = jnp.zeros_like(dsw)
        d16 = jnp.where(g == 3, dsw, zero)
        d8 = jnp.where(g == 2, dsw, zero) + d16 + _shu(d16, 8, rows)
        d4 = jnp.where(g == 1, dsw, zero) + d8 + _shu(d8, 4, rows)
        d2 = jnp.where(g == 0, dsw, zero) + d4 + _shu(d4, 2, rows)
        d1 = d2 + _shu(d2, 1, rows)
        du_ref[...] = (d1 - dzp).astype(BF)

    return pl.pallas_call(
        body, grid=(G, Bn),
        in_specs=[pl.BlockSpec((S, gd), lambda g, bi: (bi, 2 * G + g)), pl.BlockSpec((None, go, gd), lambda g, bi: (l * G + g, 0, 0)),
                  pl.BlockSpec((S, go), lambda g, bi: (bi, g))],
        out_specs=[pl.BlockSpec((S, gd), lambda g, bi: (bi, g)), pl.BlockSpec((None, go, gd), lambda g, bi: (g, 0, 0))],
        out_shape=[SDS((Bn * S, C), BF), SDS((G, go, gd), F32)],
        compiler_params=_params("parallel", "arbitrary"), name=name)(proj, wpt, dyp)


def _ffn_conv(u, w_ref, rows):
    K = w_ref.shape[0]
    acc = w_ref[K - 1:K, :] * u
    for k in range(K - 1):
        acc = acc + w_ref[k:k + 1, :] * _shd(u, K - 1 - k, rows)
    return acc


def _ffn_cb(F):
    return _tile(F, 256)


def _ffn_act_fwd(up0, w, Bn, S, F, name):
    cb = _ffn_cb(F)
    nj = F // cb

    def body(g_ref, v_ref, wg_ref, wv_ref, o_ref):
        rows = lax.broadcasted_iota(jnp.int32, g_ref.shape, 0)
        o_ref[...] = _gated(_ffn_conv(g_ref[...], wg_ref, rows), _ffn_conv(v_ref[...], wv_ref, rows)).astype(BF)

    K = w.shape[0]
    return pl.pallas_call(
        body, grid=(Bn, nj),
        in_specs=[pl.BlockSpec((S, cb), lambda bi, j: (bi, j)), pl.BlockSpec((S, cb), lambda bi, j: (bi, nj + j)),
                  pl.BlockSpec((K, cb), lambda bi, j: (0, j)), pl.BlockSpec((K, cb), lambda bi, j: (0, nj + j))],
        out_specs=pl.BlockSpec((S, cb), lambda bi, j: (bi, j)), out_shape=SDS((Bn * S, F), BF),
        compiler_params=_params("parallel", "parallel"), name=name)(up0, up0, w, w)


def _ffn_act_bwd(up0, w, dg, Bn, S, F, name):
    cb = _ffn_cb(F)
    nj = F // cb
    K = w.shape[0]

    def body(g_ref, v_ref, wg_ref, wv_ref, d_ref, dgo_ref, dvo_ref, dwg_ref, dwv_ref):
        rows = lax.broadcasted_iota(jnp.int32, g_ref.shape, 0)
        g0, v0 = g_ref[...], v_ref[...]
        _, vjp = jax.vjp(_gated, _ffn_conv(g0, wg_ref, rows), _ffn_conv(v0, wv_ref, rows))
        dgc, dvc = vjp(d_ref[...].astype(F32))

        @pl.when(pl.program_id(1) == 0)
        def _():
            dwg_ref[...] = jnp.zeros_like(dwg_ref)
            dwv_ref[...] = jnp.zeros_like(dwv_ref)

        for u0, dc, w_ref, dw_ref, do_ref in ((g0, dgc, wg_ref, dwg_ref, dgo_ref), (v0, dvc, wv_ref, dwv_ref, dvo_ref)):
            du = jnp.zeros_like(u0)
            for k in range(K):
                s = K - 1 - k
                dw_ref[k:k + 1, :] += jnp.sum(dc * _shd(u0, s, rows), axis=0, keepdims=True)
                du = du + w_ref[k:k + 1, :] * _shu(dc, s, rows)
            do_ref[...] = du.astype(BF)

    blk = pl.BlockSpec((S, cb), lambda j, bi: (bi, j))
    wblk = pl.BlockSpec((K, cb), lambda j, bi: (0, j))
    return pl.pallas_call(
        body, grid=(nj, Bn),
        in_specs=[blk, pl.BlockSpec((S, cb), lambda j, bi: (bi, nj + j)), wblk, pl.BlockSpec((K, cb), lambda j, bi: (0, nj + j)), blk],
        out_specs=[blk, blk, wblk, wblk],
        out_shape=[SDS((Bn * S, F), BF), SDS((Bn * S, F), BF), SDS((K, F), F32), SDS((K, F), F32)],
        compiler_params=_params("parallel", "arbitrary"), name=name)(up0, up0, w, w, dg)


def _softmax_rows(q, k, scale):
    sc = lax.dot_general(q, k, _DN["nt"], preferred_element_type=F32) * scale
    e = jnp.exp(sc - jnp.max(sc, axis=-1, keepdims=True))
    return e / jnp.sum(e, axis=-1, keepdims=True)


def _attn_ts(S):
    return _tile(S, 1024, 8)


def _attn_fwd(q, kv, Bn, S, Mn, D, name):
    H = XA_HEADS
    dh = D // H
    ts = _attn_ts(S)
    nsb = S // ts
    scale = dh ** -0.5

    def body(q_ref, k_ref, v_ref, o_ref):
        p = _softmax_rows(q_ref[...], k_ref[...], scale)
        o_ref[...] = lax.dot_general(p.astype(BF), v_ref[...], _DN["nn"], preferred_element_type=F32).astype(BF)

    qblk = pl.BlockSpec((ts, dh), lambda bi, h, s: (bi * nsb + s, h))
    return pl.pallas_call(
        body, grid=(Bn, H, nsb),
        in_specs=[qblk, pl.BlockSpec((Mn, dh), lambda bi, h, s: (bi, h)), pl.BlockSpec((Mn, dh), lambda bi, h, s: (bi, H + h))],
        out_specs=qblk, out_shape=SDS((Bn * S, D), BF), compiler_params=_params("parallel", "parallel", "parallel"), name=name)(q, kv, kv)


def _attn_bwd(q, kv, datt, Bn, S, Mn, D, name):
    H = XA_HEADS
    dh = D // H
    ts = _attn_ts(S)
    nsb = S // ts
    scale = dh ** -0.5

    def body(q_ref, k_ref, v_ref, do_ref, dq_ref, dk_ref, dv_ref):
        q, k, v, do = q_ref[...], k_ref[...], v_ref[...], do_ref[...]
        p = _softmax_rows(q, k, scale)
        dp = lax.dot_general(do, v, _DN["nt"], preferred_element_type=F32)
        ds = (p * (dp - jnp.sum(dp * p, axis=-1, keepdims=True)) * scale).astype(BF)
        dq_ref[...] = lax.dot_general(ds, k, _DN["nn"], preferred_element_type=F32).astype(BF)

        @pl.when(pl.program_id(2) == 0)
        def _():
            dk_ref[...] = jnp.zeros_like(dk_ref)
            dv_ref[...] = jnp.zeros_like(dv_ref)

        dk_ref[...] += lax.dot_general(ds, q, _DN["tn"], preferred_element_type=F32)
        dv_ref[...] += lax.dot_general(p.astype(BF), do, _DN["tn"], preferred_element_type=F32)

    qblk = pl.BlockSpec((ts, dh), lambda bi, h, s: (bi * nsb + s, h))
    kblk = pl.BlockSpec((Mn, dh), lambda bi, h, s: (bi, h))
    return pl.pallas_call(
        body, grid=(Bn, H, nsb),
        in_specs=[qblk, kblk, pl.BlockSpec((Mn, dh), lambda bi, h, s: (bi, H + h)), qblk],
        out_specs=[qblk, kblk, kblk], out_shape=[SDS((Bn * S, D), BF), SDS((Bn * Mn, D), F32), SDS((Bn * Mn, D), F32)],
        compiler_params=_params("parallel", "parallel", "arbitrary"), name=name)(q, kv, kv, datt)


class _Sides:
    def __init__(self, by_key=None):
        self.by_key, self.landed = dict(by_key or {}), {}

    def mm(self, key, *args, **kw):
        side = self.by_key.get(key)
        if side is None:
            return _mm(*args, **kw)
        out, self.landed[key] = _mm(*args, side=side, **kw)
        return out


def _layer_fwd(x, mem_n, W, V, l, dims, sides):
    Bn, S, Mn, D, C, F = dims
    n = f"l{l}_"
    h = _rms_fwd(x, V["mix_norm_g"][l], n + "mix_norm")
    proj = sides.mm("proj", h, W["w_in"], "nn", F32, n + "proj", bl=0)
    cv = _glu_conv_fwd(proj, V["conv_dw_w"][l], V["conv_dw_b"][l], Bn, S, C, n + "glu_conv")
    yc1 = _ln_silu_fwd(cv, V["conv_ln_g"][l], V["conv_ln_b"][l], n + "ln_silu")
    yc = sides.mm("conv_out", yc1, W["w_conv_out"], "nn", F32, n + "conv_out", bl=0)
    yp = _pool_fwd(proj, W["w_pool"], 0, Bn, S, C, D, n + "pool")
    merged = _merge_fwd(proj, yc, yp, V["pool_scale"][l], C, n + "merge")
    x1 = sides.mm("out_proj", merged, W["w_out"], "nn", F32, n + "out_proj", res=x, bl=0)
    hq = _rms_fwd(x1, V["xattn_norm_g"][l], n + "xattn_norm")
    q = sides.mm("q_proj", hq, W["w_q"], "nn", BF, n + "q_proj", bl=0)
    kv = _mm(mem_n, W["w_kv"], "nn", BF, n + "kv_proj", bl=0)
    att = _attn_fwd(q, kv, Bn, S, Mn, D, n + "attn")
    x2 = sides.mm("o_proj", att, W["w_o"], "nn", F32, n + "o_proj", res=x1, bl=0)
    hf = _rms_fwd(x2, V["ffn_norm_g"][l], n + "ffn_norm")
    up0 = sides.mm("up_proj", hf, W["w_up"], "nn", F32, n + "up_proj", bl=0)
    gact = _ffn_act_fwd(up0, V["ffn_dw_w"][l], Bn, S, F, n + "ffn_act")
    x3 = sides.mm("down_proj", gact, W["w_down"], "nn", F32, n + "down_proj", res=x2, bl=0)
    return x3, dict(x=x, h=h, proj=proj, cv=cv, yc1=yc1, yc=yc, yp=yp, merged=merged, x1=x1, hq=hq, q=q, kv=kv, att=att, x2=x2, hf=hf,
                    up0=up0, gact=gact)


def _layer_bwd_mlp(dx, dxb, sv, W, V, l, dims, sides):
    Bn, S, Mn, D, C, F = dims
    n = f"l{l}_b_"
    gw, sm = {}, {}
    dgact = sides.mm("d_gact", dxb, W["w_down"], "nt", BF, n + "d_gact", bl=0)
    gw["w_down"] = sides.mm("dw_down", sv["gact"], dxb, "tn", F32, n + "dw_down")
    dg0, dv0, dwg, dwv = _ffn_act_bwd(sv["up0"], V["ffn_dw_w"][l], dgact, Bn, S, F, n + "ffn_act")
    sm["ffn_dw_w"] = jnp.concatenate([dwg, dwv], axis=1)
    dup0 = jnp.concatenate([dg0, dv0], axis=1)
    dhf = sides.mm("d_hf", dup0, W["w_up"], "nt", F32, n + "d_hf", bl=0)
    gw["w_up"] = sides.mm("dw_up", sv["hf"], dup0, "tn", F32, n + "dw_up")
    dx2, dx2b, sm["ffn_norm_g"] = _rms_bwd(sv["x2"], V["ffn_norm_g"][l], dhf, dx, n + "ffn_norm")
    return dx2, dx2b, gw, sm


def _layer_bwd_mix(dx2, dx2b, dmem_n, sv, mem_n, W, V, l, dims, sides):
    Bn, S, Mn, D, C, F = dims
    n = f"l{l}_b_"
    gw, sm = {}, {}
    datt = _mm(dx2b, W["w_o"], "nt", BF, n + "d_att", bl=0)
    gw["w_o"] = _mm(sv["att"], dx2b, "tn", F32, n + "dw_o")
    dq, dk, dv = _attn_bwd(sv["q"], sv["kv"], datt, Bn, S, Mn, D, n + "attn")
    dkv = jnp.concatenate([dk, dv], axis=1)
    gw["w_kv"] = _mm(mem_n, dkv, "tn", F32, n + "dw_kv")
    dmem_n = _mm(dkv, W["w_kv"], "nt", F32, n + "d_mem", res=dmem_n, bl=0)
    dhq = _mm(dq, W["w_q"], "nt", F32, n + "d_hq", bl=0)
    gw["w_q"] = _mm(sv["hq"], dq, "tn", F32, n + "dw_q")
    dx1, dx1b, sm["xattn_norm_g"] = _rms_bwd(sv["x1"], V["xattn_norm_g"][l], dhq, dx2, n + "xattn_norm")
    dmerged = _mm(dx1b, W["w_out"], "nt", BF, n + "d_merged", bl=0)
    gw["w_out"] = _mm(sv["merged"], dx1b, "tn", F32, n + "dw_out")
    dgc, dgp, dyc, dyp, sm["pool_scale"] = _merge_bwd(sv["proj"], sv["yc"], sv["yp"], V["pool_scale"][l], dmerged, C, n + "merge")
    du, gw["w_pool"] = _pool_bwd(sv["proj"], W["w_pool"], dyp, 0, Bn, S, C, D, n + "pool")
    dyc1 = _mm(dyc, W["w_conv_out"], "nt", F32, n + "d_yc1", bl=0)
    gw["w_conv_out"] = _mm(sv["yc1"], dyc, "tn", F32, n + "dw_conv_out")
    dcv, sm["conv_ln_g"], sm["conv_ln_b"] = _ln_silu_bwd(sv["cv"], V["conv_ln_g"][l], V["conv_ln_b"][l], dyc1, n + "ln_silu")
    da, dgl, sm["conv_dw_w"], sm["conv_dw_b"] = _glu_conv_bwd(sv["proj"], V["conv_dw_w"][l], dcv, Bn, S, C, n + "glu_conv")
    dproj = jnp.concatenate([da, dgl, du, dgc, dgp], axis=1)
    dh = sides.mm("d_h", dproj, W["w_in"], "nt", F32, n + "d_h", bl=0)
    gw["w_in"] = sides.mm("dw_in", sv["h"], dproj, "tn", F32, n + "dw_in")
    dx, dxb, sm["mix_norm_g"] = _rms_bwd(sv["x"], V["mix_norm_g"][l], dh, dx1, n + "mix_norm")
    return dx, dxb, dmem_n, gw, sm


BIG = (("w_in", "col"), ("w_conv_out", "col"), ("w_pool", "row"), ("w_out", "row"), ("w_q", "row"), ("w_kv", "col"),
       ("w_o", "row"), ("w_up", "col"), ("w_down", "row"))
FWD_CARRY = {"proj": ("w_in",), "conv_out": ("w_conv_out", "w_pool"), "out_proj": ("w_out", "w_q"), "q_proj": ("w_o",), "o_proj": ("w_kv",),
             "up_proj": ("w_up",), "down_proj": ("w_down",)}
EARLY = ("w_down", "w_up")
BWD_CARRY_EARLY = {"d_h": ("w_up",), "dw_in": ("w_down",)}
BWD_CARRY_LATE = {"d_hf": ("w_in",), "dw_up": ("w_conv_out", "w_pool", "w_out", "w_q", "w_kv", "w_o")}


def _place():
    xi, yi, ci = lax.axis_index("x"), lax.axis_index("y"), lax.axis_index("c")
    return xi, yi, ci, 2 * xi + yi


def _chip_peer(xi, yi, ci, r):
    return (xi ^ (r >> 1), yi ^ (r & 1), ci)


def _full_shard(ref, kind, k, cs):
    if kind == "col":
        return ref.at[:, :, :, :, pl.ds(pl.multiple_of(k * cs, cs), cs)]
    return ref.at[:, :, k]


def _gather_weights(shards, kinds):
    n = len(shards)
    outs = []
    for s, kind in zip(shards, kinds):
        L, P, _, RH, CS = s.shape
        outs.append(SDS((L, P, 2, RH, CS * N_CHIPS) if kind == "col" else (L, P, N_CHIPS, 2, RH, CS), s.dtype))
    per = 7

    def body(*refs):
        srcs, fulls, (ssem, rsem) = refs[:n], refs[n:2 * n], refs[2 * n:]
        xi, yi, ci, j = _place()
        sib = (xi, yi, 1 - ci)

        def piece(i, k, c):
            kind, cs = kinds[i], shards[i].shape[-1]
            if kind == "col":
                return fulls[i].at[:, :, c, :, pl.ds(pl.multiple_of(k * cs, cs), cs)]
            return fulls[i].at[:, :, k, c]

        def copy(i, slot, src, dst, dev):
            return pltpu.make_async_remote_copy(src_ref=src, dst_ref=dst, send_sem=ssem.at[per * i + slot], recv_sem=rsem.at[per * i + slot],
                                                device_id=dev, device_id_type=MESH)

        own, first, passed = [], [], []
        for i in range(n):
            for r in (1, 2, 3):
                first.append(copy(i, r - 1, srcs[i].at[:, :, ci], piece(i, j, ci), _chip_peer(xi, yi, ci, r)))
                first[-1].start()
        for i in range(n):
            own.append(copy(i, 6, srcs[i], _full_shard(fulls[i], kinds[i], j, shards[i].shape[-1]), sib))
            own[-1].start()
        for i in range(n):
            for r in (1, 2, 3):
                got = piece(i, j ^ r, ci)
                copy(i, r - 1, got, got, sib).wait_recv()
                passed.append(copy(i, 2 + r, got, got, sib))
                passed[-1].start()
        for i in range(n):
            for r in (1, 2, 3):
                got = piece(i, j ^ r, 1 - ci)
                copy(i, 2 + r, got, got, sib).wait_recv()
        for cp in own:
            cp.wait()
        for cp in first + passed:
            cp.wait_send()

    return pl.pallas_call(
        body, in_specs=[ANY] * n, out_specs=[ANY] * n, out_shape=outs,
        scratch_shapes=[pltpu.SemaphoreType.DMA((per * n,)), pltpu.SemaphoreType.DMA((per * n,))], name="gather_weights")(*shards)


def _full_sds(s, kind):
    L, P, _, RH, CS = s.shape
    return SDS((L, P, 2, RH, CS * N_CHIPS) if kind == "col" else (L, P, N_CHIPS, 2, RH, CS), s.dtype)


def _gather_piece(full, kind, cs, k, c):
    if kind == "col":
        return full.at[:, :, c, :, pl.ds(pl.multiple_of(k * cs, cs), cs)]
    return full.at[:, :, k, c]


def _side_gather(shards, kinds):
    n = len(shards)

    def make(srcs, fulls, ssem, rsem):
        xi, yi, ci, j = _place()
        return [pltpu.make_async_remote_copy(
            src_ref=srcs[i].at[:, :, ci], dst_ref=_gather_piece(fulls[i], kinds[i], shards[i].shape[-1], j, ci), send_sem=ssem.at[3 * i + r - 1],
            recv_sem=rsem.at[3 * i + r - 1], device_id=_chip_peer(xi, yi, ci, r), device_id_type=MESH) for i in range(n) for r in (1, 2, 3)]

    return _Side(shards, [_full_sds(s, k) for s, k in zip(shards, kinds)], 3 * n, make)


def _gather_pass(fulls, shards, kinds, name):
    n = len(fulls)

    def body(*refs):
        srcs, outs, (ssem, rsem) = refs[n:2 * n], refs[2 * n:3 * n], refs[3 * n:]
        xi, yi, ci, j = _place()
        sib = (xi, yi, 1 - ci)
        cps = []
        for i in range(n):
            cs = shards[i].shape[-1]
            for r in (1, 2, 3):
                got = _gather_piece(outs[i], kinds[i], cs, j ^ r, ci)
                cps.append(pltpu.make_async_remote_copy(src_ref=got, dst_ref=got, send_sem=ssem.at[4 * i + r - 1], recv_sem=rsem.at[4 * i + r - 1],
                                                        device_id=sib, device_id_type=MESH))
            cps.append(pltpu.make_async_remote_copy(src_ref=srcs[i], dst_ref=_full_shard(outs[i], kinds[i], j, cs), send_sem=ssem.at[4 * i + 3],
                                                    recv_sem=rsem.at[4 * i + 3], device_id=sib, device_id_type=MESH))
        for cp in cps:
            cp.start()
        for cp in cps:
            cp.wait()

    return pl.pallas_call(
        body, in_specs=[ANY] * (2 * n), out_specs=[ANY] * n, out_shape=[SDS(f.shape, f.dtype) for f in fulls],
        input_output_aliases={i: i for i in range(n)},
        scratch_shapes=[pltpu.SemaphoreType.DMA((4 * n,)), pltpu.SemaphoreType.DMA((4 * n,))], name=name)(*fulls, *shards)


def _sibling_exchange(gviews, kinds, name):
    n = len(gviews)
    outs = [SDS(g.shape[:1] + g.shape[2:] if kind == "col" else g.shape[:2] + g.shape[3:], g.dtype) for g, kind in zip(gviews, kinds)]

    def body(*refs):
        gs, lands, (ssem, rsem) = refs[:n], refs[n:2 * n], refs[2 * n:]
        xi, yi, ci, _ = _place()
        cps = []
        for i in range(n):
            src = gs[i].at[:, 1 - ci] if kinds[i] == "col" else gs[i].at[:, :, 1 - ci]
            cps.append(pltpu.make_async_remote_copy(src_ref=src, dst_ref=lands[i], send_sem=ssem.at[i], recv_sem=rsem.at[i],
                                                    device_id=(xi, yi, 1 - ci), device_id_type=MESH))
            cps[-1].start()
        for cp in cps:
            cp.wait()

    return pl.pallas_call(body, in_specs=[ANY] * n, out_specs=[ANY] * n, out_shape=outs,
                          scratch_shapes=[pltpu.SemaphoreType.DMA((n,)), pltpu.SemaphoreType.DMA((n,))], name=name)(*gviews)


def _chip_sum(g, land, kind, jc, name):
    if kind == "col":
        P, _, RH, C = g.shape
        CS = C // N_CHIPS
        g_spec = pl.BlockSpec((None, None, RH, CS), lambda p, r, jc: (p, jc[1], 0, jc[0] ^ r))
        l_spec = pl.BlockSpec((None, RH, CS), lambda p, r, jc: (p, 0, jc[0] ^ r))
    else:
        P, _, _, RH, CS = g.shape
        g_spec = pl.BlockSpec((None, None, None, RH, CS), lambda p, r, jc: (p, jc[0] ^ r, jc[1], 0, 0))
        l_spec = pl.BlockSpec((None, None, RH, CS), lambda p, r, jc: (p, jc[0] ^ r, 0, 0))

    def body(jc_ref, g_ref, l_ref, own_ref, all_ref):
        s = g_ref[...] + l_ref[...]
        all_ref[...] = s.astype(BF)

        @pl.when(pl.program_id(1) == 0)
        def _():
            own_ref[...] = s

    return pl.pallas_call(
        body, grid_spec=pltpu.PrefetchScalarGridSpec(
            num_scalar_prefetch=1, grid=(P, N_CHIPS), in_specs=[g_spec, l_spec],
            out_specs=[pl.BlockSpec((None, RH, CS), lambda p, r, jc: (p, 0, 0)), pl.BlockSpec((None, None, RH, CS), lambda p, r, jc: (r, p, 0, 0))]),
        out_shape=[SDS((P, RH, CS), F32), SDS((N_CHIPS, P, RH, CS), BF)], compiler_params=_params("parallel", "arbitrary"), name=name)(jc, g, land)


def _chip_exchange_copies(srcs, lands, ssem, rsem):
    xi, yi, ci, _ = _place()
    return [pltpu.make_async_remote_copy(src_ref=srcs[i].at[r], dst_ref=lands[i].at[r], send_sem=ssem.at[3 * i + r - 1],
                                         recv_sem=rsem.at[3 * i + r - 1], device_id=_chip_peer(xi, yi, ci, r), device_id_type=MESH)
            for i in range(len(srcs)) for r in (1, 2, 3)]


def _side_chip_exchange(pieces):
    return _Side(pieces, [SDS(p.shape, p.dtype) for p in pieces], 3 * len(pieces), _chip_exchange_copies)


def _chip_exchange(pieces):
    n = len(pieces)

    def body(*refs):
        cps = _chip_exchange_copies(refs[:n], refs[n:2 * n], *refs[2 * n:])
        for cp in cps:
            cp.start()
        for cp in cps:
            cp.wait()

    return pl.pallas_call(body, in_specs=[ANY] * n, out_specs=[ANY] * n, out_shape=[SDS(p.shape, p.dtype) for p in pieces],
                          scratch_shapes=[pltpu.SemaphoreType.DMA((3 * n,)), pltpu.SemaphoreType.DMA((3 * n,))], name="grad_chip_exchange")(*pieces)


def _final_sum(own, land, jc, shard, l, L, name):
    P, RH, CS = own.shape

    def body(jc_ref, o_ref, a_ref, b_ref, c_ref, *rest):
        rest[-1][...] = ((o_ref[...] + a_ref[...].astype(F32)) + b_ref[...].astype(F32)) + c_ref[...].astype(F32)

    blk = pl.BlockSpec((None, RH, CS), lambda p, jc: (p, 0, 0))
    in_specs = [blk] + [pl.BlockSpec((None, None, RH, CS), functools.partial(lambda r, p, jc: (r, p, 0, 0), r)) for r in (1, 2, 3)]
    args = [jc, own, land, land, land]
    if shard is not None:
        in_specs.append(ANY)
        args.append(shard)
    return pl.pallas_call(
        body, grid_spec=pltpu.PrefetchScalarGridSpec(
            num_scalar_prefetch=1, grid=(P,), in_specs=in_specs,
            out_specs=pl.BlockSpec((None, None, None, RH, CS), lambda p, jc: (l, p, jc[1], 0, 0))),
        out_shape=SDS((L, P, 2, RH, CS), F32), input_output_aliases={5: 0} if shard is not None else {},
        compiler_params=_params("arbitrary"), name=name)(*args)


def _halves_exchange(shards):
    n = len(shards)

    def body(*refs):
        outs, (ssem, rsem) = refs[n:2 * n], refs[2 * n:]
        xi, yi, ci, _ = _place()
        cps = []
        for i in range(n):
            mine = outs[i].at[:, :, ci]
            cps.append(pltpu.make_async_remote_copy(src_ref=mine, dst_ref=mine, send_sem=ssem.at[i], recv_sem=rsem.at[i],
                                                    device_id=(xi, yi, 1 - ci), device_id_type=MESH))
            cps[-1].start()
        for i in range(n):
            land = outs[i].at[:, :, 1 - ci]
            pltpu.make_async_remote_copy(src_ref=land, dst_ref=land, send_sem=ssem.at[i], recv_sem=rsem.at[i],
                                         device_id=(xi, yi, 1 - ci), device_id_type=MESH).wait_recv()
        for cp in cps:
            cp.wait_send()

    return pl.pallas_call(body, in_specs=[ANY] * n, out_specs=[ANY] * n, out_shape=[SDS(s.shape, s.dtype) for s in shards],
                          input_output_aliases={i: i for i in range(n)},
                          scratch_shapes=[pltpu.SemaphoreType.DMA((n,)), pltpu.SemaphoreType.DMA((n,))], name="grad_halves_exchange")(*shards)


def _reduce_small(part):
    NR, Wd = part.shape
    ND = 2 * N_CHIPS

    def body(p_ref, o_ref, land, ssem, rsem):
        xi, yi, ci, j = _place()
        me = 2 * j + ci
        land[me] = p_ref[...]
        cps = []
        for rr in range(1, ND):
            dev = (xi ^ (rr >> 2), yi ^ ((rr >> 1) & 1), ci ^ (rr & 1))
            cps.append(pltpu.make_async_remote_copy(src_ref=p_ref, dst_ref=land.at[me], send_sem=ssem.at[rr - 1], recv_sem=rsem.at[rr - 1],
                                                    device_id=dev, device_id_type=MESH))
            cps[-1].start()
        for rr in range(1, ND):
            got = land.at[me ^ rr]
            pltpu.make_async_remote_copy(src_ref=got, dst_ref=got, send_sem=ssem.at[rr - 1], recv_sem=rsem.at[rr - 1],
                                         device_id=(xi, yi, ci), device_id_type=MESH).wait_recv()
        acc = land[0]
        for d in range(1, ND):
            acc = acc + land[d]
        o_ref[...] = acc
        for cp in cps:
            cp.wait_send()

    vm = pl.BlockSpec(memory_space=pltpu.VMEM)
    return pl.pallas_call(body, in_specs=[vm], out_specs=vm, out_shape=SDS((NR, Wd), F32),
                          scratch_shapes=[pltpu.VMEM((ND, NR, Wd), F32), pltpu.SemaphoreType.DMA((ND - 1,)), pltpu.SemaphoreType.DMA((ND - 1,))],
                          name="small_grad_allreduce")(part)


def _adamw(w, g, m, v, name):
    shape = w.shape
    C = shape[-1]
    R = w.size // C
    tb = _tile(R, max(8, (1 << 18) // C), 8)

    def body(w_ref, g_ref, m_ref, v_ref, d_ref, mo_ref, vo_ref):
        g = g_ref[...]
        m = ADAM_B1 * m_ref[...] + (1.0 - ADAM_B1) * g
        v = ADAM_B2 * v_ref[...] + (1.0 - ADAM_B2) * jnp.square(g)
        m_hat = m / (1.0 - ADAM_B1 ** ADAM_STEP)
        v_hat = v / (1.0 - ADAM_B2 ** ADAM_STEP)
        d_ref[...] = -ADAM_LR * (m_hat / (jnp.sqrt(v_hat) + ADAM_EPS) + ADAM_WD * w_ref[...])
        mo_ref[...] = m
        vo_ref[...] = v

    blk = pl.BlockSpec((tb, C), lambda i: (i, 0))
    outs = pl.pallas_call(body, grid=(R // tb,), in_specs=[blk] * 4, out_specs=[blk] * 3, out_shape=[SDS((R, C), F32)] * 3,
                          compiler_params=_params("parallel"), name=name)(*[t.reshape(R, C) for t in (w, g, m, v)])
    return [t.reshape(shape) for t in outs]


WEIGHTS = ("mix_norm_g", "w_in", "conv_dw_w", "conv_dw_b", "conv_ln_g", "conv_ln_b", "w_conv_out", "w_pool_grp", "pool_scale", "w_out",
           "xattn_norm_g", "mem_norm_g", "w_q", "w_kv", "w_o", "ffn_norm_g", "w_up", "ffn_dw_w", "w_down", "final_norm_g")
VECTORS = ("mix_norm_g", "conv_dw_b", "conv_ln_g", "conv_ln_b", "pool_scale", "xattn_norm_g", "mem_norm_g", "ffn_norm_g", "final_norm_g")


def _shard_view(t, kind):
    L, P, R, C = t.shape
    return t.reshape(L, P, 2, R // 2, C)


def _rows(t, width):
    return t.reshape(-1, width)


def _pack(parts):
    return jnp.concatenate([jnp.pad(p, ((0, (-p.shape[0]) % 8), (0, 0))) for p in parts], axis=0)


def kernel(x, mem, mix_norm_g, w_in, conv_dw_w, conv_dw_b, conv_ln_g, conv_ln_b, w_conv_out, w_pool_grp, pool_scale, w_out, xattn_norm_g, mem_norm_g, w_q, w_kv, w_o, ffn_norm_g, w_up, ffn_dw_w, w_down, final_norm_g, loss_target, m_mix_norm_g, m_w_in, m_conv_dw_w, m_conv_dw_b, m_conv_ln_g, m_conv_ln_b, m_w_conv_out, m_w_pool_grp, m_pool_scale, m_w_out, m_xattn_norm_g, m_mem_norm_g, m_w_q, m_w_kv, m_w_o, m_ffn_norm_g, m_w_up, m_ffn_dw_w, m_w_down, m_final_norm_g, v_mix_norm_g, v_w_in, v_conv_dw_w, v_conv_dw_b, v_conv_ln_g, v_conv_ln_b, v_w_conv_out, v_w_pool_grp, v_pool_scale, v_w_out, v_xattn_norm_g, v_mem_norm_g, v_w_q, v_w_kv, v_w_o, v_ffn_norm_g, v_w_up, v_ffn_dw_w, v_w_down, v_final_norm_g):
    w = dict(mix_norm_g=mix_norm_g, w_in=w_in, conv_dw_w=conv_dw_w, conv_dw_b=conv_dw_b, conv_ln_g=conv_ln_g, conv_ln_b=conv_ln_b,
             w_conv_out=w_conv_out, w_pool_grp=w_pool_grp, pool_scale=pool_scale, w_out=w_out, xattn_norm_g=xattn_norm_g,
             mem_norm_g=mem_norm_g, w_q=w_q, w_kv=w_kv, w_o=w_o, ffn_norm_g=ffn_norm_g, w_up=w_up, ffn_dw_w=ffn_dw_w, w_down=w_down,
             final_norm_g=final_norm_g)
    m = dict(zip(WEIGHTS, (m_mix_norm_g, m_w_in, m_conv_dw_w, m_conv_dw_b, m_conv_ln_g, m_conv_ln_b, m_w_conv_out, m_w_pool_grp, m_pool_scale,
                           m_w_out, m_xattn_norm_g, m_mem_norm_g, m_w_q, m_w_kv, m_w_o, m_ffn_norm_g, m_w_up, m_ffn_dw_w, m_w_down, m_final_norm_g)))
    v = dict(zip(WEIGHTS, (v_mix_norm_g, v_w_in, v_conv_dw_w, v_conv_dw_b, v_conv_ln_g, v_conv_ln_b, v_w_conv_out, v_w_pool_grp, v_pool_scale,
                           v_w_out, v_xattn_norm_g, v_mem_norm_g, v_w_q, v_w_kv, v_w_o, v_ffn_norm_g, v_w_up, v_ffn_dw_w, v_w_down, v_final_norm_g)))
    xi, yi, ci, j = _place()
    jc = jnp.stack([j, ci]).astype(jnp.int32)
    L = w_in.shape[0]
    G = len(POOL_WINDOWS)
    kinds = dict(BIG)

    def to_mat(name, t):
        if name == "w_pool":
            return jnp.swapaxes(t, 2, 3)
        return t[:, None]

    def from_mat(name, t):
        if name == "w_pool":
            return jnp.swapaxes(t, 2, 3)
        return t[:, 0]

    src = {name: w["w_pool_grp" if name == "w_pool" else name] for name, _ in BIG}

    KC, cs_c = conv_dw_w.shape[1], conv_dw_w.shape[2]
    KF, cs_f = ffn_dw_w.shape[1], ffn_dw_w.shape[2]
    taps = jnp.concatenate([conv_dw_w.reshape(L * KC, cs_c), ffn_dw_w.reshape(L * KF * (cs_f // cs_c), cs_c)], axis=0)
    n_taps = taps.shape[0]
    taps = jnp.pad(taps, ((0, (-n_taps) % 16), (0, 0)))
    names = [name for name, _ in BIG]
    mats = {name: to_mat(name, src[name]).astype(BF) for name in names}

    def layer_shards(l, subset):
        return [_shard_view(mats[name][l:l + 1], kinds[name]) for name in subset]

    def as_weights(subset, fulls):
        return {name: f.reshape(G if name == "w_pool" else 1, -1, f.shape[-1]) for name, f in zip(subset, fulls)}

    fulls = _gather_weights(layer_shards(0, names) + [_shard_view(taps[None, None], "row")], [kinds[name] for name in names] + ["row"])
    W = [as_weights(names, fulls[:-1])]
    taps_all = fulls[-1].reshape(N_CHIPS, -1, cs_c)[:, :n_taps]
    V = {name: w[name] for name in VECTORS}
    V["conv_dw_w"] = taps_all[:, :L * KC].reshape(N_CHIPS, L, KC, cs_c).transpose(1, 2, 0, 3).reshape(L, KC, N_CHIPS * cs_c)
    V["ffn_dw_w"] = taps_all[:, L * KC:].reshape(N_CHIPS, L, KF, cs_f).transpose(1, 2, 0, 3).reshape(L, KF, N_CHIPS * cs_f)

    Bn, S, D = x.shape
    Mn = mem.shape[1]
    dims = (Bn, S, Mn, D, conv_dw_b.shape[1], w_down.shape[1] * N_CHIPS)
    xt = x.reshape(Bn * S, D)
    memf = mem.reshape(Bn * Mn, D)
    mem_n = _rms_fwd(memf, V["mem_norm_g"], "mem_norm")
    saved = []
    for l in range(L):
        sides = _Sides()
        if l + 1 < L:
            sides = _Sides({key: _side_gather(layer_shards(l + 1, subset), [kinds[name] for name in subset]) for key, subset in FWD_CARRY.items()})
        xt, sv = _layer_fwd(xt, mem_n, W[l], V, l, dims, sides)
        saved.append(sv)
        if l + 1 < L:
            nxt = {}
            for key, subset in FWD_CARRY.items():
                done = _gather_pass(sides.landed[key], layer_shards(l + 1, subset), [kinds[name] for name in subset], f"gather_pass_l{l + 1}_{key}")
                nxt.update(as_weights(subset, done))
            W.append(nxt)
    loss, dx, dgf = _loss_bwd(xt, V["final_norm_g"], loss_target.reshape(Bn * S, D), "loss")
    loss = lax.psum(loss[0, 0], ("x", "y", "c"))

    late_names = [name for name in names if name not in EARLY]

    def chip_sums(gw, subset, l, tag):
        gv = []
        for name in subset:
            g = gw[name] if gw[name].ndim == 3 else gw[name][None]
            P, R, C = g.shape
            gv.append(g.reshape(P, 2, R // 2, C) if kinds[name] == "col" else g.reshape(P, N_CHIPS, 2, R // (2 * N_CHIPS), C))
        lands = _sibling_exchange(gv, [kinds[name] for name in subset], f"grad_sibling_exchange_{tag}_l{l}")
        own, pieces = {}, {}
        for name, g, land in zip(subset, gv, lands):
            own[name], pieces[name] = _chip_sum(g, land, kinds[name], jc, f"chip_sum_{name}_{l}")
        return own, pieces

    def carry(table, pieces):
        return _Sides({key: _side_chip_exchange([pieces[name] for name in subset]) for key, subset in table.items()})

    def landed(table, sides):
        return {name: land for key, subset in table.items() for name, land in zip(subset, sides.landed[key])}

    dxb, dmem_n = dx, None
    smalls, owns, got = [None] * L, [{} for _ in range(L)], [{} for _ in range(L)]
    late = None
    for l in reversed(range(L)):
        sides = carry(BWD_CARRY_LATE, late) if late is not None else _Sides()
        dx, dxb, gw, sm = _layer_bwd_mlp(dx, dxb, saved[l], W[l], V, l, dims, sides)
        if late is not None:
            got[l + 1].update(landed(BWD_CARRY_LATE, sides))
        own, early = chip_sums(gw, EARLY, l, "mlp")
        owns[l].update(own)
        sides = carry(BWD_CARRY_EARLY, early)
        dx, dxb, dmem_n, gw, sm2 = _layer_bwd_mix(dx, dxb, dmem_n, saved[l], mem_n, W[l], V, l, dims, sides)
        got[l].update(landed(BWD_CARRY_EARLY, sides))
        smalls[l] = {**sm, **sm2}
        own, late = chip_sums(gw, late_names, l, "mix")
        owns[l].update(own)
    got[0].update(zip(late_names, _chip_exchange([late[name] for name in late_names])))
    owns = [[o[name] for name in names] for o in owns]
    got = [[g[name] for name in names] for g in got]
    grad_x = dx.reshape(Bn, S, D)
    _, _, dgm = _rms_bwd(memf, V["mem_norm_g"], dmem_n, None, "mem_norm_b")
    small = {k: jnp.stack([sm[k] for sm in smalls]) if k in ("conv_dw_w", "ffn_dw_w") else jnp.concatenate([sm[k] for sm in smalls], axis=0)
             for k in smalls[0]}
    small["mem_norm_g"] = dgm
    small["final_norm_g"] = dgf
    mine = []
    for i, name in enumerate(names):
        shard = None
        for l in range(L):
            shard = _final_sum(owns[l][i], got[l][i], jc, shard, l, L, f"final_sum_{name}_{l}")
        mine.append(shard)
    gshards = _halves_exchange(mine)
    grads = {}
    for (name, kind), gs in zip(BIG, gshards):
        Lg, P, _, RH, CS = gs.shape
        grads["w_pool_grp" if name == "w_pool" else name] = from_mat(name, gs.reshape(Lg, P, 2 * RH, CS))

    small_w = conv_dw_b.shape[1]
    order = VECTORS + ("conv_dw_w", "ffn_dw_w")
    parts = [_rows(small[name], small_w) for name in order]
    counts = [p.shape[0] for p in parts]
    summed = _reduce_small(_pack(parts))
    off = 0
    for name, cnt in zip(order, counts):
        t = summed[off:off + cnt]
        off += cnt + (-cnt) % 8
        if name in VECTORS:
            grads[name] = t.reshape(w[name].shape)
        else:
            full = t.reshape(small[name].shape)
            cs = w[name].shape[2]
            grads[name] = lax.dynamic_slice_in_dim(full, j * cs, cs, axis=2)

    delta, new_m, new_v = {}, {}, {}
    for name, _ in BIG:
        key = "w_pool_grp" if name == "w_pool" else name
        outs = _adamw(*[to_mat(name, t) for t in (w[key], grads[key], m[key], v[key])], "adamw_" + name)
        delta[key], new_m[key], new_v[key] = [from_mat(name, t) for t in outs]
    vec = [_pack([_rows(d[name], small_w) for name in VECTORS]) for d in (w, grads, m, v)]
    outs = _adamw(*vec, "adamw_vectors")
    off = 0
    for name in VECTORS:
        cnt = w[name].size // small_w
        for d, t in zip((delta, new_m, new_v), outs):
            d[name] = t[off:off + cnt].reshape(w[name].shape)
        off += cnt + (-cnt) % 8
    for name in ("conv_dw_w", "ffn_dw_w"):
        delta[name], new_m[name], new_v[name] = _adamw(w[name], grads[name], m[name], v[name], "adamw_" + name)

    return (loss, grad_x, *[grads[k] for k in WEIGHTS], *[delta[k] for k in WEIGHTS], *[new_m[k] for k in WEIGHTS], *[new_v[k] for k in WEIGHTS])
```

```python
import functools

import jax
import jax.numpy as jnp
from jax import lax
from jax.experimental import pallas as pl
from jax.experimental.pallas import tpu as pltpu

F32 = jnp.float32
BF = jnp.bfloat16
SDS = jax.ShapeDtypeStruct
MESH = pl.DeviceIdType.MESH
ANY = pl.BlockSpec(memory_space=pl.ANY)

EPS = 1e-6
XA_HEADS = 4
POOL_WINDOWS = (2, 4, 8, 16)
N_CHIPS = 4
ADAM_LR, ADAM_B1, ADAM_B2, ADAM_EPS, ADAM_WD, ADAM_STEP = 0.001, 0.9, 0.999, 1e-08, 0.01, 10

LANES = 128
ROW_BLOCK = 512
VMEM_LIMIT = 56 * 1024 * 1024


def _params(*sem):
    return pltpu.CompilerParams(dimension_semantics=sem if sem else None, vmem_limit_bytes=VMEM_LIMIT)


def _tile(n, cap, mult=LANES):
    if n <= cap:
        return n
    for t in range(cap - cap % mult, 0, -mult):
        if n % t == 0:
            return t
    return n


_DN = {"nn": (((1,), (0,)), ((), ())), "nt": (((1,), (1,)), ((), ())), "tn": (((0,), (0,)), ((), ()))}


class _Side:
    def __init__(self, ins, outs, n, make):
        self.ins, self.outs, self.n, self.make = list(ins), list(outs), n, make


def _call(body, args, *, grid, in_specs, out_specs, out_shape, semantics, name, scratch_shapes=(), side=None):
    if side is None:
        return pl.pallas_call(body, grid=grid, in_specs=in_specs, out_specs=out_specs, out_shape=out_shape, scratch_shapes=list(scratch_shapes),
                              compiler_params=_params(*semantics), name=name)(*args)
    n_in, n_out, n_scr, n_si, n_so = len(args), len(out_shape), len(scratch_shapes), len(side.ins), len(side.outs)

    def carrying(*refs):
        ins, s_in = refs[:n_in], refs[n_in:n_in + n_si]
        outs, s_out = refs[n_in + n_si:n_in + n_si + n_out], refs[n_in + n_si + n_out:n_in + n_si + n_out + n_so]
        scr = refs[n_in + n_si + n_out + n_so:]
        copies = side.make(s_in, s_out, scr[n_scr], scr[n_scr + 1])
        ids = [pl.program_id(d) for d in range(len(grid))]
        first, last = ids[0] == 0, ids[0] == grid[0] - 1
        for d in range(1, len(grid)):
            first, last = first & (ids[d] == 0), last & (ids[d] == grid[d] - 1)

        @pl.when(first)
        def _():
            for cp in copies:
                cp.start()

        body(*ins, *outs, *scr[:n_scr])

        @pl.when(last)
        def _():
            for cp in copies:
                cp.wait()

    outs = pl.pallas_call(
        carrying, grid=grid, in_specs=list(in_specs) + [ANY] * n_si, out_specs=list(out_specs) + [ANY] * n_so,
        out_shape=list(out_shape) + side.outs,
        scratch_shapes=list(scratch_shapes) + [pltpu.SemaphoreType.DMA((side.n,)), pltpu.SemaphoreType.DMA((side.n,))],
        compiler_params=_params(*["arbitrary"] * len(grid)), name=name)(*args, *side.ins)
    return list(outs[:n_out]), list(outs[n_out:])


MM_VMEM_BUDGET = 40 * 1024 * 1024
MM_STEP_MACS = 2200 * 1024 * 1024
MXU_WIDTH = 256
MM_STEP_COST_BYTES = 1 << 20


def _divisors(n):
    return [t for t in range(LANES, n + 1, LANES) if n % t == 0] or [n]


def _mm_tiles(M, N, K, a_bytes, b_bytes, o_bytes):
    best = None
    for tk in _divisors(K):
        for tm in _divisors(M):
            for tn in _divisors(N):
                nk = K // tk
                foot = 2 * (tm * tk * a_bytes + tk * tn * b_bytes + tm * tn * o_bytes) + (tm * tn * 4 if nk > 1 else 0)
                if foot > MM_VMEM_BUDGET or tm * tn * tk > MM_STEP_MACS or tn < min(N, MXU_WIDTH) or tm < min(M, MXU_WIDTH):
                    continue
                steps = (M // tm) * (N // tn) * nk
                traffic = M * K * a_bytes * (N // tn if nk > 1 else 1) + K * N * b_bytes * (M // tm) + M * N * o_bytes
                cost = traffic + steps * MM_STEP_COST_BYTES + (nk - 1) * M * N * 8
                if best is None or cost < best[0]:
                    best = (cost, tm, tn, tk)
    assert best is not None, (M, N, K)
    return best[1:]


def _mm(a, b, dims, out_dtype, name, res=None, bl=None, side=None):
    bs = b.shape[1:] if bl is not None else b.shape
    if dims == "nn":
        (M, K), (K2, N) = a.shape, bs
    elif dims == "nt":
        (M, K), (N, K2) = a.shape, bs
    else:
        (K, M), (K2, N) = a.shape, bs
    assert K == K2, (name, a.shape, b.shape)
    tm, tn, tk = _mm_tiles(M, N, K, a.dtype.itemsize, b.dtype.itemsize,
                           jnp.dtype(out_dtype).itemsize + (res.dtype.itemsize if res is not None else 0))
    nk = K // tk
    lead = (None,) if bl is not None else ()
    pre = (lambda *ix: (bl,) + ix) if bl is not None else (lambda *ix: ix)
    if dims == "tn":
        a_spec = pl.BlockSpec((tk, tm), lambda i, j, k: (k, i))
    else:
        a_spec = pl.BlockSpec((tm, tk), lambda i, j, k: (i, k))
    if dims == "nt":
        b_spec = pl.BlockSpec(lead + (tn, tk), lambda i, j, k: pre(j, k))
    else:
        b_spec = pl.BlockSpec(lead + (tk, tn), lambda i, j, k: pre(k, j))
    o_spec = pl.BlockSpec((tm, tn), lambda i, j, k: (i, j))
    in_specs, args = [a_spec, b_spec], [a, b]
    if res is not None:
        in_specs.append(o_spec)
        args.append(res)
    n_main = len(args)

    def body(*refs):
        a_ref, b_ref = refs[0], refs[1]
        r_ref = refs[2] if res is not None else None
        o_ref = refs[n_main]
        p = lax.dot_general(a_ref[...].astype(BF), b_ref[...].astype(BF), _DN[dims], preferred_element_type=F32)

        def finish(t):
            if r_ref is not None:
                t = t + r_ref[...]
            o_ref[...] = t.astype(out_dtype)

        if nk == 1:
            finish(p)
        else:
            acc = refs[n_main + 1]
            k = pl.program_id(2)

            @pl.when(k == 0)
            def _():
                acc[...] = p

            @pl.when(k > 0)
            def _():
                acc[...] += p

            @pl.when(k == nk - 1)
            def _():
                finish(acc[...])

    res_ = _call(body, args, grid=(M // tm, N // tn, nk), in_specs=in_specs, out_specs=[o_spec], out_shape=[SDS((M, N), out_dtype)],
                 scratch_shapes=[pltpu.VMEM((tm, tn), F32)] if nk > 1 else [], semantics=("parallel", "parallel", "arbitrary"),
                 name=name, side=side)
    return res_[0] if side is None else (res_[0][0], res_[1])


def _rms(x, g):
    return x * lax.rsqrt(jnp.mean(x * x, axis=-1, keepdims=True) + EPS) * g


def _ln_silu(x, g, b):
    mu = jnp.mean(x, axis=-1, keepdims=True)
    xc = x - mu
    var = jnp.mean(xc * xc, axis=-1, keepdims=True)
    return jax.nn.silu(xc * lax.rsqrt(var + EPS) * g + b)


def _merge(gc, gp, yc, yp, ps):
    return jax.nn.sigmoid(gc) * yc + jax.nn.sigmoid(gp) * (yp * ps)


def _gated(gate, val):
    return jax.nn.gelu(gate) * val


def _rms_fwd(x, g, name):
    T, D = x.shape
    tb = _tile(T, ROW_BLOCK, 8)

    def body(x_ref, g_ref, o_ref):
        o_ref[...] = _rms(x_ref[...], g_ref[...]).astype(BF)

    row = pl.BlockSpec((tb, D), lambda i: (i, 0))
    return pl.pallas_call(body, grid=(T // tb,), in_specs=[row, pl.BlockSpec((1, D), lambda i: (0, 0))], out_specs=row,
                          out_shape=SDS((T, D), BF), compiler_params=_params("parallel"), name=name)(x, g.reshape(1, D))


def _rms_bwd(x, g, dh, dres, name):
    T, D = x.shape
    tb = _tile(T, ROW_BLOCK, 8)

    def body(*refs):
        if dres is not None:
            x_ref, g_ref, dh_ref, dres_ref, dx_ref, dxb_ref, dg_ref = refs
        else:
            x_ref, g_ref, dh_ref, dx_ref, dxb_ref, dg_ref = refs
        _, vjp = jax.vjp(_rms, x_ref[...], g_ref[...])
        dx, dg = vjp(dh_ref[...].astype(F32))
        if dres is not None:
            dx = dx + dres_ref[...]
        dx_ref[...] = dx
        dxb_ref[...] = dx.astype(BF)

        @pl.when(pl.program_id(0) == 0)
        def _():
            dg_ref[...] = jnp.zeros_like(dg_ref)

        dg_ref[...] += dg

    row = pl.BlockSpec((tb, D), lambda i: (i, 0))
    vec = pl.BlockSpec((1, D), lambda i: (0, 0))
    ins = [x, g.reshape(1, D), dh] + ([dres] if dres is not None else [])
    return pl.pallas_call(
        body, grid=(T // tb,), in_specs=[row, vec, row] + ([row] if dres is not None else []), out_specs=[row, row, vec],
        out_shape=[SDS((T, D), F32), SDS((T, D), BF), SDS((1, D), F32)], compiler_params=_params("arbitrary"), name=name)(*ins)


def _loss_bwd(x, g, target, name):
    T, D = x.shape
    tb = _tile(T, ROW_BLOCK, 8)
    nb = T // tb

    def body(x_ref, g_ref, t_ref, loss_ref, dx_ref, dg_ref, acc):
        i = pl.program_id(0)
        y, vjp = jax.vjp(_rms, x_ref[...], g_ref[...])
        err = y - t_ref[...]
        dx, dg = vjp(err * (1.0 / D))
        dx_ref[...] = dx

        @pl.when(i == 0)
        def _():
            dg_ref[...] = jnp.zeros_like(dg_ref)
            acc[...] = jnp.zeros_like(acc)

        dg_ref[...] += dg
        acc[...] += jnp.sum(err * err, axis=0, keepdims=True)

        @pl.when(i == nb - 1)
        def _():
            loss_ref[...] = jnp.full(loss_ref.shape, (0.5 / D) * jnp.sum(acc[...]), F32)

    row = pl.BlockSpec((tb, D), lambda i: (i, 0))
    vec = pl.BlockSpec((1, D), lambda i: (0, 0))
    return pl.pallas_call(
        body, grid=(nb,), in_specs=[row, vec, row], out_specs=[pl.BlockSpec((1, LANES), lambda i: (0, 0)), row, vec],
        out_shape=[SDS((1, LANES), F32), SDS((T, D), F32), SDS((1, D), F32)], scratch_shapes=[pltpu.VMEM((1, D), F32)],
        compiler_params=_params("arbitrary"), name=name)(x, g.reshape(1, D), target)


def _ln_silu_fwd(cv, g, b, name):
    T, C = cv.shape
    tb = _tile(T, ROW_BLOCK, 8)

    def body(x_ref, g_ref, b_ref, o_ref):
        o_ref[...] = _ln_silu(x_ref[...], g_ref[...], b_ref[...]).astype(BF)

    row = pl.BlockSpec((tb, C), lambda i: (i, 0))
    vec = pl.BlockSpec((1, C), lambda i: (0, 0))
    return pl.pallas_call(body, grid=(T // tb,), in_specs=[row, vec, vec], out_specs=row, out_shape=SDS((T, C), BF),
                          compiler_params=_params("parallel"), name=name)(cv, g.reshape(1, C), b.reshape(1, C))


def _ln_silu_bwd(cv, g, b, dy, name):
    T, C = cv.shape
    tb = _tile(T, ROW_BLOCK, 8)

    def body(x_ref, g_ref, b_ref, dy_ref, dx_ref, dg_ref, db_ref):
        _, vjp = jax.vjp(_ln_silu, x_ref[...], g_ref[...], b_ref[...])
        dx, dg, db = vjp(dy_ref[...].astype(F32))
        dx_ref[...] = dx

        @pl.when(pl.program_id(0) == 0)
        def _():
            dg_ref[...] = jnp.zeros_like(dg_ref)
            db_ref[...] = jnp.zeros_like(db_ref)

        dg_ref[...] += dg
        db_ref[...] += db

    row = pl.BlockSpec((tb, C), lambda i: (i, 0))
    vec = pl.BlockSpec((1, C), lambda i: (0, 0))
    return pl.pallas_call(
        body, grid=(T // tb,), in_specs=[row, vec, vec, row], out_specs=[row, vec, vec],
        out_shape=[SDS((T, C), F32), SDS((1, C), F32), SDS((1, C), F32)], compiler_params=_params("arbitrary"),
        name=name)(cv, g.reshape(1, C), b.reshape(1, C), dy)


def _merge_fwd(proj, yc, yp, ps, C, name):
    T, D = yc.shape
    tb = _tile(T, ROW_BLOCK, 8)
    nj = D // C

    def body(gc_ref, gp_ref, yc_ref, yp_ref, ps_ref, o_ref):
        o_ref[...] = _merge(gc_ref[...], gp_ref[...], yc_ref[...], yp_ref[...], ps_ref[...]).astype(BF)

    blk = pl.BlockSpec((tb, C), lambda i, j: (i, j))
    return pl.pallas_call(
        body, grid=(T // tb, nj),
        in_specs=[pl.BlockSpec((tb, C), lambda i, j: (i, 3 + j)), pl.BlockSpec((tb, C), lambda i, j: (i, 3 + nj + j)), blk, blk,
                  pl.BlockSpec((1, C), lambda i, j: (0, j))],
        out_specs=blk, out_shape=SDS((T, D), BF), compiler_params=_params("parallel", "parallel"), name=name)(proj, proj, yc, yp, ps.reshape(1, D))


def _merge_bwd(proj, yc, yp, ps, dm, C, name, side=None):
    T, D = yc.shape
    tb = _tile(T, ROW_BLOCK, 8)
    nj = D // C

    def body(gc_ref, gp_ref, yc_ref, yp_ref, ps_ref, dm_ref, dgc_ref, dgp_ref, dyc_ref, dyp_ref, dps_ref):
        _, vjp = jax.vjp(_merge, gc_ref[...], gp_ref[...], yc_ref[...], yp_ref[...], ps_ref[...])
        dgc, dgp, dyc, dyp, dps = vjp(dm_ref[...].astype(F32))
        dgc_ref[...] = dgc.astype(BF)
        dgp_ref[...] = dgp.astype(BF)
        dyc_ref[...] = dyc.astype(BF)
        dyp_ref[...] = dyp.astype(BF)

        @pl.when(pl.program_id(1) == 0)
        def _():
            dps_ref[...] = jnp.zeros_like(dps_ref)

        dps_ref[...] += dps

    blk = pl.BlockSpec((tb, C), lambda j, i: (i, j))
    vec = pl.BlockSpec((1, C), lambda j, i: (0, j))
    return _call(
        body, (proj, proj, yc, yp, ps.reshape(1, D), dm), grid=(nj, T // tb),
        in_specs=[pl.BlockSpec((tb, C), lambda j, i: (i, 3 + j)), pl.BlockSpec((tb, C), lambda j, i: (i, 3 + nj + j)), blk, blk, vec, blk],
        out_specs=[blk, blk, blk, blk, vec], out_shape=[SDS((T, D), BF)] * 4 + [SDS((1, D), F32)],
        semantics=("parallel", "arbitrary"), name=name, side=side)


def _shd(v, s, rows):
    if s == 0:
        return v
    return jnp.where(rows >= s, pltpu.roll(v, s, 0), 0.0)


def _shu(v, s, rows):
    if s == 0:
        return v
    n = v.shape[0]
    return jnp.where(rows < n - s, pltpu.roll(v, n - s, 0), 0.0)


def _glu_conv_fwd(proj, w, b, Bn, S, C, name):
    K = w.shape[0]
    sl = min(LANES, C)
    ns = C // sl

    def body(a_ref, gl_ref, w_ref, b_ref, o_ref):
        y0 = a_ref[...] * jax.nn.sigmoid(gl_ref[...])
        rows = lax.broadcasted_iota(jnp.int32, y0.shape, 0)
        acc = jnp.zeros_like(y0) + b_ref[...]
        for k in range(K):
            acc = acc + w_ref[k:k + 1, :] * _shd(y0, K - 1 - k, rows)
        o_ref[...] = acc

    return pl.pallas_call(
        body, grid=(Bn, ns),
        in_specs=[pl.BlockSpec((S, sl), lambda bi, j: (bi, j)), pl.BlockSpec((S, sl), lambda bi, j: (bi, ns + j)),
                  pl.BlockSpec((K, sl), lambda bi, j: (0, j)), pl.BlockSpec((1, sl), lambda bi, j: (0, j))],
        out_specs=pl.BlockSpec((S, sl), lambda bi, j: (bi, j)), out_shape=SDS((Bn * S, C), F32),
        compiler_params=_params("parallel", "parallel"), name=name)(proj, proj, w, b.reshape(1, C))


def _glu_conv_bwd(proj, w, dcv, Bn, S, C, name, side=None):
    K = w.shape[0]
    sl = min(LANES, C)
    ns = C // sl

    def body(a_ref, gl_ref, w_ref, d_ref, da_ref, dgl_ref, dw_ref, db_ref):
        a = a_ref[...]
        sg = jax.nn.sigmoid(gl_ref[...])
        y0 = a * sg
        d = d_ref[...]
        rows = lax.broadcasted_iota(jnp.int32, y0.shape, 0)

        @pl.when(pl.program_id(1) == 0)
        def _():
            dw_ref[...] = jnp.zeros_like(dw_ref)
            db_ref[...] = jnp.zeros_like(db_ref)

        dy0 = jnp.zeros_like(y0)
        for k in range(K):
            s = K - 1 - k
            dw_ref[k:k + 1, :] += jnp.sum(d * _shd(y0, s, rows), axis=0, keepdims=True)
            dy0 = dy0 + w_ref[k:k + 1, :] * _shu(d, s, rows)
        db_ref[...] += jnp.sum(d, axis=0, keepdims=True)
        da_ref[...] = (dy0 * sg).astype(BF)
        dgl_ref[...] = (dy0 * a * sg * (1.0 - sg)).astype(BF)

    blk = pl.BlockSpec((S, sl), lambda j, bi: (bi, j))
    return _call(
        body, (proj, proj, w, dcv), grid=(ns, Bn),
        in_specs=[blk, pl.BlockSpec((S, sl), lambda j, bi: (bi, ns + j)), pl.BlockSpec((K, sl), lambda j, bi: (0, j)), blk],
        out_specs=[blk, blk, pl.BlockSpec((K, sl), lambda j, bi: (0, j)), pl.BlockSpec((1, sl), lambda j, bi: (0, j))],
        out_shape=[SDS((Bn * S, C), BF), SDS((Bn * S, C), BF), SDS((K, C), F32), SDS((1, C), F32)],
        semantics=("parallel", "arbitrary"), name=name, side=side)


def _pool_z(u, g, rows):
    s2 = u + _shd(u, 1, rows)
    s4 = s2 + _shd(s2, 2, rows)
    s8 = s4 + _shd(s4, 4, rows)
    s16 = s8 + _shd(s8, 8, rows)
    sw = jnp.where(g == 0, s2, jnp.where(g == 1, s4, jnp.where(g == 2, s8, s16)))
    cnt = jnp.minimum(rows + 1, POOL_WINDOWS[0] << g).astype(F32)
    return sw / cnt - u, cnt


def _pool_fwd(proj, wpt, l, Bn, S, C, D, name):
    G = len(POOL_WINDOWS)
    gd, go = C // G, D // G

    def body(u_ref, w_ref, o_ref):
        g = pl.program_id(1)
        u = u_ref[...]
        rows = lax.broadcasted_iota(jnp.int32, u.shape, 0)
        zp, _ = _pool_z(u, g, rows)
        o_ref[...] = lax.dot_general(zp.astype(BF), w_ref[...], _DN["nt"], preferred_element_type=F32)

    return pl.pallas_call(
        body, grid=(Bn, G),
        in_specs=[pl.BlockSpec((S, gd), lambda bi, g: (bi, 2 * G + g)), pl.BlockSpec((None, go, gd), lambda bi, g: (l * G + g, 0, 0))],
        out_specs=pl.BlockSpec((S, go), lambda bi, g: (bi, g)), out_shape=SDS((Bn * S, D), F32),
        compiler_params=_params("parallel", "parallel"), name=name)(proj, wpt)


def _pool_bwd(proj, wpt, dyp, l, Bn, S, C, D, name):
    G = len(POOL_WINDOWS)
    gd, go = C // G, D // G

    def body(u_ref, w_ref, d_ref, du_ref, dw_ref):
        g = pl.program_id(0)
        u = u_ref[...]
        rows = lax.broadcasted_iota(jnp.int32, u.shape, 0)
        zp, cnt = _pool_z(u, g, rows)
        d = d_ref[...]
        dzp = lax.dot_general(d, w_ref[...], _DN["nn"], preferred_element_type=F32)

        @pl.when(pl.program_id(1) == 0)
        def _():
            dw_ref[...] = jnp.zeros_like(dw_ref)

        dw_ref[...] += lax.dot_general(d, zp.astype(BF), _DN["tn"], preferred_element_type=F32)
        dsw = dzp / cnt
        zero = jnp.zeros_like(dsw)
        d16 = jnp.where(g == 3, dsw, zero)
        d8 = jnp.where(g == 2, dsw, zero) + d16 + _shu(d16, 8, rows)
        d4 = jnp.where(g == 1, dsw, zero) + d8 + _shu(d8, 4, rows)
        d2 = jnp.where(g == 0, dsw, zero) + d4 + _shu(d4, 2, rows)
        d1 = d2 + _shu(d2, 1, rows)
        du_ref[...] = (d1 - dzp).astype(BF)

    return pl.pallas_call(
        body, grid=(G, Bn),
        in_specs=[pl.BlockSpec((S, gd), lambda g, bi: (bi, 2 * G + g)), pl.BlockSpec((None, go, gd), lambda g, bi: (l * G + g, 0, 0)),
                  pl.BlockSpec((S, go), lambda g, bi: (bi, g))],
        out_specs=[pl.BlockSpec((S, gd), lambda g, bi: (bi, g)), pl.BlockSpec((None, go, gd), lambda g, bi: (g, 0, 0))],
        out_shape=[SDS((Bn * S, C), BF), SDS((G, go, gd), F32)],
        compiler_params=_params("parallel", "arbitrary"), name=name)(proj, wpt, dyp)


def _ffn_conv(u, w_ref, rows):
    K = w_ref.shape[0]
    acc = w_ref[K - 1:K, :] * u
    for k in range(K - 1):
        acc = acc + w_ref[k:k + 1, :] * _shd(u, K - 1 - k, rows)
    return acc


def _ffn_cb(F):
    return _tile(F, 256)


def _ffn_act_fwd(up0, w, Bn, S, F, name):
    cb = _ffn_cb(F)
    nj = F // cb

    def body(g_ref, v_ref, wg_ref, wv_ref, o_ref):
        rows = lax.broadcasted_iota(jnp.int32, g_ref.shape, 0)
        o_ref[...] = _gated(_ffn_conv(g_ref[...], wg_ref, rows), _ffn_conv(v_ref[...], wv_ref, rows)).astype(BF)

    K = w.shape[0]
    return pl.pallas_call(
        body, grid=(Bn, nj),
        in_specs=[pl.BlockSpec((S, cb), lambda bi, j: (bi, j)), pl.BlockSpec((S, cb), lambda bi, j: (bi, nj + j)),
                  pl.BlockSpec((K, cb), lambda bi, j: (0, j)), pl.BlockSpec((K, cb), lambda bi, j: (0, nj + j))],
        out_specs=pl.BlockSpec((S, cb), lambda bi, j: (bi, j)), out_shape=SDS((Bn * S, F), BF),
        compiler_params=_params("parallel", "parallel"), name=name)(up0, up0, w, w)


SUBLANES = 8
FFN_HALO = SUBLANES
FFN_ROWS = 64
GELU_C0, GELU_C1 = 0.7978845608028654, 0.044715


def _gelu_and_grad(x):
    x2 = x * x
    t = jnp.tanh(GELU_C0 * (x + GELU_C1 * (x2 * x)))
    cdf = 0.5 * (1.0 + t)
    return x * cdf, cdf + (0.5 * GELU_C0) * x * (1.0 - t * t) * (1.0 + (3.0 * GELU_C1) * x2)


def _ffn_act_bwd(up0, w, dg, Bn, S, F, name, side=None):
    cb = min(LANES, F)
    nj = F // cb
    K = w.shape[0]
    rc = FFN_ROWS if S % FFN_ROWS == 0 else S
    win = rc + 2 * FFN_HALO
    assert K - 1 <= FFN_HALO and rc % SUBLANES == 0

    def body(g_ref, v_ref, wg_ref, wv_ref, d_ref, dgo_ref, dvo_ref, dwg_ref, dwv_ref, gp, vp, dp):
        for pad, src in ((gp, g_ref), (vp, v_ref), (dp, d_ref)):
            pad[0:FFN_HALO, :] = jnp.zeros((FFN_HALO, cb), F32)
            pad[FFN_HALO + S:, :] = jnp.zeros((FFN_HALO, cb), F32)
            pad[FFN_HALO:FFN_HALO + S, :] = src[...].astype(F32)
        wg = [wg_ref[k:k + 1, :] for k in range(K)]
        wv = [wv_ref[k:k + 1, :] for k in range(K)]

        def conv(u, ws):
            acc = ws[K - 1] * u
            for k in range(K - 1):
                acc = acc + ws[k] * pltpu.roll(u, K - 1 - k, 0)
            return acc

        def conv_t(dc, ws):
            acc = ws[K - 1] * dc
            for k in range(K - 1):
                acc = acc + ws[k] * pltpu.roll(dc, win - (K - 1 - k), 0)
            return acc

        def fold(t):
            acc = t[FFN_HALO:FFN_HALO + SUBLANES]
            for i in range(1, rc // SUBLANES):
                acc = acc + t[FFN_HALO + SUBLANES * i:FFN_HALO + SUBLANES * (i + 1)]
            return acc

        def chunk(c, sums):
            r0 = pl.multiple_of(c * rc, SUBLANES)
            g, v, d = gp[pl.ds(r0, win), :], vp[pl.ds(r0, win), :], dp[pl.ds(r0, win), :]
            ge, dge = _gelu_and_grad(conv(g, wg))
            dgc = d * conv(v, wv) * dge
            dvc = d * ge
            dgo_ref[pl.ds(r0, rc), :] = conv_t(dgc, wg)[FFN_HALO:FFN_HALO + rc].astype(BF)
            dvo_ref[pl.ds(r0, rc), :] = conv_t(dvc, wv)[FFN_HALO:FFN_HALO + rc].astype(BF)
            new = []
            for u, dc in ((g, dgc), (v, dvc)):
                for k in range(K):
                    new.append(fold(dc * (u if k == K - 1 else pltpu.roll(u, K - 1 - k, 0))))
            return tuple(a + b for a, b in zip(sums, new))

        sums = lax.fori_loop(0, S // rc, chunk, tuple(jnp.zeros((SUBLANES, cb), F32) for _ in range(2 * K)))

        @pl.when(pl.program_id(1) == 0)
        def _():
            dwg_ref[...] = jnp.zeros_like(dwg_ref)
            dwv_ref[...] = jnp.zeros_like(dwv_ref)

        for k in range(K):
            dwg_ref[k:k + 1, :] += jnp.sum(sums[k], axis=0, keepdims=True)
            dwv_ref[k:k + 1, :] += jnp.sum(sums[K + k], axis=0, keepdims=True)

    blk = pl.BlockSpec((S, cb), lambda j, bi: (bi, j))
    wblk = pl.BlockSpec((K, cb), lambda j, bi: (0, j))
    return _call(
        body, (up0, up0, w, w, dg), grid=(nj, Bn),
        in_specs=[blk, pl.BlockSpec((S, cb), lambda j, bi: (bi, nj + j)), wblk, pl.BlockSpec((K, cb), lambda j, bi: (0, nj + j)), blk],
        out_specs=[blk, blk, wblk, wblk],
        out_shape=[SDS((Bn * S, F), BF), SDS((Bn * S, F), BF), SDS((K, F), F32), SDS((K, F), F32)],
        scratch_shapes=[pltpu.VMEM((S + 2 * FFN_HALO, cb), F32)] * 3, semantics=("parallel", "arbitrary"), name=name, side=side)


def _softmax_rows(q, k, scale):
    sc = lax.dot_general(q, k, _DN["nt"], preferred_element_type=F32) * scale
    e = jnp.exp(sc - jnp.max(sc, axis=-1, keepdims=True))
    return e / jnp.sum(e, axis=-1, keepdims=True)


def _attn_ts(S):
    return _tile(S, 1024, 8)


def _attn_fwd(q, kv, Bn, S, Mn, D, name):
    H = XA_HEADS
    dh = D // H
    ts = _attn_ts(S)
    nsb = S // ts
    scale = dh ** -0.5

    def body(q_ref, k_ref, v_ref, o_ref):
        p = _softmax_rows(q_ref[...], k_ref[...], scale)
        o_ref[...] = lax.dot_general(p.astype(BF), v_ref[...], _DN["nn"], preferred_element_type=F32).astype(BF)

    qblk = pl.BlockSpec((ts, dh), lambda bi, h, s: (bi * nsb + s, h))
    return pl.pallas_call(
        body, grid=(Bn, H, nsb),
        in_specs=[qblk, pl.BlockSpec((Mn, dh), lambda bi, h, s: (bi, h)), pl.BlockSpec((Mn, dh), lambda bi, h, s: (bi, H + h))],
        out_specs=qblk, out_shape=SDS((Bn * S, D), BF), compiler_params=_params("parallel", "parallel", "parallel"), name=name)(q, kv, kv)


def _attn_bwd(q, kv, datt, Bn, S, Mn, D, name):
    H = XA_HEADS
    dh = D // H
    ts = _attn_ts(S)
    nsb = S // ts
    scale = dh ** -0.5

    def body(q_ref, k_ref, v_ref, do_ref, dq_ref, dk_ref, dv_ref):
        q, k, v, do = q_ref[...], k_ref[...], v_ref[...], do_ref[...]
        p = _softmax_rows(q, k, scale)
        dp = lax.dot_general(do, v, _DN["nt"], preferred_element_type=F32)
        ds = (p * (dp - jnp.sum(dp * p, axis=-1, keepdims=True)) * scale).astype(BF)
        dq_ref[...] = lax.dot_general(ds, k, _DN["nn"], preferred_element_type=F32).astype(BF)

        @pl.when(pl.program_id(2) == 0)
        def _():
            dk_ref[...] = jnp.zeros_like(dk_ref)
            dv_ref[...] = jnp.zeros_like(dv_ref)

        dk_ref[...] += lax.dot_general(ds, q, _DN["tn"], preferred_element_type=F32)
        dv_ref[...] += lax.dot_general(p.astype(BF), do, _DN["tn"], preferred_element_type=F32)

    qblk = pl.BlockSpec((ts, dh), lambda bi, h, s: (bi * nsb + s, h))
    kblk = pl.BlockSpec((Mn, dh), lambda bi, h, s: (bi, h))
    return pl.pallas_call(
        body, grid=(Bn, H, nsb),
        in_specs=[qblk, kblk, pl.BlockSpec((Mn, dh), lambda bi, h, s: (bi, H + h)), qblk],
        out_specs=[qblk, kblk, kblk], out_shape=[SDS((Bn * S, D), BF), SDS((Bn * Mn, D), F32), SDS((Bn * Mn, D), F32)],
        compiler_params=_params("parallel", "parallel", "arbitrary"), name=name)(q, kv, kv, datt)


class _Sides:
    def __init__(self, by_key=None):
        self.by_key, self.landed = dict(by_key or {}), {}

    def run(self, key, fn, *args, **kw):
        side = self.by_key.get(key)
        if side is None:
            return fn(*args, **kw)
        out, self.landed[key] = fn(*args, side=side, **kw)
        return out

    def mm(self, key, *args, **kw):
        return self.run(key, _mm, *args, **kw)


def _layer_fwd(x, mem_n, W, V, l, dims, sides):
    Bn, S, Mn, D, C, F = dims
    n = f"l{l}_"
    h = _rms_fwd(x, V["mix_norm_g"][l], n + "mix_norm")
    proj = sides.mm("proj", h, W["w_in"], "nn", F32, n + "proj", bl=0)
    cv = _glu_conv_fwd(proj, V["conv_dw_w"][l], V["conv_dw_b"][l], Bn, S, C, n + "glu_conv")
    yc1 = _ln_silu_fwd(cv, V["conv_ln_g"][l], V["conv_ln_b"][l], n + "ln_silu")
    yc = sides.mm("conv_out", yc1, W["w_conv_out"], "nn", F32, n + "conv_out", bl=0)
    yp = _pool_fwd(proj, W["w_pool"], 0, Bn, S, C, D, n + "pool")
    merged = _merge_fwd(proj, yc, yp, V["pool_scale"][l], C, n + "merge")
    x1 = sides.mm("out_proj", merged, W["w_out"], "nn", F32, n + "out_proj", res=x, bl=0)
    hq = _rms_fwd(x1, V["xattn_norm_g"][l], n + "xattn_norm")
    q = sides.mm("q_proj", hq, W["w_q"], "nn", BF, n + "q_proj", bl=0)
    kv = _mm(mem_n, W["w_kv"], "nn", BF, n + "kv_proj", bl=0)
    att = _attn_fwd(q, kv, Bn, S, Mn, D, n + "attn")
    x2 = sides.mm("o_proj", att, W["w_o"], "nn", F32, n + "o_proj", res=x1, bl=0)
    hf = _rms_fwd(x2, V["ffn_norm_g"][l], n + "ffn_norm")
    up0 = sides.mm("up_proj", hf, W["w_up"], "nn", F32, n + "up_proj", bl=0)
    gact = _ffn_act_fwd(up0, V["ffn_dw_w"][l], Bn, S, F, n + "ffn_act")
    x3 = sides.mm("down_proj", gact, W["w_down"], "nn", F32, n + "down_proj", res=x2, bl=0)
    return x3, dict(x=x, h=h, proj=proj, cv=cv, yc1=yc1, yc=yc, yp=yp, merged=merged, x1=x1, hq=hq, q=q, kv=kv, att=att, x2=x2, hf=hf,
                    up0=up0, gact=gact)


def _layer_bwd_mlp(dx, dxb, sv, W, V, l, dims, sides):
    Bn, S, Mn, D, C, F = dims
    n = f"l{l}_b_"
    gw, sm = {}, {}
    dgact = sides.mm("d_gact", dxb, W["w_down"], "nt", BF, n + "d_gact", bl=0)
    gw["w_down"] = sides.mm("dw_down", sv["gact"], dxb, "tn", F32, n + "dw_down")
    dg0, dv0, dwg, dwv = sides.run("ffn_act_b", _ffn_act_bwd, sv["up0"], V["ffn_dw_w"][l], dgact, Bn, S, F, n + "ffn_act")
    sm["ffn_dw_w"] = jnp.concatenate([dwg, dwv], axis=1)
    dup0 = jnp.concatenate([dg0, dv0], axis=1)
    dhf = sides.mm("d_hf", dup0, W["w_up"], "nt", F32, n + "d_hf", bl=0)
    gw["w_up"] = sides.mm("dw_up", sv["hf"], dup0, "tn", F32, n + "dw_up")
    dx2, dx2b, sm["ffn_norm_g"] = _rms_bwd(sv["x2"], V["ffn_norm_g"][l], dhf, dx, n + "ffn_norm")
    return dx2, dx2b, gw, sm


def _layer_bwd_mix(dx2, dx2b, dmem_n, sv, mem_n, W, V, l, dims, sides):
    Bn, S, Mn, D, C, F = dims
    n = f"l{l}_b_"
    gw, sm = {}, {}
    datt = _mm(dx2b, W["w_o"], "nt", BF, n + "d_att", bl=0)
    gw["w_o"] = _mm(sv["att"], dx2b, "tn", F32, n + "dw_o")
    dq, dk, dv = _attn_bwd(sv["q"], sv["kv"], datt, Bn, S, Mn, D, n + "attn")
    dkv = jnp.concatenate([dk, dv], axis=1)
    gw["w_kv"] = _mm(mem_n, dkv, "tn", F32, n + "dw_kv")
    dmem_n = _mm(dkv, W["w_kv"], "nt", F32, n + "d_mem", res=dmem_n, bl=0)
    dhq = _mm(dq, W["w_q"], "nt", F32, n + "d_hq", bl=0)
    gw["w_q"] = _mm(sv["hq"], dq, "tn", F32, n + "dw_q")
    dx1, dx1b, sm["xattn_norm_g"] = _rms_bwd(sv["x1"], V["xattn_norm_g"][l], dhq, dx2, n + "xattn_norm")
    dmerged = _mm(dx1b, W["w_out"], "nt", BF, n + "d_merged", bl=0)
    gw["w_out"] = _mm(sv["merged"], dx1b, "tn", F32, n + "dw_out")
    dgc, dgp, dyc, dyp, sm["pool_scale"] = sides.run("merge_b", _merge_bwd, sv["proj"], sv["yc"], sv["yp"], V["pool_scale"][l], dmerged, C, n + "merge")
    du, gw["w_pool"] = _pool_bwd(sv["proj"], W["w_pool"], dyp, 0, Bn, S, C, D, n + "pool")
    dyc1 = _mm(dyc, W["w_conv_out"], "nt", F32, n + "d_yc1", bl=0)
    gw["w_conv_out"] = _mm(sv["yc1"], dyc, "tn", F32, n + "dw_conv_out")
    dcv, sm["conv_ln_g"], sm["conv_ln_b"] = _ln_silu_bwd(sv["cv"], V["conv_ln_g"][l], V["conv_ln_b"][l], dyc1, n + "ln_silu")
    da, dgl, sm["conv_dw_w"], sm["conv_dw_b"] = sides.run("glu_conv_b", _glu_conv_bwd, sv["proj"], V["conv_dw_w"][l], dcv, Bn, S, C, n + "glu_conv")
    dproj = jnp.concatenate([da, dgl, du, dgc, dgp], axis=1)
    dh = sides.mm("d_h", dproj, W["w_in"], "nt", F32, n + "d_h", bl=0)
    gw["w_in"] = sides.mm("dw_in", sv["h"], dproj, "tn", F32, n + "dw_in")
    dx, dxb, sm["mix_norm_g"] = _rms_bwd(sv["x"], V["mix_norm_g"][l], dh, dx1, n + "mix_norm")
    return dx, dxb, dmem_n, gw, sm


BIG = (("w_in", "col"), ("w_conv_out", "col"), ("w_pool", "row"), ("w_out", "row"), ("w_q", "row"), ("w_kv", "col"),
       ("w_o", "row"), ("w_up", "col"), ("w_down", "row"))
FWD_CARRY = {"proj": ("w_in",), "conv_out": ("w_conv_out", "w_pool"), "out_proj": ("w_out", "w_q"), "q_proj": ("w_o",), "o_proj": ("w_kv",),
             "up_proj": ("w_up",), "down_proj": ("w_down",)}
EARLY = ("w_down", "w_up")
BWD_CARRY_EARLY = {"merge_b": ("w_down",), "glu_conv_b": ("w_up",)}
BWD_CARRY_LATE = {"ffn_act_b": ("w_in", "w_conv_out", "w_pool", "w_out", "w_q", "w_kv", "w_o")}


def _place():
    xi, yi, ci = lax.axis_index("x"), lax.axis_index("y"), lax.axis_index("c")
    return xi, yi, ci, 2 * xi + yi


def _chip_peer(xi, yi, ci, r):
    return (xi ^ (r >> 1), yi ^ (r & 1), ci)


def _full_shard(ref, kind, k, cs):
    if kind == "col":
        return ref.at[:, :, :, :, pl.ds(pl.multiple_of(k * cs, cs), cs)]
    return ref.at[:, :, k]


def _gather_weights(shards, kinds):
    n = len(shards)
    outs = []
    for s, kind in zip(shards, kinds):
        L, P, _, RH, CS = s.shape
        outs.append(SDS((L, P, 2, RH, CS * N_CHIPS) if kind == "col" else (L, P, N_CHIPS, 2, RH, CS), s.dtype))
    per = 7

    def body(*refs):
        srcs, fulls, (ssem, rsem) = refs[:n], refs[n:2 * n], refs[2 * n:]
        xi, yi, ci, j = _place()
        sib = (xi, yi, 1 - ci)

        def piece(i, k, c):
            kind, cs = kinds[i], shards[i].shape[-1]
            if kind == "col":
                return fulls[i].at[:, :, c, :, pl.ds(pl.multiple_of(k * cs, cs), cs)]
            return fulls[i].at[:, :, k, c]

        def copy(i, slot, src, dst, dev):
            return pltpu.make_async_remote_copy(src_ref=src, dst_ref=dst, send_sem=ssem.at[per * i + slot], recv_sem=rsem.at[per * i + slot],
                                                device_id=dev, device_id_type=MESH)

        own, first, passed = [], [], []
        for i in range(n):
            for r in (1, 2, 3):
                first.append(copy(i, r - 1, srcs[i].at[:, :, ci], piece(i, j, ci), _chip_peer(xi, yi, ci, r)))
                first[-1].start()
        for i in range(n):
            own.append(copy(i, 6, srcs[i], _full_shard(fulls[i], kinds[i], j, shards[i].shape[-1]), sib))
            own[-1].start()
        for i in range(n):
            for r in (1, 2, 3):
                got = piece(i, j ^ r, ci)
                copy(i, r - 1, got, got, sib).wait_recv()
                passed.append(copy(i, 2 + r, got, got, sib))
                passed[-1].start()
        for i in range(n):
            for r in (1, 2, 3):
                got = piece(i, j ^ r, 1 - ci)
                copy(i, 2 + r, got, got, sib).wait_recv()
        for cp in own:
            cp.wait()
        for cp in first + passed:
            cp.wait_send()

    return pl.pallas_call(
        body, in_specs=[ANY] * n, out_specs=[ANY] * n, out_shape=outs,
        scratch_shapes=[pltpu.SemaphoreType.DMA((per * n,)), pltpu.SemaphoreType.DMA((per * n,))], name="gather_weights")(*shards)


def _full_sds(s, kind):
    L, P, _, RH, CS = s.shape
    return SDS((L, P, 2, RH, CS * N_CHIPS) if kind == "col" else (L, P, N_CHIPS, 2, RH, CS), s.dtype)


def _gather_piece(full, kind, cs, k, c):
    if kind == "col":
        return full.at[:, :, c, :, pl.ds(pl.multiple_of(k * cs, cs), cs)]
    return full.at[:, :, k, c]


def _side_gather(shards, kinds):
    n = len(shards)

    def make(srcs, fulls, ssem, rsem):
        xi, yi, ci, j = _place()
        return [pltpu.make_async_remote_copy(
            src_ref=srcs[i].at[:, :, ci], dst_ref=_gather_piece(fulls[i], kinds[i], shards[i].shape[-1], j, ci), send_sem=ssem.at[3 * i + r - 1],
            recv_sem=rsem.at[3 * i + r - 1], device_id=_chip_peer(xi, yi, ci, r), device_id_type=MESH) for i in range(n) for r in (1, 2, 3)]

    return _Side(shards, [_full_sds(s, k) for s, k in zip(shards, kinds)], 3 * n, make)


def _gather_pass(fulls, shards, kinds, name):
    n = len(fulls)

    def body(*refs):
        srcs, outs, (ssem, rsem) = refs[n:2 * n], refs[2 * n:3 * n], refs[3 * n:]
        xi, yi, ci, j = _place()
        sib = (xi, yi, 1 - ci)
        cps = []
        for i in range(n):
            cs = shards[i].shape[-1]
            for r in (1, 2, 3):
                got = _gather_piece(outs[i], kinds[i], cs, j ^ r, ci)
                cps.append(pltpu.make_async_remote_copy(src_ref=got, dst_ref=got, send_sem=ssem.at[4 * i + r - 1], recv_sem=rsem.at[4 * i + r - 1],
                                                        device_id=sib, device_id_type=MESH))
            cps.append(pltpu.make_async_remote_copy(src_ref=srcs[i], dst_ref=_full_shard(outs[i], kinds[i], j, cs), send_sem=ssem.at[4 * i + 3],
                                                    recv_sem=rsem.at[4 * i + 3], device_id=sib, device_id_type=MESH))
        for cp in cps:
            cp.start()
        for cp in cps:
            cp.wait()

    return pl.pallas_call(
        body, in_specs=[ANY] * (2 * n), out_specs=[ANY] * n, out_shape=[SDS(f.shape, f.dtype) for f in fulls],
        input_output_aliases={i: i for i in range(n)},
        scratch_shapes=[pltpu.SemaphoreType.DMA((4 * n,)), pltpu.SemaphoreType.DMA((4 * n,))], name=name)(*fulls, *shards)


def _sibling_exchange(gviews, kinds, name):
    n = len(gviews)
    outs = [SDS(g.shape[:1] + g.shape[2:] if kind == "col" else g.shape[:2] + g.shape[3:], g.dtype) for g, kind in zip(gviews, kinds)]

    def body(*refs):
        gs, lands, (ssem, rsem) = refs[:n], refs[n:2 * n], refs[2 * n:]
        xi, yi, ci, _ = _place()
        cps = []
        for i in range(n):
            src = gs[i].at[:, 1 - ci] if kinds[i] == "col" else gs[i].at[:, :, 1 - ci]
            cps.append(pltpu.make_async_remote_copy(src_ref=src, dst_ref=lands[i], send_sem=ssem.at[i], recv_sem=rsem.at[i],
                                                    device_id=(xi, yi, 1 - ci), device_id_type=MESH))
            cps[-1].start()
        for cp in cps:
            cp.wait()

    return pl.pallas_call(body, in_specs=[ANY] * n, out_specs=[ANY] * n, out_shape=outs,
                          scratch_shapes=[pltpu.SemaphoreType.DMA((n,)), pltpu.SemaphoreType.DMA((n,))], name=name)(*gviews)


def _chip_sum(g, land, kind, jc, name):
    if kind == "col":
        P, _, RH, C = g.shape
        CS = C // N_CHIPS
        g_spec = pl.BlockSpec((None, None, RH, CS), lambda p, r, jc: (p, jc[1], 0, jc[0] ^ r))
        l_spec = pl.BlockSpec((None, RH, CS), lambda p, r, jc: (p, 0, jc[0] ^ r))
    else:
        P, _, _, RH, CS = g.shape
        g_spec = pl.BlockSpec((None, None, None, RH, CS), lambda p, r, jc: (p, jc[0] ^ r, jc[1], 0, 0))
        l_spec = pl.BlockSpec((None, None, RH, CS), lambda p, r, jc: (p, jc[0] ^ r, 0, 0))

    def body(jc_ref, g_ref, l_ref, own_ref, all_ref):
        s = g_ref[...] + l_ref[...]
        all_ref[...] = s.astype(BF)

        @pl.when(pl.program_id(1) == 0)
        def _():
            own_ref[...] = s

    return pl.pallas_call(
        body, grid_spec=pltpu.PrefetchScalarGridSpec(
            num_scalar_prefetch=1, grid=(P, N_CHIPS), in_specs=[g_spec, l_spec],
            out_specs=[pl.BlockSpec((None, RH, CS), lambda p, r, jc: (p, 0, 0)), pl.BlockSpec((None, None, RH, CS), lambda p, r, jc: (r, p, 0, 0))]),
        out_shape=[SDS((P, RH, CS), F32), SDS((N_CHIPS, P, RH, CS), BF)], compiler_params=_params("parallel", "arbitrary"), name=name)(jc, g, land)


def _chip_exchange_copies(srcs, lands, ssem, rsem):
    xi, yi, ci, _ = _place()
    return [pltpu.make_async_remote_copy(src_ref=srcs[i].at[r], dst_ref=lands[i].at[r], send_sem=ssem.at[3 * i + r - 1],
                                         recv_sem=rsem.at[3 * i + r - 1], device_id=_chip_peer(xi, yi, ci, r), device_id_type=MESH)
            for i in range(len(srcs)) for r in (1, 2, 3)]


def _side_chip_exchange(pieces):
    return _Side(pieces, [SDS(p.shape, p.dtype) for p in pieces], 3 * len(pieces), _chip_exchange_copies)


def _chip_exchange(pieces):
    n = len(pieces)

    def body(*refs):
        cps = _chip_exchange_copies(refs[:n], refs[n:2 * n], *refs[2 * n:])
        for cp in cps:
            cp.start()
        for cp in cps:
            cp.wait()

    return pl.pallas_call(body, in_specs=[ANY] * n, out_specs=[ANY] * n, out_shape=[SDS(p.shape, p.dtype) for p in pieces],
                          scratch_shapes=[pltpu.SemaphoreType.DMA((3 * n,)), pltpu.SemaphoreType.DMA((3 * n,))], name="grad_chip_exchange")(*pieces)


def _final_sum(own, land, jc, shard, l, L, name):
    P, RH, CS = own.shape

    def body(jc_ref, o_ref, a_ref, b_ref, c_ref, *rest):
        rest[-1][...] = ((o_ref[...] + a_ref[...].astype(F32)) + b_ref[...].astype(F32)) + c_ref[...].astype(F32)

    blk = pl.BlockSpec((None, RH, CS), lambda p, jc: (p, 0, 0))
    in_specs = [blk] + [pl.BlockSpec((None, None, RH, CS), functools.partial(lambda r, p, jc: (r, p, 0, 0), r)) for r in (1, 2, 3)]
    args = [jc, own, land, land, land]
    if shard is not None:
        in_specs.append(ANY)
        args.append(shard)
    return pl.pallas_call(
        body, grid_spec=pltpu.PrefetchScalarGridSpec(
            num_scalar_prefetch=1, grid=(P,), in_specs=in_specs,
            out_specs=pl.BlockSpec((None, None, None, RH, CS), lambda p, jc: (l, p, jc[1], 0, 0))),
        out_shape=SDS((L, P, 2, RH, CS), F32), input_output_aliases={5: 0} if shard is not None else {},
        compiler_params=_params("arbitrary"), name=name)(*args)


def _halves_exchange(shards):
    n = len(shards)

    def body(*refs):
        outs, (ssem, rsem) = refs[n:2 * n], refs[2 * n:]
        xi, yi, ci, _ = _place()
        cps = []
        for i in range(n):
            mine = outs[i].at[:, :, ci]
            cps.append(pltpu.make_async_remote_copy(src_ref=mine, dst_ref=mine, send_sem=ssem.at[i], recv_sem=rsem.at[i],
                                                    device_id=(xi, yi, 1 - ci), device_id_type=MESH))
            cps[-1].start()
        for i in range(n):
            land = outs[i].at[:, :, 1 - ci]
            pltpu.make_async_remote_copy(src_ref=land, dst_ref=land, send_sem=ssem.at[i], recv_sem=rsem.at[i],
                                         device_id=(xi, yi, 1 - ci), device_id_type=MESH).wait_recv()
        for cp in cps:
            cp.wait_send()

    return pl.pallas_call(body, in_specs=[ANY] * n, out_specs=[ANY] * n, out_shape=[SDS(s.shape, s.dtype) for s in shards],
                          input_output_aliases={i: i for i in range(n)},
                          scratch_shapes=[pltpu.SemaphoreType.DMA((n,)), pltpu.SemaphoreType.DMA((n,))], name="grad_halves_exchange")(*shards)


def _reduce_small(part):
    NR, Wd = part.shape
    ND = 2 * N_CHIPS

    def body(p_ref, o_ref, land, ssem, rsem):
        xi, yi, ci, j = _place()
        me = 2 * j + ci
        land[me] = p_ref[...]
        cps = []
        for rr in range(1, ND):
            dev = (xi ^ (rr >> 2), yi ^ ((rr >> 1) & 1), ci ^ (rr & 1))
            cps.append(pltpu.make_async_remote_copy(src_ref=p_ref, dst_ref=land.at[me], send_sem=ssem.at[rr - 1], recv_sem=rsem.at[rr - 1],
                                                    device_id=dev, device_id_type=MESH))
            cps[-1].start()
        for rr in range(1, ND):
            got = land.at[me ^ rr]
            pltpu.make_async_remote_copy(src_ref=got, dst_ref=got, send_sem=ssem.at[rr - 1], recv_sem=rsem.at[rr - 1],
                                         device_id=(xi, yi, ci), device_id_type=MESH).wait_recv()
        acc = land[0]
        for d in range(1, ND):
            acc = acc + land[d]
        o_ref[...] = acc
        for cp in cps:
            cp.wait_send()

    vm = pl.BlockSpec(memory_space=pltpu.VMEM)
    return pl.pallas_call(body, in_specs=[vm], out_specs=vm, out_shape=SDS((NR, Wd), F32),
                          scratch_shapes=[pltpu.VMEM((ND, NR, Wd), F32), pltpu.SemaphoreType.DMA((ND - 1,)), pltpu.SemaphoreType.DMA((ND - 1,))],
                          name="small_grad_allreduce")(part)


def _adamw(w, g, m, v, name):
    shape = w.shape
    C = shape[-1]
    R = w.size // C
    tb = _tile(R, max(8, (1 << 18) // C), 8)

    def body(w_ref, g_ref, m_ref, v_ref, d_ref, mo_ref, vo_ref):
        g = g_ref[...]
        m = ADAM_B1 * m_ref[...] + (1.0 - ADAM_B1) * g
        v = ADAM_B2 * v_ref[...] + (1.0 - ADAM_B2) * jnp.square(g)
        m_hat = m / (1.0 - ADAM_B1 ** ADAM_STEP)
        v_hat = v / (1.0 - ADAM_B2 ** ADAM_STEP)
        d_ref[...] = -ADAM_LR * (m_hat / (jnp.sqrt(v_hat) + ADAM_EPS) + ADAM_WD * w_ref[...])
        mo_ref[...] = m
        vo_ref[...] = v

    blk = pl.BlockSpec((tb, C), lambda i: (i, 0))
    outs = pl.pallas_call(body, grid=(R // tb,), in_specs=[blk] * 4, out_specs=[blk] * 3, out_shape=[SDS((R, C), F32)] * 3,
                          compiler_params=_params("parallel"), name=name)(*[t.reshape(R, C) for t in (w, g, m, v)])
    return [t.reshape(shape) for t in outs]


WEIGHTS = ("mix_norm_g", "w_in", "conv_dw_w", "conv_dw_b", "conv_ln_g", "conv_ln_b", "w_conv_out", "w_pool_grp", "pool_scale", "w_out",
           "xattn_norm_g", "mem_norm_g", "w_q", "w_kv", "w_o", "ffn_norm_g", "w_up", "ffn_dw_w", "w_down", "final_norm_g")
VECTORS = ("mix_norm_g", "conv_dw_b", "conv_ln_g", "conv_ln_b", "pool_scale", "xattn_norm_g", "mem_norm_g", "ffn_norm_g", "final_norm_g")


def _shard_view(t, kind):
    L, P, R, C = t.shape
    return t.reshape(L, P, 2, R // 2, C)


def _rows(t, width):
    return t.reshape(-1, width)


def _pack(parts):
    return jnp.concatenate([jnp.pad(p, ((0, (-p.shape[0]) % 8), (0, 0))) for p in parts], axis=0)


def kernel(x, mem, mix_norm_g, w_in, conv_dw_w, conv_dw_b, conv_ln_g, conv_ln_b, w_conv_out, w_pool_grp, pool_scale, w_out, xattn_norm_g, mem_norm_g, w_q, w_kv, w_o, ffn_norm_g, w_up, ffn_dw_w, w_down, final_norm_g, loss_target, m_mix_norm_g, m_w_in, m_conv_dw_w, m_conv_dw_b, m_conv_ln_g, m_conv_ln_b, m_w_conv_out, m_w_pool_grp, m_pool_scale, m_w_out, m_xattn_norm_g, m_mem_norm_g, m_w_q, m_w_kv, m_w_o, m_ffn_norm_g, m_w_up, m_ffn_dw_w, m_w_down, m_final_norm_g, v_mix_norm_g, v_w_in, v_conv_dw_w, v_conv_dw_b, v_conv_ln_g, v_conv_ln_b, v_w_conv_out, v_w_pool_grp, v_pool_scale, v_w_out, v_xattn_norm_g, v_mem_norm_g, v_w_q, v_w_kv, v_w_o, v_ffn_norm_g, v_w_up, v_ffn_dw_w, v_w_down, v_final_norm_g):
    w = dict(mix_norm_g=mix_norm_g, w_in=w_in, conv_dw_w=conv_dw_w, conv_dw_b=conv_dw_b, conv_ln_g=conv_ln_g, conv_ln_b=conv_ln_b,
             w_conv_out=w_conv_out, w_pool_grp=w_pool_grp, pool_scale=pool_scale, w_out=w_out, xattn_norm_g=xattn_norm_g,
             mem_norm_g=mem_norm_g, w_q=w_q, w_kv=w_kv, w_o=w_o, ffn_norm_g=ffn_norm_g, w_up=w_up, ffn_dw_w=ffn_dw_w, w_down=w_down,
             final_norm_g=final_norm_g)
    m = dict(zip(WEIGHTS, (m_mix_norm_g, m_w_in, m_conv_dw_w, m_conv_dw_b, m_conv_ln_g, m_conv_ln_b, m_w_conv_out, m_w_pool_grp, m_pool_scale,
                           m_w_out, m_xattn_norm_g, m_mem_norm_g, m_w_q, m_w_kv, m_w_o, m_ffn_norm_g, m_w_up, m_ffn_dw_w, m_w_down, m_final_norm_g)))
    v = dict(zip(WEIGHTS, (v_mix_norm_g, v_w_in, v_conv_dw_w, v_conv_dw_b, v_conv_ln_g, v_conv_ln_b, v_w_conv_out, v_w_pool_grp, v_pool_scale,
                           v_w_out, v_xattn_norm_g, v_mem_norm_g, v_w_q, v_w_kv, v_w_o, v_ffn_norm_g, v_w_up, v_ffn_dw_w, v_w_down, v_final_norm_g)))
    xi, yi, ci, j = _place()
    jc = jnp.stack([j, ci]).astype(jnp.int32)
    L = w_in.shape[0]
    G = len(POOL_WINDOWS)
    kinds = dict(BIG)

    def to_mat(name, t):
        if name == "w_pool":
            return jnp.swapaxes(t, 2, 3)
        return t[:, None]

    def from_mat(name, t):
        if name == "w_pool":
            return jnp.swapaxes(t, 2, 3)
        return t[:, 0]

    src = {name: w["w_pool_grp" if name == "w_pool" else name] for name, _ in BIG}

    KC, cs_c = conv_dw_w.shape[1], conv_dw_w.shape[2]
    KF, cs_f = ffn_dw_w.shape[1], ffn_dw_w.shape[2]
    taps = jnp.concatenate([conv_dw_w.reshape(L * KC, cs_c), ffn_dw_w.reshape(L * KF * (cs_f // cs_c), cs_c)], axis=0)
    n_taps = taps.shape[0]
    taps = jnp.pad(taps, ((0, (-n_taps) % 16), (0, 0)))
    names = [name for name, _ in BIG]
    mats = {name: to_mat(name, src[name]).astype(BF) for name in names}

    def layer_shards(l, subset):
        return [_shard_view(mats[name][l:l + 1], kinds[name]) for name in subset]

    def as_weights(subset, fulls):
        return {name: f.reshape(G if name == "w_pool" else 1, -1, f.shape[-1]) for name, f in zip(subset, fulls)}

    fulls = _gather_weights(layer_shards(0, names) + [_shard_view(taps[None, None], "row")], [kinds[name] for name in names] + ["row"])
    W = [as_weights(names, fulls[:-1])]
    taps_all = fulls[-1].reshape(N_CHIPS, -1, cs_c)[:, :n_taps]
    V = {name: w[name] for name in VECTORS}
    V["conv_dw_w"] = taps_all[:, :L * KC].reshape(N_CHIPS, L, KC, cs_c).transpose(1, 2, 0, 3).reshape(L, KC, N_CHIPS * cs_c)
    V["ffn_dw_w"] = taps_all[:, L * KC:].reshape(N_CHIPS, L, KF, cs_f).transpose(1, 2, 0, 3).reshape(L, KF, N_CHIPS * cs_f)

    Bn, S, D = x.shape
    Mn = mem.shape[1]
    dims = (Bn, S, Mn, D, conv_dw_b.shape[1], w_down.shape[1] * N_CHIPS)
    xt = x.reshape(Bn * S, D)
    memf = mem.reshape(Bn * Mn, D)
    mem_n = _rms_fwd(memf, V["mem_norm_g"], "mem_norm")
    saved = []
    for l in range(L):
        sides = _Sides()
        if l + 1 < L:
            sides = _Sides({key: _side_gather(layer_shards(l + 1, subset), [kinds[name] for name in subset]) for key, subset in FWD_CARRY.items()})
        xt, sv = _layer_fwd(xt, mem_n, W[l], V, l, dims, sides)
        saved.append(sv)
        if l + 1 < L:
            nxt = {}
            for key, subset in FWD_CARRY.items():
                done = _gather_pass(sides.landed[key], layer_shards(l + 1, subset), [kinds[name] for name in subset], f"gather_pass_l{l + 1}_{key}")
                nxt.update(as_weights(subset, done))
            W.append(nxt)
    loss, dx, dgf = _loss_bwd(xt, V["final_norm_g"], loss_target.reshape(Bn * S, D), "loss")
    loss = lax.psum(loss[0, 0], ("x", "y", "c"))

    late_names = [name for name in names if name not in EARLY]

    def chip_sums(gw, subset, l, tag):
        gv = []
        for name in subset:
            g = gw[name] if gw[name].ndim == 3 else gw[name][None]
            P, R, C = g.shape
            gv.append(g.reshape(P, 2, R // 2, C) if kinds[name] == "col" else g.reshape(P, N_CHIPS, 2, R // (2 * N_CHIPS), C))
        lands = _sibling_exchange(gv, [kinds[name] for name in subset], f"grad_sibling_exchange_{tag}_l{l}")
        own, pieces = {}, {}
        for name, g, land in zip(subset, gv, lands):
            own[name], pieces[name] = _chip_sum(g, land, kinds[name], jc, f"chip_sum_{name}_{l}")
        return own, pieces

    def carry(table, pieces):
        return _Sides({key: _side_chip_exchange([pieces[name] for name in subset]) for key, subset in table.items()})

    def landed(table, sides):
        return {name: land for key, subset in table.items() for name, land in zip(subset, sides.landed[key])}

    dxb, dmem_n = dx, None
    smalls, owns, got = [None] * L, [{} for _ in range(L)], [{} for _ in range(L)]
    late = None
    for l in reversed(range(L)):
        sides = carry(BWD_CARRY_LATE, late) if late is not None else _Sides()
        dx, dxb, gw, sm = _layer_bwd_mlp(dx, dxb, saved[l], W[l], V, l, dims, sides)
        if late is not None:
            got[l + 1].update(landed(BWD_CARRY_LATE, sides))
        own, early = chip_sums(gw, EARLY, l, "mlp")
        owns[l].update(own)
        sides = carry(BWD_CARRY_EARLY, early)
        dx, dxb, dmem_n, gw, sm2 = _layer_bwd_mix(dx, dxb, dmem_n, saved[l], mem_n, W[l], V, l, dims, sides)
        got[l].update(landed(BWD_CARRY_EARLY, sides))
        smalls[l] = {**sm, **sm2}
        own, late = chip_sums(gw, late_names, l, "mix")
        owns[l].update(own)
    got[0].update(zip(late_names, _chip_exchange([late[name] for name in late_names])))
    owns = [[o[name] for name in names] for o in owns]
    got = [[g[name] for name in names] for g in got]
    grad_x = dx.reshape(Bn, S, D)
    _, _, dgm = _rms_bwd(memf, V["mem_norm_g"], dmem_n, None, "mem_norm_b")
    small = {k: jnp.stack([sm[k] for sm in smalls]) if k in ("conv_dw_w", "ffn_dw_w") else jnp.concatenate([sm[k] for sm in smalls], axis=0)
             for k in smalls[0]}
    small["mem_norm_g"] = dgm
    small["final_norm_g"] = dgf
    mine = []
    for i, name in enumerate(names):
        shard = None
        for l in range(L):
            shard = _final_sum(owns[l][i], got[l][i], jc, shard, l, L, f"final_sum_{name}_{l}")
        mine.append(shard)
    gshards = _halves_exchange(mine)
    grads = {}
    for (name, kind), gs in zip(BIG, gshards):
        Lg, P, _, RH, CS = gs.shape
        grads["w_pool_grp" if name == "w_pool" else name] = from_mat(name, gs.reshape(Lg, P, 2 * RH, CS))

    small_w = conv_dw_b.shape[1]
    order = VECTORS + ("conv_dw_w", "ffn_dw_w")
    parts = [_rows(small[name], small_w) for name in order]
    counts = [p.shape[0] for p in parts]
    summed = _reduce_small(_pack(parts))
    off = 0
    for name, cnt in zip(order, counts):
        t = summed[off:off + cnt]
        off += cnt + (-cnt) % 8
        if name in VECTORS:
            grads[name] = t.reshape(w[name].shape)
        else:
            full = t.reshape(small[name].shape)
            cs = w[name].shape[2]
            grads[name] = lax.dynamic_slice_in_dim(full, j * cs, cs, axis=2)

    delta, new_m, new_v = {}, {}, {}
    for name, _ in BIG:
        key = "w_pool_grp" if name == "w_pool" else name
        outs = _adamw(*[to_mat(name, t) for t in (w[key], grads[key], m[key], v[key])], "adamw_" + name)
        delta[key], new_m[key], new_v[key] = [from_mat(name, t) for t in outs]
    vec = [_pack([_rows(d[name], small_w) for name in VECTORS]) for d in (w, grads, m, v)]
    outs = _adamw(*vec, "adamw_vectors")
    off = 0
    for name in VECTORS:
        cnt = w[name].size // small_w
        for d, t in zip((delta, new_m, new_v), outs):
            d[name] = t[off:off + cnt].reshape(w[name].shape)
        off += cnt + (-cnt) % 8
    for name in ("conv_dw_w", "ffn_dw_w"):
        delta[name], new_m[name], new_v[name] = _adamw(w[name], grads[name], m[name], v[name], "adamw_" + name)

    return (loss, grad_x, *[grads[k] for k in WEIGHTS], *[delta[k] for k in WEIGHTS], *[new_m[k] for k in WEIGHTS], *[new_v[k] for k in WEIGHTS])
```

```python
import functools

import jax
import jax.numpy as jnp
from jax import lax
from jax.experimental import pallas as pl
from jax.experimental.pallas import tpu as pltpu

F32 = jnp.float32
BF = jnp.bfloat16
SDS = jax.ShapeDtypeStruct
MESH = pl.DeviceIdType.MESH
ANY = pl.BlockSpec(memory_space=pl.ANY)

EPS = 1e-6
XA_HEADS = 4
POOL_WINDOWS = (2, 4, 8, 16)
N_CHIPS = 4
ADAM_LR, ADAM_B1, ADAM_B2, ADAM_EPS, ADAM_WD, ADAM_STEP = 0.001, 0.9, 0.999, 1e-08, 0.01, 10

LANES = 128
ROW_BLOCK = 512
VMEM_LIMIT = 56 * 1024 * 1024


def _params(*sem):
    return pltpu.CompilerParams(dimension_semantics=sem if sem else None, vmem_limit_bytes=VMEM_LIMIT)


def _tile(n, cap, mult=LANES):
    if n <= cap:
        return n
    for t in range(cap - cap % mult, 0, -mult):
        if n % t == 0:
            return t
    return n


_DN = {"nn": (((1,), (0,)), ((), ())), "nt": (((1,), (1,)), ((), ())), "tn": (((0,), (0,)), ((), ()))}


class _Side:
    def __init__(self, ins, outs, n, make):
        self.ins, self.outs, self.n, self.make = list(ins), list(outs), n, make


def _call(body, args, *, grid, in_specs, out_specs, out_shape, semantics, name, scratch_shapes=(), side=None):
    if side is None:
        return pl.pallas_call(body, grid=grid, in_specs=in_specs, out_specs=out_specs, out_shape=out_shape, scratch_shapes=list(scratch_shapes),
                              compiler_params=_params(*semantics), name=name)(*args)
    n_in, n_out, n_scr, n_si, n_so = len(args), len(out_shape), len(scratch_shapes), len(side.ins), len(side.outs)

    def carrying(*refs):
        ins, s_in = refs[:n_in], refs[n_in:n_in + n_si]
        outs, s_out = refs[n_in + n_si:n_in + n_si + n_out], refs[n_in + n_si + n_out:n_in + n_si + n_out + n_so]
        scr = refs[n_in + n_si + n_out + n_so:]
        copies = side.make(s_in, s_out, scr[n_scr], scr[n_scr + 1])
        ids = [pl.program_id(d) for d in range(len(grid))]
        first, last = ids[0] == 0, ids[0] == grid[0] - 1
        for d in range(1, len(grid)):
            first, last = first & (ids[d] == 0), last & (ids[d] == grid[d] - 1)

        @pl.when(first)
        def _():
            for cp in copies:
                cp.start()

        body(*ins, *outs, *scr[:n_scr])

        @pl.when(last)
        def _():
            for cp in copies:
                cp.wait()

    outs = pl.pallas_call(
        carrying, grid=grid, in_specs=list(in_specs) + [ANY] * n_si, out_specs=list(out_specs) + [ANY] * n_so,
        out_shape=list(out_shape) + side.outs,
        scratch_shapes=list(scratch_shapes) + [pltpu.SemaphoreType.DMA((side.n,)), pltpu.SemaphoreType.DMA((side.n,))],
        compiler_params=_params(*["arbitrary"] * len(grid)), name=name)(*args, *side.ins)
    return list(outs[:n_out]), list(outs[n_out:])


MM_VMEM_BUDGET = 40 * 1024 * 1024
MM_STEP_MACS = 2200 * 1024 * 1024
MXU_WIDTH = 256
MM_STEP_COST_BYTES = 1 << 20


def _divisors(n):
    return [t for t in range(LANES, n + 1, LANES) if n % t == 0] or [n]


def _mm_tiles(M, N, K, a_bytes, b_bytes, o_bytes):
    best = None
    for tk in _divisors(K):
        for tm in _divisors(M):
            for tn in _divisors(N):
                nk = K // tk
                foot = 2 * (tm * tk * a_bytes + tk * tn * b_bytes + tm * tn * o_bytes) + (tm * tn * 4 if nk > 1 else 0)
                if foot > MM_VMEM_BUDGET or tm * tn * tk > MM_STEP_MACS or tn < min(N, MXU_WIDTH) or tm < min(M, MXU_WIDTH):
                    continue
                steps = (M // tm) * (N // tn) * nk
                traffic = M * K * a_bytes * (N // tn if nk > 1 else 1) + K * N * b_bytes * (M // tm) + M * N * o_bytes
                exposed = tm * tk * a_bytes + tk * tn * b_bytes + tm * tn * o_bytes
                cost = traffic + exposed + steps * MM_STEP_COST_BYTES + (nk - 1) * M * N * 8
                if best is None or cost < best[0]:
                    best = (cost, tm, tn, tk)
    assert best is not None, (M, N, K)
    return best[1:]


def _mm(a, b, dims, out_dtype, name, res=None, bl=None, side=None, twin=None):
    bs = b.shape[1:] if bl is not None else b.shape
    if dims == "nn":
        (M, K), (K2, N) = a.shape, bs
    elif dims == "nt":
        (M, K), (N, K2) = a.shape, bs
    else:
        (K, M), (K2, N) = a.shape, bs
    assert K == K2, (name, a.shape, b.shape)
    tm, tn, tk = _mm_tiles(M, N, K, a.dtype.itemsize, b.dtype.itemsize, jnp.dtype(out_dtype).itemsize
                           + (res.dtype.itemsize if res is not None else 0) + (jnp.dtype(twin).itemsize if twin is not None else 0))
    nk = K // tk
    lead = (None,) if bl is not None else ()
    pre = (lambda *ix: (bl,) + ix) if bl is not None else (lambda *ix: ix)
    if dims == "tn":
        a_spec = pl.BlockSpec((tk, tm), lambda i, j, k: (k, i))
    else:
        a_spec = pl.BlockSpec((tm, tk), lambda i, j, k: (i, k))
    if dims == "nt":
        b_spec = pl.BlockSpec(lead + (tn, tk), lambda i, j, k: pre(j, k))
    else:
        b_spec = pl.BlockSpec(lead + (tk, tn), lambda i, j, k: pre(k, j))
    o_spec = pl.BlockSpec((tm, tn), lambda i, j, k: (i, j))
    in_specs, args = [a_spec, b_spec], [a, b]
    if res is not None:
        in_specs.append(o_spec)
        args.append(res)
    n_main = len(args)
    n_out = 1 if twin is None else 2

    def body(*refs):
        a_ref, b_ref = refs[0], refs[1]
        r_ref = refs[2] if res is not None else None
        o_ref = refs[n_main]
        p = lax.dot_general(a_ref[...].astype(BF), b_ref[...].astype(BF), _DN[dims], preferred_element_type=F32)

        def finish(t):
            if r_ref is not None:
                t = t + r_ref[...]
            o_ref[...] = t.astype(out_dtype)
            if twin is not None:
                refs[n_main + 1][...] = t.astype(twin)

        if nk == 1:
            finish(p)
        else:
            acc = refs[n_main + n_out]
            k = pl.program_id(2)

            @pl.when(k == 0)
            def _():
                acc[...] = p

            @pl.when(k > 0)
            def _():
                acc[...] += p

            @pl.when(k == nk - 1)
            def _():
                finish(acc[...])

    got = _call(body, args, grid=(M // tm, N // tn, nk), in_specs=in_specs, out_specs=[o_spec] * n_out,
                out_shape=[SDS((M, N), out_dtype)] + ([SDS((M, N), twin)] if twin is not None else []),
                scratch_shapes=[pltpu.VMEM((tm, tn), F32)] if nk > 1 else [], semantics=("parallel", "parallel", "arbitrary"),
                name=name, side=side)
    outs, landed = (got, None) if side is None else got
    out = outs[0] if twin is None else (outs[0], outs[1])
    return out if side is None else (out, landed)


def _rms(x, g):
    return x * lax.rsqrt(jnp.mean(x * x, axis=-1, keepdims=True) + EPS) * g


def _ln_silu(x, g, b):
    mu = jnp.mean(x, axis=-1, keepdims=True)
    xc = x - mu
    var = jnp.mean(xc * xc, axis=-1, keepdims=True)
    return jax.nn.silu(xc * lax.rsqrt(var + EPS) * g + b)


def _merge(gc, gp, yc, yp, ps):
    return jax.nn.sigmoid(gc) * yc + jax.nn.sigmoid(gp) * (yp * ps)


def _gated(gate, val):
    return jax.nn.gelu(gate) * val


def _rms_fwd(x, g, name):
    T, D = x.shape
    tb = _tile(T, ROW_BLOCK, 8)

    def body(x_ref, g_ref, o_ref):
        o_ref[...] = _rms(x_ref[...], g_ref[...]).astype(BF)

    row = pl.BlockSpec((tb, D), lambda i: (i, 0))
    return pl.pallas_call(body, grid=(T // tb,), in_specs=[row, pl.BlockSpec((1, D), lambda i: (0, 0))], out_specs=row,
                          out_shape=SDS((T, D), BF), compiler_params=_params("parallel"), name=name)(x, g.reshape(1, D))


def _rms_bwd(x, g, dh, dres, name):
    T, D = x.shape
    tb = _tile(T, ROW_BLOCK, 8)

    def body(*refs):
        if dres is not None:
            x_ref, g_ref, dh_ref, dres_ref, dx_ref, dxb_ref, dg_ref = refs
        else:
            x_ref, g_ref, dh_ref, dx_ref, dxb_ref, dg_ref = refs
        _, vjp = jax.vjp(_rms, x_ref[...], g_ref[...])
        dx, dg = vjp(dh_ref[...].astype(F32))
        if dres is not None:
            dx = dx + dres_ref[...]
        dx_ref[...] = dx
        dxb_ref[...] = dx.astype(BF)

        @pl.when(pl.program_id(0) == 0)
        def _():
            dg_ref[...] = jnp.zeros_like(dg_ref)

        dg_ref[...] += dg

    row = pl.BlockSpec((tb, D), lambda i: (i, 0))
    vec = pl.BlockSpec((1, D), lambda i: (0, 0))
    ins = [x, g.reshape(1, D), dh] + ([dres] if dres is not None else [])
    return pl.pallas_call(
        body, grid=(T // tb,), in_specs=[row, vec, row] + ([row] if dres is not None else []), out_specs=[row, row, vec],
        out_shape=[SDS((T, D), F32), SDS((T, D), BF), SDS((1, D), F32)], compiler_params=_params("arbitrary"), name=name)(*ins)


def _loss_bwd(x, g, target, name):
    T, D = x.shape
    tb = _tile(T, ROW_BLOCK, 8)
    nb = T // tb

    def body(x_ref, g_ref, t_ref, loss_ref, dx_ref, dg_ref, acc):
        i = pl.program_id(0)
        y, vjp = jax.vjp(_rms, x_ref[...], g_ref[...])
        err = y - t_ref[...]
        dx, dg = vjp(err * (1.0 / D))
        dx_ref[...] = dx

        @pl.when(i == 0)
        def _():
            dg_ref[...] = jnp.zeros_like(dg_ref)
            acc[...] = jnp.zeros_like(acc)

        dg_ref[...] += dg
        acc[...] += jnp.sum(err * err, axis=0, keepdims=True)

        @pl.when(i == nb - 1)
        def _():
            loss_ref[...] = jnp.full(loss_ref.shape, (0.5 / D) * jnp.sum(acc[...]), F32)

    row = pl.BlockSpec((tb, D), lambda i: (i, 0))
    vec = pl.BlockSpec((1, D), lambda i: (0, 0))
    return pl.pallas_call(
        body, grid=(nb,), in_specs=[row, vec, row], out_specs=[pl.BlockSpec((1, LANES), lambda i: (0, 0)), row, vec],
        out_shape=[SDS((1, LANES), F32), SDS((T, D), F32), SDS((1, D), F32)], scratch_shapes=[pltpu.VMEM((1, D), F32)],
        compiler_params=_params("arbitrary"), name=name)(x, g.reshape(1, D), target)


def _ln_silu_fwd(cv, g, b, name):
    T, C = cv.shape
    tb = _tile(T, ROW_BLOCK, 8)

    def body(x_ref, g_ref, b_ref, o_ref):
        o_ref[...] = _ln_silu(x_ref[...], g_ref[...], b_ref[...]).astype(BF)

    row = pl.BlockSpec((tb, C), lambda i: (i, 0))
    vec = pl.BlockSpec((1, C), lambda i: (0, 0))
    return pl.pallas_call(body, grid=(T // tb,), in_specs=[row, vec, vec], out_specs=row, out_shape=SDS((T, C), BF),
                          compiler_params=_params("parallel"), name=name)(cv, g.reshape(1, C), b.reshape(1, C))


def _ln_silu_bwd(cv, g, b, dy, name):
    T, C = cv.shape
    tb = _tile(T, ROW_BLOCK, 8)

    def body(x_ref, g_ref, b_ref, dy_ref, dx_ref, dg_ref, db_ref):
        _, vjp = jax.vjp(_ln_silu, x_ref[...], g_ref[...], b_ref[...])
        dx, dg, db = vjp(dy_ref[...].astype(F32))
        dx_ref[...] = dx

        @pl.when(pl.program_id(0) == 0)
        def _():
            dg_ref[...] = jnp.zeros_like(dg_ref)
            db_ref[...] = jnp.zeros_like(db_ref)

        dg_ref[...] += dg
        db_ref[...] += db

    row = pl.BlockSpec((tb, C), lambda i: (i, 0))
    vec = pl.BlockSpec((1, C), lambda i: (0, 0))
    return pl.pallas_call(
        body, grid=(T // tb,), in_specs=[row, vec, vec, row], out_specs=[row, vec, vec],
        out_shape=[SDS((T, C), F32), SDS((1, C), F32), SDS((1, C), F32)], compiler_params=_params("arbitrary"),
        name=name)(cv, g.reshape(1, C), b.reshape(1, C), dy)


def _merge_fwd(proj, yc, yp, ps, C, name):
    T, D = yc.shape
    tb = _tile(T, ROW_BLOCK, 8)
    nj = D // C

    def body(gc_ref, gp_ref, yc_ref, yp_ref, ps_ref, o_ref):
        o_ref[...] = _merge(gc_ref[...], gp_ref[...], yc_ref[...], yp_ref[...], ps_ref[...]).astype(BF)

    blk = pl.BlockSpec((tb, C), lambda i, j: (i, j))
    return pl.pallas_call(
        body, grid=(T // tb, nj),
        in_specs=[pl.BlockSpec((tb, C), lambda i, j: (i, 3 + j)), pl.BlockSpec((tb, C), lambda i, j: (i, 3 + nj + j)), blk, blk,
                  pl.BlockSpec((1, C), lambda i, j: (0, j))],
        out_specs=blk, out_shape=SDS((T, D), BF), compiler_params=_params("parallel", "parallel"), name=name)(proj, proj, yc, yp, ps.reshape(1, D))


def _merge_bwd(proj, yc, yp, ps, dm, C, name, side=None):
    T, D = yc.shape
    tb = _tile(T, ROW_BLOCK, 8)
    nj = D // C

    def body(gc_ref, gp_ref, yc_ref, yp_ref, ps_ref, dm_ref, dgc_ref, dgp_ref, dyc_ref, dyp_ref, dps_ref):
        _, vjp = jax.vjp(_merge, gc_ref[...], gp_ref[...], yc_ref[...], yp_ref[...], ps_ref[...])
        dgc, dgp, dyc, dyp, dps = vjp(dm_ref[...].astype(F32))
        dgc_ref[...] = dgc.astype(BF)
        dgp_ref[...] = dgp.astype(BF)
        dyc_ref[...] = dyc.astype(BF)
        dyp_ref[...] = dyp.astype(BF)

        @pl.when(pl.program_id(1) == 0)
        def _():
            dps_ref[...] = jnp.zeros_like(dps_ref)

        dps_ref[...] += dps

    blk = pl.BlockSpec((tb, C), lambda j, i: (i, j))
    vec = pl.BlockSpec((1, C), lambda j, i: (0, j))
    return _call(
        body, (proj, proj, yc, yp, ps.reshape(1, D), dm), grid=(nj, T // tb),
        in_specs=[pl.BlockSpec((tb, C), lambda j, i: (i, 3 + j)), pl.BlockSpec((tb, C), lambda j, i: (i, 3 + nj + j)), blk, blk, vec, blk],
        out_specs=[blk, blk, blk, blk, vec], out_shape=[SDS((T, D), BF)] * 4 + [SDS((1, D), F32)],
        semantics=("parallel", "arbitrary"), name=name, side=side)


def _shd(v, s, rows):
    if s == 0:
        return v
    return jnp.where(rows >= s, pltpu.roll(v, s, 0), 0.0)


def _shu(v, s, rows):
    if s == 0:
        return v
    n = v.shape[0]
    return jnp.where(rows < n - s, pltpu.roll(v, n - s, 0), 0.0)


def _glu_conv_fwd(proj, w, b, Bn, S, C, name):
    K = w.shape[0]
    sl = min(LANES, C)
    ns = C // sl

    def body(a_ref, gl_ref, w_ref, b_ref, o_ref):
        y0 = a_ref[...] * jax.nn.sigmoid(gl_ref[...])
        rows = lax.broadcasted_iota(jnp.int32, y0.shape, 0)
        acc = jnp.zeros_like(y0) + b_ref[...]
        for k in range(K):
            acc = acc + w_ref[k:k + 1, :] * _shd(y0, K - 1 - k, rows)
        o_ref[...] = acc

    return pl.pallas_call(
        body, grid=(Bn, ns),
        in_specs=[pl.BlockSpec((S, sl), lambda bi, j: (bi, j)), pl.BlockSpec((S, sl), lambda bi, j: (bi, ns + j)),
                  pl.BlockSpec((K, sl), lambda bi, j: (0, j)), pl.BlockSpec((1, sl), lambda bi, j: (0, j))],
        out_specs=pl.BlockSpec((S, sl), lambda bi, j: (bi, j)), out_shape=SDS((Bn * S, C), F32),
        compiler_params=_params("parallel", "parallel"), name=name)(proj, proj, w, b.reshape(1, C))


def _glu_conv_bwd(proj, w, dcv, Bn, S, C, name, side=None):
    K = w.shape[0]
    sl = min(LANES, C)
    ns = C // sl

    def body(a_ref, gl_ref, w_ref, d_ref, da_ref, dgl_ref, dw_ref, db_ref):
        a = a_ref[...]
        sg = jax.nn.sigmoid(gl_ref[...])
        y0 = a * sg
        d = d_ref[...]
        rows = lax.broadcasted_iota(jnp.int32, y0.shape, 0)

        @pl.when(pl.program_id(1) == 0)
        def _():
            dw_ref[...] = jnp.zeros_like(dw_ref)
            db_ref[...] = jnp.zeros_like(db_ref)

        dy0 = jnp.zeros_like(y0)
        for k in range(K):
            s = K - 1 - k
            dw_ref[k:k + 1, :] += jnp.sum(d * _shd(y0, s, rows), axis=0, keepdims=True)
            dy0 = dy0 + w_ref[k:k + 1, :] * _shu(d, s, rows)
        db_ref[...] += jnp.sum(d, axis=0, keepdims=True)
        da_ref[...] = (dy0 * sg).astype(BF)
        dgl_ref[...] = (dy0 * a * sg * (1.0 - sg)).astype(BF)

    blk = pl.BlockSpec((S, sl), lambda j, bi: (bi, j))
    return _call(
        body, (proj, proj, w, dcv), grid=(ns, Bn),
        in_specs=[blk, pl.BlockSpec((S, sl), lambda j, bi: (bi, ns + j)), pl.BlockSpec((K, sl), lambda j, bi: (0, j)), blk],
        out_specs=[blk, blk, pl.BlockSpec((K, sl), lambda j, bi: (0, j)), pl.BlockSpec((1, sl), lambda j, bi: (0, j))],
        out_shape=[SDS((Bn * S, C), BF), SDS((Bn * S, C), BF), SDS((K, C), F32), SDS((1, C), F32)],
        semantics=("parallel", "arbitrary"), name=name, side=side)


def _pool_z(u, g, rows):
    s2 = u + _shd(u, 1, rows)
    s4 = s2 + _shd(s2, 2, rows)
    s8 = s4 + _shd(s4, 4, rows)
    s16 = s8 + _shd(s8, 8, rows)
    sw = jnp.where(g == 0, s2, jnp.where(g == 1, s4, jnp.where(g == 2, s8, s16)))
    cnt = jnp.minimum(rows + 1, POOL_WINDOWS[0] << g).astype(F32)
    return sw / cnt - u, cnt


def _pool_fwd(proj, wpt, l, Bn, S, C, D, name):
    G = len(POOL_WINDOWS)
    gd, go = C // G, D // G

    def body(u_ref, w_ref, o_ref):
        g = pl.program_id(1)
        u = u_ref[...]
        rows = lax.broadcasted_iota(jnp.int32, u.shape, 0)
        zp, _ = _pool_z(u, g, rows)
        o_ref[...] = lax.dot_general(zp.astype(BF), w_ref[...], _DN["nt"], preferred_element_type=F32)

    return pl.pallas_call(
        body, grid=(Bn, G),
        in_specs=[pl.BlockSpec((S, gd), lambda bi, g: (bi, 2 * G + g)), pl.BlockSpec((None, go, gd), lambda bi, g: (l * G + g, 0, 0))],
        out_specs=pl.BlockSpec((S, go), lambda bi, g: (bi, g)), out_shape=SDS((Bn * S, D), F32),
        compiler_params=_params("parallel", "parallel"), name=name)(proj, wpt)


def _pool_bwd(proj, wpt, dyp, l, Bn, S, C, D, name):
    G = len(POOL_WINDOWS)
    gd, go = C // G, D // G

    def body(u_ref, w_ref, d_ref, du_ref, dw_ref):
        g = pl.program_id(0)
        u = u_ref[...]
        rows = lax.broadcasted_iota(jnp.int32, u.shape, 0)
        zp, cnt = _pool_z(u, g, rows)
        d = d_ref[...]
        dzp = lax.dot_general(d, w_ref[...], _DN["nn"], preferred_element_type=F32)

        @pl.when(pl.program_id(1) == 0)
        def _():
            dw_ref[...] = jnp.zeros_like(dw_ref)

        dw_ref[...] += lax.dot_general(d, zp.astype(BF), _DN["tn"], preferred_element_type=F32)
        dsw = dzp / cnt
        zero = jnp.zeros_like(dsw)
        d16 = jnp.where(g == 3, dsw, zero)
        d8 = jnp.where(g == 2, dsw, zero) + d16 + _shu(d16, 8, rows)
        d4 = jnp.where(g == 1, dsw, zero) + d8 + _shu(d8, 4, rows)
        d2 = jnp.where(g == 0, dsw, zero) + d4 + _shu(d4, 2, rows)
        d1 = d2 + _shu(d2, 1, rows)
        du_ref[...] = (d1 - dzp).astype(BF)

    return pl.pallas_call(
        body, grid=(G, Bn),
        in_specs=[pl.BlockSpec((S, gd), lambda g, bi: (bi, 2 * G + g)), pl.BlockSpec((None, go, gd), lambda g, bi: (l * G + g, 0, 0)),
                  pl.BlockSpec((S, go), lambda g, bi: (bi, g))],
        out_specs=[pl.BlockSpec((S, gd), lambda g, bi: (bi, g)), pl.BlockSpec((None, go, gd), lambda g, bi: (g, 0, 0))],
        out_shape=[SDS((Bn * S, C), BF), SDS((G, go, gd), F32)],
        compiler_params=_params("parallel", "arbitrary"), name=name)(proj, wpt, dyp)


def _ffn_conv(u, w_ref, rows):
    K = w_ref.shape[0]
    acc = w_ref[K - 1:K, :] * u
    for k in range(K - 1):
        acc = acc + w_ref[k:k + 1, :] * _shd(u, K - 1 - k, rows)
    return acc


def _ffn_cb(F):
    return _tile(F, 256)


def _ffn_act_fwd(up0, w, Bn, S, F, name):
    cb = _ffn_cb(F)
    nj = F // cb

    def body(g_ref, v_ref, wg_ref, wv_ref, o_ref):
        rows = lax.broadcasted_iota(jnp.int32, g_ref.shape, 0)
        o_ref[...] = _gated(_ffn_conv(g_ref[...], wg_ref, rows), _ffn_conv(v_ref[...], wv_ref, rows)).astype(BF)

    K = w.shape[0]
    return pl.pallas_call(
        body, grid=(Bn, nj),
        in_specs=[pl.BlockSpec((S, cb), lambda bi, j: (bi, j)), pl.BlockSpec((S, cb), lambda bi, j: (bi, nj + j)),
                  pl.BlockSpec((K, cb), lambda bi, j: (0, j)), pl.BlockSpec((K, cb), lambda bi, j: (0, nj + j))],
        out_specs=pl.BlockSpec((S, cb), lambda bi, j: (bi, j)), out_shape=SDS((Bn * S, F), BF),
        compiler_params=_params("parallel", "parallel"), name=name)(up0, up0, w, w)


SUBLANES = 8
FFN_HALO = SUBLANES
FFN_ROWS = 64
GELU_C0, GELU_C1 = 0.7978845608028654, 0.044715


def _gelu_and_grad(x):
    x2 = x * x
    t = jnp.tanh(GELU_C0 * (x + GELU_C1 * (x2 * x)))
    cdf = 0.5 * (1.0 + t)
    return x * cdf, cdf + (0.5 * GELU_C0) * x * (1.0 - t * t) * (1.0 + (3.0 * GELU_C1) * x2)


def _ffn_act_bwd(up0, w, dg, Bn, S, F, name, side=None):
    cb = min(LANES, F)
    nj = F // cb
    K = w.shape[0]
    rc = FFN_ROWS if S % FFN_ROWS == 0 else S
    win = rc + 2 * FFN_HALO
    assert K - 1 <= FFN_HALO and rc % SUBLANES == 0

    def body(g_ref, v_ref, wg_ref, wv_ref, d_ref, dgo_ref, dvo_ref, dwg_ref, dwv_ref, gp, vp, dp):
        for pad, src in ((gp, g_ref), (vp, v_ref), (dp, d_ref)):
            pad[0:FFN_HALO, :] = jnp.zeros((FFN_HALO, cb), F32)
            pad[FFN_HALO + S:, :] = jnp.zeros((FFN_HALO, cb), F32)
            pad[FFN_HALO:FFN_HALO + S, :] = src[...].astype(F32)
        wg = [wg_ref[k:k + 1, :] for k in range(K)]
        wv = [wv_ref[k:k + 1, :] for k in range(K)]

        def conv(u, ws):
            acc = ws[K - 1] * u
            for k in range(K - 1):
                acc = acc + ws[k] * pltpu.roll(u, K - 1 - k, 0)
            return acc

        def conv_t(dc, ws):
            acc = ws[K - 1] * dc
            for k in range(K - 1):
                acc = acc + ws[k] * pltpu.roll(dc, win - (K - 1 - k), 0)
            return acc

        def fold(t):
            acc = t[FFN_HALO:FFN_HALO + SUBLANES]
            for i in range(1, rc // SUBLANES):
                acc = acc + t[FFN_HALO + SUBLANES * i:FFN_HALO + SUBLANES * (i + 1)]
            return acc

        def chunk(c, sums):
            r0 = pl.multiple_of(c * rc, SUBLANES)
            g, v, d = gp[pl.ds(r0, win), :], vp[pl.ds(r0, win), :], dp[pl.ds(r0, win), :]
            ge, dge = _gelu_and_grad(conv(g, wg))
            dgc = d * conv(v, wv) * dge
            dvc = d * ge
            dgo_ref[pl.ds(r0, rc), :] = conv_t(dgc, wg)[FFN_HALO:FFN_HALO + rc].astype(BF)
            dvo_ref[pl.ds(r0, rc), :] = conv_t(dvc, wv)[FFN_HALO:FFN_HALO + rc].astype(BF)
            new = []
            for u, dc in ((g, dgc), (v, dvc)):
                for k in range(K):
                    new.append(fold(dc * (u if k == K - 1 else pltpu.roll(u, K - 1 - k, 0))))
            return tuple(a + b for a, b in zip(sums, new))

        sums = lax.fori_loop(0, S // rc, chunk, tuple(jnp.zeros((SUBLANES, cb), F32) for _ in range(2 * K)))

        @pl.when(pl.program_id(1) == 0)
        def _():
            dwg_ref[...] = jnp.zeros_like(dwg_ref)
            dwv_ref[...] = jnp.zeros_like(dwv_ref)

        for k in range(K):
            dwg_ref[k:k + 1, :] += jnp.sum(sums[k], axis=0, keepdims=True)
            dwv_ref[k:k + 1, :] += jnp.sum(sums[K + k], axis=0, keepdims=True)

    blk = pl.BlockSpec((S, cb), lambda j, bi: (bi, j))
    wblk = pl.BlockSpec((K, cb), lambda j, bi: (0, j))
    return _call(
        body, (up0, up0, w, w, dg), grid=(nj, Bn),
        in_specs=[blk, pl.BlockSpec((S, cb), lambda j, bi: (bi, nj + j)), wblk, pl.BlockSpec((K, cb), lambda j, bi: (0, nj + j)), blk],
        out_specs=[blk, blk, wblk, wblk],
        out_shape=[SDS((Bn * S, F), BF), SDS((Bn * S, F), BF), SDS((K, F), F32), SDS((K, F), F32)],
        scratch_shapes=[pltpu.VMEM((S + 2 * FFN_HALO, cb), F32)] * 3, semantics=("parallel", "arbitrary"), name=name, side=side)


def _softmax_rows(q, k, scale):
    sc = lax.dot_general(q, k, _DN["nt"], preferred_element_type=F32) * scale
    e = jnp.exp(sc - jnp.max(sc, axis=-1, keepdims=True))
    return e / jnp.sum(e, axis=-1, keepdims=True)


def _attn_ts(S):
    return _tile(S, 1024, 8)


def _attn_fwd(q, kv, Bn, S, Mn, D, name):
    H = XA_HEADS
    dh = D // H
    ts = _attn_ts(S)
    nsb = S // ts
    scale = dh ** -0.5

    def body(q_ref, k_ref, v_ref, o_ref):
        p = _softmax_rows(q_ref[...], k_ref[...], scale)
        o_ref[...] = lax.dot_general(p.astype(BF), v_ref[...], _DN["nn"], preferred_element_type=F32).astype(BF)

    qblk = pl.BlockSpec((ts, dh), lambda bi, h, s: (bi * nsb + s, h))
    return pl.pallas_call(
        body, grid=(Bn, H, nsb),
        in_specs=[qblk, pl.BlockSpec((Mn, dh), lambda bi, h, s: (bi, h)), pl.BlockSpec((Mn, dh), lambda bi, h, s: (bi, H + h))],
        out_specs=qblk, out_shape=SDS((Bn * S, D), BF), compiler_params=_params("parallel", "parallel", "parallel"), name=name)(q, kv, kv)


def _attn_bwd(q, kv, datt, Bn, S, Mn, D, name):
    H = XA_HEADS
    dh = D // H
    ts = _attn_ts(S)
    nsb = S // ts
    scale = dh ** -0.5

    def body(q_ref, k_ref, v_ref, do_ref, dq_ref, dk_ref, dv_ref):
        q, k, v, do = q_ref[...], k_ref[...], v_ref[...], do_ref[...]
        p = _softmax_rows(q, k, scale)
        dp = lax.dot_general(do, v, _DN["nt"], preferred_element_type=F32)
        ds = (p * (dp - jnp.sum(dp * p, axis=-1, keepdims=True)) * scale).astype(BF)
        dq_ref[...] = lax.dot_general(ds, k, _DN["nn"], preferred_element_type=F32).astype(BF)

        @pl.when(pl.program_id(2) == 0)
        def _():
            dk_ref[...] = jnp.zeros_like(dk_ref)
            dv_ref[...] = jnp.zeros_like(dv_ref)

        dk_ref[...] += lax.dot_general(ds, q, _DN["tn"], preferred_element_type=F32)
        dv_ref[...] += lax.dot_general(p.astype(BF), do, _DN["tn"], preferred_element_type=F32)

    qblk = pl.BlockSpec((ts, dh), lambda bi, h, s: (bi * nsb + s, h))
    kblk = pl.BlockSpec((Mn, dh), lambda bi, h, s: (bi, h))
    return pl.pallas_call(
        body, grid=(Bn, H, nsb),
        in_specs=[qblk, kblk, pl.BlockSpec((Mn, dh), lambda bi, h, s: (bi, H + h)), qblk],
        out_specs=[qblk, kblk, kblk], out_shape=[SDS((Bn * S, D), BF), SDS((Bn * Mn, D), F32), SDS((Bn * Mn, D), F32)],
        compiler_params=_params("parallel", "parallel", "arbitrary"), name=name)(q, kv, kv, datt)


class _Sides:
    def __init__(self, by_key=None):
        self.by_key, self.landed = dict(by_key or {}), {}

    def run(self, key, fn, *args, **kw):
        side = self.by_key.get(key)
        if side is None:
            return fn(*args, **kw)
        out, self.landed[key] = fn(*args, side=side, **kw)
        return out

    def mm(self, key, *args, **kw):
        return self.run(key, _mm, *args, **kw)


def _layer_fwd(x, mem_n, W, V, l, dims, sides):
    Bn, S, Mn, D, C, F = dims
    n = f"l{l}_"
    h = _rms_fwd(x, V["mix_norm_g"][l], n + "mix_norm")
    proj = sides.mm("proj", h, W["w_in"], "nn", F32, n + "proj", bl=0)
    cv = _glu_conv_fwd(proj, V["conv_dw_w"][l], V["conv_dw_b"][l], Bn, S, C, n + "glu_conv")
    yc1 = _ln_silu_fwd(cv, V["conv_ln_g"][l], V["conv_ln_b"][l], n + "ln_silu")
    yc = sides.mm("conv_out", yc1, W["w_conv_out"], "nn", F32, n + "conv_out", bl=0)
    yp = _pool_fwd(proj, W["w_pool"], 0, Bn, S, C, D, n + "pool")
    merged = _merge_fwd(proj, yc, yp, V["pool_scale"][l], C, n + "merge")
    x1 = sides.mm("out_proj", merged, W["w_out"], "nn", F32, n + "out_proj", res=x, bl=0)
    hq = _rms_fwd(x1, V["xattn_norm_g"][l], n + "xattn_norm")
    q = sides.mm("q_proj", hq, W["w_q"], "nn", BF, n + "q_proj", bl=0)
    kv = _mm(mem_n, W["w_kv"], "nn", BF, n + "kv_proj", bl=0)
    att = _attn_fwd(q, kv, Bn, S, Mn, D, n + "attn")
    x2 = sides.mm("o_proj", att, W["w_o"], "nn", F32, n + "o_proj", res=x1, bl=0)
    hf = _rms_fwd(x2, V["ffn_norm_g"][l], n + "ffn_norm")
    up0 = sides.mm("up_proj", hf, W["w_up"], "nn", F32, n + "up_proj", bl=0)
    gact = _ffn_act_fwd(up0, V["ffn_dw_w"][l], Bn, S, F, n + "ffn_act")
    x3 = sides.mm("down_proj", gact, W["w_down"], "nn", F32, n + "down_proj", res=x2, bl=0)
    return x3, dict(x=x, h=h, proj=proj, cv=cv, yc1=yc1, yc=yc, yp=yp, merged=merged, x1=x1, hq=hq, q=q, kv=kv, att=att, x2=x2, hf=hf,
                    up0=up0, gact=gact)


def _layer_bwd_mlp(dx, dxb, sv, W, V, l, dims, sides):
    Bn, S, Mn, D, C, F = dims
    n = f"l{l}_b_"
    gw, sm = {}, {}
    dgact = sides.mm("d_gact", dxb, W["w_down"], "nt", BF, n + "d_gact", bl=0)
    gw["w_down"] = sides.mm("dw_down", sv["gact"], dxb, "tn", F32, n + "dw_down", twin=BF)
    dg0, dv0, dwg, dwv = sides.run("ffn_act_b", _ffn_act_bwd, sv["up0"], V["ffn_dw_w"][l], dgact, Bn, S, F, n + "ffn_act")
    sm["ffn_dw_w"] = jnp.concatenate([dwg, dwv], axis=1)
    dup0 = jnp.concatenate([dg0, dv0], axis=1)
    dhf = sides.mm("d_hf", dup0, W["w_up"], "nt", F32, n + "d_hf", bl=0)
    gw["w_up"] = sides.mm("dw_up", sv["hf"], dup0, "tn", F32, n + "dw_up", twin=BF)
    dx2, dx2b, sm["ffn_norm_g"] = _rms_bwd(sv["x2"], V["ffn_norm_g"][l], dhf, dx, n + "ffn_norm")
    return dx2, dx2b, gw, sm


def _layer_bwd_mix(dx2, dx2b, dmem_n, sv, mem_n, W, V, l, dims, sides):
    Bn, S, Mn, D, C, F = dims
    n = f"l{l}_b_"
    gw, sm = {}, {}
    datt = _mm(dx2b, W["w_o"], "nt", BF, n + "d_att", bl=0)
    gw["w_o"] = _mm(sv["att"], dx2b, "tn", F32, n + "dw_o", twin=BF)
    dq, dk, dv = _attn_bwd(sv["q"], sv["kv"], datt, Bn, S, Mn, D, n + "attn")
    dkv = jnp.concatenate([dk, dv], axis=1)
    gw["w_kv"] = _mm(mem_n, dkv, "tn", F32, n + "dw_kv", twin=BF)
    dmem_n = _mm(dkv, W["w_kv"], "nt", F32, n + "d_mem", res=dmem_n, bl=0)
    dhq = _mm(dq, W["w_q"], "nt", F32, n + "d_hq", bl=0)
    gw["w_q"] = _mm(sv["hq"], dq, "tn", F32, n + "dw_q", twin=BF)
    dx1, dx1b, sm["xattn_norm_g"] = _rms_bwd(sv["x1"], V["xattn_norm_g"][l], dhq, dx2, n + "xattn_norm")
    dmerged = _mm(dx1b, W["w_out"], "nt", BF, n + "d_merged", bl=0)
    gw["w_out"] = _mm(sv["merged"], dx1b, "tn", F32, n + "dw_out", twin=BF)
    dgc, dgp, dyc, dyp, sm["pool_scale"] = sides.run("merge_b", _merge_bwd, sv["proj"], sv["yc"], sv["yp"], V["pool_scale"][l], dmerged, C, n + "merge")
    du, dwp = _pool_bwd(sv["proj"], W["w_pool"], dyp, 0, Bn, S, C, D, n + "pool")
    gw["w_pool"] = (dwp, dwp.astype(BF))
    dyc1 = _mm(dyc, W["w_conv_out"], "nt", F32, n + "d_yc1", bl=0)
    gw["w_conv_out"] = _mm(sv["yc1"], dyc, "tn", F32, n + "dw_conv_out", twin=BF)
    dcv, sm["conv_ln_g"], sm["conv_ln_b"] = _ln_silu_bwd(sv["cv"], V["conv_ln_g"][l], V["conv_ln_b"][l], dyc1, n + "ln_silu")
    da, dgl, sm["conv_dw_w"], sm["conv_dw_b"] = sides.run("glu_conv_b", _glu_conv_bwd, sv["proj"], V["conv_dw_w"][l], dcv, Bn, S, C, n + "glu_conv")
    dproj = jnp.concatenate([da, dgl, du, dgc, dgp], axis=1)
    dh = sides.mm("d_h", dproj, W["w_in"], "nt", F32, n + "d_h", bl=0)
    gw["w_in"] = sides.mm("dw_in", sv["h"], dproj, "tn", F32, n + "dw_in", twin=BF)
    dx, dxb, sm["mix_norm_g"] = _rms_bwd(sv["x"], V["mix_norm_g"][l], dh, dx1, n + "mix_norm")
    return dx, dxb, dmem_n, gw, sm


BIG = (("w_in", "col"), ("w_conv_out", "col"), ("w_pool", "row"), ("w_out", "row"), ("w_q", "row"), ("w_kv", "col"),
       ("w_o", "row"), ("w_up", "col"), ("w_down", "row"))
FWD_CARRY = {"proj": ("w_in",), "conv_out": ("w_conv_out", "w_pool"), "out_proj": ("w_out", "w_q"), "q_proj": ("w_o",), "o_proj": ("w_kv",),
             "up_proj": ("w_up",), "down_proj": ("w_down",)}
EARLY = ("w_down", "w_up")
BWD_CARRY_EARLY = {"merge_b": ("w_down",), "glu_conv_b": ("w_up",)}
BWD_CARRY_LATE = {"ffn_act_b": ("w_in", "w_conv_out", "w_pool", "w_out", "w_q", "w_kv", "w_o")}


def _place():
    xi, yi, ci = lax.axis_index("x"), lax.axis_index("y"), lax.axis_index("c")
    return xi, yi, ci, 2 * xi + yi


def _chip_peer(xi, yi, ci, r):
    return (xi ^ (r >> 1), yi ^ (r & 1), ci)


def _full_shard(ref, kind, k, cs):
    if kind == "col":
        return ref.at[:, :, :, :, pl.ds(pl.multiple_of(k * cs, cs), cs)]
    return ref.at[:, :, k]


def _gather_weights(shards, kinds):
    n = len(shards)
    outs = []
    for s, kind in zip(shards, kinds):
        L, P, _, RH, CS = s.shape
        outs.append(SDS((L, P, 2, RH, CS * N_CHIPS) if kind == "col" else (L, P, N_CHIPS, 2, RH, CS), s.dtype))
    per = 7

    def body(*refs):
        srcs, fulls, (ssem, rsem) = refs[:n], refs[n:2 * n], refs[2 * n:]
        xi, yi, ci, j = _place()
        sib = (xi, yi, 1 - ci)

        def piece(i, k, c):
            kind, cs = kinds[i], shards[i].shape[-1]
            if kind == "col":
                return fulls[i].at[:, :, c, :, pl.ds(pl.multiple_of(k * cs, cs), cs)]
            return fulls[i].at[:, :, k, c]

        def copy(i, slot, src, dst, dev):
            return pltpu.make_async_remote_copy(src_ref=src, dst_ref=dst, send_sem=ssem.at[per * i + slot], recv_sem=rsem.at[per * i + slot],
                                                device_id=dev, device_id_type=MESH)

        own, first, passed = [], [], []
        for i in range(n):
            for r in (1, 2, 3):
                first.append(copy(i, r - 1, srcs[i].at[:, :, ci], piece(i, j, ci), _chip_peer(xi, yi, ci, r)))
                first[-1].start()
        for i in range(n):
            own.append(copy(i, 6, srcs[i], _full_shard(fulls[i], kinds[i], j, shards[i].shape[-1]), sib))
            own[-1].start()
        for i in range(n):
            for r in (1, 2, 3):
                got = piece(i, j ^ r, ci)
                copy(i, r - 1, got, got, sib).wait_recv()
                passed.append(copy(i, 2 + r, got, got, sib))
                passed[-1].start()
        for i in range(n):
            for r in (1, 2, 3):
                got = piece(i, j ^ r, 1 - ci)
                copy(i, 2 + r, got, got, sib).wait_recv()
        for cp in own:
            cp.wait()
        for cp in first + passed:
            cp.wait_send()

    return pl.pallas_call(
        body, in_specs=[ANY] * n, out_specs=[ANY] * n, out_shape=outs,
        scratch_shapes=[pltpu.SemaphoreType.DMA((per * n,)), pltpu.SemaphoreType.DMA((per * n,))], name="gather_weights")(*shards)


def _full_sds(s, kind):
    L, P, _, RH, CS = s.shape
    return SDS((L, P, 2, RH, CS * N_CHIPS) if kind == "col" else (L, P, N_CHIPS, 2, RH, CS), s.dtype)


def _gather_piece(full, kind, cs, k, c):
    if kind == "col":
        return full.at[:, :, c, :, pl.ds(pl.multiple_of(k * cs, cs), cs)]
    return full.at[:, :, k, c]


def _side_gather(shards, kinds):
    n = len(shards)

    def make(srcs, fulls, ssem, rsem):
        xi, yi, ci, j = _place()
        return [pltpu.make_async_remote_copy(
            src_ref=srcs[i].at[:, :, ci], dst_ref=_gather_piece(fulls[i], kinds[i], shards[i].shape[-1], j, ci), send_sem=ssem.at[3 * i + r - 1],
            recv_sem=rsem.at[3 * i + r - 1], device_id=_chip_peer(xi, yi, ci, r), device_id_type=MESH) for i in range(n) for r in (1, 2, 3)]

    return _Side(shards, [_full_sds(s, k) for s, k in zip(shards, kinds)], 3 * n, make)


def _gather_pass(fulls, shards, kinds, name):
    n = len(fulls)

    def body(*refs):
        srcs, outs, (ssem, rsem) = refs[n:2 * n], refs[2 * n:3 * n], refs[3 * n:]
        xi, yi, ci, j = _place()
        sib = (xi, yi, 1 - ci)
        cps = []
        for i in range(n):
            cs = shards[i].shape[-1]
            for r in (1, 2, 3):
                got = _gather_piece(outs[i], kinds[i], cs, j ^ r, ci)
                cps.append(pltpu.make_async_remote_copy(src_ref=got, dst_ref=got, send_sem=ssem.at[4 * i + r - 1], recv_sem=rsem.at[4 * i + r - 1],
                                                        device_id=sib, device_id_type=MESH))
            cps.append(pltpu.make_async_remote_copy(src_ref=srcs[i], dst_ref=_full_shard(outs[i], kinds[i], j, cs), send_sem=ssem.at[4 * i + 3],
                                                    recv_sem=rsem.at[4 * i + 3], device_id=sib, device_id_type=MESH))
        for cp in cps:
            cp.start()
        for cp in cps:
            cp.wait()

    return pl.pallas_call(
        body, in_specs=[ANY] * (2 * n), out_specs=[ANY] * n, out_shape=[SDS(f.shape, f.dtype) for f in fulls],
        input_output_aliases={i: i for i in range(n)},
        scratch_shapes=[pltpu.SemaphoreType.DMA((4 * n,)), pltpu.SemaphoreType.DMA((4 * n,))], name=name)(*fulls, *shards)


def _sibling_exchange(gviews, kinds, name):
    n = len(gviews)
    outs = [SDS(g.shape[:1] + g.shape[2:] if kind == "col" else g.shape[:2] + g.shape[3:], g.dtype) for g, kind in zip(gviews, kinds)]

    def body(*refs):
        gs, lands, (ssem, rsem) = refs[:n], refs[n:2 * n], refs[2 * n:]
        xi, yi, ci, _ = _place()
        cps = []
        for i in range(n):
            src = gs[i].at[:, 1 - ci] if kinds[i] == "col" else gs[i].at[:, :, 1 - ci]
            cps.append(pltpu.make_async_remote_copy(src_ref=src, dst_ref=lands[i], send_sem=ssem.at[i], recv_sem=rsem.at[i],
                                                    device_id=(xi, yi, 1 - ci), device_id_type=MESH))
            cps[-1].start()
        for cp in cps:
            cp.wait()

    return pl.pallas_call(body, in_specs=[ANY] * n, out_specs=[ANY] * n, out_shape=outs,
                          scratch_shapes=[pltpu.SemaphoreType.DMA((n,)), pltpu.SemaphoreType.DMA((n,))], name=name)(*gviews)


def _chip_sum(g, land, kind, jc, name):
    if kind == "col":
        P, _, RH, C = g.shape
        CS = C // N_CHIPS
        g_spec = pl.BlockSpec((None, None, RH, CS), lambda p, r, jc: (p, jc[1], 0, jc[0] ^ r))
        l_spec = pl.BlockSpec((None, RH, CS), lambda p, r, jc: (p, 0, jc[0] ^ r))
    else:
        P, _, _, RH, CS = g.shape
        g_spec = pl.BlockSpec((None, None, None, RH, CS), lambda p, r, jc: (p, jc[0] ^ r, jc[1], 0, 0))
        l_spec = pl.BlockSpec((None, None, RH, CS), lambda p, r, jc: (p, jc[0] ^ r, 0, 0))

    def body(jc_ref, g_ref, l_ref, own_ref, all_ref):
        s = g_ref[...] + l_ref[...].astype(F32)
        all_ref[...] = s.astype(BF)

        @pl.when(pl.program_id(1) == 0)
        def _():
            own_ref[...] = s

    return pl.pallas_call(
        body, grid_spec=pltpu.PrefetchScalarGridSpec(
            num_scalar_prefetch=1, grid=(P, N_CHIPS), in_specs=[g_spec, l_spec],
            out_specs=[pl.BlockSpec((None, RH, CS), lambda p, r, jc: (p, 0, 0)), pl.BlockSpec((None, None, RH, CS), lambda p, r, jc: (r, p, 0, 0))]),
        out_shape=[SDS((P, RH, CS), F32), SDS((N_CHIPS, P, RH, CS), BF)], compiler_params=_params("parallel", "arbitrary"), name=name)(jc, g, land)


def _chip_exchange_copies(srcs, lands, ssem, rsem):
    xi, yi, ci, _ = _place()
    return [pltpu.make_async_remote_copy(src_ref=srcs[i].at[r], dst_ref=lands[i].at[r], send_sem=ssem.at[3 * i + r - 1],
                                         recv_sem=rsem.at[3 * i + r - 1], device_id=_chip_peer(xi, yi, ci, r), device_id_type=MESH)
            for i in range(len(srcs)) for r in (1, 2, 3)]


def _side_chip_exchange(pieces):
    return _Side(pieces, [SDS(p.shape, p.dtype) for p in pieces], 3 * len(pieces), _chip_exchange_copies)


def _chip_exchange(pieces):
    n = len(pieces)

    def body(*refs):
        cps = _chip_exchange_copies(refs[:n], refs[n:2 * n], *refs[2 * n:])
        for cp in cps:
            cp.start()
        for cp in cps:
            cp.wait()

    return pl.pallas_call(body, in_specs=[ANY] * n, out_specs=[ANY] * n, out_shape=[SDS(p.shape, p.dtype) for p in pieces],
                          scratch_shapes=[pltpu.SemaphoreType.DMA((3 * n,)), pltpu.SemaphoreType.DMA((3 * n,))], name="grad_chip_exchange")(*pieces)


def _final_sum(own, land, jc, shard, l, L, name):
    P, RH, CS = own.shape

    def body(jc_ref, o_ref, a_ref, b_ref, c_ref, *rest):
        rest[-1][...] = ((o_ref[...] + a_ref[...].astype(F32)) + b_ref[...].astype(F32)) + c_ref[...].astype(F32)

    blk = pl.BlockSpec((None, RH, CS), lambda p, jc: (p, 0, 0))
    in_specs = [blk] + [pl.BlockSpec((None, None, RH, CS), functools.partial(lambda r, p, jc: (r, p, 0, 0), r)) for r in (1, 2, 3)]
    args = [jc, own, land, land, land]
    if shard is not None:
        in_specs.append(ANY)
        args.append(shard)
    return pl.pallas_call(
        body, grid_spec=pltpu.PrefetchScalarGridSpec(
            num_scalar_prefetch=1, grid=(P,), in_specs=in_specs,
            out_specs=pl.BlockSpec((None, None, None, RH, CS), lambda p, jc: (l, p, jc[1], 0, 0))),
        out_shape=SDS((L, P, 2, RH, CS), F32), input_output_aliases={5: 0} if shard is not None else {},
        compiler_params=_params("arbitrary"), name=name)(*args)


def _halves_exchange(shards):
    n = len(shards)

    def body(*refs):
        outs, (ssem, rsem) = refs[n:2 * n], refs[2 * n:]
        xi, yi, ci, _ = _place()
        cps = []
        for i in range(n):
            mine = outs[i].at[:, :, ci]
            cps.append(pltpu.make_async_remote_copy(src_ref=mine, dst_ref=mine, send_sem=ssem.at[i], recv_sem=rsem.at[i],
                                                    device_id=(xi, yi, 1 - ci), device_id_type=MESH))
            cps[-1].start()
        for i in range(n):
            land = outs[i].at[:, :, 1 - ci]
            pltpu.make_async_remote_copy(src_ref=land, dst_ref=land, send_sem=ssem.at[i], recv_sem=rsem.at[i],
                                         device_id=(xi, yi, 1 - ci), device_id_type=MESH).wait_recv()
        for cp in cps:
            cp.wait_send()

    return pl.pallas_call(body, in_specs=[ANY] * n, out_specs=[ANY] * n, out_shape=[SDS(s.shape, s.dtype) for s in shards],
                          input_output_aliases={i: i for i in range(n)},
                          scratch_shapes=[pltpu.SemaphoreType.DMA((n,)), pltpu.SemaphoreType.DMA((n,))], name="grad_halves_exchange")(*shards)


def _reduce_small(part):
    NR, Wd = part.shape
    ND = 2 * N_CHIPS

    def body(p_ref, o_ref, land, ssem, rsem):
        xi, yi, ci, j = _place()
        me = 2 * j + ci
        land[me] = p_ref[...]
        cps = []
        for rr in range(1, ND):
            dev = (xi ^ (rr >> 2), yi ^ ((rr >> 1) & 1), ci ^ (rr & 1))
            cps.append(pltpu.make_async_remote_copy(src_ref=p_ref, dst_ref=land.at[me], send_sem=ssem.at[rr - 1], recv_sem=rsem.at[rr - 1],
                                                    device_id=dev, device_id_type=MESH))
            cps[-1].start()
        for rr in range(1, ND):
            got = land.at[me ^ rr]
            pltpu.make_async_remote_copy(src_ref=got, dst_ref=got, send_sem=ssem.at[rr - 1], recv_sem=rsem.at[rr - 1],
                                         device_id=(xi, yi, ci), device_id_type=MESH).wait_recv()
        acc = land[0]
        for d in range(1, ND):
            acc = acc + land[d]
        o_ref[...] = acc
        for cp in cps:
            cp.wait_send()

    vm = pl.BlockSpec(memory_space=pltpu.VMEM)
    return pl.pallas_call(body, in_specs=[vm], out_specs=vm, out_shape=SDS((NR, Wd), F32),
                          scratch_shapes=[pltpu.VMEM((ND, NR, Wd), F32), pltpu.SemaphoreType.DMA((ND - 1,)), pltpu.SemaphoreType.DMA((ND - 1,))],
                          name="small_grad_allreduce")(part)


def _adamw(w, g, m, v, name):
    shape = w.shape
    C = shape[-1]
    R = w.size // C
    tb = _tile(R, max(8, (1 << 18) // C), 8)

    def body(w_ref, g_ref, m_ref, v_ref, d_ref, mo_ref, vo_ref):
        g = g_ref[...]
        m = ADAM_B1 * m_ref[...] + (1.0 - ADAM_B1) * g
        v = ADAM_B2 * v_ref[...] + (1.0 - ADAM_B2) * jnp.square(g)
        m_hat = m / (1.0 - ADAM_B1 ** ADAM_STEP)
        v_hat = v / (1.0 - ADAM_B2 ** ADAM_STEP)
        d_ref[...] = -ADAM_LR * (m_hat / (jnp.sqrt(v_hat) + ADAM_EPS) + ADAM_WD * w_ref[...])
        mo_ref[...] = m
        vo_ref[...] = v

    blk = pl.BlockSpec((tb, C), lambda i: (i, 0))
    outs = pl.pallas_call(body, grid=(R // tb,), in_specs=[blk] * 4, out_specs=[blk] * 3, out_shape=[SDS((R, C), F32)] * 3,
                          compiler_params=_params("parallel"), name=name)(*[t.reshape(R, C) for t in (w, g, m, v)])
    return [t.reshape(shape) for t in outs]


WEIGHTS = ("mix_norm_g", "w_in", "conv_dw_w", "conv_dw_b", "conv_ln_g", "conv_ln_b", "w_conv_out", "w_pool_grp", "pool_scale", "w_out",
           "xattn_norm_g", "mem_norm_g", "w_q", "w_kv", "w_o", "ffn_norm_g", "w_up", "ffn_dw_w", "w_down", "final_norm_g")
VECTORS = ("mix_norm_g", "conv_dw_b", "conv_ln_g", "conv_ln_b", "pool_scale", "xattn_norm_g", "mem_norm_g", "ffn_norm_g", "final_norm_g")


def _shard_view(t, kind):
    L, P, R, C = t.shape
    return t.reshape(L, P, 2, R // 2, C)


def _rows(t, width):
    return t.reshape(-1, width)


def _pack(parts):
    return jnp.concatenate([jnp.pad(p, ((0, (-p.shape[0]) % 8), (0, 0))) for p in parts], axis=0)


def kernel(x, mem, mix_norm_g, w_in, conv_dw_w, conv_dw_b, conv_ln_g, conv_ln_b, w_conv_out, w_pool_grp, pool_scale, w_out, xattn_norm_g, mem_norm_g, w_q, w_kv, w_o, ffn_norm_g, w_up, ffn_dw_w, w_down, final_norm_g, loss_target, m_mix_norm_g, m_w_in, m_conv_dw_w, m_conv_dw_b, m_conv_ln_g, m_conv_ln_b, m_w_conv_out, m_w_pool_grp, m_pool_scale, m_w_out, m_xattn_norm_g, m_mem_norm_g, m_w_q, m_w_kv, m_w_o, m_ffn_norm_g, m_w_up, m_ffn_dw_w, m_w_down, m_final_norm_g, v_mix_norm_g, v_w_in, v_conv_dw_w, v_conv_dw_b, v_conv_ln_g, v_conv_ln_b, v_w_conv_out, v_w_pool_grp, v_pool_scale, v_w_out, v_xattn_norm_g, v_mem_norm_g, v_w_q, v_w_kv, v_w_o, v_ffn_norm_g, v_w_up, v_ffn_dw_w, v_w_down, v_final_norm_g):
    w = dict(mix_norm_g=mix_norm_g, w_in=w_in, conv_dw_w=conv_dw_w, conv_dw_b=conv_dw_b, conv_ln_g=conv_ln_g, conv_ln_b=conv_ln_b,
             w_conv_out=w_conv_out, w_pool_grp=w_pool_grp, pool_scale=pool_scale, w_out=w_out, xattn_norm_g=xattn_norm_g,
             mem_norm_g=mem_norm_g, w_q=w_q, w_kv=w_kv, w_o=w_o, ffn_norm_g=ffn_norm_g, w_up=w_up, ffn_dw_w=ffn_dw_w, w_down=w_down,
             final_norm_g=final_norm_g)
    m = dict(zip(WEIGHTS, (m_mix_norm_g, m_w_in, m_conv_dw_w, m_conv_dw_b, m_conv_ln_g, m_conv_ln_b, m_w_conv_out, m_w_pool_grp, m_pool_scale,
                           m_w_out, m_xattn_norm_g, m_mem_norm_g, m_w_q, m_w_kv, m_w_o, m_ffn_norm_g, m_w_up, m_ffn_dw_w, m_w_down, m_final_norm_g)))
    v = dict(zip(WEIGHTS, (v_mix_norm_g, v_w_in, v_conv_dw_w, v_conv_dw_b, v_conv_ln_g, v_conv_ln_b, v_w_conv_out, v_w_pool_grp, v_pool_scale,
                           v_w_out, v_xattn_norm_g, v_mem_norm_g, v_w_q, v_w_kv, v_w_o, v_ffn_norm_g, v_w_up, v_ffn_dw_w, v_w_down, v_final_norm_g)))
    xi, yi, ci, j = _place()
    jc = jnp.stack([j, ci]).astype(jnp.int32)
    L = w_in.shape[0]
    G = len(POOL_WINDOWS)
    kinds = dict(BIG)

    def to_mat(name, t):
        if name == "w_pool":
            return jnp.swapaxes(t, 2, 3)
        return t[:, None]

    def from_mat(name, t):
        if name == "w_pool":
            return jnp.swapaxes(t, 2, 3)
        return t[:, 0]

    src = {name: w["w_pool_grp" if name == "w_pool" else name] for name, _ in BIG}

    KC, cs_c = conv_dw_w.shape[1], conv_dw_w.shape[2]
    KF, cs_f = ffn_dw_w.shape[1], ffn_dw_w.shape[2]
    taps = jnp.concatenate([conv_dw_w.reshape(L * KC, cs_c), ffn_dw_w.reshape(L * KF * (cs_f // cs_c), cs_c)], axis=0)
    n_taps = taps.shape[0]
    taps = jnp.pad(taps, ((0, (-n_taps) % 16), (0, 0)))
    names = [name for name, _ in BIG]
    mats = {name: to_mat(name, src[name]).astype(BF) for name in names}

    def layer_shards(l, subset):
        return [_shard_view(mats[name][l:l + 1], kinds[name]) for name in subset]

    def as_weights(subset, fulls):
        return {name: f.reshape(G if name == "w_pool" else 1, -1, f.shape[-1]) for name, f in zip(subset, fulls)}

    fulls = _gather_weights(layer_shards(0, names) + [_shard_view(taps[None, None], "row")], [kinds[name] for name in names] + ["row"])
    W = [as_weights(names, fulls[:-1])]
    taps_all = fulls[-1].reshape(N_CHIPS, -1, cs_c)[:, :n_taps]
    V = {name: w[name] for name in VECTORS}
    V["conv_dw_w"] = taps_all[:, :L * KC].reshape(N_CHIPS, L, KC, cs_c).transpose(1, 2, 0, 3).reshape(L, KC, N_CHIPS * cs_c)
    V["ffn_dw_w"] = taps_all[:, L * KC:].reshape(N_CHIPS, L, KF, cs_f).transpose(1, 2, 0, 3).reshape(L, KF, N_CHIPS * cs_f)

    Bn, S, D = x.shape
    Mn = mem.shape[1]
    dims = (Bn, S, Mn, D, conv_dw_b.shape[1], w_down.shape[1] * N_CHIPS)
    xt = x.reshape(Bn * S, D)
    memf = mem.reshape(Bn * Mn, D)
    mem_n = _rms_fwd(memf, V["mem_norm_g"], "mem_norm")
    saved = []
    for l in range(L):
        sides = _Sides()
        if l + 1 < L:
            sides = _Sides({key: _side_gather(layer_shards(l + 1, subset), [kinds[name] for name in subset]) for key, subset in FWD_CARRY.items()})
        xt, sv = _layer_fwd(xt, mem_n, W[l], V, l, dims, sides)
        saved.append(sv)
        if l + 1 < L:
            carried = [name for subset in FWD_CARRY.values() for name in subset]
            landed = [f for key in FWD_CARRY for f in sides.landed[key]]
            done = _gather_pass(landed, layer_shards(l + 1, carried), [kinds[name] for name in carried], f"gather_pass_l{l + 1}")
            W.append(as_weights(carried, done))
    loss, dx, dgf = _loss_bwd(xt, V["final_norm_g"], loss_target.reshape(Bn * S, D), "loss")
    loss = lax.psum(loss[0, 0], ("x", "y", "c"))

    late_names = [name for name in names if name not in EARLY]

    def chip_sums(gw, subset, l, tag):
        def view(g, name):
            g = g if g.ndim == 3 else g[None]
            P, R, C = g.shape
            return g.reshape(P, 2, R // 2, C) if kinds[name] == "col" else g.reshape(P, N_CHIPS, 2, R // (2 * N_CHIPS), C)

        gv = [view(gw[name][0], name) for name in subset]
        lands = _sibling_exchange([view(gw[name][1], name) for name in subset], [kinds[name] for name in subset],
                                  f"grad_sibling_exchange_{tag}_l{l}")
        own, pieces = {}, {}
        for name, g, land in zip(subset, gv, lands):
            own[name], pieces[name] = _chip_sum(g, land, kinds[name], jc, f"chip_sum_{name}_{l}")
        return own, pieces

    def carry(table, pieces):
        return _Sides({key: _side_chip_exchange([pieces[name] for name in subset]) for key, subset in table.items()})

    def landed(table, sides):
        return {name: land for key, subset in table.items() for name, land in zip(subset, sides.landed[key])}

    dxb, dmem_n = dx, None
    smalls, owns, got = [None] * L, [{} for _ in range(L)], [{} for _ in range(L)]
    late = None
    for l in reversed(range(L)):
        sides = carry(BWD_CARRY_LATE, late) if late is not None else _Sides()
        dx, dxb, gw, sm = _layer_bwd_mlp(dx, dxb, saved[l], W[l], V, l, dims, sides)
        if late is not None:
            got[l + 1].update(landed(BWD_CARRY_LATE, sides))
        own, early = chip_sums(gw, EARLY, l, "mlp")
        owns[l].update(own)
        sides = carry(BWD_CARRY_EARLY, early)
        dx, dxb, dmem_n, gw, sm2 = _layer_bwd_mix(dx, dxb, dmem_n, saved[l], mem_n, W[l], V, l, dims, sides)
        got[l].update(landed(BWD_CARRY_EARLY, sides))
        smalls[l] = {**sm, **sm2}
        own, late = chip_sums(gw, late_names, l, "mix")
        owns[l].update(own)
    got[0].update(zip(late_names, _chip_exchange([late[name] for name in late_names])))
    owns = [[o[name] for name in names] for o in owns]
    got = [[g[name] for name in names] for g in got]
    grad_x = dx.reshape(Bn, S, D)
    _, _, dgm = _rms_bwd(memf, V["mem_norm_g"], dmem_n, None, "mem_norm_b")
    small = {k: jnp.stack([sm[k] for sm in smalls]) if k in ("conv_dw_w", "ffn_dw_w") else jnp.concatenate([sm[k] for sm in smalls], axis=0)
             for k in smalls[0]}
    small["mem_norm_g"] = dgm
    small["final_norm_g"] = dgf
    mine = []
    for i, name in enumerate(names):
        shard = None
        for l in range(L):
            shard = _final_sum(owns[l][i], got[l][i], jc, shard, l, L, f"final_sum_{name}_{l}")
        mine.append(shard)
    gshards = _halves_exchange(mine)
    grads = {}
    for (name, kind), gs in zip(BIG, gshards):
        Lg, P, _, RH, CS = gs.shape
        grads["w_pool_grp" if name == "w_pool" else name] = from_mat(name, gs.reshape(Lg, P, 2 * RH, CS))

    small_w = conv_dw_b.shape[1]
    order = VECTORS + ("conv_dw_w", "ffn_dw_w")
    parts = [_rows(small[name], small_w) for name in order]
    counts = [p.shape[0] for p in parts]
    summed = _reduce_small(_pack(parts))
    off = 0
    for name, cnt in zip(order, counts):
        t = summed[off:off + cnt]
        off += cnt + (-cnt) % 8
        if name in VECTORS:
            grads[name] = t.reshape(w[name].shape)
        else:
            full = t.reshape(small[name].shape)
            cs = w[name].shape[2]
            grads[name] = lax.dynamic_slice_in_dim(full, j * cs, cs, axis=2)

    delta, new_m, new_v = {}, {}, {}
    for name, _ in BIG:
        key = "w_pool_grp" if name == "w_pool" else name
        outs = _adamw(*[to_mat(name, t) for t in (w[key], grads[key], m[key], v[key])], "adamw_" + name)
        delta[key], new_m[key], new_v[key] = [from_mat(name, t) for t in outs]
    vec = [_pack([_rows(d[name], small_w) for name in VECTORS]) for d in (w, grads, m, v)]
    outs = _adamw(*vec, "adamw_vectors")
    off = 0
    for name in VECTORS:
        cnt = w[name].size // small_w
        for d, t in zip((delta, new_m, new_v), outs):
            d[name] = t[off:off + cnt].reshape(w[name].shape)
        off += cnt + (-cnt) % 8
    for name in ("conv_dw_w", "ffn_dw_w"):
        delta[name], new_m[name], new_v[name] = _adamw(w[name], grads[name], m[name], v[name], "adamw_" + name)

    return (loss, grad_x, *[grads[k] for k in WEIGHTS], *[delta[k] for k in WEIGHTS], *[new_m[k] for k in WEIGHTS], *[new_v[k] for k in WEIGHTS])
```

```python
import functools

import jax
import jax.numpy as jnp
from jax import lax
from jax.experimental import pallas as pl
from jax.experimental.pallas import tpu as pltpu

F32 = jnp.float32
BF = jnp.bfloat16
SDS = jax.ShapeDtypeStruct
MESH = pl.DeviceIdType.MESH
ANY = pl.BlockSpec(memory_space=pl.ANY)

EPS = 1e-6
XA_HEADS = 4
POOL_WINDOWS = (2, 4, 8, 16)
N_CHIPS = 4
ADAM_LR, ADAM_B1, ADAM_B2, ADAM_EPS, ADAM_WD, ADAM_STEP = 0.001, 0.9, 0.999, 1e-08, 0.01, 10

LANES = 128
ROW_BLOCK = 512
VMEM_LIMIT = 56 * 1024 * 1024


def _params(*sem):
    return pltpu.CompilerParams(dimension_semantics=sem if sem else None, vmem_limit_bytes=VMEM_LIMIT)


def _tile(n, cap, mult=LANES):
    if n <= cap:
        return n
    for t in range(cap - cap % mult, 0, -mult):
        if n % t == 0:
            return t
    return n


_DN = {"nn": (((1,), (0,)), ((), ())), "nt": (((1,), (1,)), ((), ())), "tn": (((0,), (0,)), ((), ()))}


class _Side:
    def __init__(self, ins, outs, n, make):
        self.ins, self.outs, self.n, self.make = list(ins), list(outs), n, make


def _call(body, args, *, grid, in_specs, out_specs, out_shape, semantics, name, scratch_shapes=(), side=None):
    if side is None:
        return pl.pallas_call(body, grid=grid, in_specs=in_specs, out_specs=out_specs, out_shape=out_shape, scratch_shapes=list(scratch_shapes),
                              compiler_params=_params(*semantics), name=name)(*args)
    n_in, n_out, n_scr, n_si, n_so = len(args), len(out_shape), len(scratch_shapes), len(side.ins), len(side.outs)

    def carrying(*refs):
        ins, s_in = refs[:n_in], refs[n_in:n_in + n_si]
        outs, s_out = refs[n_in + n_si:n_in + n_si + n_out], refs[n_in + n_si + n_out:n_in + n_si + n_out + n_so]
        scr = refs[n_in + n_si + n_out + n_so:]
        copies = side.make(s_in, s_out, scr[n_scr], scr[n_scr + 1])
        ids = [pl.program_id(d) for d in range(len(grid))]
        first, last = ids[0] == 0, ids[0] == grid[0] - 1
        for d in range(1, len(grid)):
            first, last = first & (ids[d] == 0), last & (ids[d] == grid[d] - 1)

        @pl.when(first)
        def _():
            for cp in copies:
                cp.start()

        body(*ins, *outs, *scr[:n_scr])

        @pl.when(last)
        def _():
            for cp in copies:
                cp.wait()

    outs = pl.pallas_call(
        carrying, grid=grid, in_specs=list(in_specs) + [ANY] * n_si, out_specs=list(out_specs) + [ANY] * n_so,
        out_shape=list(out_shape) + side.outs,
        scratch_shapes=list(scratch_shapes) + [pltpu.SemaphoreType.DMA((side.n,)), pltpu.SemaphoreType.DMA((side.n,))],
        compiler_params=_params(*["arbitrary"] * len(grid)), name=name)(*args, *side.ins)
    return list(outs[:n_out]), list(outs[n_out:])


MM_VMEM_BUDGET = 40 * 1024 * 1024
MM_STEP_MACS = 2200 * 1024 * 1024
MXU_WIDTH = 256
MM_STEP_COST_BYTES = 1 << 20


def _divisors(n):
    return [t for t in range(LANES, n + 1, LANES) if n % t == 0] or [n]


def _mm_tiles(M, N, K, a_bytes, b_bytes, o_bytes):
    best = None
    for tk in _divisors(K):
        for tm in _divisors(M):
            for tn in _divisors(N):
                nk = K // tk
                foot = 2 * (tm * tk * a_bytes + tk * tn * b_bytes + tm * tn * o_bytes) + (tm * tn * 4 if nk > 1 else 0)
                if foot > MM_VMEM_BUDGET or tm * tn * tk > MM_STEP_MACS or tn < min(N, MXU_WIDTH) or tm < min(M, MXU_WIDTH):
                    continue
                steps = (M // tm) * (N // tn) * nk
                traffic = M * K * a_bytes * (N // tn if nk > 1 else 1) + K * N * b_bytes * (M // tm) + M * N * o_bytes
                exposed = tm * tk * a_bytes + tk * tn * b_bytes + tm * tn * o_bytes
                cost = traffic + exposed + steps * MM_STEP_COST_BYTES + (nk - 1) * M * N * 8
                if best is None or cost < best[0]:
                    best = (cost, tm, tn, tk)
    assert best is not None, (M, N, K)
    return best[1:]


def _mm(a, b, dims, out_dtype, name, res=None, bl=None, side=None, twin=None):
    bs = b.shape[1:] if bl is not None else b.shape
    if dims == "nn":
        (M, K), (K2, N) = a.shape, bs
    elif dims == "nt":
        (M, K), (N, K2) = a.shape, bs
    else:
        (K, M), (K2, N) = a.shape, bs
    assert K == K2, (name, a.shape, b.shape)
    tm, tn, tk = _mm_tiles(M, N, K, a.dtype.itemsize, b.dtype.itemsize, jnp.dtype(out_dtype).itemsize
                           + (res.dtype.itemsize if res is not None else 0) + (jnp.dtype(twin).itemsize if twin is not None else 0))
    nk = K // tk
    lead = (None,) if bl is not None else ()
    pre = (lambda *ix: (bl,) + ix) if bl is not None else (lambda *ix: ix)
    if dims == "tn":
        a_spec = pl.BlockSpec((tk, tm), lambda i, j, k: (k, i))
    else:
        a_spec = pl.BlockSpec((tm, tk), lambda i, j, k: (i, k))
    if dims == "nt":
        b_spec = pl.BlockSpec(lead + (tn, tk), lambda i, j, k: pre(j, k))
    else:
        b_spec = pl.BlockSpec(lead + (tk, tn), lambda i, j, k: pre(k, j))
    o_spec = pl.BlockSpec((tm, tn), lambda i, j, k: (i, j))
    in_specs, args = [a_spec, b_spec], [a, b]
    if res is not None:
        in_specs.append(o_spec)
        args.append(res)
    n_main = len(args)
    n_out = 1 if twin is None else 2

    def body(*refs):
        a_ref, b_ref = refs[0], refs[1]
        r_ref = refs[2] if res is not None else None
        o_ref = refs[n_main]
        p = lax.dot_general(a_ref[...].astype(BF), b_ref[...].astype(BF), _DN[dims], preferred_element_type=F32)

        def finish(t):
            if r_ref is not None:
                t = t + r_ref[...]
            o_ref[...] = t.astype(out_dtype)
            if twin is not None:
                refs[n_main + 1][...] = t.astype(twin)

        if nk == 1:
            finish(p)
        else:
            acc = refs[n_main + n_out]
            k = pl.program_id(2)

            @pl.when(k == 0)
            def _():
                acc[...] = p

            @pl.when(k > 0)
            def _():
                acc[...] += p

            @pl.when(k == nk - 1)
            def _():
                finish(acc[...])

    got = _call(body, args, grid=(M // tm, N // tn, nk), in_specs=in_specs, out_specs=[o_spec] * n_out,
                out_shape=[SDS((M, N), out_dtype)] + ([SDS((M, N), twin)] if twin is not None else []),
                scratch_shapes=[pltpu.VMEM((tm, tn), F32)] if nk > 1 else [], semantics=("parallel", "parallel", "arbitrary"),
                name=name, side=side)
    outs, landed = (got, None) if side is None else got
    out = outs[0] if twin is None else (outs[0], outs[1])
    return out if side is None else (out, landed)


def _rms(x, g):
    return x * lax.rsqrt(jnp.mean(x * x, axis=-1, keepdims=True) + EPS) * g


def _ln_silu(x, g, b):
    mu = jnp.mean(x, axis=-1, keepdims=True)
    xc = x - mu
    var = jnp.mean(xc * xc, axis=-1, keepdims=True)
    return jax.nn.silu(xc * lax.rsqrt(var + EPS) * g + b)


def _merge(gc, gp, yc, yp, ps):
    return jax.nn.sigmoid(gc) * yc + jax.nn.sigmoid(gp) * (yp * ps)


def _gated(gate, val):
    return jax.nn.gelu(gate) * val


def _rms_fwd(x, g, name):
    T, D = x.shape
    tb = _tile(T, ROW_BLOCK, 8)

    def body(x_ref, g_ref, o_ref):
        o_ref[...] = _rms(x_ref[...], g_ref[...]).astype(BF)

    row = pl.BlockSpec((tb, D), lambda i: (i, 0))
    return pl.pallas_call(body, grid=(T // tb,), in_specs=[row, pl.BlockSpec((1, D), lambda i: (0, 0))], out_specs=row,
                          out_shape=SDS((T, D), BF), compiler_params=_params("parallel"), name=name)(x, g.reshape(1, D))


def _rms_bwd(x, g, dh, dres, name):
    T, D = x.shape
    tb = _tile(T, ROW_BLOCK, 8)

    def body(*refs):
        if dres is not None:
            x_ref, g_ref, dh_ref, dres_ref, dx_ref, dxb_ref, dg_ref = refs
        else:
            x_ref, g_ref, dh_ref, dx_ref, dxb_ref, dg_ref = refs
        _, vjp = jax.vjp(_rms, x_ref[...], g_ref[...])
        dx, dg = vjp(dh_ref[...].astype(F32))
        if dres is not None:
            dx = dx + dres_ref[...]
        dx_ref[...] = dx
        dxb_ref[...] = dx.astype(BF)

        @pl.when(pl.program_id(0) == 0)
        def _():
            dg_ref[...] = jnp.zeros_like(dg_ref)

        dg_ref[...] += dg

    row = pl.BlockSpec((tb, D), lambda i: (i, 0))
    vec = pl.BlockSpec((1, D), lambda i: (0, 0))
    ins = [x, g.reshape(1, D), dh] + ([dres] if dres is not None else [])
    return pl.pallas_call(
        body, grid=(T // tb,), in_specs=[row, vec, row] + ([row] if dres is not None else []), out_specs=[row, row, vec],
        out_shape=[SDS((T, D), F32), SDS((T, D), BF), SDS((1, D), F32)], compiler_params=_params("arbitrary"), name=name)(*ins)


def _row_tile(M, K, N, per_row_bytes):
    fixed = K * N * 2
    fit = [t for t in _divisors(M) if fixed + 2 * t * per_row_bytes <= MM_VMEM_BUDGET and t * K * N <= MM_STEP_MACS]
    return max(fit) if fit else min(_divisors(M))


def _mm_rms_fwd(a, b, res, g, name, side=None):
    M, K = a.shape
    N = b.shape[2]
    tm = _row_tile(M, K, N, K * 2 + N * (4 + 4 + 2))

    def body(a_ref, b_ref, r_ref, g_ref, x_ref, h_ref):
        x = r_ref[...] + lax.dot_general(a_ref[...], b_ref[...], _DN["nn"], preferred_element_type=F32)
        x_ref[...] = x
        h_ref[...] = _rms(x, g_ref[...]).astype(BF)

    row = pl.BlockSpec((tm, N), lambda i: (i, 0))
    return _call(body, (a, b, res, g.reshape(1, N)), grid=(M // tm,),
                 in_specs=[pl.BlockSpec((tm, K), lambda i: (i, 0)), pl.BlockSpec((None, K, N), lambda i: (0, 0, 0), pipeline_mode=pl.Buffered(1)), row,
                           pl.BlockSpec((1, N), lambda i: (0, 0))],
                 out_specs=[row, row], out_shape=[SDS((M, N), F32), SDS((M, N), BF)], semantics=("parallel",), name=name, side=side)


def _mm_rms_bwd(a, b, x, g, dres, name):
    M, K = a.shape
    N = b.shape[1]
    tm = _row_tile(M, K, N, K * 2 + N * (4 + 4 + 4 + 2))

    def body(a_ref, b_ref, x_ref, g_ref, r_ref, dx_ref, dxb_ref, dg_ref):
        dh = lax.dot_general(a_ref[...], b_ref[...], _DN["nt"], preferred_element_type=F32)
        _, vjp = jax.vjp(_rms, x_ref[...], g_ref[...])
        dx, dg = vjp(dh)
        dx = dx + r_ref[...]
        dx_ref[...] = dx
        dxb_ref[...] = dx.astype(BF)

        @pl.when(pl.program_id(0) == 0)
        def _():
            dg_ref[...] = jnp.zeros_like(dg_ref)

        dg_ref[...] += dg

    row = pl.BlockSpec((tm, N), lambda i: (i, 0))
    vec = pl.BlockSpec((1, N), lambda i: (0, 0))
    return pl.pallas_call(
        body, grid=(M // tm,),
        in_specs=[pl.BlockSpec((tm, K), lambda i: (i, 0)), pl.BlockSpec((None, N, K), lambda i: (0, 0, 0), pipeline_mode=pl.Buffered(1)), row, vec, row],
        out_specs=[row, row, vec], out_shape=[SDS((M, N), F32), SDS((M, N), BF), SDS((1, N), F32)],
        compiler_params=_params("arbitrary"), name=name)(a, b, x, g.reshape(1, N), dres)


def _loss_bwd(x, g, target, name):
    T, D = x.shape
    tb = _tile(T, ROW_BLOCK, 8)
    nb = T // tb

    def body(x_ref, g_ref, t_ref, loss_ref, dx_ref, dg_ref, acc):
        i = pl.program_id(0)
        y, vjp = jax.vjp(_rms, x_ref[...], g_ref[...])
        err = y - t_ref[...]
        dx, dg = vjp(err * (1.0 / D))
        dx_ref[...] = dx

        @pl.when(i == 0)
        def _():
            dg_ref[...] = jnp.zeros_like(dg_ref)
            acc[...] = jnp.zeros_like(acc)

        dg_ref[...] += dg
        acc[...] += jnp.sum(err * err, axis=0, keepdims=True)

        @pl.when(i == nb - 1)
        def _():
            loss_ref[...] = jnp.full(loss_ref.shape, (0.5 / D) * jnp.sum(acc[...]), F32)

    row = pl.BlockSpec((tb, D), lambda i: (i, 0))
    vec = pl.BlockSpec((1, D), lambda i: (0, 0))
    return pl.pallas_call(
        body, grid=(nb,), in_specs=[row, vec, row], out_specs=[pl.BlockSpec((1, LANES), lambda i: (0, 0)), row, vec],
        out_shape=[SDS((1, LANES), F32), SDS((T, D), F32), SDS((1, D), F32)], scratch_shapes=[pltpu.VMEM((1, D), F32)],
        compiler_params=_params("arbitrary"), name=name)(x, g.reshape(1, D), target)


def _ln_silu_fwd(cv, g, b, name):
    T, C = cv.shape
    tb = _tile(T, ROW_BLOCK, 8)

    def body(x_ref, g_ref, b_ref, o_ref):
        o_ref[...] = _ln_silu(x_ref[...], g_ref[...], b_ref[...]).astype(BF)

    row = pl.BlockSpec((tb, C), lambda i: (i, 0))
    vec = pl.BlockSpec((1, C), lambda i: (0, 0))
    return pl.pallas_call(body, grid=(T // tb,), in_specs=[row, vec, vec], out_specs=row, out_shape=SDS((T, C), BF),
                          compiler_params=_params("parallel"), name=name)(cv, g.reshape(1, C), b.reshape(1, C))


def _ln_silu_bwd(cv, g, b, dy, name):
    T, C = cv.shape
    tb = _tile(T, ROW_BLOCK, 8)

    def body(x_ref, g_ref, b_ref, dy_ref, dx_ref, dg_ref, db_ref):
        _, vjp = jax.vjp(_ln_silu, x_ref[...], g_ref[...], b_ref[...])
        dx, dg, db = vjp(dy_ref[...].astype(F32))
        dx_ref[...] = dx

        @pl.when(pl.program_id(0) == 0)
        def _():
            dg_ref[...] = jnp.zeros_like(dg_ref)
            db_ref[...] = jnp.zeros_like(db_ref)

        dg_ref[...] += dg
        db_ref[...] += db

    row = pl.BlockSpec((tb, C), lambda i: (i, 0))
    vec = pl.BlockSpec((1, C), lambda i: (0, 0))
    return pl.pallas_call(
        body, grid=(T // tb,), in_specs=[row, vec, vec, row], out_specs=[row, vec, vec],
        out_shape=[SDS((T, C), F32), SDS((1, C), F32), SDS((1, C), F32)], compiler_params=_params("arbitrary"),
        name=name)(cv, g.reshape(1, C), b.reshape(1, C), dy)


def _merge_fwd(proj, yc, yp, ps, C, name):
    T, D = yc.shape
    tb = _tile(T, ROW_BLOCK, 8)
    nj = D // C

    def body(gc_ref, gp_ref, yc_ref, yp_ref, ps_ref, o_ref):
        o_ref[...] = _merge(gc_ref[...], gp_ref[...], yc_ref[...], yp_ref[...], ps_ref[...]).astype(BF)

    blk = pl.BlockSpec((tb, C), lambda i, j: (i, j))
    return pl.pallas_call(
        body, grid=(T // tb, nj),
        in_specs=[pl.BlockSpec((tb, C), lambda i, j: (i, 3 + j)), pl.BlockSpec((tb, C), lambda i, j: (i, 3 + nj + j)), blk, blk,
                  pl.BlockSpec((1, C), lambda i, j: (0, j))],
        out_specs=blk, out_shape=SDS((T, D), BF), compiler_params=_params("parallel", "parallel"), name=name)(proj, proj, yc, yp, ps.reshape(1, D))


def _merge_bwd(proj, yc, yp, ps, dm, C, name, side=None):
    T, D = yc.shape
    tb = _tile(T, ROW_BLOCK, 8)
    nj = D // C

    def body(gc_ref, gp_ref, yc_ref, yp_ref, ps_ref, dm_ref, dgc_ref, dgp_ref, dyc_ref, dyp_ref, dps_ref):
        _, vjp = jax.vjp(_merge, gc_ref[...], gp_ref[...], yc_ref[...], yp_ref[...], ps_ref[...])
        dgc, dgp, dyc, dyp, dps = vjp(dm_ref[...].astype(F32))
        dgc_ref[...] = dgc.astype(BF)
        dgp_ref[...] = dgp.astype(BF)
        dyc_ref[...] = dyc.astype(BF)
        dyp_ref[...] = dyp.astype(BF)

        @pl.when(pl.program_id(1) == 0)
        def _():
            dps_ref[...] = jnp.zeros_like(dps_ref)

        dps_ref[...] += dps

    blk = pl.BlockSpec((tb, C), lambda j, i: (i, j))
    vec = pl.BlockSpec((1, C), lambda j, i: (0, j))
    return _call(
        body, (proj, proj, yc, yp, ps.reshape(1, D), dm), grid=(nj, T // tb),
        in_specs=[pl.BlockSpec((tb, C), lambda j, i: (i, 3 + j)), pl.BlockSpec((tb, C), lambda j, i: (i, 3 + nj + j)), blk, blk, vec, blk],
        out_specs=[blk, blk, blk, blk, vec], out_shape=[SDS((T, D), BF)] * 4 + [SDS((1, D), F32)],
        semantics=("parallel", "arbitrary"), name=name, side=side)


def _shd(v, s, rows):
    if s == 0:
        return v
    return jnp.where(rows >= s, pltpu.roll(v, s, 0), 0.0)


def _shu(v, s, rows):
    if s == 0:
        return v
    n = v.shape[0]
    return jnp.where(rows < n - s, pltpu.roll(v, n - s, 0), 0.0)


def _glu_conv_fwd(proj, w, b, Bn, S, C, name):
    K = w.shape[0]
    sl = min(LANES, C)
    ns = C // sl

    def body(a_ref, gl_ref, w_ref, b_ref, o_ref):
        y0 = a_ref[...] * jax.nn.sigmoid(gl_ref[...])
        rows = lax.broadcasted_iota(jnp.int32, y0.shape, 0)
        acc = jnp.zeros_like(y0) + b_ref[...]
        for k in range(K):
            acc = acc + w_ref[k:k + 1, :] * _shd(y0, K - 1 - k, rows)
        o_ref[...] = acc

    return pl.pallas_call(
        body, grid=(Bn, ns),
        in_specs=[pl.BlockSpec((S, sl), lambda bi, j: (bi, j)), pl.BlockSpec((S, sl), lambda bi, j: (bi, ns + j)),
                  pl.BlockSpec((K, sl), lambda bi, j: (0, j)), pl.BlockSpec((1, sl), lambda bi, j: (0, j))],
        out_specs=pl.BlockSpec((S, sl), lambda bi, j: (bi, j)), out_shape=SDS((Bn * S, C), F32),
        compiler_params=_params("parallel", "parallel"), name=name)(proj, proj, w, b.reshape(1, C))


def _glu_conv_bwd(proj, w, dcv, Bn, S, C, name, side=None):
    K = w.shape[0]
    sl = min(LANES, C)
    ns = C // sl

    def body(a_ref, gl_ref, w_ref, d_ref, da_ref, dgl_ref, dw_ref, db_ref):
        a = a_ref[...]
        sg = jax.nn.sigmoid(gl_ref[...])
        y0 = a * sg
        d = d_ref[...]
        rows = lax.broadcasted_iota(jnp.int32, y0.shape, 0)

        @pl.when(pl.program_id(1) == 0)
        def _():
            dw_ref[...] = jnp.zeros_like(dw_ref)
            db_ref[...] = jnp.zeros_like(db_ref)

        dy0 = jnp.zeros_like(y0)
        for k in range(K):
            s = K - 1 - k
            dw_ref[k:k + 1, :] += jnp.sum(d * _shd(y0, s, rows), axis=0, keepdims=True)
            dy0 = dy0 + w_ref[k:k + 1, :] * _shu(d, s, rows)
        db_ref[...] += jnp.sum(d, axis=0, keepdims=True)
        da_ref[...] = (dy0 * sg).astype(BF)
        dgl_ref[...] = (dy0 * a * sg * (1.0 - sg)).astype(BF)

    blk = pl.BlockSpec((S, sl), lambda j, bi: (bi, j))
    return _call(
        body, (proj, proj, w, dcv), grid=(ns, Bn),
        in_specs=[blk, pl.BlockSpec((S, sl), lambda j, bi: (bi, ns + j)), pl.BlockSpec((K, sl), lambda j, bi: (0, j)), blk],
        out_specs=[blk, blk, pl.BlockSpec((K, sl), lambda j, bi: (0, j)), pl.BlockSpec((1, sl), lambda j, bi: (0, j))],
        out_shape=[SDS((Bn * S, C), BF), SDS((Bn * S, C), BF), SDS((K, C), F32), SDS((1, C), F32)],
        semantics=("parallel", "arbitrary"), name=name, side=side)


def _pool_z(u, g, rows):
    s2 = u + _shd(u, 1, rows)
    s4 = s2 + _shd(s2, 2, rows)
    s8 = s4 + _shd(s4, 4, rows)
    s16 = s8 + _shd(s8, 8, rows)
    sw = jnp.where(g == 0, s2, jnp.where(g == 1, s4, jnp.where(g == 2, s8, s16)))
    cnt = jnp.minimum(rows + 1, POOL_WINDOWS[0] << g).astype(F32)
    return sw / cnt - u, cnt


def _pool_fwd(proj, wpt, l, Bn, S, C, D, name):
    G = len(POOL_WINDOWS)
    gd, go = C // G, D // G

    def body(u_ref, w_ref, o_ref):
        g = pl.program_id(1)
        u = u_ref[...]
        rows = lax.broadcasted_iota(jnp.int32, u.shape, 0)
        zp, _ = _pool_z(u, g, rows)
        o_ref[...] = lax.dot_general(zp.astype(BF), w_ref[...], _DN["nt"], preferred_element_type=F32)

    return pl.pallas_call(
        body, grid=(Bn, G),
        in_specs=[pl.BlockSpec((S, gd), lambda bi, g: (bi, 2 * G + g)), pl.BlockSpec((None, go, gd), lambda bi, g: (l * G + g, 0, 0))],
        out_specs=pl.BlockSpec((S, go), lambda bi, g: (bi, g)), out_shape=SDS((Bn * S, D), F32),
        compiler_params=_params("parallel", "parallel"), name=name)(proj, wpt)


def _pool_bwd(proj, wpt, dyp, l, Bn, S, C, D, name):
    G = len(POOL_WINDOWS)
    gd, go = C // G, D // G

    def body(u_ref, w_ref, d_ref, du_ref, dw_ref):
        g = pl.program_id(0)
        u = u_ref[...]
        rows = lax.broadcasted_iota(jnp.int32, u.shape, 0)
        zp, cnt = _pool_z(u, g, rows)
        d = d_ref[...]
        dzp = lax.dot_general(d, w_ref[...], _DN["nn"], preferred_element_type=F32)

        @pl.when(pl.program_id(1) == 0)
        def _():
            dw_ref[...] = jnp.zeros_like(dw_ref)

        dw_ref[...] += lax.dot_general(d, zp.astype(BF), _DN["tn"], preferred_element_type=F32)
        dsw = dzp / cnt
        zero = jnp.zeros_like(dsw)
        d16 = jnp.where(g == 3, dsw, zero)
        d8 = jnp.where(g == 2, dsw, zero) + d16 + _shu(d16, 8, rows)
        d4 = jnp.where(g == 1, dsw, zero) + d8 + _shu(d8, 4, rows)
        d2 = jnp.where(g == 0, dsw, zero) + d4 + _shu(d4, 2, rows)
        d1 = d2 + _shu(d2, 1, rows)
        du_ref[...] = (d1 - dzp).astype(BF)

    return pl.pallas_call(
        body, grid=(G, Bn),
        in_specs=[pl.BlockSpec((S, gd), lambda g, bi: (bi, 2 * G + g)), pl.BlockSpec((None, go, gd), lambda g, bi: (l * G + g, 0, 0)),
                  pl.BlockSpec((S, go), lambda g, bi: (bi, g))],
        out_specs=[pl.BlockSpec((S, gd), lambda g, bi: (bi, g)), pl.BlockSpec((None, go, gd), lambda g, bi: (g, 0, 0))],
        out_shape=[SDS((Bn * S, C), BF), SDS((G, go, gd), F32)],
        compiler_params=_params("parallel", "arbitrary"), name=name)(proj, wpt, dyp)


def _ffn_conv(u, w_ref, rows):
    K = w_ref.shape[0]
    acc = w_ref[K - 1:K, :] * u
    for k in range(K - 1):
        acc = acc + w_ref[k:k + 1, :] * _shd(u, K - 1 - k, rows)
    return acc


def _ffn_cb(F):
    return _tile(F, 256)


def _ffn_act_fwd(up0, w, Bn, S, F, name):
    cb = _ffn_cb(F)
    nj = F // cb

    def body(g_ref, v_ref, wg_ref, wv_ref, o_ref):
        rows = lax.broadcasted_iota(jnp.int32, g_ref.shape, 0)
        o_ref[...] = _gated(_ffn_conv(g_ref[...], wg_ref, rows), _ffn_conv(v_ref[...], wv_ref, rows)).astype(BF)

    K = w.shape[0]
    return pl.pallas_call(
        body, grid=(Bn, nj),
        in_specs=[pl.BlockSpec((S, cb), lambda bi, j: (bi, j)), pl.BlockSpec((S, cb), lambda bi, j: (bi, nj + j)),
                  pl.BlockSpec((K, cb), lambda bi, j: (0, j)), pl.BlockSpec((K, cb), lambda bi, j: (0, nj + j))],
        out_specs=pl.BlockSpec((S, cb), lambda bi, j: (bi, j)), out_shape=SDS((Bn * S, F), BF),
        compiler_params=_params("parallel", "parallel"), name=name)(up0, up0, w, w)


SUBLANES = 8
FFN_HALO = SUBLANES
FFN_ROWS = 64
GELU_C0, GELU_C1 = 0.7978845608028654, 0.044715


def _gelu_and_grad(x):
    x2 = x * x
    t = jnp.tanh(GELU_C0 * (x + GELU_C1 * (x2 * x)))
    cdf = 0.5 * (1.0 + t)
    return x * cdf, cdf + (0.5 * GELU_C0) * x * (1.0 - t * t) * (1.0 + (3.0 * GELU_C1) * x2)


def _ffn_act_bwd(up0, w, dg, Bn, S, F, name, side=None):
    cb = min(LANES, F)
    nj = F // cb
    K = w.shape[0]
    rc = FFN_ROWS if S % FFN_ROWS == 0 else S
    win = rc + 2 * FFN_HALO
    assert K - 1 <= FFN_HALO and rc % SUBLANES == 0

    def body(g_ref, v_ref, wg_ref, wv_ref, d_ref, dgo_ref, dvo_ref, dwg_ref, dwv_ref, gp, vp, dp):
        for pad, src in ((gp, g_ref), (vp, v_ref), (dp, d_ref)):
            pad[0:FFN_HALO, :] = jnp.zeros((FFN_HALO, cb), F32)
            pad[FFN_HALO + S:, :] = jnp.zeros((FFN_HALO, cb), F32)
            pad[FFN_HALO:FFN_HALO + S, :] = src[...].astype(F32)
        wg = [wg_ref[k:k + 1, :] for k in range(K)]
        wv = [wv_ref[k:k + 1, :] for k in range(K)]

        def conv(u, ws):
            acc = ws[K - 1] * u
            for k in range(K - 1):
                acc = acc + ws[k] * pltpu.roll(u, K - 1 - k, 0)
            return acc

        def conv_t(dc, ws):
            acc = ws[K - 1] * dc
            for k in range(K - 1):
                acc = acc + ws[k] * pltpu.roll(dc, win - (K - 1 - k), 0)
            return acc

        def fold(t):
            acc = t[FFN_HALO:FFN_HALO + SUBLANES]
            for i in range(1, rc // SUBLANES):
                acc = acc + t[FFN_HALO + SUBLANES * i:FFN_HALO + SUBLANES * (i + 1)]
            return acc

        def chunk(c, sums):
            r0 = pl.multiple_of(c * rc, SUBLANES)
            g, v, d = gp[pl.ds(r0, win), :], vp[pl.ds(r0, win), :], dp[pl.ds(r0, win), :]
            ge, dge = _gelu_and_grad(conv(g, wg))
            dgc = d * conv(v, wv) * dge
            dvc = d * ge
            dgo_ref[pl.ds(r0, rc), :] = conv_t(dgc, wg)[FFN_HALO:FFN_HALO + rc].astype(BF)
            dvo_ref[pl.ds(r0, rc), :] = conv_t(dvc, wv)[FFN_HALO:FFN_HALO + rc].astype(BF)
            new = []
            for u, dc in ((g, dgc), (v, dvc)):
                for k in range(K):
                    new.append(fold(dc * (u if k == K - 1 else pltpu.roll(u, K - 1 - k, 0))))
            return tuple(a + b for a, b in zip(sums, new))

        sums = lax.fori_loop(0, S // rc, chunk, tuple(jnp.zeros((SUBLANES, cb), F32) for _ in range(2 * K)))

        @pl.when(pl.program_id(1) == 0)
        def _():
            dwg_ref[...] = jnp.zeros_like(dwg_ref)
            dwv_ref[...] = jnp.zeros_like(dwv_ref)

        for k in range(K):
            dwg_ref[k:k + 1, :] += jnp.sum(sums[k], axis=0, keepdims=True)
            dwv_ref[k:k + 1, :] += jnp.sum(sums[K + k], axis=0, keepdims=True)

    blk = pl.BlockSpec((S, cb), lambda j, bi: (bi, j))
    wblk = pl.BlockSpec((K, cb), lambda j, bi: (0, j))
    return _call(
        body, (up0, up0, w, w, dg), grid=(nj, Bn),
        in_specs=[blk, pl.BlockSpec((S, cb), lambda j, bi: (bi, nj + j)), wblk, pl.BlockSpec((K, cb), lambda j, bi: (0, nj + j)), blk],
        out_specs=[blk, blk, wblk, wblk],
        out_shape=[SDS((Bn * S, F), BF), SDS((Bn * S, F), BF), SDS((K, F), F32), SDS((K, F), F32)],
        scratch_shapes=[pltpu.VMEM((S + 2 * FFN_HALO, cb), F32)] * 3, semantics=("parallel", "arbitrary"), name=name, side=side)


def _softmax_rows(q, k, scale):
    sc = lax.dot_general(q, k, _DN["nt"], preferred_element_type=F32) * scale
    e = jnp.exp(sc - jnp.max(sc, axis=-1, keepdims=True))
    return e / jnp.sum(e, axis=-1, keepdims=True)


def _attn_ts(S):
    return _tile(S, 1024, 8)


def _attn_fwd(q, kv, Bn, S, Mn, D, name):
    H = XA_HEADS
    dh = D // H
    ts = _attn_ts(S)
    nsb = S // ts
    scale = dh ** -0.5

    def body(q_ref, k_ref, v_ref, o_ref):
        p = _softmax_rows(q_ref[...], k_ref[...], scale)
        o_ref[...] = lax.dot_general(p.astype(BF), v_ref[...], _DN["nn"], preferred_element_type=F32).astype(BF)

    qblk = pl.BlockSpec((ts, dh), lambda bi, h, s: (bi * nsb + s, h))
    return pl.pallas_call(
        body, grid=(Bn, H, nsb),
        in_specs=[qblk, pl.BlockSpec((Mn, dh), lambda bi, h, s: (bi, h)), pl.BlockSpec((Mn, dh), lambda bi, h, s: (bi, H + h))],
        out_specs=qblk, out_shape=SDS((Bn * S, D), BF), compiler_params=_params("parallel", "parallel", "parallel"), name=name)(q, kv, kv)


def _attn_bwd(q, kv, datt, Bn, S, Mn, D, name):
    H = XA_HEADS
    dh = D // H
    ts = _attn_ts(S)
    nsb = S // ts
    scale = dh ** -0.5

    def body(q_ref, k_ref, v_ref, do_ref, dq_ref, dk_ref, dv_ref):
        q, k, v, do = q_ref[...], k_ref[...], v_ref[...], do_ref[...]
        p = _softmax_rows(q, k, scale)
        dp = lax.dot_general(do, v, _DN["nt"], preferred_element_type=F32)
        ds = (p * (dp - jnp.sum(dp * p, axis=-1, keepdims=True)) * scale).astype(BF)
        dq_ref[...] = lax.dot_general(ds, k, _DN["nn"], preferred_element_type=F32).astype(BF)

        @pl.when(pl.program_id(2) == 0)
        def _():
            dk_ref[...] = jnp.zeros_like(dk_ref)
            dv_ref[...] = jnp.zeros_like(dv_ref)

        dk_ref[...] += lax.dot_general(ds, q, _DN["tn"], preferred_element_type=F32)
        dv_ref[...] += lax.dot_general(p.astype(BF), do, _DN["tn"], preferred_element_type=F32)

    qblk = pl.BlockSpec((ts, dh), lambda bi, h, s: (bi * nsb + s, h))
    kblk = pl.BlockSpec((Mn, dh), lambda bi, h, s: (bi, h))
    return pl.pallas_call(
        body, grid=(Bn, H, nsb),
        in_specs=[qblk, kblk, pl.BlockSpec((Mn, dh), lambda bi, h, s: (bi, H + h)), qblk],
        out_specs=[qblk, kblk, kblk], out_shape=[SDS((Bn * S, D), BF), SDS((Bn * Mn, D), F32), SDS((Bn * Mn, D), F32)],
        compiler_params=_params("parallel", "parallel", "arbitrary"), name=name)(q, kv, kv, datt)


class _Sides:
    def __init__(self, by_key=None):
        self.by_key, self.landed = dict(by_key or {}), {}

    def run(self, key, fn, *args, **kw):
        side = self.by_key.get(key)
        if side is None:
            return fn(*args, **kw)
        out, self.landed[key] = fn(*args, side=side, **kw)
        return out

    def mm(self, key, *args, **kw):
        return self.run(key, _mm, *args, **kw)


def _layer_fwd(x, h, mem_n, W, V, l, dims, sides, next_g):
    Bn, S, Mn, D, C, F = dims
    n = f"l{l}_"
    proj = sides.mm("proj", h, W["w_in"], "nn", F32, n + "proj", bl=0)
    cv = _glu_conv_fwd(proj, V["conv_dw_w"][l], V["conv_dw_b"][l], Bn, S, C, n + "glu_conv")
    yc1 = _ln_silu_fwd(cv, V["conv_ln_g"][l], V["conv_ln_b"][l], n + "ln_silu")
    yc = sides.mm("conv_out", yc1, W["w_conv_out"], "nn", F32, n + "conv_out", bl=0)
    yp = _pool_fwd(proj, W["w_pool"], 0, Bn, S, C, D, n + "pool")
    merged = _merge_fwd(proj, yc, yp, V["pool_scale"][l], C, n + "merge")
    x1, hq = sides.run("out_proj", _mm_rms_fwd, merged, W["w_out"], x, V["xattn_norm_g"][l], n + "out_proj")
    q = sides.mm("q_proj", hq, W["w_q"], "nn", BF, n + "q_proj", bl=0)
    kv = _mm(mem_n, W["w_kv"], "nn", BF, n + "kv_proj", bl=0)
    att = _attn_fwd(q, kv, Bn, S, Mn, D, n + "attn")
    x2, hf = sides.run("o_proj", _mm_rms_fwd, att, W["w_o"], x1, V["ffn_norm_g"][l], n + "o_proj")
    up0 = sides.mm("up_proj", hf, W["w_up"], "nn", F32, n + "up_proj", bl=0)
    gact = _ffn_act_fwd(up0, V["ffn_dw_w"][l], Bn, S, F, n + "ffn_act")
    if next_g is not None:
        x3, h3 = sides.run("down_proj", _mm_rms_fwd, gact, W["w_down"], x2, next_g, n + "down_proj")
    else:
        x3, h3 = sides.mm("down_proj", gact, W["w_down"], "nn", F32, n + "down_proj", res=x2, bl=0), None
    return x3, h3, dict(x=x, h=h, proj=proj, cv=cv, yc1=yc1, yc=yc, yp=yp, merged=merged, x1=x1, hq=hq, q=q, kv=kv, att=att, x2=x2,
                        hf=hf, up0=up0, gact=gact)


def _layer_bwd_mlp(dx, dxb, sv, W, V, l, dims, sides):
    Bn, S, Mn, D, C, F = dims
    n = f"l{l}_b_"
    gw, sm = {}, {}
    dgact = sides.mm("d_gact", dxb, W["w_down"], "nt", BF, n + "d_gact", bl=0)
    gw["w_down"] = sides.mm("dw_down", sv["gact"], dxb, "tn", F32, n + "dw_down", twin=BF)
    dg0, dv0, dwg, dwv = sides.run("ffn_act_b", _ffn_act_bwd, sv["up0"], V["ffn_dw_w"][l], dgact, Bn, S, F, n + "ffn_act")
    sm["ffn_dw_w"] = jnp.concatenate([dwg, dwv], axis=1)
    dup0 = jnp.concatenate([dg0, dv0], axis=1)
    dx2, dx2b, sm["ffn_norm_g"] = _mm_rms_bwd(dup0, W["w_up"], sv["x2"], V["ffn_norm_g"][l], dx, n + "d_hf")
    gw["w_up"] = sides.mm("dw_up", sv["hf"], dup0, "tn", F32, n + "dw_up", twin=BF)
    return dx2, dx2b, gw, sm


def _layer_bwd_mix(dx2, dx2b, dmem_n, sv, mem_n, W, V, l, dims, sides):
    Bn, S, Mn, D, C, F = dims
    n = f"l{l}_b_"
    gw, sm = {}, {}
    datt = _mm(dx2b, W["w_o"], "nt", BF, n + "d_att", bl=0)
    gw["w_o"] = _mm(sv["att"], dx2b, "tn", F32, n + "dw_o", twin=BF)
    dq, dk, dv = _attn_bwd(sv["q"], sv["kv"], datt, Bn, S, Mn, D, n + "attn")
    dkv = jnp.concatenate([dk, dv], axis=1)
    gw["w_kv"] = _mm(mem_n, dkv, "tn", F32, n + "dw_kv", twin=BF)
    dmem_n = _mm(dkv, W["w_kv"], "nt", F32, n + "d_mem", res=dmem_n, bl=0)
    dx1, dx1b, sm["xattn_norm_g"] = _mm_rms_bwd(dq, W["w_q"], sv["x1"], V["xattn_norm_g"][l], dx2, n + "d_hq")
    gw["w_q"] = _mm(sv["hq"], dq, "tn", F32, n + "dw_q", twin=BF)
    dmerged = _mm(dx1b, W["w_out"], "nt", BF, n + "d_merged", bl=0)
    gw["w_out"] = _mm(sv["merged"], dx1b, "tn", F32, n + "dw_out", twin=BF)
    dgc, dgp, dyc, dyp, sm["pool_scale"] = sides.run("merge_b", _merge_bwd, sv["proj"], sv["yc"], sv["yp"], V["pool_scale"][l], dmerged, C, n + "merge")
    du, dwp = _pool_bwd(sv["proj"], W["w_pool"], dyp, 0, Bn, S, C, D, n + "pool")
    gw["w_pool"] = (dwp, dwp.astype(BF))
    dyc1 = _mm(dyc, W["w_conv_out"], "nt", F32, n + "d_yc1", bl=0)
    gw["w_conv_out"] = _mm(sv["yc1"], dyc, "tn", F32, n + "dw_conv_out", twin=BF)
    dcv, sm["conv_ln_g"], sm["conv_ln_b"] = _ln_silu_bwd(sv["cv"], V["conv_ln_g"][l], V["conv_ln_b"][l], dyc1, n + "ln_silu")
    da, dgl, sm["conv_dw_w"], sm["conv_dw_b"] = sides.run("glu_conv_b", _glu_conv_bwd, sv["proj"], V["conv_dw_w"][l], dcv, Bn, S, C, n + "glu_conv")
    dproj = jnp.concatenate([da, dgl, du, dgc, dgp], axis=1)
    dx, dxb, sm["mix_norm_g"] = _mm_rms_bwd(dproj, W["w_in"], sv["x"], V["mix_norm_g"][l], dx1, n + "d_h")
    gw["w_in"] = _mm(sv["h"], dproj, "tn", F32, n + "dw_in", twin=BF)
    return dx, dxb, dmem_n, gw, sm


BIG = (("w_in", "col"), ("w_conv_out", "col"), ("w_pool", "row"), ("w_out", "row"), ("w_q", "row"), ("w_kv", "col"),
       ("w_o", "row"), ("w_up", "col"), ("w_down", "row"))
FWD_CARRY = {"proj": ("w_in",), "conv_out": ("w_conv_out", "w_pool"), "out_proj": ("w_out", "w_q"), "q_proj": ("w_o",), "o_proj": ("w_kv",),
             "up_proj": ("w_up",), "down_proj": ("w_down",)}
EARLY = ("w_down", "w_up")
BWD_CARRY_EARLY = {"merge_b": ("w_down",), "glu_conv_b": ("w_up",)}
BWD_CARRY_LATE = {"ffn_act_b": ("w_in", "w_conv_out", "w_pool", "w_out", "w_q", "w_kv", "w_o")}


def _place():
    xi, yi, ci = lax.axis_index("x"), lax.axis_index("y"), lax.axis_index("c")
    return xi, yi, ci, 2 * xi + yi


def _chip_peer(xi, yi, ci, r):
    return (xi ^ (r >> 1), yi ^ (r & 1), ci)


def _full_shard(ref, kind, k, cs):
    if kind == "col":
        return ref.at[:, :, :, :, pl.ds(pl.multiple_of(k * cs, cs), cs)]
    return ref.at[:, :, k]


def _gather_weights(shards, kinds):
    n = len(shards)
    outs = []
    for s, kind in zip(shards, kinds):
        L, P, _, RH, CS = s.shape
        outs.append(SDS((L, P, 2, RH, CS * N_CHIPS) if kind == "col" else (L, P, N_CHIPS, 2, RH, CS), s.dtype))
    per = 7

    def body(*refs):
        srcs, fulls, (ssem, rsem) = refs[:n], refs[n:2 * n], refs[2 * n:]
        xi, yi, ci, j = _place()
        sib = (xi, yi, 1 - ci)

        def piece(i, k, c):
            kind, cs = kinds[i], shards[i].shape[-1]
            if kind == "col":
                return fulls[i].at[:, :, c, :, pl.ds(pl.multiple_of(k * cs, cs), cs)]
            return fulls[i].at[:, :, k, c]

        def copy(i, slot, src, dst, dev):
            return pltpu.make_async_remote_copy(src_ref=src, dst_ref=dst, send_sem=ssem.at[per * i + slot], recv_sem=rsem.at[per * i + slot],
                                                device_id=dev, device_id_type=MESH)

        own, first, passed = [], [], []
        for i in range(n):
            for r in (1, 2, 3):
                first.append(copy(i, r - 1, srcs[i].at[:, :, ci], piece(i, j, ci), _chip_peer(xi, yi, ci, r)))
                first[-1].start()
        for i in range(n):
            own.append(copy(i, 6, srcs[i], _full_shard(fulls[i], kinds[i], j, shards[i].shape[-1]), sib))
            own[-1].start()
        for i in range(n):
            for r in (1, 2, 3):
                got = piece(i, j ^ r, ci)
                copy(i, r - 1, got, got, sib).wait_recv()
                passed.append(copy(i, 2 + r, got, got, sib))
                passed[-1].start()
        for i in range(n):
            for r in (1, 2, 3):
                got = piece(i, j ^ r, 1 - ci)
                copy(i, 2 + r, got, got, sib).wait_recv()
        for cp in own:
            cp.wait()
        for cp in first + passed:
            cp.wait_send()

    return pl.pallas_call(
        body, in_specs=[ANY] * n, out_specs=[ANY] * n, out_shape=outs,
        scratch_shapes=[pltpu.SemaphoreType.DMA((per * n,)), pltpu.SemaphoreType.DMA((per * n,))], name="gather_weights")(*shards)


def _full_sds(s, kind):
    L, P, _, RH, CS = s.shape
    return SDS((L, P, 2, RH, CS * N_CHIPS) if kind == "col" else (L, P, N_CHIPS, 2, RH, CS), s.dtype)


def _gather_piece(full, kind, cs, k, c):
    if kind == "col":
        return full.at[:, :, c, :, pl.ds(pl.multiple_of(k * cs, cs), cs)]
    return full.at[:, :, k, c]


def _side_gather(shards, kinds):
    n = len(shards)

    def make(srcs, fulls, ssem, rsem):
        xi, yi, ci, j = _place()
        return [pltpu.make_async_remote_copy(
            src_ref=srcs[i].at[:, :, ci], dst_ref=_gather_piece(fulls[i], kinds[i], shards[i].shape[-1], j, ci), send_sem=ssem.at[3 * i + r - 1],
            recv_sem=rsem.at[3 * i + r - 1], device_id=_chip_peer(xi, yi, ci, r), device_id_type=MESH) for i in range(n) for r in (1, 2, 3)]

    return _Side(shards, [_full_sds(s, k) for s, k in zip(shards, kinds)], 3 * n, make)


def _gather_pass(fulls, shards, kinds, name):
    n = len(fulls)

    def body(*refs):
        srcs, outs, (ssem, rsem) = refs[n:2 * n], refs[2 * n:3 * n], refs[3 * n:]
        xi, yi, ci, j = _place()
        sib = (xi, yi, 1 - ci)
        cps = []
        for i in range(n):
            cs = shards[i].shape[-1]
            for r in (1, 2, 3):
                got = _gather_piece(outs[i], kinds[i], cs, j ^ r, ci)
                cps.append(pltpu.make_async_remote_copy(src_ref=got, dst_ref=got, send_sem=ssem.at[4 * i + r - 1], recv_sem=rsem.at[4 * i + r - 1],
                                                        device_id=sib, device_id_type=MESH))
            cps.append(pltpu.make_async_remote_copy(src_ref=srcs[i], dst_ref=_full_shard(outs[i], kinds[i], j, cs), send_sem=ssem.at[4 * i + 3],
                                                    recv_sem=rsem.at[4 * i + 3], device_id=sib, device_id_type=MESH))
        for cp in cps:
            cp.start()
        for cp in cps:
            cp.wait()

    return pl.pallas_call(
        body, in_specs=[ANY] * (2 * n), out_specs=[ANY] * n, out_shape=[SDS(f.shape, f.dtype) for f in fulls],
        input_output_aliases={i: i for i in range(n)},
        scratch_shapes=[pltpu.SemaphoreType.DMA((4 * n,)), pltpu.SemaphoreType.DMA((4 * n,))], name=name)(*fulls, *shards)


def _sibling_exchange(gviews, kinds, name):
    n = len(gviews)
    outs = [SDS(g.shape[:1] + g.shape[2:] if kind == "col" else g.shape[:2] + g.shape[3:], g.dtype) for g, kind in zip(gviews, kinds)]

    def body(*refs):
        gs, lands, (ssem, rsem) = refs[:n], refs[n:2 * n], refs[2 * n:]
        xi, yi, ci, _ = _place()
        cps = []
        for i in range(n):
            src = gs[i].at[:, 1 - ci] if kinds[i] == "col" else gs[i].at[:, :, 1 - ci]
            cps.append(pltpu.make_async_remote_copy(src_ref=src, dst_ref=lands[i], send_sem=ssem.at[i], recv_sem=rsem.at[i],
                                                    device_id=(xi, yi, 1 - ci), device_id_type=MESH))
            cps[-1].start()
        for cp in cps:
            cp.wait()

    return pl.pallas_call(body, in_specs=[ANY] * n, out_specs=[ANY] * n, out_shape=outs,
                          scratch_shapes=[pltpu.SemaphoreType.DMA((n,)), pltpu.SemaphoreType.DMA((n,))], name=name)(*gviews)


def _chip_sum(g, land, kind, jc, name):
    if kind == "col":
        P, _, RH, C = g.shape
        CS = C // N_CHIPS
        g_spec = pl.BlockSpec((None, None, RH, CS), lambda p, r, jc: (p, jc[1], 0, jc[0] ^ r))
        l_spec = pl.BlockSpec((None, RH, CS), lambda p, r, jc: (p, 0, jc[0] ^ r))
    else:
        P, _, _, RH, CS = g.shape
        g_spec = pl.BlockSpec((None, None, None, RH, CS), lambda p, r, jc: (p, jc[0] ^ r, jc[1], 0, 0))
        l_spec = pl.BlockSpec((None, None, RH, CS), lambda p, r, jc: (p, jc[0] ^ r, 0, 0))

    def body(jc_ref, g_ref, l_ref, own_ref, all_ref):
        s = g_ref[...] + l_ref[...].astype(F32)
        all_ref[...] = s.astype(BF)

        @pl.when(pl.program_id(1) == 0)
        def _():
            own_ref[...] = s

    return pl.pallas_call(
        body, grid_spec=pltpu.PrefetchScalarGridSpec(
            num_scalar_prefetch=1, grid=(P, N_CHIPS), in_specs=[g_spec, l_spec],
            out_specs=[pl.BlockSpec((None, RH, CS), lambda p, r, jc: (p, 0, 0)), pl.BlockSpec((None, None, RH, CS), lambda p, r, jc: (r, p, 0, 0))]),
        out_shape=[SDS((P, RH, CS), F32), SDS((N_CHIPS, P, RH, CS), BF)], compiler_params=_params("parallel", "arbitrary"), name=name)(jc, g, land)


def _chip_exchange_copies(srcs, lands, ssem, rsem):
    xi, yi, ci, _ = _place()
    return [pltpu.make_async_remote_copy(src_ref=srcs[i].at[r], dst_ref=lands[i].at[r], send_sem=ssem.at[3 * i + r - 1],
                                         recv_sem=rsem.at[3 * i + r - 1], device_id=_chip_peer(xi, yi, ci, r), device_id_type=MESH)
            for i in range(len(srcs)) for r in (1, 2, 3)]


def _side_chip_exchange(pieces):
    return _Side(pieces, [SDS(p.shape, p.dtype) for p in pieces], 3 * len(pieces), _chip_exchange_copies)


def _chip_exchange(pieces):
    n = len(pieces)

    def body(*refs):
        cps = _chip_exchange_copies(refs[:n], refs[n:2 * n], *refs[2 * n:])
        for cp in cps:
            cp.start()
        for cp in cps:
            cp.wait()

    return pl.pallas_call(body, in_specs=[ANY] * n, out_specs=[ANY] * n, out_shape=[SDS(p.shape, p.dtype) for p in pieces],
                          scratch_shapes=[pltpu.SemaphoreType.DMA((3 * n,)), pltpu.SemaphoreType.DMA((3 * n,))], name="grad_chip_exchange")(*pieces)


def _final_sum(own, land, jc, shard, l, L, name):
    P, RH, CS = own.shape

    def body(jc_ref, o_ref, a_ref, b_ref, c_ref, *rest):
        rest[-1][...] = ((o_ref[...] + a_ref[...].astype(F32)) + b_ref[...].astype(F32)) + c_ref[...].astype(F32)

    blk = pl.BlockSpec((None, RH, CS), lambda p, jc: (p, 0, 0))
    in_specs = [blk] + [pl.BlockSpec((None, None, RH, CS), functools.partial(lambda r, p, jc: (r, p, 0, 0), r)) for r in (1, 2, 3)]
    args = [jc, own, land, land, land]
    if shard is not None:
        in_specs.append(ANY)
        args.append(shard)
    return pl.pallas_call(
        body, grid_spec=pltpu.PrefetchScalarGridSpec(
            num_scalar_prefetch=1, grid=(P,), in_specs=in_specs,
            out_specs=pl.BlockSpec((None, None, None, RH, CS), lambda p, jc: (l, p, jc[1], 0, 0))),
        out_shape=SDS((L, P, 2, RH, CS), F32), input_output_aliases={5: 0} if shard is not None else {},
        compiler_params=_params("arbitrary"), name=name)(*args)


def _halves_exchange(shards):
    n = len(shards)

    def body(*refs):
        outs, (ssem, rsem) = refs[n:2 * n], refs[2 * n:]
        xi, yi, ci, _ = _place()
        cps = []
        for i in range(n):
            mine = outs[i].at[:, :, ci]
            cps.append(pltpu.make_async_remote_copy(src_ref=mine, dst_ref=mine, send_sem=ssem.at[i], recv_sem=rsem.at[i],
                                                    device_id=(xi, yi, 1 - ci), device_id_type=MESH))
            cps[-1].start()
        for i in range(n):
            land = outs[i].at[:, :, 1 - ci]
            pltpu.make_async_remote_copy(src_ref=land, dst_ref=land, send_sem=ssem.at[i], recv_sem=rsem.at[i],
                                         device_id=(xi, yi, 1 - ci), device_id_type=MESH).wait_recv()
        for cp in cps:
            cp.wait_send()

    return pl.pallas_call(body, in_specs=[ANY] * n, out_specs=[ANY] * n, out_shape=[SDS(s.shape, s.dtype) for s in shards],
                          input_output_aliases={i: i for i in range(n)},
                          scratch_shapes=[pltpu.SemaphoreType.DMA((n,)), pltpu.SemaphoreType.DMA((n,))], name="grad_halves_exchange")(*shards)


def _reduce_small(part):
    NR, Wd = part.shape
    ND = 2 * N_CHIPS

    def body(p_ref, o_ref, land, ssem, rsem):
        xi, yi, ci, j = _place()
        me = 2 * j + ci
        land[me] = p_ref[...]
        cps = []
        for rr in range(1, ND):
            dev = (xi ^ (rr >> 2), yi ^ ((rr >> 1) & 1), ci ^ (rr & 1))
            cps.append(pltpu.make_async_remote_copy(src_ref=p_ref, dst_ref=land.at[me], send_sem=ssem.at[rr - 1], recv_sem=rsem.at[rr - 1],
                                                    device_id=dev, device_id_type=MESH))
            cps[-1].start()
        for rr in range(1, ND):
            got = land.at[me ^ rr]
            pltpu.make_async_remote_copy(src_ref=got, dst_ref=got, send_sem=ssem.at[rr - 1], recv_sem=rsem.at[rr - 1],
                                         device_id=(xi, yi, ci), device_id_type=MESH).wait_recv()
        acc = land[0]
        for d in range(1, ND):
            acc = acc + land[d]
        o_ref[...] = acc
        for cp in cps:
            cp.wait_send()

    vm = pl.BlockSpec(memory_space=pltpu.VMEM)
    return pl.pallas_call(body, in_specs=[vm], out_specs=vm, out_shape=SDS((NR, Wd), F32),
                          scratch_shapes=[pltpu.VMEM((ND, NR, Wd), F32), pltpu.SemaphoreType.DMA((ND - 1,)), pltpu.SemaphoreType.DMA((ND - 1,))],
                          name="small_grad_allreduce")(part)


def _adamw(w, g, m, v, name):
    shape = w.shape
    C = shape[-1]
    R = w.size // C
    tb = _tile(R, max(8, (1 << 18) // C), 8)

    def body(w_ref, g_ref, m_ref, v_ref, d_ref, mo_ref, vo_ref):
        g = g_ref[...]
        m = ADAM_B1 * m_ref[...] + (1.0 - ADAM_B1) * g
        v = ADAM_B2 * v_ref[...] + (1.0 - ADAM_B2) * jnp.square(g)
        m_hat = m / (1.0 - ADAM_B1 ** ADAM_STEP)
        v_hat = v / (1.0 - ADAM_B2 ** ADAM_STEP)
        d_ref[...] = -ADAM_LR * (m_hat / (jnp.sqrt(v_hat) + ADAM_EPS) + ADAM_WD * w_ref[...])
        mo_ref[...] = m
        vo_ref[...] = v

    blk = pl.BlockSpec((tb, C), lambda i: (i, 0))
    outs = pl.pallas_call(body, grid=(R // tb,), in_specs=[blk] * 4, out_specs=[blk] * 3, out_shape=[SDS((R, C), F32)] * 3,
                          compiler_params=_params("parallel"), name=name)(*[t.reshape(R, C) for t in (w, g, m, v)])
    return [t.reshape(shape) for t in outs]


WEIGHTS = ("mix_norm_g", "w_in", "conv_dw_w", "conv_dw_b", "conv_ln_g", "conv_ln_b", "w_conv_out", "w_pool_grp", "pool_scale", "w_out",
           "xattn_norm_g", "mem_norm_g", "w_q", "w_kv", "w_o", "ffn_norm_g", "w_up", "ffn_dw_w", "w_down", "final_norm_g")
VECTORS = ("mix_norm_g", "conv_dw_b", "conv_ln_g", "conv_ln_b", "pool_scale", "xattn_norm_g", "mem_norm_g", "ffn_norm_g", "final_norm_g")


def _shard_view(t, kind):
    L, P, R, C = t.shape
    return t.reshape(L, P, 2, R // 2, C)


def _rows(t, width):
    return t.reshape(-1, width)


def _pack(parts):
    return jnp.concatenate([jnp.pad(p, ((0, (-p.shape[0]) % 8), (0, 0))) for p in parts], axis=0)


def kernel(x, mem, mix_norm_g, w_in, conv_dw_w, conv_dw_b, conv_ln_g, conv_ln_b, w_conv_out, w_pool_grp, pool_scale, w_out, xattn_norm_g, mem_norm_g, w_q, w_kv, w_o, ffn_norm_g, w_up, ffn_dw_w, w_down, final_norm_g, loss_target, m_mix_norm_g, m_w_in, m_conv_dw_w, m_conv_dw_b, m_conv_ln_g, m_conv_ln_b, m_w_conv_out, m_w_pool_grp, m_pool_scale, m_w_out, m_xattn_norm_g, m_mem_norm_g, m_w_q, m_w_kv, m_w_o, m_ffn_norm_g, m_w_up, m_ffn_dw_w, m_w_down, m_final_norm_g, v_mix_norm_g, v_w_in, v_conv_dw_w, v_conv_dw_b, v_conv_ln_g, v_conv_ln_b, v_w_conv_out, v_w_pool_grp, v_pool_scale, v_w_out, v_xattn_norm_g, v_mem_norm_g, v_w_q, v_w_kv, v_w_o, v_ffn_norm_g, v_w_up, v_ffn_dw_w, v_w_down, v_final_norm_g):
    w = dict(mix_norm_g=mix_norm_g, w_in=w_in, conv_dw_w=conv_dw_w, conv_dw_b=conv_dw_b, conv_ln_g=conv_ln_g, conv_ln_b=conv_ln_b,
             w_conv_out=w_conv_out, w_pool_grp=w_pool_grp, pool_scale=pool_scale, w_out=w_out, xattn_norm_g=xattn_norm_g,
             mem_norm_g=mem_norm_g, w_q=w_q, w_kv=w_kv, w_o=w_o, ffn_norm_g=ffn_norm_g, w_up=w_up, ffn_dw_w=ffn_dw_w, w_down=w_down,
             final_norm_g=final_norm_g)
    m = dict(zip(WEIGHTS, (m_mix_norm_g, m_w_in, m_conv_dw_w, m_conv_dw_b, m_conv_ln_g, m_conv_ln_b, m_w_conv_out, m_w_pool_grp, m_pool_scale,
                           m_w_out, m_xattn_norm_g, m_mem_norm_g, m_w_q, m_w_kv, m_w_o, m_ffn_norm_g, m_w_up, m_ffn_dw_w, m_w_down, m_final_norm_g)))
    v = dict(zip(WEIGHTS, (v_mix_norm_g, v_w_in, v_conv_dw_w, v_conv_dw_b, v_conv_ln_g, v_conv_ln_b, v_w_conv_out, v_w_pool_grp, v_pool_scale,
                           v_w_out, v_xattn_norm_g, v_mem_norm_g, v_w_q, v_w_kv, v_w_o, v_ffn_norm_g, v_w_up, v_ffn_dw_w, v_w_down, v_final_norm_g)))
    xi, yi, ci, j = _place()
    jc = jnp.stack([j, ci]).astype(jnp.int32)
    L = w_in.shape[0]
    G = len(POOL_WINDOWS)
    kinds = dict(BIG)

    def to_mat(name, t):
        if name == "w_pool":
            return jnp.swapaxes(t, 2, 3)
        return t[:, None]

    def from_mat(name, t):
        if name == "w_pool":
            return jnp.swapaxes(t, 2, 3)
        return t[:, 0]

    src = {name: w["w_pool_grp" if name == "w_pool" else name] for name, _ in BIG}

    KC, cs_c = conv_dw_w.shape[1], conv_dw_w.shape[2]
    KF, cs_f = ffn_dw_w.shape[1], ffn_dw_w.shape[2]
    taps = jnp.concatenate([conv_dw_w.reshape(L * KC, cs_c), ffn_dw_w.reshape(L * KF * (cs_f // cs_c), cs_c)], axis=0)
    n_taps = taps.shape[0]
    taps = jnp.pad(taps, ((0, (-n_taps) % 16), (0, 0)))
    names = [name for name, _ in BIG]
    mats = {name: to_mat(name, src[name]).astype(BF) for name in names}

    def layer_shards(l, subset):
        return [_shard_view(mats[name][l:l + 1], kinds[name]) for name in subset]

    def as_weights(subset, fulls):
        return {name: f.reshape(G if name == "w_pool" else 1, -1, f.shape[-1]) for name, f in zip(subset, fulls)}

    fulls = _gather_weights(layer_shards(0, names) + [_shard_view(taps[None, None], "row")], [kinds[name] for name in names] + ["row"])
    W = [as_weights(names, fulls[:-1])]
    taps_all = fulls[-1].reshape(N_CHIPS, -1, cs_c)[:, :n_taps]
    V = {name: w[name] for name in VECTORS}
    V["conv_dw_w"] = taps_all[:, :L * KC].reshape(N_CHIPS, L, KC, cs_c).transpose(1, 2, 0, 3).reshape(L, KC, N_CHIPS * cs_c)
    V["ffn_dw_w"] = taps_all[:, L * KC:].reshape(N_CHIPS, L, KF, cs_f).transpose(1, 2, 0, 3).reshape(L, KF, N_CHIPS * cs_f)

    Bn, S, D = x.shape
    Mn = mem.shape[1]
    dims = (Bn, S, Mn, D, conv_dw_b.shape[1], w_down.shape[1] * N_CHIPS)
    xt = x.reshape(Bn * S, D)
    memf = mem.reshape(Bn * Mn, D)
    mem_n = _rms_fwd(memf, V["mem_norm_g"], "mem_norm")
    saved = []
    ht = _rms_fwd(xt, V["mix_norm_g"][0], "l0_mix_norm")
    for l in range(L):
        sides = _Sides()
        if l + 1 < L:
            sides = _Sides({key: _side_gather(layer_shards(l + 1, subset), [kinds[name] for name in subset]) for key, subset in FWD_CARRY.items()})
        xt, ht, sv = _layer_fwd(xt, ht, mem_n, W[l], V, l, dims, sides, V["mix_norm_g"][l + 1] if l + 1 < L else None)
        saved.append(sv)
        if l + 1 < L:
            carried = [name for subset in FWD_CARRY.values() for name in subset]
            landed = [f for key in FWD_CARRY for f in sides.landed[key]]
            done = _gather_pass(landed, layer_shards(l + 1, carried), [kinds[name] for name in carried], f"gather_pass_l{l + 1}")
            W.append(as_weights(carried, done))
    loss, dx, dgf = _loss_bwd(xt, V["final_norm_g"], loss_target.reshape(Bn * S, D), "loss")
    loss = lax.psum(loss[0, 0], ("x", "y", "c"))

    late_names = [name for name in names if name not in EARLY]

    def chip_sums(gw, subset, l, tag):
        def view(g, name):
            g = g if g.ndim == 3 else g[None]
            P, R, C = g.shape
            return g.reshape(P, 2, R // 2, C) if kinds[name] == "col" else g.reshape(P, N_CHIPS, 2, R // (2 * N_CHIPS), C)

        gv = [view(gw[name][0], name) for name in subset]
        lands = _sibling_exchange([view(gw[name][1], name) for name in subset], [kinds[name] for name in subset],
                                  f"grad_sibling_exchange_{tag}_l{l}")
        own, pieces = {}, {}
        for name, g, land in zip(subset, gv, lands):
            own[name], pieces[name] = _chip_sum(g, land, kinds[name], jc, f"chip_sum_{name}_{l}")
        return own, pieces

    def carry(table, pieces):
        return _Sides({key: _side_chip_exchange([pieces[name] for name in subset]) for key, subset in table.items()})

    def landed(table, sides):
        return {name: land for key, subset in table.items() for name, land in zip(subset, sides.landed[key])}

    dxb, dmem_n = dx, None
    smalls, owns, got = [None] * L, [{} for _ in range(L)], [{} for _ in range(L)]
    late = None
    for l in reversed(range(L)):
        sides = carry(BWD_CARRY_LATE, late) if late is not None else _Sides()
        dx, dxb, gw, sm = _layer_bwd_mlp(dx, dxb, saved[l], W[l], V, l, dims, sides)
        if late is not None:
            got[l + 1].update(landed(BWD_CARRY_LATE, sides))
        own, early = chip_sums(gw, EARLY, l, "mlp")
        owns[l].update(own)
        sides = carry(BWD_CARRY_EARLY, early)
        dx, dxb, dmem_n, gw, sm2 = _layer_bwd_mix(dx, dxb, dmem_n, saved[l], mem_n, W[l], V, l, dims, sides)
        got[l].update(landed(BWD_CARRY_EARLY, sides))
        smalls[l] = {**sm, **sm2}
        own, late = chip_sums(gw, late_names, l, "mix")
        owns[l].update(own)
    got[0].update(zip(late_names, _chip_exchange([late[name] for name in late_names])))
    owns = [[o[name] for name in names] for o in owns]
    got = [[g[name] for name in names] for g in got]
    grad_x = dx.reshape(Bn, S, D)
    _, _, dgm = _rms_bwd(memf, V["mem_norm_g"], dmem_n, None, "mem_norm_b")
    small = {k: jnp.stack([sm[k] for sm in smalls]) if k in ("conv_dw_w", "ffn_dw_w") else jnp.concatenate([sm[k] for sm in smalls], axis=0)
             for k in smalls[0]}
    small["mem_norm_g"] = dgm
    small["final_norm_g"] = dgf
    mine = []
    for i, name in enumerate(names):
        shard = None
        for l in range(L):
            shard = _final_sum(owns[l][i], got[l][i], jc, shard, l, L, f"final_sum_{name}_{l}")
        mine.append(shard)
    gshards = _halves_exchange(mine)
    grads = {}
    for (name, kind), gs in zip(BIG, gshards):
        Lg, P, _, RH, CS = gs.shape
        grads["w_pool_grp" if name == "w_pool" else name] = from_mat(name, gs.reshape(Lg, P, 2 * RH, CS))

    small_w = conv_dw_b.shape[1]
    order = VECTORS + ("conv_dw_w", "ffn_dw_w")
    parts = [_rows(small[name], small_w) for name in order]
    counts = [p.shape[0] for p in parts]
    summed = _reduce_small(_pack(parts))
    off = 0
    for name, cnt in zip(order, counts):
        t = summed[off:off + cnt]
        off += cnt + (-cnt) % 8
        if name in VECTORS:
            grads[name] = t.reshape(w[name].shape)
        else:
            full = t.reshape(small[name].shape)
            cs = w[name].shape[2]
            grads[name] = lax.dynamic_slice_in_dim(full, j * cs, cs, axis=2)

    delta, new_m, new_v = {}, {}, {}
    for name, _ in BIG:
        key = "w_pool_grp" if name == "w_pool" else name
        outs = _adamw(*[to_mat(name, t) for t in (w[key], grads[key], m[key], v[key])], "adamw_" + name)
        delta[key], new_m[key], new_v[key] = [from_mat(name, t) for t in outs]
    vec = [_pack([_rows(d[name], small_w) for name in VECTORS]) for d in (w, grads, m, v)]
    outs = _adamw(*vec, "adamw_vectors")
    off = 0
    for name in VECTORS:
        cnt = w[name].size // small_w
        for d, t in zip((delta, new_m, new_v), outs):
            d[name] = t[off:off + cnt].reshape(w[name].shape)
        off += cnt + (-cnt) % 8
    for name in ("conv_dw_w", "ffn_dw_w"):
        delta[name], new_m[name], new_v[name] = _adamw(w[name], grads[name], m[name], v[name], "adamw_" + name)

    return (loss, grad_x, *[grads[k] for k in WEIGHTS], *[delta[k] for k in WEIGHTS], *[new_m[k] for k in WEIGHTS], *[new_v[k] for k in WEIGHTS])
```

```python
import functools

import jax
import jax.numpy as jnp
from jax import lax
from jax.experimental import pallas as pl
from jax.experimental.pallas import tpu as pltpu

F32 = jnp.float32
BF = jnp.bfloat16
SDS = jax.ShapeDtypeStruct
MESH = pl.DeviceIdType.MESH
ANY = pl.BlockSpec(memory_space=pl.ANY)

EPS = 1e-6
XA_HEADS = 4
POOL_WINDOWS = (2, 4, 8, 16)
N_CHIPS = 4
ADAM_LR, ADAM_B1, ADAM_B2, ADAM_EPS, ADAM_WD, ADAM_STEP = 0.001, 0.9, 0.999, 1e-08, 0.01, 10

LANES = 128
ROW_BLOCK = 512
VMEM_LIMIT = 56 * 1024 * 1024


def _params(*sem):
    return pltpu.CompilerParams(dimension_semantics=sem if sem else None, vmem_limit_bytes=VMEM_LIMIT)


def _tile(n, cap, mult=LANES):
    if n <= cap:
        return n
    for t in range(cap - cap % mult, 0, -mult):
        if n % t == 0:
            return t
    return n


_DN = {"nn": (((1,), (0,)), ((), ())), "nt": (((1,), (1,)), ((), ())), "tn": (((0,), (0,)), ((), ()))}


class _Side:
    def __init__(self, ins, outs, n, make):
        self.ins, self.outs, self.n, self.make = list(ins), list(outs), n, make


def _call(body, args, *, grid, in_specs, out_specs, out_shape, semantics, name, scratch_shapes=(), side=None, prefetch=(), aliases=None):
    n_pf = len(prefetch)
    aliases = {n_pf + i: o for i, o in (aliases or {}).items()}
    n_in, n_out, n_scr = len(args), len(out_shape), len(scratch_shapes)
    n_si, n_so = (len(side.ins), len(side.outs)) if side is not None else (0, 0)

    def carrying(*refs):
        pf, refs = refs[:n_pf], refs[n_pf:]
        ins, s_in = refs[:n_in], refs[n_in:n_in + n_si]
        outs, s_out = refs[n_in + n_si:n_in + n_si + n_out], refs[n_in + n_si + n_out:n_in + n_si + n_out + n_so]
        scr = refs[n_in + n_si + n_out + n_so:]
        if side is None:
            return body(*pf, *ins, *outs, *scr)
        copies = side.make(s_in, s_out, scr[n_scr], scr[n_scr + 1])
        ids = [pl.program_id(d) for d in range(len(grid))]
        first, last = ids[0] == 0, ids[0] == grid[0] - 1
        for d in range(1, len(grid)):
            first, last = first & (ids[d] == 0), last & (ids[d] == grid[d] - 1)

        @pl.when(first)
        def _():
            for cp in copies:
                cp.start()

        body(*pf, *ins, *outs, *scr[:n_scr])

        @pl.when(last)
        def _():
            for cp in copies:
                cp.wait()

    sems = [pltpu.SemaphoreType.DMA((side.n,)), pltpu.SemaphoreType.DMA((side.n,))] if side is not None else []
    outs = pl.pallas_call(
        carrying, grid_spec=pltpu.PrefetchScalarGridSpec(
            num_scalar_prefetch=n_pf, grid=grid, in_specs=list(in_specs) + [ANY] * n_si, out_specs=list(out_specs) + [ANY] * n_so,
            scratch_shapes=list(scratch_shapes) + sems),
        out_shape=list(out_shape) + (side.outs if side is not None else []), input_output_aliases=aliases,
        compiler_params=_params(*(semantics if side is None else ["arbitrary"] * len(grid))), name=name)(
            *prefetch, *args, *(side.ins if side is not None else []))
    return list(outs) if side is None else (list(outs[:n_out]), list(outs[n_out:]))


MM_VMEM_BUDGET = 40 * 1024 * 1024
MM_STEP_MACS = 2200 * 1024 * 1024
MXU_WIDTH = 256
MM_STEP_COST_BYTES = 1 << 20


def _divisors(n):
    return [t for t in range(LANES, n + 1, LANES) if n % t == 0] or [n]


def _mm_tiles(M, N, K, a_bytes, b_bytes, o_bytes):
    best = None
    for tk in _divisors(K):
        for tm in _divisors(M):
            for tn in _divisors(N):
                nk = K // tk
                foot = 2 * (tm * tk * a_bytes + tk * tn * b_bytes + tm * tn * o_bytes) + (tm * tn * 4 if nk > 1 else 0)
                if foot > MM_VMEM_BUDGET or tm * tn * tk > MM_STEP_MACS or tn < min(N, MXU_WIDTH) or tm < min(M, MXU_WIDTH):
                    continue
                steps = (M // tm) * (N // tn) * nk
                traffic = M * K * a_bytes * (N // tn if nk > 1 else 1) + K * N * b_bytes * (M // tm) + M * N * o_bytes
                exposed = tm * tk * a_bytes + tk * tn * b_bytes + tm * tn * o_bytes
                cost = traffic + exposed + steps * MM_STEP_COST_BYTES + (nk - 1) * M * N * 8
                if best is None or cost < best[0]:
                    best = (cost, tm, tn, tk)
    assert best is not None, (M, N, K)
    return best[1:]


def _mm(a, b, dims, out_dtype, name, res=None, bl=None, side=None, twin=None):
    bs = b.shape[1:] if bl is not None else b.shape
    if dims == "nn":
        (M, K), (K2, N) = a.shape, bs
    elif dims == "nt":
        (M, K), (N, K2) = a.shape, bs
    else:
        (K, M), (K2, N) = a.shape, bs
    assert K == K2, (name, a.shape, b.shape)
    tm, tn, tk = _mm_tiles(M, N, K, a.dtype.itemsize, b.dtype.itemsize, jnp.dtype(out_dtype).itemsize
                           + (res.dtype.itemsize if res is not None else 0) + (jnp.dtype(twin).itemsize if twin is not None else 0))
    nk = K // tk
    lead = (None,) if bl is not None else ()
    pre = (lambda *ix: (bl,) + ix) if bl is not None else (lambda *ix: ix)
    if dims == "tn":
        a_spec = pl.BlockSpec((tk, tm), lambda i, j, k: (k, i))
    else:
        a_spec = pl.BlockSpec((tm, tk), lambda i, j, k: (i, k))
    if dims == "nt":
        b_spec = pl.BlockSpec(lead + (tn, tk), lambda i, j, k: pre(j, k))
    else:
        b_spec = pl.BlockSpec(lead + (tk, tn), lambda i, j, k: pre(k, j))
    o_spec = pl.BlockSpec((tm, tn), lambda i, j, k: (i, j))
    in_specs, args = [a_spec, b_spec], [a, b]
    if res is not None:
        in_specs.append(o_spec)
        args.append(res)
    n_main = len(args)
    n_out = 1 if twin is None else 2

    def body(*refs):
        a_ref, b_ref = refs[0], refs[1]
        r_ref = refs[2] if res is not None else None
        o_ref = refs[n_main]
        p = lax.dot_general(a_ref[...].astype(BF), b_ref[...].astype(BF), _DN[dims], preferred_element_type=F32)

        def finish(t):
            if r_ref is not None:
                t = t + r_ref[...]
            o_ref[...] = t.astype(out_dtype)
            if twin is not None:
                refs[n_main + 1][...] = t.astype(twin)

        if nk == 1:
            finish(p)
        else:
            acc = refs[n_main + n_out]
            k = pl.program_id(2)

            @pl.when(k == 0)
            def _():
                acc[...] = p

            @pl.when(k > 0)
            def _():
                acc[...] += p

            @pl.when(k == nk - 1)
            def _():
                finish(acc[...])

    got = _call(body, args, grid=(M // tm, N // tn, nk), in_specs=in_specs, out_specs=[o_spec] * n_out,
                out_shape=[SDS((M, N), out_dtype)] + ([SDS((M, N), twin)] if twin is not None else []),
                scratch_shapes=[pltpu.VMEM((tm, tn), F32)] if nk > 1 else [], semantics=("parallel", "parallel", "arbitrary"),
                name=name, side=side)
    outs, landed = (got, None) if side is None else got
    out = outs[0] if twin is None else (outs[0], outs[1])
    return out if side is None else (out, landed)


def _rms(x, g):
    return x * lax.rsqrt(jnp.mean(x * x, axis=-1, keepdims=True) + EPS) * g


def _ln_silu(x, g, b):
    mu = jnp.mean(x, axis=-1, keepdims=True)
    xc = x - mu
    var = jnp.mean(xc * xc, axis=-1, keepdims=True)
    return jax.nn.silu(xc * lax.rsqrt(var + EPS) * g + b)


def _merge(gc, gp, yc, yp, ps):
    return jax.nn.sigmoid(gc) * yc + jax.nn.sigmoid(gp) * (yp * ps)


def _gated(gate, val):
    return jax.nn.gelu(gate) * val


def _rms_fwd(x, g, name):
    T, D = x.shape
    tb = _tile(T, ROW_BLOCK, 8)

    def body(x_ref, g_ref, o_ref):
        o_ref[...] = _rms(x_ref[...], g_ref[...]).astype(BF)

    row = pl.BlockSpec((tb, D), lambda i: (i, 0))
    return pl.pallas_call(body, grid=(T // tb,), in_specs=[row, pl.BlockSpec((1, D), lambda i: (0, 0))], out_specs=row,
                          out_shape=SDS((T, D), BF), compiler_params=_params("parallel"), name=name)(x, g.reshape(1, D))


def _rms_bwd(x, g, dh, dres, name):
    T, D = x.shape
    tb = _tile(T, ROW_BLOCK, 8)

    def body(*refs):
        if dres is not None:
            x_ref, g_ref, dh_ref, dres_ref, dx_ref, dxb_ref, dg_ref = refs
        else:
            x_ref, g_ref, dh_ref, dx_ref, dxb_ref, dg_ref = refs
        _, vjp = jax.vjp(_rms, x_ref[...], g_ref[...])
        dx, dg = vjp(dh_ref[...].astype(F32))
        if dres is not None:
            dx = dx + dres_ref[...]
        dx_ref[...] = dx
        dxb_ref[...] = dx.astype(BF)

        @pl.when(pl.program_id(0) == 0)
        def _():
            dg_ref[...] = jnp.zeros_like(dg_ref)

        dg_ref[...] += dg

    row = pl.BlockSpec((tb, D), lambda i: (i, 0))
    vec = pl.BlockSpec((1, D), lambda i: (0, 0))
    ins = [x, g.reshape(1, D), dh] + ([dres] if dres is not None else [])
    return pl.pallas_call(
        body, grid=(T // tb,), in_specs=[row, vec, row] + ([row] if dres is not None else []), out_specs=[row, row, vec],
        out_shape=[SDS((T, D), F32), SDS((T, D), BF), SDS((1, D), F32)], compiler_params=_params("arbitrary"), name=name)(*ins)


def _row_tile(M, K, N, per_row_bytes):
    fixed = K * N * 2
    fit = [t for t in _divisors(M) if fixed + 2 * t * per_row_bytes <= MM_VMEM_BUDGET and t * K * N <= MM_STEP_MACS]
    return max(fit) if fit else min(_divisors(M))


def _mm_rms_fwd(a, b, res, g, name, side=None):
    M, K = a.shape
    N = b.shape[2]
    tm = _row_tile(M, K, N, K * 2 + N * (4 + 4 + 2))

    def body(a_ref, b_ref, r_ref, g_ref, x_ref, h_ref):
        x = r_ref[...] + lax.dot_general(a_ref[...], b_ref[...], _DN["nn"], preferred_element_type=F32)
        x_ref[...] = x
        h_ref[...] = _rms(x, g_ref[...]).astype(BF)

    row = pl.BlockSpec((tm, N), lambda i: (i, 0))
    return _call(body, (a, b, res, g.reshape(1, N)), grid=(M // tm,),
                 in_specs=[pl.BlockSpec((tm, K), lambda i: (i, 0)), pl.BlockSpec((None, K, N), lambda i: (0, 0, 0), pipeline_mode=pl.Buffered(1)), row,
                           pl.BlockSpec((1, N), lambda i: (0, 0))],
                 out_specs=[row, row], out_shape=[SDS((M, N), F32), SDS((M, N), BF)], semantics=("parallel",), name=name, side=side)


def _mm_rms_bwd(a, b, x, g, dres, name):
    M, K = a.shape
    N = b.shape[1]
    tm = _row_tile(M, K, N, K * 2 + N * (4 + 4 + 4 + 2))

    def body(a_ref, b_ref, x_ref, g_ref, r_ref, dx_ref, dxb_ref, dg_ref):
        dh = lax.dot_general(a_ref[...], b_ref[...], _DN["nt"], preferred_element_type=F32)
        _, vjp = jax.vjp(_rms, x_ref[...], g_ref[...])
        dx, dg = vjp(dh)
        dx = dx + r_ref[...]
        dx_ref[...] = dx
        dxb_ref[...] = dx.astype(BF)

        @pl.when(pl.program_id(0) == 0)
        def _():
            dg_ref[...] = jnp.zeros_like(dg_ref)

        dg_ref[...] += dg

    row = pl.BlockSpec((tm, N), lambda i: (i, 0))
    vec = pl.BlockSpec((1, N), lambda i: (0, 0))
    return pl.pallas_call(
        body, grid=(M // tm,),
        in_specs=[pl.BlockSpec((tm, K), lambda i: (i, 0)), pl.BlockSpec((None, N, K), lambda i: (0, 0, 0), pipeline_mode=pl.Buffered(1)), row, vec, row],
        out_specs=[row, row, vec], out_shape=[SDS((M, N), F32), SDS((M, N), BF), SDS((1, N), F32)],
        compiler_params=_params("arbitrary"), name=name)(a, b, x, g.reshape(1, N), dres)


def _loss_bwd(x, g, target, name):
    T, D = x.shape
    tb = _tile(T, ROW_BLOCK, 8)
    nb = T // tb

    def body(x_ref, g_ref, t_ref, loss_ref, dx_ref, dg_ref, acc):
        i = pl.program_id(0)
        y, vjp = jax.vjp(_rms, x_ref[...], g_ref[...])
        err = y - t_ref[...]
        dx, dg = vjp(err * (1.0 / D))
        dx_ref[...] = dx

        @pl.when(i == 0)
        def _():
            dg_ref[...] = jnp.zeros_like(dg_ref)
            acc[...] = jnp.zeros_like(acc)

        dg_ref[...] += dg
        acc[...] += jnp.sum(err * err, axis=0, keepdims=True)

        @pl.when(i == nb - 1)
        def _():
            loss_ref[...] = jnp.full(loss_ref.shape, (0.5 / D) * jnp.sum(acc[...]), F32)

    row = pl.BlockSpec((tb, D), lambda i: (i, 0))
    vec = pl.BlockSpec((1, D), lambda i: (0, 0))
    return pl.pallas_call(
        body, grid=(nb,), in_specs=[row, vec, row], out_specs=[pl.BlockSpec((1, LANES), lambda i: (0, 0)), row, vec],
        out_shape=[SDS((1, LANES), F32), SDS((T, D), F32), SDS((1, D), F32)], scratch_shapes=[pltpu.VMEM((1, D), F32)],
        compiler_params=_params("arbitrary"), name=name)(x, g.reshape(1, D), target)


def _ln_silu_fwd(cv, g, b, name):
    T, C = cv.shape
    tb = _tile(T, ROW_BLOCK, 8)

    def body(x_ref, g_ref, b_ref, o_ref):
        o_ref[...] = _ln_silu(x_ref[...], g_ref[...], b_ref[...]).astype(BF)

    row = pl.BlockSpec((tb, C), lambda i: (i, 0))
    vec = pl.BlockSpec((1, C), lambda i: (0, 0))
    return pl.pallas_call(body, grid=(T // tb,), in_specs=[row, vec, vec], out_specs=row, out_shape=SDS((T, C), BF),
                          compiler_params=_params("parallel"), name=name)(cv, g.reshape(1, C), b.reshape(1, C))


def _ln_silu_bwd(cv, g, b, dy, name):
    T, C = cv.shape
    tb = _tile(T, ROW_BLOCK, 8)

    def body(x_ref, g_ref, b_ref, dy_ref, dx_ref, dg_ref, db_ref):
        _, vjp = jax.vjp(_ln_silu, x_ref[...], g_ref[...], b_ref[...])
        dx, dg, db = vjp(dy_ref[...].astype(F32))
        dx_ref[...] = dx

        @pl.when(pl.program_id(0) == 0)
        def _():
            dg_ref[...] = jnp.zeros_like(dg_ref)
            db_ref[...] = jnp.zeros_like(db_ref)

        dg_ref[...] += dg
        db_ref[...] += db

    row = pl.BlockSpec((tb, C), lambda i: (i, 0))
    vec = pl.BlockSpec((1, C), lambda i: (0, 0))
    return pl.pallas_call(
        body, grid=(T // tb,), in_specs=[row, vec, vec, row], out_specs=[row, vec, vec],
        out_shape=[SDS((T, C), F32), SDS((1, C), F32), SDS((1, C), F32)], compiler_params=_params("arbitrary"),
        name=name)(cv, g.reshape(1, C), b.reshape(1, C), dy)


def _merge_fwd(proj, yc, yp, ps, C, name):
    T, D = yc.shape
    tb = _tile(T, ROW_BLOCK, 8)
    nj = D // C

    def body(gc_ref, gp_ref, yc_ref, yp_ref, ps_ref, o_ref):
        o_ref[...] = _merge(gc_ref[...], gp_ref[...], yc_ref[...].astype(F32), yp_ref[...].astype(F32), ps_ref[...]).astype(BF)

    blk = pl.BlockSpec((tb, C), lambda i, j: (i, j))
    return pl.pallas_call(
        body, grid=(T // tb, nj),
        in_specs=[pl.BlockSpec((tb, C), lambda i, j: (i, 3 + j)), pl.BlockSpec((tb, C), lambda i, j: (i, 3 + nj + j)), blk, blk,
                  pl.BlockSpec((1, C), lambda i, j: (0, j))],
        out_specs=blk, out_shape=SDS((T, D), BF), compiler_params=_params("parallel", "parallel"), name=name)(proj, proj, yc, yp, ps.reshape(1, D))


def _merge_bwd(proj, yc, yp, ps, dm, C, name, side=None):
    T, D = yc.shape
    tb = _tile(T, ROW_BLOCK, 8)
    nj = D // C

    def body(gc_ref, gp_ref, yc_ref, yp_ref, ps_ref, dm_ref, dgc_ref, dgp_ref, dyc_ref, dyp_ref, dps_ref):
        _, vjp = jax.vjp(_merge, gc_ref[...], gp_ref[...], yc_ref[...].astype(F32), yp_ref[...].astype(F32), ps_ref[...])
        dgc, dgp, dyc, dyp, dps = vjp(dm_ref[...].astype(F32))
        dgc_ref[...] = dgc.astype(BF)
        dgp_ref[...] = dgp.astype(BF)
        dyc_ref[...] = dyc.astype(BF)
        dyp_ref[...] = dyp.astype(BF)

        @pl.when(pl.program_id(1) == 0)
        def _():
            dps_ref[...] = jnp.zeros_like(dps_ref)

        dps_ref[...] += dps

    blk = pl.BlockSpec((tb, C), lambda j, i: (i, j))
    vec = pl.BlockSpec((1, C), lambda j, i: (0, j))
    return _call(
        body, (proj, proj, yc, yp, ps.reshape(1, D), dm), grid=(nj, T // tb),
        in_specs=[pl.BlockSpec((tb, C), lambda j, i: (i, 3 + j)), pl.BlockSpec((tb, C), lambda j, i: (i, 3 + nj + j)), blk, blk, vec, blk],
        out_specs=[blk, blk, blk, blk, vec], out_shape=[SDS((T, D), BF)] * 4 + [SDS((1, D), F32)],
        semantics=("parallel", "arbitrary"), name=name, side=side)


def _shd(v, s, rows):
    if s == 0:
        return v
    return jnp.where(rows >= s, pltpu.roll(v, s, 0), 0.0)


def _shu(v, s, rows):
    if s == 0:
        return v
    n = v.shape[0]
    return jnp.where(rows < n - s, pltpu.roll(v, n - s, 0), 0.0)


def _glu_conv_fwd(proj, w, b, Bn, S, C, name):
    K = w.shape[0]
    sl = min(LANES, C)
    ns = C // sl

    def body(a_ref, gl_ref, w_ref, b_ref, o_ref):
        y0 = a_ref[...] * jax.nn.sigmoid(gl_ref[...])
        rows = lax.broadcasted_iota(jnp.int32, y0.shape, 0)
        acc = jnp.zeros_like(y0) + b_ref[...]
        for k in range(K):
            acc = acc + w_ref[k:k + 1, :] * _shd(y0, K - 1 - k, rows)
        o_ref[...] = acc

    return pl.pallas_call(
        body, grid=(Bn, ns),
        in_specs=[pl.BlockSpec((S, sl), lambda bi, j: (bi, j)), pl.BlockSpec((S, sl), lambda bi, j: (bi, ns + j)),
                  pl.BlockSpec((K, sl), lambda bi, j: (0, j)), pl.BlockSpec((1, sl), lambda bi, j: (0, j))],
        out_specs=pl.BlockSpec((S, sl), lambda bi, j: (bi, j)), out_shape=SDS((Bn * S, C), F32),
        compiler_params=_params("parallel", "parallel"), name=name)(proj, proj, w, b.reshape(1, C))


def _glu_conv_bwd(proj, w, dcv, Bn, S, C, name, side=None):
    K = w.shape[0]
    sl = min(LANES, C)
    ns = C // sl

    def body(a_ref, gl_ref, w_ref, d_ref, da_ref, dgl_ref, dw_ref, db_ref):
        a = a_ref[...]
        sg = jax.nn.sigmoid(gl_ref[...])
        y0 = a * sg
        d = d_ref[...]
        rows = lax.broadcasted_iota(jnp.int32, y0.shape, 0)

        @pl.when(pl.program_id(1) == 0)
        def _():
            dw_ref[...] = jnp.zeros_like(dw_ref)
            db_ref[...] = jnp.zeros_like(db_ref)

        dy0 = jnp.zeros_like(y0)
        for k in range(K):
            s = K - 1 - k
            dw_ref[k:k + 1, :] += jnp.sum(d * _shd(y0, s, rows), axis=0, keepdims=True)
            dy0 = dy0 + w_ref[k:k + 1, :] * _shu(d, s, rows)
        db_ref[...] += jnp.sum(d, axis=0, keepdims=True)
        da_ref[...] = (dy0 * sg).astype(BF)
        dgl_ref[...] = (dy0 * a * sg * (1.0 - sg)).astype(BF)

    blk = pl.BlockSpec((S, sl), lambda j, bi: (bi, j))
    return _call(
        body, (proj, proj, w, dcv), grid=(ns, Bn),
        in_specs=[blk, pl.BlockSpec((S, sl), lambda j, bi: (bi, ns + j)), pl.BlockSpec((K, sl), lambda j, bi: (0, j)), blk],
        out_specs=[blk, blk, pl.BlockSpec((K, sl), lambda j, bi: (0, j)), pl.BlockSpec((1, sl), lambda j, bi: (0, j))],
        out_shape=[SDS((Bn * S, C), BF), SDS((Bn * S, C), BF), SDS((K, C), F32), SDS((1, C), F32)],
        semantics=("parallel", "arbitrary"), name=name, side=side)


def _pool_z(u, g, rows):
    s2 = u + _shd(u, 1, rows)
    s4 = s2 + _shd(s2, 2, rows)
    s8 = s4 + _shd(s4, 4, rows)
    s16 = s8 + _shd(s8, 8, rows)
    sw = jnp.where(g == 0, s2, jnp.where(g == 1, s4, jnp.where(g == 2, s8, s16)))
    cnt = jnp.minimum(rows + 1, POOL_WINDOWS[0] << g).astype(F32)
    return sw / cnt - u, cnt


def _pool_fwd(proj, wpt, l, Bn, S, C, D, name):
    G = len(POOL_WINDOWS)
    gd, go = C // G, D // G

    def body(u_ref, w_ref, o_ref):
        g = pl.program_id(1)
        u = u_ref[...]
        rows = lax.broadcasted_iota(jnp.int32, u.shape, 0)
        zp, _ = _pool_z(u, g, rows)
        o_ref[...] = lax.dot_general(zp.astype(BF), w_ref[...], _DN["nt"], preferred_element_type=F32).astype(BF)

    return pl.pallas_call(
        body, grid=(Bn, G),
        in_specs=[pl.BlockSpec((S, gd), lambda bi, g: (bi, 2 * G + g)), pl.BlockSpec((None, go, gd), lambda bi, g: (l * G + g, 0, 0))],
        out_specs=pl.BlockSpec((S, go), lambda bi, g: (bi, g)), out_shape=SDS((Bn * S, D), BF),
        compiler_params=_params("parallel", "parallel"), name=name)(proj, wpt)


def _pool_bwd(proj, wpt, dyp, l, Bn, S, C, D, name):
    G = len(POOL_WINDOWS)
    gd, go = C // G, D // G

    def body(u_ref, w_ref, d_ref, du_ref, dw_ref):
        g = pl.program_id(0)
        u = u_ref[...]
        rows = lax.broadcasted_iota(jnp.int32, u.shape, 0)
        zp, cnt = _pool_z(u, g, rows)
        d = d_ref[...]
        dzp = lax.dot_general(d, w_ref[...], _DN["nn"], preferred_element_type=F32)

        @pl.when(pl.program_id(1) == 0)
        def _():
            dw_ref[...] = jnp.zeros_like(dw_ref)

        dw_ref[...] += lax.dot_general(d, zp.astype(BF), _DN["tn"], preferred_element_type=F32)
        dsw = dzp / cnt
        zero = jnp.zeros_like(dsw)
        d16 = jnp.where(g == 3, dsw, zero)
        d8 = jnp.where(g == 2, dsw, zero) + d16 + _shu(d16, 8, rows)
        d4 = jnp.where(g == 1, dsw, zero) + d8 + _shu(d8, 4, rows)
        d2 = jnp.where(g == 0, dsw, zero) + d4 + _shu(d4, 2, rows)
        d1 = d2 + _shu(d2, 1, rows)
        du_ref[...] = (d1 - dzp).astype(BF)

    return pl.pallas_call(
        body, grid=(G, Bn),
        in_specs=[pl.BlockSpec((S, gd), lambda g, bi: (bi, 2 * G + g)), pl.BlockSpec((None, go, gd), lambda g, bi: (l * G + g, 0, 0)),
                  pl.BlockSpec((S, go), lambda g, bi: (bi, g))],
        out_specs=[pl.BlockSpec((S, gd), lambda g, bi: (bi, g)), pl.BlockSpec((None, go, gd), lambda g, bi: (g, 0, 0))],
        out_shape=[SDS((Bn * S, C), BF), SDS((G, go, gd), F32)],
        compiler_params=_params("parallel", "arbitrary"), name=name)(proj, wpt, dyp)


def _ffn_conv(u, w_ref, rows):
    K = w_ref.shape[0]
    acc = w_ref[K - 1:K, :] * u
    for k in range(K - 1):
        acc = acc + w_ref[k:k + 1, :] * _shd(u, K - 1 - k, rows)
    return acc


def _ffn_cb(F):
    return _tile(F, 256)


def _ffn_act_fwd(up0, w, Bn, S, F, name):
    cb = _ffn_cb(F)
    nj = F // cb

    def body(g_ref, v_ref, wg_ref, wv_ref, o_ref):
        rows = lax.broadcasted_iota(jnp.int32, g_ref.shape, 0)
        o_ref[...] = _gated(_ffn_conv(g_ref[...], wg_ref, rows), _ffn_conv(v_ref[...], wv_ref, rows)).astype(BF)

    K = w.shape[0]
    return pl.pallas_call(
        body, grid=(Bn, nj),
        in_specs=[pl.BlockSpec((S, cb), lambda bi, j: (bi, j)), pl.BlockSpec((S, cb), lambda bi, j: (bi, nj + j)),
                  pl.BlockSpec((K, cb), lambda bi, j: (0, j)), pl.BlockSpec((K, cb), lambda bi, j: (0, nj + j))],
        out_specs=pl.BlockSpec((S, cb), lambda bi, j: (bi, j)), out_shape=SDS((Bn * S, F), BF),
        compiler_params=_params("parallel", "parallel"), name=name)(up0, up0, w, w)


SUBLANES = 8
FFN_HALO = SUBLANES
FFN_ROWS = 64
GELU_C0, GELU_C1 = 0.7978845608028654, 0.044715


def _gelu_and_grad(x):
    x2 = x * x
    t = jnp.tanh(GELU_C0 * (x + GELU_C1 * (x2 * x)))
    cdf = 0.5 * (1.0 + t)
    return x * cdf, cdf + (0.5 * GELU_C0) * x * (1.0 - t * t) * (1.0 + (3.0 * GELU_C1) * x2)


def _ffn_act_bwd(up0, w, dg, Bn, S, F, name, side=None):
    cb = min(LANES, F)
    nj = F // cb
    K = w.shape[0]
    rc = FFN_ROWS if S % FFN_ROWS == 0 else S
    win = rc + 2 * FFN_HALO
    assert K - 1 <= FFN_HALO and rc % SUBLANES == 0

    def body(g_ref, v_ref, wg_ref, wv_ref, d_ref, dgo_ref, dvo_ref, dwg_ref, dwv_ref, gp, vp, dp):
        for pad, src in ((gp, g_ref), (vp, v_ref), (dp, d_ref)):
            pad[0:FFN_HALO, :] = jnp.zeros((FFN_HALO, cb), F32)
            pad[FFN_HALO + S:, :] = jnp.zeros((FFN_HALO, cb), F32)
            pad[FFN_HALO:FFN_HALO + S, :] = src[...].astype(F32)
        wg = [wg_ref[k:k + 1, :] for k in range(K)]
        wv = [wv_ref[k:k + 1, :] for k in range(K)]

        def taps(u):
            return [pltpu.roll(u, K - 1 - k, 0) for k in range(K - 1)] + [u]

        def conv(us, ws):
            acc = ws[K - 1] * us[K - 1]
            for k in range(K - 1):
                acc = acc + ws[k] * us[k]
            return acc

        def conv_t(dc, ws):
            acc = ws[K - 1] * dc
            for k in range(K - 1):
                acc = acc + ws[k] * pltpu.roll(dc, win - (K - 1 - k), 0)
            return acc

        def fold(t):
            acc = t[FFN_HALO:FFN_HALO + SUBLANES]
            for i in range(1, rc // SUBLANES):
                acc = acc + t[FFN_HALO + SUBLANES * i:FFN_HALO + SUBLANES * (i + 1)]
            return acc

        def chunk(c, sums):
            r0 = pl.multiple_of(c * rc, SUBLANES)
            gs, vs, d = taps(gp[pl.ds(r0, win), :]), taps(vp[pl.ds(r0, win), :]), dp[pl.ds(r0, win), :]
            ge, dge = _gelu_and_grad(conv(gs, wg))
            dgc = d * conv(vs, wv) * dge
            dvc = d * ge
            dgo_ref[pl.ds(r0, rc), :] = conv_t(dgc, wg)[FFN_HALO:FFN_HALO + rc].astype(BF)
            dvo_ref[pl.ds(r0, rc), :] = conv_t(dvc, wv)[FFN_HALO:FFN_HALO + rc].astype(BF)
            new = [fold(dc * u) for us, dc in ((gs, dgc), (vs, dvc)) for u in us]
            return tuple(a + b for a, b in zip(sums, new))

        sums = lax.fori_loop(0, S // rc, chunk, tuple(jnp.zeros((SUBLANES, cb), F32) for _ in range(2 * K)))

        @pl.when(pl.program_id(1) == 0)
        def _():
            dwg_ref[...] = jnp.zeros_like(dwg_ref)
            dwv_ref[...] = jnp.zeros_like(dwv_ref)

        for k in range(K):
            dwg_ref[k:k + 1, :] += jnp.sum(sums[k], axis=0, keepdims=True)
            dwv_ref[k:k + 1, :] += jnp.sum(sums[K + k], axis=0, keepdims=True)

    blk = pl.BlockSpec((S, cb), lambda j, bi: (bi, j))
    wblk = pl.BlockSpec((K, cb), lambda j, bi: (0, j))
    return _call(
        body, (up0, up0, w, w, dg), grid=(nj, Bn),
        in_specs=[blk, pl.BlockSpec((S, cb), lambda j, bi: (bi, nj + j)), wblk, pl.BlockSpec((K, cb), lambda j, bi: (0, nj + j)), blk],
        out_specs=[blk, blk, wblk, wblk],
        out_shape=[SDS((Bn * S, F), BF), SDS((Bn * S, F), BF), SDS((K, F), F32), SDS((K, F), F32)],
        scratch_shapes=[pltpu.VMEM((S + 2 * FFN_HALO, cb), F32)] * 3, semantics=("parallel", "arbitrary"), name=name, side=side)


def _softmax_rows(q, k, scale):
    sc = lax.dot_general(q, k, _DN["nt"], preferred_element_type=F32) * scale
    e = jnp.exp(sc - jnp.max(sc, axis=-1, keepdims=True))
    return e / jnp.sum(e, axis=-1, keepdims=True)


def _attn_ts(S):
    return _tile(S, 1024, 8)


def _attn_fwd(q, kv, Bn, S, Mn, D, name):
    H = XA_HEADS
    dh = D // H
    ts = _attn_ts(S)
    nsb = S // ts
    scale = dh ** -0.5

    def body(q_ref, k_ref, v_ref, o_ref):
        p = _softmax_rows(q_ref[...], k_ref[...], scale)
        o_ref[...] = lax.dot_general(p.astype(BF), v_ref[...], _DN["nn"], preferred_element_type=F32).astype(BF)

    qblk = pl.BlockSpec((ts, dh), lambda bi, h, s: (bi * nsb + s, h))
    return pl.pallas_call(
        body, grid=(Bn, H, nsb),
        in_specs=[qblk, pl.BlockSpec((Mn, dh), lambda bi, h, s: (bi, h)), pl.BlockSpec((Mn, dh), lambda bi, h, s: (bi, H + h))],
        out_specs=qblk, out_shape=SDS((Bn * S, D), BF), compiler_params=_params("parallel", "parallel", "parallel"), name=name)(q, kv, kv)


def _attn_bwd(q, kv, datt, Bn, S, Mn, D, name):
    H = XA_HEADS
    dh = D // H
    ts = _attn_ts(S)
    nsb = S // ts
    scale = dh ** -0.5

    def body(q_ref, k_ref, v_ref, do_ref, dq_ref, dk_ref, dv_ref):
        q, k, v, do = q_ref[...], k_ref[...], v_ref[...], do_ref[...]
        p = _softmax_rows(q, k, scale)
        dp = lax.dot_general(do, v, _DN["nt"], preferred_element_type=F32)
        ds = (p * (dp - jnp.sum(dp * p, axis=-1, keepdims=True)) * scale).astype(BF)
        dq_ref[...] = lax.dot_general(ds, k, _DN["nn"], preferred_element_type=F32).astype(BF)

        @pl.when(pl.program_id(2) == 0)
        def _():
            dk_ref[...] = jnp.zeros_like(dk_ref)
            dv_ref[...] = jnp.zeros_like(dv_ref)

        dk_ref[...] += lax.dot_general(ds, q, _DN["tn"], preferred_element_type=F32)
        dv_ref[...] += lax.dot_general(p.astype(BF), do, _DN["tn"], preferred_element_type=F32)

    qblk = pl.BlockSpec((ts, dh), lambda bi, h, s: (bi * nsb + s, h))
    kblk = pl.BlockSpec((Mn, dh), lambda bi, h, s: (bi, h))
    return pl.pallas_call(
        body, grid=(Bn, H, nsb),
        in_specs=[qblk, kblk, pl.BlockSpec((Mn, dh), lambda bi, h, s: (bi, H + h)), qblk],
        out_specs=[qblk, kblk, kblk], out_shape=[SDS((Bn * S, D), BF), SDS((Bn * Mn, D), F32), SDS((Bn * Mn, D), F32)],
        compiler_params=_params("parallel", "parallel", "arbitrary"), name=name)(q, kv, kv, datt)


class _Sides:
    def __init__(self, by_key=None):
        self.by_key, self.landed = dict(by_key or {}), {}

    def run(self, key, fn, *args, **kw):
        side = self.by_key.get(key)
        if side is None:
            return fn(*args, **kw)
        out, self.landed[key] = fn(*args, side=side, **kw)
        return out

    def mm(self, key, *args, **kw):
        return self.run(key, _mm, *args, **kw)


def _layer_fwd(x, h, mem_n, W, V, l, dims, sides, next_g):
    Bn, S, Mn, D, C, F = dims
    n = f"l{l}_"
    proj = sides.mm("proj", h, W["w_in"], "nn", F32, n + "proj", bl=0)
    cv = _glu_conv_fwd(proj, V["conv_dw_w"][l], V["conv_dw_b"][l], Bn, S, C, n + "glu_conv")
    yc1 = _ln_silu_fwd(cv, V["conv_ln_g"][l], V["conv_ln_b"][l], n + "ln_silu")
    yc = sides.mm("conv_out", yc1, W["w_conv_out"], "nn", BF, n + "conv_out", bl=0)
    yp = _pool_fwd(proj, W["w_pool"], 0, Bn, S, C, D, n + "pool")
    merged = _merge_fwd(proj, yc, yp, V["pool_scale"][l], C, n + "merge")
    x1, hq = sides.run("out_proj", _mm_rms_fwd, merged, W["w_out"], x, V["xattn_norm_g"][l], n + "out_proj")
    q = sides.mm("q_proj", hq, W["w_q"], "nn", BF, n + "q_proj", bl=0)
    kv = _mm(mem_n, W["w_kv"], "nn", BF, n + "kv_proj", bl=0)
    att = _attn_fwd(q, kv, Bn, S, Mn, D, n + "attn")
    x2, hf = sides.run("o_proj", _mm_rms_fwd, att, W["w_o"], x1, V["ffn_norm_g"][l], n + "o_proj")
    up0 = sides.mm("up_proj", hf, W["w_up"], "nn", F32, n + "up_proj", bl=0)
    gact = _ffn_act_fwd(up0, V["ffn_dw_w"][l], Bn, S, F, n + "ffn_act")
    if next_g is not None:
        x3, h3 = sides.run("down_proj", _mm_rms_fwd, gact, W["w_down"], x2, next_g, n + "down_proj")
    else:
        x3, h3 = sides.mm("down_proj", gact, W["w_down"], "nn", F32, n + "down_proj", res=x2, bl=0), None
    return x3, h3, dict(x=x, h=h, proj=proj, cv=cv, yc1=yc1, yc=yc, yp=yp, merged=merged, x1=x1, hq=hq, q=q, kv=kv, att=att, x2=x2,
                        hf=hf, up0=up0, gact=gact)


def _layer_bwd_mlp(dx, dxb, sv, W, V, l, dims, sides):
    Bn, S, Mn, D, C, F = dims
    n = f"l{l}_b_"
    gw, sm = {}, {}
    dgact = sides.mm("d_gact", dxb, W["w_down"], "nt", BF, n + "d_gact", bl=0)
    gw["w_down"] = sides.mm("dw_down", sv["gact"], dxb, "tn", F32, n + "dw_down", twin=BF)
    dg0, dv0, dwg, dwv = sides.run("ffn_act_b", _ffn_act_bwd, sv["up0"], V["ffn_dw_w"][l], dgact, Bn, S, F, n + "ffn_act")
    sm["ffn_dw_w"] = jnp.concatenate([dwg, dwv], axis=1)
    dup0 = jnp.concatenate([dg0, dv0], axis=1)
    dx2, dx2b, sm["ffn_norm_g"] = _mm_rms_bwd(dup0, W["w_up"], sv["x2"], V["ffn_norm_g"][l], dx, n + "d_hf")
    gw["w_up"] = sides.mm("dw_up", sv["hf"], dup0, "tn", F32, n + "dw_up", twin=BF)
    return dx2, dx2b, gw, sm


def _layer_bwd_mix(dx2, dx2b, dmem_n, sv, mem_n, W, V, l, dims, sides):
    Bn, S, Mn, D, C, F = dims
    n = f"l{l}_b_"
    gw, sm = {}, {}
    datt = _mm(dx2b, W["w_o"], "nt", BF, n + "d_att", bl=0)
    gw["w_o"] = _mm(sv["att"], dx2b, "tn", F32, n + "dw_o", twin=BF)
    dq, dk, dv = _attn_bwd(sv["q"], sv["kv"], datt, Bn, S, Mn, D, n + "attn")
    dkv = jnp.concatenate([dk, dv], axis=1)
    gw["w_kv"] = _mm(mem_n, dkv, "tn", F32, n + "dw_kv", twin=BF)
    dmem_n = _mm(dkv, W["w_kv"], "nt", F32, n + "d_mem", res=dmem_n, bl=0)
    dx1, dx1b, sm["xattn_norm_g"] = _mm_rms_bwd(dq, W["w_q"], sv["x1"], V["xattn_norm_g"][l], dx2, n + "d_hq")
    gw["w_q"] = _mm(sv["hq"], dq, "tn", F32, n + "dw_q", twin=BF)
    dmerged = _mm(dx1b, W["w_out"], "nt", BF, n + "d_merged", bl=0)
    gw["w_out"] = _mm(sv["merged"], dx1b, "tn", F32, n + "dw_out", twin=BF)
    dgc, dgp, dyc, dyp, sm["pool_scale"] = sides.run("merge_b", _merge_bwd, sv["proj"], sv["yc"], sv["yp"], V["pool_scale"][l], dmerged, C, n + "merge")
    du, dwp = _pool_bwd(sv["proj"], W["w_pool"], dyp, 0, Bn, S, C, D, n + "pool")
    gw["w_pool"] = (dwp, dwp.astype(BF))
    dyc1 = _mm(dyc, W["w_conv_out"], "nt", F32, n + "d_yc1", bl=0)
    gw["w_conv_out"] = _mm(sv["yc1"], dyc, "tn", F32, n + "dw_conv_out", twin=BF)
    dcv, sm["conv_ln_g"], sm["conv_ln_b"] = _ln_silu_bwd(sv["cv"], V["conv_ln_g"][l], V["conv_ln_b"][l], dyc1, n + "ln_silu")
    da, dgl, sm["conv_dw_w"], sm["conv_dw_b"] = sides.run("glu_conv_b", _glu_conv_bwd, sv["proj"], V["conv_dw_w"][l], dcv, Bn, S, C, n + "glu_conv")
    dproj = jnp.concatenate([da, dgl, du, dgc, dgp], axis=1)
    dx, dxb, sm["mix_norm_g"] = _mm_rms_bwd(dproj, W["w_in"], sv["x"], V["mix_norm_g"][l], dx1, n + "d_h")
    gw["w_in"] = _mm(sv["h"], dproj, "tn", F32, n + "dw_in", twin=BF)
    return dx, dxb, dmem_n, gw, sm


BIG = (("w_in", "col"), ("w_conv_out", "col"), ("w_pool", "row"), ("w_out", "row"), ("w_q", "row"), ("w_kv", "col"),
       ("w_o", "row"), ("w_up", "col"), ("w_down", "row"))
FWD_CARRY = {"proj": ("w_in",), "conv_out": ("w_conv_out", "w_pool"), "out_proj": ("w_out", "w_q"), "q_proj": ("w_o",), "o_proj": ("w_kv",),
             "up_proj": ("w_up",), "down_proj": ("w_down",)}
EARLY = ("w_down", "w_up")
BWD_CARRY_EARLY = {"merge_b": ("w_down",), "glu_conv_b": ("w_up",)}
BWD_CARRY_LATE = {"ffn_act_b": ("w_in", "w_conv_out", "w_pool", "w_out", "w_q", "w_kv", "w_o")}


def _place():
    xi, yi, ci = lax.axis_index("x"), lax.axis_index("y"), lax.axis_index("c")
    return xi, yi, ci, 2 * xi + yi


def _chip_peer(xi, yi, ci, r):
    return (xi ^ (r >> 1), yi ^ (r & 1), ci)


def _full_shard(ref, kind, k, cs):
    if kind == "col":
        return ref.at[:, :, :, :, pl.ds(pl.multiple_of(k * cs, cs), cs)]
    return ref.at[:, :, k]


def _gather_weights(shards, kinds):
    n = len(shards)
    outs = []
    for s, kind in zip(shards, kinds):
        L, P, _, RH, CS = s.shape
        outs.append(SDS((L, P, 2, RH, CS * N_CHIPS) if kind == "col" else (L, P, N_CHIPS, 2, RH, CS), s.dtype))
    per = 7

    def body(*refs):
        srcs, fulls, (ssem, rsem) = refs[:n], refs[n:2 * n], refs[2 * n:]
        xi, yi, ci, j = _place()
        sib = (xi, yi, 1 - ci)

        def piece(i, k, c):
            kind, cs = kinds[i], shards[i].shape[-1]
            if kind == "col":
                return fulls[i].at[:, :, c, :, pl.ds(pl.multiple_of(k * cs, cs), cs)]
            return fulls[i].at[:, :, k, c]

        def copy(i, slot, src, dst, dev):
            return pltpu.make_async_remote_copy(src_ref=src, dst_ref=dst, send_sem=ssem.at[per * i + slot], recv_sem=rsem.at[per * i + slot],
                                                device_id=dev, device_id_type=MESH)

        own, first, passed = [], [], []
        for i in range(n):
            for r in (1, 2, 3):
                first.append(copy(i, r - 1, srcs[i].at[:, :, ci], piece(i, j, ci), _chip_peer(xi, yi, ci, r)))
                first[-1].start()
        for i in range(n):
            own.append(copy(i, 6, srcs[i], _full_shard(fulls[i], kinds[i], j, shards[i].shape[-1]), sib))
            own[-1].start()
        for i in range(n):
            for r in (1, 2, 3):
                got = piece(i, j ^ r, ci)
                copy(i, r - 1, got, got, sib).wait_recv()
                passed.append(copy(i, 2 + r, got, got, sib))
                passed[-1].start()
        for i in range(n):
            for r in (1, 2, 3):
                got = piece(i, j ^ r, 1 - ci)
                copy(i, 2 + r, got, got, sib).wait_recv()
        for cp in own:
            cp.wait()
        for cp in first + passed:
            cp.wait_send()

    return pl.pallas_call(
        body, in_specs=[ANY] * n, out_specs=[ANY] * n, out_shape=outs,
        scratch_shapes=[pltpu.SemaphoreType.DMA((per * n,)), pltpu.SemaphoreType.DMA((per * n,))], name="gather_weights")(*shards)


def _full_sds(s, kind):
    L, P, _, RH, CS = s.shape
    return SDS((L, P, 2, RH, CS * N_CHIPS) if kind == "col" else (L, P, N_CHIPS, 2, RH, CS), s.dtype)


def _gather_piece(full, kind, cs, k, c):
    if kind == "col":
        return full.at[:, :, c, :, pl.ds(pl.multiple_of(k * cs, cs), cs)]
    return full.at[:, :, k, c]


def _side_gather(shards, kinds):
    n = len(shards)

    def make(srcs, fulls, ssem, rsem):
        xi, yi, ci, j = _place()
        return [pltpu.make_async_remote_copy(
            src_ref=srcs[i].at[:, :, ci], dst_ref=_gather_piece(fulls[i], kinds[i], shards[i].shape[-1], j, ci), send_sem=ssem.at[3 * i + r - 1],
            recv_sem=rsem.at[3 * i + r - 1], device_id=_chip_peer(xi, yi, ci, r), device_id_type=MESH) for i in range(n) for r in (1, 2, 3)]

    return _Side(shards, [_full_sds(s, k) for s, k in zip(shards, kinds)], 3 * n, make)


def _gather_pass(fulls, shards, kinds, name):
    n = len(fulls)

    def body(*refs):
        srcs, outs, (ssem, rsem) = refs[n:2 * n], refs[2 * n:3 * n], refs[3 * n:]
        xi, yi, ci, j = _place()
        sib = (xi, yi, 1 - ci)
        cps = []
        for i in range(n):
            cs = shards[i].shape[-1]
            for r in (1, 2, 3):
                got = _gather_piece(outs[i], kinds[i], cs, j ^ r, ci)
                cps.append(pltpu.make_async_remote_copy(src_ref=got, dst_ref=got, send_sem=ssem.at[4 * i + r - 1], recv_sem=rsem.at[4 * i + r - 1],
                                                        device_id=sib, device_id_type=MESH))
            cps.append(pltpu.make_async_remote_copy(src_ref=srcs[i], dst_ref=_full_shard(outs[i], kinds[i], j, cs), send_sem=ssem.at[4 * i + 3],
                                                    recv_sem=rsem.at[4 * i + 3], device_id=sib, device_id_type=MESH))
        for cp in cps:
            cp.start()
        for cp in cps:
            cp.wait()

    return pl.pallas_call(
        body, in_specs=[ANY] * (2 * n), out_specs=[ANY] * n, out_shape=[SDS(f.shape, f.dtype) for f in fulls],
        input_output_aliases={i: i for i in range(n)},
        scratch_shapes=[pltpu.SemaphoreType.DMA((4 * n,)), pltpu.SemaphoreType.DMA((4 * n,))], name=name)(*fulls, *shards)


def _sibling_exchange(gviews, kinds, name):
    n = len(gviews)
    outs = [SDS(g.shape[:1] + g.shape[2:] if kind == "col" else g.shape[:2] + g.shape[3:], g.dtype) for g, kind in zip(gviews, kinds)]

    def body(*refs):
        gs, lands, (ssem, rsem) = refs[:n], refs[n:2 * n], refs[2 * n:]
        xi, yi, ci, _ = _place()
        cps = []
        for i in range(n):
            src = gs[i].at[:, 1 - ci] if kinds[i] == "col" else gs[i].at[:, :, 1 - ci]
            cps.append(pltpu.make_async_remote_copy(src_ref=src, dst_ref=lands[i], send_sem=ssem.at[i], recv_sem=rsem.at[i],
                                                    device_id=(xi, yi, 1 - ci), device_id_type=MESH))
            cps[-1].start()
        for cp in cps:
            cp.wait()

    return pl.pallas_call(body, in_specs=[ANY] * n, out_specs=[ANY] * n, out_shape=outs,
                          scratch_shapes=[pltpu.SemaphoreType.DMA((n,)), pltpu.SemaphoreType.DMA((n,))], name=name)(*gviews)


def _chip_sums(gs, lands, kinds, jc, name):
    n = len(gs)
    args, in_specs, out_specs, out_shape = [], [], [], []
    for g, land, kind in zip(gs, lands, kinds):
        if kind == "col":
            P, _, RH, C = g.shape
            CS = C // N_CHIPS
            in_specs += [pl.BlockSpec((P, None, RH, CS), lambda r, jc: (0, jc[1], 0, jc[0] ^ r)),
                         pl.BlockSpec((P, RH, CS), lambda r, jc: (0, 0, jc[0] ^ r))]
        else:
            P, _, _, RH, CS = g.shape
            in_specs += [pl.BlockSpec((P, None, None, RH, CS), lambda r, jc: (0, jc[0] ^ r, jc[1], 0, 0)),
                         pl.BlockSpec((P, None, RH, CS), lambda r, jc: (0, jc[0] ^ r, 0, 0))]
        args += [g, land]
        out_specs += [pl.BlockSpec((P, RH, CS), lambda r, jc: (0, 0, 0)), pl.BlockSpec((None, P, RH, CS), lambda r, jc: (r, 0, 0, 0))]
        out_shape += [SDS((P, RH, CS), F32), SDS((N_CHIPS, P, RH, CS), BF)]

    def body(jc_ref, *refs):
        ins, outs = refs[:2 * n], refs[2 * n:]
        for i in range(n):
            s = ins[2 * i][...] + ins[2 * i + 1][...].astype(F32)
            outs[2 * i + 1][...] = s.astype(BF)

            @pl.when(pl.program_id(0) == 0)
            def _():
                outs[2 * i][...] = s

    outs = _call(body, args, grid=(N_CHIPS,), in_specs=in_specs, out_specs=out_specs, out_shape=out_shape, semantics=("arbitrary",),
                 name=name, prefetch=(jc,))
    return outs[0::2], outs[1::2]


def _chip_exchange_copies(srcs, lands, ssem, rsem):
    xi, yi, ci, _ = _place()
    return [pltpu.make_async_remote_copy(src_ref=srcs[i].at[r], dst_ref=lands[i].at[r], send_sem=ssem.at[3 * i + r - 1],
                                         recv_sem=rsem.at[3 * i + r - 1], device_id=_chip_peer(xi, yi, ci, r), device_id_type=MESH)
            for i in range(len(srcs)) for r in (1, 2, 3)]


def _side_chip_exchange(pieces):
    return _Side(pieces, [SDS(p.shape, p.dtype) for p in pieces], 3 * len(pieces), _chip_exchange_copies)


FINAL_SUM_STEPS = 2


def _final_sums(owns, lands, jc, shards, l, L, name, side=None):
    n = len(owns)
    args, in_specs, out_specs, out_shape = [], [], [], []
    for own, land in zip(owns, lands):
        P, RH, CS = own.shape
        hr = RH // FINAL_SUM_STEPS
        in_specs += [pl.BlockSpec((P, hr, CS), lambda h, jc: (0, h, 0))]
        in_specs += [pl.BlockSpec((None, P, hr, CS), functools.partial(lambda r, h, jc: (r, 0, h, 0), r)) for r in (1, 2, 3)]
        args += [own, land, land, land]
        out_specs.append(pl.BlockSpec((None, P, None, hr, CS), lambda h, jc: (l, 0, jc[1], h, 0)))
        out_shape.append(SDS((L, P, 2, RH, CS), F32))
    aliases = None
    if shards is not None:
        aliases = {4 * n + i: i for i in range(n)}
        in_specs += [ANY] * n
        args += list(shards)

    def body(jc_ref, *refs):
        outs = refs[len(args):]
        for i in range(n):
            o, a, b, c = (refs[4 * i + t][...] for t in range(4))
            outs[i][...] = ((o + a.astype(F32)) + b.astype(F32)) + c.astype(F32)

    return _call(body, args, grid=(FINAL_SUM_STEPS,), in_specs=in_specs, out_specs=out_specs, out_shape=out_shape, semantics=("arbitrary",),
                 name=name, prefetch=(jc,), aliases=aliases, side=side)


def _halves_exchange(shards, l, name):
    n = len(shards)

    def body(*refs):
        outs, (ssem, rsem) = refs[n:2 * n], refs[2 * n:]
        xi, yi, ci, _ = _place()
        cps = []
        for i in range(n):
            mine = outs[i].at[l, :, ci]
            cps.append(pltpu.make_async_remote_copy(src_ref=mine, dst_ref=mine, send_sem=ssem.at[i], recv_sem=rsem.at[i],
                                                    device_id=(xi, yi, 1 - ci), device_id_type=MESH))
            cps[-1].start()
        for i in range(n):
            land = outs[i].at[l, :, 1 - ci]
            pltpu.make_async_remote_copy(src_ref=land, dst_ref=land, send_sem=ssem.at[i], recv_sem=rsem.at[i],
                                         device_id=(xi, yi, 1 - ci), device_id_type=MESH).wait_recv()
        for cp in cps:
            cp.wait_send()

    return pl.pallas_call(body, in_specs=[ANY] * n, out_specs=[ANY] * n, out_shape=[SDS(s.shape, s.dtype) for s in shards],
                          input_output_aliases={i: i for i in range(n)},
                          scratch_shapes=[pltpu.SemaphoreType.DMA((n,)), pltpu.SemaphoreType.DMA((n,))], name=name)(*shards)


def _reduce_small(part):
    NR, Wd = part.shape
    ND = 2 * N_CHIPS

    def body(p_ref, o_ref, land, ssem, rsem):
        xi, yi, ci, j = _place()
        me = 2 * j + ci
        land[me] = p_ref[...]
        cps = []
        for rr in range(1, ND):
            dev = (xi ^ (rr >> 2), yi ^ ((rr >> 1) & 1), ci ^ (rr & 1))
            cps.append(pltpu.make_async_remote_copy(src_ref=p_ref, dst_ref=land.at[me], send_sem=ssem.at[rr - 1], recv_sem=rsem.at[rr - 1],
                                                    device_id=dev, device_id_type=MESH))
            cps[-1].start()
        for rr in range(1, ND):
            got = land.at[me ^ rr]
            pltpu.make_async_remote_copy(src_ref=got, dst_ref=got, send_sem=ssem.at[rr - 1], recv_sem=rsem.at[rr - 1],
                                         device_id=(xi, yi, ci), device_id_type=MESH).wait_recv()
        acc = land[0]
        for d in range(1, ND):
            acc = acc + land[d]
        o_ref[...] = acc
        for cp in cps:
            cp.wait_send()

    vm = pl.BlockSpec(memory_space=pltpu.VMEM)
    return pl.pallas_call(body, in_specs=[vm], out_specs=vm, out_shape=SDS((NR, Wd), F32),
                          scratch_shapes=[pltpu.VMEM((ND, NR, Wd), F32), pltpu.SemaphoreType.DMA((ND - 1,)), pltpu.SemaphoreType.DMA((ND - 1,))],
                          name="small_grad_allreduce")(part)


def _adamw_update(w_ref, g_ref, m_ref, v_ref, d_ref, mo_ref, vo_ref):
    g = g_ref[...]
    m = ADAM_B1 * m_ref[...] + (1.0 - ADAM_B1) * g
    v = ADAM_B2 * v_ref[...] + (1.0 - ADAM_B2) * jnp.square(g)
    m_hat = m / (1.0 - ADAM_B1 ** ADAM_STEP)
    v_hat = v / (1.0 - ADAM_B2 ** ADAM_STEP)
    d_ref[...] = -ADAM_LR * (m_hat / (jnp.sqrt(v_hat) + ADAM_EPS) + ADAM_WD * w_ref[...])
    mo_ref[...] = m
    vo_ref[...] = v


ADAMW_STEPS = 8


def _adamw_layer(ws, gs, ms, vs, prev, l, name, side=None):
    n = len(ws)
    args, in_specs, out_specs, out_shape = [], [], [], []
    for w, g, m, v in zip(ws, gs, ms, vs):
        L, R, C = w.shape
        blk = pl.BlockSpec((None, R // ADAMW_STEPS, C), lambda i: (l, i, 0))
        in_specs += [blk] * 4
        args += [w, g, m, v]
        out_specs += [blk] * 3
        out_shape += [SDS((L, R, C), F32)] * 3
    aliases = None
    if prev is not None:
        aliases = {4 * n + i: i for i in range(3 * n)}
        in_specs += [ANY] * (3 * n)
        args += list(prev)

    def body(*refs):
        outs = refs[len(args):]
        for i in range(n):
            _adamw_update(*refs[4 * i:4 * i + 4], *outs[3 * i:3 * i + 3])

    return _call(body, args, grid=(ADAMW_STEPS,), in_specs=in_specs, out_specs=out_specs, out_shape=out_shape, semantics=("parallel",),
                 name=name, aliases=aliases, side=side)


def _adamw(w, g, m, v, name):
    shape = w.shape
    C = shape[-1]
    R = w.size // C
    tb = _tile(R, max(8, (1 << 18) // C), 8)
    body = functools.partial(_adamw_update)
    blk = pl.BlockSpec((tb, C), lambda i: (i, 0))
    outs = pl.pallas_call(body, grid=(R // tb,), in_specs=[blk] * 4, out_specs=[blk] * 3, out_shape=[SDS((R, C), F32)] * 3,
                          compiler_params=_params("parallel"), name=name)(*[t.reshape(R, C) for t in (w, g, m, v)])
    return [t.reshape(shape) for t in outs]


WEIGHTS = ("mix_norm_g", "w_in", "conv_dw_w", "conv_dw_b", "conv_ln_g", "conv_ln_b", "w_conv_out", "w_pool_grp", "pool_scale", "w_out",
           "xattn_norm_g", "mem_norm_g", "w_q", "w_kv", "w_o", "ffn_norm_g", "w_up", "ffn_dw_w", "w_down", "final_norm_g")
VECTORS = ("mix_norm_g", "conv_dw_b", "conv_ln_g", "conv_ln_b", "pool_scale", "xattn_norm_g", "mem_norm_g", "ffn_norm_g", "final_norm_g")


def _shard_view(t, kind):
    L, P, R, C = t.shape
    return t.reshape(L, P, 2, R // 2, C)


def _rows(t, width):
    return t.reshape(-1, width)


def _pack(parts):
    return jnp.concatenate([jnp.pad(p, ((0, (-p.shape[0]) % 8), (0, 0))) for p in parts], axis=0)


def kernel(x, mem, mix_norm_g, w_in, conv_dw_w, conv_dw_b, conv_ln_g, conv_ln_b, w_conv_out, w_pool_grp, pool_scale, w_out, xattn_norm_g, mem_norm_g, w_q, w_kv, w_o, ffn_norm_g, w_up, ffn_dw_w, w_down, final_norm_g, loss_target, m_mix_norm_g, m_w_in, m_conv_dw_w, m_conv_dw_b, m_conv_ln_g, m_conv_ln_b, m_w_conv_out, m_w_pool_grp, m_pool_scale, m_w_out, m_xattn_norm_g, m_mem_norm_g, m_w_q, m_w_kv, m_w_o, m_ffn_norm_g, m_w_up, m_ffn_dw_w, m_w_down, m_final_norm_g, v_mix_norm_g, v_w_in, v_conv_dw_w, v_conv_dw_b, v_conv_ln_g, v_conv_ln_b, v_w_conv_out, v_w_pool_grp, v_pool_scale, v_w_out, v_xattn_norm_g, v_mem_norm_g, v_w_q, v_w_kv, v_w_o, v_ffn_norm_g, v_w_up, v_ffn_dw_w, v_w_down, v_final_norm_g):
    w = dict(mix_norm_g=mix_norm_g, w_in=w_in, conv_dw_w=conv_dw_w, conv_dw_b=conv_dw_b, conv_ln_g=conv_ln_g, conv_ln_b=conv_ln_b,
             w_conv_out=w_conv_out, w_pool_grp=w_pool_grp, pool_scale=pool_scale, w_out=w_out, xattn_norm_g=xattn_norm_g,
             mem_norm_g=mem_norm_g, w_q=w_q, w_kv=w_kv, w_o=w_o, ffn_norm_g=ffn_norm_g, w_up=w_up, ffn_dw_w=ffn_dw_w, w_down=w_down,
             final_norm_g=final_norm_g)
    m = dict(zip(WEIGHTS, (m_mix_norm_g, m_w_in, m_conv_dw_w, m_conv_dw_b, m_conv_ln_g, m_conv_ln_b, m_w_conv_out, m_w_pool_grp, m_pool_scale,
                           m_w_out, m_xattn_norm_g, m_mem_norm_g, m_w_q, m_w_kv, m_w_o, m_ffn_norm_g, m_w_up, m_ffn_dw_w, m_w_down, m_final_norm_g)))
    v = dict(zip(WEIGHTS, (v_mix_norm_g, v_w_in, v_conv_dw_w, v_conv_dw_b, v_conv_ln_g, v_conv_ln_b, v_w_conv_out, v_w_pool_grp, v_pool_scale,
                           v_w_out, v_xattn_norm_g, v_mem_norm_g, v_w_q, v_w_kv, v_w_o, v_ffn_norm_g, v_w_up, v_ffn_dw_w, v_w_down, v_final_norm_g)))
    xi, yi, ci, j = _place()
    jc = jnp.stack([j, ci]).astype(jnp.int32)
    L = w_in.shape[0]
    G = len(POOL_WINDOWS)
    kinds = dict(BIG)

    def to_mat(name, t):
        if name == "w_pool":
            return jnp.swapaxes(t, 2, 3)
        return t[:, None]

    def from_mat(name, t):
        if name == "w_pool":
            return jnp.swapaxes(t, 2, 3)
        return t[:, 0]

    src = {name: w["w_pool_grp" if name == "w_pool" else name] for name, _ in BIG}

    KC, cs_c = conv_dw_w.shape[1], conv_dw_w.shape[2]
    KF, cs_f = ffn_dw_w.shape[1], ffn_dw_w.shape[2]
    taps = jnp.concatenate([conv_dw_w.reshape(L * KC, cs_c), ffn_dw_w.reshape(L * KF * (cs_f // cs_c), cs_c)], axis=0)
    n_taps = taps.shape[0]
    taps = jnp.pad(taps, ((0, (-n_taps) % 16), (0, 0)))
    names = [name for name, _ in BIG]
    mats = {name: to_mat(name, src[name]).astype(BF) for name in names}

    def layer_shards(l, subset):
        return [_shard_view(mats[name][l:l + 1], kinds[name]) for name in subset]

    def as_weights(subset, fulls):
        return {name: f.reshape(G if name == "w_pool" else 1, -1, f.shape[-1]) for name, f in zip(subset, fulls)}

    fulls = _gather_weights(layer_shards(0, names) + [_shard_view(taps[None, None], "row")], [kinds[name] for name in names] + ["row"])
    W = [as_weights(names, fulls[:-1])]
    taps_all = fulls[-1].reshape(N_CHIPS, -1, cs_c)[:, :n_taps]
    V = {name: w[name] for name in VECTORS}
    V["conv_dw_w"] = taps_all[:, :L * KC].reshape(N_CHIPS, L, KC, cs_c).transpose(1, 2, 0, 3).reshape(L, KC, N_CHIPS * cs_c)
    V["ffn_dw_w"] = taps_all[:, L * KC:].reshape(N_CHIPS, L, KF, cs_f).transpose(1, 2, 0, 3).reshape(L, KF, N_CHIPS * cs_f)

    Bn, S, D = x.shape
    Mn = mem.shape[1]
    dims = (Bn, S, Mn, D, conv_dw_b.shape[1], w_down.shape[1] * N_CHIPS)
    xt = x.reshape(Bn * S, D)
    memf = mem.reshape(Bn * Mn, D)
    mem_n = _rms_fwd(memf, V["mem_norm_g"], "mem_norm")
    saved = []
    ht = _rms_fwd(xt, V["mix_norm_g"][0], "l0_mix_norm")
    for l in range(L):
        sides = _Sides()
        if l + 1 < L:
            sides = _Sides({key: _side_gather(layer_shards(l + 1, subset), [kinds[name] for name in subset]) for key, subset in FWD_CARRY.items()})
        xt, ht, sv = _layer_fwd(xt, ht, mem_n, W[l], V, l, dims, sides, V["mix_norm_g"][l + 1] if l + 1 < L else None)
        saved.append(sv)
        if l + 1 < L:
            carried = [name for subset in FWD_CARRY.values() for name in subset]
            landed = [f for key in FWD_CARRY for f in sides.landed[key]]
            done = _gather_pass(landed, layer_shards(l + 1, carried), [kinds[name] for name in carried], f"gather_pass_l{l + 1}")
            W.append(as_weights(carried, done))
    loss, dx, dgf = _loss_bwd(xt, V["final_norm_g"], loss_target.reshape(Bn * S, D), "loss")
    loss = lax.psum(loss[0, 0], ("x", "y", "c"))

    late_names = [name for name in names if name not in EARLY]

    def chip_sums(gw, subset, l, tag):
        def view(g, name):
            g = g if g.ndim == 3 else g[None]
            P, R, C = g.shape
            return g.reshape(P, 2, R // 2, C) if kinds[name] == "col" else g.reshape(P, N_CHIPS, 2, R // (2 * N_CHIPS), C)

        gv = [view(gw[name][0], name) for name in subset]
        lands = _sibling_exchange([view(gw[name][1], name) for name in subset], [kinds[name] for name in subset],
                                  f"grad_sibling_exchange_{tag}_l{l}")
        own, pieces = _chip_sums(gv, lands, [kinds[name] for name in subset], jc, f"chip_sums_{tag}_l{l}")
        return dict(zip(subset, own)), dict(zip(subset, pieces))

    def carry(table, pieces):
        return _Sides({key: _side_chip_exchange([pieces[name] for name in subset]) for key, subset in table.items()})

    def landed(table, sides):
        return {name: land for key, subset in table.items() for name, land in zip(subset, sides.landed[key])}

    dxb, dmem_n = dx, None
    smalls, owns, got = [None] * L, [{} for _ in range(L)], [{} for _ in range(L)]
    late = None
    for l in reversed(range(L)):
        sides = carry(BWD_CARRY_LATE, late) if late is not None else _Sides()
        dx, dxb, gw, sm = _layer_bwd_mlp(dx, dxb, saved[l], W[l], V, l, dims, sides)
        if late is not None:
            got[l + 1].update(landed(BWD_CARRY_LATE, sides))
        own, early = chip_sums(gw, EARLY, l, "mlp")
        owns[l].update(own)
        sides = carry(BWD_CARRY_EARLY, early)
        dx, dxb, dmem_n, gw, sm2 = _layer_bwd_mix(dx, dxb, dmem_n, saved[l], mem_n, W[l], V, l, dims, sides)
        got[l].update(landed(BWD_CARRY_EARLY, sides))
        smalls[l] = {**sm, **sm2}
        own, late = chip_sums(gw, late_names, l, "mix")
        owns[l].update(own)
    grad_x = dx.reshape(Bn, S, D)
    _, _, dgm = _rms_bwd(memf, V["mem_norm_g"], dmem_n, None, "mem_norm_b")
    small = {k: jnp.stack([sm[k] for sm in smalls]) if k in ("conv_dw_w", "ffn_dw_w") else jnp.concatenate([sm[k] for sm in smalls], axis=0)
             for k in smalls[0]}
    small["mem_norm_g"] = dgm
    small["final_norm_g"] = dgf

    assert L >= 2
    keys = ["w_pool_grp" if name == "w_pool" else name for name in names]
    rows3 = lambda t: t.reshape(t.shape[0], -1, t.shape[-1])
    wmv = [[rows3(to_mat(name, d[key])) for name, key in zip(names, keys)] for d in (w, m, v)]
    gshards, updates = None, None
    for l in reversed(range(L)):
        side_a = _side_chip_exchange([late["w_in"]]) if l == L - 1 else None
        rest = [name for name in late_names if name != "w_in"]
        side_b = _side_chip_exchange([late[name] for name in rest]) if l == L - 1 else None
        if l == 0:
            got[0].update(tail_landed)
        ordered = lambda d: [d[name] for name in names]
        gshards = _final_sums(ordered(owns[l]), ordered(got[l]), jc, gshards, l, L, f"final_sums_l{l}", side=side_a)
        if side_a is not None:
            gshards, landed_a = gshards
            tail_landed = {"w_in": landed_a[0]}
        gshards = _halves_exchange(gshards, l, f"grad_halves_exchange_l{l}")
        updates = _adamw_layer(wmv[0], [rows3(t) for t in gshards], wmv[1], wmv[2], updates, l, f"adamw_l{l}", side=side_b)
        if side_b is not None:
            updates, landed_b = updates
            tail_landed.update(zip(rest, landed_b))
    grads, delta, new_m, new_v = {}, {}, {}, {}
    for i, (name, key) in enumerate(zip(names, keys)):
        Lg, P, _, RH, CS = gshards[i].shape
        grads[key] = from_mat(name, gshards[i].reshape(Lg, P, 2 * RH, CS))
        for d, t in zip((delta, new_m, new_v), updates[3 * i:3 * i + 3]):
            d[key] = from_mat(name, t.reshape(Lg, P, 2 * RH, CS))

    small_w = conv_dw_b.shape[1]
    order = VECTORS + ("conv_dw_w", "ffn_dw_w")
    parts = [_rows(small[name], small_w) for name in order]
    counts = [p.shape[0] for p in parts]
    summed = _reduce_small(_pack(parts))
    off = 0
    for name, cnt in zip(order, counts):
        t = summed[off:off + cnt]
        off += cnt + (-cnt) % 8
        if name in VECTORS:
            grads[name] = t.reshape(w[name].shape)
        else:
            full = t.reshape(small[name].shape)
            cs = w[name].shape[2]
            grads[name] = lax.dynamic_slice_in_dim(full, j * cs, cs, axis=2)

    vec =[_pack([_rows(d[name], small_w) for name in VECTORS]) for d in (w, grads, m, v)]
    outs = _adamw(*vec, "adamw_vectors")
    off = 0
    for name in VECTORS:
        cnt = w[name].size // small_w
        for d, t in zip((delta, new_m, new_v), outs):
            d[name] = t[off:off + cnt].reshape(w[name].shape)
        off += cnt + (-cnt) % 8
    for name in ("conv_dw_w", "ffn_dw_w"):
        delta[name], new_m[name], new_v[name] = _adamw(w[name], grads[name], m[name], v[name], "adamw_" + name)

    return (loss, grad_x, *[grads[k] for k in WEIGHTS], *[delta[k] for k in WEIGHTS], *[new_m[k] for k in WEIGHTS], *[new_v[k] for k in WEIGHTS])
```

```python
import functools

import jax
import jax.numpy as jnp
from jax import lax
from jax.experimental import pallas as pl
from jax.experimental.pallas import tpu as pltpu

F32 = jnp.float32
BF = jnp.bfloat16
SDS = jax.ShapeDtypeStruct
MESH = pl.DeviceIdType.MESH
ANY = pl.BlockSpec(memory_space=pl.ANY)

EPS = 1e-6
XA_HEADS = 4
POOL_WINDOWS = (2, 4, 8, 16)
N_CHIPS = 4
ADAM_LR, ADAM_B1, ADAM_B2, ADAM_EPS, ADAM_WD, ADAM_STEP = 0.001, 0.9, 0.999, 1e-08, 0.01, 10

LANES = 128
ROW_BLOCK = 512
VMEM_LIMIT = 56 * 1024 * 1024


def _params(*sem):
    return pltpu.CompilerParams(dimension_semantics=sem if sem else None, vmem_limit_bytes=VMEM_LIMIT)


def _tile(n, cap, mult=LANES):
    if n <= cap:
        return n
    for t in range(cap - cap % mult, 0, -mult):
        if n % t == 0:
            return t
    return n


_DN = {"nn": (((1,), (0,)), ((), ())), "nt": (((1,), (1,)), ((), ())), "tn": (((0,), (0,)), ((), ()))}


class _Side:
    def __init__(self, ins, outs, n, make):
        self.ins, self.outs, self.n, self.make = list(ins), list(outs), n, make


def _call(body, args, *, grid, in_specs, out_specs, out_shape, semantics, name, scratch_shapes=(), side=None, prefetch=(), aliases=None):
    n_pf = len(prefetch)
    aliases = {n_pf + i: o for i, o in (aliases or {}).items()}
    n_in, n_out, n_scr = len(args), len(out_shape), len(scratch_shapes)
    n_si, n_so = (len(side.ins), len(side.outs)) if side is not None else (0, 0)

    def carrying(*refs):
        pf, refs = refs[:n_pf], refs[n_pf:]
        ins, s_in = refs[:n_in], refs[n_in:n_in + n_si]
        outs, s_out = refs[n_in + n_si:n_in + n_si + n_out], refs[n_in + n_si + n_out:n_in + n_si + n_out + n_so]
        scr = refs[n_in + n_si + n_out + n_so:]
        if side is None:
            return body(*pf, *ins, *outs, *scr)
        copies = side.make(s_in, s_out, scr[n_scr], scr[n_scr + 1])
        ids = [pl.program_id(d) for d in range(len(grid))]
        first, last = ids[0] == 0, ids[0] == grid[0] - 1
        for d in range(1, len(grid)):
            first, last = first & (ids[d] == 0), last & (ids[d] == grid[d] - 1)

        @pl.when(first)
        def _():
            for cp in copies:
                cp.start()

        body(*pf, *ins, *outs, *scr[:n_scr])

        @pl.when(last)
        def _():
            for cp in copies:
                cp.wait()

    sems = [pltpu.SemaphoreType.DMA((side.n,)), pltpu.SemaphoreType.DMA((side.n,))] if side is not None else []
    outs = pl.pallas_call(
        carrying, grid_spec=pltpu.PrefetchScalarGridSpec(
            num_scalar_prefetch=n_pf, grid=grid, in_specs=list(in_specs) + [ANY] * n_si, out_specs=list(out_specs) + [ANY] * n_so,
            scratch_shapes=list(scratch_shapes) + sems),
        out_shape=list(out_shape) + (side.outs if side is not None else []), input_output_aliases=aliases,
        compiler_params=_params(*(semantics if side is None else ["arbitrary"] * len(grid))), name=name)(
            *prefetch, *args, *(side.ins if side is not None else []))
    return list(outs) if side is None else (list(outs[:n_out]), list(outs[n_out:]))


MM_VMEM_BUDGET = 40 * 1024 * 1024
MM_STEP_MACS = 2200 * 1024 * 1024
MXU_WIDTH = 256
MM_STEP_COST_BYTES = 1 << 20


def _divisors(n):
    return [t for t in range(LANES, n + 1, LANES) if n % t == 0] or [n]


def _mm_tiles(M, N, K, a_bytes, b_bytes, o_bytes):
    best = None
    for tk in _divisors(K):
        for tm in _divisors(M):
            for tn in _divisors(N):
                nk = K // tk
                foot = 2 * (tm * tk * a_bytes + tk * tn * b_bytes + tm * tn * o_bytes) + (tm * tn * 4 if nk > 1 else 0)
                if foot > MM_VMEM_BUDGET or tm * tn * tk > MM_STEP_MACS or tn < min(N, MXU_WIDTH) or tm < min(M, MXU_WIDTH):
                    continue
                steps = (M // tm) * (N // tn) * nk
                traffic = M * K * a_bytes * (N // tn if nk > 1 else 1) + K * N * b_bytes * (M // tm) + M * N * o_bytes
                exposed = tm * tk * a_bytes + tk * tn * b_bytes + tm * tn * o_bytes
                cost = traffic + exposed + steps * MM_STEP_COST_BYTES + (nk - 1) * M * N * 8
                if best is None or cost < best[0]:
                    best = (cost, tm, tn, tk)
    assert best is not None, (M, N, K)
    return best[1:]


def _mm(a, b, dims, out_dtype, name, res=None, bl=None, side=None, twin=None):
    bs = b.shape[1:] if bl is not None else b.shape
    if dims == "nn":
        (M, K), (K2, N) = a.shape, bs
    elif dims == "nt":
        (M, K), (N, K2) = a.shape, bs
    else:
        (K, M), (K2, N) = a.shape, bs
    assert K == K2, (name, a.shape, b.shape)
    tm, tn, tk = _mm_tiles(M, N, K, a.dtype.itemsize, b.dtype.itemsize, jnp.dtype(out_dtype).itemsize
                           + (res.dtype.itemsize if res is not None else 0) + (jnp.dtype(twin).itemsize if twin is not None else 0))
    nk = K // tk
    lead = (None,) if bl is not None else ()
    pre = (lambda *ix: (bl,) + ix) if bl is not None else (lambda *ix: ix)
    if dims == "tn":
        a_spec = pl.BlockSpec((tk, tm), lambda i, j, k: (k, i))
    else:
        a_spec = pl.BlockSpec((tm, tk), lambda i, j, k: (i, k))
    if dims == "nt":
        b_spec = pl.BlockSpec(lead + (tn, tk), lambda i, j, k: pre(j, k))
    else:
        b_spec = pl.BlockSpec(lead + (tk, tn), lambda i, j, k: pre(k, j))
    o_spec = pl.BlockSpec((tm, tn), lambda i, j, k: (i, j))
    in_specs, args = [a_spec, b_spec], [a, b]
    if res is not None:
        in_specs.append(o_spec)
        args.append(res)
    n_main = len(args)
    n_out = 1 if twin is None else 2

    def body(*refs):
        a_ref, b_ref = refs[0], refs[1]
        r_ref = refs[2] if res is not None else None
        o_ref = refs[n_main]
        p = lax.dot_general(a_ref[...].astype(BF), b_ref[...].astype(BF), _DN[dims], preferred_element_type=F32)

        def finish(t):
            if r_ref is not None:
                t = t + r_ref[...]
            o_ref[...] = t.astype(out_dtype)
            if twin is not None:
                refs[n_main + 1][...] = t.astype(twin)

        if nk == 1:
            finish(p)
        else:
            acc = refs[n_main + n_out]
            k = pl.program_id(2)

            @pl.when(k == 0)
            def _():
                acc[...] = p

            @pl.when(k > 0)
            def _():
                acc[...] += p

            @pl.when(k == nk - 1)
            def _():
                finish(acc[...])

    got = _call(body, args, grid=(M // tm, N // tn, nk), in_specs=in_specs, out_specs=[o_spec] * n_out,
                out_shape=[SDS((M, N), out_dtype)] + ([SDS((M, N), twin)] if twin is not None else []),
                scratch_shapes=[pltpu.VMEM((tm, tn), F32)] if nk > 1 else [], semantics=("parallel", "parallel", "arbitrary"),
                name=name, side=side)
    outs, landed = (got, None) if side is None else got
    out = outs[0] if twin is None else (outs[0], outs[1])
    return out if side is None else (out, landed)


def _rms(x, g):
    return x * lax.rsqrt(jnp.mean(x * x, axis=-1, keepdims=True) + EPS) * g


def _ln_silu(x, g, b):
    mu = jnp.mean(x, axis=-1, keepdims=True)
    xc = x - mu
    var = jnp.mean(xc * xc, axis=-1, keepdims=True)
    return jax.nn.silu(xc * lax.rsqrt(var + EPS) * g + b)


def _merge(gc, gp, yc, yp, ps):
    return jax.nn.sigmoid(gc) * yc + jax.nn.sigmoid(gp) * (yp * ps)


def _gated(gate, val):
    return jax.nn.gelu(gate) * val


def _rms_fwd(x, g, name):
    T, D = x.shape
    tb = _tile(T, ROW_BLOCK, 8)

    def body(x_ref, g_ref, o_ref):
        o_ref[...] = _rms(x_ref[...], g_ref[...]).astype(BF)

    row = pl.BlockSpec((tb, D), lambda i: (i, 0))
    return pl.pallas_call(body, grid=(T // tb,), in_specs=[row, pl.BlockSpec((1, D), lambda i: (0, 0))], out_specs=row,
                          out_shape=SDS((T, D), BF), compiler_params=_params("parallel"), name=name)(x, g.reshape(1, D))


def _rms_bwd(x, g, dh, dres, name):
    T, D = x.shape
    tb = _tile(T, ROW_BLOCK, 8)

    def body(*refs):
        if dres is not None:
            x_ref, g_ref, dh_ref, dres_ref, dx_ref, dxb_ref, dg_ref = refs
        else:
            x_ref, g_ref, dh_ref, dx_ref, dxb_ref, dg_ref = refs
        _, vjp = jax.vjp(_rms, x_ref[...], g_ref[...])
        dx, dg = vjp(dh_ref[...].astype(F32))
        if dres is not None:
            dx = dx + dres_ref[...]
        dx_ref[...] = dx
        dxb_ref[...] = dx.astype(BF)

        @pl.when(pl.program_id(0) == 0)
        def _():
            dg_ref[...] = jnp.zeros_like(dg_ref)

        dg_ref[...] += dg

    row = pl.BlockSpec((tb, D), lambda i: (i, 0))
    vec = pl.BlockSpec((1, D), lambda i: (0, 0))
    ins = [x, g.reshape(1, D), dh] + ([dres] if dres is not None else [])
    return pl.pallas_call(
        body, grid=(T // tb,), in_specs=[row, vec, row] + ([row] if dres is not None else []), out_specs=[row, row, vec],
        out_shape=[SDS((T, D), F32), SDS((T, D), BF), SDS((1, D), F32)], compiler_params=_params("arbitrary"), name=name)(*ins)


def _row_tile(M, K, N, per_row_bytes):
    fixed = K * N * 2
    fit = [t for t in _divisors(M) if fixed + 2 * t * per_row_bytes <= MM_VMEM_BUDGET and t * K * N <= MM_STEP_MACS]
    return max(fit) if fit else min(_divisors(M))


def _mm_rms_fwd(a, b, res, g, name, side=None):
    M, K = a.shape
    N = b.shape[2]
    tm = _row_tile(M, K, N, K * 2 + N * (4 + 4 + 2))

    def body(a_ref, b_ref, r_ref, g_ref, x_ref, h_ref):
        x = r_ref[...] + lax.dot_general(a_ref[...], b_ref[...], _DN["nn"], preferred_element_type=F32)
        x_ref[...] = x
        h_ref[...] = _rms(x, g_ref[...]).astype(BF)

    row = pl.BlockSpec((tm, N), lambda i: (i, 0))
    return _call(body, (a, b, res, g.reshape(1, N)), grid=(M // tm,),
                 in_specs=[pl.BlockSpec((tm, K), lambda i: (i, 0)), pl.BlockSpec((None, K, N), lambda i: (0, 0, 0), pipeline_mode=pl.Buffered(1)), row,
                           pl.BlockSpec((1, N), lambda i: (0, 0))],
                 out_specs=[row, row], out_shape=[SDS((M, N), F32), SDS((M, N), BF)], semantics=("parallel",), name=name, side=side)


def _mm_rms_bwd(a, b, x, g, dres, name):
    M, K = a.shape
    N = b.shape[1]
    tm = _row_tile(M, K, N, K * 2 + N * (4 + 4 + 4 + 2))

    def body(a_ref, b_ref, x_ref, g_ref, r_ref, dx_ref, dxb_ref, dg_ref):
        dh = lax.dot_general(a_ref[...], b_ref[...], _DN["nt"], preferred_element_type=F32)
        _, vjp = jax.vjp(_rms, x_ref[...], g_ref[...])
        dx, dg = vjp(dh)
        dx = dx + r_ref[...]
        dx_ref[...] = dx
        dxb_ref[...] = dx.astype(BF)

        @pl.when(pl.program_id(0) == 0)
        def _():
            dg_ref[...] = jnp.zeros_like(dg_ref)

        dg_ref[...] += dg

    row = pl.BlockSpec((tm, N), lambda i: (i, 0))
    vec = pl.BlockSpec((1, N), lambda i: (0, 0))
    return pl.pallas_call(
        body, grid=(M // tm,),
        in_specs=[pl.BlockSpec((tm, K), lambda i: (i, 0)), pl.BlockSpec((None, N, K), lambda i: (0, 0, 0), pipeline_mode=pl.Buffered(1)), row, vec, row],
        out_specs=[row, row, vec], out_shape=[SDS((M, N), F32), SDS((M, N), BF), SDS((1, N), F32)],
        compiler_params=_params("arbitrary"), name=name)(a, b, x, g.reshape(1, N), dres)


def _loss_bwd(x, g, target, name):
    T, D = x.shape
    tb = _tile(T, ROW_BLOCK, 8)
    nb = T // tb

    def body(x_ref, g_ref, t_ref, loss_ref, dx_ref, dg_ref, acc):
        i = pl.program_id(0)
        y, vjp = jax.vjp(_rms, x_ref[...], g_ref[...])
        err = y - t_ref[...]
        dx, dg = vjp(err * (1.0 / D))
        dx_ref[...] = dx

        @pl.when(i == 0)
        def _():
            dg_ref[...] = jnp.zeros_like(dg_ref)
            acc[...] = jnp.zeros_like(acc)

        dg_ref[...] += dg
        acc[...] += jnp.sum(err * err, axis=0, keepdims=True)

        @pl.when(i == nb - 1)
        def _():
            loss_ref[...] = jnp.full(loss_ref.shape, (0.5 / D) * jnp.sum(acc[...]), F32)

    row = pl.BlockSpec((tb, D), lambda i: (i, 0))
    vec = pl.BlockSpec((1, D), lambda i: (0, 0))
    return pl.pallas_call(
        body, grid=(nb,), in_specs=[row, vec, row], out_specs=[pl.BlockSpec((1, LANES), lambda i: (0, 0)), row, vec],
        out_shape=[SDS((1, LANES), F32), SDS((T, D), F32), SDS((1, D), F32)], scratch_shapes=[pltpu.VMEM((1, D), F32)],
        compiler_params=_params("arbitrary"), name=name)(x, g.reshape(1, D), target)


def _ln_silu_fwd(cv, g, b, name):
    T, C = cv.shape
    tb = _tile(T, ROW_BLOCK, 8)

    def body(x_ref, g_ref, b_ref, o_ref):
        o_ref[...] = _ln_silu(x_ref[...], g_ref[...], b_ref[...]).astype(BF)

    row = pl.BlockSpec((tb, C), lambda i: (i, 0))
    vec = pl.BlockSpec((1, C), lambda i: (0, 0))
    return pl.pallas_call(body, grid=(T // tb,), in_specs=[row, vec, vec], out_specs=row, out_shape=SDS((T, C), BF),
                          compiler_params=_params("parallel"), name=name)(cv, g.reshape(1, C), b.reshape(1, C))


def _ln_silu_bwd(cv, g, b, dy, name):
    T, C = cv.shape
    tb = _tile(T, ROW_BLOCK, 8)

    def body(x_ref, g_ref, b_ref, dy_ref, dx_ref, dg_ref, db_ref):
        _, vjp = jax.vjp(_ln_silu, x_ref[...], g_ref[...], b_ref[...])
        dx, dg, db = vjp(dy_ref[...].astype(F32))
        dx_ref[...] = dx

        @pl.when(pl.program_id(0) == 0)
        def _():
            dg_ref[...] = jnp.zeros_like(dg_ref)
            db_ref[...] = jnp.zeros_like(db_ref)

        dg_ref[...] += dg
        db_ref[...] += db

    row = pl.BlockSpec((tb, C), lambda i: (i, 0))
    vec = pl.BlockSpec((1, C), lambda i: (0, 0))
    return pl.pallas_call(
        body, grid=(T // tb,), in_specs=[row, vec, vec, row], out_specs=[row, vec, vec],
        out_shape=[SDS((T, C), F32), SDS((1, C), F32), SDS((1, C), F32)], compiler_params=_params("arbitrary"),
        name=name)(cv, g.reshape(1, C), b.reshape(1, C), dy)


def _merge_fwd(proj, yc, yp, ps, C, name):
    T, D = yc.shape
    tb = _tile(T, ROW_BLOCK, 8)
    nj = D // C

    def body(gc_ref, gp_ref, yc_ref, yp_ref, ps_ref, o_ref):
        o_ref[...] = _merge(gc_ref[...], gp_ref[...], yc_ref[...].astype(F32), yp_ref[...].astype(F32), ps_ref[...]).astype(BF)

    blk = pl.BlockSpec((tb, C), lambda i, j: (i, j))
    return pl.pallas_call(
        body, grid=(T // tb, nj),
        in_specs=[pl.BlockSpec((tb, C), lambda i, j: (i, 3 + j)), pl.BlockSpec((tb, C), lambda i, j: (i, 3 + nj + j)), blk, blk,
                  pl.BlockSpec((1, C), lambda i, j: (0, j))],
        out_specs=blk, out_shape=SDS((T, D), BF), compiler_params=_params("parallel", "parallel"), name=name)(proj, proj, yc, yp, ps.reshape(1, D))


def _merge_bwd(proj, yc, yp, ps, dm, C, name, side=None):
    T, D = yc.shape
    tb = _tile(T, ROW_BLOCK, 8)
    nj = D // C

    def body(gc_ref, gp_ref, yc_ref, yp_ref, ps_ref, dm_ref, dgc_ref, dgp_ref, dyc_ref, dyp_ref, dps_ref):
        _, vjp = jax.vjp(_merge, gc_ref[...], gp_ref[...], yc_ref[...].astype(F32), yp_ref[...].astype(F32), ps_ref[...])
        dgc, dgp, dyc, dyp, dps = vjp(dm_ref[...].astype(F32))
        dgc_ref[...] = dgc.astype(BF)
        dgp_ref[...] = dgp.astype(BF)
        dyc_ref[...] = dyc.astype(BF)
        dyp_ref[...] = dyp.astype(BF)

        @pl.when(pl.program_id(1) == 0)
        def _():
            dps_ref[...] = jnp.zeros_like(dps_ref)

        dps_ref[...] += dps

    blk = pl.BlockSpec((tb, C), lambda j, i: (i, j))
    vec = pl.BlockSpec((1, C), lambda j, i: (0, j))
    return _call(
        body, (proj, proj, yc, yp, ps.reshape(1, D), dm), grid=(nj, T // tb),
        in_specs=[pl.BlockSpec((tb, C), lambda j, i: (i, 3 + j)), pl.BlockSpec((tb, C), lambda j, i: (i, 3 + nj + j)), blk, blk, vec, blk],
        out_specs=[blk, blk, blk, blk, vec], out_shape=[SDS((T, D), BF)] * 4 + [SDS((1, D), F32)],
        semantics=("parallel", "arbitrary"), name=name, side=side)


def _shd(v, s, rows):
    if s == 0:
        return v
    return jnp.where(rows >= s, pltpu.roll(v, s, 0), 0.0)


def _shu(v, s, rows):
    if s == 0:
        return v
    n = v.shape[0]
    return jnp.where(rows < n - s, pltpu.roll(v, n - s, 0), 0.0)


def _glu_conv_fwd(proj, w, b, Bn, S, C, name):
    K = w.shape[0]
    sl = min(LANES, C)
    ns = C // sl

    def body(a_ref, gl_ref, w_ref, b_ref, o_ref):
        y0 = a_ref[...] * jax.nn.sigmoid(gl_ref[...])
        rows = lax.broadcasted_iota(jnp.int32, y0.shape, 0)
        acc = jnp.zeros_like(y0) + b_ref[...]
        for k in range(K):
            acc = acc + w_ref[k:k + 1, :] * _shd(y0, K - 1 - k, rows)
        o_ref[...] = acc

    return pl.pallas_call(
        body, grid=(Bn, ns),
        in_specs=[pl.BlockSpec((S, sl), lambda bi, j: (bi, j)), pl.BlockSpec((S, sl), lambda bi, j: (bi, ns + j)),
                  pl.BlockSpec((K, sl), lambda bi, j: (0, j)), pl.BlockSpec((1, sl), lambda bi, j: (0, j))],
        out_specs=pl.BlockSpec((S, sl), lambda bi, j: (bi, j)), out_shape=SDS((Bn * S, C), F32),
        compiler_params=_params("parallel", "parallel"), name=name)(proj, proj, w, b.reshape(1, C))


def _glu_conv_bwd(proj, w, dcv, Bn, S, C, name, side=None):
    K = w.shape[0]
    sl = min(LANES, C)
    ns = C // sl

    def body(a_ref, gl_ref, w_ref, d_ref, da_ref, dgl_ref, dw_ref, db_ref):
        a = a_ref[...]
        sg = jax.nn.sigmoid(gl_ref[...])
        y0 = a * sg
        d = d_ref[...]
        rows = lax.broadcasted_iota(jnp.int32, y0.shape, 0)

        @pl.when(pl.program_id(1) == 0)
        def _():
            dw_ref[...] = jnp.zeros_like(dw_ref)
            db_ref[...] = jnp.zeros_like(db_ref)

        dy0 = jnp.zeros_like(y0)
        for k in range(K):
            s = K - 1 - k
            dw_ref[k:k + 1, :] += jnp.sum(d * _shd(y0, s, rows), axis=0, keepdims=True)
            dy0 = dy0 + w_ref[k:k + 1, :] * _shu(d, s, rows)
        db_ref[...] += jnp.sum(d, axis=0, keepdims=True)
        da_ref[...] = (dy0 * sg).astype(BF)
        dgl_ref[...] = (dy0 * a * sg * (1.0 - sg)).astype(BF)

    blk = pl.BlockSpec((S, sl), lambda j, bi: (bi, j))
    return _call(
        body, (proj, proj, w, dcv), grid=(ns, Bn),
        in_specs=[blk, pl.BlockSpec((S, sl), lambda j, bi: (bi, ns + j)), pl.BlockSpec((K, sl), lambda j, bi: (0, j)), blk],
        out_specs=[blk, blk, pl.BlockSpec((K, sl), lambda j, bi: (0, j)), pl.BlockSpec((1, sl), lambda j, bi: (0, j))],
        out_shape=[SDS((Bn * S, C), BF), SDS((Bn * S, C), BF), SDS((K, C), F32), SDS((1, C), F32)],
        semantics=("parallel", "arbitrary"), name=name, side=side)


def _pool_z(u, g, rows):
    s2 = u + _shd(u, 1, rows)
    s4 = s2 + _shd(s2, 2, rows)
    s8 = s4 + _shd(s4, 4, rows)
    s16 = s8 + _shd(s8, 8, rows)
    sw = jnp.where(g == 0, s2, jnp.where(g == 1, s4, jnp.where(g == 2, s8, s16)))
    cnt = jnp.minimum(rows + 1, POOL_WINDOWS[0] << g).astype(F32)
    return sw / cnt - u, cnt


def _pool_fwd(proj, wpt, l, Bn, S, C, D, name):
    G = len(POOL_WINDOWS)
    gd, go = C // G, D // G

    def body(u_ref, w_ref, o_ref):
        g = pl.program_id(1)
        u = u_ref[...]
        rows = lax.broadcasted_iota(jnp.int32, u.shape, 0)
        zp, _ = _pool_z(u, g, rows)
        o_ref[...] = lax.dot_general(zp.astype(BF), w_ref[...], _DN["nt"], preferred_element_type=F32).astype(BF)

    return pl.pallas_call(
        body, grid=(Bn, G),
        in_specs=[pl.BlockSpec((S, gd), lambda bi, g: (bi, 2 * G + g)), pl.BlockSpec((None, go, gd), lambda bi, g: (l * G + g, 0, 0))],
        out_specs=pl.BlockSpec((S, go), lambda bi, g: (bi, g)), out_shape=SDS((Bn * S, D), BF),
        compiler_params=_params("parallel", "parallel"), name=name)(proj, wpt)


def _pool_bwd(proj, wpt, dyp, l, Bn, S, C, D, name):
    G = len(POOL_WINDOWS)
    gd, go = C // G, D // G

    def body(u_ref, w_ref, d_ref, du_ref, dw_ref):
        g = pl.program_id(0)
        u = u_ref[...]
        rows = lax.broadcasted_iota(jnp.int32, u.shape, 0)
        zp, cnt = _pool_z(u, g, rows)
        d = d_ref[...]
        dzp = lax.dot_general(d, w_ref[...], _DN["nn"], preferred_element_type=F32)

        @pl.when(pl.program_id(1) == 0)
        def _():
            dw_ref[...] = jnp.zeros_like(dw_ref)

        dw_ref[...] += lax.dot_general(d, zp.astype(BF), _DN["tn"], preferred_element_type=F32)
        dsw = dzp / cnt
        zero = jnp.zeros_like(dsw)
        d16 = jnp.where(g == 3, dsw, zero)
        d8 = jnp.where(g == 2, dsw, zero) + d16 + _shu(d16, 8, rows)
        d4 = jnp.where(g == 1, dsw, zero) + d8 + _shu(d8, 4, rows)
        d2 = jnp.where(g == 0, dsw, zero) + d4 + _shu(d4, 2, rows)
        d1 = d2 + _shu(d2, 1, rows)
        du_ref[...] = (d1 - dzp).astype(BF)

    return pl.pallas_call(
        body, grid=(G, Bn),
        in_specs=[pl.BlockSpec((S, gd), lambda g, bi: (bi, 2 * G + g)), pl.BlockSpec((None, go, gd), lambda g, bi: (l * G + g, 0, 0)),
                  pl.BlockSpec((S, go), lambda g, bi: (bi, g))],
        out_specs=[pl.BlockSpec((S, gd), lambda g, bi: (bi, g)), pl.BlockSpec((None, go, gd), lambda g, bi: (g, 0, 0))],
        out_shape=[SDS((Bn * S, C), BF), SDS((G, go, gd), F32)],
        compiler_params=_params("parallel", "arbitrary"), name=name)(proj, wpt, dyp)


def _ffn_conv(u, w_ref, rows):
    K = w_ref.shape[0]
    acc = w_ref[K - 1:K, :] * u
    for k in range(K - 1):
        acc = acc + w_ref[k:k + 1, :] * _shd(u, K - 1 - k, rows)
    return acc


def _ffn_cb(F):
    return _tile(F, 256)


def _ffn_act_fwd(up0, w, Bn, S, F, name):
    cb = _ffn_cb(F)
    nj = F // cb

    def body(g_ref, v_ref, wg_ref, wv_ref, o_ref):
        rows = lax.broadcasted_iota(jnp.int32, g_ref.shape, 0)
        o_ref[...] = _gated(_ffn_conv(g_ref[...], wg_ref, rows), _ffn_conv(v_ref[...], wv_ref, rows)).astype(BF)

    K = w.shape[0]
    return pl.pallas_call(
        body, grid=(Bn, nj),
        in_specs=[pl.BlockSpec((S, cb), lambda bi, j: (bi, j)), pl.BlockSpec((S, cb), lambda bi, j: (bi, nj + j)),
                  pl.BlockSpec((K, cb), lambda bi, j: (0, j)), pl.BlockSpec((K, cb), lambda bi, j: (0, nj + j))],
        out_specs=pl.BlockSpec((S, cb), lambda bi, j: (bi, j)), out_shape=SDS((Bn * S, F), BF),
        compiler_params=_params("parallel", "parallel"), name=name)(up0, up0, w, w)


SUBLANES = 8
FFN_HALO = SUBLANES
FFN_ROWS = 64
GELU_C0, GELU_C1 = 0.7978845608028654, 0.044715


def _gelu_and_grad(x):
    x2 = x * x
    t = jnp.tanh(GELU_C0 * (x + GELU_C1 * (x2 * x)))
    cdf = 0.5 * (1.0 + t)
    return x * cdf, cdf + (0.5 * GELU_C0) * x * (1.0 - t * t) * (1.0 + (3.0 * GELU_C1) * x2)


def _ffn_act_bwd(up0, w, dg, Bn, S, F, name, side=None):
    cb = min(LANES, F)
    nj = F // cb
    K = w.shape[0]
    rc = FFN_ROWS if S % FFN_ROWS == 0 else S
    win = rc + 2 * FFN_HALO
    assert K - 1 <= FFN_HALO and rc % SUBLANES == 0

    def body(g_ref, v_ref, wg_ref, wv_ref, d_ref, dgo_ref, dvo_ref, dwg_ref, dwv_ref, gp, vp, dp):
        for pad, src in ((gp, g_ref), (vp, v_ref), (dp, d_ref)):
            pad[0:FFN_HALO, :] = jnp.zeros((FFN_HALO, cb), F32)
            pad[FFN_HALO + S:, :] = jnp.zeros((FFN_HALO, cb), F32)
            pad[FFN_HALO:FFN_HALO + S, :] = src[...].astype(F32)
        wg = [wg_ref[k:k + 1, :] for k in range(K)]
        wv = [wv_ref[k:k + 1, :] for k in range(K)]

        def taps(u):
            return [pltpu.roll(u, K - 1 - k, 0) for k in range(K - 1)] + [u]

        def conv(us, ws):
            acc = ws[K - 1] * us[K - 1]
            for k in range(K - 1):
                acc = acc + ws[k] * us[k]
            return acc

        def conv_t(dc, ws):
            acc = ws[K - 1] * dc
            for k in range(K - 1):
                acc = acc + ws[k] * pltpu.roll(dc, win - (K - 1 - k), 0)
            return acc

        def fold(t):
            acc = t[FFN_HALO:FFN_HALO + SUBLANES]
            for i in range(1, rc // SUBLANES):
                acc = acc + t[FFN_HALO + SUBLANES * i:FFN_HALO + SUBLANES * (i + 1)]
            return acc

        def chunk(c, sums):
            r0 = pl.multiple_of(c * rc, SUBLANES)
            gs, vs, d = taps(gp[pl.ds(r0, win), :]), taps(vp[pl.ds(r0, win), :]), dp[pl.ds(r0, win), :]
            ge, dge = _gelu_and_grad(conv(gs, wg))
            dgc = d * conv(vs, wv) * dge
            dvc = d * ge
            dgo_ref[pl.ds(r0, rc), :] = conv_t(dgc, wg)[FFN_HALO:FFN_HALO + rc].astype(BF)
            dvo_ref[pl.ds(r0, rc), :] = conv_t(dvc, wv)[FFN_HALO:FFN_HALO + rc].astype(BF)
            new = [fold(dc * u) for us, dc in ((gs, dgc), (vs, dvc)) for u in us]
            return tuple(a + b for a, b in zip(sums, new))

        sums = lax.fori_loop(0, S // rc, chunk, tuple(jnp.zeros((SUBLANES, cb), F32) for _ in range(2 * K)))

        @pl.when(pl.program_id(1) == 0)
        def _():
            dwg_ref[...] = jnp.zeros_like(dwg_ref)
            dwv_ref[...] = jnp.zeros_like(dwv_ref)

        for k in range(K):
            dwg_ref[k:k + 1, :] += jnp.sum(sums[k], axis=0, keepdims=True)
            dwv_ref[k:k + 1, :] += jnp.sum(sums[K + k], axis=0, keepdims=True)

    blk = pl.BlockSpec((S, cb), lambda j, bi: (bi, j))
    wblk = pl.BlockSpec((K, cb), lambda j, bi: (0, j))
    return _call(
        body, (up0, up0, w, w, dg), grid=(nj, Bn),
        in_specs=[blk, pl.BlockSpec((S, cb), lambda j, bi: (bi, nj + j)), wblk, pl.BlockSpec((K, cb), lambda j, bi: (0, nj + j)), blk],
        out_specs=[blk, blk, wblk, wblk],
        out_shape=[SDS((Bn * S, F), BF), SDS((Bn * S, F), BF), SDS((K, F), F32), SDS((K, F), F32)],
        scratch_shapes=[pltpu.VMEM((S + 2 * FFN_HALO, cb), F32)] * 3, semantics=("parallel", "arbitrary"), name=name, side=side)


def _softmax_rows(q, k, scale):
    sc = lax.dot_general(q, k, _DN["nt"], preferred_element_type=F32) * scale
    e = jnp.exp(sc - jnp.max(sc, axis=-1, keepdims=True))
    return e / jnp.sum(e, axis=-1, keepdims=True)


def _attn_ts(S):
    return _tile(S, 1024, 8)


def _attn_fwd(q, kv, Bn, S, Mn, D, name):
    H = XA_HEADS
    dh = D // H
    ts = _attn_ts(S)
    nsb = S // ts
    scale = dh ** -0.5

    def body(q_ref, k_ref, v_ref, o_ref):
        p = _softmax_rows(q_ref[...], k_ref[...], scale)
        o_ref[...] = lax.dot_general(p.astype(BF), v_ref[...], _DN["nn"], preferred_element_type=F32).astype(BF)

    qblk = pl.BlockSpec((ts, dh), lambda bi, h, s: (bi * nsb + s, h))
    return pl.pallas_call(
        body, grid=(Bn, H, nsb),
        in_specs=[qblk, pl.BlockSpec((Mn, dh), lambda bi, h, s: (bi, h)), pl.BlockSpec((Mn, dh), lambda bi, h, s: (bi, H + h))],
        out_specs=qblk, out_shape=SDS((Bn * S, D), BF), compiler_params=_params("parallel", "parallel", "parallel"), name=name)(q, kv, kv)


def _attn_bwd(q, kv, datt, Bn, S, Mn, D, name):
    H = XA_HEADS
    dh = D // H
    ts = _attn_ts(S)
    nsb = S // ts
    scale = dh ** -0.5

    def body(q_ref, k_ref, v_ref, do_ref, dq_ref, dk_ref, dv_ref):
        q, k, v, do = q_ref[...], k_ref[...], v_ref[...], do_ref[...]
        p = _softmax_rows(q, k, scale)
        dp = lax.dot_general(do, v, _DN["nt"], preferred_element_type=F32)
        ds = (p * (dp - jnp.sum(dp * p, axis=-1, keepdims=True)) * scale).astype(BF)
        dq_ref[...] = lax.dot_general(ds, k, _DN["nn"], preferred_element_type=F32).astype(BF)

        @pl.when(pl.program_id(2) == 0)
        def _():
            dk_ref[...] = jnp.zeros_like(dk_ref)
            dv_ref[...] = jnp.zeros_like(dv_ref)

        dk_ref[...] += lax.dot_general(ds, q, _DN["tn"], preferred_element_type=F32)
        dv_ref[...] += lax.dot_general(p.astype(BF), do, _DN["tn"], preferred_element_type=F32)

    qblk = pl.BlockSpec((ts, dh), lambda bi, h, s: (bi * nsb + s, h))
    kblk = pl.BlockSpec((Mn, dh), lambda bi, h, s: (bi, h))
    return pl.pallas_call(
        body, grid=(Bn, H, nsb),
        in_specs=[qblk, kblk, pl.BlockSpec((Mn, dh), lambda bi, h, s: (bi, H + h)), qblk],
        out_specs=[qblk, kblk, kblk], out_shape=[SDS((Bn * S, D), BF), SDS((Bn * Mn, D), F32), SDS((Bn * Mn, D), F32)],
        compiler_params=_params("parallel", "parallel", "arbitrary"), name=name)(q, kv, kv, datt)


class _Sides:
    def __init__(self, by_key=None):
        self.by_key, self.landed = dict(by_key or {}), {}

    def run(self, key, fn, *args, **kw):
        side = self.by_key.get(key)
        if side is None:
            return fn(*args, **kw)
        out, self.landed[key] = fn(*args, side=side, **kw)
        return out

    def mm(self, key, *args, **kw):
        return self.run(key, _mm, *args, **kw)


def _layer_fwd(x, h, mem_n, W, V, l, dims, sides, next_g):
    Bn, S, Mn, D, C, F = dims
    n = f"l{l}_"
    proj = sides.mm("proj", h, W["w_in"], "nn", F32, n + "proj", bl=0)
    cv = _glu_conv_fwd(proj, V["conv_dw_w"][l], V["conv_dw_b"][l], Bn, S, C, n + "glu_conv")
    yc1 = _ln_silu_fwd(cv, V["conv_ln_g"][l], V["conv_ln_b"][l], n + "ln_silu")
    yc = sides.mm("conv_out", yc1, W["w_conv_out"], "nn", BF, n + "conv_out", bl=0)
    yp = _pool_fwd(proj, W["w_pool"], 0, Bn, S, C, D, n + "pool")
    merged = _merge_fwd(proj, yc, yp, V["pool_scale"][l], C, n + "merge")
    x1, hq = sides.run("out_proj", _mm_rms_fwd, merged, W["w_out"], x, V["xattn_norm_g"][l], n + "out_proj")
    q = sides.mm("q_proj", hq, W["w_q"], "nn", BF, n + "q_proj", bl=0)
    kv = _mm(mem_n, W["w_kv"], "nn", BF, n + "kv_proj", bl=0)
    att = _attn_fwd(q, kv, Bn, S, Mn, D, n + "attn")
    x2, hf = sides.run("o_proj", _mm_rms_fwd, att, W["w_o"], x1, V["ffn_norm_g"][l], n + "o_proj")
    up0 = sides.mm("up_proj", hf, W["w_up"], "nn", F32, n + "up_proj", bl=0)
    gact = _ffn_act_fwd(up0, V["ffn_dw_w"][l], Bn, S, F, n + "ffn_act")
    if next_g is not None:
        x3, h3 = sides.run("down_proj", _mm_rms_fwd, gact, W["w_down"], x2, next_g, n + "down_proj")
    else:
        x3, h3 = sides.mm("down_proj", gact, W["w_down"], "nn", F32, n + "down_proj", res=x2, bl=0), None
    return x3, h3, dict(x=x, h=h, proj=proj, cv=cv, yc1=yc1, yc=yc, yp=yp, merged=merged, x1=x1, hq=hq, q=q, kv=kv, att=att, x2=x2,
                        hf=hf, up0=up0, gact=gact)


def _layer_bwd_mlp(dx, dxb, sv, W, V, l, dims, sides):
    Bn, S, Mn, D, C, F = dims
    n = f"l{l}_b_"
    gw, sm = {}, {}
    dgact = sides.mm("d_gact", dxb, W["w_down"], "nt", BF, n + "d_gact", bl=0)
    gw["w_down"] = sides.mm("dw_down", sv["gact"], dxb, "tn", F32, n + "dw_down", twin=BF)
    dg0, dv0, dwg, dwv = sides.run("ffn_act_b", _ffn_act_bwd, sv["up0"], V["ffn_dw_w"][l], dgact, Bn, S, F, n + "ffn_act")
    sm["ffn_dw_w"] = jnp.concatenate([dwg, dwv], axis=1)
    dup0 = jnp.concatenate([dg0, dv0], axis=1)
    dx2, dx2b, sm["ffn_norm_g"] = _mm_rms_bwd(dup0, W["w_up"], sv["x2"], V["ffn_norm_g"][l], dx, n + "d_hf")
    gw["w_up"] = sides.mm("dw_up", sv["hf"], dup0, "tn", F32, n + "dw_up", twin=BF)
    return dx2, dx2b, gw, sm


def _layer_bwd_mix(dx2, dx2b, dmem_n, sv, mem_n, W, V, l, dims, sides):
    Bn, S, Mn, D, C, F = dims
    n = f"l{l}_b_"
    gw, sm = {}, {}
    datt = _mm(dx2b, W["w_o"], "nt", BF, n + "d_att", bl=0)
    gw["w_o"] = _mm(sv["att"], dx2b, "tn", F32, n + "dw_o", twin=BF)
    dq, dk, dv = _attn_bwd(sv["q"], sv["kv"], datt, Bn, S, Mn, D, n + "attn")
    dkv = jnp.concatenate([dk, dv], axis=1)
    gw["w_kv"] = _mm(mem_n, dkv, "tn", F32, n + "dw_kv", twin=BF)
    dmem_n = _mm(dkv, W["w_kv"], "nt", F32, n + "d_mem", res=dmem_n, bl=0)
    dx1, dx1b, sm["xattn_norm_g"] = _mm_rms_bwd(dq, W["w_q"], sv["x1"], V["xattn_norm_g"][l], dx2, n + "d_hq")
    gw["w_q"] = _mm(sv["hq"], dq, "tn", F32, n + "dw_q", twin=BF)
    dmerged = _mm(dx1b, W["w_out"], "nt", BF, n + "d_merged", bl=0)
    gw["w_out"] = _mm(sv["merged"], dx1b, "tn", F32, n + "dw_out", twin=BF)
    dgc, dgp, dyc, dyp, sm["pool_scale"] = sides.run("merge_b", _merge_bwd, sv["proj"], sv["yc"], sv["yp"], V["pool_scale"][l], dmerged, C, n + "merge")
    du, dwp = _pool_bwd(sv["proj"], W["w_pool"], dyp, 0, Bn, S, C, D, n + "pool")
    gw["w_pool"] = (dwp, dwp.astype(BF))
    dyc1 = _mm(dyc, W["w_conv_out"], "nt", F32, n + "d_yc1", bl=0)
    gw["w_conv_out"] = _mm(sv["yc1"], dyc, "tn", F32, n + "dw_conv_out", twin=BF)
    dcv, sm["conv_ln_g"], sm["conv_ln_b"] = _ln_silu_bwd(sv["cv"], V["conv_ln_g"][l], V["conv_ln_b"][l], dyc1, n + "ln_silu")
    da, dgl, sm["conv_dw_w"], sm["conv_dw_b"] = sides.run("glu_conv_b", _glu_conv_bwd, sv["proj"], V["conv_dw_w"][l], dcv, Bn, S, C, n + "glu_conv")
    dproj = jnp.concatenate([da, dgl, du, dgc, dgp], axis=1)
    dx, dxb, sm["mix_norm_g"] = _mm_rms_bwd(dproj, W["w_in"], sv["x"], V["mix_norm_g"][l], dx1, n + "d_h")
    gw["w_in"] = _mm(sv["h"], dproj, "tn", F32, n + "dw_in", twin=BF)
    return dx, dxb, dmem_n, gw, sm


BIG = (("w_in", "col"), ("w_conv_out", "col"), ("w_pool", "row"), ("w_out", "row"), ("w_q", "row"), ("w_kv", "col"),
       ("w_o", "row"), ("w_up", "col"), ("w_down", "row"))
FWD_CARRY = {"proj": ("w_in",), "conv_out": ("w_conv_out", "w_pool"), "out_proj": ("w_out", "w_q"), "q_proj": ("w_o",), "o_proj": ("w_kv",),
             "up_proj": ("w_up",), "down_proj": ("w_down",)}
EARLY = ("w_down", "w_up")
BWD_CARRY_EARLY = {"merge_b": ("w_down",), "glu_conv_b": ("w_up",)}
BWD_CARRY_LATE = {"ffn_act_b": ("w_in", "w_conv_out", "w_pool", "w_out", "w_q", "w_kv", "w_o")}


def _place():
    xi, yi, ci = lax.axis_index("x"), lax.axis_index("y"), lax.axis_index("c")
    return xi, yi, ci, 2 * xi + yi


def _chip_peer(xi, yi, ci, r):
    return (xi ^ (r >> 1), yi ^ (r & 1), ci)


def _full_shard(ref, kind, k, cs):
    if kind == "col":
        return ref.at[:, :, :, :, pl.ds(pl.multiple_of(k * cs, cs), cs)]
    return ref.at[:, :, k]


def _gather_weights(shards, kinds):
    n = len(shards)
    outs = []
    for s, kind in zip(shards, kinds):
        L, P, _, RH, CS = s.shape
        outs.append(SDS((L, P, 2, RH, CS * N_CHIPS) if kind == "col" else (L, P, N_CHIPS, 2, RH, CS), s.dtype))
    per = 7

    def body(*refs):
        srcs, fulls, (ssem, rsem) = refs[:n], refs[n:2 * n], refs[2 * n:]
        xi, yi, ci, j = _place()
        sib = (xi, yi, 1 - ci)

        def piece(i, k, c):
            kind, cs = kinds[i], shards[i].shape[-1]
            if kind == "col":
                return fulls[i].at[:, :, c, :, pl.ds(pl.multiple_of(k * cs, cs), cs)]
            return fulls[i].at[:, :, k, c]

        def copy(i, slot, src, dst, dev):
            return pltpu.make_async_remote_copy(src_ref=src, dst_ref=dst, send_sem=ssem.at[per * i + slot], recv_sem=rsem.at[per * i + slot],
                                                device_id=dev, device_id_type=MESH)

        own, first, passed = [], [], []
        for i in range(n):
            for r in (1, 2, 3):
                first.append(copy(i, r - 1, srcs[i].at[:, :, ci], piece(i, j, ci), _chip_peer(xi, yi, ci, r)))
                first[-1].start()
        for i in range(n):
            own.append(copy(i, 6, srcs[i], _full_shard(fulls[i], kinds[i], j, shards[i].shape[-1]), sib))
            own[-1].start()
        for i in range(n):
            for r in (1, 2, 3):
                got = piece(i, j ^ r, ci)
                copy(i, r - 1, got, got, sib).wait_recv()
                passed.append(copy(i, 2 + r, got, got, sib))
                passed[-1].start()
        for i in range(n):
            for r in (1, 2, 3):
                got = piece(i, j ^ r, 1 - ci)
                copy(i, 2 + r, got, got, sib).wait_recv()
        for cp in own:
            cp.wait()
        for cp in first + passed:
            cp.wait_send()

    return pl.pallas_call(
        body, in_specs=[ANY] * n, out_specs=[ANY] * n, out_shape=outs,
        scratch_shapes=[pltpu.SemaphoreType.DMA((per * n,)), pltpu.SemaphoreType.DMA((per * n,))], name="gather_weights")(*shards)


def _full_sds(s, kind):
    L, P, _, RH, CS = s.shape
    return SDS((L, P, 2, RH, CS * N_CHIPS) if kind == "col" else (L, P, N_CHIPS, 2, RH, CS), s.dtype)


def _gather_piece(full, kind, cs, k, c):
    if kind == "col":
        return full.at[:, :, c, :, pl.ds(pl.multiple_of(k * cs, cs), cs)]
    return full.at[:, :, k, c]


def _side_gather(shards, kinds):
    n = len(shards)

    def make(srcs, fulls, ssem, rsem):
        xi, yi, ci, j = _place()
        return [pltpu.make_async_remote_copy(
            src_ref=srcs[i].at[:, :, ci], dst_ref=_gather_piece(fulls[i], kinds[i], shards[i].shape[-1], j, ci), send_sem=ssem.at[3 * i + r - 1],
            recv_sem=rsem.at[3 * i + r - 1], device_id=_chip_peer(xi, yi, ci, r), device_id_type=MESH) for i in range(n) for r in (1, 2, 3)]

    return _Side(shards, [_full_sds(s, k) for s, k in zip(shards, kinds)], 3 * n, make)


def _gather_pass(fulls, shards, kinds, name):
    n = len(fulls)

    def body(*refs):
        srcs, outs, (ssem, rsem) = refs[n:2 * n], refs[2 * n:3 * n], refs[3 * n:]
        xi, yi, ci, j = _place()
        sib = (xi, yi, 1 - ci)
        cps = []
        for i in range(n):
            cs = shards[i].shape[-1]
            for r in (1, 2, 3):
                got = _gather_piece(outs[i], kinds[i], cs, j ^ r, ci)
                cps.append(pltpu.make_async_remote_copy(src_ref=got, dst_ref=got, send_sem=ssem.at[4 * i + r - 1], recv_sem=rsem.at[4 * i + r - 1],
                                                        device_id=sib, device_id_type=MESH))
            cps.append(pltpu.make_async_remote_copy(src_ref=srcs[i], dst_ref=_full_shard(outs[i], kinds[i], j, cs), send_sem=ssem.at[4 * i + 3],
                                                    recv_sem=rsem.at[4 * i + 3], device_id=sib, device_id_type=MESH))
        for cp in cps:
            cp.start()
        for cp in cps:
            cp.wait()

    return pl.pallas_call(
        body, in_specs=[ANY] * (2 * n), out_specs=[ANY] * n, out_shape=[SDS(f.shape, f.dtype) for f in fulls],
        input_output_aliases={i: i for i in range(n)},
        scratch_shapes=[pltpu.SemaphoreType.DMA((4 * n,)), pltpu.SemaphoreType.DMA((4 * n,))], name=name)(*fulls, *shards)


def _sibling_exchange(gviews, kinds, name):
    n = len(gviews)
    outs = [SDS(g.shape[:1] + g.shape[2:] if kind == "col" else g.shape[:2] + g.shape[3:], g.dtype) for g, kind in zip(gviews, kinds)]

    def body(*refs):
        gs, lands, (ssem, rsem) = refs[:n], refs[n:2 * n], refs[2 * n:]
        xi, yi, ci, _ = _place()
        cps = []
        for i in range(n):
            src = gs[i].at[:, 1 - ci] if kinds[i] == "col" else gs[i].at[:, :, 1 - ci]
            cps.append(pltpu.make_async_remote_copy(src_ref=src, dst_ref=lands[i], send_sem=ssem.at[i], recv_sem=rsem.at[i],
                                                    device_id=(xi, yi, 1 - ci), device_id_type=MESH))
            cps[-1].start()
        for cp in cps:
            cp.wait()

    return pl.pallas_call(body, in_specs=[ANY] * n, out_specs=[ANY] * n, out_shape=outs,
                          scratch_shapes=[pltpu.SemaphoreType.DMA((n,)), pltpu.SemaphoreType.DMA((n,))], name=name)(*gviews)


def _chip_sums(gs, lands, kinds, jc, name):
    n = len(gs)
    args, in_specs, out_specs, out_shape = [], [], [], []
    for g, land, kind in zip(gs, lands, kinds):
        if kind == "col":
            P, _, RH, C = g.shape
            CS = C // N_CHIPS
            in_specs += [pl.BlockSpec((P, None, RH, CS), lambda r, jc: (0, jc[1], 0, jc[0] ^ r)),
                         pl.BlockSpec((P, RH, CS), lambda r, jc: (0, 0, jc[0] ^ r))]
        else:
            P, _, _, RH, CS = g.shape
            in_specs += [pl.BlockSpec((P, None, None, RH, CS), lambda r, jc: (0, jc[0] ^ r, jc[1], 0, 0)),
                         pl.BlockSpec((P, None, RH, CS), lambda r, jc: (0, jc[0] ^ r, 0, 0))]
        args += [g, land]
        out_specs += [pl.BlockSpec((P, RH, CS), lambda r, jc: (0, 0, 0)), pl.BlockSpec((None, P, RH, CS), lambda r, jc: (r, 0, 0, 0))]
        out_shape += [SDS((P, RH, CS), F32), SDS((N_CHIPS, P, RH, CS), BF)]

    def body(jc_ref, *refs):
        ins, outs = refs[:2 * n], refs[2 * n:]
        for i in range(n):
            s = ins[2 * i][...] + ins[2 * i + 1][...].astype(F32)
            outs[2 * i + 1][...] = s.astype(BF)

            @pl.when(pl.program_id(0) == 0)
            def _():
                outs[2 * i][...] = s

    outs = _call(body, args, grid=(N_CHIPS,), in_specs=in_specs, out_specs=out_specs, out_shape=out_shape, semantics=("arbitrary",),
                 name=name, prefetch=(jc,))
    return outs[0::2], outs[1::2]


def _chip_exchange_copies(srcs, lands, ssem, rsem):
    xi, yi, ci, _ = _place()
    return [pltpu.make_async_remote_copy(src_ref=srcs[i].at[r], dst_ref=lands[i].at[r], send_sem=ssem.at[3 * i + r - 1],
                                         recv_sem=rsem.at[3 * i + r - 1], device_id=_chip_peer(xi, yi, ci, r), device_id_type=MESH)
            for i in range(len(srcs)) for r in (1, 2, 3)]


def _side_chip_exchange(pieces):
    return _Side(pieces, [SDS(p.shape, p.dtype) for p in pieces], 3 * len(pieces), _chip_exchange_copies)


FINAL_SUM_STEPS = 2


def _final_sums(owns, lands, jc, shards, l, L, name, side=None):
    n = len(owns)
    args, in_specs, out_specs, out_shape = [], [], [], []
    for own, land in zip(owns, lands):
        P, RH, CS = own.shape
        hr = RH // FINAL_SUM_STEPS
        in_specs += [pl.BlockSpec((P, hr, CS), lambda h, jc: (0, h, 0))]
        in_specs += [pl.BlockSpec((None, P, hr, CS), functools.partial(lambda r, h, jc: (r, 0, h, 0), r)) for r in (1, 2, 3)]
        args += [own, land, land, land]
        out_specs.append(pl.BlockSpec((None, P, None, hr, CS), lambda h, jc: (l, 0, jc[1], h, 0)))
        out_shape.append(SDS((L, P, 2, RH, CS), F32))
    aliases = None
    if shards is not None:
        aliases = {4 * n + i: i for i in range(n)}
        in_specs += [ANY] * n
        args += list(shards)

    def body(jc_ref, *refs):
        outs = refs[len(args):]
        for i in range(n):
            o, a, b, c = (refs[4 * i + t][...] for t in range(4))
            outs[i][...] = ((o + a.astype(F32)) + b.astype(F32)) + c.astype(F32)

    return _call(body, args, grid=(FINAL_SUM_STEPS,), in_specs=in_specs, out_specs=out_specs, out_shape=out_shape, semantics=("arbitrary",),
                 name=name, prefetch=(jc,), aliases=aliases, side=side)


def _halves_exchange(shards, l, name):
    n = len(shards)

    def body(*refs):
        outs, (ssem, rsem) = refs[n:2 * n], refs[2 * n:]
        xi, yi, ci, _ = _place()
        cps = []
        for i in range(n):
            mine = outs[i].at[l, :, ci]
            cps.append(pltpu.make_async_remote_copy(src_ref=mine, dst_ref=mine, send_sem=ssem.at[i], recv_sem=rsem.at[i],
                                                    device_id=(xi, yi, 1 - ci), device_id_type=MESH))
            cps[-1].start()
        for i in range(n):
            land = outs[i].at[l, :, 1 - ci]
            pltpu.make_async_remote_copy(src_ref=land, dst_ref=land, send_sem=ssem.at[i], recv_sem=rsem.at[i],
                                         device_id=(xi, yi, 1 - ci), device_id_type=MESH).wait_recv()
        for cp in cps:
            cp.wait_send()

    return pl.pallas_call(body, in_specs=[ANY] * n, out_specs=[ANY] * n, out_shape=[SDS(s.shape, s.dtype) for s in shards],
                          input_output_aliases={i: i for i in range(n)},
                          scratch_shapes=[pltpu.SemaphoreType.DMA((n,)), pltpu.SemaphoreType.DMA((n,))], name=name)(*shards)


def _reduce_small(part, pieces):
    NR, Wd = part.shape
    ND = 2 * N_CHIPS
    n = len(pieces)

    def body(p_ref, *refs):
        srcs, o_ref, lands, (land, ssem, rsem, xs, xr) = refs[:n], refs[n], refs[n + 1:2 * n + 1], refs[2 * n + 1:]
        exchange = _chip_exchange_copies(srcs, lands, xs, xr)
        for cp in exchange:
            cp.start()
        xi, yi, ci, j = _place()
        me = 2 * j + ci
        land[me] = p_ref[...]
        cps = []
        for rr in range(1, ND):
            dev = (xi ^ (rr >> 2), yi ^ ((rr >> 1) & 1), ci ^ (rr & 1))
            cps.append(pltpu.make_async_remote_copy(src_ref=p_ref, dst_ref=land.at[me], send_sem=ssem.at[rr - 1], recv_sem=rsem.at[rr - 1],
                                                    device_id=dev, device_id_type=MESH))
            cps[-1].start()
        for rr in range(1, ND):
            got = land.at[me ^ rr]
            pltpu.make_async_remote_copy(src_ref=got, dst_ref=got, send_sem=ssem.at[rr - 1], recv_sem=rsem.at[rr - 1],
                                         device_id=(xi, yi, ci), device_id_type=MESH).wait_recv()
        acc = land[0]
        for d in range(1, ND):
            acc = acc + land[d]
        o_ref[...] = acc
        for cp in cps:
            cp.wait_send()
        for cp in exchange:
            cp.wait()

    vm = pl.BlockSpec(memory_space=pltpu.VMEM)
    outs = pl.pallas_call(
        body, in_specs=[vm] + [ANY] * n, out_specs=[vm] + [ANY] * n, out_shape=[SDS((NR, Wd), F32)] + [SDS(p.shape, p.dtype) for p in pieces],
        scratch_shapes=[pltpu.VMEM((ND, NR, Wd), F32), pltpu.SemaphoreType.DMA((ND - 1,)), pltpu.SemaphoreType.DMA((ND - 1,)),
                        pltpu.SemaphoreType.DMA((3 * n,)), pltpu.SemaphoreType.DMA((3 * n,))],
        name="small_grad_allreduce")(part, *pieces)
    return outs[0], list(outs[1:])


def _adamw_update(w_ref, g_ref, m_ref, v_ref, d_ref, mo_ref, vo_ref):
    g = g_ref[...]
    m = ADAM_B1 * m_ref[...] + (1.0 - ADAM_B1) * g
    v = ADAM_B2 * v_ref[...] + (1.0 - ADAM_B2) * jnp.square(g)
    m_hat = m / (1.0 - ADAM_B1 ** ADAM_STEP)
    v_hat = v / (1.0 - ADAM_B2 ** ADAM_STEP)
    d_ref[...] = -ADAM_LR * (m_hat / (jnp.sqrt(v_hat) + ADAM_EPS) + ADAM_WD * w_ref[...])
    mo_ref[...] = m
    vo_ref[...] = v


ADAMW_STEPS = 8


def _adamw_layer(ws, gs, ms, vs, prev, l, name, side=None):
    n = len(ws)
    args, in_specs, out_specs, out_shape = [], [], [], []
    for w, g, m, v in zip(ws, gs, ms, vs):
        L, R, C = w.shape
        blk = pl.BlockSpec((None, R // ADAMW_STEPS, C), lambda i: (l, i, 0))
        in_specs += [blk] * 4
        args += [w, g, m, v]
        out_specs += [blk] * 3
        out_shape += [SDS((L, R, C), F32)] * 3
    aliases = None
    if prev is not None:
        aliases = {4 * n + i: i for i in range(3 * n)}
        in_specs += [ANY] * (3 * n)
        args += list(prev)

    def body(*refs):
        outs = refs[len(args):]
        for i in range(n):
            _adamw_update(*refs[4 * i:4 * i + 4], *outs[3 * i:3 * i + 3])

    return _call(body, args, grid=(ADAMW_STEPS,), in_specs=in_specs, out_specs=out_specs, out_shape=out_shape, semantics=("parallel",),
                 name=name, aliases=aliases, side=side)


def _adamw(w, g, m, v, name):
    shape = w.shape
    C = shape[-1]
    R = w.size // C
    tb = _tile(R, max(8, (1 << 18) // C), 8)
    body = functools.partial(_adamw_update)
    blk = pl.BlockSpec((tb, C), lambda i: (i, 0))
    outs = pl.pallas_call(body, grid=(R // tb,), in_specs=[blk] * 4, out_specs=[blk] * 3, out_shape=[SDS((R, C), F32)] * 3,
                          compiler_params=_params("parallel"), name=name)(*[t.reshape(R, C) for t in (w, g, m, v)])
    return [t.reshape(shape) for t in outs]


WEIGHTS = ("mix_norm_g", "w_in", "conv_dw_w", "conv_dw_b", "conv_ln_g", "conv_ln_b", "w_conv_out", "w_pool_grp", "pool_scale", "w_out",
           "xattn_norm_g", "mem_norm_g", "w_q", "w_kv", "w_o", "ffn_norm_g", "w_up", "ffn_dw_w", "w_down", "final_norm_g")
VECTORS = ("mix_norm_g", "conv_dw_b", "conv_ln_g", "conv_ln_b", "pool_scale", "xattn_norm_g", "mem_norm_g", "ffn_norm_g", "final_norm_g")


def _shard_view(t, kind):
    L, P, R, C = t.shape
    return t.reshape(L, P, 2, R // 2, C)


def _rows(t, width):
    return t.reshape(-1, width)


def _pack(parts):
    return jnp.concatenate([jnp.pad(p, ((0, (-p.shape[0]) % 8), (0, 0))) for p in parts], axis=0)


def kernel(x, mem, mix_norm_g, w_in, conv_dw_w, conv_dw_b, conv_ln_g, conv_ln_b, w_conv_out, w_pool_grp, pool_scale, w_out, xattn_norm_g, mem_norm_g, w_q, w_kv, w_o, ffn_norm_g, w_up, ffn_dw_w, w_down, final_norm_g, loss_target, m_mix_norm_g, m_w_in, m_conv_dw_w, m_conv_dw_b, m_conv_ln_g, m_conv_ln_b, m_w_conv_out, m_w_pool_grp, m_pool_scale, m_w_out, m_xattn_norm_g, m_mem_norm_g, m_w_q, m_w_kv, m_w_o, m_ffn_norm_g, m_w_up, m_ffn_dw_w, m_w_down, m_final_norm_g, v_mix_norm_g, v_w_in, v_conv_dw_w, v_conv_dw_b, v_conv_ln_g, v_conv_ln_b, v_w_conv_out, v_w_pool_grp, v_pool_scale, v_w_out, v_xattn_norm_g, v_mem_norm_g, v_w_q, v_w_kv, v_w_o, v_ffn_norm_g, v_w_up, v_ffn_dw_w, v_w_down, v_final_norm_g):
    w = dict(mix_norm_g=mix_norm_g, w_in=w_in, conv_dw_w=conv_dw_w, conv_dw_b=conv_dw_b, conv_ln_g=conv_ln_g, conv_ln_b=conv_ln_b,
             w_conv_out=w_conv_out, w_pool_grp=w_pool_grp, pool_scale=pool_scale, w_out=w_out, xattn_norm_g=xattn_norm_g,
             mem_norm_g=mem_norm_g, w_q=w_q, w_kv=w_kv, w_o=w_o, ffn_norm_g=ffn_norm_g, w_up=w_up, ffn_dw_w=ffn_dw_w, w_down=w_down,
             final_norm_g=final_norm_g)
    m = dict(zip(WEIGHTS, (m_mix_norm_g, m_w_in, m_conv_dw_w, m_conv_dw_b, m_conv_ln_g, m_conv_ln_b, m_w_conv_out, m_w_pool_grp, m_pool_scale,
                           m_w_out, m_xattn_norm_g, m_mem_norm_g, m_w_q, m_w_kv, m_w_o, m_ffn_norm_g, m_w_up, m_ffn_dw_w, m_w_down, m_final_norm_g)))
    v = dict(zip(WEIGHTS, (v_mix_norm_g, v_w_in, v_conv_dw_w, v_conv_dw_b, v_conv_ln_g, v_conv_ln_b, v_w_conv_out, v_w_pool_grp, v_pool_scale,
                           v_w_out, v_xattn_norm_g, v_mem_norm_g, v_w_q, v_w_kv, v_w_o, v_ffn_norm_g, v_w_up, v_ffn_dw_w, v_w_down, v_final_norm_g)))
    xi, yi, ci, j = _place()
    jc = jnp.stack([j, ci]).astype(jnp.int32)
    L = w_in.shape[0]
    G = len(POOL_WINDOWS)
    kinds = dict(BIG)

    def to_mat(name, t):
        if name == "w_pool":
            return jnp.swapaxes(t, 2, 3)
        return t[:, None]

    def from_mat(name, t):
        if name == "w_pool":
            return jnp.swapaxes(t, 2, 3)
        return t[:, 0]

    src = {name: w["w_pool_grp" if name == "w_pool" else name] for name, _ in BIG}

    KC, cs_c = conv_dw_w.shape[1], conv_dw_w.shape[2]
    KF, cs_f = ffn_dw_w.shape[1], ffn_dw_w.shape[2]
    taps = jnp.concatenate([conv_dw_w.reshape(L * KC, cs_c), ffn_dw_w.reshape(L * KF * (cs_f // cs_c), cs_c)], axis=0)
    n_taps = taps.shape[0]
    taps = jnp.pad(taps, ((0, (-n_taps) % 16), (0, 0)))
    names = [name for name, _ in BIG]
    mats = {name: to_mat(name, src[name]).astype(BF) for name in names}

    def layer_shards(l, subset):
        return [_shard_view(mats[name][l:l + 1], kinds[name]) for name in subset]

    def as_weights(subset, fulls):
        return {name: f.reshape(G if name == "w_pool" else 1, -1, f.shape[-1]) for name, f in zip(subset, fulls)}

    fulls = _gather_weights(layer_shards(0, names) + [_shard_view(taps[None, None], "row")], [kinds[name] for name in names] + ["row"])
    W = [as_weights(names, fulls[:-1])]
    taps_all = fulls[-1].reshape(N_CHIPS, -1, cs_c)[:, :n_taps]
    V = {name: w[name] for name in VECTORS}
    V["conv_dw_w"] = taps_all[:, :L * KC].reshape(N_CHIPS, L, KC, cs_c).transpose(1, 2, 0, 3).reshape(L, KC, N_CHIPS * cs_c)
    V["ffn_dw_w"] = taps_all[:, L * KC:].reshape(N_CHIPS, L, KF, cs_f).transpose(1, 2, 0, 3).reshape(L, KF, N_CHIPS * cs_f)

    Bn, S, D = x.shape
    Mn = mem.shape[1]
    dims = (Bn, S, Mn, D, conv_dw_b.shape[1], w_down.shape[1] * N_CHIPS)
    xt = x.reshape(Bn * S, D)
    memf = mem.reshape(Bn * Mn, D)
    mem_n = _rms_fwd(memf, V["mem_norm_g"], "mem_norm")
    saved = []
    ht = _rms_fwd(xt, V["mix_norm_g"][0], "l0_mix_norm")
    for l in range(L):
        sides = _Sides()
        if l + 1 < L:
            sides = _Sides({key: _side_gather(layer_shards(l + 1, subset), [kinds[name] for name in subset]) for key, subset in FWD_CARRY.items()})
        xt, ht, sv = _layer_fwd(xt, ht, mem_n, W[l], V, l, dims, sides, V["mix_norm_g"][l + 1] if l + 1 < L else None)
        saved.append(sv)
        if l + 1 < L:
            carried = [name for subset in FWD_CARRY.values() for name in subset]
            landed = [f for key in FWD_CARRY for f in sides.landed[key]]
            done = _gather_pass(landed, layer_shards(l + 1, carried), [kinds[name] for name in carried], f"gather_pass_l{l + 1}")
            W.append(as_weights(carried, done))
    loss, dx, dgf = _loss_bwd(xt, V["final_norm_g"], loss_target.reshape(Bn * S, D), "loss")
    loss = lax.psum(loss[0, 0], ("x", "y", "c"))

    late_names = [name for name in names if name not in EARLY]

    def chip_sums(gw, subset, l, tag):
        def view(g, name):
            g = g if g.ndim == 3 else g[None]
            P, R, C = g.shape
            return g.reshape(P, 2, R // 2, C) if kinds[name] == "col" else g.reshape(P, N_CHIPS, 2, R // (2 * N_CHIPS), C)

        gv = [view(gw[name][0], name) for name in subset]
        lands = _sibling_exchange([view(gw[name][1], name) for name in subset], [kinds[name] for name in subset],
                                  f"grad_sibling_exchange_{tag}_l{l}")
        own, pieces = _chip_sums(gv, lands, [kinds[name] for name in subset], jc, f"chip_sums_{tag}_l{l}")
        return dict(zip(subset, own)), dict(zip(subset, pieces))

    def carry(table, pieces):
        return _Sides({key: _side_chip_exchange([pieces[name] for name in subset]) for key, subset in table.items()})

    def landed(table, sides):
        return {name: land for key, subset in table.items() for name, land in zip(subset, sides.landed[key])}

    dxb, dmem_n = dx, None
    smalls, owns, got = [None] * L, [{} for _ in range(L)], [{} for _ in range(L)]
    late = None
    for l in reversed(range(L)):
        sides = carry(BWD_CARRY_LATE, late) if late is not None else _Sides()
        dx, dxb, gw, sm = _layer_bwd_mlp(dx, dxb, saved[l], W[l], V, l, dims, sides)
        if late is not None:
            got[l + 1].update(landed(BWD_CARRY_LATE, sides))
        own, early = chip_sums(gw, EARLY, l, "mlp")
        owns[l].update(own)
        sides = carry(BWD_CARRY_EARLY, early)
        dx, dxb, dmem_n, gw, sm2 = _layer_bwd_mix(dx, dxb, dmem_n, saved[l], mem_n, W[l], V, l, dims, sides)
        got[l].update(landed(BWD_CARRY_EARLY, sides))
        smalls[l] = {**sm, **sm2}
        own, late = chip_sums(gw, late_names, l, "mix")
        owns[l].update(own)
    grad_x = dx.reshape(Bn, S, D)
    _, _, dgm = _rms_bwd(memf, V["mem_norm_g"], dmem_n, None, "mem_norm_b")
    small = {k: jnp.stack([sm[k] for sm in smalls]) if k in ("conv_dw_w", "ffn_dw_w") else jnp.concatenate([sm[k] for sm in smalls], axis=0)
             for k in smalls[0]}
    small["mem_norm_g"] = dgm
    small["final_norm_g"] = dgf

    small_w = conv_dw_b.shape[1]
    order = VECTORS + ("conv_dw_w", "ffn_dw_w")
    parts = [_rows(small[name], small_w) for name in order]
    counts = [p.shape[0] for p in parts]
    summed, landed_late = _reduce_small(_pack(parts), [late[name] for name in late_names])
    got[0].update(zip(late_names, landed_late))

    keys = ["w_pool_grp" if name == "w_pool" else name for name in names]
    rows3 = lambda t: t.reshape(t.shape[0], -1, t.shape[-1])
    wmv = [[rows3(to_mat(name, d[key])) for name, key in zip(names, keys)] for d in (w, m, v)]
    gshards, updates = None, None
    for l in reversed(range(L)):
        gshards = _final_sums([owns[l][name] for name in names], [got[l][name] for name in names], jc, gshards, l, L, f"final_sums_l{l}")
        gshards = _halves_exchange(gshards, l, f"grad_halves_exchange_l{l}")
        updates = _adamw_layer(wmv[0], [rows3(t) for t in gshards], wmv[1], wmv[2], updates, l, f"adamw_l{l}")
    grads, delta, new_m, new_v = {}, {}, {}, {}
    for i, (name, key) in enumerate(zip(names, keys)):
        Lg, P, _, RH, CS = gshards[i].shape
        grads[key] = from_mat(name, gshards[i].reshape(Lg, P, 2 * RH, CS))
        for d, t in zip((delta, new_m, new_v), updates[3 * i:3 * i + 3]):
            d[key] = from_mat(name, t.reshape(Lg, P, 2 * RH, CS))

    off = 0
    for name, cnt in zip(order, counts):
        t = summed[off:off + cnt]
        off += cnt + (-cnt) % 8
        if name in VECTORS:
            grads[name] = t.reshape(w[name].shape)
        else:
            full = t.reshape(small[name].shape)
            cs = w[name].shape[2]
            grads[name] = lax.dynamic_slice_in_dim(full, j * cs, cs, axis=2)

    vec =[_pack([_rows(d[name], small_w) for name in VECTORS]) for d in (w, grads, m, v)]
    outs = _adamw(*vec, "adamw_vectors")
    off = 0
    for name in VECTORS:
        cnt = w[name].size // small_w
        for d, t in zip((delta, new_m, new_v), outs):
            d[name] = t[off:off + cnt].reshape(w[name].shape)
        off += cnt + (-cnt) % 8
    for name in ("conv_dw_w", "ffn_dw_w"):
        delta[name], new_m[name], new_v[name] = _adamw(w[name], grads[name], m[name], v[name], "adamw_" + name)

    return (loss, grad_x, *[grads[k] for k in WEIGHTS], *[delta[k] for k in WEIGHTS], *[new_m[k] for k in WEIGHTS], *[new_v[k] for k in WEIGHTS])
```

```python
import functools

import jax
import jax.numpy as jnp
from jax import lax
from jax.experimental import pallas as pl
from jax.experimental.pallas import tpu as pltpu

F32 = jnp.float32
BF = jnp.bfloat16
SDS = jax.ShapeDtypeStruct
MESH = pl.DeviceIdType.MESH
ANY = pl.BlockSpec(memory_space=pl.ANY)

EPS = 1e-6
XA_HEADS = 4
POOL_WINDOWS = (2, 4, 8, 16)
N_CHIPS = 4
ADAM_LR, ADAM_B1, ADAM_B2, ADAM_EPS, ADAM_WD, ADAM_STEP = 0.001, 0.9, 0.999, 1e-08, 0.01, 10

LANES = 128
ROW_BLOCK = 512
VMEM_LIMIT = 56 * 1024 * 1024


def _params(*sem):
    return pltpu.CompilerParams(dimension_semantics=sem if sem else None, vmem_limit_bytes=VMEM_LIMIT)


def _tile(n, cap, mult=LANES):
    if n <= cap:
        return n
    for t in range(cap - cap % mult, 0, -mult):
        if n % t == 0:
            return t
    return n


_DN = {"nn": (((1,), (0,)), ((), ())), "nt": (((1,), (1,)), ((), ())), "tn": (((0,), (0,)), ((), ()))}


class _Side:
    def __init__(self, ins, outs, n, make, n_alias=0):
        self.ins, self.outs, self.n, self.make, self.n_alias = list(ins), list(outs), n, make, n_alias


def _call(body, args, *, grid, in_specs, out_specs, out_shape, semantics, name, scratch_shapes=(), side=None, prefetch=(), aliases=None):
    n_pf = len(prefetch)
    aliases = {n_pf + i: o for i, o in (aliases or {}).items()}
    n_in, n_out, n_scr = len(args), len(out_shape), len(scratch_shapes)
    n_si, n_so = (len(side.ins), len(side.outs)) if side is not None else (0, 0)
    if side is not None:
        aliases.update({n_pf + n_in + n_si - side.n_alias + i: n_out + i for i in range(side.n_alias)})

    def carrying(*refs):
        pf, refs = refs[:n_pf], refs[n_pf:]
        ins, s_in = refs[:n_in], refs[n_in:n_in + n_si]
        outs, s_out = refs[n_in + n_si:n_in + n_si + n_out], refs[n_in + n_si + n_out:n_in + n_si + n_out + n_so]
        scr = refs[n_in + n_si + n_out + n_so:]
        if side is None:
            return body(*pf, *ins, *outs, *scr)
        copies = side.make(s_in, s_out, scr[n_scr], scr[n_scr + 1])
        ids = [pl.program_id(d) for d in range(len(grid))]
        first, last = ids[0] == 0, ids[0] == grid[0] - 1
        for d in range(1, len(grid)):
            first, last = first & (ids[d] == 0), last & (ids[d] == grid[d] - 1)

        @pl.when(first)
        def _():
            for cp in copies:
                cp.start()

        body(*pf, *ins, *outs, *scr[:n_scr])

        @pl.when(last)
        def _():
            for cp in copies:
                cp.wait()

    sems = [pltpu.SemaphoreType.DMA((side.n,)), pltpu.SemaphoreType.DMA((side.n,))] if side is not None else []
    outs = pl.pallas_call(
        carrying, grid_spec=pltpu.PrefetchScalarGridSpec(
            num_scalar_prefetch=n_pf, grid=grid, in_specs=list(in_specs) + [ANY] * n_si, out_specs=list(out_specs) + [ANY] * n_so,
            scratch_shapes=list(scratch_shapes) + sems),
        out_shape=list(out_shape) + (side.outs if side is not None else []), input_output_aliases=aliases,
        compiler_params=_params(*(semantics if side is None else ["arbitrary"] * len(grid))), name=name)(
            *prefetch, *args, *(side.ins if side is not None else []))
    return list(outs) if side is None else (list(outs[:n_out]), list(outs[n_out:]))


def _call1(body, args, *, out_spec, out_shape, side=None, **kw):
    got = _call(body, args, out_specs=[out_spec], out_shape=[out_shape], side=side, **kw)
    return got[0] if side is None else (got[0][0], got[1])


MM_VMEM_BUDGET = 40 * 1024 * 1024
MM_STEP_MACS = 2200 * 1024 * 1024
MXU_WIDTH = 256
MM_STEP_COST_BYTES = 1 << 20


def _divisors(n):
    return [t for t in range(LANES, n + 1, LANES) if n % t == 0] or [n]


def _mm_tiles(M, N, K, a_bytes, b_bytes, o_bytes):
    best = None
    for tk in _divisors(K):
        for tm in _divisors(M):
            for tn in _divisors(N):
                nk = K // tk
                foot = 2 * (tm * tk * a_bytes + tk * tn * b_bytes + tm * tn * o_bytes) + (tm * tn * 4 if nk > 1 else 0)
                if foot > MM_VMEM_BUDGET or tm * tn * tk > MM_STEP_MACS or tn < min(N, MXU_WIDTH) or tm < min(M, MXU_WIDTH):
                    continue
                steps = (M // tm) * (N // tn) * nk
                traffic = M * K * a_bytes * (N // tn if nk > 1 else 1) + K * N * b_bytes * (M // tm) + M * N * o_bytes
                exposed = tm * tk * a_bytes + tk * tn * b_bytes + tm * tn * o_bytes
                cost = traffic + exposed + steps * MM_STEP_COST_BYTES + (nk - 1) * M * N * 8
                if best is None or cost < best[0]:
                    best = (cost, tm, tn, tk)
    assert best is not None, (M, N, K)
    return best[1:]


def _mm(a, b, dims, out_dtype, name, res=None, bl=None, side=None, twin=None):
    bs = b.shape[1:] if bl is not None else b.shape
    if dims == "nn":
        (M, K), (K2, N) = a.shape, bs
    elif dims == "nt":
        (M, K), (N, K2) = a.shape, bs
    else:
        (K, M), (K2, N) = a.shape, bs
    assert K == K2, (name, a.shape, b.shape)
    tm, tn, tk = _mm_tiles(M, N, K, a.dtype.itemsize, b.dtype.itemsize, jnp.dtype(out_dtype).itemsize
                           + (res.dtype.itemsize if res is not None else 0) + (jnp.dtype(twin).itemsize if twin is not None else 0))
    nk = K // tk
    lead = (None,) if bl is not None else ()
    pre = (lambda *ix: (bl,) + ix) if bl is not None else (lambda *ix: ix)
    if dims == "tn":
        a_spec = pl.BlockSpec((tk, tm), lambda i, j, k: (k, i))
    else:
        a_spec = pl.BlockSpec((tm, tk), lambda i, j, k: (i, k))
    if dims == "nt":
        b_spec = pl.BlockSpec(lead + (tn, tk), lambda i, j, k: pre(j, k))
    else:
        b_spec = pl.BlockSpec(lead + (tk, tn), lambda i, j, k: pre(k, j))
    o_spec = pl.BlockSpec((tm, tn), lambda i, j, k: (i, j))
    in_specs, args = [a_spec, b_spec], [a, b]
    if res is not None:
        in_specs.append(o_spec)
        args.append(res)
    n_main = len(args)
    n_out = 1 if twin is None else 2

    def body(*refs):
        a_ref, b_ref = refs[0], refs[1]
        r_ref = refs[2] if res is not None else None
        o_ref = refs[n_main]
        p = lax.dot_general(a_ref[...].astype(BF), b_ref[...].astype(BF), _DN[dims], preferred_element_type=F32)

        def finish(t):
            if r_ref is not None:
                t = t + r_ref[...]
            o_ref[...] = t.astype(out_dtype)
            if twin is not None:
                refs[n_main + 1][...] = t.astype(twin)

        if nk == 1:
            finish(p)
        else:
            acc = refs[n_main + n_out]
            k = pl.program_id(2)

            @pl.when(k == 0)
            def _():
                acc[...] = p

            @pl.when(k > 0)
            def _():
                acc[...] += p

            @pl.when(k == nk - 1)
            def _():
                finish(acc[...])

    got = _call(body, args, grid=(M // tm, N // tn, nk), in_specs=in_specs, out_specs=[o_spec] * n_out,
                out_shape=[SDS((M, N), out_dtype)] + ([SDS((M, N), twin)] if twin is not None else []),
                scratch_shapes=[pltpu.VMEM((tm, tn), F32)] if nk > 1 else [], semantics=("parallel", "parallel", "arbitrary"),
                name=name, side=side)
    outs, landed = (got, None) if side is None else got
    out = outs[0] if twin is None else (outs[0], outs[1])
    return out if side is None else (out, landed)


def _rms(x, g):
    return x * lax.rsqrt(jnp.mean(x * x, axis=-1, keepdims=True) + EPS) * g


def _ln_silu(x, g, b):
    mu = jnp.mean(x, axis=-1, keepdims=True)
    xc = x - mu
    var = jnp.mean(xc * xc, axis=-1, keepdims=True)
    return jax.nn.silu(xc * lax.rsqrt(var + EPS) * g + b)


def _merge(gc, gp, yc, yp, ps):
    return jax.nn.sigmoid(gc) * yc + jax.nn.sigmoid(gp) * (yp * ps)


def _gated(gate, val):
    return jax.nn.gelu(gate) * val


def _rms_fwd(x, g, name):
    T, D = x.shape
    tb = _tile(T, ROW_BLOCK, 8)

    def body(x_ref, g_ref, o_ref):
        o_ref[...] = _rms(x_ref[...], g_ref[...]).astype(BF)

    row = pl.BlockSpec((tb, D), lambda i: (i, 0))
    return pl.pallas_call(body, grid=(T // tb,), in_specs=[row, pl.BlockSpec((1, D), lambda i: (0, 0))], out_specs=row,
                          out_shape=SDS((T, D), BF), compiler_params=_params("parallel"), name=name)(x, g.reshape(1, D))


def _rms_bwd(x, g, dh, dres, name):
    T, D = x.shape
    tb = _tile(T, ROW_BLOCK, 8)

    def body(*refs):
        if dres is not None:
            x_ref, g_ref, dh_ref, dres_ref, dx_ref, dxb_ref, dg_ref = refs
        else:
            x_ref, g_ref, dh_ref, dx_ref, dxb_ref, dg_ref = refs
        _, vjp = jax.vjp(_rms, x_ref[...], g_ref[...])
        dx, dg = vjp(dh_ref[...].astype(F32))
        if dres is not None:
            dx = dx + dres_ref[...]
        dx_ref[...] = dx
        dxb_ref[...] = dx.astype(BF)

        @pl.when(pl.program_id(0) == 0)
        def _():
            dg_ref[...] = jnp.zeros_like(dg_ref)

        dg_ref[...] += dg

    row = pl.BlockSpec((tb, D), lambda i: (i, 0))
    vec = pl.BlockSpec((1, D), lambda i: (0, 0))
    ins = [x, g.reshape(1, D), dh] + ([dres] if dres is not None else [])
    return pl.pallas_call(
        body, grid=(T // tb,), in_specs=[row, vec, row] + ([row] if dres is not None else []), out_specs=[row, row, vec],
        out_shape=[SDS((T, D), F32), SDS((T, D), BF), SDS((1, D), F32)], compiler_params=_params("arbitrary"), name=name)(*ins)


def _row_tile(M, K, N, per_row_bytes):
    fixed = K * N * 2
    fit = [t for t in _divisors(M) if fixed + 2 * t * per_row_bytes <= MM_VMEM_BUDGET and t * K * N <= MM_STEP_MACS]
    return max(fit) if fit else min(_divisors(M))


def _mm_rms_fwd(a, b, res, g, name, side=None):
    M, K = a.shape
    N = b.shape[2]
    tm = _row_tile(M, K, N, K * 2 + N * (4 + 4 + 2))

    def body(a_ref, b_ref, r_ref, g_ref, x_ref, h_ref):
        x = r_ref[...] + lax.dot_general(a_ref[...], b_ref[...], _DN["nn"], preferred_element_type=F32)
        x_ref[...] = x
        h_ref[...] = _rms(x, g_ref[...]).astype(BF)

    row = pl.BlockSpec((tm, N), lambda i: (i, 0))
    return _call(body, (a, b, res, g.reshape(1, N)), grid=(M // tm,),
                 in_specs=[pl.BlockSpec((tm, K), lambda i: (i, 0)), pl.BlockSpec((None, K, N), lambda i: (0, 0, 0), pipeline_mode=pl.Buffered(1)), row,
                           pl.BlockSpec((1, N), lambda i: (0, 0))],
                 out_specs=[row, row], out_shape=[SDS((M, N), F32), SDS((M, N), BF)], semantics=("parallel",), name=name, side=side)


def _mm_rms_bwd(a, b, x, g, dres, name):
    M, K = a.shape
    N = b.shape[1]
    tm = _row_tile(M, K, N, K * 2 + N * (4 + 4 + 4 + 2))

    def body(a_ref, b_ref, x_ref, g_ref, r_ref, dx_ref, dxb_ref, dg_ref):
        dh = lax.dot_general(a_ref[...], b_ref[...], _DN["nt"], preferred_element_type=F32)
        _, vjp = jax.vjp(_rms, x_ref[...], g_ref[...])
        dx, dg = vjp(dh)
        dx = dx + r_ref[...]
        dx_ref[...] = dx
        dxb_ref[...] = dx.astype(BF)

        @pl.when(pl.program_id(0) == 0)
        def _():
            dg_ref[...] = jnp.zeros_like(dg_ref)

        dg_ref[...] += dg

    row = pl.BlockSpec((tm, N), lambda i: (i, 0))
    vec = pl.BlockSpec((1, N), lambda i: (0, 0))
    return pl.pallas_call(
        body, grid=(M // tm,),
        in_specs=[pl.BlockSpec((tm, K), lambda i: (i, 0)), pl.BlockSpec((None, N, K), lambda i: (0, 0, 0), pipeline_mode=pl.Buffered(1)), row, vec, row],
        out_specs=[row, row, vec], out_shape=[SDS((M, N), F32), SDS((M, N), BF), SDS((1, N), F32)],
        compiler_params=_params("arbitrary"), name=name)(a, b, x, g.reshape(1, N), dres)


def _loss_bwd(x, g, target, name):
    T, D = x.shape
    tb = _tile(T, ROW_BLOCK, 8)
    nb = T // tb

    def body(x_ref, g_ref, t_ref, loss_ref, dx_ref, dg_ref, acc):
        i = pl.program_id(0)
        y, vjp = jax.vjp(_rms, x_ref[...], g_ref[...])
        err = y - t_ref[...]
        dx, dg = vjp(err * (1.0 / D))
        dx_ref[...] = dx

        @pl.when(i == 0)
        def _():
            dg_ref[...] = jnp.zeros_like(dg_ref)
            acc[...] = jnp.zeros_like(acc)

        dg_ref[...] += dg
        acc[...] += jnp.sum(err * err, axis=0, keepdims=True)

        @pl.when(i == nb - 1)
        def _():
            loss_ref[...] = jnp.full(loss_ref.shape, (0.5 / D) * jnp.sum(acc[...]), F32)

    row = pl.BlockSpec((tb, D), lambda i: (i, 0))
    vec = pl.BlockSpec((1, D), lambda i: (0, 0))
    return pl.pallas_call(
        body, grid=(nb,), in_specs=[row, vec, row], out_specs=[pl.BlockSpec((1, LANES), lambda i: (0, 0)), row, vec],
        out_shape=[SDS((1, LANES), F32), SDS((T, D), F32), SDS((1, D), F32)], scratch_shapes=[pltpu.VMEM((1, D), F32)],
        compiler_params=_params("arbitrary"), name=name)(x, g.reshape(1, D), target)


def _ln_silu_fwd(cv, g, b, name):
    T, C = cv.shape
    tb = _tile(T, ROW_BLOCK, 8)

    def body(x_ref, g_ref, b_ref, o_ref):
        o_ref[...] = _ln_silu(x_ref[...], g_ref[...], b_ref[...]).astype(BF)

    row = pl.BlockSpec((tb, C), lambda i: (i, 0))
    vec = pl.BlockSpec((1, C), lambda i: (0, 0))
    return pl.pallas_call(body, grid=(T // tb,), in_specs=[row, vec, vec], out_specs=row, out_shape=SDS((T, C), BF),
                          compiler_params=_params("parallel"), name=name)(cv, g.reshape(1, C), b.reshape(1, C))


def _ln_silu_bwd(cv, g, b, dy, name):
    T, C = cv.shape
    tb = _tile(T, ROW_BLOCK, 8)

    def body(x_ref, g_ref, b_ref, dy_ref, dx_ref, dg_ref, db_ref):
        _, vjp = jax.vjp(_ln_silu, x_ref[...], g_ref[...], b_ref[...])
        dx, dg, db = vjp(dy_ref[...].astype(F32))
        dx_ref[...] = dx

        @pl.when(pl.program_id(0) == 0)
        def _():
            dg_ref[...] = jnp.zeros_like(dg_ref)
            db_ref[...] = jnp.zeros_like(db_ref)

        dg_ref[...] += dg
        db_ref[...] += db

    row = pl.BlockSpec((tb, C), lambda i: (i, 0))
    vec = pl.BlockSpec((1, C), lambda i: (0, 0))
    return pl.pallas_call(
        body, grid=(T // tb,), in_specs=[row, vec, vec, row], out_specs=[row, vec, vec],
        out_shape=[SDS((T, C), F32), SDS((1, C), F32), SDS((1, C), F32)], compiler_params=_params("arbitrary"),
        name=name)(cv, g.reshape(1, C), b.reshape(1, C), dy)


def _merge_fwd(proj, yc, yp, ps, C, name, side=None):
    T, D = yc.shape
    tb = _tile(T, ROW_BLOCK, 8)
    nj = D // C

    def body(gc_ref, gp_ref, yc_ref, yp_ref, ps_ref, o_ref):
        o_ref[...] = _merge(gc_ref[...], gp_ref[...], yc_ref[...].astype(F32), yp_ref[...].astype(F32), ps_ref[...]).astype(BF)

    blk = pl.BlockSpec((tb, C), lambda i, j: (i, j))
    return _call1(
        body, (proj, proj, yc, yp, ps.reshape(1, D)), grid=(T // tb, nj),
        in_specs=[pl.BlockSpec((tb, C), lambda i, j: (i, 3 + j)), pl.BlockSpec((tb, C), lambda i, j: (i, 3 + nj + j)), blk, blk,
                  pl.BlockSpec((1, C), lambda i, j: (0, j))],
        out_spec=blk, out_shape=SDS((T, D), BF), semantics=("parallel", "parallel"), name=name, side=side)


def _merge_bwd(proj, yc, yp, ps, dm, C, name, side=None):
    T, D = yc.shape
    tb = _tile(T, ROW_BLOCK, 8)
    nj = D // C

    def body(gc_ref, gp_ref, yc_ref, yp_ref, ps_ref, dm_ref, dgc_ref, dgp_ref, dyc_ref, dyp_ref, dps_ref):
        _, vjp = jax.vjp(_merge, gc_ref[...], gp_ref[...], yc_ref[...].astype(F32), yp_ref[...].astype(F32), ps_ref[...])
        dgc, dgp, dyc, dyp, dps = vjp(dm_ref[...].astype(F32))
        dgc_ref[...] = dgc.astype(BF)
        dgp_ref[...] = dgp.astype(BF)
        dyc_ref[...] = dyc.astype(BF)
        dyp_ref[...] = dyp.astype(BF)

        @pl.when(pl.program_id(1) == 0)
        def _():
            dps_ref[...] = jnp.zeros_like(dps_ref)

        dps_ref[...] += dps

    blk = pl.BlockSpec((tb, C), lambda j, i: (i, j))
    vec = pl.BlockSpec((1, C), lambda j, i: (0, j))
    return _call(
        body, (proj, proj, yc, yp, ps.reshape(1, D), dm), grid=(nj, T // tb),
        in_specs=[pl.BlockSpec((tb, C), lambda j, i: (i, 3 + j)), pl.BlockSpec((tb, C), lambda j, i: (i, 3 + nj + j)), blk, blk, vec, blk],
        out_specs=[blk, blk, blk, blk, vec], out_shape=[SDS((T, D), BF)] * 4 + [SDS((1, D), F32)],
        semantics=("parallel", "arbitrary"), name=name, side=side)


def _shd(v, s, rows):
    if s == 0:
        return v
    return jnp.where(rows >= s, pltpu.roll(v, s, 0), 0.0)


def _shu(v, s, rows):
    if s == 0:
        return v
    n = v.shape[0]
    return jnp.where(rows < n - s, pltpu.roll(v, n - s, 0), 0.0)


def _glu_conv_fwd(proj, w, b, Bn, S, C, name, side=None):
    K = w.shape[0]
    sl = min(LANES, C)
    ns = C // sl

    def body(a_ref, gl_ref, w_ref, b_ref, o_ref):
        y0 = a_ref[...] * jax.nn.sigmoid(gl_ref[...])
        rows = lax.broadcasted_iota(jnp.int32, y0.shape, 0)
        acc = jnp.zeros_like(y0) + b_ref[...]
        for k in range(K):
            acc = acc + w_ref[k:k + 1, :] * _shd(y0, K - 1 - k, rows)
        o_ref[...] = acc

    return _call1(
        body, (proj, proj, w, b.reshape(1, C)), grid=(Bn, ns),
        in_specs=[pl.BlockSpec((S, sl), lambda bi, j: (bi, j)), pl.BlockSpec((S, sl), lambda bi, j: (bi, ns + j)),
                  pl.BlockSpec((K, sl), lambda bi, j: (0, j)), pl.BlockSpec((1, sl), lambda bi, j: (0, j))],
        out_spec=pl.BlockSpec((S, sl), lambda bi, j: (bi, j)), out_shape=SDS((Bn * S, C), F32),
        semantics=("parallel", "parallel"), name=name, side=side)


def _glu_conv_bwd(proj, w, dcv, Bn, S, C, name, side=None):
    K = w.shape[0]
    sl = min(LANES, C)
    ns = C // sl

    def body(a_ref, gl_ref, w_ref, d_ref, da_ref, dgl_ref, dw_ref, db_ref):
        a = a_ref[...]
        sg = jax.nn.sigmoid(gl_ref[...])
        y0 = a * sg
        d = d_ref[...]
        rows = lax.broadcasted_iota(jnp.int32, y0.shape, 0)

        @pl.when(pl.program_id(1) == 0)
        def _():
            dw_ref[...] = jnp.zeros_like(dw_ref)
            db_ref[...] = jnp.zeros_like(db_ref)

        dy0 = jnp.zeros_like(y0)
        for k in range(K):
            s = K - 1 - k
            dw_ref[k:k + 1, :] += jnp.sum(d * _shd(y0, s, rows), axis=0, keepdims=True)
            dy0 = dy0 + w_ref[k:k + 1, :] * _shu(d, s, rows)
        db_ref[...] += jnp.sum(d, axis=0, keepdims=True)
        da_ref[...] = (dy0 * sg).astype(BF)
        dgl_ref[...] = (dy0 * a * sg * (1.0 - sg)).astype(BF)

    blk = pl.BlockSpec((S, sl), lambda j, bi: (bi, j))
    return _call(
        body, (proj, proj, w, dcv), grid=(ns, Bn),
        in_specs=[blk, pl.BlockSpec((S, sl), lambda j, bi: (bi, ns + j)), pl.BlockSpec((K, sl), lambda j, bi: (0, j)), blk],
        out_specs=[blk, blk, pl.BlockSpec((K, sl), lambda j, bi: (0, j)), pl.BlockSpec((1, sl), lambda j, bi: (0, j))],
        out_shape=[SDS((Bn * S, C), BF), SDS((Bn * S, C), BF), SDS((K, C), F32), SDS((1, C), F32)],
        semantics=("parallel", "arbitrary"), name=name, side=side)


def _pool_z(u, g, rows):
    s2 = u + _shd(u, 1, rows)
    s4 = s2 + _shd(s2, 2, rows)
    s8 = s4 + _shd(s4, 4, rows)
    s16 = s8 + _shd(s8, 8, rows)
    sw = jnp.where(g == 0, s2, jnp.where(g == 1, s4, jnp.where(g == 2, s8, s16)))
    cnt = jnp.minimum(rows + 1, POOL_WINDOWS[0] << g).astype(F32)
    return sw / cnt - u, cnt


def _pool_fwd(proj, wpt, l, Bn, S, C, D, name):
    G = len(POOL_WINDOWS)
    gd, go = C // G, D // G

    def body(u_ref, w_ref, o_ref):
        g = pl.program_id(1)
        u = u_ref[...]
        rows = lax.broadcasted_iota(jnp.int32, u.shape, 0)
        zp, _ = _pool_z(u, g, rows)
        o_ref[...] = lax.dot_general(zp.astype(BF), w_ref[...], _DN["nt"], preferred_element_type=F32).astype(BF)

    return pl.pallas_call(
        body, grid=(Bn, G),
        in_specs=[pl.BlockSpec((S, gd), lambda bi, g: (bi, 2 * G + g)), pl.BlockSpec((None, go, gd), lambda bi, g: (l * G + g, 0, 0))],
        out_specs=pl.BlockSpec((S, go), lambda bi, g: (bi, g)), out_shape=SDS((Bn * S, D), BF),
        compiler_params=_params("parallel", "parallel"), name=name)(proj, wpt)


def _pool_bwd(proj, wpt, dyp, l, Bn, S, C, D, name):
    G = len(POOL_WINDOWS)
    gd, go = C // G, D // G

    def body(u_ref, w_ref, d_ref, du_ref, dw_ref):
        g = pl.program_id(0)
        u = u_ref[...]
        rows = lax.broadcasted_iota(jnp.int32, u.shape, 0)
        zp, cnt = _pool_z(u, g, rows)
        d = d_ref[...]
        dzp = lax.dot_general(d, w_ref[...], _DN["nn"], preferred_element_type=F32)

        @pl.when(pl.program_id(1) == 0)
        def _():
            dw_ref[...] = jnp.zeros_like(dw_ref)

        dw_ref[...] += lax.dot_general(d, zp.astype(BF), _DN["tn"], preferred_element_type=F32)
        dsw = dzp / cnt
        zero = jnp.zeros_like(dsw)
        d16 = jnp.where(g == 3, dsw, zero)
        d8 = jnp.where(g == 2, dsw, zero) + d16 + _shu(d16, 8, rows)
        d4 = jnp.where(g == 1, dsw, zero) + d8 + _shu(d8, 4, rows)
        d2 = jnp.where(g == 0, dsw, zero) + d4 + _shu(d4, 2, rows)
        d1 = d2 + _shu(d2, 1, rows)
        du_ref[...] = (d1 - dzp).astype(BF)

    return pl.pallas_call(
        body, grid=(G, Bn),
        in_specs=[pl.BlockSpec((S, gd), lambda g, bi: (bi, 2 * G + g)), pl.BlockSpec((None, go, gd), lambda g, bi: (l * G + g, 0, 0)),
                  pl.BlockSpec((S, go), lambda g, bi: (bi, g))],
        out_specs=[pl.BlockSpec((S, gd), lambda g, bi: (bi, g)), pl.BlockSpec((None, go, gd), lambda g, bi: (g, 0, 0))],
        out_shape=[SDS((Bn * S, C), BF), SDS((G, go, gd), F32)],
        compiler_params=_params("parallel", "arbitrary"), name=name)(proj, wpt, dyp)


def _ffn_conv(u, w_ref, rows):
    K = w_ref.shape[0]
    acc = w_ref[K - 1:K, :] * u
    for k in range(K - 1):
        acc = acc + w_ref[k:k + 1, :] * _shd(u, K - 1 - k, rows)
    return acc


def _ffn_cb(F):
    return _tile(F, 256)


def _ffn_act_fwd(up0, w, Bn, S, F, name, side=None):
    cb = _ffn_cb(F)
    nj = F // cb

    def body(g_ref, v_ref, wg_ref, wv_ref, o_ref):
        rows = lax.broadcasted_iota(jnp.int32, g_ref.shape, 0)
        o_ref[...] = _gated(_ffn_conv(g_ref[...], wg_ref, rows), _ffn_conv(v_ref[...], wv_ref, rows)).astype(BF)

    K = w.shape[0]
    return _call1(
        body, (up0, up0, w, w), grid=(Bn, nj),
        in_specs=[pl.BlockSpec((S, cb), lambda bi, j: (bi, j)), pl.BlockSpec((S, cb), lambda bi, j: (bi, nj + j)),
                  pl.BlockSpec((K, cb), lambda bi, j: (0, j)), pl.BlockSpec((K, cb), lambda bi, j: (0, nj + j))],
        out_spec=pl.BlockSpec((S, cb), lambda bi, j: (bi, j)), out_shape=SDS((Bn * S, F), BF),
        semantics=("parallel", "parallel"), name=name, side=side)


SUBLANES = 8
FFN_HALO = SUBLANES
FFN_ROWS = 64
GELU_C0, GELU_C1 = 0.7978845608028654, 0.044715


def _gelu_and_grad(x):
    x2 = x * x
    t = jnp.tanh(GELU_C0 * (x + GELU_C1 * (x2 * x)))
    cdf = 0.5 * (1.0 + t)
    return x * cdf, cdf + (0.5 * GELU_C0) * x * (1.0 - t * t) * (1.0 + (3.0 * GELU_C1) * x2)


def _ffn_act_bwd(up0, w, dg, Bn, S, F, name, side=None):
    cb = min(LANES, F)
    nj = F // cb
    K = w.shape[0]
    rc = FFN_ROWS if S % FFN_ROWS == 0 else S
    win = rc + 2 * FFN_HALO
    assert K - 1 <= FFN_HALO and rc % SUBLANES == 0

    def body(g_ref, v_ref, wg_ref, wv_ref, d_ref, dgo_ref, dvo_ref, dwg_ref, dwv_ref, gp, vp, dp):
        for pad, src in ((gp, g_ref), (vp, v_ref), (dp, d_ref)):
            pad[0:FFN_HALO, :] = jnp.zeros((FFN_HALO, cb), F32)
            pad[FFN_HALO + S:, :] = jnp.zeros((FFN_HALO, cb), F32)
            pad[FFN_HALO:FFN_HALO + S, :] = src[...].astype(F32)
        wg = [wg_ref[k:k + 1, :] for k in range(K)]
        wv = [wv_ref[k:k + 1, :] for k in range(K)]

        def taps(u):
            return [pltpu.roll(u, K - 1 - k, 0) for k in range(K - 1)] + [u]

        def conv(us, ws):
            acc = ws[K - 1] * us[K - 1]
            for k in range(K - 1):
                acc = acc + ws[k] * us[k]
            return acc

        def conv_t(dc, ws):
            acc = ws[K - 1] * dc
            for k in range(K - 1):
                acc = acc + ws[k] * pltpu.roll(dc, win - (K - 1 - k), 0)
            return acc

        def fold(t):
            acc = t[FFN_HALO:FFN_HALO + SUBLANES]
            for i in range(1, rc // SUBLANES):
                acc = acc + t[FFN_HALO + SUBLANES * i:FFN_HALO + SUBLANES * (i + 1)]
            return acc

        def chunk(c, sums):
            r0 = pl.multiple_of(c * rc, SUBLANES)
            gs, vs, d = taps(gp[pl.ds(r0, win), :]), taps(vp[pl.ds(r0, win), :]), dp[pl.ds(r0, win), :]
            ge, dge = _gelu_and_grad(conv(gs, wg))
            dgc = d * conv(vs, wv) * dge
            dvc = d * ge
            dgo_ref[pl.ds(r0, rc), :] = conv_t(dgc, wg)[FFN_HALO:FFN_HALO + rc].astype(BF)
            dvo_ref[pl.ds(r0, rc), :] = conv_t(dvc, wv)[FFN_HALO:FFN_HALO + rc].astype(BF)
            new = [fold(dc * u) for us, dc in ((gs, dgc), (vs, dvc)) for u in us]
            return tuple(a + b for a, b in zip(sums, new))

        sums = lax.fori_loop(0, S // rc, chunk, tuple(jnp.zeros((SUBLANES, cb), F32) for _ in range(2 * K)))

        @pl.when(pl.program_id(1) == 0)
        def _():
            dwg_ref[...] = jnp.zeros_like(dwg_ref)
            dwv_ref[...] = jnp.zeros_like(dwv_ref)

        for k in range(K):
            dwg_ref[k:k + 1, :] += jnp.sum(sums[k], axis=0, keepdims=True)
            dwv_ref[k:k + 1, :] += jnp.sum(sums[K + k], axis=0, keepdims=True)

    blk = pl.BlockSpec((S, cb), lambda j, bi: (bi, j))
    wblk = pl.BlockSpec((K, cb), lambda j, bi: (0, j))
    return _call(
        body, (up0, up0, w, w, dg), grid=(nj, Bn),
        in_specs=[blk, pl.BlockSpec((S, cb), lambda j, bi: (bi, nj + j)), wblk, pl.BlockSpec((K, cb), lambda j, bi: (0, nj + j)), blk],
        out_specs=[blk, blk, wblk, wblk],
        out_shape=[SDS((Bn * S, F), BF), SDS((Bn * S, F), BF), SDS((K, F), F32), SDS((K, F), F32)],
        scratch_shapes=[pltpu.VMEM((S + 2 * FFN_HALO, cb), F32)] * 3, semantics=("parallel", "arbitrary"), name=name, side=side)


def _softmax_rows(q, k, scale):
    sc = lax.dot_general(q, k, _DN["nt"], preferred_element_type=F32) * scale
    e = jnp.exp(sc - jnp.max(sc, axis=-1, keepdims=True))
    return e / jnp.sum(e, axis=-1, keepdims=True)


def _attn_ts(S):
    return _tile(S, 1024, 8)


def _attn_fwd(q, kv, Bn, S, Mn, D, name, side=None):
    H = XA_HEADS
    dh = D // H
    ts = _attn_ts(S)
    nsb = S // ts
    scale = dh ** -0.5

    def body(q_ref, k_ref, v_ref, o_ref):
        p = _softmax_rows(q_ref[...], k_ref[...], scale)
        o_ref[...] = lax.dot_general(p.astype(BF), v_ref[...], _DN["nn"], preferred_element_type=F32).astype(BF)

    qblk = pl.BlockSpec((ts, dh), lambda bi, h, s: (bi * nsb + s, h))
    return _call1(
        body, (q, kv, kv), grid=(Bn, H, nsb),
        in_specs=[qblk, pl.BlockSpec((Mn, dh), lambda bi, h, s: (bi, h)), pl.BlockSpec((Mn, dh), lambda bi, h, s: (bi, H + h))],
        out_spec=qblk, out_shape=SDS((Bn * S, D), BF), semantics=("parallel", "parallel", "parallel"), name=name, side=side)


def _attn_bwd(q, kv, datt, Bn, S, Mn, D, name):
    H = XA_HEADS
    dh = D // H
    ts = _attn_ts(S)
    nsb = S // ts
    scale = dh ** -0.5

    def body(q_ref, k_ref, v_ref, do_ref, dq_ref, dk_ref, dv_ref):
        q, k, v, do = q_ref[...], k_ref[...], v_ref[...], do_ref[...]
        p = _softmax_rows(q, k, scale)
        dp = lax.dot_general(do, v, _DN["nt"], preferred_element_type=F32)
        ds = (p * (dp - jnp.sum(dp * p, axis=-1, keepdims=True)) * scale).astype(BF)
        dq_ref[...] = lax.dot_general(ds, k, _DN["nn"], preferred_element_type=F32).astype(BF)

        @pl.when(pl.program_id(2) == 0)
        def _():
            dk_ref[...] = jnp.zeros_like(dk_ref)
            dv_ref[...] = jnp.zeros_like(dv_ref)

        dk_ref[...] += lax.dot_general(ds, q, _DN["tn"], preferred_element_type=F32)
        dv_ref[...] += lax.dot_general(p.astype(BF), do, _DN["tn"], preferred_element_type=F32)

    qblk = pl.BlockSpec((ts, dh), lambda bi, h, s: (bi * nsb + s, h))
    kblk = pl.BlockSpec((Mn, dh), lambda bi, h, s: (bi, h))
    return pl.pallas_call(
        body, grid=(Bn, H, nsb),
        in_specs=[qblk, kblk, pl.BlockSpec((Mn, dh), lambda bi, h, s: (bi, H + h)), qblk],
        out_specs=[qblk, kblk, kblk], out_shape=[SDS((Bn * S, D), BF), SDS((Bn * Mn, D), F32), SDS((Bn * Mn, D), F32)],
        compiler_params=_params("parallel", "parallel", "arbitrary"), name=name)(q, kv, kv, datt)


class _Sides:
    def __init__(self, by_key=None, on_land=None):
        self.by_key, self.landed, self.on_land = dict(by_key or {}), {}, on_land

    def run(self, key, fn, *args, **kw):
        side = self.by_key.get(key)
        if side is None:
            return fn(*args, **kw)
        out, self.landed[key] = fn(*args, side=side() if callable(side) else side, **kw)
        if self.on_land is not None:
            self.on_land(key, self.landed[key])
        return out

    def mm(self, key, *args, **kw):
        return self.run(key, _mm, *args, **kw)


def _layer_fwd(x, h, mem_n, W, V, l, dims, sides, next_g):
    Bn, S, Mn, D, C, F = dims
    n = f"l{l}_"
    proj = sides.mm("proj", h, W["w_in"], "nn", F32, n + "proj", bl=0)
    cv = sides.run("glu_conv", _glu_conv_fwd, proj, V["conv_dw_w"][l], V["conv_dw_b"][l], Bn, S, C, n + "glu_conv")
    yc1 = _ln_silu_fwd(cv, V["conv_ln_g"][l], V["conv_ln_b"][l], n + "ln_silu")
    yc = sides.mm("conv_out", yc1, W["w_conv_out"], "nn", BF, n + "conv_out", bl=0)
    yp = _pool_fwd(proj, W["w_pool"], 0, Bn, S, C, D, n + "pool")
    merged = sides.run("merge", _merge_fwd, proj, yc, yp, V["pool_scale"][l], C, n + "merge")
    x1, hq = sides.run("out_proj", _mm_rms_fwd, merged, W["w_out"], x, V["xattn_norm_g"][l], n + "out_proj")
    q = sides.mm("q_proj", hq, W["w_q"], "nn", BF, n + "q_proj", bl=0)
    kv = _mm(mem_n, W["w_kv"], "nn", BF, n + "kv_proj", bl=0)
    att = sides.run("attn", _attn_fwd, q, kv, Bn, S, Mn, D, n + "attn")
    x2, hf = sides.run("o_proj", _mm_rms_fwd, att, W["w_o"], x1, V["ffn_norm_g"][l], n + "o_proj")
    up0 = sides.mm("up_proj", hf, W["w_up"], "nn", F32, n + "up_proj", bl=0)
    gact = sides.run("ffn_act", _ffn_act_fwd, up0, V["ffn_dw_w"][l], Bn, S, F, n + "ffn_act")
    if next_g is not None:
        x3, h3 = sides.run("down_proj", _mm_rms_fwd, gact, W["w_down"], x2, next_g, n + "down_proj")
    else:
        x3, h3 = sides.mm("down_proj", gact, W["w_down"], "nn", F32, n + "down_proj", res=x2, bl=0), None
    return x3, h3, dict(x=x, h=h, proj=proj, cv=cv, yc1=yc1, yc=yc, yp=yp, merged=merged, x1=x1, hq=hq, q=q, kv=kv, att=att, x2=x2,
                        hf=hf, up0=up0, gact=gact)


def _layer_bwd_mlp(dx, dxb, sv, W, V, l, dims, sides):
    Bn, S, Mn, D, C, F = dims
    n = f"l{l}_b_"
    gw, sm = {}, {}
    dgact = sides.mm("d_gact", dxb, W["w_down"], "nt", BF, n + "d_gact", bl=0)
    gw["w_down"] = sides.mm("dw_down", sv["gact"], dxb, "tn", F32, n + "dw_down", twin=BF)
    dg0, dv0, dwg, dwv = sides.run("ffn_act_b", _ffn_act_bwd, sv["up0"], V["ffn_dw_w"][l], dgact, Bn, S, F, n + "ffn_act")
    sm["ffn_dw_w"] = jnp.concatenate([dwg, dwv], axis=1)
    dup0 = jnp.concatenate([dg0, dv0], axis=1)
    dx2, dx2b, sm["ffn_norm_g"] = _mm_rms_bwd(dup0, W["w_up"], sv["x2"], V["ffn_norm_g"][l], dx, n + "d_hf")
    gw["w_up"] = sides.mm("dw_up", sv["hf"], dup0, "tn", F32, n + "dw_up", twin=BF)
    return dx2, dx2b, gw, sm


def _layer_bwd_mix(dx2, dx2b, dmem_n, sv, mem_n, W, V, l, dims, sides):
    Bn, S, Mn, D, C, F = dims
    n = f"l{l}_b_"
    gw, sm = {}, {}
    datt = _mm(dx2b, W["w_o"], "nt", BF, n + "d_att", bl=0)
    gw["w_o"] = _mm(sv["att"], dx2b, "tn", F32, n + "dw_o", twin=BF)
    dq, dk, dv = _attn_bwd(sv["q"], sv["kv"], datt, Bn, S, Mn, D, n + "attn")
    dkv = jnp.concatenate([dk, dv], axis=1)
    gw["w_kv"] = _mm(mem_n, dkv, "tn", F32, n + "dw_kv", twin=BF)
    dmem_n = _mm(dkv, W["w_kv"], "nt", F32, n + "d_mem", res=dmem_n, bl=0)
    dx1, dx1b, sm["xattn_norm_g"] = _mm_rms_bwd(dq, W["w_q"], sv["x1"], V["xattn_norm_g"][l], dx2, n + "d_hq")
    gw["w_q"] = _mm(sv["hq"], dq, "tn", F32, n + "dw_q", twin=BF)
    dmerged = _mm(dx1b, W["w_out"], "nt", BF, n + "d_merged", bl=0)
    gw["w_out"] = _mm(sv["merged"], dx1b, "tn", F32, n + "dw_out", twin=BF)
    dgc, dgp, dyc, dyp, sm["pool_scale"] = sides.run("merge_b", _merge_bwd, sv["proj"], sv["yc"], sv["yp"], V["pool_scale"][l], dmerged, C, n + "merge")
    du, dwp = _pool_bwd(sv["proj"], W["w_pool"], dyp, 0, Bn, S, C, D, n + "pool")
    gw["w_pool"] = (dwp, dwp.astype(BF))
    dyc1 = _mm(dyc, W["w_conv_out"], "nt", F32, n + "d_yc1", bl=0)
    gw["w_conv_out"] = _mm(sv["yc1"], dyc, "tn", F32, n + "dw_conv_out", twin=BF)
    dcv, sm["conv_ln_g"], sm["conv_ln_b"] = _ln_silu_bwd(sv["cv"], V["conv_ln_g"][l], V["conv_ln_b"][l], dyc1, n + "ln_silu")
    da, dgl, sm["conv_dw_w"], sm["conv_dw_b"] = sides.run("glu_conv_b", _glu_conv_bwd, sv["proj"], V["conv_dw_w"][l], dcv, Bn, S, C, n + "glu_conv")
    dproj = jnp.concatenate([da, dgl, du, dgc, dgp], axis=1)
    dx, dxb, sm["mix_norm_g"] = _mm_rms_bwd(dproj, W["w_in"], sv["x"], V["mix_norm_g"][l], dx1, n + "d_h")
    gw["w_in"] = _mm(sv["h"], dproj, "tn", F32, n + "dw_in", twin=BF)
    return dx, dxb, dmem_n, gw, sm


BIG = (("w_in", "col"), ("w_conv_out", "col"), ("w_pool", "row"), ("w_out", "row"), ("w_q", "row"), ("w_kv", "col"),
       ("w_o", "row"), ("w_up", "col"), ("w_down", "row"))
ALL_RELS = (1, 2, 3)
GATHER_FIRST = ("w_in", "w_conv_out", "w_pool")
FWD_CARRY = {
    (0, "proj"): (("w_out", 0, ALL_RELS), ("w_q", 0, ALL_RELS), ("w_o", 0, ALL_RELS)),
    (0, "glu_conv"): (("w_kv", 0, ALL_RELS),),
    (0, "merge"): (("w_up", 0, (1,)),),
    (0, "out_proj"): (("w_down", 0, (1, 2)),),
    (0, "q_proj"): (("w_down", 0, (3,)),),
    (0, "attn"): (("w_up", 0, (2,)),),
    (0, "o_proj"): (("w_up", 0, (3,)),),
    (0, "up_proj"): (("w_in", 1, ALL_RELS), ("w_conv_out", 1, ALL_RELS), ("w_pool", 1, ALL_RELS)),
    (0, "ffn_act"): (("w_out", 1, ALL_RELS), ("w_q", 1, ALL_RELS), ("w_kv", 1, ALL_RELS)),
    (0, "down_proj"): (("w_o", 1, ALL_RELS),),
    (1, "proj"): (("w_down", 1, ALL_RELS),),
    (1, "glu_conv"): (("w_up", 1, (1,)),),
    (1, "merge"): (("w_up", 1, (2,)),),
    (1, "attn"): (("w_up", 1, (3,)),),
}
PASS_GROUPS = ((("w_out", 0), ("w_q", 0), ("w_o", 0), ("w_kv", 0)), (("w_up", 0), ("w_down", 0)),
               (("w_in", 1), ("w_conv_out", 1), ("w_pool", 1), ("w_out", 1), ("w_q", 1), ("w_kv", 1), ("w_o", 1)),
               (("w_up", 1), ("w_down", 1)))
EARLY = ("w_down", "w_up")
BWD_CARRY_EARLY = {"merge_b": ("w_down",), "glu_conv_b": ("w_up",)}
BWD_CARRY_LATE = {"ffn_act_b": ("w_in", "w_conv_out", "w_pool", "w_out", "w_q", "w_kv", "w_o")}


def _place():
    xi, yi, ci = lax.axis_index("x"), lax.axis_index("y"), lax.axis_index("c")
    return xi, yi, ci, 2 * xi + yi


def _chip_peer(xi, yi, ci, r):
    return (xi ^ (r >> 1), yi ^ (r & 1), ci)


def _full_shard(ref, kind, k, cs):
    if kind == "col":
        return ref.at[:, :, :, :, pl.ds(pl.multiple_of(k * cs, cs), cs)]
    return ref.at[:, :, k]


def _gather_weights(shards, kinds):
    n = len(shards)
    outs = []
    for s, kind in zip(shards, kinds):
        L, P, _, RH, CS = s.shape
        outs.append(SDS((L, P, 2, RH, CS * N_CHIPS) if kind == "col" else (L, P, N_CHIPS, 2, RH, CS), s.dtype))
    per = 7

    def body(*refs):
        srcs, fulls, (ssem, rsem) = refs[:n], refs[n:2 * n], refs[2 * n:]
        xi, yi, ci, j = _place()
        sib = (xi, yi, 1 - ci)

        def piece(i, k, c):
            kind, cs = kinds[i], shards[i].shape[-1]
            if kind == "col":
                return fulls[i].at[:, :, c, :, pl.ds(pl.multiple_of(k * cs, cs), cs)]
            return fulls[i].at[:, :, k, c]

        def copy(i, slot, src, dst, dev):
            return pltpu.make_async_remote_copy(src_ref=src, dst_ref=dst, send_sem=ssem.at[per * i + slot], recv_sem=rsem.at[per * i + slot],
                                                device_id=dev, device_id_type=MESH)

        own, first, passed = [], [], []
        for i in range(n):
            for r in (1, 2, 3):
                first.append(copy(i, r - 1, srcs[i].at[:, :, ci], piece(i, j, ci), _chip_peer(xi, yi, ci, r)))
                first[-1].start()
        for i in range(n):
            own.append(copy(i, 6, srcs[i], _full_shard(fulls[i], kinds[i], j, shards[i].shape[-1]), sib))
            own[-1].start()
        for i in range(n):
            for r in (1, 2, 3):
                got = piece(i, j ^ r, ci)
                copy(i, r - 1, got, got, sib).wait_recv()
                passed.append(copy(i, 2 + r, got, got, sib))
                passed[-1].start()
        for i in range(n):
            for r in (1, 2, 3):
                got = piece(i, j ^ r, 1 - ci)
                copy(i, 2 + r, got, got, sib).wait_recv()
        for cp in own:
            cp.wait()
        for cp in first + passed:
            cp.wait_send()

    return pl.pallas_call(
        body, in_specs=[ANY] * n, out_specs=[ANY] * n, out_shape=outs,
        scratch_shapes=[pltpu.SemaphoreType.DMA((per * n,)), pltpu.SemaphoreType.DMA((per * n,))], name="gather_weights")(*shards)


def _full_sds(s, kind):
    L, P, _, RH, CS = s.shape
    return SDS((L, P, 2, RH, CS * N_CHIPS) if kind == "col" else (L, P, N_CHIPS, 2, RH, CS), s.dtype)


def _gather_piece(full, kind, cs, k, c):
    if kind == "col":
        return full.at[:, :, c, :, pl.ds(pl.multiple_of(k * cs, cs), cs)]
    return full.at[:, :, k, c]


def _side_gather(shards, kinds, rels, fulls):
    n = len(shards)

    def make(srcs, outs, ssem, rsem):
        xi, yi, ci, j = _place()
        return [pltpu.make_async_remote_copy(
            src_ref=srcs[i].at[:, :, ci], dst_ref=_gather_piece(outs[i], kinds[i], shards[i].shape[-1], j, ci), send_sem=ssem.at[3 * i + r - 1],
            recv_sem=rsem.at[3 * i + r - 1], device_id=_chip_peer(xi, yi, ci, r), device_id_type=MESH) for i in range(n) for r in rels[i]]

    prior = [f for f in fulls if f is not None]
    assert len(prior) in (0, n)
    return _Side(list(shards) + prior, [_full_sds(s, k) for s, k in zip(shards, kinds)], 3 * n, make, n_alias=len(prior))


def _gather_pass(fulls, shards, kinds, name):
    n = len(fulls)

    def body(*refs):
        srcs, outs, (ssem, rsem) = refs[n:2 * n], refs[2 * n:3 * n], refs[3 * n:]
        xi, yi, ci, j = _place()
        sib = (xi, yi, 1 - ci)
        cps = []
        for i in range(n):
            cs = shards[i].shape[-1]
            for r in (1, 2, 3):
                got = _gather_piece(outs[i], kinds[i], cs, j ^ r, ci)
                cps.append(pltpu.make_async_remote_copy(src_ref=got, dst_ref=got, send_sem=ssem.at[4 * i + r - 1], recv_sem=rsem.at[4 * i + r - 1],
                                                        device_id=sib, device_id_type=MESH))
            cps.append(pltpu.make_async_remote_copy(src_ref=srcs[i], dst_ref=_full_shard(outs[i], kinds[i], j, cs), send_sem=ssem.at[4 * i + 3],
                                                    recv_sem=rsem.at[4 * i + 3], device_id=sib, device_id_type=MESH))
        for cp in cps:
            cp.start()
        for cp in cps:
            cp.wait()

    return pl.pallas_call(
        body, in_specs=[ANY] * (2 * n), out_specs=[ANY] * n, out_shape=[SDS(f.shape, f.dtype) for f in fulls],
        input_output_aliases={i: i for i in range(n)},
        scratch_shapes=[pltpu.SemaphoreType.DMA((4 * n,)), pltpu.SemaphoreType.DMA((4 * n,))], name=name)(*fulls, *shards)


def _sibling_exchange(gviews, kinds, name):
    n = len(gviews)
    outs = [SDS(g.shape[:1] + g.shape[2:] if kind == "col" else g.shape[:2] + g.shape[3:], g.dtype) for g, kind in zip(gviews, kinds)]

    def body(*refs):
        gs, lands, (ssem, rsem) = refs[:n], refs[n:2 * n], refs[2 * n:]
        xi, yi, ci, _ = _place()
        cps = []
        for i in range(n):
            src = gs[i].at[:, 1 - ci] if kinds[i] == "col" else gs[i].at[:, :, 1 - ci]
            cps.append(pltpu.make_async_remote_copy(src_ref=src, dst_ref=lands[i], send_sem=ssem.at[i], recv_sem=rsem.at[i],
                                                    device_id=(xi, yi, 1 - ci), device_id_type=MESH))
            cps[-1].start()
        for cp in cps:
            cp.wait()

    return pl.pallas_call(body, in_specs=[ANY] * n, out_specs=[ANY] * n, out_shape=outs,
                          scratch_shapes=[pltpu.SemaphoreType.DMA((n,)), pltpu.SemaphoreType.DMA((n,))], name=name)(*gviews)


def _chip_sums(gs, lands, kinds, jc, name):
    n = len(gs)
    args, in_specs, out_specs, out_shape = [], [], [], []
    for g, land, kind in zip(gs, lands, kinds):
        if kind == "col":
            P, _, RH, C = g.shape
            CS = C // N_CHIPS
            in_specs += [pl.BlockSpec((P, None, RH, CS), lambda r, jc: (0, jc[1], 0, jc[0] ^ r)),
                         pl.BlockSpec((P, RH, CS), lambda r, jc: (0, 0, jc[0] ^ r))]
        else:
            P, _, _, RH, CS = g.shape
            in_specs += [pl.BlockSpec((P, None, None, RH, CS), lambda r, jc: (0, jc[0] ^ r, jc[1], 0, 0)),
                         pl.BlockSpec((P, None, RH, CS), lambda r, jc: (0, jc[0] ^ r, 0, 0))]
        args += [g, land]
        out_specs += [pl.BlockSpec((P, RH, CS), lambda r, jc: (0, 0, 0)), pl.BlockSpec((None, P, RH, CS), lambda r, jc: (r, 0, 0, 0))]
        out_shape += [SDS((P, RH, CS), F32), SDS((N_CHIPS, P, RH, CS), BF)]

    def body(jc_ref, *refs):
        ins, outs = refs[:2 * n], refs[2 * n:]
        for i in range(n):
            s = ins[2 * i][...] + ins[2 * i + 1][...].astype(F32)
            outs[2 * i + 1][...] = s.astype(BF)

            @pl.when(pl.program_id(0) == 0)
            def _():
                outs[2 * i][...] = s

    outs = _call(body, args, grid=(N_CHIPS,), in_specs=in_specs, out_specs=out_specs, out_shape=out_shape, semantics=("arbitrary",),
                 name=name, prefetch=(jc,))
    return outs[0::2], outs[1::2]


def _chip_exchange_copies(srcs, lands, ssem, rsem):
    xi, yi, ci, _ = _place()
    return [pltpu.make_async_remote_copy(src_ref=srcs[i].at[r], dst_ref=lands[i].at[r], send_sem=ssem.at[3 * i + r - 1],
                                         recv_sem=rsem.at[3 * i + r - 1], device_id=_chip_peer(xi, yi, ci, r), device_id_type=MESH)
            for i in range(len(srcs)) for r in (1, 2, 3)]


def _side_chip_exchange(pieces):
    return _Side(pieces, [SDS(p.shape, p.dtype) for p in pieces], 3 * len(pieces), _chip_exchange_copies)


FINAL_SUM_STEPS = 2


def _final_sums(owns, lands, jc, shards, l, L, name, side=None):
    n = len(owns)
    args, in_specs, out_specs, out_shape = [], [], [], []
    for own, land in zip(owns, lands):
        P, RH, CS = own.shape
        hr = RH // FINAL_SUM_STEPS
        in_specs += [pl.BlockSpec((P, hr, CS), lambda h, jc: (0, h, 0))]
        in_specs += [pl.BlockSpec((None, P, hr, CS), functools.partial(lambda r, h, jc: (r, 0, h, 0), r)) for r in (1, 2, 3)]
        args += [own, land, land, land]
        out_specs.append(pl.BlockSpec((None, P, None, hr, CS), lambda h, jc: (l, 0, jc[1], h, 0)))
        out_shape.append(SDS((L, P, 2, RH, CS), F32))
    aliases = None
    if shards is not None:
        aliases = {4 * n + i: i for i in range(n)}
        in_specs += [ANY] * n
        args += list(shards)

    def body(jc_ref, *refs):
        outs = refs[len(args):]
        for i in range(n):
            o, a, b, c = (refs[4 * i + t][...] for t in range(4))
            outs[i][...] = ((o + a.astype(F32)) + b.astype(F32)) + c.astype(F32)

    return _call(body, args, grid=(FINAL_SUM_STEPS,), in_specs=in_specs, out_specs=out_specs, out_shape=out_shape, semantics=("arbitrary",),
                 name=name, prefetch=(jc,), aliases=aliases, side=side)


def _halves_exchange(shards, l, name):
    n = len(shards)

    def body(*refs):
        outs, (ssem, rsem) = refs[n:2 * n], refs[2 * n:]
        xi, yi, ci, _ = _place()
        cps = []
        for i in range(n):
            mine = outs[i].at[l, :, ci]
            cps.append(pltpu.make_async_remote_copy(src_ref=mine, dst_ref=mine, send_sem=ssem.at[i], recv_sem=rsem.at[i],
                                                    device_id=(xi, yi, 1 - ci), device_id_type=MESH))
            cps[-1].start()
        for i in range(n):
            land = outs[i].at[l, :, 1 - ci]
            pltpu.make_async_remote_copy(src_ref=land, dst_ref=land, send_sem=ssem.at[i], recv_sem=rsem.at[i],
                                         device_id=(xi, yi, 1 - ci), device_id_type=MESH).wait_recv()
        for cp in cps:
            cp.wait_send()

    return pl.pallas_call(body, in_specs=[ANY] * n, out_specs=[ANY] * n, out_shape=[SDS(s.shape, s.dtype) for s in shards],
                          input_output_aliases={i: i for i in range(n)},
                          scratch_shapes=[pltpu.SemaphoreType.DMA((n,)), pltpu.SemaphoreType.DMA((n,))], name=name)(*shards)


def _reduce_small(part, pieces):
    NR, Wd = part.shape
    ND = 2 * N_CHIPS
    n = len(pieces)

    def body(p_ref, *refs):
        srcs, o_ref, lands, (land, ssem, rsem, xs, xr) = refs[:n], refs[n], refs[n + 1:2 * n + 1], refs[2 * n + 1:]
        exchange = _chip_exchange_copies(srcs, lands, xs, xr)
        for cp in exchange:
            cp.start()
        xi, yi, ci, j = _place()
        me = 2 * j + ci
        land[me] = p_ref[...]
        cps = []
        for rr in range(1, ND):
            dev = (xi ^ (rr >> 2), yi ^ ((rr >> 1) & 1), ci ^ (rr & 1))
            cps.append(pltpu.make_async_remote_copy(src_ref=p_ref, dst_ref=land.at[me], send_sem=ssem.at[rr - 1], recv_sem=rsem.at[rr - 1],
                                                    device_id=dev, device_id_type=MESH))
            cps[-1].start()
        for rr in range(1, ND):
            got = land.at[me ^ rr]
            pltpu.make_async_remote_copy(src_ref=got, dst_ref=got, send_sem=ssem.at[rr - 1], recv_sem=rsem.at[rr - 1],
                                         device_id=(xi, yi, ci), device_id_type=MESH).wait_recv()
        acc = land[0]
        for d in range(1, ND):
            acc = acc + land[d]
        o_ref[...] = acc
        for cp in cps:
            cp.wait_send()
        for cp in exchange:
            cp.wait()

    vm = pl.BlockSpec(memory_space=pltpu.VMEM)
    outs = pl.pallas_call(
        body, in_specs=[vm] + [ANY] * n, out_specs=[vm] + [ANY] * n, out_shape=[SDS((NR, Wd), F32)] + [SDS(p.shape, p.dtype) for p in pieces],
        scratch_shapes=[pltpu.VMEM((ND, NR, Wd), F32), pltpu.SemaphoreType.DMA((ND - 1,)), pltpu.SemaphoreType.DMA((ND - 1,)),
                        pltpu.SemaphoreType.DMA((3 * n,)), pltpu.SemaphoreType.DMA((3 * n,))],
        name="small_grad_allreduce")(part, *pieces)
    return outs[0], list(outs[1:])


def _adamw_update(w_ref, g_ref, m_ref, v_ref, d_ref, mo_ref, vo_ref):
    g = g_ref[...]
    m = ADAM_B1 * m_ref[...] + (1.0 - ADAM_B1) * g
    v = ADAM_B2 * v_ref[...] + (1.0 - ADAM_B2) * jnp.square(g)
    m_hat = m / (1.0 - ADAM_B1 ** ADAM_STEP)
    v_hat = v / (1.0 - ADAM_B2 ** ADAM_STEP)
    d_ref[...] = -ADAM_LR * (m_hat / (jnp.sqrt(v_hat) + ADAM_EPS) + ADAM_WD * w_ref[...])
    mo_ref[...] = m
    vo_ref[...] = v


ADAMW_STEPS = 8


def _adamw_layer(ws, gs, ms, vs, prev, l, name, side=None):
    n = len(ws)
    args, in_specs, out_specs, out_shape = [], [], [], []
    for w, g, m, v in zip(ws, gs, ms, vs):
        L, R, C = w.shape
        blk = pl.BlockSpec((None, R // ADAMW_STEPS, C), lambda i: (l, i, 0))
        in_specs += [blk] * 4
        args += [w, g, m, v]
        out_specs += [blk] * 3
        out_shape += [SDS((L, R, C), F32)] * 3
    aliases = None
    if prev is not None:
        aliases = {4 * n + i: i for i in range(3 * n)}
        in_specs += [ANY] * (3 * n)
        args += list(prev)

    def body(*refs):
        outs = refs[len(args):]
        for i in range(n):
            _adamw_update(*refs[4 * i:4 * i + 4], *outs[3 * i:3 * i + 3])

    return _call(body, args, grid=(ADAMW_STEPS,), in_specs=in_specs, out_specs=out_specs, out_shape=out_shape, semantics=("parallel",),
                 name=name, aliases=aliases, side=side)


def _adamw(w, g, m, v, name):
    shape = w.shape
    C = shape[-1]
    R = w.size // C
    tb = _tile(R, max(8, (1 << 18) // C), 8)
    body = functools.partial(_adamw_update)
    blk = pl.BlockSpec((tb, C), lambda i: (i, 0))
    outs = pl.pallas_call(body, grid=(R // tb,), in_specs=[blk] * 4, out_specs=[blk] * 3, out_shape=[SDS((R, C), F32)] * 3,
                          compiler_params=_params("parallel"), name=name)(*[t.reshape(R, C) for t in (w, g, m, v)])
    return [t.reshape(shape) for t in outs]


WEIGHTS = ("mix_norm_g", "w_in", "conv_dw_w", "conv_dw_b", "conv_ln_g", "conv_ln_b", "w_conv_out", "w_pool_grp", "pool_scale", "w_out",
           "xattn_norm_g", "mem_norm_g", "w_q", "w_kv", "w_o", "ffn_norm_g", "w_up", "ffn_dw_w", "w_down", "final_norm_g")
VECTORS = ("mix_norm_g", "conv_dw_b", "conv_ln_g", "conv_ln_b", "pool_scale", "xattn_norm_g", "mem_norm_g", "ffn_norm_g", "final_norm_g")


def _shard_view(t, kind):
    L, P, R, C = t.shape
    return t.reshape(L, P, 2, R // 2, C)


def _rows(t, width):
    return t.reshape(-1, width)


def _pack(parts):
    return jnp.concatenate([jnp.pad(p, ((0, (-p.shape[0]) % 8), (0, 0))) for p in parts], axis=0)


def kernel(x, mem, mix_norm_g, w_in, conv_dw_w, conv_dw_b, conv_ln_g, conv_ln_b, w_conv_out, w_pool_grp, pool_scale, w_out, xattn_norm_g, mem_norm_g, w_q, w_kv, w_o, ffn_norm_g, w_up, ffn_dw_w, w_down, final_norm_g, loss_target, m_mix_norm_g, m_w_in, m_conv_dw_w, m_conv_dw_b, m_conv_ln_g, m_conv_ln_b, m_w_conv_out, m_w_pool_grp, m_pool_scale, m_w_out, m_xattn_norm_g, m_mem_norm_g, m_w_q, m_w_kv, m_w_o, m_ffn_norm_g, m_w_up, m_ffn_dw_w, m_w_down, m_final_norm_g, v_mix_norm_g, v_w_in, v_conv_dw_w, v_conv_dw_b, v_conv_ln_g, v_conv_ln_b, v_w_conv_out, v_w_pool_grp, v_pool_scale, v_w_out, v_xattn_norm_g, v_mem_norm_g, v_w_q, v_w_kv, v_w_o, v_ffn_norm_g, v_w_up, v_ffn_dw_w, v_w_down, v_final_norm_g):
    w = dict(mix_norm_g=mix_norm_g, w_in=w_in, conv_dw_w=conv_dw_w, conv_dw_b=conv_dw_b, conv_ln_g=conv_ln_g, conv_ln_b=conv_ln_b,
             w_conv_out=w_conv_out, w_pool_grp=w_pool_grp, pool_scale=pool_scale, w_out=w_out, xattn_norm_g=xattn_norm_g,
             mem_norm_g=mem_norm_g, w_q=w_q, w_kv=w_kv, w_o=w_o, ffn_norm_g=ffn_norm_g, w_up=w_up, ffn_dw_w=ffn_dw_w, w_down=w_down,
             final_norm_g=final_norm_g)
    m = dict(zip(WEIGHTS, (m_mix_norm_g, m_w_in, m_conv_dw_w, m_conv_dw_b, m_conv_ln_g, m_conv_ln_b, m_w_conv_out, m_w_pool_grp, m_pool_scale,
                           m_w_out, m_xattn_norm_g, m_mem_norm_g, m_w_q, m_w_kv, m_w_o, m_ffn_norm_g, m_w_up, m_ffn_dw_w, m_w_down, m_final_norm_g)))
    v = dict(zip(WEIGHTS, (v_mix_norm_g, v_w_in, v_conv_dw_w, v_conv_dw_b, v_conv_ln_g, v_conv_ln_b, v_w_conv_out, v_w_pool_grp, v_pool_scale,
                           v_w_out, v_xattn_norm_g, v_mem_norm_g, v_w_q, v_w_kv, v_w_o, v_ffn_norm_g, v_w_up, v_ffn_dw_w, v_w_down, v_final_norm_g)))
    xi, yi, ci, j = _place()
    jc = jnp.stack([j, ci]).astype(jnp.int32)
    L = w_in.shape[0]
    G = len(POOL_WINDOWS)
    kinds = dict(BIG)

    def to_mat(name, t):
        if name == "w_pool":
            return jnp.swapaxes(t, 2, 3)
        return t[:, None]

    def from_mat(name, t):
        if name == "w_pool":
            return jnp.swapaxes(t, 2, 3)
        return t[:, 0]

    src = {name: w["w_pool_grp" if name == "w_pool" else name] for name, _ in BIG}

    KC, cs_c = conv_dw_w.shape[1], conv_dw_w.shape[2]
    KF, cs_f = ffn_dw_w.shape[1], ffn_dw_w.shape[2]
    taps = jnp.concatenate([conv_dw_w.reshape(L * KC, cs_c), ffn_dw_w.reshape(L * KF * (cs_f // cs_c), cs_c)], axis=0)
    n_taps = taps.shape[0]
    taps = jnp.pad(taps, ((0, (-n_taps) % 16), (0, 0)))
    names = [name for name, _ in BIG]
    mats = {name: to_mat(name, src[name]).astype(BF) for name in names}

    def layer_shards(l, subset):
        return [_shard_view(mats[name][l:l + 1], kinds[name]) for name in subset]

    def as_weight(name, f):
        return f.reshape(G if name == "w_pool" else 1, -1, f.shape[-1])

    assert L == 2
    fulls = _gather_weights(layer_shards(0, GATHER_FIRST) + [_shard_view(taps[None, None], "row")], [kinds[name] for name in GATHER_FIRST] + ["row"])
    ready = {(name, 0): as_weight(name, f) for name, f in zip(GATHER_FIRST, fulls)}
    landing = {}
    taps_all = fulls[-1].reshape(N_CHIPS, -1, cs_c)[:, :n_taps]
    V = {name: w[name] for name in VECTORS}
    V["conv_dw_w"] = taps_all[:, :L * KC].reshape(N_CHIPS, L, KC, cs_c).transpose(1, 2, 0, 3).reshape(L, KC, N_CHIPS * cs_c)
    V["ffn_dw_w"] = taps_all[:, L * KC:].reshape(N_CHIPS, L, KF, cs_f).transpose(1, 2, 0, 3).reshape(L, KF, N_CHIPS * cs_f)

    Bn, S, D = x.shape
    Mn = mem.shape[1]
    dims = (Bn, S, Mn, D, conv_dw_b.shape[1], w_down.shape[1] * N_CHIPS)
    xt = x.reshape(Bn * S, D)
    memf = mem.reshape(Bn * Mn, D)
    mem_n = _rms_fwd(memf, V["mem_norm_g"], "mem_norm")

    class LayerWeights:
        def __init__(self, l):
            self.l = l

        def __getitem__(self, name):
            if (name, self.l) not in ready:
                group = next(g for g in PASS_GROUPS if (name, self.l) in g)
                done = _gather_pass([landing.pop(t) for t in group], [layer_shards(lw, [nm])[0] for nm, lw in group],
                                    [kinds[nm] for nm, _ in group], f"gather_pass_{name}_{self.l}")
                ready.update({t: as_weight(t[0], f) for t, f in zip(group, done)})
            return ready[(name, self.l)]

    def carried_gather(entries):
        return lambda: _side_gather([layer_shards(lw, [nm])[0] for nm, lw, _ in entries], [kinds[nm] for nm, _, _ in entries],
                                    [rels for _, _, rels in entries], [landing.get((nm, lw)) for nm, lw, _ in entries])

    saved, W = [], []
    ht = _rms_fwd(xt, V["mix_norm_g"][0], "l0_mix_norm")
    for l in range(L):
        mine = {key: entries for (cl, key), entries in FWD_CARRY.items() if cl == l}
        sides = _Sides({key: carried_gather(entries) for key, entries in mine.items()},
                       on_land=lambda key, fulls, mine=mine: landing.update({(nm, lw): f for (nm, lw, _), f in zip(mine[key], fulls)}))
        W.append(LayerWeights(l))
        xt, ht, sv = _layer_fwd(xt, ht, mem_n, W[l], V, l, dims, sides, V["mix_norm_g"][l + 1] if l + 1 < L else None)
        saved.append(sv)
    loss, dx, dgf = _loss_bwd(xt, V["final_norm_g"], loss_target.reshape(Bn * S, D), "loss")
    loss = lax.psum(loss[0, 0], ("x", "y", "c"))

    late_names = [name for name in names if name not in EARLY]

    def chip_sums(gw, subset, l, tag):
        def view(g, name):
            g = g if g.ndim == 3 else g[None]
            P, R, C = g.shape
            return g.reshape(P, 2, R // 2, C) if kinds[name] == "col" else g.reshape(P, N_CHIPS, 2, R // (2 * N_CHIPS), C)

        gv = [view(gw[name][0], name) for name in subset]
        lands = _sibling_exchange([view(gw[name][1], name) for name in subset], [kinds[name] for name in subset],
                                  f"grad_sibling_exchange_{tag}_l{l}")
        own, pieces = _chip_sums(gv, lands, [kinds[name] for name in subset], jc, f"chip_sums_{tag}_l{l}")
        return dict(zip(subset, own)), dict(zip(subset, pieces))

    def carry(table, pieces):
        return _Sides({key: _side_chip_exchange([pieces[name] for name in subset]) for key, subset in table.items()})

    def landed(table, sides):
        return {name: land for key, subset in table.items() for name, land in zip(subset, sides.landed[key])}

    dxb, dmem_n = dx, None
    smalls, owns, got = [None] * L, [{} for _ in range(L)], [{} for _ in range(L)]
    late = None
    for l in reversed(range(L)):
        sides = carry(BWD_CARRY_LATE, late) if late is not None else _Sides()
        dx, dxb, gw, sm = _layer_bwd_mlp(dx, dxb, saved[l], W[l], V, l, dims, sides)
        if late is not None:
            got[l + 1].update(landed(BWD_CARRY_LATE, sides))
        own, early = chip_sums(gw, EARLY, l, "mlp")
        owns[l].update(own)
        sides = carry(BWD_CARRY_EARLY, early)
        dx, dxb, dmem_n, gw, sm2 = _layer_bwd_mix(dx, dxb, dmem_n, saved[l], mem_n, W[l], V, l, dims, sides)
        got[l].update(landed(BWD_CARRY_EARLY, sides))
        smalls[l] = {**sm, **sm2}
        own, late = chip_sums(gw, late_names, l, "mix")
        owns[l].update(own)
    grad_x = dx.reshape(Bn, S, D)
    _, _, dgm = _rms_bwd(memf, V["mem_norm_g"], dmem_n, None, "mem_norm_b")
    small = {k: jnp.stack([sm[k] for sm in smalls]) if k in ("conv_dw_w", "ffn_dw_w") else jnp.concatenate([sm[k] for sm in smalls], axis=0)
             for k in smalls[0]}
    small["mem_norm_g"] = dgm
    small["final_norm_g"] = dgf

    small_w = conv_dw_b.shape[1]
    order = VECTORS + ("conv_dw_w", "ffn_dw_w")
    parts = [_rows(small[name], small_w) for name in order]
    counts = [p.shape[0] for p in parts]
    summed, landed_late = _reduce_small(_pack(parts), [late[name] for name in late_names])
    got[0].update(zip(late_names, landed_late))

    keys = ["w_pool_grp" if name == "w_pool" else name for name in names]
    rows3 = lambda t: t.reshape(t.shape[0], -1, t.shape[-1])
    wmv = [[rows3(to_mat(name, d[key])) for name, key in zip(names, keys)] for d in (w, m, v)]
    gshards, updates = None, None
    for l in reversed(range(L)):
        gshards = _final_sums([owns[l][name] for name in names], [got[l][name] for name in names], jc, gshards, l, L, f"final_sums_l{l}")
        gshards = _halves_exchange(gshards, l, f"grad_halves_exchange_l{l}")
        updates = _adamw_layer(wmv[0], [rows3(t) for t in gshards], wmv[1], wmv[2], updates, l, f"adamw_l{l}")
    grads, delta, new_m, new_v = {}, {}, {}, {}
    for i, (name, key) in enumerate(zip(names, keys)):
        Lg, P, _, RH, CS = gshards[i].shape
        grads[key] = from_mat(name, gshards[i].reshape(Lg, P, 2 * RH, CS))
        for d, t in zip((delta, new_m, new_v), updates[3 * i:3 * i + 3]):
            d[key] = from_mat(name, t.reshape(Lg, P, 2 * RH, CS))

    off = 0
    for name, cnt in zip(order, counts):
        t = summed[off:off + cnt]
        off += cnt + (-cnt) % 8
        if name in VECTORS:
            grads[name] = t.reshape(w[name].shape)
        else:
            full = t.reshape(small[name].shape)
            cs = w[name].shape[2]
            grads[name] = lax.dynamic_slice_in_dim(full, j * cs, cs, axis=2)

    vec =[_pack([_rows(d[name], small_w) for name in VECTORS]) for d in (w, grads, m, v)]
    outs = _adamw(*vec, "adamw_vectors")
    off = 0
    for name in VECTORS:
        cnt = w[name].size // small_w
        for d, t in zip((delta, new_m, new_v), outs):
            d[name] = t[off:off + cnt].reshape(w[name].shape)
        off += cnt + (-cnt) % 8
    for name in ("conv_dw_w", "ffn_dw_w"):
        delta[name], new_m[name], new_v[name] = _adamw(w[name], grads[name], m[name], v[name], "adamw_" + name)

    return (loss, grad_x, *[grads[k] for k in WEIGHTS], *[delta[k] for k in WEIGHTS], *[new_m[k] for k in WEIGHTS], *[new_v[k] for k in WEIGHTS])
```

```python
import functools

import jax
import jax.numpy as jnp
from jax import lax
from jax.experimental import pallas as pl
from jax.experimental.pallas import tpu as pltpu

F32 = jnp.float32
BF = jnp.bfloat16
SDS = jax.ShapeDtypeStruct
MESH = pl.DeviceIdType.MESH
ANY = pl.BlockSpec(memory_space=pl.ANY)

EPS = 1e-6
XA_HEADS = 4
POOL_WINDOWS = (2, 4, 8, 16)
N_CHIPS = 4
ADAM_LR, ADAM_B1, ADAM_B2, ADAM_EPS, ADAM_WD, ADAM_STEP = 0.001, 0.9, 0.999, 1e-08, 0.01, 10

LANES = 128
ROW_BLOCK = 512
VMEM_LIMIT = 56 * 1024 * 1024


def _params(*sem):
    return pltpu.CompilerParams(dimension_semantics=sem if sem else None, vmem_limit_bytes=VMEM_LIMIT)


def _tile(n, cap, mult=LANES):
    if n <= cap:
        return n
    for t in range(cap - cap % mult, 0, -mult):
        if n % t == 0:
            return t
    return n


_DN = {"nn": (((1,), (0,)), ((), ())), "nt": (((1,), (1,)), ((), ())), "tn": (((0,), (0,)), ((), ()))}


class _Side:
    def __init__(self, ins, outs, n, make, n_alias=0):
        self.ins, self.outs, self.n, self.make, self.n_alias = list(ins), list(outs), n, make, n_alias


def _call(body, args, *, grid, in_specs, out_specs, out_shape, semantics, name, scratch_shapes=(), side=None, prefetch=(), aliases=None):
    n_pf = len(prefetch)
    aliases = {n_pf + i: o for i, o in (aliases or {}).items()}
    n_in, n_out, n_scr = len(args), len(out_shape), len(scratch_shapes)
    n_si, n_so = (len(side.ins), len(side.outs)) if side is not None else (0, 0)
    if side is not None:
        aliases.update({n_pf + n_in + n_si - side.n_alias + i: n_out + i for i in range(side.n_alias)})

    def carrying(*refs):
        pf, refs = refs[:n_pf], refs[n_pf:]
        ins, s_in = refs[:n_in], refs[n_in:n_in + n_si]
        outs, s_out = refs[n_in + n_si:n_in + n_si + n_out], refs[n_in + n_si + n_out:n_in + n_si + n_out + n_so]
        scr = refs[n_in + n_si + n_out + n_so:]
        if side is None:
            return body(*pf, *ins, *outs, *scr)
        copies = side.make(s_in, s_out, scr[n_scr], scr[n_scr + 1])
        ids = [pl.program_id(d) for d in range(len(grid))]
        first, last = ids[0] == 0, ids[0] == grid[0] - 1
        for d in range(1, len(grid)):
            first, last = first & (ids[d] == 0), last & (ids[d] == grid[d] - 1)

        @pl.when(first)
        def _():
            for cp in copies:
                cp.start()

        body(*pf, *ins, *outs, *scr[:n_scr])

        @pl.when(last)
        def _():
            for cp in copies:
                cp.wait()

    sems = [pltpu.SemaphoreType.DMA((side.n,)), pltpu.SemaphoreType.DMA((side.n,))] if side is not None else []
    outs = pl.pallas_call(
        carrying, grid_spec=pltpu.PrefetchScalarGridSpec(
            num_scalar_prefetch=n_pf, grid=grid, in_specs=list(in_specs) + [ANY] * n_si, out_specs=list(out_specs) + [ANY] * n_so,
            scratch_shapes=list(scratch_shapes) + sems),
        out_shape=list(out_shape) + (side.outs if side is not None else []), input_output_aliases=aliases,
        compiler_params=_params(*(semantics if side is None else ["arbitrary"] * len(grid))), name=name)(
            *prefetch, *args, *(side.ins if side is not None else []))
    return list(outs) if side is None else (list(outs[:n_out]), list(outs[n_out:]))


def _call1(body, args, *, out_spec, out_shape, side=None, **kw):
    got = _call(body, args, out_specs=[out_spec], out_shape=[out_shape], side=side, **kw)
    return got[0] if side is None else (got[0][0], got[1])


MM_VMEM_BUDGET = 40 * 1024 * 1024
MM_STEP_MACS = 2200 * 1024 * 1024
MXU_WIDTH = 256
MM_STEP_COST_BYTES = 1 << 20


def _divisors(n):
    return [t for t in range(LANES, n + 1, LANES) if n % t == 0] or [n]


def _mm_tiles(M, N, K, a_bytes, b_bytes, o_bytes):
    best = None
    for tk in _divisors(K):
        for tm in _divisors(M):
            for tn in _divisors(N):
                nk = K // tk
                foot = 2 * (tm * tk * a_bytes + tk * tn * b_bytes + tm * tn * o_bytes) + (tm * tn * 4 if nk > 1 else 0)
                if foot > MM_VMEM_BUDGET or tm * tn * tk > MM_STEP_MACS or tn < min(N, MXU_WIDTH) or tm < min(M, MXU_WIDTH):
                    continue
                steps = (M // tm) * (N // tn) * nk
                traffic = M * K * a_bytes * (N // tn if nk > 1 else 1) + K * N * b_bytes * (M // tm) + M * N * o_bytes
                exposed = tm * tk * a_bytes + tk * tn * b_bytes + tm * tn * o_bytes
                cost = traffic + exposed + steps * MM_STEP_COST_BYTES + (nk - 1) * M * N * 8
                if best is None or cost < best[0]:
                    best = (cost, tm, tn, tk)
    assert best is not None, (M, N, K)
    return best[1:]


def _mm(a, b, dims, out_dtype, name, res=None, bl=None, side=None, twin=None):
    bs = b.shape[1:] if bl is not None else b.shape
    if dims == "nn":
        (M, K), (K2, N) = a.shape, bs
    elif dims == "nt":
        (M, K), (N, K2) = a.shape, bs
    else:
        (K, M), (K2, N) = a.shape, bs
    assert K == K2, (name, a.shape, b.shape)
    tm, tn, tk = _mm_tiles(M, N, K, a.dtype.itemsize, b.dtype.itemsize, jnp.dtype(out_dtype).itemsize
                           + (res.dtype.itemsize if res is not None else 0) + (jnp.dtype(twin).itemsize if twin is not None else 0))
    nk = K // tk
    lead = (None,) if bl is not None else ()
    pre = (lambda *ix: (bl,) + ix) if bl is not None else (lambda *ix: ix)
    if dims == "tn":
        a_spec = pl.BlockSpec((tk, tm), lambda i, j, k: (k, i))
    else:
        a_spec = pl.BlockSpec((tm, tk), lambda i, j, k: (i, k))
    if dims == "nt":
        b_spec = pl.BlockSpec(lead + (tn, tk), lambda i, j, k: pre(j, k))
    else:
        b_spec = pl.BlockSpec(lead + (tk, tn), lambda i, j, k: pre(k, j))
    o_spec = pl.BlockSpec((tm, tn), lambda i, j, k: (i, j))
    in_specs, args = [a_spec, b_spec], [a, b]
    if res is not None:
        in_specs.append(o_spec)
        args.append(res)
    n_main = len(args)
    n_out = 1 if twin is None else 2

    def body(*refs):
        a_ref, b_ref = refs[0], refs[1]
        r_ref = refs[2] if res is not None else None
        o_ref = refs[n_main]
        p = lax.dot_general(a_ref[...].astype(BF), b_ref[...].astype(BF), _DN[dims], preferred_element_type=F32)

        def finish(t):
            if r_ref is not None:
                t = t + r_ref[...]
            o_ref[...] = t.astype(out_dtype)
            if twin is not None:
                refs[n_main + 1][...] = t.astype(twin)

        if nk == 1:
            finish(p)
        else:
            acc = refs[n_main + n_out]
            k = pl.program_id(2)

            @pl.when(k == 0)
            def _():
                acc[...] = p

            @pl.when(k > 0)
            def _():
                acc[...] += p

            @pl.when(k == nk - 1)
            def _():
                finish(acc[...])

    got = _call(body, args, grid=(M // tm, N // tn, nk), in_specs=in_specs, out_specs=[o_spec] * n_out,
                out_shape=[SDS((M, N), out_dtype)] + ([SDS((M, N), twin)] if twin is not None else []),
                scratch_shapes=[pltpu.VMEM((tm, tn), F32)] if nk > 1 else [], semantics=("parallel", "parallel", "arbitrary"),
                name=name, side=side)
    outs, landed = (got, None) if side is None else got
    out = outs[0] if twin is None else (outs[0], outs[1])
    return out if side is None else (out, landed)


def _rms(x, g):
    return x * lax.rsqrt(jnp.mean(x * x, axis=-1, keepdims=True) + EPS) * g


def _ln_silu(x, g, b):
    mu = jnp.mean(x, axis=-1, keepdims=True)
    xc = x - mu
    var = jnp.mean(xc * xc, axis=-1, keepdims=True)
    return jax.nn.silu(xc * lax.rsqrt(var + EPS) * g + b)


def _merge(gc, gp, yc, yp, ps):
    return jax.nn.sigmoid(gc) * yc + jax.nn.sigmoid(gp) * (yp * ps)


def _gated(gate, val):
    return jax.nn.gelu(gate) * val


def _rms_fwd(x, g, name):
    T, D = x.shape
    tb = _tile(T, ROW_BLOCK, 8)

    def body(x_ref, g_ref, o_ref):
        o_ref[...] = _rms(x_ref[...], g_ref[...]).astype(BF)

    row = pl.BlockSpec((tb, D), lambda i: (i, 0))
    return pl.pallas_call(body, grid=(T // tb,), in_specs=[row, pl.BlockSpec((1, D), lambda i: (0, 0))], out_specs=row,
                          out_shape=SDS((T, D), BF), compiler_params=_params("parallel"), name=name)(x, g.reshape(1, D))


def _rms_bwd(x, g, dh, dres, name):
    T, D = x.shape
    tb = _tile(T, ROW_BLOCK, 8)

    def body(*refs):
        if dres is not None:
            x_ref, g_ref, dh_ref, dres_ref, dx_ref, dxb_ref, dg_ref = refs
        else:
            x_ref, g_ref, dh_ref, dx_ref, dxb_ref, dg_ref = refs
        _, vjp = jax.vjp(_rms, x_ref[...], g_ref[...])
        dx, dg = vjp(dh_ref[...].astype(F32))
        if dres is not None:
            dx = dx + dres_ref[...]
        dx_ref[...] = dx
        dxb_ref[...] = dx.astype(BF)

        @pl.when(pl.program_id(0) == 0)
        def _():
            dg_ref[...] = jnp.zeros_like(dg_ref)

        dg_ref[...] += dg

    row = pl.BlockSpec((tb, D), lambda i: (i, 0))
    vec = pl.BlockSpec((1, D), lambda i: (0, 0))
    ins = [x, g.reshape(1, D), dh] + ([dres] if dres is not None else [])
    return pl.pallas_call(
        body, grid=(T // tb,), in_specs=[row, vec, row] + ([row] if dres is not None else []), out_specs=[row, row, vec],
        out_shape=[SDS((T, D), F32), SDS((T, D), BF), SDS((1, D), F32)], compiler_params=_params("arbitrary"), name=name)(*ins)


def _row_tile(M, K, N, per_row_bytes):
    fixed = K * N * 2
    fit = [t for t in _divisors(M) if fixed + 2 * t * per_row_bytes <= MM_VMEM_BUDGET and t * K * N <= MM_STEP_MACS]
    return max(fit) if fit else min(_divisors(M))


def _mm_rms_fwd(a, b, res, g, name, side=None):
    M, K = a.shape
    N = b.shape[2]
    tm = _row_tile(M, K, N, K * 2 + N * (4 + 4 + 2))

    def body(a_ref, b_ref, r_ref, g_ref, x_ref, h_ref):
        x = r_ref[...] + lax.dot_general(a_ref[...], b_ref[...], _DN["nn"], preferred_element_type=F32)
        x_ref[...] = x
        h_ref[...] = _rms(x, g_ref[...]).astype(BF)

    row = pl.BlockSpec((tm, N), lambda i: (i, 0))
    return _call(body, (a, b, res, g.reshape(1, N)), grid=(M // tm,),
                 in_specs=[pl.BlockSpec((tm, K), lambda i: (i, 0)), pl.BlockSpec((None, K, N), lambda i: (0, 0, 0), pipeline_mode=pl.Buffered(1)), row,
                           pl.BlockSpec((1, N), lambda i: (0, 0))],
                 out_specs=[row, row], out_shape=[SDS((M, N), F32), SDS((M, N), BF)], semantics=("parallel",), name=name, side=side)


def _mm_rms_bwd(a, b, x, g, dres, name):
    M, K = a.shape
    N = b.shape[1]
    tm = _row_tile(M, K, N, K * 2 + N * (4 + 4 + 4 + 2))

    def body(a_ref, b_ref, x_ref, g_ref, r_ref, dx_ref, dxb_ref, dg_ref):
        dh = lax.dot_general(a_ref[...], b_ref[...], _DN["nt"], preferred_element_type=F32)
        _, vjp = jax.vjp(_rms, x_ref[...], g_ref[...])
        dx, dg = vjp(dh)
        dx = dx + r_ref[...]
        dx_ref[...] = dx
        dxb_ref[...] = dx.astype(BF)

        @pl.when(pl.program_id(0) == 0)
        def _():
            dg_ref[...] = jnp.zeros_like(dg_ref)

        dg_ref[...] += dg

    row = pl.BlockSpec((tm, N), lambda i: (i, 0))
    vec = pl.BlockSpec((1, N), lambda i: (0, 0))
    return pl.pallas_call(
        body, grid=(M // tm,),
        in_specs=[pl.BlockSpec((tm, K), lambda i: (i, 0)), pl.BlockSpec((None, N, K), lambda i: (0, 0, 0), pipeline_mode=pl.Buffered(1)), row, vec, row],
        out_specs=[row, row, vec], out_shape=[SDS((M, N), F32), SDS((M, N), BF), SDS((1, N), F32)],
        compiler_params=_params("arbitrary"), name=name)(a, b, x, g.reshape(1, N), dres)


def _loss_bwd(x, g, target, name):
    T, D = x.shape
    tb = _tile(T, ROW_BLOCK, 8)
    nb = T // tb

    def body(x_ref, g_ref, t_ref, loss_ref, dx_ref, dg_ref, acc):
        i = pl.program_id(0)
        y, vjp = jax.vjp(_rms, x_ref[...], g_ref[...])
        err = y - t_ref[...]
        dx, dg = vjp(err * (1.0 / D))
        dx_ref[...] = dx

        @pl.when(i == 0)
        def _():
            dg_ref[...] = jnp.zeros_like(dg_ref)
            acc[...] = jnp.zeros_like(acc)

        dg_ref[...] += dg
        acc[...] += jnp.sum(err * err, axis=0, keepdims=True)

        @pl.when(i == nb - 1)
        def _():
            loss_ref[...] = jnp.full(loss_ref.shape, (0.5 / D) * jnp.sum(acc[...]), F32)

    row = pl.BlockSpec((tb, D), lambda i: (i, 0))
    vec = pl.BlockSpec((1, D), lambda i: (0, 0))
    return pl.pallas_call(
        body, grid=(nb,), in_specs=[row, vec, row], out_specs=[pl.BlockSpec((1, LANES), lambda i: (0, 0)), row, vec],
        out_shape=[SDS((1, LANES), F32), SDS((T, D), F32), SDS((1, D), F32)], scratch_shapes=[pltpu.VMEM((1, D), F32)],
        compiler_params=_params("arbitrary"), name=name)(x, g.reshape(1, D), target)


def _ln_silu_fwd(cv, g, b, name):
    T, C = cv.shape
    tb = _tile(T, ROW_BLOCK, 8)

    def body(x_ref, g_ref, b_ref, o_ref):
        o_ref[...] = _ln_silu(x_ref[...], g_ref[...], b_ref[...]).astype(BF)

    row = pl.BlockSpec((tb, C), lambda i: (i, 0))
    vec = pl.BlockSpec((1, C), lambda i: (0, 0))
    return pl.pallas_call(body, grid=(T // tb,), in_specs=[row, vec, vec], out_specs=row, out_shape=SDS((T, C), BF),
                          compiler_params=_params("parallel"), name=name)(cv, g.reshape(1, C), b.reshape(1, C))


def _ln_silu_bwd(cv, g, b, dy, name):
    T, C = cv.shape
    tb = _tile(T, ROW_BLOCK, 8)

    def body(x_ref, g_ref, b_ref, dy_ref, dx_ref, dg_ref, db_ref):
        _, vjp = jax.vjp(_ln_silu, x_ref[...], g_ref[...], b_ref[...])
        dx, dg, db = vjp(dy_ref[...].astype(F32))
        dx_ref[...] = dx

        @pl.when(pl.program_id(0) == 0)
        def _():
            dg_ref[...] = jnp.zeros_like(dg_ref)
            db_ref[...] = jnp.zeros_like(db_ref)

        dg_ref[...] += dg
        db_ref[...] += db

    row = pl.BlockSpec((tb, C), lambda i: (i, 0))
    vec = pl.BlockSpec((1, C), lambda i: (0, 0))
    return pl.pallas_call(
        body, grid=(T // tb,), in_specs=[row, vec, vec, row], out_specs=[row, vec, vec],
        out_shape=[SDS((T, C), F32), SDS((1, C), F32), SDS((1, C), F32)], compiler_params=_params("arbitrary"),
        name=name)(cv, g.reshape(1, C), b.reshape(1, C), dy)


def _merge_fwd(proj, yc, yp, ps, C, name, side=None):
    T, D = yc.shape
    tb = _tile(T, ROW_BLOCK, 8)
    nj = D // C

    def body(gc_ref, gp_ref, yc_ref, yp_ref, ps_ref, o_ref):
        o_ref[...] = _merge(gc_ref[...], gp_ref[...], yc_ref[...].astype(F32), yp_ref[...].astype(F32), ps_ref[...]).astype(BF)

    blk = pl.BlockSpec((tb, C), lambda i, j: (i, j))
    return _call1(
        body, (proj, proj, yc, yp, ps.reshape(1, D)), grid=(T // tb, nj),
        in_specs=[pl.BlockSpec((tb, C), lambda i, j: (i, 3 + j)), pl.BlockSpec((tb, C), lambda i, j: (i, 3 + nj + j)), blk, blk,
                  pl.BlockSpec((1, C), lambda i, j: (0, j))],
        out_spec=blk, out_shape=SDS((T, D), BF), semantics=("parallel", "parallel"), name=name, side=side)


def _merge_bwd(proj, yc, yp, ps, dm, C, name, side=None):
    T, D = yc.shape
    tb = _tile(T, ROW_BLOCK, 8)
    nj = D // C

    def body(gc_ref, gp_ref, yc_ref, yp_ref, ps_ref, dm_ref, dgc_ref, dgp_ref, dyc_ref, dyp_ref, dps_ref):
        _, vjp = jax.vjp(_merge, gc_ref[...], gp_ref[...], yc_ref[...].astype(F32), yp_ref[...].astype(F32), ps_ref[...])
        dgc, dgp, dyc, dyp, dps = vjp(dm_ref[...].astype(F32))
        dgc_ref[...] = dgc.astype(BF)
        dgp_ref[...] = dgp.astype(BF)
        dyc_ref[...] = dyc.astype(BF)
        dyp_ref[...] = dyp.astype(BF)

        @pl.when(pl.program_id(1) == 0)
        def _():
            dps_ref[...] = jnp.zeros_like(dps_ref)

        dps_ref[...] += dps

    blk = pl.BlockSpec((tb, C), lambda j, i: (i, j))
    vec = pl.BlockSpec((1, C), lambda j, i: (0, j))
    return _call(
        body, (proj, proj, yc, yp, ps.reshape(1, D), dm), grid=(nj, T // tb),
        in_specs=[pl.BlockSpec((tb, C), lambda j, i: (i, 3 + j)), pl.BlockSpec((tb, C), lambda j, i: (i, 3 + nj + j)), blk, blk, vec, blk],
        out_specs=[blk, blk, blk, blk, vec], out_shape=[SDS((T, D), BF)] * 4 + [SDS((1, D), F32)],
        semantics=("parallel", "arbitrary"), name=name, side=side)


def _shd(v, s, rows):
    if s == 0:
        return v
    return jnp.where(rows >= s, pltpu.roll(v, s, 0), 0.0)


def _shu(v, s, rows):
    if s == 0:
        return v
    n = v.shape[0]
    return jnp.where(rows < n - s, pltpu.roll(v, n - s, 0), 0.0)


def _glu_conv_fwd(proj, w, b, Bn, S, C, name, side=None):
    K = w.shape[0]
    sl = min(LANES, C)
    ns = C // sl

    def body(a_ref, gl_ref, w_ref, b_ref, o_ref):
        y0 = a_ref[...] * jax.nn.sigmoid(gl_ref[...])
        rows = lax.broadcasted_iota(jnp.int32, y0.shape, 0)
        acc = jnp.zeros_like(y0) + b_ref[...]
        for k in range(K):
            acc = acc + w_ref[k:k + 1, :] * _shd(y0, K - 1 - k, rows)
        o_ref[...] = acc

    return _call1(
        body, (proj, proj, w, b.reshape(1, C)), grid=(Bn, ns),
        in_specs=[pl.BlockSpec((S, sl), lambda bi, j: (bi, j)), pl.BlockSpec((S, sl), lambda bi, j: (bi, ns + j)),
                  pl.BlockSpec((K, sl), lambda bi, j: (0, j)), pl.BlockSpec((1, sl), lambda bi, j: (0, j))],
        out_spec=pl.BlockSpec((S, sl), lambda bi, j: (bi, j)), out_shape=SDS((Bn * S, C), F32),
        semantics=("parallel", "parallel"), name=name, side=side)


def _glu_conv_bwd(proj, w, dcv, Bn, S, C, name, side=None):
    K = w.shape[0]
    sl = min(LANES, C)
    ns = C // sl

    def body(a_ref, gl_ref, w_ref, d_ref, da_ref, dgl_ref, dw_ref, db_ref):
        a = a_ref[...]
        sg = jax.nn.sigmoid(gl_ref[...])
        y0 = a * sg
        d = d_ref[...]
        rows = lax.broadcasted_iota(jnp.int32, y0.shape, 0)

        @pl.when(pl.program_id(1) == 0)
        def _():
            dw_ref[...] = jnp.zeros_like(dw_ref)
            db_ref[...] = jnp.zeros_like(db_ref)

        dy0 = jnp.zeros_like(y0)
        for k in range(K):
            s = K - 1 - k
            dw_ref[k:k + 1, :] += jnp.sum(d * _shd(y0, s, rows), axis=0, keepdims=True)
            dy0 = dy0 + w_ref[k:k + 1, :] * _shu(d, s, rows)
        db_ref[...] += jnp.sum(d, axis=0, keepdims=True)
        da_ref[...] = (dy0 * sg).astype(BF)
        dgl_ref[...] = (dy0 * a * sg * (1.0 - sg)).astype(BF)

    blk = pl.BlockSpec((S, sl), lambda j, bi: (bi, j))
    return _call(
        body, (proj, proj, w, dcv), grid=(ns, Bn),
        in_specs=[blk, pl.BlockSpec((S, sl), lambda j, bi: (bi, ns + j)), pl.BlockSpec((K, sl), lambda j, bi: (0, j)), blk],
        out_specs=[blk, blk, pl.BlockSpec((K, sl), lambda j, bi: (0, j)), pl.BlockSpec((1, sl), lambda j, bi: (0, j))],
        out_shape=[SDS((Bn * S, C), BF), SDS((Bn * S, C), BF), SDS((K, C), F32), SDS((1, C), F32)],
        semantics=("parallel", "arbitrary"), name=name, side=side)


def _pool_z(u, g, rows):
    s2 = u + _shd(u, 1, rows)
    s4 = s2 + _shd(s2, 2, rows)
    s8 = s4 + _shd(s4, 4, rows)
    s16 = s8 + _shd(s8, 8, rows)
    sw = jnp.where(g == 0, s2, jnp.where(g == 1, s4, jnp.where(g == 2, s8, s16)))
    cnt = jnp.minimum(rows + 1, POOL_WINDOWS[0] << g).astype(F32)
    return sw / cnt - u, cnt


def _pool_fwd(proj, wpt, l, Bn, S, C, D, name):
    G = len(POOL_WINDOWS)
    gd, go = C // G, D // G

    def body(u_ref, w_ref, o_ref):
        g = pl.program_id(1)
        u = u_ref[...]
        rows = lax.broadcasted_iota(jnp.int32, u.shape, 0)
        zp, _ = _pool_z(u, g, rows)
        o_ref[...] = lax.dot_general(zp.astype(BF), w_ref[...], _DN["nt"], preferred_element_type=F32).astype(BF)

    return pl.pallas_call(
        body, grid=(Bn, G),
        in_specs=[pl.BlockSpec((S, gd), lambda bi, g: (bi, 2 * G + g)), pl.BlockSpec((None, go, gd), lambda bi, g: (l * G + g, 0, 0))],
        out_specs=pl.BlockSpec((S, go), lambda bi, g: (bi, g)), out_shape=SDS((Bn * S, D), BF),
        compiler_params=_params("parallel", "parallel"), name=name)(proj, wpt)


def _pool_bwd(proj, wpt, dyp, l, Bn, S, C, D, name):
    G = len(POOL_WINDOWS)
    gd, go = C // G, D // G

    def body(u_ref, w_ref, d_ref, du_ref, dw_ref):
        g = pl.program_id(0)
        u = u_ref[...]
        rows = lax.broadcasted_iota(jnp.int32, u.shape, 0)
        zp, cnt = _pool_z(u, g, rows)
        d = d_ref[...]
        dzp = lax.dot_general(d, w_ref[...], _DN["nn"], preferred_element_type=F32)

        @pl.when(pl.program_id(1) == 0)
        def _():
            dw_ref[...] = jnp.zeros_like(dw_ref)

        dw_ref[...] += lax.dot_general(d, zp.astype(BF), _DN["tn"], preferred_element_type=F32)
        dsw = dzp / cnt
        zero = jnp.zeros_like(dsw)
        d16 = jnp.where(g == 3, dsw, zero)
        d8 = jnp.where(g == 2, dsw, zero) + d16 + _shu(d16, 8, rows)
        d4 = jnp.where(g == 1, dsw, zero) + d8 + _shu(d8, 4, rows)
        d2 = jnp.where(g == 0, dsw, zero) + d4 + _shu(d4, 2, rows)
        d1 = d2 + _shu(d2, 1, rows)
        du_ref[...] = (d1 - dzp).astype(BF)

    return pl.pallas_call(
        body, grid=(G, Bn),
        in_specs=[pl.BlockSpec((S, gd), lambda g, bi: (bi, 2 * G + g)), pl.BlockSpec((None, go, gd), lambda g, bi: (l * G + g, 0, 0)),
                  pl.BlockSpec((S, go), lambda g, bi: (bi, g))],
        out_specs=[pl.BlockSpec((S, gd), lambda g, bi: (bi, g)), pl.BlockSpec((None, go, gd), lambda g, bi: (g, 0, 0))],
        out_shape=[SDS((Bn * S, C), BF), SDS((G, go, gd), F32)],
        compiler_params=_params("parallel", "arbitrary"), name=name)(proj, wpt, dyp)


def _ffn_conv(u, w_ref, rows):
    K = w_ref.shape[0]
    acc = w_ref[K - 1:K, :] * u
    for k in range(K - 1):
        acc = acc + w_ref[k:k + 1, :] * _shd(u, K - 1 - k, rows)
    return acc


def _ffn_cb(F):
    return _tile(F, 256)


def _ffn_act_fwd(up0, w, Bn, S, F, name, side=None):
    cb = _ffn_cb(F)
    nj = F // cb

    def body(g_ref, v_ref, wg_ref, wv_ref, o_ref):
        rows = lax.broadcasted_iota(jnp.int32, g_ref.shape, 0)
        o_ref[...] = _gated(_ffn_conv(g_ref[...], wg_ref, rows), _ffn_conv(v_ref[...], wv_ref, rows)).astype(BF)

    K = w.shape[0]
    return _call1(
        body, (up0, up0, w, w), grid=(Bn, nj),
        in_specs=[pl.BlockSpec((S, cb), lambda bi, j: (bi, j)), pl.BlockSpec((S, cb), lambda bi, j: (bi, nj + j)),
                  pl.BlockSpec((K, cb), lambda bi, j: (0, j)), pl.BlockSpec((K, cb), lambda bi, j: (0, nj + j))],
        out_spec=pl.BlockSpec((S, cb), lambda bi, j: (bi, j)), out_shape=SDS((Bn * S, F), BF),
        semantics=("parallel", "parallel"), name=name, side=side)


SUBLANES = 8
FFN_HALO = SUBLANES
FFN_ROWS = 64
GELU_C0, GELU_C1 = 0.7978845608028654, 0.044715


def _gelu_and_grad(x):
    x2 = x * x
    t = jnp.tanh(GELU_C0 * (x + GELU_C1 * (x2 * x)))
    cdf = 0.5 * (1.0 + t)
    return x * cdf, cdf + (0.5 * GELU_C0) * x * (1.0 - t * t) * (1.0 + (3.0 * GELU_C1) * x2)


def _ffn_act_bwd(up0, w, dg, Bn, S, F, name, side=None):
    cb = min(LANES, F)
    nj = F // cb
    K = w.shape[0]
    rc = FFN_ROWS if S % FFN_ROWS == 0 else S
    win = rc + 2 * FFN_HALO
    assert K - 1 <= FFN_HALO and rc % SUBLANES == 0

    def body(g_ref, v_ref, wg_ref, wv_ref, d_ref, dgo_ref, dvo_ref, dwg_ref, dwv_ref, gp, vp, dp):
        for pad, src in ((gp, g_ref), (vp, v_ref), (dp, d_ref)):
            pad[0:FFN_HALO, :] = jnp.zeros((FFN_HALO, cb), F32)
            pad[FFN_HALO + S:, :] = jnp.zeros((FFN_HALO, cb), F32)
            pad[FFN_HALO:FFN_HALO + S, :] = src[...].astype(F32)
        wg = [wg_ref[k:k + 1, :] for k in range(K)]
        wv = [wv_ref[k:k + 1, :] for k in range(K)]

        def taps(u):
            return [pltpu.roll(u, K - 1 - k, 0) for k in range(K - 1)] + [u]

        def conv(us, ws):
            acc = ws[K - 1] * us[K - 1]
            for k in range(K - 1):
                acc = acc + ws[k] * us[k]
            return acc

        def conv_t(dc, ws):
            acc = ws[K - 1] * dc
            for k in range(K - 1):
                acc = acc + ws[k] * pltpu.roll(dc, win - (K - 1 - k), 0)
            return acc

        def fold(t):
            acc = t[FFN_HALO:FFN_HALO + SUBLANES]
            for i in range(1, rc // SUBLANES):
                acc = acc + t[FFN_HALO + SUBLANES * i:FFN_HALO + SUBLANES * (i + 1)]
            return acc

        def chunk(c, sums):
            r0 = pl.multiple_of(c * rc, SUBLANES)
            gs, vs, d = taps(gp[pl.ds(r0, win), :]), taps(vp[pl.ds(r0, win), :]), dp[pl.ds(r0, win), :]
            ge, dge = _gelu_and_grad(conv(gs, wg))
            dgc = d * conv(vs, wv) * dge
            dvc = d * ge
            dgo_ref[pl.ds(r0, rc), :] = conv_t(dgc, wg)[FFN_HALO:FFN_HALO + rc].astype(BF)
            dvo_ref[pl.ds(r0, rc), :] = conv_t(dvc, wv)[FFN_HALO:FFN_HALO + rc].astype(BF)
            new = [fold(dc * u) for us, dc in ((gs, dgc), (vs, dvc)) for u in us]
            return tuple(a + b for a, b in zip(sums, new))

        sums = lax.fori_loop(0, S // rc, chunk, tuple(jnp.zeros((SUBLANES, cb), F32) for _ in range(2 * K)))

        @pl.when(pl.program_id(1) == 0)
        def _():
            dwg_ref[...] = jnp.zeros_like(dwg_ref)
            dwv_ref[...] = jnp.zeros_like(dwv_ref)

        for k in range(K):
            dwg_ref[k:k + 1, :] += jnp.sum(sums[k], axis=0, keepdims=True)
            dwv_ref[k:k + 1, :] += jnp.sum(sums[K + k], axis=0, keepdims=True)

    blk = pl.BlockSpec((S, cb), lambda j, bi: (bi, j))
    wblk = pl.BlockSpec((K, cb), lambda j, bi: (0, j))
    return _call(
        body, (up0, up0, w, w, dg), grid=(nj, Bn),
        in_specs=[blk, pl.BlockSpec((S, cb), lambda j, bi: (bi, nj + j)), wblk, pl.BlockSpec((K, cb), lambda j, bi: (0, nj + j)), blk],
        out_specs=[blk, blk, wblk, wblk],
        out_shape=[SDS((Bn * S, F), BF), SDS((Bn * S, F), BF), SDS((K, F), F32), SDS((K, F), F32)],
        scratch_shapes=[pltpu.VMEM((S + 2 * FFN_HALO, cb), F32)] * 3, semantics=("parallel", "arbitrary"), name=name, side=side)


def _softmax_rows(q, k, scale):
    sc = lax.dot_general(q, k, _DN["nt"], preferred_element_type=F32) * scale
    e = jnp.exp(sc - jnp.max(sc, axis=-1, keepdims=True))
    return e / jnp.sum(e, axis=-1, keepdims=True)


def _attn_ts(S):
    return _tile(S, 1024, 8)


def _attn_fwd(q, kv, Bn, S, Mn, D, name, side=None):
    H = XA_HEADS
    dh = D // H
    ts = _attn_ts(S)
    nsb = S // ts
    scale = dh ** -0.5

    def body(q_ref, k_ref, v_ref, o_ref):
        p = _softmax_rows(q_ref[...], k_ref[...], scale)
        o_ref[...] = lax.dot_general(p.astype(BF), v_ref[...], _DN["nn"], preferred_element_type=F32).astype(BF)

    qblk = pl.BlockSpec((ts, dh), lambda bi, h, s: (bi * nsb + s, h))
    return _call1(
        body, (q, kv, kv), grid=(Bn, H, nsb),
        in_specs=[qblk, pl.BlockSpec((Mn, dh), lambda bi, h, s: (bi, h)), pl.BlockSpec((Mn, dh), lambda bi, h, s: (bi, H + h))],
        out_spec=qblk, out_shape=SDS((Bn * S, D), BF), semantics=("parallel", "parallel", "parallel"), name=name, side=side)


def _attn_bwd(q, kv, datt, Bn, S, Mn, D, name):
    H = XA_HEADS
    dh = D // H
    ts = _attn_ts(S)
    nsb = S // ts
    scale = dh ** -0.5

    def body(q_ref, k_ref, v_ref, do_ref, dq_ref, dk_ref, dv_ref):
        q, k, v, do = q_ref[...], k_ref[...], v_ref[...], do_ref[...]
        p = _softmax_rows(q, k, scale)
        dp = lax.dot_general(do, v, _DN["nt"], preferred_element_type=F32)
        ds = (p * (dp - jnp.sum(dp * p, axis=-1, keepdims=True)) * scale).astype(BF)
        dq_ref[...] = lax.dot_general(ds, k, _DN["nn"], preferred_element_type=F32).astype(BF)

        @pl.when(pl.program_id(2) == 0)
        def _():
            dk_ref[...] = jnp.zeros_like(dk_ref)
            dv_ref[...] = jnp.zeros_like(dv_ref)

        dk_ref[...] += lax.dot_general(ds, q, _DN["tn"], preferred_element_type=F32)
        dv_ref[...] += lax.dot_general(p.astype(BF), do, _DN["tn"], preferred_element_type=F32)

    qblk = pl.BlockSpec((ts, dh), lambda bi, h, s: (bi * nsb + s, h))
    kblk = pl.BlockSpec((Mn, dh), lambda bi, h, s: (bi, h))
    return pl.pallas_call(
        body, grid=(Bn, H, nsb),
        in_specs=[qblk, kblk, pl.BlockSpec((Mn, dh), lambda bi, h, s: (bi, H + h)), qblk],
        out_specs=[qblk, kblk, kblk], out_shape=[SDS((Bn * S, D), BF), SDS((Bn * Mn, D), F32), SDS((Bn * Mn, D), F32)],
        compiler_params=_params("parallel", "parallel", "arbitrary"), name=name)(q, kv, kv, datt)


class _Sides:
    def __init__(self, by_key=None, on_land=None):
        self.by_key, self.landed, self.on_land = dict(by_key or {}), {}, on_land

    def run(self, key, fn, *args, **kw):
        side = self.by_key.get(key)
        if side is None:
            return fn(*args, **kw)
        out, self.landed[key] = fn(*args, side=side() if callable(side) else side, **kw)
        if self.on_land is not None:
            self.on_land(key, self.landed[key])
        return out

    def mm(self, key, *args, **kw):
        return self.run(key, _mm, *args, **kw)


def _layer_fwd(x, h, mem_n, W, V, l, dims, sides, next_g):
    Bn, S, Mn, D, C, F = dims
    n = f"l{l}_"
    proj = sides.mm("proj", h, W["w_in"], "nn", F32, n + "proj", bl=0)
    cv = sides.run("glu_conv", _glu_conv_fwd, proj, V["conv_dw_w"][l], V["conv_dw_b"][l], Bn, S, C, n + "glu_conv")
    yc1 = _ln_silu_fwd(cv, V["conv_ln_g"][l], V["conv_ln_b"][l], n + "ln_silu")
    yc = sides.mm("conv_out", yc1, W["w_conv_out"], "nn", BF, n + "conv_out", bl=0)
    yp = _pool_fwd(proj, W["w_pool"], 0, Bn, S, C, D, n + "pool")
    merged = sides.run("merge", _merge_fwd, proj, yc, yp, V["pool_scale"][l], C, n + "merge")
    x1, hq = sides.run("out_proj", _mm_rms_fwd, merged, W["w_out"], x, V["xattn_norm_g"][l], n + "out_proj")
    q = sides.mm("q_proj", hq, W["w_q"], "nn", BF, n + "q_proj", bl=0)
    kv = _mm(mem_n, W["w_kv"], "nn", BF, n + "kv_proj", bl=0)
    att = sides.run("attn", _attn_fwd, q, kv, Bn, S, Mn, D, n + "attn")
    x2, hf = sides.run("o_proj", _mm_rms_fwd, att, W["w_o"], x1, V["ffn_norm_g"][l], n + "o_proj")
    up0 = sides.mm("up_proj", hf, W["w_up"], "nn", F32, n + "up_proj", bl=0)
    gact = sides.run("ffn_act", _ffn_act_fwd, up0, V["ffn_dw_w"][l], Bn, S, F, n + "ffn_act")
    if next_g is not None:
        x3, h3 = sides.run("down_proj", _mm_rms_fwd, gact, W["w_down"], x2, next_g, n + "down_proj")
    else:
        x3, h3 = sides.mm("down_proj", gact, W["w_down"], "nn", F32, n + "down_proj", res=x2, bl=0), None
    return x3, h3, dict(x=x, h=h, proj=proj, cv=cv, yc1=yc1, yc=yc, yp=yp, merged=merged, x1=x1, hq=hq, q=q, kv=kv, att=att, x2=x2,
                        hf=hf, up0=up0, gact=gact)


def _layer_bwd_mlp(dx, dxb, sv, W, V, l, dims, sides):
    Bn, S, Mn, D, C, F = dims
    n = f"l{l}_b_"
    gw, sm = {}, {}
    dgact = sides.mm("d_gact", dxb, W["w_down"], "nt", BF, n + "d_gact", bl=0)
    gw["w_down"] = sides.mm("dw_down", sv["gact"], dxb, "tn", F32, n + "dw_down", twin=BF)
    dg0, dv0, dwg, dwv = sides.run("ffn_act_b", _ffn_act_bwd, sv["up0"], V["ffn_dw_w"][l], dgact, Bn, S, F, n + "ffn_act")
    sm["ffn_dw_w"] = jnp.concatenate([dwg, dwv], axis=1)
    dup0 = jnp.concatenate([dg0, dv0], axis=1)
    dx2, dx2b, sm["ffn_norm_g"] = _mm_rms_bwd(dup0, W["w_up"], sv["x2"], V["ffn_norm_g"][l], dx, n + "d_hf")
    gw["w_up"] = sides.mm("dw_up", sv["hf"], dup0, "tn", F32, n + "dw_up", twin=BF)
    return dx2, dx2b, gw, sm


def _layer_bwd_mix(dx2, dx2b, dmem_n, sv, mem_n, W, V, l, dims, sides):
    Bn, S, Mn, D, C, F = dims
    n = f"l{l}_b_"
    gw, sm = {}, {}
    datt = _mm(dx2b, W["w_o"], "nt", BF, n + "d_att", bl=0)
    gw["w_o"] = _mm(sv["att"], dx2b, "tn", F32, n + "dw_o", twin=BF)
    dq, dk, dv = _attn_bwd(sv["q"], sv["kv"], datt, Bn, S, Mn, D, n + "attn")
    dkv = jnp.concatenate([dk, dv], axis=1)
    gw["w_kv"] = _mm(mem_n, dkv, "tn", F32, n + "dw_kv", twin=BF)
    dmem_n = _mm(dkv, W["w_kv"], "nt", F32, n + "d_mem", res=dmem_n, bl=0)
    dx1, dx1b, sm["xattn_norm_g"] = _mm_rms_bwd(dq, W["w_q"], sv["x1"], V["xattn_norm_g"][l], dx2, n + "d_hq")
    gw["w_q"] = _mm(sv["hq"], dq, "tn", F32, n + "dw_q", twin=BF)
    dmerged = _mm(dx1b, W["w_out"], "nt", BF, n + "d_merged", bl=0)
    gw["w_out"] = _mm(sv["merged"], dx1b, "tn", F32, n + "dw_out", twin=BF)
    dgc, dgp, dyc, dyp, sm["pool_scale"] = sides.run("merge_b", _merge_bwd, sv["proj"], sv["yc"], sv["yp"], V["pool_scale"][l], dmerged, C, n + "merge")
    du, dwp = _pool_bwd(sv["proj"], W["w_pool"], dyp, 0, Bn, S, C, D, n + "pool")
    gw["w_pool"] = (dwp, dwp.astype(BF))
    dyc1 = _mm(dyc, W["w_conv_out"], "nt", F32, n + "d_yc1", bl=0)
    gw["w_conv_out"] = _mm(sv["yc1"], dyc, "tn", F32, n + "dw_conv_out", twin=BF)
    dcv, sm["conv_ln_g"], sm["conv_ln_b"] = _ln_silu_bwd(sv["cv"], V["conv_ln_g"][l], V["conv_ln_b"][l], dyc1, n + "ln_silu")
    da, dgl, sm["conv_dw_w"], sm["conv_dw_b"] = sides.run("glu_conv_b", _glu_conv_bwd, sv["proj"], V["conv_dw_w"][l], dcv, Bn, S, C, n + "glu_conv")
    dproj = jnp.concatenate([da, dgl, du, dgc, dgp], axis=1)
    dx, dxb, sm["mix_norm_g"] = _mm_rms_bwd(dproj, W["w_in"], sv["x"], V["mix_norm_g"][l], dx1, n + "d_h")
    gw["w_in"] = _mm(sv["h"], dproj, "tn", F32, n + "dw_in", twin=BF)
    return dx, dxb, dmem_n, gw, sm


BIG = (("w_in", "col"), ("w_conv_out", "col"), ("w_pool", "row"), ("w_out", "row"), ("w_q", "row"), ("w_kv", "col"),
       ("w_o", "row"), ("w_up", "col"), ("w_down", "row"))
ALL_RELS = (1, 2, 3)
GATHER_FIRST = ("w_in", "w_conv_out", "w_pool", "w_out", "w_q", "w_o")
FWD_CARRY = {
    (0, "proj"): (("w_up", 0, (1, 2)),),
    (0, "glu_conv"): (("w_up", 0, (3,)),),
    (0, "merge"): (("w_kv", 0, ALL_RELS),),
    (0, "attn"): (("w_down", 0, (1, 2)),),
    (0, "o_proj"): (("w_down", 0, (3,)),),
    (0, "up_proj"): (("w_in", 1, ALL_RELS), ("w_conv_out", 1, ALL_RELS), ("w_pool", 1, ALL_RELS)),
    (0, "ffn_act"): (("w_out", 1, ALL_RELS), ("w_q", 1, ALL_RELS), ("w_kv", 1, ALL_RELS)),
    (0, "down_proj"): (("w_o", 1, ALL_RELS),),
    (1, "proj"): (("w_up", 1, (1, 2)),),
    (1, "glu_conv"): (("w_up", 1, (3,)),),
    (1, "merge"): (("w_down", 1, (1, 2)),),
    (1, "attn"): (("w_down", 1, (3,)),),
}
PASS_GROUPS = ((("w_kv", 0),), (("w_up", 0), ("w_down", 0)),
               (("w_in", 1), ("w_conv_out", 1), ("w_pool", 1), ("w_out", 1), ("w_q", 1), ("w_kv", 1), ("w_o", 1)),
               (("w_up", 1), ("w_down", 1)))
EARLY = ("w_down", "w_up")
BWD_CARRY_EARLY = {"merge_b": ("w_down",), "glu_conv_b": ("w_up",)}
BWD_CARRY_LATE = {"ffn_act_b": ("w_in", "w_conv_out", "w_pool", "w_out", "w_q", "w_kv", "w_o")}


def _place():
    xi, yi, ci = lax.axis_index("x"), lax.axis_index("y"), lax.axis_index("c")
    return xi, yi, ci, 2 * xi + yi


def _chip_peer(xi, yi, ci, r):
    return (xi ^ (r >> 1), yi ^ (r & 1), ci)


def _full_shard(ref, kind, k, cs):
    if kind == "col":
        return ref.at[:, :, :, :, pl.ds(pl.multiple_of(k * cs, cs), cs)]
    return ref.at[:, :, k]


def _gather_weights(shards, kinds):
    n = len(shards)
    outs = []
    for s, kind in zip(shards, kinds):
        L, P, _, RH, CS = s.shape
        outs.append(SDS((L, P, 2, RH, CS * N_CHIPS) if kind == "col" else (L, P, N_CHIPS, 2, RH, CS), s.dtype))
    per = 7

    def body(*refs):
        srcs, fulls, (ssem, rsem) = refs[:n], refs[n:2 * n], refs[2 * n:]
        xi, yi, ci, j = _place()
        sib = (xi, yi, 1 - ci)

        def piece(i, k, c):
            kind, cs = kinds[i], shards[i].shape[-1]
            if kind == "col":
                return fulls[i].at[:, :, c, :, pl.ds(pl.multiple_of(k * cs, cs), cs)]
            return fulls[i].at[:, :, k, c]

        def copy(i, slot, src, dst, dev):
            return pltpu.make_async_remote_copy(src_ref=src, dst_ref=dst, send_sem=ssem.at[per * i + slot], recv_sem=rsem.at[per * i + slot],
                                                device_id=dev, device_id_type=MESH)

        own, first, passed = [], [], []
        for i in range(n):
            for r in (1, 2, 3):
                first.append(copy(i, r - 1, srcs[i].at[:, :, ci], piece(i, j, ci), _chip_peer(xi, yi, ci, r)))
                first[-1].start()
        for i in range(n):
            own.append(copy(i, 6, srcs[i], _full_shard(fulls[i], kinds[i], j, shards[i].shape[-1]), sib))
            own[-1].start()
        for i in range(n):
            for r in (1, 2, 3):
                got = piece(i, j ^ r, ci)
                copy(i, r - 1, got, got, sib).wait_recv()
                passed.append(copy(i, 2 + r, got, got, sib))
                passed[-1].start()
        for i in range(n):
            for r in (1, 2, 3):
                got = piece(i, j ^ r, 1 - ci)
                copy(i, 2 + r, got, got, sib).wait_recv()
        for cp in own:
            cp.wait()
        for cp in first + passed:
            cp.wait_send()

    return pl.pallas_call(
        body, in_specs=[ANY] * n, out_specs=[ANY] * n, out_shape=outs,
        scratch_shapes=[pltpu.SemaphoreType.DMA((per * n,)), pltpu.SemaphoreType.DMA((per * n,))], name="gather_weights")(*shards)


def _full_sds(s, kind):
    L, P, _, RH, CS = s.shape
    return SDS((L, P, 2, RH, CS * N_CHIPS) if kind == "col" else (L, P, N_CHIPS, 2, RH, CS), s.dtype)


def _gather_piece(full, kind, cs, k, c):
    if kind == "col":
        return full.at[:, :, c, :, pl.ds(pl.multiple_of(k * cs, cs), cs)]
    return full.at[:, :, k, c]


def _side_gather(shards, kinds, rels, fulls):
    n = len(shards)

    def make(srcs, outs, ssem, rsem):
        xi, yi, ci, j = _place()
        return [pltpu.make_async_remote_copy(
            src_ref=srcs[i].at[:, :, ci], dst_ref=_gather_piece(outs[i], kinds[i], shards[i].shape[-1], j, ci), send_sem=ssem.at[3 * i + r - 1],
            recv_sem=rsem.at[3 * i + r - 1], device_id=_chip_peer(xi, yi, ci, r), device_id_type=MESH) for i in range(n) for r in rels[i]]

    prior = [f for f in fulls if f is not None]
    assert len(prior) in (0, n)
    return _Side(list(shards) + prior, [_full_sds(s, k) for s, k in zip(shards, kinds)], 3 * n, make, n_alias=len(prior))


def _gather_pass(fulls, shards, kinds, name):
    n = len(fulls)

    def body(*refs):
        srcs, outs, (ssem, rsem) = refs[n:2 * n], refs[2 * n:3 * n], refs[3 * n:]
        xi, yi, ci, j = _place()
        sib = (xi, yi, 1 - ci)
        cps = []
        for i in range(n):
            cs = shards[i].shape[-1]
            for r in (1, 2, 3):
                got = _gather_piece(outs[i], kinds[i], cs, j ^ r, ci)
                cps.append(pltpu.make_async_remote_copy(src_ref=got, dst_ref=got, send_sem=ssem.at[4 * i + r - 1], recv_sem=rsem.at[4 * i + r - 1],
                                                        device_id=sib, device_id_type=MESH))
            cps.append(pltpu.make_async_remote_copy(src_ref=srcs[i], dst_ref=_full_shard(outs[i], kinds[i], j, cs), send_sem=ssem.at[4 * i + 3],
                                                    recv_sem=rsem.at[4 * i + 3], device_id=sib, device_id_type=MESH))
        for cp in cps:
            cp.start()
        for cp in cps:
            cp.wait()

    return pl.pallas_call(
        body, in_specs=[ANY] * (2 * n), out_specs=[ANY] * n, out_shape=[SDS(f.shape, f.dtype) for f in fulls],
        input_output_aliases={i: i for i in range(n)},
        scratch_shapes=[pltpu.SemaphoreType.DMA((4 * n,)), pltpu.SemaphoreType.DMA((4 * n,))], name=name)(*fulls, *shards)


def _sibling_exchange(gviews, kinds, name):
    n = len(gviews)
    outs = [SDS(g.shape[:1] + g.shape[2:] if kind == "col" else g.shape[:2] + g.shape[3:], g.dtype) for g, kind in zip(gviews, kinds)]

    def body(*refs):
        gs, lands, (ssem, rsem) = refs[:n], refs[n:2 * n], refs[2 * n:]
        xi, yi, ci, _ = _place()
        cps = []
        for i in range(n):
            src = gs[i].at[:, 1 - ci] if kinds[i] == "col" else gs[i].at[:, :, 1 - ci]
            cps.append(pltpu.make_async_remote_copy(src_ref=src, dst_ref=lands[i], send_sem=ssem.at[i], recv_sem=rsem.at[i],
                                                    device_id=(xi, yi, 1 - ci), device_id_type=MESH))
            cps[-1].start()
        for cp in cps:
            cp.wait()

    return pl.pallas_call(body, in_specs=[ANY] * n, out_specs=[ANY] * n, out_shape=outs,
                          scratch_shapes=[pltpu.SemaphoreType.DMA((n,)), pltpu.SemaphoreType.DMA((n,))], name=name)(*gviews)


def _chip_sums(gs, lands, kinds, jc, name):
    n = len(gs)
    args, in_specs, out_specs, out_shape = [], [], [], []
    for g, land, kind in zip(gs, lands, kinds):
        if kind == "col":
            P, _, RH, C = g.shape
            CS = C // N_CHIPS
            in_specs += [pl.BlockSpec((P, None, RH, CS), lambda r, jc: (0, jc[1], 0, jc[0] ^ r)),
                         pl.BlockSpec((P, RH, CS), lambda r, jc: (0, 0, jc[0] ^ r))]
        else:
            P, _, _, RH, CS = g.shape
            in_specs += [pl.BlockSpec((P, None, None, RH, CS), lambda r, jc: (0, jc[0] ^ r, jc[1], 0, 0)),
                         pl.BlockSpec((P, None, RH, CS), lambda r, jc: (0, jc[0] ^ r, 0, 0))]
        args += [g, land]
        out_specs += [pl.BlockSpec((P, RH, CS), lambda r, jc: (0, 0, 0)), pl.BlockSpec((None, P, RH, CS), lambda r, jc: (r, 0, 0, 0))]
        out_shape += [SDS((P, RH, CS), F32), SDS((N_CHIPS, P, RH, CS), BF)]

    def body(jc_ref, *refs):
        ins, outs = refs[:2 * n], refs[2 * n:]
        for i in range(n):
            s = ins[2 * i][...] + ins[2 * i + 1][...].astype(F32)
            outs[2 * i + 1][...] = s.astype(BF)

            @pl.when(pl.program_id(0) == 0)
            def _():
                outs[2 * i][...] = s

    outs = _call(body, args, grid=(N_CHIPS,), in_specs=in_specs, out_specs=out_specs, out_shape=out_shape, semantics=("arbitrary",),
                 name=name, prefetch=(jc,))
    return outs[0::2], outs[1::2]


def _chip_exchange_copies(srcs, lands, ssem, rsem):
    xi, yi, ci, _ = _place()
    return [pltpu.make_async_remote_copy(src_ref=srcs[i].at[r], dst_ref=lands[i].at[r], send_sem=ssem.at[3 * i + r - 1],
                                         recv_sem=rsem.at[3 * i + r - 1], device_id=_chip_peer(xi, yi, ci, r), device_id_type=MESH)
            for i in range(len(srcs)) for r in (1, 2, 3)]


def _side_chip_exchange(pieces):
    return _Side(pieces, [SDS(p.shape, p.dtype) for p in pieces], 3 * len(pieces), _chip_exchange_copies)


FINAL_SUM_STEPS = 2


def _final_sums(owns, lands, jc, shards, l, L, name, side=None):
    n = len(owns)
    args, in_specs, out_specs, out_shape = [], [], [], []
    for own, land in zip(owns, lands):
        P, RH, CS = own.shape
        hr = RH // FINAL_SUM_STEPS
        in_specs += [pl.BlockSpec((P, hr, CS), lambda h, jc: (0, h, 0))]
        in_specs += [pl.BlockSpec((None, P, hr, CS), functools.partial(lambda r, h, jc: (r, 0, h, 0), r)) for r in (1, 2, 3)]
        args += [own, land, land, land]
        out_specs.append(pl.BlockSpec((None, P, None, hr, CS), lambda h, jc: (l, 0, jc[1], h, 0)))
        out_shape.append(SDS((L, P, 2, RH, CS), F32))
    aliases = None
    if shards is not None:
        aliases = {4 * n + i: i for i in range(n)}
        in_specs += [ANY] * n
        args += list(shards)

    def body(jc_ref, *refs):
        outs = refs[len(args):]
        for i in range(n):
            o, a, b, c = (refs[4 * i + t][...] for t in range(4))
            outs[i][...] = ((o + a.astype(F32)) + b.astype(F32)) + c.astype(F32)

    return _call(body, args, grid=(FINAL_SUM_STEPS,), in_specs=in_specs, out_specs=out_specs, out_shape=out_shape, semantics=("arbitrary",),
                 name=name, prefetch=(jc,), aliases=aliases, side=side)


def _halves_exchange(shards, l, name):
    n = len(shards)

    def body(*refs):
        outs, (ssem, rsem) = refs[n:2 * n], refs[2 * n:]
        xi, yi, ci, _ = _place()
        cps = []
        for i in range(n):
            mine = outs[i].at[l, :, ci]
            cps.append(pltpu.make_async_remote_copy(src_ref=mine, dst_ref=mine, send_sem=ssem.at[i], recv_sem=rsem.at[i],
                                                    device_id=(xi, yi, 1 - ci), device_id_type=MESH))
            cps[-1].start()
        for i in range(n):
            land = outs[i].at[l, :, 1 - ci]
            pltpu.make_async_remote_copy(src_ref=land, dst_ref=land, send_sem=ssem.at[i], recv_sem=rsem.at[i],
                                         device_id=(xi, yi, 1 - ci), device_id_type=MESH).wait_recv()
        for cp in cps:
            cp.wait_send()

    return pl.pallas_call(body, in_specs=[ANY] * n, out_specs=[ANY] * n, out_shape=[SDS(s.shape, s.dtype) for s in shards],
                          input_output_aliases={i: i for i in range(n)},
                          scratch_shapes=[pltpu.SemaphoreType.DMA((n,)), pltpu.SemaphoreType.DMA((n,))], name=name)(*shards)


def _reduce_small(part, pieces):
    NR, Wd = part.shape
    ND = 2 * N_CHIPS
    n = len(pieces)

    def body(p_ref, *refs):
        srcs, o_ref, lands, (land, ssem, rsem, xs, xr) = refs[:n], refs[n], refs[n + 1:2 * n + 1], refs[2 * n + 1:]
        exchange = _chip_exchange_copies(srcs, lands, xs, xr)
        for cp in exchange:
            cp.start()
        xi, yi, ci, j = _place()
        me = 2 * j + ci
        land[me] = p_ref[...]
        cps = []
        for rr in range(1, ND):
            dev = (xi ^ (rr >> 2), yi ^ ((rr >> 1) & 1), ci ^ (rr & 1))
            cps.append(pltpu.make_async_remote_copy(src_ref=p_ref, dst_ref=land.at[me], send_sem=ssem.at[rr - 1], recv_sem=rsem.at[rr - 1],
                                                    device_id=dev, device_id_type=MESH))
            cps[-1].start()
        for rr in range(1, ND):
            got = land.at[me ^ rr]
            pltpu.make_async_remote_copy(src_ref=got, dst_ref=got, send_sem=ssem.at[rr - 1], recv_sem=rsem.at[rr - 1],
                                         device_id=(xi, yi, ci), device_id_type=MESH).wait_recv()
        acc = land[0]
        for d in range(1, ND):
            acc = acc + land[d]
        o_ref[...] = acc
        for cp in cps:
            cp.wait_send()
        for cp in exchange:
            cp.wait()

    vm = pl.BlockSpec(memory_space=pltpu.VMEM)
    outs = pl.pallas_call(
        body, in_specs=[vm] + [ANY] * n, out_specs=[vm] + [ANY] * n, out_shape=[SDS((NR, Wd), F32)] + [SDS(p.shape, p.dtype) for p in pieces],
        scratch_shapes=[pltpu.VMEM((ND, NR, Wd), F32), pltpu.SemaphoreType.DMA((ND - 1,)), pltpu.SemaphoreType.DMA((ND - 1,)),
                        pltpu.SemaphoreType.DMA((3 * n,)), pltpu.SemaphoreType.DMA((3 * n,))],
        name="small_grad_allreduce")(part, *pieces)
    return outs[0], list(outs[1:])


def _adamw_update(w_ref, g_ref, m_ref, v_ref, d_ref, mo_ref, vo_ref):
    g = g_ref[...]
    m = ADAM_B1 * m_ref[...] + (1.0 - ADAM_B1) * g
    v = ADAM_B2 * v_ref[...] + (1.0 - ADAM_B2) * jnp.square(g)
    m_hat = m / (1.0 - ADAM_B1 ** ADAM_STEP)
    v_hat = v / (1.0 - ADAM_B2 ** ADAM_STEP)
    d_ref[...] = -ADAM_LR * (m_hat / (jnp.sqrt(v_hat) + ADAM_EPS) + ADAM_WD * w_ref[...])
    mo_ref[...] = m
    vo_ref[...] = v


ADAMW_STEPS = 8


def _adamw_layer(ws, gs, ms, vs, prev, l, name, side=None):
    n = len(ws)
    args, in_specs, out_specs, out_shape = [], [], [], []
    for w, g, m, v in zip(ws, gs, ms, vs):
        L, R, C = w.shape
        blk = pl.BlockSpec((None, R // ADAMW_STEPS, C), lambda i: (l, i, 0))
        in_specs += [blk] * 4
        args += [w, g, m, v]
        out_specs += [blk] * 3
        out_shape += [SDS((L, R, C), F32)] * 3
    aliases = None
    if prev is not None:
        aliases = {4 * n + i: i for i in range(3 * n)}
        in_specs += [ANY] * (3 * n)
        args += list(prev)

    def body(*refs):
        outs = refs[len(args):]
        for i in range(n):
            _adamw_update(*refs[4 * i:4 * i + 4], *outs[3 * i:3 * i + 3])

    return _call(body, args, grid=(ADAMW_STEPS,), in_specs=in_specs, out_specs=out_specs, out_shape=out_shape, semantics=("parallel",),
                 name=name, aliases=aliases, side=side)


def _adamw(w, g, m, v, name):
    shape = w.shape
    C = shape[-1]
    R = w.size // C
    tb = _tile(R, max(8, (1 << 18) // C), 8)
    body = functools.partial(_adamw_update)
    blk = pl.BlockSpec((tb, C), lambda i: (i, 0))
    outs = pl.pallas_call(body, grid=(R // tb,), in_specs=[blk] * 4, out_specs=[blk] * 3, out_shape=[SDS((R, C), F32)] * 3,
                          compiler_params=_params("parallel"), name=name)(*[t.reshape(R, C) for t in (w, g, m, v)])
    return [t.reshape(shape) for t in outs]


WEIGHTS = ("mix_norm_g", "w_in", "conv_dw_w", "conv_dw_b", "conv_ln_g", "conv_ln_b", "w_conv_out", "w_pool_grp", "pool_scale", "w_out",
           "xattn_norm_g", "mem_norm_g", "w_q", "w_kv", "w_o", "ffn_norm_g", "w_up", "ffn_dw_w", "w_down", "final_norm_g")
VECTORS = ("mix_norm_g", "conv_dw_b", "conv_ln_g", "conv_ln_b", "pool_scale", "xattn_norm_g", "mem_norm_g", "ffn_norm_g", "final_norm_g")


def _shard_view(t, kind):
    L, P, R, C = t.shape
    return t.reshape(L, P, 2, R // 2, C)


def _rows(t, width):
    return t.reshape(-1, width)


def _pack(parts):
    return jnp.concatenate([jnp.pad(p, ((0, (-p.shape[0]) % 8), (0, 0))) for p in parts], axis=0)


def kernel(x, mem, mix_norm_g, w_in, conv_dw_w, conv_dw_b, conv_ln_g, conv_ln_b, w_conv_out, w_pool_grp, pool_scale, w_out, xattn_norm_g, mem_norm_g, w_q, w_kv, w_o, ffn_norm_g, w_up, ffn_dw_w, w_down, final_norm_g, loss_target, m_mix_norm_g, m_w_in, m_conv_dw_w, m_conv_dw_b, m_conv_ln_g, m_conv_ln_b, m_w_conv_out, m_w_pool_grp, m_pool_scale, m_w_out, m_xattn_norm_g, m_mem_norm_g, m_w_q, m_w_kv, m_w_o, m_ffn_norm_g, m_w_up, m_ffn_dw_w, m_w_down, m_final_norm_g, v_mix_norm_g, v_w_in, v_conv_dw_w, v_conv_dw_b, v_conv_ln_g, v_conv_ln_b, v_w_conv_out, v_w_pool_grp, v_pool_scale, v_w_out, v_xattn_norm_g, v_mem_norm_g, v_w_q, v_w_kv, v_w_o, v_ffn_norm_g, v_w_up, v_ffn_dw_w, v_w_down, v_final_norm_g):
    w = dict(mix_norm_g=mix_norm_g, w_in=w_in, conv_dw_w=conv_dw_w, conv_dw_b=conv_dw_b, conv_ln_g=conv_ln_g, conv_ln_b=conv_ln_b,
             w_conv_out=w_conv_out, w_pool_grp=w_pool_grp, pool_scale=pool_scale, w_out=w_out, xattn_norm_g=xattn_norm_g,
             mem_norm_g=mem_norm_g, w_q=w_q, w_kv=w_kv, w_o=w_o, ffn_norm_g=ffn_norm_g, w_up=w_up, ffn_dw_w=ffn_dw_w, w_down=w_down,
             final_norm_g=final_norm_g)
    m = dict(zip(WEIGHTS, (m_mix_norm_g, m_w_in, m_conv_dw_w, m_conv_dw_b, m_conv_ln_g, m_conv_ln_b, m_w_conv_out, m_w_pool_grp, m_pool_scale,
                           m_w_out, m_xattn_norm_g, m_mem_norm_g, m_w_q, m_w_kv, m_w_o, m_ffn_norm_g, m_w_up, m_ffn_dw_w, m_w_down, m_final_norm_g)))
    v = dict(zip(WEIGHTS, (v_mix_norm_g, v_w_in, v_conv_dw_w, v_conv_dw_b, v_conv_ln_g, v_conv_ln_b, v_w_conv_out, v_w_pool_grp, v_pool_scale,
                           v_w_out, v_xattn_norm_g, v_mem_norm_g, v_w_q, v_w_kv, v_w_o, v_ffn_norm_g, v_w_up, v_ffn_dw_w, v_w_down, v_final_norm_g)))
    xi, yi, ci, j = _place()
    jc = jnp.stack([j, ci]).astype(jnp.int32)
    L = w_in.shape[0]
    G = len(POOL_WINDOWS)
    kinds = dict(BIG)

    def to_mat(name, t):
        if name == "w_pool":
            return jnp.swapaxes(t, 2, 3)
        return t[:, None]

    def from_mat(name, t):
        if name == "w_pool":
            return jnp.swapaxes(t, 2, 3)
        return t[:, 0]

    src = {name: w["w_pool_grp" if name == "w_pool" else name] for name, _ in BIG}

    KC, cs_c = conv_dw_w.shape[1], conv_dw_w.shape[2]
    KF, cs_f = ffn_dw_w.shape[1], ffn_dw_w.shape[2]
    taps = jnp.concatenate([conv_dw_w.reshape(L * KC, cs_c), ffn_dw_w.reshape(L * KF * (cs_f // cs_c), cs_c)], axis=0)
    n_taps = taps.shape[0]
    taps = jnp.pad(taps, ((0, (-n_taps) % 16), (0, 0)))
    names = [name for name, _ in BIG]
    mats = {name: to_mat(name, src[name]).astype(BF) for name in names}

    def layer_shards(l, subset):
        return [_shard_view(mats[name][l:l + 1], kinds[name]) for name in subset]

    def as_weight(name, f):
        return f.reshape(G if name == "w_pool" else 1, -1, f.shape[-1])

    assert L == 2
    fulls = _gather_weights(layer_shards(0, GATHER_FIRST) + [_shard_view(taps[None, None], "row")], [kinds[name] for name in GATHER_FIRST] + ["row"])
    ready = {(name, 0): as_weight(name, f) for name, f in zip(GATHER_FIRST, fulls)}
    landing = {}
    taps_all = fulls[-1].reshape(N_CHIPS, -1, cs_c)[:, :n_taps]
    V = {name: w[name] for name in VECTORS}
    V["conv_dw_w"] = taps_all[:, :L * KC].reshape(N_CHIPS, L, KC, cs_c).transpose(1, 2, 0, 3).reshape(L, KC, N_CHIPS * cs_c)
    V["ffn_dw_w"] = taps_all[:, L * KC:].reshape(N_CHIPS, L, KF, cs_f).transpose(1, 2, 0, 3).reshape(L, KF, N_CHIPS * cs_f)

    Bn, S, D = x.shape
    Mn = mem.shape[1]
    dims = (Bn, S, Mn, D, conv_dw_b.shape[1], w_down.shape[1] * N_CHIPS)
    xt = x.reshape(Bn * S, D)
    memf = mem.reshape(Bn * Mn, D)
    mem_n = _rms_fwd(memf, V["mem_norm_g"], "mem_norm")

    class LayerWeights:
        def __init__(self, l):
            self.l = l

        def __getitem__(self, name):
            if (name, self.l) not in ready:
                group = next(g for g in PASS_GROUPS if (name, self.l) in g)
                done = _gather_pass([landing.pop(t) for t in group], [layer_shards(lw, [nm])[0] for nm, lw in group],
                                    [kinds[nm] for nm, _ in group], f"gather_pass_{name}_{self.l}")
                ready.update({t: as_weight(t[0], f) for t, f in zip(group, done)})
            return ready[(name, self.l)]

    def carried_gather(entries):
        return lambda: _side_gather([layer_shards(lw, [nm])[0] for nm, lw, _ in entries], [kinds[nm] for nm, _, _ in entries],
                                    [rels for _, _, rels in entries], [landing.get((nm, lw)) for nm, lw, _ in entries])

    saved, W = [], []
    ht = _rms_fwd(xt, V["mix_norm_g"][0], "l0_mix_norm")
    for l in range(L):
        mine = {key: entries for (cl, key), entries in FWD_CARRY.items() if cl == l}
        sides = _Sides({key: carried_gather(entries) for key, entries in mine.items()},
                       on_land=lambda key, fulls, mine=mine: landing.update({(nm, lw): f for (nm, lw, _), f in zip(mine[key], fulls)}))
        W.append(LayerWeights(l))
        xt, ht, sv = _layer_fwd(xt, ht, mem_n, W[l], V, l, dims, sides, V["mix_norm_g"][l + 1] if l + 1 < L else None)
        saved.append(sv)
    loss, dx, dgf = _loss_bwd(xt, V["final_norm_g"], loss_target.reshape(Bn * S, D), "loss")
    loss = lax.psum(loss[0, 0], ("x", "y", "c"))

    late_names = [name for name in names if name not in EARLY]

    def chip_sums(gw, subset, l, tag):
        def view(g, name):
            g = g if g.ndim == 3 else g[None]
            P, R, C = g.shape
            return g.reshape(P, 2, R // 2, C) if kinds[name] == "col" else g.reshape(P, N_CHIPS, 2, R // (2 * N_CHIPS), C)

        gv = [view(gw[name][0], name) for name in subset]
        lands = _sibling_exchange([view(gw[name][1], name) for name in subset], [kinds[name] for name in subset],
                                  f"grad_sibling_exchange_{tag}_l{l}")
        own, pieces = _chip_sums(gv, lands, [kinds[name] for name in subset], jc, f"chip_sums_{tag}_l{l}")
        return dict(zip(subset, own)), dict(zip(subset, pieces))

    def carry(table, pieces):
        return _Sides({key: _side_chip_exchange([pieces[name] for name in subset]) for key, subset in table.items()})

    def landed(table, sides):
        return {name: land for key, subset in table.items() for name, land in zip(subset, sides.landed[key])}

    dxb, dmem_n = dx, None
    smalls, owns, got = [None] * L, [{} for _ in range(L)], [{} for _ in range(L)]
    late = None
    for l in reversed(range(L)):
        sides = carry(BWD_CARRY_LATE, late) if late is not None else _Sides()
        dx, dxb, gw, sm = _layer_bwd_mlp(dx, dxb, saved[l], W[l], V, l, dims, sides)
        if late is not None:
            got[l + 1].update(landed(BWD_CARRY_LATE, sides))
        own, early = chip_sums(gw, EARLY, l, "mlp")
        owns[l].update(own)
        sides = carry(BWD_CARRY_EARLY, early)
        dx, dxb, dmem_n, gw, sm2 = _layer_bwd_mix(dx, dxb, dmem_n, saved[l], mem_n, W[l], V, l, dims, sides)
        got[l].update(landed(BWD_CARRY_EARLY, sides))
        smalls[l] = {**sm, **sm2}
        own, late = chip_sums(gw, late_names, l, "mix")
        owns[l].update(own)
    grad_x = dx.reshape(Bn, S, D)
    _, _, dgm = _rms_bwd(memf, V["mem_norm_g"], dmem_n, None, "mem_norm_b")
    small = {k: jnp.stack([sm[k] for sm in smalls]) if k in ("conv_dw_w", "ffn_dw_w") else jnp.concatenate([sm[k] for sm in smalls], axis=0)
             for k in smalls[0]}
    small["mem_norm_g"] = dgm
    small["final_norm_g"] = dgf

    small_w = conv_dw_b.shape[1]
    order = VECTORS + ("conv_dw_w", "ffn_dw_w")
    parts = [_rows(small[name], small_w) for name in order]
    counts = [p.shape[0] for p in parts]
    summed, landed_late = _reduce_small(_pack(parts), [late[name] for name in late_names])
    got[0].update(zip(late_names, landed_late))

    keys = ["w_pool_grp" if name == "w_pool" else name for name in names]
    rows3 = lambda t: t.reshape(t.shape[0], -1, t.shape[-1])
    wmv = [[rows3(to_mat(name, d[key])) for name, key in zip(names, keys)] for d in (w, m, v)]
    gshards, updates = None, None
    for l in reversed(range(L)):
        gshards = _final_sums([owns[l][name] for name in names], [got[l][name] for name in names], jc, gshards, l, L, f"final_sums_l{l}")
        gshards = _halves_exchange(gshards, l, f"grad_halves_exchange_l{l}")
        updates = _adamw_layer(wmv[0], [rows3(t) for t in gshards], wmv[1], wmv[2], updates, l, f"adamw_l{l}")
    grads, delta, new_m, new_v = {}, {}, {}, {}
    for i, (name, key) in enumerate(zip(names, keys)):
        Lg, P, _, RH, CS = gshards[i].shape
        grads[key] = from_mat(name, gshards[i].reshape(Lg, P, 2 * RH, CS))
        for d, t in zip((delta, new_m, new_v), updates[3 * i:3 * i + 3]):
            d[key] = from_mat(name, t.reshape(Lg, P, 2 * RH, CS))

    off = 0
    for name, cnt in zip(order, counts):
        t = summed[off:off + cnt]
        off += cnt + (-cnt) % 8
        if name in VECTORS:
            grads[name] = t.reshape(w[name].shape)
        else:
            full = t.reshape(small[name].shape)
            cs = w[name].shape[2]
            grads[name] = lax.dynamic_slice_in_dim(full, j * cs, cs, axis=2)

    vec =[_pack([_rows(d[name], small_w) for name in VECTORS]) for d in (w, grads, m, v)]
    outs = _adamw(*vec, "adamw_vectors")
    off = 0
    for name in VECTORS:
        cnt = w[name].size // small_w
        for d, t in zip((delta, new_m, new_v), outs):
            d[name] = t[off:off + cnt].reshape(w[name].shape)
        off += cnt + (-cnt) % 8
    for name in ("conv_dw_w", "ffn_dw_w"):
        delta[name], new_m[name], new_v[name] = _adamw(w[name], grads[name], m[name], v[name], "adamw_" + name)

    return (loss, grad_x, *[grads[k] for k in WEIGHTS], *[delta[k] for k in WEIGHTS], *[new_m[k] for k in WEIGHTS], *[new_v[k] for k in WEIGHTS])
```

```python
import functools

import jax
import jax.numpy as jnp
from jax import lax
from jax.experimental import pallas as pl
from jax.experimental.pallas import tpu as pltpu

F32 = jnp.float32
BF = jnp.bfloat16
SDS = jax.ShapeDtypeStruct
MESH = pl.DeviceIdType.MESH
ANY = pl.BlockSpec(memory_space=pl.ANY)

EPS = 1e-6
XA_HEADS = 4
POOL_WINDOWS = (2, 4, 8, 16)
N_CHIPS = 4
ADAM_LR, ADAM_B1, ADAM_B2, ADAM_EPS, ADAM_WD, ADAM_STEP = 0.001, 0.9, 0.999, 1e-08, 0.01, 10

LANES = 128
ROW_BLOCK = 512
VMEM_LIMIT = 56 * 1024 * 1024


def _params(*sem):
    return pltpu.CompilerParams(dimension_semantics=sem if sem else None, vmem_limit_bytes=VMEM_LIMIT)


def _tile(n, cap, mult=LANES):
    if n <= cap:
        return n
    for t in range(cap - cap % mult, 0, -mult):
        if n % t == 0:
            return t
    return n


_DN = {"nn": (((1,), (0,)), ((), ())), "nt": (((1,), (1,)), ((), ())), "tn": (((0,), (0,)), ((), ()))}


class _Side:
    def __init__(self, ins, outs, n, make, n_alias=0):
        self.ins, self.outs, self.n, self.make, self.n_alias = list(ins), list(outs), n, make, n_alias


def _call(body, args, *, grid, in_specs, out_specs, out_shape, semantics, name, scratch_shapes=(), side=None, prefetch=(), aliases=None):
    n_pf = len(prefetch)
    aliases = {n_pf + i: o for i, o in (aliases or {}).items()}
    n_in, n_out, n_scr = len(args), len(out_shape), len(scratch_shapes)
    n_si, n_so = (len(side.ins), len(side.outs)) if side is not None else (0, 0)
    if side is not None:
        aliases.update({n_pf + n_in + n_si - side.n_alias + i: n_out + i for i in range(side.n_alias)})

    def carrying(*refs):
        pf, refs = refs[:n_pf], refs[n_pf:]
        ins, s_in = refs[:n_in], refs[n_in:n_in + n_si]
        outs, s_out = refs[n_in + n_si:n_in + n_si + n_out], refs[n_in + n_si + n_out:n_in + n_si + n_out + n_so]
        scr = refs[n_in + n_si + n_out + n_so:]
        if side is None:
            return body(*pf, *ins, *outs, *scr)
        copies = side.make(s_in, s_out, scr[n_scr], scr[n_scr + 1])
        ids = [pl.program_id(d) for d in range(len(grid))]
        first, last = ids[0] == 0, ids[0] == grid[0] - 1
        for d in range(1, len(grid)):
            first, last = first & (ids[d] == 0), last & (ids[d] == grid[d] - 1)

        @pl.when(first)
        def _():
            for cp in copies:
                cp.start()

        body(*pf, *ins, *outs, *scr[:n_scr])

        @pl.when(last)
        def _():
            for cp in copies:
                cp.wait()

    sems = [pltpu.SemaphoreType.DMA((side.n,)), pltpu.SemaphoreType.DMA((side.n,))] if side is not None else []
    outs = pl.pallas_call(
        carrying, grid_spec=pltpu.PrefetchScalarGridSpec(
            num_scalar_prefetch=n_pf, grid=grid, in_specs=list(in_specs) + [ANY] * n_si, out_specs=list(out_specs) + [ANY] * n_so,
            scratch_shapes=list(scratch_shapes) + sems),
        out_shape=list(out_shape) + (side.outs if side is not None else []), input_output_aliases=aliases,
        compiler_params=_params(*(semantics if side is None else ["arbitrary"] * len(grid))), name=name)(
            *prefetch, *args, *(side.ins if side is not None else []))
    return list(outs) if side is None else (list(outs[:n_out]), list(outs[n_out:]))


def _call1(body, args, *, out_spec, out_shape, side=None, **kw):
    got = _call(body, args, out_specs=[out_spec], out_shape=[out_shape], side=side, **kw)
    return got[0] if side is None else (got[0][0], got[1])


MM_VMEM_BUDGET = 40 * 1024 * 1024
MM_STEP_MACS = 2200 * 1024 * 1024
MXU_WIDTH = 256
MM_STEP_COST_BYTES = 1 << 20


def _divisors(n):
    return [t for t in range(LANES, n + 1, LANES) if n % t == 0] or [n]


def _mm_tiles(M, N, K, a_bytes, b_bytes, o_bytes):
    best = None
    for tk in _divisors(K):
        for tm in _divisors(M):
            for tn in _divisors(N):
                nk = K // tk
                foot = 2 * (tm * tk * a_bytes + tk * tn * b_bytes + tm * tn * o_bytes) + (tm * tn * 4 if nk > 1 else 0)
                if foot > MM_VMEM_BUDGET or tm * tn * tk > MM_STEP_MACS or tn < min(N, MXU_WIDTH) or tm < min(M, MXU_WIDTH):
                    continue
                steps = (M // tm) * (N // tn) * nk
                traffic = M * K * a_bytes * (N // tn if nk > 1 else 1) + K * N * b_bytes * (M // tm) + M * N * o_bytes
                exposed = tm * tk * a_bytes + tk * tn * b_bytes + tm * tn * o_bytes
                cost = traffic + exposed + steps * MM_STEP_COST_BYTES + (nk - 1) * M * N * 8
                if best is None or cost < best[0]:
                    best = (cost, tm, tn, tk)
    assert best is not None, (M, N, K)
    return best[1:]


def _mm(a, b, dims, out_dtype, name, res=None, bl=None, side=None, twin=None):
    bs = b.shape[1:] if bl is not None else b.shape
    if dims == "nn":
        (M, K), (K2, N) = a.shape, bs
    elif dims == "nt":
        (M, K), (N, K2) = a.shape, bs
    else:
        (K, M), (K2, N) = a.shape, bs
    assert K == K2, (name, a.shape, b.shape)
    tm, tn, tk = _mm_tiles(M, N, K, a.dtype.itemsize, b.dtype.itemsize, jnp.dtype(out_dtype).itemsize
                           + (res.dtype.itemsize if res is not None else 0) + (jnp.dtype(twin).itemsize if twin is not None else 0))
    nk = K // tk
    lead = (None,) if bl is not None else ()
    pre = (lambda *ix: (bl,) + ix) if bl is not None else (lambda *ix: ix)
    if dims == "tn":
        a_spec = pl.BlockSpec((tk, tm), lambda i, j, k: (k, i))
    else:
        a_spec = pl.BlockSpec((tm, tk), lambda i, j, k: (i, k))
    if dims == "nt":
        b_spec = pl.BlockSpec(lead + (tn, tk), lambda i, j, k: pre(j, k))
    else:
        b_spec = pl.BlockSpec(lead + (tk, tn), lambda i, j, k: pre(k, j))
    o_spec = pl.BlockSpec((tm, tn), lambda i, j, k: (i, j))
    in_specs, args = [a_spec, b_spec], [a, b]
    if res is not None:
        in_specs.append(o_spec)
        args.append(res)
    n_main = len(args)
    n_out = 1 if twin is None else 2

    def body(*refs):
        a_ref, b_ref = refs[0], refs[1]
        r_ref = refs[2] if res is not None else None
        o_ref = refs[n_main]
        p = lax.dot_general(a_ref[...].astype(BF), b_ref[...].astype(BF), _DN[dims], preferred_element_type=F32)

        def finish(t):
            if r_ref is not None:
                t = t + r_ref[...]
            o_ref[...] = t.astype(out_dtype)
            if twin is not None:
                refs[n_main + 1][...] = t.astype(twin)

        if nk == 1:
            finish(p)
        else:
            acc = refs[n_main + n_out]
            k = pl.program_id(2)

            @pl.when(k == 0)
            def _():
                acc[...] = p

            @pl.when(k > 0)
            def _():
                acc[...] += p

            @pl.when(k == nk - 1)
            def _():
                finish(acc[...])

    got = _call(body, args, grid=(M // tm, N // tn, nk), in_specs=in_specs, out_specs=[o_spec] * n_out,
                out_shape=[SDS((M, N), out_dtype)] + ([SDS((M, N), twin)] if twin is not None else []),
                scratch_shapes=[pltpu.VMEM((tm, tn), F32)] if nk > 1 else [], semantics=("parallel", "parallel", "arbitrary"),
                name=name, side=side)
    outs, landed = (got, None) if side is None else got
    out = outs[0] if twin is None else (outs[0], outs[1])
    return out if side is None else (out, landed)


def _rms(x, g):
    return x * lax.rsqrt(jnp.mean(x * x, axis=-1, keepdims=True) + EPS) * g


def _ln_silu(x, g, b):
    mu = jnp.mean(x, axis=-1, keepdims=True)
    xc = x - mu
    var = jnp.mean(xc * xc, axis=-1, keepdims=True)
    return jax.nn.silu(xc * lax.rsqrt(var + EPS) * g + b)


def _merge(gc, gp, yc, yp, ps):
    return jax.nn.sigmoid(gc) * yc + jax.nn.sigmoid(gp) * (yp * ps)


def _gated(gate, val):
    return jax.nn.gelu(gate) * val


def _rms_fwd(x, g, name):
    T, D = x.shape
    tb = _tile(T, ROW_BLOCK, 8)

    def body(x_ref, g_ref, o_ref):
        o_ref[...] = _rms(x_ref[...], g_ref[...]).astype(BF)

    row = pl.BlockSpec((tb, D), lambda i: (i, 0))
    return pl.pallas_call(body, grid=(T // tb,), in_specs=[row, pl.BlockSpec((1, D), lambda i: (0, 0))], out_specs=row,
                          out_shape=SDS((T, D), BF), compiler_params=_params("parallel"), name=name)(x, g.reshape(1, D))


def _rms_bwd(x, g, dh, dres, name):
    T, D = x.shape
    tb = _tile(T, ROW_BLOCK, 8)

    def body(*refs):
        if dres is not None:
            x_ref, g_ref, dh_ref, dres_ref, dx_ref, dxb_ref, dg_ref = refs
        else:
            x_ref, g_ref, dh_ref, dx_ref, dxb_ref, dg_ref = refs
        _, vjp = jax.vjp(_rms, x_ref[...], g_ref[...])
        dx, dg = vjp(dh_ref[...].astype(F32))
        if dres is not None:
            dx = dx + dres_ref[...]
        dx_ref[...] = dx
        dxb_ref[...] = dx.astype(BF)

        @pl.when(pl.program_id(0) == 0)
        def _():
            dg_ref[...] = jnp.zeros_like(dg_ref)

        dg_ref[...] += dg

    row = pl.BlockSpec((tb, D), lambda i: (i, 0))
    vec = pl.BlockSpec((1, D), lambda i: (0, 0))
    ins = [x, g.reshape(1, D), dh] + ([dres] if dres is not None else [])
    return pl.pallas_call(
        body, grid=(T // tb,), in_specs=[row, vec, row] + ([row] if dres is not None else []), out_specs=[row, row, vec],
        out_shape=[SDS((T, D), F32), SDS((T, D), BF), SDS((1, D), F32)], compiler_params=_params("arbitrary"), name=name)(*ins)


def _row_tile(M, K, N, per_row_bytes):
    fixed = K * N * 2
    fit = [t for t in _divisors(M) if fixed + 2 * t * per_row_bytes <= MM_VMEM_BUDGET and t * K * N <= MM_STEP_MACS]
    return max(fit) if fit else min(_divisors(M))


def _mm_rms_fwd(a, b, res, g, name, side=None):
    M, K = a.shape
    N = b.shape[2]
    tm = _row_tile(M, K, N, K * 2 + N * (4 + 4 + 2))

    def body(a_ref, b_ref, r_ref, g_ref, x_ref, h_ref):
        x = r_ref[...] + lax.dot_general(a_ref[...], b_ref[...], _DN["nn"], preferred_element_type=F32)
        x_ref[...] = x
        h_ref[...] = _rms(x, g_ref[...]).astype(BF)

    row = pl.BlockSpec((tm, N), lambda i: (i, 0))
    return _call(body, (a, b, res, g.reshape(1, N)), grid=(M // tm,),
                 in_specs=[pl.BlockSpec((tm, K), lambda i: (i, 0)), pl.BlockSpec((None, K, N), lambda i: (0, 0, 0), pipeline_mode=pl.Buffered(1)), row,
                           pl.BlockSpec((1, N), lambda i: (0, 0))],
                 out_specs=[row, row], out_shape=[SDS((M, N), F32), SDS((M, N), BF)], semantics=("parallel",), name=name, side=side)


def _mm_rms_bwd(a, b, x, g, dres, name, side=None):
    M, K = a.shape
    N = b.shape[1]
    tm = _row_tile(M, K, N, K * 2 + N * (4 + 4 + 4 + 2))

    def body(a_ref, b_ref, x_ref, g_ref, r_ref, dx_ref, dxb_ref, dg_ref):
        dh = lax.dot_general(a_ref[...], b_ref[...], _DN["nt"], preferred_element_type=F32)
        _, vjp = jax.vjp(_rms, x_ref[...], g_ref[...])
        dx, dg = vjp(dh)
        dx = dx + r_ref[...]
        dx_ref[...] = dx
        dxb_ref[...] = dx.astype(BF)

        @pl.when(pl.program_id(0) == 0)
        def _():
            dg_ref[...] = jnp.zeros_like(dg_ref)

        dg_ref[...] += dg

    row = pl.BlockSpec((tm, N), lambda i: (i, 0))
    vec = pl.BlockSpec((1, N), lambda i: (0, 0))
    return _call(
        body, (a, b, x, g.reshape(1, N), dres), grid=(M // tm,),
        in_specs=[pl.BlockSpec((tm, K), lambda i: (i, 0)), pl.BlockSpec((None, N, K), lambda i: (0, 0, 0), pipeline_mode=pl.Buffered(1)), row, vec, row],
        out_specs=[row, row, vec], out_shape=[SDS((M, N), F32), SDS((M, N), BF), SDS((1, N), F32)],
        semantics=("arbitrary",), name=name, side=side)


def _loss_bwd(x, g, target, name):
    T, D = x.shape
    tb = _tile(T, ROW_BLOCK, 8)
    nb = T // tb

    def body(x_ref, g_ref, t_ref, loss_ref, dx_ref, dg_ref, acc):
        i = pl.program_id(0)
        y, vjp = jax.vjp(_rms, x_ref[...], g_ref[...])
        err = y - t_ref[...]
        dx, dg = vjp(err * (1.0 / D))
        dx_ref[...] = dx

        @pl.when(i == 0)
        def _():
            dg_ref[...] = jnp.zeros_like(dg_ref)
            acc[...] = jnp.zeros_like(acc)

        dg_ref[...] += dg
        acc[...] += jnp.sum(err * err, axis=0, keepdims=True)

        @pl.when(i == nb - 1)
        def _():
            loss_ref[...] = jnp.full(loss_ref.shape, (0.5 / D) * jnp.sum(acc[...]), F32)

    row = pl.BlockSpec((tb, D), lambda i: (i, 0))
    vec = pl.BlockSpec((1, D), lambda i: (0, 0))
    return pl.pallas_call(
        body, grid=(nb,), in_specs=[row, vec, row], out_specs=[pl.BlockSpec((1, LANES), lambda i: (0, 0)), row, vec],
        out_shape=[SDS((1, LANES), F32), SDS((T, D), F32), SDS((1, D), F32)], scratch_shapes=[pltpu.VMEM((1, D), F32)],
        compiler_params=_params("arbitrary"), name=name)(x, g.reshape(1, D), target)


def _ln_silu_fwd(cv, g, b, name):
    T, C = cv.shape
    tb = _tile(T, ROW_BLOCK, 8)

    def body(x_ref, g_ref, b_ref, o_ref):
        o_ref[...] = _ln_silu(x_ref[...], g_ref[...], b_ref[...]).astype(BF)

    row = pl.BlockSpec((tb, C), lambda i: (i, 0))
    vec = pl.BlockSpec((1, C), lambda i: (0, 0))
    return pl.pallas_call(body, grid=(T // tb,), in_specs=[row, vec, vec], out_specs=row, out_shape=SDS((T, C), BF),
                          compiler_params=_params("parallel"), name=name)(cv, g.reshape(1, C), b.reshape(1, C))


def _ln_silu_bwd(cv, g, b, dy, name, side=None):
    T, C = cv.shape
    tb = _tile(T, ROW_BLOCK, 8)

    def body(x_ref, g_ref, b_ref, dy_ref, dx_ref, dg_ref, db_ref):
        _, vjp = jax.vjp(_ln_silu, x_ref[...], g_ref[...], b_ref[...])
        dx, dg, db = vjp(dy_ref[...].astype(F32))
        dx_ref[...] = dx

        @pl.when(pl.program_id(0) == 0)
        def _():
            dg_ref[...] = jnp.zeros_like(dg_ref)
            db_ref[...] = jnp.zeros_like(db_ref)

        dg_ref[...] += dg
        db_ref[...] += db

    row = pl.BlockSpec((tb, C), lambda i: (i, 0))
    vec = pl.BlockSpec((1, C), lambda i: (0, 0))
    return _call(
        body, (cv, g.reshape(1, C), b.reshape(1, C), dy), grid=(T // tb,), in_specs=[row, vec, vec, row], out_specs=[row, vec, vec],
        out_shape=[SDS((T, C), F32), SDS((1, C), F32), SDS((1, C), F32)], semantics=("arbitrary",), name=name, side=side)


def _merge_fwd(proj, yc, yp, ps, C, name, side=None):
    T, D = yc.shape
    tb = _tile(T, ROW_BLOCK, 8)
    nj = D // C

    def body(gc_ref, gp_ref, yc_ref, yp_ref, ps_ref, o_ref):
        o_ref[...] = _merge(gc_ref[...], gp_ref[...], yc_ref[...].astype(F32), yp_ref[...].astype(F32), ps_ref[...]).astype(BF)

    blk = pl.BlockSpec((tb, C), lambda i, j: (i, j))
    return _call1(
        body, (proj, proj, yc, yp, ps.reshape(1, D)), grid=(T // tb, nj),
        in_specs=[pl.BlockSpec((tb, C), lambda i, j: (i, 3 + j)), pl.BlockSpec((tb, C), lambda i, j: (i, 3 + nj + j)), blk, blk,
                  pl.BlockSpec((1, C), lambda i, j: (0, j))],
        out_spec=blk, out_shape=SDS((T, D), BF), semantics=("parallel", "parallel"), name=name, side=side)


def _merge_bwd(proj, yc, yp, ps, dm, C, name, side=None):
    T, D = yc.shape
    tb = _tile(T, ROW_BLOCK, 8)
    nj = D // C

    def body(gc_ref, gp_ref, yc_ref, yp_ref, ps_ref, dm_ref, dgc_ref, dgp_ref, dyc_ref, dyp_ref, dps_ref):
        _, vjp = jax.vjp(_merge, gc_ref[...], gp_ref[...], yc_ref[...].astype(F32), yp_ref[...].astype(F32), ps_ref[...])
        dgc, dgp, dyc, dyp, dps = vjp(dm_ref[...].astype(F32))
        dgc_ref[...] = dgc.astype(BF)
        dgp_ref[...] = dgp.astype(BF)
        dyc_ref[...] = dyc.astype(BF)
        dyp_ref[...] = dyp.astype(BF)

        @pl.when(pl.program_id(1) == 0)
        def _():
            dps_ref[...] = jnp.zeros_like(dps_ref)

        dps_ref[...] += dps

    blk = pl.BlockSpec((tb, C), lambda j, i: (i, j))
    vec = pl.BlockSpec((1, C), lambda j, i: (0, j))
    return _call(
        body, (proj, proj, yc, yp, ps.reshape(1, D), dm), grid=(nj, T // tb),
        in_specs=[pl.BlockSpec((tb, C), lambda j, i: (i, 3 + j)), pl.BlockSpec((tb, C), lambda j, i: (i, 3 + nj + j)), blk, blk, vec, blk],
        out_specs=[blk, blk, blk, blk, vec], out_shape=[SDS((T, D), BF)] * 4 + [SDS((1, D), F32)],
        semantics=("parallel", "arbitrary"), name=name, side=side)


def _shd(v, s, rows):
    if s == 0:
        return v
    return jnp.where(rows >= s, pltpu.roll(v, s, 0), 0.0)


def _shu(v, s, rows):
    if s == 0:
        return v
    n = v.shape[0]
    return jnp.where(rows < n - s, pltpu.roll(v, n - s, 0), 0.0)


def _glu_conv_fwd(proj, w, b, Bn, S, C, name, side=None):
    K = w.shape[0]
    sl = min(LANES, C)
    ns = C // sl

    def body(a_ref, gl_ref, w_ref, b_ref, o_ref):
        y0 = a_ref[...] * jax.nn.sigmoid(gl_ref[...])
        rows = lax.broadcasted_iota(jnp.int32, y0.shape, 0)
        acc = jnp.zeros_like(y0) + b_ref[...]
        for k in range(K):
            acc = acc + w_ref[k:k + 1, :] * _shd(y0, K - 1 - k, rows)
        o_ref[...] = acc

    return _call1(
        body, (proj, proj, w, b.reshape(1, C)), grid=(Bn, ns),
        in_specs=[pl.BlockSpec((S, sl), lambda bi, j: (bi, j)), pl.BlockSpec((S, sl), lambda bi, j: (bi, ns + j)),
                  pl.BlockSpec((K, sl), lambda bi, j: (0, j)), pl.BlockSpec((1, sl), lambda bi, j: (0, j))],
        out_spec=pl.BlockSpec((S, sl), lambda bi, j: (bi, j)), out_shape=SDS((Bn * S, C), F32),
        semantics=("parallel", "parallel"), name=name, side=side)


def _glu_conv_bwd(proj, w, dcv, Bn, S, C, name, side=None):
    K = w.shape[0]
    sl = min(LANES, C)
    ns = C // sl

    def body(a_ref, gl_ref, w_ref, d_ref, da_ref, dgl_ref, dw_ref, db_ref):
        a = a_ref[...]
        sg = jax.nn.sigmoid(gl_ref[...])
        y0 = a * sg
        d = d_ref[...]
        rows = lax.broadcasted_iota(jnp.int32, y0.shape, 0)

        @pl.when(pl.program_id(1) == 0)
        def _():
            dw_ref[...] = jnp.zeros_like(dw_ref)
            db_ref[...] = jnp.zeros_like(db_ref)

        dy0 = jnp.zeros_like(y0)
        for k in range(K):
            s = K - 1 - k
            dw_ref[k:k + 1, :] += jnp.sum(d * _shd(y0, s, rows), axis=0, keepdims=True)
            dy0 = dy0 + w_ref[k:k + 1, :] * _shu(d, s, rows)
        db_ref[...] += jnp.sum(d, axis=0, keepdims=True)
        da_ref[...] = (dy0 * sg).astype(BF)
        dgl_ref[...] = (dy0 * a * sg * (1.0 - sg)).astype(BF)

    blk = pl.BlockSpec((S, sl), lambda j, bi: (bi, j))
    return _call(
        body, (proj, proj, w, dcv), grid=(ns, Bn),
        in_specs=[blk, pl.BlockSpec((S, sl), lambda j, bi: (bi, ns + j)), pl.BlockSpec((K, sl), lambda j, bi: (0, j)), blk],
        out_specs=[blk, blk, pl.BlockSpec((K, sl), lambda j, bi: (0, j)), pl.BlockSpec((1, sl), lambda j, bi: (0, j))],
        out_shape=[SDS((Bn * S, C), BF), SDS((Bn * S, C), BF), SDS((K, C), F32), SDS((1, C), F32)],
        semantics=("parallel", "arbitrary"), name=name, side=side)


def _pool_z(u, g, rows):
    s2 = u + _shd(u, 1, rows)
    s4 = s2 + _shd(s2, 2, rows)
    s8 = s4 + _shd(s4, 4, rows)
    s16 = s8 + _shd(s8, 8, rows)
    sw = jnp.where(g == 0, s2, jnp.where(g == 1, s4, jnp.where(g == 2, s8, s16)))
    cnt = jnp.minimum(rows + 1, POOL_WINDOWS[0] << g).astype(F32)
    return sw / cnt - u, cnt


def _pool_fwd(proj, wpt, l, Bn, S, C, D, name):
    G = len(POOL_WINDOWS)
    gd, go = C // G, D // G

    def body(u_ref, w_ref, o_ref):
        g = pl.program_id(1)
        u = u_ref[...]
        rows = lax.broadcasted_iota(jnp.int32, u.shape, 0)
        zp, _ = _pool_z(u, g, rows)
        o_ref[...] = lax.dot_general(zp.astype(BF), w_ref[...], _DN["nt"], preferred_element_type=F32).astype(BF)

    return pl.pallas_call(
        body, grid=(Bn, G),
        in_specs=[pl.BlockSpec((S, gd), lambda bi, g: (bi, 2 * G + g)), pl.BlockSpec((None, go, gd), lambda bi, g: (l * G + g, 0, 0))],
        out_specs=pl.BlockSpec((S, go), lambda bi, g: (bi, g)), out_shape=SDS((Bn * S, D), BF),
        compiler_params=_params("parallel", "parallel"), name=name)(proj, wpt)


def _pool_bwd(proj, wpt, dyp, l, Bn, S, C, D, name):
    G = len(POOL_WINDOWS)
    gd, go = C // G, D // G

    def body(u_ref, w_ref, d_ref, du_ref, dw_ref):
        g = pl.program_id(0)
        u = u_ref[...]
        rows = lax.broadcasted_iota(jnp.int32, u.shape, 0)
        zp, cnt = _pool_z(u, g, rows)
        d = d_ref[...]
        dzp = lax.dot_general(d, w_ref[...], _DN["nn"], preferred_element_type=F32)

        @pl.when(pl.program_id(1) == 0)
        def _():
            dw_ref[...] = jnp.zeros_like(dw_ref)

        dw_ref[...] += lax.dot_general(d, zp.astype(BF), _DN["tn"], preferred_element_type=F32)
        dsw = dzp / cnt
        zero = jnp.zeros_like(dsw)
        d16 = jnp.where(g == 3, dsw, zero)
        d8 = jnp.where(g == 2, dsw, zero) + d16 + _shu(d16, 8, rows)
        d4 = jnp.where(g == 1, dsw, zero) + d8 + _shu(d8, 4, rows)
        d2 = jnp.where(g == 0, dsw, zero) + d4 + _shu(d4, 2, rows)
        d1 = d2 + _shu(d2, 1, rows)
        du_ref[...] = (d1 - dzp).astype(BF)

    return pl.pallas_call(
        body, grid=(G, Bn),
        in_specs=[pl.BlockSpec((S, gd), lambda g, bi: (bi, 2 * G + g)), pl.BlockSpec((None, go, gd), lambda g, bi: (l * G + g, 0, 0)),
                  pl.BlockSpec((S, go), lambda g, bi: (bi, g))],
        out_specs=[pl.BlockSpec((S, gd), lambda g, bi: (bi, g)), pl.BlockSpec((None, go, gd), lambda g, bi: (g, 0, 0))],
        out_shape=[SDS((Bn * S, C), BF), SDS((G, go, gd), F32)],
        compiler_params=_params("parallel", "arbitrary"), name=name)(proj, wpt, dyp)


def _ffn_conv(u, w_ref, rows):
    K = w_ref.shape[0]
    acc = w_ref[K - 1:K, :] * u
    for k in range(K - 1):
        acc = acc + w_ref[k:k + 1, :] * _shd(u, K - 1 - k, rows)
    return acc


def _ffn_cb(F):
    return _tile(F, 256)


def _ffn_act_fwd(up0, w, Bn, S, F, name, side=None):
    cb = _ffn_cb(F)
    nj = F // cb

    def body(g_ref, v_ref, wg_ref, wv_ref, o_ref):
        rows = lax.broadcasted_iota(jnp.int32, g_ref.shape, 0)
        o_ref[...] = _gated(_ffn_conv(g_ref[...], wg_ref, rows), _ffn_conv(v_ref[...], wv_ref, rows)).astype(BF)

    K = w.shape[0]
    return _call1(
        body, (up0, up0, w, w), grid=(Bn, nj),
        in_specs=[pl.BlockSpec((S, cb), lambda bi, j: (bi, j)), pl.BlockSpec((S, cb), lambda bi, j: (bi, nj + j)),
                  pl.BlockSpec((K, cb), lambda bi, j: (0, j)), pl.BlockSpec((K, cb), lambda bi, j: (0, nj + j))],
        out_spec=pl.BlockSpec((S, cb), lambda bi, j: (bi, j)), out_shape=SDS((Bn * S, F), BF),
        semantics=("parallel", "parallel"), name=name, side=side)


SUBLANES = 8
FFN_HALO = SUBLANES
FFN_ROWS = 64
GELU_C0, GELU_C1 = 0.7978845608028654, 0.044715


def _gelu_and_grad(x):
    x2 = x * x
    t = jnp.tanh(GELU_C0 * (x + GELU_C1 * (x2 * x)))
    cdf = 0.5 * (1.0 + t)
    return x * cdf, cdf + (0.5 * GELU_C0) * x * (1.0 - t * t) * (1.0 + (3.0 * GELU_C1) * x2)


def _ffn_act_bwd(up0, w, dg, Bn, S, F, name, side=None):
    cb = min(LANES, F)
    nj = F // cb
    K = w.shape[0]
    rc = FFN_ROWS if S % FFN_ROWS == 0 else S
    win = rc + 2 * FFN_HALO
    assert K - 1 <= FFN_HALO and rc % SUBLANES == 0

    def body(g_ref, v_ref, wg_ref, wv_ref, d_ref, dgo_ref, dvo_ref, dwg_ref, dwv_ref, gp, vp, dp):
        for pad, src in ((gp, g_ref), (vp, v_ref), (dp, d_ref)):
            pad[0:FFN_HALO, :] = jnp.zeros((FFN_HALO, cb), F32)
            pad[FFN_HALO + S:, :] = jnp.zeros((FFN_HALO, cb), F32)
            pad[FFN_HALO:FFN_HALO + S, :] = src[...].astype(F32)
        wg = [wg_ref[k:k + 1, :] for k in range(K)]
        wv = [wv_ref[k:k + 1, :] for k in range(K)]

        def taps(u):
            return [pltpu.roll(u, K - 1 - k, 0) for k in range(K - 1)] + [u]

        def conv(us, ws):
            acc = ws[K - 1] * us[K - 1]
            for k in range(K - 1):
                acc = acc + ws[k] * us[k]
            return acc

        def conv_t(dc, ws):
            acc = ws[K - 1] * dc
            for k in range(K - 1):
                acc = acc + ws[k] * pltpu.roll(dc, win - (K - 1 - k), 0)
            return acc

        def fold(t):
            acc = t[FFN_HALO:FFN_HALO + SUBLANES]
            for i in range(1, rc // SUBLANES):
                acc = acc + t[FFN_HALO + SUBLANES * i:FFN_HALO + SUBLANES * (i + 1)]
            return acc

        def chunk(c, sums):
            r0 = pl.multiple_of(c * rc, SUBLANES)
            gs, vs, d = taps(gp[pl.ds(r0, win), :]), taps(vp[pl.ds(r0, win), :]), dp[pl.ds(r0, win), :]
            ge, dge = _gelu_and_grad(conv(gs, wg))
            dgc = d * conv(vs, wv) * dge
            dvc = d * ge
            dgo_ref[pl.ds(r0, rc), :] = conv_t(dgc, wg)[FFN_HALO:FFN_HALO + rc].astype(BF)
            dvo_ref[pl.ds(r0, rc), :] = conv_t(dvc, wv)[FFN_HALO:FFN_HALO + rc].astype(BF)
            new = [fold(dc * u) for us, dc in ((gs, dgc), (vs, dvc)) for u in us]
            return tuple(a + b for a, b in zip(sums, new))

        sums = lax.fori_loop(0, S // rc, chunk, tuple(jnp.zeros((SUBLANES, cb), F32) for _ in range(2 * K)))

        @pl.when(pl.program_id(1) == 0)
        def _():
            dwg_ref[...] = jnp.zeros_like(dwg_ref)
            dwv_ref[...] = jnp.zeros_like(dwv_ref)

        for k in range(K):
            dwg_ref[k:k + 1, :] += jnp.sum(sums[k], axis=0, keepdims=True)
            dwv_ref[k:k + 1, :] += jnp.sum(sums[K + k], axis=0, keepdims=True)

    blk = pl.BlockSpec((S, cb), lambda j, bi: (bi, j))
    wblk = pl.BlockSpec((K, cb), lambda j, bi: (0, j))
    return _call(
        body, (up0, up0, w, w, dg), grid=(nj, Bn),
        in_specs=[blk, pl.BlockSpec((S, cb), lambda j, bi: (bi, nj + j)), wblk, pl.BlockSpec((K, cb), lambda j, bi: (0, nj + j)), blk],
        out_specs=[blk, blk, wblk, wblk],
        out_shape=[SDS((Bn * S, F), BF), SDS((Bn * S, F), BF), SDS((K, F), F32), SDS((K, F), F32)],
        scratch_shapes=[pltpu.VMEM((S + 2 * FFN_HALO, cb), F32)] * 3, semantics=("parallel", "arbitrary"), name=name, side=side)


def _softmax_rows(q, k, scale):
    sc = lax.dot_general(q, k, _DN["nt"], preferred_element_type=F32) * scale
    e = jnp.exp(sc - jnp.max(sc, axis=-1, keepdims=True))
    return e / jnp.sum(e, axis=-1, keepdims=True)


def _attn_ts(S):
    return _tile(S, 1024, 8)


def _attn_fwd(q, kv, Bn, S, Mn, D, name, side=None):
    H = XA_HEADS
    dh = D // H
    ts = _attn_ts(S)
    nsb = S // ts
    scale = dh ** -0.5

    def body(q_ref, k_ref, v_ref, o_ref):
        p = _softmax_rows(q_ref[...], k_ref[...], scale)
        o_ref[...] = lax.dot_general(p.astype(BF), v_ref[...], _DN["nn"], preferred_element_type=F32).astype(BF)

    qblk = pl.BlockSpec((ts, dh), lambda bi, h, s: (bi * nsb + s, h))
    return _call1(
        body, (q, kv, kv), grid=(Bn, H, nsb),
        in_specs=[qblk, pl.BlockSpec((Mn, dh), lambda bi, h, s: (bi, h)), pl.BlockSpec((Mn, dh), lambda bi, h, s: (bi, H + h))],
        out_spec=qblk, out_shape=SDS((Bn * S, D), BF), semantics=("parallel", "parallel", "parallel"), name=name, side=side)


def _attn_bwd(q, kv, datt, Bn, S, Mn, D, name):
    H = XA_HEADS
    dh = D // H
    ts = _attn_ts(S)
    nsb = S // ts
    scale = dh ** -0.5

    def body(q_ref, k_ref, v_ref, do_ref, dq_ref, dk_ref, dv_ref):
        q, k, v, do = q_ref[...], k_ref[...], v_ref[...], do_ref[...]
        p = _softmax_rows(q, k, scale)
        dp = lax.dot_general(do, v, _DN["nt"], preferred_element_type=F32)
        ds = (p * (dp - jnp.sum(dp * p, axis=-1, keepdims=True)) * scale).astype(BF)
        dq_ref[...] = lax.dot_general(ds, k, _DN["nn"], preferred_element_type=F32).astype(BF)

        @pl.when(pl.program_id(2) == 0)
        def _():
            dk_ref[...] = jnp.zeros_like(dk_ref)
            dv_ref[...] = jnp.zeros_like(dv_ref)

        dk_ref[...] += lax.dot_general(ds, q, _DN["tn"], preferred_element_type=F32)
        dv_ref[...] += lax.dot_general(p.astype(BF), do, _DN["tn"], preferred_element_type=F32)

    qblk = pl.BlockSpec((ts, dh), lambda bi, h, s: (bi * nsb + s, h))
    kblk = pl.BlockSpec((Mn, dh), lambda bi, h, s: (bi, h))
    return pl.pallas_call(
        body, grid=(Bn, H, nsb),
        in_specs=[qblk, kblk, pl.BlockSpec((Mn, dh), lambda bi, h, s: (bi, H + h)), qblk],
        out_specs=[qblk, kblk, kblk], out_shape=[SDS((Bn * S, D), BF), SDS((Bn * Mn, D), F32), SDS((Bn * Mn, D), F32)],
        compiler_params=_params("parallel", "parallel", "arbitrary"), name=name)(q, kv, kv, datt)


class _Sides:
    def __init__(self, by_key=None, on_land=None):
        self.by_key, self.landed, self.on_land = dict(by_key or {}), {}, on_land

    def run(self, key, fn, *args, **kw):
        side = self.by_key.get(key)
        if side is None:
            return fn(*args, **kw)
        out, self.landed[key] = fn(*args, side=side() if callable(side) else side, **kw)
        if self.on_land is not None:
            self.on_land(key, self.landed[key])
        return out

    def mm(self, key, *args, **kw):
        return self.run(key, _mm, *args, **kw)


def _layer_fwd(x, h, mem_n, W, V, l, dims, sides, next_g):
    Bn, S, Mn, D, C, F = dims
    n = f"l{l}_"
    proj = sides.mm("proj", h, W["w_in"], "nn", F32, n + "proj", bl=0)
    cv = sides.run("glu_conv", _glu_conv_fwd, proj, V["conv_dw_w"][l], V["conv_dw_b"][l], Bn, S, C, n + "glu_conv")
    yc1 = _ln_silu_fwd(cv, V["conv_ln_g"][l], V["conv_ln_b"][l], n + "ln_silu")
    yc = sides.mm("conv_out", yc1, W["w_conv_out"], "nn", BF, n + "conv_out", bl=0)
    yp = _pool_fwd(proj, W["w_pool"], 0, Bn, S, C, D, n + "pool")
    merged = sides.run("merge", _merge_fwd, proj, yc, yp, V["pool_scale"][l], C, n + "merge")
    x1, hq = sides.run("out_proj", _mm_rms_fwd, merged, W["w_out"], x, V["xattn_norm_g"][l], n + "out_proj")
    q = sides.mm("q_proj", hq, W["w_q"], "nn", BF, n + "q_proj", bl=0)
    kv = _mm(mem_n, W["w_kv"], "nn", BF, n + "kv_proj", bl=0)
    att = sides.run("attn", _attn_fwd, q, kv, Bn, S, Mn, D, n + "attn")
    x2, hf = sides.run("o_proj", _mm_rms_fwd, att, W["w_o"], x1, V["ffn_norm_g"][l], n + "o_proj")
    up0 = sides.mm("up_proj", hf, W["w_up"], "nn", F32, n + "up_proj", bl=0)
    gact = sides.run("ffn_act", _ffn_act_fwd, up0, V["ffn_dw_w"][l], Bn, S, F, n + "ffn_act")
    if next_g is not None:
        x3, h3 = sides.run("down_proj", _mm_rms_fwd, gact, W["w_down"], x2, next_g, n + "down_proj")
    else:
        x3, h3 = sides.mm("down_proj", gact, W["w_down"], "nn", F32, n + "down_proj", res=x2, bl=0), None
    return x3, h3, dict(x=x, h=h, proj=proj, cv=cv, yc1=yc1, yc=yc, yp=yp, merged=merged, x1=x1, hq=hq, q=q, kv=kv, att=att, x2=x2,
                        hf=hf, up0=up0, gact=gact)


def _layer_bwd_mlp(dx, dxb, sv, W, V, l, dims, sides):
    Bn, S, Mn, D, C, F = dims
    n = f"l{l}_b_"
    gw, sm = {}, {}
    dgact = sides.mm("d_gact", dxb, W["w_down"], "nt", BF, n + "d_gact", bl=0)
    gw["w_down"] = sides.mm("dw_down", sv["gact"], dxb, "tn", F32, n + "dw_down", twin=BF)
    dg0, dv0, dwg, dwv = sides.run("ffn_act_b", _ffn_act_bwd, sv["up0"], V["ffn_dw_w"][l], dgact, Bn, S, F, n + "ffn_act")
    sm["ffn_dw_w"] = jnp.concatenate([dwg, dwv], axis=1)
    dup0 = jnp.concatenate([dg0, dv0], axis=1)
    dx2, dx2b, sm["ffn_norm_g"] = _mm_rms_bwd(dup0, W["w_up"], sv["x2"], V["ffn_norm_g"][l], dx, n + "d_hf")
    gw["w_up"] = sides.mm("dw_up", sv["hf"], dup0, "tn", F32, n + "dw_up", twin=BF)
    return dx2, dx2b, gw, sm


def _layer_bwd_mix(dx2, dx2b, dmem_n, sv, mem_n, W, V, l, dims, sides, gw):
    Bn, S, Mn, D, C, F = dims
    n = f"l{l}_b_"
    sm = {}
    datt = sides.mm("d_att", dx2b, W["w_o"], "nt", BF, n + "d_att", bl=0)
    gw["w_o"] = _mm(sv["att"], dx2b, "tn", F32, n + "dw_o", twin=BF)
    dq, dk, dv = _attn_bwd(sv["q"], sv["kv"], datt, Bn, S, Mn, D, n + "attn")
    dkv = jnp.concatenate([dk, dv], axis=1)
    gw["w_kv"] = _mm(mem_n, dkv, "tn", F32, n + "dw_kv", twin=BF)
    dmem_n = _mm(dkv, W["w_kv"], "nt", F32, n + "d_mem", res=dmem_n, bl=0)
    dx1, dx1b, sm["xattn_norm_g"] = _mm_rms_bwd(dq, W["w_q"], sv["x1"], V["xattn_norm_g"][l], dx2, n + "d_hq")
    gw["w_q"] = _mm(sv["hq"], dq, "tn", F32, n + "dw_q", twin=BF)
    dmerged = sides.mm("d_merged", dx1b, W["w_out"], "nt", BF, n + "d_merged", bl=0)
    gw["w_out"] = _mm(sv["merged"], dx1b, "tn", F32, n + "dw_out", twin=BF)
    dgc, dgp, dyc, dyp, sm["pool_scale"] = sides.run("merge_b", _merge_bwd, sv["proj"], sv["yc"], sv["yp"], V["pool_scale"][l], dmerged, C, n + "merge")
    du, dwp = _pool_bwd(sv["proj"], W["w_pool"], dyp, 0, Bn, S, C, D, n + "pool")
    gw["w_pool"] = (dwp, dwp.astype(BF))
    dyc1 = _mm(dyc, W["w_conv_out"], "nt", F32, n + "d_yc1", bl=0)
    gw["w_conv_out"] = _mm(sv["yc1"], dyc, "tn", F32, n + "dw_conv_out", twin=BF)
    dcv, sm["conv_ln_g"], sm["conv_ln_b"] = sides.run("ln_silu_b", _ln_silu_bwd, sv["cv"], V["conv_ln_g"][l], V["conv_ln_b"][l], dyc1, n + "ln_silu")
    da, dgl, sm["conv_dw_w"], sm["conv_dw_b"] = sides.run("glu_conv_b", _glu_conv_bwd, sv["proj"], V["conv_dw_w"][l], dcv, Bn, S, C, n + "glu_conv")
    dproj = jnp.concatenate([da, dgl, du, dgc, dgp], axis=1)
    dx, dxb, sm["mix_norm_g"] = sides.run("d_h", _mm_rms_bwd, dproj, W["w_in"], sv["x"], V["mix_norm_g"][l], dx1, n + "d_h")
    gw["w_in"] = sides.mm("dw_in", sv["h"], dproj, "tn", F32, n + "dw_in", twin=BF)
    return dx, dxb, dmem_n, sm


BIG = (("w_in", "col"), ("w_conv_out", "col"), ("w_pool", "row"), ("w_out", "row"), ("w_q", "row"), ("w_kv", "col"),
       ("w_o", "row"), ("w_up", "col"), ("w_down", "row"))
ALL_RELS = (1, 2, 3)
GATHER_FIRST = ("w_in", "w_conv_out", "w_pool", "w_out", "w_q", "w_o")
FWD_CARRY = {
    (0, "proj"): (("w_up", 0, (1, 2)),),
    (0, "glu_conv"): (("w_up", 0, (3,)),),
    (0, "merge"): (("w_kv", 0, ALL_RELS),),
    (0, "attn"): (("w_down", 0, (1, 2)),),
    (0, "o_proj"): (("w_down", 0, (3,)),),
    (0, "up_proj"): (("w_in", 1, ALL_RELS), ("w_conv_out", 1, ALL_RELS), ("w_pool", 1, ALL_RELS)),
    (0, "ffn_act"): (("w_out", 1, ALL_RELS), ("w_q", 1, ALL_RELS), ("w_kv", 1, ALL_RELS)),
    (0, "down_proj"): (("w_o", 1, ALL_RELS),),
    (1, "proj"): (("w_up", 1, (1, 2)),),
    (1, "glu_conv"): (("w_up", 1, (3,)),),
    (1, "merge"): (("w_down", 1, (1, 2)),),
    (1, "attn"): (("w_down", 1, (3,)),),
}
PASS_GROUPS = ((("w_kv", 0),), (("w_up", 0), ("w_down", 0)),
               (("w_in", 1), ("w_conv_out", 1), ("w_pool", 1), ("w_out", 1), ("w_q", 1), ("w_kv", 1), ("w_o", 1)),
               (("w_up", 1), ("w_down", 1)))
EARLY = ("w_down", "w_up")
BWD_CARRY_EARLY = {"merge_b": ("w_down",), "glu_conv_b": ("w_up",)}
BWD_CARRY_LATE = {"ffn_act_b": ("w_in", "w_conv_out", "w_pool", "w_out", "w_q", "w_kv", "w_o")}
BWD_LAST_LAYER = (("att", ("w_o", "w_kv", "w_q"), "d_merged", {"d_h": ("w_o", "w_kv", "w_q")}),
                  ("tok", ("w_out", "w_pool", "w_conv_out"), "ln_silu_b", {"dw_in": ("w_out", "w_pool", "w_conv_out")}))


def _place():
    xi, yi, ci = lax.axis_index("x"), lax.axis_index("y"), lax.axis_index("c")
    return xi, yi, ci, 2 * xi + yi


def _chip_peer(xi, yi, ci, r):
    return (xi ^ (r >> 1), yi ^ (r & 1), ci)


def _full_shard(ref, kind, k, cs):
    if kind == "col":
        return ref.at[:, :, :, :, pl.ds(pl.multiple_of(k * cs, cs), cs)]
    return ref.at[:, :, k]


def _gather_weights(shards, kinds):
    n = len(shards)
    outs = []
    for s, kind in zip(shards, kinds):
        L, P, _, RH, CS = s.shape
        outs.append(SDS((L, P, 2, RH, CS * N_CHIPS) if kind == "col" else (L, P, N_CHIPS, 2, RH, CS), s.dtype))
    per = 7

    def body(*refs):
        srcs, fulls, (ssem, rsem) = refs[:n], refs[n:2 * n], refs[2 * n:]
        xi, yi, ci, j = _place()
        sib = (xi, yi, 1 - ci)

        def piece(i, k, c):
            kind, cs = kinds[i], shards[i].shape[-1]
            if kind == "col":
                return fulls[i].at[:, :, c, :, pl.ds(pl.multiple_of(k * cs, cs), cs)]
            return fulls[i].at[:, :, k, c]

        def copy(i, slot, src, dst, dev):
            return pltpu.make_async_remote_copy(src_ref=src, dst_ref=dst, send_sem=ssem.at[per * i + slot], recv_sem=rsem.at[per * i + slot],
                                                device_id=dev, device_id_type=MESH)

        own, first, passed = [], [], []
        for i in range(n):
            for r in (1, 2, 3):
                first.append(copy(i, r - 1, srcs[i].at[:, :, ci], piece(i, j, ci), _chip_peer(xi, yi, ci, r)))
                first[-1].start()
        for i in range(n):
            own.append(copy(i, 6, srcs[i], _full_shard(fulls[i], kinds[i], j, shards[i].shape[-1]), sib))
            own[-1].start()
        for i in range(n):
            for r in (1, 2, 3):
                got = piece(i, j ^ r, ci)
                copy(i, r - 1, got, got, sib).wait_recv()
                passed.append(copy(i, 2 + r, got, got, sib))
                passed[-1].start()
        for i in range(n):
            for r in (1, 2, 3):
                got = piece(i, j ^ r, 1 - ci)
                copy(i, 2 + r, got, got, sib).wait_recv()
        for cp in own:
            cp.wait()
        for cp in first + passed:
            cp.wait_send()

    return pl.pallas_call(
        body, in_specs=[ANY] * n, out_specs=[ANY] * n, out_shape=outs,
        scratch_shapes=[pltpu.SemaphoreType.DMA((per * n,)), pltpu.SemaphoreType.DMA((per * n,))], name="gather_weights")(*shards)


def _full_sds(s, kind):
    L, P, _, RH, CS = s.shape
    return SDS((L, P, 2, RH, CS * N_CHIPS) if kind == "col" else (L, P, N_CHIPS, 2, RH, CS), s.dtype)


def _gather_piece(full, kind, cs, k, c):
    if kind == "col":
        return full.at[:, :, c, :, pl.ds(pl.multiple_of(k * cs, cs), cs)]
    return full.at[:, :, k, c]


def _side_gather(shards, kinds, rels, fulls):
    n = len(shards)

    def make(srcs, outs, ssem, rsem):
        xi, yi, ci, j = _place()
        return [pltpu.make_async_remote_copy(
            src_ref=srcs[i].at[:, :, ci], dst_ref=_gather_piece(outs[i], kinds[i], shards[i].shape[-1], j, ci), send_sem=ssem.at[3 * i + r - 1],
            recv_sem=rsem.at[3 * i + r - 1], device_id=_chip_peer(xi, yi, ci, r), device_id_type=MESH) for i in range(n) for r in rels[i]]

    prior = [f for f in fulls if f is not None]
    assert len(prior) in (0, n)
    return _Side(list(shards) + prior, [_full_sds(s, k) for s, k in zip(shards, kinds)], 3 * n, make, n_alias=len(prior))


def _gather_pass(fulls, shards, kinds, name):
    n = len(fulls)

    def body(*refs):
        srcs, outs, (ssem, rsem) = refs[n:2 * n], refs[2 * n:3 * n], refs[3 * n:]
        xi, yi, ci, j = _place()
        sib = (xi, yi, 1 - ci)
        cps = []
        for i in range(n):
            cs = shards[i].shape[-1]
            for r in (1, 2, 3):
                got = _gather_piece(outs[i], kinds[i], cs, j ^ r, ci)
                cps.append(pltpu.make_async_remote_copy(src_ref=got, dst_ref=got, send_sem=ssem.at[4 * i + r - 1], recv_sem=rsem.at[4 * i + r - 1],
                                                        device_id=sib, device_id_type=MESH))
            cps.append(pltpu.make_async_remote_copy(src_ref=srcs[i], dst_ref=_full_shard(outs[i], kinds[i], j, cs), send_sem=ssem.at[4 * i + 3],
                                                    recv_sem=rsem.at[4 * i + 3], device_id=sib, device_id_type=MESH))
        for cp in cps:
            cp.start()
        for cp in cps:
            cp.wait()

    return pl.pallas_call(
        body, in_specs=[ANY] * (2 * n), out_specs=[ANY] * n, out_shape=[SDS(f.shape, f.dtype) for f in fulls],
        input_output_aliases={i: i for i in range(n)},
        scratch_shapes=[pltpu.SemaphoreType.DMA((4 * n,)), pltpu.SemaphoreType.DMA((4 * n,))], name=name)(*fulls, *shards)


def _sibling_exchange(gviews, kinds, name):
    n = len(gviews)
    outs = [SDS(g.shape[:1] + g.shape[2:] if kind == "col" else g.shape[:2] + g.shape[3:], g.dtype) for g, kind in zip(gviews, kinds)]

    def body(*refs):
        gs, lands, (ssem, rsem) = refs[:n], refs[n:2 * n], refs[2 * n:]
        xi, yi, ci, _ = _place()
        cps = []
        for i in range(n):
            src = gs[i].at[:, 1 - ci] if kinds[i] == "col" else gs[i].at[:, :, 1 - ci]
            cps.append(pltpu.make_async_remote_copy(src_ref=src, dst_ref=lands[i], send_sem=ssem.at[i], recv_sem=rsem.at[i],
                                                    device_id=(xi, yi, 1 - ci), device_id_type=MESH))
            cps[-1].start()
        for cp in cps:
            cp.wait()

    return pl.pallas_call(body, in_specs=[ANY] * n, out_specs=[ANY] * n, out_shape=outs,
                          scratch_shapes=[pltpu.SemaphoreType.DMA((n,)), pltpu.SemaphoreType.DMA((n,))], name=name)(*gviews)


def _side_sibling_exchange(gviews, kinds):
    outs = [SDS(g.shape[:1] + g.shape[2:] if kind == "col" else g.shape[:2] + g.shape[3:], g.dtype) for g, kind in zip(gviews, kinds)]

    def make(gs, lands, ssem, rsem):
        xi, yi, ci, _ = _place()
        return [pltpu.make_async_remote_copy(src_ref=gs[i].at[:, 1 - ci] if kinds[i] == "col" else gs[i].at[:, :, 1 - ci], dst_ref=lands[i],
                                             send_sem=ssem.at[i], recv_sem=rsem.at[i], device_id=(xi, yi, 1 - ci), device_id_type=MESH)
                for i in range(len(gs))]

    return _Side(gviews, outs, len(gviews), make)


def _chip_sums(gs, lands, kinds, jc, name):
    n = len(gs)
    args, in_specs, out_specs, out_shape = [], [], [], []
    for g, land, kind in zip(gs, lands, kinds):
        if kind == "col":
            P, _, RH, C = g.shape
            CS = C // N_CHIPS
            in_specs += [pl.BlockSpec((P, None, RH, CS), lambda r, jc: (0, jc[1], 0, jc[0] ^ r)),
                         pl.BlockSpec((P, RH, CS), lambda r, jc: (0, 0, jc[0] ^ r))]
        else:
            P, _, _, RH, CS = g.shape
            in_specs += [pl.BlockSpec((P, None, None, RH, CS), lambda r, jc: (0, jc[0] ^ r, jc[1], 0, 0)),
                         pl.BlockSpec((P, None, RH, CS), lambda r, jc: (0, jc[0] ^ r, 0, 0))]
        args += [g, land]
        out_specs += [pl.BlockSpec((P, RH, CS), lambda r, jc: (0, 0, 0)), pl.BlockSpec((None, P, RH, CS), lambda r, jc: (r, 0, 0, 0))]
        out_shape += [SDS((P, RH, CS), F32), SDS((N_CHIPS, P, RH, CS), BF)]

    def body(jc_ref, *refs):
        ins, outs = refs[:2 * n], refs[2 * n:]
        for i in range(n):
            s = ins[2 * i][...] + ins[2 * i + 1][...].astype(F32)
            outs[2 * i + 1][...] = s.astype(BF)

            @pl.when(pl.program_id(0) == 0)
            def _():
                outs[2 * i][...] = s

    outs = _call(body, args, grid=(N_CHIPS,), in_specs=in_specs, out_specs=out_specs, out_shape=out_shape, semantics=("arbitrary",),
                 name=name, prefetch=(jc,))
    return outs[0::2], outs[1::2]


def _chip_exchange_copies(srcs, lands, ssem, rsem):
    xi, yi, ci, _ = _place()
    return [pltpu.make_async_remote_copy(src_ref=srcs[i].at[r], dst_ref=lands[i].at[r], send_sem=ssem.at[3 * i + r - 1],
                                         recv_sem=rsem.at[3 * i + r - 1], device_id=_chip_peer(xi, yi, ci, r), device_id_type=MESH)
            for i in range(len(srcs)) for r in (1, 2, 3)]


def _side_chip_exchange(pieces):
    return _Side(pieces, [SDS(p.shape, p.dtype) for p in pieces], 3 * len(pieces), _chip_exchange_copies)


FINAL_SUM_STEPS = 2


def _final_sums(owns, lands, jc, shards, l, L, name, side=None):
    n = len(owns)
    args, in_specs, out_specs, out_shape = [], [], [], []
    for own, land in zip(owns, lands):
        P, RH, CS = own.shape
        hr = RH // FINAL_SUM_STEPS
        in_specs += [pl.BlockSpec((P, hr, CS), lambda h, jc: (0, h, 0))]
        in_specs += [pl.BlockSpec((None, P, hr, CS), functools.partial(lambda r, h, jc: (r, 0, h, 0), r)) for r in (1, 2, 3)]
        args += [own, land, land, land]
        out_specs.append(pl.BlockSpec((None, P, None, hr, CS), lambda h, jc: (l, 0, jc[1], h, 0)))
        out_shape.append(SDS((L, P, 2, RH, CS), F32))
    aliases = None
    if shards is not None:
        aliases = {4 * n + i: i for i in range(n)}
        in_specs += [ANY] * n
        args += list(shards)

    def body(jc_ref, *refs):
        outs = refs[len(args):]
        for i in range(n):
            o, a, b, c = (refs[4 * i + t][...] for t in range(4))
            outs[i][...] = ((o + a.astype(F32)) + b.astype(F32)) + c.astype(F32)

    return _call(body, args, grid=(FINAL_SUM_STEPS,), in_specs=in_specs, out_specs=out_specs, out_shape=out_shape, semantics=("arbitrary",),
                 name=name, prefetch=(jc,), aliases=aliases, side=side)


def _halves_exchange(shards, l, name):
    n = len(shards)

    def body(*refs):
        outs, (ssem, rsem) = refs[n:2 * n], refs[2 * n:]
        xi, yi, ci, _ = _place()
        cps = []
        for i in range(n):
            mine = outs[i].at[l, :, ci]
            cps.append(pltpu.make_async_remote_copy(src_ref=mine, dst_ref=mine, send_sem=ssem.at[i], recv_sem=rsem.at[i],
                                                    device_id=(xi, yi, 1 - ci), device_id_type=MESH))
            cps[-1].start()
        for i in range(n):
            land = outs[i].at[l, :, 1 - ci]
            pltpu.make_async_remote_copy(src_ref=land, dst_ref=land, send_sem=ssem.at[i], recv_sem=rsem.at[i],
                                         device_id=(xi, yi, 1 - ci), device_id_type=MESH).wait_recv()
        for cp in cps:
            cp.wait_send()

    return pl.pallas_call(body, in_specs=[ANY] * n, out_specs=[ANY] * n, out_shape=[SDS(s.shape, s.dtype) for s in shards],
                          input_output_aliases={i: i for i in range(n)},
                          scratch_shapes=[pltpu.SemaphoreType.DMA((n,)), pltpu.SemaphoreType.DMA((n,))], name=name)(*shards)


def _reduce_small(part, pieces):
    NR, Wd = part.shape
    ND = 2 * N_CHIPS
    n = len(pieces)

    def body(p_ref, *refs):
        srcs, o_ref, lands, (land, ssem, rsem, xs, xr) = refs[:n], refs[n], refs[n + 1:2 * n + 1], refs[2 * n + 1:]
        exchange = _chip_exchange_copies(srcs, lands, xs, xr)
        for cp in exchange:
            cp.start()
        xi, yi, ci, j = _place()
        me = 2 * j + ci
        land[me] = p_ref[...]
        cps = []
        for rr in range(1, ND):
            dev = (xi ^ (rr >> 2), yi ^ ((rr >> 1) & 1), ci ^ (rr & 1))
            cps.append(pltpu.make_async_remote_copy(src_ref=p_ref, dst_ref=land.at[me], send_sem=ssem.at[rr - 1], recv_sem=rsem.at[rr - 1],
                                                    device_id=dev, device_id_type=MESH))
            cps[-1].start()
        for rr in range(1, ND):
            got = land.at[me ^ rr]
            pltpu.make_async_remote_copy(src_ref=got, dst_ref=got, send_sem=ssem.at[rr - 1], recv_sem=rsem.at[rr - 1],
                                         device_id=(xi, yi, ci), device_id_type=MESH).wait_recv()
        acc = land[0]
        for d in range(1, ND):
            acc = acc + land[d]
        o_ref[...] = acc
        for cp in cps:
            cp.wait_send()
        for cp in exchange:
            cp.wait()

    vm = pl.BlockSpec(memory_space=pltpu.VMEM)
    outs = pl.pallas_call(
        body, in_specs=[vm] + [ANY] * n, out_specs=[vm] + [ANY] * n, out_shape=[SDS((NR, Wd), F32)] + [SDS(p.shape, p.dtype) for p in pieces],
        scratch_shapes=[pltpu.VMEM((ND, NR, Wd), F32), pltpu.SemaphoreType.DMA((ND - 1,)), pltpu.SemaphoreType.DMA((ND - 1,)),
                        pltpu.SemaphoreType.DMA((3 * n,)), pltpu.SemaphoreType.DMA((3 * n,))],
        name="small_grad_allreduce")(part, *pieces)
    return outs[0], list(outs[1:])


def _adamw_update(w_ref, g_ref, m_ref, v_ref, d_ref, mo_ref, vo_ref):
    g = g_ref[...]
    m = ADAM_B1 * m_ref[...] + (1.0 - ADAM_B1) * g
    v = ADAM_B2 * v_ref[...] + (1.0 - ADAM_B2) * jnp.square(g)
    m_hat = m / (1.0 - ADAM_B1 ** ADAM_STEP)
    v_hat = v / (1.0 - ADAM_B2 ** ADAM_STEP)
    d_ref[...] = -ADAM_LR * (m_hat / (jnp.sqrt(v_hat) + ADAM_EPS) + ADAM_WD * w_ref[...])
    mo_ref[...] = m
    vo_ref[...] = v


ADAMW_STEPS = 8


def _adamw_layer(ws, gs, ms, vs, prev, l, name, side=None):
    n = len(ws)
    args, in_specs, out_specs, out_shape = [], [], [], []
    for w, g, m, v in zip(ws, gs, ms, vs):
        L, R, C = w.shape
        blk = pl.BlockSpec((None, R // ADAMW_STEPS, C), lambda i: (l, i, 0))
        in_specs += [blk] * 4
        args += [w, g, m, v]
        out_specs += [blk] * 3
        out_shape += [SDS((L, R, C), F32)] * 3
    aliases = None
    if prev is not None:
        aliases = {4 * n + i: i for i in range(3 * n)}
        in_specs += [ANY] * (3 * n)
        args += list(prev)

    def body(*refs):
        outs = refs[len(args):]
        for i in range(n):
            _adamw_update(*refs[4 * i:4 * i + 4], *outs[3 * i:3 * i + 3])

    return _call(body, args, grid=(ADAMW_STEPS,), in_specs=in_specs, out_specs=out_specs, out_shape=out_shape, semantics=("parallel",),
                 name=name, aliases=aliases, side=side)


def _adamw(w, g, m, v, name):
    shape = w.shape
    C = shape[-1]
    R = w.size // C
    tb = _tile(R, max(8, (1 << 18) // C), 8)
    body = functools.partial(_adamw_update)
    blk = pl.BlockSpec((tb, C), lambda i: (i, 0))
    outs = pl.pallas_call(body, grid=(R // tb,), in_specs=[blk] * 4, out_specs=[blk] * 3, out_shape=[SDS((R, C), F32)] * 3,
                          compiler_params=_params("parallel"), name=name)(*[t.reshape(R, C) for t in (w, g, m, v)])
    return [t.reshape(shape) for t in outs]


WEIGHTS = ("mix_norm_g", "w_in", "conv_dw_w", "conv_dw_b", "conv_ln_g", "conv_ln_b", "w_conv_out", "w_pool_grp", "pool_scale", "w_out",
           "xattn_norm_g", "mem_norm_g", "w_q", "w_kv", "w_o", "ffn_norm_g", "w_up", "ffn_dw_w", "w_down", "final_norm_g")
VECTORS = ("mix_norm_g", "conv_dw_b", "conv_ln_g", "conv_ln_b", "pool_scale", "xattn_norm_g", "mem_norm_g", "ffn_norm_g", "final_norm_g")


def _shard_view(t, kind):
    L, P, R, C = t.shape
    return t.reshape(L, P, 2, R // 2, C)


def _rows(t, width):
    return t.reshape(-1, width)


def _pack(parts):
    return jnp.concatenate([jnp.pad(p, ((0, (-p.shape[0]) % 8), (0, 0))) for p in parts], axis=0)


def kernel(x, mem, mix_norm_g, w_in, conv_dw_w, conv_dw_b, conv_ln_g, conv_ln_b, w_conv_out, w_pool_grp, pool_scale, w_out, xattn_norm_g, mem_norm_g, w_q, w_kv, w_o, ffn_norm_g, w_up, ffn_dw_w, w_down, final_norm_g, loss_target, m_mix_norm_g, m_w_in, m_conv_dw_w, m_conv_dw_b, m_conv_ln_g, m_conv_ln_b, m_w_conv_out, m_w_pool_grp, m_pool_scale, m_w_out, m_xattn_norm_g, m_mem_norm_g, m_w_q, m_w_kv, m_w_o, m_ffn_norm_g, m_w_up, m_ffn_dw_w, m_w_down, m_final_norm_g, v_mix_norm_g, v_w_in, v_conv_dw_w, v_conv_dw_b, v_conv_ln_g, v_conv_ln_b, v_w_conv_out, v_w_pool_grp, v_pool_scale, v_w_out, v_xattn_norm_g, v_mem_norm_g, v_w_q, v_w_kv, v_w_o, v_ffn_norm_g, v_w_up, v_ffn_dw_w, v_w_down, v_final_norm_g):
    w = dict(mix_norm_g=mix_norm_g, w_in=w_in, conv_dw_w=conv_dw_w, conv_dw_b=conv_dw_b, conv_ln_g=conv_ln_g, conv_ln_b=conv_ln_b,
             w_conv_out=w_conv_out, w_pool_grp=w_pool_grp, pool_scale=pool_scale, w_out=w_out, xattn_norm_g=xattn_norm_g,
             mem_norm_g=mem_norm_g, w_q=w_q, w_kv=w_kv, w_o=w_o, ffn_norm_g=ffn_norm_g, w_up=w_up, ffn_dw_w=ffn_dw_w, w_down=w_down,
             final_norm_g=final_norm_g)
    m = dict(zip(WEIGHTS, (m_mix_norm_g, m_w_in, m_conv_dw_w, m_conv_dw_b, m_conv_ln_g, m_conv_ln_b, m_w_conv_out, m_w_pool_grp, m_pool_scale,
                           m_w_out, m_xattn_norm_g, m_mem_norm_g, m_w_q, m_w_kv, m_w_o, m_ffn_norm_g, m_w_up, m_ffn_dw_w, m_w_down, m_final_norm_g)))
    v = dict(zip(WEIGHTS, (v_mix_norm_g, v_w_in, v_conv_dw_w, v_conv_dw_b, v_conv_ln_g, v_conv_ln_b, v_w_conv_out, v_w_pool_grp, v_pool_scale,
                           v_w_out, v_xattn_norm_g, v_mem_norm_g, v_w_q, v_w_kv, v_w_o, v_ffn_norm_g, v_w_up, v_ffn_dw_w, v_w_down, v_final_norm_g)))
    xi, yi, ci, j = _place()
    jc = jnp.stack([j, ci]).astype(jnp.int32)
    L = w_in.shape[0]
    G = len(POOL_WINDOWS)
    kinds = dict(BIG)

    def to_mat(name, t):
        if name == "w_pool":
            return jnp.swapaxes(t, 2, 3)
        return t[:, None]

    def from_mat(name, t):
        if name == "w_pool":
            return jnp.swapaxes(t, 2, 3)
        return t[:, 0]

    src = {name: w["w_pool_grp" if name == "w_pool" else name] for name, _ in BIG}

    KC, cs_c = conv_dw_w.shape[1], conv_dw_w.shape[2]
    KF, cs_f = ffn_dw_w.shape[1], ffn_dw_w.shape[2]
    taps = jnp.concatenate([conv_dw_w.reshape(L * KC, cs_c), ffn_dw_w.reshape(L * KF * (cs_f // cs_c), cs_c)], axis=0)
    n_taps = taps.shape[0]
    taps = jnp.pad(taps, ((0, (-n_taps) % 16), (0, 0)))
    names = [name for name, _ in BIG]
    mats = {name: to_mat(name, src[name]).astype(BF) for name in names}

    def layer_shards(l, subset):
        return [_shard_view(mats[name][l:l + 1], kinds[name]) for name in subset]

    def as_weight(name, f):
        return f.reshape(G if name == "w_pool" else 1, -1, f.shape[-1])

    assert L == 2
    fulls = _gather_weights(layer_shards(0, GATHER_FIRST) + [_shard_view(taps[None, None], "row")], [kinds[name] for name in GATHER_FIRST] + ["row"])
    ready = {(name, 0): as_weight(name, f) for name, f in zip(GATHER_FIRST, fulls)}
    landing = {}
    taps_all = fulls[-1].reshape(N_CHIPS, -1, cs_c)[:, :n_taps]
    V = {name: w[name] for name in VECTORS}
    V["conv_dw_w"] = taps_all[:, :L * KC].reshape(N_CHIPS, L, KC, cs_c).transpose(1, 2, 0, 3).reshape(L, KC, N_CHIPS * cs_c)
    V["ffn_dw_w"] = taps_all[:, L * KC:].reshape(N_CHIPS, L, KF, cs_f).transpose(1, 2, 0, 3).reshape(L, KF, N_CHIPS * cs_f)

    Bn, S, D = x.shape
    Mn = mem.shape[1]
    dims = (Bn, S, Mn, D, conv_dw_b.shape[1], w_down.shape[1] * N_CHIPS)
    xt = x.reshape(Bn * S, D)
    memf = mem.reshape(Bn * Mn, D)
    mem_n = _rms_fwd(memf, V["mem_norm_g"], "mem_norm")

    class LayerWeights:
        def __init__(self, l):
            self.l = l

        def __getitem__(self, name):
            if (name, self.l) not in ready:
                group = next(g for g in PASS_GROUPS if (name, self.l) in g)
                done = _gather_pass([landing.pop(t) for t in group], [layer_shards(lw, [nm])[0] for nm, lw in group],
                                    [kinds[nm] for nm, _ in group], f"gather_pass_{name}_{self.l}")
                ready.update({t: as_weight(t[0], f) for t, f in zip(group, done)})
            return ready[(name, self.l)]

    def carried_gather(entries):
        return lambda: _side_gather([layer_shards(lw, [nm])[0] for nm, lw, _ in entries], [kinds[nm] for nm, _, _ in entries],
                                    [rels for _, _, rels in entries], [landing.get((nm, lw)) for nm, lw, _ in entries])

    saved, W = [], []
    ht = _rms_fwd(xt, V["mix_norm_g"][0], "l0_mix_norm")
    for l in range(L):
        mine = {key: entries for (cl, key), entries in FWD_CARRY.items() if cl == l}
        sides = _Sides({key: carried_gather(entries) for key, entries in mine.items()},
                       on_land=lambda key, fulls, mine=mine: landing.update({(nm, lw): f for (nm, lw, _), f in zip(mine[key], fulls)}))
        W.append(LayerWeights(l))
        xt, ht, sv = _layer_fwd(xt, ht, mem_n, W[l], V, l, dims, sides, V["mix_norm_g"][l + 1] if l + 1 < L else None)
        saved.append(sv)
    loss, dx, dgf = _loss_bwd(xt, V["final_norm_g"], loss_target.reshape(Bn * S, D), "loss")
    loss = lax.psum(loss[0, 0], ("x", "y", "c"))

    late_names = [name for name in names if name not in EARLY]

    def views(gw, subset, twin):
        out = []
        for name in subset:
            g = gw[name][twin] if gw[name][twin].ndim == 3 else gw[name][twin][None]
            P, R, C = g.shape
            out.append(g.reshape(P, 2, R // 2, C) if kinds[name] == "col" else g.reshape(P, N_CHIPS, 2, R // (2 * N_CHIPS), C))
        return out

    def group_kinds(subset):
        return [kinds[name] for name in subset]

    class Reduction:
        def __init__(self, gw, subset, l, tag, first, table):
            self.gw, self.subset, self.l, self.tag, self.first, self.table = gw, subset, l, tag, first, table

        def sides(self):
            by_key = {self.first: lambda: _side_sibling_exchange(views(self.gw, self.subset, 1), group_kinds(self.subset))}
            by_key.update({key: (lambda names_=names_: _side_chip_exchange([self.pieces[nm] for nm in names_])) for key, names_ in self.table.items()})
            return by_key

        def on_land(self, key, landed):
            if key == self.first:
                self.sums(landed)
            elif key in self.table:
                got[self.l].update(zip(self.table[key], landed))

        def sums(self, lands):
            own, pieces = _chip_sums(views(self.gw, self.subset, 0), lands, group_kinds(self.subset), jc, f"chip_sums_{self.tag}_l{self.l}")
            owns[self.l].update(zip(self.subset, own))
            self.pieces = dict(zip(self.subset, pieces))

    def riding(reductions):
        return _Sides({key: side for r in reductions for key, side in r.sides().items()},
                      on_land=lambda key, landed: [r.on_land(key, landed) for r in reductions])

    dxb, dmem_n = dx, None
    smalls, owns, got = [None] * L, [{} for _ in range(L)], [{} for _ in range(L)]
    late = None
    for l in reversed(range(L)):
        dx, dxb, gw, sm = _layer_bwd_mlp(dx, dxb, saved[l], W[l], V, l, dims, riding([late] if late is not None else []))
        gw_mix = {}
        reductions = [Reduction(gw, EARLY, l, "mlp", "d_att", BWD_CARRY_EARLY)]
        if l == 0:
            reductions += [Reduction(gw_mix, names_, 0, tag, first, table) for tag, names_, first, table in BWD_LAST_LAYER]
        dx, dxb, dmem_n, sm2 = _layer_bwd_mix(dx, dxb, dmem_n, saved[l], mem_n, W[l], V, l, dims, riding(reductions), gw_mix)
        smalls[l] = {**sm, **sm2}
        late = Reduction(gw_mix, late_names, l, "mix", "d_gact", BWD_CARRY_LATE) if l > 0 else None
    last = Reduction(gw_mix, ("w_in",), 0, "in", None, {})
    last.sums(_sibling_exchange(views(gw_mix, last.subset, 1), group_kinds(last.subset), "grad_sibling_exchange_in_l0"))
    grad_x = dx.reshape(Bn, S, D)
    _, _, dgm = _rms_bwd(memf, V["mem_norm_g"], dmem_n, None, "mem_norm_b")
    small = {k: jnp.stack([sm[k] for sm in smalls]) if k in ("conv_dw_w", "ffn_dw_w") else jnp.concatenate([sm[k] for sm in smalls], axis=0)
             for k in smalls[0]}
    small["mem_norm_g"] = dgm
    small["final_norm_g"] = dgf

    small_w = conv_dw_b.shape[1]
    order = VECTORS + ("conv_dw_w", "ffn_dw_w")
    parts = [_rows(small[name], small_w) for name in order]
    counts = [p.shape[0] for p in parts]
    summed, landed_last = _reduce_small(_pack(parts), [last.pieces[name] for name in last.subset])
    got[0].update(zip(last.subset, landed_last))

    keys = ["w_pool_grp" if name == "w_pool" else name for name in names]
    rows3 = lambda t: t.reshape(t.shape[0], -1, t.shape[-1])
    wmv = [[rows3(to_mat(name, d[key])) for name, key in zip(names, keys)] for d in (w, m, v)]
    gshards, updates = None, None
    for l in reversed(range(L)):
        gshards = _final_sums([owns[l][name] for name in names], [got[l][name] for name in names], jc, gshards, l, L, f"final_sums_l{l}")
        gshards = _halves_exchange(gshards, l, f"grad_halves_exchange_l{l}")
        updates = _adamw_layer(wmv[0], [rows3(t) for t in gshards], wmv[1], wmv[2], updates, l, f"adamw_l{l}")
    grads, delta, new_m, new_v = {}, {}, {}, {}
    for i, (name, key) in enumerate(zip(names, keys)):
        Lg, P, _, RH, CS = gshards[i].shape
        grads[key] = from_mat(name, gshards[i].reshape(Lg, P, 2 * RH, CS))
        for d, t in zip((delta, new_m, new_v), updates[3 * i:3 * i + 3]):
            d[key] = from_mat(name, t.reshape(Lg, P, 2 * RH, CS))

    off = 0
    for name, cnt in zip(order, counts):
        t = summed[off:off + cnt]
        off += cnt + (-cnt) % 8
        if name in VECTORS:
            grads[name] = t.reshape(w[name].shape)
        else:
            full = t.reshape(small[name].shape)
            cs = w[name].shape[2]
            grads[name] = lax.dynamic_slice_in_dim(full, j * cs, cs, axis=2)

    vec =[_pack([_rows(d[name], small_w) for name in VECTORS]) for d in (w, grads, m, v)]
    outs = _adamw(*vec, "adamw_vectors")
    off = 0
    for name in VECTORS:
        cnt = w[name].size // small_w
        for d, t in zip((delta, new_m, new_v), outs):
            d[name] = t[off:off + cnt].reshape(w[name].shape)
        off += cnt + (-cnt) % 8
    for name in ("conv_dw_w", "ffn_dw_w"):
        delta[name], new_m[name], new_v[name] = _adamw(w[name], grads[name], m[name], v[name], "adamw_" + name)

    return (loss, grad_x, *[grads[k] for k in WEIGHTS], *[delta[k] for k in WEIGHTS], *[new_m[k] for k in WEIGHTS], *[new_v[k] for k in WEIGHTS])
```

```python
import functools

import jax
import jax.numpy as jnp
from jax import lax
from jax.experimental import pallas as pl
from jax.experimental.pallas import tpu as pltpu

F32 = jnp.float32
BF = jnp.bfloat16
SDS = jax.ShapeDtypeStruct
MESH = pl.DeviceIdType.MESH
ANY = pl.BlockSpec(memory_space=pl.ANY)

EPS = 1e-6
XA_HEADS = 4
POOL_WINDOWS = (2, 4, 8, 16)
N_CHIPS = 4
ADAM_LR, ADAM_B1, ADAM_B2, ADAM_EPS, ADAM_WD, ADAM_STEP = 0.001, 0.9, 0.999, 1e-08, 0.01, 10

LANES = 128
ROW_BLOCK = 512
VMEM_LIMIT = 56 * 1024 * 1024


def _params(*sem):
    return pltpu.CompilerParams(dimension_semantics=sem if sem else None, vmem_limit_bytes=VMEM_LIMIT)


def _tile(n, cap, mult=LANES):
    if n <= cap:
        return n
    for t in range(cap - cap % mult, 0, -mult):
        if n % t == 0:
            return t
    return n


_DN = {"nn": (((1,), (0,)), ((), ())), "nt": (((1,), (1,)), ((), ())), "tn": (((0,), (0,)), ((), ()))}


class _Side:
    def __init__(self, ins, outs, n, make, n_alias=0):
        self.ins, self.outs, self.n, self.make, self.n_alias = list(ins), list(outs), n, make, n_alias


def _call(body, args, *, grid, in_specs, out_specs, out_shape, semantics, name, scratch_shapes=(), side=None, prefetch=(), aliases=None):
    n_pf = len(prefetch)
    aliases = {n_pf + i: o for i, o in (aliases or {}).items()}
    n_in, n_out, n_scr = len(args), len(out_shape), len(scratch_shapes)
    n_si, n_so = (len(side.ins), len(side.outs)) if side is not None else (0, 0)
    if side is not None:
        aliases.update({n_pf + n_in + n_si - side.n_alias + i: n_out + i for i in range(side.n_alias)})

    def carrying(*refs):
        pf, refs = refs[:n_pf], refs[n_pf:]
        ins, s_in = refs[:n_in], refs[n_in:n_in + n_si]
        outs, s_out = refs[n_in + n_si:n_in + n_si + n_out], refs[n_in + n_si + n_out:n_in + n_si + n_out + n_so]
        scr = refs[n_in + n_si + n_out + n_so:]
        if side is None:
            return body(*pf, *ins, *outs, *scr)
        copies = side.make(s_in, s_out, scr[n_scr], scr[n_scr + 1])
        ids = [pl.program_id(d) for d in range(len(grid))]
        first, last = ids[0] == 0, ids[0] == grid[0] - 1
        for d in range(1, len(grid)):
            first, last = first & (ids[d] == 0), last & (ids[d] == grid[d] - 1)

        @pl.when(first)
        def _():
            for cp in copies:
                cp.start()

        body(*pf, *ins, *outs, *scr[:n_scr])

        @pl.when(last)
        def _():
            for cp in copies:
                cp.wait()

    sems = [pltpu.SemaphoreType.DMA((side.n,)), pltpu.SemaphoreType.DMA((side.n,))] if side is not None else []
    outs = pl.pallas_call(
        carrying, grid_spec=pltpu.PrefetchScalarGridSpec(
            num_scalar_prefetch=n_pf, grid=grid, in_specs=list(in_specs) + [ANY] * n_si, out_specs=list(out_specs) + [ANY] * n_so,
            scratch_shapes=list(scratch_shapes) + sems),
        out_shape=list(out_shape) + (side.outs if side is not None else []), input_output_aliases=aliases,
        compiler_params=_params(*(semantics if side is None else ["arbitrary"] * len(grid))), name=name)(
            *prefetch, *args, *(side.ins if side is not None else []))
    return list(outs) if side is None else (list(outs[:n_out]), list(outs[n_out:]))


def _call1(body, args, *, out_spec, out_shape, side=None, **kw):
    got = _call(body, args, out_specs=[out_spec], out_shape=[out_shape], side=side, **kw)
    return got[0] if side is None else (got[0][0], got[1])


MM_VMEM_BUDGET = 40 * 1024 * 1024
MM_STEP_MACS = 2200 * 1024 * 1024
MXU_WIDTH = 256
MM_STEP_COST_BYTES = 1 << 20


def _divisors(n):
    return [t for t in range(LANES, n + 1, LANES) if n % t == 0] or [n]


def _mm_tiles(M, N, K, a_bytes, b_bytes, o_bytes, n_unit=None):
    best = None
    for tk in _divisors(K):
        for tm in _divisors(M):
            for tn in _divisors(N if n_unit is None else n_unit):
                nk = K // tk
                foot = 2 * (tm * tk * a_bytes + tk * tn * b_bytes + tm * tn * o_bytes) + (tm * tn * 4 if nk > 1 else 0)
                if foot > MM_VMEM_BUDGET or tm * tn * tk > MM_STEP_MACS or tn < min(N, MXU_WIDTH) or tm < min(M, MXU_WIDTH):
                    continue
                steps = (M // tm) * (N // tn) * nk
                traffic = M * K * a_bytes * (N // tn if nk > 1 else 1) + K * N * b_bytes * (M // tm) + M * N * o_bytes
                exposed = tm * tk * a_bytes + tk * tn * b_bytes + tm * tn * o_bytes
                cost = traffic + exposed + steps * MM_STEP_COST_BYTES + (nk - 1) * M * N * 8
                if best is None or cost < best[0]:
                    best = (cost, tm, tn, tk)
    assert best is not None, (M, N, K)
    return best[1:]


def _mm(a, b, dims, out_dtype, name, res=None, bl=None, side=None, twin=None, b_halves=False):
    bs = b.shape[1:] if bl is not None or b_halves else b.shape
    if dims == "nn":
        (M, K), (K2, N) = a.shape, bs
    elif dims == "nt":
        (M, K), (N, K2) = a.shape, bs
    else:
        (K, M), (K2, N) = a.shape, bs
    assert K == K2, (name, a.shape, b.shape)
    n_half = N
    if b_halves:
        assert dims == "tn" and bl is None
        N = 2 * n_half
    tm, tn, tk = _mm_tiles(M, N, K, a.dtype.itemsize, b.dtype.itemsize, jnp.dtype(out_dtype).itemsize
                           + (res.dtype.itemsize if res is not None else 0) + (jnp.dtype(twin).itemsize if twin is not None else 0),
                           n_unit=n_half if b_halves else None)
    nk = K // tk
    lead = (None,) if bl is not None or b_halves else ()
    pre = (lambda *ix: (bl,) + ix) if bl is not None else (lambda *ix: ix)
    if b_halves:
        per_half = n_half // tn
        pre = lambda k, j: (j // per_half, k, j % per_half)
    if dims == "tn":
        a_spec = pl.BlockSpec((tk, tm), lambda i, j, k: (k, i))
    else:
        a_spec = pl.BlockSpec((tm, tk), lambda i, j, k: (i, k))
    if dims == "nt":
        b_spec = pl.BlockSpec(lead + (tn, tk), lambda i, j, k: pre(j, k))
    else:
        b_spec = pl.BlockSpec(lead + (tk, tn), lambda i, j, k: pre(k, j))
    o_spec = pl.BlockSpec((tm, tn), lambda i, j, k: (i, j))
    in_specs, args = [a_spec, b_spec], [a, b]
    if res is not None:
        in_specs.append(o_spec)
        args.append(res)
    n_main = len(args)
    n_out = 1 if twin is None else 2

    def body(*refs):
        a_ref, b_ref = refs[0], refs[1]
        r_ref = refs[2] if res is not None else None
        o_ref = refs[n_main]
        p = lax.dot_general(a_ref[...].astype(BF), b_ref[...].astype(BF), _DN[dims], preferred_element_type=F32)

        def finish(t):
            if r_ref is not None:
                t = t + r_ref[...]
            o_ref[...] = t.astype(out_dtype)
            if twin is not None:
                refs[n_main + 1][...] = t.astype(twin)

        if nk == 1:
            finish(p)
        else:
            acc = refs[n_main + n_out]
            k = pl.program_id(2)

            @pl.when(k == 0)
            def _():
                acc[...] = p

            @pl.when(k > 0)
            def _():
                acc[...] += p

            @pl.when(k == nk - 1)
            def _():
                finish(acc[...])

    got = _call(body, args, grid=(M // tm, N // tn, nk), in_specs=in_specs, out_specs=[o_spec] * n_out,
                out_shape=[SDS((M, N), out_dtype)] + ([SDS((M, N), twin)] if twin is not None else []),
                scratch_shapes=[pltpu.VMEM((tm, tn), F32)] if nk > 1 else [], semantics=("parallel", "parallel", "arbitrary"),
                name=name, side=side)
    outs, landed = (got, None) if side is None else got
    out = outs[0] if twin is None else (outs[0], outs[1])
    return out if side is None else (out, landed)


def _rms(x, g):
    return x * lax.rsqrt(jnp.mean(x * x, axis=-1, keepdims=True) + EPS) * g


def _ln_silu(x, g, b):
    mu = jnp.mean(x, axis=-1, keepdims=True)
    xc = x - mu
    var = jnp.mean(xc * xc, axis=-1, keepdims=True)
    return jax.nn.silu(xc * lax.rsqrt(var + EPS) * g + b)


def _merge(gc, gp, yc, yp, ps):
    return jax.nn.sigmoid(gc) * yc + jax.nn.sigmoid(gp) * (yp * ps)


def _gated(gate, val):
    return jax.nn.gelu(gate) * val


def _rms_fwd(x, g, name):
    T, D = x.shape
    tb = _tile(T, ROW_BLOCK, 8)

    def body(x_ref, g_ref, o_ref):
        o_ref[...] = _rms(x_ref[...], g_ref[...]).astype(BF)

    row = pl.BlockSpec((tb, D), lambda i: (i, 0))
    return pl.pallas_call(body, grid=(T // tb,), in_specs=[row, pl.BlockSpec((1, D), lambda i: (0, 0))], out_specs=row,
                          out_shape=SDS((T, D), BF), compiler_params=_params("parallel"), name=name)(x, g.reshape(1, D))


def _rms_bwd(x, g, dh, dres, name):
    T, D = x.shape
    tb = _tile(T, ROW_BLOCK, 8)

    def body(*refs):
        if dres is not None:
            x_ref, g_ref, dh_ref, dres_ref, dx_ref, dxb_ref, dg_ref = refs
        else:
            x_ref, g_ref, dh_ref, dx_ref, dxb_ref, dg_ref = refs
        _, vjp = jax.vjp(_rms, x_ref[...], g_ref[...])
        dx, dg = vjp(dh_ref[...].astype(F32))
        if dres is not None:
            dx = dx + dres_ref[...]
        dx_ref[...] = dx
        dxb_ref[...] = dx.astype(BF)

        @pl.when(pl.program_id(0) == 0)
        def _():
            dg_ref[...] = jnp.zeros_like(dg_ref)

        dg_ref[...] += dg

    row = pl.BlockSpec((tb, D), lambda i: (i, 0))
    vec = pl.BlockSpec((1, D), lambda i: (0, 0))
    ins = [x, g.reshape(1, D), dh] + ([dres] if dres is not None else [])
    return pl.pallas_call(
        body, grid=(T // tb,), in_specs=[row, vec, row] + ([row] if dres is not None else []), out_specs=[row, row, vec],
        out_shape=[SDS((T, D), F32), SDS((T, D), BF), SDS((1, D), F32)], compiler_params=_params("arbitrary"), name=name)(*ins)


def _row_tile(M, K, N, per_row_bytes):
    fixed = K * N * 2
    fit = [t for t in _divisors(M) if fixed + 2 * t * per_row_bytes <= MM_VMEM_BUDGET and t * K * N <= MM_STEP_MACS]
    return max(fit) if fit else min(_divisors(M))


def _mm_rms_fwd(a, b, res, g, name, side=None):
    M, K = a.shape
    N = b.shape[2]
    tm = _row_tile(M, K, N, K * 2 + N * (4 + 4 + 2))

    def body(a_ref, b_ref, r_ref, g_ref, x_ref, h_ref):
        x = r_ref[...] + lax.dot_general(a_ref[...], b_ref[...], _DN["nn"], preferred_element_type=F32)
        x_ref[...] = x
        h_ref[...] = _rms(x, g_ref[...]).astype(BF)

    row = pl.BlockSpec((tm, N), lambda i: (i, 0))
    return _call(body, (a, b, res, g.reshape(1, N)), grid=(M // tm,),
                 in_specs=[pl.BlockSpec((tm, K), lambda i: (i, 0)), pl.BlockSpec((None, K, N), lambda i: (0, 0, 0), pipeline_mode=pl.Buffered(1)), row,
                           pl.BlockSpec((1, N), lambda i: (0, 0))],
                 out_specs=[row, row], out_shape=[SDS((M, N), F32), SDS((M, N), BF)], semantics=("parallel",), name=name, side=side)


def _mm_rms_bwd(a, b, x, g, dres, name, side=None, a_halves=False):
    M = a.shape[-2]
    N, K = b.shape[1:]
    tm = _row_tile(M, K, N, K * 2 + N * (4 + 4 + 4 + 2))

    def body(a_ref, b_ref, x_ref, g_ref, r_ref, dx_ref, dxb_ref, dg_ref):
        if a_halves:
            dh = (lax.dot_general(a_ref[0], b_ref[:, :K // 2], _DN["nt"], preferred_element_type=F32)
                  + lax.dot_general(a_ref[1], b_ref[:, K // 2:], _DN["nt"], preferred_element_type=F32))
        else:
            dh = lax.dot_general(a_ref[...], b_ref[...], _DN["nt"], preferred_element_type=F32)
        _, vjp = jax.vjp(_rms, x_ref[...], g_ref[...])
        dx, dg = vjp(dh)
        dx = dx + r_ref[...]
        dx_ref[...] = dx
        dxb_ref[...] = dx.astype(BF)

        @pl.when(pl.program_id(0) == 0)
        def _():
            dg_ref[...] = jnp.zeros_like(dg_ref)

        dg_ref[...] += dg

    row = pl.BlockSpec((tm, N), lambda i: (i, 0))
    vec = pl.BlockSpec((1, N), lambda i: (0, 0))
    return _call(
        body, (a, b, x, g.reshape(1, N), dres), grid=(M // tm,),
        in_specs=[pl.BlockSpec((2, tm, K // 2), lambda i: (0, i, 0)) if a_halves else pl.BlockSpec((tm, K), lambda i: (i, 0)),
                  pl.BlockSpec((None, N, K), lambda i: (0, 0, 0), pipeline_mode=pl.Buffered(1)), row, vec, row],
        out_specs=[row, row, vec], out_shape=[SDS((M, N), F32), SDS((M, N), BF), SDS((1, N), F32)],
        semantics=("arbitrary",), name=name, side=side)


def _loss_bwd(x, g, target, name):
    T, D = x.shape
    tb = _tile(T, ROW_BLOCK, 8)
    nb = T // tb

    def body(x_ref, g_ref, t_ref, loss_ref, dx_ref, dg_ref, acc):
        i = pl.program_id(0)
        y, vjp = jax.vjp(_rms, x_ref[...], g_ref[...])
        err = y - t_ref[...]
        dx, dg = vjp(err * (1.0 / D))
        dx_ref[...] = dx

        @pl.when(i == 0)
        def _():
            dg_ref[...] = jnp.zeros_like(dg_ref)
            acc[...] = jnp.zeros_like(acc)

        dg_ref[...] += dg
        acc[...] += jnp.sum(err * err, axis=0, keepdims=True)

        @pl.when(i == nb - 1)
        def _():
            loss_ref[...] = jnp.full(loss_ref.shape, (0.5 / D) * jnp.sum(acc[...]), F32)

    row = pl.BlockSpec((tb, D), lambda i: (i, 0))
    vec = pl.BlockSpec((1, D), lambda i: (0, 0))
    return pl.pallas_call(
        body, grid=(nb,), in_specs=[row, vec, row], out_specs=[pl.BlockSpec((1, LANES), lambda i: (0, 0)), row, vec],
        out_shape=[SDS((1, LANES), F32), SDS((T, D), F32), SDS((1, D), F32)], scratch_shapes=[pltpu.VMEM((1, D), F32)],
        compiler_params=_params("arbitrary"), name=name)(x, g.reshape(1, D), target)


def _ln_silu_fwd(cv, g, b, name):
    T, C = cv.shape
    tb = _tile(T, ROW_BLOCK, 8)

    def body(x_ref, g_ref, b_ref, o_ref):
        o_ref[...] = _ln_silu(x_ref[...], g_ref[...], b_ref[...]).astype(BF)

    row = pl.BlockSpec((tb, C), lambda i: (i, 0))
    vec = pl.BlockSpec((1, C), lambda i: (0, 0))
    return pl.pallas_call(body, grid=(T // tb,), in_specs=[row, vec, vec], out_specs=row, out_shape=SDS((T, C), BF),
                          compiler_params=_params("parallel"), name=name)(cv, g.reshape(1, C), b.reshape(1, C))


def _ln_silu_bwd(cv, g, b, dy, name, side=None):
    T, C = cv.shape
    tb = _tile(T, ROW_BLOCK, 8)

    def body(x_ref, g_ref, b_ref, dy_ref, dx_ref, dg_ref, db_ref):
        _, vjp = jax.vjp(_ln_silu, x_ref[...], g_ref[...], b_ref[...])
        dx, dg, db = vjp(dy_ref[...].astype(F32))
        dx_ref[...] = dx

        @pl.when(pl.program_id(0) == 0)
        def _():
            dg_ref[...] = jnp.zeros_like(dg_ref)
            db_ref[...] = jnp.zeros_like(db_ref)

        dg_ref[...] += dg
        db_ref[...] += db

    row = pl.BlockSpec((tb, C), lambda i: (i, 0))
    vec = pl.BlockSpec((1, C), lambda i: (0, 0))
    return _call(
        body, (cv, g.reshape(1, C), b.reshape(1, C), dy), grid=(T // tb,), in_specs=[row, vec, vec, row], out_specs=[row, vec, vec],
        out_shape=[SDS((T, C), F32), SDS((1, C), F32), SDS((1, C), F32)], semantics=("arbitrary",), name=name, side=side)


def _merge_fwd(proj, yc, yp, ps, C, name, side=None):
    T, D = yc.shape
    tb = _tile(T, ROW_BLOCK, 8)
    nj = D // C

    def body(gc_ref, gp_ref, yc_ref, yp_ref, ps_ref, o_ref):
        o_ref[...] = _merge(gc_ref[...], gp_ref[...], yc_ref[...].astype(F32), yp_ref[...].astype(F32), ps_ref[...]).astype(BF)

    blk = pl.BlockSpec((tb, C), lambda i, j: (i, j))
    return _call1(
        body, (proj, proj, yc, yp, ps.reshape(1, D)), grid=(T // tb, nj),
        in_specs=[pl.BlockSpec((tb, C), lambda i, j: (i, 3 + j)), pl.BlockSpec((tb, C), lambda i, j: (i, 3 + nj + j)), blk, blk,
                  pl.BlockSpec((1, C), lambda i, j: (0, j))],
        out_spec=blk, out_shape=SDS((T, D), BF), semantics=("parallel", "parallel"), name=name, side=side)


def _merge_bwd(proj, yc, yp, ps, dm, C, name, side=None):
    T, D = yc.shape
    tb = _tile(T, ROW_BLOCK, 8)
    nj = D // C

    def body(gc_ref, gp_ref, yc_ref, yp_ref, ps_ref, dm_ref, dgc_ref, dgp_ref, dyc_ref, dyp_ref, dps_ref):
        _, vjp = jax.vjp(_merge, gc_ref[...], gp_ref[...], yc_ref[...].astype(F32), yp_ref[...].astype(F32), ps_ref[...])
        dgc, dgp, dyc, dyp, dps = vjp(dm_ref[...].astype(F32))
        dgc_ref[...] = dgc.astype(BF)
        dgp_ref[...] = dgp.astype(BF)
        dyc_ref[...] = dyc.astype(BF)
        dyp_ref[...] = dyp.astype(BF)

        @pl.when(pl.program_id(1) == 0)
        def _():
            dps_ref[...] = jnp.zeros_like(dps_ref)

        dps_ref[...] += dps

    blk = pl.BlockSpec((tb, C), lambda j, i: (i, j))
    vec = pl.BlockSpec((1, C), lambda j, i: (0, j))
    return _call(
        body, (proj, proj, yc, yp, ps.reshape(1, D), dm), grid=(nj, T // tb),
        in_specs=[pl.BlockSpec((tb, C), lambda j, i: (i, 3 + j)), pl.BlockSpec((tb, C), lambda j, i: (i, 3 + nj + j)), blk, blk, vec, blk],
        out_specs=[blk, blk, blk, blk, vec], out_shape=[SDS((T, D), BF)] * 4 + [SDS((1, D), F32)],
        semantics=("parallel", "arbitrary"), name=name, side=side)


def _shd(v, s, rows):
    if s == 0:
        return v
    return jnp.where(rows >= s, pltpu.roll(v, s, 0), 0.0)


def _shu(v, s, rows):
    if s == 0:
        return v
    n = v.shape[0]
    return jnp.where(rows < n - s, pltpu.roll(v, n - s, 0), 0.0)


def _glu_conv_fwd(proj, w, b, Bn, S, C, name, side=None):
    K = w.shape[0]
    sl = min(LANES, C)
    ns = C // sl

    def body(a_ref, gl_ref, w_ref, b_ref, o_ref):
        y0 = a_ref[...] * jax.nn.sigmoid(gl_ref[...])
        rows = lax.broadcasted_iota(jnp.int32, y0.shape, 0)
        acc = jnp.zeros_like(y0) + b_ref[...]
        for k in range(K):
            acc = acc + w_ref[k:k + 1, :] * _shd(y0, K - 1 - k, rows)
        o_ref[...] = acc

    return _call1(
        body, (proj, proj, w, b.reshape(1, C)), grid=(Bn, ns),
        in_specs=[pl.BlockSpec((S, sl), lambda bi, j: (bi, j)), pl.BlockSpec((S, sl), lambda bi, j: (bi, ns + j)),
                  pl.BlockSpec((K, sl), lambda bi, j: (0, j)), pl.BlockSpec((1, sl), lambda bi, j: (0, j))],
        out_spec=pl.BlockSpec((S, sl), lambda bi, j: (bi, j)), out_shape=SDS((Bn * S, C), F32),
        semantics=("parallel", "parallel"), name=name, side=side)


def _glu_conv_bwd(proj, w, dcv, Bn, S, C, name, side=None):
    K = w.shape[0]
    sl = min(LANES, C)
    ns = C // sl

    def body(a_ref, gl_ref, w_ref, d_ref, da_ref, dgl_ref, dw_ref, db_ref):
        a = a_ref[...]
        sg = jax.nn.sigmoid(gl_ref[...])
        y0 = a * sg
        d = d_ref[...]
        rows = lax.broadcasted_iota(jnp.int32, y0.shape, 0)

        @pl.when(pl.program_id(1) == 0)
        def _():
            dw_ref[...] = jnp.zeros_like(dw_ref)
            db_ref[...] = jnp.zeros_like(db_ref)

        dy0 = jnp.zeros_like(y0)
        for k in range(K):
            s = K - 1 - k
            dw_ref[k:k + 1, :] += jnp.sum(d * _shd(y0, s, rows), axis=0, keepdims=True)
            dy0 = dy0 + w_ref[k:k + 1, :] * _shu(d, s, rows)
        db_ref[...] += jnp.sum(d, axis=0, keepdims=True)
        da_ref[...] = (dy0 * sg).astype(BF)
        dgl_ref[...] = (dy0 * a * sg * (1.0 - sg)).astype(BF)

    blk = pl.BlockSpec((S, sl), lambda j, bi: (bi, j))
    return _call(
        body, (proj, proj, w, dcv), grid=(ns, Bn),
        in_specs=[blk, pl.BlockSpec((S, sl), lambda j, bi: (bi, ns + j)), pl.BlockSpec((K, sl), lambda j, bi: (0, j)), blk],
        out_specs=[blk, blk, pl.BlockSpec((K, sl), lambda j, bi: (0, j)), pl.BlockSpec((1, sl), lambda j, bi: (0, j))],
        out_shape=[SDS((Bn * S, C), BF), SDS((Bn * S, C), BF), SDS((K, C), F32), SDS((1, C), F32)],
        semantics=("parallel", "arbitrary"), name=name, side=side)


def _pool_z(u, g, rows):
    s2 = u + _shd(u, 1, rows)
    s4 = s2 + _shd(s2, 2, rows)
    s8 = s4 + _shd(s4, 4, rows)
    s16 = s8 + _shd(s8, 8, rows)
    sw = jnp.where(g == 0, s2, jnp.where(g == 1, s4, jnp.where(g == 2, s8, s16)))
    cnt = jnp.minimum(rows + 1, POOL_WINDOWS[0] << g).astype(F32)
    return sw / cnt - u, cnt


def _pool_fwd(proj, wpt, l, Bn, S, C, D, name):
    G = len(POOL_WINDOWS)
    gd, go = C // G, D // G

    def body(u_ref, w_ref, o_ref):
        g = pl.program_id(1)
        u = u_ref[...]
        rows = lax.broadcasted_iota(jnp.int32, u.shape, 0)
        zp, _ = _pool_z(u, g, rows)
        o_ref[...] = lax.dot_general(zp.astype(BF), w_ref[...], _DN["nt"], preferred_element_type=F32).astype(BF)

    return pl.pallas_call(
        body, grid=(Bn, G),
        in_specs=[pl.BlockSpec((S, gd), lambda bi, g: (bi, 2 * G + g)), pl.BlockSpec((None, go, gd), lambda bi, g: (l * G + g, 0, 0))],
        out_specs=pl.BlockSpec((S, go), lambda bi, g: (bi, g)), out_shape=SDS((Bn * S, D), BF),
        compiler_params=_params("parallel", "parallel"), name=name)(proj, wpt)


def _pool_bwd(proj, wpt, dyp, l, Bn, S, C, D, name):
    G = len(POOL_WINDOWS)
    gd, go = C // G, D // G

    def body(u_ref, w_ref, d_ref, du_ref, dw_ref):
        g = pl.program_id(0)
        u = u_ref[...]
        rows = lax.broadcasted_iota(jnp.int32, u.shape, 0)
        zp, cnt = _pool_z(u, g, rows)
        d = d_ref[...]
        dzp = lax.dot_general(d, w_ref[...], _DN["nn"], preferred_element_type=F32)

        @pl.when(pl.program_id(1) == 0)
        def _():
            dw_ref[...] = jnp.zeros_like(dw_ref)

        dw_ref[...] += lax.dot_general(d, zp.astype(BF), _DN["tn"], preferred_element_type=F32)
        dsw = dzp / cnt
        zero = jnp.zeros_like(dsw)
        d16 = jnp.where(g == 3, dsw, zero)
        d8 = jnp.where(g == 2, dsw, zero) + d16 + _shu(d16, 8, rows)
        d4 = jnp.where(g == 1, dsw, zero) + d8 + _shu(d8, 4, rows)
        d2 = jnp.where(g == 0, dsw, zero) + d4 + _shu(d4, 2, rows)
        d1 = d2 + _shu(d2, 1, rows)
        du_ref[...] = (d1 - dzp).astype(BF)

    return pl.pallas_call(
        body, grid=(G, Bn),
        in_specs=[pl.BlockSpec((S, gd), lambda g, bi: (bi, 2 * G + g)), pl.BlockSpec((None, go, gd), lambda g, bi: (l * G + g, 0, 0)),
                  pl.BlockSpec((S, go), lambda g, bi: (bi, g))],
        out_specs=[pl.BlockSpec((S, gd), lambda g, bi: (bi, g)), pl.BlockSpec((None, go, gd), lambda g, bi: (g, 0, 0))],
        out_shape=[SDS((Bn * S, C), BF), SDS((G, go, gd), F32)],
        compiler_params=_params("parallel", "arbitrary"), name=name)(proj, wpt, dyp)


def _ffn_conv(u, w_ref, rows):
    K = w_ref.shape[0]
    acc = w_ref[K - 1:K, :] * u
    for k in range(K - 1):
        acc = acc + w_ref[k:k + 1, :] * _shd(u, K - 1 - k, rows)
    return acc


def _ffn_cb(F):
    return _tile(F, 256)


def _ffn_act_fwd(up0, w, Bn, S, F, name, side=None):
    cb = _ffn_cb(F)
    nj = F // cb

    def body(g_ref, v_ref, wg_ref, wv_ref, o_ref):
        rows = lax.broadcasted_iota(jnp.int32, g_ref.shape, 0)
        o_ref[...] = _gated(_ffn_conv(g_ref[...], wg_ref, rows), _ffn_conv(v_ref[...], wv_ref, rows)).astype(BF)

    K = w.shape[0]
    return _call1(
        body, (up0, up0, w, w), grid=(Bn, nj),
        in_specs=[pl.BlockSpec((S, cb), lambda bi, j: (bi, j)), pl.BlockSpec((S, cb), lambda bi, j: (bi, nj + j)),
                  pl.BlockSpec((K, cb), lambda bi, j: (0, j)), pl.BlockSpec((K, cb), lambda bi, j: (0, nj + j))],
        out_spec=pl.BlockSpec((S, cb), lambda bi, j: (bi, j)), out_shape=SDS((Bn * S, F), BF),
        semantics=("parallel", "parallel"), name=name, side=side)


SUBLANES = 8
FFN_HALO = SUBLANES
FFN_ROWS = 64
GELU_C0, GELU_C1 = 0.7978845608028654, 0.044715


def _gelu_and_grad(x):
    x2 = x * x
    t = jnp.tanh(GELU_C0 * (x + GELU_C1 * (x2 * x)))
    cdf = 0.5 * (1.0 + t)
    return x * cdf, cdf + (0.5 * GELU_C0) * x * (1.0 - t * t) * (1.0 + (3.0 * GELU_C1) * x2)


def _ffn_act_bwd(up0, w, dg, Bn, S, F, name, side=None):
    cb = min(LANES, F)
    nj = F // cb
    K = w.shape[0]
    rc = FFN_ROWS if S % FFN_ROWS == 0 else S
    win = rc + 2 * FFN_HALO
    assert K - 1 <= FFN_HALO and rc % SUBLANES == 0

    def body(g_ref, v_ref, wg_ref, wv_ref, d_ref, do_ref, dwg_ref, dwv_ref, gp, vp, dp):
        for pad, src in ((gp, g_ref), (vp, v_ref), (dp, d_ref)):
            pad[0:FFN_HALO, :] = jnp.zeros((FFN_HALO, cb), F32)
            pad[FFN_HALO + S:, :] = jnp.zeros((FFN_HALO, cb), F32)
            pad[FFN_HALO:FFN_HALO + S, :] = src[...].astype(F32)
        wg = [wg_ref[k:k + 1, :] for k in range(K)]
        wv = [wv_ref[k:k + 1, :] for k in range(K)]

        def taps(u):
            return [pltpu.roll(u, K - 1 - k, 0) for k in range(K - 1)] + [u]

        def conv(us, ws):
            acc = ws[K - 1] * us[K - 1]
            for k in range(K - 1):
                acc = acc + ws[k] * us[k]
            return acc

        def conv_t(dc, ws):
            acc = ws[K - 1] * dc
            for k in range(K - 1):
                acc = acc + ws[k] * pltpu.roll(dc, win - (K - 1 - k), 0)
            return acc

        def fold(t):
            acc = t[FFN_HALO:FFN_HALO + SUBLANES]
            for i in range(1, rc // SUBLANES):
                acc = acc + t[FFN_HALO + SUBLANES * i:FFN_HALO + SUBLANES * (i + 1)]
            return acc

        def chunk(c, sums):
            r0 = pl.multiple_of(c * rc, SUBLANES)
            gs, vs, d = taps(gp[pl.ds(r0, win), :]), taps(vp[pl.ds(r0, win), :]), dp[pl.ds(r0, win), :]
            ge, dge = _gelu_and_grad(conv(gs, wg))
            dgc = d * conv(vs, wv) * dge
            dvc = d * ge
            do_ref[0, pl.ds(r0, rc), :] = conv_t(dgc, wg)[FFN_HALO:FFN_HALO + rc].astype(BF)
            do_ref[1, pl.ds(r0, rc), :] = conv_t(dvc, wv)[FFN_HALO:FFN_HALO + rc].astype(BF)
            new = [fold(dc * u) for us, dc in ((gs, dgc), (vs, dvc)) for u in us]
            return tuple(a + b for a, b in zip(sums, new))

        sums = lax.fori_loop(0, S // rc, chunk, tuple(jnp.zeros((SUBLANES, cb), F32) for _ in range(2 * K)))

        @pl.when(pl.program_id(1) == 0)
        def _():
            dwg_ref[...] = jnp.zeros_like(dwg_ref)
            dwv_ref[...] = jnp.zeros_like(dwv_ref)

        for k in range(K):
            dwg_ref[k:k + 1, :] += jnp.sum(sums[k], axis=0, keepdims=True)
            dwv_ref[k:k + 1, :] += jnp.sum(sums[K + k], axis=0, keepdims=True)

    blk = pl.BlockSpec((S, cb), lambda j, bi: (bi, j))
    wblk = pl.BlockSpec((K, cb), lambda j, bi: (0, j))
    return _call(
        body, (up0, up0, w, w, dg), grid=(nj, Bn),
        in_specs=[blk, pl.BlockSpec((S, cb), lambda j, bi: (bi, nj + j)), wblk, pl.BlockSpec((K, cb), lambda j, bi: (0, nj + j)), blk],
        out_specs=[pl.BlockSpec((2, S, cb), lambda j, bi: (0, bi, j)), wblk, wblk],
        out_shape=[SDS((2, Bn * S, F), BF), SDS((K, F), F32), SDS((K, F), F32)],
        scratch_shapes=[pltpu.VMEM((S + 2 * FFN_HALO, cb), F32)] * 3, semantics=("parallel", "arbitrary"), name=name, side=side)


def _softmax_rows(q, k, scale):
    sc = lax.dot_general(q, k, _DN["nt"], preferred_element_type=F32) * scale
    e = jnp.exp(sc - jnp.max(sc, axis=-1, keepdims=True))
    return e / jnp.sum(e, axis=-1, keepdims=True)


def _attn_ts(S):
    return _tile(S, 1024, 8)


def _attn_fwd(q, kv, Bn, S, Mn, D, name, side=None):
    H = XA_HEADS
    dh = D // H
    ts = _attn_ts(S)
    nsb = S // ts
    scale = dh ** -0.5

    def body(q_ref, k_ref, v_ref, o_ref):
        p = _softmax_rows(q_ref[...], k_ref[...], scale)
        o_ref[...] = lax.dot_general(p.astype(BF), v_ref[...], _DN["nn"], preferred_element_type=F32).astype(BF)

    qblk = pl.BlockSpec((ts, dh), lambda bi, h, s: (bi * nsb + s, h))
    return _call1(
        body, (q, kv, kv), grid=(Bn, H, nsb),
        in_specs=[qblk, pl.BlockSpec((Mn, dh), lambda bi, h, s: (bi, h)), pl.BlockSpec((Mn, dh), lambda bi, h, s: (bi, H + h))],
        out_spec=qblk, out_shape=SDS((Bn * S, D), BF), semantics=("parallel", "parallel", "parallel"), name=name, side=side)


def _attn_bwd(q, kv, datt, Bn, S, Mn, D, name):
    H = XA_HEADS
    dh = D // H
    ts = _attn_ts(S)
    nsb = S // ts
    scale = dh ** -0.5

    def body(q_ref, k_ref, v_ref, do_ref, dq_ref, dk_ref, dv_ref):
        q, k, v, do = q_ref[...], k_ref[...], v_ref[...], do_ref[...]
        p = _softmax_rows(q, k, scale)
        dp = lax.dot_general(do, v, _DN["nt"], preferred_element_type=F32)
        ds = (p * (dp - jnp.sum(dp * p, axis=-1, keepdims=True)) * scale).astype(BF)
        dq_ref[...] = lax.dot_general(ds, k, _DN["nn"], preferred_element_type=F32).astype(BF)

        @pl.when(pl.program_id(2) == 0)
        def _():
            dk_ref[...] = jnp.zeros_like(dk_ref)
            dv_ref[...] = jnp.zeros_like(dv_ref)

        dk_ref[...] += lax.dot_general(ds, q, _DN["tn"], preferred_element_type=F32)
        dv_ref[...] += lax.dot_general(p.astype(BF), do, _DN["tn"], preferred_element_type=F32)

    qblk = pl.BlockSpec((ts, dh), lambda bi, h, s: (bi * nsb + s, h))
    kblk = pl.BlockSpec((Mn, dh), lambda bi, h, s: (bi, h))
    return pl.pallas_call(
        body, grid=(Bn, H, nsb),
        in_specs=[qblk, kblk, pl.BlockSpec((Mn, dh), lambda bi, h, s: (bi, H + h)), qblk],
        out_specs=[qblk, kblk, kblk], out_shape=[SDS((Bn * S, D), BF), SDS((Bn * Mn, D), F32), SDS((Bn * Mn, D), F32)],
        compiler_params=_params("parallel", "parallel", "arbitrary"), name=name)(q, kv, kv, datt)


class _Sides:
    def __init__(self, by_key=None, on_land=None):
        self.by_key, self.landed, self.on_land = dict(by_key or {}), {}, on_land

    def run(self, key, fn, *args, **kw):
        side = self.by_key.get(key)
        if side is None:
            return fn(*args, **kw)
        out, self.landed[key] = fn(*args, side=side() if callable(side) else side, **kw)
        if self.on_land is not None:
            self.on_land(key, self.landed[key])
        return out

    def mm(self, key, *args, **kw):
        return self.run(key, _mm, *args, **kw)


def _layer_fwd(x, h, mem_n, W, V, l, dims, sides, next_g):
    Bn, S, Mn, D, C, F = dims
    n = f"l{l}_"
    proj = sides.mm("proj", h, W["w_in"], "nn", F32, n + "proj", bl=0)
    cv = sides.run("glu_conv", _glu_conv_fwd, proj, V["conv_dw_w"][l], V["conv_dw_b"][l], Bn, S, C, n + "glu_conv")
    yc1 = _ln_silu_fwd(cv, V["conv_ln_g"][l], V["conv_ln_b"][l], n + "ln_silu")
    yc = sides.mm("conv_out", yc1, W["w_conv_out"], "nn", BF, n + "conv_out", bl=0)
    yp = _pool_fwd(proj, W["w_pool"], 0, Bn, S, C, D, n + "pool")
    merged = sides.run("merge", _merge_fwd, proj, yc, yp, V["pool_scale"][l], C, n + "merge")
    x1, hq = sides.run("out_proj", _mm_rms_fwd, merged, W["w_out"], x, V["xattn_norm_g"][l], n + "out_proj")
    q = sides.mm("q_proj", hq, W["w_q"], "nn", BF, n + "q_proj", bl=0)
    kv = _mm(mem_n, W["w_kv"], "nn", BF, n + "kv_proj", bl=0)
    att = sides.run("attn", _attn_fwd, q, kv, Bn, S, Mn, D, n + "attn")
    x2, hf = sides.run("o_proj", _mm_rms_fwd, att, W["w_o"], x1, V["ffn_norm_g"][l], n + "o_proj")
    up0 = sides.mm("up_proj", hf, W["w_up"], "nn", F32, n + "up_proj", bl=0)
    gact = sides.run("ffn_act", _ffn_act_fwd, up0, V["ffn_dw_w"][l], Bn, S, F, n + "ffn_act")
    if next_g is not None:
        x3, h3 = sides.run("down_proj", _mm_rms_fwd, gact, W["w_down"], x2, next_g, n + "down_proj")
    else:
        x3, h3 = sides.mm("down_proj", gact, W["w_down"], "nn", F32, n + "down_proj", res=x2, bl=0), None
    return x3, h3, dict(x=x, h=h, proj=proj, cv=cv, yc1=yc1, yc=yc, yp=yp, merged=merged, x1=x1, hq=hq, q=q, kv=kv, att=att, x2=x2,
                        hf=hf, up0=up0, gact=gact)


def _layer_bwd_mlp(dx, dxb, sv, W, V, l, dims, sides):
    Bn, S, Mn, D, C, F = dims
    n = f"l{l}_b_"
    gw, sm = {}, {}
    dgact = sides.mm("d_gact", dxb, W["w_down"], "nt", BF, n + "d_gact", bl=0)
    gw["w_down"] = sides.mm("dw_down", sv["gact"], dxb, "tn", F32, n + "dw_down", twin=BF)
    dup0, dwg, dwv = sides.run("ffn_act_b", _ffn_act_bwd, sv["up0"], V["ffn_dw_w"][l], dgact, Bn, S, F, n + "ffn_act")
    sm["ffn_dw_w"] = jnp.concatenate([dwg, dwv], axis=1)
    dx2, dx2b, sm["ffn_norm_g"] = _mm_rms_bwd(dup0, W["w_up"], sv["x2"], V["ffn_norm_g"][l], dx, n + "d_hf", a_halves=True)
    gw["w_up"] = sides.mm("dw_up", sv["hf"], dup0, "tn", F32, n + "dw_up", twin=BF, b_halves=True)
    return dx2, dx2b, gw, sm


def _layer_bwd_mix(dx2, dx2b, dmem_n, sv, mem_n, W, V, l, dims, sides, gw):
    Bn, S, Mn, D, C, F = dims
    n = f"l{l}_b_"
    sm = {}
    datt = sides.mm("d_att", dx2b, W["w_o"], "nt", BF, n + "d_att", bl=0)
    gw["w_o"] = _mm(sv["att"], dx2b, "tn", F32, n + "dw_o", twin=BF)
    dq, dk, dv = _attn_bwd(sv["q"], sv["kv"], datt, Bn, S, Mn, D, n + "attn")
    dkv = jnp.concatenate([dk, dv], axis=1)
    gw["w_kv"] = _mm(mem_n, dkv, "tn", F32, n + "dw_kv", twin=BF)
    dmem_n = _mm(dkv, W["w_kv"], "nt", F32, n + "d_mem", res=dmem_n, bl=0)
    dx1, dx1b, sm["xattn_norm_g"] = _mm_rms_bwd(dq, W["w_q"], sv["x1"], V["xattn_norm_g"][l], dx2, n + "d_hq")
    gw["w_q"] = _mm(sv["hq"], dq, "tn", F32, n + "dw_q", twin=BF)
    dmerged = sides.mm("d_merged", dx1b, W["w_out"], "nt", BF, n + "d_merged", bl=0)
    gw["w_out"] = _mm(sv["merged"], dx1b, "tn", F32, n + "dw_out", twin=BF)
    dgc, dgp, dyc, dyp, sm["pool_scale"] = sides.run("merge_b", _merge_bwd, sv["proj"], sv["yc"], sv["yp"], V["pool_scale"][l], dmerged, C, n + "merge")
    du, dwp = _pool_bwd(sv["proj"], W["w_pool"], dyp, 0, Bn, S, C, D, n + "pool")
    gw["w_pool"] = (dwp, dwp.astype(BF))
    dyc1 = _mm(dyc, W["w_conv_out"], "nt", F32, n + "d_yc1", bl=0)
    gw["w_conv_out"] = _mm(sv["yc1"], dyc, "tn", F32, n + "dw_conv_out", twin=BF)
    dcv, sm["conv_ln_g"], sm["conv_ln_b"] = sides.run("ln_silu_b", _ln_silu_bwd, sv["cv"], V["conv_ln_g"][l], V["conv_ln_b"][l], dyc1, n + "ln_silu")
    da, dgl, sm["conv_dw_w"], sm["conv_dw_b"] = sides.run("glu_conv_b", _glu_conv_bwd, sv["proj"], V["conv_dw_w"][l], dcv, Bn, S, C, n + "glu_conv")
    dproj = jnp.concatenate([da, dgl, du, dgc, dgp], axis=1)
    dx, dxb, sm["mix_norm_g"] = sides.run("d_h", _mm_rms_bwd, dproj, W["w_in"], sv["x"], V["mix_norm_g"][l], dx1, n + "d_h")
    gw["w_in"] = sides.mm("dw_in", sv["h"], dproj, "tn", F32, n + "dw_in", twin=BF)
    return dx, dxb, dmem_n, sm


BIG = (("w_in", "col"), ("w_conv_out", "col"), ("w_pool", "row"), ("w_out", "row"), ("w_q", "row"), ("w_kv", "col"),
       ("w_o", "row"), ("w_up", "col"), ("w_down", "row"))
ALL_RELS = (1, 2, 3)
GATHER_FIRST = ("w_in", "w_conv_out", "w_pool", "w_out", "w_q", "w_o")
FWD_CARRY = {
    (0, "proj"): (("w_up", 0, (1, 2)),),
    (0, "glu_conv"): (("w_up", 0, (3,)),),
    (0, "merge"): (("w_kv", 0, ALL_RELS),),
    (0, "attn"): (("w_down", 0, (1, 2)),),
    (0, "o_proj"): (("w_down", 0, (3,)),),
    (0, "up_proj"): (("w_in", 1, ALL_RELS), ("w_conv_out", 1, ALL_RELS), ("w_pool", 1, ALL_RELS)),
    (0, "ffn_act"): (("w_out", 1, ALL_RELS), ("w_q", 1, ALL_RELS), ("w_kv", 1, ALL_RELS)),
    (0, "down_proj"): (("w_o", 1, ALL_RELS),),
    (1, "proj"): (("w_up", 1, (1, 2)),),
    (1, "glu_conv"): (("w_up", 1, (3,)),),
    (1, "merge"): (("w_down", 1, (1, 2)),),
    (1, "attn"): (("w_down", 1, (3,)),),
}
PASS_GROUPS = ((("w_kv", 0),), (("w_up", 0), ("w_down", 0)),
               (("w_in", 1), ("w_conv_out", 1), ("w_pool", 1), ("w_out", 1), ("w_q", 1), ("w_kv", 1), ("w_o", 1)),
               (("w_up", 1), ("w_down", 1)))
EARLY = ("w_down", "w_up")
BWD_CARRY_EARLY = {"merge_b": ("w_down",), "glu_conv_b": ("w_up",)}
BWD_CARRY_LATE = {"ffn_act_b": ("w_in", "w_conv_out", "w_pool", "w_out", "w_q", "w_kv", "w_o")}
BWD_LAST_LAYER = (("att", ("w_o", "w_kv", "w_q"), "d_merged", {"d_h": ("w_o", "w_kv", "w_q")}),
                  ("tok", ("w_out", "w_pool", "w_conv_out"), "ln_silu_b", {"dw_in": ("w_out", "w_pool", "w_conv_out")}))


def _place():
    xi, yi, ci = lax.axis_index("x"), lax.axis_index("y"), lax.axis_index("c")
    return xi, yi, ci, 2 * xi + yi


def _chip_peer(xi, yi, ci, r):
    return (xi ^ (r >> 1), yi ^ (r & 1), ci)


def _full_shard(ref, kind, k, cs):
    if kind == "col":
        return ref.at[:, :, :, :, pl.ds(pl.multiple_of(k * cs, cs), cs)]
    return ref.at[:, :, k]


def _gather_weights(shards, kinds):
    n = len(shards)
    outs = []
    for s, kind in zip(shards, kinds):
        L, P, _, RH, CS = s.shape
        outs.append(SDS((L, P, 2, RH, CS * N_CHIPS) if kind == "col" else (L, P, N_CHIPS, 2, RH, CS), s.dtype))
    per = 7

    def body(*refs):
        srcs, fulls, (ssem, rsem) = refs[:n], refs[n:2 * n], refs[2 * n:]
        xi, yi, ci, j = _place()
        sib = (xi, yi, 1 - ci)

        def piece(i, k, c):
            kind, cs = kinds[i], shards[i].shape[-1]
            if kind == "col":
                return fulls[i].at[:, :, c, :, pl.ds(pl.multiple_of(k * cs, cs), cs)]
            return fulls[i].at[:, :, k, c]

        def copy(i, slot, src, dst, dev):
            return pltpu.make_async_remote_copy(src_ref=src, dst_ref=dst, send_sem=ssem.at[per * i + slot], recv_sem=rsem.at[per * i + slot],
                                                device_id=dev, device_id_type=MESH)

        own, first, passed = [], [], []
        for i in range(n):
            for r in (1, 2, 3):
                first.append(copy(i, r - 1, srcs[i].at[:, :, ci], piece(i, j, ci), _chip_peer(xi, yi, ci, r)))
                first[-1].start()
        for i in range(n):
            own.append(copy(i, 6, srcs[i], _full_shard(fulls[i], kinds[i], j, shards[i].shape[-1]), sib))
            own[-1].start()
        for i in range(n):
            for r in (1, 2, 3):
                got = piece(i, j ^ r, ci)
                copy(i, r - 1, got, got, sib).wait_recv()
                passed.append(copy(i, 2 + r, got, got, sib))
                passed[-1].start()
        for i in range(n):
            for r in (1, 2, 3):
                got = piece(i, j ^ r, 1 - ci)
                copy(i, 2 + r, got, got, sib).wait_recv()
        for cp in own:
            cp.wait()
        for cp in first + passed:
            cp.wait_send()

    return pl.pallas_call(
        body, in_specs=[ANY] * n, out_specs=[ANY] * n, out_shape=outs,
        scratch_shapes=[pltpu.SemaphoreType.DMA((per * n,)), pltpu.SemaphoreType.DMA((per * n,))], name="gather_weights")(*shards)


def _full_sds(s, kind):
    L, P, _, RH, CS = s.shape
    return SDS((L, P, 2, RH, CS * N_CHIPS) if kind == "col" else (L, P, N_CHIPS, 2, RH, CS), s.dtype)


def _gather_piece(full, kind, cs, k, c):
    if kind == "col":
        return full.at[:, :, c, :, pl.ds(pl.multiple_of(k * cs, cs), cs)]
    return full.at[:, :, k, c]


def _side_gather(shards, kinds, rels, fulls):
    n = len(shards)

    def make(srcs, outs, ssem, rsem):
        xi, yi, ci, j = _place()
        return [pltpu.make_async_remote_copy(
            src_ref=srcs[i].at[:, :, ci], dst_ref=_gather_piece(outs[i], kinds[i], shards[i].shape[-1], j, ci), send_sem=ssem.at[3 * i + r - 1],
            recv_sem=rsem.at[3 * i + r - 1], device_id=_chip_peer(xi, yi, ci, r), device_id_type=MESH) for i in range(n) for r in rels[i]]

    prior = [f for f in fulls if f is not None]
    assert len(prior) in (0, n)
    return _Side(list(shards) + prior, [_full_sds(s, k) for s, k in zip(shards, kinds)], 3 * n, make, n_alias=len(prior))


def _gather_pass(fulls, shards, kinds, name):
    n = len(fulls)

    def body(*refs):
        srcs, outs, (ssem, rsem) = refs[n:2 * n], refs[2 * n:3 * n], refs[3 * n:]
        xi, yi, ci, j = _place()
        sib = (xi, yi, 1 - ci)
        cps = []
        for i in range(n):
            cs = shards[i].shape[-1]
            for r in (1, 2, 3):
                got = _gather_piece(outs[i], kinds[i], cs, j ^ r, ci)
                cps.append(pltpu.make_async_remote_copy(src_ref=got, dst_ref=got, send_sem=ssem.at[4 * i + r - 1], recv_sem=rsem.at[4 * i + r - 1],
                                                        device_id=sib, device_id_type=MESH))
            cps.append(pltpu.make_async_remote_copy(src_ref=srcs[i], dst_ref=_full_shard(outs[i], kinds[i], j, cs), send_sem=ssem.at[4 * i + 3],
                                                    recv_sem=rsem.at[4 * i + 3], device_id=sib, device_id_type=MESH))
        for cp in cps:
            cp.start()
        for cp in cps:
            cp.wait()

    return pl.pallas_call(
        body, in_specs=[ANY] * (2 * n), out_specs=[ANY] * n, out_shape=[SDS(f.shape, f.dtype) for f in fulls],
        input_output_aliases={i: i for i in range(n)},
        scratch_shapes=[pltpu.SemaphoreType.DMA((4 * n,)), pltpu.SemaphoreType.DMA((4 * n,))], name=name)(*fulls, *shards)


def _sibling_exchange(gviews, kinds, name):
    n = len(gviews)
    outs = [SDS(g.shape[:1] + g.shape[2:] if kind == "col" else g.shape[:2] + g.shape[3:], g.dtype) for g, kind in zip(gviews, kinds)]

    def body(*refs):
        gs, lands, (ssem, rsem) = refs[:n], refs[n:2 * n], refs[2 * n:]
        xi, yi, ci, _ = _place()
        cps = []
        for i in range(n):
            src = gs[i].at[:, 1 - ci] if kinds[i] == "col" else gs[i].at[:, :, 1 - ci]
            cps.append(pltpu.make_async_remote_copy(src_ref=src, dst_ref=lands[i], send_sem=ssem.at[i], recv_sem=rsem.at[i],
                                                    device_id=(xi, yi, 1 - ci), device_id_type=MESH))
            cps[-1].start()
        for cp in cps:
            cp.wait()

    return pl.pallas_call(body, in_specs=[ANY] * n, out_specs=[ANY] * n, out_shape=outs,
                          scratch_shapes=[pltpu.SemaphoreType.DMA((n,)), pltpu.SemaphoreType.DMA((n,))], name=name)(*gviews)


def _side_sibling_exchange(gviews, kinds):
    outs = [SDS(g.shape[:1] + g.shape[2:] if kind == "col" else g.shape[:2] + g.shape[3:], g.dtype) for g, kind in zip(gviews, kinds)]

    def make(gs, lands, ssem, rsem):
        xi, yi, ci, _ = _place()
        return [pltpu.make_async_remote_copy(src_ref=gs[i].at[:, 1 - ci] if kinds[i] == "col" else gs[i].at[:, :, 1 - ci], dst_ref=lands[i],
                                             send_sem=ssem.at[i], recv_sem=rsem.at[i], device_id=(xi, yi, 1 - ci), device_id_type=MESH)
                for i in range(len(gs))]

    return _Side(gviews, outs, len(gviews), make)


def _chip_sums(gs, lands, kinds, jc, name):
    n = len(gs)
    args, in_specs, out_specs, out_shape = [], [], [], []
    for g, land, kind in zip(gs, lands, kinds):
        if kind == "col":
            P, _, RH, C = g.shape
            CS = C // N_CHIPS
            in_specs += [pl.BlockSpec((P, None, RH, CS), lambda r, jc: (0, jc[1], 0, jc[0] ^ r)),
                         pl.BlockSpec((P, RH, CS), lambda r, jc: (0, 0, jc[0] ^ r))]
        else:
            P, _, _, RH, CS = g.shape
            in_specs += [pl.BlockSpec((P, None, None, RH, CS), lambda r, jc: (0, jc[0] ^ r, jc[1], 0, 0)),
                         pl.BlockSpec((P, None, RH, CS), lambda r, jc: (0, jc[0] ^ r, 0, 0))]
        args += [g, land]
        out_specs += [pl.BlockSpec((P, RH, CS), lambda r, jc: (0, 0, 0)), pl.BlockSpec((None, P, RH, CS), lambda r, jc: (r, 0, 0, 0))]
        out_shape += [SDS((P, RH, CS), F32), SDS((N_CHIPS, P, RH, CS), BF)]

    def body(jc_ref, *refs):
        ins, outs = refs[:2 * n], refs[2 * n:]
        for i in range(n):
            s = ins[2 * i][...] + ins[2 * i + 1][...].astype(F32)
            outs[2 * i + 1][...] = s.astype(BF)

            @pl.when(pl.program_id(0) == 0)
            def _():
                outs[2 * i][...] = s

    outs = _call(body, args, grid=(N_CHIPS,), in_specs=in_specs, out_specs=out_specs, out_shape=out_shape, semantics=("arbitrary",),
                 name=name, prefetch=(jc,))
    return outs[0::2], outs[1::2]


def _chip_exchange_copies(srcs, lands, ssem, rsem):
    xi, yi, ci, _ = _place()
    return [pltpu.make_async_remote_copy(src_ref=srcs[i].at[r], dst_ref=lands[i].at[r], send_sem=ssem.at[3 * i + r - 1],
                                         recv_sem=rsem.at[3 * i + r - 1], device_id=_chip_peer(xi, yi, ci, r), device_id_type=MESH)
            for i in range(len(srcs)) for r in (1, 2, 3)]


def _side_chip_exchange(pieces):
    return _Side(pieces, [SDS(p.shape, p.dtype) for p in pieces], 3 * len(pieces), _chip_exchange_copies)


FINAL_SUM_STEPS = 2


def _final_sums(owns, lands, jc, shards, l, L, name, side=None):
    n = len(owns)
    args, in_specs, out_specs, out_shape = [], [], [], []
    for own, land in zip(owns, lands):
        P, RH, CS = own.shape
        hr = RH // FINAL_SUM_STEPS
        in_specs += [pl.BlockSpec((P, hr, CS), lambda h, jc: (0, h, 0))]
        in_specs += [pl.BlockSpec((None, P, hr, CS), functools.partial(lambda r, h, jc: (r, 0, h, 0), r)) for r in (1, 2, 3)]
        args += [own, land, land, land]
        out_specs.append(pl.BlockSpec((None, P, None, hr, CS), lambda h, jc: (l, 0, jc[1], h, 0)))
        out_shape.append(SDS((L, P, 2, RH, CS), F32))
    aliases = None
    if shards is not None:
        aliases = {4 * n + i: i for i in range(n)}
        in_specs += [ANY] * n
        args += list(shards)

    def body(jc_ref, *refs):
        outs = refs[len(args):]
        for i in range(n):
            o, a, b, c = (refs[4 * i + t][...] for t in range(4))
            outs[i][...] = ((o + a.astype(F32)) + b.astype(F32)) + c.astype(F32)

    return _call(body, args, grid=(FINAL_SUM_STEPS,), in_specs=in_specs, out_specs=out_specs, out_shape=out_shape, semantics=("arbitrary",),
                 name=name, prefetch=(jc,), aliases=aliases, side=side)


def _halves_exchange(shards, l, name):
    n = len(shards)

    def body(*refs):
        outs, (ssem, rsem) = refs[n:2 * n], refs[2 * n:]
        xi, yi, ci, _ = _place()
        cps = []
        for i in range(n):
            mine = outs[i].at[l, :, ci]
            cps.append(pltpu.make_async_remote_copy(src_ref=mine, dst_ref=mine, send_sem=ssem.at[i], recv_sem=rsem.at[i],
                                                    device_id=(xi, yi, 1 - ci), device_id_type=MESH))
            cps[-1].start()
        for i in range(n):
            land = outs[i].at[l, :, 1 - ci]
            pltpu.make_async_remote_copy(src_ref=land, dst_ref=land, send_sem=ssem.at[i], recv_sem=rsem.at[i],
                                         device_id=(xi, yi, 1 - ci), device_id_type=MESH).wait_recv()
        for cp in cps:
            cp.wait_send()

    return pl.pallas_call(body, in_specs=[ANY] * n, out_specs=[ANY] * n, out_shape=[SDS(s.shape, s.dtype) for s in shards],
                          input_output_aliases={i: i for i in range(n)},
                          scratch_shapes=[pltpu.SemaphoreType.DMA((n,)), pltpu.SemaphoreType.DMA((n,))], name=name)(*shards)


def _reduce_small(part, pieces):
    NR, Wd = part.shape
    ND = 2 * N_CHIPS
    n = len(pieces)

    def body(p_ref, *refs):
        srcs, o_ref, lands, (land, ssem, rsem, xs, xr) = refs[:n], refs[n], refs[n + 1:2 * n + 1], refs[2 * n + 1:]
        exchange = _chip_exchange_copies(srcs, lands, xs, xr)
        for cp in exchange:
            cp.start()
        xi, yi, ci, j = _place()
        me = 2 * j + ci
        land[me] = p_ref[...]
        cps = []
        for rr in range(1, ND):
            dev = (xi ^ (rr >> 2), yi ^ ((rr >> 1) & 1), ci ^ (rr & 1))
            cps.append(pltpu.make_async_remote_copy(src_ref=p_ref, dst_ref=land.at[me], send_sem=ssem.at[rr - 1], recv_sem=rsem.at[rr - 1],
                                                    device_id=dev, device_id_type=MESH))
            cps[-1].start()
        for rr in range(1, ND):
            got = land.at[me ^ rr]
            pltpu.make_async_remote_copy(src_ref=got, dst_ref=got, send_sem=ssem.at[rr - 1], recv_sem=rsem.at[rr - 1],
                                         device_id=(xi, yi, ci), device_id_type=MESH).wait_recv()
        acc = land[0]
        for d in range(1, ND):
            acc = acc + land[d]
        o_ref[...] = acc
        for cp in cps:
            cp.wait_send()
        for cp in exchange:
            cp.wait()

    vm = pl.BlockSpec(memory_space=pltpu.VMEM)
    outs = pl.pallas_call(
        body, in_specs=[vm] + [ANY] * n, out_specs=[vm] + [ANY] * n, out_shape=[SDS((NR, Wd), F32)] + [SDS(p.shape, p.dtype) for p in pieces],
        scratch_shapes=[pltpu.VMEM((ND, NR, Wd), F32), pltpu.SemaphoreType.DMA((ND - 1,)), pltpu.SemaphoreType.DMA((ND - 1,)),
                        pltpu.SemaphoreType.DMA((3 * n,)), pltpu.SemaphoreType.DMA((3 * n,))],
        name="small_grad_allreduce")(part, *pieces)
    return outs[0], list(outs[1:])


def _adamw_update(w_ref, g_ref, m_ref, v_ref, d_ref, mo_ref, vo_ref):
    g = g_ref[...]
    m = ADAM_B1 * m_ref[...] + (1.0 - ADAM_B1) * g
    v = ADAM_B2 * v_ref[...] + (1.0 - ADAM_B2) * jnp.square(g)
    m_hat = m / (1.0 - ADAM_B1 ** ADAM_STEP)
    v_hat = v / (1.0 - ADAM_B2 ** ADAM_STEP)
    d_ref[...] = -ADAM_LR * (m_hat / (jnp.sqrt(v_hat) + ADAM_EPS) + ADAM_WD * w_ref[...])
    mo_ref[...] = m
    vo_ref[...] = v


ADAMW_STEPS = 8


def _adamw_layer(ws, gs, ms, vs, prev, l, name, side=None):
    n = len(ws)
    args, in_specs, out_specs, out_shape = [], [], [], []
    for w, g, m, v in zip(ws, gs, ms, vs):
        L, R, C = w.shape
        blk = pl.BlockSpec((None, R // ADAMW_STEPS, C), lambda i: (l, i, 0))
        in_specs += [blk] * 4
        args += [w, g, m, v]
        out_specs += [blk] * 3
        out_shape += [SDS((L, R, C), F32)] * 3
    aliases = None
    if prev is not None:
        aliases = {4 * n + i: i for i in range(3 * n)}
        in_specs += [ANY] * (3 * n)
        args += list(prev)

    def body(*refs):
        outs = refs[len(args):]
        for i in range(n):
            _adamw_update(*refs[4 * i:4 * i + 4], *outs[3 * i:3 * i + 3])

    return _call(body, args, grid=(ADAMW_STEPS,), in_specs=in_specs, out_specs=out_specs, out_shape=out_shape, semantics=("parallel",),
                 name=name, aliases=aliases, side=side)


def _adamw(w, g, m, v, name):
    shape = w.shape
    C = shape[-1]
    R = w.size // C
    tb = _tile(R, max(8, (1 << 18) // C), 8)
    body = functools.partial(_adamw_update)
    blk = pl.BlockSpec((tb, C), lambda i: (i, 0))
    outs = pl.pallas_call(body, grid=(R // tb,), in_specs=[blk] * 4, out_specs=[blk] * 3, out_shape=[SDS((R, C), F32)] * 3,
                          compiler_params=_params("parallel"), name=name)(*[t.reshape(R, C) for t in (w, g, m, v)])
    return [t.reshape(shape) for t in outs]


WEIGHTS = ("mix_norm_g", "w_in", "conv_dw_w", "conv_dw_b", "conv_ln_g", "conv_ln_b", "w_conv_out", "w_pool_grp", "pool_scale", "w_out",
           "xattn_norm_g", "mem_norm_g", "w_q", "w_kv", "w_o", "ffn_norm_g", "w_up", "ffn_dw_w", "w_down", "final_norm_g")
VECTORS = ("mix_norm_g", "conv_dw_b", "conv_ln_g", "conv_ln_b", "pool_scale", "xattn_norm_g", "mem_norm_g", "ffn_norm_g", "final_norm_g")


def _shard_view(t, kind):
    L, P, R, C = t.shape
    return t.reshape(L, P, 2, R // 2, C)


def _rows(t, width):
    return t.reshape(-1, width)


def _pack(parts):
    return jnp.concatenate([jnp.pad(p, ((0, (-p.shape[0]) % 8), (0, 0))) for p in parts], axis=0)


def kernel(x, mem, mix_norm_g, w_in, conv_dw_w, conv_dw_b, conv_ln_g, conv_ln_b, w_conv_out, w_pool_grp, pool_scale, w_out, xattn_norm_g, mem_norm_g, w_q, w_kv, w_o, ffn_norm_g, w_up, ffn_dw_w, w_down, final_norm_g, loss_target, m_mix_norm_g, m_w_in, m_conv_dw_w, m_conv_dw_b, m_conv_ln_g, m_conv_ln_b, m_w_conv_out, m_w_pool_grp, m_pool_scale, m_w_out, m_xattn_norm_g, m_mem_norm_g, m_w_q, m_w_kv, m_w_o, m_ffn_norm_g, m_w_up, m_ffn_dw_w, m_w_down, m_final_norm_g, v_mix_norm_g, v_w_in, v_conv_dw_w, v_conv_dw_b, v_conv_ln_g, v_conv_ln_b, v_w_conv_out, v_w_pool_grp, v_pool_scale, v_w_out, v_xattn_norm_g, v_mem_norm_g, v_w_q, v_w_kv, v_w_o, v_ffn_norm_g, v_w_up, v_ffn_dw_w, v_w_down, v_final_norm_g):
    w = dict(mix_norm_g=mix_norm_g, w_in=w_in, conv_dw_w=conv_dw_w, conv_dw_b=conv_dw_b, conv_ln_g=conv_ln_g, conv_ln_b=conv_ln_b,
             w_conv_out=w_conv_out, w_pool_grp=w_pool_grp, pool_scale=pool_scale, w_out=w_out, xattn_norm_g=xattn_norm_g,
             mem_norm_g=mem_norm_g, w_q=w_q, w_kv=w_kv, w_o=w_o, ffn_norm_g=ffn_norm_g, w_up=w_up, ffn_dw_w=ffn_dw_w, w_down=w_down,
             final_norm_g=final_norm_g)
    m = dict(zip(WEIGHTS, (m_mix_norm_g, m_w_in, m_conv_dw_w, m_conv_dw_b, m_conv_ln_g, m_conv_ln_b, m_w_conv_out, m_w_pool_grp, m_pool_scale,
                           m_w_out, m_xattn_norm_g, m_mem_norm_g, m_w_q, m_w_kv, m_w_o, m_ffn_norm_g, m_w_up, m_ffn_dw_w, m_w_down, m_final_norm_g)))
    v = dict(zip(WEIGHTS, (v_mix_norm_g, v_w_in, v_conv_dw_w, v_conv_dw_b, v_conv_ln_g, v_conv_ln_b, v_w_conv_out, v_w_pool_grp, v_pool_scale,
                           v_w_out, v_xattn_norm_g, v_mem_norm_g, v_w_q, v_w_kv, v_w_o, v_ffn_norm_g, v_w_up, v_ffn_dw_w, v_w_down, v_final_norm_g)))
    xi, yi, ci, j = _place()
    jc = jnp.stack([j, ci]).astype(jnp.int32)
    L = w_in.shape[0]
    G = len(POOL_WINDOWS)
    kinds = dict(BIG)

    def to_mat(name, t):
        if name == "w_pool":
            return jnp.swapaxes(t, 2, 3)
        return t[:, None]

    def from_mat(name, t):
        if name == "w_pool":
            return jnp.swapaxes(t, 2, 3)
        return t[:, 0]

    src = {name: w["w_pool_grp" if name == "w_pool" else name] for name, _ in BIG}

    KC, cs_c = conv_dw_w.shape[1], conv_dw_w.shape[2]
    KF, cs_f = ffn_dw_w.shape[1], ffn_dw_w.shape[2]
    taps = jnp.concatenate([conv_dw_w.reshape(L * KC, cs_c), ffn_dw_w.reshape(L * KF * (cs_f // cs_c), cs_c)], axis=0)
    n_taps = taps.shape[0]
    taps = jnp.pad(taps, ((0, (-n_taps) % 16), (0, 0)))
    names = [name for name, _ in BIG]
    mats = {name: to_mat(name, src[name]).astype(BF) for name in names}

    def layer_shards(l, subset):
        return [_shard_view(mats[name][l:l + 1], kinds[name]) for name in subset]

    def as_weight(name, f):
        return f.reshape(G if name == "w_pool" else 1, -1, f.shape[-1])

    assert L == 2
    fulls = _gather_weights(layer_shards(0, GATHER_FIRST) + [_shard_view(taps[None, None], "row")], [kinds[name] for name in GATHER_FIRST] + ["row"])
    ready = {(name, 0): as_weight(name, f) for name, f in zip(GATHER_FIRST, fulls)}
    landing = {}
    taps_all = fulls[-1].reshape(N_CHIPS, -1, cs_c)[:, :n_taps]
    V = {name: w[name] for name in VECTORS}
    V["conv_dw_w"] = taps_all[:, :L * KC].reshape(N_CHIPS, L, KC, cs_c).transpose(1, 2, 0, 3).reshape(L, KC, N_CHIPS * cs_c)
    V["ffn_dw_w"] = taps_all[:, L * KC:].reshape(N_CHIPS, L, KF, cs_f).transpose(1, 2, 0, 3).reshape(L, KF, N_CHIPS * cs_f)

    Bn, S, D = x.shape
    Mn = mem.shape[1]
    dims = (Bn, S, Mn, D, conv_dw_b.shape[1], w_down.shape[1] * N_CHIPS)
    xt = x.reshape(Bn * S, D)
    memf = mem.reshape(Bn * Mn, D)
    mem_n = _rms_fwd(memf, V["mem_norm_g"], "mem_norm")

    class LayerWeights:
        def __init__(self, l):
            self.l = l

        def __getitem__(self, name):
            if (name, self.l) not in ready:
                group = next(g for g in PASS_GROUPS if (name, self.l) in g)
                done = _gather_pass([landing.pop(t) for t in group], [layer_shards(lw, [nm])[0] for nm, lw in group],
                                    [kinds[nm] for nm, _ in group], f"gather_pass_{name}_{self.l}")
                ready.update({t: as_weight(t[0], f) for t, f in zip(group, done)})
            return ready[(name, self.l)]

    def carried_gather(entries):
        return lambda: _side_gather([layer_shards(lw, [nm])[0] for nm, lw, _ in entries], [kinds[nm] for nm, _, _ in entries],
                                    [rels for _, _, rels in entries], [landing.get((nm, lw)) for nm, lw, _ in entries])

    saved, W = [], []
    ht = _rms_fwd(xt, V["mix_norm_g"][0], "l0_mix_norm")
    for l in range(L):
        mine = {key: entries for (cl, key), entries in FWD_CARRY.items() if cl == l}
        sides = _Sides({key: carried_gather(entries) for key, entries in mine.items()},
                       on_land=lambda key, fulls, mine=mine: landing.update({(nm, lw): f for (nm, lw, _), f in zip(mine[key], fulls)}))
        W.append(LayerWeights(l))
        xt, ht, sv = _layer_fwd(xt, ht, mem_n, W[l], V, l, dims, sides, V["mix_norm_g"][l + 1] if l + 1 < L else None)
        saved.append(sv)
    loss, dx, dgf = _loss_bwd(xt, V["final_norm_g"], loss_target.reshape(Bn * S, D), "loss")
    loss = lax.psum(loss[0, 0], ("x", "y", "c"))

    late_names = [name for name in names if name not in EARLY]

    def views(gw, subset, twin):
        out = []
        for name in subset:
            g = gw[name][twin] if gw[name][twin].ndim == 3 else gw[name][twin][None]
            P, R, C = g.shape
            out.append(g.reshape(P, 2, R // 2, C) if kinds[name] == "col" else g.reshape(P, N_CHIPS, 2, R // (2 * N_CHIPS), C))
        return out

    def group_kinds(subset):
        return [kinds[name] for name in subset]

    class Reduction:
        def __init__(self, gw, subset, l, tag, first, table):
            self.gw, self.subset, self.l, self.tag, self.first, self.table = gw, subset, l, tag, first, table

        def sides(self):
            by_key = {self.first: lambda: _side_sibling_exchange(views(self.gw, self.subset, 1), group_kinds(self.subset))}
            by_key.update({key: (lambda names_=names_: _side_chip_exchange([self.pieces[nm] for nm in names_])) for key, names_ in self.table.items()})
            return by_key

        def on_land(self, key, landed):
            if key == self.first:
                self.sums(landed)
            elif key in self.table:
                got[self.l].update(zip(self.table[key], landed))

        def sums(self, lands):
            own, pieces = _chip_sums(views(self.gw, self.subset, 0), lands, group_kinds(self.subset), jc, f"chip_sums_{self.tag}_l{self.l}")
            owns[self.l].update(zip(self.subset, own))
            self.pieces = dict(zip(self.subset, pieces))

    def riding(reductions):
        return _Sides({key: side for r in reductions for key, side in r.sides().items()},
                      on_land=lambda key, landed: [r.on_land(key, landed) for r in reductions])

    dxb, dmem_n = dx, None
    smalls, owns, got = [None] * L, [{} for _ in range(L)], [{} for _ in range(L)]
    late = None
    for l in reversed(range(L)):
        dx, dxb, gw, sm = _layer_bwd_mlp(dx, dxb, saved[l], W[l], V, l, dims, riding([late] if late is not None else []))
        gw_mix = {}
        reductions = [Reduction(gw, EARLY, l, "mlp", "d_att", BWD_CARRY_EARLY)]
        if l == 0:
            reductions += [Reduction(gw_mix, names_, 0, tag, first, table) for tag, names_, first, table in BWD_LAST_LAYER]
        dx, dxb, dmem_n, sm2 = _layer_bwd_mix(dx, dxb, dmem_n, saved[l], mem_n, W[l], V, l, dims, riding(reductions), gw_mix)
        smalls[l] = {**sm, **sm2}
        late = Reduction(gw_mix, late_names, l, "mix", "d_gact", BWD_CARRY_LATE) if l > 0 else None
    last = Reduction(gw_mix, ("w_in",), 0, "in", None, {})
    last.sums(_sibling_exchange(views(gw_mix, last.subset, 1), group_kinds(last.subset), "grad_sibling_exchange_in_l0"))
    grad_x = dx.reshape(Bn, S, D)
    _, _, dgm = _rms_bwd(memf, V["mem_norm_g"], dmem_n, None, "mem_norm_b")
    small = {k: jnp.stack([sm[k] for sm in smalls]) if k in ("conv_dw_w", "ffn_dw_w") else jnp.concatenate([sm[k] for sm in smalls], axis=0)
             for k in smalls[0]}
    small["mem_norm_g"] = dgm
    small["final_norm_g"] = dgf

    small_w = conv_dw_b.shape[1]
    order = VECTORS + ("conv_dw_w", "ffn_dw_w")
    parts = [_rows(small[name], small_w) for name in order]
    counts = [p.shape[0] for p in parts]
    summed, landed_last = _reduce_small(_pack(parts), [last.pieces[name] for name in last.subset])
    got[0].update(zip(last.subset, landed_last))

    keys = ["w_pool_grp" if name == "w_pool" else name for name in names]
    rows3 = lambda t: t.reshape(t.shape[0], -1, t.shape[-1])
    wmv = [[rows3(to_mat(name, d[key])) for name, key in zip(names, keys)] for d in (w, m, v)]
    gshards, updates = None, None
    for l in reversed(range(L)):
        gshards = _final_sums([owns[l][name] for name in names], [got[l][name] for name in names], jc, gshards, l, L, f"final_sums_l{l}")
        gshards = _halves_exchange(gshards, l, f"grad_halves_exchange_l{l}")
        updates = _adamw_layer(wmv[0], [rows3(t) for t in gshards], wmv[1], wmv[2], updates, l, f"adamw_l{l}")
    grads, delta, new_m, new_v = {}, {}, {}, {}
    for i, (name, key) in enumerate(zip(names, keys)):
        Lg, P, _, RH, CS = gshards[i].shape
        grads[key] = from_mat(name, gshards[i].reshape(Lg, P, 2 * RH, CS))
        for d, t in zip((delta, new_m, new_v), updates[3 * i:3 * i + 3]):
            d[key] = from_mat(name, t.reshape(Lg, P, 2 * RH, CS))

    off = 0
    for name, cnt in zip(order, counts):
        t = summed[off:off + cnt]
        off += cnt + (-cnt) % 8
        if name in VECTORS:
            grads[name] = t.reshape(w[name].shape)
        else:
            full = t.reshape(small[name].shape)
            cs = w[name].shape[2]
            grads[name] = lax.dynamic_slice_in_dim(full, j * cs, cs, axis=2)

    vec =[_pack([_rows(d[name], small_w) for name in VECTORS]) for d in (w, grads, m, v)]
    outs = _adamw(*vec, "adamw_vectors")
    off = 0
    for name in VECTORS:
        cnt = w[name].size // small_w
        for d, t in zip((delta, new_m, new_v), outs):
            d[name] = t[off:off + cnt].reshape(w[name].shape)
        off += cnt + (-cnt) % 8
    for name in ("conv_dw_w", "ffn_dw_w"):
        delta[name], new_m[name], new_v[name] = _adamw(w[name], grads[name], m[name], v[name], "adamw_" + name)

    return (loss, grad_x, *[grads[k] for k in WEIGHTS], *[delta[k] for k in WEIGHTS], *[new_m[k] for k in WEIGHTS], *[new_v[k] for k in WEIGHTS])
```

```python
import functools

import jax
import jax.numpy as jnp
from jax import lax
from jax.experimental import pallas as pl
from jax.experimental.pallas import tpu as pltpu

F32 = jnp.float32
BF = jnp.bfloat16
SDS = jax.ShapeDtypeStruct
MESH = pl.DeviceIdType.MESH
ANY = pl.BlockSpec(memory_space=pl.ANY)

EPS = 1e-6
XA_HEADS = 4
POOL_WINDOWS = (2, 4, 8, 16)
N_CHIPS = 4
ADAM_LR, ADAM_B1, ADAM_B2, ADAM_EPS, ADAM_WD, ADAM_STEP = 0.001, 0.9, 0.999, 1e-08, 0.01, 10

LANES = 128
ROW_BLOCK = 512
VMEM_LIMIT = 56 * 1024 * 1024


def _params(*sem):
    return pltpu.CompilerParams(dimension_semantics=sem if sem else None, vmem_limit_bytes=VMEM_LIMIT)


def _tile(n, cap, mult=LANES):
    if n <= cap:
        return n
    for t in range(cap - cap % mult, 0, -mult):
        if n % t == 0:
            return t
    return n


_DN = {"nn": (((1,), (0,)), ((), ())), "nt": (((1,), (1,)), ((), ())), "tn": (((0,), (0,)), ((), ()))}


class _Side:
    def __init__(self, ins, outs, n, make, n_alias=0):
        self.ins, self.outs, self.n, self.make, self.n_alias = list(ins), list(outs), n, make, n_alias


def _call(body, args, *, grid, in_specs, out_specs, out_shape, semantics, name, scratch_shapes=(), side=None, prefetch=(), aliases=None):
    n_pf = len(prefetch)
    aliases = {n_pf + i: o for i, o in (aliases or {}).items()}
    n_in, n_out, n_scr = len(args), len(out_shape), len(scratch_shapes)
    n_si, n_so = (len(side.ins), len(side.outs)) if side is not None else (0, 0)
    if side is not None:
        aliases.update({n_pf + n_in + n_si - side.n_alias + i: n_out + i for i in range(side.n_alias)})

    def carrying(*refs):
        pf, refs = refs[:n_pf], refs[n_pf:]
        ins, s_in = refs[:n_in], refs[n_in:n_in + n_si]
        outs, s_out = refs[n_in + n_si:n_in + n_si + n_out], refs[n_in + n_si + n_out:n_in + n_si + n_out + n_so]
        scr = refs[n_in + n_si + n_out + n_so:]
        if side is None:
            return body(*pf, *ins, *outs, *scr)
        copies = side.make(s_in, s_out, scr[n_scr], scr[n_scr + 1])
        ids = [pl.program_id(d) for d in range(len(grid))]
        first, last = ids[0] == 0, ids[0] == grid[0] - 1
        for d in range(1, len(grid)):
            first, last = first & (ids[d] == 0), last & (ids[d] == grid[d] - 1)

        @pl.when(first)
        def _():
            for cp in copies:
                cp.start()

        body(*pf, *ins, *outs, *scr[:n_scr])

        @pl.when(last)
        def _():
            for cp in copies:
                cp.wait()

    sems = [pltpu.SemaphoreType.DMA((side.n,)), pltpu.SemaphoreType.DMA((side.n,))] if side is not None else []
    outs = pl.pallas_call(
        carrying, grid_spec=pltpu.PrefetchScalarGridSpec(
            num_scalar_prefetch=n_pf, grid=grid, in_specs=list(in_specs) + [ANY] * n_si, out_specs=list(out_specs) + [ANY] * n_so,
            scratch_shapes=list(scratch_shapes) + sems),
        out_shape=list(out_shape) + (side.outs if side is not None else []), input_output_aliases=aliases,
        compiler_params=_params(*(semantics if side is None else ["arbitrary"] * len(grid))), name=name)(
            *prefetch, *args, *(side.ins if side is not None else []))
    return list(outs) if side is None else (list(outs[:n_out]), list(outs[n_out:]))


def _call1(body, args, *, out_spec, out_shape, side=None, **kw):
    got = _call(body, args, out_specs=[out_spec], out_shape=[out_shape], side=side, **kw)
    return got[0] if side is None else (got[0][0], got[1])


MM_VMEM_BUDGET = 40 * 1024 * 1024
MM_STEP_MACS = 2200 * 1024 * 1024
MXU_WIDTH = 256
MM_STEP_COST_BYTES = 1 << 20


def _divisors(n):
    return [t for t in range(LANES, n + 1, LANES) if n % t == 0] or [n]


def _mm_tiles(M, N, K, a_bytes, b_bytes, o_bytes, n_unit=None):
    best = None
    for tk in _divisors(K):
        for tm in _divisors(M):
            for tn in _divisors(N if n_unit is None else n_unit):
                nk = K // tk
                foot = 2 * (tm * tk * a_bytes + tk * tn * b_bytes + tm * tn * o_bytes) + (tm * tn * 4 if nk > 1 else 0)
                if foot > MM_VMEM_BUDGET or tm * tn * tk > MM_STEP_MACS or tn < min(N, MXU_WIDTH) or tm < min(M, MXU_WIDTH):
                    continue
                steps = (M // tm) * (N // tn) * nk
                traffic = M * K * a_bytes * (N // tn if nk > 1 else 1) + K * N * b_bytes * (M // tm) + M * N * o_bytes
                exposed = tm * tk * a_bytes + tk * tn * b_bytes + tm * tn * o_bytes
                cost = traffic + exposed + steps * MM_STEP_COST_BYTES + (nk - 1) * M * N * 8
                if best is None or cost < best[0]:
                    best = (cost, tm, tn, tk)
    assert best is not None, (M, N, K)
    return best[1:]


def _mm(a, b, dims, out_dtype, name, res=None, bl=None, side=None, twin=None, b_halves=False):
    bs = b.shape[1:] if bl is not None or b_halves else b.shape
    if dims == "nn":
        (M, K), (K2, N) = a.shape, bs
    elif dims == "nt":
        (M, K), (N, K2) = a.shape, bs
    else:
        (K, M), (K2, N) = a.shape, bs
    assert K == K2, (name, a.shape, b.shape)
    n_half = N
    if b_halves:
        assert dims == "tn" and bl is None
        N = 2 * n_half
    tm, tn, tk = _mm_tiles(M, N, K, a.dtype.itemsize, b.dtype.itemsize, jnp.dtype(out_dtype).itemsize
                           + (res.dtype.itemsize if res is not None else 0) + (jnp.dtype(twin).itemsize if twin is not None else 0),
                           n_unit=n_half if b_halves else None)
    nk = K // tk
    lead = (None,) if bl is not None or b_halves else ()
    pre = (lambda *ix: (bl,) + ix) if bl is not None else (lambda *ix: ix)
    if b_halves:
        per_half = n_half // tn
        pre = lambda k, j: (j // per_half, k, j % per_half)
    if dims == "tn":
        a_spec = pl.BlockSpec((tk, tm), lambda i, j, k: (k, i))
    else:
        a_spec = pl.BlockSpec((tm, tk), lambda i, j, k: (i, k))
    if dims == "nt":
        b_spec = pl.BlockSpec(lead + (tn, tk), lambda i, j, k: pre(j, k))
    else:
        b_spec = pl.BlockSpec(lead + (tk, tn), lambda i, j, k: pre(k, j))
    o_spec = pl.BlockSpec((tm, tn), lambda i, j, k: (i, j))
    in_specs, args = [a_spec, b_spec], [a, b]
    if res is not None:
        in_specs.append(o_spec)
        args.append(res)
    n_main = len(args)
    n_out = 1 if twin is None else 2

    def body(*refs):
        a_ref, b_ref = refs[0], refs[1]
        r_ref = refs[2] if res is not None else None
        o_ref = refs[n_main]
        p = lax.dot_general(a_ref[...].astype(BF), b_ref[...].astype(BF), _DN[dims], preferred_element_type=F32)

        def finish(t):
            if r_ref is not None:
                t = t + r_ref[...]
            o_ref[...] = t.astype(out_dtype)
            if twin is not None:
                refs[n_main + 1][...] = t.astype(twin)

        if nk == 1:
            finish(p)
        else:
            acc = refs[n_main + n_out]
            k = pl.program_id(2)

            @pl.when(k == 0)
            def _():
                acc[...] = p

            @pl.when(k > 0)
            def _():
                acc[...] += p

            @pl.when(k == nk - 1)
            def _():
                finish(acc[...])

    got = _call(body, args, grid=(M // tm, N // tn, nk), in_specs=in_specs, out_specs=[o_spec] * n_out,
                out_shape=[SDS((M, N), out_dtype)] + ([SDS((M, N), twin)] if twin is not None else []),
                scratch_shapes=[pltpu.VMEM((tm, tn), F32)] if nk > 1 else [], semantics=("parallel", "parallel", "arbitrary"),
                name=name, side=side)
    outs, landed = (got, None) if side is None else got
    out = outs[0] if twin is None else (outs[0], outs[1])
    return out if side is None else (out, landed)


def _rms(x, g):
    return x * lax.rsqrt(jnp.mean(x * x, axis=-1, keepdims=True) + EPS) * g


def _ln_silu(x, g, b):
    mu = jnp.mean(x, axis=-1, keepdims=True)
    xc = x - mu
    var = jnp.mean(xc * xc, axis=-1, keepdims=True)
    return jax.nn.silu(xc * lax.rsqrt(var + EPS) * g + b)


def _merge(gc, gp, yc, yp, ps):
    return jax.nn.sigmoid(gc) * yc + jax.nn.sigmoid(gp) * (yp * ps)


def _gated(gate, val):
    return jax.nn.gelu(gate) * val


def _rms_fwd(x, g, name):
    T, D = x.shape
    tb = _tile(T, ROW_BLOCK, 8)

    def body(x_ref, g_ref, o_ref):
        o_ref[...] = _rms(x_ref[...], g_ref[...]).astype(BF)

    row = pl.BlockSpec((tb, D), lambda i: (i, 0))
    return pl.pallas_call(body, grid=(T // tb,), in_specs=[row, pl.BlockSpec((1, D), lambda i: (0, 0))], out_specs=row,
                          out_shape=SDS((T, D), BF), compiler_params=_params("parallel"), name=name)(x, g.reshape(1, D))


def _rms_bwd(x, g, dh, dres, name):
    T, D = x.shape
    tb = _tile(T, ROW_BLOCK, 8)

    def body(*refs):
        if dres is not None:
            x_ref, g_ref, dh_ref, dres_ref, dx_ref, dxb_ref, dg_ref = refs
        else:
            x_ref, g_ref, dh_ref, dx_ref, dxb_ref, dg_ref = refs
        _, vjp = jax.vjp(_rms, x_ref[...], g_ref[...])
        dx, dg = vjp(dh_ref[...].astype(F32))
        if dres is not None:
            dx = dx + dres_ref[...]
        dx_ref[...] = dx
        dxb_ref[...] = dx.astype(BF)

        @pl.when(pl.program_id(0) == 0)
        def _():
            dg_ref[...] = jnp.zeros_like(dg_ref)

        dg_ref[...] += dg

    row = pl.BlockSpec((tb, D), lambda i: (i, 0))
    vec = pl.BlockSpec((1, D), lambda i: (0, 0))
    ins = [x, g.reshape(1, D), dh] + ([dres] if dres is not None else [])
    return pl.pallas_call(
        body, grid=(T // tb,), in_specs=[row, vec, row] + ([row] if dres is not None else []), out_specs=[row, row, vec],
        out_shape=[SDS((T, D), F32), SDS((T, D), BF), SDS((1, D), F32)], compiler_params=_params("arbitrary"), name=name)(*ins)


def _row_tile(M, K, N, per_row_bytes):
    fixed = K * N * 2
    fit = [t for t in _divisors(M) if fixed + 2 * t * per_row_bytes <= MM_VMEM_BUDGET and t * K * N <= MM_STEP_MACS]
    return max(fit) if fit else min(_divisors(M))


def _mm_rms_fwd(a, b, res, g, name, side=None):
    M, K = a.shape
    N = b.shape[2]
    tm = _row_tile(M, K, N, K * 2 + N * (4 + 4 + 2))

    def body(a_ref, b_ref, r_ref, g_ref, x_ref, h_ref):
        x = r_ref[...] + lax.dot_general(a_ref[...], b_ref[...], _DN["nn"], preferred_element_type=F32)
        x_ref[...] = x
        h_ref[...] = _rms(x, g_ref[...]).astype(BF)

    row = pl.BlockSpec((tm, N), lambda i: (i, 0))
    return _call(body, (a, b, res, g.reshape(1, N)), grid=(M // tm,),
                 in_specs=[pl.BlockSpec((tm, K), lambda i: (i, 0)), pl.BlockSpec((None, K, N), lambda i: (0, 0, 0), pipeline_mode=pl.Buffered(1)), row,
                           pl.BlockSpec((1, N), lambda i: (0, 0))],
                 out_specs=[row, row], out_shape=[SDS((M, N), F32), SDS((M, N), BF)], semantics=("parallel",), name=name, side=side)


def _mm_rms_bwd(a, b, x, g, dres, name, side=None, a_halves=False):
    M = a.shape[-2]
    N, K = b.shape[1:]
    tm = _row_tile(M, K, N, K * 2 + N * (4 + 4 + 4 + 2))

    def body(a_ref, b_ref, x_ref, g_ref, r_ref, dx_ref, dxb_ref, dg_ref):
        if a_halves:
            dh = (lax.dot_general(a_ref[0], b_ref[:, :K // 2], _DN["nt"], preferred_element_type=F32)
                  + lax.dot_general(a_ref[1], b_ref[:, K // 2:], _DN["nt"], preferred_element_type=F32))
        else:
            dh = lax.dot_general(a_ref[...], b_ref[...], _DN["nt"], preferred_element_type=F32)
        _, vjp = jax.vjp(_rms, x_ref[...], g_ref[...])
        dx, dg = vjp(dh)
        dx = dx + r_ref[...]
        dx_ref[...] = dx
        dxb_ref[...] = dx.astype(BF)

        @pl.when(pl.program_id(0) == 0)
        def _():
            dg_ref[...] = jnp.zeros_like(dg_ref)

        dg_ref[...] += dg

    row = pl.BlockSpec((tm, N), lambda i: (i, 0))
    vec = pl.BlockSpec((1, N), lambda i: (0, 0))
    return _call(
        body, (a, b, x, g.reshape(1, N), dres), grid=(M // tm,),
        in_specs=[pl.BlockSpec((2, tm, K // 2), lambda i: (0, i, 0)) if a_halves else pl.BlockSpec((tm, K), lambda i: (i, 0)),
                  pl.BlockSpec((None, N, K), lambda i: (0, 0, 0), pipeline_mode=pl.Buffered(1)), row, vec, row],
        out_specs=[row, row, vec], out_shape=[SDS((M, N), F32), SDS((M, N), BF), SDS((1, N), F32)],
        semantics=("arbitrary",), name=name, side=side)


def _loss_bwd(x, g, target, name):
    T, D = x.shape
    tb = _tile(T, ROW_BLOCK, 8)
    nb = T // tb

    def body(x_ref, g_ref, t_ref, loss_ref, dx_ref, dg_ref, acc):
        i = pl.program_id(0)
        y, vjp = jax.vjp(_rms, x_ref[...], g_ref[...])
        err = y - t_ref[...]
        dx, dg = vjp(err * (1.0 / D))
        dx_ref[...] = dx

        @pl.when(i == 0)
        def _():
            dg_ref[...] = jnp.zeros_like(dg_ref)
            acc[...] = jnp.zeros_like(acc)

        dg_ref[...] += dg
        acc[...] += jnp.sum(err * err, axis=0, keepdims=True)

        @pl.when(i == nb - 1)
        def _():
            loss_ref[...] = jnp.full(loss_ref.shape, (0.5 / D) * jnp.sum(acc[...]), F32)

    row = pl.BlockSpec((tb, D), lambda i: (i, 0))
    vec = pl.BlockSpec((1, D), lambda i: (0, 0))
    return pl.pallas_call(
        body, grid=(nb,), in_specs=[row, vec, row], out_specs=[pl.BlockSpec((1, LANES), lambda i: (0, 0)), row, vec],
        out_shape=[SDS((1, LANES), F32), SDS((T, D), F32), SDS((1, D), F32)], scratch_shapes=[pltpu.VMEM((1, D), F32)],
        compiler_params=_params("arbitrary"), name=name)(x, g.reshape(1, D), target)


def _ln_silu_fwd(cv, g, b, name):
    T, C = cv.shape
    tb = _tile(T, ROW_BLOCK, 8)

    def body(x_ref, g_ref, b_ref, o_ref):
        o_ref[...] = _ln_silu(x_ref[...], g_ref[...], b_ref[...]).astype(BF)

    row = pl.BlockSpec((tb, C), lambda i: (i, 0))
    vec = pl.BlockSpec((1, C), lambda i: (0, 0))
    return pl.pallas_call(body, grid=(T // tb,), in_specs=[row, vec, vec], out_specs=row, out_shape=SDS((T, C), BF),
                          compiler_params=_params("parallel"), name=name)(cv, g.reshape(1, C), b.reshape(1, C))


def _ln_silu_bwd(cv, g, b, dy, name, side=None):
    T, C = cv.shape
    tb = _tile(T, ROW_BLOCK, 8)

    def body(x_ref, g_ref, b_ref, dy_ref, dx_ref, dg_ref, db_ref):
        _, vjp = jax.vjp(_ln_silu, x_ref[...], g_ref[...], b_ref[...])
        dx, dg, db = vjp(dy_ref[...].astype(F32))
        dx_ref[...] = dx

        @pl.when(pl.program_id(0) == 0)
        def _():
            dg_ref[...] = jnp.zeros_like(dg_ref)
            db_ref[...] = jnp.zeros_like(db_ref)

        dg_ref[...] += dg
        db_ref[...] += db

    row = pl.BlockSpec((tb, C), lambda i: (i, 0))
    vec = pl.BlockSpec((1, C), lambda i: (0, 0))
    return _call(
        body, (cv, g.reshape(1, C), b.reshape(1, C), dy), grid=(T // tb,), in_specs=[row, vec, vec, row], out_specs=[row, vec, vec],
        out_shape=[SDS((T, C), F32), SDS((1, C), F32), SDS((1, C), F32)], semantics=("arbitrary",), name=name, side=side)


def _merge_fwd(proj, yc, yp, ps, C, name, side=None):
    T, D = yc.shape
    tb = _tile(T, ROW_BLOCK, 8)
    nj = D // C

    def body(gc_ref, gp_ref, yc_ref, yp_ref, ps_ref, o_ref):
        o_ref[...] = _merge(gc_ref[...], gp_ref[...], yc_ref[...].astype(F32), yp_ref[...].astype(F32), ps_ref[...]).astype(BF)

    blk = pl.BlockSpec((tb, C), lambda i, j: (i, j))
    return _call1(
        body, (proj, proj, yc, yp, ps.reshape(1, D)), grid=(T // tb, nj),
        in_specs=[pl.BlockSpec((tb, C), lambda i, j: (i, 3 + j)), pl.BlockSpec((tb, C), lambda i, j: (i, 3 + nj + j)), blk, blk,
                  pl.BlockSpec((1, C), lambda i, j: (0, j))],
        out_spec=blk, out_shape=SDS((T, D), BF), semantics=("parallel", "parallel"), name=name, side=side)


def _merge_bwd(proj, yc, yp, ps, dm, C, name, side=None):
    T, D = yc.shape
    tb = _tile(T, ROW_BLOCK, 8)
    nj = D // C

    def body(gc_ref, gp_ref, yc_ref, yp_ref, ps_ref, dm_ref, dgc_ref, dgp_ref, dyc_ref, dyp_ref, dps_ref):
        _, vjp = jax.vjp(_merge, gc_ref[...], gp_ref[...], yc_ref[...].astype(F32), yp_ref[...].astype(F32), ps_ref[...])
        dgc, dgp, dyc, dyp, dps = vjp(dm_ref[...].astype(F32))
        dgc_ref[...] = dgc.astype(BF)
        dgp_ref[...] = dgp.astype(BF)
        dyc_ref[...] = dyc.astype(BF)
        dyp_ref[...] = dyp.astype(BF)

        @pl.when(pl.program_id(1) == 0)
        def _():
            dps_ref[...] = jnp.zeros_like(dps_ref)

        dps_ref[...] += dps

    blk = pl.BlockSpec((tb, C), lambda j, i: (i, j))
    vec = pl.BlockSpec((1, C), lambda j, i: (0, j))
    return _call(
        body, (proj, proj, yc, yp, ps.reshape(1, D), dm), grid=(nj, T // tb),
        in_specs=[pl.BlockSpec((tb, C), lambda j, i: (i, 3 + j)), pl.BlockSpec((tb, C), lambda j, i: (i, 3 + nj + j)), blk, blk, vec, blk],
        out_specs=[blk, blk, blk, blk, vec], out_shape=[SDS((T, D), BF)] * 4 + [SDS((1, D), F32)],
        semantics=("parallel", "arbitrary"), name=name, side=side)


def _shd(v, s, rows):
    if s == 0:
        return v
    return jnp.where(rows >= s, pltpu.roll(v, s, 0), 0.0)


def _shu(v, s, rows):
    if s == 0:
        return v
    n = v.shape[0]
    return jnp.where(rows < n - s, pltpu.roll(v, n - s, 0), 0.0)


def _glu_conv_fwd(proj, w, b, Bn, S, C, name, side=None):
    K = w.shape[0]
    sl = min(LANES, C)
    ns = C // sl

    def body(a_ref, gl_ref, w_ref, b_ref, o_ref):
        y0 = a_ref[...] * jax.nn.sigmoid(gl_ref[...])
        rows = lax.broadcasted_iota(jnp.int32, y0.shape, 0)
        acc = jnp.zeros_like(y0) + b_ref[...]
        for k in range(K):
            acc = acc + w_ref[k:k + 1, :] * _shd(y0, K - 1 - k, rows)
        o_ref[...] = acc

    return _call1(
        body, (proj, proj, w, b.reshape(1, C)), grid=(Bn, ns),
        in_specs=[pl.BlockSpec((S, sl), lambda bi, j: (bi, j)), pl.BlockSpec((S, sl), lambda bi, j: (bi, ns + j)),
                  pl.BlockSpec((K, sl), lambda bi, j: (0, j)), pl.BlockSpec((1, sl), lambda bi, j: (0, j))],
        out_spec=pl.BlockSpec((S, sl), lambda bi, j: (bi, j)), out_shape=SDS((Bn * S, C), F32),
        semantics=("parallel", "parallel"), name=name, side=side)


def _glu_conv_bwd(proj, w, dcv, Bn, S, C, name, side=None):
    K = w.shape[0]
    sl = min(LANES, C)
    ns = C // sl

    def body(a_ref, gl_ref, w_ref, d_ref, da_ref, dgl_ref, dw_ref, db_ref):
        a = a_ref[...]
        sg = jax.nn.sigmoid(gl_ref[...])
        y0 = a * sg
        d = d_ref[...]
        rows = lax.broadcasted_iota(jnp.int32, y0.shape, 0)

        @pl.when(pl.program_id(1) == 0)
        def _():
            dw_ref[...] = jnp.zeros_like(dw_ref)
            db_ref[...] = jnp.zeros_like(db_ref)

        dy0 = jnp.zeros_like(y0)
        for k in range(K):
            s = K - 1 - k
            dw_ref[k:k + 1, :] += jnp.sum(d * _shd(y0, s, rows), axis=0, keepdims=True)
            dy0 = dy0 + w_ref[k:k + 1, :] * _shu(d, s, rows)
        db_ref[...] += jnp.sum(d, axis=0, keepdims=True)
        da_ref[...] = (dy0 * sg).astype(BF)
        dgl_ref[...] = (dy0 * a * sg * (1.0 - sg)).astype(BF)

    blk = pl.BlockSpec((S, sl), lambda j, bi: (bi, j))
    return _call(
        body, (proj, proj, w, dcv), grid=(ns, Bn),
        in_specs=[blk, pl.BlockSpec((S, sl), lambda j, bi: (bi, ns + j)), pl.BlockSpec((K, sl), lambda j, bi: (0, j)), blk],
        out_specs=[blk, blk, pl.BlockSpec((K, sl), lambda j, bi: (0, j)), pl.BlockSpec((1, sl), lambda j, bi: (0, j))],
        out_shape=[SDS((Bn * S, C), BF), SDS((Bn * S, C), BF), SDS((K, C), F32), SDS((1, C), F32)],
        semantics=("parallel", "arbitrary"), name=name, side=side)


def _pool_z(u, g, rows):
    s2 = u + _shd(u, 1, rows)
    s4 = s2 + _shd(s2, 2, rows)
    s8 = s4 + _shd(s4, 4, rows)
    s16 = s8 + _shd(s8, 8, rows)
    sw = jnp.where(g == 0, s2, jnp.where(g == 1, s4, jnp.where(g == 2, s8, s16)))
    cnt = jnp.minimum(rows + 1, POOL_WINDOWS[0] << g).astype(F32)
    return sw / cnt - u, cnt


def _pool_fwd(proj, wpt, l, Bn, S, C, D, name):
    G = len(POOL_WINDOWS)
    gd, go = C // G, D // G

    def body(u_ref, w_ref, o_ref):
        g = pl.program_id(1)
        u = u_ref[...]
        rows = lax.broadcasted_iota(jnp.int32, u.shape, 0)
        zp, _ = _pool_z(u, g, rows)
        o_ref[...] = lax.dot_general(zp.astype(BF), w_ref[...], _DN["nt"], preferred_element_type=F32).astype(BF)

    return pl.pallas_call(
        body, grid=(Bn, G),
        in_specs=[pl.BlockSpec((S, gd), lambda bi, g: (bi, 2 * G + g)), pl.BlockSpec((None, go, gd), lambda bi, g: (l * G + g, 0, 0))],
        out_specs=pl.BlockSpec((S, go), lambda bi, g: (bi, g)), out_shape=SDS((Bn * S, D), BF),
        compiler_params=_params("parallel", "parallel"), name=name)(proj, wpt)


def _pool_bwd(proj, wpt, dyp, l, Bn, S, C, D, name):
    G = len(POOL_WINDOWS)
    gd, go = C // G, D // G

    def body(u_ref, w_ref, d_ref, du_ref, dw_ref):
        g = pl.program_id(0)
        u = u_ref[...]
        rows = lax.broadcasted_iota(jnp.int32, u.shape, 0)
        zp, cnt = _pool_z(u, g, rows)
        d = d_ref[...]
        dzp = lax.dot_general(d, w_ref[...], _DN["nn"], preferred_element_type=F32)

        @pl.when(pl.program_id(1) == 0)
        def _():
            dw_ref[...] = jnp.zeros_like(dw_ref)

        dw_ref[...] += lax.dot_general(d, zp.astype(BF), _DN["tn"], preferred_element_type=F32)
        dsw = dzp / cnt
        zero = jnp.zeros_like(dsw)
        d16 = jnp.where(g == 3, dsw, zero)
        d8 = jnp.where(g == 2, dsw, zero) + d16 + _shu(d16, 8, rows)
        d4 = jnp.where(g == 1, dsw, zero) + d8 + _shu(d8, 4, rows)
        d2 = jnp.where(g == 0, dsw, zero) + d4 + _shu(d4, 2, rows)
        d1 = d2 + _shu(d2, 1, rows)
        du_ref[...] = (d1 - dzp).astype(BF)

    return pl.pallas_call(
        body, grid=(G, Bn),
        in_specs=[pl.BlockSpec((S, gd), lambda g, bi: (bi, 2 * G + g)), pl.BlockSpec((None, go, gd), lambda g, bi: (l * G + g, 0, 0)),
                  pl.BlockSpec((S, go), lambda g, bi: (bi, g))],
        out_specs=[pl.BlockSpec((S, gd), lambda g, bi: (bi, g)), pl.BlockSpec((None, go, gd), lambda g, bi: (g, 0, 0))],
        out_shape=[SDS((Bn * S, C), BF), SDS((G, go, gd), F32)],
        compiler_params=_params("parallel", "arbitrary"), name=name)(proj, wpt, dyp)


def _ffn_conv(u, w_ref, rows):
    K = w_ref.shape[0]
    acc = w_ref[K - 1:K, :] * u
    for k in range(K - 1):
        acc = acc + w_ref[k:k + 1, :] * _shd(u, K - 1 - k, rows)
    return acc


def _ffn_cb(F):
    return _tile(F, 256)


def _ffn_act_fwd(up0, w, Bn, S, F, name, side=None):
    cb = _ffn_cb(F)
    nj = F // cb

    def body(g_ref, v_ref, wg_ref, wv_ref, o_ref):
        rows = lax.broadcasted_iota(jnp.int32, g_ref.shape, 0)
        o_ref[...] = _gated(_ffn_conv(g_ref[...], wg_ref, rows), _ffn_conv(v_ref[...], wv_ref, rows)).astype(BF)

    K = w.shape[0]
    return _call1(
        body, (up0, up0, w, w), grid=(Bn, nj),
        in_specs=[pl.BlockSpec((S, cb), lambda bi, j: (bi, j)), pl.BlockSpec((S, cb), lambda bi, j: (bi, nj + j)),
                  pl.BlockSpec((K, cb), lambda bi, j: (0, j)), pl.BlockSpec((K, cb), lambda bi, j: (0, nj + j))],
        out_spec=pl.BlockSpec((S, cb), lambda bi, j: (bi, j)), out_shape=SDS((Bn * S, F), BF),
        semantics=("parallel", "parallel"), name=name, side=side)


SUBLANES = 8
FFN_HALO = SUBLANES
FFN_ROWS = 64
GELU_C0, GELU_C1 = 0.7978845608028654, 0.044715


def _gelu_and_grad(x):
    x2 = x * x
    t = jnp.tanh(GELU_C0 * (x + GELU_C1 * (x2 * x)))
    cdf = 0.5 * (1.0 + t)
    return x * cdf, cdf + (0.5 * GELU_C0) * x * (1.0 - t * t) * (1.0 + (3.0 * GELU_C1) * x2)


def _ffn_act_bwd(up0, w, dg, Bn, S, F, name, side=None):
    cb = min(LANES, F)
    nj = F // cb
    K = w.shape[0]
    rc = FFN_ROWS if S % FFN_ROWS == 0 else S
    win = rc + 2 * FFN_HALO
    assert K - 1 <= FFN_HALO and rc % SUBLANES == 0

    def body(g_ref, v_ref, wg_ref, wv_ref, d_ref, do_ref, dwg_ref, dwv_ref, gp, vp, dp):
        for pad, src in ((gp, g_ref), (vp, v_ref), (dp, d_ref)):
            pad[0:FFN_HALO, :] = jnp.zeros((FFN_HALO, cb), F32)
            pad[FFN_HALO + S:, :] = jnp.zeros((FFN_HALO, cb), F32)
            pad[FFN_HALO:FFN_HALO + S, :] = src[...].astype(F32)
        wg = [wg_ref[k:k + 1, :] for k in range(K)]
        wv = [wv_ref[k:k + 1, :] for k in range(K)]

        def taps(u):
            return [pltpu.roll(u, K - 1 - k, 0) for k in range(K - 1)] + [u]

        def conv(us, ws):
            acc = ws[K - 1] * us[K - 1]
            for k in range(K - 1):
                acc = acc + ws[k] * us[k]
            return acc

        def conv_t(dc, ws):
            acc = ws[K - 1] * dc
            for k in range(K - 1):
                acc = acc + ws[k] * pltpu.roll(dc, win - (K - 1 - k), 0)
            return acc

        def fold(t):
            acc = t[FFN_HALO:FFN_HALO + SUBLANES]
            for i in range(1, rc // SUBLANES):
                acc = acc + t[FFN_HALO + SUBLANES * i:FFN_HALO + SUBLANES * (i + 1)]
            return acc

        def chunk(c, sums):
            r0 = pl.multiple_of(c * rc, SUBLANES)
            gs, vs, d = taps(gp[pl.ds(r0, win), :]), taps(vp[pl.ds(r0, win), :]), dp[pl.ds(r0, win), :]
            ge, dge = _gelu_and_grad(conv(gs, wg))
            dgc = d * conv(vs, wv) * dge
            dvc = d * ge
            do_ref[0, pl.ds(r0, rc), :] = conv_t(dgc, wg)[FFN_HALO:FFN_HALO + rc].astype(BF)
            do_ref[1, pl.ds(r0, rc), :] = conv_t(dvc, wv)[FFN_HALO:FFN_HALO + rc].astype(BF)
            new = [fold(dc * u) for us, dc in ((gs, dgc), (vs, dvc)) for u in us]
            return tuple(a + b for a, b in zip(sums, new))

        sums = lax.fori_loop(0, S // rc, chunk, tuple(jnp.zeros((SUBLANES, cb), F32) for _ in range(2 * K)))

        @pl.when(pl.program_id(1) == 0)
        def _():
            dwg_ref[...] = jnp.zeros_like(dwg_ref)
            dwv_ref[...] = jnp.zeros_like(dwv_ref)

        for k in range(K):
            dwg_ref[k:k + 1, :] += jnp.sum(sums[k], axis=0, keepdims=True)
            dwv_ref[k:k + 1, :] += jnp.sum(sums[K + k], axis=0, keepdims=True)

    blk = pl.BlockSpec((S, cb), lambda j, bi: (bi, j))
    wblk = pl.BlockSpec((K, cb), lambda j, bi: (0, j))
    return _call(
        body, (up0, up0, w, w, dg), grid=(nj, Bn),
        in_specs=[blk, pl.BlockSpec((S, cb), lambda j, bi: (bi, nj + j)), wblk, pl.BlockSpec((K, cb), lambda j, bi: (0, nj + j)), blk],
        out_specs=[pl.BlockSpec((2, S, cb), lambda j, bi: (0, bi, j)), wblk, wblk],
        out_shape=[SDS((2, Bn * S, F), BF), SDS((K, F), F32), SDS((K, F), F32)],
        scratch_shapes=[pltpu.VMEM((S + 2 * FFN_HALO, cb), F32)] * 3, semantics=("parallel", "arbitrary"), name=name, side=side)


def _softmax_rows(q, k, scale):
    sc = lax.dot_general(q, k, _DN["nt"], preferred_element_type=F32) * scale
    e = jnp.exp(sc - jnp.max(sc, axis=-1, keepdims=True))
    return e / jnp.sum(e, axis=-1, keepdims=True)


def _attn_ts(S):
    return _tile(S, 1024, 8)


def _attn_fwd(q, kv, Bn, S, Mn, D, name, side=None):
    H = XA_HEADS
    dh = D // H
    ts = _attn_ts(S)
    nsb = S // ts
    scale = dh ** -0.5

    def body(q_ref, k_ref, v_ref, o_ref):
        p = _softmax_rows(q_ref[...], k_ref[...], scale)
        o_ref[...] = lax.dot_general(p.astype(BF), v_ref[...], _DN["nn"], preferred_element_type=F32).astype(BF)

    qblk = pl.BlockSpec((ts, dh), lambda bi, h, s: (bi * nsb + s, h))
    return _call1(
        body, (q, kv, kv), grid=(Bn, H, nsb),
        in_specs=[qblk, pl.BlockSpec((Mn, dh), lambda bi, h, s: (bi, h)), pl.BlockSpec((Mn, dh), lambda bi, h, s: (bi, H + h))],
        out_spec=qblk, out_shape=SDS((Bn * S, D), BF), semantics=("parallel", "parallel", "parallel"), name=name, side=side)


def _attn_bwd(q, kv, datt, Bn, S, Mn, D, name):
    H = XA_HEADS
    dh = D // H
    ts = _attn_ts(S)
    nsb = S // ts
    scale = dh ** -0.5

    def body(q_ref, k_ref, v_ref, do_ref, dq_ref, dk_ref, dv_ref):
        q, k, v, do = q_ref[...], k_ref[...], v_ref[...], do_ref[...]
        p = _softmax_rows(q, k, scale)
        dp = lax.dot_general(do, v, _DN["nt"], preferred_element_type=F32)
        ds = (p * (dp - jnp.sum(dp * p, axis=-1, keepdims=True)) * scale).astype(BF)
        dq_ref[...] = lax.dot_general(ds, k, _DN["nn"], preferred_element_type=F32).astype(BF)

        @pl.when(pl.program_id(2) == 0)
        def _():
            dk_ref[...] = jnp.zeros_like(dk_ref)
            dv_ref[...] = jnp.zeros_like(dv_ref)

        dk_ref[...] += lax.dot_general(ds, q, _DN["tn"], preferred_element_type=F32)
        dv_ref[...] += lax.dot_general(p.astype(BF), do, _DN["tn"], preferred_element_type=F32)

    qblk = pl.BlockSpec((ts, dh), lambda bi, h, s: (bi * nsb + s, h))
    kblk = pl.BlockSpec((Mn, dh), lambda bi, h, s: (bi, h))
    return pl.pallas_call(
        body, grid=(Bn, H, nsb),
        in_specs=[qblk, kblk, pl.BlockSpec((Mn, dh), lambda bi, h, s: (bi, H + h)), qblk],
        out_specs=[qblk, kblk, kblk], out_shape=[SDS((Bn * S, D), BF), SDS((Bn * Mn, D), F32), SDS((Bn * Mn, D), F32)],
        compiler_params=_params("parallel", "parallel", "arbitrary"), name=name)(q, kv, kv, datt)


class _Sides:
    def __init__(self, by_key=None, on_land=None):
        self.by_key, self.landed, self.on_land = dict(by_key or {}), {}, on_land

    def run(self, key, fn, *args, **kw):
        side = self.by_key.get(key)
        if side is None:
            return fn(*args, **kw)
        out, self.landed[key] = fn(*args, side=side() if callable(side) else side, **kw)
        if self.on_land is not None:
            self.on_land(key, self.landed[key])
        return out

    def mm(self, key, *args, **kw):
        return self.run(key, _mm, *args, **kw)


def _layer_fwd(x, h, mem_n, W, V, l, dims, sides, next_g):
    Bn, S, Mn, D, C, F = dims
    n = f"l{l}_"
    proj = sides.mm("proj", h, W["w_in"], "nn", F32, n + "proj", bl=0)
    cv = sides.run("glu_conv", _glu_conv_fwd, proj, V["conv_dw_w"][l], V["conv_dw_b"][l], Bn, S, C, n + "glu_conv")
    yc1 = _ln_silu_fwd(cv, V["conv_ln_g"][l], V["conv_ln_b"][l], n + "ln_silu")
    yc = sides.mm("conv_out", yc1, W["w_conv_out"], "nn", BF, n + "conv_out", bl=0)
    yp = _pool_fwd(proj, W["w_pool"], 0, Bn, S, C, D, n + "pool")
    merged = sides.run("merge", _merge_fwd, proj, yc, yp, V["pool_scale"][l], C, n + "merge")
    x1, hq = sides.run("out_proj", _mm_rms_fwd, merged, W["w_out"], x, V["xattn_norm_g"][l], n + "out_proj")
    q = sides.mm("q_proj", hq, W["w_q"], "nn", BF, n + "q_proj", bl=0)
    kv = _mm(mem_n, W["w_kv"], "nn", BF, n + "kv_proj", bl=0)
    att = sides.run("attn", _attn_fwd, q, kv, Bn, S, Mn, D, n + "attn")
    x2, hf = sides.run("o_proj", _mm_rms_fwd, att, W["w_o"], x1, V["ffn_norm_g"][l], n + "o_proj")
    up0 = sides.mm("up_proj", hf, W["w_up"], "nn", F32, n + "up_proj", bl=0)
    gact = sides.run("ffn_act", _ffn_act_fwd, up0, V["ffn_dw_w"][l], Bn, S, F, n + "ffn_act")
    if next_g is not None:
        x3, h3 = sides.run("down_proj", _mm_rms_fwd, gact, W["w_down"], x2, next_g, n + "down_proj")
    else:
        x3, h3 = sides.mm("down_proj", gact, W["w_down"], "nn", F32, n + "down_proj", res=x2, bl=0), None
    return x3, h3, dict(x=x, h=h, proj=proj, cv=cv, yc1=yc1, yc=yc, yp=yp, merged=merged, x1=x1, hq=hq, q=q, kv=kv, att=att, x2=x2,
                        hf=hf, up0=up0, gact=gact)


def _layer_bwd_mlp(dx, dxb, sv, W, V, l, dims, sides):
    Bn, S, Mn, D, C, F = dims
    n = f"l{l}_b_"
    gw, sm = {}, {}
    dgact = sides.mm("d_gact", dxb, W["w_down"], "nt", BF, n + "d_gact", bl=0)
    gw["w_down"] = sides.mm("dw_down", sv["gact"], dxb, "tn", F32, n + "dw_down", twin=BF)
    dup0, dwg, dwv = sides.run("ffn_act_b", _ffn_act_bwd, sv["up0"], V["ffn_dw_w"][l], dgact, Bn, S, F, n + "ffn_act")
    sm["ffn_dw_w"] = jnp.concatenate([dwg, dwv], axis=1)
    dx2, dx2b, sm["ffn_norm_g"] = _mm_rms_bwd(dup0, W["w_up"], sv["x2"], V["ffn_norm_g"][l], dx, n + "d_hf", a_halves=True)
    gw["w_up"] = sides.mm("dw_up", sv["hf"], dup0, "tn", F32, n + "dw_up", twin=BF, b_halves=True)
    return dx2, dx2b, gw, sm


def _layer_bwd_mix(dx2, dx2b, dmem_n, sv, mem_n, W, V, l, dims, sides, gw):
    Bn, S, Mn, D, C, F = dims
    n = f"l{l}_b_"
    sm = {}
    datt = sides.mm("d_att", dx2b, W["w_o"], "nt", BF, n + "d_att", bl=0)
    gw["w_o"] = _mm(sv["att"], dx2b, "tn", F32, n + "dw_o", twin=BF)
    dq, dk, dv = _attn_bwd(sv["q"], sv["kv"], datt, Bn, S, Mn, D, n + "attn")
    dkv = jnp.concatenate([dk, dv], axis=1)
    gw["w_kv"] = _mm(mem_n, dkv, "tn", F32, n + "dw_kv", twin=BF)
    dmem_n = _mm(dkv, W["w_kv"], "nt", F32, n + "d_mem", res=dmem_n, bl=0)
    dx1, dx1b, sm["xattn_norm_g"] = _mm_rms_bwd(dq, W["w_q"], sv["x1"], V["xattn_norm_g"][l], dx2, n + "d_hq")
    gw["w_q"] = _mm(sv["hq"], dq, "tn", F32, n + "dw_q", twin=BF)
    dmerged = sides.mm("d_merged", dx1b, W["w_out"], "nt", BF, n + "d_merged", bl=0)
    gw["w_out"] = _mm(sv["merged"], dx1b, "tn", F32, n + "dw_out", twin=BF)
    dgc, dgp, dyc, dyp, sm["pool_scale"] = sides.run("merge_b", _merge_bwd, sv["proj"], sv["yc"], sv["yp"], V["pool_scale"][l], dmerged, C, n + "merge")
    du, dwp = _pool_bwd(sv["proj"], W["w_pool"], dyp, 0, Bn, S, C, D, n + "pool")
    gw["w_pool"] = (dwp, dwp.astype(BF))
    dyc1 = _mm(dyc, W["w_conv_out"], "nt", F32, n + "d_yc1", bl=0)
    gw["w_conv_out"] = _mm(sv["yc1"], dyc, "tn", F32, n + "dw_conv_out", twin=BF)
    dcv, sm["conv_ln_g"], sm["conv_ln_b"] = sides.run("ln_silu_b", _ln_silu_bwd, sv["cv"], V["conv_ln_g"][l], V["conv_ln_b"][l], dyc1, n + "ln_silu")
    da, dgl, sm["conv_dw_w"], sm["conv_dw_b"] = sides.run("glu_conv_b", _glu_conv_bwd, sv["proj"], V["conv_dw_w"][l], dcv, Bn, S, C, n + "glu_conv")
    dproj = jnp.concatenate([da, dgl, du, dgc, dgp], axis=1)
    dx, dxb, sm["mix_norm_g"] = sides.run("d_h", _mm_rms_bwd, dproj, W["w_in"], sv["x"], V["mix_norm_g"][l], dx1, n + "d_h")
    gw["w_in"] = sides.mm("dw_in", sv["h"], dproj, "tn", F32, n + "dw_in", twin=BF)
    return dx, dxb, dmem_n, sm


BIG = (("w_in", "col"), ("w_conv_out", "col"), ("w_pool", "row"), ("w_out", "row"), ("w_q", "row"), ("w_kv", "col"),
       ("w_o", "row"), ("w_up", "col"), ("w_down", "row"))
ALL_RELS = (1, 2, 3)
GATHER_FIRST = ("w_in", "w_conv_out", "w_pool", "w_out", "w_q", "w_o")
FWD_CARRY = {
    (0, "proj"): (("w_up", 0, (1, 2)),),
    (0, "glu_conv"): (("w_up", 0, (3,)),),
    (0, "merge"): (("w_kv", 0, ALL_RELS),),
    (0, "attn"): (("w_down", 0, ALL_RELS),),
    (0, "up_proj"): (("w_in", 1, ALL_RELS), ("w_conv_out", 1, ALL_RELS), ("w_pool", 1, ALL_RELS)),
    (0, "ffn_act"): (("w_out", 1, ALL_RELS), ("w_q", 1, ALL_RELS), ("w_kv", 1, ALL_RELS), ("w_o", 1, ALL_RELS)),
    (1, "proj"): (("w_up", 1, (1, 2)),),
    (1, "glu_conv"): (("w_up", 1, (3,)),),
    (1, "merge"): (("w_down", 1, (1, 2)),),
    (1, "attn"): (("w_down", 1, (3,)),),
}
PASS_CARRY = {
    (0, "out_proj"): (("w_kv", 0),),
    (0, "o_proj"): (("w_up", 0), ("w_down", 0)),
    (0, "down_proj"): (("w_in", 1), ("w_conv_out", 1), ("w_pool", 1), ("w_out", 1), ("w_q", 1), ("w_kv", 1), ("w_o", 1)),
    (1, "o_proj"): (("w_up", 1), ("w_down", 1)),
}
EARLY = ("w_down", "w_up")
BWD_CARRY_EARLY = {"merge_b": ("w_down",), "glu_conv_b": ("w_up",)}
BWD_CARRY_LATE = {"ffn_act_b": ("w_in", "w_conv_out", "w_pool", "w_out", "w_q", "w_kv", "w_o")}
BWD_LAST_LAYER = (("att", ("w_o", "w_kv", "w_q"), "d_merged", {"d_h": ("w_o", "w_kv", "w_q")}),
                  ("tok", ("w_out", "w_pool", "w_conv_out"), "ln_silu_b", {"dw_in": ("w_out", "w_pool", "w_conv_out")}))


def _place():
    xi, yi, ci = lax.axis_index("x"), lax.axis_index("y"), lax.axis_index("c")
    return xi, yi, ci, 2 * xi + yi


def _chip_peer(xi, yi, ci, r):
    return (xi ^ (r >> 1), yi ^ (r & 1), ci)


def _full_shard(ref, kind, k, cs):
    if kind == "col":
        return ref.at[:, :, :, :, pl.ds(pl.multiple_of(k * cs, cs), cs)]
    return ref.at[:, :, k]


def _gather_weights(shards, kinds):
    n = len(shards)
    outs = []
    for s, kind in zip(shards, kinds):
        L, P, _, RH, CS = s.shape
        outs.append(SDS((L, P, 2, RH, CS * N_CHIPS) if kind == "col" else (L, P, N_CHIPS, 2, RH, CS), s.dtype))
    per = 7

    def body(*refs):
        srcs, fulls, (ssem, rsem) = refs[:n], refs[n:2 * n], refs[2 * n:]
        xi, yi, ci, j = _place()
        sib = (xi, yi, 1 - ci)

        def piece(i, k, c):
            kind, cs = kinds[i], shards[i].shape[-1]
            if kind == "col":
                return fulls[i].at[:, :, c, :, pl.ds(pl.multiple_of(k * cs, cs), cs)]
            return fulls[i].at[:, :, k, c]

        def copy(i, slot, src, dst, dev):
            return pltpu.make_async_remote_copy(src_ref=src, dst_ref=dst, send_sem=ssem.at[per * i + slot], recv_sem=rsem.at[per * i + slot],
                                                device_id=dev, device_id_type=MESH)

        own, first, passed = [], [], []
        for i in range(n):
            for r in (1, 2, 3):
                first.append(copy(i, r - 1, srcs[i].at[:, :, ci], piece(i, j, ci), _chip_peer(xi, yi, ci, r)))
                first[-1].start()
        for i in range(n):
            own.append(copy(i, 6, srcs[i], _full_shard(fulls[i], kinds[i], j, shards[i].shape[-1]), sib))
            own[-1].start()
        for i in range(n):
            for r in (1, 2, 3):
                got = piece(i, j ^ r, ci)
                copy(i, r - 1, got, got, sib).wait_recv()
                passed.append(copy(i, 2 + r, got, got, sib))
                passed[-1].start()
        for i in range(n):
            for r in (1, 2, 3):
                got = piece(i, j ^ r, 1 - ci)
                copy(i, 2 + r, got, got, sib).wait_recv()
        for cp in own:
            cp.wait()
        for cp in first + passed:
            cp.wait_send()

    return pl.pallas_call(
        body, in_specs=[ANY] * n, out_specs=[ANY] * n, out_shape=outs,
        scratch_shapes=[pltpu.SemaphoreType.DMA((per * n,)), pltpu.SemaphoreType.DMA((per * n,))], name="gather_weights")(*shards)


def _full_sds(s, kind):
    L, P, _, RH, CS = s.shape
    return SDS((L, P, 2, RH, CS * N_CHIPS) if kind == "col" else (L, P, N_CHIPS, 2, RH, CS), s.dtype)


def _gather_piece(full, kind, cs, k, c):
    if kind == "col":
        return full.at[:, :, c, :, pl.ds(pl.multiple_of(k * cs, cs), cs)]
    return full.at[:, :, k, c]


def _side_gather(shards, kinds, rels, fulls):
    n = len(shards)

    def make(srcs, outs, ssem, rsem):
        xi, yi, ci, j = _place()
        return [pltpu.make_async_remote_copy(
            src_ref=srcs[i].at[:, :, ci], dst_ref=_gather_piece(outs[i], kinds[i], shards[i].shape[-1], j, ci), send_sem=ssem.at[3 * i + r - 1],
            recv_sem=rsem.at[3 * i + r - 1], device_id=_chip_peer(xi, yi, ci, r), device_id_type=MESH) for i in range(n) for r in rels[i]]

    prior = [f for f in fulls if f is not None]
    assert len(prior) in (0, n)
    return _Side(list(shards) + prior, [_full_sds(s, k) for s, k in zip(shards, kinds)], 3 * n, make, n_alias=len(prior))


def _side_gather_pass(fulls, shards, kinds):
    n = len(fulls)

    def make(srcs, outs, ssem, rsem):
        xi, yi, ci, j = _place()
        sib = (xi, yi, 1 - ci)
        cps = []
        for i in range(n):
            cs = shards[i].shape[-1]
            for r in (1, 2, 3):
                got = _gather_piece(outs[i], kinds[i], cs, j ^ r, ci)
                cps.append(pltpu.make_async_remote_copy(src_ref=got, dst_ref=got, send_sem=ssem.at[4 * i + r - 1], recv_sem=rsem.at[4 * i + r - 1],
                                                        device_id=sib, device_id_type=MESH))
            cps.append(pltpu.make_async_remote_copy(src_ref=srcs[i], dst_ref=_full_shard(outs[i], kinds[i], j, cs), send_sem=ssem.at[4 * i + 3],
                                                    recv_sem=rsem.at[4 * i + 3], device_id=sib, device_id_type=MESH))
        return cps

    return _Side(list(shards) + list(fulls), [SDS(f.shape, f.dtype) for f in fulls], 4 * n, make, n_alias=n)


def _sibling_exchange(gviews, kinds, name):
    n = len(gviews)
    outs = [SDS(g.shape[:1] + g.shape[2:] if kind == "col" else g.shape[:2] + g.shape[3:], g.dtype) for g, kind in zip(gviews, kinds)]

    def body(*refs):
        gs, lands, (ssem, rsem) = refs[:n], refs[n:2 * n], refs[2 * n:]
        xi, yi, ci, _ = _place()
        cps = []
        for i in range(n):
            src = gs[i].at[:, 1 - ci] if kinds[i] == "col" else gs[i].at[:, :, 1 - ci]
            cps.append(pltpu.make_async_remote_copy(src_ref=src, dst_ref=lands[i], send_sem=ssem.at[i], recv_sem=rsem.at[i],
                                                    device_id=(xi, yi, 1 - ci), device_id_type=MESH))
            cps[-1].start()
        for cp in cps:
            cp.wait()

    return pl.pallas_call(body, in_specs=[ANY] * n, out_specs=[ANY] * n, out_shape=outs,
                          scratch_shapes=[pltpu.SemaphoreType.DMA((n,)), pltpu.SemaphoreType.DMA((n,))], name=name)(*gviews)


def _side_sibling_exchange(gviews, kinds):
    outs = [SDS(g.shape[:1] + g.shape[2:] if kind == "col" else g.shape[:2] + g.shape[3:], g.dtype) for g, kind in zip(gviews, kinds)]

    def make(gs, lands, ssem, rsem):
        xi, yi, ci, _ = _place()
        return [pltpu.make_async_remote_copy(src_ref=gs[i].at[:, 1 - ci] if kinds[i] == "col" else gs[i].at[:, :, 1 - ci], dst_ref=lands[i],
                                             send_sem=ssem.at[i], recv_sem=rsem.at[i], device_id=(xi, yi, 1 - ci), device_id_type=MESH)
                for i in range(len(gs))]

    return _Side(gviews, outs, len(gviews), make)


def _chip_sums(gs, lands, kinds, jc, name):
    n = len(gs)
    args, in_specs, out_specs, out_shape = [], [], [], []
    for g, land, kind in zip(gs, lands, kinds):
        if kind == "col":
            P, _, RH, C = g.shape
            CS = C // N_CHIPS
            in_specs += [pl.BlockSpec((P, None, RH, CS), lambda r, jc: (0, jc[1], 0, jc[0] ^ r)),
                         pl.BlockSpec((P, RH, CS), lambda r, jc: (0, 0, jc[0] ^ r))]
        else:
            P, _, _, RH, CS = g.shape
            in_specs += [pl.BlockSpec((P, None, None, RH, CS), lambda r, jc: (0, jc[0] ^ r, jc[1], 0, 0)),
                         pl.BlockSpec((P, None, RH, CS), lambda r, jc: (0, jc[0] ^ r, 0, 0))]
        args += [g, land]
        out_specs += [pl.BlockSpec((P, RH, CS), lambda r, jc: (0, 0, 0)), pl.BlockSpec((None, P, RH, CS), lambda r, jc: (r, 0, 0, 0))]
        out_shape += [SDS((P, RH, CS), F32), SDS((N_CHIPS, P, RH, CS), BF)]

    def body(jc_ref, *refs):
        ins, outs = refs[:2 * n], refs[2 * n:]
        for i in range(n):
            s = ins[2 * i][...] + ins[2 * i + 1][...].astype(F32)
            outs[2 * i + 1][...] = s.astype(BF)

            @pl.when(pl.program_id(0) == 0)
            def _():
                outs[2 * i][...] = s

    outs = _call(body, args, grid=(N_CHIPS,), in_specs=in_specs, out_specs=out_specs, out_shape=out_shape, semantics=("arbitrary",),
                 name=name, prefetch=(jc,))
    return outs[0::2], outs[1::2]


def _chip_exchange_copies(srcs, lands, ssem, rsem):
    xi, yi, ci, _ = _place()
    return [pltpu.make_async_remote_copy(src_ref=srcs[i].at[r], dst_ref=lands[i].at[r], send_sem=ssem.at[3 * i + r - 1],
                                         recv_sem=rsem.at[3 * i + r - 1], device_id=_chip_peer(xi, yi, ci, r), device_id_type=MESH)
            for i in range(len(srcs)) for r in (1, 2, 3)]


def _side_chip_exchange(pieces):
    return _Side(pieces, [SDS(p.shape, p.dtype) for p in pieces], 3 * len(pieces), _chip_exchange_copies)


FINAL_SUM_STEPS = 2


def _final_sums(owns, lands, jc, shards, l, L, name, side=None):
    n = len(owns)
    args, in_specs, out_specs, out_shape = [], [], [], []
    for own, land in zip(owns, lands):
        P, RH, CS = own.shape
        hr = RH // FINAL_SUM_STEPS
        in_specs += [pl.BlockSpec((P, hr, CS), lambda h, jc: (0, h, 0))]
        in_specs += [pl.BlockSpec((None, P, hr, CS), functools.partial(lambda r, h, jc: (r, 0, h, 0), r)) for r in (1, 2, 3)]
        args += [own, land, land, land]
        out_specs.append(pl.BlockSpec((None, P, None, hr, CS), lambda h, jc: (l, 0, jc[1], h, 0)))
        out_shape.append(SDS((L, P, 2, RH, CS), F32))
    aliases = None
    if shards is not None:
        aliases = {4 * n + i: i for i in range(n)}
        in_specs += [ANY] * n
        args += list(shards)

    def body(jc_ref, *refs):
        outs = refs[len(args):]
        for i in range(n):
            o, a, b, c = (refs[4 * i + t][...] for t in range(4))
            outs[i][...] = ((o + a.astype(F32)) + b.astype(F32)) + c.astype(F32)

    return _call(body, args, grid=(FINAL_SUM_STEPS,), in_specs=in_specs, out_specs=out_specs, out_shape=out_shape, semantics=("arbitrary",),
                 name=name, prefetch=(jc,), aliases=aliases, side=side)


def _halves_exchange(shards, l, name):
    n = len(shards)

    def body(*refs):
        outs, (ssem, rsem) = refs[n:2 * n], refs[2 * n:]
        xi, yi, ci, _ = _place()
        cps = []
        for i in range(n):
            mine = outs[i].at[l, :, ci]
            cps.append(pltpu.make_async_remote_copy(src_ref=mine, dst_ref=mine, send_sem=ssem.at[i], recv_sem=rsem.at[i],
                                                    device_id=(xi, yi, 1 - ci), device_id_type=MESH))
            cps[-1].start()
        for i in range(n):
            land = outs[i].at[l, :, 1 - ci]
            pltpu.make_async_remote_copy(src_ref=land, dst_ref=land, send_sem=ssem.at[i], recv_sem=rsem.at[i],
                                         device_id=(xi, yi, 1 - ci), device_id_type=MESH).wait_recv()
        for cp in cps:
            cp.wait_send()

    return pl.pallas_call(body, in_specs=[ANY] * n, out_specs=[ANY] * n, out_shape=[SDS(s.shape, s.dtype) for s in shards],
                          input_output_aliases={i: i for i in range(n)},
                          scratch_shapes=[pltpu.SemaphoreType.DMA((n,)), pltpu.SemaphoreType.DMA((n,))], name=name)(*shards)


def _reduce_small(part, pieces):
    NR, Wd = part.shape
    ND = 2 * N_CHIPS
    n = len(pieces)

    def body(p_ref, *refs):
        srcs, o_ref, lands, (land, ssem, rsem, xs, xr) = refs[:n], refs[n], refs[n + 1:2 * n + 1], refs[2 * n + 1:]
        exchange = _chip_exchange_copies(srcs, lands, xs, xr)
        for cp in exchange:
            cp.start()
        xi, yi, ci, j = _place()
        me = 2 * j + ci
        land[me] = p_ref[...]
        cps = []
        for rr in range(1, ND):
            dev = (xi ^ (rr >> 2), yi ^ ((rr >> 1) & 1), ci ^ (rr & 1))
            cps.append(pltpu.make_async_remote_copy(src_ref=p_ref, dst_ref=land.at[me], send_sem=ssem.at[rr - 1], recv_sem=rsem.at[rr - 1],
                                                    device_id=dev, device_id_type=MESH))
            cps[-1].start()
        for rr in range(1, ND):
            got = land.at[me ^ rr]
            pltpu.make_async_remote_copy(src_ref=got, dst_ref=got, send_sem=ssem.at[rr - 1], recv_sem=rsem.at[rr - 1],
                                         device_id=(xi, yi, ci), device_id_type=MESH).wait_recv()
        acc = land[0]
        for d in range(1, ND):
            acc = acc + land[d]
        o_ref[...] = acc
        for cp in cps:
            cp.wait_send()
        for cp in exchange:
            cp.wait()

    vm = pl.BlockSpec(memory_space=pltpu.VMEM)
    outs = pl.pallas_call(
        body, in_specs=[vm] + [ANY] * n, out_specs=[vm] + [ANY] * n, out_shape=[SDS((NR, Wd), F32)] + [SDS(p.shape, p.dtype) for p in pieces],
        scratch_shapes=[pltpu.VMEM((ND, NR, Wd), F32), pltpu.SemaphoreType.DMA((ND - 1,)), pltpu.SemaphoreType.DMA((ND - 1,)),
                        pltpu.SemaphoreType.DMA((3 * n,)), pltpu.SemaphoreType.DMA((3 * n,))],
        name="small_grad_allreduce")(part, *pieces)
    return outs[0], list(outs[1:])


def _adamw_update(w_ref, g_ref, m_ref, v_ref, d_ref, mo_ref, vo_ref):
    g = g_ref[...]
    m = ADAM_B1 * m_ref[...] + (1.0 - ADAM_B1) * g
    v = ADAM_B2 * v_ref[...] + (1.0 - ADAM_B2) * jnp.square(g)
    m_hat = m / (1.0 - ADAM_B1 ** ADAM_STEP)
    v_hat = v / (1.0 - ADAM_B2 ** ADAM_STEP)
    d_ref[...] = -ADAM_LR * (m_hat / (jnp.sqrt(v_hat) + ADAM_EPS) + ADAM_WD * w_ref[...])
    mo_ref[...] = m
    vo_ref[...] = v


ADAMW_STEPS = 8


def _adamw_layer(ws, gs, ms, vs, prev, l, name, side=None):
    n = len(ws)
    args, in_specs, out_specs, out_shape = [], [], [], []
    for w, g, m, v in zip(ws, gs, ms, vs):
        L, R, C = w.shape
        blk = pl.BlockSpec((None, R // ADAMW_STEPS, C), lambda i: (l, i, 0))
        in_specs += [blk] * 4
        args += [w, g, m, v]
        out_specs += [blk] * 3
        out_shape += [SDS((L, R, C), F32)] * 3
    aliases = None
    if prev is not None:
        aliases = {4 * n + i: i for i in range(3 * n)}
        in_specs += [ANY] * (3 * n)
        args += list(prev)

    def body(*refs):
        outs = refs[len(args):]
        for i in range(n):
            _adamw_update(*refs[4 * i:4 * i + 4], *outs[3 * i:3 * i + 3])

    return _call(body, args, grid=(ADAMW_STEPS,), in_specs=in_specs, out_specs=out_specs, out_shape=out_shape, semantics=("parallel",),
                 name=name, aliases=aliases, side=side)


def _adamw(w, g, m, v, name):
    shape = w.shape
    C = shape[-1]
    R = w.size // C
    tb = _tile(R, max(8, (1 << 18) // C), 8)
    body = functools.partial(_adamw_update)
    blk = pl.BlockSpec((tb, C), lambda i: (i, 0))
    outs = pl.pallas_call(body, grid=(R // tb,), in_specs=[blk] * 4, out_specs=[blk] * 3, out_shape=[SDS((R, C), F32)] * 3,
                          compiler_params=_params("parallel"), name=name)(*[t.reshape(R, C) for t in (w, g, m, v)])
    return [t.reshape(shape) for t in outs]


WEIGHTS = ("mix_norm_g", "w_in", "conv_dw_w", "conv_dw_b", "conv_ln_g", "conv_ln_b", "w_conv_out", "w_pool_grp", "pool_scale", "w_out",
           "xattn_norm_g", "mem_norm_g", "w_q", "w_kv", "w_o", "ffn_norm_g", "w_up", "ffn_dw_w", "w_down", "final_norm_g")
VECTORS = ("mix_norm_g", "conv_dw_b", "conv_ln_g", "conv_ln_b", "pool_scale", "xattn_norm_g", "mem_norm_g", "ffn_norm_g", "final_norm_g")


def _shard_view(t, kind):
    L, P, R, C = t.shape
    return t.reshape(L, P, 2, R // 2, C)


def _rows(t, width):
    return t.reshape(-1, width)


def _pack(parts):
    return jnp.concatenate([jnp.pad(p, ((0, (-p.shape[0]) % 8), (0, 0))) for p in parts], axis=0)


def kernel(x, mem, mix_norm_g, w_in, conv_dw_w, conv_dw_b, conv_ln_g, conv_ln_b, w_conv_out, w_pool_grp, pool_scale, w_out, xattn_norm_g, mem_norm_g, w_q, w_kv, w_o, ffn_norm_g, w_up, ffn_dw_w, w_down, final_norm_g, loss_target, m_mix_norm_g, m_w_in, m_conv_dw_w, m_conv_dw_b, m_conv_ln_g, m_conv_ln_b, m_w_conv_out, m_w_pool_grp, m_pool_scale, m_w_out, m_xattn_norm_g, m_mem_norm_g, m_w_q, m_w_kv, m_w_o, m_ffn_norm_g, m_w_up, m_ffn_dw_w, m_w_down, m_final_norm_g, v_mix_norm_g, v_w_in, v_conv_dw_w, v_conv_dw_b, v_conv_ln_g, v_conv_ln_b, v_w_conv_out, v_w_pool_grp, v_pool_scale, v_w_out, v_xattn_norm_g, v_mem_norm_g, v_w_q, v_w_kv, v_w_o, v_ffn_norm_g, v_w_up, v_ffn_dw_w, v_w_down, v_final_norm_g):
    w = dict(mix_norm_g=mix_norm_g, w_in=w_in, conv_dw_w=conv_dw_w, conv_dw_b=conv_dw_b, conv_ln_g=conv_ln_g, conv_ln_b=conv_ln_b,
             w_conv_out=w_conv_out, w_pool_grp=w_pool_grp, pool_scale=pool_scale, w_out=w_out, xattn_norm_g=xattn_norm_g,
             mem_norm_g=mem_norm_g, w_q=w_q, w_kv=w_kv, w_o=w_o, ffn_norm_g=ffn_norm_g, w_up=w_up, ffn_dw_w=ffn_dw_w, w_down=w_down,
             final_norm_g=final_norm_g)
    m = dict(zip(WEIGHTS, (m_mix_norm_g, m_w_in, m_conv_dw_w, m_conv_dw_b, m_conv_ln_g, m_conv_ln_b, m_w_conv_out, m_w_pool_grp, m_pool_scale,
                           m_w_out, m_xattn_norm_g, m_mem_norm_g, m_w_q, m_w_kv, m_w_o, m_ffn_norm_g, m_w_up, m_ffn_dw_w, m_w_down, m_final_norm_g)))
    v = dict(zip(WEIGHTS, (v_mix_norm_g, v_w_in, v_conv_dw_w, v_conv_dw_b, v_conv_ln_g, v_conv_ln_b, v_w_conv_out, v_w_pool_grp, v_pool_scale,
                           v_w_out, v_xattn_norm_g, v_mem_norm_g, v_w_q, v_w_kv, v_w_o, v_ffn_norm_g, v_w_up, v_ffn_dw_w, v_w_down, v_final_norm_g)))
    xi, yi, ci, j = _place()
    jc = jnp.stack([j, ci]).astype(jnp.int32)
    L = w_in.shape[0]
    G = len(POOL_WINDOWS)
    kinds = dict(BIG)

    def to_mat(name, t):
        if name == "w_pool":
            return jnp.swapaxes(t, 2, 3)
        return t[:, None]

    def from_mat(name, t):
        if name == "w_pool":
            return jnp.swapaxes(t, 2, 3)
        return t[:, 0]

    src = {name: w["w_pool_grp" if name == "w_pool" else name] for name, _ in BIG}

    KC, cs_c = conv_dw_w.shape[1], conv_dw_w.shape[2]
    KF, cs_f = ffn_dw_w.shape[1], ffn_dw_w.shape[2]
    taps = jnp.concatenate([conv_dw_w.reshape(L * KC, cs_c), ffn_dw_w.reshape(L * KF * (cs_f // cs_c), cs_c)], axis=0)
    n_taps = taps.shape[0]
    taps = jnp.pad(taps, ((0, (-n_taps) % 16), (0, 0)))
    names = [name for name, _ in BIG]
    mats = {name: to_mat(name, src[name]).astype(BF) for name in names}

    def layer_shards(l, subset):
        return [_shard_view(mats[name][l:l + 1], kinds[name]) for name in subset]

    def as_weight(name, f):
        return f.reshape(G if name == "w_pool" else 1, -1, f.shape[-1])

    assert L == 2
    fulls = _gather_weights(layer_shards(0, GATHER_FIRST) + [_shard_view(taps[None, None], "row")], [kinds[name] for name in GATHER_FIRST] + ["row"])
    ready = {(name, 0): as_weight(name, f) for name, f in zip(GATHER_FIRST, fulls)}
    landing = {}
    taps_all = fulls[-1].reshape(N_CHIPS, -1, cs_c)[:, :n_taps]
    V = {name: w[name] for name in VECTORS}
    V["conv_dw_w"] = taps_all[:, :L * KC].reshape(N_CHIPS, L, KC, cs_c).transpose(1, 2, 0, 3).reshape(L, KC, N_CHIPS * cs_c)
    V["ffn_dw_w"] = taps_all[:, L * KC:].reshape(N_CHIPS, L, KF, cs_f).transpose(1, 2, 0, 3).reshape(L, KF, N_CHIPS * cs_f)

    Bn, S, D = x.shape
    Mn = mem.shape[1]
    dims = (Bn, S, Mn, D, conv_dw_b.shape[1], w_down.shape[1] * N_CHIPS)
    xt = x.reshape(Bn * S, D)
    memf = mem.reshape(Bn * Mn, D)
    mem_n = _rms_fwd(memf, V["mem_norm_g"], "mem_norm")

    class LayerWeights:
        def __init__(self, l):
            self.l = l

        def __getitem__(self, name):
            return ready[(name, self.l)]

    def carried_gather(entries):
        return lambda: _side_gather([layer_shards(lw, [nm])[0] for nm, lw, _ in entries], [kinds[nm] for nm, _, _ in entries],
                                    [rels for _, _, rels in entries], [landing.get((nm, lw)) for nm, lw, _ in entries])

    def carried_pass(group):
        return lambda: _side_gather_pass([landing.pop(t) for t in group], [layer_shards(lw, [nm])[0] for nm, lw in group],
                                         [kinds[nm] for nm, _ in group])

    def on_land(l):
        def handle(key, fulls):
            if (l, key) in FWD_CARRY:
                landing.update({(nm, lw): f for (nm, lw, _), f in zip(FWD_CARRY[(l, key)], fulls)})
            else:
                ready.update({t: as_weight(t[0], f) for t, f in zip(PASS_CARRY[(l, key)], fulls)})
        return handle

    saved, W = [], []
    ht = _rms_fwd(xt, V["mix_norm_g"][0], "l0_mix_norm")
    for l in range(L):
        by_key = {key: carried_gather(entries) for (cl, key), entries in FWD_CARRY.items() if cl == l}
        by_key.update({key: carried_pass(group) for (cl, key), group in PASS_CARRY.items() if cl == l})
        sides = _Sides(by_key, on_land=on_land(l))
        W.append(LayerWeights(l))
        xt, ht, sv = _layer_fwd(xt, ht, mem_n, W[l], V, l, dims, sides, V["mix_norm_g"][l + 1] if l + 1 < L else None)
        saved.append(sv)
    loss, dx, dgf = _loss_bwd(xt, V["final_norm_g"], loss_target.reshape(Bn * S, D), "loss")
    loss = lax.psum(loss[0, 0], ("x", "y", "c"))

    late_names = [name for name in names if name not in EARLY]

    def views(gw, subset, twin):
        out = []
        for name in subset:
            g = gw[name][twin] if gw[name][twin].ndim == 3 else gw[name][twin][None]
            P, R, C = g.shape
            out.append(g.reshape(P, 2, R // 2, C) if kinds[name] == "col" else g.reshape(P, N_CHIPS, 2, R // (2 * N_CHIPS), C))
        return out

    def group_kinds(subset):
        return [kinds[name] for name in subset]

    class Reduction:
        def __init__(self, gw, subset, l, tag, first, table):
            self.gw, self.subset, self.l, self.tag, self.first, self.table = gw, subset, l, tag, first, table

        def sides(self):
            by_key = {self.first: lambda: _side_sibling_exchange(views(self.gw, self.subset, 1), group_kinds(self.subset))}
            by_key.update({key: (lambda names_=names_: _side_chip_exchange([self.pieces[nm] for nm in names_])) for key, names_ in self.table.items()})
            return by_key

        def on_land(self, key, landed):
            if key == self.first:
                self.sums(landed)
            elif key in self.table:
                got[self.l].update(zip(self.table[key], landed))

        def sums(self, lands):
            own, pieces = _chip_sums(views(self.gw, self.subset, 0), lands, group_kinds(self.subset), jc, f"chip_sums_{self.tag}_l{self.l}")
            owns[self.l].update(zip(self.subset, own))
            self.pieces = dict(zip(self.subset, pieces))

    def riding(reductions):
        return _Sides({key: side for r in reductions for key, side in r.sides().items()},
                      on_land=lambda key, landed: [r.on_land(key, landed) for r in reductions])

    dxb, dmem_n = dx, None
    smalls, owns, got = [None] * L, [{} for _ in range(L)], [{} for _ in range(L)]
    late = None
    for l in reversed(range(L)):
        dx, dxb, gw, sm = _layer_bwd_mlp(dx, dxb, saved[l], W[l], V, l, dims, riding([late] if late is not None else []))
        gw_mix = {}
        reductions = [Reduction(gw, EARLY, l, "mlp", "d_att", BWD_CARRY_EARLY)]
        if l == 0:
            reductions += [Reduction(gw_mix, names_, 0, tag, first, table) for tag, names_, first, table in BWD_LAST_LAYER]
        dx, dxb, dmem_n, sm2 = _layer_bwd_mix(dx, dxb, dmem_n, saved[l], mem_n, W[l], V, l, dims, riding(reductions), gw_mix)
        smalls[l] = {**sm, **sm2}
        late = Reduction(gw_mix, late_names, l, "mix", "d_gact", BWD_CARRY_LATE) if l > 0 else None
    last = Reduction(gw_mix, ("w_in",), 0, "in", None, {})
    last.sums(_sibling_exchange(views(gw_mix, last.subset, 1), group_kinds(last.subset), "grad_sibling_exchange_in_l0"))
    grad_x = dx.reshape(Bn, S, D)
    _, _, dgm = _rms_bwd(memf, V["mem_norm_g"], dmem_n, None, "mem_norm_b")
    small = {k: jnp.stack([sm[k] for sm in smalls]) if k in ("conv_dw_w", "ffn_dw_w") else jnp.concatenate([sm[k] for sm in smalls], axis=0)
             for k in smalls[0]}
    small["mem_norm_g"] = dgm
    small["final_norm_g"] = dgf

    small_w = conv_dw_b.shape[1]
    order = VECTORS + ("conv_dw_w", "ffn_dw_w")
    parts = [_rows(small[name], small_w) for name in order]
    counts = [p.shape[0] for p in parts]
    summed, landed_last = _reduce_small(_pack(parts), [last.pieces[name] for name in last.subset])
    got[0].update(zip(last.subset, landed_last))

    keys = ["w_pool_grp" if name == "w_pool" else name for name in names]
    rows3 = lambda t: t.reshape(t.shape[0], -1, t.shape[-1])
    wmv = [[rows3(to_mat(name, d[key])) for name, key in zip(names, keys)] for d in (w, m, v)]
    gshards, updates = None, None
    for l in reversed(range(L)):
        gshards = _final_sums([owns[l][name] for name in names], [got[l][name] for name in names], jc, gshards, l, L, f"final_sums_l{l}")
        gshards = _halves_exchange(gshards, l, f"grad_halves_exchange_l{l}")
        updates = _adamw_layer(wmv[0], [rows3(t) for t in gshards], wmv[1], wmv[2], updates, l, f"adamw_l{l}")
    grads, delta, new_m, new_v = {}, {}, {}, {}
    for i, (name, key) in enumerate(zip(names, keys)):
        Lg, P, _, RH, CS = gshards[i].shape
        grads[key] = from_mat(name, gshards[i].reshape(Lg, P, 2 * RH, CS))
        for d, t in zip((delta, new_m, new_v), updates[3 * i:3 * i + 3]):
            d[key] = from_mat(name, t.reshape(Lg, P, 2 * RH, CS))

    off = 0
    for name, cnt in zip(order, counts):
        t = summed[off:off + cnt]
        off += cnt + (-cnt) % 8
        if name in VECTORS:
            grads[name] = t.reshape(w[name].shape)
        else:
            full = t.reshape(small[name].shape)
            cs = w[name].shape[2]
            grads[name] = lax.dynamic_slice_in_dim(full, j * cs, cs, axis=2)

    vec =[_pack([_rows(d[name], small_w) for name in VECTORS]) for d in (w, grads, m, v)]
    outs = _adamw(*vec, "adamw_vectors")
    off = 0
    for name in VECTORS:
        cnt = w[name].size // small_w
        for d, t in zip((delta, new_m, new_v), outs):
            d[name] = t[off:off + cnt].reshape(w[name].shape)
        off += cnt + (-cnt) % 8
    for name in ("conv_dw_w", "ffn_dw_w"):
        delta[name], new_m[name], new_v[name] = _adamw(w[name], grads[name], m[name], v[name], "adamw_" + name)

    return (loss, grad_x, *[grads[k] for k in WEIGHTS], *[delta[k] for k in WEIGHTS], *[new_m[k] for k in WEIGHTS], *[new_v[k] for k in WEIGHTS])
```

```python
import functools
import math

import jax
import jax.numpy as jnp
from jax import lax
from jax.experimental import pallas as pl
from jax.experimental.pallas import tpu as pltpu

F32 = jnp.float32
BF = jnp.bfloat16
SDS = jax.ShapeDtypeStruct
MESH = pl.DeviceIdType.MESH
ANY = pl.BlockSpec(memory_space=pl.ANY)

EPS = 1e-6
XA_HEADS = 4
POOL_WINDOWS = (2, 4, 8, 16)
N_CHIPS = 4
ADAM_LR, ADAM_B1, ADAM_B2, ADAM_EPS, ADAM_WD, ADAM_STEP = 0.001, 0.9, 0.999, 1e-08, 0.01, 10

LANES = 128
ROW_BLOCK = 512
VMEM_LIMIT = 56 * 1024 * 1024


def _params(*sem):
    return pltpu.CompilerParams(dimension_semantics=sem if sem else None, vmem_limit_bytes=VMEM_LIMIT)


def _tile(n, cap, mult=LANES):
    if n <= cap:
        return n
    for t in range(cap - cap % mult, 0, -mult):
        if n % t == 0:
            return t
    return n


_DN = {"nn": (((1,), (0,)), ((), ())), "nt": (((1,), (1,)), ((), ())), "tn": (((0,), (0,)), ((), ()))}


class _Side:
    def __init__(self, ins, outs, n, make, n_alias=0):
        self.ins, self.outs, self.n, self.make, self.n_alias = list(ins), list(outs), n, make, n_alias


def _call(body, args, *, grid, in_specs, out_specs, out_shape, semantics, name, scratch_shapes=(), side=None, prefetch=(), aliases=None):
    n_pf = len(prefetch)
    aliases = {n_pf + i: o for i, o in (aliases or {}).items()}
    n_in, n_out, n_scr = len(args), len(out_shape), len(scratch_shapes)
    n_si, n_so = (len(side.ins), len(side.outs)) if side is not None else (0, 0)
    if side is not None:
        aliases.update({n_pf + n_in + n_si - side.n_alias + i: n_out + i for i in range(side.n_alias)})

    def carrying(*refs):
        pf, refs = refs[:n_pf], refs[n_pf:]
        ins, s_in = refs[:n_in], refs[n_in:n_in + n_si]
        outs, s_out = refs[n_in + n_si:n_in + n_si + n_out], refs[n_in + n_si + n_out:n_in + n_si + n_out + n_so]
        scr = refs[n_in + n_si + n_out + n_so:]
        if side is None:
            return body(*pf, *ins, *outs, *scr)
        copies = side.make(s_in, s_out, scr[n_scr], scr[n_scr + 1])
        ids = [pl.program_id(d) for d in range(len(grid))]
        first, last = ids[0] == 0, ids[0] == grid[0] - 1
        for d in range(1, len(grid)):
            first, last = first & (ids[d] == 0), last & (ids[d] == grid[d] - 1)

        @pl.when(first)
        def _():
            for cp in copies:
                cp.start()

        body(*pf, *ins, *outs, *scr[:n_scr])

        @pl.when(last)
        def _():
            for cp in copies:
                cp.wait()

    sems = [pltpu.SemaphoreType.DMA((side.n,)), pltpu.SemaphoreType.DMA((side.n,))] if side is not None else []
    outs = pl.pallas_call(
        carrying, grid_spec=pltpu.PrefetchScalarGridSpec(
            num_scalar_prefetch=n_pf, grid=grid, in_specs=list(in_specs) + [ANY] * n_si, out_specs=list(out_specs) + [ANY] * n_so,
            scratch_shapes=list(scratch_shapes) + sems),
        out_shape=list(out_shape) + (side.outs if side is not None else []), input_output_aliases=aliases,
        compiler_params=_params(*(semantics if side is None else ["arbitrary"] * len(grid))), name=name)(
            *prefetch, *args, *(side.ins if side is not None else []))
    return list(outs) if side is None else (list(outs[:n_out]), list(outs[n_out:]))


def _call1(body, args, *, out_spec, out_shape, side=None, **kw):
    got = _call(body, args, out_specs=[out_spec], out_shape=[out_shape], side=side, **kw)
    return got[0] if side is None else (got[0][0], got[1])


MM_VMEM_BUDGET = 40 * 1024 * 1024
MM_STEP_MACS = 2200 * 1024 * 1024
MXU_WIDTH = 256
MM_STEP_COST_BYTES = 1 << 20


def _divisors(n):
    return [t for t in range(LANES, n + 1, LANES) if n % t == 0] or [n]


def _mm_tiles(M, N, K, a_bytes, b_bytes, o_bytes, n_unit=None):
    best = None
    for tk in _divisors(K):
        for tm in _divisors(M):
            for tn in _divisors(N if n_unit is None else n_unit):
                nk = K // tk
                foot = 2 * (tm * tk * a_bytes + tk * tn * b_bytes + tm * tn * o_bytes) + (tm * tn * 4 if nk > 1 else 0)
                if (foot > MM_VMEM_BUDGET or tm * tn * tk > MM_STEP_MACS or tn < min(N if n_unit is None else n_unit, MXU_WIDTH)
                        or tm < min(M, MXU_WIDTH)):
                    continue
                steps = (M // tm) * (N // tn) * nk
                traffic = M * K * a_bytes * (N // tn if nk > 1 else 1) + K * N * b_bytes * (M // tm) + M * N * o_bytes
                exposed = tm * tk * a_bytes + tk * tn * b_bytes + tm * tn * o_bytes
                cost = traffic + exposed + steps * MM_STEP_COST_BYTES + (nk - 1) * M * N * 8
                if best is None or cost < best[0]:
                    best = (cost, tm, tn, tk)
    assert best is not None, (M, N, K)
    return best[1:]


def _mm(a, b, dims, out_dtype, name, res=None, bl=None, side=None, twin=None, b_halves=False, part=None):
    bs = b.shape[1:] if bl is not None or b_halves else b.shape
    if dims == "nn":
        (M, K), (K2, N) = a.shape, bs
    elif dims == "nt":
        (M, K), (N, K2) = a.shape, bs
    else:
        (K, M), (K2, N) = a.shape, bs
    assert K == K2, (name, a.shape, b.shape)
    n_half = N
    if b_halves:
        assert dims == "tn" and bl is None
        N = 2 * n_half
    n_total, n_first, earlier = part if part is not None else (N, 0, None)
    tm, tn, tk = _mm_tiles(M, N, K, a.dtype.itemsize, b.dtype.itemsize, jnp.dtype(out_dtype).itemsize
                           + (res.dtype.itemsize if res is not None else 0) + (jnp.dtype(twin).itemsize if twin is not None else 0),
                           n_unit=math.gcd(n_half, n_first) if b_halves or n_first else None)
    nk = K // tk
    lead = (None,) if bl is not None or b_halves else ()
    pre = (lambda *ix: (bl,) + ix) if bl is not None else (lambda *ix: ix)
    if b_halves:
        per_half = n_half // tn
        pre = lambda k, j: (j // per_half, k, j % per_half)
    if dims == "tn":
        a_spec = pl.BlockSpec((tk, tm), lambda i, j, k: (k, i))
    else:
        a_spec = pl.BlockSpec((tm, tk), lambda i, j, k: (i, k))
    if dims == "nt":
        b_spec = pl.BlockSpec(lead + (tn, tk), lambda i, j, k: pre(j, k))
    else:
        b_spec = pl.BlockSpec(lead + (tk, tn), lambda i, j, k: pre(k, j))
    assert n_first % tn == 0 and (part is None or res is None)
    o_spec = pl.BlockSpec((tm, tn), lambda i, j, k: (i, n_first // tn + j))
    in_specs, args = [a_spec, b_spec], [a, b]
    if res is not None:
        in_specs.append(o_spec)
        args.append(res)
    n_main = len(args)
    n_out = 1 if twin is None else 2
    aliases = None
    if earlier is not None:
        earlier = list(earlier) if twin is not None else [earlier]
        aliases = {n_main + t: t for t in range(n_out)}
        in_specs += [ANY] * n_out
        args += earlier

    def body(*refs):
        refs = refs[:n_main] + refs[len(args):]
        a_ref, b_ref = refs[0], refs[1]
        r_ref = refs[2] if res is not None else None
        o_ref = refs[n_main]
        p = lax.dot_general(a_ref[...].astype(BF), b_ref[...].astype(BF), _DN[dims], preferred_element_type=F32)

        def finish(t):
            if r_ref is not None:
                t = t + r_ref[...]
            o_ref[...] = t.astype(out_dtype)
            if twin is not None:
                refs[n_main + 1][...] = t.astype(twin)

        if nk == 1:
            finish(p)
        else:
            acc = refs[n_main + n_out]
            k = pl.program_id(2)

            @pl.when(k == 0)
            def _():
                acc[...] = p

            @pl.when(k > 0)
            def _():
                acc[...] += p

            @pl.when(k == nk - 1)
            def _():
                finish(acc[...])

    got = _call(body, args, grid=(M // tm, N // tn, nk), in_specs=in_specs, out_specs=[o_spec] * n_out,
                out_shape=[SDS((M, n_total), out_dtype)] + ([SDS((M, n_total), twin)] if twin is not None else []),
                scratch_shapes=[pltpu.VMEM((tm, tn), F32)] if nk > 1 else [], semantics=("parallel", "parallel", "arbitrary"),
                name=name, side=side, aliases=aliases)
    outs, landed = (got, None) if side is None else got
    out = outs[0] if twin is None else (outs[0], outs[1])
    return out if side is None else (out, landed)


def _rms(x, g):
    return x * lax.rsqrt(jnp.mean(x * x, axis=-1, keepdims=True) + EPS) * g


def _ln_silu(x, g, b):
    mu = jnp.mean(x, axis=-1, keepdims=True)
    xc = x - mu
    var = jnp.mean(xc * xc, axis=-1, keepdims=True)
    return jax.nn.silu(xc * lax.rsqrt(var + EPS) * g + b)


def _merge(gc, gp, yc, yp, ps):
    return jax.nn.sigmoid(gc) * yc + jax.nn.sigmoid(gp) * (yp * ps)


def _gated(gate, val):
    return jax.nn.gelu(gate) * val


def _rms_fwd(x, g, name):
    T, D = x.shape
    tb = _tile(T, ROW_BLOCK, 8)

    def body(x_ref, g_ref, o_ref):
        o_ref[...] = _rms(x_ref[...], g_ref[...]).astype(BF)

    row = pl.BlockSpec((tb, D), lambda i: (i, 0))
    return pl.pallas_call(body, grid=(T // tb,), in_specs=[row, pl.BlockSpec((1, D), lambda i: (0, 0))], out_specs=row,
                          out_shape=SDS((T, D), BF), compiler_params=_params("parallel"), name=name)(x, g.reshape(1, D))


def _rms_bwd(x, g, dh, dres, name):
    T, D = x.shape
    tb = _tile(T, ROW_BLOCK, 8)

    def body(*refs):
        if dres is not None:
            x_ref, g_ref, dh_ref, dres_ref, dx_ref, dxb_ref, dg_ref = refs
        else:
            x_ref, g_ref, dh_ref, dx_ref, dxb_ref, dg_ref = refs
        _, vjp = jax.vjp(_rms, x_ref[...], g_ref[...])
        dx, dg = vjp(dh_ref[...].astype(F32))
        if dres is not None:
            dx = dx + dres_ref[...]
        dx_ref[...] = dx
        dxb_ref[...] = dx.astype(BF)

        @pl.when(pl.program_id(0) == 0)
        def _():
            dg_ref[...] = jnp.zeros_like(dg_ref)

        dg_ref[...] += dg

    row = pl.BlockSpec((tb, D), lambda i: (i, 0))
    vec = pl.BlockSpec((1, D), lambda i: (0, 0))
    ins = [x, g.reshape(1, D), dh] + ([dres] if dres is not None else [])
    return pl.pallas_call(
        body, grid=(T // tb,), in_specs=[row, vec, row] + ([row] if dres is not None else []), out_specs=[row, row, vec],
        out_shape=[SDS((T, D), F32), SDS((T, D), BF), SDS((1, D), F32)], compiler_params=_params("arbitrary"), name=name)(*ins)


def _row_tile(M, K, N, per_row_bytes):
    fixed = K * N * 2
    fit = [t for t in _divisors(M) if fixed + 2 * t * per_row_bytes <= MM_VMEM_BUDGET and t * K * N <= MM_STEP_MACS]
    return max(fit) if fit else min(_divisors(M))


def _mm_rms_fwd(a, b, res, g, name, side=None):
    M, K = a.shape
    N = b.shape[2]
    tm = _row_tile(M, K, N, K * 2 + N * (4 + 4 + 2))

    def body(a_ref, b_ref, r_ref, g_ref, x_ref, h_ref):
        x = r_ref[...] + lax.dot_general(a_ref[...], b_ref[...], _DN["nn"], preferred_element_type=F32)
        x_ref[...] = x
        h_ref[...] = _rms(x, g_ref[...]).astype(BF)

    row = pl.BlockSpec((tm, N), lambda i: (i, 0))
    return _call(body, (a, b, res, g.reshape(1, N)), grid=(M // tm,),
                 in_specs=[pl.BlockSpec((tm, K), lambda i: (i, 0)), pl.BlockSpec((None, K, N), lambda i: (0, 0, 0), pipeline_mode=pl.Buffered(1)), row,
                           pl.BlockSpec((1, N), lambda i: (0, 0))],
                 out_specs=[row, row], out_shape=[SDS((M, N), F32), SDS((M, N), BF)], semantics=("parallel",), name=name, side=side)


def _mm_rms_bwd(a_parts, b, x, g, dres, name, side=None):
    n_a = len(a_parts)
    M = a_parts[0].shape[-2]
    N, K = b.shape[1:]
    assert K == sum(p.shape[-1] * (p.shape[0] if p.ndim == 3 else 1) for p in a_parts)
    tm = _row_tile(M, K, N, K * 2 + N * (4 + 4 + 4 + 2))

    def body(*refs):
        a_refs, (b_ref, x_ref, g_ref, r_ref, dx_ref, dxb_ref, dg_ref) = refs[:n_a], refs[n_a:]
        dh, col = None, 0
        for p, a_ref in zip(a_parts, a_refs):
            for blk in ([a_ref[h] for h in range(p.shape[0])] if p.ndim == 3 else [a_ref[...]]):
                t = lax.dot_general(blk, b_ref[:, col:col + p.shape[-1]], _DN["nt"], preferred_element_type=F32)
                dh = t if dh is None else dh + t
                col += p.shape[-1]
        _, vjp = jax.vjp(_rms, x_ref[...], g_ref[...])
        dx, dg = vjp(dh)
        dx = dx + r_ref[...]
        dx_ref[...] = dx
        dxb_ref[...] = dx.astype(BF)

        @pl.when(pl.program_id(0) == 0)
        def _():
            dg_ref[...] = jnp.zeros_like(dg_ref)

        dg_ref[...] += dg

    row = pl.BlockSpec((tm, N), lambda i: (i, 0))
    vec = pl.BlockSpec((1, N), lambda i: (0, 0))
    a_specs = [pl.BlockSpec((p.shape[0], tm, p.shape[2]), lambda i: (0, i, 0)) if p.ndim == 3 else pl.BlockSpec((tm, p.shape[1]), lambda i: (i, 0))
               for p in a_parts]
    return _call(
        body, (*a_parts, b, x, g.reshape(1, N), dres), grid=(M // tm,),
        in_specs=a_specs + [pl.BlockSpec((None, N, K), lambda i: (0, 0, 0), pipeline_mode=pl.Buffered(1)), row, vec, row],
        out_specs=[row, row, vec], out_shape=[SDS((M, N), F32), SDS((M, N), BF), SDS((1, N), F32)],
        semantics=("arbitrary",), name=name, side=side)


def _loss_bwd(x, g, target, name):
    T, D = x.shape
    tb = _tile(T, ROW_BLOCK, 8)
    nb = T // tb

    def body(x_ref, g_ref, t_ref, loss_ref, dx_ref, dg_ref, acc):
        i = pl.program_id(0)
        y, vjp = jax.vjp(_rms, x_ref[...], g_ref[...])
        err = y - t_ref[...]
        dx, dg = vjp(err * (1.0 / D))
        dx_ref[...] = dx

        @pl.when(i == 0)
        def _():
            dg_ref[...] = jnp.zeros_like(dg_ref)
            acc[...] = jnp.zeros_like(acc)

        dg_ref[...] += dg
        acc[...] += jnp.sum(err * err, axis=0, keepdims=True)

        @pl.when(i == nb - 1)
        def _():
            loss_ref[...] = jnp.full(loss_ref.shape, (0.5 / D) * jnp.sum(acc[...]), F32)

    row = pl.BlockSpec((tb, D), lambda i: (i, 0))
    vec = pl.BlockSpec((1, D), lambda i: (0, 0))
    return pl.pallas_call(
        body, grid=(nb,), in_specs=[row, vec, row], out_specs=[pl.BlockSpec((1, LANES), lambda i: (0, 0)), row, vec],
        out_shape=[SDS((1, LANES), F32), SDS((T, D), F32), SDS((1, D), F32)], scratch_shapes=[pltpu.VMEM((1, D), F32)],
        compiler_params=_params("arbitrary"), name=name)(x, g.reshape(1, D), target)


def _ln_silu_fwd(cv, g, b, name):
    T, C = cv.shape
    tb = _tile(T, ROW_BLOCK, 8)

    def body(x_ref, g_ref, b_ref, o_ref):
        o_ref[...] = _ln_silu(x_ref[...], g_ref[...], b_ref[...]).astype(BF)

    row = pl.BlockSpec((tb, C), lambda i: (i, 0))
    vec = pl.BlockSpec((1, C), lambda i: (0, 0))
    return pl.pallas_call(body, grid=(T // tb,), in_specs=[row, vec, vec], out_specs=row, out_shape=SDS((T, C), BF),
                          compiler_params=_params("parallel"), name=name)(cv, g.reshape(1, C), b.reshape(1, C))


def _ln_silu_bwd(cv, g, b, dy, name, side=None):
    T, C = cv.shape
    tb = _tile(T, ROW_BLOCK, 8)

    def body(x_ref, g_ref, b_ref, dy_ref, dx_ref, dg_ref, db_ref):
        _, vjp = jax.vjp(_ln_silu, x_ref[...], g_ref[...], b_ref[...])
        dx, dg, db = vjp(dy_ref[...].astype(F32))
        dx_ref[...] = dx

        @pl.when(pl.program_id(0) == 0)
        def _():
            dg_ref[...] = jnp.zeros_like(dg_ref)
            db_ref[...] = jnp.zeros_like(db_ref)

        dg_ref[...] += dg
        db_ref[...] += db

    row = pl.BlockSpec((tb, C), lambda i: (i, 0))
    vec = pl.BlockSpec((1, C), lambda i: (0, 0))
    return _call(
        body, (cv, g.reshape(1, C), b.reshape(1, C), dy), grid=(T // tb,), in_specs=[row, vec, vec, row], out_specs=[row, vec, vec],
        out_shape=[SDS((T, C), F32), SDS((1, C), F32), SDS((1, C), F32)], semantics=("arbitrary",), name=name, side=side)


def _merge_fwd(proj, yc, yp, ps, C, name, side=None):
    T, D = yc.shape
    tb = _tile(T, ROW_BLOCK, 8)
    nj = D // C

    def body(gc_ref, gp_ref, yc_ref, yp_ref, ps_ref, o_ref):
        o_ref[...] = _merge(gc_ref[...], gp_ref[...], yc_ref[...].astype(F32), yp_ref[...].astype(F32), ps_ref[...]).astype(BF)

    blk = pl.BlockSpec((tb, C), lambda i, j: (i, j))
    return _call1(
        body, (proj, proj, yc, yp, ps.reshape(1, D)), grid=(T // tb, nj),
        in_specs=[pl.BlockSpec((tb, C), lambda i, j: (i, 3 + j)), pl.BlockSpec((tb, C), lambda i, j: (i, 3 + nj + j)), blk, blk,
                  pl.BlockSpec((1, C), lambda i, j: (0, j))],
        out_spec=blk, out_shape=SDS((T, D), BF), semantics=("parallel", "parallel"), name=name, side=side)


def _merge_bwd(proj, yc, yp, ps, dm, C, name, side=None):
    T, D = yc.shape
    tb = _tile(T, ROW_BLOCK, 8)
    nj = D // C

    def body(gc_ref, gp_ref, yc_ref, yp_ref, ps_ref, dm_ref, dg_ref, dyc_ref, dyp_ref, dps_ref):
        _, vjp = jax.vjp(_merge, gc_ref[...], gp_ref[...], yc_ref[...].astype(F32), yp_ref[...].astype(F32), ps_ref[...])
        dgc, dgp, dyc, dyp, dps = vjp(dm_ref[...].astype(F32))
        dg_ref[0] = dgc.astype(BF)
        dg_ref[1] = dgp.astype(BF)
        dyc_ref[...] = dyc.astype(BF)
        dyp_ref[...] = dyp.astype(BF)

        @pl.when(pl.program_id(1) == 0)
        def _():
            dps_ref[...] = jnp.zeros_like(dps_ref)

        dps_ref[...] += dps

    blk = pl.BlockSpec((tb, C), lambda j, i: (i, j))
    vec = pl.BlockSpec((1, C), lambda j, i: (0, j))
    return _call(
        body, (proj, proj, yc, yp, ps.reshape(1, D), dm), grid=(nj, T // tb),
        in_specs=[pl.BlockSpec((tb, C), lambda j, i: (i, 3 + j)), pl.BlockSpec((tb, C), lambda j, i: (i, 3 + nj + j)), blk, blk, vec, blk],
        out_specs=[pl.BlockSpec((2, tb, C), lambda j, i: (0, i, j)), blk, blk, vec],
        out_shape=[SDS((2, T, D), BF), SDS((T, D), BF), SDS((T, D), BF), SDS((1, D), F32)],
        semantics=("parallel", "arbitrary"), name=name, side=side)


def _shd(v, s, rows):
    if s == 0:
        return v
    return jnp.where(rows >= s, pltpu.roll(v, s, 0), 0.0)


def _shu(v, s, rows):
    if s == 0:
        return v
    n = v.shape[0]
    return jnp.where(rows < n - s, pltpu.roll(v, n - s, 0), 0.0)


def _glu_conv_fwd(proj, w, b, Bn, S, C, name, side=None):
    K = w.shape[0]
    sl = min(LANES, C)
    ns = C // sl

    def body(a_ref, gl_ref, w_ref, b_ref, o_ref):
        y0 = a_ref[...] * jax.nn.sigmoid(gl_ref[...])
        rows = lax.broadcasted_iota(jnp.int32, y0.shape, 0)
        acc = jnp.zeros_like(y0) + b_ref[...]
        for k in range(K):
            acc = acc + w_ref[k:k + 1, :] * _shd(y0, K - 1 - k, rows)
        o_ref[...] = acc

    return _call1(
        body, (proj, proj, w, b.reshape(1, C)), grid=(Bn, ns),
        in_specs=[pl.BlockSpec((S, sl), lambda bi, j: (bi, j)), pl.BlockSpec((S, sl), lambda bi, j: (bi, ns + j)),
                  pl.BlockSpec((K, sl), lambda bi, j: (0, j)), pl.BlockSpec((1, sl), lambda bi, j: (0, j))],
        out_spec=pl.BlockSpec((S, sl), lambda bi, j: (bi, j)), out_shape=SDS((Bn * S, C), F32),
        semantics=("parallel", "parallel"), name=name, side=side)


def _glu_conv_bwd(proj, w, dcv, Bn, S, C, name, side=None):
    K = w.shape[0]
    sl = min(LANES, C)
    ns = C // sl

    def body(a_ref, gl_ref, w_ref, d_ref, dagl_ref, dw_ref, db_ref):
        a = a_ref[...]
        sg = jax.nn.sigmoid(gl_ref[...])
        y0 = a * sg
        d = d_ref[...]
        rows = lax.broadcasted_iota(jnp.int32, y0.shape, 0)

        @pl.when(pl.program_id(1) == 0)
        def _():
            dw_ref[...] = jnp.zeros_like(dw_ref)
            db_ref[...] = jnp.zeros_like(db_ref)

        dy0 = jnp.zeros_like(y0)
        for k in range(K):
            s = K - 1 - k
            dw_ref[k:k + 1, :] += jnp.sum(d * _shd(y0, s, rows), axis=0, keepdims=True)
            dy0 = dy0 + w_ref[k:k + 1, :] * _shu(d, s, rows)
        db_ref[...] += jnp.sum(d, axis=0, keepdims=True)
        dagl_ref[0] = (dy0 * sg).astype(BF)
        dagl_ref[1] = (dy0 * a * sg * (1.0 - sg)).astype(BF)

    blk = pl.BlockSpec((S, sl), lambda j, bi: (bi, j))
    return _call(
        body, (proj, proj, w, dcv), grid=(ns, Bn),
        in_specs=[blk, pl.BlockSpec((S, sl), lambda j, bi: (bi, ns + j)), pl.BlockSpec((K, sl), lambda j, bi: (0, j)), blk],
        out_specs=[pl.BlockSpec((2, S, sl), lambda j, bi: (0, bi, j)), pl.BlockSpec((K, sl), lambda j, bi: (0, j)),
                   pl.BlockSpec((1, sl), lambda j, bi: (0, j))],
        out_shape=[SDS((2, Bn * S, C), BF), SDS((K, C), F32), SDS((1, C), F32)],
        semantics=("parallel", "arbitrary"), name=name, side=side)


def _pool_z(u, g, rows):
    s2 = u + _shd(u, 1, rows)
    s4 = s2 + _shd(s2, 2, rows)
    s8 = s4 + _shd(s4, 4, rows)
    s16 = s8 + _shd(s8, 8, rows)
    sw = jnp.where(g == 0, s2, jnp.where(g == 1, s4, jnp.where(g == 2, s8, s16)))
    cnt = jnp.minimum(rows + 1, POOL_WINDOWS[0] << g).astype(F32)
    return sw / cnt - u, cnt


def _pool_fwd(proj, wpt, l, Bn, S, C, D, name):
    G = len(POOL_WINDOWS)
    gd, go = C // G, D // G

    def body(u_ref, w_ref, o_ref):
        g = pl.program_id(1)
        u = u_ref[...]
        rows = lax.broadcasted_iota(jnp.int32, u.shape, 0)
        zp, _ = _pool_z(u, g, rows)
        o_ref[...] = lax.dot_general(zp.astype(BF), w_ref[...], _DN["nt"], preferred_element_type=F32).astype(BF)

    return pl.pallas_call(
        body, grid=(Bn, G),
        in_specs=[pl.BlockSpec((S, gd), lambda bi, g: (bi, 2 * G + g)), pl.BlockSpec((None, go, gd), lambda bi, g: (l * G + g, 0, 0))],
        out_specs=pl.BlockSpec((S, go), lambda bi, g: (bi, g)), out_shape=SDS((Bn * S, D), BF),
        compiler_params=_params("parallel", "parallel"), name=name)(proj, wpt)


def _pool_bwd(proj, wpt, dyp, l, Bn, S, C, D, name):
    G = len(POOL_WINDOWS)
    gd, go = C // G, D // G

    def body(u_ref, w_ref, d_ref, du_ref, dw_ref):
        g = pl.program_id(0)
        u = u_ref[...]
        rows = lax.broadcasted_iota(jnp.int32, u.shape, 0)
        zp, cnt = _pool_z(u, g, rows)
        d = d_ref[...]
        dzp = lax.dot_general(d, w_ref[...], _DN["nn"], preferred_element_type=F32)

        @pl.when(pl.program_id(1) == 0)
        def _():
            dw_ref[...] = jnp.zeros_like(dw_ref)

        dw_ref[...] += lax.dot_general(d, zp.astype(BF), _DN["tn"], preferred_element_type=F32)
        dsw = dzp / cnt
        zero = jnp.zeros_like(dsw)
        d16 = jnp.where(g == 3, dsw, zero)
        d8 = jnp.where(g == 2, dsw, zero) + d16 + _shu(d16, 8, rows)
        d4 = jnp.where(g == 1, dsw, zero) + d8 + _shu(d8, 4, rows)
        d2 = jnp.where(g == 0, dsw, zero) + d4 + _shu(d4, 2, rows)
        d1 = d2 + _shu(d2, 1, rows)
        du_ref[...] = (d1 - dzp).astype(BF)

    return pl.pallas_call(
        body, grid=(G, Bn),
        in_specs=[pl.BlockSpec((S, gd), lambda g, bi: (bi, 2 * G + g)), pl.BlockSpec((None, go, gd), lambda g, bi: (l * G + g, 0, 0)),
                  pl.BlockSpec((S, go), lambda g, bi: (bi, g))],
        out_specs=[pl.BlockSpec((S, gd), lambda g, bi: (bi, g)), pl.BlockSpec((None, go, gd), lambda g, bi: (g, 0, 0))],
        out_shape=[SDS((Bn * S, C), BF), SDS((G, go, gd), F32)],
        compiler_params=_params("parallel", "arbitrary"), name=name)(proj, wpt, dyp)


def _ffn_conv(u, w_ref, rows):
    K = w_ref.shape[0]
    acc = w_ref[K - 1:K, :] * u
    for k in range(K - 1):
        acc = acc + w_ref[k:k + 1, :] * _shd(u, K - 1 - k, rows)
    return acc


def _ffn_cb(F):
    return _tile(F, 256)


def _ffn_act_fwd(up0, w, Bn, S, F, name, side=None):
    cb = _ffn_cb(F)
    nj = F // cb

    def body(g_ref, v_ref, wg_ref, wv_ref, o_ref):
        rows = lax.broadcasted_iota(jnp.int32, g_ref.shape, 0)
        o_ref[...] = _gated(_ffn_conv(g_ref[...], wg_ref, rows), _ffn_conv(v_ref[...], wv_ref, rows)).astype(BF)

    K = w.shape[0]
    return _call1(
        body, (up0, up0, w, w), grid=(Bn, nj),
        in_specs=[pl.BlockSpec((S, cb), lambda bi, j: (bi, j)), pl.BlockSpec((S, cb), lambda bi, j: (bi, nj + j)),
                  pl.BlockSpec((K, cb), lambda bi, j: (0, j)), pl.BlockSpec((K, cb), lambda bi, j: (0, nj + j))],
        out_spec=pl.BlockSpec((S, cb), lambda bi, j: (bi, j)), out_shape=SDS((Bn * S, F), BF),
        semantics=("parallel", "parallel"), name=name, side=side)


SUBLANES = 8
FFN_HALO = SUBLANES
FFN_ROWS = 64
GELU_C0, GELU_C1 = 0.7978845608028654, 0.044715


def _gelu_and_grad(x):
    x2 = x * x
    t = jnp.tanh(GELU_C0 * (x + GELU_C1 * (x2 * x)))
    cdf = 0.5 * (1.0 + t)
    return x * cdf, cdf + (0.5 * GELU_C0) * x * (1.0 - t * t) * (1.0 + (3.0 * GELU_C1) * x2)


def _ffn_act_bwd(up0, w, dg, Bn, S, F, name, side=None):
    cb = min(LANES, F)
    nj = F // cb
    K = w.shape[0]
    rc = FFN_ROWS if S % FFN_ROWS == 0 else S
    win = rc + 2 * FFN_HALO
    assert K - 1 <= FFN_HALO and rc % SUBLANES == 0

    def body(g_ref, v_ref, wg_ref, wv_ref, d_ref, do_ref, dwg_ref, dwv_ref, gp, vp, dp):
        for pad, src in ((gp, g_ref), (vp, v_ref), (dp, d_ref)):
            pad[0:FFN_HALO, :] = jnp.zeros((FFN_HALO, cb), F32)
            pad[FFN_HALO + S:, :] = jnp.zeros((FFN_HALO, cb), F32)
            pad[FFN_HALO:FFN_HALO + S, :] = src[...].astype(F32)
        wg = [wg_ref[k:k + 1, :] for k in range(K)]
        wv = [wv_ref[k:k + 1, :] for k in range(K)]

        def taps(u):
            return [pltpu.roll(u, K - 1 - k, 0) for k in range(K - 1)] + [u]

        def conv(us, ws):
            acc = ws[K - 1] * us[K - 1]
            for k in range(K - 1):
                acc = acc + ws[k] * us[k]
            return acc

        def conv_t(dc, ws):
            acc = ws[K - 1] * dc
            for k in range(K - 1):
                acc = acc + ws[k] * pltpu.roll(dc, win - (K - 1 - k), 0)
            return acc

        def fold(t):
            acc = t[FFN_HALO:FFN_HALO + SUBLANES]
            for i in range(1, rc // SUBLANES):
                acc = acc + t[FFN_HALO + SUBLANES * i:FFN_HALO + SUBLANES * (i + 1)]
            return acc

        def chunk(c, sums):
            r0 = pl.multiple_of(c * rc, SUBLANES)
            gs, vs, d = taps(gp[pl.ds(r0, win), :]), taps(vp[pl.ds(r0, win), :]), dp[pl.ds(r0, win), :]
            ge, dge = _gelu_and_grad(conv(gs, wg))
            dgc = d * conv(vs, wv) * dge
            dvc = d * ge
            do_ref[0, pl.ds(r0, rc), :] = conv_t(dgc, wg)[FFN_HALO:FFN_HALO + rc].astype(BF)
            do_ref[1, pl.ds(r0, rc), :] = conv_t(dvc, wv)[FFN_HALO:FFN_HALO + rc].astype(BF)
            new = [fold(dc * u) for us, dc in ((gs, dgc), (vs, dvc)) for u in us]
            return tuple(a + b for a, b in zip(sums, new))

        sums = lax.fori_loop(0, S // rc, chunk, tuple(jnp.zeros((SUBLANES, cb), F32) for _ in range(2 * K)))

        @pl.when(pl.program_id(1) == 0)
        def _():
            dwg_ref[...] = jnp.zeros_like(dwg_ref)
            dwv_ref[...] = jnp.zeros_like(dwv_ref)

        for k in range(K):
            dwg_ref[k:k + 1, :] += jnp.sum(sums[k], axis=0, keepdims=True)
            dwv_ref[k:k + 1, :] += jnp.sum(sums[K + k], axis=0, keepdims=True)

    blk = pl.BlockSpec((S, cb), lambda j, bi: (bi, j))
    wblk = pl.BlockSpec((K, cb), lambda j, bi: (0, j))
    return _call(
        body, (up0, up0, w, w, dg), grid=(nj, Bn),
        in_specs=[blk, pl.BlockSpec((S, cb), lambda j, bi: (bi, nj + j)), wblk, pl.BlockSpec((K, cb), lambda j, bi: (0, nj + j)), blk],
        out_specs=[pl.BlockSpec((2, S, cb), lambda j, bi: (0, bi, j)), wblk, wblk],
        out_shape=[SDS((2, Bn * S, F), BF), SDS((K, F), F32), SDS((K, F), F32)],
        scratch_shapes=[pltpu.VMEM((S + 2 * FFN_HALO, cb), F32)] * 3, semantics=("parallel", "arbitrary"), name=name, side=side)


def _softmax_rows(q, k, scale):
    sc = lax.dot_general(q, k, _DN["nt"], preferred_element_type=F32) * scale
    e = jnp.exp(sc - jnp.max(sc, axis=-1, keepdims=True))
    return e / jnp.sum(e, axis=-1, keepdims=True)


def _attn_ts(S):
    return _tile(S, 1024, 8)


def _attn_fwd(q, kv, Bn, S, Mn, D, name, side=None):
    H = XA_HEADS
    dh = D // H
    ts = _attn_ts(S)
    nsb = S // ts
    scale = dh ** -0.5

    def body(q_ref, k_ref, v_ref, o_ref):
        p = _softmax_rows(q_ref[...], k_ref[...], scale)
        o_ref[...] = lax.dot_general(p.astype(BF), v_ref[...], _DN["nn"], preferred_element_type=F32).astype(BF)

    qblk = pl.BlockSpec((ts, dh), lambda bi, h, s: (bi * nsb + s, h))
    return _call1(
        body, (q, kv, kv), grid=(Bn, H, nsb),
        in_specs=[qblk, pl.BlockSpec((Mn, dh), lambda bi, h, s: (bi, h)), pl.BlockSpec((Mn, dh), lambda bi, h, s: (bi, H + h))],
        out_spec=qblk, out_shape=SDS((Bn * S, D), BF), semantics=("parallel", "parallel", "parallel"), name=name, side=side)


def _attn_bwd(q, kv, datt, Bn, S, Mn, D, name):
    H = XA_HEADS
    dh = D // H
    ts = _attn_ts(S)
    nsb = S // ts
    scale = dh ** -0.5

    def body(q_ref, k_ref, v_ref, do_ref, dq_ref, dk_ref, dv_ref):
        q, k, v, do = q_ref[...], k_ref[...], v_ref[...], do_ref[...]
        p = _softmax_rows(q, k, scale)
        dp = lax.dot_general(do, v, _DN["nt"], preferred_element_type=F32)
        ds = (p * (dp - jnp.sum(dp * p, axis=-1, keepdims=True)) * scale).astype(BF)
        dq_ref[...] = lax.dot_general(ds, k, _DN["nn"], preferred_element_type=F32).astype(BF)

        @pl.when(pl.program_id(2) == 0)
        def _():
            dk_ref[...] = jnp.zeros_like(dk_ref)
            dv_ref[...] = jnp.zeros_like(dv_ref)

        dk_ref[...] += lax.dot_general(ds, q, _DN["tn"], preferred_element_type=F32)
        dv_ref[...] += lax.dot_general(p.astype(BF), do, _DN["tn"], preferred_element_type=F32)

    qblk = pl.BlockSpec((ts, dh), lambda bi, h, s: (bi * nsb + s, h))
    kblk = pl.BlockSpec((Mn, dh), lambda bi, h, s: (bi, h))
    return pl.pallas_call(
        body, grid=(Bn, H, nsb),
        in_specs=[qblk, kblk, pl.BlockSpec((Mn, dh), lambda bi, h, s: (bi, H + h)), qblk],
        out_specs=[qblk, kblk, kblk], out_shape=[SDS((Bn * S, D), BF), SDS((Bn * Mn, D), F32), SDS((Bn * Mn, D), F32)],
        compiler_params=_params("parallel", "parallel", "arbitrary"), name=name)(q, kv, kv, datt)


class _Sides:
    def __init__(self, by_key=None, on_land=None):
        self.by_key, self.landed, self.on_land = dict(by_key or {}), {}, on_land

    def run(self, key, fn, *args, **kw):
        side = self.by_key.get(key)
        if side is None:
            return fn(*args, **kw)
        out, self.landed[key] = fn(*args, side=side() if callable(side) else side, **kw)
        if self.on_land is not None:
            self.on_land(key, self.landed[key])
        return out

    def mm(self, key, *args, **kw):
        return self.run(key, _mm, *args, **kw)


def _layer_fwd(x, h, mem_n, W, V, l, dims, sides, next_g):
    Bn, S, Mn, D, C, F = dims
    n = f"l{l}_"
    proj = sides.mm("proj", h, W["w_in"], "nn", F32, n + "proj", bl=0)
    cv = sides.run("glu_conv", _glu_conv_fwd, proj, V["conv_dw_w"][l], V["conv_dw_b"][l], Bn, S, C, n + "glu_conv")
    yc1 = _ln_silu_fwd(cv, V["conv_ln_g"][l], V["conv_ln_b"][l], n + "ln_silu")
    yc = sides.mm("conv_out", yc1, W["w_conv_out"], "nn", BF, n + "conv_out", bl=0)
    yp = _pool_fwd(proj, W["w_pool"], 0, Bn, S, C, D, n + "pool")
    merged = sides.run("merge", _merge_fwd, proj, yc, yp, V["pool_scale"][l], C, n + "merge")
    x1, hq = sides.run("out_proj", _mm_rms_fwd, merged, W["w_out"], x, V["xattn_norm_g"][l], n + "out_proj")
    q = sides.mm("q_proj", hq, W["w_q"], "nn", BF, n + "q_proj", bl=0)
    kv = _mm(mem_n, W["w_kv"], "nn", BF, n + "kv_proj", bl=0)
    att = sides.run("attn", _attn_fwd, q, kv, Bn, S, Mn, D, n + "attn")
    x2, hf = sides.run("o_proj", _mm_rms_fwd, att, W["w_o"], x1, V["ffn_norm_g"][l], n + "o_proj")
    up0 = sides.mm("up_proj", hf, W["w_up"], "nn", F32, n + "up_proj", bl=0)
    gact = sides.run("ffn_act", _ffn_act_fwd, up0, V["ffn_dw_w"][l], Bn, S, F, n + "ffn_act")
    if next_g is not None:
        x3, h3 = sides.run("down_proj", _mm_rms_fwd, gact, W["w_down"], x2, next_g, n + "down_proj")
    else:
        x3, h3 = sides.mm("down_proj", gact, W["w_down"], "nn", F32, n + "down_proj", res=x2, bl=0), None
    return x3, h3, dict(x=x, h=h, proj=proj, cv=cv, yc1=yc1, yc=yc, yp=yp, merged=merged, x1=x1, hq=hq, q=q, kv=kv, att=att, x2=x2,
                        hf=hf, up0=up0, gact=gact)


def _layer_bwd_mlp(dx, dxb, sv, W, V, l, dims, sides):
    Bn, S, Mn, D, C, F = dims
    n = f"l{l}_b_"
    gw, sm = {}, {}
    dgact = sides.mm("d_gact", dxb, W["w_down"], "nt", BF, n + "d_gact", bl=0)
    gw["w_down"] = sides.mm("dw_down", sv["gact"], dxb, "tn", F32, n + "dw_down", twin=BF)
    dup0, dwg, dwv = sides.run("ffn_act_b", _ffn_act_bwd, sv["up0"], V["ffn_dw_w"][l], dgact, Bn, S, F, n + "ffn_act")
    sm["ffn_dw_w"] = jnp.concatenate([dwg, dwv], axis=1)
    dx2, dx2b, sm["ffn_norm_g"] = _mm_rms_bwd([dup0], W["w_up"], sv["x2"], V["ffn_norm_g"][l], dx, n + "d_hf")
    gw["w_up"] = sides.mm("dw_up", sv["hf"], dup0, "tn", F32, n + "dw_up", twin=BF, b_halves=True)
    return dx2, dx2b, gw, sm


def _layer_bwd_mix(dx2, dx2b, dmem_n, sv, mem_n, W, V, l, dims, sides, gw):
    Bn, S, Mn, D, C, F = dims
    n = f"l{l}_b_"
    sm = {}
    datt = sides.mm("d_att", dx2b, W["w_o"], "nt", BF, n + "d_att", bl=0)
    gw["w_o"] = _mm(sv["att"], dx2b, "tn", F32, n + "dw_o", twin=BF)
    dq, dk, dv = _attn_bwd(sv["q"], sv["kv"], datt, Bn, S, Mn, D, n + "attn")
    dkv = jnp.concatenate([dk, dv], axis=1)
    gw["w_kv"] = _mm(mem_n, dkv, "tn", F32, n + "dw_kv", twin=BF)
    dmem_n = _mm(dkv, W["w_kv"], "nt", F32, n + "d_mem", res=dmem_n, bl=0)
    dx1, dx1b, sm["xattn_norm_g"] = _mm_rms_bwd([dq], W["w_q"], sv["x1"], V["xattn_norm_g"][l], dx2, n + "d_hq")
    gw["w_q"] = _mm(sv["hq"], dq, "tn", F32, n + "dw_q", twin=BF)
    dmerged = sides.mm("d_merged", dx1b, W["w_out"], "nt", BF, n + "d_merged", bl=0)
    gw["w_out"] = _mm(sv["merged"], dx1b, "tn", F32, n + "dw_out", twin=BF)
    dgates, dyc, dyp, sm["pool_scale"] = sides.run("merge_b", _merge_bwd, sv["proj"], sv["yc"], sv["yp"], V["pool_scale"][l], dmerged, C, n + "merge")
    du, dwp = _pool_bwd(sv["proj"], W["w_pool"], dyp, 0, Bn, S, C, D, n + "pool")
    gw["w_pool"] = (dwp, dwp.astype(BF))
    dyc1 = _mm(dyc, W["w_conv_out"], "nt", F32, n + "d_yc1", bl=0)
    gw["w_conv_out"] = _mm(sv["yc1"], dyc, "tn", F32, n + "dw_conv_out", twin=BF)
    dcv, sm["conv_ln_g"], sm["conv_ln_b"] = sides.run("ln_silu_b", _ln_silu_bwd, sv["cv"], V["conv_ln_g"][l], V["conv_ln_b"][l], dyc1, n + "ln_silu")
    dagl, sm["conv_dw_w"], sm["conv_dw_b"] = sides.run("glu_conv_b", _glu_conv_bwd, sv["proj"], V["conv_dw_w"][l], dcv, Bn, S, C, n + "glu_conv")
    dx, dxb, sm["mix_norm_g"] = sides.run("d_h", _mm_rms_bwd, [dagl, du, dgates], W["w_in"], sv["x"], V["mix_norm_g"][l], dx1, n + "d_h")
    n_in = W["w_in"].shape[2]
    part = _mm(sv["h"], dagl, "tn", F32, n + "dw_in_conv", twin=BF, b_halves=True, part=(n_in, 0, None))
    part = _mm(sv["h"], du, "tn", F32, n + "dw_in_pool", twin=BF, part=(n_in, 2 * C, part))
    gw["w_in"] = sides.mm("dw_in", sv["h"], dgates, "tn", F32, n + "dw_in", twin=BF, b_halves=True, part=(n_in, 3 * C, part))
    return dx, dxb, dmem_n, sm


BIG = (("w_in", "col"), ("w_conv_out", "col"), ("w_pool", "row"), ("w_out", "row"), ("w_q", "row"), ("w_kv", "col"),
       ("w_o", "row"), ("w_up", "col"), ("w_down", "row"))
ALL_RELS = (1, 2, 3)
GATHER_FIRST = ("w_in", "w_conv_out", "w_pool", "w_out", "w_q", "w_o")
FWD_CARRY = {
    (0, "proj"): (("w_up", 0, (1, 2)),),
    (0, "glu_conv"): (("w_up", 0, (3,)),),
    (0, "merge"): (("w_kv", 0, ALL_RELS),),
    (0, "q_proj"): (("w_down", 0, (1, 2)),),
    (0, "attn"): (("w_down", 0, (3,)),),
    (0, "up_proj"): (("w_in", 1, ALL_RELS), ("w_conv_out", 1, ALL_RELS), ("w_pool", 1, ALL_RELS), ("w_o", 1, ALL_RELS)),
    (0, "ffn_act"): (("w_out", 1, ALL_RELS), ("w_q", 1, ALL_RELS), ("w_kv", 1, ALL_RELS)),
    (1, "proj"): (("w_up", 1, (1, 2)),),
    (1, "glu_conv"): (("w_up", 1, (3,)),),
    (1, "merge"): (("w_down", 1, (1, 2)),),
    (1, "attn"): (("w_down", 1, (3,)),),
}
PASS_CARRY = {
    (0, "out_proj"): (("w_kv", 0),),
    (0, "o_proj"): (("w_up", 0), ("w_down", 0)),
    (0, "down_proj"): (("w_in", 1), ("w_conv_out", 1), ("w_pool", 1), ("w_out", 1), ("w_q", 1), ("w_kv", 1), ("w_o", 1)),
    (1, "o_proj"): (("w_up", 1), ("w_down", 1)),
}
EARLY = ("w_down", "w_up")
BWD_CARRY_EARLY = {"merge_b": ("w_down",), "glu_conv_b": ("w_up",)}
BWD_CARRY_LATE = {"ffn_act_b": ("w_in", "w_conv_out", "w_pool", "w_out", "w_q", "w_kv", "w_o")}
BWD_LAST_LAYER = (("att", ("w_o", "w_kv", "w_q"), "d_merged", {"d_h": ("w_o", "w_kv", "w_q")}),
                  ("tok", ("w_out", "w_pool", "w_conv_out"), "ln_silu_b", {"dw_in": ("w_out", "w_pool", "w_conv_out")}))


def _place():
    xi, yi, ci = lax.axis_index("x"), lax.axis_index("y"), lax.axis_index("c")
    return xi, yi, ci, 2 * xi + yi


def _chip_peer(xi, yi, ci, r):
    return (xi ^ (r >> 1), yi ^ (r & 1), ci)


def _full_shard(ref, kind, k, cs):
    if kind == "col":
        return ref.at[:, :, :, :, pl.ds(pl.multiple_of(k * cs, cs), cs)]
    return ref.at[:, :, k]


def _gather_weights(shards, kinds):
    n = len(shards)
    outs = []
    for s, kind in zip(shards, kinds):
        L, P, _, RH, CS = s.shape
        outs.append(SDS((L, P, 2, RH, CS * N_CHIPS) if kind == "col" else (L, P, N_CHIPS, 2, RH, CS), s.dtype))
    per = 7

    def body(*refs):
        srcs, fulls, (ssem, rsem) = refs[:n], refs[n:2 * n], refs[2 * n:]
        xi, yi, ci, j = _place()
        sib = (xi, yi, 1 - ci)

        def piece(i, k, c):
            kind, cs = kinds[i], shards[i].shape[-1]
            if kind == "col":
                return fulls[i].at[:, :, c, :, pl.ds(pl.multiple_of(k * cs, cs), cs)]
            return fulls[i].at[:, :, k, c]

        def copy(i, slot, src, dst, dev):
            return pltpu.make_async_remote_copy(src_ref=src, dst_ref=dst, send_sem=ssem.at[per * i + slot], recv_sem=rsem.at[per * i + slot],
                                                device_id=dev, device_id_type=MESH)

        own, first, passed = [], [], []
        for i in range(n):
            for r in (1, 2, 3):
                first.append(copy(i, r - 1, srcs[i].at[:, :, ci], piece(i, j, ci), _chip_peer(xi, yi, ci, r)))
                first[-1].start()
        for i in range(n):
            own.append(copy(i, 6, srcs[i], _full_shard(fulls[i], kinds[i], j, shards[i].shape[-1]), sib))
            own[-1].start()
        for i in range(n):
            for r in (1, 2, 3):
                got = piece(i, j ^ r, ci)
                copy(i, r - 1, got, got, sib).wait_recv()
                passed.append(copy(i, 2 + r, got, got, sib))
                passed[-1].start()
        for i in range(n):
            for r in (1, 2, 3):
                got = piece(i, j ^ r, 1 - ci)
                copy(i, 2 + r, got, got, sib).wait_recv()
        for cp in own:
            cp.wait()
        for cp in first + passed:
            cp.wait_send()

    return pl.pallas_call(
        body, in_specs=[ANY] * n, out_specs=[ANY] * n, out_shape=outs,
        scratch_shapes=[pltpu.SemaphoreType.DMA((per * n,)), pltpu.SemaphoreType.DMA((per * n,))], name="gather_weights")(*shards)


def _full_sds(s, kind):
    L, P, _, RH, CS = s.shape
    return SDS((L, P, 2, RH, CS * N_CHIPS) if kind == "col" else (L, P, N_CHIPS, 2, RH, CS), s.dtype)


def _gather_piece(full, kind, cs, k, c):
    if kind == "col":
        return full.at[:, :, c, :, pl.ds(pl.multiple_of(k * cs, cs), cs)]
    return full.at[:, :, k, c]


def _side_gather(shards, kinds, rels, fulls):
    n = len(shards)

    def make(srcs, outs, ssem, rsem):
        xi, yi, ci, j = _place()
        return [pltpu.make_async_remote_copy(
            src_ref=srcs[i].at[:, :, ci], dst_ref=_gather_piece(outs[i], kinds[i], shards[i].shape[-1], j, ci), send_sem=ssem.at[3 * i + r - 1],
            recv_sem=rsem.at[3 * i + r - 1], device_id=_chip_peer(xi, yi, ci, r), device_id_type=MESH) for i in range(n) for r in rels[i]]

    prior = [f for f in fulls if f is not None]
    assert len(prior) in (0, n)
    return _Side(list(shards) + prior, [_full_sds(s, k) for s, k in zip(shards, kinds)], 3 * n, make, n_alias=len(prior))


def _side_gather_pass(fulls, shards, kinds):
    n = len(fulls)

    def make(srcs, outs, ssem, rsem):
        xi, yi, ci, j = _place()
        sib = (xi, yi, 1 - ci)
        cps = []
        for i in range(n):
            cs = shards[i].shape[-1]
            for r in (1, 2, 3):
                got = _gather_piece(outs[i], kinds[i], cs, j ^ r, ci)
                cps.append(pltpu.make_async_remote_copy(src_ref=got, dst_ref=got, send_sem=ssem.at[4 * i + r - 1], recv_sem=rsem.at[4 * i + r - 1],
                                                        device_id=sib, device_id_type=MESH))
            cps.append(pltpu.make_async_remote_copy(src_ref=srcs[i], dst_ref=_full_shard(outs[i], kinds[i], j, cs), send_sem=ssem.at[4 * i + 3],
                                                    recv_sem=rsem.at[4 * i + 3], device_id=sib, device_id_type=MESH))
        return cps

    return _Side(list(shards) + list(fulls), [SDS(f.shape, f.dtype) for f in fulls], 4 * n, make, n_alias=n)


def _sibling_exchange(gviews, kinds, name):
    n = len(gviews)
    outs = [SDS(g.shape[:1] + g.shape[2:] if kind == "col" else g.shape[:2] + g.shape[3:], g.dtype) for g, kind in zip(gviews, kinds)]

    def body(*refs):
        gs, lands, (ssem, rsem) = refs[:n], refs[n:2 * n], refs[2 * n:]
        xi, yi, ci, _ = _place()
        cps = []
        for i in range(n):
            src = gs[i].at[:, 1 - ci] if kinds[i] == "col" else gs[i].at[:, :, 1 - ci]
            cps.append(pltpu.make_async_remote_copy(src_ref=src, dst_ref=lands[i], send_sem=ssem.at[i], recv_sem=rsem.at[i],
                                                    device_id=(xi, yi, 1 - ci), device_id_type=MESH))
            cps[-1].start()
        for cp in cps:
            cp.wait()

    return pl.pallas_call(body, in_specs=[ANY] * n, out_specs=[ANY] * n, out_shape=outs,
                          scratch_shapes=[pltpu.SemaphoreType.DMA((n,)), pltpu.SemaphoreType.DMA((n,))], name=name)(*gviews)


def _side_sibling_exchange(gviews, kinds):
    outs = [SDS(g.shape[:1] + g.shape[2:] if kind == "col" else g.shape[:2] + g.shape[3:], g.dtype) for g, kind in zip(gviews, kinds)]

    def make(gs, lands, ssem, rsem):
        xi, yi, ci, _ = _place()
        return [pltpu.make_async_remote_copy(src_ref=gs[i].at[:, 1 - ci] if kinds[i] == "col" else gs[i].at[:, :, 1 - ci], dst_ref=lands[i],
                                             send_sem=ssem.at[i], recv_sem=rsem.at[i], device_id=(xi, yi, 1 - ci), device_id_type=MESH)
                for i in range(len(gs))]

    return _Side(gviews, outs, len(gviews), make)


def _chip_sums(gs, lands, kinds, jc, name):
    n = len(gs)
    args, in_specs, out_specs, out_shape = [], [], [], []
    for g, land, kind in zip(gs, lands, kinds):
        if kind == "col":
            P, _, RH, C = g.shape
            CS = C // N_CHIPS
            in_specs += [pl.BlockSpec((P, None, RH, CS), lambda r, jc: (0, jc[1], 0, jc[0] ^ r)),
                         pl.BlockSpec((P, RH, CS), lambda r, jc: (0, 0, jc[0] ^ r))]
        else:
            P, _, _, RH, CS = g.shape
            in_specs += [pl.BlockSpec((P, None, None, RH, CS), lambda r, jc: (0, jc[0] ^ r, jc[1], 0, 0)),
                         pl.BlockSpec((P, None, RH, CS), lambda r, jc: (0, jc[0] ^ r, 0, 0))]
        args += [g, land]
        out_specs += [pl.BlockSpec((P, RH, CS), lambda r, jc: (0, 0, 0)), pl.BlockSpec((None, P, RH, CS), lambda r, jc: (r, 0, 0, 0))]
        out_shape += [SDS((P, RH, CS), F32), SDS((N_CHIPS, P, RH, CS), BF)]

    def body(jc_ref, *refs):
        ins, outs = refs[:2 * n], refs[2 * n:]
        for i in range(n):
            s = ins[2 * i][...] + ins[2 * i + 1][...].astype(F32)
            outs[2 * i + 1][...] = s.astype(BF)

            @pl.when(pl.program_id(0) == 0)
            def _():
                outs[2 * i][...] = s

    outs = _call(body, args, grid=(N_CHIPS,), in_specs=in_specs, out_specs=out_specs, out_shape=out_shape, semantics=("arbitrary",),
                 name=name, prefetch=(jc,))
    return outs[0::2], outs[1::2]


def _chip_exchange_copies(srcs, lands, ssem, rsem):
    xi, yi, ci, _ = _place()
    return [pltpu.make_async_remote_copy(src_ref=srcs[i].at[r], dst_ref=lands[i].at[r], send_sem=ssem.at[3 * i + r - 1],
                                         recv_sem=rsem.at[3 * i + r - 1], device_id=_chip_peer(xi, yi, ci, r), device_id_type=MESH)
            for i in range(len(srcs)) for r in (1, 2, 3)]


def _side_chip_exchange(pieces):
    return _Side(pieces, [SDS(p.shape, p.dtype) for p in pieces], 3 * len(pieces), _chip_exchange_copies)


FINAL_SUM_STEPS = 2


def _final_sums(owns, lands, jc, shards, l, L, name, side=None):
    n = len(owns)
    args, in_specs, out_specs, out_shape = [], [], [], []
    for own, land in zip(owns, lands):
        P, RH, CS = own.shape
        hr = RH // FINAL_SUM_STEPS
        in_specs += [pl.BlockSpec((P, hr, CS), lambda h, jc: (0, h, 0))]
        in_specs += [pl.BlockSpec((None, P, hr, CS), functools.partial(lambda r, h, jc: (r, 0, h, 0), r)) for r in (1, 2, 3)]
        args += [own, land, land, land]
        out_specs.append(pl.BlockSpec((None, P, None, hr, CS), lambda h, jc: (l, 0, jc[1], h, 0)))
        out_shape.append(SDS((L, P, 2, RH, CS), F32))
    aliases = None
    if shards is not None:
        aliases = {4 * n + i: i for i in range(n)}
        in_specs += [ANY] * n
        args += list(shards)

    def body(jc_ref, *refs):
        outs = refs[len(args):]
        for i in range(n):
            o, a, b, c = (refs[4 * i + t][...] for t in range(4))
            outs[i][...] = ((o + a.astype(F32)) + b.astype(F32)) + c.astype(F32)

    return _call(body, args, grid=(FINAL_SUM_STEPS,), in_specs=in_specs, out_specs=out_specs, out_shape=out_shape, semantics=("arbitrary",),
                 name=name, prefetch=(jc,), aliases=aliases, side=side)


def _halves_exchange(shards, l, name):
    n = len(shards)

    def body(*refs):
        outs, (ssem, rsem) = refs[n:2 * n], refs[2 * n:]
        xi, yi, ci, _ = _place()
        cps = []
        for i in range(n):
            mine = outs[i].at[l, :, ci]
            cps.append(pltpu.make_async_remote_copy(src_ref=mine, dst_ref=mine, send_sem=ssem.at[i], recv_sem=rsem.at[i],
                                                    device_id=(xi, yi, 1 - ci), device_id_type=MESH))
            cps[-1].start()
        for i in range(n):
            land = outs[i].at[l, :, 1 - ci]
            pltpu.make_async_remote_copy(src_ref=land, dst_ref=land, send_sem=ssem.at[i], recv_sem=rsem.at[i],
                                         device_id=(xi, yi, 1 - ci), device_id_type=MESH).wait_recv()
        for cp in cps:
            cp.wait_send()

    return pl.pallas_call(body, in_specs=[ANY] * n, out_specs=[ANY] * n, out_shape=[SDS(s.shape, s.dtype) for s in shards],
                          input_output_aliases={i: i for i in range(n)},
                          scratch_shapes=[pltpu.SemaphoreType.DMA((n,)), pltpu.SemaphoreType.DMA((n,))], name=name)(*shards)


def _reduce_small(part, pieces):
    NR, Wd = part.shape
    ND = 2 * N_CHIPS
    n = len(pieces)

    def body(p_ref, *refs):
        srcs, o_ref, lands, (land, ssem, rsem, xs, xr) = refs[:n], refs[n], refs[n + 1:2 * n + 1], refs[2 * n + 1:]
        exchange = _chip_exchange_copies(srcs, lands, xs, xr)
        for cp in exchange:
            cp.start()
        xi, yi, ci, j = _place()
        me = 2 * j + ci
        land[me] = p_ref[...]
        cps = []
        for rr in range(1, ND):
            dev = (xi ^ (rr >> 2), yi ^ ((rr >> 1) & 1), ci ^ (rr & 1))
            cps.append(pltpu.make_async_remote_copy(src_ref=p_ref, dst_ref=land.at[me], send_sem=ssem.at[rr - 1], recv_sem=rsem.at[rr - 1],
                                                    device_id=dev, device_id_type=MESH))
            cps[-1].start()
        for rr in range(1, ND):
            got = land.at[me ^ rr]
            pltpu.make_async_remote_copy(src_ref=got, dst_ref=got, send_sem=ssem.at[rr - 1], recv_sem=rsem.at[rr - 1],
                                         device_id=(xi, yi, ci), device_id_type=MESH).wait_recv()
        acc = land[0]
        for d in range(1, ND):
            acc = acc + land[d]
        o_ref[...] = acc
        for cp in cps:
            cp.wait_send()
        for cp in exchange:
            cp.wait()

    vm = pl.BlockSpec(memory_space=pltpu.VMEM)
    outs = pl.pallas_call(
        body, in_specs=[vm] + [ANY] * n, out_specs=[vm] + [ANY] * n, out_shape=[SDS((NR, Wd), F32)] + [SDS(p.shape, p.dtype) for p in pieces],
        scratch_shapes=[pltpu.VMEM((ND, NR, Wd), F32), pltpu.SemaphoreType.DMA((ND - 1,)), pltpu.SemaphoreType.DMA((ND - 1,)),
                        pltpu.SemaphoreType.DMA((3 * n,)), pltpu.SemaphoreType.DMA((3 * n,))],
        name="small_grad_allreduce")(part, *pieces)
    return outs[0], list(outs[1:])


def _adamw_update(w_ref, g_ref, m_ref, v_ref, d_ref, mo_ref, vo_ref):
    g = g_ref[...]
    m = ADAM_B1 * m_ref[...] + (1.0 - ADAM_B1) * g
    v = ADAM_B2 * v_ref[...] + (1.0 - ADAM_B2) * jnp.square(g)
    m_hat = m / (1.0 - ADAM_B1 ** ADAM_STEP)
    v_hat = v / (1.0 - ADAM_B2 ** ADAM_STEP)
    d_ref[...] = -ADAM_LR * (m_hat / (jnp.sqrt(v_hat) + ADAM_EPS) + ADAM_WD * w_ref[...])
    mo_ref[...] = m
    vo_ref[...] = v


ADAMW_STEPS = 8


def _adamw_layer(ws, gs, ms, vs, prev, l, name, side=None):
    n = len(ws)
    args, in_specs, out_specs, out_shape = [], [], [], []
    for w, g, m, v in zip(ws, gs, ms, vs):
        L, R, C = w.shape
        blk = pl.BlockSpec((None, R // ADAMW_STEPS, C), lambda i: (l, i, 0))
        in_specs += [blk] * 4
        args += [w, g, m, v]
        out_specs += [blk] * 3
        out_shape += [SDS((L, R, C), F32)] * 3
    aliases = None
    if prev is not None:
        aliases = {4 * n + i: i for i in range(3 * n)}
        in_specs += [ANY] * (3 * n)
        args += list(prev)

    def body(*refs):
        outs = refs[len(args):]
        for i in range(n):
            _adamw_update(*refs[4 * i:4 * i + 4], *outs[3 * i:3 * i + 3])

    return _call(body, args, grid=(ADAMW_STEPS,), in_specs=in_specs, out_specs=out_specs, out_shape=out_shape, semantics=("parallel",),
                 name=name, aliases=aliases, side=side)


def _adamw(w, g, m, v, name):
    shape = w.shape
    C = shape[-1]
    R = w.size // C
    tb = _tile(R, max(8, (1 << 18) // C), 8)
    body = functools.partial(_adamw_update)
    blk = pl.BlockSpec((tb, C), lambda i: (i, 0))
    outs = pl.pallas_call(body, grid=(R // tb,), in_specs=[blk] * 4, out_specs=[blk] * 3, out_shape=[SDS((R, C), F32)] * 3,
                          compiler_params=_params("parallel"), name=name)(*[t.reshape(R, C) for t in (w, g, m, v)])
    return [t.reshape(shape) for t in outs]


WEIGHTS = ("mix_norm_g", "w_in", "conv_dw_w", "conv_dw_b", "conv_ln_g", "conv_ln_b", "w_conv_out", "w_pool_grp", "pool_scale", "w_out",
           "xattn_norm_g", "mem_norm_g", "w_q", "w_kv", "w_o", "ffn_norm_g", "w_up", "ffn_dw_w", "w_down", "final_norm_g")
VECTORS = ("mix_norm_g", "conv_dw_b", "conv_ln_g", "conv_ln_b", "pool_scale", "xattn_norm_g", "mem_norm_g", "ffn_norm_g", "final_norm_g")


def _shard_view(t, kind):
    L, P, R, C = t.shape
    return t.reshape(L, P, 2, R // 2, C)


def _rows(t, width):
    return t.reshape(-1, width)


def _pack(parts):
    return jnp.concatenate([jnp.pad(p, ((0, (-p.shape[0]) % 8), (0, 0))) for p in parts], axis=0)


def kernel(x, mem, mix_norm_g, w_in, conv_dw_w, conv_dw_b, conv_ln_g, conv_ln_b, w_conv_out, w_pool_grp, pool_scale, w_out, xattn_norm_g, mem_norm_g, w_q, w_kv, w_o, ffn_norm_g, w_up, ffn_dw_w, w_down, final_norm_g, loss_target, m_mix_norm_g, m_w_in, m_conv_dw_w, m_conv_dw_b, m_conv_ln_g, m_conv_ln_b, m_w_conv_out, m_w_pool_grp, m_pool_scale, m_w_out, m_xattn_norm_g, m_mem_norm_g, m_w_q, m_w_kv, m_w_o, m_ffn_norm_g, m_w_up, m_ffn_dw_w, m_w_down, m_final_norm_g, v_mix_norm_g, v_w_in, v_conv_dw_w, v_conv_dw_b, v_conv_ln_g, v_conv_ln_b, v_w_conv_out, v_w_pool_grp, v_pool_scale, v_w_out, v_xattn_norm_g, v_mem_norm_g, v_w_q, v_w_kv, v_w_o, v_ffn_norm_g, v_w_up, v_ffn_dw_w, v_w_down, v_final_norm_g):
    w = dict(mix_norm_g=mix_norm_g, w_in=w_in, conv_dw_w=conv_dw_w, conv_dw_b=conv_dw_b, conv_ln_g=conv_ln_g, conv_ln_b=conv_ln_b,
             w_conv_out=w_conv_out, w_pool_grp=w_pool_grp, pool_scale=pool_scale, w_out=w_out, xattn_norm_g=xattn_norm_g,
             mem_norm_g=mem_norm_g, w_q=w_q, w_kv=w_kv, w_o=w_o, ffn_norm_g=ffn_norm_g, w_up=w_up, ffn_dw_w=ffn_dw_w, w_down=w_down,
             final_norm_g=final_norm_g)
    m = dict(zip(WEIGHTS, (m_mix_norm_g, m_w_in, m_conv_dw_w, m_conv_dw_b, m_conv_ln_g, m_conv_ln_b, m_w_conv_out, m_w_pool_grp, m_pool_scale,
                           m_w_out, m_xattn_norm_g, m_mem_norm_g, m_w_q, m_w_kv, m_w_o, m_ffn_norm_g, m_w_up, m_ffn_dw_w, m_w_down, m_final_norm_g)))
    v = dict(zip(WEIGHTS, (v_mix_norm_g, v_w_in, v_conv_dw_w, v_conv_dw_b, v_conv_ln_g, v_conv_ln_b, v_w_conv_out, v_w_pool_grp, v_pool_scale,
                           v_w_out, v_xattn_norm_g, v_mem_norm_g, v_w_q, v_w_kv, v_w_o, v_ffn_norm_g, v_w_up, v_ffn_dw_w, v_w_down, v_final_norm_g)))
    xi, yi, ci, j = _place()
    jc = jnp.stack([j, ci]).astype(jnp.int32)
    L = w_in.shape[0]
    G = len(POOL_WINDOWS)
    kinds = dict(BIG)

    def to_mat(name, t):
        if name == "w_pool":
            return jnp.swapaxes(t, 2, 3)
        return t[:, None]

    def from_mat(name, t):
        if name == "w_pool":
            return jnp.swapaxes(t, 2, 3)
        return t[:, 0]

    src = {name: w["w_pool_grp" if name == "w_pool" else name] for name, _ in BIG}

    KC, cs_c = conv_dw_w.shape[1], conv_dw_w.shape[2]
    KF, cs_f = ffn_dw_w.shape[1], ffn_dw_w.shape[2]
    taps = jnp.concatenate([conv_dw_w.reshape(L * KC, cs_c), ffn_dw_w.reshape(L * KF * (cs_f // cs_c), cs_c)], axis=0)
    n_taps = taps.shape[0]
    taps = jnp.pad(taps, ((0, (-n_taps) % 16), (0, 0)))
    names = [name for name, _ in BIG]
    mats = {name: to_mat(name, src[name]).astype(BF) for name in names}

    def layer_shards(l, subset):
        return [_shard_view(mats[name][l:l + 1], kinds[name]) for name in subset]

    def as_weight(name, f):
        return f.reshape(G if name == "w_pool" else 1, -1, f.shape[-1])

    assert L == 2
    fulls = _gather_weights(layer_shards(0, GATHER_FIRST) + [_shard_view(taps[None, None], "row")], [kinds[name] for name in GATHER_FIRST] + ["row"])
    ready = {(name, 0): as_weight(name, f) for name, f in zip(GATHER_FIRST, fulls)}
    landing = {}
    taps_all = fulls[-1].reshape(N_CHIPS, -1, cs_c)[:, :n_taps]
    V = {name: w[name] for name in VECTORS}
    V["conv_dw_w"] = taps_all[:, :L * KC].reshape(N_CHIPS, L, KC, cs_c).transpose(1, 2, 0, 3).reshape(L, KC, N_CHIPS * cs_c)
    V["ffn_dw_w"] = taps_all[:, L * KC:].reshape(N_CHIPS, L, KF, cs_f).transpose(1, 2, 0, 3).reshape(L, KF, N_CHIPS * cs_f)

    Bn, S, D = x.shape
    Mn = mem.shape[1]
    dims = (Bn, S, Mn, D, conv_dw_b.shape[1], w_down.shape[1] * N_CHIPS)
    xt = x.reshape(Bn * S, D)
    memf = mem.reshape(Bn * Mn, D)
    mem_n = _rms_fwd(memf, V["mem_norm_g"], "mem_norm")

    class LayerWeights:
        def __init__(self, l):
            self.l = l

        def __getitem__(self, name):
            return ready[(name, self.l)]

    def carried_gather(entries):
        return lambda: _side_gather([layer_shards(lw, [nm])[0] for nm, lw, _ in entries], [kinds[nm] for nm, _, _ in entries],
                                    [rels for _, _, rels in entries], [landing.get((nm, lw)) for nm, lw, _ in entries])

    def carried_pass(group):
        return lambda: _side_gather_pass([landing.pop(t) for t in group], [layer_shards(lw, [nm])[0] for nm, lw in group],
                                         [kinds[nm] for nm, _ in group])

    def on_land(l):
        def handle(key, fulls):
            if (l, key) in FWD_CARRY:
                landing.update({(nm, lw): f for (nm, lw, _), f in zip(FWD_CARRY[(l, key)], fulls)})
            else:
                ready.update({t: as_weight(t[0], f) for t, f in zip(PASS_CARRY[(l, key)], fulls)})
        return handle

    saved, W = [], []
    ht = _rms_fwd(xt, V["mix_norm_g"][0], "l0_mix_norm")
    for l in range(L):
        by_key = {key: carried_gather(entries) for (cl, key), entries in FWD_CARRY.items() if cl == l}
        by_key.update({key: carried_pass(group) for (cl, key), group in PASS_CARRY.items() if cl == l})
        sides = _Sides(by_key, on_land=on_land(l))
        W.append(LayerWeights(l))
        xt, ht, sv = _layer_fwd(xt, ht, mem_n, W[l], V, l, dims, sides, V["mix_norm_g"][l + 1] if l + 1 < L else None)
        saved.append(sv)
    loss, dx, dgf = _loss_bwd(xt, V["final_norm_g"], loss_target.reshape(Bn * S, D), "loss")
    loss = lax.psum(loss[0, 0], ("x", "y", "c"))

    late_names = [name for name in names if name not in EARLY]

    def views(gw, subset, twin):
        out = []
        for name in subset:
            g = gw[name][twin] if gw[name][twin].ndim == 3 else gw[name][twin][None]
            P, R, C = g.shape
            out.append(g.reshape(P, 2, R // 2, C) if kinds[name] == "col" else g.reshape(P, N_CHIPS, 2, R // (2 * N_CHIPS), C))
        return out

    def group_kinds(subset):
        return [kinds[name] for name in subset]

    class Reduction:
        def __init__(self, gw, subset, l, tag, first, table):
            self.gw, self.subset, self.l, self.tag, self.first, self.table = gw, subset, l, tag, first, table

        def sides(self):
            by_key = {self.first: lambda: _side_sibling_exchange(views(self.gw, self.subset, 1), group_kinds(self.subset))}
            by_key.update({key: (lambda names_=names_: _side_chip_exchange([self.pieces[nm] for nm in names_])) for key, names_ in self.table.items()})
            return by_key

        def on_land(self, key, landed):
            if key == self.first:
                self.sums(landed)
            elif key in self.table:
                got[self.l].update(zip(self.table[key], landed))

        def sums(self, lands):
            own, pieces = _chip_sums(views(self.gw, self.subset, 0), lands, group_kinds(self.subset), jc, f"chip_sums_{self.tag}_l{self.l}")
            owns[self.l].update(zip(self.subset, own))
            self.pieces = dict(zip(self.subset, pieces))

    def riding(reductions):
        return _Sides({key: side for r in reductions for key, side in r.sides().items()},
                      on_land=lambda key, landed: [r.on_land(key, landed) for r in reductions])

    dxb, dmem_n = dx, None
    smalls, owns, got = [None] * L, [{} for _ in range(L)], [{} for _ in range(L)]
    late = None
    for l in reversed(range(L)):
        dx, dxb, gw, sm = _layer_bwd_mlp(dx, dxb, saved[l], W[l], V, l, dims, riding([late] if late is not None else []))
        gw_mix = {}
        reductions = [Reduction(gw, EARLY, l, "mlp", "d_att", BWD_CARRY_EARLY)]
        if l == 0:
            reductions += [Reduction(gw_mix, names_, 0, tag, first, table) for tag, names_, first, table in BWD_LAST_LAYER]
        dx, dxb, dmem_n, sm2 = _layer_bwd_mix(dx, dxb, dmem_n, saved[l], mem_n, W[l], V, l, dims, riding(reductions), gw_mix)
        smalls[l] = {**sm, **sm2}
        late = Reduction(gw_mix, late_names, l, "mix", "d_gact", BWD_CARRY_LATE) if l > 0 else None
    last = Reduction(gw_mix, ("w_in",), 0, "in", None, {})
    last.sums(_sibling_exchange(views(gw_mix, last.subset, 1), group_kinds(last.subset), "grad_sibling_exchange_in_l0"))
    grad_x = dx.reshape(Bn, S, D)
    _, _, dgm = _rms_bwd(memf, V["mem_norm_g"], dmem_n, None, "mem_norm_b")
    small = {k: jnp.stack([sm[k] for sm in smalls]) if k in ("conv_dw_w", "ffn_dw_w") else jnp.concatenate([sm[k] for sm in smalls], axis=0)
             for k in smalls[0]}
    small["mem_norm_g"] = dgm
    small["final_norm_g"] = dgf

    small_w = conv_dw_b.shape[1]
    order = VECTORS + ("conv_dw_w", "ffn_dw_w")
    parts = [_rows(small[name], small_w) for name in order]
    counts = [p.shape[0] for p in parts]
    summed, landed_last = _reduce_small(_pack(parts), [last.pieces[name] for name in last.subset])
    got[0].update(zip(last.subset, landed_last))

    keys = ["w_pool_grp" if name == "w_pool" else name for name in names]
    rows3 = lambda t: t.reshape(t.shape[0], -1, t.shape[-1])
    wmv = [[rows3(to_mat(name, d[key])) for name, key in zip(names, keys)] for d in (w, m, v)]
    gshards, updates = None, None
    for l in reversed(range(L)):
        gshards = _final_sums([owns[l][name] for name in names], [got[l][name] for name in names], jc, gshards, l, L, f"final_sums_l{l}")
        gshards = _halves_exchange(gshards, l, f"grad_halves_exchange_l{l}")
        updates = _adamw_layer(wmv[0], [rows3(t) for t in gshards], wmv[1], wmv[2], updates, l, f"adamw_l{l}")
    grads, delta, new_m, new_v = {}, {}, {}, {}
    for i, (name, key) in enumerate(zip(names, keys)):
        Lg, P, _, RH, CS = gshards[i].shape
        grads[key] = from_mat(name, gshards[i].reshape(Lg, P, 2 * RH, CS))
        for d, t in zip((delta, new_m, new_v), updates[3 * i:3 * i + 3]):
            d[key] = from_mat(name, t.reshape(Lg, P, 2 * RH, CS))

    off = 0
    for name, cnt in zip(order, counts):
        t = summed[off:off + cnt]
        off += cnt + (-cnt) % 8
        if name in VECTORS:
            grads[name] = t.reshape(w[name].shape)
        else:
            full = t.reshape(small[name].shape)
            cs = w[name].shape[2]
            grads[name] = lax.dynamic_slice_in_dim(full, j * cs, cs, axis=2)

    vec =[_pack([_rows(d[name], small_w) for name in VECTORS]) for d in (w, grads, m, v)]
    outs = _adamw(*vec, "adamw_vectors")
    off = 0
    for name in VECTORS:
        cnt = w[name].size // small_w
        for d, t in zip((delta, new_m, new_v), outs):
            d[name] = t[off:off + cnt].reshape(w[name].shape)
        off += cnt + (-cnt) % 8
    for name in ("conv_dw_w", "ffn_dw_w"):
        delta[name], new_m[name], new_v[name] = _adamw(w[name], grads[name], m[name], v[name], "adamw_" + name)

    return (loss, grad_x, *[grads[k] for k in WEIGHTS], *[delta[k] for k in WEIGHTS], *[new_m[k] for k in WEIGHTS], *[new_v[k] for k in WEIGHTS])
```

```python
import functools
import math

import jax
import jax.numpy as jnp
from jax import lax
from jax.experimental import pallas as pl
from jax.experimental.pallas import tpu as pltpu

F32 = jnp.float32
BF = jnp.bfloat16
SDS = jax.ShapeDtypeStruct
MESH = pl.DeviceIdType.MESH
ANY = pl.BlockSpec(memory_space=pl.ANY)

EPS = 1e-6
XA_HEADS = 4
POOL_WINDOWS = (2, 4, 8, 16)
N_CHIPS = 4
ADAM_LR, ADAM_B1, ADAM_B2, ADAM_EPS, ADAM_WD, ADAM_STEP = 0.001, 0.9, 0.999, 1e-08, 0.01, 10

LANES = 128
ROW_BLOCK = 512
VMEM_LIMIT = 56 * 1024 * 1024


def _params(*sem):
    return pltpu.CompilerParams(dimension_semantics=sem if sem else None, vmem_limit_bytes=VMEM_LIMIT)


def _tile(n, cap, mult=LANES):
    if n <= cap:
        return n
    for t in range(cap - cap % mult, 0, -mult):
        if n % t == 0:
            return t
    return n


_DN = {"nn": (((1,), (0,)), ((), ())), "nt": (((1,), (1,)), ((), ())), "tn": (((0,), (0,)), ((), ()))}


class _Side:
    def __init__(self, ins, outs, n, make, n_alias=0):
        self.ins, self.outs, self.n, self.make, self.n_alias = list(ins), list(outs), n, make, n_alias


def _call(body, args, *, grid, in_specs, out_specs, out_shape, semantics, name, scratch_shapes=(), side=None, prefetch=(), aliases=None):
    n_pf = len(prefetch)
    aliases = {n_pf + i: o for i, o in (aliases or {}).items()}
    n_in, n_out, n_scr = len(args), len(out_shape), len(scratch_shapes)
    n_si, n_so = (len(side.ins), len(side.outs)) if side is not None else (0, 0)
    if side is not None:
        aliases.update({n_pf + n_in + n_si - side.n_alias + i: n_out + i for i in range(side.n_alias)})

    def carrying(*refs):
        pf, refs = refs[:n_pf], refs[n_pf:]
        ins, s_in = refs[:n_in], refs[n_in:n_in + n_si]
        outs, s_out = refs[n_in + n_si:n_in + n_si + n_out], refs[n_in + n_si + n_out:n_in + n_si + n_out + n_so]
        scr = refs[n_in + n_si + n_out + n_so:]
        if side is None:
            return body(*pf, *ins, *outs, *scr)
        copies = side.make(s_in, s_out, scr[n_scr], scr[n_scr + 1])
        ids = [pl.program_id(d) for d in range(len(grid))]
        first, last = ids[0] == 0, ids[0] == grid[0] - 1
        for d in range(1, len(grid)):
            first, last = first & (ids[d] == 0), last & (ids[d] == grid[d] - 1)

        @pl.when(first)
        def _():
            for cp in copies:
                cp.start()

        body(*pf, *ins, *outs, *scr[:n_scr])

        @pl.when(last)
        def _():
            for cp in copies:
                cp.wait()

    sems = [pltpu.SemaphoreType.DMA((side.n,)), pltpu.SemaphoreType.DMA((side.n,))] if side is not None else []
    outs = pl.pallas_call(
        carrying, grid_spec=pltpu.PrefetchScalarGridSpec(
            num_scalar_prefetch=n_pf, grid=grid, in_specs=list(in_specs) + [ANY] * n_si, out_specs=list(out_specs) + [ANY] * n_so,
            scratch_shapes=list(scratch_shapes) + sems),
        out_shape=list(out_shape) + (side.outs if side is not None else []), input_output_aliases=aliases,
        compiler_params=_params(*(semantics if side is None else ["arbitrary"] * len(grid))), name=name)(
            *prefetch, *args, *(side.ins if side is not None else []))
    return list(outs) if side is None else (list(outs[:n_out]), list(outs[n_out:]))


def _call1(body, args, *, out_spec, out_shape, side=None, **kw):
    got = _call(body, args, out_specs=[out_spec], out_shape=[out_shape], side=side, **kw)
    return got[0] if side is None else (got[0][0], got[1])


MM_VMEM_BUDGET = 40 * 1024 * 1024
MM_STEP_MACS = 2200 * 1024 * 1024
MXU_WIDTH = 256
MM_STEP_COST_BYTES = 1 << 20


def _divisors(n):
    return [t for t in range(LANES, n + 1, LANES) if n % t == 0] or [n]


def _mm_tiles(M, N, K, a_bytes, b_bytes, o_bytes, n_unit=None):
    best = None
    for tk in _divisors(K):
        for tm in _divisors(M):
            for tn in _divisors(N if n_unit is None else n_unit):
                nk = K // tk
                foot = 2 * (tm * tk * a_bytes + tk * tn * b_bytes + tm * tn * o_bytes) + (tm * tn * 4 if nk > 1 else 0)
                if (foot > MM_VMEM_BUDGET or tm * tn * tk > MM_STEP_MACS or tn < min(N if n_unit is None else n_unit, MXU_WIDTH)
                        or tm < min(M, MXU_WIDTH)):
                    continue
                steps = (M // tm) * (N // tn) * nk
                traffic = M * K * a_bytes * (N // tn if nk > 1 else 1) + K * N * b_bytes * (M // tm) + M * N * o_bytes
                exposed = tm * tk * a_bytes + tk * tn * b_bytes + tm * tn * o_bytes
                cost = traffic + exposed + steps * MM_STEP_COST_BYTES + (nk - 1) * M * N * 8
                if best is None or cost < best[0]:
                    best = (cost, tm, tn, tk)
    assert best is not None, (M, N, K)
    return best[1:]


def _mm(a, b, dims, out_dtype, name, res=None, bl=None, side=None, twin=None, b_halves=False, part=None):
    bs = b.shape[1:] if bl is not None or b_halves else b.shape
    if dims == "nn":
        (M, K), (K2, N) = a.shape, bs
    elif dims == "nt":
        (M, K), (N, K2) = a.shape, bs
    else:
        (K, M), (K2, N) = a.shape, bs
    assert K == K2, (name, a.shape, b.shape)
    n_half = N
    if b_halves:
        assert dims == "tn" and bl is None
        N = 2 * n_half
    n_total, n_first, earlier = part if part is not None else (N, 0, None)
    tm, tn, tk = _mm_tiles(M, N, K, a.dtype.itemsize, b.dtype.itemsize, jnp.dtype(out_dtype).itemsize
                           + (res.dtype.itemsize if res is not None else 0) + (jnp.dtype(twin).itemsize if twin is not None else 0),
                           n_unit=math.gcd(n_half, n_first) if b_halves or n_first else None)
    nk = K // tk
    lead = (None,) if bl is not None or b_halves else ()
    pre = (lambda *ix: (bl,) + ix) if bl is not None else (lambda *ix: ix)
    if b_halves:
        per_half = n_half // tn
        pre = lambda k, j: (j // per_half, k, j % per_half)
    if dims == "tn":
        a_spec = pl.BlockSpec((tk, tm), lambda i, j, k: (k, i))
    else:
        a_spec = pl.BlockSpec((tm, tk), lambda i, j, k: (i, k))
    if dims == "nt":
        b_spec = pl.BlockSpec(lead + (tn, tk), lambda i, j, k: pre(j, k))
    else:
        b_spec = pl.BlockSpec(lead + (tk, tn), lambda i, j, k: pre(k, j))
    assert n_first % tn == 0 and (part is None or res is None)
    o_spec = pl.BlockSpec((tm, tn), lambda i, j, k: (i, n_first // tn + j))
    in_specs, args = [a_spec, b_spec], [a, b]
    if res is not None:
        in_specs.append(o_spec)
        args.append(res)
    n_main = len(args)
    n_out = 1 if twin is None else 2
    aliases = None
    if earlier is not None:
        earlier = list(earlier) if twin is not None else [earlier]
        aliases = {n_main + t: t for t in range(n_out)}
        in_specs += [ANY] * n_out
        args += earlier

    def body(*refs):
        refs = refs[:n_main] + refs[len(args):]
        a_ref, b_ref = refs[0], refs[1]
        r_ref = refs[2] if res is not None else None
        o_ref = refs[n_main]
        p = lax.dot_general(a_ref[...].astype(BF), b_ref[...].astype(BF), _DN[dims], preferred_element_type=F32)

        def finish(t):
            if r_ref is not None:
                t = t + r_ref[...]
            o_ref[...] = t.astype(out_dtype)
            if twin is not None:
                refs[n_main + 1][...] = t.astype(twin)

        if nk == 1:
            finish(p)
        else:
            acc = refs[n_main + n_out]
            k = pl.program_id(2)

            @pl.when(k == 0)
            def _():
                acc[...] = p

            @pl.when(k > 0)
            def _():
                acc[...] += p

            @pl.when(k == nk - 1)
            def _():
                finish(acc[...])

    got = _call(body, args, grid=(M // tm, N // tn, nk), in_specs=in_specs, out_specs=[o_spec] * n_out,
                out_shape=[SDS((M, n_total), out_dtype)] + ([SDS((M, n_total), twin)] if twin is not None else []),
                scratch_shapes=[pltpu.VMEM((tm, tn), F32)] if nk > 1 else [], semantics=("parallel", "parallel", "arbitrary"),
                name=name, side=side, aliases=aliases)
    outs, landed = (got, None) if side is None else got
    out = outs[0] if twin is None else (outs[0], outs[1])
    return out if side is None else (out, landed)


def _rms(x, g):
    return x * lax.rsqrt(jnp.mean(x * x, axis=-1, keepdims=True) + EPS) * g


def _ln_silu(x, g, b):
    mu = jnp.mean(x, axis=-1, keepdims=True)
    xc = x - mu
    var = jnp.mean(xc * xc, axis=-1, keepdims=True)
    return jax.nn.silu(xc * lax.rsqrt(var + EPS) * g + b)


def _merge(gc, gp, yc, yp, ps):
    return jax.nn.sigmoid(gc) * yc + jax.nn.sigmoid(gp) * (yp * ps)


def _gated(gate, val):
    return jax.nn.gelu(gate) * val


def _rms_fwd(x, g, name):
    T, D = x.shape
    tb = _tile(T, ROW_BLOCK, 8)

    def body(x_ref, g_ref, o_ref):
        o_ref[...] = _rms(x_ref[...], g_ref[...]).astype(BF)

    row = pl.BlockSpec((tb, D), lambda i: (i, 0))
    return pl.pallas_call(body, grid=(T // tb,), in_specs=[row, pl.BlockSpec((1, D), lambda i: (0, 0))], out_specs=row,
                          out_shape=SDS((T, D), BF), compiler_params=_params("parallel"), name=name)(x, g.reshape(1, D))


def _rms_bwd(x, g, dh, dres, name):
    T, D = x.shape
    tb = _tile(T, ROW_BLOCK, 8)

    def body(*refs):
        if dres is not None:
            x_ref, g_ref, dh_ref, dres_ref, dx_ref, dxb_ref, dg_ref = refs
        else:
            x_ref, g_ref, dh_ref, dx_ref, dxb_ref, dg_ref = refs
        _, vjp = jax.vjp(_rms, x_ref[...], g_ref[...])
        dx, dg = vjp(dh_ref[...].astype(F32))
        if dres is not None:
            dx = dx + dres_ref[...]
        dx_ref[...] = dx
        dxb_ref[...] = dx.astype(BF)

        @pl.when(pl.program_id(0) == 0)
        def _():
            dg_ref[...] = jnp.zeros_like(dg_ref)

        dg_ref[...] += dg

    row = pl.BlockSpec((tb, D), lambda i: (i, 0))
    vec = pl.BlockSpec((1, D), lambda i: (0, 0))
    ins = [x, g.reshape(1, D), dh] + ([dres] if dres is not None else [])
    return pl.pallas_call(
        body, grid=(T // tb,), in_specs=[row, vec, row] + ([row] if dres is not None else []), out_specs=[row, row, vec],
        out_shape=[SDS((T, D), F32), SDS((T, D), BF), SDS((1, D), F32)], compiler_params=_params("arbitrary"), name=name)(*ins)


def _row_tile(M, K, N, per_row_bytes):
    fixed = K * N * 2
    fit = [t for t in _divisors(M) if fixed + 2 * t * per_row_bytes <= MM_VMEM_BUDGET and t * K * N <= MM_STEP_MACS]
    return max(fit) if fit else min(_divisors(M))


def _mm_rms_fwd(a, b, res, g, name, side=None):
    M, K = a.shape
    N = b.shape[2]
    tm = _row_tile(M, K, N, K * 2 + N * (4 + 4 + 2))

    def body(a_ref, b_ref, r_ref, g_ref, x_ref, h_ref):
        x = r_ref[...] + lax.dot_general(a_ref[...], b_ref[...], _DN["nn"], preferred_element_type=F32)
        x_ref[...] = x
        h_ref[...] = _rms(x, g_ref[...]).astype(BF)

    row = pl.BlockSpec((tm, N), lambda i: (i, 0))
    return _call(body, (a, b, res, g.reshape(1, N)), grid=(M // tm,),
                 in_specs=[pl.BlockSpec((tm, K), lambda i: (i, 0)), pl.BlockSpec((None, K, N), lambda i: (0, 0, 0), pipeline_mode=pl.Buffered(1)), row,
                           pl.BlockSpec((1, N), lambda i: (0, 0))],
                 out_specs=[row, row], out_shape=[SDS((M, N), F32), SDS((M, N), BF)], semantics=("parallel",), name=name, side=side)


def _mm_rms_bwd(a_parts, b, x, g, dres, name, side=None):
    n_a = len(a_parts)
    M = a_parts[0].shape[-2]
    N, K = b.shape[1:]
    assert K == sum(p.shape[-1] * (p.shape[0] if p.ndim == 3 else 1) for p in a_parts)
    tm = _row_tile(M, K, N, K * 2 + N * (4 + 4 + 4 + 2))

    def body(*refs):
        a_refs, (b_ref, x_ref, g_ref, r_ref, dx_ref, dxb_ref, dg_ref) = refs[:n_a], refs[n_a:]
        dh, col = None, 0
        for p, a_ref in zip(a_parts, a_refs):
            for blk in ([a_ref[h] for h in range(p.shape[0])] if p.ndim == 3 else [a_ref[...]]):
                t = lax.dot_general(blk, b_ref[:, col:col + p.shape[-1]], _DN["nt"], preferred_element_type=F32)
                dh = t if dh is None else dh + t
                col += p.shape[-1]
        _, vjp = jax.vjp(_rms, x_ref[...], g_ref[...])
        dx, dg = vjp(dh)
        dx = dx + r_ref[...]
        dx_ref[...] = dx
        dxb_ref[...] = dx.astype(BF)

        @pl.when(pl.program_id(0) == 0)
        def _():
            dg_ref[...] = jnp.zeros_like(dg_ref)

        dg_ref[...] += dg

    row = pl.BlockSpec((tm, N), lambda i: (i, 0))
    vec = pl.BlockSpec((1, N), lambda i: (0, 0))
    a_specs = [pl.BlockSpec((p.shape[0], tm, p.shape[2]), lambda i: (0, i, 0)) if p.ndim == 3 else pl.BlockSpec((tm, p.shape[1]), lambda i: (i, 0))
               for p in a_parts]
    return _call(
        body, (*a_parts, b, x, g.reshape(1, N), dres), grid=(M // tm,),
        in_specs=a_specs + [pl.BlockSpec((None, N, K), lambda i: (0, 0, 0), pipeline_mode=pl.Buffered(1)), row, vec, row],
        out_specs=[row, row, vec], out_shape=[SDS((M, N), F32), SDS((M, N), BF), SDS((1, N), F32)],
        semantics=("arbitrary",), name=name, side=side)


def _loss_bwd(x, g, target, name):
    T, D = x.shape
    tb = _tile(T, ROW_BLOCK, 8)
    nb = T // tb

    def body(x_ref, g_ref, t_ref, loss_ref, dx_ref, dg_ref, acc):
        i = pl.program_id(0)
        y, vjp = jax.vjp(_rms, x_ref[...], g_ref[...])
        err = y - t_ref[...]
        dx, dg = vjp(err * (1.0 / D))
        dx_ref[...] = dx

        @pl.when(i == 0)
        def _():
            dg_ref[...] = jnp.zeros_like(dg_ref)
            acc[...] = jnp.zeros_like(acc)

        dg_ref[...] += dg
        acc[...] += jnp.sum(err * err, axis=0, keepdims=True)

        @pl.when(i == nb - 1)
        def _():
            loss_ref[...] = jnp.full(loss_ref.shape, (0.5 / D) * jnp.sum(acc[...]), F32)

    row = pl.BlockSpec((tb, D), lambda i: (i, 0))
    vec = pl.BlockSpec((1, D), lambda i: (0, 0))
    return pl.pallas_call(
        body, grid=(nb,), in_specs=[row, vec, row], out_specs=[pl.BlockSpec((1, LANES), lambda i: (0, 0)), row, vec],
        out_shape=[SDS((1, LANES), F32), SDS((T, D), F32), SDS((1, D), F32)], scratch_shapes=[pltpu.VMEM((1, D), F32)],
        compiler_params=_params("arbitrary"), name=name)(x, g.reshape(1, D), target)


def _ln_silu_mm(cv, g, b, w, name):
    T, C = cv.shape
    D = w.shape[2]
    tb = _tile(T, 2 * ROW_BLOCK, 8)

    def body(x_ref, g_ref, b_ref, w_ref, y1_ref, y_ref):
        y1 = _ln_silu(x_ref[...], g_ref[...], b_ref[...]).astype(BF)
        y1_ref[...] = y1
        y_ref[...] = lax.dot_general(y1, w_ref[...], _DN["nn"], preferred_element_type=F32).astype(BF)

    row = pl.BlockSpec((tb, C), lambda i: (i, 0))
    vec = pl.BlockSpec((1, C), lambda i: (0, 0))
    return pl.pallas_call(
        body, grid=(T // tb,), in_specs=[row, vec, vec, pl.BlockSpec((None, C, D), lambda i: (0, 0, 0), pipeline_mode=pl.Buffered(1))],
        out_specs=[row, pl.BlockSpec((tb, D), lambda i: (i, 0))], out_shape=[SDS((T, C), BF), SDS((T, D), BF)],
        compiler_params=_params("parallel"), name=name)(cv, g.reshape(1, C), b.reshape(1, C), w)


def _mm_ln_silu_bwd(dyc, w, cv, g, b, name, side=None):
    T, C = cv.shape
    D = w.shape[2]
    tb = _tile(T, 2 * ROW_BLOCK, 8)

    def body(d_ref, w_ref, x_ref, g_ref, b_ref, dx_ref, dg_ref, db_ref):
        dy1 = lax.dot_general(d_ref[...], w_ref[...], _DN["nt"], preferred_element_type=F32)
        _, vjp = jax.vjp(_ln_silu, x_ref[...], g_ref[...], b_ref[...])
        dx, dg, db = vjp(dy1)
        dx_ref[...] = dx

        @pl.when(pl.program_id(0) == 0)
        def _():
            dg_ref[...] = jnp.zeros_like(dg_ref)
            db_ref[...] = jnp.zeros_like(db_ref)

        dg_ref[...] += dg
        db_ref[...] += db

    row = pl.BlockSpec((tb, C), lambda i: (i, 0))
    vec = pl.BlockSpec((1, C), lambda i: (0, 0))
    return _call(
        body, (dyc, w, cv, g.reshape(1, C), b.reshape(1, C)), grid=(T // tb,),
        in_specs=[pl.BlockSpec((tb, D), lambda i: (i, 0)), pl.BlockSpec((None, C, D), lambda i: (0, 0, 0), pipeline_mode=pl.Buffered(1)), row, vec, vec],
        out_specs=[row, vec, vec], out_shape=[SDS((T, C), F32), SDS((1, C), F32), SDS((1, C), F32)], semantics=("arbitrary",),
        name=name, side=side)


def _merge_fwd(proj, yc, yp, ps, C, name, side=None):
    T, D = yc.shape
    tb = _tile(T, ROW_BLOCK, 8)
    nj = D // C

    def body(gc_ref, gp_ref, yc_ref, yp_ref, ps_ref, o_ref):
        o_ref[...] = _merge(gc_ref[...], gp_ref[...], yc_ref[...].astype(F32), yp_ref[...].astype(F32), ps_ref[...]).astype(BF)

    blk = pl.BlockSpec((tb, C), lambda i, j: (i, j))
    return _call1(
        body, (proj, proj, yc, yp, ps.reshape(1, D)), grid=(T // tb, nj),
        in_specs=[pl.BlockSpec((tb, C), lambda i, j: (i, 3 + j)), pl.BlockSpec((tb, C), lambda i, j: (i, 3 + nj + j)), blk, blk,
                  pl.BlockSpec((1, C), lambda i, j: (0, j))],
        out_spec=blk, out_shape=SDS((T, D), BF), semantics=("parallel", "parallel"), name=name, side=side)


def _merge_bwd(proj, yc, yp, ps, dm, C, name, side=None):
    T, D = yc.shape
    tb = _tile(T, ROW_BLOCK, 8)
    nj = D // C

    def body(gc_ref, gp_ref, yc_ref, yp_ref, ps_ref, dm_ref, dg_ref, dyc_ref, dyp_ref, dps_ref):
        _, vjp = jax.vjp(_merge, gc_ref[...], gp_ref[...], yc_ref[...].astype(F32), yp_ref[...].astype(F32), ps_ref[...])
        dgc, dgp, dyc, dyp, dps = vjp(dm_ref[...].astype(F32))
        dg_ref[0] = dgc.astype(BF)
        dg_ref[1] = dgp.astype(BF)
        dyc_ref[...] = dyc.astype(BF)
        dyp_ref[...] = dyp.astype(BF)

        @pl.when(pl.program_id(1) == 0)
        def _():
            dps_ref[...] = jnp.zeros_like(dps_ref)

        dps_ref[...] += dps

    blk = pl.BlockSpec((tb, C), lambda j, i: (i, j))
    vec = pl.BlockSpec((1, C), lambda j, i: (0, j))
    return _call(
        body, (proj, proj, yc, yp, ps.reshape(1, D), dm), grid=(nj, T // tb),
        in_specs=[pl.BlockSpec((tb, C), lambda j, i: (i, 3 + j)), pl.BlockSpec((tb, C), lambda j, i: (i, 3 + nj + j)), blk, blk, vec, blk],
        out_specs=[pl.BlockSpec((2, tb, C), lambda j, i: (0, i, j)), blk, blk, vec],
        out_shape=[SDS((2, T, D), BF), SDS((T, D), BF), SDS((T, D), BF), SDS((1, D), F32)],
        semantics=("parallel", "arbitrary"), name=name, side=side)


def _shd(v, s, rows):
    if s == 0:
        return v
    return jnp.where(rows >= s, pltpu.roll(v, s, 0), 0.0)


def _shu(v, s, rows):
    if s == 0:
        return v
    n = v.shape[0]
    return jnp.where(rows < n - s, pltpu.roll(v, n - s, 0), 0.0)


def _glu_conv_fwd(proj, w, b, Bn, S, C, name, side=None):
    K = w.shape[0]
    sl = min(LANES, C)
    ns = C // sl

    def body(a_ref, gl_ref, w_ref, b_ref, o_ref):
        y0 = a_ref[...] * jax.nn.sigmoid(gl_ref[...])
        rows = lax.broadcasted_iota(jnp.int32, y0.shape, 0)
        acc = jnp.zeros_like(y0) + b_ref[...]
        for k in range(K):
            acc = acc + w_ref[k:k + 1, :] * _shd(y0, K - 1 - k, rows)
        o_ref[...] = acc

    return _call1(
        body, (proj, proj, w, b.reshape(1, C)), grid=(Bn, ns),
        in_specs=[pl.BlockSpec((S, sl), lambda bi, j: (bi, j)), pl.BlockSpec((S, sl), lambda bi, j: (bi, ns + j)),
                  pl.BlockSpec((K, sl), lambda bi, j: (0, j)), pl.BlockSpec((1, sl), lambda bi, j: (0, j))],
        out_spec=pl.BlockSpec((S, sl), lambda bi, j: (bi, j)), out_shape=SDS((Bn * S, C), F32),
        semantics=("parallel", "parallel"), name=name, side=side)


def _glu_conv_bwd(proj, w, dcv, Bn, S, C, name, side=None):
    K = w.shape[0]
    sl = min(LANES, C)
    ns = C // sl

    def body(a_ref, gl_ref, w_ref, d_ref, dagl_ref, dw_ref, db_ref):
        a = a_ref[...]
        sg = jax.nn.sigmoid(gl_ref[...])
        y0 = a * sg
        d = d_ref[...]
        rows = lax.broadcasted_iota(jnp.int32, y0.shape, 0)

        @pl.when(pl.program_id(1) == 0)
        def _():
            dw_ref[...] = jnp.zeros_like(dw_ref)
            db_ref[...] = jnp.zeros_like(db_ref)

        dy0 = jnp.zeros_like(y0)
        for k in range(K):
            s = K - 1 - k
            dw_ref[k:k + 1, :] += jnp.sum(d * _shd(y0, s, rows), axis=0, keepdims=True)
            dy0 = dy0 + w_ref[k:k + 1, :] * _shu(d, s, rows)
        db_ref[...] += jnp.sum(d, axis=0, keepdims=True)
        dagl_ref[0] = (dy0 * sg).astype(BF)
        dagl_ref[1] = (dy0 * a * sg * (1.0 - sg)).astype(BF)

    blk = pl.BlockSpec((S, sl), lambda j, bi: (bi, j))
    return _call(
        body, (proj, proj, w, dcv), grid=(ns, Bn),
        in_specs=[blk, pl.BlockSpec((S, sl), lambda j, bi: (bi, ns + j)), pl.BlockSpec((K, sl), lambda j, bi: (0, j)), blk],
        out_specs=[pl.BlockSpec((2, S, sl), lambda j, bi: (0, bi, j)), pl.BlockSpec((K, sl), lambda j, bi: (0, j)),
                   pl.BlockSpec((1, sl), lambda j, bi: (0, j))],
        out_shape=[SDS((2, Bn * S, C), BF), SDS((K, C), F32), SDS((1, C), F32)],
        semantics=("parallel", "arbitrary"), name=name, side=side)


def _pool_z(u, g, rows):
    s2 = u + _shd(u, 1, rows)
    s4 = s2 + _shd(s2, 2, rows)
    s8 = s4 + _shd(s4, 4, rows)
    s16 = s8 + _shd(s8, 8, rows)
    sw = jnp.where(g == 0, s2, jnp.where(g == 1, s4, jnp.where(g == 2, s8, s16)))
    cnt = jnp.minimum(rows + 1, POOL_WINDOWS[0] << g).astype(F32)
    return sw / cnt - u, cnt


def _pool_fwd(proj, wpt, l, Bn, S, C, D, name):
    G = len(POOL_WINDOWS)
    gd, go = C // G, D // G

    def body(u_ref, w_ref, o_ref):
        g = pl.program_id(1)
        u = u_ref[...]
        rows = lax.broadcasted_iota(jnp.int32, u.shape, 0)
        zp, _ = _pool_z(u, g, rows)
        o_ref[...] = lax.dot_general(zp.astype(BF), w_ref[...], _DN["nt"], preferred_element_type=F32).astype(BF)

    return pl.pallas_call(
        body, grid=(Bn, G),
        in_specs=[pl.BlockSpec((S, gd), lambda bi, g: (bi, 2 * G + g)), pl.BlockSpec((None, go, gd), lambda bi, g: (l * G + g, 0, 0))],
        out_specs=pl.BlockSpec((S, go), lambda bi, g: (bi, g)), out_shape=SDS((Bn * S, D), BF),
        compiler_params=_params("parallel", "parallel"), name=name)(proj, wpt)


def _pool_bwd(proj, wpt, dyp, l, Bn, S, C, D, name):
    G = len(POOL_WINDOWS)
    gd, go = C // G, D // G

    def body(u_ref, w_ref, d_ref, du_ref, dw_ref):
        g = pl.program_id(0)
        u = u_ref[...]
        rows = lax.broadcasted_iota(jnp.int32, u.shape, 0)
        zp, cnt = _pool_z(u, g, rows)
        d = d_ref[...]
        dzp = lax.dot_general(d, w_ref[...], _DN["nn"], preferred_element_type=F32)

        @pl.when(pl.program_id(1) == 0)
        def _():
            dw_ref[...] = jnp.zeros_like(dw_ref)

        dw_ref[...] += lax.dot_general(d, zp.astype(BF), _DN["tn"], preferred_element_type=F32)
        dsw = dzp / cnt
        zero = jnp.zeros_like(dsw)
        d16 = jnp.where(g == 3, dsw, zero)
        d8 = jnp.where(g == 2, dsw, zero) + d16 + _shu(d16, 8, rows)
        d4 = jnp.where(g == 1, dsw, zero) + d8 + _shu(d8, 4, rows)
        d2 = jnp.where(g == 0, dsw, zero) + d4 + _shu(d4, 2, rows)
        d1 = d2 + _shu(d2, 1, rows)
        du_ref[...] = (d1 - dzp).astype(BF)

    return pl.pallas_call(
        body, grid=(G, Bn),
        in_specs=[pl.BlockSpec((S, gd), lambda g, bi: (bi, 2 * G + g)), pl.BlockSpec((None, go, gd), lambda g, bi: (l * G + g, 0, 0)),
                  pl.BlockSpec((S, go), lambda g, bi: (bi, g))],
        out_specs=[pl.BlockSpec((S, gd), lambda g, bi: (bi, g)), pl.BlockSpec((None, go, gd), lambda g, bi: (g, 0, 0))],
        out_shape=[SDS((Bn * S, C), BF), SDS((G, go, gd), F32)],
        compiler_params=_params("parallel", "arbitrary"), name=name)(proj, wpt, dyp)


def _ffn_conv(u, w_ref, rows):
    K = w_ref.shape[0]
    acc = w_ref[K - 1:K, :] * u
    for k in range(K - 1):
        acc = acc + w_ref[k:k + 1, :] * _shd(u, K - 1 - k, rows)
    return acc


def _ffn_cb(F):
    return _tile(F, 256)


def _ffn_act_fwd(up0, w, Bn, S, F, name, side=None):
    cb = _ffn_cb(F)
    nj = F // cb

    def body(g_ref, v_ref, wg_ref, wv_ref, o_ref):
        rows = lax.broadcasted_iota(jnp.int32, g_ref.shape, 0)
        o_ref[...] = _gated(_ffn_conv(g_ref[...], wg_ref, rows), _ffn_conv(v_ref[...], wv_ref, rows)).astype(BF)

    K = w.shape[0]
    return _call1(
        body, (up0, up0, w, w), grid=(Bn, nj),
        in_specs=[pl.BlockSpec((S, cb), lambda bi, j: (bi, j)), pl.BlockSpec((S, cb), lambda bi, j: (bi, nj + j)),
                  pl.BlockSpec((K, cb), lambda bi, j: (0, j)), pl.BlockSpec((K, cb), lambda bi, j: (0, nj + j))],
        out_spec=pl.BlockSpec((S, cb), lambda bi, j: (bi, j)), out_shape=SDS((Bn * S, F), BF),
        semantics=("parallel", "parallel"), name=name, side=side)


SUBLANES = 8
FFN_HALO = SUBLANES
FFN_ROWS = 128
GELU_C0, GELU_C1 = 0.7978845608028654, 0.044715


def _gelu_and_grad(x):
    x2 = x * x
    t = jnp.tanh(GELU_C0 * (x + GELU_C1 * (x2 * x)))
    cdf = 0.5 * (1.0 + t)
    return x * cdf, cdf + (0.5 * GELU_C0) * x * (1.0 - t * t) * (1.0 + (3.0 * GELU_C1) * x2)


def _ffn_act_bwd(up0, w, dg, Bn, S, F, name, side=None):
    cb = min(LANES, F)
    nj = F // cb
    K = w.shape[0]
    rc = FFN_ROWS if S % FFN_ROWS == 0 else S
    win = rc + 2 * FFN_HALO
    assert K - 1 <= FFN_HALO and rc % SUBLANES == 0

    def body(g_ref, v_ref, wg_ref, wv_ref, d_ref, do_ref, dwg_ref, dwv_ref, gp, vp, dp):
        for pad, src in ((gp, g_ref), (vp, v_ref), (dp, d_ref)):
            pad[0:FFN_HALO, :] = jnp.zeros((FFN_HALO, cb), F32)
            pad[FFN_HALO + S:, :] = jnp.zeros((FFN_HALO, cb), F32)
            pad[FFN_HALO:FFN_HALO + S, :] = src[...].astype(F32)
        wg = [wg_ref[k:k + 1, :] for k in range(K)]
        wv = [wv_ref[k:k + 1, :] for k in range(K)]

        def taps(u):
            return [pltpu.roll(u, K - 1 - k, 0) for k in range(K - 1)] + [u]

        def conv(us, ws):
            acc = ws[K - 1] * us[K - 1]
            for k in range(K - 1):
                acc = acc + ws[k] * us[k]
            return acc

        def conv_t(dc, ws):
            acc = ws[K - 1] * dc
            for k in range(K - 1):
                acc = acc + ws[k] * pltpu.roll(dc, win - (K - 1 - k), 0)
            return acc

        def fold(t):
            acc = t[FFN_HALO:FFN_HALO + SUBLANES]
            for i in range(1, rc // SUBLANES):
                acc = acc + t[FFN_HALO + SUBLANES * i:FFN_HALO + SUBLANES * (i + 1)]
            return acc

        def chunk(c, sums):
            r0 = pl.multiple_of(c * rc, SUBLANES)
            gs, vs, d = taps(gp[pl.ds(r0, win), :]), taps(vp[pl.ds(r0, win), :]), dp[pl.ds(r0, win), :]
            ge, dge = _gelu_and_grad(conv(gs, wg))
            dgc = d * conv(vs, wv) * dge
            dvc = d * ge
            do_ref[0, pl.ds(r0, rc), :] = conv_t(dgc, wg)[FFN_HALO:FFN_HALO + rc].astype(BF)
            do_ref[1, pl.ds(r0, rc), :] = conv_t(dvc, wv)[FFN_HALO:FFN_HALO + rc].astype(BF)
            new = [fold(dc * u) for us, dc in ((gs, dgc), (vs, dvc)) for u in us]
            return tuple(a + b for a, b in zip(sums, new))

        sums = lax.fori_loop(0, S // rc, chunk, tuple(jnp.zeros((SUBLANES, cb), F32) for _ in range(2 * K)))

        @pl.when(pl.program_id(1) == 0)
        def _():
            dwg_ref[...] = jnp.zeros_like(dwg_ref)
            dwv_ref[...] = jnp.zeros_like(dwv_ref)

        for k in range(K):
            dwg_ref[k:k + 1, :] += jnp.sum(sums[k], axis=0, keepdims=True)
            dwv_ref[k:k + 1, :] += jnp.sum(sums[K + k], axis=0, keepdims=True)

    blk = pl.BlockSpec((S, cb), lambda j, bi: (bi, j))
    wblk = pl.BlockSpec((K, cb), lambda j, bi: (0, j))
    return _call(
        body, (up0, up0, w, w, dg), grid=(nj, Bn),
        in_specs=[blk, pl.BlockSpec((S, cb), lambda j, bi: (bi, nj + j)), wblk, pl.BlockSpec((K, cb), lambda j, bi: (0, nj + j)), blk],
        out_specs=[pl.BlockSpec((2, S, cb), lambda j, bi: (0, bi, j)), wblk, wblk],
        out_shape=[SDS((2, Bn * S, F), BF), SDS((K, F), F32), SDS((K, F), F32)],
        scratch_shapes=[pltpu.VMEM((S + 2 * FFN_HALO, cb), F32)] * 3, semantics=("parallel", "arbitrary"), name=name, side=side)


def _softmax_rows(q, k, scale):
    sc = lax.dot_general(q, k, _DN["nt"], preferred_element_type=F32) * scale
    e = jnp.exp(sc - jnp.max(sc, axis=-1, keepdims=True))
    return e / jnp.sum(e, axis=-1, keepdims=True)


def _attn_ts(S):
    return _tile(S, 1024, 8)


def _attn_fwd(q, kv, Bn, S, Mn, D, name, side=None):
    H = XA_HEADS
    dh = D // H
    ts = _attn_ts(S)
    nsb = S // ts
    scale = dh ** -0.5

    def body(q_ref, k_ref, v_ref, o_ref):
        p = _softmax_rows(q_ref[...], k_ref[...], scale)
        o_ref[...] = lax.dot_general(p.astype(BF), v_ref[...], _DN["nn"], preferred_element_type=F32).astype(BF)

    qblk = pl.BlockSpec((ts, dh), lambda bi, h, s: (bi * nsb + s, h))
    return _call1(
        body, (q, kv, kv), grid=(Bn, H, nsb),
        in_specs=[qblk, pl.BlockSpec((Mn, dh), lambda bi, h, s: (bi, h)), pl.BlockSpec((Mn, dh), lambda bi, h, s: (bi, H + h))],
        out_spec=qblk, out_shape=SDS((Bn * S, D), BF), semantics=("parallel", "parallel", "parallel"), name=name, side=side)


def _attn_bwd(q, kv, datt, Bn, S, Mn, D, name):
    H = XA_HEADS
    dh = D // H
    ts = _attn_ts(S)
    nsb = S // ts
    scale = dh ** -0.5

    def body(q_ref, k_ref, v_ref, do_ref, dq_ref, dk_ref, dv_ref):
        q, k, v, do = q_ref[...], k_ref[...], v_ref[...], do_ref[...]
        p = _softmax_rows(q, k, scale)
        dp = lax.dot_general(do, v, _DN["nt"], preferred_element_type=F32)
        ds = (p * (dp - jnp.sum(dp * p, axis=-1, keepdims=True)) * scale).astype(BF)
        dq_ref[...] = lax.dot_general(ds, k, _DN["nn"], preferred_element_type=F32).astype(BF)

        @pl.when(pl.program_id(2) == 0)
        def _():
            dk_ref[...] = jnp.zeros_like(dk_ref)
            dv_ref[...] = jnp.zeros_like(dv_ref)

        dk_ref[...] += lax.dot_general(ds, q, _DN["tn"], preferred_element_type=F32)
        dv_ref[...] += lax.dot_general(p.astype(BF), do, _DN["tn"], preferred_element_type=F32)

    qblk = pl.BlockSpec((ts, dh), lambda bi, h, s: (bi * nsb + s, h))
    kblk = pl.BlockSpec((Mn, dh), lambda bi, h, s: (bi, h))
    return pl.pallas_call(
        body, grid=(Bn, H, nsb),
        in_specs=[qblk, kblk, pl.BlockSpec((Mn, dh), lambda bi, h, s: (bi, H + h)), qblk],
        out_specs=[qblk, kblk, kblk], out_shape=[SDS((Bn * S, D), BF), SDS((Bn * Mn, D), F32), SDS((Bn * Mn, D), F32)],
        compiler_params=_params("parallel", "parallel", "arbitrary"), name=name)(q, kv, kv, datt)


class _Sides:
    def __init__(self, by_key=None, on_land=None):
        self.by_key, self.landed, self.on_land = dict(by_key or {}), {}, on_land

    def run(self, key, fn, *args, **kw):
        side = self.by_key.get(key)
        if side is None:
            return fn(*args, **kw)
        out, self.landed[key] = fn(*args, side=side() if callable(side) else side, **kw)
        if self.on_land is not None:
            self.on_land(key, self.landed[key])
        return out

    def mm(self, key, *args, **kw):
        return self.run(key, _mm, *args, **kw)


def _layer_fwd(x, h, mem_n, W, V, l, dims, sides, next_g):
    Bn, S, Mn, D, C, F = dims
    n = f"l{l}_"
    proj = sides.mm("proj", h, W["w_in"], "nn", F32, n + "proj", bl=0)
    cv = sides.run("glu_conv", _glu_conv_fwd, proj, V["conv_dw_w"][l], V["conv_dw_b"][l], Bn, S, C, n + "glu_conv")
    yc1, yc = _ln_silu_mm(cv, V["conv_ln_g"][l], V["conv_ln_b"][l], W["w_conv_out"], n + "conv_out")
    yp = _pool_fwd(proj, W["w_pool"], 0, Bn, S, C, D, n + "pool")
    merged = sides.run("merge", _merge_fwd, proj, yc, yp, V["pool_scale"][l], C, n + "merge")
    x1, hq = sides.run("out_proj", _mm_rms_fwd, merged, W["w_out"], x, V["xattn_norm_g"][l], n + "out_proj")
    q = sides.mm("q_proj", hq, W["w_q"], "nn", BF, n + "q_proj", bl=0)
    kv = _mm(mem_n, W["w_kv"], "nn", BF, n + "kv_proj", bl=0)
    att = sides.run("attn", _attn_fwd, q, kv, Bn, S, Mn, D, n + "attn")
    x2, hf = sides.run("o_proj", _mm_rms_fwd, att, W["w_o"], x1, V["ffn_norm_g"][l], n + "o_proj")
    up0 = sides.mm("up_proj", hf, W["w_up"], "nn", F32, n + "up_proj", bl=0)
    gact = sides.run("ffn_act", _ffn_act_fwd, up0, V["ffn_dw_w"][l], Bn, S, F, n + "ffn_act")
    if next_g is not None:
        x3, h3 = sides.run("down_proj", _mm_rms_fwd, gact, W["w_down"], x2, next_g, n + "down_proj")
    else:
        x3, h3 = sides.mm("down_proj", gact, W["w_down"], "nn", F32, n + "down_proj", res=x2, bl=0), None
    return x3, h3, dict(x=x, h=h, proj=proj, cv=cv, yc1=yc1, yc=yc, yp=yp, merged=merged, x1=x1, hq=hq, q=q, kv=kv, att=att, x2=x2,
                        hf=hf, up0=up0, gact=gact)


def _layer_bwd_mlp(dx, dxb, sv, W, V, l, dims, sides):
    Bn, S, Mn, D, C, F = dims
    n = f"l{l}_b_"
    gw, sm = {}, {}
    dgact = sides.mm("d_gact", dxb, W["w_down"], "nt", BF, n + "d_gact", bl=0)
    gw["w_down"] = sides.mm("dw_down", sv["gact"], dxb, "tn", F32, n + "dw_down", twin=BF)
    dup0, dwg, dwv = sides.run("ffn_act_b", _ffn_act_bwd, sv["up0"], V["ffn_dw_w"][l], dgact, Bn, S, F, n + "ffn_act")
    sm["ffn_dw_w"] = jnp.concatenate([dwg, dwv], axis=1)
    dx2, dx2b, sm["ffn_norm_g"] = _mm_rms_bwd([dup0], W["w_up"], sv["x2"], V["ffn_norm_g"][l], dx, n + "d_hf")
    gw["w_up"] = sides.mm("dw_up", sv["hf"], dup0, "tn", F32, n + "dw_up", twin=BF, b_halves=True)
    return dx2, dx2b, gw, sm


def _layer_bwd_mix(dx2, dx2b, dmem_n, sv, mem_n, W, V, l, dims, sides, gw):
    Bn, S, Mn, D, C, F = dims
    n = f"l{l}_b_"
    sm = {}
    datt = sides.mm("d_att", dx2b, W["w_o"], "nt", BF, n + "d_att", bl=0)
    gw["w_o"] = _mm(sv["att"], dx2b, "tn", F32, n + "dw_o", twin=BF)
    dq, dk, dv = _attn_bwd(sv["q"], sv["kv"], datt, Bn, S, Mn, D, n + "attn")
    dkv = jnp.concatenate([dk, dv], axis=1)
    gw["w_kv"] = _mm(mem_n, dkv, "tn", F32, n + "dw_kv", twin=BF)
    dmem_n = _mm(dkv, W["w_kv"], "nt", F32, n + "d_mem", res=dmem_n, bl=0)
    dx1, dx1b, sm["xattn_norm_g"] = _mm_rms_bwd([dq], W["w_q"], sv["x1"], V["xattn_norm_g"][l], dx2, n + "d_hq")
    gw["w_q"] = _mm(sv["hq"], dq, "tn", F32, n + "dw_q", twin=BF)
    dmerged = sides.mm("d_merged", dx1b, W["w_out"], "nt", BF, n + "d_merged", bl=0)
    gw["w_out"] = _mm(sv["merged"], dx1b, "tn", F32, n + "dw_out", twin=BF)
    dgates, dyc, dyp, sm["pool_scale"] = sides.run("merge_b", _merge_bwd, sv["proj"], sv["yc"], sv["yp"], V["pool_scale"][l], dmerged, C, n + "merge")
    du, dwp = _pool_bwd(sv["proj"], W["w_pool"], dyp, 0, Bn, S, C, D, n + "pool")
    gw["w_pool"] = (dwp, dwp.astype(BF))
    gw["w_conv_out"] = _mm(sv["yc1"], dyc, "tn", F32, n + "dw_conv_out", twin=BF)
    dcv, sm["conv_ln_g"], sm["conv_ln_b"] = sides.run("ln_silu_b", _mm_ln_silu_bwd, dyc, W["w_conv_out"], sv["cv"], V["conv_ln_g"][l],
                                                      V["conv_ln_b"][l], n + "d_yc1")
    dagl, sm["conv_dw_w"], sm["conv_dw_b"] = sides.run("glu_conv_b", _glu_conv_bwd, sv["proj"], V["conv_dw_w"][l], dcv, Bn, S, C, n + "glu_conv")
    dx, dxb, sm["mix_norm_g"] = sides.run("d_h", _mm_rms_bwd, [dagl, du, dgates], W["w_in"], sv["x"], V["mix_norm_g"][l], dx1, n + "d_h")
    n_in = W["w_in"].shape[2]
    part = _mm(sv["h"], dagl, "tn", F32, n + "dw_in_conv", twin=BF, b_halves=True, part=(n_in, 0, None))
    part = _mm(sv["h"], du, "tn", F32, n + "dw_in_pool", twin=BF, part=(n_in, 2 * C, part))
    gw["w_in"] = sides.mm("dw_in", sv["h"], dgates, "tn", F32, n + "dw_in", twin=BF, b_halves=True, part=(n_in, 3 * C, part))
    return dx, dxb, dmem_n, sm


BIG = (("w_in", "col"), ("w_conv_out", "col"), ("w_pool", "row"), ("w_out", "row"), ("w_q", "row"), ("w_kv", "col"),
       ("w_o", "row"), ("w_up", "col"), ("w_down", "row"))
ALL_RELS = (1, 2, 3)
GATHER_FIRST = ("w_in", "w_conv_out", "w_pool", "w_out", "w_q", "w_o")
FWD_CARRY = {
    (0, "proj"): (("w_up", 0, (1, 2)),),
    (0, "glu_conv"): (("w_up", 0, (3,)),),
    (0, "merge"): (("w_kv", 0, ALL_RELS),),
    (0, "q_proj"): (("w_down", 0, (1, 2)),),
    (0, "attn"): (("w_down", 0, (3,)),),
    (0, "up_proj"): (("w_in", 1, ALL_RELS), ("w_conv_out", 1, ALL_RELS), ("w_pool", 1, ALL_RELS), ("w_o", 1, ALL_RELS)),
    (0, "ffn_act"): (("w_out", 1, ALL_RELS), ("w_q", 1, ALL_RELS), ("w_kv", 1, ALL_RELS)),
    (1, "proj"): (("w_up", 1, (1, 2)),),
    (1, "glu_conv"): (("w_up", 1, (3,)),),
    (1, "merge"): (("w_down", 1, (1, 2)),),
    (1, "attn"): (("w_down", 1, (3,)),),
}
PASS_CARRY = {
    (0, "out_proj"): (("w_kv", 0),),
    (0, "o_proj"): (("w_up", 0), ("w_down", 0)),
    (0, "down_proj"): (("w_in", 1), ("w_conv_out", 1), ("w_pool", 1), ("w_out", 1), ("w_q", 1), ("w_kv", 1), ("w_o", 1)),
    (1, "o_proj"): (("w_up", 1), ("w_down", 1)),
}
EARLY = ("w_down", "w_up")
BWD_CARRY_EARLY = {"merge_b": ("w_down",), "glu_conv_b": ("w_up",)}
BWD_CARRY_LATE = {"ffn_act_b": ("w_in", "w_conv_out", "w_pool", "w_out", "w_q", "w_kv", "w_o")}
BWD_LAST_LAYER = (("att", ("w_o", "w_kv", "w_q"), "d_merged", {"d_h": ("w_o", "w_kv", "w_q")}),
                  ("tok", ("w_out", "w_pool", "w_conv_out"), "ln_silu_b", {"dw_in": ("w_out", "w_pool", "w_conv_out")}))


def _place():
    xi, yi, ci = lax.axis_index("x"), lax.axis_index("y"), lax.axis_index("c")
    return xi, yi, ci, 2 * xi + yi


def _chip_peer(xi, yi, ci, r):
    return (xi ^ (r >> 1), yi ^ (r & 1), ci)


def _full_shard(ref, kind, k, cs):
    if kind == "col":
        return ref.at[:, :, :, :, pl.ds(pl.multiple_of(k * cs, cs), cs)]
    return ref.at[:, :, k]


def _gather_weights(shards, kinds):
    n = len(shards)
    outs = []
    for s, kind in zip(shards, kinds):
        L, P, _, RH, CS = s.shape
        outs.append(SDS((L, P, 2, RH, CS * N_CHIPS) if kind == "col" else (L, P, N_CHIPS, 2, RH, CS), s.dtype))
    per = 7

    def body(*refs):
        srcs, fulls, (ssem, rsem) = refs[:n], refs[n:2 * n], refs[2 * n:]
        xi, yi, ci, j = _place()
        sib = (xi, yi, 1 - ci)

        def piece(i, k, c):
            kind, cs = kinds[i], shards[i].shape[-1]
            if kind == "col":
                return fulls[i].at[:, :, c, :, pl.ds(pl.multiple_of(k * cs, cs), cs)]
            return fulls[i].at[:, :, k, c]

        def copy(i, slot, src, dst, dev):
            return pltpu.make_async_remote_copy(src_ref=src, dst_ref=dst, send_sem=ssem.at[per * i + slot], recv_sem=rsem.at[per * i + slot],
                                                device_id=dev, device_id_type=MESH)

        own, first, passed = [], [], []
        for i in range(n):
            for r in (1, 2, 3):
                first.append(copy(i, r - 1, srcs[i].at[:, :, ci], piece(i, j, ci), _chip_peer(xi, yi, ci, r)))
                first[-1].start()
        for i in range(n):
            own.append(copy(i, 6, srcs[i], _full_shard(fulls[i], kinds[i], j, shards[i].shape[-1]), sib))
            own[-1].start()
        for i in range(n):
            for r in (1, 2, 3):
                got = piece(i, j ^ r, ci)
                copy(i, r - 1, got, got, sib).wait_recv()
                passed.append(copy(i, 2 + r, got, got, sib))
                passed[-1].start()
        for i in range(n):
            for r in (1, 2, 3):
                got = piece(i, j ^ r, 1 - ci)
                copy(i, 2 + r, got, got, sib).wait_recv()
        for cp in own:
            cp.wait()
        for cp in first + passed:
            cp.wait_send()

    return pl.pallas_call(
        body, in_specs=[ANY] * n, out_specs=[ANY] * n, out_shape=outs,
        scratch_shapes=[pltpu.SemaphoreType.DMA((per * n,)), pltpu.SemaphoreType.DMA((per * n,))], name="gather_weights")(*shards)


def _full_sds(s, kind):
    L, P, _, RH, CS = s.shape
    return SDS((L, P, 2, RH, CS * N_CHIPS) if kind == "col" else (L, P, N_CHIPS, 2, RH, CS), s.dtype)


def _gather_piece(full, kind, cs, k, c):
    if kind == "col":
        return full.at[:, :, c, :, pl.ds(pl.multiple_of(k * cs, cs), cs)]
    return full.at[:, :, k, c]


def _side_gather(shards, kinds, rels, fulls):
    n = len(shards)

    def make(srcs, outs, ssem, rsem):
        xi, yi, ci, j = _place()
        return [pltpu.make_async_remote_copy(
            src_ref=srcs[i].at[:, :, ci], dst_ref=_gather_piece(outs[i], kinds[i], shards[i].shape[-1], j, ci), send_sem=ssem.at[3 * i + r - 1],
            recv_sem=rsem.at[3 * i + r - 1], device_id=_chip_peer(xi, yi, ci, r), device_id_type=MESH) for i in range(n) for r in rels[i]]

    prior = [f for f in fulls if f is not None]
    assert len(prior) in (0, n)
    return _Side(list(shards) + prior, [_full_sds(s, k) for s, k in zip(shards, kinds)], 3 * n, make, n_alias=len(prior))


def _side_gather_pass(fulls, shards, kinds):
    n = len(fulls)

    def make(srcs, outs, ssem, rsem):
        xi, yi, ci, j = _place()
        sib = (xi, yi, 1 - ci)
        cps = []
        for i in range(n):
            cs = shards[i].shape[-1]
            for r in (1, 2, 3):
                got = _gather_piece(outs[i], kinds[i], cs, j ^ r, ci)
                cps.append(pltpu.make_async_remote_copy(src_ref=got, dst_ref=got, send_sem=ssem.at[4 * i + r - 1], recv_sem=rsem.at[4 * i + r - 1],
                                                        device_id=sib, device_id_type=MESH))
            cps.append(pltpu.make_async_remote_copy(src_ref=srcs[i], dst_ref=_full_shard(outs[i], kinds[i], j, cs), send_sem=ssem.at[4 * i + 3],
                                                    recv_sem=rsem.at[4 * i + 3], device_id=sib, device_id_type=MESH))
        return cps

    return _Side(list(shards) + list(fulls), [SDS(f.shape, f.dtype) for f in fulls], 4 * n, make, n_alias=n)


def _sibling_exchange(gviews, kinds, name):
    n = len(gviews)
    outs = [SDS(g.shape[:1] + g.shape[2:] if kind == "col" else g.shape[:2] + g.shape[3:], g.dtype) for g, kind in zip(gviews, kinds)]

    def body(*refs):
        gs, lands, (ssem, rsem) = refs[:n], refs[n:2 * n], refs[2 * n:]
        xi, yi, ci, _ = _place()
        cps = []
        for i in range(n):
            src = gs[i].at[:, 1 - ci] if kinds[i] == "col" else gs[i].at[:, :, 1 - ci]
            cps.append(pltpu.make_async_remote_copy(src_ref=src, dst_ref=lands[i], send_sem=ssem.at[i], recv_sem=rsem.at[i],
                                                    device_id=(xi, yi, 1 - ci), device_id_type=MESH))
            cps[-1].start()
        for cp in cps:
            cp.wait()

    return pl.pallas_call(body, in_specs=[ANY] * n, out_specs=[ANY] * n, out_shape=outs,
                          scratch_shapes=[pltpu.SemaphoreType.DMA((n,)), pltpu.SemaphoreType.DMA((n,))], name=name)(*gviews)


def _side_sibling_exchange(gviews, kinds):
    outs = [SDS(g.shape[:1] + g.shape[2:] if kind == "col" else g.shape[:2] + g.shape[3:], g.dtype) for g, kind in zip(gviews, kinds)]

    def make(gs, lands, ssem, rsem):
        xi, yi, ci, _ = _place()
        return [pltpu.make_async_remote_copy(src_ref=gs[i].at[:, 1 - ci] if kinds[i] == "col" else gs[i].at[:, :, 1 - ci], dst_ref=lands[i],
                                             send_sem=ssem.at[i], recv_sem=rsem.at[i], device_id=(xi, yi, 1 - ci), device_id_type=MESH)
                for i in range(len(gs))]

    return _Side(gviews, outs, len(gviews), make)


def _chip_sums(gs, lands, kinds, jc, name):
    n = len(gs)
    args, in_specs, out_specs, out_shape = [], [], [], []
    for g, land, kind in zip(gs, lands, kinds):
        if kind == "col":
            P, _, RH, C = g.shape
            CS = C // N_CHIPS
            in_specs += [pl.BlockSpec((P, None, RH, CS), lambda r, jc: (0, jc[1], 0, jc[0] ^ r)),
                         pl.BlockSpec((P, RH, CS), lambda r, jc: (0, 0, jc[0] ^ r))]
        else:
            P, _, _, RH, CS = g.shape
            in_specs += [pl.BlockSpec((P, None, None, RH, CS), lambda r, jc: (0, jc[0] ^ r, jc[1], 0, 0)),
                         pl.BlockSpec((P, None, RH, CS), lambda r, jc: (0, jc[0] ^ r, 0, 0))]
        args += [g, land]
        out_specs += [pl.BlockSpec((P, RH, CS), lambda r, jc: (0, 0, 0)), pl.BlockSpec((None, P, RH, CS), lambda r, jc: (r, 0, 0, 0))]
        out_shape += [SDS((P, RH, CS), F32), SDS((N_CHIPS, P, RH, CS), BF)]

    def body(jc_ref, *refs):
        ins, outs = refs[:2 * n], refs[2 * n:]
        for i in range(n):
            s = ins[2 * i][...] + ins[2 * i + 1][...].astype(F32)
            outs[2 * i + 1][...] = s.astype(BF)

            @pl.when(pl.program_id(0) == 0)
            def _():
                outs[2 * i][...] = s

    outs = _call(body, args, grid=(N_CHIPS,), in_specs=in_specs, out_specs=out_specs, out_shape=out_shape, semantics=("arbitrary",),
                 name=name, prefetch=(jc,))
    return outs[0::2], outs[1::2]


def _chip_exchange_copies(srcs, lands, ssem, rsem):
    xi, yi, ci, _ = _place()
    return [pltpu.make_async_remote_copy(src_ref=srcs[i].at[r], dst_ref=lands[i].at[r], send_sem=ssem.at[3 * i + r - 1],
                                         recv_sem=rsem.at[3 * i + r - 1], device_id=_chip_peer(xi, yi, ci, r), device_id_type=MESH)
            for i in range(len(srcs)) for r in (1, 2, 3)]


def _side_chip_exchange(pieces):
    return _Side(pieces, [SDS(p.shape, p.dtype) for p in pieces], 3 * len(pieces), _chip_exchange_copies)


FINAL_SUM_STEPS = 2


def _final_sums(owns, lands, jc, shards, l, L, name, side=None):
    n = len(owns)
    args, in_specs, out_specs, out_shape = [], [], [], []
    for own, land in zip(owns, lands):
        P, RH, CS = own.shape
        hr = RH // FINAL_SUM_STEPS
        in_specs += [pl.BlockSpec((P, hr, CS), lambda h, jc: (0, h, 0))]
        in_specs += [pl.BlockSpec((None, P, hr, CS), functools.partial(lambda r, h, jc: (r, 0, h, 0), r)) for r in (1, 2, 3)]
        args += [own, land, land, land]
        out_specs.append(pl.BlockSpec((None, P, None, hr, CS), lambda h, jc: (l, 0, jc[1], h, 0)))
        out_shape.append(SDS((L, P, 2, RH, CS), F32))
    aliases = None
    if shards is not None:
        aliases = {4 * n + i: i for i in range(n)}
        in_specs += [ANY] * n
        args += list(shards)

    def body(jc_ref, *refs):
        outs = refs[len(args):]
        for i in range(n):
            o, a, b, c = (refs[4 * i + t][...] for t in range(4))
            outs[i][...] = ((o + a.astype(F32)) + b.astype(F32)) + c.astype(F32)

    return _call(body, args, grid=(FINAL_SUM_STEPS,), in_specs=in_specs, out_specs=out_specs, out_shape=out_shape, semantics=("arbitrary",),
                 name=name, prefetch=(jc,), aliases=aliases, side=side)


def _halves_exchange(shards, l, name):
    n = len(shards)

    def body(*refs):
        outs, (ssem, rsem) = refs[n:2 * n], refs[2 * n:]
        xi, yi, ci, _ = _place()
        cps = []
        for i in range(n):
            mine = outs[i].at[l, :, ci]
            cps.append(pltpu.make_async_remote_copy(src_ref=mine, dst_ref=mine, send_sem=ssem.at[i], recv_sem=rsem.at[i],
                                                    device_id=(xi, yi, 1 - ci), device_id_type=MESH))
            cps[-1].start()
        for i in range(n):
            land = outs[i].at[l, :, 1 - ci]
            pltpu.make_async_remote_copy(src_ref=land, dst_ref=land, send_sem=ssem.at[i], recv_sem=rsem.at[i],
                                         device_id=(xi, yi, 1 - ci), device_id_type=MESH).wait_recv()
        for cp in cps:
            cp.wait_send()

    return pl.pallas_call(body, in_specs=[ANY] * n, out_specs=[ANY] * n, out_shape=[SDS(s.shape, s.dtype) for s in shards],
                          input_output_aliases={i: i for i in range(n)},
                          scratch_shapes=[pltpu.SemaphoreType.DMA((n,)), pltpu.SemaphoreType.DMA((n,))], name=name)(*shards)


def _reduce_small(part, pieces):
    NR, Wd = part.shape
    ND = 2 * N_CHIPS
    n = len(pieces)

    def body(p_ref, *refs):
        srcs, o_ref, lands, (land, ssem, rsem, xs, xr) = refs[:n], refs[n], refs[n + 1:2 * n + 1], refs[2 * n + 1:]
        exchange = _chip_exchange_copies(srcs, lands, xs, xr)
        for cp in exchange:
            cp.start()
        xi, yi, ci, j = _place()
        me = 2 * j + ci
        land[me] = p_ref[...]
        cps = []
        for rr in range(1, ND):
            dev = (xi ^ (rr >> 2), yi ^ ((rr >> 1) & 1), ci ^ (rr & 1))
            cps.append(pltpu.make_async_remote_copy(src_ref=p_ref, dst_ref=land.at[me], send_sem=ssem.at[rr - 1], recv_sem=rsem.at[rr - 1],
                                                    device_id=dev, device_id_type=MESH))
            cps[-1].start()
        for rr in range(1, ND):
            got = land.at[me ^ rr]
            pltpu.make_async_remote_copy(src_ref=got, dst_ref=got, send_sem=ssem.at[rr - 1], recv_sem=rsem.at[rr - 1],
                                         device_id=(xi, yi, ci), device_id_type=MESH).wait_recv()
        acc = land[0]
        for d in range(1, ND):
            acc = acc + land[d]
        o_ref[...] = acc
        for cp in cps:
            cp.wait_send()
        for cp in exchange:
            cp.wait()

    vm = pl.BlockSpec(memory_space=pltpu.VMEM)
    outs = pl.pallas_call(
        body, in_specs=[vm] + [ANY] * n, out_specs=[vm] + [ANY] * n, out_shape=[SDS((NR, Wd), F32)] + [SDS(p.shape, p.dtype) for p in pieces],
        scratch_shapes=[pltpu.VMEM((ND, NR, Wd), F32), pltpu.SemaphoreType.DMA((ND - 1,)), pltpu.SemaphoreType.DMA((ND - 1,)),
                        pltpu.SemaphoreType.DMA((3 * n,)), pltpu.SemaphoreType.DMA((3 * n,))],
        name="small_grad_allreduce")(part, *pieces)
    return outs[0], list(outs[1:])


def _adamw_update(w_ref, g_ref, m_ref, v_ref, d_ref, mo_ref, vo_ref):
    g = g_ref[...]
    m = ADAM_B1 * m_ref[...] + (1.0 - ADAM_B1) * g
    v = ADAM_B2 * v_ref[...] + (1.0 - ADAM_B2) * jnp.square(g)
    m_hat = m / (1.0 - ADAM_B1 ** ADAM_STEP)
    v_hat = v / (1.0 - ADAM_B2 ** ADAM_STEP)
    d_ref[...] = -ADAM_LR * (m_hat / (jnp.sqrt(v_hat) + ADAM_EPS) + ADAM_WD * w_ref[...])
    mo_ref[...] = m
    vo_ref[...] = v


ADAMW_STEPS = 8


def _adamw_layer(ws, gs, ms, vs, prev, l, name, side=None):
    n = len(ws)
    args, in_specs, out_specs, out_shape = [], [], [], []
    for w, g, m, v in zip(ws, gs, ms, vs):
        L, R, C = w.shape
        blk = pl.BlockSpec((None, R // ADAMW_STEPS, C), lambda i: (l, i, 0))
        in_specs += [blk] * 4
        args += [w, g, m, v]
        out_specs += [blk] * 3
        out_shape += [SDS((L, R, C), F32)] * 3
    aliases = None
    if prev is not None:
        aliases = {4 * n + i: i for i in range(3 * n)}
        in_specs += [ANY] * (3 * n)
        args += list(prev)

    def body(*refs):
        outs = refs[len(args):]
        for i in range(n):
            _adamw_update(*refs[4 * i:4 * i + 4], *outs[3 * i:3 * i + 3])

    return _call(body, args, grid=(ADAMW_STEPS,), in_specs=in_specs, out_specs=out_specs, out_shape=out_shape, semantics=("parallel",),
                 name=name, aliases=aliases, side=side)


def _adamw(w, g, m, v, name):
    shape = w.shape
    C = shape[-1]
    R = w.size // C
    tb = _tile(R, max(8, (1 << 18) // C), 8)
    body = functools.partial(_adamw_update)
    blk = pl.BlockSpec((tb, C), lambda i: (i, 0))
    outs = pl.pallas_call(body, grid=(R // tb,), in_specs=[blk] * 4, out_specs=[blk] * 3, out_shape=[SDS((R, C), F32)] * 3,
                          compiler_params=_params("parallel"), name=name)(*[t.reshape(R, C) for t in (w, g, m, v)])
    return [t.reshape(shape) for t in outs]


WEIGHTS = ("mix_norm_g", "w_in", "conv_dw_w", "conv_dw_b", "conv_ln_g", "conv_ln_b", "w_conv_out", "w_pool_grp", "pool_scale", "w_out",
           "xattn_norm_g", "mem_norm_g", "w_q", "w_kv", "w_o", "ffn_norm_g", "w_up", "ffn_dw_w", "w_down", "final_norm_g")
VECTORS = ("mix_norm_g", "conv_dw_b", "conv_ln_g", "conv_ln_b", "pool_scale", "xattn_norm_g", "mem_norm_g", "ffn_norm_g", "final_norm_g")


def _shard_view(t, kind):
    L, P, R, C = t.shape
    return t.reshape(L, P, 2, R // 2, C)


def _rows(t, width):
    return t.reshape(-1, width)


def _pack(parts):
    return jnp.concatenate([jnp.pad(p, ((0, (-p.shape[0]) % 8), (0, 0))) for p in parts], axis=0)


def kernel(x, mem, mix_norm_g, w_in, conv_dw_w, conv_dw_b, conv_ln_g, conv_ln_b, w_conv_out, w_pool_grp, pool_scale, w_out, xattn_norm_g, mem_norm_g, w_q, w_kv, w_o, ffn_norm_g, w_up, ffn_dw_w, w_down, final_norm_g, loss_target, m_mix_norm_g, m_w_in, m_conv_dw_w, m_conv_dw_b, m_conv_ln_g, m_conv_ln_b, m_w_conv_out, m_w_pool_grp, m_pool_scale, m_w_out, m_xattn_norm_g, m_mem_norm_g, m_w_q, m_w_kv, m_w_o, m_ffn_norm_g, m_w_up, m_ffn_dw_w, m_w_down, m_final_norm_g, v_mix_norm_g, v_w_in, v_conv_dw_w, v_conv_dw_b, v_conv_ln_g, v_conv_ln_b, v_w_conv_out, v_w_pool_grp, v_pool_scale, v_w_out, v_xattn_norm_g, v_mem_norm_g, v_w_q, v_w_kv, v_w_o, v_ffn_norm_g, v_w_up, v_ffn_dw_w, v_w_down, v_final_norm_g):
    w = dict(mix_norm_g=mix_norm_g, w_in=w_in, conv_dw_w=conv_dw_w, conv_dw_b=conv_dw_b, conv_ln_g=conv_ln_g, conv_ln_b=conv_ln_b,
             w_conv_out=w_conv_out, w_pool_grp=w_pool_grp, pool_scale=pool_scale, w_out=w_out, xattn_norm_g=xattn_norm_g,
             mem_norm_g=mem_norm_g, w_q=w_q, w_kv=w_kv, w_o=w_o, ffn_norm_g=ffn_norm_g, w_up=w_up, ffn_dw_w=ffn_dw_w, w_down=w_down,
             final_norm_g=final_norm_g)
    m = dict(zip(WEIGHTS, (m_mix_norm_g, m_w_in, m_conv_dw_w, m_conv_dw_b, m_conv_ln_g, m_conv_ln_b, m_w_conv_out, m_w_pool_grp, m_pool_scale,
                           m_w_out, m_xattn_norm_g, m_mem_norm_g, m_w_q, m_w_kv, m_w_o, m_ffn_norm_g, m_w_up, m_ffn_dw_w, m_w_down, m_final_norm_g)))
    v = dict(zip(WEIGHTS, (v_mix_norm_g, v_w_in, v_conv_dw_w, v_conv_dw_b, v_conv_ln_g, v_conv_ln_b, v_w_conv_out, v_w_pool_grp, v_pool_scale,
                           v_w_out, v_xattn_norm_g, v_mem_norm_g, v_w_q, v_w_kv, v_w_o, v_ffn_norm_g, v_w_up, v_ffn_dw_w, v_w_down, v_final_norm_g)))
    xi, yi, ci, j = _place()
    jc = jnp.stack([j, ci]).astype(jnp.int32)
    L = w_in.shape[0]
    G = len(POOL_WINDOWS)
    kinds = dict(BIG)

    def to_mat(name, t):
        if name == "w_pool":
            return jnp.swapaxes(t, 2, 3)
        return t[:, None]

    def from_mat(name, t):
        if name == "w_pool":
            return jnp.swapaxes(t, 2, 3)
        return t[:, 0]

    src = {name: w["w_pool_grp" if name == "w_pool" else name] for name, _ in BIG}

    KC, cs_c = conv_dw_w.shape[1], conv_dw_w.shape[2]
    KF, cs_f = ffn_dw_w.shape[1], ffn_dw_w.shape[2]
    taps = jnp.concatenate([conv_dw_w.reshape(L * KC, cs_c), ffn_dw_w.reshape(L * KF * (cs_f // cs_c), cs_c)], axis=0)
    n_taps = taps.shape[0]
    taps = jnp.pad(taps, ((0, (-n_taps) % 16), (0, 0)))
    names = [name for name, _ in BIG]
    mats = {name: to_mat(name, src[name]).astype(BF) for name in names}

    def layer_shards(l, subset):
        return [_shard_view(mats[name][l:l + 1], kinds[name]) for name in subset]

    def as_weight(name, f):
        return f.reshape(G if name == "w_pool" else 1, -1, f.shape[-1])

    assert L == 2
    fulls = _gather_weights(layer_shards(0, GATHER_FIRST) + [_shard_view(taps[None, None], "row")], [kinds[name] for name in GATHER_FIRST] + ["row"])
    ready = {(name, 0): as_weight(name, f) for name, f in zip(GATHER_FIRST, fulls)}
    landing = {}
    taps_all = fulls[-1].reshape(N_CHIPS, -1, cs_c)[:, :n_taps]
    V = {name: w[name] for name in VECTORS}
    V["conv_dw_w"] = taps_all[:, :L * KC].reshape(N_CHIPS, L, KC, cs_c).transpose(1, 2, 0, 3).reshape(L, KC, N_CHIPS * cs_c)
    V["ffn_dw_w"] = taps_all[:, L * KC:].reshape(N_CHIPS, L, KF, cs_f).transpose(1, 2, 0, 3).reshape(L, KF, N_CHIPS * cs_f)

    Bn, S, D = x.shape
    Mn = mem.shape[1]
    dims = (Bn, S, Mn, D, conv_dw_b.shape[1], w_down.shape[1] * N_CHIPS)
    xt = x.reshape(Bn * S, D)
    memf = mem.reshape(Bn * Mn, D)
    mem_n = _rms_fwd(memf, V["mem_norm_g"], "mem_norm")

    class LayerWeights:
        def __init__(self, l):
            self.l = l

        def __getitem__(self, name):
            return ready[(name, self.l)]

    def carried_gather(entries):
        return lambda: _side_gather([layer_shards(lw, [nm])[0] for nm, lw, _ in entries], [kinds[nm] for nm, _, _ in entries],
                                    [rels for _, _, rels in entries], [landing.get((nm, lw)) for nm, lw, _ in entries])

    def carried_pass(group):
        return lambda: _side_gather_pass([landing.pop(t) for t in group], [layer_shards(lw, [nm])[0] for nm, lw in group],
                                         [kinds[nm] for nm, _ in group])

    def on_land(l):
        def handle(key, fulls):
            if (l, key) in FWD_CARRY:
                landing.update({(nm, lw): f for (nm, lw, _), f in zip(FWD_CARRY[(l, key)], fulls)})
            else:
                ready.update({t: as_weight(t[0], f) for t, f in zip(PASS_CARRY[(l, key)], fulls)})
        return handle

    saved, W = [], []
    ht = _rms_fwd(xt, V["mix_norm_g"][0], "l0_mix_norm")
    for l in range(L):
        by_key = {key: carried_gather(entries) for (cl, key), entries in FWD_CARRY.items() if cl == l}
        by_key.update({key: carried_pass(group) for (cl, key), group in PASS_CARRY.items() if cl == l})
        sides = _Sides(by_key, on_land=on_land(l))
        W.append(LayerWeights(l))
        xt, ht, sv = _layer_fwd(xt, ht, mem_n, W[l], V, l, dims, sides, V["mix_norm_g"][l + 1] if l + 1 < L else None)
        saved.append(sv)
    loss, dx, dgf = _loss_bwd(xt, V["final_norm_g"], loss_target.reshape(Bn * S, D), "loss")
    loss = lax.psum(loss[0, 0], ("x", "y", "c"))

    late_names = [name for name in names if name not in EARLY]

    def views(gw, subset, twin):
        out = []
        for name in subset:
            g = gw[name][twin] if gw[name][twin].ndim == 3 else gw[name][twin][None]
            P, R, C = g.shape
            out.append(g.reshape(P, 2, R // 2, C) if kinds[name] == "col" else g.reshape(P, N_CHIPS, 2, R // (2 * N_CHIPS), C))
        return out

    def group_kinds(subset):
        return [kinds[name] for name in subset]

    class Reduction:
        def __init__(self, gw, subset, l, tag, first, table):
            self.gw, self.subset, self.l, self.tag, self.first, self.table = gw, subset, l, tag, first, table

        def sides(self):
            by_key = {self.first: lambda: _side_sibling_exchange(views(self.gw, self.subset, 1), group_kinds(self.subset))}
            by_key.update({key: (lambda names_=names_: _side_chip_exchange([self.pieces[nm] for nm in names_])) for key, names_ in self.table.items()})
            return by_key

        def on_land(self, key, landed):
            if key == self.first:
                self.sums(landed)
            elif key in self.table:
                got[self.l].update(zip(self.table[key], landed))

        def sums(self, lands):
            own, pieces = _chip_sums(views(self.gw, self.subset, 0), lands, group_kinds(self.subset), jc, f"chip_sums_{self.tag}_l{self.l}")
            owns[self.l].update(zip(self.subset, own))
            self.pieces = dict(zip(self.subset, pieces))

    def riding(reductions):
        return _Sides({key: side for r in reductions for key, side in r.sides().items()},
                      on_land=lambda key, landed: [r.on_land(key, landed) for r in reductions])

    dxb, dmem_n = dx, None
    smalls, owns, got = [None] * L, [{} for _ in range(L)], [{} for _ in range(L)]
    late = None
    for l in reversed(range(L)):
        dx, dxb, gw, sm = _layer_bwd_mlp(dx, dxb, saved[l], W[l], V, l, dims, riding([late] if late is not None else []))
        gw_mix = {}
        reductions = [Reduction(gw, EARLY, l, "mlp", "d_att", BWD_CARRY_EARLY)]
        if l == 0:
            reductions += [Reduction(gw_mix, names_, 0, tag, first, table) for tag, names_, first, table in BWD_LAST_LAYER]
        dx, dxb, dmem_n, sm2 = _layer_bwd_mix(dx, dxb, dmem_n, saved[l], mem_n, W[l], V, l, dims, riding(reductions), gw_mix)
        smalls[l] = {**sm, **sm2}
        late = Reduction(gw_mix, late_names, l, "mix", "d_gact", BWD_CARRY_LATE) if l > 0 else None
    last = Reduction(gw_mix, ("w_in",), 0, "in", None, {})
    last.sums(_sibling_exchange(views(gw_mix, last.subset, 1), group_kinds(last.subset), "grad_sibling_exchange_in_l0"))
    grad_x = dx.reshape(Bn, S, D)
    _, _, dgm = _rms_bwd(memf, V["mem_norm_g"], dmem_n, None, "mem_norm_b")
    small = {k: jnp.stack([sm[k] for sm in smalls]) if k in ("conv_dw_w", "ffn_dw_w") else jnp.concatenate([sm[k] for sm in smalls], axis=0)
             for k in smalls[0]}
    small["mem_norm_g"] = dgm
    small["final_norm_g"] = dgf

    small_w = conv_dw_b.shape[1]
    order = VECTORS + ("conv_dw_w", "ffn_dw_w")
    parts = [_rows(small[name], small_w) for name in order]
    counts = [p.shape[0] for p in parts]
    summed, landed_last = _reduce_small(_pack(parts), [last.pieces[name] for name in last.subset])
    got[0].update(zip(last.subset, landed_last))

    keys = ["w_pool_grp" if name == "w_pool" else name for name in names]
    rows3 = lambda t: t.reshape(t.shape[0], -1, t.shape[-1])
    wmv = [[rows3(to_mat(name, d[key])) for name, key in zip(names, keys)] for d in (w, m, v)]
    gshards, updates = None, None
    for l in reversed(range(L)):
        gshards = _final_sums([owns[l][name] for name in names], [got[l][name] for name in names], jc, gshards, l, L, f"final_sums_l{l}")
        gshards = _halves_exchange(gshards, l, f"grad_halves_exchange_l{l}")
        updates = _adamw_layer(wmv[0], [rows3(t) for t in gshards], wmv[1], wmv[2], updates, l, f"adamw_l{l}")
    grads, delta, new_m, new_v = {}, {}, {}, {}
    for i, (name, key) in enumerate(zip(names, keys)):
        Lg, P, _, RH, CS = gshards[i].shape
        grads[key] = from_mat(name, gshards[i].reshape(Lg, P, 2 * RH, CS))
        for d, t in zip((delta, new_m, new_v), updates[3 * i:3 * i + 3]):
            d[key] = from_mat(name, t.reshape(Lg, P, 2 * RH, CS))

    off = 0
    for name, cnt in zip(order, counts):
        t = summed[off:off + cnt]
        off += cnt + (-cnt) % 8
        if name in VECTORS:
            grads[name] = t.reshape(w[name].shape)
        else:
            full = t.reshape(small[name].shape)
            cs = w[name].shape[2]
            grads[name] = lax.dynamic_slice_in_dim(full, j * cs, cs, axis=2)

    vec =[_pack([_rows(d[name], small_w) for name in VECTORS]) for d in (w, grads, m, v)]
    outs = _adamw(*vec, "adamw_vectors")
    off = 0
    for name in VECTORS:
        cnt = w[name].size // small_w
        for d, t in zip((delta, new_m, new_v), outs):
            d[name] = t[off:off + cnt].reshape(w[name].shape)
        off += cnt + (-cnt) % 8
    for name in ("conv_dw_w", "ffn_dw_w"):
        delta[name], new_m[name], new_v[name] = _adamw(w[name], grads[name], m[name], v[name], "adamw_" + name)

    return (loss, grad_x, *[grads[k] for k in WEIGHTS], *[delta[k] for k in WEIGHTS], *[new_m[k] for k in WEIGHTS], *[new_v[k] for k in WEIGHTS])
```

```python
import functools
import math

import jax
import jax.numpy as jnp
from jax import lax
from jax.experimental import pallas as pl
from jax.experimental.pallas import tpu as pltpu

F32 = jnp.float32
BF = jnp.bfloat16
SDS = jax.ShapeDtypeStruct
MESH = pl.DeviceIdType.MESH
ANY = pl.BlockSpec(memory_space=pl.ANY)

EPS = 1e-6
XA_HEADS = 4
POOL_WINDOWS = (2, 4, 8, 16)
N_CHIPS = 4
ADAM_LR, ADAM_B1, ADAM_B2, ADAM_EPS, ADAM_WD, ADAM_STEP = 0.001, 0.9, 0.999, 1e-08, 0.01, 10

LANES = 128
ROW_BLOCK = 512
VMEM_LIMIT = 56 * 1024 * 1024


def _params(*sem):
    return pltpu.CompilerParams(dimension_semantics=sem if sem else None, vmem_limit_bytes=VMEM_LIMIT)


def _tile(n, cap, mult=LANES):
    if n <= cap:
        return n
    for t in range(cap - cap % mult, 0, -mult):
        if n % t == 0:
            return t
    return n


_DN = {"nn": (((1,), (0,)), ((), ())), "nt": (((1,), (1,)), ((), ())), "tn": (((0,), (0,)), ((), ()))}


class _Side:
    def __init__(self, ins, outs, n, make, n_alias=0):
        self.ins, self.outs, self.n, self.make, self.n_alias = list(ins), list(outs), n, make, n_alias


def _call(body, args, *, grid, in_specs, out_specs, out_shape, semantics, name, scratch_shapes=(), side=None, prefetch=(), aliases=None):
    n_pf = len(prefetch)
    aliases = {n_pf + i: o for i, o in (aliases or {}).items()}
    n_in, n_out, n_scr = len(args), len(out_shape), len(scratch_shapes)
    n_si, n_so = (len(side.ins), len(side.outs)) if side is not None else (0, 0)
    if side is not None:
        aliases.update({n_pf + n_in + n_si - side.n_alias + i: n_out + i for i in range(side.n_alias)})

    def carrying(*refs):
        pf, refs = refs[:n_pf], refs[n_pf:]
        ins, s_in = refs[:n_in], refs[n_in:n_in + n_si]
        outs, s_out = refs[n_in + n_si:n_in + n_si + n_out], refs[n_in + n_si + n_out:n_in + n_si + n_out + n_so]
        scr = refs[n_in + n_si + n_out + n_so:]
        if side is None:
            return body(*pf, *ins, *outs, *scr)
        copies = side.make(s_in, s_out, scr[n_scr], scr[n_scr + 1])
        ids = [pl.program_id(d) for d in range(len(grid))]
        first, last = ids[0] == 0, ids[0] == grid[0] - 1
        for d in range(1, len(grid)):
            first, last = first & (ids[d] == 0), last & (ids[d] == grid[d] - 1)

        @pl.when(first)
        def _():
            for cp in copies:
                cp.start()

        body(*pf, *ins, *outs, *scr[:n_scr])

        @pl.when(last)
        def _():
            for cp in copies:
                cp.wait()

    sems = [pltpu.SemaphoreType.DMA((side.n,)), pltpu.SemaphoreType.DMA((side.n,))] if side is not None else []
    outs = pl.pallas_call(
        carrying, grid_spec=pltpu.PrefetchScalarGridSpec(
            num_scalar_prefetch=n_pf, grid=grid, in_specs=list(in_specs) + [ANY] * n_si, out_specs=list(out_specs) + [ANY] * n_so,
            scratch_shapes=list(scratch_shapes) + sems),
        out_shape=list(out_shape) + (side.outs if side is not None else []), input_output_aliases=aliases,
        compiler_params=_params(*(semantics if side is None else ["arbitrary"] * len(grid))), name=name)(
            *prefetch, *args, *(side.ins if side is not None else []))
    return list(outs) if side is None else (list(outs[:n_out]), list(outs[n_out:]))


def _call1(body, args, *, out_spec, out_shape, side=None, **kw):
    got = _call(body, args, out_specs=[out_spec], out_shape=[out_shape], side=side, **kw)
    return got[0] if side is None else (got[0][0], got[1])


MM_VMEM_BUDGET = 40 * 1024 * 1024
MM_STEP_MACS = 2200 * 1024 * 1024
MXU_WIDTH = 256
MM_STEP_COST_BYTES = 1 << 20


def _divisors(n):
    return [t for t in range(LANES, n + 1, LANES) if n % t == 0] or [n]


def _mm_tiles(M, N, K, a_bytes, b_bytes, o_bytes, n_unit=None):
    best = None
    for tk in _divisors(K):
        for tm in _divisors(M):
            for tn in _divisors(N if n_unit is None else n_unit):
                nk = K // tk
                foot = 2 * (tm * tk * a_bytes + tk * tn * b_bytes + tm * tn * o_bytes) + (tm * tn * 4 if nk > 1 else 0)
                if (foot > MM_VMEM_BUDGET or tm * tn * tk > MM_STEP_MACS or tn < min(N if n_unit is None else n_unit, MXU_WIDTH)
                        or tm < min(M, MXU_WIDTH)):
                    continue
                steps = (M // tm) * (N // tn) * nk
                traffic = M * K * a_bytes * (N // tn if nk > 1 else 1) + K * N * b_bytes * (M // tm) + M * N * o_bytes
                exposed = tm * tk * a_bytes + tk * tn * b_bytes + tm * tn * o_bytes
                cost = traffic + exposed + steps * MM_STEP_COST_BYTES + (nk - 1) * M * N * 8
                if best is None or cost < best[0]:
                    best = (cost, tm, tn, tk)
    assert best is not None, (M, N, K)
    return best[1:]


def _mm(a, b, dims, out_dtype, name, res=None, bl=None, side=None, twin=None, b_halves=False, part=None):
    bs = b.shape[1:] if bl is not None or b_halves else b.shape
    if dims == "nn":
        (M, K), (K2, N) = a.shape, bs
    elif dims == "nt":
        (M, K), (N, K2) = a.shape, bs
    else:
        (K, M), (K2, N) = a.shape, bs
    assert K == K2, (name, a.shape, b.shape)
    n_half = N
    if b_halves:
        assert dims == "tn" and bl is None
        N = 2 * n_half
    n_total, n_first, earlier = part if part is not None else (N, 0, None)
    tm, tn, tk = _mm_tiles(M, N, K, a.dtype.itemsize, b.dtype.itemsize, jnp.dtype(out_dtype).itemsize
                           + (res.dtype.itemsize if res is not None else 0) + (jnp.dtype(twin).itemsize if twin is not None else 0),
                           n_unit=math.gcd(n_half, n_first) if b_halves or n_first else None)
    nk = K // tk
    lead = (None,) if bl is not None or b_halves else ()
    pre = (lambda *ix: (bl,) + ix) if bl is not None else (lambda *ix: ix)
    if b_halves:
        per_half = n_half // tn
        pre = lambda k, j: (j // per_half, k, j % per_half)
    if dims == "tn":
        a_spec = pl.BlockSpec((tk, tm), lambda i, j, k: (k, i))
    else:
        a_spec = pl.BlockSpec((tm, tk), lambda i, j, k: (i, k))
    if dims == "nt":
        b_spec = pl.BlockSpec(lead + (tn, tk), lambda i, j, k: pre(j, k))
    else:
        b_spec = pl.BlockSpec(lead + (tk, tn), lambda i, j, k: pre(k, j))
    assert n_first % tn == 0 and (part is None or res is None)
    o_spec = pl.BlockSpec((tm, tn), lambda i, j, k: (i, n_first // tn + j))
    in_specs, args = [a_spec, b_spec], [a, b]
    if res is not None:
        in_specs.append(o_spec)
        args.append(res)
    n_main = len(args)
    n_out = 1 if twin is None else 2
    aliases = None
    if earlier is not None:
        earlier = list(earlier) if twin is not None else [earlier]
        aliases = {n_main + t: t for t in range(n_out)}
        in_specs += [ANY] * n_out
        args += earlier

    def body(*refs):
        refs = refs[:n_main] + refs[len(args):]
        a_ref, b_ref = refs[0], refs[1]
        r_ref = refs[2] if res is not None else None
        o_ref = refs[n_main]
        p = lax.dot_general(a_ref[...].astype(BF), b_ref[...].astype(BF), _DN[dims], preferred_element_type=F32)

        def finish(t):
            if r_ref is not None:
                t = t + r_ref[...]
            o_ref[...] = t.astype(out_dtype)
            if twin is not None:
                refs[n_main + 1][...] = t.astype(twin)

        if nk == 1:
            finish(p)
        else:
            acc = refs[n_main + n_out]
            k = pl.program_id(2)

            @pl.when(k == 0)
            def _():
                acc[...] = p

            @pl.when(k > 0)
            def _():
                acc[...] += p

            @pl.when(k == nk - 1)
            def _():
                finish(acc[...])

    got = _call(body, args, grid=(M // tm, N // tn, nk), in_specs=in_specs, out_specs=[o_spec] * n_out,
                out_shape=[SDS((M, n_total), out_dtype)] + ([SDS((M, n_total), twin)] if twin is not None else []),
                scratch_shapes=[pltpu.VMEM((tm, tn), F32)] if nk > 1 else [], semantics=("parallel", "parallel", "arbitrary"),
                name=name, side=side, aliases=aliases)
    outs, landed = (got, None) if side is None else got
    out = outs[0] if twin is None else (outs[0], outs[1])
    return out if side is None else (out, landed)


def _rms(x, g):
    return x * lax.rsqrt(jnp.mean(x * x, axis=-1, keepdims=True) + EPS) * g


def _ln_silu(x, g, b):
    mu = jnp.mean(x, axis=-1, keepdims=True)
    xc = x - mu
    var = jnp.mean(xc * xc, axis=-1, keepdims=True)
    return jax.nn.silu(xc * lax.rsqrt(var + EPS) * g + b)


def _merge(gc, gp, yc, yp, ps):
    return jax.nn.sigmoid(gc) * yc + jax.nn.sigmoid(gp) * (yp * ps)


def _gated(gate, val):
    return jax.nn.gelu(gate) * val


def _rms_fwd(x, g, name):
    T, D = x.shape
    tb = _tile(T, ROW_BLOCK, 8)

    def body(x_ref, g_ref, o_ref):
        o_ref[...] = _rms(x_ref[...], g_ref[...]).astype(BF)

    row = pl.BlockSpec((tb, D), lambda i: (i, 0))
    return pl.pallas_call(body, grid=(T // tb,), in_specs=[row, pl.BlockSpec((1, D), lambda i: (0, 0))], out_specs=row,
                          out_shape=SDS((T, D), BF), compiler_params=_params("parallel"), name=name)(x, g.reshape(1, D))


def _rms_bwd(x, g, dh, dres, name):
    T, D = x.shape
    tb = _tile(T, ROW_BLOCK, 8)

    def body(*refs):
        if dres is not None:
            x_ref, g_ref, dh_ref, dres_ref, dx_ref, dxb_ref, dg_ref = refs
        else:
            x_ref, g_ref, dh_ref, dx_ref, dxb_ref, dg_ref = refs
        _, vjp = jax.vjp(_rms, x_ref[...], g_ref[...])
        dx, dg = vjp(dh_ref[...].astype(F32))
        if dres is not None:
            dx = dx + dres_ref[...]
        dx_ref[...] = dx
        dxb_ref[...] = dx.astype(BF)

        @pl.when(pl.program_id(0) == 0)
        def _():
            dg_ref[...] = jnp.zeros_like(dg_ref)

        dg_ref[...] += dg

    row = pl.BlockSpec((tb, D), lambda i: (i, 0))
    vec = pl.BlockSpec((1, D), lambda i: (0, 0))
    ins = [x, g.reshape(1, D), dh] + ([dres] if dres is not None else [])
    return pl.pallas_call(
        body, grid=(T // tb,), in_specs=[row, vec, row] + ([row] if dres is not None else []), out_specs=[row, row, vec],
        out_shape=[SDS((T, D), F32), SDS((T, D), BF), SDS((1, D), F32)], compiler_params=_params("arbitrary"), name=name)(*ins)


def _row_tile(M, K, N, per_row_bytes):
    fixed = K * N * 2
    fit = [t for t in _divisors(M) if fixed + 2 * t * per_row_bytes <= MM_VMEM_BUDGET and t * K * N <= 2 * MM_STEP_MACS]
    return max(fit) if fit else min(_divisors(M))


def _mm_rms_fwd(a, b, res, g, name, side=None):
    M, K = a.shape
    N = b.shape[2]
    tm = _row_tile(M, K, N, K * 2 + N * (4 + 4 + 2))

    def body(a_ref, b_ref, r_ref, g_ref, x_ref, h_ref):
        x = r_ref[...] + lax.dot_general(a_ref[...], b_ref[...], _DN["nn"], preferred_element_type=F32)
        x_ref[...] = x
        h_ref[...] = _rms(x, g_ref[...]).astype(BF)

    row = pl.BlockSpec((tm, N), lambda i: (i, 0))
    return _call(body, (a, b, res, g.reshape(1, N)), grid=(M // tm,),
                 in_specs=[pl.BlockSpec((tm, K), lambda i: (i, 0)), pl.BlockSpec((None, K, N), lambda i: (0, 0, 0), pipeline_mode=pl.Buffered(1)), row,
                           pl.BlockSpec((1, N), lambda i: (0, 0))],
                 out_specs=[row, row], out_shape=[SDS((M, N), F32), SDS((M, N), BF)], semantics=("parallel",), name=name, side=side)


def _mm_rms_bwd(a_parts, b, x, g, dres, name, side=None):
    n_a = len(a_parts)
    M = a_parts[0].shape[-2]
    N, K = b.shape[1:]
    assert K == sum(p.shape[-1] * (p.shape[0] if p.ndim == 3 else 1) for p in a_parts)
    tm = _row_tile(M, K, N, K * 2 + N * (4 + 4 + 4 + 2))

    def body(*refs):
        a_refs, (b_ref, x_ref, g_ref, r_ref, dx_ref, dxb_ref, dg_ref) = refs[:n_a], refs[n_a:]
        dh, col = None, 0
        for p, a_ref in zip(a_parts, a_refs):
            for blk in ([a_ref[h] for h in range(p.shape[0])] if p.ndim == 3 else [a_ref[...]]):
                t = lax.dot_general(blk, b_ref[:, col:col + p.shape[-1]], _DN["nt"], preferred_element_type=F32)
                dh = t if dh is None else dh + t
                col += p.shape[-1]
        _, vjp = jax.vjp(_rms, x_ref[...], g_ref[...])
        dx, dg = vjp(dh)
        dx = dx + r_ref[...]
        dx_ref[...] = dx
        dxb_ref[...] = dx.astype(BF)

        @pl.when(pl.program_id(0) == 0)
        def _():
            dg_ref[...] = jnp.zeros_like(dg_ref)

        dg_ref[...] += dg

    row = pl.BlockSpec((tm, N), lambda i: (i, 0))
    vec = pl.BlockSpec((1, N), lambda i: (0, 0))
    a_specs = [pl.BlockSpec((p.shape[0], tm, p.shape[2]), lambda i: (0, i, 0)) if p.ndim == 3 else pl.BlockSpec((tm, p.shape[1]), lambda i: (i, 0))
               for p in a_parts]
    return _call(
        body, (*a_parts, b, x, g.reshape(1, N), dres), grid=(M // tm,),
        in_specs=a_specs + [pl.BlockSpec((None, N, K), lambda i: (0, 0, 0), pipeline_mode=pl.Buffered(1)), row, vec, row],
        out_specs=[row, row, vec], out_shape=[SDS((M, N), F32), SDS((M, N), BF), SDS((1, N), F32)],
        semantics=("arbitrary",), name=name, side=side)


def _loss_bwd(x, g, target, name):
    T, D = x.shape
    tb = _tile(T, ROW_BLOCK, 8)
    nb = T // tb

    def body(x_ref, g_ref, t_ref, loss_ref, dx_ref, dg_ref, acc):
        i = pl.program_id(0)
        y, vjp = jax.vjp(_rms, x_ref[...], g_ref[...])
        err = y - t_ref[...]
        dx, dg = vjp(err * (1.0 / D))
        dx_ref[...] = dx

        @pl.when(i == 0)
        def _():
            dg_ref[...] = jnp.zeros_like(dg_ref)
            acc[...] = jnp.zeros_like(acc)

        dg_ref[...] += dg
        acc[...] += jnp.sum(err * err, axis=0, keepdims=True)

        @pl.when(i == nb - 1)
        def _():
            loss_ref[...] = jnp.full(loss_ref.shape, (0.5 / D) * jnp.sum(acc[...]), F32)

    row = pl.BlockSpec((tb, D), lambda i: (i, 0))
    vec = pl.BlockSpec((1, D), lambda i: (0, 0))
    return pl.pallas_call(
        body, grid=(nb,), in_specs=[row, vec, row], out_specs=[pl.BlockSpec((1, LANES), lambda i: (0, 0)), row, vec],
        out_shape=[SDS((1, LANES), F32), SDS((T, D), F32), SDS((1, D), F32)], scratch_shapes=[pltpu.VMEM((1, D), F32)],
        compiler_params=_params("arbitrary"), name=name)(x, g.reshape(1, D), target)


def _ln_silu_mm(cv, g, b, w, name):
    T, C = cv.shape
    D = w.shape[2]
    tb = _tile(T, 2 * ROW_BLOCK, 8)

    def body(x_ref, g_ref, b_ref, w_ref, y1_ref, y_ref):
        y1 = _ln_silu(x_ref[...], g_ref[...], b_ref[...]).astype(BF)
        y1_ref[...] = y1
        y_ref[...] = lax.dot_general(y1, w_ref[...], _DN["nn"], preferred_element_type=F32).astype(BF)

    row = pl.BlockSpec((tb, C), lambda i: (i, 0))
    vec = pl.BlockSpec((1, C), lambda i: (0, 0))
    return pl.pallas_call(
        body, grid=(T // tb,), in_specs=[row, vec, vec, pl.BlockSpec((None, C, D), lambda i: (0, 0, 0), pipeline_mode=pl.Buffered(1))],
        out_specs=[row, pl.BlockSpec((tb, D), lambda i: (i, 0))], out_shape=[SDS((T, C), BF), SDS((T, D), BF)],
        compiler_params=_params("parallel"), name=name)(cv, g.reshape(1, C), b.reshape(1, C), w)


def _mm_ln_silu_bwd(dyc, w, cv, g, b, name, side=None):
    T, C = cv.shape
    D = w.shape[2]
    tb = _tile(T, 2 * ROW_BLOCK, 8)

    def body(d_ref, w_ref, x_ref, g_ref, b_ref, dx_ref, dg_ref, db_ref):
        dy1 = lax.dot_general(d_ref[...], w_ref[...], _DN["nt"], preferred_element_type=F32)
        _, vjp = jax.vjp(_ln_silu, x_ref[...], g_ref[...], b_ref[...])
        dx, dg, db = vjp(dy1)
        dx_ref[...] = dx

        @pl.when(pl.program_id(0) == 0)
        def _():
            dg_ref[...] = jnp.zeros_like(dg_ref)
            db_ref[...] = jnp.zeros_like(db_ref)

        dg_ref[...] += dg
        db_ref[...] += db

    row = pl.BlockSpec((tb, C), lambda i: (i, 0))
    vec = pl.BlockSpec((1, C), lambda i: (0, 0))
    return _call(
        body, (dyc, w, cv, g.reshape(1, C), b.reshape(1, C)), grid=(T // tb,),
        in_specs=[pl.BlockSpec((tb, D), lambda i: (i, 0)), pl.BlockSpec((None, C, D), lambda i: (0, 0, 0), pipeline_mode=pl.Buffered(1)), row, vec, vec],
        out_specs=[row, vec, vec], out_shape=[SDS((T, C), F32), SDS((1, C), F32), SDS((1, C), F32)], semantics=("arbitrary",),
        name=name, side=side)


def _merge_fwd(proj, yc, yp, ps, C, name, side=None):
    T, D = yc.shape
    tb = _tile(T, ROW_BLOCK, 8)
    nj = D // C

    def body(gc_ref, gp_ref, yc_ref, yp_ref, ps_ref, o_ref):
        o_ref[...] = _merge(gc_ref[...], gp_ref[...], yc_ref[...].astype(F32), yp_ref[...].astype(F32), ps_ref[...]).astype(BF)

    blk = pl.BlockSpec((tb, C), lambda i, j: (i, j))
    return _call1(
        body, (proj, proj, yc, yp, ps.reshape(1, D)), grid=(T // tb, nj),
        in_specs=[pl.BlockSpec((tb, C), lambda i, j: (i, 3 + j)), pl.BlockSpec((tb, C), lambda i, j: (i, 3 + nj + j)), blk, blk,
                  pl.BlockSpec((1, C), lambda i, j: (0, j))],
        out_spec=blk, out_shape=SDS((T, D), BF), semantics=("parallel", "parallel"), name=name, side=side)


def _merge_bwd(proj, yc, yp, ps, dm, C, name, side=None):
    T, D = yc.shape
    tb = _tile(T, ROW_BLOCK, 8)
    nj = D // C

    def body(gc_ref, gp_ref, yc_ref, yp_ref, ps_ref, dm_ref, dg_ref, dyc_ref, dyp_ref, dps_ref):
        _, vjp = jax.vjp(_merge, gc_ref[...], gp_ref[...], yc_ref[...].astype(F32), yp_ref[...].astype(F32), ps_ref[...])
        dgc, dgp, dyc, dyp, dps = vjp(dm_ref[...].astype(F32))
        dg_ref[0] = dgc.astype(BF)
        dg_ref[1] = dgp.astype(BF)
        dyc_ref[...] = dyc.astype(BF)
        dyp_ref[...] = dyp.astype(BF)

        @pl.when(pl.program_id(1) == 0)
        def _():
            dps_ref[...] = jnp.zeros_like(dps_ref)

        dps_ref[...] += dps

    blk = pl.BlockSpec((tb, C), lambda j, i: (i, j))
    vec = pl.BlockSpec((1, C), lambda j, i: (0, j))
    return _call(
        body, (proj, proj, yc, yp, ps.reshape(1, D), dm), grid=(nj, T // tb),
        in_specs=[pl.BlockSpec((tb, C), lambda j, i: (i, 3 + j)), pl.BlockSpec((tb, C), lambda j, i: (i, 3 + nj + j)), blk, blk, vec, blk],
        out_specs=[pl.BlockSpec((2, tb, C), lambda j, i: (0, i, j)), blk, blk, vec],
        out_shape=[SDS((2, T, D), BF), SDS((T, D), BF), SDS((T, D), BF), SDS((1, D), F32)],
        semantics=("parallel", "arbitrary"), name=name, side=side)


def _shd(v, s, rows):
    if s == 0:
        return v
    return jnp.where(rows >= s, pltpu.roll(v, s, 0), 0.0)


def _shu(v, s, rows):
    if s == 0:
        return v
    n = v.shape[0]
    return jnp.where(rows < n - s, pltpu.roll(v, n - s, 0), 0.0)


def _glu_conv_fwd(proj, w, b, Bn, S, C, name, side=None):
    K = w.shape[0]
    sl = min(LANES, C)
    ns = C // sl

    def body(a_ref, gl_ref, w_ref, b_ref, o_ref):
        y0 = a_ref[...] * jax.nn.sigmoid(gl_ref[...])
        rows = lax.broadcasted_iota(jnp.int32, y0.shape, 0)
        acc = jnp.zeros_like(y0) + b_ref[...]
        for k in range(K):
            acc = acc + w_ref[k:k + 1, :] * _shd(y0, K - 1 - k, rows)
        o_ref[...] = acc

    return _call1(
        body, (proj, proj, w, b.reshape(1, C)), grid=(Bn, ns),
        in_specs=[pl.BlockSpec((S, sl), lambda bi, j: (bi, j)), pl.BlockSpec((S, sl), lambda bi, j: (bi, ns + j)),
                  pl.BlockSpec((K, sl), lambda bi, j: (0, j)), pl.BlockSpec((1, sl), lambda bi, j: (0, j))],
        out_spec=pl.BlockSpec((S, sl), lambda bi, j: (bi, j)), out_shape=SDS((Bn * S, C), F32),
        semantics=("parallel", "parallel"), name=name, side=side)


def _glu_conv_bwd(proj, w, dcv, Bn, S, C, name, side=None):
    K = w.shape[0]
    sl = min(LANES, C)
    ns = C // sl

    def body(a_ref, gl_ref, w_ref, d_ref, dagl_ref, dw_ref, db_ref):
        a = a_ref[...]
        sg = jax.nn.sigmoid(gl_ref[...])
        y0 = a * sg
        d = d_ref[...]
        rows = lax.broadcasted_iota(jnp.int32, y0.shape, 0)

        @pl.when(pl.program_id(1) == 0)
        def _():
            dw_ref[...] = jnp.zeros_like(dw_ref)
            db_ref[...] = jnp.zeros_like(db_ref)

        dy0 = jnp.zeros_like(y0)
        for k in range(K):
            s = K - 1 - k
            dw_ref[k:k + 1, :] += jnp.sum(d * _shd(y0, s, rows), axis=0, keepdims=True)
            dy0 = dy0 + w_ref[k:k + 1, :] * _shu(d, s, rows)
        db_ref[...] += jnp.sum(d, axis=0, keepdims=True)
        dagl_ref[0] = (dy0 * sg).astype(BF)
        dagl_ref[1] = (dy0 * a * sg * (1.0 - sg)).astype(BF)

    blk = pl.BlockSpec((S, sl), lambda j, bi: (bi, j))
    return _call(
        body, (proj, proj, w, dcv), grid=(ns, Bn),
        in_specs=[blk, pl.BlockSpec((S, sl), lambda j, bi: (bi, ns + j)), pl.BlockSpec((K, sl), lambda j, bi: (0, j)), blk],
        out_specs=[pl.BlockSpec((2, S, sl), lambda j, bi: (0, bi, j)), pl.BlockSpec((K, sl), lambda j, bi: (0, j)),
                   pl.BlockSpec((1, sl), lambda j, bi: (0, j))],
        out_shape=[SDS((2, Bn * S, C), BF), SDS((K, C), F32), SDS((1, C), F32)],
        semantics=("parallel", "arbitrary"), name=name, side=side)


def _pool_z(u, g, rows):
    s2 = u + _shd(u, 1, rows)
    s4 = s2 + _shd(s2, 2, rows)
    s8 = s4 + _shd(s4, 4, rows)
    s16 = s8 + _shd(s8, 8, rows)
    sw = jnp.where(g == 0, s2, jnp.where(g == 1, s4, jnp.where(g == 2, s8, s16)))
    cnt = jnp.minimum(rows + 1, POOL_WINDOWS[0] << g).astype(F32)
    return sw / cnt - u, cnt


def _pool_fwd(proj, wpt, l, Bn, S, C, D, name):
    G = len(POOL_WINDOWS)
    gd, go = C // G, D // G

    def body(u_ref, w_ref, o_ref):
        g = pl.program_id(1)
        u = u_ref[...]
        rows = lax.broadcasted_iota(jnp.int32, u.shape, 0)
        zp, _ = _pool_z(u, g, rows)
        o_ref[...] = lax.dot_general(zp.astype(BF), w_ref[...], _DN["nt"], preferred_element_type=F32).astype(BF)

    return pl.pallas_call(
        body, grid=(Bn, G),
        in_specs=[pl.BlockSpec((S, gd), lambda bi, g: (bi, 2 * G + g)), pl.BlockSpec((None, go, gd), lambda bi, g: (l * G + g, 0, 0))],
        out_specs=pl.BlockSpec((S, go), lambda bi, g: (bi, g)), out_shape=SDS((Bn * S, D), BF),
        compiler_params=_params("parallel", "parallel"), name=name)(proj, wpt)


def _pool_bwd(proj, wpt, dyp, l, Bn, S, C, D, name):
    G = len(POOL_WINDOWS)
    gd, go = C // G, D // G

    def body(u_ref, w_ref, d_ref, du_ref, dw_ref):
        g = pl.program_id(0)
        u = u_ref[...]
        rows = lax.broadcasted_iota(jnp.int32, u.shape, 0)
        zp, cnt = _pool_z(u, g, rows)
        d = d_ref[...]
        dzp = lax.dot_general(d, w_ref[...], _DN["nn"], preferred_element_type=F32)

        @pl.when(pl.program_id(1) == 0)
        def _():
            dw_ref[...] = jnp.zeros_like(dw_ref)

        dw_ref[...] += lax.dot_general(d, zp.astype(BF), _DN["tn"], preferred_element_type=F32)
        dsw = dzp / cnt
        zero = jnp.zeros_like(dsw)
        d16 = jnp.where(g == 3, dsw, zero)
        d8 = jnp.where(g == 2, dsw, zero) + d16 + _shu(d16, 8, rows)
        d4 = jnp.where(g == 1, dsw, zero) + d8 + _shu(d8, 4, rows)
        d2 = jnp.where(g == 0, dsw, zero) + d4 + _shu(d4, 2, rows)
        d1 = d2 + _shu(d2, 1, rows)
        du_ref[...] = (d1 - dzp).astype(BF)

    return pl.pallas_call(
        body, grid=(G, Bn),
        in_specs=[pl.BlockSpec((S, gd), lambda g, bi: (bi, 2 * G + g)), pl.BlockSpec((None, go, gd), lambda g, bi: (l * G + g, 0, 0)),
                  pl.BlockSpec((S, go), lambda g, bi: (bi, g))],
        out_specs=[pl.BlockSpec((S, gd), lambda g, bi: (bi, g)), pl.BlockSpec((None, go, gd), lambda g, bi: (g, 0, 0))],
        out_shape=[SDS((Bn * S, C), BF), SDS((G, go, gd), F32)],
        compiler_params=_params("parallel", "arbitrary"), name=name)(proj, wpt, dyp)


def _ffn_conv(u, w_ref, rows):
    K = w_ref.shape[0]
    acc = w_ref[K - 1:K, :] * u
    for k in range(K - 1):
        acc = acc + w_ref[k:k + 1, :] * _shd(u, K - 1 - k, rows)
    return acc


def _ffn_cb(F):
    return _tile(F, 256)


def _ffn_act_fwd(up0, w, Bn, S, F, name, side=None):
    cb = _ffn_cb(F)
    nj = F // cb

    def body(g_ref, v_ref, wg_ref, wv_ref, o_ref):
        rows = lax.broadcasted_iota(jnp.int32, g_ref.shape, 0)
        o_ref[...] = _gated(_ffn_conv(g_ref[...], wg_ref, rows), _ffn_conv(v_ref[...], wv_ref, rows)).astype(BF)

    K = w.shape[0]
    return _call1(
        body, (up0, up0, w, w), grid=(Bn, nj),
        in_specs=[pl.BlockSpec((S, cb), lambda bi, j: (bi, j)), pl.BlockSpec((S, cb), lambda bi, j: (bi, nj + j)),
                  pl.BlockSpec((K, cb), lambda bi, j: (0, j)), pl.BlockSpec((K, cb), lambda bi, j: (0, nj + j))],
        out_spec=pl.BlockSpec((S, cb), lambda bi, j: (bi, j)), out_shape=SDS((Bn * S, F), BF),
        semantics=("parallel", "parallel"), name=name, side=side)


SUBLANES = 8
FFN_HALO = SUBLANES
FFN_ROWS = 128
GELU_C0, GELU_C1 = 0.7978845608028654, 0.044715


def _gelu_and_grad(x):
    x2 = x * x
    t = jnp.tanh(GELU_C0 * (x + GELU_C1 * (x2 * x)))
    cdf = 0.5 * (1.0 + t)
    return x * cdf, cdf + (0.5 * GELU_C0) * x * (1.0 - t * t) * (1.0 + (3.0 * GELU_C1) * x2)


def _ffn_act_bwd(up0, w, dg, Bn, S, F, name, side=None):
    cb = min(LANES, F)
    nj = F // cb
    K = w.shape[0]
    rc = FFN_ROWS if S % FFN_ROWS == 0 else S
    win = rc + 2 * FFN_HALO
    assert K - 1 <= FFN_HALO and rc % SUBLANES == 0

    def body(g_ref, v_ref, wg_ref, wv_ref, d_ref, do_ref, dwg_ref, dwv_ref, gp, vp, dp):
        for pad, src in ((gp, g_ref), (vp, v_ref), (dp, d_ref)):
            pad[0:FFN_HALO, :] = jnp.zeros((FFN_HALO, cb), F32)
            pad[FFN_HALO + S:, :] = jnp.zeros((FFN_HALO, cb), F32)
            pad[FFN_HALO:FFN_HALO + S, :] = src[...].astype(F32)
        wg = [wg_ref[k:k + 1, :] for k in range(K)]
        wv = [wv_ref[k:k + 1, :] for k in range(K)]

        def taps(u):
            return [pltpu.roll(u, K - 1 - k, 0) for k in range(K - 1)] + [u]

        def conv(us, ws):
            acc = ws[K - 1] * us[K - 1]
            for k in range(K - 1):
                acc = acc + ws[k] * us[k]
            return acc

        def conv_t(dc, ws):
            acc = ws[K - 1] * dc
            for k in range(K - 1):
                acc = acc + ws[k] * pltpu.roll(dc, win - (K - 1 - k), 0)
            return acc

        def fold(t):
            acc = t[FFN_HALO:FFN_HALO + SUBLANES]
            for i in range(1, rc // SUBLANES):
                acc = acc + t[FFN_HALO + SUBLANES * i:FFN_HALO + SUBLANES * (i + 1)]
            return acc

        def chunk(c, sums):
            r0 = pl.multiple_of(c * rc, SUBLANES)
            gs, vs, d = taps(gp[pl.ds(r0, win), :]), taps(vp[pl.ds(r0, win), :]), dp[pl.ds(r0, win), :]
            ge, dge = _gelu_and_grad(conv(gs, wg))
            dgc = d * conv(vs, wv) * dge
            dvc = d * ge
            do_ref[0, pl.ds(r0, rc), :] = conv_t(dgc, wg)[FFN_HALO:FFN_HALO + rc].astype(BF)
            do_ref[1, pl.ds(r0, rc), :] = conv_t(dvc, wv)[FFN_HALO:FFN_HALO + rc].astype(BF)
            new = [fold(dc * u) for us, dc in ((gs, dgc), (vs, dvc)) for u in us]
            return tuple(a + b for a, b in zip(sums, new))

        sums = lax.fori_loop(0, S // rc, chunk, tuple(jnp.zeros((SUBLANES, cb), F32) for _ in range(2 * K)))

        @pl.when(pl.program_id(1) == 0)
        def _():
            dwg_ref[...] = jnp.zeros_like(dwg_ref)
            dwv_ref[...] = jnp.zeros_like(dwv_ref)

        for k in range(K):
            dwg_ref[k:k + 1, :] += jnp.sum(sums[k], axis=0, keepdims=True)
            dwv_ref[k:k + 1, :] += jnp.sum(sums[K + k], axis=0, keepdims=True)

    blk = pl.BlockSpec((S, cb), lambda j, bi: (bi, j))
    wblk = pl.BlockSpec((K, cb), lambda j, bi: (0, j))
    return _call(
        body, (up0, up0, w, w, dg), grid=(nj, Bn),
        in_specs=[blk, pl.BlockSpec((S, cb), lambda j, bi: (bi, nj + j)), wblk, pl.BlockSpec((K, cb), lambda j, bi: (0, nj + j)), blk],
        out_specs=[pl.BlockSpec((2, S, cb), lambda j, bi: (0, bi, j)), wblk, wblk],
        out_shape=[SDS((2, Bn * S, F), BF), SDS((K, F), F32), SDS((K, F), F32)],
        scratch_shapes=[pltpu.VMEM((S + 2 * FFN_HALO, cb), F32)] * 3, semantics=("parallel", "arbitrary"), name=name, side=side)


def _softmax_rows(q, k, scale):
    sc = lax.dot_general(q, k, _DN["nt"], preferred_element_type=F32) * scale
    e = jnp.exp(sc - jnp.max(sc, axis=-1, keepdims=True))
    return e / jnp.sum(e, axis=-1, keepdims=True)


def _attn_ts(S):
    return _tile(S, 1024, 8)


def _attn_fwd(q, kv, Bn, S, Mn, D, name, side=None):
    H = XA_HEADS
    dh = D // H
    ts = _attn_ts(S)
    nsb = S // ts
    scale = dh ** -0.5

    def body(q_ref, k_ref, v_ref, o_ref):
        p = _softmax_rows(q_ref[...], k_ref[...], scale)
        o_ref[...] = lax.dot_general(p.astype(BF), v_ref[...], _DN["nn"], preferred_element_type=F32).astype(BF)

    qblk = pl.BlockSpec((ts, dh), lambda bi, h, s: (bi * nsb + s, h))
    return _call1(
        body, (q, kv, kv), grid=(Bn, H, nsb),
        in_specs=[qblk, pl.BlockSpec((Mn, dh), lambda bi, h, s: (bi, h)), pl.BlockSpec((Mn, dh), lambda bi, h, s: (bi, H + h))],
        out_spec=qblk, out_shape=SDS((Bn * S, D), BF), semantics=("parallel", "parallel", "parallel"), name=name, side=side)


def _attn_bwd(q, kv, datt, Bn, S, Mn, D, name):
    H = XA_HEADS
    dh = D // H
    ts = _attn_ts(S)
    nsb = S // ts
    scale = dh ** -0.5

    def body(q_ref, k_ref, v_ref, do_ref, dq_ref, dk_ref, dv_ref):
        q, k, v, do = q_ref[...], k_ref[...], v_ref[...], do_ref[...]
        p = _softmax_rows(q, k, scale)
        dp = lax.dot_general(do, v, _DN["nt"], preferred_element_type=F32)
        ds = (p * (dp - jnp.sum(dp * p, axis=-1, keepdims=True)) * scale).astype(BF)
        dq_ref[...] = lax.dot_general(ds, k, _DN["nn"], preferred_element_type=F32).astype(BF)

        @pl.when(pl.program_id(2) == 0)
        def _():
            dk_ref[...] = jnp.zeros_like(dk_ref)
            dv_ref[...] = jnp.zeros_like(dv_ref)

        dk_ref[...] += lax.dot_general(ds, q, _DN["tn"], preferred_element_type=F32)
        dv_ref[...] += lax.dot_general(p.astype(BF), do, _DN["tn"], preferred_element_type=F32)

    qblk = pl.BlockSpec((ts, dh), lambda bi, h, s: (bi * nsb + s, h))
    kblk = pl.BlockSpec((Mn, dh), lambda bi, h, s: (bi, h))
    return pl.pallas_call(
        body, grid=(Bn, H, nsb),
        in_specs=[qblk, kblk, pl.BlockSpec((Mn, dh), lambda bi, h, s: (bi, H + h)), qblk],
        out_specs=[qblk, kblk, kblk], out_shape=[SDS((Bn * S, D), BF), SDS((Bn * Mn, D), F32), SDS((Bn * Mn, D), F32)],
        compiler_params=_params("parallel", "parallel", "arbitrary"), name=name)(q, kv, kv, datt)


class _Sides:
    def __init__(self, by_key=None, on_land=None):
        self.by_key, self.landed, self.on_land = dict(by_key or {}), {}, on_land

    def run(self, key, fn, *args, **kw):
        side = self.by_key.get(key)
        if side is None:
            return fn(*args, **kw)
        out, self.landed[key] = fn(*args, side=side() if callable(side) else side, **kw)
        if self.on_land is not None:
            self.on_land(key, self.landed[key])
        return out

    def mm(self, key, *args, **kw):
        return self.run(key, _mm, *args, **kw)


def _layer_fwd(x, h, mem_n, W, V, l, dims, sides, next_g):
    Bn, S, Mn, D, C, F = dims
    n = f"l{l}_"
    proj = sides.mm("proj", h, W["w_in"], "nn", F32, n + "proj", bl=0)
    cv = sides.run("glu_conv", _glu_conv_fwd, proj, V["conv_dw_w"][l], V["conv_dw_b"][l], Bn, S, C, n + "glu_conv")
    yc1, yc = _ln_silu_mm(cv, V["conv_ln_g"][l], V["conv_ln_b"][l], W["w_conv_out"], n + "conv_out")
    yp = _pool_fwd(proj, W["w_pool"], 0, Bn, S, C, D, n + "pool")
    merged = sides.run("merge", _merge_fwd, proj, yc, yp, V["pool_scale"][l], C, n + "merge")
    x1, hq = sides.run("out_proj", _mm_rms_fwd, merged, W["w_out"], x, V["xattn_norm_g"][l], n + "out_proj")
    q = sides.mm("q_proj", hq, W["w_q"], "nn", BF, n + "q_proj", bl=0)
    kv = _mm(mem_n, W["w_kv"], "nn", BF, n + "kv_proj", bl=0)
    att = sides.run("attn", _attn_fwd, q, kv, Bn, S, Mn, D, n + "attn")
    x2, hf = sides.run("o_proj", _mm_rms_fwd, att, W["w_o"], x1, V["ffn_norm_g"][l], n + "o_proj")
    up0 = sides.mm("up_proj", hf, W["w_up"], "nn", F32, n + "up_proj", bl=0)
    gact = sides.run("ffn_act", _ffn_act_fwd, up0, V["ffn_dw_w"][l], Bn, S, F, n + "ffn_act")
    if next_g is not None:
        x3, h3 = sides.run("down_proj", _mm_rms_fwd, gact, W["w_down"], x2, next_g, n + "down_proj")
    else:
        x3, h3 = sides.mm("down_proj", gact, W["w_down"], "nn", F32, n + "down_proj", res=x2, bl=0), None
    return x3, h3, dict(x=x, h=h, proj=proj, cv=cv, yc1=yc1, yc=yc, yp=yp, merged=merged, x1=x1, hq=hq, q=q, kv=kv, att=att, x2=x2,
                        hf=hf, up0=up0, gact=gact)


def _layer_bwd_mlp(dx, dxb, sv, W, V, l, dims, sides):
    Bn, S, Mn, D, C, F = dims
    n = f"l{l}_b_"
    gw, sm = {}, {}
    dgact = sides.mm("d_gact", dxb, W["w_down"], "nt", BF, n + "d_gact", bl=0)
    gw["w_down"] = sides.mm("dw_down", sv["gact"], dxb, "tn", F32, n + "dw_down", twin=BF)
    dup0, dwg, dwv = sides.run("ffn_act_b", _ffn_act_bwd, sv["up0"], V["ffn_dw_w"][l], dgact, Bn, S, F, n + "ffn_act")
    sm["ffn_dw_w"] = jnp.concatenate([dwg, dwv], axis=1)
    dx2, dx2b, sm["ffn_norm_g"] = _mm_rms_bwd([dup0], W["w_up"], sv["x2"], V["ffn_norm_g"][l], dx, n + "d_hf")
    gw["w_up"] = sides.mm("dw_up", sv["hf"], dup0, "tn", F32, n + "dw_up", twin=BF, b_halves=True)
    return dx2, dx2b, gw, sm


def _layer_bwd_mix(dx2, dx2b, dmem_n, sv, mem_n, W, V, l, dims, sides, gw):
    Bn, S, Mn, D, C, F = dims
    n = f"l{l}_b_"
    sm = {}
    datt = sides.mm("d_att", dx2b, W["w_o"], "nt", BF, n + "d_att", bl=0)
    gw["w_o"] = _mm(sv["att"], dx2b, "tn", F32, n + "dw_o", twin=BF)
    dq, dk, dv = _attn_bwd(sv["q"], sv["kv"], datt, Bn, S, Mn, D, n + "attn")
    dkv = jnp.concatenate([dk, dv], axis=1)
    gw["w_kv"] = _mm(mem_n, dkv, "tn", F32, n + "dw_kv", twin=BF)
    dmem_n = _mm(dkv, W["w_kv"], "nt", F32, n + "d_mem", res=dmem_n, bl=0)
    dx1, dx1b, sm["xattn_norm_g"] = _mm_rms_bwd([dq], W["w_q"], sv["x1"], V["xattn_norm_g"][l], dx2, n + "d_hq")
    gw["w_q"] = _mm(sv["hq"], dq, "tn", F32, n + "dw_q", twin=BF)
    dmerged = sides.mm("d_merged", dx1b, W["w_out"], "nt", BF, n + "d_merged", bl=0)
    gw["w_out"] = _mm(sv["merged"], dx1b, "tn", F32, n + "dw_out", twin=BF)
    dgates, dyc, dyp, sm["pool_scale"] = sides.run("merge_b", _merge_bwd, sv["proj"], sv["yc"], sv["yp"], V["pool_scale"][l], dmerged, C, n + "merge")
    du, dwp = _pool_bwd(sv["proj"], W["w_pool"], dyp, 0, Bn, S, C, D, n + "pool")
    gw["w_pool"] = (dwp, dwp.astype(BF))
    gw["w_conv_out"] = _mm(sv["yc1"], dyc, "tn", F32, n + "dw_conv_out", twin=BF)
    dcv, sm["conv_ln_g"], sm["conv_ln_b"] = sides.run("ln_silu_b", _mm_ln_silu_bwd, dyc, W["w_conv_out"], sv["cv"], V["conv_ln_g"][l],
                                                      V["conv_ln_b"][l], n + "d_yc1")
    dagl, sm["conv_dw_w"], sm["conv_dw_b"] = sides.run("glu_conv_b", _glu_conv_bwd, sv["proj"], V["conv_dw_w"][l], dcv, Bn, S, C, n + "glu_conv")
    dx, dxb, sm["mix_norm_g"] = sides.run("d_h", _mm_rms_bwd, [dagl, du, dgates], W["w_in"], sv["x"], V["mix_norm_g"][l], dx1, n + "d_h")
    n_in = W["w_in"].shape[2]
    part = _mm(sv["h"], dagl, "tn", F32, n + "dw_in_conv", twin=BF, b_halves=True, part=(n_in, 0, None))
    part = _mm(sv["h"], du, "tn", F32, n + "dw_in_pool", twin=BF, part=(n_in, 2 * C, part))
    gw["w_in"] = sides.mm("dw_in", sv["h"], dgates, "tn", F32, n + "dw_in", twin=BF, b_halves=True, part=(n_in, 3 * C, part))
    return dx, dxb, dmem_n, sm


BIG = (("w_in", "col"), ("w_conv_out", "col"), ("w_pool", "row"), ("w_out", "row"), ("w_q", "row"), ("w_kv", "col"),
       ("w_o", "row"), ("w_up", "col"), ("w_down", "row"))
ALL_RELS = (1, 2, 3)
GATHER_FIRST = ("w_in", "w_conv_out", "w_pool", "w_out", "w_q", "w_o")
FWD_CARRY = {
    (0, "proj"): (("w_up", 0, (1, 2)),),
    (0, "glu_conv"): (("w_kv", 0, ALL_RELS),),
    (0, "merge"): (("w_up", 0, (3,)),),
    (0, "q_proj"): (("w_down", 0, (1, 2)),),
    (0, "attn"): (("w_down", 0, (3,)),),
    (0, "up_proj"): (("w_in", 1, ALL_RELS), ("w_conv_out", 1, ALL_RELS), ("w_pool", 1, ALL_RELS), ("w_o", 1, ALL_RELS)),
    (0, "ffn_act"): (("w_out", 1, ALL_RELS), ("w_q", 1, ALL_RELS), ("w_kv", 1, ALL_RELS)),
    (1, "proj"): (("w_up", 1, (1, 2)),),
    (1, "glu_conv"): (("w_down", 1, (1, 2)),),
    (1, "merge"): (("w_up", 1, (3,)),),
    (1, "attn"): (("w_down", 1, (3,)),),
}
PASS_CARRY = {
    (0, "out_proj"): (("w_kv", 0),),
    (0, "o_proj"): (("w_up", 0), ("w_down", 0)),
    (0, "down_proj"): (("w_in", 1), ("w_conv_out", 1), ("w_pool", 1), ("w_out", 1), ("w_q", 1), ("w_kv", 1), ("w_o", 1)),
    (1, "o_proj"): (("w_up", 1), ("w_down", 1)),
}
EARLY = ("w_down", "w_up")
BWD_CARRY_EARLY = {"merge_b": ("w_down",), "glu_conv_b": ("w_up",)}
BWD_CARRY_LATE = {"ffn_act_b": ("w_in", "w_conv_out", "w_pool", "w_out", "w_q", "w_kv", "w_o")}
BWD_LAST_LAYER = (("att", ("w_o", "w_kv", "w_q"), "d_merged", {"d_h": ("w_o", "w_kv", "w_q")}),
                  ("tok", ("w_out", "w_pool", "w_conv_out"), "ln_silu_b", {"dw_in": ("w_out", "w_pool", "w_conv_out")}))


def _place():
    xi, yi, ci = lax.axis_index("x"), lax.axis_index("y"), lax.axis_index("c")
    return xi, yi, ci, 2 * xi + yi


def _chip_peer(xi, yi, ci, r):
    return (xi ^ (r >> 1), yi ^ (r & 1), ci)


def _full_shard(ref, kind, k, cs):
    if kind == "col":
        return ref.at[:, :, :, :, pl.ds(pl.multiple_of(k * cs, cs), cs)]
    return ref.at[:, :, k]


def _gather_weights(shards, kinds):
    n = len(shards)
    outs = []
    for s, kind in zip(shards, kinds):
        L, P, _, RH, CS = s.shape
        outs.append(SDS((L, P, 2, RH, CS * N_CHIPS) if kind == "col" else (L, P, N_CHIPS, 2, RH, CS), s.dtype))
    per = 7

    def body(*refs):
        srcs, fulls, (ssem, rsem) = refs[:n], refs[n:2 * n], refs[2 * n:]
        xi, yi, ci, j = _place()
        sib = (xi, yi, 1 - ci)

        def piece(i, k, c):
            kind, cs = kinds[i], shards[i].shape[-1]
            if kind == "col":
                return fulls[i].at[:, :, c, :, pl.ds(pl.multiple_of(k * cs, cs), cs)]
            return fulls[i].at[:, :, k, c]

        def copy(i, slot, src, dst, dev):
            return pltpu.make_async_remote_copy(src_ref=src, dst_ref=dst, send_sem=ssem.at[per * i + slot], recv_sem=rsem.at[per * i + slot],
                                                device_id=dev, device_id_type=MESH)

        own, first, passed = [], [], []
        for i in range(n):
            for r in (1, 2, 3):
                first.append(copy(i, r - 1, srcs[i].at[:, :, ci], piece(i, j, ci), _chip_peer(xi, yi, ci, r)))
                first[-1].start()
        for i in range(n):
            own.append(copy(i, 6, srcs[i], _full_shard(fulls[i], kinds[i], j, shards[i].shape[-1]), sib))
            own[-1].start()
        for i in range(n):
            for r in (1, 2, 3):
                got = piece(i, j ^ r, ci)
                copy(i, r - 1, got, got, sib).wait_recv()
                passed.append(copy(i, 2 + r, got, got, sib))
                passed[-1].start()
        for i in range(n):
            for r in (1, 2, 3):
                got = piece(i, j ^ r, 1 - ci)
                copy(i, 2 + r, got, got, sib).wait_recv()
        for cp in own:
            cp.wait()
        for cp in first + passed:
            cp.wait_send()

    return pl.pallas_call(
        body, in_specs=[ANY] * n, out_specs=[ANY] * n, out_shape=outs,
        scratch_shapes=[pltpu.SemaphoreType.DMA((per * n,)), pltpu.SemaphoreType.DMA((per * n,))], name="gather_weights")(*shards)


def _full_sds(s, kind):
    L, P, _, RH, CS = s.shape
    return SDS((L, P, 2, RH, CS * N_CHIPS) if kind == "col" else (L, P, N_CHIPS, 2, RH, CS), s.dtype)


def _gather_piece(full, kind, cs, k, c):
    if kind == "col":
        return full.at[:, :, c, :, pl.ds(pl.multiple_of(k * cs, cs), cs)]
    return full.at[:, :, k, c]


def _side_gather(shards, kinds, rels, fulls):
    n = len(shards)

    def make(srcs, outs, ssem, rsem):
        xi, yi, ci, j = _place()
        return [pltpu.make_async_remote_copy(
            src_ref=srcs[i].at[:, :, ci], dst_ref=_gather_piece(outs[i], kinds[i], shards[i].shape[-1], j, ci), send_sem=ssem.at[3 * i + r - 1],
            recv_sem=rsem.at[3 * i + r - 1], device_id=_chip_peer(xi, yi, ci, r), device_id_type=MESH) for i in range(n) for r in rels[i]]

    prior = [f for f in fulls if f is not None]
    assert len(prior) in (0, n)
    return _Side(list(shards) + prior, [_full_sds(s, k) for s, k in zip(shards, kinds)], 3 * n, make, n_alias=len(prior))


def _side_gather_pass(fulls, shards, kinds):
    n = len(fulls)

    def make(srcs, outs, ssem, rsem):
        xi, yi, ci, j = _place()
        sib = (xi, yi, 1 - ci)
        cps = []
        for i in range(n):
            cs = shards[i].shape[-1]
            for r in (1, 2, 3):
                got = _gather_piece(outs[i], kinds[i], cs, j ^ r, ci)
                cps.append(pltpu.make_async_remote_copy(src_ref=got, dst_ref=got, send_sem=ssem.at[4 * i + r - 1], recv_sem=rsem.at[4 * i + r - 1],
                                                        device_id=sib, device_id_type=MESH))
            cps.append(pltpu.make_async_remote_copy(src_ref=srcs[i], dst_ref=_full_shard(outs[i], kinds[i], j, cs), send_sem=ssem.at[4 * i + 3],
                                                    recv_sem=rsem.at[4 * i + 3], device_id=sib, device_id_type=MESH))
        return cps

    return _Side(list(shards) + list(fulls), [SDS(f.shape, f.dtype) for f in fulls], 4 * n, make, n_alias=n)


def _sibling_exchange(gviews, kinds, name):
    n = len(gviews)
    outs = [SDS(g.shape[:1] + g.shape[2:] if kind == "col" else g.shape[:2] + g.shape[3:], g.dtype) for g, kind in zip(gviews, kinds)]

    def body(*refs):
        gs, lands, (ssem, rsem) = refs[:n], refs[n:2 * n], refs[2 * n:]
        xi, yi, ci, _ = _place()
        cps = []
        for i in range(n):
            src = gs[i].at[:, 1 - ci] if kinds[i] == "col" else gs[i].at[:, :, 1 - ci]
            cps.append(pltpu.make_async_remote_copy(src_ref=src, dst_ref=lands[i], send_sem=ssem.at[i], recv_sem=rsem.at[i],
                                                    device_id=(xi, yi, 1 - ci), device_id_type=MESH))
            cps[-1].start()
        for cp in cps:
            cp.wait()

    return pl.pallas_call(body, in_specs=[ANY] * n, out_specs=[ANY] * n, out_shape=outs,
                          scratch_shapes=[pltpu.SemaphoreType.DMA((n,)), pltpu.SemaphoreType.DMA((n,))], name=name)(*gviews)


def _side_sibling_exchange(gviews, kinds):
    outs = [SDS(g.shape[:1] + g.shape[2:] if kind == "col" else g.shape[:2] + g.shape[3:], g.dtype) for g, kind in zip(gviews, kinds)]

    def make(gs, lands, ssem, rsem):
        xi, yi, ci, _ = _place()
        return [pltpu.make_async_remote_copy(src_ref=gs[i].at[:, 1 - ci] if kinds[i] == "col" else gs[i].at[:, :, 1 - ci], dst_ref=lands[i],
                                             send_sem=ssem.at[i], recv_sem=rsem.at[i], device_id=(xi, yi, 1 - ci), device_id_type=MESH)
                for i in range(len(gs))]

    return _Side(gviews, outs, len(gviews), make)


def _chip_sums(gs, lands, kinds, jc, name):
    n = len(gs)
    args, in_specs, out_specs, out_shape = [], [], [], []
    for g, land, kind in zip(gs, lands, kinds):
        if kind == "col":
            P, _, RH, C = g.shape
            CS = C // N_CHIPS
            in_specs += [pl.BlockSpec((P, None, RH, CS), lambda r, jc: (0, jc[1], 0, jc[0] ^ r)),
                         pl.BlockSpec((P, RH, CS), lambda r, jc: (0, 0, jc[0] ^ r))]
        else:
            P, _, _, RH, CS = g.shape
            in_specs += [pl.BlockSpec((P, None, None, RH, CS), lambda r, jc: (0, jc[0] ^ r, jc[1], 0, 0)),
                         pl.BlockSpec((P, None, RH, CS), lambda r, jc: (0, jc[0] ^ r, 0, 0))]
        args += [g, land]
        out_specs += [pl.BlockSpec((P, RH, CS), lambda r, jc: (0, 0, 0)), pl.BlockSpec((None, P, RH, CS), lambda r, jc: (r, 0, 0, 0))]
        out_shape += [SDS((P, RH, CS), F32), SDS((N_CHIPS, P, RH, CS), BF)]

    def body(jc_ref, *refs):
        ins, outs = refs[:2 * n], refs[2 * n:]
        for i in range(n):
            s = ins[2 * i][...] + ins[2 * i + 1][...].astype(F32)
            outs[2 * i + 1][...] = s.astype(BF)

            @pl.when(pl.program_id(0) == 0)
            def _():
                outs[2 * i][...] = s

    outs = _call(body, args, grid=(N_CHIPS,), in_specs=in_specs, out_specs=out_specs, out_shape=out_shape, semantics=("arbitrary",),
                 name=name, prefetch=(jc,))
    return outs[0::2], outs[1::2]


def _chip_exchange_copies(srcs, lands, ssem, rsem):
    xi, yi, ci, _ = _place()
    return [pltpu.make_async_remote_copy(src_ref=srcs[i].at[r], dst_ref=lands[i].at[r], send_sem=ssem.at[3 * i + r - 1],
                                         recv_sem=rsem.at[3 * i + r - 1], device_id=_chip_peer(xi, yi, ci, r), device_id_type=MESH)
            for i in range(len(srcs)) for r in (1, 2, 3)]


def _side_chip_exchange(pieces):
    return _Side(pieces, [SDS(p.shape, p.dtype) for p in pieces], 3 * len(pieces), _chip_exchange_copies)


FINAL_SUM_STEPS = 2


def _final_sums(owns, lands, jc, shards, l, L, name, side=None):
    n = len(owns)
    args, in_specs, out_specs, out_shape = [], [], [], []
    for own, land in zip(owns, lands):
        P, RH, CS = own.shape
        hr = RH // FINAL_SUM_STEPS
        in_specs += [pl.BlockSpec((P, hr, CS), lambda h, jc: (0, h, 0))]
        in_specs += [pl.BlockSpec((None, P, hr, CS), functools.partial(lambda r, h, jc: (r, 0, h, 0), r)) for r in (1, 2, 3)]
        args += [own, land, land, land]
        out_specs.append(pl.BlockSpec((None, P, None, hr, CS), lambda h, jc: (l, 0, jc[1], h, 0)))
        out_shape.append(SDS((L, P, 2, RH, CS), F32))
    aliases = None
    if shards is not None:
        aliases = {4 * n + i: i for i in range(n)}
        in_specs += [ANY] * n
        args += list(shards)

    def body(jc_ref, *refs):
        outs = refs[len(args):]
        for i in range(n):
            o, a, b, c = (refs[4 * i + t][...] for t in range(4))
            outs[i][...] = ((o + a.astype(F32)) + b.astype(F32)) + c.astype(F32)

    return _call(body, args, grid=(FINAL_SUM_STEPS,), in_specs=in_specs, out_specs=out_specs, out_shape=out_shape, semantics=("arbitrary",),
                 name=name, prefetch=(jc,), aliases=aliases, side=side)


def _halves_exchange(shards, l, name):
    n = len(shards)

    def body(*refs):
        outs, (ssem, rsem) = refs[n:2 * n], refs[2 * n:]
        xi, yi, ci, _ = _place()
        cps = []
        for i in range(n):
            mine = outs[i].at[l, :, ci]
            cps.append(pltpu.make_async_remote_copy(src_ref=mine, dst_ref=mine, send_sem=ssem.at[i], recv_sem=rsem.at[i],
                                                    device_id=(xi, yi, 1 - ci), device_id_type=MESH))
            cps[-1].start()
        for i in range(n):
            land = outs[i].at[l, :, 1 - ci]
            pltpu.make_async_remote_copy(src_ref=land, dst_ref=land, send_sem=ssem.at[i], recv_sem=rsem.at[i],
                                         device_id=(xi, yi, 1 - ci), device_id_type=MESH).wait_recv()
        for cp in cps:
            cp.wait_send()

    return pl.pallas_call(body, in_specs=[ANY] * n, out_specs=[ANY] * n, out_shape=[SDS(s.shape, s.dtype) for s in shards],
                          input_output_aliases={i: i for i in range(n)},
                          scratch_shapes=[pltpu.SemaphoreType.DMA((n,)), pltpu.SemaphoreType.DMA((n,))], name=name)(*shards)


def _reduce_small(part, pieces):
    NR, Wd = part.shape
    ND = 2 * N_CHIPS
    n = len(pieces)

    def body(p_ref, *refs):
        srcs, o_ref, lands, (land, ssem, rsem, xs, xr) = refs[:n], refs[n], refs[n + 1:2 * n + 1], refs[2 * n + 1:]
        exchange = _chip_exchange_copies(srcs, lands, xs, xr)
        for cp in exchange:
            cp.start()
        xi, yi, ci, j = _place()
        me = 2 * j + ci
        land[me] = p_ref[...]
        cps = []
        for rr in range(1, ND):
            dev = (xi ^ (rr >> 2), yi ^ ((rr >> 1) & 1), ci ^ (rr & 1))
            cps.append(pltpu.make_async_remote_copy(src_ref=p_ref, dst_ref=land.at[me], send_sem=ssem.at[rr - 1], recv_sem=rsem.at[rr - 1],
                                                    device_id=dev, device_id_type=MESH))
            cps[-1].start()
        for rr in range(1, ND):
            got = land.at[me ^ rr]
            pltpu.make_async_remote_copy(src_ref=got, dst_ref=got, send_sem=ssem.at[rr - 1], recv_sem=rsem.at[rr - 1],
                                         device_id=(xi, yi, ci), device_id_type=MESH).wait_recv()
        acc = land[0]
        for d in range(1, ND):
            acc = acc + land[d]
        o_ref[...] = acc
        for cp in cps:
            cp.wait_send()
        for cp in exchange:
            cp.wait()

    vm = pl.BlockSpec(memory_space=pltpu.VMEM)
    outs = pl.pallas_call(
        body, in_specs=[vm] + [ANY] * n, out_specs=[vm] + [ANY] * n, out_shape=[SDS((NR, Wd), F32)] + [SDS(p.shape, p.dtype) for p in pieces],
        scratch_shapes=[pltpu.VMEM((ND, NR, Wd), F32), pltpu.SemaphoreType.DMA((ND - 1,)), pltpu.SemaphoreType.DMA((ND - 1,)),
                        pltpu.SemaphoreType.DMA((3 * n,)), pltpu.SemaphoreType.DMA((3 * n,))],
        name="small_grad_allreduce")(part, *pieces)
    return outs[0], list(outs[1:])


def _adamw_update(w_ref, g_ref, m_ref, v_ref, d_ref, mo_ref, vo_ref):
    g = g_ref[...]
    m = ADAM_B1 * m_ref[...] + (1.0 - ADAM_B1) * g
    v = ADAM_B2 * v_ref[...] + (1.0 - ADAM_B2) * jnp.square(g)
    m_hat = m / (1.0 - ADAM_B1 ** ADAM_STEP)
    v_hat = v / (1.0 - ADAM_B2 ** ADAM_STEP)
    d_ref[...] = -ADAM_LR * (m_hat / (jnp.sqrt(v_hat) + ADAM_EPS) + ADAM_WD * w_ref[...])
    mo_ref[...] = m
    vo_ref[...] = v


ADAMW_STEPS = 8


def _adamw_layer(ws, gs, ms, vs, prev, l, name, side=None):
    n = len(ws)
    args, in_specs, out_specs, out_shape = [], [], [], []
    for w, g, m, v in zip(ws, gs, ms, vs):
        L, R, C = w.shape
        blk = pl.BlockSpec((None, R // ADAMW_STEPS, C), lambda i: (l, i, 0))
        in_specs += [blk] * 4
        args += [w, g, m, v]
        out_specs += [blk] * 3
        out_shape += [SDS((L, R, C), F32)] * 3
    aliases = None
    if prev is not None:
        aliases = {4 * n + i: i for i in range(3 * n)}
        in_specs += [ANY] * (3 * n)
        args += list(prev)

    def body(*refs):
        outs = refs[len(args):]
        for i in range(n):
            _adamw_update(*refs[4 * i:4 * i + 4], *outs[3 * i:3 * i + 3])

    return _call(body, args, grid=(ADAMW_STEPS,), in_specs=in_specs, out_specs=out_specs, out_shape=out_shape, semantics=("parallel",),
                 name=name, aliases=aliases, side=side)


def _adamw(w, g, m, v, name):
    shape = w.shape
    C = shape[-1]
    R = w.size // C
    tb = _tile(R, max(8, (1 << 18) // C), 8)
    body = functools.partial(_adamw_update)
    blk = pl.BlockSpec((tb, C), lambda i: (i, 0))
    outs = pl.pallas_call(body, grid=(R // tb,), in_specs=[blk] * 4, out_specs=[blk] * 3, out_shape=[SDS((R, C), F32)] * 3,
                          compiler_params=_params("parallel"), name=name)(*[t.reshape(R, C) for t in (w, g, m, v)])
    return [t.reshape(shape) for t in outs]


WEIGHTS = ("mix_norm_g", "w_in", "conv_dw_w", "conv_dw_b", "conv_ln_g", "conv_ln_b", "w_conv_out", "w_pool_grp", "pool_scale", "w_out",
           "xattn_norm_g", "mem_norm_g", "w_q", "w_kv", "w_o", "ffn_norm_g", "w_up", "ffn_dw_w", "w_down", "final_norm_g")
VECTORS = ("mix_norm_g", "conv_dw_b", "conv_ln_g", "conv_ln_b", "pool_scale", "xattn_norm_g", "mem_norm_g", "ffn_norm_g", "final_norm_g")


def _shard_view(t, kind):
    L, P, R, C = t.shape
    return t.reshape(L, P, 2, R // 2, C)


def _rows(t, width):
    return t.reshape(-1, width)


def _pack(parts):
    return jnp.concatenate([jnp.pad(p, ((0, (-p.shape[0]) % 8), (0, 0))) for p in parts], axis=0)


def kernel(x, mem, mix_norm_g, w_in, conv_dw_w, conv_dw_b, conv_ln_g, conv_ln_b, w_conv_out, w_pool_grp, pool_scale, w_out, xattn_norm_g, mem_norm_g, w_q, w_kv, w_o, ffn_norm_g, w_up, ffn_dw_w, w_down, final_norm_g, loss_target, m_mix_norm_g, m_w_in, m_conv_dw_w, m_conv_dw_b, m_conv_ln_g, m_conv_ln_b, m_w_conv_out, m_w_pool_grp, m_pool_scale, m_w_out, m_xattn_norm_g, m_mem_norm_g, m_w_q, m_w_kv, m_w_o, m_ffn_norm_g, m_w_up, m_ffn_dw_w, m_w_down, m_final_norm_g, v_mix_norm_g, v_w_in, v_conv_dw_w, v_conv_dw_b, v_conv_ln_g, v_conv_ln_b, v_w_conv_out, v_w_pool_grp, v_pool_scale, v_w_out, v_xattn_norm_g, v_mem_norm_g, v_w_q, v_w_kv, v_w_o, v_ffn_norm_g, v_w_up, v_ffn_dw_w, v_w_down, v_final_norm_g):
    w = dict(mix_norm_g=mix_norm_g, w_in=w_in, conv_dw_w=conv_dw_w, conv_dw_b=conv_dw_b, conv_ln_g=conv_ln_g, conv_ln_b=conv_ln_b,
             w_conv_out=w_conv_out, w_pool_grp=w_pool_grp, pool_scale=pool_scale, w_out=w_out, xattn_norm_g=xattn_norm_g,
             mem_norm_g=mem_norm_g, w_q=w_q, w_kv=w_kv, w_o=w_o, ffn_norm_g=ffn_norm_g, w_up=w_up, ffn_dw_w=ffn_dw_w, w_down=w_down,
             final_norm_g=final_norm_g)
    m = dict(zip(WEIGHTS, (m_mix_norm_g, m_w_in, m_conv_dw_w, m_conv_dw_b, m_conv_ln_g, m_conv_ln_b, m_w_conv_out, m_w_pool_grp, m_pool_scale,
                           m_w_out, m_xattn_norm_g, m_mem_norm_g, m_w_q, m_w_kv, m_w_o, m_ffn_norm_g, m_w_up, m_ffn_dw_w, m_w_down, m_final_norm_g)))
    v = dict(zip(WEIGHTS, (v_mix_norm_g, v_w_in, v_conv_dw_w, v_conv_dw_b, v_conv_ln_g, v_conv_ln_b, v_w_conv_out, v_w_pool_grp, v_pool_scale,
                           v_w_out, v_xattn_norm_g, v_mem_norm_g, v_w_q, v_w_kv, v_w_o, v_ffn_norm_g, v_w_up, v_ffn_dw_w, v_w_down, v_final_norm_g)))
    xi, yi, ci, j = _place()
    jc = jnp.stack([j, ci]).astype(jnp.int32)
    L = w_in.shape[0]
    G = len(POOL_WINDOWS)
    kinds = dict(BIG)

    def to_mat(name, t):
        if name == "w_pool":
            return jnp.swapaxes(t, 2, 3)
        return t[:, None]

    def from_mat(name, t):
        if name == "w_pool":
            return jnp.swapaxes(t, 2, 3)
        return t[:, 0]

    src = {name: w["w_pool_grp" if name == "w_pool" else name] for name, _ in BIG}

    KC, cs_c = conv_dw_w.shape[1], conv_dw_w.shape[2]
    KF, cs_f = ffn_dw_w.shape[1], ffn_dw_w.shape[2]
    taps = jnp.concatenate([conv_dw_w.reshape(L * KC, cs_c), ffn_dw_w.reshape(L * KF * (cs_f // cs_c), cs_c)], axis=0)
    n_taps = taps.shape[0]
    taps = jnp.pad(taps, ((0, (-n_taps) % 16), (0, 0)))
    names = [name for name, _ in BIG]
    mats = {name: to_mat(name, src[name]).astype(BF) for name in names}

    def layer_shards(l, subset):
        return [_shard_view(mats[name][l:l + 1], kinds[name]) for name in subset]

    def as_weight(name, f):
        return f.reshape(G if name == "w_pool" else 1, -1, f.shape[-1])

    assert L == 2
    fulls = _gather_weights(layer_shards(0, GATHER_FIRST) + [_shard_view(taps[None, None], "row")], [kinds[name] for name in GATHER_FIRST] + ["row"])
    ready = {(name, 0): as_weight(name, f) for name, f in zip(GATHER_FIRST, fulls)}
    landing = {}
    taps_all = fulls[-1].reshape(N_CHIPS, -1, cs_c)[:, :n_taps]
    V = {name: w[name] for name in VECTORS}
    V["conv_dw_w"] = taps_all[:, :L * KC].reshape(N_CHIPS, L, KC, cs_c).transpose(1, 2, 0, 3).reshape(L, KC, N_CHIPS * cs_c)
    V["ffn_dw_w"] = taps_all[:, L * KC:].reshape(N_CHIPS, L, KF, cs_f).transpose(1, 2, 0, 3).reshape(L, KF, N_CHIPS * cs_f)

    Bn, S, D = x.shape
    Mn = mem.shape[1]
    dims = (Bn, S, Mn, D, conv_dw_b.shape[1], w_down.shape[1] * N_CHIPS)
    xt = x.reshape(Bn * S, D)
    memf = mem.reshape(Bn * Mn, D)
    mem_n = _rms_fwd(memf, V["mem_norm_g"], "mem_norm")

    class LayerWeights:
        def __init__(self, l):
            self.l = l

        def __getitem__(self, name):
            return ready[(name, self.l)]

    def carried_gather(entries):
        return lambda: _side_gather([layer_shards(lw, [nm])[0] for nm, lw, _ in entries], [kinds[nm] for nm, _, _ in entries],
                                    [rels for _, _, rels in entries], [landing.get((nm, lw)) for nm, lw, _ in entries])

    def carried_pass(group):
        return lambda: _side_gather_pass([landing.pop(t) for t in group], [layer_shards(lw, [nm])[0] for nm, lw in group],
                                         [kinds[nm] for nm, _ in group])

    def on_land(l):
        def handle(key, fulls):
            if (l, key) in FWD_CARRY:
                landing.update({(nm, lw): f for (nm, lw, _), f in zip(FWD_CARRY[(l, key)], fulls)})
            else:
                ready.update({t: as_weight(t[0], f) for t, f in zip(PASS_CARRY[(l, key)], fulls)})
        return handle

    saved, W = [], []
    ht = _rms_fwd(xt, V["mix_norm_g"][0], "l0_mix_norm")
    for l in range(L):
        by_key = {key: carried_gather(entries) for (cl, key), entries in FWD_CARRY.items() if cl == l}
        by_key.update({key: carried_pass(group) for (cl, key), group in PASS_CARRY.items() if cl == l})
        sides = _Sides(by_key, on_land=on_land(l))
        W.append(LayerWeights(l))
        xt, ht, sv = _layer_fwd(xt, ht, mem_n, W[l], V, l, dims, sides, V["mix_norm_g"][l + 1] if l + 1 < L else None)
        saved.append(sv)
    loss, dx, dgf = _loss_bwd(xt, V["final_norm_g"], loss_target.reshape(Bn * S, D), "loss")
    loss = lax.psum(loss[0, 0], ("x", "y", "c"))

    late_names = [name for name in names if name not in EARLY]

    def views(gw, subset, twin):
        out = []
        for name in subset:
            g = gw[name][twin] if gw[name][twin].ndim == 3 else gw[name][twin][None]
            P, R, C = g.shape
            out.append(g.reshape(P, 2, R // 2, C) if kinds[name] == "col" else g.reshape(P, N_CHIPS, 2, R // (2 * N_CHIPS), C))
        return out

    def group_kinds(subset):
        return [kinds[name] for name in subset]

    class Reduction:
        def __init__(self, gw, subset, l, tag, first, table):
            self.gw, self.subset, self.l, self.tag, self.first, self.table = gw, subset, l, tag, first, table

        def sides(self):
            by_key = {self.first: lambda: _side_sibling_exchange(views(self.gw, self.subset, 1), group_kinds(self.subset))}
            by_key.update({key: (lambda names_=names_: _side_chip_exchange([self.pieces[nm] for nm in names_])) for key, names_ in self.table.items()})
            return by_key

        def on_land(self, key, landed):
            if key == self.first:
                self.sums(landed)
            elif key in self.table:
                got[self.l].update(zip(self.table[key], landed))

        def sums(self, lands):
            own, pieces = _chip_sums(views(self.gw, self.subset, 0), lands, group_kinds(self.subset), jc, f"chip_sums_{self.tag}_l{self.l}")
            owns[self.l].update(zip(self.subset, own))
            self.pieces = dict(zip(self.subset, pieces))

    def riding(reductions):
        return _Sides({key: side for r in reductions for key, side in r.sides().items()},
                      on_land=lambda key, landed: [r.on_land(key, landed) for r in reductions])

    dxb, dmem_n = dx, None
    smalls, owns, got = [None] * L, [{} for _ in range(L)], [{} for _ in range(L)]
    late = None
    for l in reversed(range(L)):
        dx, dxb, gw, sm = _layer_bwd_mlp(dx, dxb, saved[l], W[l], V, l, dims, riding([late] if late is not None else []))
        gw_mix = {}
        reductions = [Reduction(gw, EARLY, l, "mlp", "d_att", BWD_CARRY_EARLY)]
        if l == 0:
            reductions += [Reduction(gw_mix, names_, 0, tag, first, table) for tag, names_, first, table in BWD_LAST_LAYER]
        dx, dxb, dmem_n, sm2 = _layer_bwd_mix(dx, dxb, dmem_n, saved[l], mem_n, W[l], V, l, dims, riding(reductions), gw_mix)
        smalls[l] = {**sm, **sm2}
        late = Reduction(gw_mix, late_names, l, "mix", "d_gact", BWD_CARRY_LATE) if l > 0 else None
    last = Reduction(gw_mix, ("w_in",), 0, "in", None, {})
    last.sums(_sibling_exchange(views(gw_mix, last.subset, 1), group_kinds(last.subset), "grad_sibling_exchange_in_l0"))
    grad_x = dx.reshape(Bn, S, D)
    _, _, dgm = _rms_bwd(memf, V["mem_norm_g"], dmem_n, None, "mem_norm_b")
    small = {k: jnp.stack([sm[k] for sm in smalls]) if k in ("conv_dw_w", "ffn_dw_w") else jnp.concatenate([sm[k] for sm in smalls], axis=0)
             for k in smalls[0]}
    small["mem_norm_g"] = dgm
    small["final_norm_g"] = dgf

    small_w = conv_dw_b.shape[1]
    order = VECTORS + ("conv_dw_w", "ffn_dw_w")
    parts = [_rows(small[name], small_w) for name in order]
    counts = [p.shape[0] for p in parts]
    summed, landed_last = _reduce_small(_pack(parts), [last.pieces[name] for name in last.subset])
    got[0].update(zip(last.subset, landed_last))

    keys = ["w_pool_grp" if name == "w_pool" else name for name in names]
    rows3 = lambda t: t.reshape(t.shape[0], -1, t.shape[-1])
    wmv = [[rows3(to_mat(name, d[key])) for name, key in zip(names, keys)] for d in (w, m, v)]
    gshards, updates = None, None
    for l in reversed(range(L)):
        gshards = _final_sums([owns[l][name] for name in names], [got[l][name] for name in names], jc, gshards, l, L, f"final_sums_l{l}")
        gshards = _halves_exchange(gshards, l, f"grad_halves_exchange_l{l}")
        updates = _adamw_layer(wmv[0], [rows3(t) for t in gshards], wmv[1], wmv[2], updates, l, f"adamw_l{l}")
    grads, delta, new_m, new_v = {}, {}, {}, {}
    for i, (name, key) in enumerate(zip(names, keys)):
        Lg, P, _, RH, CS = gshards[i].shape
        grads[key] = from_mat(name, gshards[i].reshape(Lg, P, 2 * RH, CS))
        for d, t in zip((delta, new_m, new_v), updates[3 * i:3 * i + 3]):
            d[key] = from_mat(name, t.reshape(Lg, P, 2 * RH, CS))

    off = 0
    for name, cnt in zip(order, counts):
        t = summed[off:off + cnt]
        off += cnt + (-cnt) % 8
        if name in VECTORS:
            grads[name] = t.reshape(w[name].shape)
        else:
            full = t.reshape(small[name].shape)
            cs = w[name].shape[2]
            grads[name] = lax.dynamic_slice_in_dim(full, j * cs, cs, axis=2)

    vec =[_pack([_rows(d[name], small_w) for name in VECTORS]) for d in (w, grads, m, v)]
    outs = _adamw(*vec, "adamw_vectors")
    off = 0
    for name in VECTORS:
        cnt = w[name].size // small_w
        for d, t in zip((delta, new_m, new_v), outs):
            d[name] = t[off:off + cnt].reshape(w[name].shape)
        off += cnt + (-cnt) % 8
    for name in ("conv_dw_w", "ffn_dw_w"):
        delta[name], new_m[name], new_v[name] = _adamw(w[name], grads[name], m[name], v[name], "adamw_" + name)

    return (loss, grad_x, *[grads[k] for k in WEIGHTS], *[delta[k] for k in WEIGHTS], *[new_m[k] for k in WEIGHTS], *[new_v[k] for k in WEIGHTS])
```

```python
import functools
import math

import jax
import jax.numpy as jnp
from jax import lax
from jax.experimental import pallas as pl
from jax.experimental.pallas import tpu as pltpu

F32 = jnp.float32
BF = jnp.bfloat16
SDS = jax.ShapeDtypeStruct
MESH = pl.DeviceIdType.MESH
ANY = pl.BlockSpec(memory_space=pl.ANY)

EPS = 1e-6
XA_HEADS = 4
POOL_WINDOWS = (2, 4, 8, 16)
N_CHIPS = 4
ADAM_LR, ADAM_B1, ADAM_B2, ADAM_EPS, ADAM_WD, ADAM_STEP = 0.001, 0.9, 0.999, 1e-08, 0.01, 10

LANES = 128
ROW_BLOCK = 512
VMEM_LIMIT = 56 * 1024 * 1024


def _params(*sem):
    return pltpu.CompilerParams(dimension_semantics=sem if sem else None, vmem_limit_bytes=VMEM_LIMIT)


def _tile(n, cap, mult=LANES):
    if n <= cap:
        return n
    for t in range(cap - cap % mult, 0, -mult):
        if n % t == 0:
            return t
    return n


_DN = {"nn": (((1,), (0,)), ((), ())), "nt": (((1,), (1,)), ((), ())), "tn": (((0,), (0,)), ((), ()))}


class _Side:
    def __init__(self, ins, outs, n, make, n_alias=0):
        self.ins, self.outs, self.n, self.make, self.n_alias = list(ins), list(outs), n, make, n_alias


def _call(body, args, *, grid, in_specs, out_specs, out_shape, semantics, name, scratch_shapes=(), side=None, prefetch=(), aliases=None):
    n_pf = len(prefetch)
    aliases = {n_pf + i: o for i, o in (aliases or {}).items()}
    n_in, n_out, n_scr = len(args), len(out_shape), len(scratch_shapes)
    n_si, n_so = (len(side.ins), len(side.outs)) if side is not None else (0, 0)
    if side is not None:
        aliases.update({n_pf + n_in + n_si - side.n_alias + i: n_out + i for i in range(side.n_alias)})

    def carrying(*refs):
        pf, refs = refs[:n_pf], refs[n_pf:]
        ins, s_in = refs[:n_in], refs[n_in:n_in + n_si]
        outs, s_out = refs[n_in + n_si:n_in + n_si + n_out], refs[n_in + n_si + n_out:n_in + n_si + n_out + n_so]
        scr = refs[n_in + n_si + n_out + n_so:]
        if side is None:
            return body(*pf, *ins, *outs, *scr)
        copies = side.make(s_in, s_out, scr[n_scr], scr[n_scr + 1])
        ids = [pl.program_id(d) for d in range(len(grid))]
        first, last = ids[0] == 0, ids[0] == grid[0] - 1
        for d in range(1, len(grid)):
            first, last = first & (ids[d] == 0), last & (ids[d] == grid[d] - 1)

        @pl.when(first)
        def _():
            for cp in copies:
                cp.start()

        body(*pf, *ins, *outs, *scr[:n_scr])

        @pl.when(last)
        def _():
            for cp in copies:
                cp.wait()

    sems = [pltpu.SemaphoreType.DMA((side.n,)), pltpu.SemaphoreType.DMA((side.n,))] if side is not None else []
    outs = pl.pallas_call(
        carrying, grid_spec=pltpu.PrefetchScalarGridSpec(
            num_scalar_prefetch=n_pf, grid=grid, in_specs=list(in_specs) + [ANY] * n_si, out_specs=list(out_specs) + [ANY] * n_so,
            scratch_shapes=list(scratch_shapes) + sems),
        out_shape=list(out_shape) + (side.outs if side is not None else []), input_output_aliases=aliases,
        compiler_params=_params(*(semantics if side is None else ["arbitrary"] * len(grid))), name=name)(
            *prefetch, *args, *(side.ins if side is not None else []))
    return list(outs) if side is None else (list(outs[:n_out]), list(outs[n_out:]))


def _call1(body, args, *, out_spec, out_shape, side=None, **kw):
    got = _call(body, args, out_specs=[out_spec], out_shape=[out_shape], side=side, **kw)
    return got[0] if side is None else (got[0][0], got[1])


MM_VMEM_BUDGET = 40 * 1024 * 1024
MM_STEP_MACS = 2200 * 1024 * 1024
MXU_WIDTH = 256
MM_STEP_COST_BYTES = 1 << 20


def _divisors(n):
    return [t for t in range(LANES, n + 1, LANES) if n % t == 0] or [n]


def _mm_tiles(M, N, K, a_bytes, b_bytes, o_bytes, n_unit=None):
    best = None
    for tk in _divisors(K):
        for tm in _divisors(M):
            for tn in _divisors(N if n_unit is None else n_unit):
                nk = K // tk
                foot = 2 * (tm * tk * a_bytes + tk * tn * b_bytes + tm * tn * o_bytes) + (tm * tn * 4 if nk > 1 else 0)
                if (foot > MM_VMEM_BUDGET or tm * tn * tk > MM_STEP_MACS or tn < min(N if n_unit is None else n_unit, MXU_WIDTH)
                        or tm < min(M, MXU_WIDTH)):
                    continue
                steps = (M // tm) * (N // tn) * nk
                traffic = M * K * a_bytes * (N // tn if nk > 1 else 1) + K * N * b_bytes * (M // tm) + M * N * o_bytes
                exposed = tm * tk * a_bytes + tk * tn * b_bytes + tm * tn * o_bytes
                cost = traffic + exposed + steps * MM_STEP_COST_BYTES + (nk - 1) * M * N * 8
                if best is None or cost < best[0]:
                    best = (cost, tm, tn, tk)
    assert best is not None, (M, N, K)
    return best[1:]


def _mm(a, b, dims, out_dtype, name, res=None, bl=None, side=None, twin=None, b_halves=False, part=None):
    bs = b.shape[1:] if bl is not None or b_halves else b.shape
    if dims == "nn":
        (M, K), (K2, N) = a.shape, bs
    elif dims == "nt":
        (M, K), (N, K2) = a.shape, bs
    else:
        (K, M), (K2, N) = a.shape, bs
    assert K == K2, (name, a.shape, b.shape)
    n_half = N
    if b_halves:
        assert dims == "tn" and bl is None
        N = 2 * n_half
    n_total, n_first, earlier = part if part is not None else (N, 0, None)
    tm, tn, tk = _mm_tiles(M, N, K, a.dtype.itemsize, b.dtype.itemsize, jnp.dtype(out_dtype).itemsize
                           + (res.dtype.itemsize if res is not None else 0) + (jnp.dtype(twin).itemsize if twin is not None else 0),
                           n_unit=math.gcd(n_half, n_first) if b_halves or n_first else None)
    nk = K // tk
    lead = (None,) if bl is not None or b_halves else ()
    pre = (lambda *ix: (bl,) + ix) if bl is not None else (lambda *ix: ix)
    if b_halves:
        per_half = n_half // tn
        pre = lambda k, j: (j // per_half, k, j % per_half)
    if dims == "tn":
        a_spec = pl.BlockSpec((tk, tm), lambda i, j, k: (k, i))
    else:
        a_spec = pl.BlockSpec((tm, tk), lambda i, j, k: (i, k))
    if dims == "nt":
        b_spec = pl.BlockSpec(lead + (tn, tk), lambda i, j, k: pre(j, k))
    else:
        b_spec = pl.BlockSpec(lead + (tk, tn), lambda i, j, k: pre(k, j))
    assert n_first % tn == 0 and (part is None or res is None)
    o_spec = pl.BlockSpec((tm, tn), lambda i, j, k: (i, n_first // tn + j))
    in_specs, args = [a_spec, b_spec], [a, b]
    if res is not None:
        in_specs.append(o_spec)
        args.append(res)
    n_main = len(args)
    n_out = 1 if twin is None else 2
    aliases = None
    if earlier is not None:
        earlier = list(earlier) if twin is not None else [earlier]
        aliases = {n_main + t: t for t in range(n_out)}
        in_specs += [ANY] * n_out
        args += earlier

    def body(*refs):
        refs = refs[:n_main] + refs[len(args):]
        a_ref, b_ref = refs[0], refs[1]
        r_ref = refs[2] if res is not None else None
        o_ref = refs[n_main]
        p = lax.dot_general(a_ref[...].astype(BF), b_ref[...].astype(BF), _DN[dims], preferred_element_type=F32)

        def finish(t):
            if r_ref is not None:
                t = t + r_ref[...]
            o_ref[...] = t.astype(out_dtype)
            if twin is not None:
                refs[n_main + 1][...] = t.astype(twin)

        if nk == 1:
            finish(p)
        else:
            acc = refs[n_main + n_out]
            k = pl.program_id(2)

            @pl.when(k == 0)
            def _():
                acc[...] = p

            @pl.when(k > 0)
            def _():
                acc[...] += p

            @pl.when(k == nk - 1)
            def _():
                finish(acc[...])

    got = _call(body, args, grid=(M // tm, N // tn, nk), in_specs=in_specs, out_specs=[o_spec] * n_out,
                out_shape=[SDS((M, n_total), out_dtype)] + ([SDS((M, n_total), twin)] if twin is not None else []),
                scratch_shapes=[pltpu.VMEM((tm, tn), F32)] if nk > 1 else [], semantics=("parallel", "parallel", "arbitrary"),
                name=name, side=side, aliases=aliases)
    outs, landed = (got, None) if side is None else got
    out = outs[0] if twin is None else (outs[0], outs[1])
    return out if side is None else (out, landed)


def _rms(x, g):
    return x * lax.rsqrt(jnp.mean(x * x, axis=-1, keepdims=True) + EPS) * g


def _ln_silu(x, g, b):
    mu = jnp.mean(x, axis=-1, keepdims=True)
    xc = x - mu
    var = jnp.mean(xc * xc, axis=-1, keepdims=True)
    return jax.nn.silu(xc * lax.rsqrt(var + EPS) * g + b)


def _merge(gc, gp, yc, yp, ps):
    return jax.nn.sigmoid(gc) * yc + jax.nn.sigmoid(gp) * (yp * ps)


def _gated(gate, val):
    return jax.nn.gelu(gate) * val


def _rms_fwd(x, g, name):
    T, D = x.shape
    tb = _tile(T, ROW_BLOCK, 8)

    def body(x_ref, g_ref, o_ref):
        o_ref[...] = _rms(x_ref[...], g_ref[...]).astype(BF)

    row = pl.BlockSpec((tb, D), lambda i: (i, 0))
    return pl.pallas_call(body, grid=(T // tb,), in_specs=[row, pl.BlockSpec((1, D), lambda i: (0, 0))], out_specs=row,
                          out_shape=SDS((T, D), BF), compiler_params=_params("parallel"), name=name)(x, g.reshape(1, D))


def _rms_bwd(x, g, dh, dres, name):
    T, D = x.shape
    tb = _tile(T, ROW_BLOCK, 8)

    def body(*refs):
        if dres is not None:
            x_ref, g_ref, dh_ref, dres_ref, dx_ref, dxb_ref, dg_ref = refs
        else:
            x_ref, g_ref, dh_ref, dx_ref, dxb_ref, dg_ref = refs
        _, vjp = jax.vjp(_rms, x_ref[...], g_ref[...])
        dx, dg = vjp(dh_ref[...].astype(F32))
        if dres is not None:
            dx = dx + dres_ref[...]
        dx_ref[...] = dx
        dxb_ref[...] = dx.astype(BF)

        @pl.when(pl.program_id(0) == 0)
        def _():
            dg_ref[...] = jnp.zeros_like(dg_ref)

        dg_ref[...] += dg

    row = pl.BlockSpec((tb, D), lambda i: (i, 0))
    vec = pl.BlockSpec((1, D), lambda i: (0, 0))
    ins = [x, g.reshape(1, D), dh] + ([dres] if dres is not None else [])
    return pl.pallas_call(
        body, grid=(T // tb,), in_specs=[row, vec, row] + ([row] if dres is not None else []), out_specs=[row, row, vec],
        out_shape=[SDS((T, D), F32), SDS((T, D), BF), SDS((1, D), F32)], compiler_params=_params("arbitrary"), name=name)(*ins)


def _row_tile(M, K, N, per_row_bytes):
    fixed = K * N * 2
    fit = [t for t in _divisors(M) if fixed + 2 * t * per_row_bytes <= MM_VMEM_BUDGET and t * K * N <= 2 * MM_STEP_MACS]
    return max(fit) if fit else min(_divisors(M))


def _mm_rms_fwd(a, b, res, g, name, side=None):
    M, K = a.shape
    N = b.shape[2]
    tm = _row_tile(M, K, N, K * 2 + N * (4 + 4 + 2))

    def body(a_ref, b_ref, r_ref, g_ref, x_ref, h_ref):
        x = r_ref[...] + lax.dot_general(a_ref[...], b_ref[...], _DN["nn"], preferred_element_type=F32)
        x_ref[...] = x
        h_ref[...] = _rms(x, g_ref[...]).astype(BF)

    row = pl.BlockSpec((tm, N), lambda i: (i, 0))
    return _call(body, (a, b, res, g.reshape(1, N)), grid=(M // tm,),
                 in_specs=[pl.BlockSpec((tm, K), lambda i: (i, 0)), pl.BlockSpec((None, K, N), lambda i: (0, 0, 0), pipeline_mode=pl.Buffered(1)), row,
                           pl.BlockSpec((1, N), lambda i: (0, 0))],
                 out_specs=[row, row], out_shape=[SDS((M, N), F32), SDS((M, N), BF)], semantics=("parallel",), name=name, side=side)


def _mm_rms_bwd(a_parts, b, x, g, dres, name, side=None):
    n_a = len(a_parts)
    M = a_parts[0].shape[-2]
    N, K = b.shape[1:]
    assert K == sum(p.shape[-1] * (p.shape[0] if p.ndim == 3 else 1) for p in a_parts)
    tm = _row_tile(M, K, N, K * 2 + N * (4 + 4 + 4 + 2))

    def body(*refs):
        a_refs, (b_ref, x_ref, g_ref, r_ref, dx_ref, dxb_ref, dg_ref) = refs[:n_a], refs[n_a:]
        dh, col = None, 0
        for p, a_ref in zip(a_parts, a_refs):
            for blk in ([a_ref[h] for h in range(p.shape[0])] if p.ndim == 3 else [a_ref[...]]):
                t = lax.dot_general(blk, b_ref[:, col:col + p.shape[-1]], _DN["nt"], preferred_element_type=F32)
                dh = t if dh is None else dh + t
                col += p.shape[-1]
        _, vjp = jax.vjp(_rms, x_ref[...], g_ref[...])
        dx, dg = vjp(dh)
        dx = dx + r_ref[...]
        dx_ref[...] = dx
        dxb_ref[...] = dx.astype(BF)

        @pl.when(pl.program_id(0) == 0)
        def _():
            dg_ref[...] = jnp.zeros_like(dg_ref)

        dg_ref[...] += dg

    row = pl.BlockSpec((tm, N), lambda i: (i, 0))
    vec = pl.BlockSpec((1, N), lambda i: (0, 0))
    a_specs = [pl.BlockSpec((p.shape[0], tm, p.shape[2]), lambda i: (0, i, 0)) if p.ndim == 3 else pl.BlockSpec((tm, p.shape[1]), lambda i: (i, 0))
               for p in a_parts]
    return _call(
        body, (*a_parts, b, x, g.reshape(1, N), dres), grid=(M // tm,),
        in_specs=a_specs + [pl.BlockSpec((None, N, K), lambda i: (0, 0, 0), pipeline_mode=pl.Buffered(1)), row, vec, row],
        out_specs=[row, row, vec], out_shape=[SDS((M, N), F32), SDS((M, N), BF), SDS((1, N), F32)],
        semantics=("arbitrary",), name=name, side=side)


def _loss_bwd(x, g, target, name):
    T, D = x.shape
    tb = _tile(T, ROW_BLOCK, 8)
    nb = T // tb

    def body(x_ref, g_ref, t_ref, loss_ref, dx_ref, dg_ref, acc):
        i = pl.program_id(0)
        y, vjp = jax.vjp(_rms, x_ref[...], g_ref[...])
        err = y - t_ref[...]
        dx, dg = vjp(err * (1.0 / D))
        dx_ref[...] = dx

        @pl.when(i == 0)
        def _():
            dg_ref[...] = jnp.zeros_like(dg_ref)
            acc[...] = jnp.zeros_like(acc)

        dg_ref[...] += dg
        acc[...] += jnp.sum(err * err, axis=0, keepdims=True)

        @pl.when(i == nb - 1)
        def _():
            loss_ref[...] = jnp.full(loss_ref.shape, (0.5 / D) * jnp.sum(acc[...]), F32)

    row = pl.BlockSpec((tb, D), lambda i: (i, 0))
    vec = pl.BlockSpec((1, D), lambda i: (0, 0))
    return pl.pallas_call(
        body, grid=(nb,), in_specs=[row, vec, row], out_specs=[pl.BlockSpec((1, LANES), lambda i: (0, 0)), row, vec],
        out_shape=[SDS((1, LANES), F32), SDS((T, D), F32), SDS((1, D), F32)], scratch_shapes=[pltpu.VMEM((1, D), F32)],
        compiler_params=_params("arbitrary"), name=name)(x, g.reshape(1, D), target)


def _ln_silu_mm(cv, g, b, w, name):
    T, C = cv.shape
    D = w.shape[2]
    tb = _tile(T, 2 * ROW_BLOCK, 8)

    def body(x_ref, g_ref, b_ref, w_ref, y1_ref, y_ref):
        y1 = _ln_silu(x_ref[...], g_ref[...], b_ref[...]).astype(BF)
        y1_ref[...] = y1
        y_ref[...] = lax.dot_general(y1, w_ref[...], _DN["nn"], preferred_element_type=F32).astype(BF)

    row = pl.BlockSpec((tb, C), lambda i: (i, 0))
    vec = pl.BlockSpec((1, C), lambda i: (0, 0))
    return pl.pallas_call(
        body, grid=(T // tb,), in_specs=[row, vec, vec, pl.BlockSpec((None, C, D), lambda i: (0, 0, 0), pipeline_mode=pl.Buffered(1))],
        out_specs=[row, pl.BlockSpec((tb, D), lambda i: (i, 0))], out_shape=[SDS((T, C), BF), SDS((T, D), BF)],
        compiler_params=_params("parallel"), name=name)(cv, g.reshape(1, C), b.reshape(1, C), w)


def _mm_ln_silu_bwd(dyc, w, cv, g, b, name, side=None):
    T, C = cv.shape
    D = w.shape[2]
    tb = _tile(T, 2 * ROW_BLOCK, 8)

    def body(d_ref, w_ref, x_ref, g_ref, b_ref, dx_ref, dg_ref, db_ref):
        dy1 = lax.dot_general(d_ref[...], w_ref[...], _DN["nt"], preferred_element_type=F32)
        _, vjp = jax.vjp(_ln_silu, x_ref[...], g_ref[...], b_ref[...])
        dx, dg, db = vjp(dy1)
        dx_ref[...] = dx

        @pl.when(pl.program_id(0) == 0)
        def _():
            dg_ref[...] = jnp.zeros_like(dg_ref)
            db_ref[...] = jnp.zeros_like(db_ref)

        dg_ref[...] += dg
        db_ref[...] += db

    row = pl.BlockSpec((tb, C), lambda i: (i, 0))
    vec = pl.BlockSpec((1, C), lambda i: (0, 0))
    return _call(
        body, (dyc, w, cv, g.reshape(1, C), b.reshape(1, C)), grid=(T // tb,),
        in_specs=[pl.BlockSpec((tb, D), lambda i: (i, 0)), pl.BlockSpec((None, C, D), lambda i: (0, 0, 0), pipeline_mode=pl.Buffered(1)), row, vec, vec],
        out_specs=[row, vec, vec], out_shape=[SDS((T, C), F32), SDS((1, C), F32), SDS((1, C), F32)], semantics=("arbitrary",),
        name=name, side=side)


def _merge_fwd(proj, yc, yp, ps, C, name, side=None):
    T, D = yc.shape
    tb = _tile(T, ROW_BLOCK, 8)
    nj = D // C

    def body(gc_ref, gp_ref, yc_ref, yp_ref, ps_ref, o_ref):
        o_ref[...] = _merge(gc_ref[...], gp_ref[...], yc_ref[...].astype(F32), yp_ref[...].astype(F32), ps_ref[...]).astype(BF)

    blk = pl.BlockSpec((tb, C), lambda i, j: (i, j))
    return _call1(
        body, (proj, proj, yc, yp, ps.reshape(1, D)), grid=(T // tb, nj),
        in_specs=[pl.BlockSpec((tb, C), lambda i, j: (i, 3 + j)), pl.BlockSpec((tb, C), lambda i, j: (i, 3 + nj + j)), blk, blk,
                  pl.BlockSpec((1, C), lambda i, j: (0, j))],
        out_spec=blk, out_shape=SDS((T, D), BF), semantics=("parallel", "parallel"), name=name, side=side)


def _merge_bwd(proj, yc, yp, ps, dm, C, name, side=None):
    T, D = yc.shape
    tb = _tile(T, ROW_BLOCK, 8)
    nj = D // C

    def body(gc_ref, gp_ref, yc_ref, yp_ref, ps_ref, dm_ref, dg_ref, dyc_ref, dyp_ref, dps_ref):
        _, vjp = jax.vjp(_merge, gc_ref[...], gp_ref[...], yc_ref[...].astype(F32), yp_ref[...].astype(F32), ps_ref[...])
        dgc, dgp, dyc, dyp, dps = vjp(dm_ref[...].astype(F32))
        dg_ref[0] = dgc.astype(BF)
        dg_ref[1] = dgp.astype(BF)
        dyc_ref[...] = dyc.astype(BF)
        dyp_ref[...] = dyp.astype(BF)

        @pl.when(pl.program_id(1) == 0)
        def _():
            dps_ref[...] = jnp.zeros_like(dps_ref)

        dps_ref[...] += dps

    blk = pl.BlockSpec((tb, C), lambda j, i: (i, j))
    vec = pl.BlockSpec((1, C), lambda j, i: (0, j))
    return _call(
        body, (proj, proj, yc, yp, ps.reshape(1, D), dm), grid=(nj, T // tb),
        in_specs=[pl.BlockSpec((tb, C), lambda j, i: (i, 3 + j)), pl.BlockSpec((tb, C), lambda j, i: (i, 3 + nj + j)), blk, blk, vec, blk],
        out_specs=[pl.BlockSpec((2, tb, C), lambda j, i: (0, i, j)), blk, blk, vec],
        out_shape=[SDS((2, T, D), BF), SDS((T, D), BF), SDS((T, D), BF), SDS((1, D), F32)],
        semantics=("parallel", "arbitrary"), name=name, side=side)


def _shd(v, s, rows):
    if s == 0:
        return v
    return jnp.where(rows >= s, pltpu.roll(v, s, 0), 0.0)


def _shu(v, s, rows):
    if s == 0:
        return v
    n = v.shape[0]
    return jnp.where(rows < n - s, pltpu.roll(v, n - s, 0), 0.0)


def _glu_conv_fwd(proj, w, b, Bn, S, C, name, side=None):
    K = w.shape[0]
    sl = min(LANES, C)
    ns = C // sl

    def body(a_ref, gl_ref, w_ref, b_ref, o_ref):
        y0 = a_ref[...] * jax.nn.sigmoid(gl_ref[...])
        rows = lax.broadcasted_iota(jnp.int32, y0.shape, 0)
        acc = jnp.zeros_like(y0) + b_ref[...]
        for k in range(K):
            acc = acc + w_ref[k:k + 1, :] * _shd(y0, K - 1 - k, rows)
        o_ref[...] = acc

    return _call1(
        body, (proj, proj, w, b.reshape(1, C)), grid=(Bn, ns),
        in_specs=[pl.BlockSpec((S, sl), lambda bi, j: (bi, j)), pl.BlockSpec((S, sl), lambda bi, j: (bi, ns + j)),
                  pl.BlockSpec((K, sl), lambda bi, j: (0, j)), pl.BlockSpec((1, sl), lambda bi, j: (0, j))],
        out_spec=pl.BlockSpec((S, sl), lambda bi, j: (bi, j)), out_shape=SDS((Bn * S, C), F32),
        semantics=("parallel", "parallel"), name=name, side=side)


def _glu_conv_bwd(proj, w, dcv, Bn, S, C, name, side=None):
    K = w.shape[0]
    sl = min(LANES, C)
    ns = C // sl

    def body(a_ref, gl_ref, w_ref, d_ref, dagl_ref, dw_ref, db_ref):
        a = a_ref[...]
        sg = jax.nn.sigmoid(gl_ref[...])
        y0 = a * sg
        d = d_ref[...]
        rows = lax.broadcasted_iota(jnp.int32, y0.shape, 0)

        @pl.when(pl.program_id(1) == 0)
        def _():
            dw_ref[...] = jnp.zeros_like(dw_ref)
            db_ref[...] = jnp.zeros_like(db_ref)

        dy0 = jnp.zeros_like(y0)
        for k in range(K):
            s = K - 1 - k
            dw_ref[k:k + 1, :] += jnp.sum(d * _shd(y0, s, rows), axis=0, keepdims=True)
            dy0 = dy0 + w_ref[k:k + 1, :] * _shu(d, s, rows)
        db_ref[...] += jnp.sum(d, axis=0, keepdims=True)
        dagl_ref[0] = (dy0 * sg).astype(BF)
        dagl_ref[1] = (dy0 * a * sg * (1.0 - sg)).astype(BF)

    blk = pl.BlockSpec((S, sl), lambda j, bi: (bi, j))
    return _call(
        body, (proj, proj, w, dcv), grid=(ns, Bn),
        in_specs=[blk, pl.BlockSpec((S, sl), lambda j, bi: (bi, ns + j)), pl.BlockSpec((K, sl), lambda j, bi: (0, j)), blk],
        out_specs=[pl.BlockSpec((2, S, sl), lambda j, bi: (0, bi, j)), pl.BlockSpec((K, sl), lambda j, bi: (0, j)),
                   pl.BlockSpec((1, sl), lambda j, bi: (0, j))],
        out_shape=[SDS((2, Bn * S, C), BF), SDS((K, C), F32), SDS((1, C), F32)],
        semantics=("parallel", "arbitrary"), name=name, side=side)


def _pool_z(u, g, rows):
    s2 = u + _shd(u, 1, rows)
    s4 = s2 + _shd(s2, 2, rows)
    s8 = s4 + _shd(s4, 4, rows)
    s16 = s8 + _shd(s8, 8, rows)
    sw = jnp.where(g == 0, s2, jnp.where(g == 1, s4, jnp.where(g == 2, s8, s16)))
    cnt = jnp.minimum(rows + 1, POOL_WINDOWS[0] << g).astype(F32)
    return sw / cnt - u, cnt


def _pool_fwd(proj, wpt, l, Bn, S, C, D, name):
    G = len(POOL_WINDOWS)
    gd, go = C // G, D // G

    def body(u_ref, w_ref, o_ref):
        g = pl.program_id(1)
        u = u_ref[...]
        rows = lax.broadcasted_iota(jnp.int32, u.shape, 0)
        zp, _ = _pool_z(u, g, rows)
        o_ref[...] = lax.dot_general(zp.astype(BF), w_ref[...], _DN["nt"], preferred_element_type=F32).astype(BF)

    return pl.pallas_call(
        body, grid=(Bn, G),
        in_specs=[pl.BlockSpec((S, gd), lambda bi, g: (bi, 2 * G + g)), pl.BlockSpec((None, go, gd), lambda bi, g: (l * G + g, 0, 0))],
        out_specs=pl.BlockSpec((S, go), lambda bi, g: (bi, g)), out_shape=SDS((Bn * S, D), BF),
        compiler_params=_params("parallel", "parallel"), name=name)(proj, wpt)


POOL_HALO = 16
POOL_ROWS = 128


def _pool_bwd(proj, wpt, dyp, l, Bn, S, C, D, name):
    G = len(POOL_WINDOWS)
    gd, go = C // G, D // G
    rc = POOL_ROWS if S % POOL_ROWS == 0 else S
    win = rc + 2 * POOL_HALO
    assert POOL_WINDOWS[-1] - 1 <= POOL_HALO

    def body(u_ref, w_ref, d_ref, du_ref, dw_ref, up, dp):
        g = pl.program_id(0)
        for pad, src in ((up, u_ref), (dp, d_ref)):
            pad[0:POOL_HALO, :] = jnp.zeros((POOL_HALO, pad.shape[1]), pad.dtype)
            pad[POOL_HALO + S:, :] = jnp.zeros((POOL_HALO, pad.shape[1]), pad.dtype)
            pad[POOL_HALO:POOL_HALO + S, :] = src[...]

        @pl.when(pl.program_id(1) == 0)
        def _():
            dw_ref[...] = jnp.zeros_like(dw_ref)

        w = w_ref[...]
        inner = slice(POOL_HALO, POOL_HALO + rc)

        def up_by(t, s):
            return pltpu.roll(t, win - s, 0)

        def chunk(c, carry):
            r0 = pl.multiple_of(c * rc, SUBLANES)
            u, d = up[pl.ds(r0, win), :], dp[pl.ds(r0, win), :]
            t = r0 - POOL_HALO + lax.broadcasted_iota(jnp.int32, u.shape, 0)
            cnt = jnp.maximum(jnp.minimum(t + 1, POOL_WINDOWS[0] << g), 1).astype(F32)
            s2 = u + pltpu.roll(u, 1, 0)
            s4 = s2 + pltpu.roll(s2, 2, 0)
            s8 = s4 + pltpu.roll(s4, 4, 0)
            s16 = s8 + pltpu.roll(s8, 8, 0)
            zp = jnp.where(g == 0, s2, jnp.where(g == 1, s4, jnp.where(g == 2, s8, s16))) / cnt - u
            dzp = lax.dot_general(d, w, _DN["nn"], preferred_element_type=F32)
            dw_ref[...] += lax.dot_general(d[inner], zp[inner].astype(BF), _DN["tn"], preferred_element_type=F32)
            dsw = dzp / cnt
            zero = jnp.zeros_like(dsw)
            d16 = jnp.where(g == 3, dsw, zero)
            d8 = jnp.where(g == 2, dsw, zero) + d16 + up_by(d16, 8)
            d4 = jnp.where(g == 1, dsw, zero) + d8 + up_by(d8, 4)
            d2 = jnp.where(g == 0, dsw, zero) + d4 + up_by(d4, 2)
            d1 = d2 + up_by(d2, 1)
            du_ref[pl.ds(r0, rc), :] = (d1 - dzp)[inner].astype(BF)
            return carry

        lax.fori_loop(0, S // rc, chunk, 0)

    return pl.pallas_call(
        body, grid=(G, Bn),
        in_specs=[pl.BlockSpec((S, gd), lambda g, bi: (bi, 2 * G + g)), pl.BlockSpec((None, go, gd), lambda g, bi: (l * G + g, 0, 0)),
                  pl.BlockSpec((S, go), lambda g, bi: (bi, g))],
        out_specs=[pl.BlockSpec((S, gd), lambda g, bi: (bi, g)), pl.BlockSpec((None, go, gd), lambda g, bi: (g, 0, 0))],
        out_shape=[SDS((Bn * S, C), BF), SDS((G, go, gd), F32)],
        scratch_shapes=[pltpu.VMEM((S + 2 * POOL_HALO, gd), F32), pltpu.VMEM((S + 2 * POOL_HALO, go), BF)],
        compiler_params=_params("parallel", "arbitrary"), name=name)(proj, wpt, dyp)


def _ffn_conv(u, w_ref, rows):
    K = w_ref.shape[0]
    acc = w_ref[K - 1:K, :] * u
    for k in range(K - 1):
        acc = acc + w_ref[k:k + 1, :] * _shd(u, K - 1 - k, rows)
    return acc


def _ffn_cb(F):
    return _tile(F, 256)


def _ffn_act_fwd(up0, w, Bn, S, F, name, side=None):
    cb = _ffn_cb(F)
    nj = F // cb

    def body(g_ref, v_ref, wg_ref, wv_ref, o_ref):
        rows = lax.broadcasted_iota(jnp.int32, g_ref.shape, 0)
        o_ref[...] = _gated(_ffn_conv(g_ref[...], wg_ref, rows), _ffn_conv(v_ref[...], wv_ref, rows)).astype(BF)

    K = w.shape[0]
    return _call1(
        body, (up0, up0, w, w), grid=(Bn, nj),
        in_specs=[pl.BlockSpec((S, cb), lambda bi, j: (bi, j)), pl.BlockSpec((S, cb), lambda bi, j: (bi, nj + j)),
                  pl.BlockSpec((K, cb), lambda bi, j: (0, j)), pl.BlockSpec((K, cb), lambda bi, j: (0, nj + j))],
        out_spec=pl.BlockSpec((S, cb), lambda bi, j: (bi, j)), out_shape=SDS((Bn * S, F), BF),
        semantics=("parallel", "parallel"), name=name, side=side)


SUBLANES = 8
FFN_HALO = SUBLANES
FFN_ROWS = 128
GELU_C0, GELU_C1 = 0.7978845608028654, 0.044715


def _gelu_and_grad(x):
    x2 = x * x
    t = jnp.tanh(GELU_C0 * (x + GELU_C1 * (x2 * x)))
    cdf = 0.5 * (1.0 + t)
    return x * cdf, cdf + (0.5 * GELU_C0) * x * (1.0 - t * t) * (1.0 + (3.0 * GELU_C1) * x2)


def _ffn_act_bwd(up0, w, dg, Bn, S, F, name, side=None):
    cb = min(LANES, F)
    nj = F // cb
    K = w.shape[0]
    rc = FFN_ROWS if S % FFN_ROWS == 0 else S
    win = rc + 2 * FFN_HALO
    assert K - 1 <= FFN_HALO and rc % SUBLANES == 0

    def body(g_ref, v_ref, wg_ref, wv_ref, d_ref, do_ref, dwg_ref, dwv_ref, gp, vp, dp):
        for pad, src in ((gp, g_ref), (vp, v_ref), (dp, d_ref)):
            pad[0:FFN_HALO, :] = jnp.zeros((FFN_HALO, cb), F32)
            pad[FFN_HALO + S:, :] = jnp.zeros((FFN_HALO, cb), F32)
            pad[FFN_HALO:FFN_HALO + S, :] = src[...].astype(F32)
        wg = [wg_ref[k:k + 1, :] for k in range(K)]
        wv = [wv_ref[k:k + 1, :] for k in range(K)]

        def taps(u):
            return [pltpu.roll(u, K - 1 - k, 0) for k in range(K - 1)] + [u]

        def conv(us, ws):
            acc = ws[K - 1] * us[K - 1]
            for k in range(K - 1):
                acc = acc + ws[k] * us[k]
            return acc

        def conv_t(dc, ws):
            acc = ws[K - 1] * dc
            for k in range(K - 1):
                acc = acc + ws[k] * pltpu.roll(dc, win - (K - 1 - k), 0)
            return acc

        def fold(t):
            acc = t[FFN_HALO:FFN_HALO + SUBLANES]
            for i in range(1, rc // SUBLANES):
                acc = acc + t[FFN_HALO + SUBLANES * i:FFN_HALO + SUBLANES * (i + 1)]
            return acc

        def chunk(c, sums):
            r0 = pl.multiple_of(c * rc, SUBLANES)
            gs, vs, d = taps(gp[pl.ds(r0, win), :]), taps(vp[pl.ds(r0, win), :]), dp[pl.ds(r0, win), :]
            ge, dge = _gelu_and_grad(conv(gs, wg))
            dgc = d * conv(vs, wv) * dge
            dvc = d * ge
            do_ref[0, pl.ds(r0, rc), :] = conv_t(dgc, wg)[FFN_HALO:FFN_HALO + rc].astype(BF)
            do_ref[1, pl.ds(r0, rc), :] = conv_t(dvc, wv)[FFN_HALO:FFN_HALO + rc].astype(BF)
            new = [fold(dc * u) for us, dc in ((gs, dgc), (vs, dvc)) for u in us]
            return tuple(a + b for a, b in zip(sums, new))

        sums = lax.fori_loop(0, S // rc, chunk, tuple(jnp.zeros((SUBLANES, cb), F32) for _ in range(2 * K)))

        @pl.when(pl.program_id(1) == 0)
        def _():
            dwg_ref[...] = jnp.zeros_like(dwg_ref)
            dwv_ref[...] = jnp.zeros_like(dwv_ref)

        for k in range(K):
            dwg_ref[k:k + 1, :] += jnp.sum(sums[k], axis=0, keepdims=True)
            dwv_ref[k:k + 1, :] += jnp.sum(sums[K + k], axis=0, keepdims=True)

    blk = pl.BlockSpec((S, cb), lambda j, bi: (bi, j))
    wblk = pl.BlockSpec((K, cb), lambda j, bi: (0, j))
    return _call(
        body, (up0, up0, w, w, dg), grid=(nj, Bn),
        in_specs=[blk, pl.BlockSpec((S, cb), lambda j, bi: (bi, nj + j)), wblk, pl.BlockSpec((K, cb), lambda j, bi: (0, nj + j)), blk],
        out_specs=[pl.BlockSpec((2, S, cb), lambda j, bi: (0, bi, j)), wblk, wblk],
        out_shape=[SDS((2, Bn * S, F), BF), SDS((K, F), F32), SDS((K, F), F32)],
        scratch_shapes=[pltpu.VMEM((S + 2 * FFN_HALO, cb), F32)] * 3, semantics=("parallel", "arbitrary"), name=name, side=side)


def _softmax_rows(q, k, scale):
    sc = lax.dot_general(q, k, _DN["nt"], preferred_element_type=F32) * scale
    e = jnp.exp(sc - jnp.max(sc, axis=-1, keepdims=True))
    return e / jnp.sum(e, axis=-1, keepdims=True)


def _attn_ts(S):
    return _tile(S, 1024, 8)


def _attn_fwd(q, kv, Bn, S, Mn, D, name, side=None):
    H = XA_HEADS
    dh = D // H
    ts = _attn_ts(S)
    nsb = S // ts
    scale = dh ** -0.5

    def body(q_ref, k_ref, v_ref, o_ref):
        p = _softmax_rows(q_ref[...], k_ref[...], scale)
        o_ref[...] = lax.dot_general(p.astype(BF), v_ref[...], _DN["nn"], preferred_element_type=F32).astype(BF)

    qblk = pl.BlockSpec((ts, dh), lambda bi, h, s: (bi * nsb + s, h))
    return _call1(
        body, (q, kv, kv), grid=(Bn, H, nsb),
        in_specs=[qblk, pl.BlockSpec((Mn, dh), lambda bi, h, s: (bi, h)), pl.BlockSpec((Mn, dh), lambda bi, h, s: (bi, H + h))],
        out_spec=qblk, out_shape=SDS((Bn * S, D), BF), semantics=("parallel", "parallel", "parallel"), name=name, side=side)


def _attn_bwd(q, kv, datt, Bn, S, Mn, D, name):
    H = XA_HEADS
    dh = D // H
    ts = _attn_ts(S)
    nsb = S // ts
    scale = dh ** -0.5

    def body(q_ref, k_ref, v_ref, do_ref, dq_ref, dk_ref, dv_ref):
        q, k, v, do = q_ref[...], k_ref[...], v_ref[...], do_ref[...]
        p = _softmax_rows(q, k, scale)
        dp = lax.dot_general(do, v, _DN["nt"], preferred_element_type=F32)
        ds = (p * (dp - jnp.sum(dp * p, axis=-1, keepdims=True)) * scale).astype(BF)
        dq_ref[...] = lax.dot_general(ds, k, _DN["nn"], preferred_element_type=F32).astype(BF)

        @pl.when(pl.program_id(2) == 0)
        def _():
            dk_ref[...] = jnp.zeros_like(dk_ref)
            dv_ref[...] = jnp.zeros_like(dv_ref)

        dk_ref[...] += lax.dot_general(ds, q, _DN["tn"], preferred_element_type=F32)
        dv_ref[...] += lax.dot_general(p.astype(BF), do, _DN["tn"], preferred_element_type=F32)

    qblk = pl.BlockSpec((ts, dh), lambda bi, h, s: (bi * nsb + s, h))
    kblk = pl.BlockSpec((Mn, dh), lambda bi, h, s: (bi, h))
    return pl.pallas_call(
        body, grid=(Bn, H, nsb),
        in_specs=[qblk, kblk, pl.BlockSpec((Mn, dh), lambda bi, h, s: (bi, H + h)), qblk],
        out_specs=[qblk, kblk, kblk], out_shape=[SDS((Bn * S, D), BF), SDS((Bn * Mn, D), F32), SDS((Bn * Mn, D), F32)],
        compiler_params=_params("parallel", "parallel", "arbitrary"), name=name)(q, kv, kv, datt)


class _Sides:
    def __init__(self, by_key=None, on_land=None):
        self.by_key, self.landed, self.on_land = dict(by_key or {}), {}, on_land

    def run(self, key, fn, *args, **kw):
        side = self.by_key.get(key)
        if side is None:
            return fn(*args, **kw)
        out, self.landed[key] = fn(*args, side=side() if callable(side) else side, **kw)
        if self.on_land is not None:
            self.on_land(key, self.landed[key])
        return out

    def mm(self, key, *args, **kw):
        return self.run(key, _mm, *args, **kw)


def _layer_fwd(x, h, mem_n, W, V, l, dims, sides, next_g):
    Bn, S, Mn, D, C, F = dims
    n = f"l{l}_"
    proj = sides.mm("proj", h, W["w_in"], "nn", F32, n + "proj", bl=0)
    cv = sides.run("glu_conv", _glu_conv_fwd, proj, V["conv_dw_w"][l], V["conv_dw_b"][l], Bn, S, C, n + "glu_conv")
    yc1, yc = _ln_silu_mm(cv, V["conv_ln_g"][l], V["conv_ln_b"][l], W["w_conv_out"], n + "conv_out")
    yp = _pool_fwd(proj, W["w_pool"], 0, Bn, S, C, D, n + "pool")
    merged = sides.run("merge", _merge_fwd, proj, yc, yp, V["pool_scale"][l], C, n + "merge")
    x1, hq = sides.run("out_proj", _mm_rms_fwd, merged, W["w_out"], x, V["xattn_norm_g"][l], n + "out_proj")
    q = sides.mm("q_proj", hq, W["w_q"], "nn", BF, n + "q_proj", bl=0)
    kv = _mm(mem_n, W["w_kv"], "nn", BF, n + "kv_proj", bl=0)
    att = sides.run("attn", _attn_fwd, q, kv, Bn, S, Mn, D, n + "attn")
    x2, hf = sides.run("o_proj", _mm_rms_fwd, att, W["w_o"], x1, V["ffn_norm_g"][l], n + "o_proj")
    up0 = sides.mm("up_proj", hf, W["w_up"], "nn", F32, n + "up_proj", bl=0)
    gact = sides.run("ffn_act", _ffn_act_fwd, up0, V["ffn_dw_w"][l], Bn, S, F, n + "ffn_act")
    if next_g is not None:
        x3, h3 = sides.run("down_proj", _mm_rms_fwd, gact, W["w_down"], x2, next_g, n + "down_proj")
    else:
        x3, h3 = sides.mm("down_proj", gact, W["w_down"], "nn", F32, n + "down_proj", res=x2, bl=0), None
    return x3, h3, dict(x=x, h=h, proj=proj, cv=cv, yc1=yc1, yc=yc, yp=yp, merged=merged, x1=x1, hq=hq, q=q, kv=kv, att=att, x2=x2,
                        hf=hf, up0=up0, gact=gact)


def _layer_bwd_mlp(dx, dxb, sv, W, V, l, dims, sides):
    Bn, S, Mn, D, C, F = dims
    n = f"l{l}_b_"
    gw, sm = {}, {}
    dgact = sides.mm("d_gact", dxb, W["w_down"], "nt", BF, n + "d_gact", bl=0)
    gw["w_down"] = sides.mm("dw_down", sv["gact"], dxb, "tn", F32, n + "dw_down", twin=BF)
    dup0, dwg, dwv = sides.run("ffn_act_b", _ffn_act_bwd, sv["up0"], V["ffn_dw_w"][l], dgact, Bn, S, F, n + "ffn_act")
    sm["ffn_dw_w"] = jnp.concatenate([dwg, dwv], axis=1)
    dx2, dx2b, sm["ffn_norm_g"] = _mm_rms_bwd([dup0], W["w_up"], sv["x2"], V["ffn_norm_g"][l], dx, n + "d_hf")
    gw["w_up"] = sides.mm("dw_up", sv["hf"], dup0, "tn", F32, n + "dw_up", twin=BF, b_halves=True)
    return dx2, dx2b, gw, sm


def _layer_bwd_mix(dx2, dx2b, dmem_n, sv, mem_n, W, V, l, dims, sides, gw):
    Bn, S, Mn, D, C, F = dims
    n = f"l{l}_b_"
    sm = {}
    datt = sides.mm("d_att", dx2b, W["w_o"], "nt", BF, n + "d_att", bl=0)
    gw["w_o"] = _mm(sv["att"], dx2b, "tn", F32, n + "dw_o", twin=BF)
    dq, dk, dv = _attn_bwd(sv["q"], sv["kv"], datt, Bn, S, Mn, D, n + "attn")
    dkv = jnp.concatenate([dk, dv], axis=1)
    gw["w_kv"] = _mm(mem_n, dkv, "tn", F32, n + "dw_kv", twin=BF)
    dmem_n = _mm(dkv, W["w_kv"], "nt", F32, n + "d_mem", res=dmem_n, bl=0)
    dx1, dx1b, sm["xattn_norm_g"] = _mm_rms_bwd([dq], W["w_q"], sv["x1"], V["xattn_norm_g"][l], dx2, n + "d_hq")
    gw["w_q"] = _mm(sv["hq"], dq, "tn", F32, n + "dw_q", twin=BF)
    dmerged = sides.mm("d_merged", dx1b, W["w_out"], "nt", BF, n + "d_merged", bl=0)
    gw["w_out"] = _mm(sv["merged"], dx1b, "tn", F32, n + "dw_out", twin=BF)
    dgates, dyc, dyp, sm["pool_scale"] = sides.run("merge_b", _merge_bwd, sv["proj"], sv["yc"], sv["yp"], V["pool_scale"][l], dmerged, C, n + "merge")
    du, dwp = _pool_bwd(sv["proj"], W["w_pool"], dyp, 0, Bn, S, C, D, n + "pool")
    gw["w_pool"] = (dwp, dwp.astype(BF))
    gw["w_conv_out"] = _mm(sv["yc1"], dyc, "tn", F32, n + "dw_conv_out", twin=BF)
    dcv, sm["conv_ln_g"], sm["conv_ln_b"] = sides.run("ln_silu_b", _mm_ln_silu_bwd, dyc, W["w_conv_out"], sv["cv"], V["conv_ln_g"][l],
                                                      V["conv_ln_b"][l], n + "d_yc1")
    dagl, sm["conv_dw_w"], sm["conv_dw_b"] = sides.run("glu_conv_b", _glu_conv_bwd, sv["proj"], V["conv_dw_w"][l], dcv, Bn, S, C, n + "glu_conv")
    dx, dxb, sm["mix_norm_g"] = sides.run("d_h", _mm_rms_bwd, [dagl, du, dgates], W["w_in"], sv["x"], V["mix_norm_g"][l], dx1, n + "d_h")
    n_in = W["w_in"].shape[2]
    part = _mm(sv["h"], dagl, "tn", F32, n + "dw_in_conv", twin=BF, b_halves=True, part=(n_in, 0, None))
    part = _mm(sv["h"], du, "tn", F32, n + "dw_in_pool", twin=BF, part=(n_in, 2 * C, part))
    gw["w_in"] = sides.mm("dw_in", sv["h"], dgates, "tn", F32, n + "dw_in", twin=BF, b_halves=True, part=(n_in, 3 * C, part))
    return dx, dxb, dmem_n, sm


BIG = (("w_in", "col"), ("w_conv_out", "col"), ("w_pool", "row"), ("w_out", "row"), ("w_q", "row"), ("w_kv", "col"),
       ("w_o", "row"), ("w_up", "col"), ("w_down", "row"))
ALL_RELS = (1, 2, 3)
GATHER_FIRST = ("w_in", "w_conv_out", "w_pool", "w_out", "w_q", "w_o")
FWD_CARRY = {
    (0, "proj"): (("w_up", 0, (1, 2)),),
    (0, "glu_conv"): (("w_kv", 0, ALL_RELS),),
    (0, "merge"): (("w_up", 0, (3,)),),
    (0, "q_proj"): (("w_down", 0, (1, 2)),),
    (0, "attn"): (("w_down", 0, (3,)),),
    (0, "up_proj"): (("w_in", 1, ALL_RELS), ("w_conv_out", 1, ALL_RELS), ("w_pool", 1, ALL_RELS), ("w_o", 1, ALL_RELS)),
    (0, "ffn_act"): (("w_out", 1, ALL_RELS), ("w_q", 1, ALL_RELS), ("w_kv", 1, ALL_RELS)),
    (1, "proj"): (("w_up", 1, (1, 2)),),
    (1, "glu_conv"): (("w_down", 1, (1, 2)),),
    (1, "merge"): (("w_up", 1, (3,)),),
    (1, "attn"): (("w_down", 1, (3,)),),
}
PASS_CARRY = {
    (0, "out_proj"): (("w_kv", 0),),
    (0, "o_proj"): (("w_up", 0), ("w_down", 0)),
    (0, "down_proj"): (("w_in", 1), ("w_conv_out", 1), ("w_pool", 1), ("w_out", 1), ("w_q", 1), ("w_kv", 1), ("w_o", 1)),
    (1, "o_proj"): (("w_up", 1), ("w_down", 1)),
}
EARLY = ("w_down", "w_up")
BWD_CARRY_EARLY = {"merge_b": ("w_down",), "glu_conv_b": ("w_up",)}
BWD_CARRY_LATE = {"ffn_act_b": ("w_in", "w_conv_out", "w_pool", "w_out", "w_q", "w_kv", "w_o")}
BWD_LAST_LAYER = (("att", ("w_o", "w_kv", "w_q"), "d_merged", {"d_h": ("w_o", "w_kv", "w_q")}),
                  ("tok", ("w_out", "w_pool", "w_conv_out"), "ln_silu_b", {"dw_in": ("w_out", "w_pool", "w_conv_out")}))


def _place():
    xi, yi, ci = lax.axis_index("x"), lax.axis_index("y"), lax.axis_index("c")
    return xi, yi, ci, 2 * xi + yi


def _chip_peer(xi, yi, ci, r):
    return (xi ^ (r >> 1), yi ^ (r & 1), ci)


def _full_shard(ref, kind, k, cs):
    if kind == "col":
        return ref.at[:, :, :, :, pl.ds(pl.multiple_of(k * cs, cs), cs)]
    return ref.at[:, :, k]


def _gather_weights(shards, kinds):
    n = len(shards)
    outs = []
    for s, kind in zip(shards, kinds):
        L, P, _, RH, CS = s.shape
        outs.append(SDS((L, P, 2, RH, CS * N_CHIPS) if kind == "col" else (L, P, N_CHIPS, 2, RH, CS), s.dtype))
    per = 7

    def body(*refs):
        srcs, fulls, (ssem, rsem) = refs[:n], refs[n:2 * n], refs[2 * n:]
        xi, yi, ci, j = _place()
        sib = (xi, yi, 1 - ci)

        def piece(i, k, c):
            kind, cs = kinds[i], shards[i].shape[-1]
            if kind == "col":
                return fulls[i].at[:, :, c, :, pl.ds(pl.multiple_of(k * cs, cs), cs)]
            return fulls[i].at[:, :, k, c]

        def copy(i, slot, src, dst, dev):
            return pltpu.make_async_remote_copy(src_ref=src, dst_ref=dst, send_sem=ssem.at[per * i + slot], recv_sem=rsem.at[per * i + slot],
                                                device_id=dev, device_id_type=MESH)

        own, first, passed = [], [], []
        for i in range(n):
            for r in (1, 2, 3):
                first.append(copy(i, r - 1, srcs[i].at[:, :, ci], piece(i, j, ci), _chip_peer(xi, yi, ci, r)))
                first[-1].start()
        for i in range(n):
            own.append(copy(i, 6, srcs[i], _full_shard(fulls[i], kinds[i], j, shards[i].shape[-1]), sib))
            own[-1].start()
        for i in range(n):
            for r in (1, 2, 3):
                got = piece(i, j ^ r, ci)
                copy(i, r - 1, got, got, sib).wait_recv()
                passed.append(copy(i, 2 + r, got, got, sib))
                passed[-1].start()
        for i in range(n):
            for r in (1, 2, 3):
                got = piece(i, j ^ r, 1 - ci)
                copy(i, 2 + r, got, got, sib).wait_recv()
        for cp in own:
            cp.wait()
        for cp in first + passed:
            cp.wait_send()

    return pl.pallas_call(
        body, in_specs=[ANY] * n, out_specs=[ANY] * n, out_shape=outs,
        scratch_shapes=[pltpu.SemaphoreType.DMA((per * n,)), pltpu.SemaphoreType.DMA((per * n,))], name="gather_weights")(*shards)


def _full_sds(s, kind):
    L, P, _, RH, CS = s.shape
    return SDS((L, P, 2, RH, CS * N_CHIPS) if kind == "col" else (L, P, N_CHIPS, 2, RH, CS), s.dtype)


def _gather_piece(full, kind, cs, k, c):
    if kind == "col":
        return full.at[:, :, c, :, pl.ds(pl.multiple_of(k * cs, cs), cs)]
    return full.at[:, :, k, c]


def _side_gather(shards, kinds, rels, fulls):
    n = len(shards)

    def make(srcs, outs, ssem, rsem):
        xi, yi, ci, j = _place()
        return [pltpu.make_async_remote_copy(
            src_ref=srcs[i].at[:, :, ci], dst_ref=_gather_piece(outs[i], kinds[i], shards[i].shape[-1], j, ci), send_sem=ssem.at[3 * i + r - 1],
            recv_sem=rsem.at[3 * i + r - 1], device_id=_chip_peer(xi, yi, ci, r), device_id_type=MESH) for i in range(n) for r in rels[i]]

    prior = [f for f in fulls if f is not None]
    assert len(prior) in (0, n)
    return _Side(list(shards) + prior, [_full_sds(s, k) for s, k in zip(shards, kinds)], 3 * n, make, n_alias=len(prior))


def _side_gather_pass(fulls, shards, kinds):
    n = len(fulls)

    def make(srcs, outs, ssem, rsem):
        xi, yi, ci, j = _place()
        sib = (xi, yi, 1 - ci)
        cps = []
        for i in range(n):
            cs = shards[i].shape[-1]
            for r in (1, 2, 3):
                got = _gather_piece(outs[i], kinds[i], cs, j ^ r, ci)
                cps.append(pltpu.make_async_remote_copy(src_ref=got, dst_ref=got, send_sem=ssem.at[4 * i + r - 1], recv_sem=rsem.at[4 * i + r - 1],
                                                        device_id=sib, device_id_type=MESH))
            cps.append(pltpu.make_async_remote_copy(src_ref=srcs[i], dst_ref=_full_shard(outs[i], kinds[i], j, cs), send_sem=ssem.at[4 * i + 3],
                                                    recv_sem=rsem.at[4 * i + 3], device_id=sib, device_id_type=MESH))
        return cps

    return _Side(list(shards) + list(fulls), [SDS(f.shape, f.dtype) for f in fulls], 4 * n, make, n_alias=n)


def _sibling_exchange(gviews, kinds, name):
    n = len(gviews)
    outs = [SDS(g.shape[:1] + g.shape[2:] if kind == "col" else g.shape[:2] + g.shape[3:], g.dtype) for g, kind in zip(gviews, kinds)]

    def body(*refs):
        gs, lands, (ssem, rsem) = refs[:n], refs[n:2 * n], refs[2 * n:]
        xi, yi, ci, _ = _place()
        cps = []
        for i in range(n):
            src = gs[i].at[:, 1 - ci] if kinds[i] == "col" else gs[i].at[:, :, 1 - ci]
            cps.append(pltpu.make_async_remote_copy(src_ref=src, dst_ref=lands[i], send_sem=ssem.at[i], recv_sem=rsem.at[i],
                                                    device_id=(xi, yi, 1 - ci), device_id_type=MESH))
            cps[-1].start()
        for cp in cps:
            cp.wait()

    return pl.pallas_call(body, in_specs=[ANY] * n, out_specs=[ANY] * n, out_shape=outs,
                          scratch_shapes=[pltpu.SemaphoreType.DMA((n,)), pltpu.SemaphoreType.DMA((n,))], name=name)(*gviews)


def _side_sibling_exchange(gviews, kinds):
    outs = [SDS(g.shape[:1] + g.shape[2:] if kind == "col" else g.shape[:2] + g.shape[3:], g.dtype) for g, kind in zip(gviews, kinds)]

    def make(gs, lands, ssem, rsem):
        xi, yi, ci, _ = _place()
        return [pltpu.make_async_remote_copy(src_ref=gs[i].at[:, 1 - ci] if kinds[i] == "col" else gs[i].at[:, :, 1 - ci], dst_ref=lands[i],
                                             send_sem=ssem.at[i], recv_sem=rsem.at[i], device_id=(xi, yi, 1 - ci), device_id_type=MESH)
                for i in range(len(gs))]

    return _Side(gviews, outs, len(gviews), make)


def _chip_sums(gs, lands, kinds, jc, name):
    n = len(gs)
    args, in_specs, out_specs, out_shape = [], [], [], []
    for g, land, kind in zip(gs, lands, kinds):
        if kind == "col":
            P, _, RH, C = g.shape
            CS = C // N_CHIPS
            in_specs += [pl.BlockSpec((P, None, RH, CS), lambda r, jc: (0, jc[1], 0, jc[0] ^ r)),
                         pl.BlockSpec((P, RH, CS), lambda r, jc: (0, 0, jc[0] ^ r))]
        else:
            P, _, _, RH, CS = g.shape
            in_specs += [pl.BlockSpec((P, None, None, RH, CS), lambda r, jc: (0, jc[0] ^ r, jc[1], 0, 0)),
                         pl.BlockSpec((P, None, RH, CS), lambda r, jc: (0, jc[0] ^ r, 0, 0))]
        args += [g, land]
        out_specs += [pl.BlockSpec((P, RH, CS), lambda r, jc: (0, 0, 0)), pl.BlockSpec((None, P, RH, CS), lambda r, jc: (r, 0, 0, 0))]
        out_shape += [SDS((P, RH, CS), F32), SDS((N_CHIPS, P, RH, CS), BF)]

    def body(jc_ref, *refs):
        ins, outs = refs[:2 * n], refs[2 * n:]
        for i in range(n):
            s = ins[2 * i][...] + ins[2 * i + 1][...].astype(F32)
            outs[2 * i + 1][...] = s.astype(BF)

            @pl.when(pl.program_id(0) == 0)
            def _():
                outs[2 * i][...] = s

    outs = _call(body, args, grid=(N_CHIPS,), in_specs=in_specs, out_specs=out_specs, out_shape=out_shape, semantics=("arbitrary",),
                 name=name, prefetch=(jc,))
    return outs[0::2], outs[1::2]


def _chip_exchange_copies(srcs, lands, ssem, rsem):
    xi, yi, ci, _ = _place()
    return [pltpu.make_async_remote_copy(src_ref=srcs[i].at[r], dst_ref=lands[i].at[r], send_sem=ssem.at[3 * i + r - 1],
                                         recv_sem=rsem.at[3 * i + r - 1], device_id=_chip_peer(xi, yi, ci, r), device_id_type=MESH)
            for i in range(len(srcs)) for r in (1, 2, 3)]


def _side_chip_exchange(pieces):
    return _Side(pieces, [SDS(p.shape, p.dtype) for p in pieces], 3 * len(pieces), _chip_exchange_copies)


FINAL_SUM_STEPS = 2


def _final_sums(owns, lands, jc, shards, l, L, name, side=None):
    n = len(owns)
    args, in_specs, out_specs, out_shape = [], [], [], []
    for own, land in zip(owns, lands):
        P, RH, CS = own.shape
        hr = RH // FINAL_SUM_STEPS
        in_specs += [pl.BlockSpec((P, hr, CS), lambda h, jc: (0, h, 0))]
        in_specs += [pl.BlockSpec((None, P, hr, CS), functools.partial(lambda r, h, jc: (r, 0, h, 0), r)) for r in (1, 2, 3)]
        args += [own, land, land, land]
        out_specs.append(pl.BlockSpec((None, P, None, hr, CS), lambda h, jc: (l, 0, jc[1], h, 0)))
        out_shape.append(SDS((L, P, 2, RH, CS), F32))
    aliases = None
    if shards is not None:
        aliases = {4 * n + i: i for i in range(n)}
        in_specs += [ANY] * n
        args += list(shards)

    def body(jc_ref, *refs):
        outs = refs[len(args):]
        for i in range(n):
            o, a, b, c = (refs[4 * i + t][...] for t in range(4))
            outs[i][...] = ((o + a.astype(F32)) + b.astype(F32)) + c.astype(F32)

    return _call(body, args, grid=(FINAL_SUM_STEPS,), in_specs=in_specs, out_specs=out_specs, out_shape=out_shape, semantics=("arbitrary",),
                 name=name, prefetch=(jc,), aliases=aliases, side=side)


def _halves_exchange(shards, l, name):
    n = len(shards)

    def body(*refs):
        outs, (ssem, rsem) = refs[n:2 * n], refs[2 * n:]
        xi, yi, ci, _ = _place()
        cps = []
        for i in range(n):
            mine = outs[i].at[l, :, ci]
            cps.append(pltpu.make_async_remote_copy(src_ref=mine, dst_ref=mine, send_sem=ssem.at[i], recv_sem=rsem.at[i],
                                                    device_id=(xi, yi, 1 - ci), device_id_type=MESH))
            cps[-1].start()
        for i in range(n):
            land = outs[i].at[l, :, 1 - ci]
            pltpu.make_async_remote_copy(src_ref=land, dst_ref=land, send_sem=ssem.at[i], recv_sem=rsem.at[i],
                                         device_id=(xi, yi, 1 - ci), device_id_type=MESH).wait_recv()
        for cp in cps:
            cp.wait_send()

    return pl.pallas_call(body, in_specs=[ANY] * n, out_specs=[ANY] * n, out_shape=[SDS(s.shape, s.dtype) for s in shards],
                          input_output_aliases={i: i for i in range(n)},
                          scratch_shapes=[pltpu.SemaphoreType.DMA((n,)), pltpu.SemaphoreType.DMA((n,))], name=name)(*shards)


def _reduce_small(part, pieces):
    NR, Wd = part.shape
    ND = 2 * N_CHIPS
    n = len(pieces)

    def body(p_ref, *refs):
        srcs, o_ref, lands, (land, ssem, rsem, xs, xr) = refs[:n], refs[n], refs[n + 1:2 * n + 1], refs[2 * n + 1:]
        exchange = _chip_exchange_copies(srcs, lands, xs, xr)
        for cp in exchange:
            cp.start()
        xi, yi, ci, j = _place()
        me = 2 * j + ci
        land[me] = p_ref[...]
        cps = []
        for rr in range(1, ND):
            dev = (xi ^ (rr >> 2), yi ^ ((rr >> 1) & 1), ci ^ (rr & 1))
            cps.append(pltpu.make_async_remote_copy(src_ref=p_ref, dst_ref=land.at[me], send_sem=ssem.at[rr - 1], recv_sem=rsem.at[rr - 1],
                                                    device_id=dev, device_id_type=MESH))
            cps[-1].start()
        for rr in range(1, ND):
            got = land.at[me ^ rr]
            pltpu.make_async_remote_copy(src_ref=got, dst_ref=got, send_sem=ssem.at[rr - 1], recv_sem=rsem.at[rr - 1],
                                         device_id=(xi, yi, ci), device_id_type=MESH).wait_recv()
        acc = land[0]
        for d in range(1, ND):
            acc = acc + land[d]
        o_ref[...] = acc
        for cp in cps:
            cp.wait_send()
        for cp in exchange:
            cp.wait()

    vm = pl.BlockSpec(memory_space=pltpu.VMEM)
    outs = pl.pallas_call(
        body, in_specs=[vm] + [ANY] * n, out_specs=[vm] + [ANY] * n, out_shape=[SDS((NR, Wd), F32)] + [SDS(p.shape, p.dtype) for p in pieces],
        scratch_shapes=[pltpu.VMEM((ND, NR, Wd), F32), pltpu.SemaphoreType.DMA((ND - 1,)), pltpu.SemaphoreType.DMA((ND - 1,)),
                        pltpu.SemaphoreType.DMA((3 * n,)), pltpu.SemaphoreType.DMA((3 * n,))],
        name="small_grad_allreduce")(part, *pieces)
    return outs[0], list(outs[1:])


def _adamw_update(w_ref, g_ref, m_ref, v_ref, d_ref, mo_ref, vo_ref):
    g = g_ref[...]
    m = ADAM_B1 * m_ref[...] + (1.0 - ADAM_B1) * g
    v = ADAM_B2 * v_ref[...] + (1.0 - ADAM_B2) * jnp.square(g)
    m_hat = m / (1.0 - ADAM_B1 ** ADAM_STEP)
    v_hat = v / (1.0 - ADAM_B2 ** ADAM_STEP)
    d_ref[...] = -ADAM_LR * (m_hat / (jnp.sqrt(v_hat) + ADAM_EPS) + ADAM_WD * w_ref[...])
    mo_ref[...] = m
    vo_ref[...] = v


ADAMW_STEPS = 8


def _adamw_layer(ws, gs, ms, vs, prev, l, name, side=None):
    n = len(ws)
    args, in_specs, out_specs, out_shape = [], [], [], []
    for w, g, m, v in zip(ws, gs, ms, vs):
        L, R, C = w.shape
        blk = pl.BlockSpec((None, R // ADAMW_STEPS, C), lambda i: (l, i, 0))
        in_specs += [blk] * 4
        args += [w, g, m, v]
        out_specs += [blk] * 3
        out_shape += [SDS((L, R, C), F32)] * 3
    aliases = None
    if prev is not None:
        aliases = {4 * n + i: i for i in range(3 * n)}
        in_specs += [ANY] * (3 * n)
        args += list(prev)

    def body(*refs):
        outs = refs[len(args):]
        for i in range(n):
            _adamw_update(*refs[4 * i:4 * i + 4], *outs[3 * i:3 * i + 3])

    return _call(body, args, grid=(ADAMW_STEPS,), in_specs=in_specs, out_specs=out_specs, out_shape=out_shape, semantics=("parallel",),
                 name=name, aliases=aliases, side=side)


def _adamw(w, g, m, v, name):
    shape = w.shape
    C = shape[-1]
    R = w.size // C
    tb = _tile(R, max(8, (1 << 18) // C), 8)
    body = functools.partial(_adamw_update)
    blk = pl.BlockSpec((tb, C), lambda i: (i, 0))
    outs = pl.pallas_call(body, grid=(R // tb,), in_specs=[blk] * 4, out_specs=[blk] * 3, out_shape=[SDS((R, C), F32)] * 3,
                          compiler_params=_params("parallel"), name=name)(*[t.reshape(R, C) for t in (w, g, m, v)])
    return [t.reshape(shape) for t in outs]


WEIGHTS = ("mix_norm_g", "w_in", "conv_dw_w", "conv_dw_b", "conv_ln_g", "conv_ln_b", "w_conv_out", "w_pool_grp", "pool_scale", "w_out",
           "xattn_norm_g", "mem_norm_g", "w_q", "w_kv", "w_o", "ffn_norm_g", "w_up", "ffn_dw_w", "w_down", "final_norm_g")
VECTORS = ("mix_norm_g", "conv_dw_b", "conv_ln_g", "conv_ln_b", "pool_scale", "xattn_norm_g", "mem_norm_g", "ffn_norm_g", "final_norm_g")


def _shard_view(t, kind):
    L, P, R, C = t.shape
    return t.reshape(L, P, 2, R // 2, C)


def _rows(t, width):
    return t.reshape(-1, width)


def _pack(parts):
    return jnp.concatenate([jnp.pad(p, ((0, (-p.shape[0]) % 8), (0, 0))) for p in parts], axis=0)


def kernel(x, mem, mix_norm_g, w_in, conv_dw_w, conv_dw_b, conv_ln_g, conv_ln_b, w_conv_out, w_pool_grp, pool_scale, w_out, xattn_norm_g, mem_norm_g, w_q, w_kv, w_o, ffn_norm_g, w_up, ffn_dw_w, w_down, final_norm_g, loss_target, m_mix_norm_g, m_w_in, m_conv_dw_w, m_conv_dw_b, m_conv_ln_g, m_conv_ln_b, m_w_conv_out, m_w_pool_grp, m_pool_scale, m_w_out, m_xattn_norm_g, m_mem_norm_g, m_w_q, m_w_kv, m_w_o, m_ffn_norm_g, m_w_up, m_ffn_dw_w, m_w_down, m_final_norm_g, v_mix_norm_g, v_w_in, v_conv_dw_w, v_conv_dw_b, v_conv_ln_g, v_conv_ln_b, v_w_conv_out, v_w_pool_grp, v_pool_scale, v_w_out, v_xattn_norm_g, v_mem_norm_g, v_w_q, v_w_kv, v_w_o, v_ffn_norm_g, v_w_up, v_ffn_dw_w, v_w_down, v_final_norm_g):
    w = dict(mix_norm_g=mix_norm_g, w_in=w_in, conv_dw_w=conv_dw_w, conv_dw_b=conv_dw_b, conv_ln_g=conv_ln_g, conv_ln_b=conv_ln_b,
             w_conv_out=w_conv_out, w_pool_grp=w_pool_grp, pool_scale=pool_scale, w_out=w_out, xattn_norm_g=xattn_norm_g,
             mem_norm_g=mem_norm_g, w_q=w_q, w_kv=w_kv, w_o=w_o, ffn_norm_g=ffn_norm_g, w_up=w_up, ffn_dw_w=ffn_dw_w, w_down=w_down,
             final_norm_g=final_norm_g)
    m = dict(zip(WEIGHTS, (m_mix_norm_g, m_w_in, m_conv_dw_w, m_conv_dw_b, m_conv_ln_g, m_conv_ln_b, m_w_conv_out, m_w_pool_grp, m_pool_scale,
                           m_w_out, m_xattn_norm_g, m_mem_norm_g, m_w_q, m_w_kv, m_w_o, m_ffn_norm_g, m_w_up, m_ffn_dw_w, m_w_down, m_final_norm_g)))
    v = dict(zip(WEIGHTS, (v_mix_norm_g, v_w_in, v_conv_dw_w, v_conv_dw_b, v_conv_ln_g, v_conv_ln_b, v_w_conv_out, v_w_pool_grp, v_pool_scale,
                           v_w_out, v_xattn_norm_g, v_mem_norm_g, v_w_q, v_w_kv, v_w_o, v_ffn_norm_g, v_w_up, v_ffn_dw_w, v_w_down, v_final_norm_g)))
    xi, yi, ci, j = _place()
    jc = jnp.stack([j, ci]).astype(jnp.int32)
    L = w_in.shape[0]
    G = len(POOL_WINDOWS)
    kinds = dict(BIG)

    def to_mat(name, t):
        if name == "w_pool":
            return jnp.swapaxes(t, 2, 3)
        return t[:, None]

    def from_mat(name, t):
        if name == "w_pool":
            return jnp.swapaxes(t, 2, 3)
        return t[:, 0]

    src = {name: w["w_pool_grp" if name == "w_pool" else name] for name, _ in BIG}

    KC, cs_c = conv_dw_w.shape[1], conv_dw_w.shape[2]
    KF, cs_f = ffn_dw_w.shape[1], ffn_dw_w.shape[2]
    taps = jnp.concatenate([conv_dw_w.reshape(L * KC, cs_c), ffn_dw_w.reshape(L * KF * (cs_f // cs_c), cs_c)], axis=0)
    n_taps = taps.shape[0]
    taps = jnp.pad(taps, ((0, (-n_taps) % 16), (0, 0)))
    names = [name for name, _ in BIG]
    mats = {name: to_mat(name, src[name]).astype(BF) for name in names}

    def layer_shards(l, subset):
        return [_shard_view(mats[name][l:l + 1], kinds[name]) for name in subset]

    def as_weight(name, f):
        return f.reshape(G if name == "w_pool" else 1, -1, f.shape[-1])

    assert L == 2
    fulls = _gather_weights(layer_shards(0, GATHER_FIRST) + [_shard_view(taps[None, None], "row")], [kinds[name] for name in GATHER_FIRST] + ["row"])
    ready = {(name, 0): as_weight(name, f) for name, f in zip(GATHER_FIRST, fulls)}
    landing = {}
    taps_all = fulls[-1].reshape(N_CHIPS, -1, cs_c)[:, :n_taps]
    V = {name: w[name] for name in VECTORS}
    V["conv_dw_w"] = taps_all[:, :L * KC].reshape(N_CHIPS, L, KC, cs_c).transpose(1, 2, 0, 3).reshape(L, KC, N_CHIPS * cs_c)
    V["ffn_dw_w"] = taps_all[:, L * KC:].reshape(N_CHIPS, L, KF, cs_f).transpose(1, 2, 0, 3).reshape(L, KF, N_CHIPS * cs_f)

    Bn, S, D = x.shape
    Mn = mem.shape[1]
    dims = (Bn, S, Mn, D, conv_dw_b.shape[1], w_down.shape[1] * N_CHIPS)
    xt = x.reshape(Bn * S, D)
    memf = mem.reshape(Bn * Mn, D)
    mem_n = _rms_fwd(memf, V["mem_norm_g"], "mem_norm")

    class LayerWeights:
        def __init__(self, l):
            self.l = l

        def __getitem__(self, name):
            return ready[(name, self.l)]

    def carried_gather(entries):
        return lambda: _side_gather([layer_shards(lw, [nm])[0] for nm, lw, _ in entries], [kinds[nm] for nm, _, _ in entries],
                                    [rels for _, _, rels in entries], [landing.get((nm, lw)) for nm, lw, _ in entries])

    def carried_pass(group):
        return lambda: _side_gather_pass([landing.pop(t) for t in group], [layer_shards(lw, [nm])[0] for nm, lw in group],
                                         [kinds[nm] for nm, _ in group])

    def on_land(l):
        def handle(key, fulls):
            if (l, key) in FWD_CARRY:
                landing.update({(nm, lw): f for (nm, lw, _), f in zip(FWD_CARRY[(l, key)], fulls)})
            else:
                ready.update({t: as_weight(t[0], f) for t, f in zip(PASS_CARRY[(l, key)], fulls)})
        return handle

    saved, W = [], []
    ht = _rms_fwd(xt, V["mix_norm_g"][0], "l0_mix_norm")
    for l in range(L):
        by_key = {key: carried_gather(entries) for (cl, key), entries in FWD_CARRY.items() if cl == l}
        by_key.update({key: carried_pass(group) for (cl, key), group in PASS_CARRY.items() if cl == l})
        sides = _Sides(by_key, on_land=on_land(l))
        W.append(LayerWeights(l))
        xt, ht, sv = _layer_fwd(xt, ht, mem_n, W[l], V, l, dims, sides, V["mix_norm_g"][l + 1] if l + 1 < L else None)
        saved.append(sv)
    loss, dx, dgf = _loss_bwd(xt, V["final_norm_g"], loss_target.reshape(Bn * S, D), "loss")
    loss = lax.psum(loss[0, 0], ("x", "y", "c"))

    late_names = [name for name in names if name not in EARLY]

    def views(gw, subset, twin):
        out = []
        for name in subset:
            g = gw[name][twin] if gw[name][twin].ndim == 3 else gw[name][twin][None]
            P, R, C = g.shape
            out.append(g.reshape(P, 2, R // 2, C) if kinds[name] == "col" else g.reshape(P, N_CHIPS, 2, R // (2 * N_CHIPS), C))
        return out

    def group_kinds(subset):
        return [kinds[name] for name in subset]

    class Reduction:
        def __init__(self, gw, subset, l, tag, first, table):
            self.gw, self.subset, self.l, self.tag, self.first, self.table = gw, subset, l, tag, first, table

        def sides(self):
            by_key = {self.first: lambda: _side_sibling_exchange(views(self.gw, self.subset, 1), group_kinds(self.subset))}
            by_key.update({key: (lambda names_=names_: _side_chip_exchange([self.pieces[nm] for nm in names_])) for key, names_ in self.table.items()})
            return by_key

        def on_land(self, key, landed):
            if key == self.first:
                self.sums(landed)
            elif key in self.table:
                got[self.l].update(zip(self.table[key], landed))

        def sums(self, lands):
            own, pieces = _chip_sums(views(self.gw, self.subset, 0), lands, group_kinds(self.subset), jc, f"chip_sums_{self.tag}_l{self.l}")
            owns[self.l].update(zip(self.subset, own))
            self.pieces = dict(zip(self.subset, pieces))

    def riding(reductions):
        return _Sides({key: side for r in reductions for key, side in r.sides().items()},
                      on_land=lambda key, landed: [r.on_land(key, landed) for r in reductions])

    dxb, dmem_n = dx, None
    smalls, owns, got = [None] * L, [{} for _ in range(L)], [{} for _ in range(L)]
    late = None
    for l in reversed(range(L)):
        dx, dxb, gw, sm = _layer_bwd_mlp(dx, dxb, saved[l], W[l], V, l, dims, riding([late] if late is not None else []))
        gw_mix = {}
        reductions = [Reduction(gw, EARLY, l, "mlp", "d_att", BWD_CARRY_EARLY)]
        if l == 0:
            reductions += [Reduction(gw_mix, names_, 0, tag, first, table) for tag, names_, first, table in BWD_LAST_LAYER]
        dx, dxb, dmem_n, sm2 = _layer_bwd_mix(dx, dxb, dmem_n, saved[l], mem_n, W[l], V, l, dims, riding(reductions), gw_mix)
        smalls[l] = {**sm, **sm2}
        late = Reduction(gw_mix, late_names, l, "mix", "d_gact", BWD_CARRY_LATE) if l > 0 else None
    last = Reduction(gw_mix, ("w_in",), 0, "in", None, {})
    last.sums(_sibling_exchange(views(gw_mix, last.subset, 1), group_kinds(last.subset), "grad_sibling_exchange_in_l0"))
    grad_x = dx.reshape(Bn, S, D)
    _, _, dgm = _rms_bwd(memf, V["mem_norm_g"], dmem_n, None, "mem_norm_b")
    small = {k: jnp.stack([sm[k] for sm in smalls]) if k in ("conv_dw_w", "ffn_dw_w") else jnp.concatenate([sm[k] for sm in smalls], axis=0)
             for k in smalls[0]}
    small["mem_norm_g"] = dgm
    small["final_norm_g"] = dgf

    small_w = conv_dw_b.shape[1]
    order = VECTORS + ("conv_dw_w", "ffn_dw_w")
    parts = [_rows(small[name], small_w) for name in order]
    counts = [p.shape[0] for p in parts]
    summed, landed_last = _reduce_small(_pack(parts), [last.pieces[name] for name in last.subset])
    got[0].update(zip(last.subset, landed_last))

    keys = ["w_pool_grp" if name == "w_pool" else name for name in names]
    rows3 = lambda t: t.reshape(t.shape[0], -1, t.shape[-1])
    wmv = [[rows3(to_mat(name, d[key])) for name, key in zip(names, keys)] for d in (w, m, v)]
    gshards, updates = None, None
    for l in reversed(range(L)):
        gshards = _final_sums([owns[l][name] for name in names], [got[l][name] for name in names], jc, gshards, l, L, f"final_sums_l{l}")
        gshards = _halves_exchange(gshards, l, f"grad_halves_exchange_l{l}")
        updates = _adamw_layer(wmv[0], [rows3(t) for t in gshards], wmv[1], wmv[2], updates, l, f"adamw_l{l}")
    grads, delta, new_m, new_v = {}, {}, {}, {}
    for i, (name, key) in enumerate(zip(names, keys)):
        Lg, P, _, RH, CS = gshards[i].shape
        grads[key] = from_mat(name, gshards[i].reshape(Lg, P, 2 * RH, CS))
        for d, t in zip((delta, new_m, new_v), updates[3 * i:3 * i + 3]):
            d[key] = from_mat(name, t.reshape(Lg, P, 2 * RH, CS))

    off = 0
    for name, cnt in zip(order, counts):
        t = summed[off:off + cnt]
        off += cnt + (-cnt) % 8
        if name in VECTORS:
            grads[name] = t.reshape(w[name].shape)
        else:
            full = t.reshape(small[name].shape)
            cs = w[name].shape[2]
            grads[name] = lax.dynamic_slice_in_dim(full, j * cs, cs, axis=2)

    vec =[_pack([_rows(d[name], small_w) for name in VECTORS]) for d in (w, grads, m, v)]
    outs = _adamw(*vec, "adamw_vectors")
    off = 0
    for name in VECTORS:
        cnt = w[name].size // small_w
        for d, t in zip((delta, new_m, new_v), outs):
            d[name] = t[off:off + cnt].reshape(w[name].shape)
        off += cnt + (-cnt) % 8
    for name in ("conv_dw_w", "ffn_dw_w"):
        delta[name], new_m[name], new_v[name] = _adamw(w[name], grads[name], m[name], v[name], "adamw_" + name)

    return (loss, grad_x, *[grads[k] for k in WEIGHTS], *[delta[k] for k in WEIGHTS], *[new_m[k] for k in WEIGHTS], *[new_v[k] for k in WEIGHTS])
```

```python
import functools
import math

import jax
import jax.numpy as jnp
from jax import lax
from jax.experimental import pallas as pl
from jax.experimental.pallas import tpu as pltpu

F32 = jnp.float32
BF = jnp.bfloat16
SDS = jax.ShapeDtypeStruct
MESH = pl.DeviceIdType.MESH
ANY = pl.BlockSpec(memory_space=pl.ANY)

EPS = 1e-6
XA_HEADS = 4
POOL_WINDOWS = (2, 4, 8, 16)
N_CHIPS = 4
ADAM_LR, ADAM_B1, ADAM_B2, ADAM_EPS, ADAM_WD, ADAM_STEP = 0.001, 0.9, 0.999, 1e-08, 0.01, 10

LANES = 128
ROW_BLOCK = 512
VMEM_LIMIT = 56 * 1024 * 1024


def _params(*sem):
    return pltpu.CompilerParams(dimension_semantics=sem if sem else None, vmem_limit_bytes=VMEM_LIMIT)


def _tile(n, cap, mult=LANES):
    if n <= cap:
        return n
    for t in range(cap - cap % mult, 0, -mult):
        if n % t == 0:
            return t
    return n


_DN = {"nn": (((1,), (0,)), ((), ())), "nt": (((1,), (1,)), ((), ())), "tn": (((0,), (0,)), ((), ()))}


class _Side:
    def __init__(self, ins, outs, n, make, n_alias=0):
        self.ins, self.outs, self.n, self.make, self.n_alias = list(ins), list(outs), n, make, n_alias


def _call(body, args, *, grid, in_specs, out_specs, out_shape, semantics, name, scratch_shapes=(), side=None, prefetch=(), aliases=None):
    n_pf = len(prefetch)
    aliases = {n_pf + i: o for i, o in (aliases or {}).items()}
    n_in, n_out, n_scr = len(args), len(out_shape), len(scratch_shapes)
    n_si, n_so = (len(side.ins), len(side.outs)) if side is not None else (0, 0)
    if side is not None:
        aliases.update({n_pf + n_in + n_si - side.n_alias + i: n_out + i for i in range(side.n_alias)})

    def carrying(*refs):
        pf, refs = refs[:n_pf], refs[n_pf:]
        ins, s_in = refs[:n_in], refs[n_in:n_in + n_si]
        outs, s_out = refs[n_in + n_si:n_in + n_si + n_out], refs[n_in + n_si + n_out:n_in + n_si + n_out + n_so]
        scr = refs[n_in + n_si + n_out + n_so:]
        if side is None:
            return body(*pf, *ins, *outs, *scr)
        copies = side.make(s_in, s_out, scr[n_scr], scr[n_scr + 1])
        ids = [pl.program_id(d) for d in range(len(grid))]
        first, last = ids[0] == 0, ids[0] == grid[0] - 1
        for d in range(1, len(grid)):
            first, last = first & (ids[d] == 0), last & (ids[d] == grid[d] - 1)

        @pl.when(first)
        def _():
            for cp in copies:
                cp.start()

        body(*pf, *ins, *outs, *scr[:n_scr])

        @pl.when(last)
        def _():
            for cp in copies:
                cp.wait()

    sems = [pltpu.SemaphoreType.DMA((side.n,)), pltpu.SemaphoreType.DMA((side.n,))] if side is not None else []
    outs = pl.pallas_call(
        carrying, grid_spec=pltpu.PrefetchScalarGridSpec(
            num_scalar_prefetch=n_pf, grid=grid, in_specs=list(in_specs) + [ANY] * n_si, out_specs=list(out_specs) + [ANY] * n_so,
            scratch_shapes=list(scratch_shapes) + sems),
        out_shape=list(out_shape) + (side.outs if side is not None else []), input_output_aliases=aliases,
        compiler_params=_params(*(semantics if side is None else ["arbitrary"] * len(grid))), name=name)(
            *prefetch, *args, *(side.ins if side is not None else []))
    return list(outs) if side is None else (list(outs[:n_out]), list(outs[n_out:]))


def _call1(body, args, *, out_spec, out_shape, side=None, **kw):
    got = _call(body, args, out_specs=[out_spec], out_shape=[out_shape], side=side, **kw)
    return got[0] if side is None else (got[0][0], got[1])


MM_VMEM_BUDGET = 40 * 1024 * 1024
MM_STEP_MACS = 2200 * 1024 * 1024
MXU_WIDTH = 256
MM_STEP_COST_BYTES = 1 << 20


def _divisors(n):
    return [t for t in range(LANES, n + 1, LANES) if n % t == 0] or [n]


def _mm_tiles(M, N, K, a_bytes, b_bytes, o_bytes, n_unit=None):
    best = None
    for tk in _divisors(K):
        for tm in _divisors(M):
            for tn in _divisors(N if n_unit is None else n_unit):
                nk = K // tk
                foot = 2 * (tm * tk * a_bytes + tk * tn * b_bytes + tm * tn * o_bytes) + (tm * tn * 4 if nk > 1 else 0)
                if (foot > MM_VMEM_BUDGET or tm * tn * tk > MM_STEP_MACS or tn < min(N if n_unit is None else n_unit, MXU_WIDTH)
                        or tm < min(M, MXU_WIDTH)):
                    continue
                steps = (M // tm) * (N // tn) * nk
                traffic = M * K * a_bytes * (N // tn if nk > 1 else 1) + K * N * b_bytes * (M // tm) + M * N * o_bytes
                exposed = tm * tk * a_bytes + tk * tn * b_bytes + tm * tn * o_bytes
                cost = traffic + exposed + steps * MM_STEP_COST_BYTES + (nk - 1) * M * N * 8
                if best is None or cost < best[0]:
                    best = (cost, tm, tn, tk)
    assert best is not None, (M, N, K)
    return best[1:]


def _mm(a, b, dims, out_dtype, name, res=None, bl=None, side=None, twin=None, b_halves=False, part=None):
    bs = b.shape[1:] if bl is not None or b_halves else b.shape
    if dims == "nn":
        (M, K), (K2, N) = a.shape, bs
    elif dims == "nt":
        (M, K), (N, K2) = a.shape, bs
    else:
        (K, M), (K2, N) = a.shape, bs
    assert K == K2, (name, a.shape, b.shape)
    n_half = N
    if b_halves:
        assert dims == "tn" and bl is None
        N = 2 * n_half
    n_total, n_first, earlier = part if part is not None else (N, 0, None)
    tm, tn, tk = _mm_tiles(M, N, K, a.dtype.itemsize, b.dtype.itemsize, jnp.dtype(out_dtype).itemsize
                           + (res.dtype.itemsize if res is not None else 0) + (jnp.dtype(twin).itemsize if twin is not None else 0),
                           n_unit=math.gcd(n_half, n_first) if b_halves or n_first else None)
    nk = K // tk
    lead = (None,) if bl is not None or b_halves else ()
    pre = (lambda *ix: (bl,) + ix) if bl is not None else (lambda *ix: ix)
    if b_halves:
        per_half = n_half // tn
        pre = lambda k, j: (j // per_half, k, j % per_half)
    if dims == "tn":
        a_spec = pl.BlockSpec((tk, tm), lambda i, j, k: (k, i))
    else:
        a_spec = pl.BlockSpec((tm, tk), lambda i, j, k: (i, k))
    if dims == "nt":
        b_spec = pl.BlockSpec(lead + (tn, tk), lambda i, j, k: pre(j, k))
    else:
        b_spec = pl.BlockSpec(lead + (tk, tn), lambda i, j, k: pre(k, j))
    assert n_first % tn == 0 and (part is None or res is None)
    o_spec = pl.BlockSpec((tm, tn), lambda i, j, k: (i, n_first // tn + j))
    in_specs, args = [a_spec, b_spec], [a, b]
    if res is not None:
        in_specs.append(o_spec)
        args.append(res)
    n_main = len(args)
    n_out = 1 if twin is None else 2
    aliases = None
    if earlier is not None:
        earlier = list(earlier) if twin is not None else [earlier]
        aliases = {n_main + t: t for t in range(n_out)}
        in_specs += [ANY] * n_out
        args += earlier

    def body(*refs):
        refs = refs[:n_main] + refs[len(args):]
        a_ref, b_ref = refs[0], refs[1]
        r_ref = refs[2] if res is not None else None
        o_ref = refs[n_main]
        p = lax.dot_general(a_ref[...].astype(BF), b_ref[...].astype(BF), _DN[dims], preferred_element_type=F32)

        def finish(t):
            if r_ref is not None:
                t = t + r_ref[...]
            o_ref[...] = t.astype(out_dtype)
            if twin is not None:
                refs[n_main + 1][...] = t.astype(twin)

        if nk == 1:
            finish(p)
        else:
            acc = refs[n_main + n_out]
            k = pl.program_id(2)

            @pl.when(k == 0)
            def _():
                acc[...] = p

            @pl.when(k > 0)
            def _():
                acc[...] += p

            @pl.when(k == nk - 1)
            def _():
                finish(acc[...])

    got = _call(body, args, grid=(M // tm, N // tn, nk), in_specs=in_specs, out_specs=[o_spec] * n_out,
                out_shape=[SDS((M, n_total), out_dtype)] + ([SDS((M, n_total), twin)] if twin is not None else []),
                scratch_shapes=[pltpu.VMEM((tm, tn), F32)] if nk > 1 else [], semantics=("parallel", "parallel", "arbitrary"),
                name=name, side=side, aliases=aliases)
    outs, landed = (got, None) if side is None else got
    out = outs[0] if twin is None else (outs[0], outs[1])
    return out if side is None else (out, landed)


def _rms(x, g):
    return x * lax.rsqrt(jnp.mean(x * x, axis=-1, keepdims=True) + EPS) * g


def _ln_silu(x, g, b):
    mu = jnp.mean(x, axis=-1, keepdims=True)
    xc = x - mu
    var = jnp.mean(xc * xc, axis=-1, keepdims=True)
    return jax.nn.silu(xc * lax.rsqrt(var + EPS) * g + b)


def _merge(gc, gp, yc, yp, ps):
    return jax.nn.sigmoid(gc) * yc + jax.nn.sigmoid(gp) * (yp * ps)


def _gated(gate, val):
    return jax.nn.gelu(gate) * val


def _rms_fwd(x, g, name):
    T, D = x.shape
    tb = _tile(T, ROW_BLOCK, 8)

    def body(x_ref, g_ref, o_ref):
        o_ref[...] = _rms(x_ref[...], g_ref[...]).astype(BF)

    row = pl.BlockSpec((tb, D), lambda i: (i, 0))
    return pl.pallas_call(body, grid=(T // tb,), in_specs=[row, pl.BlockSpec((1, D), lambda i: (0, 0))], out_specs=row,
                          out_shape=SDS((T, D), BF), compiler_params=_params("parallel"), name=name)(x, g.reshape(1, D))


def _rms_bwd(x, g, dh, dres, name):
    T, D = x.shape
    tb = _tile(T, ROW_BLOCK, 8)

    def body(*refs):
        if dres is not None:
            x_ref, g_ref, dh_ref, dres_ref, dx_ref, dxb_ref, dg_ref = refs
        else:
            x_ref, g_ref, dh_ref, dx_ref, dxb_ref, dg_ref = refs
        _, vjp = jax.vjp(_rms, x_ref[...], g_ref[...])
        dx, dg = vjp(dh_ref[...].astype(F32))
        if dres is not None:
            dx = dx + dres_ref[...]
        dx_ref[...] = dx
        dxb_ref[...] = dx.astype(BF)

        @pl.when(pl.program_id(0) == 0)
        def _():
            dg_ref[...] = jnp.zeros_like(dg_ref)

        dg_ref[...] += dg

    row = pl.BlockSpec((tb, D), lambda i: (i, 0))
    vec = pl.BlockSpec((1, D), lambda i: (0, 0))
    ins = [x, g.reshape(1, D), dh] + ([dres] if dres is not None else [])
    return pl.pallas_call(
        body, grid=(T // tb,), in_specs=[row, vec, row] + ([row] if dres is not None else []), out_specs=[row, row, vec],
        out_shape=[SDS((T, D), F32), SDS((T, D), BF), SDS((1, D), F32)], compiler_params=_params("arbitrary"), name=name)(*ins)


def _row_tile(M, K, N, per_row_bytes):
    fixed = K * N * 2
    fit = [t for t in _divisors(M) if fixed + 2 * t * per_row_bytes <= MM_VMEM_BUDGET and t * K * N <= 2 * MM_STEP_MACS]
    return max(fit) if fit else min(_divisors(M))


def _mm_rms_fwd(a, b, res, g, name, side=None):
    M, K = a.shape
    N = b.shape[2]
    tm = _row_tile(M, K, N, K * 2 + N * (4 + 4 + 2))

    def body(a_ref, b_ref, r_ref, g_ref, x_ref, h_ref):
        x = r_ref[...] + lax.dot_general(a_ref[...], b_ref[...], _DN["nn"], preferred_element_type=F32)
        x_ref[...] = x
        h_ref[...] = _rms(x, g_ref[...]).astype(BF)

    row = pl.BlockSpec((tm, N), lambda i: (i, 0))
    return _call(body, (a, b, res, g.reshape(1, N)), grid=(M // tm,),
                 in_specs=[pl.BlockSpec((tm, K), lambda i: (i, 0)), pl.BlockSpec((None, K, N), lambda i: (0, 0, 0), pipeline_mode=pl.Buffered(1)), row,
                           pl.BlockSpec((1, N), lambda i: (0, 0))],
                 out_specs=[row, row], out_shape=[SDS((M, N), F32), SDS((M, N), BF)], semantics=("parallel",), name=name, side=side)


def _mm_rms_bwd(a_parts, b, x, g, dres, name, side=None):
    n_a = len(a_parts)
    M = a_parts[0].shape[-2]
    N, K = b.shape[1:]
    assert K == sum(p.shape[-1] * (p.shape[0] if p.ndim == 3 else 1) for p in a_parts)
    tm = _row_tile(M, K, N, K * 2 + N * (4 + 4 + 4 + 2))

    def body(*refs):
        a_refs, (b_ref, x_ref, g_ref, r_ref, dx_ref, dxb_ref, dg_ref) = refs[:n_a], refs[n_a:]
        dh, col = None, 0
        for p, a_ref in zip(a_parts, a_refs):
            for blk in ([a_ref[h] for h in range(p.shape[0])] if p.ndim == 3 else [a_ref[...]]):
                t = lax.dot_general(blk, b_ref[:, col:col + p.shape[-1]], _DN["nt"], preferred_element_type=F32)
                dh = t if dh is None else dh + t
                col += p.shape[-1]
        _, vjp = jax.vjp(_rms, x_ref[...], g_ref[...])
        dx, dg = vjp(dh)
        dx = dx + r_ref[...]
        dx_ref[...] = dx
        dxb_ref[...] = dx.astype(BF)

        @pl.when(pl.program_id(0) == 0)
        def _():
            dg_ref[...] = jnp.zeros_like(dg_ref)

        dg_ref[...] += dg

    row = pl.BlockSpec((tm, N), lambda i: (i, 0))
    vec = pl.BlockSpec((1, N), lambda i: (0, 0))
    a_specs = [pl.BlockSpec((p.shape[0], tm, p.shape[2]), lambda i: (0, i, 0)) if p.ndim == 3 else pl.BlockSpec((tm, p.shape[1]), lambda i: (i, 0))
               for p in a_parts]
    return _call(
        body, (*a_parts, b, x, g.reshape(1, N), dres), grid=(M // tm,),
        in_specs=a_specs + [pl.BlockSpec((None, N, K), lambda i: (0, 0, 0), pipeline_mode=pl.Buffered(1)), row, vec, row],
        out_specs=[row, row, vec], out_shape=[SDS((M, N), F32), SDS((M, N), BF), SDS((1, N), F32)],
        semantics=("arbitrary",), name=name, side=side)


def _loss_bwd(x, g, target, name):
    T, D = x.shape
    tb = _tile(T, ROW_BLOCK, 8)
    nb = T // tb

    def body(x_ref, g_ref, t_ref, loss_ref, dx_ref, dg_ref, acc):
        i = pl.program_id(0)
        y, vjp = jax.vjp(_rms, x_ref[...], g_ref[...])
        err = y - t_ref[...]
        dx, dg = vjp(err * (1.0 / D))
        dx_ref[...] = dx

        @pl.when(i == 0)
        def _():
            dg_ref[...] = jnp.zeros_like(dg_ref)
            acc[...] = jnp.zeros_like(acc)

        dg_ref[...] += dg
        acc[...] += jnp.sum(err * err, axis=0, keepdims=True)

        @pl.when(i == nb - 1)
        def _():
            loss_ref[...] = jnp.full(loss_ref.shape, (0.5 / D) * jnp.sum(acc[...]), F32)

    row = pl.BlockSpec((tb, D), lambda i: (i, 0))
    vec = pl.BlockSpec((1, D), lambda i: (0, 0))
    return pl.pallas_call(
        body, grid=(nb,), in_specs=[row, vec, row], out_specs=[pl.BlockSpec((1, LANES), lambda i: (0, 0)), row, vec],
        out_shape=[SDS((1, LANES), F32), SDS((T, D), F32), SDS((1, D), F32)], scratch_shapes=[pltpu.VMEM((1, D), F32)],
        compiler_params=_params("arbitrary"), name=name)(x, g.reshape(1, D), target)


def _ln_silu_mm(cv, g, b, w, name):
    T, C = cv.shape
    D = w.shape[2]
    tb = _tile(T, 2 * ROW_BLOCK, 8)

    def body(x_ref, g_ref, b_ref, w_ref, y1_ref, y_ref):
        y1 = _ln_silu(x_ref[...], g_ref[...], b_ref[...]).astype(BF)
        y1_ref[...] = y1
        y_ref[...] = lax.dot_general(y1, w_ref[...], _DN["nn"], preferred_element_type=F32).astype(BF)

    row = pl.BlockSpec((tb, C), lambda i: (i, 0))
    vec = pl.BlockSpec((1, C), lambda i: (0, 0))
    return pl.pallas_call(
        body, grid=(T // tb,), in_specs=[row, vec, vec, pl.BlockSpec((None, C, D), lambda i: (0, 0, 0), pipeline_mode=pl.Buffered(1))],
        out_specs=[row, pl.BlockSpec((tb, D), lambda i: (i, 0))], out_shape=[SDS((T, C), BF), SDS((T, D), BF)],
        compiler_params=_params("parallel"), name=name)(cv, g.reshape(1, C), b.reshape(1, C), w)


def _mm_ln_silu_bwd(dyc, w, cv, g, b, name, side=None):
    T, C = cv.shape
    D = w.shape[2]
    tb = _tile(T, 2 * ROW_BLOCK, 8)

    def body(d_ref, w_ref, x_ref, g_ref, b_ref, dx_ref, dg_ref, db_ref):
        dy1 = lax.dot_general(d_ref[...], w_ref[...], _DN["nt"], preferred_element_type=F32)
        _, vjp = jax.vjp(_ln_silu, x_ref[...], g_ref[...], b_ref[...])
        dx, dg, db = vjp(dy1)
        dx_ref[...] = dx

        @pl.when(pl.program_id(0) == 0)
        def _():
            dg_ref[...] = jnp.zeros_like(dg_ref)
            db_ref[...] = jnp.zeros_like(db_ref)

        dg_ref[...] += dg
        db_ref[...] += db

    row = pl.BlockSpec((tb, C), lambda i: (i, 0))
    vec = pl.BlockSpec((1, C), lambda i: (0, 0))
    return _call(
        body, (dyc, w, cv, g.reshape(1, C), b.reshape(1, C)), grid=(T // tb,),
        in_specs=[pl.BlockSpec((tb, D), lambda i: (i, 0)), pl.BlockSpec((None, C, D), lambda i: (0, 0, 0), pipeline_mode=pl.Buffered(1)), row, vec, vec],
        out_specs=[row, vec, vec], out_shape=[SDS((T, C), F32), SDS((1, C), F32), SDS((1, C), F32)], semantics=("arbitrary",),
        name=name, side=side)


def _merge_fwd(proj, yc, yp, ps, C, name, side=None):
    T, D = yc.shape
    tb = _tile(T, ROW_BLOCK, 8)
    nj = D // C

    def body(gc_ref, gp_ref, yc_ref, yp_ref, ps_ref, o_ref):
        o_ref[...] = _merge(gc_ref[...], gp_ref[...], yc_ref[...].astype(F32), yp_ref[...].astype(F32), ps_ref[...]).astype(BF)

    blk = pl.BlockSpec((tb, C), lambda i, j: (i, j))
    return _call1(
        body, (proj, proj, yc, yp, ps.reshape(1, D)), grid=(T // tb, nj),
        in_specs=[pl.BlockSpec((tb, C), lambda i, j: (i, 3 + j)), pl.BlockSpec((tb, C), lambda i, j: (i, 3 + nj + j)), blk, blk,
                  pl.BlockSpec((1, C), lambda i, j: (0, j))],
        out_spec=blk, out_shape=SDS((T, D), BF), semantics=("parallel", "parallel"), name=name, side=side)


def _merge_bwd(proj, yc, yp, ps, dm, C, name, side=None):
    T, D = yc.shape
    tb = _tile(T, ROW_BLOCK, 8)
    nj = D // C

    def body(gc_ref, gp_ref, yc_ref, yp_ref, ps_ref, dm_ref, dg_ref, dyc_ref, dyp_ref, dps_ref):
        _, vjp = jax.vjp(_merge, gc_ref[...], gp_ref[...], yc_ref[...].astype(F32), yp_ref[...].astype(F32), ps_ref[...])
        dgc, dgp, dyc, dyp, dps = vjp(dm_ref[...].astype(F32))
        dg_ref[0] = dgc.astype(BF)
        dg_ref[1] = dgp.astype(BF)
        dyc_ref[...] = dyc.astype(BF)
        dyp_ref[...] = dyp.astype(BF)

        @pl.when(pl.program_id(1) == 0)
        def _():
            dps_ref[...] = jnp.zeros_like(dps_ref)

        dps_ref[...] += dps

    blk = pl.BlockSpec((tb, C), lambda j, i: (i, j))
    vec = pl.BlockSpec((1, C), lambda j, i: (0, j))
    return _call(
        body, (proj, proj, yc, yp, ps.reshape(1, D), dm), grid=(nj, T // tb),
        in_specs=[pl.BlockSpec((tb, C), lambda j, i: (i, 3 + j)), pl.BlockSpec((tb, C), lambda j, i: (i, 3 + nj + j)), blk, blk, vec, blk],
        out_specs=[pl.BlockSpec((2, tb, C), lambda j, i: (0, i, j)), blk, blk, vec],
        out_shape=[SDS((2, T, D), BF), SDS((T, D), BF), SDS((T, D), BF), SDS((1, D), F32)],
        semantics=("parallel", "arbitrary"), name=name, side=side)


def _shd(v, s, rows):
    if s == 0:
        return v
    return jnp.where(rows >= s, pltpu.roll(v, s, 0), 0.0)


def _shu(v, s, rows):
    if s == 0:
        return v
    n = v.shape[0]
    return jnp.where(rows < n - s, pltpu.roll(v, n - s, 0), 0.0)


def _glu_conv_fwd(proj, w, b, Bn, S, C, name, side=None):
    K = w.shape[0]
    sl = min(LANES, C)
    ns = C // sl

    def body(a_ref, gl_ref, w_ref, b_ref, o_ref):
        y0 = a_ref[...] * jax.nn.sigmoid(gl_ref[...])
        rows = lax.broadcasted_iota(jnp.int32, y0.shape, 0)
        acc = jnp.zeros_like(y0) + b_ref[...]
        for k in range(K):
            acc = acc + w_ref[k:k + 1, :] * _shd(y0, K - 1 - k, rows)
        o_ref[...] = acc

    return _call1(
        body, (proj, proj, w, b.reshape(1, C)), grid=(Bn, ns),
        in_specs=[pl.BlockSpec((S, sl), lambda bi, j: (bi, j)), pl.BlockSpec((S, sl), lambda bi, j: (bi, ns + j)),
                  pl.BlockSpec((K, sl), lambda bi, j: (0, j)), pl.BlockSpec((1, sl), lambda bi, j: (0, j))],
        out_spec=pl.BlockSpec((S, sl), lambda bi, j: (bi, j)), out_shape=SDS((Bn * S, C), F32),
        semantics=("parallel", "parallel"), name=name, side=side)


def _glu_conv_bwd(proj, w, dcv, Bn, S, C, name, side=None):
    K = w.shape[0]
    sl = min(LANES, C)
    ns = C // sl

    def body(a_ref, gl_ref, w_ref, d_ref, dagl_ref, dw_ref, db_ref):
        a = a_ref[...]
        sg = jax.nn.sigmoid(gl_ref[...])
        y0 = a * sg
        d = d_ref[...]
        rows = lax.broadcasted_iota(jnp.int32, y0.shape, 0)

        @pl.when(pl.program_id(1) == 0)
        def _():
            dw_ref[...] = jnp.zeros_like(dw_ref)
            db_ref[...] = jnp.zeros_like(db_ref)

        dy0 = jnp.zeros_like(y0)
        for k in range(K):
            s = K - 1 - k
            dw_ref[k:k + 1, :] += jnp.sum(d * _shd(y0, s, rows), axis=0, keepdims=True)
            dy0 = dy0 + w_ref[k:k + 1, :] * _shu(d, s, rows)
        db_ref[...] += jnp.sum(d, axis=0, keepdims=True)
        dagl_ref[0] = (dy0 * sg).astype(BF)
        dagl_ref[1] = (dy0 * a * sg * (1.0 - sg)).astype(BF)

    blk = pl.BlockSpec((S, sl), lambda j, bi: (bi, j))
    return _call(
        body, (proj, proj, w, dcv), grid=(ns, Bn),
        in_specs=[blk, pl.BlockSpec((S, sl), lambda j, bi: (bi, ns + j)), pl.BlockSpec((K, sl), lambda j, bi: (0, j)), blk],
        out_specs=[pl.BlockSpec((2, S, sl), lambda j, bi: (0, bi, j)), pl.BlockSpec((K, sl), lambda j, bi: (0, j)),
                   pl.BlockSpec((1, sl), lambda j, bi: (0, j))],
        out_shape=[SDS((2, Bn * S, C), BF), SDS((K, C), F32), SDS((1, C), F32)],
        semantics=("parallel", "arbitrary"), name=name, side=side)


def _pool_z(u, g, rows):
    s2 = u + _shd(u, 1, rows)
    s4 = s2 + _shd(s2, 2, rows)
    s8 = s4 + _shd(s4, 4, rows)
    s16 = s8 + _shd(s8, 8, rows)
    sw = jnp.where(g == 0, s2, jnp.where(g == 1, s4, jnp.where(g == 2, s8, s16)))
    cnt = jnp.minimum(rows + 1, POOL_WINDOWS[0] << g).astype(F32)
    return sw / cnt - u, cnt


def _pool_fwd(proj, wpt, l, Bn, S, C, D, name):
    G = len(POOL_WINDOWS)
    gd, go = C // G, D // G

    def body(u_ref, w_ref, o_ref):
        g = pl.program_id(1)
        u = u_ref[...]
        rows = lax.broadcasted_iota(jnp.int32, u.shape, 0)
        zp, _ = _pool_z(u, g, rows)
        o_ref[...] = lax.dot_general(zp.astype(BF), w_ref[...], _DN["nt"], preferred_element_type=F32).astype(BF)

    return pl.pallas_call(
        body, grid=(Bn, G),
        in_specs=[pl.BlockSpec((S, gd), lambda bi, g: (bi, 2 * G + g)), pl.BlockSpec((None, go, gd), lambda bi, g: (l * G + g, 0, 0))],
        out_specs=pl.BlockSpec((S, go), lambda bi, g: (bi, g)), out_shape=SDS((Bn * S, D), BF),
        compiler_params=_params("parallel", "parallel"), name=name)(proj, wpt)


def _pool_bwd(proj, wpt, dyp, l, Bn, S, C, D, name):
    G = len(POOL_WINDOWS)
    gd, go = C // G, D // G

    def body(u_ref, w_ref, d_ref, du_ref, dw_ref):
        g = pl.program_id(0)
        u = u_ref[...]
        rows = lax.broadcasted_iota(jnp.int32, u.shape, 0)
        zp, cnt = _pool_z(u, g, rows)
        d = d_ref[...]
        dzp = lax.dot_general(d, w_ref[...], _DN["nn"], preferred_element_type=F32)

        @pl.when(pl.program_id(1) == 0)
        def _():
            dw_ref[...] = jnp.zeros_like(dw_ref)

        dw_ref[...] += lax.dot_general(d, zp.astype(BF), _DN["tn"], preferred_element_type=F32)
        dsw = dzp / cnt
        zero = jnp.zeros_like(dsw)
        d16 = jnp.where(g == 3, dsw, zero)
        d8 = jnp.where(g == 2, dsw, zero) + d16 + _shu(d16, 8, rows)
        d4 = jnp.where(g == 1, dsw, zero) + d8 + _shu(d8, 4, rows)
        d2 = jnp.where(g == 0, dsw, zero) + d4 + _shu(d4, 2, rows)
        d1 = d2 + _shu(d2, 1, rows)
        du_ref[...] = (d1 - dzp).astype(BF)

    return pl.pallas_call(
        body, grid=(G, Bn),
        in_specs=[pl.BlockSpec((S, gd), lambda g, bi: (bi, 2 * G + g)), pl.BlockSpec((None, go, gd), lambda g, bi: (l * G + g, 0, 0)),
                  pl.BlockSpec((S, go), lambda g, bi: (bi, g))],
        out_specs=[pl.BlockSpec((S, gd), lambda g, bi: (bi, g)), pl.BlockSpec((None, go, gd), lambda g, bi: (g, 0, 0))],
        out_shape=[SDS((Bn * S, C), BF), SDS((G, go, gd), F32)],
        compiler_params=_params("parallel", "arbitrary"), name=name)(proj, wpt, dyp)


def _ffn_conv(u, w_ref, rows):
    K = w_ref.shape[0]
    acc = w_ref[K - 1:K, :] * u
    for k in range(K - 1):
        acc = acc + w_ref[k:k + 1, :] * _shd(u, K - 1 - k, rows)
    return acc


def _ffn_cb(F):
    return _tile(F, 256)


def _ffn_act_fwd(up0, w, Bn, S, F, name, side=None):
    cb = _ffn_cb(F)
    nj = F // cb

    def body(g_ref, v_ref, wg_ref, wv_ref, o_ref):
        rows = lax.broadcasted_iota(jnp.int32, g_ref.shape, 0)
        o_ref[...] = _gated(_ffn_conv(g_ref[...], wg_ref, rows), _ffn_conv(v_ref[...], wv_ref, rows)).astype(BF)

    K = w.shape[0]
    return _call1(
        body, (up0, up0, w, w), grid=(Bn, nj),
        in_specs=[pl.BlockSpec((S, cb), lambda bi, j: (bi, j)), pl.BlockSpec((S, cb), lambda bi, j: (bi, nj + j)),
                  pl.BlockSpec((K, cb), lambda bi, j: (0, j)), pl.BlockSpec((K, cb), lambda bi, j: (0, nj + j))],
        out_spec=pl.BlockSpec((S, cb), lambda bi, j: (bi, j)), out_shape=SDS((Bn * S, F), BF),
        semantics=("parallel", "parallel"), name=name, side=side)


SUBLANES = 8
FFN_HALO = SUBLANES
FFN_ROWS = 128
GELU_C0, GELU_C1 = 0.7978845608028654, 0.044715


def _gelu_and_grad(x):
    x2 = x * x
    t = jnp.tanh(GELU_C0 * (x + GELU_C1 * (x2 * x)))
    cdf = 0.5 * (1.0 + t)
    return x * cdf, cdf + (0.5 * GELU_C0) * x * (1.0 - t * t) * (1.0 + (3.0 * GELU_C1) * x2)


def _ffn_act_bwd(up0, w, dg, Bn, S, F, name, side=None):
    cb = min(LANES, F)
    nj = F // cb
    K = w.shape[0]
    rc = FFN_ROWS if S % FFN_ROWS == 0 else S
    win = rc + 2 * FFN_HALO
    assert K - 1 <= FFN_HALO and rc % SUBLANES == 0

    def body(g_ref, v_ref, wg_ref, wv_ref, d_ref, do_ref, dwg_ref, dwv_ref, gp, vp, dp):
        for pad, src in ((gp, g_ref), (vp, v_ref), (dp, d_ref)):
            pad[0:FFN_HALO, :] = jnp.zeros((FFN_HALO, cb), F32)
            pad[FFN_HALO + S:, :] = jnp.zeros((FFN_HALO, cb), F32)
            pad[FFN_HALO:FFN_HALO + S, :] = src[...].astype(F32)
        wg = [wg_ref[k:k + 1, :] for k in range(K)]
        wv = [wv_ref[k:k + 1, :] for k in range(K)]

        def taps(u):
            return [pltpu.roll(u, K - 1 - k, 0) for k in range(K - 1)] + [u]

        def conv(us, ws):
            acc = ws[K - 1] * us[K - 1]
            for k in range(K - 1):
                acc = acc + ws[k] * us[k]
            return acc

        def conv_t(dc, ws):
            acc = ws[K - 1] * dc
            for k in range(K - 1):
                acc = acc + ws[k] * pltpu.roll(dc, win - (K - 1 - k), 0)
            return acc

        def fold(t):
            acc = t[FFN_HALO:FFN_HALO + SUBLANES]
            for i in range(1, rc // SUBLANES):
                acc = acc + t[FFN_HALO + SUBLANES * i:FFN_HALO + SUBLANES * (i + 1)]
            return acc

        def chunk(c, sums):
            r0 = pl.multiple_of(c * rc, SUBLANES)
            gs, vs, d = taps(gp[pl.ds(r0, win), :]), taps(vp[pl.ds(r0, win), :]), dp[pl.ds(r0, win), :]
            ge, dge = _gelu_and_grad(conv(gs, wg))
            dgc = d * conv(vs, wv) * dge
            dvc = d * ge
            do_ref[0, pl.ds(r0, rc), :] = conv_t(dgc, wg)[FFN_HALO:FFN_HALO + rc].astype(BF)
            do_ref[1, pl.ds(r0, rc), :] = conv_t(dvc, wv)[FFN_HALO:FFN_HALO + rc].astype(BF)
            new = [fold(dc * u) for us, dc in ((gs, dgc), (vs, dvc)) for u in us]
            return tuple(a + b for a, b in zip(sums, new))

        sums = lax.fori_loop(0, S // rc, chunk, tuple(jnp.zeros((SUBLANES, cb), F32) for _ in range(2 * K)))

        @pl.when(pl.program_id(1) == 0)
        def _():
            dwg_ref[...] = jnp.zeros_like(dwg_ref)
            dwv_ref[...] = jnp.zeros_like(dwv_ref)

        for k in range(K):
            dwg_ref[k:k + 1, :] += jnp.sum(sums[k], axis=0, keepdims=True)
            dwv_ref[k:k + 1, :] += jnp.sum(sums[K + k], axis=0, keepdims=True)

    blk = pl.BlockSpec((S, cb), lambda j, bi: (bi, j))
    wblk = pl.BlockSpec((K, cb), lambda j, bi: (0, j))
    return _call(
        body, (up0, up0, w, w, dg), grid=(nj, Bn),
        in_specs=[blk, pl.BlockSpec((S, cb), lambda j, bi: (bi, nj + j)), wblk, pl.BlockSpec((K, cb), lambda j, bi: (0, nj + j)), blk],
        out_specs=[pl.BlockSpec((2, S, cb), lambda j, bi: (0, bi, j)), wblk, wblk],
        out_shape=[SDS((2, Bn * S, F), BF), SDS((K, F), F32), SDS((K, F), F32)],
        scratch_shapes=[pltpu.VMEM((S + 2 * FFN_HALO, cb), F32)] * 3, semantics=("parallel", "arbitrary"), name=name, side=side)


def _softmax_rows(q, k, scale):
    sc = lax.dot_general(q, k, _DN["nt"], preferred_element_type=F32) * scale
    e = jnp.exp(sc - jnp.max(sc, axis=-1, keepdims=True))
    return e / jnp.sum(e, axis=-1, keepdims=True)


def _attn_ts(S):
    return _tile(S, 1024, 8)


def _attn_fwd(q, kv, Bn, S, Mn, D, name, side=None):
    H = XA_HEADS
    dh = D // H
    ts = _attn_ts(S)
    nsb = S // ts
    scale = dh ** -0.5

    def body(q_ref, k_ref, v_ref, o_ref):
        p = _softmax_rows(q_ref[...], k_ref[...], scale)
        o_ref[...] = lax.dot_general(p.astype(BF), v_ref[...], _DN["nn"], preferred_element_type=F32).astype(BF)

    qblk = pl.BlockSpec((ts, dh), lambda bi, h, s: (bi * nsb + s, h))
    return _call1(
        body, (q, kv, kv), grid=(Bn, H, nsb),
        in_specs=[qblk, pl.BlockSpec((Mn, dh), lambda bi, h, s: (bi, h)), pl.BlockSpec((Mn, dh), lambda bi, h, s: (bi, H + h))],
        out_spec=qblk, out_shape=SDS((Bn * S, D), BF), semantics=("parallel", "parallel", "parallel"), name=name, side=side)


def _attn_bwd(q, kv, datt, Bn, S, Mn, D, name):
    H = XA_HEADS
    dh = D // H
    ts = _attn_ts(S)
    nsb = S // ts
    scale = dh ** -0.5

    def body(q_ref, k_ref, v_ref, do_ref, dq_ref, dk_ref, dv_ref):
        q, k, v, do = q_ref[...], k_ref[...], v_ref[...], do_ref[...]
        p = _softmax_rows(q, k, scale)
        dp = lax.dot_general(do, v, _DN["nt"], preferred_element_type=F32)
        ds = (p * (dp - jnp.sum(dp * p, axis=-1, keepdims=True)) * scale).astype(BF)
        dq_ref[...] = lax.dot_general(ds, k, _DN["nn"], preferred_element_type=F32).astype(BF)

        @pl.when(pl.program_id(2) == 0)
        def _():
            dk_ref[...] = jnp.zeros_like(dk_ref)
            dv_ref[...] = jnp.zeros_like(dv_ref)

        dk_ref[...] += lax.dot_general(ds, q, _DN["tn"], preferred_element_type=F32)
        dv_ref[...] += lax.dot_general(p.astype(BF), do, _DN["tn"], preferred_element_type=F32)

    qblk = pl.BlockSpec((ts, dh), lambda bi, h, s: (bi * nsb + s, h))
    kblk = pl.BlockSpec((Mn, dh), lambda bi, h, s: (bi, h))
    return pl.pallas_call(
        body, grid=(Bn, H, nsb),
        in_specs=[qblk, kblk, pl.BlockSpec((Mn, dh), lambda bi, h, s: (bi, H + h)), qblk],
        out_specs=[qblk, kblk, kblk], out_shape=[SDS((Bn * S, D), BF), SDS((Bn * Mn, D), F32), SDS((Bn * Mn, D), F32)],
        compiler_params=_params("parallel", "parallel", "arbitrary"), name=name)(q, kv, kv, datt)


class _Sides:
    def __init__(self, by_key=None, on_land=None):
        self.by_key, self.landed, self.on_land = dict(by_key or {}), {}, on_land

    def run(self, key, fn, *args, **kw):
        side = self.by_key.get(key)
        if side is None:
            return fn(*args, **kw)
        out, self.landed[key] = fn(*args, side=side() if callable(side) else side, **kw)
        if self.on_land is not None:
            self.on_land(key, self.landed[key])
        return out

    def mm(self, key, *args, **kw):
        return self.run(key, _mm, *args, **kw)


def _layer_fwd(x, h, mem_n, W, V, l, dims, sides, next_g):
    Bn, S, Mn, D, C, F = dims
    n = f"l{l}_"
    proj = sides.mm("proj", h, W["w_in"], "nn", F32, n + "proj", bl=0)
    cv = sides.run("glu_conv", _glu_conv_fwd, proj, V["conv_dw_w"][l], V["conv_dw_b"][l], Bn, S, C, n + "glu_conv")
    yc1, yc = _ln_silu_mm(cv, V["conv_ln_g"][l], V["conv_ln_b"][l], W["w_conv_out"], n + "conv_out")
    yp = _pool_fwd(proj, W["w_pool"], 0, Bn, S, C, D, n + "pool")
    merged = sides.run("merge", _merge_fwd, proj, yc, yp, V["pool_scale"][l], C, n + "merge")
    x1, hq = sides.run("out_proj", _mm_rms_fwd, merged, W["w_out"], x, V["xattn_norm_g"][l], n + "out_proj")
    q = sides.mm("q_proj", hq, W["w_q"], "nn", BF, n + "q_proj", bl=0)
    kv = _mm(mem_n, W["w_kv"], "nn", BF, n + "kv_proj", bl=0)
    att = sides.run("attn", _attn_fwd, q, kv, Bn, S, Mn, D, n + "attn")
    x2, hf = sides.run("o_proj", _mm_rms_fwd, att, W["w_o"], x1, V["ffn_norm_g"][l], n + "o_proj")
    up0 = sides.mm("up_proj", hf, W["w_up"], "nn", F32, n + "up_proj", bl=0)
    gact = sides.run("ffn_act", _ffn_act_fwd, up0, V["ffn_dw_w"][l], Bn, S, F, n + "ffn_act")
    if next_g is not None:
        x3, h3 = sides.run("down_proj", _mm_rms_fwd, gact, W["w_down"], x2, next_g, n + "down_proj")
    else:
        x3, h3 = sides.mm("down_proj", gact, W["w_down"], "nn", F32, n + "down_proj", res=x2, bl=0), None
    return x3, h3, dict(x=x, h=h, proj=proj, cv=cv, yc1=yc1, yc=yc, yp=yp, merged=merged, x1=x1, hq=hq, q=q, kv=kv, att=att, x2=x2,
                        hf=hf, up0=up0, gact=gact)


def _layer_bwd_mlp(dx, dxb, sv, W, V, l, dims, sides):
    Bn, S, Mn, D, C, F = dims
    n = f"l{l}_b_"
    gw, sm = {}, {}
    dgact = sides.mm("d_gact", dxb, W["w_down"], "nt", BF, n + "d_gact", bl=0)
    gw["w_down"] = sides.mm("dw_down", sv["gact"], dxb, "tn", F32, n + "dw_down", twin=BF)
    dup0, dwg, dwv = sides.run("ffn_act_b", _ffn_act_bwd, sv["up0"], V["ffn_dw_w"][l], dgact, Bn, S, F, n + "ffn_act")
    sm["ffn_dw_w"] = jnp.concatenate([dwg, dwv], axis=1)
    dx2, dx2b, sm["ffn_norm_g"] = _mm_rms_bwd([dup0], W["w_up"], sv["x2"], V["ffn_norm_g"][l], dx, n + "d_hf")
    gw["w_up"] = sides.mm("dw_up", sv["hf"], dup0, "tn", F32, n + "dw_up", twin=BF, b_halves=True)
    return dx2, dx2b, gw, sm


def _layer_bwd_mix(dx2, dx2b, dmem_n, sv, mem_n, W, V, l, dims, sides, gw):
    Bn, S, Mn, D, C, F = dims
    n = f"l{l}_b_"
    sm = {}
    datt = sides.mm("d_att", dx2b, W["w_o"], "nt", BF, n + "d_att", bl=0)
    gw["w_o"] = _mm(sv["att"], dx2b, "tn", F32, n + "dw_o", twin=BF)
    dq, dk, dv = _attn_bwd(sv["q"], sv["kv"], datt, Bn, S, Mn, D, n + "attn")
    dkv = jnp.concatenate([dk, dv], axis=1)
    gw["w_kv"] = _mm(mem_n, dkv, "tn", F32, n + "dw_kv", twin=BF)
    dmem_n = _mm(dkv, W["w_kv"], "nt", F32, n + "d_mem", res=dmem_n, bl=0)
    dx1, dx1b, sm["xattn_norm_g"] = _mm_rms_bwd([dq], W["w_q"], sv["x1"], V["xattn_norm_g"][l], dx2, n + "d_hq")
    gw["w_q"] = _mm(sv["hq"], dq, "tn", F32, n + "dw_q", twin=BF)
    dmerged = sides.mm("d_merged", dx1b, W["w_out"], "nt", BF, n + "d_merged", bl=0)
    gw["w_out"] = _mm(sv["merged"], dx1b, "tn", F32, n + "dw_out", twin=BF)
    dgates, dyc, dyp, sm["pool_scale"] = sides.run("merge_b", _merge_bwd, sv["proj"], sv["yc"], sv["yp"], V["pool_scale"][l], dmerged, C, n + "merge")
    du, dwp = _pool_bwd(sv["proj"], W["w_pool"], dyp, 0, Bn, S, C, D, n + "pool")
    gw["w_pool"] = (dwp, dwp.astype(BF))
    gw["w_conv_out"] = _mm(sv["yc1"], dyc, "tn", F32, n + "dw_conv_out", twin=BF)
    dcv, sm["conv_ln_g"], sm["conv_ln_b"] = sides.run("ln_silu_b", _mm_ln_silu_bwd, dyc, W["w_conv_out"], sv["cv"], V["conv_ln_g"][l],
                                                      V["conv_ln_b"][l], n + "d_yc1")
    dagl, sm["conv_dw_w"], sm["conv_dw_b"] = sides.run("glu_conv_b", _glu_conv_bwd, sv["proj"], V["conv_dw_w"][l], dcv, Bn, S, C, n + "glu_conv")
    dx, dxb, sm["mix_norm_g"] = sides.run("d_h", _mm_rms_bwd, [dagl, du, dgates], W["w_in"], sv["x"], V["mix_norm_g"][l], dx1, n + "d_h")
    n_in = W["w_in"].shape[2]
    part = _mm(sv["h"], dagl, "tn", F32, n + "dw_in_conv", twin=BF, b_halves=True, part=(n_in, 0, None))
    part = _mm(sv["h"], du, "tn", F32, n + "dw_in_pool", twin=BF, part=(n_in, 2 * C, part))
    gw["w_in"] = sides.mm("dw_in", sv["h"], dgates, "tn", F32, n + "dw_in", twin=BF, b_halves=True, part=(n_in, 3 * C, part))
    return dx, dxb, dmem_n, sm


BIG = (("w_in", "col"), ("w_conv_out", "col"), ("w_pool", "row"), ("w_out", "row"), ("w_q", "row"), ("w_kv", "col"),
       ("w_o", "row"), ("w_up", "col"), ("w_down", "row"))
ALL_RELS = (1, 2, 3)
GATHER_FIRST = ("w_in", "w_conv_out", "w_pool", "w_out")
FWD_CARRY = {
    (0, "proj"): (("w_up", 0, (1, 2)), ("w_q", 0, ALL_RELS), ("w_o", 0, ALL_RELS)),
    (0, "glu_conv"): (("w_kv", 0, ALL_RELS),),
    (0, "merge"): (("w_up", 0, (3,)),),
    (0, "q_proj"): (("w_down", 0, (1, 2)),),
    (0, "attn"): (("w_down", 0, (3,)),),
    (0, "up_proj"): (("w_in", 1, ALL_RELS), ("w_conv_out", 1, ALL_RELS), ("w_pool", 1, ALL_RELS), ("w_o", 1, ALL_RELS)),
    (0, "ffn_act"): (("w_out", 1, ALL_RELS), ("w_q", 1, ALL_RELS), ("w_kv", 1, ALL_RELS)),
    (1, "proj"): (("w_up", 1, (1, 2)),),
    (1, "glu_conv"): (("w_down", 1, (1, 2)),),
    (1, "merge"): (("w_up", 1, (3,)),),
    (1, "attn"): (("w_down", 1, (3,)),),
}
PASS_CARRY = {
    (0, "out_proj"): (("w_kv", 0), ("w_q", 0), ("w_o", 0)),
    (0, "o_proj"): (("w_up", 0), ("w_down", 0)),
    (0, "down_proj"): (("w_in", 1), ("w_conv_out", 1), ("w_pool", 1), ("w_out", 1), ("w_q", 1), ("w_kv", 1), ("w_o", 1)),
    (1, "o_proj"): (("w_up", 1), ("w_down", 1)),
}
EARLY = ("w_down", "w_up")
BWD_CARRY_EARLY = {"merge_b": ("w_down",), "glu_conv_b": ("w_up",)}
BWD_CARRY_LATE = {"ffn_act_b": ("w_in", "w_conv_out", "w_pool", "w_out", "w_q", "w_kv", "w_o")}
BWD_LAST_LAYER = (("att", ("w_o", "w_kv", "w_q"), "d_merged", {"d_h": ("w_o", "w_kv", "w_q")}),
                  ("tok", ("w_out", "w_pool", "w_conv_out"), "ln_silu_b", {"dw_in": ("w_out", "w_pool", "w_conv_out")}))


def _place():
    xi, yi, ci = lax.axis_index("x"), lax.axis_index("y"), lax.axis_index("c")
    return xi, yi, ci, 2 * xi + yi


def _chip_peer(xi, yi, ci, r):
    return (xi ^ (r >> 1), yi ^ (r & 1), ci)


def _full_shard(ref, kind, k, cs):
    if kind == "col":
        return ref.at[:, :, :, :, pl.ds(pl.multiple_of(k * cs, cs), cs)]
    return ref.at[:, :, k]


def _gather_weights(shards, kinds):
    n = len(shards)
    outs = []
    for s, kind in zip(shards, kinds):
        L, P, _, RH, CS = s.shape
        outs.append(SDS((L, P, 2, RH, CS * N_CHIPS) if kind == "col" else (L, P, N_CHIPS, 2, RH, CS), s.dtype))
    per = 7

    def body(*refs):
        srcs, fulls, (ssem, rsem) = refs[:n], refs[n:2 * n], refs[2 * n:]
        xi, yi, ci, j = _place()
        sib = (xi, yi, 1 - ci)

        def piece(i, k, c):
            kind, cs = kinds[i], shards[i].shape[-1]
            if kind == "col":
                return fulls[i].at[:, :, c, :, pl.ds(pl.multiple_of(k * cs, cs), cs)]
            return fulls[i].at[:, :, k, c]

        def copy(i, slot, src, dst, dev):
            return pltpu.make_async_remote_copy(src_ref=src, dst_ref=dst, send_sem=ssem.at[per * i + slot], recv_sem=rsem.at[per * i + slot],
                                                device_id=dev, device_id_type=MESH)

        own, first, passed = [], [], []
        for i in range(n):
            for r in (1, 2, 3):
                first.append(copy(i, r - 1, srcs[i].at[:, :, ci], piece(i, j, ci), _chip_peer(xi, yi, ci, r)))
                first[-1].start()
        for i in range(n):
            own.append(copy(i, 6, srcs[i], _full_shard(fulls[i], kinds[i], j, shards[i].shape[-1]), sib))
            own[-1].start()
        for i in range(n):
            for r in (1, 2, 3):
                got = piece(i, j ^ r, ci)
                copy(i, r - 1, got, got, sib).wait_recv()
                passed.append(copy(i, 2 + r, got, got, sib))
                passed[-1].start()
        for i in range(n):
            for r in (1, 2, 3):
                got = piece(i, j ^ r, 1 - ci)
                copy(i, 2 + r, got, got, sib).wait_recv()
        for cp in own:
            cp.wait()
        for cp in first + passed:
            cp.wait_send()

    return pl.pallas_call(
        body, in_specs=[ANY] * n, out_specs=[ANY] * n, out_shape=outs,
        scratch_shapes=[pltpu.SemaphoreType.DMA((per * n,)), pltpu.SemaphoreType.DMA((per * n,))], name="gather_weights")(*shards)


def _full_sds(s, kind):
    L, P, _, RH, CS = s.shape
    return SDS((L, P, 2, RH, CS * N_CHIPS) if kind == "col" else (L, P, N_CHIPS, 2, RH, CS), s.dtype)


def _gather_piece(full, kind, cs, k, c):
    if kind == "col":
        return full.at[:, :, c, :, pl.ds(pl.multiple_of(k * cs, cs), cs)]
    return full.at[:, :, k, c]


def _side_gather(shards, kinds, rels, fulls):
    n = len(shards)

    def make(srcs, outs, ssem, rsem):
        xi, yi, ci, j = _place()
        return [pltpu.make_async_remote_copy(
            src_ref=srcs[i].at[:, :, ci], dst_ref=_gather_piece(outs[i], kinds[i], shards[i].shape[-1], j, ci), send_sem=ssem.at[3 * i + r - 1],
            recv_sem=rsem.at[3 * i + r - 1], device_id=_chip_peer(xi, yi, ci, r), device_id_type=MESH) for i in range(n) for r in rels[i]]

    prior = [f for f in fulls if f is not None]
    assert len(prior) in (0, n)
    return _Side(list(shards) + prior, [_full_sds(s, k) for s, k in zip(shards, kinds)], 3 * n, make, n_alias=len(prior))


def _side_gather_pass(fulls, shards, kinds):
    n = len(fulls)

    def make(srcs, outs, ssem, rsem):
        xi, yi, ci, j = _place()
        sib = (xi, yi, 1 - ci)
        cps = []
        for i in range(n):
            cs = shards[i].shape[-1]
            for r in (1, 2, 3):
                got = _gather_piece(outs[i], kinds[i], cs, j ^ r, ci)
                cps.append(pltpu.make_async_remote_copy(src_ref=got, dst_ref=got, send_sem=ssem.at[4 * i + r - 1], recv_sem=rsem.at[4 * i + r - 1],
                                                        device_id=sib, device_id_type=MESH))
            cps.append(pltpu.make_async_remote_copy(src_ref=srcs[i], dst_ref=_full_shard(outs[i], kinds[i], j, cs), send_sem=ssem.at[4 * i + 3],
                                                    recv_sem=rsem.at[4 * i + 3], device_id=sib, device_id_type=MESH))
        return cps

    return _Side(list(shards) + list(fulls), [SDS(f.shape, f.dtype) for f in fulls], 4 * n, make, n_alias=n)


def _sibling_exchange(gviews, kinds, name):
    n = len(gviews)
    outs = [SDS(g.shape[:1] + g.shape[2:] if kind == "col" else g.shape[:2] + g.shape[3:], g.dtype) for g, kind in zip(gviews, kinds)]

    def body(*refs):
        gs, lands, (ssem, rsem) = refs[:n], refs[n:2 * n], refs[2 * n:]
        xi, yi, ci, _ = _place()
        cps = []
        for i in range(n):
            src = gs[i].at[:, 1 - ci] if kinds[i] == "col" else gs[i].at[:, :, 1 - ci]
            cps.append(pltpu.make_async_remote_copy(src_ref=src, dst_ref=lands[i], send_sem=ssem.at[i], recv_sem=rsem.at[i],
                                                    device_id=(xi, yi, 1 - ci), device_id_type=MESH))
            cps[-1].start()
        for cp in cps:
            cp.wait()

    return pl.pallas_call(body, in_specs=[ANY] * n, out_specs=[ANY] * n, out_shape=outs,
                          scratch_shapes=[pltpu.SemaphoreType.DMA((n,)), pltpu.SemaphoreType.DMA((n,))], name=name)(*gviews)


def _side_sibling_exchange(gviews, kinds):
    outs = [SDS(g.shape[:1] + g.shape[2:] if kind == "col" else g.shape[:2] + g.shape[3:], g.dtype) for g, kind in zip(gviews, kinds)]

    def make(gs, lands, ssem, rsem):
        xi, yi, ci, _ = _place()
        return [pltpu.make_async_remote_copy(src_ref=gs[i].at[:, 1 - ci] if kinds[i] == "col" else gs[i].at[:, :, 1 - ci], dst_ref=lands[i],
                                             send_sem=ssem.at[i], recv_sem=rsem.at[i], device_id=(xi, yi, 1 - ci), device_id_type=MESH)
                for i in range(len(gs))]

    return _Side(gviews, outs, len(gviews), make)


def _chip_sums(gs, lands, kinds, jc, name):
    n = len(gs)
    args, in_specs, out_specs, out_shape = [], [], [], []
    for g, land, kind in zip(gs, lands, kinds):
        if kind == "col":
            P, _, RH, C = g.shape
            CS = C // N_CHIPS
            in_specs += [pl.BlockSpec((P, None, RH, CS), lambda r, jc: (0, jc[1], 0, jc[0] ^ r)),
                         pl.BlockSpec((P, RH, CS), lambda r, jc: (0, 0, jc[0] ^ r))]
        else:
            P, _, _, RH, CS = g.shape
            in_specs += [pl.BlockSpec((P, None, None, RH, CS), lambda r, jc: (0, jc[0] ^ r, jc[1], 0, 0)),
                         pl.BlockSpec((P, None, RH, CS), lambda r, jc: (0, jc[0] ^ r, 0, 0))]
        args += [g, land]
        out_specs += [pl.BlockSpec((P, RH, CS), lambda r, jc: (0, 0, 0)), pl.BlockSpec((None, P, RH, CS), lambda r, jc: (r, 0, 0, 0))]
        out_shape += [SDS((P, RH, CS), F32), SDS((N_CHIPS, P, RH, CS), BF)]

    def body(jc_ref, *refs):
        ins, outs = refs[:2 * n], refs[2 * n:]
        for i in range(n):
            s = ins[2 * i][...] + ins[2 * i + 1][...].astype(F32)
            outs[2 * i + 1][...] = s.astype(BF)

            @pl.when(pl.program_id(0) == 0)
            def _():
                outs[2 * i][...] = s

    outs = _call(body, args, grid=(N_CHIPS,), in_specs=in_specs, out_specs=out_specs, out_shape=out_shape, semantics=("arbitrary",),
                 name=name, prefetch=(jc,))
    return outs[0::2], outs[1::2]


def _chip_exchange_copies(srcs, lands, ssem, rsem):
    xi, yi, ci, _ = _place()
    return [pltpu.make_async_remote_copy(src_ref=srcs[i].at[r], dst_ref=lands[i].at[r], send_sem=ssem.at[3 * i + r - 1],
                                         recv_sem=rsem.at[3 * i + r - 1], device_id=_chip_peer(xi, yi, ci, r), device_id_type=MESH)
            for i in range(len(srcs)) for r in (1, 2, 3)]


def _side_chip_exchange(pieces):
    return _Side(pieces, [SDS(p.shape, p.dtype) for p in pieces], 3 * len(pieces), _chip_exchange_copies)


FINAL_SUM_STEPS = 2


def _final_sums(owns, lands, jc, shards, l, L, name, side=None):
    n = len(owns)
    args, in_specs, out_specs, out_shape = [], [], [], []
    for own, land in zip(owns, lands):
        P, RH, CS = own.shape
        hr = RH // FINAL_SUM_STEPS
        in_specs += [pl.BlockSpec((P, hr, CS), lambda h, jc: (0, h, 0))]
        in_specs += [pl.BlockSpec((None, P, hr, CS), functools.partial(lambda r, h, jc: (r, 0, h, 0), r)) for r in (1, 2, 3)]
        args += [own, land, land, land]
        out_specs.append(pl.BlockSpec((None, P, None, hr, CS), lambda h, jc: (l, 0, jc[1], h, 0)))
        out_shape.append(SDS((L, P, 2, RH, CS), F32))
    aliases = None
    if shards is not None:
        aliases = {4 * n + i: i for i in range(n)}
        in_specs += [ANY] * n
        args += list(shards)

    def body(jc_ref, *refs):
        outs = refs[len(args):]
        for i in range(n):
            o, a, b, c = (refs[4 * i + t][...] for t in range(4))
            outs[i][...] = ((o + a.astype(F32)) + b.astype(F32)) + c.astype(F32)

    return _call(body, args, grid=(FINAL_SUM_STEPS,), in_specs=in_specs, out_specs=out_specs, out_shape=out_shape, semantics=("arbitrary",),
                 name=name, prefetch=(jc,), aliases=aliases, side=side)


def _halves_exchange(shards, l, name):
    n = len(shards)

    def body(*refs):
        outs, (ssem, rsem) = refs[n:2 * n], refs[2 * n:]
        xi, yi, ci, _ = _place()
        cps = []
        for i in range(n):
            mine = outs[i].at[l, :, ci]
            cps.append(pltpu.make_async_remote_copy(src_ref=mine, dst_ref=mine, send_sem=ssem.at[i], recv_sem=rsem.at[i],
                                                    device_id=(xi, yi, 1 - ci), device_id_type=MESH))
            cps[-1].start()
        for i in range(n):
            land = outs[i].at[l, :, 1 - ci]
            pltpu.make_async_remote_copy(src_ref=land, dst_ref=land, send_sem=ssem.at[i], recv_sem=rsem.at[i],
                                         device_id=(xi, yi, 1 - ci), device_id_type=MESH).wait_recv()
        for cp in cps:
            cp.wait_send()

    return pl.pallas_call(body, in_specs=[ANY] * n, out_specs=[ANY] * n, out_shape=[SDS(s.shape, s.dtype) for s in shards],
                          input_output_aliases={i: i for i in range(n)},
                          scratch_shapes=[pltpu.SemaphoreType.DMA((n,)), pltpu.SemaphoreType.DMA((n,))], name=name)(*shards)


def _reduce_small(part, pieces):
    NR, Wd = part.shape
    ND = 2 * N_CHIPS
    n = len(pieces)

    def body(p_ref, *refs):
        srcs, o_ref, lands, (land, ssem, rsem, xs, xr) = refs[:n], refs[n], refs[n + 1:2 * n + 1], refs[2 * n + 1:]
        exchange = _chip_exchange_copies(srcs, lands, xs, xr)
        for cp in exchange:
            cp.start()
        xi, yi, ci, j = _place()
        me = 2 * j + ci
        land[me] = p_ref[...]
        cps = []
        for rr in range(1, ND):
            dev = (xi ^ (rr >> 2), yi ^ ((rr >> 1) & 1), ci ^ (rr & 1))
            cps.append(pltpu.make_async_remote_copy(src_ref=p_ref, dst_ref=land.at[me], send_sem=ssem.at[rr - 1], recv_sem=rsem.at[rr - 1],
                                                    device_id=dev, device_id_type=MESH))
            cps[-1].start()
        for rr in range(1, ND):
            got = land.at[me ^ rr]
            pltpu.make_async_remote_copy(src_ref=got, dst_ref=got, send_sem=ssem.at[rr - 1], recv_sem=rsem.at[rr - 1],
                                         device_id=(xi, yi, ci), device_id_type=MESH).wait_recv()
        acc = land[0]
        for d in range(1, ND):
            acc = acc + land[d]
        o_ref[...] = acc
        for cp in cps:
            cp.wait_send()
        for cp in exchange:
            cp.wait()

    vm = pl.BlockSpec(memory_space=pltpu.VMEM)
    outs = pl.pallas_call(
        body, in_specs=[vm] + [ANY] * n, out_specs=[vm] + [ANY] * n, out_shape=[SDS((NR, Wd), F32)] + [SDS(p.shape, p.dtype) for p in pieces],
        scratch_shapes=[pltpu.VMEM((ND, NR, Wd), F32), pltpu.SemaphoreType.DMA((ND - 1,)), pltpu.SemaphoreType.DMA((ND - 1,)),
                        pltpu.SemaphoreType.DMA((3 * n,)), pltpu.SemaphoreType.DMA((3 * n,))],
        name="small_grad_allreduce")(part, *pieces)
    return outs[0], list(outs[1:])


def _adamw_update(w_ref, g_ref, m_ref, v_ref, d_ref, mo_ref, vo_ref):
    g = g_ref[...]
    m = ADAM_B1 * m_ref[...] + (1.0 - ADAM_B1) * g
    v = ADAM_B2 * v_ref[...] + (1.0 - ADAM_B2) * jnp.square(g)
    m_hat = m / (1.0 - ADAM_B1 ** ADAM_STEP)
    v_hat = v / (1.0 - ADAM_B2 ** ADAM_STEP)
    d_ref[...] = -ADAM_LR * (m_hat / (jnp.sqrt(v_hat) + ADAM_EPS) + ADAM_WD * w_ref[...])
    mo_ref[...] = m
    vo_ref[...] = v


ADAMW_STEPS = 8


def _adamw_layer(ws, gs, ms, vs, prev, l, name, side=None):
    n = len(ws)
    args, in_specs, out_specs, out_shape = [], [], [], []
    for w, g, m, v in zip(ws, gs, ms, vs):
        L, R, C = w.shape
        blk = pl.BlockSpec((None, R // ADAMW_STEPS, C), lambda i: (l, i, 0))
        in_specs += [blk] * 4
        args += [w, g, m, v]
        out_specs += [blk] * 3
        out_shape += [SDS((L, R, C), F32)] * 3
    aliases = None
    if prev is not None:
        aliases = {4 * n + i: i for i in range(3 * n)}
        in_specs += [ANY] * (3 * n)
        args += list(prev)

    def body(*refs):
        outs = refs[len(args):]
        for i in range(n):
            _adamw_update(*refs[4 * i:4 * i + 4], *outs[3 * i:3 * i + 3])

    return _call(body, args, grid=(ADAMW_STEPS,), in_specs=in_specs, out_specs=out_specs, out_shape=out_shape, semantics=("parallel",),
                 name=name, aliases=aliases, side=side)


def _adamw(w, g, m, v, name):
    shape = w.shape
    C = shape[-1]
    R = w.size // C
    tb = _tile(R, max(8, (1 << 18) // C), 8)
    body = functools.partial(_adamw_update)
    blk = pl.BlockSpec((tb, C), lambda i: (i, 0))
    outs = pl.pallas_call(body, grid=(R // tb,), in_specs=[blk] * 4, out_specs=[blk] * 3, out_shape=[SDS((R, C), F32)] * 3,
                          compiler_params=_params("parallel"), name=name)(*[t.reshape(R, C) for t in (w, g, m, v)])
    return [t.reshape(shape) for t in outs]


WEIGHTS = ("mix_norm_g", "w_in", "conv_dw_w", "conv_dw_b", "conv_ln_g", "conv_ln_b", "w_conv_out", "w_pool_grp", "pool_scale", "w_out",
           "xattn_norm_g", "mem_norm_g", "w_q", "w_kv", "w_o", "ffn_norm_g", "w_up", "ffn_dw_w", "w_down", "final_norm_g")
VECTORS = ("mix_norm_g", "conv_dw_b", "conv_ln_g", "conv_ln_b", "pool_scale", "xattn_norm_g", "mem_norm_g", "ffn_norm_g", "final_norm_g")


def _shard_view(t, kind):
    L, P, R, C = t.shape
    return t.reshape(L, P, 2, R // 2, C)


def _rows(t, width):
    return t.reshape(-1, width)


def _pack(parts):
    return jnp.concatenate([jnp.pad(p, ((0, (-p.shape[0]) % 8), (0, 0))) for p in parts], axis=0)


def kernel(x, mem, mix_norm_g, w_in, conv_dw_w, conv_dw_b, conv_ln_g, conv_ln_b, w_conv_out, w_pool_grp, pool_scale, w_out, xattn_norm_g, mem_norm_g, w_q, w_kv, w_o, ffn_norm_g, w_up, ffn_dw_w, w_down, final_norm_g, loss_target, m_mix_norm_g, m_w_in, m_conv_dw_w, m_conv_dw_b, m_conv_ln_g, m_conv_ln_b, m_w_conv_out, m_w_pool_grp, m_pool_scale, m_w_out, m_xattn_norm_g, m_mem_norm_g, m_w_q, m_w_kv, m_w_o, m_ffn_norm_g, m_w_up, m_ffn_dw_w, m_w_down, m_final_norm_g, v_mix_norm_g, v_w_in, v_conv_dw_w, v_conv_dw_b, v_conv_ln_g, v_conv_ln_b, v_w_conv_out, v_w_pool_grp, v_pool_scale, v_w_out, v_xattn_norm_g, v_mem_norm_g, v_w_q, v_w_kv, v_w_o, v_ffn_norm_g, v_w_up, v_ffn_dw_w, v_w_down, v_final_norm_g):
    w = dict(mix_norm_g=mix_norm_g, w_in=w_in, conv_dw_w=conv_dw_w, conv_dw_b=conv_dw_b, conv_ln_g=conv_ln_g, conv_ln_b=conv_ln_b,
             w_conv_out=w_conv_out, w_pool_grp=w_pool_grp, pool_scale=pool_scale, w_out=w_out, xattn_norm_g=xattn_norm_g,
             mem_norm_g=mem_norm_g, w_q=w_q, w_kv=w_kv, w_o=w_o, ffn_norm_g=ffn_norm_g, w_up=w_up, ffn_dw_w=ffn_dw_w, w_down=w_down,
             final_norm_g=final_norm_g)
    m = dict(zip(WEIGHTS, (m_mix_norm_g, m_w_in, m_conv_dw_w, m_conv_dw_b, m_conv_ln_g, m_conv_ln_b, m_w_conv_out, m_w_pool_grp, m_pool_scale,
                           m_w_out, m_xattn_norm_g, m_mem_norm_g, m_w_q, m_w_kv, m_w_o, m_ffn_norm_g, m_w_up, m_ffn_dw_w, m_w_down, m_final_norm_g)))
    v = dict(zip(WEIGHTS, (v_mix_norm_g, v_w_in, v_conv_dw_w, v_conv_dw_b, v_conv_ln_g, v_conv_ln_b, v_w_conv_out, v_w_pool_grp, v_pool_scale,
                           v_w_out, v_xattn_norm_g, v_mem_norm_g, v_w_q, v_w_kv, v_w_o, v_ffn_norm_g, v_w_up, v_ffn_dw_w, v_w_down, v_final_norm_g)))
    xi, yi, ci, j = _place()
    jc = jnp.stack([j, ci]).astype(jnp.int32)
    L = w_in.shape[0]
    G = len(POOL_WINDOWS)
    kinds = dict(BIG)

    def to_mat(name, t):
        if name == "w_pool":
            return jnp.swapaxes(t, 2, 3)
        return t[:, None]

    def from_mat(name, t):
        if name == "w_pool":
            return jnp.swapaxes(t, 2, 3)
        return t[:, 0]

    src = {name: w["w_pool_grp" if name == "w_pool" else name] for name, _ in BIG}

    KC, cs_c = conv_dw_w.shape[1], conv_dw_w.shape[2]
    KF, cs_f = ffn_dw_w.shape[1], ffn_dw_w.shape[2]
    taps = jnp.concatenate([conv_dw_w.reshape(L * KC, cs_c), ffn_dw_w.reshape(L * KF * (cs_f // cs_c), cs_c)], axis=0)
    n_taps = taps.shape[0]
    taps = jnp.pad(taps, ((0, (-n_taps) % 16), (0, 0)))
    names = [name for name, _ in BIG]
    mats = {name: to_mat(name, src[name]).astype(BF) for name in names}

    def layer_shards(l, subset):
        return [_shard_view(mats[name][l:l + 1], kinds[name]) for name in subset]

    def as_weight(name, f):
        return f.reshape(G if name == "w_pool" else 1, -1, f.shape[-1])

    assert L == 2
    fulls = _gather_weights(layer_shards(0, GATHER_FIRST) + [_shard_view(taps[None, None], "row")], [kinds[name] for name in GATHER_FIRST] + ["row"])
    ready = {(name, 0): as_weight(name, f) for name, f in zip(GATHER_FIRST, fulls)}
    landing = {}
    taps_all = fulls[-1].reshape(N_CHIPS, -1, cs_c)[:, :n_taps]
    V = {name: w[name] for name in VECTORS}
    V["conv_dw_w"] = taps_all[:, :L * KC].reshape(N_CHIPS, L, KC, cs_c).transpose(1, 2, 0, 3).reshape(L, KC, N_CHIPS * cs_c)
    V["ffn_dw_w"] = taps_all[:, L * KC:].reshape(N_CHIPS, L, KF, cs_f).transpose(1, 2, 0, 3).reshape(L, KF, N_CHIPS * cs_f)

    Bn, S, D = x.shape
    Mn = mem.shape[1]
    dims = (Bn, S, Mn, D, conv_dw_b.shape[1], w_down.shape[1] * N_CHIPS)
    xt = x.reshape(Bn * S, D)
    memf = mem.reshape(Bn * Mn, D)
    mem_n = _rms_fwd(memf, V["mem_norm_g"], "mem_norm")

    class LayerWeights:
        def __init__(self, l):
            self.l = l

        def __getitem__(self, name):
            return ready[(name, self.l)]

    def carried_gather(entries):
        return lambda: _side_gather([layer_shards(lw, [nm])[0] for nm, lw, _ in entries], [kinds[nm] for nm, _, _ in entries],
                                    [rels for _, _, rels in entries], [landing.get((nm, lw)) for nm, lw, _ in entries])

    def carried_pass(group):
        return lambda: _side_gather_pass([landing.pop(t) for t in group], [layer_shards(lw, [nm])[0] for nm, lw in group],
                                         [kinds[nm] for nm, _ in group])

    def on_land(l):
        def handle(key, fulls):
            if (l, key) in FWD_CARRY:
                landing.update({(nm, lw): f for (nm, lw, _), f in zip(FWD_CARRY[(l, key)], fulls)})
            else:
                ready.update({t: as_weight(t[0], f) for t, f in zip(PASS_CARRY[(l, key)], fulls)})
        return handle

    saved, W = [], []
    ht = _rms_fwd(xt, V["mix_norm_g"][0], "l0_mix_norm")
    for l in range(L):
        by_key = {key: carried_gather(entries) for (cl, key), entries in FWD_CARRY.items() if cl == l}
        by_key.update({key: carried_pass(group) for (cl, key), group in PASS_CARRY.items() if cl == l})
        sides = _Sides(by_key, on_land=on_land(l))
        W.append(LayerWeights(l))
        xt, ht, sv = _layer_fwd(xt, ht, mem_n, W[l], V, l, dims, sides, V["mix_norm_g"][l + 1] if l + 1 < L else None)
        saved.append(sv)
    loss, dx, dgf = _loss_bwd(xt, V["final_norm_g"], loss_target.reshape(Bn * S, D), "loss")
    loss = lax.psum(loss[0, 0], ("x", "y", "c"))

    late_names = [name for name in names if name not in EARLY]

    def views(gw, subset, twin):
        out = []
        for name in subset:
            g = gw[name][twin] if gw[name][twin].ndim == 3 else gw[name][twin][None]
            P, R, C = g.shape
            out.append(g.reshape(P, 2, R // 2, C) if kinds[name] == "col" else g.reshape(P, N_CHIPS, 2, R // (2 * N_CHIPS), C))
        return out

    def group_kinds(subset):
        return [kinds[name] for name in subset]

    class Reduction:
        def __init__(self, gw, subset, l, tag, first, table):
            self.gw, self.subset, self.l, self.tag, self.first, self.table = gw, subset, l, tag, first, table

        def sides(self):
            by_key = {self.first: lambda: _side_sibling_exchange(views(self.gw, self.subset, 1), group_kinds(self.subset))}
            by_key.update({key: (lambda names_=names_: _side_chip_exchange([self.pieces[nm] for nm in names_])) for key, names_ in self.table.items()})
            return by_key

        def on_land(self, key, landed):
            if key == self.first:
                self.sums(landed)
            elif key in self.table:
                got[self.l].update(zip(self.table[key], landed))

        def sums(self, lands):
            own, pieces = _chip_sums(views(self.gw, self.subset, 0), lands, group_kinds(self.subset), jc, f"chip_sums_{self.tag}_l{self.l}")
            owns[self.l].update(zip(self.subset, own))
            self.pieces = dict(zip(self.subset, pieces))

    def riding(reductions):
        return _Sides({key: side for r in reductions for key, side in r.sides().items()},
                      on_land=lambda key, landed: [r.on_land(key, landed) for r in reductions])

    dxb, dmem_n = dx, None
    smalls, owns, got = [None] * L, [{} for _ in range(L)], [{} for _ in range(L)]
    late = None
    for l in reversed(range(L)):
        dx, dxb, gw, sm = _layer_bwd_mlp(dx, dxb, saved[l], W[l], V, l, dims, riding([late] if late is not None else []))
        gw_mix = {}
        reductions = [Reduction(gw, EARLY, l, "mlp", "d_att", BWD_CARRY_EARLY)]
        if l == 0:
            reductions += [Reduction(gw_mix, names_, 0, tag, first, table) for tag, names_, first, table in BWD_LAST_LAYER]
        dx, dxb, dmem_n, sm2 = _layer_bwd_mix(dx, dxb, dmem_n, saved[l], mem_n, W[l], V, l, dims, riding(reductions), gw_mix)
        smalls[l] = {**sm, **sm2}
        late = Reduction(gw_mix, late_names, l, "mix", "d_gact", BWD_CARRY_LATE) if l > 0 else None
    last = Reduction(gw_mix, ("w_in",), 0, "in", None, {})
    last.sums(_sibling_exchange(views(gw_mix, last.subset, 1), group_kinds(last.subset), "grad_sibling_exchange_in_l0"))
    grad_x = dx.reshape(Bn, S, D)
    _, _, dgm = _rms_bwd(memf, V["mem_norm_g"], dmem_n, None, "mem_norm_b")
    small = {k: jnp.stack([sm[k] for sm in smalls]) if k in ("conv_dw_w", "ffn_dw_w") else jnp.concatenate([sm[k] for sm in smalls], axis=0)
             for k in smalls[0]}
    small["mem_norm_g"] = dgm
    small["final_norm_g"] = dgf

    small_w = conv_dw_b.shape[1]
    order = VECTORS + ("conv_dw_w", "ffn_dw_w")
    parts = [_rows(small[name], small_w) for name in order]
    counts = [p.shape[0] for p in parts]
    summed, landed_last = _reduce_small(_pack(parts), [last.pieces[name] for name in last.subset])
    got[0].update(zip(last.subset, landed_last))

    keys = ["w_pool_grp" if name == "w_pool" else name for name in names]
    rows3 = lambda t: t.reshape(t.shape[0], -1, t.shape[-1])
    wmv = [[rows3(to_mat(name, d[key])) for name, key in zip(names, keys)] for d in (w, m, v)]
    gshards, updates = None, None
    for l in reversed(range(L)):
        gshards = _final_sums([owns[l][name] for name in names], [got[l][name] for name in names], jc, gshards, l, L, f"final_sums_l{l}")
        gshards = _halves_exchange(gshards, l, f"grad_halves_exchange_l{l}")
        updates = _adamw_layer(wmv[0], [rows3(t) for t in gshards], wmv[1], wmv[2], updates, l, f"adamw_l{l}")
    grads, delta, new_m, new_v = {}, {}, {}, {}
    for i, (name, key) in enumerate(zip(names, keys)):
        Lg, P, _, RH, CS = gshards[i].shape
        grads[key] = from_mat(name, gshards[i].reshape(Lg, P, 2 * RH, CS))
        for d, t in zip((delta, new_m, new_v), updates[3 * i:3 * i + 3]):
            d[key] = from_mat(name, t.reshape(Lg, P, 2 * RH, CS))

    off = 0
    for name, cnt in zip(order, counts):
        t = summed[off:off + cnt]
        off += cnt + (-cnt) % 8
        if name in VECTORS:
            grads[name] = t.reshape(w[name].shape)
        else:
            full = t.reshape(small[name].shape)
            cs = w[name].shape[2]
            grads[name] = lax.dynamic_slice_in_dim(full, j * cs, cs, axis=2)

    vec =[_pack([_rows(d[name], small_w) for name in VECTORS]) for d in (w, grads, m, v)]
    outs = _adamw(*vec, "adamw_vectors")
    off = 0
    for name in VECTORS:
        cnt = w[name].size // small_w
        for d, t in zip((delta, new_m, new_v), outs):
            d[name] = t[off:off + cnt].reshape(w[name].shape)
        off += cnt + (-cnt) % 8
    for name in ("conv_dw_w", "ffn_dw_w"):
        delta[name], new_m[name], new_v[name] = _adamw(w[name], grads[name], m[name], v[name], "adamw_" + name)

    return (loss, grad_x, *[grads[k] for k in WEIGHTS], *[delta[k] for k in WEIGHTS], *[new_m[k] for k in WEIGHTS], *[new_v[k] for k in WEIGHTS])
```

```python
import functools
import math

import jax
import jax.numpy as jnp
from jax import lax
from jax.experimental import pallas as pl
from jax.experimental.pallas import tpu as pltpu

F32 = jnp.float32
BF = jnp.bfloat16
SDS = jax.ShapeDtypeStruct
MESH = pl.DeviceIdType.MESH
ANY = pl.BlockSpec(memory_space=pl.ANY)

EPS = 1e-6
XA_HEADS = 4
POOL_WINDOWS = (2, 4, 8, 16)
N_CHIPS = 4
ADAM_LR, ADAM_B1, ADAM_B2, ADAM_EPS, ADAM_WD, ADAM_STEP = 0.001, 0.9, 0.999, 1e-08, 0.01, 10

LANES = 128
ROW_BLOCK = 512
VMEM_LIMIT = 56 * 1024 * 1024


def _params(*sem):
    return pltpu.CompilerParams(dimension_semantics=sem if sem else None, vmem_limit_bytes=VMEM_LIMIT)


def _tile(n, cap, mult=LANES):
    if n <= cap:
        return n
    for t in range(cap - cap % mult, 0, -mult):
        if n % t == 0:
            return t
    return n


_DN = {"nn": (((1,), (0,)), ((), ())), "nt": (((1,), (1,)), ((), ())), "tn": (((0,), (0,)), ((), ()))}


class _Side:
    def __init__(self, ins, outs, n, make, n_alias=0):
        self.ins, self.outs, self.n, self.make, self.n_alias = list(ins), list(outs), n, make, n_alias


def _call(body, args, *, grid, in_specs, out_specs, out_shape, semantics, name, scratch_shapes=(), side=None, prefetch=(), aliases=None):
    n_pf = len(prefetch)
    aliases = {n_pf + i: o for i, o in (aliases or {}).items()}
    n_in, n_out, n_scr = len(args), len(out_shape), len(scratch_shapes)
    n_si, n_so = (len(side.ins), len(side.outs)) if side is not None else (0, 0)
    if side is not None:
        aliases.update({n_pf + n_in + n_si - side.n_alias + i: n_out + i for i in range(side.n_alias)})

    def carrying(*refs):
        pf, refs = refs[:n_pf], refs[n_pf:]
        ins, s_in = refs[:n_in], refs[n_in:n_in + n_si]
        outs, s_out = refs[n_in + n_si:n_in + n_si + n_out], refs[n_in + n_si + n_out:n_in + n_si + n_out + n_so]
        scr = refs[n_in + n_si + n_out + n_so:]
        if side is None:
            return body(*pf, *ins, *outs, *scr)
        copies = side.make(s_in, s_out, scr[n_scr], scr[n_scr + 1])
        ids = [pl.program_id(d) for d in range(len(grid))]
        first, last = ids[0] == 0, ids[0] == grid[0] - 1
        for d in range(1, len(grid)):
            first, last = first & (ids[d] == 0), last & (ids[d] == grid[d] - 1)

        @pl.when(first)
        def _():
            for cp in copies:
                cp.start()

        body(*pf, *ins, *outs, *scr[:n_scr])

        @pl.when(last)
        def _():
            for cp in copies:
                cp.wait()

    sems = [pltpu.SemaphoreType.DMA((side.n,)), pltpu.SemaphoreType.DMA((side.n,))] if side is not None else []
    outs = pl.pallas_call(
        carrying, grid_spec=pltpu.PrefetchScalarGridSpec(
            num_scalar_prefetch=n_pf, grid=grid, in_specs=list(in_specs) + [ANY] * n_si, out_specs=list(out_specs) + [ANY] * n_so,
            scratch_shapes=list(scratch_shapes) + sems),
        out_shape=list(out_shape) + (side.outs if side is not None else []), input_output_aliases=aliases,
        compiler_params=_params(*(semantics if side is None else ["arbitrary"] * len(grid))), name=name)(
            *prefetch, *args, *(side.ins if side is not None else []))
    return list(outs) if side is None else (list(outs[:n_out]), list(outs[n_out:]))


def _call1(body, args, *, out_spec, out_shape, side=None, **kw):
    got = _call(body, args, out_specs=[out_spec], out_shape=[out_shape], side=side, **kw)
    return got[0] if side is None else (got[0][0], got[1])


MM_VMEM_BUDGET = 40 * 1024 * 1024
MM_STEP_MACS = 2200 * 1024 * 1024
MXU_WIDTH = 256
MM_STEP_COST_BYTES = 1 << 20


def _divisors(n):
    return [t for t in range(LANES, n + 1, LANES) if n % t == 0] or [n]


def _mm_tiles(M, N, K, a_bytes, b_bytes, o_bytes, n_unit=None):
    best = None
    for tk in _divisors(K):
        for tm in _divisors(M):
            for tn in _divisors(N if n_unit is None else n_unit):
                nk = K // tk
                foot = 2 * (tm * tk * a_bytes + tk * tn * b_bytes + tm * tn * o_bytes) + (tm * tn * 4 if nk > 1 else 0)
                if (foot > MM_VMEM_BUDGET or tm * tn * tk > MM_STEP_MACS or tn < min(N if n_unit is None else n_unit, MXU_WIDTH)
                        or tm < min(M, MXU_WIDTH)):
                    continue
                steps = (M // tm) * (N // tn) * nk
                traffic = M * K * a_bytes * (N // tn if nk > 1 else 1) + K * N * b_bytes * (M // tm) + M * N * o_bytes
                exposed = tm * tk * a_bytes + tk * tn * b_bytes + tm * tn * o_bytes
                cost = traffic + exposed + steps * MM_STEP_COST_BYTES + (nk - 1) * M * N * 8
                if best is None or cost < best[0]:
                    best = (cost, tm, tn, tk)
    assert best is not None, (M, N, K)
    return best[1:]


def _mm(a, b, dims, out_dtype, name, res=None, bl=None, side=None, twin=None, b_halves=False, part=None):
    bs = b.shape[1:] if bl is not None or b_halves else b.shape
    if dims == "nn":
        (M, K), (K2, N) = a.shape, bs
    elif dims == "nt":
        (M, K), (N, K2) = a.shape, bs
    else:
        (K, M), (K2, N) = a.shape, bs
    assert K == K2, (name, a.shape, b.shape)
    n_half = N
    if b_halves:
        assert dims == "tn" and bl is None
        N = 2 * n_half
    n_total, n_first, earlier = part if part is not None else (N, 0, None)
    tm, tn, tk = _mm_tiles(M, N, K, a.dtype.itemsize, b.dtype.itemsize, jnp.dtype(out_dtype).itemsize
                           + (res.dtype.itemsize if res is not None else 0) + (jnp.dtype(twin).itemsize if twin is not None else 0),
                           n_unit=math.gcd(n_half, n_first) if b_halves or n_first else None)
    nk = K // tk
    lead = (None,) if bl is not None or b_halves else ()
    pre = (lambda *ix: (bl,) + ix) if bl is not None else (lambda *ix: ix)
    if b_halves:
        per_half = n_half // tn
        pre = lambda k, j: (j // per_half, k, j % per_half)
    if dims == "tn":
        a_spec = pl.BlockSpec((tk, tm), lambda i, j, k: (k, i))
    else:
        a_spec = pl.BlockSpec((tm, tk), lambda i, j, k: (i, k))
    if dims == "nt":
        b_spec = pl.BlockSpec(lead + (tn, tk), lambda i, j, k: pre(j, k))
    else:
        b_spec = pl.BlockSpec(lead + (tk, tn), lambda i, j, k: pre(k, j))
    assert n_first % tn == 0 and (part is None or res is None)
    o_spec = pl.BlockSpec((tm, tn), lambda i, j, k: (i, n_first // tn + j))
    in_specs, args = [a_spec, b_spec], [a, b]
    if res is not None:
        in_specs.append(o_spec)
        args.append(res)
    n_main = len(args)
    n_out = 1 if twin is None else 2
    aliases = None
    if earlier is not None:
        earlier = list(earlier) if twin is not None else [earlier]
        aliases = {n_main + t: t for t in range(n_out)}
        in_specs += [ANY] * n_out
        args += earlier

    def body(*refs):
        refs = refs[:n_main] + refs[len(args):]
        a_ref, b_ref = refs[0], refs[1]
        r_ref = refs[2] if res is not None else None
        o_ref = refs[n_main]
        p = lax.dot_general(a_ref[...].astype(BF), b_ref[...].astype(BF), _DN[dims], preferred_element_type=F32)

        def finish(t):
            if r_ref is not None:
                t = t + r_ref[...]
            o_ref[...] = t.astype(out_dtype)
            if twin is not None:
                refs[n_main + 1][...] = t.astype(twin)

        if nk == 1:
            finish(p)
        else:
            acc = refs[n_main + n_out]
            k = pl.program_id(2)

            @pl.when(k == 0)
            def _():
                acc[...] = p

            @pl.when(k > 0)
            def _():
                acc[...] += p

            @pl.when(k == nk - 1)
            def _():
                finish(acc[...])

    got = _call(body, args, grid=(M // tm, N // tn, nk), in_specs=in_specs, out_specs=[o_spec] * n_out,
                out_shape=[SDS((M, n_total), out_dtype)] + ([SDS((M, n_total), twin)] if twin is not None else []),
                scratch_shapes=[pltpu.VMEM((tm, tn), F32)] if nk > 1 else [], semantics=("parallel", "parallel", "arbitrary"),
                name=name, side=side, aliases=aliases)
    outs, landed = (got, None) if side is None else got
    out = outs[0] if twin is None else (outs[0], outs[1])
    return out if side is None else (out, landed)


def _rms(x, g):
    return x * lax.rsqrt(jnp.mean(x * x, axis=-1, keepdims=True) + EPS) * g


def _ln_silu(x, g, b):
    mu = jnp.mean(x, axis=-1, keepdims=True)
    xc = x - mu
    var = jnp.mean(xc * xc, axis=-1, keepdims=True)
    return jax.nn.silu(xc * lax.rsqrt(var + EPS) * g + b)


def _merge(gc, gp, yc, yp, ps):
    return jax.nn.sigmoid(gc) * yc + jax.nn.sigmoid(gp) * (yp * ps)


def _gated(gate, val):
    return jax.nn.gelu(gate) * val


def _rms_fwd(x, g, name):
    T, D = x.shape
    tb = _tile(T, ROW_BLOCK, 8)

    def body(x_ref, g_ref, o_ref):
        o_ref[...] = _rms(x_ref[...], g_ref[...]).astype(BF)

    row = pl.BlockSpec((tb, D), lambda i: (i, 0))
    return pl.pallas_call(body, grid=(T // tb,), in_specs=[row, pl.BlockSpec((1, D), lambda i: (0, 0))], out_specs=row,
                          out_shape=SDS((T, D), BF), compiler_params=_params("parallel"), name=name)(x, g.reshape(1, D))


def _rms_bwd(x, g, dh, dres, name):
    T, D = x.shape
    tb = _tile(T, ROW_BLOCK, 8)

    def body(*refs):
        if dres is not None:
            x_ref, g_ref, dh_ref, dres_ref, dx_ref, dxb_ref, dg_ref = refs
        else:
            x_ref, g_ref, dh_ref, dx_ref, dxb_ref, dg_ref = refs
        _, vjp = jax.vjp(_rms, x_ref[...], g_ref[...])
        dx, dg = vjp(dh_ref[...].astype(F32))
        if dres is not None:
            dx = dx + dres_ref[...]
        dx_ref[...] = dx
        dxb_ref[...] = dx.astype(BF)

        @pl.when(pl.program_id(0) == 0)
        def _():
            dg_ref[...] = jnp.zeros_like(dg_ref)

        dg_ref[...] += dg

    row = pl.BlockSpec((tb, D), lambda i: (i, 0))
    vec = pl.BlockSpec((1, D), lambda i: (0, 0))
    ins = [x, g.reshape(1, D), dh] + ([dres] if dres is not None else [])
    return pl.pallas_call(
        body, grid=(T // tb,), in_specs=[row, vec, row] + ([row] if dres is not None else []), out_specs=[row, row, vec],
        out_shape=[SDS((T, D), F32), SDS((T, D), BF), SDS((1, D), F32)], compiler_params=_params("arbitrary"), name=name)(*ins)


def _row_tile(M, K, N, per_row_bytes):
    fixed = K * N * 2
    fit = [t for t in _divisors(M) if fixed + 2 * t * per_row_bytes <= MM_VMEM_BUDGET and t * K * N <= 2 * MM_STEP_MACS]
    return max(fit) if fit else min(_divisors(M))


def _mm_rms_fwd(a, b, res, g, name, side=None):
    M, K = a.shape
    N = b.shape[2]
    tm = _row_tile(M, K, N, K * 2 + N * (4 + 4 + 2))

    def body(a_ref, b_ref, r_ref, g_ref, x_ref, h_ref):
        x = r_ref[...] + lax.dot_general(a_ref[...], b_ref[...], _DN["nn"], preferred_element_type=F32)
        x_ref[...] = x
        h_ref[...] = _rms(x, g_ref[...]).astype(BF)

    row = pl.BlockSpec((tm, N), lambda i: (i, 0))
    return _call(body, (a, b, res, g.reshape(1, N)), grid=(M // tm,),
                 in_specs=[pl.BlockSpec((tm, K), lambda i: (i, 0)), pl.BlockSpec((None, K, N), lambda i: (0, 0, 0), pipeline_mode=pl.Buffered(1)), row,
                           pl.BlockSpec((1, N), lambda i: (0, 0))],
                 out_specs=[row, row], out_shape=[SDS((M, N), F32), SDS((M, N), BF)], semantics=("parallel",), name=name, side=side)


def _mm_rms_bwd(a_parts, b, x, g, dres, name, side=None):
    n_a = len(a_parts)
    M = a_parts[0].shape[-2]
    N, K = b.shape[1:]
    assert K == sum(p.shape[-1] * (p.shape[0] if p.ndim == 3 else 1) for p in a_parts)
    tm = _row_tile(M, K, N, K * 2 + N * (4 + 4 + 4 + 2))

    def body(*refs):
        a_refs, (b_ref, x_ref, g_ref, r_ref, dx_ref, dxb_ref, dg_ref) = refs[:n_a], refs[n_a:]
        dh, col = None, 0
        for p, a_ref in zip(a_parts, a_refs):
            for blk in ([a_ref[h] for h in range(p.shape[0])] if p.ndim == 3 else [a_ref[...]]):
                t = lax.dot_general(blk, b_ref[:, col:col + p.shape[-1]], _DN["nt"], preferred_element_type=F32)
                dh = t if dh is None else dh + t
                col += p.shape[-1]
        _, vjp = jax.vjp(_rms, x_ref[...], g_ref[...])
        dx, dg = vjp(dh)
        dx = dx + r_ref[...]
        dx_ref[...] = dx
        dxb_ref[...] = dx.astype(BF)

        @pl.when(pl.program_id(0) == 0)
        def _():
            dg_ref[...] = jnp.zeros_like(dg_ref)

        dg_ref[...] += dg

    row = pl.BlockSpec((tm, N), lambda i: (i, 0))
    vec = pl.BlockSpec((1, N), lambda i: (0, 0))
    a_specs = [pl.BlockSpec((p.shape[0], tm, p.shape[2]), lambda i: (0, i, 0)) if p.ndim == 3 else pl.BlockSpec((tm, p.shape[1]), lambda i: (i, 0))
               for p in a_parts]
    return _call(
        body, (*a_parts, b, x, g.reshape(1, N), dres), grid=(M // tm,),
        in_specs=a_specs + [pl.BlockSpec((None, N, K), lambda i: (0, 0, 0), pipeline_mode=pl.Buffered(1)), row, vec, row],
        out_specs=[row, row, vec], out_shape=[SDS((M, N), F32), SDS((M, N), BF), SDS((1, N), F32)],
        semantics=("arbitrary",), name=name, side=side)


def _loss_bwd(x, g, target, name):
    T, D = x.shape
    tb = _tile(T, ROW_BLOCK, 8)
    nb = T // tb

    def body(x_ref, g_ref, t_ref, loss_ref, dx_ref, dg_ref, acc):
        i = pl.program_id(0)
        y, vjp = jax.vjp(_rms, x_ref[...], g_ref[...])
        err = y - t_ref[...]
        dx, dg = vjp(err * (1.0 / D))
        dx_ref[...] = dx

        @pl.when(i == 0)
        def _():
            dg_ref[...] = jnp.zeros_like(dg_ref)
            acc[...] = jnp.zeros_like(acc)

        dg_ref[...] += dg
        acc[...] += jnp.sum(err * err, axis=0, keepdims=True)

        @pl.when(i == nb - 1)
        def _():
            loss_ref[...] = jnp.full(loss_ref.shape, (0.5 / D) * jnp.sum(acc[...]), F32)

    row = pl.BlockSpec((tb, D), lambda i: (i, 0))
    vec = pl.BlockSpec((1, D), lambda i: (0, 0))
    return pl.pallas_call(
        body, grid=(nb,), in_specs=[row, vec, row], out_specs=[pl.BlockSpec((1, LANES), lambda i: (0, 0)), row, vec],
        out_shape=[SDS((1, LANES), F32), SDS((T, D), F32), SDS((1, D), F32)], scratch_shapes=[pltpu.VMEM((1, D), F32)],
        compiler_params=_params("arbitrary"), name=name)(x, g.reshape(1, D), target)


def _ln_silu_mm(cv, g, b, w, name):
    T, C = cv.shape
    D = w.shape[2]
    tb = _tile(T, 2 * ROW_BLOCK, 8)

    def body(x_ref, g_ref, b_ref, w_ref, y1_ref, y_ref):
        y1 = _ln_silu(x_ref[...], g_ref[...], b_ref[...]).astype(BF)
        y1_ref[...] = y1
        y_ref[...] = lax.dot_general(y1, w_ref[...], _DN["nn"], preferred_element_type=F32).astype(BF)

    row = pl.BlockSpec((tb, C), lambda i: (i, 0))
    vec = pl.BlockSpec((1, C), lambda i: (0, 0))
    return pl.pallas_call(
        body, grid=(T // tb,), in_specs=[row, vec, vec, pl.BlockSpec((None, C, D), lambda i: (0, 0, 0), pipeline_mode=pl.Buffered(1))],
        out_specs=[row, pl.BlockSpec((tb, D), lambda i: (i, 0))], out_shape=[SDS((T, C), BF), SDS((T, D), BF)],
        compiler_params=_params("parallel"), name=name)(cv, g.reshape(1, C), b.reshape(1, C), w)


def _mm_ln_silu_bwd(dyc, w, cv, g, b, name, side=None):
    T, C = cv.shape
    D = w.shape[2]
    tb = _tile(T, 2 * ROW_BLOCK, 8)

    def body(d_ref, w_ref, x_ref, g_ref, b_ref, dx_ref, dg_ref, db_ref):
        dy1 = lax.dot_general(d_ref[...], w_ref[...], _DN["nt"], preferred_element_type=F32)
        _, vjp = jax.vjp(_ln_silu, x_ref[...], g_ref[...], b_ref[...])
        dx, dg, db = vjp(dy1)
        dx_ref[...] = dx

        @pl.when(pl.program_id(0) == 0)
        def _():
            dg_ref[...] = jnp.zeros_like(dg_ref)
            db_ref[...] = jnp.zeros_like(db_ref)

        dg_ref[...] += dg
        db_ref[...] += db

    row = pl.BlockSpec((tb, C), lambda i: (i, 0))
    vec = pl.BlockSpec((1, C), lambda i: (0, 0))
    return _call(
        body, (dyc, w, cv, g.reshape(1, C), b.reshape(1, C)), grid=(T // tb,),
        in_specs=[pl.BlockSpec((tb, D), lambda i: (i, 0)), pl.BlockSpec((None, C, D), lambda i: (0, 0, 0), pipeline_mode=pl.Buffered(1)), row, vec, vec],
        out_specs=[row, vec, vec], out_shape=[SDS((T, C), F32), SDS((1, C), F32), SDS((1, C), F32)], semantics=("arbitrary",),
        name=name, side=side)


def _merge_fwd(proj, yc, yp, ps, C, name, side=None):
    T, D = yc.shape
    tb = _tile(T, ROW_BLOCK, 8)
    nj = D // C

    def body(gc_ref, gp_ref, yc_ref, yp_ref, ps_ref, o_ref):
        o_ref[...] = _merge(gc_ref[...], gp_ref[...], yc_ref[...].astype(F32), yp_ref[...].astype(F32), ps_ref[...]).astype(BF)

    blk = pl.BlockSpec((tb, C), lambda i, j: (i, j))
    return _call1(
        body, (proj, proj, yc, yp, ps.reshape(1, D)), grid=(T // tb, nj),
        in_specs=[pl.BlockSpec((tb, C), lambda i, j: (i, 3 + j)), pl.BlockSpec((tb, C), lambda i, j: (i, 3 + nj + j)), blk, blk,
                  pl.BlockSpec((1, C), lambda i, j: (0, j))],
        out_spec=blk, out_shape=SDS((T, D), BF), semantics=("parallel", "parallel"), name=name, side=side)


def _merge_bwd(proj, yc, yp, ps, dm, C, name, side=None):
    T, D = yc.shape
    tb = _tile(T, ROW_BLOCK, 8)
    nj = D // C

    def body(gc_ref, gp_ref, yc_ref, yp_ref, ps_ref, dm_ref, dg_ref, dyc_ref, dyp_ref, dps_ref):
        _, vjp = jax.vjp(_merge, gc_ref[...], gp_ref[...], yc_ref[...].astype(F32), yp_ref[...].astype(F32), ps_ref[...])
        dgc, dgp, dyc, dyp, dps = vjp(dm_ref[...].astype(F32))
        dg_ref[0] = dgc.astype(BF)
        dg_ref[1] = dgp.astype(BF)
        dyc_ref[...] = dyc.astype(BF)
        dyp_ref[...] = dyp.astype(BF)

        @pl.when(pl.program_id(1) == 0)
        def _():
            dps_ref[...] = jnp.zeros_like(dps_ref)

        dps_ref[...] += dps

    blk = pl.BlockSpec((tb, C), lambda j, i: (i, j))
    vec = pl.BlockSpec((1, C), lambda j, i: (0, j))
    return _call(
        body, (proj, proj, yc, yp, ps.reshape(1, D), dm), grid=(nj, T // tb),
        in_specs=[pl.BlockSpec((tb, C), lambda j, i: (i, 3 + j)), pl.BlockSpec((tb, C), lambda j, i: (i, 3 + nj + j)), blk, blk, vec, blk],
        out_specs=[pl.BlockSpec((2, tb, C), lambda j, i: (0, i, j)), blk, blk, vec],
        out_shape=[SDS((2, T, D), BF), SDS((T, D), BF), SDS((T, D), BF), SDS((1, D), F32)],
        semantics=("parallel", "arbitrary"), name=name, side=side)


def _shd(v, s, rows):
    if s == 0:
        return v
    return jnp.where(rows >= s, pltpu.roll(v, s, 0), 0.0)


def _shu(v, s, rows):
    if s == 0:
        return v
    n = v.shape[0]
    return jnp.where(rows < n - s, pltpu.roll(v, n - s, 0), 0.0)


def _glu_conv_fwd(proj, w, b, Bn, S, C, name, side=None):
    K = w.shape[0]
    sl = min(LANES, C)
    ns = C // sl

    def body(a_ref, gl_ref, w_ref, b_ref, o_ref):
        y0 = a_ref[...] * jax.nn.sigmoid(gl_ref[...])
        rows = lax.broadcasted_iota(jnp.int32, y0.shape, 0)
        acc = jnp.zeros_like(y0) + b_ref[...]
        for k in range(K):
            acc = acc + w_ref[k:k + 1, :] * _shd(y0, K - 1 - k, rows)
        o_ref[...] = acc

    return _call1(
        body, (proj, proj, w, b.reshape(1, C)), grid=(Bn, ns),
        in_specs=[pl.BlockSpec((S, sl), lambda bi, j: (bi, j)), pl.BlockSpec((S, sl), lambda bi, j: (bi, ns + j)),
                  pl.BlockSpec((K, sl), lambda bi, j: (0, j)), pl.BlockSpec((1, sl), lambda bi, j: (0, j))],
        out_spec=pl.BlockSpec((S, sl), lambda bi, j: (bi, j)), out_shape=SDS((Bn * S, C), F32),
        semantics=("parallel", "parallel"), name=name, side=side)


def _glu_conv_bwd(proj, w, dcv, Bn, S, C, name, side=None):
    K = w.shape[0]
    sl = min(LANES, C)
    ns = C // sl

    def body(a_ref, gl_ref, w_ref, d_ref, dagl_ref, dw_ref, db_ref):
        a = a_ref[...]
        sg = jax.nn.sigmoid(gl_ref[...])
        y0 = a * sg
        d = d_ref[...]
        rows = lax.broadcasted_iota(jnp.int32, y0.shape, 0)

        @pl.when(pl.program_id(1) == 0)
        def _():
            dw_ref[...] = jnp.zeros_like(dw_ref)
            db_ref[...] = jnp.zeros_like(db_ref)

        dy0 = jnp.zeros_like(y0)
        for k in range(K):
            s = K - 1 - k
            dw_ref[k:k + 1, :] += jnp.sum(d * _shd(y0, s, rows), axis=0, keepdims=True)
            dy0 = dy0 + w_ref[k:k + 1, :] * _shu(d, s, rows)
        db_ref[...] += jnp.sum(d, axis=0, keepdims=True)
        dagl_ref[0] = (dy0 * sg).astype(BF)
        dagl_ref[1] = (dy0 * a * sg * (1.0 - sg)).astype(BF)

    blk = pl.BlockSpec((S, sl), lambda j, bi: (bi, j))
    return _call(
        body, (proj, proj, w, dcv), grid=(ns, Bn),
        in_specs=[blk, pl.BlockSpec((S, sl), lambda j, bi: (bi, ns + j)), pl.BlockSpec((K, sl), lambda j, bi: (0, j)), blk],
        out_specs=[pl.BlockSpec((2, S, sl), lambda j, bi: (0, bi, j)), pl.BlockSpec((K, sl), lambda j, bi: (0, j)),
                   pl.BlockSpec((1, sl), lambda j, bi: (0, j))],
        out_shape=[SDS((2, Bn * S, C), BF), SDS((K, C), F32), SDS((1, C), F32)],
        semantics=("parallel", "arbitrary"), name=name, side=side)


def _pool_z(u, g, rows):
    s2 = u + _shd(u, 1, rows)
    s4 = s2 + _shd(s2, 2, rows)
    s8 = s4 + _shd(s4, 4, rows)
    s16 = s8 + _shd(s8, 8, rows)
    sw = jnp.where(g == 0, s2, jnp.where(g == 1, s4, jnp.where(g == 2, s8, s16)))
    cnt = jnp.minimum(rows + 1, POOL_WINDOWS[0] << g).astype(F32)
    return sw / cnt - u, cnt


def _pool_fwd(proj, wpt, l, Bn, S, C, D, name):
    G = len(POOL_WINDOWS)
    gd, go = C // G, D // G

    def body(u_ref, w_ref, o_ref):
        g = pl.program_id(1)
        u = u_ref[...]
        rows = lax.broadcasted_iota(jnp.int32, u.shape, 0)
        zp, _ = _pool_z(u, g, rows)
        o_ref[...] = lax.dot_general(zp.astype(BF), w_ref[...], _DN["nt"], preferred_element_type=F32).astype(BF)

    return pl.pallas_call(
        body, grid=(Bn, G),
        in_specs=[pl.BlockSpec((S, gd), lambda bi, g: (bi, 2 * G + g)), pl.BlockSpec((None, go, gd), lambda bi, g: (l * G + g, 0, 0))],
        out_specs=pl.BlockSpec((S, go), lambda bi, g: (bi, g)), out_shape=SDS((Bn * S, D), BF),
        compiler_params=_params("parallel", "parallel"), name=name)(proj, wpt)


def _pool_bwd(proj, wpt, dyp, l, Bn, S, C, D, name):
    G = len(POOL_WINDOWS)
    gd, go = C // G, D // G

    def body(u_ref, w_ref, d_ref, du_ref, dw_ref):
        g = pl.program_id(0)
        u = u_ref[...]
        rows = lax.broadcasted_iota(jnp.int32, u.shape, 0)
        zp, cnt = _pool_z(u, g, rows)
        d = d_ref[...]
        dzp = lax.dot_general(d, w_ref[...], _DN["nn"], preferred_element_type=F32)

        @pl.when(pl.program_id(1) == 0)
        def _():
            dw_ref[...] = jnp.zeros_like(dw_ref)

        dw_ref[...] += lax.dot_general(d, zp.astype(BF), _DN["tn"], preferred_element_type=F32)
        dsw = dzp / cnt
        zero = jnp.zeros_like(dsw)
        d16 = jnp.where(g == 3, dsw, zero)
        d8 = jnp.where(g == 2, dsw, zero) + d16 + _shu(d16, 8, rows)
        d4 = jnp.where(g == 1, dsw, zero) + d8 + _shu(d8, 4, rows)
        d2 = jnp.where(g == 0, dsw, zero) + d4 + _shu(d4, 2, rows)
        d1 = d2 + _shu(d2, 1, rows)
        du_ref[...] = (d1 - dzp).astype(BF)

    return pl.pallas_call(
        body, grid=(G, Bn),
        in_specs=[pl.BlockSpec((S, gd), lambda g, bi: (bi, 2 * G + g)), pl.BlockSpec((None, go, gd), lambda g, bi: (l * G + g, 0, 0)),
                  pl.BlockSpec((S, go), lambda g, bi: (bi, g))],
        out_specs=[pl.BlockSpec((S, gd), lambda g, bi: (bi, g)), pl.BlockSpec((None, go, gd), lambda g, bi: (g, 0, 0))],
        out_shape=[SDS((Bn * S, C), BF), SDS((G, go, gd), F32)],
        compiler_params=_params("parallel", "arbitrary"), name=name)(proj, wpt, dyp)


def _ffn_conv(u, w_ref, rows):
    K = w_ref.shape[0]
    acc = w_ref[K - 1:K, :] * u
    for k in range(K - 1):
        acc = acc + w_ref[k:k + 1, :] * _shd(u, K - 1 - k, rows)
    return acc


def _ffn_cb(F):
    return _tile(F, 256)


def _ffn_act_fwd(up0, w, Bn, S, F, name, side=None):
    cb = _ffn_cb(F)
    nj = F // cb

    def body(g_ref, v_ref, wg_ref, wv_ref, o_ref):
        rows = lax.broadcasted_iota(jnp.int32, g_ref.shape, 0)
        o_ref[...] = _gated(_ffn_conv(g_ref[...], wg_ref, rows), _ffn_conv(v_ref[...], wv_ref, rows)).astype(BF)

    K = w.shape[0]
    return _call1(
        body, (up0, up0, w, w), grid=(Bn, nj),
        in_specs=[pl.BlockSpec((S, cb), lambda bi, j: (bi, j)), pl.BlockSpec((S, cb), lambda bi, j: (bi, nj + j)),
                  pl.BlockSpec((K, cb), lambda bi, j: (0, j)), pl.BlockSpec((K, cb), lambda bi, j: (0, nj + j))],
        out_spec=pl.BlockSpec((S, cb), lambda bi, j: (bi, j)), out_shape=SDS((Bn * S, F), BF),
        semantics=("parallel", "parallel"), name=name, side=side)


SUBLANES = 8
FFN_HALO = SUBLANES
FFN_ROWS = 128
GELU_C0, GELU_C1 = 0.7978845608028654, 0.044715


def _gelu_and_grad(x):
    x2 = x * x
    t = jnp.tanh(GELU_C0 * (x + GELU_C1 * (x2 * x)))
    cdf = 0.5 * (1.0 + t)
    return x * cdf, cdf + (0.5 * GELU_C0) * x * (1.0 - t * t) * (1.0 + (3.0 * GELU_C1) * x2)


def _ffn_act_bwd(up0, w, dg, Bn, S, F, name, side=None):
    cb = min(LANES, F)
    nj = F // cb
    K = w.shape[0]
    rc = FFN_ROWS if S % FFN_ROWS == 0 else S
    win = rc + 2 * FFN_HALO
    assert K - 1 <= FFN_HALO and rc % SUBLANES == 0

    def body(g_ref, v_ref, wg_ref, wv_ref, d_ref, do_ref, dwg_ref, dwv_ref, gp, vp, dp):
        for pad, src in ((gp, g_ref), (vp, v_ref), (dp, d_ref)):
            pad[0:FFN_HALO, :] = jnp.zeros((FFN_HALO, cb), F32)
            pad[FFN_HALO + S:, :] = jnp.zeros((FFN_HALO, cb), F32)
            pad[FFN_HALO:FFN_HALO + S, :] = src[...].astype(F32)
        wg = [wg_ref[k:k + 1, :] for k in range(K)]
        wv = [wv_ref[k:k + 1, :] for k in range(K)]

        def taps(u):
            return [pltpu.roll(u, K - 1 - k, 0) for k in range(K - 1)] + [u]

        def conv(us, ws):
            acc = ws[K - 1] * us[K - 1]
            for k in range(K - 1):
                acc = acc + ws[k] * us[k]
            return acc

        def conv_t(dc, ws):
            acc = ws[K - 1] * dc
            for k in range(K - 1):
                acc = acc + ws[k] * pltpu.roll(dc, win - (K - 1 - k), 0)
            return acc

        def fold(t):
            acc = t[FFN_HALO:FFN_HALO + SUBLANES]
            for i in range(1, rc // SUBLANES):
                acc = acc + t[FFN_HALO + SUBLANES * i:FFN_HALO + SUBLANES * (i + 1)]
            return acc

        def chunk(c, sums):
            r0 = pl.multiple_of(c * rc, SUBLANES)
            gs, vs, d = taps(gp[pl.ds(r0, win), :]), taps(vp[pl.ds(r0, win), :]), dp[pl.ds(r0, win), :]
            ge, dge = _gelu_and_grad(conv(gs, wg))
            dgc = d * conv(vs, wv) * dge
            dvc = d * ge
            do_ref[0, pl.ds(r0, rc), :] = conv_t(dgc, wg)[FFN_HALO:FFN_HALO + rc].astype(BF)
            do_ref[1, pl.ds(r0, rc), :] = conv_t(dvc, wv)[FFN_HALO:FFN_HALO + rc].astype(BF)
            new = [fold(dc * u) for us, dc in ((gs, dgc), (vs, dvc)) for u in us]
            return tuple(a + b for a, b in zip(sums, new))

        sums = lax.fori_loop(0, S // rc, chunk, tuple(jnp.zeros((SUBLANES, cb), F32) for _ in range(2 * K)))

        @pl.when(pl.program_id(1) == 0)
        def _():
            dwg_ref[...] = jnp.zeros_like(dwg_ref)
            dwv_ref[...] = jnp.zeros_like(dwv_ref)

        for k in range(K):
            dwg_ref[k:k + 1, :] += jnp.sum(sums[k], axis=0, keepdims=True)
            dwv_ref[k:k + 1, :] += jnp.sum(sums[K + k], axis=0, keepdims=True)

    blk = pl.BlockSpec((S, cb), lambda j, bi: (bi, j))
    wblk = pl.BlockSpec((K, cb), lambda j, bi: (0, j))
    return _call(
        body, (up0, up0, w, w, dg), grid=(nj, Bn),
        in_specs=[blk, pl.BlockSpec((S, cb), lambda j, bi: (bi, nj + j)), wblk, pl.BlockSpec((K, cb), lambda j, bi: (0, nj + j)), blk],
        out_specs=[pl.BlockSpec((2, S, cb), lambda j, bi: (0, bi, j)), wblk, wblk],
        out_shape=[SDS((2, Bn * S, F), BF), SDS((K, F), F32), SDS((K, F), F32)],
        scratch_shapes=[pltpu.VMEM((S + 2 * FFN_HALO, cb), F32)] * 3, semantics=("parallel", "arbitrary"), name=name, side=side)


def _softmax_rows(q, k, scale):
    sc = lax.dot_general(q, k, _DN["nt"], preferred_element_type=F32) * scale
    e = jnp.exp(sc - jnp.max(sc, axis=-1, keepdims=True))
    return e / jnp.sum(e, axis=-1, keepdims=True)


def _attn_ts(S):
    return _tile(S, 1024, 8)


def _attn_fwd(q, kv, Bn, S, Mn, D, name, side=None):
    H = XA_HEADS
    dh = D // H
    ts = _attn_ts(S)
    nsb = S // ts
    scale = dh ** -0.5

    def body(q_ref, k_ref, v_ref, o_ref):
        p = _softmax_rows(q_ref[...], k_ref[...], scale)
        o_ref[...] = lax.dot_general(p.astype(BF), v_ref[...], _DN["nn"], preferred_element_type=F32).astype(BF)

    qblk = pl.BlockSpec((ts, dh), lambda bi, h, s: (bi * nsb + s, h))
    return _call1(
        body, (q, kv, kv), grid=(Bn, H, nsb),
        in_specs=[qblk, pl.BlockSpec((Mn, dh), lambda bi, h, s: (bi, h)), pl.BlockSpec((Mn, dh), lambda bi, h, s: (bi, H + h))],
        out_spec=qblk, out_shape=SDS((Bn * S, D), BF), semantics=("parallel", "parallel", "parallel"), name=name, side=side)


def _attn_bwd(q, kv, datt, Bn, S, Mn, D, name):
    H = XA_HEADS
    dh = D // H
    ts = _attn_ts(S)
    nsb = S // ts
    scale = dh ** -0.5

    def body(q_ref, k_ref, v_ref, do_ref, dq_ref, dk_ref, dv_ref):
        q, k, v, do = q_ref[...], k_ref[...], v_ref[...], do_ref[...]
        p = _softmax_rows(q, k, scale)
        dp = lax.dot_general(do, v, _DN["nt"], preferred_element_type=F32)
        ds = (p * (dp - jnp.sum(dp * p, axis=-1, keepdims=True)) * scale).astype(BF)
        dq_ref[...] = lax.dot_general(ds, k, _DN["nn"], preferred_element_type=F32).astype(BF)

        @pl.when(pl.program_id(2) == 0)
        def _():
            dk_ref[...] = jnp.zeros_like(dk_ref)
            dv_ref[...] = jnp.zeros_like(dv_ref)

        dk_ref[...] += lax.dot_general(ds, q, _DN["tn"], preferred_element_type=F32)
        dv_ref[...] += lax.dot_general(p.astype(BF), do, _DN["tn"], preferred_element_type=F32)

    qblk = pl.BlockSpec((ts, dh), lambda bi, h, s: (bi * nsb + s, h))
    kblk = pl.BlockSpec((Mn, dh), lambda bi, h, s: (bi, h))
    return pl.pallas_call(
        body, grid=(Bn, H, nsb),
        in_specs=[qblk, kblk, pl.BlockSpec((Mn, dh), lambda bi, h, s: (bi, H + h)), qblk],
        out_specs=[qblk, kblk, kblk], out_shape=[SDS((Bn * S, D), BF), SDS((Bn * Mn, D), F32), SDS((Bn * Mn, D), F32)],
        compiler_params=_params("parallel", "parallel", "arbitrary"), name=name)(q, kv, kv, datt)


class _Sides:
    def __init__(self, by_key=None, on_land=None):
        self.by_key, self.landed, self.on_land = dict(by_key or {}), {}, on_land

    def run(self, key, fn, *args, **kw):
        side = self.by_key.get(key)
        if side is None:
            return fn(*args, **kw)
        out, self.landed[key] = fn(*args, side=side() if callable(side) else side, **kw)
        if self.on_land is not None:
            self.on_land(key, self.landed[key])
        return out

    def mm(self, key, *args, **kw):
        return self.run(key, _mm, *args, **kw)


def _layer_fwd(x, h, mem_n, W, V, l, dims, sides, next_g):
    Bn, S, Mn, D, C, F = dims
    n = f"l{l}_"
    proj = sides.mm("proj", h, W["w_in"], "nn", F32, n + "proj", bl=0)
    cv = sides.run("glu_conv", _glu_conv_fwd, proj, V["conv_dw_w"][l], V["conv_dw_b"][l], Bn, S, C, n + "glu_conv")
    yc1, yc = _ln_silu_mm(cv, V["conv_ln_g"][l], V["conv_ln_b"][l], W["w_conv_out"], n + "conv_out")
    yp = _pool_fwd(proj, W["w_pool"], 0, Bn, S, C, D, n + "pool")
    merged = sides.run("merge", _merge_fwd, proj, yc, yp, V["pool_scale"][l], C, n + "merge")
    x1, hq = sides.run("out_proj", _mm_rms_fwd, merged, W["w_out"], x, V["xattn_norm_g"][l], n + "out_proj")
    q = sides.mm("q_proj", hq, W["w_q"], "nn", BF, n + "q_proj", bl=0)
    kv = _mm(mem_n, W["w_kv"], "nn", BF, n + "kv_proj", bl=0)
    att = sides.run("attn", _attn_fwd, q, kv, Bn, S, Mn, D, n + "attn")
    x2, hf = sides.run("o_proj", _mm_rms_fwd, att, W["w_o"], x1, V["ffn_norm_g"][l], n + "o_proj")
    up0 = sides.mm("up_proj", hf, W["w_up"], "nn", F32, n + "up_proj", bl=0)
    gact = sides.run("ffn_act", _ffn_act_fwd, up0, V["ffn_dw_w"][l], Bn, S, F, n + "ffn_act")
    if next_g is not None:
        x3, h3 = sides.run("down_proj", _mm_rms_fwd, gact, W["w_down"], x2, next_g, n + "down_proj")
    else:
        x3, h3 = sides.mm("down_proj", gact, W["w_down"], "nn", F32, n + "down_proj", res=x2, bl=0), None
    return x3, h3, dict(x=x, h=h, proj=proj, cv=cv, yc1=yc1, yc=yc, yp=yp, merged=merged, x1=x1, hq=hq, q=q, kv=kv, att=att, x2=x2,
                        hf=hf, up0=up0, gact=gact)


def _layer_bwd_mlp(dx, dxb, sv, W, V, l, dims, sides):
    Bn, S, Mn, D, C, F = dims
    n = f"l{l}_b_"
    gw, sm = {}, {}
    dgact = sides.mm("d_gact", dxb, W["w_down"], "nt", BF, n + "d_gact", bl=0)
    gw["w_down"] = sides.mm("dw_down", sv["gact"], dxb, "tn", F32, n + "dw_down", twin=BF)
    dup0, dwg, dwv = sides.run("ffn_act_b", _ffn_act_bwd, sv["up0"], V["ffn_dw_w"][l], dgact, Bn, S, F, n + "ffn_act")
    sm["ffn_dw_w"] = jnp.concatenate([dwg, dwv], axis=1)
    dx2, dx2b, sm["ffn_norm_g"] = _mm_rms_bwd([dup0], W["w_up"], sv["x2"], V["ffn_norm_g"][l], dx, n + "d_hf")
    gw["w_up"] = sides.mm("dw_up", sv["hf"], dup0, "tn", F32, n + "dw_up", twin=BF, b_halves=True)
    return dx2, dx2b, gw, sm


def _layer_bwd_mix(dx2, dx2b, dmem_n, sv, mem_n, W, V, l, dims, sides, gw):
    Bn, S, Mn, D, C, F = dims
    n = f"l{l}_b_"
    sm = {}
    datt = sides.mm("d_att", dx2b, W["w_o"], "nt", BF, n + "d_att", bl=0)
    gw["w_o"] = _mm(sv["att"], dx2b, "tn", F32, n + "dw_o", twin=BF)
    dq, dk, dv = _attn_bwd(sv["q"], sv["kv"], datt, Bn, S, Mn, D, n + "attn")
    dkv = jnp.concatenate([dk, dv], axis=1)
    gw["w_kv"] = _mm(mem_n, dkv, "tn", F32, n + "dw_kv", twin=BF)
    dmem_n = _mm(dkv, W["w_kv"], "nt", F32, n + "d_mem", res=dmem_n, bl=0)
    dx1, dx1b, sm["xattn_norm_g"] = _mm_rms_bwd([dq], W["w_q"], sv["x1"], V["xattn_norm_g"][l], dx2, n + "d_hq")
    gw["w_q"] = _mm(sv["hq"], dq, "tn", F32, n + "dw_q", twin=BF)
    dmerged = sides.mm("d_merged", dx1b, W["w_out"], "nt", BF, n + "d_merged", bl=0)
    gw["w_out"] = _mm(sv["merged"], dx1b, "tn", F32, n + "dw_out", twin=BF)
    dgates, dyc, dyp, sm["pool_scale"] = sides.run("merge_b", _merge_bwd, sv["proj"], sv["yc"], sv["yp"], V["pool_scale"][l], dmerged, C, n + "merge")
    du, dwp = _pool_bwd(sv["proj"], W["w_pool"], dyp, 0, Bn, S, C, D, n + "pool")
    gw["w_pool"] = (dwp, dwp.astype(BF))
    gw["w_conv_out"] = _mm(sv["yc1"], dyc, "tn", F32, n + "dw_conv_out", twin=BF)
    dcv, sm["conv_ln_g"], sm["conv_ln_b"] = sides.run("ln_silu_b", _mm_ln_silu_bwd, dyc, W["w_conv_out"], sv["cv"], V["conv_ln_g"][l],
                                                      V["conv_ln_b"][l], n + "d_yc1")
    dagl, sm["conv_dw_w"], sm["conv_dw_b"] = sides.run("glu_conv_b", _glu_conv_bwd, sv["proj"], V["conv_dw_w"][l], dcv, Bn, S, C, n + "glu_conv")
    dx, dxb, sm["mix_norm_g"] = sides.run("d_h", _mm_rms_bwd, [dagl, du, dgates], W["w_in"], sv["x"], V["mix_norm_g"][l], dx1, n + "d_h")
    n_in = W["w_in"].shape[2]
    part = _mm(sv["h"], dagl, "tn", F32, n + "dw_in_conv", twin=BF, b_halves=True, part=(n_in, 0, None))
    part = _mm(sv["h"], du, "tn", F32, n + "dw_in_pool", twin=BF, part=(n_in, 2 * C, part))
    gw["w_in"] = sides.mm("dw_in", sv["h"], dgates, "tn", F32, n + "dw_in", twin=BF, b_halves=True, part=(n_in, 3 * C, part))
    return dx, dxb, dmem_n, sm


BIG = (("w_in", "col"), ("w_conv_out", "col"), ("w_pool", "row"), ("w_out", "row"), ("w_q", "row"), ("w_kv", "col"),
       ("w_o", "row"), ("w_up", "col"), ("w_down", "row"))
ALL_RELS = (1, 2, 3)
GATHER_FIRST = ("w_in", "w_conv_out", "w_pool", "w_out")
FWD_CARRY = {
    (0, "proj"): (("w_up", 0, (1, 2)), ("w_q", 0, ALL_RELS), ("w_o", 0, ALL_RELS)),
    (0, "glu_conv"): (("w_kv", 0, ALL_RELS),),
    (0, "merge"): (("w_up", 0, (3,)),),
    (0, "q_proj"): (("w_down", 0, (1, 2)),),
    (0, "attn"): (("w_down", 0, (3,)),),
    (0, "up_proj"): (("w_in", 1, ALL_RELS), ("w_conv_out", 1, ALL_RELS), ("w_pool", 1, ALL_RELS), ("w_o", 1, ALL_RELS)),
    (0, "ffn_act"): (("w_out", 1, ALL_RELS), ("w_q", 1, ALL_RELS), ("w_kv", 1, ALL_RELS)),
    (1, "proj"): (("w_up", 1, (1, 2)),),
    (1, "glu_conv"): (("w_down", 1, (1, 2)),),
    (1, "merge"): (("w_up", 1, (3,)),),
    (1, "attn"): (("w_down", 1, (3,)),),
}
PASS_CARRY = {
    (0, "out_proj"): (("w_kv", 0), ("w_q", 0), ("w_o", 0)),
    (0, "o_proj"): (("w_up", 0), ("w_down", 0)),
    (0, "down_proj"): (("w_in", 1), ("w_conv_out", 1), ("w_pool", 1), ("w_out", 1), ("w_q", 1), ("w_kv", 1), ("w_o", 1)),
    (1, "o_proj"): (("w_up", 1), ("w_down", 1)),
}
EARLY = ("w_down", "w_up")
BWD_CARRY_EARLY = {"merge_b": ("w_down",), "glu_conv_b": ("w_up",)}
BWD_CARRY_LATE = {"ffn_act_b": ("w_in", "w_conv_out", "w_pool", "w_out", "w_q", "w_kv", "w_o")}
BWD_LAST_LAYER = (("att", ("w_o", "w_kv", "w_q"), "d_merged", {"d_h": ("w_o", "w_kv", "w_q")}),
                  ("tok", ("w_out", "w_pool", "w_conv_out"), "ln_silu_b", {"dw_in": ("w_out", "w_pool", "w_conv_out")}))


def _place():
    xi, yi, ci = lax.axis_index("x"), lax.axis_index("y"), lax.axis_index("c")
    return xi, yi, ci, 2 * xi + yi


def _chip_peer(xi, yi, ci, r):
    return (xi ^ (r >> 1), yi ^ (r & 1), ci)


def _full_shard(ref, kind, k, cs):
    if kind == "col":
        return ref.at[:, :, :, :, pl.ds(pl.multiple_of(k * cs, cs), cs)]
    return ref.at[:, :, k]


def _gather_weights(shards, kinds):
    n = len(shards)
    outs = []
    for s, kind in zip(shards, kinds):
        L, P, _, RH, CS = s.shape
        outs.append(SDS((L, P, 2, RH, CS * N_CHIPS) if kind == "col" else (L, P, N_CHIPS, 2, RH, CS), s.dtype))
    per = 7

    def body(*refs):
        srcs, fulls, (ssem, rsem) = refs[:n], refs[n:2 * n], refs[2 * n:]
        xi, yi, ci, j = _place()
        sib = (xi, yi, 1 - ci)

        def piece(i, k, c):
            kind, cs = kinds[i], shards[i].shape[-1]
            if kind == "col":
                return fulls[i].at[:, :, c, :, pl.ds(pl.multiple_of(k * cs, cs), cs)]
            return fulls[i].at[:, :, k, c]

        def copy(i, slot, src, dst, dev):
            return pltpu.make_async_remote_copy(src_ref=src, dst_ref=dst, send_sem=ssem.at[per * i + slot], recv_sem=rsem.at[per * i + slot],
                                                device_id=dev, device_id_type=MESH)

        own, first, passed = [], [], []
        for i in range(n):
            for r in (1, 2, 3):
                first.append(copy(i, r - 1, srcs[i].at[:, :, ci], piece(i, j, ci), _chip_peer(xi, yi, ci, r)))
                first[-1].start()
        for i in range(n):
            own.append(copy(i, 6, srcs[i], _full_shard(fulls[i], kinds[i], j, shards[i].shape[-1]), sib))
            own[-1].start()
        for i in range(n):
            for r in (1, 2, 3):
                got = piece(i, j ^ r, ci)
                copy(i, r - 1, got, got, sib).wait_recv()
                passed.append(copy(i, 2 + r, got, got, sib))
                passed[-1].start()
        for i in range(n):
            for r in (1, 2, 3):
                got = piece(i, j ^ r, 1 - ci)
                copy(i, 2 + r, got, got, sib).wait_recv()
        for cp in own:
            cp.wait()
        for cp in first + passed:
            cp.wait_send()

    return pl.pallas_call(
        body, in_specs=[ANY] * n, out_specs=[ANY] * n, out_shape=outs,
        scratch_shapes=[pltpu.SemaphoreType.DMA((per * n,)), pltpu.SemaphoreType.DMA((per * n,))], name="gather_weights")(*shards)


def _full_sds(s, kind):
    L, P, _, RH, CS = s.shape
    return SDS((L, P, 2, RH, CS * N_CHIPS) if kind == "col" else (L, P, N_CHIPS, 2, RH, CS), s.dtype)


def _gather_piece(full, kind, cs, k, c):
    if kind == "col":
        return full.at[:, :, c, :, pl.ds(pl.multiple_of(k * cs, cs), cs)]
    return full.at[:, :, k, c]


def _side_gather(shards, kinds, rels, fulls):
    n = len(shards)

    def make(srcs, outs, ssem, rsem):
        xi, yi, ci, j = _place()
        return [pltpu.make_async_remote_copy(
            src_ref=srcs[i].at[:, :, ci], dst_ref=_gather_piece(outs[i], kinds[i], shards[i].shape[-1], j, ci), send_sem=ssem.at[3 * i + r - 1],
            recv_sem=rsem.at[3 * i + r - 1], device_id=_chip_peer(xi, yi, ci, r), device_id_type=MESH) for i in range(n) for r in rels[i]]

    prior = [f for f in fulls if f is not None]
    assert len(prior) in (0, n)
    return _Side(list(shards) + prior, [_full_sds(s, k) for s, k in zip(shards, kinds)], 3 * n, make, n_alias=len(prior))


def _side_gather_pass(fulls, shards, kinds):
    n = len(fulls)

    def make(srcs, outs, ssem, rsem):
        xi, yi, ci, j = _place()
        sib = (xi, yi, 1 - ci)
        cps = []
        for i in range(n):
            cs = shards[i].shape[-1]
            for r in (1, 2, 3):
                got = _gather_piece(outs[i], kinds[i], cs, j ^ r, ci)
                cps.append(pltpu.make_async_remote_copy(src_ref=got, dst_ref=got, send_sem=ssem.at[4 * i + r - 1], recv_sem=rsem.at[4 * i + r - 1],
                                                        device_id=sib, device_id_type=MESH))
            cps.append(pltpu.make_async_remote_copy(src_ref=srcs[i], dst_ref=_full_shard(outs[i], kinds[i], j, cs), send_sem=ssem.at[4 * i + 3],
                                                    recv_sem=rsem.at[4 * i + 3], device_id=sib, device_id_type=MESH))
        return cps

    return _Side(list(shards) + list(fulls), [SDS(f.shape, f.dtype) for f in fulls], 4 * n, make, n_alias=n)


def _sibling_exchange(gviews, kinds, name):
    n = len(gviews)
    outs = [SDS(g.shape[:1] + g.shape[2:] if kind == "col" else g.shape[:2] + g.shape[3:], g.dtype) for g, kind in zip(gviews, kinds)]

    def body(*refs):
        gs, lands, (ssem, rsem) = refs[:n], refs[n:2 * n], refs[2 * n:]
        xi, yi, ci, _ = _place()
        cps = []
        for i in range(n):
            src = gs[i].at[:, 1 - ci] if kinds[i] == "col" else gs[i].at[:, :, 1 - ci]
            cps.append(pltpu.make_async_remote_copy(src_ref=src, dst_ref=lands[i], send_sem=ssem.at[i], recv_sem=rsem.at[i],
                                                    device_id=(xi, yi, 1 - ci), device_id_type=MESH))
            cps[-1].start()
        for cp in cps:
            cp.wait()

    return pl.pallas_call(body, in_specs=[ANY] * n, out_specs=[ANY] * n, out_shape=outs,
                          scratch_shapes=[pltpu.SemaphoreType.DMA((n,)), pltpu.SemaphoreType.DMA((n,))], name=name)(*gviews)


def _side_sibling_exchange(gviews, kinds):
    outs = [SDS(g.shape[:1] + g.shape[2:] if kind == "col" else g.shape[:2] + g.shape[3:], g.dtype) for g, kind in zip(gviews, kinds)]

    def make(gs, lands, ssem, rsem):
        xi, yi, ci, _ = _place()
        return [pltpu.make_async_remote_copy(src_ref=gs[i].at[:, 1 - ci] if kinds[i] == "col" else gs[i].at[:, :, 1 - ci], dst_ref=lands[i],
                                             send_sem=ssem.at[i], recv_sem=rsem.at[i], device_id=(xi, yi, 1 - ci), device_id_type=MESH)
                for i in range(len(gs))]

    return _Side(gviews, outs, len(gviews), make)


def _chip_sums(gs, lands, kinds, jc, name):
    n = len(gs)
    args, in_specs, out_specs, out_shape = [], [], [], []
    for g, land, kind in zip(gs, lands, kinds):
        if kind == "col":
            P, _, RH, C = g.shape
            CS = C // N_CHIPS
            in_specs += [pl.BlockSpec((P, None, RH, CS), lambda r, jc: (0, jc[1], 0, jc[0] ^ r)),
                         pl.BlockSpec((P, RH, CS), lambda r, jc: (0, 0, jc[0] ^ r))]
        else:
            P, _, _, RH, CS = g.shape
            in_specs += [pl.BlockSpec((P, None, None, RH, CS), lambda r, jc: (0, jc[0] ^ r, jc[1], 0, 0)),
                         pl.BlockSpec((P, None, RH, CS), lambda r, jc: (0, jc[0] ^ r, 0, 0))]
        args += [g, land]
        out_specs += [pl.BlockSpec((P, RH, CS), lambda r, jc: (0, 0, 0)), pl.BlockSpec((None, P, RH, CS), lambda r, jc: (r, 0, 0, 0))]
        out_shape += [SDS((P, RH, CS), F32), SDS((N_CHIPS, P, RH, CS), BF)]

    def body(jc_ref, *refs):
        ins, outs = refs[:2 * n], refs[2 * n:]
        for i in range(n):
            s = ins[2 * i][...] + ins[2 * i + 1][...].astype(F32)
            outs[2 * i + 1][...] = s.astype(BF)

            @pl.when(pl.program_id(0) == 0)
            def _():
                outs[2 * i][...] = s

    outs = _call(body, args, grid=(N_CHIPS,), in_specs=in_specs, out_specs=out_specs, out_shape=out_shape, semantics=("arbitrary",),
                 name=name, prefetch=(jc,))
    return outs[0::2], outs[1::2]


def _chip_exchange_copies(srcs, lands, ssem, rsem):
    xi, yi, ci, _ = _place()
    return [pltpu.make_async_remote_copy(src_ref=srcs[i].at[r], dst_ref=lands[i].at[r], send_sem=ssem.at[3 * i + r - 1],
                                         recv_sem=rsem.at[3 * i + r - 1], device_id=_chip_peer(xi, yi, ci, r), device_id_type=MESH)
            for i in range(len(srcs)) for r in (1, 2, 3)]


def _side_chip_exchange(pieces):
    return _Side(pieces, [SDS(p.shape, p.dtype) for p in pieces], 3 * len(pieces), _chip_exchange_copies)


FINAL_SUM_STEPS = 2


def _final_sums(owns, lands, jc, shards, l, L, name, side=None):
    n = len(owns)
    args, in_specs, out_specs, out_shape = [], [], [], []
    for own, land in zip(owns, lands):
        P, RH, CS = own.shape
        hr = RH // FINAL_SUM_STEPS
        in_specs += [pl.BlockSpec((P, hr, CS), lambda h, jc: (0, h, 0))]
        in_specs += [pl.BlockSpec((None, P, hr, CS), functools.partial(lambda r, h, jc: (r, 0, h, 0), r)) for r in (1, 2, 3)]
        args += [own, land, land, land]
        out_specs.append(pl.BlockSpec((None, P, None, hr, CS), lambda h, jc: (l, 0, jc[1], h, 0)))
        out_shape.append(SDS((L, P, 2, RH, CS), F32))
    aliases = None
    if shards is not None:
        aliases = {4 * n + i: i for i in range(n)}
        in_specs += [ANY] * n
        args += list(shards)

    def body(jc_ref, *refs):
        outs = refs[len(args):]
        for i in range(n):
            o, a, b, c = (refs[4 * i + t][...] for t in range(4))
            outs[i][...] = ((o + a.astype(F32)) + b.astype(F32)) + c.astype(F32)

    return _call(body, args, grid=(FINAL_SUM_STEPS,), in_specs=in_specs, out_specs=out_specs, out_shape=out_shape, semantics=("arbitrary",),
                 name=name, prefetch=(jc,), aliases=aliases, side=side)


def _halves_exchange(shards, l, name):
    n = len(shards)

    def body(*refs):
        outs, (ssem, rsem) = refs[n:2 * n], refs[2 * n:]
        xi, yi, ci, _ = _place()
        cps = []
        for i in range(n):
            mine = outs[i].at[l, :, ci]
            cps.append(pltpu.make_async_remote_copy(src_ref=mine, dst_ref=mine, send_sem=ssem.at[i], recv_sem=rsem.at[i],
                                                    device_id=(xi, yi, 1 - ci), device_id_type=MESH))
            cps[-1].start()
        for i in range(n):
            land = outs[i].at[l, :, 1 - ci]
            pltpu.make_async_remote_copy(src_ref=land, dst_ref=land, send_sem=ssem.at[i], recv_sem=rsem.at[i],
                                         device_id=(xi, yi, 1 - ci), device_id_type=MESH).wait_recv()
        for cp in cps:
            cp.wait_send()

    return pl.pallas_call(body, in_specs=[ANY] * n, out_specs=[ANY] * n, out_shape=[SDS(s.shape, s.dtype) for s in shards],
                          input_output_aliases={i: i for i in range(n)},
                          scratch_shapes=[pltpu.SemaphoreType.DMA((n,)), pltpu.SemaphoreType.DMA((n,))], name=name)(*shards)


def _reduce_small(part, pieces):
    NR, Wd = part.shape
    ND = 2 * N_CHIPS
    n = len(pieces)

    def body(p_ref, *refs):
        srcs, o_ref, lands, (land, ssem, rsem, xs, xr) = refs[:n], refs[n], refs[n + 1:2 * n + 1], refs[2 * n + 1:]
        exchange = _chip_exchange_copies(srcs, lands, xs, xr)
        for cp in exchange:
            cp.start()
        xi, yi, ci, j = _place()
        me = 2 * j + ci
        land[me] = p_ref[...]
        cps = []
        for rr in range(1, ND):
            dev = (xi ^ (rr >> 2), yi ^ ((rr >> 1) & 1), ci ^ (rr & 1))
            cps.append(pltpu.make_async_remote_copy(src_ref=p_ref, dst_ref=land.at[me], send_sem=ssem.at[rr - 1], recv_sem=rsem.at[rr - 1],
                                                    device_id=dev, device_id_type=MESH))
            cps[-1].start()
        for rr in range(1, ND):
            got = land.at[me ^ rr]
            pltpu.make_async_remote_copy(src_ref=got, dst_ref=got, send_sem=ssem.at[rr - 1], recv_sem=rsem.at[rr - 1],
                                         device_id=(xi, yi, ci), device_id_type=MESH).wait_recv()
        acc = land[0]
        for d in range(1, ND):
            acc = acc + land[d]
        o_ref[...] = acc
        for cp in cps:
            cp.wait_send()
        for cp in exchange:
            cp.wait()

    vm = pl.BlockSpec(memory_space=pltpu.VMEM)
    outs = pl.pallas_call(
        body, in_specs=[vm] + [ANY] * n, out_specs=[vm] + [ANY] * n, out_shape=[SDS((NR, Wd), F32)] + [SDS(p.shape, p.dtype) for p in pieces],
        scratch_shapes=[pltpu.VMEM((ND, NR, Wd), F32), pltpu.SemaphoreType.DMA((ND - 1,)), pltpu.SemaphoreType.DMA((ND - 1,)),
                        pltpu.SemaphoreType.DMA((3 * n,)), pltpu.SemaphoreType.DMA((3 * n,))],
        name="small_grad_allreduce")(part, *pieces)
    return outs[0], list(outs[1:])


def _adamw_update(w_ref, g_ref, m_ref, v_ref, d_ref, mo_ref, vo_ref):
    g = g_ref[...]
    m = ADAM_B1 * m_ref[...] + (1.0 - ADAM_B1) * g
    v = ADAM_B2 * v_ref[...] + (1.0 - ADAM_B2) * jnp.square(g)
    m_hat = m / (1.0 - ADAM_B1 ** ADAM_STEP)
    v_hat = v / (1.0 - ADAM_B2 ** ADAM_STEP)
    d_ref[...] = -ADAM_LR * (m_hat / (jnp.sqrt(v_hat) + ADAM_EPS) + ADAM_WD * w_ref[...])
    mo_ref[...] = m
    vo_ref[...] = v


ADAMW_STEPS = 8


def _adamw_layer(ws, gs, ms, vs, prev, l, name, side=None):
    n = len(ws)
    args, in_specs, out_specs, out_shape = [], [], [], []
    for w, g, m, v in zip(ws, gs, ms, vs):
        L, R, C = w.shape
        blk = pl.BlockSpec((None, R // ADAMW_STEPS, C), lambda i: (l, i, 0))
        in_specs += [blk] * 4
        args += [w, g, m, v]
        out_specs += [blk] * 3
        out_shape += [SDS((L, R, C), F32)] * 3
    aliases = None
    if prev is not None:
        aliases = {4 * n + i: i for i in range(3 * n)}
        in_specs += [ANY] * (3 * n)
        args += list(prev)

    def body(*refs):
        outs = refs[len(args):]
        for i in range(n):
            _adamw_update(*refs[4 * i:4 * i + 4], *outs[3 * i:3 * i + 3])

    return _call(body, args, grid=(ADAMW_STEPS,), in_specs=in_specs, out_specs=out_specs, out_shape=out_shape, semantics=("parallel",),
                 name=name, aliases=aliases, side=side)


def _adamw(w, g, m, v, name):
    shape = w.shape
    C = shape[-1]
    R = w.size // C
    tb = _tile(R, max(8, (1 << 18) // C), 8)
    body = functools.partial(_adamw_update)
    blk = pl.BlockSpec((tb, C), lambda i: (i, 0))
    outs = pl.pallas_call(body, grid=(R // tb,), in_specs=[blk] * 4, out_specs=[blk] * 3, out_shape=[SDS((R, C), F32)] * 3,
                          compiler_params=_params("parallel"), name=name)(*[t.reshape(R, C) for t in (w, g, m, v)])
    return [t.reshape(shape) for t in outs]


WEIGHTS = ("mix_norm_g", "w_in", "conv_dw_w", "conv_dw_b", "conv_ln_g", "conv_ln_b", "w_conv_out", "w_pool_grp", "pool_scale", "w_out",
           "xattn_norm_g", "mem_norm_g", "w_q", "w_kv", "w_o", "ffn_norm_g", "w_up", "ffn_dw_w", "w_down", "final_norm_g")
VECTORS = ("mix_norm_g", "conv_dw_b", "conv_ln_g", "conv_ln_b", "pool_scale", "xattn_norm_g", "mem_norm_g", "ffn_norm_g", "final_norm_g")


def _shard_view(t, kind):
    L, P, R, C = t.shape
    return t.reshape(L, P, 2, R // 2, C)


def _rows(t, width):
    return t.reshape(-1, width)


def _pack(parts):
    return jnp.concatenate([jnp.pad(p, ((0, (-p.shape[0]) % 8), (0, 0))) for p in parts], axis=0)


def kernel(x, mem, mix_norm_g, w_in, conv_dw_w, conv_dw_b, conv_ln_g, conv_ln_b, w_conv_out, w_pool_grp, pool_scale, w_out, xattn_norm_g, mem_norm_g, w_q, w_kv, w_o, ffn_norm_g, w_up, ffn_dw_w, w_down, final_norm_g, loss_target, m_mix_norm_g, m_w_in, m_conv_dw_w, m_conv_dw_b, m_conv_ln_g, m_conv_ln_b, m_w_conv_out, m_w_pool_grp, m_pool_scale, m_w_out, m_xattn_norm_g, m_mem_norm_g, m_w_q, m_w_kv, m_w_o, m_ffn_norm_g, m_w_up, m_ffn_dw_w, m_w_down, m_final_norm_g, v_mix_norm_g, v_w_in, v_conv_dw_w, v_conv_dw_b, v_conv_ln_g, v_conv_ln_b, v_w_conv_out, v_w_pool_grp, v_pool_scale, v_w_out, v_xattn_norm_g, v_mem_norm_g, v_w_q, v_w_kv, v_w_o, v_ffn_norm_g, v_w_up, v_ffn_dw_w, v_w_down, v_final_norm_g):
    w = dict(mix_norm_g=mix_norm_g, w_in=w_in, conv_dw_w=conv_dw_w, conv_dw_b=conv_dw_b, conv_ln_g=conv_ln_g, conv_ln_b=conv_ln_b,
             w_conv_out=w_conv_out, w_pool_grp=w_pool_grp, pool_scale=pool_scale, w_out=w_out, xattn_norm_g=xattn_norm_g,
             mem_norm_g=mem_norm_g, w_q=w_q, w_kv=w_kv, w_o=w_o, ffn_norm_g=ffn_norm_g, w_up=w_up, ffn_dw_w=ffn_dw_w, w_down=w_down,
             final_norm_g=final_norm_g)
    m = dict(zip(WEIGHTS, (m_mix_norm_g, m_w_in, m_conv_dw_w, m_conv_dw_b, m_conv_ln_g, m_conv_ln_b, m_w_conv_out, m_w_pool_grp, m_pool_scale,
                           m_w_out, m_xattn_norm_g, m_mem_norm_g, m_w_q, m_w_kv, m_w_o, m_ffn_norm_g, m_w_up, m_ffn_dw_w, m_w_down, m_final_norm_g)))
    v = dict(zip(WEIGHTS, (v_mix_norm_g, v_w_in, v_conv_dw_w, v_conv_dw_b, v_conv_ln_g, v_conv_ln_b, v_w_conv_out, v_w_pool_grp, v_pool_scale,
                           v_w_out, v_xattn_norm_g, v_mem_norm_g, v_w_q, v_w_kv, v_w_o, v_ffn_norm_g, v_w_up, v_ffn_dw_w, v_w_down, v_final_norm_g)))
    xi, yi, ci, j = _place()
    jc = jnp.stack([j, ci]).astype(jnp.int32)
    L = w_in.shape[0]
    G = len(POOL_WINDOWS)
    kinds = dict(BIG)

    def to_mat(name, t):
        if name == "w_pool":
            return jnp.swapaxes(t, 2, 3)
        return t[:, None]

    def from_mat(name, t):
        if name == "w_pool":
            return jnp.swapaxes(t, 2, 3)
        return t[:, 0]

    src = {name: w["w_pool_grp" if name == "w_pool" else name] for name, _ in BIG}

    KC, cs_c = conv_dw_w.shape[1], conv_dw_w.shape[2]
    KF, cs_f = ffn_dw_w.shape[1], ffn_dw_w.shape[2]
    taps = jnp.concatenate([conv_dw_w.reshape(L * KC, cs_c), ffn_dw_w.reshape(L * KF * (cs_f // cs_c), cs_c)], axis=0)
    n_taps = taps.shape[0]
    taps = jnp.pad(taps, ((0, (-n_taps) % 16), (0, 0)))
    names = [name for name, _ in BIG]
    mats = {name: to_mat(name, src[name]).astype(BF) for name in names}

    def layer_shards(l, subset):
        return [_shard_view(mats[name][l:l + 1], kinds[name]) for name in subset]

    def as_weight(name, f):
        return f.reshape(G if name == "w_pool" else 1, -1, f.shape[-1])

    assert L == 2
    fulls = _gather_weights(layer_shards(0, GATHER_FIRST) + [_shard_view(taps[None, None], "row")], [kinds[name] for name in GATHER_FIRST] + ["row"])
    ready = {(name, 0): as_weight(name, f) for name, f in zip(GATHER_FIRST, fulls)}
    landing = {}
    taps_all = fulls[-1].reshape(N_CHIPS, -1, cs_c)[:, :n_taps]
    V = {name: w[name] for name in VECTORS}
    V["conv_dw_w"] = taps_all[:, :L * KC].reshape(N_CHIPS, L, KC, cs_c).transpose(1, 2, 0, 3).reshape(L, KC, N_CHIPS * cs_c)
    V["ffn_dw_w"] = taps_all[:, L * KC:].reshape(N_CHIPS, L, KF, cs_f).transpose(1, 2, 0, 3).reshape(L, KF, N_CHIPS * cs_f)

    Bn, S, D = x.shape
    Mn = mem.shape[1]
    dims = (Bn, S, Mn, D, conv_dw_b.shape[1], w_down.shape[1] * N_CHIPS)
    xt = x.reshape(Bn * S, D)
    memf = mem.reshape(Bn * Mn, D)
    mem_n = _rms_fwd(memf, V["mem_norm_g"], "mem_norm")

    class LayerWeights:
        def __init__(self, l):
            self.l = l

        def __getitem__(self, name):
            return ready[(name, self.l)]

    def carried_gather(entries):
        return lambda: _side_gather([layer_shards(lw, [nm])[0] for nm, lw, _ in entries], [kinds[nm] for nm, _, _ in entries],
                                    [rels for _, _, rels in entries], [landing.get((nm, lw)) for nm, lw, _ in entries])

    def carried_pass(group):
        return lambda: _side_gather_pass([landing.pop(t) for t in group], [layer_shards(lw, [nm])[0] for nm, lw in group],
                                         [kinds[nm] for nm, _ in group])

    def on_land(l):
        def handle(key, fulls):
            if (l, key) in FWD_CARRY:
                landing.update({(nm, lw): f for (nm, lw, _), f in zip(FWD_CARRY[(l, key)], fulls)})
            else:
                ready.update({t: as_weight(t[0], f) for t, f in zip(PASS_CARRY[(l, key)], fulls)})
        return handle

    saved, W = [], []
    ht = _rms_fwd(xt, V["mix_norm_g"][0], "l0_mix_norm")
    for l in range(L):
        by_key = {key: carried_gather(entries) for (cl, key), entries in FWD_CARRY.items() if cl == l}
        by_key.update({key: carried_pass(group) for (cl, key), group in PASS_CARRY.items() if cl == l})
        sides = _Sides(by_key, on_land=on_land(l))
        W.append(LayerWeights(l))
        xt, ht, sv = _layer_fwd(xt, ht, mem_n, W[l], V, l, dims, sides, V["mix_norm_g"][l + 1] if l + 1 < L else None)
        saved.append(sv)
    loss_part, dx, dgf = _loss_bwd(xt, V["final_norm_g"], loss_target.reshape(Bn * S, D), "loss")

    late_names = [name for name in names if name not in EARLY]

    def views(gw, subset, twin):
        out = []
        for name in subset:
            g = gw[name][twin] if gw[name][twin].ndim == 3 else gw[name][twin][None]
            P, R, C = g.shape
            out.append(g.reshape(P, 2, R // 2, C) if kinds[name] == "col" else g.reshape(P, N_CHIPS, 2, R // (2 * N_CHIPS), C))
        return out

    def group_kinds(subset):
        return [kinds[name] for name in subset]

    class Reduction:
        def __init__(self, gw, subset, l, tag, first, table):
            self.gw, self.subset, self.l, self.tag, self.first, self.table = gw, subset, l, tag, first, table

        def sides(self):
            by_key = {self.first: lambda: _side_sibling_exchange(views(self.gw, self.subset, 1), group_kinds(self.subset))}
            by_key.update({key: (lambda names_=names_: _side_chip_exchange([self.pieces[nm] for nm in names_])) for key, names_ in self.table.items()})
            return by_key

        def on_land(self, key, landed):
            if key == self.first:
                self.sums(landed)
            elif key in self.table:
                got[self.l].update(zip(self.table[key], landed))

        def sums(self, lands):
            own, pieces = _chip_sums(views(self.gw, self.subset, 0), lands, group_kinds(self.subset), jc, f"chip_sums_{self.tag}_l{self.l}")
            owns[self.l].update(zip(self.subset, own))
            self.pieces = dict(zip(self.subset, pieces))

    def riding(reductions):
        return _Sides({key: side for r in reductions for key, side in r.sides().items()},
                      on_land=lambda key, landed: [r.on_land(key, landed) for r in reductions])

    dxb, dmem_n = dx, None
    smalls, owns, got = [None] * L, [{} for _ in range(L)], [{} for _ in range(L)]
    late = None
    for l in reversed(range(L)):
        dx, dxb, gw, sm = _layer_bwd_mlp(dx, dxb, saved[l], W[l], V, l, dims, riding([late] if late is not None else []))
        gw_mix = {}
        reductions = [Reduction(gw, EARLY, l, "mlp", "d_att", BWD_CARRY_EARLY)]
        if l == 0:
            reductions += [Reduction(gw_mix, names_, 0, tag, first, table) for tag, names_, first, table in BWD_LAST_LAYER]
        dx, dxb, dmem_n, sm2 = _layer_bwd_mix(dx, dxb, dmem_n, saved[l], mem_n, W[l], V, l, dims, riding(reductions), gw_mix)
        smalls[l] = {**sm, **sm2}
        late = Reduction(gw_mix, late_names, l, "mix", "d_gact", BWD_CARRY_LATE) if l > 0 else None
    last = Reduction(gw_mix, ("w_in",), 0, "in", None, {})
    last.sums(_sibling_exchange(views(gw_mix, last.subset, 1), group_kinds(last.subset), "grad_sibling_exchange_in_l0"))
    grad_x = dx.reshape(Bn, S, D)
    _, _, dgm = _rms_bwd(memf, V["mem_norm_g"], dmem_n, None, "mem_norm_b")
    small = {k: jnp.stack([sm[k] for sm in smalls]) if k in ("conv_dw_w", "ffn_dw_w") else jnp.concatenate([sm[k] for sm in smalls], axis=0)
             for k in smalls[0]}
    small["mem_norm_g"] = dgm
    small["final_norm_g"] = dgf

    small_w = conv_dw_b.shape[1]
    order = VECTORS + ("conv_dw_w", "ffn_dw_w")
    parts = [_rows(small[name], small_w) for name in order]
    counts = [p.shape[0] for p in parts]
    loss_row = sum(cnt + (-cnt) % 8 for cnt in counts)
    summed, landed_last = _reduce_small(_pack(parts + [jnp.broadcast_to(loss_part[:1, :1], (1, small_w))]),
                                        [last.pieces[name] for name in last.subset])
    loss = summed[loss_row, 0]
    got[0].update(zip(last.subset, landed_last))

    keys = ["w_pool_grp" if name == "w_pool" else name for name in names]
    rows3 = lambda t: t.reshape(t.shape[0], -1, t.shape[-1])
    wmv = [[rows3(to_mat(name, d[key])) for name, key in zip(names, keys)] for d in (w, m, v)]
    gshards, updates = None, None
    for l in reversed(range(L)):
        gshards = _final_sums([owns[l][name] for name in names], [got[l][name] for name in names], jc, gshards, l, L, f"final_sums_l{l}")
        gshards = _halves_exchange(gshards, l, f"grad_halves_exchange_l{l}")
        updates = _adamw_layer(wmv[0], [rows3(t) for t in gshards], wmv[1], wmv[2], updates, l, f"adamw_l{l}")
    grads, delta, new_m, new_v = {}, {}, {}, {}
    for i, (name, key) in enumerate(zip(names, keys)):
        Lg, P, _, RH, CS = gshards[i].shape
        grads[key] = from_mat(name, gshards[i].reshape(Lg, P, 2 * RH, CS))
        for d, t in zip((delta, new_m, new_v), updates[3 * i:3 * i + 3]):
            d[key] = from_mat(name, t.reshape(Lg, P, 2 * RH, CS))

    off = 0
    for name, cnt in zip(order, counts):
        t = summed[off:off + cnt]
        off += cnt + (-cnt) % 8
        if name in VECTORS:
            grads[name] = t.reshape(w[name].shape)
        else:
            full = t.reshape(small[name].shape)
            cs = w[name].shape[2]
            grads[name] = lax.dynamic_slice_in_dim(full, j * cs, cs, axis=2)

    vec =[_pack([_rows(d[name], small_w) for name in VECTORS]) for d in (w, grads, m, v)]
    outs = _adamw(*vec, "adamw_vectors")
    off = 0
    for name in VECTORS:
        cnt = w[name].size // small_w
        for d, t in zip((delta, new_m, new_v), outs):
            d[name] = t[off:off + cnt].reshape(w[name].shape)
        off += cnt + (-cnt) % 8
    for name in ("conv_dw_w", "ffn_dw_w"):
        delta[name], new_m[name], new_v[name] = _adamw(w[name], grads[name], m[name], v[name], "adamw_" + name)

    return (loss, grad_x, *[grads[k] for k in WEIGHTS], *[delta[k] for k in WEIGHTS], *[new_m[k] for k in WEIGHTS], *[new_v[k] for k in WEIGHTS])
```

```python
import functools
import math

import jax
import jax.numpy as jnp
from jax import lax
from jax.experimental import pallas as pl
from jax.experimental.pallas import tpu as pltpu

F32 = jnp.float32
BF = jnp.bfloat16
SDS = jax.ShapeDtypeStruct
MESH = pl.DeviceIdType.MESH
ANY = pl.BlockSpec(memory_space=pl.ANY)

EPS = 1e-6
XA_HEADS = 4
POOL_WINDOWS = (2, 4, 8, 16)
N_CHIPS = 4
ADAM_LR, ADAM_B1, ADAM_B2, ADAM_EPS, ADAM_WD, ADAM_STEP = 0.001, 0.9, 0.999, 1e-08, 0.01, 10

LANES = 128
ROW_BLOCK = 1024
VMEM_LIMIT = 56 * 1024 * 1024


def _params(*sem):
    return pltpu.CompilerParams(dimension_semantics=sem if sem else None, vmem_limit_bytes=VMEM_LIMIT)


def _tile(n, cap, mult=LANES):
    if n <= cap:
        return n
    for t in range(cap - cap % mult, 0, -mult):
        if n % t == 0:
            return t
    return n


_DN = {"nn": (((1,), (0,)), ((), ())), "nt": (((1,), (1,)), ((), ())), "tn": (((0,), (0,)), ((), ()))}


class _Side:
    def __init__(self, ins, outs, n, make, n_alias=0):
        self.ins, self.outs, self.n, self.make, self.n_alias = list(ins), list(outs), n, make, n_alias


def _call(body, args, *, grid, in_specs, out_specs, out_shape, semantics, name, scratch_shapes=(), side=None, prefetch=(), aliases=None):
    n_pf = len(prefetch)
    aliases = {n_pf + i: o for i, o in (aliases or {}).items()}
    n_in, n_out, n_scr = len(args), len(out_shape), len(scratch_shapes)
    n_si, n_so = (len(side.ins), len(side.outs)) if side is not None else (0, 0)
    if side is not None:
        aliases.update({n_pf + n_in + n_si - side.n_alias + i: n_out + i for i in range(side.n_alias)})

    def carrying(*refs):
        pf, refs = refs[:n_pf], refs[n_pf:]
        ins, s_in = refs[:n_in], refs[n_in:n_in + n_si]
        outs, s_out = refs[n_in + n_si:n_in + n_si + n_out], refs[n_in + n_si + n_out:n_in + n_si + n_out + n_so]
        scr = refs[n_in + n_si + n_out + n_so:]
        if side is None:
            return body(*pf, *ins, *outs, *scr)
        copies = side.make(s_in, s_out, scr[n_scr], scr[n_scr + 1])
        ids = [pl.program_id(d) for d in range(len(grid))]
        first, last = ids[0] == 0, ids[0] == grid[0] - 1
        for d in range(1, len(grid)):
            first, last = first & (ids[d] == 0), last & (ids[d] == grid[d] - 1)

        @pl.when(first)
        def _():
            for cp in copies:
                cp.start()

        body(*pf, *ins, *outs, *scr[:n_scr])

        @pl.when(last)
        def _():
            for cp in copies:
                cp.wait()

    sems = [pltpu.SemaphoreType.DMA((side.n,)), pltpu.SemaphoreType.DMA((side.n,))] if side is not None else []
    outs = pl.pallas_call(
        carrying, grid_spec=pltpu.PrefetchScalarGridSpec(
            num_scalar_prefetch=n_pf, grid=grid, in_specs=list(in_specs) + [ANY] * n_si, out_specs=list(out_specs) + [ANY] * n_so,
            scratch_shapes=list(scratch_shapes) + sems),
        out_shape=list(out_shape) + (side.outs if side is not None else []), input_output_aliases=aliases,
        compiler_params=_params(*(semantics if side is None else ["arbitrary"] * len(grid))), name=name)(
            *prefetch, *args, *(side.ins if side is not None else []))
    return list(outs) if side is None else (list(outs[:n_out]), list(outs[n_out:]))


def _call1(body, args, *, out_spec, out_shape, side=None, **kw):
    got = _call(body, args, out_specs=[out_spec], out_shape=[out_shape], side=side, **kw)
    return got[0] if side is None else (got[0][0], got[1])


MM_VMEM_BUDGET = 40 * 1024 * 1024
MM_STEP_MACS = 2200 * 1024 * 1024
MXU_WIDTH = 256
MM_STEP_COST_BYTES = 1 << 20


def _divisors(n):
    return [t for t in range(LANES, n + 1, LANES) if n % t == 0] or [n]


def _mm_tiles(M, N, K, a_bytes, b_bytes, o_bytes, n_unit=None):
    best = None
    for tk in _divisors(K):
        for tm in _divisors(M):
            for tn in _divisors(N if n_unit is None else n_unit):
                nk = K // tk
                foot = 2 * (tm * tk * a_bytes + tk * tn * b_bytes + tm * tn * o_bytes) + (tm * tn * 4 if nk > 1 else 0)
                if (foot > MM_VMEM_BUDGET or tm * tn * tk > MM_STEP_MACS or tn < min(N if n_unit is None else n_unit, MXU_WIDTH)
                        or tm < min(M, MXU_WIDTH)):
                    continue
                steps = (M // tm) * (N // tn) * nk
                traffic = M * K * a_bytes * (N // tn if nk > 1 else 1) + K * N * b_bytes * (M // tm) + M * N * o_bytes
                exposed = tm * tk * a_bytes + tk * tn * b_bytes + tm * tn * o_bytes
                cost = traffic + exposed + steps * MM_STEP_COST_BYTES + (nk - 1) * M * N * 8
                if best is None or cost < best[0]:
                    best = (cost, tm, tn, tk)
    assert best is not None, (M, N, K)
    return best[1:]


def _mm(a, b, dims, out_dtype, name, res=None, bl=None, side=None, twin=None, b_halves=False, part=None):
    bs = b.shape[1:] if bl is not None or b_halves else b.shape
    if dims == "nn":
        (M, K), (K2, N) = a.shape, bs
    elif dims == "nt":
        (M, K), (N, K2) = a.shape, bs
    else:
        (K, M), (K2, N) = a.shape, bs
    assert K == K2, (name, a.shape, b.shape)
    n_half = N
    if b_halves:
        assert dims == "tn" and bl is None
        N = 2 * n_half
    n_total, n_first, earlier = part if part is not None else (N, 0, None)
    tm, tn, tk = _mm_tiles(M, N, K, a.dtype.itemsize, b.dtype.itemsize, jnp.dtype(out_dtype).itemsize
                           + (res.dtype.itemsize if res is not None else 0) + (jnp.dtype(twin).itemsize if twin is not None else 0),
                           n_unit=math.gcd(n_half, n_first) if b_halves or n_first else None)
    nk = K // tk
    lead = (None,) if bl is not None or b_halves else ()
    pre = (lambda *ix: (bl,) + ix) if bl is not None else (lambda *ix: ix)
    if b_halves:
        per_half = n_half // tn
        pre = lambda k, j: (j // per_half, k, j % per_half)
    if dims == "tn":
        a_spec = pl.BlockSpec((tk, tm), lambda i, j, k: (k, i))
    else:
        a_spec = pl.BlockSpec((tm, tk), lambda i, j, k: (i, k))
    if dims == "nt":
        b_spec = pl.BlockSpec(lead + (tn, tk), lambda i, j, k: pre(j, k))
    else:
        b_spec = pl.BlockSpec(lead + (tk, tn), lambda i, j, k: pre(k, j))
    assert n_first % tn == 0 and (part is None or res is None)
    o_spec = pl.BlockSpec((tm, tn), lambda i, j, k: (i, n_first // tn + j))
    in_specs, args = [a_spec, b_spec], [a, b]
    if res is not None:
        in_specs.append(o_spec)
        args.append(res)
    n_main = len(args)
    n_out = 1 if twin is None else 2
    aliases = None
    if earlier is not None:
        earlier = list(earlier) if twin is not None else [earlier]
        aliases = {n_main + t: t for t in range(n_out)}
        in_specs += [ANY] * n_out
        args += earlier

    def body(*refs):
        refs = refs[:n_main] + refs[len(args):]
        a_ref, b_ref = refs[0], refs[1]
        r_ref = refs[2] if res is not None else None
        o_ref = refs[n_main]
        p = lax.dot_general(a_ref[...].astype(BF), b_ref[...].astype(BF), _DN[dims], preferred_element_type=F32)

        def finish(t):
            if r_ref is not None:
                t = t + r_ref[...]
            o_ref[...] = t.astype(out_dtype)
            if twin is not None:
                refs[n_main + 1][...] = t.astype(twin)

        if nk == 1:
            finish(p)
        else:
            acc = refs[n_main + n_out]
            k = pl.program_id(2)

            @pl.when(k == 0)
            def _():
                acc[...] = p

            @pl.when(k > 0)
            def _():
                acc[...] += p

            @pl.when(k == nk - 1)
            def _():
                finish(acc[...])

    got = _call(body, args, grid=(M // tm, N // tn, nk), in_specs=in_specs, out_specs=[o_spec] * n_out,
                out_shape=[SDS((M, n_total), out_dtype)] + ([SDS((M, n_total), twin)] if twin is not None else []),
                scratch_shapes=[pltpu.VMEM((tm, tn), F32)] if nk > 1 else [], semantics=("parallel", "parallel", "arbitrary"),
                name=name, side=side, aliases=aliases)
    outs, landed = (got, None) if side is None else got
    out = outs[0] if twin is None else (outs[0], outs[1])
    return out if side is None else (out, landed)


def _rms(x, g):
    return x * lax.rsqrt(jnp.mean(x * x, axis=-1, keepdims=True) + EPS) * g


def _ln_silu(x, g, b):
    mu = jnp.mean(x, axis=-1, keepdims=True)
    xc = x - mu
    var = jnp.mean(xc * xc, axis=-1, keepdims=True)
    return jax.nn.silu(xc * lax.rsqrt(var + EPS) * g + b)


def _merge(gc, gp, yc, yp, ps):
    return jax.nn.sigmoid(gc) * yc + jax.nn.sigmoid(gp) * (yp * ps)


def _gated(gate, val):
    return jax.nn.gelu(gate) * val


def _rms_fwd(x, g, name):
    T, D = x.shape
    tb = _tile(T, ROW_BLOCK, 8)

    def body(x_ref, g_ref, o_ref):
        o_ref[...] = _rms(x_ref[...], g_ref[...]).astype(BF)

    row = pl.BlockSpec((tb, D), lambda i: (i, 0))
    return pl.pallas_call(body, grid=(T // tb,), in_specs=[row, pl.BlockSpec((1, D), lambda i: (0, 0))], out_specs=row,
                          out_shape=SDS((T, D), BF), compiler_params=_params("parallel"), name=name)(x, g.reshape(1, D))


def _rms_bwd(x, g, dh, dres, name):
    T, D = x.shape
    tb = _tile(T, ROW_BLOCK, 8)

    def body(*refs):
        if dres is not None:
            x_ref, g_ref, dh_ref, dres_ref, dx_ref, dxb_ref, dg_ref = refs
        else:
            x_ref, g_ref, dh_ref, dx_ref, dxb_ref, dg_ref = refs
        _, vjp = jax.vjp(_rms, x_ref[...], g_ref[...])
        dx, dg = vjp(dh_ref[...].astype(F32))
        if dres is not None:
            dx = dx + dres_ref[...]
        dx_ref[...] = dx
        dxb_ref[...] = dx.astype(BF)

        @pl.when(pl.program_id(0) == 0)
        def _():
            dg_ref[...] = jnp.zeros_like(dg_ref)

        dg_ref[...] += dg

    row = pl.BlockSpec((tb, D), lambda i: (i, 0))
    vec = pl.BlockSpec((1, D), lambda i: (0, 0))
    ins = [x, g.reshape(1, D), dh] + ([dres] if dres is not None else [])
    return pl.pallas_call(
        body, grid=(T // tb,), in_specs=[row, vec, row] + ([row] if dres is not None else []), out_specs=[row, row, vec],
        out_shape=[SDS((T, D), F32), SDS((T, D), BF), SDS((1, D), F32)], compiler_params=_params("arbitrary"), name=name)(*ins)


def _row_tile(M, K, N, per_row_bytes):
    fixed = K * N * 2
    fit = [t for t in _divisors(M) if fixed + 2 * t * per_row_bytes <= MM_VMEM_BUDGET and t * K * N <= 2 * MM_STEP_MACS]
    return max(fit) if fit else min(_divisors(M))


def _mm_rms_fwd(a, b, res, g, name, side=None):
    M, K = a.shape
    N = b.shape[2]
    tm = _row_tile(M, K, N, K * 2 + N * (4 + 4 + 2))

    def body(a_ref, b_ref, r_ref, g_ref, x_ref, h_ref):
        x = r_ref[...] + lax.dot_general(a_ref[...], b_ref[...], _DN["nn"], preferred_element_type=F32)
        x_ref[...] = x
        h_ref[...] = _rms(x, g_ref[...]).astype(BF)

    row = pl.BlockSpec((tm, N), lambda i: (i, 0))
    return _call(body, (a, b, res, g.reshape(1, N)), grid=(M // tm,),
                 in_specs=[pl.BlockSpec((tm, K), lambda i: (i, 0)), pl.BlockSpec((None, K, N), lambda i: (0, 0, 0), pipeline_mode=pl.Buffered(1)), row,
                           pl.BlockSpec((1, N), lambda i: (0, 0))],
                 out_specs=[row, row], out_shape=[SDS((M, N), F32), SDS((M, N), BF)], semantics=("parallel",), name=name, side=side)


def _mm_rms_bwd(a_parts, b, x, g, dres, name, side=None):
    n_a = len(a_parts)
    M = a_parts[0].shape[-2]
    N, K = b.shape[1:]
    assert K == sum(p.shape[-1] * (p.shape[0] if p.ndim == 3 else 1) for p in a_parts)
    tm = _row_tile(M, K, N, K * 2 + N * (4 + 4 + 4 + 2))

    def body(*refs):
        a_refs, (b_ref, x_ref, g_ref, r_ref, dx_ref, dxb_ref, dg_ref) = refs[:n_a], refs[n_a:]
        dh, col = None, 0
        for p, a_ref in zip(a_parts, a_refs):
            for blk in ([a_ref[h] for h in range(p.shape[0])] if p.ndim == 3 else [a_ref[...]]):
                t = lax.dot_general(blk, b_ref[:, col:col + p.shape[-1]], _DN["nt"], preferred_element_type=F32)
                dh = t if dh is None else dh + t
                col += p.shape[-1]
        _, vjp = jax.vjp(_rms, x_ref[...], g_ref[...])
        dx, dg = vjp(dh)
        dx = dx + r_ref[...]
        dx_ref[...] = dx
        dxb_ref[...] = dx.astype(BF)

        @pl.when(pl.program_id(0) == 0)
        def _():
            dg_ref[...] = jnp.zeros_like(dg_ref)

        dg_ref[...] += dg

    row = pl.BlockSpec((tm, N), lambda i: (i, 0))
    vec = pl.BlockSpec((1, N), lambda i: (0, 0))
    a_specs = [pl.BlockSpec((p.shape[0], tm, p.shape[2]), lambda i: (0, i, 0)) if p.ndim == 3 else pl.BlockSpec((tm, p.shape[1]), lambda i: (i, 0))
               for p in a_parts]
    return _call(
        body, (*a_parts, b, x, g.reshape(1, N), dres), grid=(M // tm,),
        in_specs=a_specs + [pl.BlockSpec((None, N, K), lambda i: (0, 0, 0), pipeline_mode=pl.Buffered(1)), row, vec, row],
        out_specs=[row, row, vec], out_shape=[SDS((M, N), F32), SDS((M, N), BF), SDS((1, N), F32)],
        semantics=("arbitrary",), name=name, side=side)


def _loss_bwd(x, g, target, name):
    T, D = x.shape
    tb = _tile(T, ROW_BLOCK, 8)
    nb = T // tb

    def body(x_ref, g_ref, t_ref, loss_ref, dx_ref, dg_ref, acc):
        i = pl.program_id(0)
        y, vjp = jax.vjp(_rms, x_ref[...], g_ref[...])
        err = y - t_ref[...]
        dx, dg = vjp(err * (1.0 / D))
        dx_ref[...] = dx

        @pl.when(i == 0)
        def _():
            dg_ref[...] = jnp.zeros_like(dg_ref)
            acc[...] = jnp.zeros_like(acc)

        dg_ref[...] += dg
        acc[...] += jnp.sum(err * err, axis=0, keepdims=True)

        @pl.when(i == nb - 1)
        def _():
            loss_ref[...] = jnp.full(loss_ref.shape, (0.5 / D) * jnp.sum(acc[...]), F32)

    row = pl.BlockSpec((tb, D), lambda i: (i, 0))
    vec = pl.BlockSpec((1, D), lambda i: (0, 0))
    return pl.pallas_call(
        body, grid=(nb,), in_specs=[row, vec, row], out_specs=[pl.BlockSpec((1, LANES), lambda i: (0, 0)), row, vec],
        out_shape=[SDS((1, LANES), F32), SDS((T, D), F32), SDS((1, D), F32)], scratch_shapes=[pltpu.VMEM((1, D), F32)],
        compiler_params=_params("arbitrary"), name=name)(x, g.reshape(1, D), target)


def _ln_silu_mm(cv, g, b, w, name):
    T, C = cv.shape
    D = w.shape[2]
    tb = _tile(T, 2 * ROW_BLOCK, 8)

    def body(x_ref, g_ref, b_ref, w_ref, y1_ref, y_ref):
        y1 = _ln_silu(x_ref[...], g_ref[...], b_ref[...]).astype(BF)
        y1_ref[...] = y1
        y_ref[...] = lax.dot_general(y1, w_ref[...], _DN["nn"], preferred_element_type=F32).astype(BF)

    row = pl.BlockSpec((tb, C), lambda i: (i, 0))
    vec = pl.BlockSpec((1, C), lambda i: (0, 0))
    return pl.pallas_call(
        body, grid=(T // tb,), in_specs=[row, vec, vec, pl.BlockSpec((None, C, D), lambda i: (0, 0, 0), pipeline_mode=pl.Buffered(1))],
        out_specs=[row, pl.BlockSpec((tb, D), lambda i: (i, 0))], out_shape=[SDS((T, C), BF), SDS((T, D), BF)],
        compiler_params=_params("parallel"), name=name)(cv, g.reshape(1, C), b.reshape(1, C), w)


def _mm_ln_silu_bwd(dyc, w, cv, g, b, name, side=None):
    T, C = cv.shape
    D = w.shape[2]
    tb = _tile(T, 2 * ROW_BLOCK, 8)

    def body(d_ref, w_ref, x_ref, g_ref, b_ref, dx_ref, dg_ref, db_ref):
        dy1 = lax.dot_general(d_ref[...], w_ref[...], _DN["nt"], preferred_element_type=F32)
        _, vjp = jax.vjp(_ln_silu, x_ref[...], g_ref[...], b_ref[...])
        dx, dg, db = vjp(dy1)
        dx_ref[...] = dx

        @pl.when(pl.program_id(0) == 0)
        def _():
            dg_ref[...] = jnp.zeros_like(dg_ref)
            db_ref[...] = jnp.zeros_like(db_ref)

        dg_ref[...] += dg
        db_ref[...] += db

    row = pl.BlockSpec((tb, C), lambda i: (i, 0))
    vec = pl.BlockSpec((1, C), lambda i: (0, 0))
    return _call(
        body, (dyc, w, cv, g.reshape(1, C), b.reshape(1, C)), grid=(T // tb,),
        in_specs=[pl.BlockSpec((tb, D), lambda i: (i, 0)), pl.BlockSpec((None, C, D), lambda i: (0, 0, 0), pipeline_mode=pl.Buffered(1)), row, vec, vec],
        out_specs=[row, vec, vec], out_shape=[SDS((T, C), F32), SDS((1, C), F32), SDS((1, C), F32)], semantics=("arbitrary",),
        name=name, side=side)


def _merge_fwd(proj, yc, yp, ps, C, name, side=None):
    T, D = yc.shape
    tb = _tile(T, ROW_BLOCK, 8)
    nj = D // C

    def body(gc_ref, gp_ref, yc_ref, yp_ref, ps_ref, o_ref):
        o_ref[...] = _merge(gc_ref[...], gp_ref[...], yc_ref[...].astype(F32), yp_ref[...].astype(F32), ps_ref[...]).astype(BF)

    blk = pl.BlockSpec((tb, C), lambda i, j: (i, j))
    return _call1(
        body, (proj, proj, yc, yp, ps.reshape(1, D)), grid=(T // tb, nj),
        in_specs=[pl.BlockSpec((tb, C), lambda i, j: (i, 3 + j)), pl.BlockSpec((tb, C), lambda i, j: (i, 3 + nj + j)), blk, blk,
                  pl.BlockSpec((1, C), lambda i, j: (0, j))],
        out_spec=blk, out_shape=SDS((T, D), BF), semantics=("parallel", "parallel"), name=name, side=side)


def _merge_bwd(proj, yc, yp, ps, dm, C, name, side=None):
    T, D = yc.shape
    tb = _tile(T, ROW_BLOCK, 8)
    nj = D // C

    def body(gc_ref, gp_ref, yc_ref, yp_ref, ps_ref, dm_ref, dg_ref, dyc_ref, dyp_ref, dps_ref):
        _, vjp = jax.vjp(_merge, gc_ref[...], gp_ref[...], yc_ref[...].astype(F32), yp_ref[...].astype(F32), ps_ref[...])
        dgc, dgp, dyc, dyp, dps = vjp(dm_ref[...].astype(F32))
        dg_ref[0] = dgc.astype(BF)
        dg_ref[1] = dgp.astype(BF)
        dyc_ref[...] = dyc.astype(BF)
        dyp_ref[...] = dyp.astype(BF)

        @pl.when(pl.program_id(1) == 0)
        def _():
            dps_ref[...] = jnp.zeros_like(dps_ref)

        dps_ref[...] += dps

    blk = pl.BlockSpec((tb, C), lambda j, i: (i, j))
    vec = pl.BlockSpec((1, C), lambda j, i: (0, j))
    return _call(
        body, (proj, proj, yc, yp, ps.reshape(1, D), dm), grid=(nj, T // tb),
        in_specs=[pl.BlockSpec((tb, C), lambda j, i: (i, 3 + j)), pl.BlockSpec((tb, C), lambda j, i: (i, 3 + nj + j)), blk, blk, vec, blk],
        out_specs=[pl.BlockSpec((2, tb, C), lambda j, i: (0, i, j)), blk, blk, vec],
        out_shape=[SDS((2, T, D), BF), SDS((T, D), BF), SDS((T, D), BF), SDS((1, D), F32)],
        semantics=("parallel", "arbitrary"), name=name, side=side)


def _shd(v, s, rows):
    if s == 0:
        return v
    return jnp.where(rows >= s, pltpu.roll(v, s, 0), 0.0)


def _shu(v, s, rows):
    if s == 0:
        return v
    n = v.shape[0]
    return jnp.where(rows < n - s, pltpu.roll(v, n - s, 0), 0.0)


def _glu_conv_fwd(proj, w, b, Bn, S, C, name, side=None):
    K = w.shape[0]
    sl = min(LANES, C)
    ns = C // sl

    def body(a_ref, gl_ref, w_ref, b_ref, o_ref):
        y0 = a_ref[...] * jax.nn.sigmoid(gl_ref[...])
        rows = lax.broadcasted_iota(jnp.int32, y0.shape, 0)
        acc = jnp.zeros_like(y0) + b_ref[...]
        for k in range(K):
            acc = acc + w_ref[k:k + 1, :] * _shd(y0, K - 1 - k, rows)
        o_ref[...] = acc

    return _call1(
        body, (proj, proj, w, b.reshape(1, C)), grid=(Bn, ns),
        in_specs=[pl.BlockSpec((S, sl), lambda bi, j: (bi, j)), pl.BlockSpec((S, sl), lambda bi, j: (bi, ns + j)),
                  pl.BlockSpec((K, sl), lambda bi, j: (0, j)), pl.BlockSpec((1, sl), lambda bi, j: (0, j))],
        out_spec=pl.BlockSpec((S, sl), lambda bi, j: (bi, j)), out_shape=SDS((Bn * S, C), F32),
        semantics=("parallel", "parallel"), name=name, side=side)


def _glu_conv_bwd(proj, w, dcv, Bn, S, C, name, side=None):
    K = w.shape[0]
    sl = min(LANES, C)
    ns = C // sl

    def body(a_ref, gl_ref, w_ref, d_ref, dagl_ref, dw_ref, db_ref):
        a = a_ref[...]
        sg = jax.nn.sigmoid(gl_ref[...])
        y0 = a * sg
        d = d_ref[...]
        rows = lax.broadcasted_iota(jnp.int32, y0.shape, 0)

        @pl.when(pl.program_id(1) == 0)
        def _():
            dw_ref[...] = jnp.zeros_like(dw_ref)
            db_ref[...] = jnp.zeros_like(db_ref)

        dy0 = jnp.zeros_like(y0)
        for k in range(K):
            s = K - 1 - k
            dw_ref[k:k + 1, :] += jnp.sum(d * _shd(y0, s, rows), axis=0, keepdims=True)
            dy0 = dy0 + w_ref[k:k + 1, :] * _shu(d, s, rows)
        db_ref[...] += jnp.sum(d, axis=0, keepdims=True)
        dagl_ref[0] = (dy0 * sg).astype(BF)
        dagl_ref[1] = (dy0 * a * sg * (1.0 - sg)).astype(BF)

    blk = pl.BlockSpec((S, sl), lambda j, bi: (bi, j))
    return _call(
        body, (proj, proj, w, dcv), grid=(ns, Bn),
        in_specs=[blk, pl.BlockSpec((S, sl), lambda j, bi: (bi, ns + j)), pl.BlockSpec((K, sl), lambda j, bi: (0, j)), blk],
        out_specs=[pl.BlockSpec((2, S, sl), lambda j, bi: (0, bi, j)), pl.BlockSpec((K, sl), lambda j, bi: (0, j)),
                   pl.BlockSpec((1, sl), lambda j, bi: (0, j))],
        out_shape=[SDS((2, Bn * S, C), BF), SDS((K, C), F32), SDS((1, C), F32)],
        semantics=("parallel", "arbitrary"), name=name, side=side)


def _pool_z(u, g, rows):
    s2 = u + _shd(u, 1, rows)
    s4 = s2 + _shd(s2, 2, rows)
    s8 = s4 + _shd(s4, 4, rows)
    s16 = s8 + _shd(s8, 8, rows)
    sw = jnp.where(g == 0, s2, jnp.where(g == 1, s4, jnp.where(g == 2, s8, s16)))
    cnt = jnp.minimum(rows + 1, POOL_WINDOWS[0] << g).astype(F32)
    return sw / cnt - u, cnt


def _pool_fwd(proj, wpt, l, Bn, S, C, D, name):
    G = len(POOL_WINDOWS)
    gd, go = C // G, D // G

    def body(u_ref, w_ref, o_ref):
        g = pl.program_id(1)
        u = u_ref[...]
        rows = lax.broadcasted_iota(jnp.int32, u.shape, 0)
        zp, _ = _pool_z(u, g, rows)
        o_ref[...] = lax.dot_general(zp.astype(BF), w_ref[...], _DN["nt"], preferred_element_type=F32).astype(BF)

    return pl.pallas_call(
        body, grid=(Bn, G),
        in_specs=[pl.BlockSpec((S, gd), lambda bi, g: (bi, 2 * G + g)), pl.BlockSpec((None, go, gd), lambda bi, g: (l * G + g, 0, 0))],
        out_specs=pl.BlockSpec((S, go), lambda bi, g: (bi, g)), out_shape=SDS((Bn * S, D), BF),
        compiler_params=_params("parallel", "parallel"), name=name)(proj, wpt)


def _pool_bwd(proj, wpt, dyp, l, Bn, S, C, D, name):
    G = len(POOL_WINDOWS)
    gd, go = C // G, D // G

    def body(u_ref, w_ref, d_ref, du_ref, dw_ref):
        g = pl.program_id(0)
        u = u_ref[...]
        rows = lax.broadcasted_iota(jnp.int32, u.shape, 0)
        zp, cnt = _pool_z(u, g, rows)
        d = d_ref[...]
        dzp = lax.dot_general(d, w_ref[...], _DN["nn"], preferred_element_type=F32)

        @pl.when(pl.program_id(1) == 0)
        def _():
            dw_ref[...] = jnp.zeros_like(dw_ref)

        dw_ref[...] += lax.dot_general(d, zp.astype(BF), _DN["tn"], preferred_element_type=F32)
        dsw = dzp / cnt
        zero = jnp.zeros_like(dsw)
        d16 = jnp.where(g == 3, dsw, zero)
        d8 = jnp.where(g == 2, dsw, zero) + d16 + _shu(d16, 8, rows)
        d4 = jnp.where(g == 1, dsw, zero) + d8 + _shu(d8, 4, rows)
        d2 = jnp.where(g == 0, dsw, zero) + d4 + _shu(d4, 2, rows)
        d1 = d2 + _shu(d2, 1, rows)
        du_ref[...] = (d1 - dzp).astype(BF)

    return pl.pallas_call(
        body, grid=(G, Bn),
        in_specs=[pl.BlockSpec((S, gd), lambda g, bi: (bi, 2 * G + g)), pl.BlockSpec((None, go, gd), lambda g, bi: (l * G + g, 0, 0)),
                  pl.BlockSpec((S, go), lambda g, bi: (bi, g))],
        out_specs=[pl.BlockSpec((S, gd), lambda g, bi: (bi, g)), pl.BlockSpec((None, go, gd), lambda g, bi: (g, 0, 0))],
        out_shape=[SDS((Bn * S, C), BF), SDS((G, go, gd), F32)],
        compiler_params=_params("parallel", "arbitrary"), name=name)(proj, wpt, dyp)


def _ffn_conv(u, w_ref, rows):
    K = w_ref.shape[0]
    acc = w_ref[K - 1:K, :] * u
    for k in range(K - 1):
        acc = acc + w_ref[k:k + 1, :] * _shd(u, K - 1 - k, rows)
    return acc


def _ffn_cb(F):
    return _tile(F, 256)


def _ffn_act_fwd(up0, w, Bn, S, F, name, side=None):
    cb = _ffn_cb(F)
    nj = F // cb

    def body(g_ref, v_ref, wg_ref, wv_ref, o_ref):
        rows = lax.broadcasted_iota(jnp.int32, g_ref.shape, 0)
        o_ref[...] = _gated(_ffn_conv(g_ref[...], wg_ref, rows), _ffn_conv(v_ref[...], wv_ref, rows)).astype(BF)

    K = w.shape[0]
    return _call1(
        body, (up0, up0, w, w), grid=(Bn, nj),
        in_specs=[pl.BlockSpec((S, cb), lambda bi, j: (bi, j)), pl.BlockSpec((S, cb), lambda bi, j: (bi, nj + j)),
                  pl.BlockSpec((K, cb), lambda bi, j: (0, j)), pl.BlockSpec((K, cb), lambda bi, j: (0, nj + j))],
        out_spec=pl.BlockSpec((S, cb), lambda bi, j: (bi, j)), out_shape=SDS((Bn * S, F), BF),
        semantics=("parallel", "parallel"), name=name, side=side)


SUBLANES = 8
FFN_HALO = SUBLANES
FFN_ROWS = 128
GELU_C0, GELU_C1 = 0.7978845608028654, 0.044715


def _gelu_and_grad(x):
    x2 = x * x
    t = jnp.tanh(GELU_C0 * (x + GELU_C1 * (x2 * x)))
    cdf = 0.5 * (1.0 + t)
    return x * cdf, cdf + (0.5 * GELU_C0) * x * (1.0 - t * t) * (1.0 + (3.0 * GELU_C1) * x2)


def _ffn_act_bwd(up0, w, dg, Bn, S, F, name, side=None):
    cb = min(LANES, F)
    nj = F // cb
    K = w.shape[0]
    rc = FFN_ROWS if S % FFN_ROWS == 0 else S
    win = rc + 2 * FFN_HALO
    assert K - 1 <= FFN_HALO and rc % SUBLANES == 0

    def body(g_ref, v_ref, wg_ref, wv_ref, d_ref, do_ref, dwg_ref, dwv_ref, gp, vp, dp):
        for pad, src in ((gp, g_ref), (vp, v_ref), (dp, d_ref)):
            pad[0:FFN_HALO, :] = jnp.zeros((FFN_HALO, cb), F32)
            pad[FFN_HALO + S:, :] = jnp.zeros((FFN_HALO, cb), F32)
            pad[FFN_HALO:FFN_HALO + S, :] = src[...].astype(F32)
        wg = [wg_ref[k:k + 1, :] for k in range(K)]
        wv = [wv_ref[k:k + 1, :] for k in range(K)]

        def taps(u):
            return [pltpu.roll(u, K - 1 - k, 0) for k in range(K - 1)] + [u]

        def conv(us, ws):
            acc = ws[K - 1] * us[K - 1]
            for k in range(K - 1):
                acc = acc + ws[k] * us[k]
            return acc

        def conv_t(dc, ws):
            acc = ws[K - 1] * dc
            for k in range(K - 1):
                acc = acc + ws[k] * pltpu.roll(dc, win - (K - 1 - k), 0)
            return acc

        def fold(t):
            acc = t[FFN_HALO:FFN_HALO + SUBLANES]
            for i in range(1, rc // SUBLANES):
                acc = acc + t[FFN_HALO + SUBLANES * i:FFN_HALO + SUBLANES * (i + 1)]
            return acc

        def chunk(c, sums):
            r0 = pl.multiple_of(c * rc, SUBLANES)
            gs, vs, d = taps(gp[pl.ds(r0, win), :]), taps(vp[pl.ds(r0, win), :]), dp[pl.ds(r0, win), :]
            ge, dge = _gelu_and_grad(conv(gs, wg))
            dgc = d * conv(vs, wv) * dge
            dvc = d * ge
            do_ref[0, pl.ds(r0, rc), :] = conv_t(dgc, wg)[FFN_HALO:FFN_HALO + rc].astype(BF)
            do_ref[1, pl.ds(r0, rc), :] = conv_t(dvc, wv)[FFN_HALO:FFN_HALO + rc].astype(BF)
            new = [fold(dc * u) for us, dc in ((gs, dgc), (vs, dvc)) for u in us]
            return tuple(a + b for a, b in zip(sums, new))

        sums = lax.fori_loop(0, S // rc, chunk, tuple(jnp.zeros((SUBLANES, cb), F32) for _ in range(2 * K)))

        @pl.when(pl.program_id(1) == 0)
        def _():
            dwg_ref[...] = jnp.zeros_like(dwg_ref)
            dwv_ref[...] = jnp.zeros_like(dwv_ref)

        for k in range(K):
            dwg_ref[k:k + 1, :] += jnp.sum(sums[k], axis=0, keepdims=True)
            dwv_ref[k:k + 1, :] += jnp.sum(sums[K + k], axis=0, keepdims=True)

    blk = pl.BlockSpec((S, cb), lambda j, bi: (bi, j))
    wblk = pl.BlockSpec((K, cb), lambda j, bi: (0, j))
    return _call(
        body, (up0, up0, w, w, dg), grid=(nj, Bn),
        in_specs=[blk, pl.BlockSpec((S, cb), lambda j, bi: (bi, nj + j)), wblk, pl.BlockSpec((K, cb), lambda j, bi: (0, nj + j)), blk],
        out_specs=[pl.BlockSpec((2, S, cb), lambda j, bi: (0, bi, j)), wblk, wblk],
        out_shape=[SDS((2, Bn * S, F), BF), SDS((K, F), F32), SDS((K, F), F32)],
        scratch_shapes=[pltpu.VMEM((S + 2 * FFN_HALO, cb), F32)] * 3, semantics=("parallel", "arbitrary"), name=name, side=side)


def _softmax_rows(q, k, scale):
    sc = lax.dot_general(q, k, _DN["nt"], preferred_element_type=F32) * scale
    e = jnp.exp(sc - jnp.max(sc, axis=-1, keepdims=True))
    return e / jnp.sum(e, axis=-1, keepdims=True)


def _attn_ts(S):
    return _tile(S, 1024, 8)


def _attn_fwd(q, kv, Bn, S, Mn, D, name, side=None):
    H = XA_HEADS
    dh = D // H
    ts = _attn_ts(S)
    nsb = S // ts
    scale = dh ** -0.5

    def body(q_ref, k_ref, v_ref, o_ref):
        p = _softmax_rows(q_ref[...], k_ref[...], scale)
        o_ref[...] = lax.dot_general(p.astype(BF), v_ref[...], _DN["nn"], preferred_element_type=F32).astype(BF)

    qblk = pl.BlockSpec((ts, dh), lambda bi, h, s: (bi * nsb + s, h))
    return _call1(
        body, (q, kv, kv), grid=(Bn, H, nsb),
        in_specs=[qblk, pl.BlockSpec((Mn, dh), lambda bi, h, s: (bi, h)), pl.BlockSpec((Mn, dh), lambda bi, h, s: (bi, H + h))],
        out_spec=qblk, out_shape=SDS((Bn * S, D), BF), semantics=("parallel", "parallel", "parallel"), name=name, side=side)


def _attn_bwd(q, kv, datt, Bn, S, Mn, D, name):
    H = XA_HEADS
    dh = D // H
    ts = _attn_ts(S)
    nsb = S // ts
    scale = dh ** -0.5

    def body(q_ref, k_ref, v_ref, do_ref, dq_ref, dk_ref, dv_ref):
        q, k, v, do = q_ref[...], k_ref[...], v_ref[...], do_ref[...]
        p = _softmax_rows(q, k, scale)
        dp = lax.dot_general(do, v, _DN["nt"], preferred_element_type=F32)
        ds = (p * (dp - jnp.sum(dp * p, axis=-1, keepdims=True)) * scale).astype(BF)
        dq_ref[...] = lax.dot_general(ds, k, _DN["nn"], preferred_element_type=F32).astype(BF)

        @pl.when(pl.program_id(2) == 0)
        def _():
            dk_ref[...] = jnp.zeros_like(dk_ref)
            dv_ref[...] = jnp.zeros_like(dv_ref)

        dk_ref[...] += lax.dot_general(ds, q, _DN["tn"], preferred_element_type=F32)
        dv_ref[...] += lax.dot_general(p.astype(BF), do, _DN["tn"], preferred_element_type=F32)

    qblk = pl.BlockSpec((ts, dh), lambda bi, h, s: (bi * nsb + s, h))
    kblk = pl.BlockSpec((Mn, dh), lambda bi, h, s: (bi, h))
    return pl.pallas_call(
        body, grid=(Bn, H, nsb),
        in_specs=[qblk, kblk, pl.BlockSpec((Mn, dh), lambda bi, h, s: (bi, H + h)), qblk],
        out_specs=[qblk, kblk, kblk], out_shape=[SDS((Bn * S, D), BF), SDS((Bn * Mn, D), F32), SDS((Bn * Mn, D), F32)],
        compiler_params=_params("parallel", "parallel", "arbitrary"), name=name)(q, kv, kv, datt)


class _Sides:
    def __init__(self, by_key=None, on_land=None):
        self.by_key, self.landed, self.on_land = dict(by_key or {}), {}, on_land

    def run(self, key, fn, *args, **kw):
        side = self.by_key.get(key)
        if side is None:
            return fn(*args, **kw)
        out, self.landed[key] = fn(*args, side=side() if callable(side) else side, **kw)
        if self.on_land is not None:
            self.on_land(key, self.landed[key])
        return out

    def mm(self, key, *args, **kw):
        return self.run(key, _mm, *args, **kw)


def _layer_fwd(x, h, mem_n, W, V, l, dims, sides, next_g):
    Bn, S, Mn, D, C, F = dims
    n = f"l{l}_"
    proj = sides.mm("proj", h, W["w_in"], "nn", F32, n + "proj", bl=0)
    cv = sides.run("glu_conv", _glu_conv_fwd, proj, V["conv_dw_w"][l], V["conv_dw_b"][l], Bn, S, C, n + "glu_conv")
    yc1, yc = _ln_silu_mm(cv, V["conv_ln_g"][l], V["conv_ln_b"][l], W["w_conv_out"], n + "conv_out")
    yp = _pool_fwd(proj, W["w_pool"], 0, Bn, S, C, D, n + "pool")
    merged = sides.run("merge", _merge_fwd, proj, yc, yp, V["pool_scale"][l], C, n + "merge")
    x1, hq = sides.run("out_proj", _mm_rms_fwd, merged, W["w_out"], x, V["xattn_norm_g"][l], n + "out_proj")
    q = sides.mm("q_proj", hq, W["w_q"], "nn", BF, n + "q_proj", bl=0)
    kv = _mm(mem_n, W["w_kv"], "nn", BF, n + "kv_proj", bl=0)
    att = sides.run("attn", _attn_fwd, q, kv, Bn, S, Mn, D, n + "attn")
    x2, hf = sides.run("o_proj", _mm_rms_fwd, att, W["w_o"], x1, V["ffn_norm_g"][l], n + "o_proj")
    up0 = sides.mm("up_proj", hf, W["w_up"], "nn", F32, n + "up_proj", bl=0)
    gact = sides.run("ffn_act", _ffn_act_fwd, up0, V["ffn_dw_w"][l], Bn, S, F, n + "ffn_act")
    if next_g is not None:
        x3, h3 = sides.run("down_proj", _mm_rms_fwd, gact, W["w_down"], x2, next_g, n + "down_proj")
    else:
        x3, h3 = sides.mm("down_proj", gact, W["w_down"], "nn", F32, n + "down_proj", res=x2, bl=0), None
    return x3, h3, dict(x=x, h=h, proj=proj, cv=cv, yc1=yc1, yc=yc, yp=yp, merged=merged, x1=x1, hq=hq, q=q, kv=kv, att=att, x2=x2,
                        hf=hf, up0=up0, gact=gact)


def _layer_bwd_mlp(dx, dxb, sv, W, V, l, dims, sides):
    Bn, S, Mn, D, C, F = dims
    n = f"l{l}_b_"
    gw, sm = {}, {}
    dgact = sides.mm("d_gact", dxb, W["w_down"], "nt", BF, n + "d_gact", bl=0)
    gw["w_down"] = sides.mm("dw_down", sv["gact"], dxb, "tn", F32, n + "dw_down", twin=BF)
    dup0, dwg, dwv = sides.run("ffn_act_b", _ffn_act_bwd, sv["up0"], V["ffn_dw_w"][l], dgact, Bn, S, F, n + "ffn_act")
    sm["ffn_dw_w"] = jnp.concatenate([dwg, dwv], axis=1)
    dx2, dx2b, sm["ffn_norm_g"] = _mm_rms_bwd([dup0], W["w_up"], sv["x2"], V["ffn_norm_g"][l], dx, n + "d_hf")
    gw["w_up"] = sides.mm("dw_up", sv["hf"], dup0, "tn", F32, n + "dw_up", twin=BF, b_halves=True)
    return dx2, dx2b, gw, sm


def _layer_bwd_mix(dx2, dx2b, dmem_n, sv, mem_n, W, V, l, dims, sides, gw):
    Bn, S, Mn, D, C, F = dims
    n = f"l{l}_b_"
    sm = {}
    datt = sides.mm("d_att", dx2b, W["w_o"], "nt", BF, n + "d_att", bl=0)
    gw["w_o"] = _mm(sv["att"], dx2b, "tn", F32, n + "dw_o", twin=BF)
    dq, dk, dv = _attn_bwd(sv["q"], sv["kv"], datt, Bn, S, Mn, D, n + "attn")
    dkv = jnp.concatenate([dk, dv], axis=1)
    gw["w_kv"] = _mm(mem_n, dkv, "tn", F32, n + "dw_kv", twin=BF)
    dmem_n = _mm(dkv, W["w_kv"], "nt", F32, n + "d_mem", res=dmem_n, bl=0)
    dx1, dx1b, sm["xattn_norm_g"] = _mm_rms_bwd([dq], W["w_q"], sv["x1"], V["xattn_norm_g"][l], dx2, n + "d_hq")
    gw["w_q"] = _mm(sv["hq"], dq, "tn", F32, n + "dw_q", twin=BF)
    dmerged = sides.mm("d_merged", dx1b, W["w_out"], "nt", BF, n + "d_merged", bl=0)
    gw["w_out"] = _mm(sv["merged"], dx1b, "tn", F32, n + "dw_out", twin=BF)
    dgates, dyc, dyp, sm["pool_scale"] = sides.run("merge_b", _merge_bwd, sv["proj"], sv["yc"], sv["yp"], V["pool_scale"][l], dmerged, C, n + "merge")
    du, dwp = _pool_bwd(sv["proj"], W["w_pool"], dyp, 0, Bn, S, C, D, n + "pool")
    gw["w_pool"] = (dwp, dwp.astype(BF))
    gw["w_conv_out"] = _mm(sv["yc1"], dyc, "tn", F32, n + "dw_conv_out", twin=BF)
    dcv, sm["conv_ln_g"], sm["conv_ln_b"] = sides.run("ln_silu_b", _mm_ln_silu_bwd, dyc, W["w_conv_out"], sv["cv"], V["conv_ln_g"][l],
                                                      V["conv_ln_b"][l], n + "d_yc1")
    dagl, sm["conv_dw_w"], sm["conv_dw_b"] = sides.run("glu_conv_b", _glu_conv_bwd, sv["proj"], V["conv_dw_w"][l], dcv, Bn, S, C, n + "glu_conv")
    dx, dxb, sm["mix_norm_g"] = sides.run("d_h", _mm_rms_bwd, [dagl, du, dgates], W["w_in"], sv["x"], V["mix_norm_g"][l], dx1, n + "d_h")
    n_in = W["w_in"].shape[2]
    part = _mm(sv["h"], dagl, "tn", F32, n + "dw_in_conv", twin=BF, b_halves=True, part=(n_in, 0, None))
    part = _mm(sv["h"], du, "tn", F32, n + "dw_in_pool", twin=BF, part=(n_in, 2 * C, part))
    gw["w_in"] = sides.mm("dw_in", sv["h"], dgates, "tn", F32, n + "dw_in", twin=BF, b_halves=True, part=(n_in, 3 * C, part))
    return dx, dxb, dmem_n, sm


BIG = (("w_in", "col"), ("w_conv_out", "col"), ("w_pool", "row"), ("w_out", "row"), ("w_q", "row"), ("w_kv", "col"),
       ("w_o", "row"), ("w_up", "col"), ("w_down", "row"))
ALL_RELS = (1, 2, 3)
GATHER_FIRST = ("w_in", "w_conv_out", "w_pool", "w_out")
FWD_CARRY = {
    (0, "proj"): (("w_up", 0, (1, 2)), ("w_q", 0, ALL_RELS), ("w_o", 0, ALL_RELS)),
    (0, "glu_conv"): (("w_kv", 0, ALL_RELS),),
    (0, "merge"): (("w_up", 0, (3,)),),
    (0, "q_proj"): (("w_down", 0, (1, 2)),),
    (0, "attn"): (("w_down", 0, (3,)),),
    (0, "up_proj"): (("w_in", 1, ALL_RELS), ("w_conv_out", 1, ALL_RELS), ("w_pool", 1, ALL_RELS), ("w_o", 1, ALL_RELS)),
    (0, "ffn_act"): (("w_out", 1, ALL_RELS), ("w_q", 1, ALL_RELS), ("w_kv", 1, ALL_RELS)),
    (1, "proj"): (("w_up", 1, (1, 2)),),
    (1, "glu_conv"): (("w_down", 1, (1, 2)),),
    (1, "merge"): (("w_up", 1, (3,)),),
    (1, "attn"): (("w_down", 1, (3,)),),
}
PASS_CARRY = {
    (0, "out_proj"): (("w_kv", 0), ("w_q", 0), ("w_o", 0)),
    (0, "o_proj"): (("w_up", 0), ("w_down", 0)),
    (0, "down_proj"): (("w_in", 1), ("w_conv_out", 1), ("w_pool", 1), ("w_out", 1), ("w_q", 1), ("w_kv", 1), ("w_o", 1)),
    (1, "o_proj"): (("w_up", 1), ("w_down", 1)),
}
EARLY = ("w_down", "w_up")
BWD_CARRY_EARLY = {"merge_b": ("w_down",), "glu_conv_b": ("w_up",)}
BWD_CARRY_LATE = {"ffn_act_b": ("w_in", "w_conv_out", "w_pool", "w_out", "w_q", "w_kv", "w_o")}
BWD_LAST_LAYER = (("att", ("w_o", "w_kv", "w_q"), "d_merged", {"d_h": ("w_o", "w_kv", "w_q")}),
                  ("tok", ("w_out", "w_pool", "w_conv_out"), "ln_silu_b", {"dw_in": ("w_out", "w_pool", "w_conv_out")}))


def _place():
    xi, yi, ci = lax.axis_index("x"), lax.axis_index("y"), lax.axis_index("c")
    return xi, yi, ci, 2 * xi + yi


def _chip_peer(xi, yi, ci, r):
    return (xi ^ (r >> 1), yi ^ (r & 1), ci)


def _full_shard(ref, kind, k, cs):
    if kind == "col":
        return ref.at[:, :, :, :, pl.ds(pl.multiple_of(k * cs, cs), cs)]
    return ref.at[:, :, k]


def _gather_weights(shards, kinds):
    n = len(shards)
    outs = []
    for s, kind in zip(shards, kinds):
        L, P, _, RH, CS = s.shape
        outs.append(SDS((L, P, 2, RH, CS * N_CHIPS) if kind == "col" else (L, P, N_CHIPS, 2, RH, CS), s.dtype))
    per = 7

    def body(*refs):
        srcs, fulls, (ssem, rsem) = refs[:n], refs[n:2 * n], refs[2 * n:]
        xi, yi, ci, j = _place()
        sib = (xi, yi, 1 - ci)

        def piece(i, k, c):
            kind, cs = kinds[i], shards[i].shape[-1]
            if kind == "col":
                return fulls[i].at[:, :, c, :, pl.ds(pl.multiple_of(k * cs, cs), cs)]
            return fulls[i].at[:, :, k, c]

        def copy(i, slot, src, dst, dev):
            return pltpu.make_async_remote_copy(src_ref=src, dst_ref=dst, send_sem=ssem.at[per * i + slot], recv_sem=rsem.at[per * i + slot],
                                                device_id=dev, device_id_type=MESH)

        own, first, passed = [], [], []
        for i in range(n):
            for r in (1, 2, 3):
                first.append(copy(i, r - 1, srcs[i].at[:, :, ci], piece(i, j, ci), _chip_peer(xi, yi, ci, r)))
                first[-1].start()
        for i in range(n):
            own.append(copy(i, 6, srcs[i], _full_shard(fulls[i], kinds[i], j, shards[i].shape[-1]), sib))
            own[-1].start()
        for i in range(n):
            for r in (1, 2, 3):
                got = piece(i, j ^ r, ci)
                copy(i, r - 1, got, got, sib).wait_recv()
                passed.append(copy(i, 2 + r, got, got, sib))
                passed[-1].start()
        for i in range(n):
            for r in (1, 2, 3):
                got = piece(i, j ^ r, 1 - ci)
                copy(i, 2 + r, got, got, sib).wait_recv()
        for cp in own:
            cp.wait()
        for cp in first + passed:
            cp.wait_send()

    return pl.pallas_call(
        body, in_specs=[ANY] * n, out_specs=[ANY] * n, out_shape=outs,
        scratch_shapes=[pltpu.SemaphoreType.DMA((per * n,)), pltpu.SemaphoreType.DMA((per * n,))], name="gather_weights")(*shards)


def _full_sds(s, kind):
    L, P, _, RH, CS = s.shape
    return SDS((L, P, 2, RH, CS * N_CHIPS) if kind == "col" else (L, P, N_CHIPS, 2, RH, CS), s.dtype)


def _gather_piece(full, kind, cs, k, c):
    if kind == "col":
        return full.at[:, :, c, :, pl.ds(pl.multiple_of(k * cs, cs), cs)]
    return full.at[:, :, k, c]


def _side_gather(shards, kinds, rels, fulls):
    n = len(shards)

    def make(srcs, outs, ssem, rsem):
        xi, yi, ci, j = _place()
        return [pltpu.make_async_remote_copy(
            src_ref=srcs[i].at[:, :, ci], dst_ref=_gather_piece(outs[i], kinds[i], shards[i].shape[-1], j, ci), send_sem=ssem.at[3 * i + r - 1],
            recv_sem=rsem.at[3 * i + r - 1], device_id=_chip_peer(xi, yi, ci, r), device_id_type=MESH) for i in range(n) for r in rels[i]]

    prior = [f for f in fulls if f is not None]
    assert len(prior) in (0, n)
    return _Side(list(shards) + prior, [_full_sds(s, k) for s, k in zip(shards, kinds)], 3 * n, make, n_alias=len(prior))


def _side_gather_pass(fulls, shards, kinds):
    n = len(fulls)

    def make(srcs, outs, ssem, rsem):
        xi, yi, ci, j = _place()
        sib = (xi, yi, 1 - ci)
        cps = []
        for i in range(n):
            cs = shards[i].shape[-1]
            for r in (1, 2, 3):
                got = _gather_piece(outs[i], kinds[i], cs, j ^ r, ci)
                cps.append(pltpu.make_async_remote_copy(src_ref=got, dst_ref=got, send_sem=ssem.at[4 * i + r - 1], recv_sem=rsem.at[4 * i + r - 1],
                                                        device_id=sib, device_id_type=MESH))
            cps.append(pltpu.make_async_remote_copy(src_ref=srcs[i], dst_ref=_full_shard(outs[i], kinds[i], j, cs), send_sem=ssem.at[4 * i + 3],
                                                    recv_sem=rsem.at[4 * i + 3], device_id=sib, device_id_type=MESH))
        return cps

    return _Side(list(shards) + list(fulls), [SDS(f.shape, f.dtype) for f in fulls], 4 * n, make, n_alias=n)


def _sibling_exchange(gviews, kinds, name):
    n = len(gviews)
    outs = [SDS(g.shape[:1] + g.shape[2:] if kind == "col" else g.shape[:2] + g.shape[3:], g.dtype) for g, kind in zip(gviews, kinds)]

    def body(*refs):
        gs, lands, (ssem, rsem) = refs[:n], refs[n:2 * n], refs[2 * n:]
        xi, yi, ci, _ = _place()
        cps = []
        for i in range(n):
            src = gs[i].at[:, 1 - ci] if kinds[i] == "col" else gs[i].at[:, :, 1 - ci]
            cps.append(pltpu.make_async_remote_copy(src_ref=src, dst_ref=lands[i], send_sem=ssem.at[i], recv_sem=rsem.at[i],
                                                    device_id=(xi, yi, 1 - ci), device_id_type=MESH))
            cps[-1].start()
        for cp in cps:
            cp.wait()

    return pl.pallas_call(body, in_specs=[ANY] * n, out_specs=[ANY] * n, out_shape=outs,
                          scratch_shapes=[pltpu.SemaphoreType.DMA((n,)), pltpu.SemaphoreType.DMA((n,))], name=name)(*gviews)


def _side_sibling_exchange(gviews, kinds):
    outs = [SDS(g.shape[:1] + g.shape[2:] if kind == "col" else g.shape[:2] + g.shape[3:], g.dtype) for g, kind in zip(gviews, kinds)]

    def make(gs, lands, ssem, rsem):
        xi, yi, ci, _ = _place()
        return [pltpu.make_async_remote_copy(src_ref=gs[i].at[:, 1 - ci] if kinds[i] == "col" else gs[i].at[:, :, 1 - ci], dst_ref=lands[i],
                                             send_sem=ssem.at[i], recv_sem=rsem.at[i], device_id=(xi, yi, 1 - ci), device_id_type=MESH)
                for i in range(len(gs))]

    return _Side(gviews, outs, len(gviews), make)


def _chip_sums(gs, lands, kinds, jc, name):
    n = len(gs)
    args, in_specs, out_specs, out_shape = [], [], [], []
    for g, land, kind in zip(gs, lands, kinds):
        if kind == "col":
            P, _, RH, C = g.shape
            CS = C // N_CHIPS
            in_specs += [pl.BlockSpec((P, None, RH, CS), lambda r, jc: (0, jc[1], 0, jc[0] ^ r)),
                         pl.BlockSpec((P, RH, CS), lambda r, jc: (0, 0, jc[0] ^ r))]
        else:
            P, _, _, RH, CS = g.shape
            in_specs += [pl.BlockSpec((P, None, None, RH, CS), lambda r, jc: (0, jc[0] ^ r, jc[1], 0, 0)),
                         pl.BlockSpec((P, None, RH, CS), lambda r, jc: (0, jc[0] ^ r, 0, 0))]
        args += [g, land]
        out_specs += [pl.BlockSpec((P, RH, CS), lambda r, jc: (0, 0, 0)), pl.BlockSpec((None, P, RH, CS), lambda r, jc: (r, 0, 0, 0))]
        out_shape += [SDS((P, RH, CS), F32), SDS((N_CHIPS, P, RH, CS), BF)]

    def body(jc_ref, *refs):
        ins, outs = refs[:2 * n], refs[2 * n:]
        for i in range(n):
            s = ins[2 * i][...] + ins[2 * i + 1][...].astype(F32)
            outs[2 * i + 1][...] = s.astype(BF)

            @pl.when(pl.program_id(0) == 0)
            def _():
                outs[2 * i][...] = s

    outs = _call(body, args, grid=(N_CHIPS,), in_specs=in_specs, out_specs=out_specs, out_shape=out_shape, semantics=("arbitrary",),
                 name=name, prefetch=(jc,))
    return outs[0::2], outs[1::2]


def _chip_exchange_copies(srcs, lands, ssem, rsem):
    xi, yi, ci, _ = _place()
    return [pltpu.make_async_remote_copy(src_ref=srcs[i].at[r], dst_ref=lands[i].at[r], send_sem=ssem.at[3 * i + r - 1],
                                         recv_sem=rsem.at[3 * i + r - 1], device_id=_chip_peer(xi, yi, ci, r), device_id_type=MESH)
            for i in range(len(srcs)) for r in (1, 2, 3)]


def _side_chip_exchange(pieces):
    return _Side(pieces, [SDS(p.shape, p.dtype) for p in pieces], 3 * len(pieces), _chip_exchange_copies)


FINAL_SUM_STEPS = 2


def _final_sums(owns, lands, jc, shards, l, L, name, side=None):
    n = len(owns)
    args, in_specs, out_specs, out_shape = [], [], [], []
    for own, land in zip(owns, lands):
        P, RH, CS = own.shape
        hr = RH // FINAL_SUM_STEPS
        in_specs += [pl.BlockSpec((P, hr, CS), lambda h, jc: (0, h, 0))]
        in_specs += [pl.BlockSpec((None, P, hr, CS), functools.partial(lambda r, h, jc: (r, 0, h, 0), r)) for r in (1, 2, 3)]
        args += [own, land, land, land]
        out_specs.append(pl.BlockSpec((None, P, None, hr, CS), lambda h, jc: (l, 0, jc[1], h, 0)))
        out_shape.append(SDS((L, P, 2, RH, CS), F32))
    aliases = None
    if shards is not None:
        aliases = {4 * n + i: i for i in range(n)}
        in_specs += [ANY] * n
        args += list(shards)

    def body(jc_ref, *refs):
        outs = refs[len(args):]
        for i in range(n):
            o, a, b, c = (refs[4 * i + t][...] for t in range(4))
            outs[i][...] = ((o + a.astype(F32)) + b.astype(F32)) + c.astype(F32)

    return _call(body, args, grid=(FINAL_SUM_STEPS,), in_specs=in_specs, out_specs=out_specs, out_shape=out_shape, semantics=("arbitrary",),
                 name=name, prefetch=(jc,), aliases=aliases, side=side)


def _halves_exchange(shards, l, name):
    n = len(shards)

    def body(*refs):
        outs, (ssem, rsem) = refs[n:2 * n], refs[2 * n:]
        xi, yi, ci, _ = _place()
        cps = []
        for i in range(n):
            mine = outs[i].at[l, :, ci]
            cps.append(pltpu.make_async_remote_copy(src_ref=mine, dst_ref=mine, send_sem=ssem.at[i], recv_sem=rsem.at[i],
                                                    device_id=(xi, yi, 1 - ci), device_id_type=MESH))
            cps[-1].start()
        for i in range(n):
            land = outs[i].at[l, :, 1 - ci]
            pltpu.make_async_remote_copy(src_ref=land, dst_ref=land, send_sem=ssem.at[i], recv_sem=rsem.at[i],
                                         device_id=(xi, yi, 1 - ci), device_id_type=MESH).wait_recv()
        for cp in cps:
            cp.wait_send()

    return pl.pallas_call(body, in_specs=[ANY] * n, out_specs=[ANY] * n, out_shape=[SDS(s.shape, s.dtype) for s in shards],
                          input_output_aliases={i: i for i in range(n)},
                          scratch_shapes=[pltpu.SemaphoreType.DMA((n,)), pltpu.SemaphoreType.DMA((n,))], name=name)(*shards)


def _reduce_small(part, pieces):
    NR, Wd = part.shape
    ND = 2 * N_CHIPS
    n = len(pieces)

    def body(p_ref, *refs):
        srcs, o_ref, lands, (land, ssem, rsem, xs, xr) = refs[:n], refs[n], refs[n + 1:2 * n + 1], refs[2 * n + 1:]
        exchange = _chip_exchange_copies(srcs, lands, xs, xr)
        for cp in exchange:
            cp.start()
        xi, yi, ci, j = _place()
        me = 2 * j + ci
        land[me] = p_ref[...]
        cps = []
        for rr in range(1, ND):
            dev = (xi ^ (rr >> 2), yi ^ ((rr >> 1) & 1), ci ^ (rr & 1))
            cps.append(pltpu.make_async_remote_copy(src_ref=p_ref, dst_ref=land.at[me], send_sem=ssem.at[rr - 1], recv_sem=rsem.at[rr - 1],
                                                    device_id=dev, device_id_type=MESH))
            cps[-1].start()
        for rr in range(1, ND):
            got = land.at[me ^ rr]
            pltpu.make_async_remote_copy(src_ref=got, dst_ref=got, send_sem=ssem.at[rr - 1], recv_sem=rsem.at[rr - 1],
                                         device_id=(xi, yi, ci), device_id_type=MESH).wait_recv()
        acc = land[0]
        for d in range(1, ND):
            acc = acc + land[d]
        o_ref[...] = acc
        for cp in cps:
            cp.wait_send()
        for cp in exchange:
            cp.wait()

    vm = pl.BlockSpec(memory_space=pltpu.VMEM)
    outs = pl.pallas_call(
        body, in_specs=[vm] + [ANY] * n, out_specs=[vm] + [ANY] * n, out_shape=[SDS((NR, Wd), F32)] + [SDS(p.shape, p.dtype) for p in pieces],
        scratch_shapes=[pltpu.VMEM((ND, NR, Wd), F32), pltpu.SemaphoreType.DMA((ND - 1,)), pltpu.SemaphoreType.DMA((ND - 1,)),
                        pltpu.SemaphoreType.DMA((3 * n,)), pltpu.SemaphoreType.DMA((3 * n,))],
        name="small_grad_allreduce")(part, *pieces)
    return outs[0], list(outs[1:])


def _adamw_update(w_ref, g_ref, m_ref, v_ref, d_ref, mo_ref, vo_ref):
    g = g_ref[...]
    m = ADAM_B1 * m_ref[...] + (1.0 - ADAM_B1) * g
    v = ADAM_B2 * v_ref[...] + (1.0 - ADAM_B2) * jnp.square(g)
    m_hat = m / (1.0 - ADAM_B1 ** ADAM_STEP)
    v_hat = v / (1.0 - ADAM_B2 ** ADAM_STEP)
    d_ref[...] = -ADAM_LR * (m_hat / (jnp.sqrt(v_hat) + ADAM_EPS) + ADAM_WD * w_ref[...])
    mo_ref[...] = m
    vo_ref[...] = v


ADAMW_STEPS = 8


def _adamw_layer(ws, gs, ms, vs, prev, l, name, side=None):
    n = len(ws)
    args, in_specs, out_specs, out_shape = [], [], [], []
    for w, g, m, v in zip(ws, gs, ms, vs):
        L, R, C = w.shape
        blk = pl.BlockSpec((None, R // ADAMW_STEPS, C), lambda i: (l, i, 0))
        in_specs += [blk] * 4
        args += [w, g, m, v]
        out_specs += [blk] * 3
        out_shape += [SDS((L, R, C), F32)] * 3
    aliases = None
    if prev is not None:
        aliases = {4 * n + i: i for i in range(3 * n)}
        in_specs += [ANY] * (3 * n)
        args += list(prev)

    def body(*refs):
        outs = refs[len(args):]
        for i in range(n):
            _adamw_update(*refs[4 * i:4 * i + 4], *outs[3 * i:3 * i + 3])

    return _call(body, args, grid=(ADAMW_STEPS,), in_specs=in_specs, out_specs=out_specs, out_shape=out_shape, semantics=("parallel",),
                 name=name, aliases=aliases, side=side)


def _adamw(w, g, m, v, name):
    shape = w.shape
    C = shape[-1]
    R = w.size // C
    tb = _tile(R, max(8, (1 << 18) // C), 8)
    body = functools.partial(_adamw_update)
    blk = pl.BlockSpec((tb, C), lambda i: (i, 0))
    outs = pl.pallas_call(body, grid=(R // tb,), in_specs=[blk] * 4, out_specs=[blk] * 3, out_shape=[SDS((R, C), F32)] * 3,
                          compiler_params=_params("parallel"), name=name)(*[t.reshape(R, C) for t in (w, g, m, v)])
    return [t.reshape(shape) for t in outs]


WEIGHTS = ("mix_norm_g", "w_in", "conv_dw_w", "conv_dw_b", "conv_ln_g", "conv_ln_b", "w_conv_out", "w_pool_grp", "pool_scale", "w_out",
           "xattn_norm_g", "mem_norm_g", "w_q", "w_kv", "w_o", "ffn_norm_g", "w_up", "ffn_dw_w", "w_down", "final_norm_g")
VECTORS = ("mix_norm_g", "conv_dw_b", "conv_ln_g", "conv_ln_b", "pool_scale", "xattn_norm_g", "mem_norm_g", "ffn_norm_g", "final_norm_g")


def _shard_view(t, kind):
    L, P, R, C = t.shape
    return t.reshape(L, P, 2, R // 2, C)


def _rows(t, width):
    return t.reshape(-1, width)


def _pack(parts):
    return jnp.concatenate([jnp.pad(p, ((0, (-p.shape[0]) % 8), (0, 0))) for p in parts], axis=0)


def kernel(x, mem, mix_norm_g, w_in, conv_dw_w, conv_dw_b, conv_ln_g, conv_ln_b, w_conv_out, w_pool_grp, pool_scale, w_out, xattn_norm_g, mem_norm_g, w_q, w_kv, w_o, ffn_norm_g, w_up, ffn_dw_w, w_down, final_norm_g, loss_target, m_mix_norm_g, m_w_in, m_conv_dw_w, m_conv_dw_b, m_conv_ln_g, m_conv_ln_b, m_w_conv_out, m_w_pool_grp, m_pool_scale, m_w_out, m_xattn_norm_g, m_mem_norm_g, m_w_q, m_w_kv, m_w_o, m_ffn_norm_g, m_w_up, m_ffn_dw_w, m_w_down, m_final_norm_g, v_mix_norm_g, v_w_in, v_conv_dw_w, v_conv_dw_b, v_conv_ln_g, v_conv_ln_b, v_w_conv_out, v_w_pool_grp, v_pool_scale, v_w_out, v_xattn_norm_g, v_mem_norm_g, v_w_q, v_w_kv, v_w_o, v_ffn_norm_g, v_w_up, v_ffn_dw_w, v_w_down, v_final_norm_g):
    w = dict(mix_norm_g=mix_norm_g, w_in=w_in, conv_dw_w=conv_dw_w, conv_dw_b=conv_dw_b, conv_ln_g=conv_ln_g, conv_ln_b=conv_ln_b,
             w_conv_out=w_conv_out, w_pool_grp=w_pool_grp, pool_scale=pool_scale, w_out=w_out, xattn_norm_g=xattn_norm_g,
             mem_norm_g=mem_norm_g, w_q=w_q, w_kv=w_kv, w_o=w_o, ffn_norm_g=ffn_norm_g, w_up=w_up, ffn_dw_w=ffn_dw_w, w_down=w_down,
             final_norm_g=final_norm_g)
    m = dict(zip(WEIGHTS, (m_mix_norm_g, m_w_in, m_conv_dw_w, m_conv_dw_b, m_conv_ln_g, m_conv_ln_b, m_w_conv_out, m_w_pool_grp, m_pool_scale,
                           m_w_out, m_xattn_norm_g, m_mem_norm_g, m_w_q, m_w_kv, m_w_o, m_ffn_norm_g, m_w_up, m_ffn_dw_w, m_w_down, m_final_norm_g)))
    v = dict(zip(WEIGHTS, (v_mix_norm_g, v_w_in, v_conv_dw_w, v_conv_dw_b, v_conv_ln_g, v_conv_ln_b, v_w_conv_out, v_w_pool_grp, v_pool_scale,
                           v_w_out, v_xattn_norm_g, v_mem_norm_g, v_w_q, v_w_kv, v_w_o, v_ffn_norm_g, v_w_up, v_ffn_dw_w, v_w_down, v_final_norm_g)))
    xi, yi, ci, j = _place()
    jc = jnp.stack([j, ci]).astype(jnp.int32)
    L = w_in.shape[0]
    G = len(POOL_WINDOWS)
    kinds = dict(BIG)

    def to_mat(name, t):
        if name == "w_pool":
            return jnp.swapaxes(t, 2, 3)
        return t[:, None]

    def from_mat(name, t):
        if name == "w_pool":
            return jnp.swapaxes(t, 2, 3)
        return t[:, 0]

    src = {name: w["w_pool_grp" if name == "w_pool" else name] for name, _ in BIG}

    KC, cs_c = conv_dw_w.shape[1], conv_dw_w.shape[2]
    KF, cs_f = ffn_dw_w.shape[1], ffn_dw_w.shape[2]
    taps = jnp.concatenate([conv_dw_w.reshape(L * KC, cs_c), ffn_dw_w.reshape(L * KF * (cs_f // cs_c), cs_c)], axis=0)
    n_taps = taps.shape[0]
    taps = jnp.pad(taps, ((0, (-n_taps) % 16), (0, 0)))
    names = [name for name, _ in BIG]
    mats = {name: to_mat(name, src[name]).astype(BF) for name in names}

    def layer_shards(l, subset):
        return [_shard_view(mats[name][l:l + 1], kinds[name]) for name in subset]

    def as_weight(name, f):
        return f.reshape(G if name == "w_pool" else 1, -1, f.shape[-1])

    assert L == 2
    fulls = _gather_weights(layer_shards(0, GATHER_FIRST) + [_shard_view(taps[None, None], "row")], [kinds[name] for name in GATHER_FIRST] + ["row"])
    ready = {(name, 0): as_weight(name, f) for name, f in zip(GATHER_FIRST, fulls)}
    landing = {}
    taps_all = fulls[-1].reshape(N_CHIPS, -1, cs_c)[:, :n_taps]
    V = {name: w[name] for name in VECTORS}
    V["conv_dw_w"] = taps_all[:, :L * KC].reshape(N_CHIPS, L, KC, cs_c).transpose(1, 2, 0, 3).reshape(L, KC, N_CHIPS * cs_c)
    V["ffn_dw_w"] = taps_all[:, L * KC:].reshape(N_CHIPS, L, KF, cs_f).transpose(1, 2, 0, 3).reshape(L, KF, N_CHIPS * cs_f)

    Bn, S, D = x.shape
    Mn = mem.shape[1]
    dims = (Bn, S, Mn, D, conv_dw_b.shape[1], w_down.shape[1] * N_CHIPS)
    xt = x.reshape(Bn * S, D)
    memf = mem.reshape(Bn * Mn, D)
    mem_n = _rms_fwd(memf, V["mem_norm_g"], "mem_norm")

    class LayerWeights:
        def __init__(self, l):
            self.l = l

        def __getitem__(self, name):
            return ready[(name, self.l)]

    def carried_gather(entries):
        return lambda: _side_gather([layer_shards(lw, [nm])[0] for nm, lw, _ in entries], [kinds[nm] for nm, _, _ in entries],
                                    [rels for _, _, rels in entries], [landing.get((nm, lw)) for nm, lw, _ in entries])

    def carried_pass(group):
        return lambda: _side_gather_pass([landing.pop(t) for t in group], [layer_shards(lw, [nm])[0] for nm, lw in group],
                                         [kinds[nm] for nm, _ in group])

    def on_land(l):
        def handle(key, fulls):
            if (l, key) in FWD_CARRY:
                landing.update({(nm, lw): f for (nm, lw, _), f in zip(FWD_CARRY[(l, key)], fulls)})
            else:
                ready.update({t: as_weight(t[0], f) for t, f in zip(PASS_CARRY[(l, key)], fulls)})
        return handle

    saved, W = [], []
    ht = _rms_fwd(xt, V["mix_norm_g"][0], "l0_mix_norm")
    for l in range(L):
        by_key = {key: carried_gather(entries) for (cl, key), entries in FWD_CARRY.items() if cl == l}
        by_key.update({key: carried_pass(group) for (cl, key), group in PASS_CARRY.items() if cl == l})
        sides = _Sides(by_key, on_land=on_land(l))
        W.append(LayerWeights(l))
        xt, ht, sv = _layer_fwd(xt, ht, mem_n, W[l], V, l, dims, sides, V["mix_norm_g"][l + 1] if l + 1 < L else None)
        saved.append(sv)
    loss, dx, dgf = _loss_bwd(xt, V["final_norm_g"], loss_target.reshape(Bn * S, D), "loss")
    loss = lax.psum(loss[0, 0], ("x", "y", "c"))

    late_names = [name for name in names if name not in EARLY]

    def views(gw, subset, twin):
        out = []
        for name in subset:
            g = gw[name][twin] if gw[name][twin].ndim == 3 else gw[name][twin][None]
            P, R, C = g.shape
            out.append(g.reshape(P, 2, R // 2, C) if kinds[name] == "col" else g.reshape(P, N_CHIPS, 2, R // (2 * N_CHIPS), C))
        return out

    def group_kinds(subset):
        return [kinds[name] for name in subset]

    class Reduction:
        def __init__(self, gw, subset, l, tag, first, table):
            self.gw, self.subset, self.l, self.tag, self.first, self.table = gw, subset, l, tag, first, table

        def sides(self):
            by_key = {self.first: lambda: _side_sibling_exchange(views(self.gw, self.subset, 1), group_kinds(self.subset))}
            by_key.update({key: (lambda names_=names_: _side_chip_exchange([self.pieces[nm] for nm in names_])) for key, names_ in self.table.items()})
            return by_key

        def on_land(self, key, landed):
            if key == self.first:
                self.sums(landed)
            elif key in self.table:
                got[self.l].update(zip(self.table[key], landed))

        def sums(self, lands):
            own, pieces = _chip_sums(views(self.gw, self.subset, 0), lands, group_kinds(self.subset), jc, f"chip_sums_{self.tag}_l{self.l}")
            owns[self.l].update(zip(self.subset, own))
            self.pieces = dict(zip(self.subset, pieces))

    def riding(reductions):
        return _Sides({key: side for r in reductions for key, side in r.sides().items()},
                      on_land=lambda key, landed: [r.on_land(key, landed) for r in reductions])

    dxb, dmem_n = dx, None
    smalls, owns, got = [None] * L, [{} for _ in range(L)], [{} for _ in range(L)]
    late = None
    for l in reversed(range(L)):
        dx, dxb, gw, sm = _layer_bwd_mlp(dx, dxb, saved[l], W[l], V, l, dims, riding([late] if late is not None else []))
        gw_mix = {}
        reductions = [Reduction(gw, EARLY, l, "mlp", "d_att", BWD_CARRY_EARLY)]
        if l == 0:
            reductions += [Reduction(gw_mix, names_, 0, tag, first, table) for tag, names_, first, table in BWD_LAST_LAYER]
        dx, dxb, dmem_n, sm2 = _layer_bwd_mix(dx, dxb, dmem_n, saved[l], mem_n, W[l], V, l, dims, riding(reductions), gw_mix)
        smalls[l] = {**sm, **sm2}
        late = Reduction(gw_mix, late_names, l, "mix", "d_gact", BWD_CARRY_LATE) if l > 0 else None
    last = Reduction(gw_mix, ("w_in",), 0, "in", None, {})
    last.sums(_sibling_exchange(views(gw_mix, last.subset, 1), group_kinds(last.subset), "grad_sibling_exchange_in_l0"))
    grad_x = dx.reshape(Bn, S, D)
    _, _, dgm = _rms_bwd(memf, V["mem_norm_g"], dmem_n, None, "mem_norm_b")
    small = {k: jnp.stack([sm[k] for sm in smalls]) if k in ("conv_dw_w", "ffn_dw_w") else jnp.concatenate([sm[k] for sm in smalls], axis=0)
             for k in smalls[0]}
    small["mem_norm_g"] = dgm
    small["final_norm_g"] = dgf

    small_w = conv_dw_b.shape[1]
    order = VECTORS + ("conv_dw_w", "ffn_dw_w")
    parts = [_rows(small[name], small_w) for name in order]
    counts = [p.shape[0] for p in parts]
    summed, landed_last = _reduce_small(_pack(parts), [last.pieces[name] for name in last.subset])
    got[0].update(zip(last.subset, landed_last))

    keys = ["w_pool_grp" if name == "w_pool" else name for name in names]
    rows3 = lambda t: t.reshape(t.shape[0], -1, t.shape[-1])
    wmv = [[rows3(to_mat(name, d[key])) for name, key in zip(names, keys)] for d in (w, m, v)]
    gshards, updates = None, None
    for l in reversed(range(L)):
        gshards = _final_sums([owns[l][name] for name in names], [got[l][name] for name in names], jc, gshards, l, L, f"final_sums_l{l}")
        gshards = _halves_exchange(gshards, l, f"grad_halves_exchange_l{l}")
        updates = _adamw_layer(wmv[0], [rows3(t) for t in gshards], wmv[1], wmv[2], updates, l, f"adamw_l{l}")
    grads, delta, new_m, new_v = {}, {}, {}, {}
    for i, (name, key) in enumerate(zip(names, keys)):
        Lg, P, _, RH, CS = gshards[i].shape
        grads[key] = from_mat(name, gshards[i].reshape(Lg, P, 2 * RH, CS))
        for d, t in zip((delta, new_m, new_v), updates[3 * i:3 * i + 3]):
            d[key] = from_mat(name, t.reshape(Lg, P, 2 * RH, CS))

    off = 0
    for name, cnt in zip(order, counts):
        t = summed[off:off + cnt]
        off += cnt + (-cnt) % 8
        if name in VECTORS:
            grads[name] = t.reshape(w[name].shape)
        else:
            full = t.reshape(small[name].shape)
            cs = w[name].shape[2]
            grads[name] = lax.dynamic_slice_in_dim(full, j * cs, cs, axis=2)

    vec =[_pack([_rows(d[name], small_w) for name in VECTORS]) for d in (w, grads, m, v)]
    outs = _adamw(*vec, "adamw_vectors")
    off = 0
    for name in VECTORS:
        cnt = w[name].size // small_w
        for d, t in zip((delta, new_m, new_v), outs):
            d[name] = t[off:off + cnt].reshape(w[name].shape)
        off += cnt + (-cnt) % 8
    for name in ("conv_dw_w", "ffn_dw_w"):
        delta[name], new_m[name], new_v[name] = _adamw(w[name], grads[name], m[name], v[name], "adamw_" + name)

    return (loss, grad_x, *[grads[k] for k in WEIGHTS], *[delta[k] for k in WEIGHTS], *[new_m[k] for k in WEIGHTS], *[new_v[k] for k in WEIGHTS])
```

```python
import functools
import math

import jax
import jax.numpy as jnp
from jax import lax
from jax.experimental import pallas as pl
from jax.experimental.pallas import tpu as pltpu

F32 = jnp.float32
BF = jnp.bfloat16
SDS = jax.ShapeDtypeStruct
MESH = pl.DeviceIdType.MESH
ANY = pl.BlockSpec(memory_space=pl.ANY)

EPS = 1e-6
XA_HEADS = 4
POOL_WINDOWS = (2, 4, 8, 16)
N_CHIPS = 4
ADAM_LR, ADAM_B1, ADAM_B2, ADAM_EPS, ADAM_WD, ADAM_STEP = 0.001, 0.9, 0.999, 1e-08, 0.01, 10

LANES = 128
ROW_BLOCK = 512
VMEM_LIMIT = 56 * 1024 * 1024


def _params(*sem):
    return pltpu.CompilerParams(dimension_semantics=sem if sem else None, vmem_limit_bytes=VMEM_LIMIT)


def _tile(n, cap, mult=LANES):
    if n <= cap:
        return n
    for t in range(cap - cap % mult, 0, -mult):
        if n % t == 0:
            return t
    return n


_DN = {"nn": (((1,), (0,)), ((), ())), "nt": (((1,), (1,)), ((), ())), "tn": (((0,), (0,)), ((), ()))}


class _Side:
    def __init__(self, ins, outs, n, make, n_alias=0):
        self.ins, self.outs, self.n, self.make, self.n_alias = list(ins), list(outs), n, make, n_alias


def _call(body, args, *, grid, in_specs, out_specs, out_shape, semantics, name, scratch_shapes=(), side=None, prefetch=(), aliases=None):
    n_pf = len(prefetch)
    aliases = {n_pf + i: o for i, o in (aliases or {}).items()}
    n_in, n_out, n_scr = len(args), len(out_shape), len(scratch_shapes)
    n_si, n_so = (len(side.ins), len(side.outs)) if side is not None else (0, 0)
    if side is not None:
        aliases.update({n_pf + n_in + n_si - side.n_alias + i: n_out + i for i in range(side.n_alias)})

    def carrying(*refs):
        pf, refs = refs[:n_pf], refs[n_pf:]
        ins, s_in = refs[:n_in], refs[n_in:n_in + n_si]
        outs, s_out = refs[n_in + n_si:n_in + n_si + n_out], refs[n_in + n_si + n_out:n_in + n_si + n_out + n_so]
        scr = refs[n_in + n_si + n_out + n_so:]
        if side is None:
            return body(*pf, *ins, *outs, *scr)
        copies = side.make(s_in, s_out, scr[n_scr], scr[n_scr + 1])
        ids = [pl.program_id(d) for d in range(len(grid))]
        first, last = ids[0] == 0, ids[0] == grid[0] - 1
        for d in range(1, len(grid)):
            first, last = first & (ids[d] == 0), last & (ids[d] == grid[d] - 1)

        @pl.when(first)
        def _():
            for cp in copies:
                cp.start()

        body(*pf, *ins, *outs, *scr[:n_scr])

        @pl.when(last)
        def _():
            for cp in copies:
                cp.wait()

    sems = [pltpu.SemaphoreType.DMA((side.n,)), pltpu.SemaphoreType.DMA((side.n,))] if side is not None else []
    outs = pl.pallas_call(
        carrying, grid_spec=pltpu.PrefetchScalarGridSpec(
            num_scalar_prefetch=n_pf, grid=grid, in_specs=list(in_specs) + [ANY] * n_si, out_specs=list(out_specs) + [ANY] * n_so,
            scratch_shapes=list(scratch_shapes) + sems),
        out_shape=list(out_shape) + (side.outs if side is not None else []), input_output_aliases=aliases,
        compiler_params=_params(*(semantics if side is None else ["arbitrary"] * len(grid))), name=name)(
            *prefetch, *args, *(side.ins if side is not None else []))
    return list(outs) if side is None else (list(outs[:n_out]), list(outs[n_out:]))


def _call1(body, args, *, out_spec, out_shape, side=None, **kw):
    got = _call(body, args, out_specs=[out_spec], out_shape=[out_shape], side=side, **kw)
    return got[0] if side is None else (got[0][0], got[1])


MM_VMEM_BUDGET = 40 * 1024 * 1024
MM_STEP_MACS = 2200 * 1024 * 1024
MXU_WIDTH = 256
MM_STEP_COST_BYTES = 1 << 20


def _divisors(n):
    return [t for t in range(LANES, n + 1, LANES) if n % t == 0] or [n]


def _mm_tiles(M, N, K, a_bytes, b_bytes, o_bytes, n_unit=None):
    best = None
    for tk in _divisors(K):
        for tm in _divisors(M):
            for tn in _divisors(N if n_unit is None else n_unit):
                nk = K // tk
                foot = 2 * (tm * tk * a_bytes + tk * tn * b_bytes + tm * tn * o_bytes) + (tm * tn * 4 if nk > 1 else 0)
                if (foot > MM_VMEM_BUDGET or tm * tn * tk > MM_STEP_MACS or tn < min(N if n_unit is None else n_unit, MXU_WIDTH)
                        or tm < min(M, MXU_WIDTH)):
                    continue
                steps = (M // tm) * (N // tn) * nk
                traffic = M * K * a_bytes * (N // tn if nk > 1 else 1) + K * N * b_bytes * (M // tm) + M * N * o_bytes
                exposed = tm * tk * a_bytes + tk * tn * b_bytes + tm * tn * o_bytes
                cost = traffic + exposed + steps * MM_STEP_COST_BYTES + (nk - 1) * M * N * 8
                if best is None or cost < best[0]:
                    best = (cost, tm, tn, tk)
    assert best is not None, (M, N, K)
    return best[1:]


def _mm(a, b, dims, out_dtype, name, res=None, bl=None, side=None, twin=None, b_halves=False, part=None):
    bs = b.shape[1:] if bl is not None or b_halves else b.shape
    if dims == "nn":
        (M, K), (K2, N) = a.shape, bs
    elif dims == "nt":
        (M, K), (N, K2) = a.shape, bs
    else:
        (K, M), (K2, N) = a.shape, bs
    assert K == K2, (name, a.shape, b.shape)
    n_half = N
    if b_halves:
        assert dims == "tn" and bl is None
        N = 2 * n_half
    n_total, n_first, earlier = part if part is not None else (N, 0, None)
    tm, tn, tk = _mm_tiles(M, N, K, a.dtype.itemsize, b.dtype.itemsize, jnp.dtype(out_dtype).itemsize
                           + (res.dtype.itemsize if res is not None else 0) + (jnp.dtype(twin).itemsize if twin is not None else 0),
                           n_unit=math.gcd(n_half, n_first) if b_halves or n_first else None)
    nk = K // tk
    lead = (None,) if bl is not None or b_halves else ()
    pre = (lambda *ix: (bl,) + ix) if bl is not None else (lambda *ix: ix)
    if b_halves:
        per_half = n_half // tn
        pre = lambda k, j: (j // per_half, k, j % per_half)
    if dims == "tn":
        a_spec = pl.BlockSpec((tk, tm), lambda i, j, k: (k, i))
    else:
        a_spec = pl.BlockSpec((tm, tk), lambda i, j, k: (i, k))
    if dims == "nt":
        b_spec = pl.BlockSpec(lead + (tn, tk), lambda i, j, k: pre(j, k))
    else:
        b_spec = pl.BlockSpec(lead + (tk, tn), lambda i, j, k: pre(k, j))
    assert n_first % tn == 0 and (part is None or res is None)
    o_spec = pl.BlockSpec((tm, tn), lambda i, j, k: (i, n_first // tn + j))
    in_specs, args = [a_spec, b_spec], [a, b]
    if res is not None:
        in_specs.append(o_spec)
        args.append(res)
    n_main = len(args)
    n_out = 1 if twin is None else 2
    aliases = None
    if earlier is not None:
        earlier = list(earlier) if twin is not None else [earlier]
        aliases = {n_main + t: t for t in range(n_out)}
        in_specs += [ANY] * n_out
        args += earlier

    def body(*refs):
        refs = refs[:n_main] + refs[len(args):]
        a_ref, b_ref = refs[0], refs[1]
        r_ref = refs[2] if res is not None else None
        o_ref = refs[n_main]
        p = lax.dot_general(a_ref[...].astype(BF), b_ref[...].astype(BF), _DN[dims], preferred_element_type=F32)

        def finish(t):
            if r_ref is not None:
                t = t + r_ref[...]
            o_ref[...] = t.astype(out_dtype)
            if twin is not None:
                refs[n_main + 1][...] = t.astype(twin)

        if nk == 1:
            finish(p)
        else:
            acc = refs[n_main + n_out]
            k = pl.program_id(2)

            @pl.when(k == 0)
            def _():
                acc[...] = p

            @pl.when(k > 0)
            def _():
                acc[...] += p

            @pl.when(k == nk - 1)
            def _():
                finish(acc[...])

    got = _call(body, args, grid=(M // tm, N // tn, nk), in_specs=in_specs, out_specs=[o_spec] * n_out,
                out_shape=[SDS((M, n_total), out_dtype)] + ([SDS((M, n_total), twin)] if twin is not None else []),
                scratch_shapes=[pltpu.VMEM((tm, tn), F32)] if nk > 1 else [], semantics=("parallel", "parallel", "arbitrary"),
                name=name, side=side, aliases=aliases)
    outs, landed = (got, None) if side is None else got
    out = outs[0] if twin is None else (outs[0], outs[1])
    return out if side is None else (out, landed)


def _rms(x, g):
    return x * lax.rsqrt(jnp.mean(x * x, axis=-1, keepdims=True) + EPS) * g


def _ln_silu(x, g, b):
    mu = jnp.mean(x, axis=-1, keepdims=True)
    xc = x - mu
    var = jnp.mean(xc * xc, axis=-1, keepdims=True)
    return jax.nn.silu(xc * lax.rsqrt(var + EPS) * g + b)


def _merge(gc, gp, yc, yp, ps):
    return jax.nn.sigmoid(gc) * yc + jax.nn.sigmoid(gp) * (yp * ps)


def _gated(gate, val):
    return jax.nn.gelu(gate) * val


def _rms_fwd(x, g, name):
    T, D = x.shape
    tb = _tile(T, ROW_BLOCK, 8)

    def body(x_ref, g_ref, o_ref):
        o_ref[...] = _rms(x_ref[...], g_ref[...]).astype(BF)

    row = pl.BlockSpec((tb, D), lambda i: (i, 0))
    return pl.pallas_call(body, grid=(T // tb,), in_specs=[row, pl.BlockSpec((1, D), lambda i: (0, 0))], out_specs=row,
                          out_shape=SDS((T, D), BF), compiler_params=_params("parallel"), name=name)(x, g.reshape(1, D))


def _rms_bwd(x, g, dh, dres, name):
    T, D = x.shape
    tb = _tile(T, ROW_BLOCK, 8)

    def body(*refs):
        if dres is not None:
            x_ref, g_ref, dh_ref, dres_ref, dx_ref, dxb_ref, dg_ref = refs
        else:
            x_ref, g_ref, dh_ref, dx_ref, dxb_ref, dg_ref = refs
        _, vjp = jax.vjp(_rms, x_ref[...], g_ref[...])
        dx, dg = vjp(dh_ref[...].astype(F32))
        if dres is not None:
            dx = dx + dres_ref[...]
        dx_ref[...] = dx
        dxb_ref[...] = dx.astype(BF)

        @pl.when(pl.program_id(0) == 0)
        def _():
            dg_ref[...] = jnp.zeros_like(dg_ref)

        dg_ref[...] += dg

    row = pl.BlockSpec((tb, D), lambda i: (i, 0))
    vec = pl.BlockSpec((1, D), lambda i: (0, 0))
    ins = [x, g.reshape(1, D), dh] + ([dres] if dres is not None else [])
    return pl.pallas_call(
        body, grid=(T // tb,), in_specs=[row, vec, row] + ([row] if dres is not None else []), out_specs=[row, row, vec],
        out_shape=[SDS((T, D), F32), SDS((T, D), BF), SDS((1, D), F32)], compiler_params=_params("arbitrary"), name=name)(*ins)


def _row_tile(M, K, N, per_row_bytes):
    fixed = K * N * 2
    fit = [t for t in _divisors(M) if fixed + 2 * t * per_row_bytes <= MM_VMEM_BUDGET and t * K * N <= 2 * MM_STEP_MACS]
    return max(fit) if fit else min(_divisors(M))


def _mm_rms_fwd(a, b, res, g, name, side=None):
    M, K = a.shape
    N = b.shape[2]
    tm = _row_tile(M, K, N, K * 2 + N * (4 + 4 + 2))

    def body(a_ref, b_ref, r_ref, g_ref, x_ref, h_ref):
        x = r_ref[...] + lax.dot_general(a_ref[...], b_ref[...], _DN["nn"], preferred_element_type=F32)
        x_ref[...] = x
        h_ref[...] = _rms(x, g_ref[...]).astype(BF)

    row = pl.BlockSpec((tm, N), lambda i: (i, 0))
    return _call(body, (a, b, res, g.reshape(1, N)), grid=(M // tm,),
                 in_specs=[pl.BlockSpec((tm, K), lambda i: (i, 0)), pl.BlockSpec((None, K, N), lambda i: (0, 0, 0), pipeline_mode=pl.Buffered(1)), row,
                           pl.BlockSpec((1, N), lambda i: (0, 0))],
                 out_specs=[row, row], out_shape=[SDS((M, N), F32), SDS((M, N), BF)], semantics=("parallel",), name=name, side=side)


def _mm_rms_bwd(a_parts, b, x, g, dres, name, side=None):
    n_a = len(a_parts)
    M = a_parts[0].shape[-2]
    N, K = b.shape[1:]
    assert K == sum(p.shape[-1] * (p.shape[0] if p.ndim == 3 else 1) for p in a_parts)
    tm = _row_tile(M, K, N, K * 2 + N * (4 + 4 + 4 + 2))

    def body(*refs):
        a_refs, (b_ref, x_ref, g_ref, r_ref, dx_ref, dxb_ref, dg_ref) = refs[:n_a], refs[n_a:]
        dh, col = None, 0
        for p, a_ref in zip(a_parts, a_refs):
            for blk in ([a_ref[h] for h in range(p.shape[0])] if p.ndim == 3 else [a_ref[...]]):
                t = lax.dot_general(blk, b_ref[:, col:col + p.shape[-1]], _DN["nt"], preferred_element_type=F32)
                dh = t if dh is None else dh + t
                col += p.shape[-1]
        _, vjp = jax.vjp(_rms, x_ref[...], g_ref[...])
        dx, dg = vjp(dh)
        dx = dx + r_ref[...]
        dx_ref[...] = dx
        dxb_ref[...] = dx.astype(BF)

        @pl.when(pl.program_id(0) == 0)
        def _():
            dg_ref[...] = jnp.zeros_like(dg_ref)

        dg_ref[...] += dg

    row = pl.BlockSpec((tm, N), lambda i: (i, 0))
    vec = pl.BlockSpec((1, N), lambda i: (0, 0))
    a_specs = [pl.BlockSpec((p.shape[0], tm, p.shape[2]), lambda i: (0, i, 0)) if p.ndim == 3 else pl.BlockSpec((tm, p.shape[1]), lambda i: (i, 0))
               for p in a_parts]
    return _call(
        body, (*a_parts, b, x, g.reshape(1, N), dres), grid=(M // tm,),
        in_specs=a_specs + [pl.BlockSpec((None, N, K), lambda i: (0, 0, 0), pipeline_mode=pl.Buffered(1)), row, vec, row],
        out_specs=[row, row, vec], out_shape=[SDS((M, N), F32), SDS((M, N), BF), SDS((1, N), F32)],
        semantics=("arbitrary",), name=name, side=side)


def _loss_bwd(x, g, target, name):
    T, D = x.shape
    tb = _tile(T, ROW_BLOCK, 8)
    nb = T // tb

    def body(x_ref, g_ref, t_ref, loss_ref, dx_ref, dg_ref, acc):
        i = pl.program_id(0)
        y, vjp = jax.vjp(_rms, x_ref[...], g_ref[...])
        err = y - t_ref[...]
        dx, dg = vjp(err * (1.0 / D))
        dx_ref[...] = dx

        @pl.when(i == 0)
        def _():
            dg_ref[...] = jnp.zeros_like(dg_ref)
            acc[...] = jnp.zeros_like(acc)

        dg_ref[...] += dg
        acc[...] += jnp.sum(err * err, axis=0, keepdims=True)

        @pl.when(i == nb - 1)
        def _():
            loss_ref[...] = jnp.full(loss_ref.shape, (0.5 / D) * jnp.sum(acc[...]), F32)

    row = pl.BlockSpec((tb, D), lambda i: (i, 0))
    vec = pl.BlockSpec((1, D), lambda i: (0, 0))
    return pl.pallas_call(
        body, grid=(nb,), in_specs=[row, vec, row], out_specs=[pl.BlockSpec((1, LANES), lambda i: (0, 0)), row, vec],
        out_shape=[SDS((1, LANES), F32), SDS((T, D), F32), SDS((1, D), F32)], scratch_shapes=[pltpu.VMEM((1, D), F32)],
        compiler_params=_params("arbitrary"), name=name)(x, g.reshape(1, D), target)


def _ln_silu_mm(cv, g, b, w, name):
    T, C = cv.shape
    D = w.shape[2]
    tb = _tile(T, 2 * ROW_BLOCK, 8)

    def body(x_ref, g_ref, b_ref, w_ref, y1_ref, y_ref):
        y1 = _ln_silu(x_ref[...], g_ref[...], b_ref[...]).astype(BF)
        y1_ref[...] = y1
        y_ref[...] = lax.dot_general(y1, w_ref[...], _DN["nn"], preferred_element_type=F32).astype(BF)

    row = pl.BlockSpec((tb, C), lambda i: (i, 0))
    vec = pl.BlockSpec((1, C), lambda i: (0, 0))
    return pl.pallas_call(
        body, grid=(T // tb,), in_specs=[row, vec, vec, pl.BlockSpec((None, C, D), lambda i: (0, 0, 0), pipeline_mode=pl.Buffered(1))],
        out_specs=[row, pl.BlockSpec((tb, D), lambda i: (i, 0))], out_shape=[SDS((T, C), BF), SDS((T, D), BF)],
        compiler_params=_params("parallel"), name=name)(cv, g.reshape(1, C), b.reshape(1, C), w)


def _mm_ln_silu_bwd(dyc, w, cv, g, b, name, side=None):
    T, C = cv.shape
    D = w.shape[2]
    tb = _tile(T, 2 * ROW_BLOCK, 8)

    def body(d_ref, w_ref, x_ref, g_ref, b_ref, dx_ref, dg_ref, db_ref):
        dy1 = lax.dot_general(d_ref[...], w_ref[...], _DN["nt"], preferred_element_type=F32)
        _, vjp = jax.vjp(_ln_silu, x_ref[...], g_ref[...], b_ref[...])
        dx, dg, db = vjp(dy1)
        dx_ref[...] = dx

        @pl.when(pl.program_id(0) == 0)
        def _():
            dg_ref[...] = jnp.zeros_like(dg_ref)
            db_ref[...] = jnp.zeros_like(db_ref)

        dg_ref[...] += dg
        db_ref[...] += db

    row = pl.BlockSpec((tb, C), lambda i: (i, 0))
    vec = pl.BlockSpec((1, C), lambda i: (0, 0))
    return _call(
        body, (dyc, w, cv, g.reshape(1, C), b.reshape(1, C)), grid=(T // tb,),
        in_specs=[pl.BlockSpec((tb, D), lambda i: (i, 0)), pl.BlockSpec((None, C, D), lambda i: (0, 0, 0), pipeline_mode=pl.Buffered(1)), row, vec, vec],
        out_specs=[row, vec, vec], out_shape=[SDS((T, C), F32), SDS((1, C), F32), SDS((1, C), F32)], semantics=("arbitrary",),
        name=name, side=side)


def _merge_fwd(proj, yc, yp, ps, C, name, side=None):
    T, D = yc.shape
    tb = _tile(T, ROW_BLOCK, 8)
    nj = D // C

    def body(gc_ref, gp_ref, yc_ref, yp_ref, ps_ref, o_ref):
        o_ref[...] = _merge(gc_ref[...], gp_ref[...], yc_ref[...].astype(F32), yp_ref[...].astype(F32), ps_ref[...]).astype(BF)

    blk = pl.BlockSpec((tb, C), lambda i, j: (i, j))
    return _call1(
        body, (proj, proj, yc, yp, ps.reshape(1, D)), grid=(T // tb, nj),
        in_specs=[pl.BlockSpec((tb, C), lambda i, j: (i, 3 + j)), pl.BlockSpec((tb, C), lambda i, j: (i, 3 + nj + j)), blk, blk,
                  pl.BlockSpec((1, C), lambda i, j: (0, j))],
        out_spec=blk, out_shape=SDS((T, D), BF), semantics=("parallel", "parallel"), name=name, side=side)


def _merge_bwd(proj, yc, yp, ps, dm, C, name, side=None):
    T, D = yc.shape
    tb = _tile(T, ROW_BLOCK, 8)
    nj = D // C

    def body(gc_ref, gp_ref, yc_ref, yp_ref, ps_ref, dm_ref, dg_ref, dyc_ref, dyp_ref, dps_ref):
        _, vjp = jax.vjp(_merge, gc_ref[...], gp_ref[...], yc_ref[...].astype(F32), yp_ref[...].astype(F32), ps_ref[...])
        dgc, dgp, dyc, dyp, dps = vjp(dm_ref[...].astype(F32))
        dg_ref[0] = dgc.astype(BF)
        dg_ref[1] = dgp.astype(BF)
        dyc_ref[...] = dyc.astype(BF)
        dyp_ref[...] = dyp.astype(BF)

        @pl.when(pl.program_id(1) == 0)
        def _():
            dps_ref[...] = jnp.zeros_like(dps_ref)

        dps_ref[...] += dps

    blk = pl.BlockSpec((tb, C), lambda j, i: (i, j))
    vec = pl.BlockSpec((1, C), lambda j, i: (0, j))
    return _call(
        body, (proj, proj, yc, yp, ps.reshape(1, D), dm), grid=(nj, T // tb),
        in_specs=[pl.BlockSpec((tb, C), lambda j, i: (i, 3 + j)), pl.BlockSpec((tb, C), lambda j, i: (i, 3 + nj + j)), blk, blk, vec, blk],
        out_specs=[pl.BlockSpec((2, tb, C), lambda j, i: (0, i, j)), blk, blk, vec],
        out_shape=[SDS((2, T, D), BF), SDS((T, D), BF), SDS((T, D), BF), SDS((1, D), F32)],
        semantics=("parallel", "arbitrary"), name=name, side=side)


def _shd(v, s, rows):
    if s == 0:
        return v
    return jnp.where(rows >= s, pltpu.roll(v, s, 0), 0.0)


def _shu(v, s, rows):
    if s == 0:
        return v
    n = v.shape[0]
    return jnp.where(rows < n - s, pltpu.roll(v, n - s, 0), 0.0)


def _glu_conv_fwd(proj, w, b, Bn, S, C, name, side=None):
    K = w.shape[0]
    sl = min(LANES, C)
    ns = C // sl

    def body(a_ref, gl_ref, w_ref, b_ref, o_ref):
        y0 = a_ref[...] * jax.nn.sigmoid(gl_ref[...])
        rows = lax.broadcasted_iota(jnp.int32, y0.shape, 0)
        acc = jnp.zeros_like(y0) + b_ref[...]
        for k in range(K):
            acc = acc + w_ref[k:k + 1, :] * _shd(y0, K - 1 - k, rows)
        o_ref[...] = acc

    return _call1(
        body, (proj, proj, w, b.reshape(1, C)), grid=(Bn, ns),
        in_specs=[pl.BlockSpec((S, sl), lambda bi, j: (bi, j)), pl.BlockSpec((S, sl), lambda bi, j: (bi, ns + j)),
                  pl.BlockSpec((K, sl), lambda bi, j: (0, j)), pl.BlockSpec((1, sl), lambda bi, j: (0, j))],
        out_spec=pl.BlockSpec((S, sl), lambda bi, j: (bi, j)), out_shape=SDS((Bn * S, C), F32),
        semantics=("parallel", "parallel"), name=name, side=side)


def _glu_conv_bwd(proj, w, dcv, Bn, S, C, name, side=None):
    K = w.shape[0]
    sl = min(LANES, C)
    ns = C // sl

    def body(a_ref, gl_ref, w_ref, d_ref, dagl_ref, dw_ref, db_ref):
        a = a_ref[...]
        sg = jax.nn.sigmoid(gl_ref[...])
        y0 = a * sg
        d = d_ref[...]
        rows = lax.broadcasted_iota(jnp.int32, y0.shape, 0)

        @pl.when(pl.program_id(1) == 0)
        def _():
            dw_ref[...] = jnp.zeros_like(dw_ref)
            db_ref[...] = jnp.zeros_like(db_ref)

        dy0 = jnp.zeros_like(y0)
        for k in range(K):
            s = K - 1 - k
            dw_ref[k:k + 1, :] += jnp.sum(d * _shd(y0, s, rows), axis=0, keepdims=True)
            dy0 = dy0 + w_ref[k:k + 1, :] * _shu(d, s, rows)
        db_ref[...] += jnp.sum(d, axis=0, keepdims=True)
        dagl_ref[0] = (dy0 * sg).astype(BF)
        dagl_ref[1] = (dy0 * a * sg * (1.0 - sg)).astype(BF)

    blk = pl.BlockSpec((S, sl), lambda j, bi: (bi, j))
    return _call(
        body, (proj, proj, w, dcv), grid=(ns, Bn),
        in_specs=[blk, pl.BlockSpec((S, sl), lambda j, bi: (bi, ns + j)), pl.BlockSpec((K, sl), lambda j, bi: (0, j)), blk],
        out_specs=[pl.BlockSpec((2, S, sl), lambda j, bi: (0, bi, j)), pl.BlockSpec((K, sl), lambda j, bi: (0, j)),
                   pl.BlockSpec((1, sl), lambda j, bi: (0, j))],
        out_shape=[SDS((2, Bn * S, C), BF), SDS((K, C), F32), SDS((1, C), F32)],
        semantics=("parallel", "arbitrary"), name=name, side=side)


def _pool_z(u, g, rows):
    s2 = u + _shd(u, 1, rows)
    s4 = s2 + _shd(s2, 2, rows)
    s8 = s4 + _shd(s4, 4, rows)
    s16 = s8 + _shd(s8, 8, rows)
    sw = jnp.where(g == 0, s2, jnp.where(g == 1, s4, jnp.where(g == 2, s8, s16)))
    cnt = jnp.minimum(rows + 1, POOL_WINDOWS[0] << g).astype(F32)
    return sw / cnt - u, cnt


def _pool_fwd(proj, wpt, l, Bn, S, C, D, name):
    G = len(POOL_WINDOWS)
    gd, go = C // G, D // G

    def body(u_ref, w_ref, o_ref):
        g = pl.program_id(1)
        u = u_ref[...]
        rows = lax.broadcasted_iota(jnp.int32, u.shape, 0)
        zp, _ = _pool_z(u, g, rows)
        o_ref[...] = lax.dot_general(zp.astype(BF), w_ref[...], _DN["nt"], preferred_element_type=F32).astype(BF)

    return pl.pallas_call(
        body, grid=(Bn, G),
        in_specs=[pl.BlockSpec((S, gd), lambda bi, g: (bi, 2 * G + g)), pl.BlockSpec((None, go, gd), lambda bi, g: (l * G + g, 0, 0))],
        out_specs=pl.BlockSpec((S, go), lambda bi, g: (bi, g)), out_shape=SDS((Bn * S, D), BF),
        compiler_params=_params("parallel", "parallel"), name=name)(proj, wpt)


def _pool_bwd(proj, wpt, dyp, l, Bn, S, C, D, name):
    G = len(POOL_WINDOWS)
    gd, go = C // G, D // G

    def body(u_ref, w_ref, d_ref, du_ref, dw_ref):
        g = pl.program_id(0)
        u = u_ref[...]
        rows = lax.broadcasted_iota(jnp.int32, u.shape, 0)
        zp, cnt = _pool_z(u, g, rows)
        d = d_ref[...]
        dzp = lax.dot_general(d, w_ref[...], _DN["nn"], preferred_element_type=F32)

        @pl.when(pl.program_id(1) == 0)
        def _():
            dw_ref[...] = jnp.zeros_like(dw_ref)

        dw_ref[...] += lax.dot_general(d, zp.astype(BF), _DN["tn"], preferred_element_type=F32)
        dsw = dzp / cnt
        zero = jnp.zeros_like(dsw)
        d16 = jnp.where(g == 3, dsw, zero)
        d8 = jnp.where(g == 2, dsw, zero) + d16 + _shu(d16, 8, rows)
        d4 = jnp.where(g == 1, dsw, zero) + d8 + _shu(d8, 4, rows)
        d2 = jnp.where(g == 0, dsw, zero) + d4 + _shu(d4, 2, rows)
        d1 = d2 + _shu(d2, 1, rows)
        du_ref[...] = (d1 - dzp).astype(BF)

    return pl.pallas_call(
        body, grid=(G, Bn),
        in_specs=[pl.BlockSpec((S, gd), lambda g, bi: (bi, 2 * G + g)), pl.BlockSpec((None, go, gd), lambda g, bi: (l * G + g, 0, 0)),
                  pl.BlockSpec((S, go), lambda g, bi: (bi, g))],
        out_specs=[pl.BlockSpec((S, gd), lambda g, bi: (bi, g)), pl.BlockSpec((None, go, gd), lambda g, bi: (g, 0, 0))],
        out_shape=[SDS((Bn * S, C), BF), SDS((G, go, gd), F32)],
        compiler_params=_params("parallel", "arbitrary"), name=name)(proj, wpt, dyp)


def _ffn_conv(u, w_ref, rows):
    K = w_ref.shape[0]
    acc = w_ref[K - 1:K, :] * u
    for k in range(K - 1):
        acc = acc + w_ref[k:k + 1, :] * _shd(u, K - 1 - k, rows)
    return acc


def _ffn_cb(F):
    return _tile(F, 256)


def _ffn_act_fwd(up0, w, Bn, S, F, name, side=None):
    cb = _ffn_cb(F)
    nj = F // cb

    def body(g_ref, v_ref, wg_ref, wv_ref, o_ref):
        rows = lax.broadcasted_iota(jnp.int32, g_ref.shape, 0)
        o_ref[...] = _gated(_ffn_conv(g_ref[...], wg_ref, rows), _ffn_conv(v_ref[...], wv_ref, rows)).astype(BF)

    K = w.shape[0]
    return _call1(
        body, (up0, up0, w, w), grid=(Bn, nj),
        in_specs=[pl.BlockSpec((S, cb), lambda bi, j: (bi, j)), pl.BlockSpec((S, cb), lambda bi, j: (bi, nj + j)),
                  pl.BlockSpec((K, cb), lambda bi, j: (0, j)), pl.BlockSpec((K, cb), lambda bi, j: (0, nj + j))],
        out_spec=pl.BlockSpec((S, cb), lambda bi, j: (bi, j)), out_shape=SDS((Bn * S, F), BF),
        semantics=("parallel", "parallel"), name=name, side=side)


SUBLANES = 8
FFN_HALO = SUBLANES
FFN_ROWS = 256
GELU_C0, GELU_C1 = 0.7978845608028654, 0.044715


def _gelu_and_grad(x):
    x2 = x * x
    t = jnp.tanh(GELU_C0 * (x + GELU_C1 * (x2 * x)))
    cdf = 0.5 * (1.0 + t)
    return x * cdf, cdf + (0.5 * GELU_C0) * x * (1.0 - t * t) * (1.0 + (3.0 * GELU_C1) * x2)


def _ffn_act_bwd(up0, w, dg, Bn, S, F, name, side=None):
    cb = min(LANES, F)
    nj = F // cb
    K = w.shape[0]
    rc = FFN_ROWS if S % FFN_ROWS == 0 else S
    win = rc + 2 * FFN_HALO
    assert K - 1 <= FFN_HALO and rc % SUBLANES == 0

    def body(g_ref, v_ref, wg_ref, wv_ref, d_ref, do_ref, dwg_ref, dwv_ref, gp, vp, dp):
        for pad, src in ((gp, g_ref), (vp, v_ref), (dp, d_ref)):
            pad[0:FFN_HALO, :] = jnp.zeros((FFN_HALO, cb), F32)
            pad[FFN_HALO + S:, :] = jnp.zeros((FFN_HALO, cb), F32)
            pad[FFN_HALO:FFN_HALO + S, :] = src[...].astype(F32)
        wg = [wg_ref[k:k + 1, :] for k in range(K)]
        wv = [wv_ref[k:k + 1, :] for k in range(K)]

        def taps(u):
            return [pltpu.roll(u, K - 1 - k, 0) for k in range(K - 1)] + [u]

        def conv(us, ws):
            acc = ws[K - 1] * us[K - 1]
            for k in range(K - 1):
                acc = acc + ws[k] * us[k]
            return acc

        def conv_t(dc, ws):
            acc = ws[K - 1] * dc
            for k in range(K - 1):
                acc = acc + ws[k] * pltpu.roll(dc, win - (K - 1 - k), 0)
            return acc

        def fold(t):
            acc = t[FFN_HALO:FFN_HALO + SUBLANES]
            for i in range(1, rc // SUBLANES):
                acc = acc + t[FFN_HALO + SUBLANES * i:FFN_HALO + SUBLANES * (i + 1)]
            return acc

        def chunk(c, sums):
            r0 = pl.multiple_of(c * rc, SUBLANES)
            gs, vs, d = taps(gp[pl.ds(r0, win), :]), taps(vp[pl.ds(r0, win), :]), dp[pl.ds(r0, win), :]
            ge, dge = _gelu_and_grad(conv(gs, wg))
            dgc = d * conv(vs, wv) * dge
            dvc = d * ge
            do_ref[0, pl.ds(r0, rc), :] = conv_t(dgc, wg)[FFN_HALO:FFN_HALO + rc].astype(BF)
            do_ref[1, pl.ds(r0, rc), :] = conv_t(dvc, wv)[FFN_HALO:FFN_HALO + rc].astype(BF)
            new = [fold(dc * u) for us, dc in ((gs, dgc), (vs, dvc)) for u in us]
            return tuple(a + b for a, b in zip(sums, new))

        sums = lax.fori_loop(0, S // rc, chunk, tuple(jnp.zeros((SUBLANES, cb), F32) for _ in range(2 * K)))

        @pl.when(pl.program_id(1) == 0)
        def _():
            dwg_ref[...] = jnp.zeros_like(dwg_ref)
            dwv_ref[...] = jnp.zeros_like(dwv_ref)

        for k in range(K):
            dwg_ref[k:k + 1, :] += jnp.sum(sums[k], axis=0, keepdims=True)
            dwv_ref[k:k + 1, :] += jnp.sum(sums[K + k], axis=0, keepdims=True)

    blk = pl.BlockSpec((S, cb), lambda j, bi: (bi, j))
    wblk = pl.BlockSpec((K, cb), lambda j, bi: (0, j))
    return _call(
        body, (up0, up0, w, w, dg), grid=(nj, Bn),
        in_specs=[blk, pl.BlockSpec((S, cb), lambda j, bi: (bi, nj + j)), wblk, pl.BlockSpec((K, cb), lambda j, bi: (0, nj + j)), blk],
        out_specs=[pl.BlockSpec((2, S, cb), lambda j, bi: (0, bi, j)), wblk, wblk],
        out_shape=[SDS((2, Bn * S, F), BF), SDS((K, F), F32), SDS((K, F), F32)],
        scratch_shapes=[pltpu.VMEM((S + 2 * FFN_HALO, cb), F32)] * 3, semantics=("parallel", "arbitrary"), name=name, side=side)


def _softmax_rows(q, k, scale):
    sc = lax.dot_general(q, k, _DN["nt"], preferred_element_type=F32) * scale
    e = jnp.exp(sc - jnp.max(sc, axis=-1, keepdims=True))
    return e / jnp.sum(e, axis=-1, keepdims=True)


def _attn_ts(S):
    return _tile(S, 1024, 8)


def _attn_fwd(q, kv, Bn, S, Mn, D, name, side=None):
    H = XA_HEADS
    dh = D // H
    ts = _attn_ts(S)
    nsb = S // ts
    scale = dh ** -0.5

    def body(q_ref, k_ref, v_ref, o_ref):
        p = _softmax_rows(q_ref[...], k_ref[...], scale)
        o_ref[...] = lax.dot_general(p.astype(BF), v_ref[...], _DN["nn"], preferred_element_type=F32).astype(BF)

    qblk = pl.BlockSpec((ts, dh), lambda bi, h, s: (bi * nsb + s, h))
    return _call1(
        body, (q, kv, kv), grid=(Bn, H, nsb),
        in_specs=[qblk, pl.BlockSpec((Mn, dh), lambda bi, h, s: (bi, h)), pl.BlockSpec((Mn, dh), lambda bi, h, s: (bi, H + h))],
        out_spec=qblk, out_shape=SDS((Bn * S, D), BF), semantics=("parallel", "parallel", "parallel"), name=name, side=side)


def _attn_bwd(q, kv, datt, Bn, S, Mn, D, name):
    H = XA_HEADS
    dh = D // H
    ts = _attn_ts(S)
    nsb = S // ts
    scale = dh ** -0.5

    def body(q_ref, k_ref, v_ref, do_ref, dq_ref, dk_ref, dv_ref):
        q, k, v, do = q_ref[...], k_ref[...], v_ref[...], do_ref[...]
        p = _softmax_rows(q, k, scale)
        dp = lax.dot_general(do, v, _DN["nt"], preferred_element_type=F32)
        ds = (p * (dp - jnp.sum(dp * p, axis=-1, keepdims=True)) * scale).astype(BF)
        dq_ref[...] = lax.dot_general(ds, k, _DN["nn"], preferred_element_type=F32).astype(BF)

        @pl.when(pl.program_id(2) == 0)
        def _():
            dk_ref[...] = jnp.zeros_like(dk_ref)
            dv_ref[...] = jnp.zeros_like(dv_ref)

        dk_ref[...] += lax.dot_general(ds, q, _DN["tn"], preferred_element_type=F32)
        dv_ref[...] += lax.dot_general(p.astype(BF), do, _DN["tn"], preferred_element_type=F32)

    qblk = pl.BlockSpec((ts, dh), lambda bi, h, s: (bi * nsb + s, h))
    kblk = pl.BlockSpec((Mn, dh), lambda bi, h, s: (bi, h))
    return pl.pallas_call(
        body, grid=(Bn, H, nsb),
        in_specs=[qblk, kblk, pl.BlockSpec((Mn, dh), lambda bi, h, s: (bi, H + h)), qblk],
        out_specs=[qblk, kblk, kblk], out_shape=[SDS((Bn * S, D), BF), SDS((Bn * Mn, D), F32), SDS((Bn * Mn, D), F32)],
        compiler_params=_params("parallel", "parallel", "arbitrary"), name=name)(q, kv, kv, datt)


class _Sides:
    def __init__(self, by_key=None, on_land=None):
        self.by_key, self.landed, self.on_land = dict(by_key or {}), {}, on_land

    def run(self, key, fn, *args, **kw):
        side = self.by_key.get(key)
        if side is None:
            return fn(*args, **kw)
        out, self.landed[key] = fn(*args, side=side() if callable(side) else side, **kw)
        if self.on_land is not None:
            self.on_land(key, self.landed[key])
        return out

    def mm(self, key, *args, **kw):
        return self.run(key, _mm, *args, **kw)


def _layer_fwd(x, h, mem_n, W, V, l, dims, sides, next_g):
    Bn, S, Mn, D, C, F = dims
    n = f"l{l}_"
    proj = sides.mm("proj", h, W["w_in"], "nn", F32, n + "proj", bl=0)
    cv = sides.run("glu_conv", _glu_conv_fwd, proj, V["conv_dw_w"][l], V["conv_dw_b"][l], Bn, S, C, n + "glu_conv")
    yc1, yc = _ln_silu_mm(cv, V["conv_ln_g"][l], V["conv_ln_b"][l], W["w_conv_out"], n + "conv_out")
    yp = _pool_fwd(proj, W["w_pool"], 0, Bn, S, C, D, n + "pool")
    merged = sides.run("merge", _merge_fwd, proj, yc, yp, V["pool_scale"][l], C, n + "merge")
    x1, hq = sides.run("out_proj", _mm_rms_fwd, merged, W["w_out"], x, V["xattn_norm_g"][l], n + "out_proj")
    q = sides.mm("q_proj", hq, W["w_q"], "nn", BF, n + "q_proj", bl=0)
    kv = _mm(mem_n, W["w_kv"], "nn", BF, n + "kv_proj", bl=0)
    att = sides.run("attn", _attn_fwd, q, kv, Bn, S, Mn, D, n + "attn")
    x2, hf = sides.run("o_proj", _mm_rms_fwd, att, W["w_o"], x1, V["ffn_norm_g"][l], n + "o_proj")
    up0 = sides.mm("up_proj", hf, W["w_up"], "nn", F32, n + "up_proj", bl=0)
    gact = sides.run("ffn_act", _ffn_act_fwd, up0, V["ffn_dw_w"][l], Bn, S, F, n + "ffn_act")
    if next_g is not None:
        x3, h3 = sides.run("down_proj", _mm_rms_fwd, gact, W["w_down"], x2, next_g, n + "down_proj")
    else:
        x3, h3 = sides.mm("down_proj", gact, W["w_down"], "nn", F32, n + "down_proj", res=x2, bl=0), None
    return x3, h3, dict(x=x, h=h, proj=proj, cv=cv, yc1=yc1, yc=yc, yp=yp, merged=merged, x1=x1, hq=hq, q=q, kv=kv, att=att, x2=x2,
                        hf=hf, up0=up0, gact=gact)


def _layer_bwd_mlp(dx, dxb, sv, W, V, l, dims, sides):
    Bn, S, Mn, D, C, F = dims
    n = f"l{l}_b_"
    gw, sm = {}, {}
    dgact = sides.mm("d_gact", dxb, W["w_down"], "nt", BF, n + "d_gact", bl=0)
    gw["w_down"] = sides.mm("dw_down", sv["gact"], dxb, "tn", F32, n + "dw_down", twin=BF)
    dup0, dwg, dwv = sides.run("ffn_act_b", _ffn_act_bwd, sv["up0"], V["ffn_dw_w"][l], dgact, Bn, S, F, n + "ffn_act")
    sm["ffn_dw_w"] = jnp.concatenate([dwg, dwv], axis=1)
    dx2, dx2b, sm["ffn_norm_g"] = _mm_rms_bwd([dup0], W["w_up"], sv["x2"], V["ffn_norm_g"][l], dx, n + "d_hf")
    gw["w_up"] = sides.mm("dw_up", sv["hf"], dup0, "tn", F32, n + "dw_up", twin=BF, b_halves=True)
    return dx2, dx2b, gw, sm


def _layer_bwd_mix(dx2, dx2b, dmem_n, sv, mem_n, W, V, l, dims, sides, gw):
    Bn, S, Mn, D, C, F = dims
    n = f"l{l}_b_"
    sm = {}
    datt = sides.mm("d_att", dx2b, W["w_o"], "nt", BF, n + "d_att", bl=0)
    gw["w_o"] = _mm(sv["att"], dx2b, "tn", F32, n + "dw_o", twin=BF)
    dq, dk, dv = _attn_bwd(sv["q"], sv["kv"], datt, Bn, S, Mn, D, n + "attn")
    dkv = jnp.concatenate([dk, dv], axis=1)
    gw["w_kv"] = _mm(mem_n, dkv, "tn", F32, n + "dw_kv", twin=BF)
    dmem_n = _mm(dkv, W["w_kv"], "nt", F32, n + "d_mem", res=dmem_n, bl=0)
    dx1, dx1b, sm["xattn_norm_g"] = _mm_rms_bwd([dq], W["w_q"], sv["x1"], V["xattn_norm_g"][l], dx2, n + "d_hq")
    gw["w_q"] = _mm(sv["hq"], dq, "tn", F32, n + "dw_q", twin=BF)
    dmerged = sides.mm("d_merged", dx1b, W["w_out"], "nt", BF, n + "d_merged", bl=0)
    gw["w_out"] = _mm(sv["merged"], dx1b, "tn", F32, n + "dw_out", twin=BF)
    dgates, dyc, dyp, sm["pool_scale"] = sides.run("merge_b", _merge_bwd, sv["proj"], sv["yc"], sv["yp"], V["pool_scale"][l], dmerged, C, n + "merge")
    du, dwp = _pool_bwd(sv["proj"], W["w_pool"], dyp, 0, Bn, S, C, D, n + "pool")
    gw["w_pool"] = (dwp, dwp.astype(BF))
    gw["w_conv_out"] = _mm(sv["yc1"], dyc, "tn", F32, n + "dw_conv_out", twin=BF)
    dcv, sm["conv_ln_g"], sm["conv_ln_b"] = sides.run("ln_silu_b", _mm_ln_silu_bwd, dyc, W["w_conv_out"], sv["cv"], V["conv_ln_g"][l],
                                                      V["conv_ln_b"][l], n + "d_yc1")
    dagl, sm["conv_dw_w"], sm["conv_dw_b"] = sides.run("glu_conv_b", _glu_conv_bwd, sv["proj"], V["conv_dw_w"][l], dcv, Bn, S, C, n + "glu_conv")
    dx, dxb, sm["mix_norm_g"] = sides.run("d_h", _mm_rms_bwd, [dagl, du, dgates], W["w_in"], sv["x"], V["mix_norm_g"][l], dx1, n + "d_h")
    n_in = W["w_in"].shape[2]
    part = _mm(sv["h"], dagl, "tn", F32, n + "dw_in_conv", twin=BF, b_halves=True, part=(n_in, 0, None))
    part = _mm(sv["h"], du, "tn", F32, n + "dw_in_pool", twin=BF, part=(n_in, 2 * C, part))
    gw["w_in"] = sides.mm("dw_in", sv["h"], dgates, "tn", F32, n + "dw_in", twin=BF, b_halves=True, part=(n_in, 3 * C, part))
    return dx, dxb, dmem_n, sm


BIG = (("w_in", "col"), ("w_conv_out", "col"), ("w_pool", "row"), ("w_out", "row"), ("w_q", "row"), ("w_kv", "col"),
       ("w_o", "row"), ("w_up", "col"), ("w_down", "row"))
ALL_RELS = (1, 2, 3)
GATHER_FIRST = ("w_in", "w_conv_out", "w_pool", "w_out")
FWD_CARRY = {
    (0, "proj"): (("w_up", 0, (1, 2)), ("w_q", 0, ALL_RELS), ("w_o", 0, ALL_RELS)),
    (0, "glu_conv"): (("w_kv", 0, ALL_RELS),),
    (0, "merge"): (("w_up", 0, (3,)),),
    (0, "q_proj"): (("w_down", 0, (1, 2)),),
    (0, "attn"): (("w_down", 0, (3,)),),
    (0, "up_proj"): (("w_in", 1, ALL_RELS), ("w_conv_out", 1, ALL_RELS), ("w_pool", 1, ALL_RELS), ("w_o", 1, ALL_RELS)),
    (0, "ffn_act"): (("w_out", 1, ALL_RELS), ("w_q", 1, ALL_RELS), ("w_kv", 1, ALL_RELS)),
    (1, "proj"): (("w_up", 1, (1, 2)),),
    (1, "glu_conv"): (("w_down", 1, (1, 2)),),
    (1, "merge"): (("w_up", 1, (3,)),),
    (1, "attn"): (("w_down", 1, (3,)),),
}
PASS_CARRY = {
    (0, "out_proj"): (("w_kv", 0), ("w_q", 0), ("w_o", 0)),
    (0, "o_proj"): (("w_up", 0), ("w_down", 0)),
    (0, "down_proj"): (("w_in", 1), ("w_conv_out", 1), ("w_pool", 1), ("w_out", 1), ("w_q", 1), ("w_kv", 1), ("w_o", 1)),
    (1, "o_proj"): (("w_up", 1), ("w_down", 1)),
}
EARLY = ("w_down", "w_up")
BWD_CARRY_EARLY = {"merge_b": ("w_down",), "glu_conv_b": ("w_up",)}
BWD_CARRY_LATE = {"ffn_act_b": ("w_in", "w_conv_out", "w_pool", "w_out", "w_q", "w_kv", "w_o")}
BWD_LAST_LAYER = (("att", ("w_o", "w_kv", "w_q"), "d_merged", {"d_h": ("w_o", "w_kv", "w_q")}),
                  ("tok", ("w_out", "w_pool", "w_conv_out"), "ln_silu_b", {"dw_in": ("w_out", "w_pool", "w_conv_out")}))


def _place():
    xi, yi, ci = lax.axis_index("x"), lax.axis_index("y"), lax.axis_index("c")
    return xi, yi, ci, 2 * xi + yi


def _chip_peer(xi, yi, ci, r):
    return (xi ^ (r >> 1), yi ^ (r & 1), ci)


def _full_shard(ref, kind, k, cs):
    if kind == "col":
        return ref.at[:, :, :, :, pl.ds(pl.multiple_of(k * cs, cs), cs)]
    return ref.at[:, :, k]


def _gather_weights(shards, kinds):
    n = len(shards)
    outs = []
    for s, kind in zip(shards, kinds):
        L, P, _, RH, CS = s.shape
        outs.append(SDS((L, P, 2, RH, CS * N_CHIPS) if kind == "col" else (L, P, N_CHIPS, 2, RH, CS), s.dtype))
    per = 7

    def body(*refs):
        srcs, fulls, (ssem, rsem) = refs[:n], refs[n:2 * n], refs[2 * n:]
        xi, yi, ci, j = _place()
        sib = (xi, yi, 1 - ci)

        def piece(i, k, c):
            kind, cs = kinds[i], shards[i].shape[-1]
            if kind == "col":
                return fulls[i].at[:, :, c, :, pl.ds(pl.multiple_of(k * cs, cs), cs)]
            return fulls[i].at[:, :, k, c]

        def copy(i, slot, src, dst, dev):
            return pltpu.make_async_remote_copy(src_ref=src, dst_ref=dst, send_sem=ssem.at[per * i + slot], recv_sem=rsem.at[per * i + slot],
                                                device_id=dev, device_id_type=MESH)

        own, first, passed = [], [], []
        for i in range(n):
            for r in (1, 2, 3):
                first.append(copy(i, r - 1, srcs[i].at[:, :, ci], piece(i, j, ci), _chip_peer(xi, yi, ci, r)))
                first[-1].start()
        for i in range(n):
            own.append(copy(i, 6, srcs[i], _full_shard(fulls[i], kinds[i], j, shards[i].shape[-1]), sib))
            own[-1].start()
        for i in range(n):
            for r in (1, 2, 3):
                got = piece(i, j ^ r, ci)
                copy(i, r - 1, got, got, sib).wait_recv()
                passed.append(copy(i, 2 + r, got, got, sib))
                passed[-1].start()
        for i in range(n):
            for r in (1, 2, 3):
                got = piece(i, j ^ r, 1 - ci)
                copy(i, 2 + r, got, got, sib).wait_recv()
        for cp in own:
            cp.wait()
        for cp in first + passed:
            cp.wait_send()

    return pl.pallas_call(
        body, in_specs=[ANY] * n, out_specs=[ANY] * n, out_shape=outs,
        scratch_shapes=[pltpu.SemaphoreType.DMA((per * n,)), pltpu.SemaphoreType.DMA((per * n,))], name="gather_weights")(*shards)


def _full_sds(s, kind):
    L, P, _, RH, CS = s.shape
    return SDS((L, P, 2, RH, CS * N_CHIPS) if kind == "col" else (L, P, N_CHIPS, 2, RH, CS), s.dtype)


def _gather_piece(full, kind, cs, k, c):
    if kind == "col":
        return full.at[:, :, c, :, pl.ds(pl.multiple_of(k * cs, cs), cs)]
    return full.at[:, :, k, c]


def _side_gather(shards, kinds, rels, fulls):
    n = len(shards)

    def make(srcs, outs, ssem, rsem):
        xi, yi, ci, j = _place()
        return [pltpu.make_async_remote_copy(
            src_ref=srcs[i].at[:, :, ci], dst_ref=_gather_piece(outs[i], kinds[i], shards[i].shape[-1], j, ci), send_sem=ssem.at[3 * i + r - 1],
            recv_sem=rsem.at[3 * i + r - 1], device_id=_chip_peer(xi, yi, ci, r), device_id_type=MESH) for i in range(n) for r in rels[i]]

    prior = [f for f in fulls if f is not None]
    assert len(prior) in (0, n)
    return _Side(list(shards) + prior, [_full_sds(s, k) for s, k in zip(shards, kinds)], 3 * n, make, n_alias=len(prior))


def _side_gather_pass(fulls, shards, kinds):
    n = len(fulls)

    def make(srcs, outs, ssem, rsem):
        xi, yi, ci, j = _place()
        sib = (xi, yi, 1 - ci)
        cps = []
        for i in range(n):
            cs = shards[i].shape[-1]
            for r in (1, 2, 3):
                got = _gather_piece(outs[i], kinds[i], cs, j ^ r, ci)
                cps.append(pltpu.make_async_remote_copy(src_ref=got, dst_ref=got, send_sem=ssem.at[4 * i + r - 1], recv_sem=rsem.at[4 * i + r - 1],
                                                        device_id=sib, device_id_type=MESH))
            cps.append(pltpu.make_async_remote_copy(src_ref=srcs[i], dst_ref=_full_shard(outs[i], kinds[i], j, cs), send_sem=ssem.at[4 * i + 3],
                                                    recv_sem=rsem.at[4 * i + 3], device_id=sib, device_id_type=MESH))
        return cps

    return _Side(list(shards) + list(fulls), [SDS(f.shape, f.dtype) for f in fulls], 4 * n, make, n_alias=n)


def _sibling_exchange(gviews, kinds, name):
    n = len(gviews)
    outs = [SDS(g.shape[:1] + g.shape[2:] if kind == "col" else g.shape[:2] + g.shape[3:], g.dtype) for g, kind in zip(gviews, kinds)]

    def body(*refs):
        gs, lands, (ssem, rsem) = refs[:n], refs[n:2 * n], refs[2 * n:]
        xi, yi, ci, _ = _place()
        cps = []
        for i in range(n):
            src = gs[i].at[:, 1 - ci] if kinds[i] == "col" else gs[i].at[:, :, 1 - ci]
            cps.append(pltpu.make_async_remote_copy(src_ref=src, dst_ref=lands[i], send_sem=ssem.at[i], recv_sem=rsem.at[i],
                                                    device_id=(xi, yi, 1 - ci), device_id_type=MESH))
            cps[-1].start()
        for cp in cps:
            cp.wait()

    return pl.pallas_call(body, in_specs=[ANY] * n, out_specs=[ANY] * n, out_shape=outs,
                          scratch_shapes=[pltpu.SemaphoreType.DMA((n,)), pltpu.SemaphoreType.DMA((n,))], name=name)(*gviews)


def _side_sibling_exchange(gviews, kinds):
    outs = [SDS(g.shape[:1] + g.shape[2:] if kind == "col" else g.shape[:2] + g.shape[3:], g.dtype) for g, kind in zip(gviews, kinds)]

    def make(gs, lands, ssem, rsem):
        xi, yi, ci, _ = _place()
        return [pltpu.make_async_remote_copy(src_ref=gs[i].at[:, 1 - ci] if kinds[i] == "col" else gs[i].at[:, :, 1 - ci], dst_ref=lands[i],
                                             send_sem=ssem.at[i], recv_sem=rsem.at[i], device_id=(xi, yi, 1 - ci), device_id_type=MESH)
                for i in range(len(gs))]

    return _Side(gviews, outs, len(gviews), make)


def _chip_sums(gs, lands, kinds, jc, name):
    n = len(gs)
    args, in_specs, out_specs, out_shape = [], [], [], []
    for g, land, kind in zip(gs, lands, kinds):
        if kind == "col":
            P, _, RH, C = g.shape
            CS = C // N_CHIPS
            in_specs += [pl.BlockSpec((P, None, RH, CS), lambda r, jc: (0, jc[1], 0, jc[0] ^ r)),
                         pl.BlockSpec((P, RH, CS), lambda r, jc: (0, 0, jc[0] ^ r))]
        else:
            P, _, _, RH, CS = g.shape
            in_specs += [pl.BlockSpec((P, None, None, RH, CS), lambda r, jc: (0, jc[0] ^ r, jc[1], 0, 0)),
                         pl.BlockSpec((P, None, RH, CS), lambda r, jc: (0, jc[0] ^ r, 0, 0))]
        args += [g, land]
        out_specs += [pl.BlockSpec((P, RH, CS), lambda r, jc: (0, 0, 0)), pl.BlockSpec((None, P, RH, CS), lambda r, jc: (r, 0, 0, 0))]
        out_shape += [SDS((P, RH, CS), F32), SDS((N_CHIPS, P, RH, CS), BF)]

    def body(jc_ref, *refs):
        ins, outs = refs[:2 * n], refs[2 * n:]
        for i in range(n):
            s = ins[2 * i][...] + ins[2 * i + 1][...].astype(F32)
            outs[2 * i + 1][...] = s.astype(BF)

            @pl.when(pl.program_id(0) == 0)
            def _():
                outs[2 * i][...] = s

    outs = _call(body, args, grid=(N_CHIPS,), in_specs=in_specs, out_specs=out_specs, out_shape=out_shape, semantics=("arbitrary",),
                 name=name, prefetch=(jc,))
    return outs[0::2], outs[1::2]


def _chip_exchange_copies(srcs, lands, ssem, rsem):
    xi, yi, ci, _ = _place()
    return [pltpu.make_async_remote_copy(src_ref=srcs[i].at[r], dst_ref=lands[i].at[r], send_sem=ssem.at[3 * i + r - 1],
                                         recv_sem=rsem.at[3 * i + r - 1], device_id=_chip_peer(xi, yi, ci, r), device_id_type=MESH)
            for i in range(len(srcs)) for r in (1, 2, 3)]


def _side_chip_exchange(pieces):
    return _Side(pieces, [SDS(p.shape, p.dtype) for p in pieces], 3 * len(pieces), _chip_exchange_copies)


FINAL_SUM_STEPS = 2


def _final_sums(owns, lands, jc, shards, l, L, name, side=None):
    n = len(owns)
    args, in_specs, out_specs, out_shape = [], [], [], []
    for own, land in zip(owns, lands):
        P, RH, CS = own.shape
        hr = RH // FINAL_SUM_STEPS
        in_specs += [pl.BlockSpec((P, hr, CS), lambda h, jc: (0, h, 0))]
        in_specs += [pl.BlockSpec((None, P, hr, CS), functools.partial(lambda r, h, jc: (r, 0, h, 0), r)) for r in (1, 2, 3)]
        args += [own, land, land, land]
        out_specs.append(pl.BlockSpec((None, P, None, hr, CS), lambda h, jc: (l, 0, jc[1], h, 0)))
        out_shape.append(SDS((L, P, 2, RH, CS), F32))
    aliases = None
    if shards is not None:
        aliases = {4 * n + i: i for i in range(n)}
        in_specs += [ANY] * n
        args += list(shards)

    def body(jc_ref, *refs):
        outs = refs[len(args):]
        for i in range(n):
            o, a, b, c = (refs[4 * i + t][...] for t in range(4))
            outs[i][...] = ((o + a.astype(F32)) + b.astype(F32)) + c.astype(F32)

    return _call(body, args, grid=(FINAL_SUM_STEPS,), in_specs=in_specs, out_specs=out_specs, out_shape=out_shape, semantics=("arbitrary",),
                 name=name, prefetch=(jc,), aliases=aliases, side=side)


def _halves_exchange(shards, l, name):
    n = len(shards)

    def body(*refs):
        outs, (ssem, rsem) = refs[n:2 * n], refs[2 * n:]
        xi, yi, ci, _ = _place()
        cps = []
        for i in range(n):
            mine = outs[i].at[l, :, ci]
            cps.append(pltpu.make_async_remote_copy(src_ref=mine, dst_ref=mine, send_sem=ssem.at[i], recv_sem=rsem.at[i],
                                                    device_id=(xi, yi, 1 - ci), device_id_type=MESH))
            cps[-1].start()
        for i in range(n):
            land = outs[i].at[l, :, 1 - ci]
            pltpu.make_async_remote_copy(src_ref=land, dst_ref=land, send_sem=ssem.at[i], recv_sem=rsem.at[i],
                                         device_id=(xi, yi, 1 - ci), device_id_type=MESH).wait_recv()
        for cp in cps:
            cp.wait_send()

    return pl.pallas_call(body, in_specs=[ANY] * n, out_specs=[ANY] * n, out_shape=[SDS(s.shape, s.dtype) for s in shards],
                          input_output_aliases={i: i for i in range(n)},
                          scratch_shapes=[pltpu.SemaphoreType.DMA((n,)), pltpu.SemaphoreType.DMA((n,))], name=name)(*shards)


def _reduce_small(part, pieces):
    NR, Wd = part.shape
    ND = 2 * N_CHIPS
    n = len(pieces)

    def body(p_ref, *refs):
        srcs, o_ref, lands, (land, ssem, rsem, xs, xr) = refs[:n], refs[n], refs[n + 1:2 * n + 1], refs[2 * n + 1:]
        exchange = _chip_exchange_copies(srcs, lands, xs, xr)
        for cp in exchange:
            cp.start()
        xi, yi, ci, j = _place()
        me = 2 * j + ci
        land[me] = p_ref[...]
        cps = []
        for rr in range(1, ND):
            dev = (xi ^ (rr >> 2), yi ^ ((rr >> 1) & 1), ci ^ (rr & 1))
            cps.append(pltpu.make_async_remote_copy(src_ref=p_ref, dst_ref=land.at[me], send_sem=ssem.at[rr - 1], recv_sem=rsem.at[rr - 1],
                                                    device_id=dev, device_id_type=MESH))
            cps[-1].start()
        for rr in range(1, ND):
            got = land.at[me ^ rr]
            pltpu.make_async_remote_copy(src_ref=got, dst_ref=got, send_sem=ssem.at[rr - 1], recv_sem=rsem.at[rr - 1],
                                         device_id=(xi, yi, ci), device_id_type=MESH).wait_recv()
        acc = land[0]
        for d in range(1, ND):
            acc = acc + land[d]
        o_ref[...] = acc
        for cp in cps:
            cp.wait_send()
        for cp in exchange:
            cp.wait()

    vm = pl.BlockSpec(memory_space=pltpu.VMEM)
    outs = pl.pallas_call(
        body, in_specs=[vm] + [ANY] * n, out_specs=[vm] + [ANY] * n, out_shape=[SDS((NR, Wd), F32)] + [SDS(p.shape, p.dtype) for p in pieces],
        scratch_shapes=[pltpu.VMEM((ND, NR, Wd), F32), pltpu.SemaphoreType.DMA((ND - 1,)), pltpu.SemaphoreType.DMA((ND - 1,)),
                        pltpu.SemaphoreType.DMA((3 * n,)), pltpu.SemaphoreType.DMA((3 * n,))],
        name="small_grad_allreduce")(part, *pieces)
    return outs[0], list(outs[1:])


def _adamw_update(w_ref, g_ref, m_ref, v_ref, d_ref, mo_ref, vo_ref):
    g = g_ref[...]
    m = ADAM_B1 * m_ref[...] + (1.0 - ADAM_B1) * g
    v = ADAM_B2 * v_ref[...] + (1.0 - ADAM_B2) * jnp.square(g)
    m_hat = m / (1.0 - ADAM_B1 ** ADAM_STEP)
    v_hat = v / (1.0 - ADAM_B2 ** ADAM_STEP)
    d_ref[...] = -ADAM_LR * (m_hat / (jnp.sqrt(v_hat) + ADAM_EPS) + ADAM_WD * w_ref[...])
    mo_ref[...] = m
    vo_ref[...] = v


ADAMW_STEPS = 8


def _adamw_layer(ws, gs, ms, vs, prev, l, name, side=None):
    n = len(ws)
    args, in_specs, out_specs, out_shape = [], [], [], []
    for w, g, m, v in zip(ws, gs, ms, vs):
        L, R, C = w.shape
        blk = pl.BlockSpec((None, R // ADAMW_STEPS, C), lambda i: (l, i, 0))
        in_specs += [blk] * 4
        args += [w, g, m, v]
        out_specs += [blk] * 3
        out_shape += [SDS((L, R, C), F32)] * 3
    aliases = None
    if prev is not None:
        aliases = {4 * n + i: i for i in range(3 * n)}
        in_specs += [ANY] * (3 * n)
        args += list(prev)

    def body(*refs):
        outs = refs[len(args):]
        for i in range(n):
            _adamw_update(*refs[4 * i:4 * i + 4], *outs[3 * i:3 * i + 3])

    return _call(body, args, grid=(ADAMW_STEPS,), in_specs=in_specs, out_specs=out_specs, out_shape=out_shape, semantics=("parallel",),
                 name=name, aliases=aliases, side=side)


def _adamw(w, g, m, v, name):
    shape = w.shape
    C = shape[-1]
    R = w.size // C
    tb = _tile(R, max(8, (1 << 18) // C), 8)
    body = functools.partial(_adamw_update)
    blk = pl.BlockSpec((tb, C), lambda i: (i, 0))
    outs = pl.pallas_call(body, grid=(R // tb,), in_specs=[blk] * 4, out_specs=[blk] * 3, out_shape=[SDS((R, C), F32)] * 3,
                          compiler_params=_params("parallel"), name=name)(*[t.reshape(R, C) for t in (w, g, m, v)])
    return [t.reshape(shape) for t in outs]


WEIGHTS = ("mix_norm_g", "w_in", "conv_dw_w", "conv_dw_b", "conv_ln_g", "conv_ln_b", "w_conv_out", "w_pool_grp", "pool_scale", "w_out",
           "xattn_norm_g", "mem_norm_g", "w_q", "w_kv", "w_o", "ffn_norm_g", "w_up", "ffn_dw_w", "w_down", "final_norm_g")
VECTORS = ("mix_norm_g", "conv_dw_b", "conv_ln_g", "conv_ln_b", "pool_scale", "xattn_norm_g", "mem_norm_g", "ffn_norm_g", "final_norm_g")


def _shard_view(t, kind):
    L, P, R, C = t.shape
    return t.reshape(L, P, 2, R // 2, C)


def _rows(t, width):
    return t.reshape(-1, width)


def _pack(parts):
    return jnp.concatenate([jnp.pad(p, ((0, (-p.shape[0]) % 8), (0, 0))) for p in parts], axis=0)


def kernel(x, mem, mix_norm_g, w_in, conv_dw_w, conv_dw_b, conv_ln_g, conv_ln_b, w_conv_out, w_pool_grp, pool_scale, w_out, xattn_norm_g, mem_norm_g, w_q, w_kv, w_o, ffn_norm_g, w_up, ffn_dw_w, w_down, final_norm_g, loss_target, m_mix_norm_g, m_w_in, m_conv_dw_w, m_conv_dw_b, m_conv_ln_g, m_conv_ln_b, m_w_conv_out, m_w_pool_grp, m_pool_scale, m_w_out, m_xattn_norm_g, m_mem_norm_g, m_w_q, m_w_kv, m_w_o, m_ffn_norm_g, m_w_up, m_ffn_dw_w, m_w_down, m_final_norm_g, v_mix_norm_g, v_w_in, v_conv_dw_w, v_conv_dw_b, v_conv_ln_g, v_conv_ln_b, v_w_conv_out, v_w_pool_grp, v_pool_scale, v_w_out, v_xattn_norm_g, v_mem_norm_g, v_w_q, v_w_kv, v_w_o, v_ffn_norm_g, v_w_up, v_ffn_dw_w, v_w_down, v_final_norm_g):
    w = dict(mix_norm_g=mix_norm_g, w_in=w_in, conv_dw_w=conv_dw_w, conv_dw_b=conv_dw_b, conv_ln_g=conv_ln_g, conv_ln_b=conv_ln_b,
             w_conv_out=w_conv_out, w_pool_grp=w_pool_grp, pool_scale=pool_scale, w_out=w_out, xattn_norm_g=xattn_norm_g,
             mem_norm_g=mem_norm_g, w_q=w_q, w_kv=w_kv, w_o=w_o, ffn_norm_g=ffn_norm_g, w_up=w_up, ffn_dw_w=ffn_dw_w, w_down=w_down,
             final_norm_g=final_norm_g)
    m = dict(zip(WEIGHTS, (m_mix_norm_g, m_w_in, m_conv_dw_w, m_conv_dw_b, m_conv_ln_g, m_conv_ln_b, m_w_conv_out, m_w_pool_grp, m_pool_scale,
                           m_w_out, m_xattn_norm_g, m_mem_norm_g, m_w_q, m_w_kv, m_w_o, m_ffn_norm_g, m_w_up, m_ffn_dw_w, m_w_down, m_final_norm_g)))
    v = dict(zip(WEIGHTS, (v_mix_norm_g, v_w_in, v_conv_dw_w, v_conv_dw_b, v_conv_ln_g, v_conv_ln_b, v_w_conv_out, v_w_pool_grp, v_pool_scale,
                           v_w_out, v_xattn_norm_g, v_mem_norm_g, v_w_q, v_w_kv, v_w_o, v_ffn_norm_g, v_w_up, v_ffn_dw_w, v_w_down, v_final_norm_g)))
    xi, yi, ci, j = _place()
    jc = jnp.stack([j, ci]).astype(jnp.int32)
    L = w_in.shape[0]
    G = len(POOL_WINDOWS)
    kinds = dict(BIG)

    def to_mat(name, t):
        if name == "w_pool":
            return jnp.swapaxes(t, 2, 3)
        return t[:, None]

    def from_mat(name, t):
        if name == "w_pool":
            return jnp.swapaxes(t, 2, 3)
        return t[:, 0]

    src = {name: w["w_pool_grp" if name == "w_pool" else name] for name, _ in BIG}

    KC, cs_c = conv_dw_w.shape[1], conv_dw_w.shape[2]
    KF, cs_f = ffn_dw_w.shape[1], ffn_dw_w.shape[2]
    taps = jnp.concatenate([conv_dw_w.reshape(L * KC, cs_c), ffn_dw_w.reshape(L * KF * (cs_f // cs_c), cs_c)], axis=0)
    n_taps = taps.shape[0]
    taps = jnp.pad(taps, ((0, (-n_taps) % 16), (0, 0)))
    names = [name for name, _ in BIG]
    mats = {name: to_mat(name, src[name]).astype(BF) for name in names}

    def layer_shards(l, subset):
        return [_shard_view(mats[name][l:l + 1], kinds[name]) for name in subset]

    def as_weight(name, f):
        return f.reshape(G if name == "w_pool" else 1, -1, f.shape[-1])

    assert L == 2
    fulls = _gather_weights(layer_shards(0, GATHER_FIRST) + [_shard_view(taps[None, None], "row")], [kinds[name] for name in GATHER_FIRST] + ["row"])
    ready = {(name, 0): as_weight(name, f) for name, f in zip(GATHER_FIRST, fulls)}
    landing = {}
    taps_all = fulls[-1].reshape(N_CHIPS, -1, cs_c)[:, :n_taps]
    V = {name: w[name] for name in VECTORS}
    V["conv_dw_w"] = taps_all[:, :L * KC].reshape(N_CHIPS, L, KC, cs_c).transpose(1, 2, 0, 3).reshape(L, KC, N_CHIPS * cs_c)
    V["ffn_dw_w"] = taps_all[:, L * KC:].reshape(N_CHIPS, L, KF, cs_f).transpose(1, 2, 0, 3).reshape(L, KF, N_CHIPS * cs_f)

    Bn, S, D = x.shape
    Mn = mem.shape[1]
    dims = (Bn, S, Mn, D, conv_dw_b.shape[1], w_down.shape[1] * N_CHIPS)
    xt = x.reshape(Bn * S, D)
    memf = mem.reshape(Bn * Mn, D)
    mem_n = _rms_fwd(memf, V["mem_norm_g"], "mem_norm")

    class LayerWeights:
        def __init__(self, l):
            self.l = l

        def __getitem__(self, name):
            return ready[(name, self.l)]

    def carried_gather(entries):
        return lambda: _side_gather([layer_shards(lw, [nm])[0] for nm, lw, _ in entries], [kinds[nm] for nm, _, _ in entries],
                                    [rels for _, _, rels in entries], [landing.get((nm, lw)) for nm, lw, _ in entries])

    def carried_pass(group):
        return lambda: _side_gather_pass([landing.pop(t) for t in group], [layer_shards(lw, [nm])[0] for nm, lw in group],
                                         [kinds[nm] for nm, _ in group])

    def on_land(l):
        def handle(key, fulls):
            if (l, key) in FWD_CARRY:
                landing.update({(nm, lw): f for (nm, lw, _), f in zip(FWD_CARRY[(l, key)], fulls)})
            else:
                ready.update({t: as_weight(t[0], f) for t, f in zip(PASS_CARRY[(l, key)], fulls)})
        return handle

    saved, W = [], []
    ht = _rms_fwd(xt, V["mix_norm_g"][0], "l0_mix_norm")
    for l in range(L):
        by_key = {key: carried_gather(entries) for (cl, key), entries in FWD_CARRY.items() if cl == l}
        by_key.update({key: carried_pass(group) for (cl, key), group in PASS_CARRY.items() if cl == l})
        sides = _Sides(by_key, on_land=on_land(l))
        W.append(LayerWeights(l))
        xt, ht, sv = _layer_fwd(xt, ht, mem_n, W[l], V, l, dims, sides, V["mix_norm_g"][l + 1] if l + 1 < L else None)
        saved.append(sv)
    loss, dx, dgf = _loss_bwd(xt, V["final_norm_g"], loss_target.reshape(Bn * S, D), "loss")
    loss = lax.psum(loss[0, 0], ("x", "y", "c"))

    late_names = [name for name in names if name not in EARLY]

    def views(gw, subset, twin):
        out = []
        for name in subset:
            g = gw[name][twin] if gw[name][twin].ndim == 3 else gw[name][twin][None]
            P, R, C = g.shape
            out.append(g.reshape(P, 2, R // 2, C) if kinds[name] == "col" else g.reshape(P, N_CHIPS, 2, R // (2 * N_CHIPS), C))
        return out

    def group_kinds(subset):
        return [kinds[name] for name in subset]

    class Reduction:
        def __init__(self, gw, subset, l, tag, first, table):
            self.gw, self.subset, self.l, self.tag, self.first, self.table = gw, subset, l, tag, first, table

        def sides(self):
            by_key = {self.first: lambda: _side_sibling_exchange(views(self.gw, self.subset, 1), group_kinds(self.subset))}
            by_key.update({key: (lambda names_=names_: _side_chip_exchange([self.pieces[nm] for nm in names_])) for key, names_ in self.table.items()})
            return by_key

        def on_land(self, key, landed):
            if key == self.first:
                self.sums(landed)
            elif key in self.table:
                got[self.l].update(zip(self.table[key], landed))

        def sums(self, lands):
            own, pieces = _chip_sums(views(self.gw, self.subset, 0), lands, group_kinds(self.subset), jc, f"chip_sums_{self.tag}_l{self.l}")
            owns[self.l].update(zip(self.subset, own))
            self.pieces = dict(zip(self.subset, pieces))

    def riding(reductions):
        return _Sides({key: side for r in reductions for key, side in r.sides().items()},
                      on_land=lambda key, landed: [r.on_land(key, landed) for r in reductions])

    dxb, dmem_n = dx, None
    smalls, owns, got = [None] * L, [{} for _ in range(L)], [{} for _ in range(L)]
    late = None
    for l in reversed(range(L)):
        dx, dxb, gw, sm = _layer_bwd_mlp(dx, dxb, saved[l], W[l], V, l, dims, riding([late] if late is not None else []))
        gw_mix = {}
        reductions = [Reduction(gw, EARLY, l, "mlp", "d_att", BWD_CARRY_EARLY)]
        if l == 0:
            reductions += [Reduction(gw_mix, names_, 0, tag, first, table) for tag, names_, first, table in BWD_LAST_LAYER]
        dx, dxb, dmem_n, sm2 = _layer_bwd_mix(dx, dxb, dmem_n, saved[l], mem_n, W[l], V, l, dims, riding(reductions), gw_mix)
        smalls[l] = {**sm, **sm2}
        late = Reduction(gw_mix, late_names, l, "mix", "d_gact", BWD_CARRY_LATE) if l > 0 else None
    last = Reduction(gw_mix, ("w_in",), 0, "in", None, {})
    last.sums(_sibling_exchange(views(gw_mix, last.subset, 1), group_kinds(last.subset), "grad_sibling_exchange_in_l0"))
    grad_x = dx.reshape(Bn, S, D)
    _, _, dgm = _rms_bwd(memf, V["mem_norm_g"], dmem_n, None, "mem_norm_b")
    small = {k: jnp.stack([sm[k] for sm in smalls]) if k in ("conv_dw_w", "ffn_dw_w") else jnp.concatenate([sm[k] for sm in smalls], axis=0)
             for k in smalls[0]}
    small["mem_norm_g"] = dgm
    small["final_norm_g"] = dgf

    small_w = conv_dw_b.shape[1]
    order = VECTORS + ("conv_dw_w", "ffn_dw_w")
    parts = [_rows(small[name], small_w) for name in order]
    counts = [p.shape[0] for p in parts]
    summed, landed_last = _reduce_small(_pack(parts), [last.pieces[name] for name in last.subset])
    got[0].update(zip(last.subset, landed_last))

    keys = ["w_pool_grp" if name == "w_pool" else name for name in names]
    rows3 = lambda t: t.reshape(t.shape[0], -1, t.shape[-1])
    wmv = [[rows3(to_mat(name, d[key])) for name, key in zip(names, keys)] for d in (w, m, v)]
    gshards, updates = None, None
    for l in reversed(range(L)):
        gshards = _final_sums([owns[l][name] for name in names], [got[l][name] for name in names], jc, gshards, l, L, f"final_sums_l{l}")
        gshards = _halves_exchange(gshards, l, f"grad_halves_exchange_l{l}")
        updates = _adamw_layer(wmv[0], [rows3(t) for t in gshards], wmv[1], wmv[2], updates, l, f"adamw_l{l}")
    grads, delta, new_m, new_v = {}, {}, {}, {}
    for i, (name, key) in enumerate(zip(names, keys)):
        Lg, P, _, RH, CS = gshards[i].shape
        grads[key] = from_mat(name, gshards[i].reshape(Lg, P, 2 * RH, CS))
        for d, t in zip((delta, new_m, new_v), updates[3 * i:3 * i + 3]):
            d[key] = from_mat(name, t.reshape(Lg, P, 2 * RH, CS))

    off = 0
    for name, cnt in zip(order, counts):
        t = summed[off:off + cnt]
        off += cnt + (-cnt) % 8
        if name in VECTORS:
            grads[name] = t.reshape(w[name].shape)
        else:
            full = t.reshape(small[name].shape)
            cs = w[name].shape[2]
            grads[name] = lax.dynamic_slice_in_dim(full, j * cs, cs, axis=2)

    vec =[_pack([_rows(d[name], small_w) for name in VECTORS]) for d in (w, grads, m, v)]
    outs = _adamw(*vec, "adamw_vectors")
    off = 0
    for name in VECTORS:
        cnt = w[name].size // small_w
        for d, t in zip((delta, new_m, new_v), outs):
            d[name] = t[off:off + cnt].reshape(w[name].shape)
        off += cnt + (-cnt) % 8
    for name in ("conv_dw_w", "ffn_dw_w"):
        delta[name], new_m[name], new_v[name] = _adamw(w[name], grads[name], m[name], v[name], "adamw_" + name)

    return (loss, grad_x, *[grads[k] for k in WEIGHTS], *[delta[k] for k in WEIGHTS], *[new_m[k] for k in WEIGHTS], *[new_v[k] for k in WEIGHTS])
```

```python
import functools
import math

import jax
import jax.numpy as jnp
from jax import lax
from jax.experimental import pallas as pl
from jax.experimental.pallas import tpu as pltpu

F32 = jnp.float32
BF = jnp.bfloat16
SDS = jax.ShapeDtypeStruct
MESH = pl.DeviceIdType.MESH
ANY = pl.BlockSpec(memory_space=pl.ANY)

EPS = 1e-6
XA_HEADS = 4
POOL_WINDOWS = (2, 4, 8, 16)
N_CHIPS = 4
ADAM_LR, ADAM_B1, ADAM_B2, ADAM_EPS, ADAM_WD, ADAM_STEP = 0.001, 0.9, 0.999, 1e-08, 0.01, 10

LANES = 128
ROW_BLOCK = 512
VMEM_LIMIT = 56 * 1024 * 1024


def _params(*sem):
    return pltpu.CompilerParams(dimension_semantics=sem if sem else None, vmem_limit_bytes=VMEM_LIMIT)


def _tile(n, cap, mult=LANES):
    if n <= cap:
        return n
    for t in range(cap - cap % mult, 0, -mult):
        if n % t == 0:
            return t
    return n


_DN = {"nn": (((1,), (0,)), ((), ())), "nt": (((1,), (1,)), ((), ())), "tn": (((0,), (0,)), ((), ()))}


class _Side:
    def __init__(self, ins, outs, n, make, n_alias=0):
        self.ins, self.outs, self.n, self.make, self.n_alias = list(ins), list(outs), n, make, n_alias


def _call(body, args, *, grid, in_specs, out_specs, out_shape, semantics, name, scratch_shapes=(), side=None, prefetch=(), aliases=None):
    n_pf = len(prefetch)
    aliases = {n_pf + i: o for i, o in (aliases or {}).items()}
    n_in, n_out, n_scr = len(args), len(out_shape), len(scratch_shapes)
    n_si, n_so = (len(side.ins), len(side.outs)) if side is not None else (0, 0)
    if side is not None:
        aliases.update({n_pf + n_in + n_si - side.n_alias + i: n_out + i for i in range(side.n_alias)})

    def carrying(*refs):
        pf, refs = refs[:n_pf], refs[n_pf:]
        ins, s_in = refs[:n_in], refs[n_in:n_in + n_si]
        outs, s_out = refs[n_in + n_si:n_in + n_si + n_out], refs[n_in + n_si + n_out:n_in + n_si + n_out + n_so]
        scr = refs[n_in + n_si + n_out + n_so:]
        if side is None:
            return body(*pf, *ins, *outs, *scr)
        copies = side.make(s_in, s_out, scr[n_scr], scr[n_scr + 1])
        ids = [pl.program_id(d) for d in range(len(grid))]
        first, last = ids[0] == 0, ids[0] == grid[0] - 1
        for d in range(1, len(grid)):
            first, last = first & (ids[d] == 0), last & (ids[d] == grid[d] - 1)

        @pl.when(first)
        def _():
            for cp in copies:
                cp.start()

        body(*pf, *ins, *outs, *scr[:n_scr])

        @pl.when(last)
        def _():
            for cp in copies:
                cp.wait()

    sems = [pltpu.SemaphoreType.DMA((side.n,)), pltpu.SemaphoreType.DMA((side.n,))] if side is not None else []
    outs = pl.pallas_call(
        carrying, grid_spec=pltpu.PrefetchScalarGridSpec(
            num_scalar_prefetch=n_pf, grid=grid, in_specs=list(in_specs) + [ANY] * n_si, out_specs=list(out_specs) + [ANY] * n_so,
            scratch_shapes=list(scratch_shapes) + sems),
        out_shape=list(out_shape) + (side.outs if side is not None else []), input_output_aliases=aliases,
        compiler_params=_params(*(semantics if side is None else ["arbitrary"] * len(grid))), name=name)(
            *prefetch, *args, *(side.ins if side is not None else []))
    return list(outs) if side is None else (list(outs[:n_out]), list(outs[n_out:]))


def _call1(body, args, *, out_spec, out_shape, side=None, **kw):
    got = _call(body, args, out_specs=[out_spec], out_shape=[out_shape], side=side, **kw)
    return got[0] if side is None else (got[0][0], got[1])


MM_VMEM_BUDGET = 40 * 1024 * 1024
MM_STEP_MACS = 2200 * 1024 * 1024
MXU_WIDTH = 256
MM_STEP_COST_BYTES = 1 << 20


def _divisors(n):
    return [t for t in range(LANES, n + 1, LANES) if n % t == 0] or [n]


def _mm_tiles(M, N, K, a_bytes, b_bytes, o_bytes, n_unit=None):
    best = None
    for tk in _divisors(K):
        for tm in _divisors(M):
            for tn in _divisors(N if n_unit is None else n_unit):
                nk = K // tk
                foot = 2 * (tm * tk * a_bytes + tk * tn * b_bytes + tm * tn * o_bytes) + (tm * tn * 4 if nk > 1 else 0)
                if (foot > MM_VMEM_BUDGET or tm * tn * tk > MM_STEP_MACS or tn < min(N if n_unit is None else n_unit, MXU_WIDTH)
                        or tm < min(M, MXU_WIDTH)):
                    continue
                steps = (M // tm) * (N // tn) * nk
                traffic = M * K * a_bytes * (N // tn if nk > 1 else 1) + K * N * b_bytes * (M // tm) + M * N * o_bytes
                exposed = tm * tk * a_bytes + tk * tn * b_bytes + tm * tn * o_bytes
                cost = traffic + exposed + steps * MM_STEP_COST_BYTES + (nk - 1) * M * N * 8
                if best is None or cost < best[0]:
                    best = (cost, tm, tn, tk)
    assert best is not None, (M, N, K)
    return best[1:]


def _mm(a, b, dims, out_dtype, name, res=None, bl=None, side=None, twin=None, b_halves=False, part=None):
    bs = b.shape[1:] if bl is not None or b_halves else b.shape
    if dims == "nn":
        (M, K), (K2, N) = a.shape, bs
    elif dims == "nt":
        (M, K), (N, K2) = a.shape, bs
    else:
        (K, M), (K2, N) = a.shape, bs
    assert K == K2, (name, a.shape, b.shape)
    n_half = N
    if b_halves:
        assert dims == "tn" and bl is None
        N = 2 * n_half
    n_total, n_first, earlier = part if part is not None else (N, 0, None)
    tm, tn, tk = _mm_tiles(M, N, K, a.dtype.itemsize, b.dtype.itemsize, jnp.dtype(out_dtype).itemsize
                           + (res.dtype.itemsize if res is not None else 0) + (jnp.dtype(twin).itemsize if twin is not None else 0),
                           n_unit=math.gcd(n_half, n_first) if b_halves or n_first else None)
    nk = K // tk
    lead = (None,) if bl is not None or b_halves else ()
    pre = (lambda *ix: (bl,) + ix) if bl is not None else (lambda *ix: ix)
    if b_halves:
        per_half = n_half // tn
        pre = lambda k, j: (j // per_half, k, j % per_half)
    if dims == "tn":
        a_spec = pl.BlockSpec((tk, tm), lambda i, j, k: (k, i))
    else:
        a_spec = pl.BlockSpec((tm, tk), lambda i, j, k: (i, k))
    if dims == "nt":
        b_spec = pl.BlockSpec(lead + (tn, tk), lambda i, j, k: pre(j, k))
    else:
        b_spec = pl.BlockSpec(lead + (tk, tn), lambda i, j, k: pre(k, j))
    assert n_first % tn == 0 and (part is None or res is None)
    o_spec = pl.BlockSpec((tm, tn), lambda i, j, k: (i, n_first // tn + j))
    in_specs, args = [a_spec, b_spec], [a, b]
    if res is not None:
        in_specs.append(o_spec)
        args.append(res)
    n_main = len(args)
    n_out = 1 if twin is None else 2
    aliases = None
    if earlier is not None:
        earlier = list(earlier) if twin is not None else [earlier]
        aliases = {n_main + t: t for t in range(n_out)}
        in_specs += [ANY] * n_out
        args += earlier

    def body(*refs):
        refs = refs[:n_main] + refs[len(args):]
        a_ref, b_ref = refs[0], refs[1]
        r_ref = refs[2] if res is not None else None
        o_ref = refs[n_main]
        p = lax.dot_general(a_ref[...].astype(BF), b_ref[...].astype(BF), _DN[dims], preferred_element_type=F32)

        def finish(t):
            if r_ref is not None:
                t = t + r_ref[...]
            o_ref[...] = t.astype(out_dtype)
            if twin is not None:
                refs[n_main + 1][...] = t.astype(twin)

        if nk == 1:
            finish(p)
        else:
            acc = refs[n_main + n_out]
            k = pl.program_id(2)

            @pl.when(k == 0)
            def _():
                acc[...] = p

            @pl.when(k > 0)
            def _():
                acc[...] += p

            @pl.when(k == nk - 1)
            def _():
                finish(acc[...])

    got = _call(body, args, grid=(M // tm, N // tn, nk), in_specs=in_specs, out_specs=[o_spec] * n_out,
                out_shape=[SDS((M, n_total), out_dtype)] + ([SDS((M, n_total), twin)] if twin is not None else []),
                scratch_shapes=[pltpu.VMEM((tm, tn), F32)] if nk > 1 else [], semantics=("parallel", "parallel", "arbitrary"),
                name=name, side=side, aliases=aliases)
    outs, landed = (got, None) if side is None else got
    out = outs[0] if twin is None else (outs[0], outs[1])
    return out if side is None else (out, landed)


def _rms(x, g):
    return x * lax.rsqrt(jnp.mean(x * x, axis=-1, keepdims=True) + EPS) * g


def _ln_silu(x, g, b):
    mu = jnp.mean(x, axis=-1, keepdims=True)
    xc = x - mu
    var = jnp.mean(xc * xc, axis=-1, keepdims=True)
    return jax.nn.silu(xc * lax.rsqrt(var + EPS) * g + b)


def _merge(gc, gp, yc, yp, ps):
    return jax.nn.sigmoid(gc) * yc + jax.nn.sigmoid(gp) * (yp * ps)


def _gated(gate, val):
    return jax.nn.gelu(gate) * val


def _rms_fwd(x, g, name):
    T, D = x.shape
    tb = _tile(T, ROW_BLOCK, 8)

    def body(x_ref, g_ref, o_ref):
        o_ref[...] = _rms(x_ref[...], g_ref[...]).astype(BF)

    row = pl.BlockSpec((tb, D), lambda i: (i, 0))
    return pl.pallas_call(body, grid=(T // tb,), in_specs=[row, pl.BlockSpec((1, D), lambda i: (0, 0))], out_specs=row,
                          out_shape=SDS((T, D), BF), compiler_params=_params("parallel"), name=name)(x, g.reshape(1, D))


def _rms_bwd(x, g, dh, dres, name):
    T, D = x.shape
    tb = _tile(T, ROW_BLOCK, 8)

    def body(*refs):
        if dres is not None:
            x_ref, g_ref, dh_ref, dres_ref, dx_ref, dxb_ref, dg_ref = refs
        else:
            x_ref, g_ref, dh_ref, dx_ref, dxb_ref, dg_ref = refs
        _, vjp = jax.vjp(_rms, x_ref[...], g_ref[...])
        dx, dg = vjp(dh_ref[...].astype(F32))
        if dres is not None:
            dx = dx + dres_ref[...]
        dx_ref[...] = dx
        dxb_ref[...] = dx.astype(BF)

        @pl.when(pl.program_id(0) == 0)
        def _():
            dg_ref[...] = jnp.zeros_like(dg_ref)

        dg_ref[...] += dg

    row = pl.BlockSpec((tb, D), lambda i: (i, 0))
    vec = pl.BlockSpec((1, D), lambda i: (0, 0))
    ins = [x, g.reshape(1, D), dh] + ([dres] if dres is not None else [])
    return pl.pallas_call(
        body, grid=(T // tb,), in_specs=[row, vec, row] + ([row] if dres is not None else []), out_specs=[row, row, vec],
        out_shape=[SDS((T, D), F32), SDS((T, D), BF), SDS((1, D), F32)], compiler_params=_params("arbitrary"), name=name)(*ins)


def _row_tile(M, K, N, per_row_bytes):
    fixed = K * N * 2
    fit = [t for t in _divisors(M) if fixed + 2 * t * per_row_bytes <= MM_VMEM_BUDGET and t * K * N <= 2 * MM_STEP_MACS]
    return max(fit) if fit else min(_divisors(M))


def _mm_rms_fwd(a, b, res, g, name, side=None):
    M, K = a.shape
    N = b.shape[2]
    tm = _row_tile(M, K, N, K * 2 + N * (4 + 4 + 2))

    def body(a_ref, b_ref, r_ref, g_ref, x_ref, h_ref):
        x = r_ref[...] + lax.dot_general(a_ref[...], b_ref[...], _DN["nn"], preferred_element_type=F32)
        x_ref[...] = x
        h_ref[...] = _rms(x, g_ref[...]).astype(BF)

    row = pl.BlockSpec((tm, N), lambda i: (i, 0))
    return _call(body, (a, b, res, g.reshape(1, N)), grid=(M // tm,),
                 in_specs=[pl.BlockSpec((tm, K), lambda i: (i, 0)), pl.BlockSpec((None, K, N), lambda i: (0, 0, 0), pipeline_mode=pl.Buffered(1)), row,
                           pl.BlockSpec((1, N), lambda i: (0, 0))],
                 out_specs=[row, row], out_shape=[SDS((M, N), F32), SDS((M, N), BF)], semantics=("parallel",), name=name, side=side)


def _mm_rms_bwd(a_parts, b, x, g, dres, name, side=None):
    n_a = len(a_parts)
    M = a_parts[0].shape[-2]
    N, K = b.shape[1:]
    assert K == sum(p.shape[-1] * (p.shape[0] if p.ndim == 3 else 1) for p in a_parts)
    tm = _row_tile(M, K, N, K * 2 + N * (4 + 4 + 4 + 2))

    def body(*refs):
        a_refs, (b_ref, x_ref, g_ref, r_ref, dx_ref, dxb_ref, dg_ref) = refs[:n_a], refs[n_a:]
        dh, col = None, 0
        for p, a_ref in zip(a_parts, a_refs):
            for blk in ([a_ref[h] for h in range(p.shape[0])] if p.ndim == 3 else [a_ref[...]]):
                t = lax.dot_general(blk, b_ref[:, col:col + p.shape[-1]], _DN["nt"], preferred_element_type=F32)
                dh = t if dh is None else dh + t
                col += p.shape[-1]
        _, vjp = jax.vjp(_rms, x_ref[...], g_ref[...])
        dx, dg = vjp(dh)
        dx = dx + r_ref[...]
        dx_ref[...] = dx
        dxb_ref[...] = dx.astype(BF)

        @pl.when(pl.program_id(0) == 0)
        def _():
            dg_ref[...] = jnp.zeros_like(dg_ref)

        dg_ref[...] += dg

    row = pl.BlockSpec((tm, N), lambda i: (i, 0))
    vec = pl.BlockSpec((1, N), lambda i: (0, 0))
    a_specs = [pl.BlockSpec((p.shape[0], tm, p.shape[2]), lambda i: (0, i, 0)) if p.ndim == 3 else pl.BlockSpec((tm, p.shape[1]), lambda i: (i, 0))
               for p in a_parts]
    return _call(
        body, (*a_parts, b, x, g.reshape(1, N), dres), grid=(M // tm,),
        in_specs=a_specs + [pl.BlockSpec((None, N, K), lambda i: (0, 0, 0), pipeline_mode=pl.Buffered(1)), row, vec, row],
        out_specs=[row, row, vec], out_shape=[SDS((M, N), F32), SDS((M, N), BF), SDS((1, N), F32)],
        semantics=("arbitrary",), name=name, side=side)


def _loss_bwd(x, g, target, name):
    T, D = x.shape
    tb = _tile(T, ROW_BLOCK, 8)
    nb = T // tb

    def body(x_ref, g_ref, t_ref, loss_ref, dx_ref, dg_ref, acc):
        i = pl.program_id(0)
        y, vjp = jax.vjp(_rms, x_ref[...], g_ref[...])
        err = y - t_ref[...]
        dx, dg = vjp(err * (1.0 / D))
        dx_ref[...] = dx

        @pl.when(i == 0)
        def _():
            dg_ref[...] = jnp.zeros_like(dg_ref)
            acc[...] = jnp.zeros_like(acc)

        dg_ref[...] += dg
        acc[...] += jnp.sum(err * err, axis=0, keepdims=True)

        @pl.when(i == nb - 1)
        def _():
            loss_ref[...] = jnp.full(loss_ref.shape, (0.5 / D) * jnp.sum(acc[...]), F32)

    row = pl.BlockSpec((tb, D), lambda i: (i, 0))
    vec = pl.BlockSpec((1, D), lambda i: (0, 0))
    return pl.pallas_call(
        body, grid=(nb,), in_specs=[row, vec, row], out_specs=[pl.BlockSpec((1, LANES), lambda i: (0, 0)), row, vec],
        out_shape=[SDS((1, LANES), F32), SDS((T, D), F32), SDS((1, D), F32)], scratch_shapes=[pltpu.VMEM((1, D), F32)],
        compiler_params=_params("arbitrary"), name=name)(x, g.reshape(1, D), target)


def _ln_silu_mm(cv, g, b, w, name):
    T, C = cv.shape
    D = w.shape[2]
    tb = _tile(T, 2 * ROW_BLOCK, 8)

    def body(x_ref, g_ref, b_ref, w_ref, y1_ref, y_ref):
        y1 = _ln_silu(x_ref[...], g_ref[...], b_ref[...]).astype(BF)
        y1_ref[...] = y1
        y_ref[...] = lax.dot_general(y1, w_ref[...], _DN["nn"], preferred_element_type=F32).astype(BF)

    row = pl.BlockSpec((tb, C), lambda i: (i, 0))
    vec = pl.BlockSpec((1, C), lambda i: (0, 0))
    return pl.pallas_call(
        body, grid=(T // tb,), in_specs=[row, vec, vec, pl.BlockSpec((None, C, D), lambda i: (0, 0, 0), pipeline_mode=pl.Buffered(1))],
        out_specs=[row, pl.BlockSpec((tb, D), lambda i: (i, 0))], out_shape=[SDS((T, C), BF), SDS((T, D), BF)],
        compiler_params=_params("parallel"), name=name)(cv, g.reshape(1, C), b.reshape(1, C), w)


def _mm_ln_silu_bwd(dyc, w, cv, g, b, name, side=None):
    T, C = cv.shape
    D = w.shape[2]
    tb = _tile(T, 2 * ROW_BLOCK, 8)

    def body(d_ref, w_ref, x_ref, g_ref, b_ref, dx_ref, dg_ref, db_ref):
        dy1 = lax.dot_general(d_ref[...], w_ref[...], _DN["nt"], preferred_element_type=F32)
        _, vjp = jax.vjp(_ln_silu, x_ref[...], g_ref[...], b_ref[...])
        dx, dg, db = vjp(dy1)
        dx_ref[...] = dx

        @pl.when(pl.program_id(0) == 0)
        def _():
            dg_ref[...] = jnp.zeros_like(dg_ref)
            db_ref[...] = jnp.zeros_like(db_ref)

        dg_ref[...] += dg
        db_ref[...] += db

    row = pl.BlockSpec((tb, C), lambda i: (i, 0))
    vec = pl.BlockSpec((1, C), lambda i: (0, 0))
    return _call(
        body, (dyc, w, cv, g.reshape(1, C), b.reshape(1, C)), grid=(T // tb,),
        in_specs=[pl.BlockSpec((tb, D), lambda i: (i, 0)), pl.BlockSpec((None, C, D), lambda i: (0, 0, 0), pipeline_mode=pl.Buffered(1)), row, vec, vec],
        out_specs=[row, vec, vec], out_shape=[SDS((T, C), F32), SDS((1, C), F32), SDS((1, C), F32)], semantics=("arbitrary",),
        name=name, side=side)


def _merge_fwd(proj, yc, yp, ps, C, name, side=None):
    T, D = yc.shape
    tb = _tile(T, ROW_BLOCK, 8)
    nj = D // C

    def body(gc_ref, gp_ref, yc_ref, yp_ref, ps_ref, o_ref):
        o_ref[...] = _merge(gc_ref[...], gp_ref[...], yc_ref[...].astype(F32), yp_ref[...].astype(F32), ps_ref[...]).astype(BF)

    blk = pl.BlockSpec((tb, C), lambda i, j: (i, j))
    return _call1(
        body, (proj, proj, yc, yp, ps.reshape(1, D)), grid=(T // tb, nj),
        in_specs=[pl.BlockSpec((tb, C), lambda i, j: (i, 3 + j)), pl.BlockSpec((tb, C), lambda i, j: (i, 3 + nj + j)), blk, blk,
                  pl.BlockSpec((1, C), lambda i, j: (0, j))],
        out_spec=blk, out_shape=SDS((T, D), BF), semantics=("parallel", "parallel"), name=name, side=side)


def _merge_bwd(proj, yc, yp, ps, dm, C, name, side=None):
    T, D = yc.shape
    tb = _tile(T, ROW_BLOCK, 8)
    nj = D // C

    def body(gc_ref, gp_ref, yc_ref, yp_ref, ps_ref, dm_ref, dg_ref, dyc_ref, dyp_ref, dps_ref):
        _, vjp = jax.vjp(_merge, gc_ref[...], gp_ref[...], yc_ref[...].astype(F32), yp_ref[...].astype(F32), ps_ref[...])
        dgc, dgp, dyc, dyp, dps = vjp(dm_ref[...].astype(F32))
        dg_ref[0] = dgc.astype(BF)
        dg_ref[1] = dgp.astype(BF)
        dyc_ref[...] = dyc.astype(BF)
        dyp_ref[...] = dyp.astype(BF)

        @pl.when(pl.program_id(1) == 0)
        def _():
            dps_ref[...] = jnp.zeros_like(dps_ref)

        dps_ref[...] += dps

    blk = pl.BlockSpec((tb, C), lambda j, i: (i, j))
    vec = pl.BlockSpec((1, C), lambda j, i: (0, j))
    return _call(
        body, (proj, proj, yc, yp, ps.reshape(1, D), dm), grid=(nj, T // tb),
        in_specs=[pl.BlockSpec((tb, C), lambda j, i: (i, 3 + j)), pl.BlockSpec((tb, C), lambda j, i: (i, 3 + nj + j)), blk, blk, vec, blk],
        out_specs=[pl.BlockSpec((2, tb, C), lambda j, i: (0, i, j)), blk, blk, vec],
        out_shape=[SDS((2, T, D), BF), SDS((T, D), BF), SDS((T, D), BF), SDS((1, D), F32)],
        semantics=("parallel", "arbitrary"), name=name, side=side)


def _shd(v, s, rows):
    if s == 0:
        return v
    return jnp.where(rows >= s, pltpu.roll(v, s, 0), 0.0)


def _shu(v, s, rows):
    if s == 0:
        return v
    n = v.shape[0]
    return jnp.where(rows < n - s, pltpu.roll(v, n - s, 0), 0.0)


def _glu_conv_fwd(proj, w, b, Bn, S, C, name, side=None):
    K = w.shape[0]
    sl = min(LANES, C)
    ns = C // sl

    def body(a_ref, gl_ref, w_ref, b_ref, o_ref):
        y0 = a_ref[...] * jax.nn.sigmoid(gl_ref[...])
        rows = lax.broadcasted_iota(jnp.int32, y0.shape, 0)
        acc = jnp.zeros_like(y0) + b_ref[...]
        for k in range(K):
            acc = acc + w_ref[k:k + 1, :] * _shd(y0, K - 1 - k, rows)
        o_ref[...] = acc

    return _call1(
        body, (proj, proj, w, b.reshape(1, C)), grid=(Bn, ns),
        in_specs=[pl.BlockSpec((S, sl), lambda bi, j: (bi, j)), pl.BlockSpec((S, sl), lambda bi, j: (bi, ns + j)),
                  pl.BlockSpec((K, sl), lambda bi, j: (0, j)), pl.BlockSpec((1, sl), lambda bi, j: (0, j))],
        out_spec=pl.BlockSpec((S, sl), lambda bi, j: (bi, j)), out_shape=SDS((Bn * S, C), F32),
        semantics=("parallel", "parallel"), name=name, side=side)


def _glu_conv_bwd(proj, w, dcv, Bn, S, C, name, side=None):
    K = w.shape[0]
    sl = min(LANES, C)
    ns = C // sl

    def body(a_ref, gl_ref, w_ref, d_ref, dagl_ref, dw_ref, db_ref):
        a = a_ref[...]
        sg = jax.nn.sigmoid(gl_ref[...])
        y0 = a * sg
        d = d_ref[...]
        rows = lax.broadcasted_iota(jnp.int32, y0.shape, 0)

        @pl.when(pl.program_id(1) == 0)
        def _():
            dw_ref[...] = jnp.zeros_like(dw_ref)
            db_ref[...] = jnp.zeros_like(db_ref)

        dy0 = jnp.zeros_like(y0)
        for k in range(K):
            s = K - 1 - k
            dw_ref[k:k + 1, :] += jnp.sum(d * _shd(y0, s, rows), axis=0, keepdims=True)
            dy0 = dy0 + w_ref[k:k + 1, :] * _shu(d, s, rows)
        db_ref[...] += jnp.sum(d, axis=0, keepdims=True)
        dagl_ref[0] = (dy0 * sg).astype(BF)
        dagl_ref[1] = (dy0 * a * sg * (1.0 - sg)).astype(BF)

    blk = pl.BlockSpec((S, sl), lambda j, bi: (bi, j))
    return _call(
        body, (proj, proj, w, dcv), grid=(ns, Bn),
        in_specs=[blk, pl.BlockSpec((S, sl), lambda j, bi: (bi, ns + j)), pl.BlockSpec((K, sl), lambda j, bi: (0, j)), blk],
        out_specs=[pl.BlockSpec((2, S, sl), lambda j, bi: (0, bi, j)), pl.BlockSpec((K, sl), lambda j, bi: (0, j)),
                   pl.BlockSpec((1, sl), lambda j, bi: (0, j))],
        out_shape=[SDS((2, Bn * S, C), BF), SDS((K, C), F32), SDS((1, C), F32)],
        semantics=("parallel", "arbitrary"), name=name, side=side)


def _pool_z(u, g, rows):
    s2 = u + _shd(u, 1, rows)
    s4 = s2 + _shd(s2, 2, rows)
    s8 = s4 + _shd(s4, 4, rows)
    s16 = s8 + _shd(s8, 8, rows)
    sw = jnp.where(g == 0, s2, jnp.where(g == 1, s4, jnp.where(g == 2, s8, s16)))
    cnt = jnp.minimum(rows + 1, POOL_WINDOWS[0] << g).astype(F32)
    return sw / cnt - u, cnt


def _pool_fwd(proj, wpt, l, Bn, S, C, D, name):
    G = len(POOL_WINDOWS)
    gd, go = C // G, D // G

    def body(u_ref, w_ref, o_ref):
        g = pl.program_id(1)
        u = u_ref[...]
        rows = lax.broadcasted_iota(jnp.int32, u.shape, 0)
        zp, _ = _pool_z(u, g, rows)
        o_ref[...] = lax.dot_general(zp.astype(BF), w_ref[...], _DN["nt"], preferred_element_type=F32).astype(BF)

    return pl.pallas_call(
        body, grid=(Bn, G),
        in_specs=[pl.BlockSpec((S, gd), lambda bi, g: (bi, 2 * G + g)), pl.BlockSpec((None, go, gd), lambda bi, g: (l * G + g, 0, 0))],
        out_specs=pl.BlockSpec((S, go), lambda bi, g: (bi, g)), out_shape=SDS((Bn * S, D), BF),
        compiler_params=_params("parallel", "parallel"), name=name)(proj, wpt)


def _pool_bwd(proj, wpt, dyp, l, Bn, S, C, D, name):
    G = len(POOL_WINDOWS)
    gd, go = C // G, D // G

    def body(u_ref, w_ref, d_ref, du_ref, dw_ref):
        g = pl.program_id(0)
        u = u_ref[...]
        rows = lax.broadcasted_iota(jnp.int32, u.shape, 0)
        zp, cnt = _pool_z(u, g, rows)
        d = d_ref[...]
        dzp = lax.dot_general(d, w_ref[...], _DN["nn"], preferred_element_type=F32)

        @pl.when(pl.program_id(1) == 0)
        def _():
            dw_ref[...] = jnp.zeros_like(dw_ref)

        dw_ref[...] += lax.dot_general(d, zp.astype(BF), _DN["tn"], preferred_element_type=F32)
        dsw = dzp / cnt
        zero = jnp.zeros_like(dsw)
        d16 = jnp.where(g == 3, dsw, zero)
        d8 = jnp.where(g == 2, dsw, zero) + d16 + _shu(d16, 8, rows)
        d4 = jnp.where(g == 1, dsw, zero) + d8 + _shu(d8, 4, rows)
        d2 = jnp.where(g == 0, dsw, zero) + d4 + _shu(d4, 2, rows)
        d1 = d2 + _shu(d2, 1, rows)
        du_ref[...] = (d1 - dzp).astype(BF)

    return pl.pallas_call(
        body, grid=(G, Bn),
        in_specs=[pl.BlockSpec((S, gd), lambda g, bi: (bi, 2 * G + g)), pl.BlockSpec((None, go, gd), lambda g, bi: (l * G + g, 0, 0)),
                  pl.BlockSpec((S, go), lambda g, bi: (bi, g))],
        out_specs=[pl.BlockSpec((S, gd), lambda g, bi: (bi, g)), pl.BlockSpec((None, go, gd), lambda g, bi: (g, 0, 0))],
        out_shape=[SDS((Bn * S, C), BF), SDS((G, go, gd), F32)],
        compiler_params=_params("parallel", "arbitrary"), name=name)(proj, wpt, dyp)


def _ffn_conv(u, w_ref, rows):
    K = w_ref.shape[0]
    acc = w_ref[K - 1:K, :] * u
    for k in range(K - 1):
        acc = acc + w_ref[k:k + 1, :] * _shd(u, K - 1 - k, rows)
    return acc


def _ffn_cb(F):
    return _tile(F, 256)


def _ffn_act_fwd(up0, w, Bn, S, F, name, side=None):
    cb = _ffn_cb(F)
    nj = F // cb

    def body(g_ref, v_ref, wg_ref, wv_ref, o_ref):
        rows = lax.broadcasted_iota(jnp.int32, g_ref.shape, 0)
        o_ref[...] = _gated(_ffn_conv(g_ref[...], wg_ref, rows), _ffn_conv(v_ref[...], wv_ref, rows)).astype(BF)

    K = w.shape[0]
    return _call1(
        body, (up0, up0, w, w), grid=(Bn, nj),
        in_specs=[pl.BlockSpec((S, cb), lambda bi, j: (bi, j)), pl.BlockSpec((S, cb), lambda bi, j: (bi, nj + j)),
                  pl.BlockSpec((K, cb), lambda bi, j: (0, j)), pl.BlockSpec((K, cb), lambda bi, j: (0, nj + j))],
        out_spec=pl.BlockSpec((S, cb), lambda bi, j: (bi, j)), out_shape=SDS((Bn * S, F), BF),
        semantics=("parallel", "parallel"), name=name, side=side)


SUBLANES = 8
FFN_HALO = SUBLANES
FFN_ROWS = 512
GELU_C0, GELU_C1 = 0.7978845608028654, 0.044715


def _gelu_and_grad(x):
    x2 = x * x
    t = jnp.tanh(GELU_C0 * (x + GELU_C1 * (x2 * x)))
    cdf = 0.5 * (1.0 + t)
    return x * cdf, cdf + (0.5 * GELU_C0) * x * (1.0 - t * t) * (1.0 + (3.0 * GELU_C1) * x2)


def _ffn_act_bwd(up0, w, dg, Bn, S, F, name, side=None):
    cb = min(LANES, F)
    nj = F // cb
    K = w.shape[0]
    rc = FFN_ROWS if S % FFN_ROWS == 0 else S
    win = rc + 2 * FFN_HALO
    assert K - 1 <= FFN_HALO and rc % SUBLANES == 0

    def body(g_ref, v_ref, wg_ref, wv_ref, d_ref, do_ref, dwg_ref, dwv_ref, gp, vp, dp):
        for pad, src in ((gp, g_ref), (vp, v_ref), (dp, d_ref)):
            pad[0:FFN_HALO, :] = jnp.zeros((FFN_HALO, cb), F32)
            pad[FFN_HALO + S:, :] = jnp.zeros((FFN_HALO, cb), F32)
            pad[FFN_HALO:FFN_HALO + S, :] = src[...].astype(F32)
        wg = [wg_ref[k:k + 1, :] for k in range(K)]
        wv = [wv_ref[k:k + 1, :] for k in range(K)]

        def taps(u):
            return [pltpu.roll(u, K - 1 - k, 0) for k in range(K - 1)] + [u]

        def conv(us, ws):
            acc = ws[K - 1] * us[K - 1]
            for k in range(K - 1):
                acc = acc + ws[k] * us[k]
            return acc

        def conv_t(dc, ws):
            acc = ws[K - 1] * dc
            for k in range(K - 1):
                acc = acc + ws[k] * pltpu.roll(dc, win - (K - 1 - k), 0)
            return acc

        def fold(t):
            acc = t[FFN_HALO:FFN_HALO + SUBLANES]
            for i in range(1, rc // SUBLANES):
                acc = acc + t[FFN_HALO + SUBLANES * i:FFN_HALO + SUBLANES * (i + 1)]
            return acc

        def chunk(c, sums):
            r0 = pl.multiple_of(c * rc, SUBLANES)
            gs, vs, d = taps(gp[pl.ds(r0, win), :]), taps(vp[pl.ds(r0, win), :]), dp[pl.ds(r0, win), :]
            ge, dge = _gelu_and_grad(conv(gs, wg))
            dgc = d * conv(vs, wv) * dge
            dvc = d * ge
            do_ref[0, pl.ds(r0, rc), :] = conv_t(dgc, wg)[FFN_HALO:FFN_HALO + rc].astype(BF)
            do_ref[1, pl.ds(r0, rc), :] = conv_t(dvc, wv)[FFN_HALO:FFN_HALO + rc].astype(BF)
            new = [fold(dc * u) for us, dc in ((gs, dgc), (vs, dvc)) for u in us]
            return tuple(a + b for a, b in zip(sums, new))

        sums = lax.fori_loop(0, S // rc, chunk, tuple(jnp.zeros((SUBLANES, cb), F32) for _ in range(2 * K)))

        @pl.when(pl.program_id(1) == 0)
        def _():
            dwg_ref[...] = jnp.zeros_like(dwg_ref)
            dwv_ref[...] = jnp.zeros_like(dwv_ref)

        for k in range(K):
            dwg_ref[k:k + 1, :] += jnp.sum(sums[k], axis=0, keepdims=True)
            dwv_ref[k:k + 1, :] += jnp.sum(sums[K + k], axis=0, keepdims=True)

    blk = pl.BlockSpec((S, cb), lambda j, bi: (bi, j))
    wblk = pl.BlockSpec((K, cb), lambda j, bi: (0, j))
    return _call(
        body, (up0, up0, w, w, dg), grid=(nj, Bn),
        in_specs=[blk, pl.BlockSpec((S, cb), lambda j, bi: (bi, nj + j)), wblk, pl.BlockSpec((K, cb), lambda j, bi: (0, nj + j)), blk],
        out_specs=[pl.BlockSpec((2, S, cb), lambda j, bi: (0, bi, j)), wblk, wblk],
        out_shape=[SDS((2, Bn * S, F), BF), SDS((K, F), F32), SDS((K, F), F32)],
        scratch_shapes=[pltpu.VMEM((S + 2 * FFN_HALO, cb), F32)] * 3, semantics=("parallel", "arbitrary"), name=name, side=side)


def _softmax_rows(q, k, scale):
    sc = lax.dot_general(q, k, _DN["nt"], preferred_element_type=F32) * scale
    e = jnp.exp(sc - jnp.max(sc, axis=-1, keepdims=True))
    return e / jnp.sum(e, axis=-1, keepdims=True)


def _attn_ts(S):
    return _tile(S, 1024, 8)


def _attn_fwd(q, kv, Bn, S, Mn, D, name, side=None):
    H = XA_HEADS
    dh = D // H
    ts = _attn_ts(S)
    nsb = S // ts
    scale = dh ** -0.5

    def body(q_ref, k_ref, v_ref, o_ref):
        p = _softmax_rows(q_ref[...], k_ref[...], scale)
        o_ref[...] = lax.dot_general(p.astype(BF), v_ref[...], _DN["nn"], preferred_element_type=F32).astype(BF)

    qblk = pl.BlockSpec((ts, dh), lambda bi, h, s: (bi * nsb + s, h))
    return _call1(
        body, (q, kv, kv), grid=(Bn, H, nsb),
        in_specs=[qblk, pl.BlockSpec((Mn, dh), lambda bi, h, s: (bi, h)), pl.BlockSpec((Mn, dh), lambda bi, h, s: (bi, H + h))],
        out_spec=qblk, out_shape=SDS((Bn * S, D), BF), semantics=("parallel", "parallel", "parallel"), name=name, side=side)


def _attn_bwd(q, kv, datt, Bn, S, Mn, D, name):
    H = XA_HEADS
    dh = D // H
    ts = _attn_ts(S)
    nsb = S // ts
    scale = dh ** -0.5

    def body(q_ref, k_ref, v_ref, do_ref, dq_ref, dk_ref, dv_ref):
        q, k, v, do = q_ref[...], k_ref[...], v_ref[...], do_ref[...]
        p = _softmax_rows(q, k, scale)
        dp = lax.dot_general(do, v, _DN["nt"], preferred_element_type=F32)
        ds = (p * (dp - jnp.sum(dp * p, axis=-1, keepdims=True)) * scale).astype(BF)
        dq_ref[...] = lax.dot_general(ds, k, _DN["nn"], preferred_element_type=F32).astype(BF)

        @pl.when(pl.program_id(2) == 0)
        def _():
            dk_ref[...] = jnp.zeros_like(dk_ref)
            dv_ref[...] = jnp.zeros_like(dv_ref)

        dk_ref[...] += lax.dot_general(ds, q, _DN["tn"], preferred_element_type=F32)
        dv_ref[...] += lax.dot_general(p.astype(BF), do, _DN["tn"], preferred_element_type=F32)

    qblk = pl.BlockSpec((ts, dh), lambda bi, h, s: (bi * nsb + s, h))
    kblk = pl.BlockSpec((Mn, dh), lambda bi, h, s: (bi, h))
    return pl.pallas_call(
        body, grid=(Bn, H, nsb),
        in_specs=[qblk, kblk, pl.BlockSpec((Mn, dh), lambda bi, h, s: (bi, H + h)), qblk],
        out_specs=[qblk, kblk, kblk], out_shape=[SDS((Bn * S, D), BF), SDS((Bn * Mn, D), F32), SDS((Bn * Mn, D), F32)],
        compiler_params=_params("parallel", "parallel", "arbitrary"), name=name)(q, kv, kv, datt)


class _Sides:
    def __init__(self, by_key=None, on_land=None):
        self.by_key, self.landed, self.on_land = dict(by_key or {}), {}, on_land

    def run(self, key, fn, *args, **kw):
        side = self.by_key.get(key)
        if side is None:
            return fn(*args, **kw)
        out, self.landed[key] = fn(*args, side=side() if callable(side) else side, **kw)
        if self.on_land is not None:
            self.on_land(key, self.landed[key])
        return out

    def mm(self, key, *args, **kw):
        return self.run(key, _mm, *args, **kw)


def _layer_fwd(x, h, mem_n, W, V, l, dims, sides, next_g):
    Bn, S, Mn, D, C, F = dims
    n = f"l{l}_"
    proj = sides.mm("proj", h, W["w_in"], "nn", F32, n + "proj", bl=0)
    cv = sides.run("glu_conv", _glu_conv_fwd, proj, V["conv_dw_w"][l], V["conv_dw_b"][l], Bn, S, C, n + "glu_conv")
    yc1, yc = _ln_silu_mm(cv, V["conv_ln_g"][l], V["conv_ln_b"][l], W["w_conv_out"], n + "conv_out")
    yp = _pool_fwd(proj, W["w_pool"], 0, Bn, S, C, D, n + "pool")
    merged = sides.run("merge", _merge_fwd, proj, yc, yp, V["pool_scale"][l], C, n + "merge")
    x1, hq = sides.run("out_proj", _mm_rms_fwd, merged, W["w_out"], x, V["xattn_norm_g"][l], n + "out_proj")
    q = sides.mm("q_proj", hq, W["w_q"], "nn", BF, n + "q_proj", bl=0)
    kv = _mm(mem_n, W["w_kv"], "nn", BF, n + "kv_proj", bl=0)
    att = sides.run("attn", _attn_fwd, q, kv, Bn, S, Mn, D, n + "attn")
    x2, hf = sides.run("o_proj", _mm_rms_fwd, att, W["w_o"], x1, V["ffn_norm_g"][l], n + "o_proj")
    up0 = sides.mm("up_proj", hf, W["w_up"], "nn", F32, n + "up_proj", bl=0)
    gact = sides.run("ffn_act", _ffn_act_fwd, up0, V["ffn_dw_w"][l], Bn, S, F, n + "ffn_act")
    if next_g is not None:
        x3, h3 = sides.run("down_proj", _mm_rms_fwd, gact, W["w_down"], x2, next_g, n + "down_proj")
    else:
        x3, h3 = sides.mm("down_proj", gact, W["w_down"], "nn", F32, n + "down_proj", res=x2, bl=0), None
    return x3, h3, dict(x=x, h=h, proj=proj, cv=cv, yc1=yc1, yc=yc, yp=yp, merged=merged, x1=x1, hq=hq, q=q, kv=kv, att=att, x2=x2,
                        hf=hf, up0=up0, gact=gact)


def _layer_bwd_mlp(dx, dxb, sv, W, V, l, dims, sides):
    Bn, S, Mn, D, C, F = dims
    n = f"l{l}_b_"
    gw, sm = {}, {}
    dgact = sides.mm("d_gact", dxb, W["w_down"], "nt", BF, n + "d_gact", bl=0)
    gw["w_down"] = sides.mm("dw_down", sv["gact"], dxb, "tn", F32, n + "dw_down", twin=BF)
    dup0, dwg, dwv = sides.run("ffn_act_b", _ffn_act_bwd, sv["up0"], V["ffn_dw_w"][l], dgact, Bn, S, F, n + "ffn_act")
    sm["ffn_dw_w"] = jnp.concatenate([dwg, dwv], axis=1)
    dx2, dx2b, sm["ffn_norm_g"] = _mm_rms_bwd([dup0], W["w_up"], sv["x2"], V["ffn_norm_g"][l], dx, n + "d_hf")
    gw["w_up"] = sides.mm("dw_up", sv["hf"], dup0, "tn", F32, n + "dw_up", twin=BF, b_halves=True)
    return dx2, dx2b, gw, sm


def _layer_bwd_mix(dx2, dx2b, dmem_n, sv, mem_n, W, V, l, dims, sides, gw):
    Bn, S, Mn, D, C, F = dims
    n = f"l{l}_b_"
    sm = {}
    datt = sides.mm("d_att", dx2b, W["w_o"], "nt", BF, n + "d_att", bl=0)
    gw["w_o"] = _mm(sv["att"], dx2b, "tn", F32, n + "dw_o", twin=BF)
    dq, dk, dv = _attn_bwd(sv["q"], sv["kv"], datt, Bn, S, Mn, D, n + "attn")
    dkv = jnp.concatenate([dk, dv], axis=1)
    gw["w_kv"] = _mm(mem_n, dkv, "tn", F32, n + "dw_kv", twin=BF)
    dmem_n = _mm(dkv, W["w_kv"], "nt", F32, n + "d_mem", res=dmem_n, bl=0)
    dx1, dx1b, sm["xattn_norm_g"] = _mm_rms_bwd([dq], W["w_q"], sv["x1"], V["xattn_norm_g"][l], dx2, n + "d_hq")
    gw["w_q"] = _mm(sv["hq"], dq, "tn", F32, n + "dw_q", twin=BF)
    dmerged = sides.mm("d_merged", dx1b, W["w_out"], "nt", BF, n + "d_merged", bl=0)
    gw["w_out"] = _mm(sv["merged"], dx1b, "tn", F32, n + "dw_out", twin=BF)
    dgates, dyc, dyp, sm["pool_scale"] = sides.run("merge_b", _merge_bwd, sv["proj"], sv["yc"], sv["yp"], V["pool_scale"][l], dmerged, C, n + "merge")
    du, dwp = _pool_bwd(sv["proj"], W["w_pool"], dyp, 0, Bn, S, C, D, n + "pool")
    gw["w_pool"] = (dwp, dwp.astype(BF))
    gw["w_conv_out"] = _mm(sv["yc1"], dyc, "tn", F32, n + "dw_conv_out", twin=BF)
    dcv, sm["conv_ln_g"], sm["conv_ln_b"] = sides.run("ln_silu_b", _mm_ln_silu_bwd, dyc, W["w_conv_out"], sv["cv"], V["conv_ln_g"][l],
                                                      V["conv_ln_b"][l], n + "d_yc1")
    dagl, sm["conv_dw_w"], sm["conv_dw_b"] = sides.run("glu_conv_b", _glu_conv_bwd, sv["proj"], V["conv_dw_w"][l], dcv, Bn, S, C, n + "glu_conv")
    dx, dxb, sm["mix_norm_g"] = sides.run("d_h", _mm_rms_bwd, [dagl, du, dgates], W["w_in"], sv["x"], V["mix_norm_g"][l], dx1, n + "d_h")
    n_in = W["w_in"].shape[2]
    part = _mm(sv["h"], dagl, "tn", F32, n + "dw_in_conv", twin=BF, b_halves=True, part=(n_in, 0, None))
    part = _mm(sv["h"], du, "tn", F32, n + "dw_in_pool", twin=BF, part=(n_in, 2 * C, part))
    gw["w_in"] = sides.mm("dw_in", sv["h"], dgates, "tn", F32, n + "dw_in", twin=BF, b_halves=True, part=(n_in, 3 * C, part))
    return dx, dxb, dmem_n, sm


BIG = (("w_in", "col"), ("w_conv_out", "col"), ("w_pool", "row"), ("w_out", "row"), ("w_q", "row"), ("w_kv", "col"),
       ("w_o", "row"), ("w_up", "col"), ("w_down", "row"))
ALL_RELS = (1, 2, 3)
GATHER_FIRST = ("w_in", "w_conv_out", "w_pool", "w_out")
FWD_CARRY = {
    (0, "proj"): (("w_up", 0, (1, 2)), ("w_q", 0, ALL_RELS), ("w_o", 0, ALL_RELS)),
    (0, "glu_conv"): (("w_kv", 0, ALL_RELS),),
    (0, "merge"): (("w_up", 0, (3,)),),
    (0, "q_proj"): (("w_down", 0, (1, 2)),),
    (0, "attn"): (("w_down", 0, (3,)),),
    (0, "up_proj"): (("w_in", 1, ALL_RELS), ("w_conv_out", 1, ALL_RELS), ("w_pool", 1, ALL_RELS), ("w_o", 1, ALL_RELS)),
    (0, "ffn_act"): (("w_out", 1, ALL_RELS), ("w_q", 1, ALL_RELS), ("w_kv", 1, ALL_RELS)),
    (1, "proj"): (("w_up", 1, (1, 2)),),
    (1, "glu_conv"): (("w_down", 1, (1, 2)),),
    (1, "merge"): (("w_up", 1, (3,)),),
    (1, "attn"): (("w_down", 1, (3,)),),
}
PASS_CARRY = {
    (0, "out_proj"): (("w_kv", 0), ("w_q", 0), ("w_o", 0)),
    (0, "o_proj"): (("w_up", 0), ("w_down", 0)),
    (0, "down_proj"): (("w_in", 1), ("w_conv_out", 1), ("w_pool", 1), ("w_out", 1), ("w_q", 1), ("w_kv", 1), ("w_o", 1)),
    (1, "o_proj"): (("w_up", 1), ("w_down", 1)),
}
EARLY = ("w_down", "w_up")
BWD_CARRY_EARLY = {"merge_b": ("w_down",), "glu_conv_b": ("w_up",)}
BWD_CARRY_LATE = {"ffn_act_b": ("w_in", "w_conv_out", "w_pool", "w_out", "w_q", "w_kv", "w_o")}
BWD_LAST_LAYER = (("att", ("w_o", "w_kv", "w_q"), "d_merged", {"d_h": ("w_o", "w_kv", "w_q")}),
                  ("tok", ("w_out", "w_pool", "w_conv_out"), "ln_silu_b", {"dw_in": ("w_out", "w_pool", "w_conv_out")}))


def _place():
    xi, yi, ci = lax.axis_index("x"), lax.axis_index("y"), lax.axis_index("c")
    return xi, yi, ci, 2 * xi + yi


def _chip_peer(xi, yi, ci, r):
    return (xi ^ (r >> 1), yi ^ (r & 1), ci)


def _full_shard(ref, kind, k, cs):
    if kind == "col":
        return ref.at[:, :, :, :, pl.ds(pl.multiple_of(k * cs, cs), cs)]
    return ref.at[:, :, k]


def _gather_weights(shards, kinds):
    n = len(shards)
    outs = []
    for s, kind in zip(shards, kinds):
        L, P, _, RH, CS = s.shape
        outs.append(SDS((L, P, 2, RH, CS * N_CHIPS) if kind == "col" else (L, P, N_CHIPS, 2, RH, CS), s.dtype))
    per = 7

    def body(*refs):
        srcs, fulls, (ssem, rsem) = refs[:n], refs[n:2 * n], refs[2 * n:]
        xi, yi, ci, j = _place()
        sib = (xi, yi, 1 - ci)

        def piece(i, k, c):
            kind, cs = kinds[i], shards[i].shape[-1]
            if kind == "col":
                return fulls[i].at[:, :, c, :, pl.ds(pl.multiple_of(k * cs, cs), cs)]
            return fulls[i].at[:, :, k, c]

        def copy(i, slot, src, dst, dev):
            return pltpu.make_async_remote_copy(src_ref=src, dst_ref=dst, send_sem=ssem.at[per * i + slot], recv_sem=rsem.at[per * i + slot],
                                                device_id=dev, device_id_type=MESH)

        own, first, passed = [], [], []
        for i in range(n):
            for r in (1, 2, 3):
                first.append(copy(i, r - 1, srcs[i].at[:, :, ci], piece(i, j, ci), _chip_peer(xi, yi, ci, r)))
                first[-1].start()
        for i in range(n):
            own.append(copy(i, 6, srcs[i], _full_shard(fulls[i], kinds[i], j, shards[i].shape[-1]), sib))
            own[-1].start()
        for i in range(n):
            for r in (1, 2, 3):
                got = piece(i, j ^ r, ci)
                copy(i, r - 1, got, got, sib).wait_recv()
                passed.append(copy(i, 2 + r, got, got, sib))
                passed[-1].start()
        for i in range(n):
            for r in (1, 2, 3):
                got = piece(i, j ^ r, 1 - ci)
                copy(i, 2 + r, got, got, sib).wait_recv()
        for cp in own:
            cp.wait()
        for cp in first + passed:
            cp.wait_send()

    return pl.pallas_call(
        body, in_specs=[ANY] * n, out_specs=[ANY] * n, out_shape=outs,
        scratch_shapes=[pltpu.SemaphoreType.DMA((per * n,)), pltpu.SemaphoreType.DMA((per * n,))], name="gather_weights")(*shards)


def _full_sds(s, kind):
    L, P, _, RH, CS = s.shape
    return SDS((L, P, 2, RH, CS * N_CHIPS) if kind == "col" else (L, P, N_CHIPS, 2, RH, CS), s.dtype)


def _gather_piece(full, kind, cs, k, c):
    if kind == "col":
        return full.at[:, :, c, :, pl.ds(pl.multiple_of(k * cs, cs), cs)]
    return full.at[:, :, k, c]


def _side_gather(shards, kinds, rels, fulls):
    n = len(shards)

    def make(srcs, outs, ssem, rsem):
        xi, yi, ci, j = _place()
        return [pltpu.make_async_remote_copy(
            src_ref=srcs[i].at[:, :, ci], dst_ref=_gather_piece(outs[i], kinds[i], shards[i].shape[-1], j, ci), send_sem=ssem.at[3 * i + r - 1],
            recv_sem=rsem.at[3 * i + r - 1], device_id=_chip_peer(xi, yi, ci, r), device_id_type=MESH) for i in range(n) for r in rels[i]]

    prior = [f for f in fulls if f is not None]
    assert len(prior) in (0, n)
    return _Side(list(shards) + prior, [_full_sds(s, k) for s, k in zip(shards, kinds)], 3 * n, make, n_alias=len(prior))


def _side_gather_pass(fulls, shards, kinds):
    n = len(fulls)

    def make(srcs, outs, ssem, rsem):
        xi, yi, ci, j = _place()
        sib = (xi, yi, 1 - ci)
        cps = []
        for i in range(n):
            cs = shards[i].shape[-1]
            for r in (1, 2, 3):
                got = _gather_piece(outs[i], kinds[i], cs, j ^ r, ci)
                cps.append(pltpu.make_async_remote_copy(src_ref=got, dst_ref=got, send_sem=ssem.at[4 * i + r - 1], recv_sem=rsem.at[4 * i + r - 1],
                                                        device_id=sib, device_id_type=MESH))
            cps.append(pltpu.make_async_remote_copy(src_ref=srcs[i], dst_ref=_full_shard(outs[i], kinds[i], j, cs), send_sem=ssem.at[4 * i + 3],
                                                    recv_sem=rsem.at[4 * i + 3], device_id=sib, device_id_type=MESH))
        return cps

    return _Side(list(shards) + list(fulls), [SDS(f.shape, f.dtype) for f in fulls], 4 * n, make, n_alias=n)


def _sibling_exchange(gviews, kinds, name):
    n = len(gviews)
    outs = [SDS(g.shape[:1] + g.shape[2:] if kind == "col" else g.shape[:2] + g.shape[3:], g.dtype) for g, kind in zip(gviews, kinds)]

    def body(*refs):
        gs, lands, (ssem, rsem) = refs[:n], refs[n:2 * n], refs[2 * n:]
        xi, yi, ci, _ = _place()
        cps = []
        for i in range(n):
            src = gs[i].at[:, 1 - ci] if kinds[i] == "col" else gs[i].at[:, :, 1 - ci]
            cps.append(pltpu.make_async_remote_copy(src_ref=src, dst_ref=lands[i], send_sem=ssem.at[i], recv_sem=rsem.at[i],
                                                    device_id=(xi, yi, 1 - ci), device_id_type=MESH))
            cps[-1].start()
        for cp in cps:
            cp.wait()

    return pl.pallas_call(body, in_specs=[ANY] * n, out_specs=[ANY] * n, out_shape=outs,
                          scratch_shapes=[pltpu.SemaphoreType.DMA((n,)), pltpu.SemaphoreType.DMA((n,))], name=name)(*gviews)


def _side_sibling_exchange(gviews, kinds):
    outs = [SDS(g.shape[:1] + g.shape[2:] if kind == "col" else g.shape[:2] + g.shape[3:], g.dtype) for g, kind in zip(gviews, kinds)]

    def make(gs, lands, ssem, rsem):
        xi, yi, ci, _ = _place()
        return [pltpu.make_async_remote_copy(src_ref=gs[i].at[:, 1 - ci] if kinds[i] == "col" else gs[i].at[:, :, 1 - ci], dst_ref=lands[i],
                                             send_sem=ssem.at[i], recv_sem=rsem.at[i], device_id=(xi, yi, 1 - ci), device_id_type=MESH)
                for i in range(len(gs))]

    return _Side(gviews, outs, len(gviews), make)


def _chip_sums(gs, lands, kinds, jc, name):
    n = len(gs)
    args, in_specs, out_specs, out_shape = [], [], [], []
    for g, land, kind in zip(gs, lands, kinds):
        if kind == "col":
            P, _, RH, C = g.shape
            CS = C // N_CHIPS
            in_specs += [pl.BlockSpec((P, None, RH, CS), lambda r, jc: (0, jc[1], 0, jc[0] ^ r)),
                         pl.BlockSpec((P, RH, CS), lambda r, jc: (0, 0, jc[0] ^ r))]
        else:
            P, _, _, RH, CS = g.shape
            in_specs += [pl.BlockSpec((P, None, None, RH, CS), lambda r, jc: (0, jc[0] ^ r, jc[1], 0, 0)),
                         pl.BlockSpec((P, None, RH, CS), lambda r, jc: (0, jc[0] ^ r, 0, 0))]
        args += [g, land]
        out_specs += [pl.BlockSpec((P, RH, CS), lambda r, jc: (0, 0, 0)), pl.BlockSpec((None, P, RH, CS), lambda r, jc: (r, 0, 0, 0))]
        out_shape += [SDS((P, RH, CS), F32), SDS((N_CHIPS, P, RH, CS), BF)]

    def body(jc_ref, *refs):
        ins, outs = refs[:2 * n], refs[2 * n:]
        for i in range(n):
            s = ins[2 * i][...] + ins[2 * i + 1][...].astype(F32)
            outs[2 * i + 1][...] = s.astype(BF)

            @pl.when(pl.program_id(0) == 0)
            def _():
                outs[2 * i][...] = s

    outs = _call(body, args, grid=(N_CHIPS,), in_specs=in_specs, out_specs=out_specs, out_shape=out_shape, semantics=("arbitrary",),
                 name=name, prefetch=(jc,))
    return outs[0::2], outs[1::2]


def _chip_exchange_copies(srcs, lands, ssem, rsem):
    xi, yi, ci, _ = _place()
    return [pltpu.make_async_remote_copy(src_ref=srcs[i].at[r], dst_ref=lands[i].at[r], send_sem=ssem.at[3 * i + r - 1],
                                         recv_sem=rsem.at[3 * i + r - 1], device_id=_chip_peer(xi, yi, ci, r), device_id_type=MESH)
            for i in range(len(srcs)) for r in (1, 2, 3)]


def _side_chip_exchange(pieces):
    return _Side(pieces, [SDS(p.shape, p.dtype) for p in pieces], 3 * len(pieces), _chip_exchange_copies)


FINAL_SUM_STEPS = 2


def _final_sums(owns, lands, jc, shards, l, L, name, side=None):
    n = len(owns)
    args, in_specs, out_specs, out_shape = [], [], [], []
    for own, land in zip(owns, lands):
        P, RH, CS = own.shape
        hr = RH // FINAL_SUM_STEPS
        in_specs += [pl.BlockSpec((P, hr, CS), lambda h, jc: (0, h, 0))]
        in_specs += [pl.BlockSpec((None, P, hr, CS), functools.partial(lambda r, h, jc: (r, 0, h, 0), r)) for r in (1, 2, 3)]
        args += [own, land, land, land]
        out_specs.append(pl.BlockSpec((None, P, None, hr, CS), lambda h, jc: (l, 0, jc[1], h, 0)))
        out_shape.append(SDS((L, P, 2, RH, CS), F32))
    aliases = None
    if shards is not None:
        aliases = {4 * n + i: i for i in range(n)}
        in_specs += [ANY] * n
        args += list(shards)

    def body(jc_ref, *refs):
        outs = refs[len(args):]
        for i in range(n):
            o, a, b, c = (refs[4 * i + t][...] for t in range(4))
            outs[i][...] = ((o + a.astype(F32)) + b.astype(F32)) + c.astype(F32)

    return _call(body, args, grid=(FINAL_SUM_STEPS,), in_specs=in_specs, out_specs=out_specs, out_shape=out_shape, semantics=("arbitrary",),
                 name=name, prefetch=(jc,), aliases=aliases, side=side)


def _halves_exchange(shards, l, name):
    n = len(shards)

    def body(*refs):
        outs, (ssem, rsem) = refs[n:2 * n], refs[2 * n:]
        xi, yi, ci, _ = _place()
        cps = []
        for i in range(n):
            mine = outs[i].at[l, :, ci]
            cps.append(pltpu.make_async_remote_copy(src_ref=mine, dst_ref=mine, send_sem=ssem.at[i], recv_sem=rsem.at[i],
                                                    device_id=(xi, yi, 1 - ci), device_id_type=MESH))
            cps[-1].start()
        for i in range(n):
            land = outs[i].at[l, :, 1 - ci]
            pltpu.make_async_remote_copy(src_ref=land, dst_ref=land, send_sem=ssem.at[i], recv_sem=rsem.at[i],
                                         device_id=(xi, yi, 1 - ci), device_id_type=MESH).wait_recv()
        for cp in cps:
            cp.wait_send()

    return pl.pallas_call(body, in_specs=[ANY] * n, out_specs=[ANY] * n, out_shape=[SDS(s.shape, s.dtype) for s in shards],
                          input_output_aliases={i: i for i in range(n)},
                          scratch_shapes=[pltpu.SemaphoreType.DMA((n,)), pltpu.SemaphoreType.DMA((n,))], name=name)(*shards)


def _reduce_small(part, pieces):
    NR, Wd = part.shape
    ND = 2 * N_CHIPS
    n = len(pieces)

    def body(p_ref, *refs):
        srcs, o_ref, lands, (land, ssem, rsem, xs, xr) = refs[:n], refs[n], refs[n + 1:2 * n + 1], refs[2 * n + 1:]
        exchange = _chip_exchange_copies(srcs, lands, xs, xr)
        for cp in exchange:
            cp.start()
        xi, yi, ci, j = _place()
        me = 2 * j + ci
        land[me] = p_ref[...]
        cps = []
        for rr in range(1, ND):
            dev = (xi ^ (rr >> 2), yi ^ ((rr >> 1) & 1), ci ^ (rr & 1))
            cps.append(pltpu.make_async_remote_copy(src_ref=p_ref, dst_ref=land.at[me], send_sem=ssem.at[rr - 1], recv_sem=rsem.at[rr - 1],
                                                    device_id=dev, device_id_type=MESH))
            cps[-1].start()
        for rr in range(1, ND):
            got = land.at[me ^ rr]
            pltpu.make_async_remote_copy(src_ref=got, dst_ref=got, send_sem=ssem.at[rr - 1], recv_sem=rsem.at[rr - 1],
                                         device_id=(xi, yi, ci), device_id_type=MESH).wait_recv()
        acc = land[0]
        for d in range(1, ND):
            acc = acc + land[d]
        o_ref[...] = acc
        for cp in cps:
            cp.wait_send()
        for cp in exchange:
            cp.wait()

    vm = pl.BlockSpec(memory_space=pltpu.VMEM)
    outs = pl.pallas_call(
        body, in_specs=[vm] + [ANY] * n, out_specs=[vm] + [ANY] * n, out_shape=[SDS((NR, Wd), F32)] + [SDS(p.shape, p.dtype) for p in pieces],
        scratch_shapes=[pltpu.VMEM((ND, NR, Wd), F32), pltpu.SemaphoreType.DMA((ND - 1,)), pltpu.SemaphoreType.DMA((ND - 1,)),
                        pltpu.SemaphoreType.DMA((3 * n,)), pltpu.SemaphoreType.DMA((3 * n,))],
        name="small_grad_allreduce")(part, *pieces)
    return outs[0], list(outs[1:])


def _adamw_update(w_ref, g_ref, m_ref, v_ref, d_ref, mo_ref, vo_ref):
    g = g_ref[...]
    m = ADAM_B1 * m_ref[...] + (1.0 - ADAM_B1) * g
    v = ADAM_B2 * v_ref[...] + (1.0 - ADAM_B2) * jnp.square(g)
    m_hat = m / (1.0 - ADAM_B1 ** ADAM_STEP)
    v_hat = v / (1.0 - ADAM_B2 ** ADAM_STEP)
    d_ref[...] = -ADAM_LR * (m_hat / (jnp.sqrt(v_hat) + ADAM_EPS) + ADAM_WD * w_ref[...])
    mo_ref[...] = m
    vo_ref[...] = v


ADAMW_STEPS = 8


def _adamw_layer(ws, gs, ms, vs, prev, l, name, side=None):
    n = len(ws)
    args, in_specs, out_specs, out_shape = [], [], [], []
    for w, g, m, v in zip(ws, gs, ms, vs):
        L, R, C = w.shape
        blk = pl.BlockSpec((None, R // ADAMW_STEPS, C), lambda i: (l, i, 0))
        in_specs += [blk] * 4
        args += [w, g, m, v]
        out_specs += [blk] * 3
        out_shape += [SDS((L, R, C), F32)] * 3
    aliases = None
    if prev is not None:
        aliases = {4 * n + i: i for i in range(3 * n)}
        in_specs += [ANY] * (3 * n)
        args += list(prev)

    def body(*refs):
        outs = refs[len(args):]
        for i in range(n):
            _adamw_update(*refs[4 * i:4 * i + 4], *outs[3 * i:3 * i + 3])

    return _call(body, args, grid=(ADAMW_STEPS,), in_specs=in_specs, out_specs=out_specs, out_shape=out_shape, semantics=("parallel",),
                 name=name, aliases=aliases, side=side)


def _adamw(w, g, m, v, name):
    shape = w.shape
    C = shape[-1]
    R = w.size // C
    tb = _tile(R, max(8, (1 << 18) // C), 8)
    body = functools.partial(_adamw_update)
    blk = pl.BlockSpec((tb, C), lambda i: (i, 0))
    outs = pl.pallas_call(body, grid=(R // tb,), in_specs=[blk] * 4, out_specs=[blk] * 3, out_shape=[SDS((R, C), F32)] * 3,
                          compiler_params=_params("parallel"), name=name)(*[t.reshape(R, C) for t in (w, g, m, v)])
    return [t.reshape(shape) for t in outs]


WEIGHTS = ("mix_norm_g", "w_in", "conv_dw_w", "conv_dw_b", "conv_ln_g", "conv_ln_b", "w_conv_out", "w_pool_grp", "pool_scale", "w_out",
           "xattn_norm_g", "mem_norm_g", "w_q", "w_kv", "w_o", "ffn_norm_g", "w_up", "ffn_dw_w", "w_down", "final_norm_g")
VECTORS = ("mix_norm_g", "conv_dw_b", "conv_ln_g", "conv_ln_b", "pool_scale", "xattn_norm_g", "mem_norm_g", "ffn_norm_g", "final_norm_g")


def _shard_view(t, kind):
    L, P, R, C = t.shape
    return t.reshape(L, P, 2, R // 2, C)


def _rows(t, width):
    return t.reshape(-1, width)


def _pack(parts):
    return jnp.concatenate([jnp.pad(p, ((0, (-p.shape[0]) % 8), (0, 0))) for p in parts], axis=0)


def kernel(x, mem, mix_norm_g, w_in, conv_dw_w, conv_dw_b, conv_ln_g, conv_ln_b, w_conv_out, w_pool_grp, pool_scale, w_out, xattn_norm_g, mem_norm_g, w_q, w_kv, w_o, ffn_norm_g, w_up, ffn_dw_w, w_down, final_norm_g, loss_target, m_mix_norm_g, m_w_in, m_conv_dw_w, m_conv_dw_b, m_conv_ln_g, m_conv_ln_b, m_w_conv_out, m_w_pool_grp, m_pool_scale, m_w_out, m_xattn_norm_g, m_mem_norm_g, m_w_q, m_w_kv, m_w_o, m_ffn_norm_g, m_w_up, m_ffn_dw_w, m_w_down, m_final_norm_g, v_mix_norm_g, v_w_in, v_conv_dw_w, v_conv_dw_b, v_conv_ln_g, v_conv_ln_b, v_w_conv_out, v_w_pool_grp, v_pool_scale, v_w_out, v_xattn_norm_g, v_mem_norm_g, v_w_q, v_w_kv, v_w_o, v_ffn_norm_g, v_w_up, v_ffn_dw_w, v_w_down, v_final_norm_g):
    w = dict(mix_norm_g=mix_norm_g, w_in=w_in, conv_dw_w=conv_dw_w, conv_dw_b=conv_dw_b, conv_ln_g=conv_ln_g, conv_ln_b=conv_ln_b,
             w_conv_out=w_conv_out, w_pool_grp=w_pool_grp, pool_scale=pool_scale, w_out=w_out, xattn_norm_g=xattn_norm_g,
             mem_norm_g=mem_norm_g, w_q=w_q, w_kv=w_kv, w_o=w_o, ffn_norm_g=ffn_norm_g, w_up=w_up, ffn_dw_w=ffn_dw_w, w_down=w_down,
             final_norm_g=final_norm_g)
    m = dict(zip(WEIGHTS, (m_mix_norm_g, m_w_in, m_conv_dw_w, m_conv_dw_b, m_conv_ln_g, m_conv_ln_b, m_w_conv_out, m_w_pool_grp, m_pool_scale,
                           m_w_out, m_xattn_norm_g, m_mem_norm_g, m_w_q, m_w_kv, m_w_o, m_ffn_norm_g, m_w_up, m_ffn_dw_w, m_w_down, m_final_norm_g)))
    v = dict(zip(WEIGHTS, (v_mix_norm_g, v_w_in, v_conv_dw_w, v_conv_dw_b, v_conv_ln_g, v_conv_ln_b, v_w_conv_out, v_w_pool_grp, v_pool_scale,
                           v_w_out, v_xattn_norm_g, v_mem_norm_g, v_w_q, v_w_kv, v_w_o, v_ffn_norm_g, v_w_up, v_ffn_dw_w, v_w_down, v_final_norm_g)))
    xi, yi, ci, j = _place()
    jc = jnp.stack([j, ci]).astype(jnp.int32)
    L = w_in.shape[0]
    G = len(POOL_WINDOWS)
    kinds = dict(BIG)

    def to_mat(name, t):
        if name == "w_pool":
            return jnp.swapaxes(t, 2, 3)
        return t[:, None]

    def from_mat(name, t):
        if name == "w_pool":
            return jnp.swapaxes(t, 2, 3)
        return t[:, 0]

    src = {name: w["w_pool_grp" if name == "w_pool" else name] for name, _ in BIG}

    KC, cs_c = conv_dw_w.shape[1], conv_dw_w.shape[2]
    KF, cs_f = ffn_dw_w.shape[1], ffn_dw_w.shape[2]
    taps = jnp.concatenate([conv_dw_w.reshape(L * KC, cs_c), ffn_dw_w.reshape(L * KF * (cs_f // cs_c), cs_c)], axis=0)
    n_taps = taps.shape[0]
    taps = jnp.pad(taps, ((0, (-n_taps) % 16), (0, 0)))
    names = [name for name, _ in BIG]
    mats = {name: to_mat(name, src[name]).astype(BF) for name in names}

    def layer_shards(l, subset):
        return [_shard_view(mats[name][l:l + 1], kinds[name]) for name in subset]

    def as_weight(name, f):
        return f.reshape(G if name == "w_pool" else 1, -1, f.shape[-1])

    assert L == 2
    fulls = _gather_weights(layer_shards(0, GATHER_FIRST) + [_shard_view(taps[None, None], "row")], [kinds[name] for name in GATHER_FIRST] + ["row"])
    ready = {(name, 0): as_weight(name, f) for name, f in zip(GATHER_FIRST, fulls)}
    landing = {}
    taps_all = fulls[-1].reshape(N_CHIPS, -1, cs_c)[:, :n_taps]
    V = {name: w[name] for name in VECTORS}
    V["conv_dw_w"] = taps_all[:, :L * KC].reshape(N_CHIPS, L, KC, cs_c).transpose(1, 2, 0, 3).reshape(L, KC, N_CHIPS * cs_c)
    V["ffn_dw_w"] = taps_all[:, L * KC:].reshape(N_CHIPS, L, KF, cs_f).transpose(1, 2, 0, 3).reshape(L, KF, N_CHIPS * cs_f)

    Bn, S, D = x.shape
    Mn = mem.shape[1]
    dims = (Bn, S, Mn, D, conv_dw_b.shape[1], w_down.shape[1] * N_CHIPS)
    xt = x.reshape(Bn * S, D)
    memf = mem.reshape(Bn * Mn, D)
    mem_n = _rms_fwd(memf, V["mem_norm_g"], "mem_norm")

    class LayerWeights:
        def __init__(self, l):
            self.l = l

        def __getitem__(self, name):
            return ready[(name, self.l)]

    def carried_gather(entries):
        return lambda: _side_gather([layer_shards(lw, [nm])[0] for nm, lw, _ in entries], [kinds[nm] for nm, _, _ in entries],
                                    [rels for _, _, rels in entries], [landing.get((nm, lw)) for nm, lw, _ in entries])

    def carried_pass(group):
        return lambda: _side_gather_pass([landing.pop(t) for t in group], [layer_shards(lw, [nm])[0] for nm, lw in group],
                                         [kinds[nm] for nm, _ in group])

    def on_land(l):
        def handle(key, fulls):
            if (l, key) in FWD_CARRY:
                landing.update({(nm, lw): f for (nm, lw, _), f in zip(FWD_CARRY[(l, key)], fulls)})
            else:
                ready.update({t: as_weight(t[0], f) for t, f in zip(PASS_CARRY[(l, key)], fulls)})
        return handle

    saved, W = [], []
    ht = _rms_fwd(xt, V["mix_norm_g"][0], "l0_mix_norm")
    for l in range(L):
        by_key = {key: carried_gather(entries) for (cl, key), entries in FWD_CARRY.items() if cl == l}
        by_key.update({key: carried_pass(group) for (cl, key), group in PASS_CARRY.items() if cl == l})
        sides = _Sides(by_key, on_land=on_land(l))
        W.append(LayerWeights(l))
        xt, ht, sv = _layer_fwd(xt, ht, mem_n, W[l], V, l, dims, sides, V["mix_norm_g"][l + 1] if l + 1 < L else None)
        saved.append(sv)
    loss, dx, dgf = _loss_bwd(xt, V["final_norm_g"], loss_target.reshape(Bn * S, D), "loss")
    loss = lax.psum(loss[0, 0], ("x", "y", "c"))

    late_names = [name for name in names if name not in EARLY]

    def views(gw, subset, twin):
        out = []
        for name in subset:
            g = gw[name][twin] if gw[name][twin].ndim == 3 else gw[name][twin][None]
            P, R, C = g.shape
            out.append(g.reshape(P, 2, R // 2, C) if kinds[name] == "col" else g.reshape(P, N_CHIPS, 2, R // (2 * N_CHIPS), C))
        return out

    def group_kinds(subset):
        return [kinds[name] for name in subset]

    class Reduction:
        def __init__(self, gw, subset, l, tag, first, table):
            self.gw, self.subset, self.l, self.tag, self.first, self.table = gw, subset, l, tag, first, table

        def sides(self):
            by_key = {self.first: lambda: _side_sibling_exchange(views(self.gw, self.subset, 1), group_kinds(self.subset))}
            by_key.update({key: (lambda names_=names_: _side_chip_exchange([self.pieces[nm] for nm in names_])) for key, names_ in self.table.items()})
            return by_key

        def on_land(self, key, landed):
            if key == self.first:
                self.sums(landed)
            elif key in self.table:
                got[self.l].update(zip(self.table[key], landed))

        def sums(self, lands):
            own, pieces = _chip_sums(views(self.gw, self.subset, 0), lands, group_kinds(self.subset), jc, f"chip_sums_{self.tag}_l{self.l}")
            owns[self.l].update(zip(self.subset, own))
            self.pieces = dict(zip(self.subset, pieces))

    def riding(reductions):
        return _Sides({key: side for r in reductions for key, side in r.sides().items()},
                      on_land=lambda key, landed: [r.on_land(key, landed) for r in reductions])

    dxb, dmem_n = dx, None
    smalls, owns, got = [None] * L, [{} for _ in range(L)], [{} for _ in range(L)]
    late = None
    for l in reversed(range(L)):
        dx, dxb, gw, sm = _layer_bwd_mlp(dx, dxb, saved[l], W[l], V, l, dims, riding([late] if late is not None else []))
        gw_mix = {}
        reductions = [Reduction(gw, EARLY, l, "mlp", "d_att", BWD_CARRY_EARLY)]
        if l == 0:
            reductions += [Reduction(gw_mix, names_, 0, tag, first, table) for tag, names_, first, table in BWD_LAST_LAYER]
        dx, dxb, dmem_n, sm2 = _layer_bwd_mix(dx, dxb, dmem_n, saved[l], mem_n, W[l], V, l, dims, riding(reductions), gw_mix)
        smalls[l] = {**sm, **sm2}
        late = Reduction(gw_mix, late_names, l, "mix", "d_gact", BWD_CARRY_LATE) if l > 0 else None
    last = Reduction(gw_mix, ("w_in",), 0, "in", None, {})
    last.sums(_sibling_exchange(views(gw_mix, last.subset, 1), group_kinds(last.subset), "grad_sibling_exchange_in_l0"))
    grad_x = dx.reshape(Bn, S, D)
    _, _, dgm = _rms_bwd(memf, V["mem_norm_g"], dmem_n, None, "mem_norm_b")
    small = {k: jnp.stack([sm[k] for sm in smalls]) if k in ("conv_dw_w", "ffn_dw_w") else jnp.concatenate([sm[k] for sm in smalls], axis=0)
             for k in smalls[0]}
    small["mem_norm_g"] = dgm
    small["final_norm_g"] = dgf

    small_w = conv_dw_b.shape[1]
    order = VECTORS + ("conv_dw_w", "ffn_dw_w")
    parts = [_rows(small[name], small_w) for name in order]
    counts = [p.shape[0] for p in parts]
    summed, landed_last = _reduce_small(_pack(parts), [last.pieces[name] for name in last.subset])
    got[0].update(zip(last.subset, landed_last))

    keys = ["w_pool_grp" if name == "w_pool" else name for name in names]
    rows3 = lambda t: t.reshape(t.shape[0], -1, t.shape[-1])
    wmv = [[rows3(to_mat(name, d[key])) for name, key in zip(names, keys)] for d in (w, m, v)]
    gshards, updates = None, None
    for l in reversed(range(L)):
        gshards = _final_sums([owns[l][name] for name in names], [got[l][name] for name in names], jc, gshards, l, L, f"final_sums_l{l}")
        gshards = _halves_exchange(gshards, l, f"grad_halves_exchange_l{l}")
        updates = _adamw_layer(wmv[0], [rows3(t) for t in gshards], wmv[1], wmv[2], updates, l, f"adamw_l{l}")
    grads, delta, new_m, new_v = {}, {}, {}, {}
    for i, (name, key) in enumerate(zip(names, keys)):
        Lg, P, _, RH, CS = gshards[i].shape
        grads[key] = from_mat(name, gshards[i].reshape(Lg, P, 2 * RH, CS))
        for d, t in zip((delta, new_m, new_v), updates[3 * i:3 * i + 3]):
            d[key] = from_mat(name, t.reshape(Lg, P, 2 * RH, CS))

    off = 0
    for name, cnt in zip(order, counts):
        t = summed[off:off + cnt]
        off += cnt + (-cnt) % 8
        if name in VECTORS:
            grads[name] = t.reshape(w[name].shape)
        else:
            full = t.reshape(small[name].shape)
            cs = w[name].shape[2]
            grads[name] = lax.dynamic_slice_in_dim(full, j * cs, cs, axis=2)

    vec =[_pack([_rows(d[name], small_w) for name in VECTORS]) for d in (w, grads, m, v)]
    outs = _adamw(*vec, "adamw_vectors")
    off = 0
    for name in VECTORS:
        cnt = w[name].size // small_w
        for d, t in zip((delta, new_m, new_v), outs):
            d[name] = t[off:off + cnt].reshape(w[name].shape)
        off += cnt + (-cnt) % 8
    for name in ("conv_dw_w", "ffn_dw_w"):
        delta[name], new_m[name], new_v[name] = _adamw(w[name], grads[name], m[name], v[name], "adamw_" + name)

    return (loss, grad_x, *[grads[k] for k in WEIGHTS], *[delta[k] for k in WEIGHTS], *[new_m[k] for k in WEIGHTS], *[new_v[k] for k in WEIGHTS])
```
